```python
import jax, jax.numpy as jnp
from jax import lax
import numpy as np

D_MODEL = 1024
BATCH = 8
SEQ = 8192
DEPTH = 1

GRID_W = 64
CTX_LEN = 256
A_WIDTH = 1024
A_HEAD_DIM = 128
A_HEADS = A_WIDTH // A_HEAD_DIM
A_CHUNK = 64
B_WIDTH = 1024
B_BLOCKS = 8
B_BLOCK_DIM = B_WIDTH // B_BLOCKS
B_CONV = 4
RG_C = 8.0
N_BRANCH = 2
IN_COLS = 5 * A_WIDTH + 2 * B_WIDTH + N_BRANCH * D_MODEL
DEEPNORM_ALPHA = (2 * DEPTH) ** 0.25
DEEPNORM_BETA = (8 * DEPTH) ** -0.25
LN_EPS = 1e-5
RMS_EPS = 1e-6

kernel_name = "hgrn2_rglru_gated_hybrid_dit"


def _in_split_points():
    sizes = [A_WIDTH] * 5 + [B_WIDTH] * 2 + [D_MODEL] * N_BRANCH
    return [int(v) for v in np.cumsum(sizes)[:-1]]


def layer_norm(t, g, b):
    tf = t.astype(jnp.float32)
    mu = jnp.mean(tf, axis=-1, keepdims=True)
    var = jnp.mean(jnp.square(tf - mu), axis=-1, keepdims=True)
    return ((tf - mu) * lax.rsqrt(var + LN_EPS) * g.astype(jnp.float32) + b.astype(jnp.float32)).astype(t.dtype)


def rms_norm(t, g):
    tf = t.astype(jnp.float32)
    return tf * lax.rsqrt(jnp.mean(jnp.square(tf), axis=-1, keepdims=True) + RMS_EPS) * g.astype(jnp.float32)


def to_heads(t):
    b, l, _ = t.shape
    return t.reshape(b, l, A_HEADS, A_HEAD_DIM).transpose(0, 2, 1, 3)


def from_heads(t):
    b, h, l, d = t.shape
    return t.transpose(0, 2, 1, 3).reshape(b, l, h * d)


def grid_to_colmajor(t, rows):
    b, _, ch = t.shape
    return t.reshape(b, rows, GRID_W, ch).transpose(0, 2, 1, 3)


def colmajor_to_grid(t):
    b, w, r, ch = t.shape
    return t.transpose(0, 2, 1, 3).reshape(b, r * w, ch)


def gla_chunkwise(q, k, v, logf, s0):
    b, h, t, dk = q.shape
    dv = v.shape[-1]
    n = t // A_CHUNK
    q = q.reshape(b, h, n, A_CHUNK, dk)
    k = k.reshape(b, h, n, A_CHUNK, dk)
    v = v.reshape(b, h, n, A_CHUNK, dv)
    g = jnp.cumsum(logf.reshape(b, h, n, A_CHUNK, dk), axis=-2)
    g_last = g[..., -1:, :]
    q_dec = q * jnp.exp(g)
    k_inv = k * jnp.exp(-g)
    k_end = k * jnp.exp(g_last - g)
    mask = jnp.tril(jnp.ones((A_CHUNK, A_CHUNK), dtype=bool))
    scores = jnp.where(mask, jnp.einsum('bhnck,bhnsk->bhncs', q_dec, k_inv), 0.0)
    o_intra = jnp.einsum('bhncs,bhnsv->bhncv', scores, v)
    u = jnp.einsum('bhnsk,bhnsv->bhnkv', k_end, v)
    decay = jnp.exp(g_last[..., 0, :])

    def step(s, inp):
        d_n, u_n = inp
        return d_n[..., None] * s + u_n, s

    s_fin, s_start = lax.scan(step, s0, (jnp.moveaxis(decay, 2, 0), jnp.moveaxis(u, 2, 0)))
    s_start = jnp.moveaxis(s_start, 0, 2)
    o_inter = jnp.einsum('bhnck,bhnkv->bhncv', q_dec, s_start)
    return (o_intra + o_inter).reshape(b, h, t, dv), s_fin


def gla_prefixed(ctx_in, lat_in, reverse):
    if reverse:
        ctx_in = tuple(jnp.flip(a, axis=2) for a in ctx_in)
        lat_in = tuple(jnp.flip(a, axis=2) for a in lat_in)
    b, h, _, dk = ctx_in[0].shape
    dv = ctx_in[2].shape[-1]
    s0 = jnp.zeros((b, h, dk, dv), jnp.float32)
    o_c, s_c = gla_chunkwise(*ctx_in, s0)
    o_x, _ = gla_chunkwise(*lat_in, s_c)
    if reverse:
        o_c, o_x = jnp.flip(o_c, axis=2), jnp.flip(o_x, axis=2)
    return o_c, o_x


def hgrn2_features(z, lb):
    f32 = jnp.float32
    q = to_heads(jax.nn.silu(z[0].astype(f32)) * (A_HEAD_DIM ** -0.5))
    v = to_heads(z[3].astype(f32))
    f_fwd = lb[0] + (1.0 - lb[0]) * jax.nn.sigmoid(z[1].astype(f32))
    f_bwd = lb[1] + (1.0 - lb[1]) * jax.nn.sigmoid(z[2].astype(f32))
    fwd = (q, to_heads(1.0 - f_fwd), v, to_heads(jnp.log(f_fwd)))
    bwd = (q, to_heads(1.0 - f_bwd), v, to_heads(jnp.log(f_bwd)))
    return fwd, bwd


def centred_dwconv(t, w, bias):
    lo = (B_CONV - 1) // 2
    hi = B_CONV - 1 - lo
    n = t.shape[-2]
    tp = jnp.pad(t, [(0, 0)] * (t.ndim - 2) + [(lo, hi), (0, 0)])
    out = bias + tp[..., 0:n, :] * w[0]
    for kk in range(1, B_CONV):
        out = out + tp[..., kk:kk + n, :] * w[kk]
    return out


def rglru_gates(xc, w_r, b_r, w_i, b_i, lam):
    b, l, ch = xc.shape
    xb = xc.reshape(b, l, B_BLOCKS, B_BLOCK_DIM)
    r = jax.nn.sigmoid(jnp.einsum('blgi,gij->blgj', xb, w_r).reshape(b, l, ch) + b_r)
    i = jax.nn.sigmoid(jnp.einsum('blgi,gij->blgj', xb, w_i).reshape(b, l, ch) + b_i)
    log_a = -RG_C * r * jax.nn.softplus(-lam)
    a = jnp.exp(log_a)
    mult = jnp.sqrt(-jnp.expm1(2.0 * log_a))
    return a, mult * (i * xc)


def linear_scan(a, bterm, h0):
    bterm = bterm.at[:, 0].add(a[:, 0] * h0)

    def comb(left, right):
        al, bl = left
        ar, br = right
        return al * ar, ar * bl + br

    _, h = lax.associative_scan(comb, (a, bterm), axis=1)
    return h


def rglru_prefixed(xc_c, xc_x, params, reverse):
    if reverse:
        xc_c, xc_x = jnp.flip(xc_c, axis=1), jnp.flip(xc_x, axis=1)
    a_c, b_c = rglru_gates(xc_c, *params)
    h_c = linear_scan(a_c, b_c, jnp.zeros((xc_c.shape[0], B_WIDTH), jnp.float32))
    a_x, b_x = rglru_gates(xc_x, *params)
    h_x = linear_scan(a_x, b_x, h_c[:, -1])
    if reverse:
        h_c, h_x = jnp.flip(h_c, axis=1), jnp.flip(h_x, axis=1)
    return h_c, h_x


def hybrid_layer(x, ctx, c, c_ctx, w_mod, b_mod, w_in, b_in, lb, norm_a_g, conv_w, conv_b,
                 w_r, b_r, w_i, b_i, lam, p_a, p_b, w_out, ln_g, ln_b, last):
    f32 = jnp.float32
    bsz, t, _ = x.shape
    rows = t // GRID_W
    mod_x = jax.nn.silu(c) @ w_mod + b_mod
    mod_c = jax.nn.silu(c_ctx) @ w_mod + b_mod
    sh_x, sc_x, gt_x = jnp.split(mod_x[:, None, :], 3, axis=-1)
    sh_c, sc_c, gt_c = jnp.split(mod_c, 3, axis=-1)
    u_x = x * (1.0 + sc_x) + sh_x
    u_c = ctx * (1.0 + sc_c) + sh_c
    splits = _in_split_points()
    z_x = jnp.split(u_x @ w_in + b_in, splits, axis=-1)
    z_c = jnp.split(u_c @ w_in + b_in, splits, axis=-1)

    fx, bx = hgrn2_features(z_x, lb)
    fc, bc = hgrn2_features(z_c, lb)
    oc_f, ox_f = gla_prefixed(fc, fx, False)
    oc_b, ox_b = gla_prefixed(bc, bx, True)

    xc_x = centred_dwconv(grid_to_colmajor(z_x[5], rows), conv_w, conv_b).astype(f32).reshape(bsz, t, B_WIDTH)
    xc_c = centred_dwconv(z_c[5], conv_w, conv_b).astype(f32)
    hc_f, hx_f = rglru_prefixed(xc_c, xc_x, (w_r[0], b_r[0], w_i[0], b_i[0], lam[0]), False)
    hc_b, hx_b = rglru_prefixed(xc_c, xc_x, (w_r[1], b_r[1], w_i[1], b_i[1], lam[1]), True)
    hx = colmajor_to_grid((hx_f + hx_b).reshape(bsz, GRID_W, rows, B_WIDTH))

    def merge(z, o_a, h_b):
        o_a = from_heads(rms_norm(o_a, norm_a_g)) * jax.nn.silu(z[4].astype(f32))
        o_b = h_b * jax.nn.silu(z[6].astype(f32))
        y = jax.nn.sigmoid(z[7]) * (o_a @ p_a) + jax.nn.sigmoid(z[8]) * (o_b @ p_b)
        return y @ w_out

    x_new = layer_norm(DEEPNORM_ALPHA * x + gt_x * merge(z_x, ox_f + ox_b, hx), ln_g, ln_b)
    if last:
        return x_new, ctx
    ctx_new = layer_norm(DEEPNORM_ALPHA * ctx + gt_c * merge(z_c, oc_f + oc_b, hc_f + hc_b), ln_g, ln_b)
    return x_new, ctx_new


def _fwd_setup_inputs(seed: int = 0) -> dict:
    key = jax.random.key(seed)
    ks = jax.random.split(key, 24)
    f32 = jnp.float32
    n = lambda k, s, sc: jax.random.normal(k, s, f32) * sc
    u_a = jax.random.uniform(ks[17], (DEPTH, 2, B_WIDTH), f32, 0.9, 0.999)
    s_a = u_a ** (1.0 / RG_C)
    return {
        "x": n(ks[0], (BATCH, SEQ, D_MODEL), 1.0),
        "c": n(ks[1], (BATCH, D_MODEL), 1.0),
        "ctx": n(ks[2], (BATCH, CTX_LEN, D_MODEL), 1.0),
        "c_ctx": n(ks[3], (D_MODEL,), 1.0),
        "w_mod": n(ks[4], (DEPTH, D_MODEL, 3 * D_MODEL), 0.5 * D_MODEL ** -0.5),
        "b_mod": n(ks[5], (DEPTH, 3 * D_MODEL), 0.01),
        "w_in": n(ks[6], (DEPTH, D_MODEL, IN_COLS), D_MODEL ** -0.5),
        "b_in": n(ks[7], (DEPTH, IN_COLS), 0.01),
        "lb_logits": n(ks[8], (DEPTH + 1, 2, A_WIDTH), 0.1),
        "norm_a_g": 1.0 + n(ks[9], (DEPTH, A_HEAD_DIM), 0.01),
        "conv_w": n(ks[10], (DEPTH, B_CONV, B_WIDTH), B_CONV ** -0.5),
        "conv_b": n(ks[11], (DEPTH, B_WIDTH), 0.01),
        "w_r": n(ks[12], (DEPTH, 2, B_BLOCKS, B_BLOCK_DIM, B_BLOCK_DIM), B_BLOCK_DIM ** -0.5),
        "b_r": n(ks[13], (DEPTH, 2, B_WIDTH), 0.01),
        "w_i": n(ks[14], (DEPTH, 2, B_BLOCKS, B_BLOCK_DIM, B_BLOCK_DIM), B_BLOCK_DIM ** -0.5),
        "b_i": n(ks[15], (DEPTH, 2, B_WIDTH), 0.01),
        "lam": jnp.log(s_a) - jnp.log1p(-s_a),
        "p_a": n(ks[18], (DEPTH, A_WIDTH, D_MODEL), DEEPNORM_BETA * A_WIDTH ** -0.5),
        "p_b": n(ks[19], (DEPTH, B_WIDTH, D_MODEL), DEEPNORM_BETA * B_WIDTH ** -0.5),
        "w_out": n(ks[20], (DEPTH, D_MODEL, D_MODEL), DEEPNORM_BETA * D_MODEL ** -0.5),
        "ln_g": 1.0 + n(ks[21], (DEPTH, D_MODEL), 0.01),
        "ln_b": n(ks[22], (DEPTH, D_MODEL), 0.01),
    }


def _fwd_reference(x, c, ctx, c_ctx, w_mod, b_mod, w_in, b_in, lb_logits, norm_a_g, conv_w, conv_b,
              w_r, b_r, w_i, b_i, lam, p_a, p_b, w_out, ln_g, ln_b):
    lb_all = jnp.cumsum(jax.nn.softmax(lb_logits.astype(jnp.float32), axis=0), axis=0)
    for layer in range(DEPTH):
        x, ctx = hybrid_layer(
            x, ctx, c, c_ctx, w_mod[layer], b_mod[layer], w_in[layer], b_in[layer], lb_all[layer],
            norm_a_g[layer], conv_w[layer], conv_b[layer], w_r[layer], b_r[layer], w_i[layer], b_i[layer],
            lam[layer], p_a[layer], p_b[layer], w_out[layer], ln_g[layer], ln_b[layer],
            last=(layer == DEPTH - 1))
    return x


import jax as _jax
import jax.numpy as _jnp

TWIN_FORMAT = 'train_step'
FWD_PARAMS = ['x', 'c', 'ctx', 'c_ctx', 'w_mod', 'b_mod', 'w_in', 'b_in', 'lb_logits', 'norm_a_g', 'conv_w', 'conv_b', 'w_r', 'b_r', 'w_i', 'b_i', 'lam', 'p_a', 'p_b', 'w_out', 'ln_g', 'ln_b']
TWIN_WEIGHTS = ['c_ctx', 'w_mod', 'b_mod', 'w_in', 'b_in', 'lb_logits', 'norm_a_g', 'conv_w', 'conv_b', 'w_r', 'b_r', 'w_i', 'b_i', 'lam', 'p_a', 'p_b', 'w_out', 'ln_g', 'ln_b']
TWIN_DIFF_INPUT = 'x'
TWIN_INPUTS = ['x', 'c', 'ctx', 'c_ctx', 'w_mod', 'b_mod', 'w_in', 'b_in', 'lb_logits', 'norm_a_g', 'conv_w', 'conv_b', 'w_r', 'b_r', 'w_i', 'b_i', 'lam', 'p_a', 'p_b', 'w_out', 'ln_g', 'ln_b', 'loss_target', 'm_c_ctx', 'm_w_mod', 'm_b_mod', 'm_w_in', 'm_b_in', 'm_lb_logits', 'm_norm_a_g', 'm_conv_w', 'm_conv_b', 'm_w_r', 'm_b_r', 'm_w_i', 'm_b_i', 'm_lam', 'm_p_a', 'm_p_b', 'm_w_out', 'm_ln_g', 'm_ln_b', 'v_c_ctx', 'v_w_mod', 'v_b_mod', 'v_w_in', 'v_b_in', 'v_lb_logits', 'v_norm_a_g', 'v_conv_w', 'v_conv_b', 'v_w_r', 'v_b_r', 'v_w_i', 'v_b_i', 'v_lam', 'v_p_a', 'v_p_b', 'v_w_out', 'v_ln_g', 'v_ln_b']
TWIN_OUTPUTS = ['loss', 'grad_x', 'grad_c_ctx', 'grad_w_mod', 'grad_b_mod', 'grad_w_in', 'grad_b_in', 'grad_lb_logits', 'grad_norm_a_g', 'grad_conv_w', 'grad_conv_b', 'grad_w_r', 'grad_b_r', 'grad_w_i', 'grad_b_i', 'grad_lam', 'grad_p_a', 'grad_p_b', 'grad_w_out', 'grad_ln_g', 'grad_ln_b', 'delta_c_ctx', 'delta_w_mod', 'delta_b_mod', 'delta_w_in', 'delta_b_in', 'delta_lb_logits', 'delta_norm_a_g', 'delta_conv_w', 'delta_conv_b', 'delta_w_r', 'delta_b_r', 'delta_w_i', 'delta_b_i', 'delta_lam', 'delta_p_a', 'delta_p_b', 'delta_w_out', 'delta_ln_g', 'delta_ln_b', 'new_m_c_ctx', 'new_m_w_mod', 'new_m_b_mod', 'new_m_w_in', 'new_m_b_in', 'new_m_lb_logits', 'new_m_norm_a_g', 'new_m_conv_w', 'new_m_conv_b', 'new_m_w_r', 'new_m_b_r', 'new_m_w_i', 'new_m_b_i', 'new_m_lam', 'new_m_p_a', 'new_m_p_b', 'new_m_w_out', 'new_m_ln_g', 'new_m_ln_b', 'new_v_c_ctx', 'new_v_w_mod', 'new_v_b_mod', 'new_v_w_in', 'new_v_b_in', 'new_v_lb_logits', 'new_v_norm_a_g', 'new_v_conv_w', 'new_v_conv_b', 'new_v_w_r', 'new_v_b_r', 'new_v_w_i', 'new_v_b_i', 'new_v_lam', 'new_v_p_a', 'new_v_p_b', 'new_v_w_out', 'new_v_ln_g', 'new_v_ln_b']
TWIN_LEAF_KINDS = {'loss': 'loss', 'grad_x': 'grad_x', 'grad_c_ctx': 'grad_w', 'grad_w_mod': 'grad_w', 'grad_b_mod': 'grad_w', 'grad_w_in': 'grad_w', 'grad_b_in': 'grad_w', 'grad_lb_logits': 'grad_w', 'grad_norm_a_g': 'grad_w', 'grad_conv_w': 'grad_w', 'grad_conv_b': 'grad_w', 'grad_w_r': 'grad_w', 'grad_b_r': 'grad_w', 'grad_w_i': 'grad_w', 'grad_b_i': 'grad_w', 'grad_lam': 'grad_w', 'grad_p_a': 'grad_w', 'grad_p_b': 'grad_w', 'grad_w_out': 'grad_w', 'grad_ln_g': 'grad_w', 'grad_ln_b': 'grad_w', 'delta_c_ctx': 'delta_w', 'delta_w_mod': 'delta_w', 'delta_b_mod': 'delta_w', 'delta_w_in': 'delta_w', 'delta_b_in': 'delta_w', 'delta_lb_logits': 'delta_w', 'delta_norm_a_g': 'delta_w', 'delta_conv_w': 'delta_w', 'delta_conv_b': 'delta_w', 'delta_w_r': 'delta_w', 'delta_b_r': 'delta_w', 'delta_w_i': 'delta_w', 'delta_b_i': 'delta_w', 'delta_lam': 'delta_w', 'delta_p_a': 'delta_w', 'delta_p_b': 'delta_w', 'delta_w_out': 'delta_w', 'delta_ln_g': 'delta_w', 'delta_ln_b': 'delta_w', 'new_m_c_ctx': 'new_m', 'new_m_w_mod': 'new_m', 'new_m_b_mod': 'new_m', 'new_m_w_in': 'new_m', 'new_m_b_in': 'new_m', 'new_m_lb_logits': 'new_m', 'new_m_norm_a_g': 'new_m', 'new_m_conv_w': 'new_m', 'new_m_conv_b': 'new_m', 'new_m_w_r': 'new_m', 'new_m_b_r': 'new_m', 'new_m_w_i': 'new_m', 'new_m_b_i': 'new_m', 'new_m_lam': 'new_m', 'new_m_p_a': 'new_m', 'new_m_p_b': 'new_m', 'new_m_w_out': 'new_m', 'new_m_ln_g': 'new_m', 'new_m_ln_b': 'new_m', 'new_v_c_ctx': 'new_v', 'new_v_w_mod': 'new_v', 'new_v_b_mod': 'new_v', 'new_v_w_in': 'new_v', 'new_v_b_in': 'new_v', 'new_v_lb_logits': 'new_v', 'new_v_norm_a_g': 'new_v', 'new_v_conv_w': 'new_v', 'new_v_conv_b': 'new_v', 'new_v_w_r': 'new_v', 'new_v_b_r': 'new_v', 'new_v_w_i': 'new_v', 'new_v_b_i': 'new_v', 'new_v_lam': 'new_v', 'new_v_p_a': 'new_v', 'new_v_p_b': 'new_v', 'new_v_w_out': 'new_v', 'new_v_ln_g': 'new_v', 'new_v_ln_b': 'new_v'}


def _forward(args):
    return _fwd_reference(*[args[k] for k in FWD_PARAMS])


def _output_shape():
    def fwd():
        inp = _fwd_setup_inputs(0)
        return _fwd_reference(*[inp[k] for k in FWD_PARAMS])
    out = _jax.eval_shape(fwd)
    return out.shape, out.dtype

N_MICROBATCH = 1
ADAM_LR = 0.001
ADAM_B1 = 0.9
ADAM_B2 = 0.999
ADAM_EPS = 1e-08
ADAM_WD = 0.01
ADAM_STEP = 10
PER_EXAMPLE_BATCH_AXIS = {'x': 0, 'c': 0, 'ctx': 0, 'loss_target': 0}
SHARED_INPUTS = []
_WEIGHT_DTYPES = {'c_ctx': _jnp.float32, 'w_mod': _jnp.float32, 'b_mod': _jnp.float32, 'w_in': _jnp.float32, 'b_in': _jnp.float32, 'lb_logits': _jnp.float32, 'norm_a_g': _jnp.float32, 'conv_w': _jnp.float32, 'conv_b': _jnp.float32, 'w_r': _jnp.float32, 'b_r': _jnp.float32, 'w_i': _jnp.float32, 'b_i': _jnp.float32, 'lam': _jnp.float32, 'p_a': _jnp.float32, 'p_b': _jnp.float32, 'w_out': _jnp.float32, 'ln_g': _jnp.float32, 'ln_b': _jnp.float32}
MOMENT_SCALE = {'c_ctx': 9.143768e-03, 'w_mod': 6.719040e-02, 'b_mod': 1.027271e-01, 'w_in': 1.935359e-02, 'b_in': 3.655724e-02, 'lb_logits': 3.844355e-04, 'norm_a_g': 2.578620e-02, 'conv_w': 4.536886e-02, 'conv_b': 1.128633e-01, 'w_r': 2.421882e-03, 'b_r': 2.823048e-03, 'w_i': 4.574747e-03, 'b_i': 7.299662e-03, 'lam': 7.604418e-03, 'p_a': 1.370504e-02, 'p_b': 6.934462e-02, 'w_out': 6.866445e-02, 'ln_g': 6.400265e+01, 'ln_b': 8.251824e-01}


def _to_microbatches(a, axis):
    t = _jnp.moveaxis(a, axis, 0)
    t = t.reshape((N_MICROBATCH, t.shape[0] // N_MICROBATCH) + t.shape[1:])
    return _jnp.moveaxis(t, 1, axis + 1)


def setup_inputs(seed: int = 0) -> dict:
    inp = _fwd_setup_inputs(seed)
    key = _jax.random.fold_in(_jax.random.key(seed), 7919)
    shape, _ = _output_shape()
    out = dict(inp)
    out["loss_target"] = _jax.random.normal(_jax.random.fold_in(key, 0), shape, _jnp.float32)
    for i, name in enumerate(TWIN_WEIGHTS):
        w = inp[name].astype(_jnp.float32)
        if MOMENT_SCALE is None:
            s = _jnp.sqrt(_jnp.mean(_jnp.square(w)) + 1e-30)
        else:
            s = MOMENT_SCALE[name]
        km, kv = _jax.random.split(_jax.random.fold_in(key, i + 1))
        out[name] = w
        out["m_" + name] = s * _jax.random.normal(km, w.shape, _jnp.float32)
        out["v_" + name] = (s * s) * _jax.random.uniform(kv, w.shape, _jnp.float32, 0.5, 1.5)
    if N_MICROBATCH > 1:
        for name, axis in PER_EXAMPLE_BATCH_AXIS.items():
            out[name] = _to_microbatches(out[name], axis)
    return {'x': out['x'], 'c': out['c'], 'ctx': out['ctx'], 'c_ctx': out['c_ctx'], 'w_mod': out['w_mod'], 'b_mod': out['b_mod'], 'w_in': out['w_in'], 'b_in': out['b_in'], 'lb_logits': out['lb_logits'], 'norm_a_g': out['norm_a_g'], 'conv_w': out['conv_w'], 'conv_b': out['conv_b'], 'w_r': out['w_r'], 'b_r': out['b_r'], 'w_i': out['w_i'], 'b_i': out['b_i'], 'lam': out['lam'], 'p_a': out['p_a'], 'p_b': out['p_b'], 'w_out': out['w_out'], 'ln_g': out['ln_g'], 'ln_b': out['ln_b'], 'loss_target': out['loss_target'], 'm_c_ctx': out['m_c_ctx'], 'm_w_mod': out['m_w_mod'], 'm_b_mod': out['m_b_mod'], 'm_w_in': out['m_w_in'], 'm_b_in': out['m_b_in'], 'm_lb_logits': out['m_lb_logits'], 'm_norm_a_g': out['m_norm_a_g'], 'm_conv_w': out['m_conv_w'], 'm_conv_b': out['m_conv_b'], 'm_w_r': out['m_w_r'], 'm_b_r': out['m_b_r'], 'm_w_i': out['m_w_i'], 'm_b_i': out['m_b_i'], 'm_lam': out['m_lam'], 'm_p_a': out['m_p_a'], 'm_p_b': out['m_p_b'], 'm_w_out': out['m_w_out'], 'm_ln_g': out['m_ln_g'], 'm_ln_b': out['m_ln_b'], 'v_c_ctx': out['v_c_ctx'], 'v_w_mod': out['v_w_mod'], 'v_b_mod': out['v_b_mod'], 'v_w_in': out['v_w_in'], 'v_b_in': out['v_b_in'], 'v_lb_logits': out['v_lb_logits'], 'v_norm_a_g': out['v_norm_a_g'], 'v_conv_w': out['v_conv_w'], 'v_conv_b': out['v_conv_b'], 'v_w_r': out['v_w_r'], 'v_b_r': out['v_b_r'], 'v_w_i': out['v_w_i'], 'v_b_i': out['v_b_i'], 'v_lam': out['v_lam'], 'v_p_a': out['v_p_a'], 'v_p_b': out['v_p_b'], 'v_w_out': out['v_w_out'], 'v_ln_g': out['v_ln_g'], 'v_ln_b': out['v_ln_b']}


def _loss(weights, diff, rest, loss_target):
    with _jax.named_scope("forward"):
        args = {**rest, TWIN_DIFF_INPUT: diff, **{k: w.astype(_WEIGHT_DTYPES[k]) for k, w in weights.items()}}
        y = _forward(args)
    with _jax.named_scope("loss_head"):
        err = _jnp.square(y.astype(_jnp.float32) - loss_target)
        return 0.5 * _jnp.sum(_jnp.mean(err, axis=-1)) if err.ndim else 0.5 * err


def _adamw(w, g, m, v):
    m = ADAM_B1 * m + (1.0 - ADAM_B1) * g
    v = ADAM_B2 * v + (1.0 - ADAM_B2) * _jnp.square(g)
    m_hat = m / (1.0 - ADAM_B1 ** ADAM_STEP)
    v_hat = v / (1.0 - ADAM_B2 ** ADAM_STEP)
    delta = -ADAM_LR * (m_hat / (_jnp.sqrt(v_hat) + ADAM_EPS) + ADAM_WD * w)
    return delta, m, v


def reference(x, c, ctx, c_ctx, w_mod, b_mod, w_in, b_in, lb_logits, norm_a_g, conv_w, conv_b, w_r, b_r, w_i, b_i, lam, p_a, p_b, w_out, ln_g, ln_b, loss_target, m_c_ctx, m_w_mod, m_b_mod, m_w_in, m_b_in, m_lb_logits, m_norm_a_g, m_conv_w, m_conv_b, m_w_r, m_b_r, m_w_i, m_b_i, m_lam, m_p_a, m_p_b, m_w_out, m_ln_g, m_ln_b, v_c_ctx, v_w_mod, v_b_mod, v_w_in, v_b_in, v_lb_logits, v_norm_a_g, v_conv_w, v_conv_b, v_w_r, v_b_r, v_w_i, v_b_i, v_lam, v_p_a, v_p_b, v_w_out, v_ln_g, v_ln_b):
    given = dict(x=x, c=c, ctx=ctx, c_ctx=c_ctx, w_mod=w_mod, b_mod=b_mod, w_in=w_in, b_in=b_in, lb_logits=lb_logits, norm_a_g=norm_a_g, conv_w=conv_w, conv_b=conv_b, w_r=w_r, b_r=b_r, w_i=w_i, b_i=b_i, lam=lam, p_a=p_a, p_b=p_b, w_out=w_out, ln_g=ln_g, ln_b=ln_b, loss_target=loss_target, m_c_ctx=m_c_ctx, m_w_mod=m_w_mod, m_b_mod=m_b_mod, m_w_in=m_w_in, m_b_in=m_b_in, m_lb_logits=m_lb_logits, m_norm_a_g=m_norm_a_g, m_conv_w=m_conv_w, m_conv_b=m_conv_b, m_w_r=m_w_r, m_b_r=m_b_r, m_w_i=m_w_i, m_b_i=m_b_i, m_lam=m_lam, m_p_a=m_p_a, m_p_b=m_p_b, m_w_out=m_w_out, m_ln_g=m_ln_g, m_ln_b=m_ln_b, v_c_ctx=v_c_ctx, v_w_mod=v_w_mod, v_b_mod=v_b_mod, v_w_in=v_w_in, v_b_in=v_b_in, v_lb_logits=v_lb_logits, v_norm_a_g=v_norm_a_g, v_conv_w=v_conv_w, v_conv_b=v_conv_b, v_w_r=v_w_r, v_b_r=v_b_r, v_w_i=v_w_i, v_b_i=v_b_i, v_lam=v_lam, v_p_a=v_p_a, v_p_b=v_p_b, v_w_out=v_w_out, v_ln_g=v_ln_g, v_ln_b=v_ln_b)
    weights = {n: given[n] for n in TWIN_WEIGHTS}
    shared = {n: given[n] for n in SHARED_INPUTS}
    per_example = {n: given[n] for n in ['x', 'c', 'ctx']}
    grad_fn = _jax.value_and_grad(_loss, argnums=(0, 1))

    def one_microbatch(ex, loss_target):
        ex = dict(ex)
        diff = ex.pop(TWIN_DIFF_INPUT)
        return grad_fn(weights, diff, {**shared, **ex}, loss_target)

    if N_MICROBATCH == 1:
        loss, (grad_w, grad_x) = one_microbatch(per_example, given["loss_target"])
    else:
        def body(carry, xs):
            loss_sum, grad_sum = carry
            l_k, (gw_k, gx_k) = one_microbatch(xs[0], xs[1])
            with _jax.named_scope("update"):
                return (loss_sum + l_k, _jax.tree.map(_jnp.add, grad_sum, gw_k)), gx_k

        init = (_jnp.zeros((), _jnp.float32), _jax.tree.map(_jnp.zeros_like, weights))
        (loss, grad_w), grad_x = _jax.lax.scan(body, init, (per_example, given["loss_target"]))
    with _jax.named_scope("update"):
        delta_w, new_m, new_v = {}, {}, {}
        for n in TWIN_WEIGHTS:
            delta_w[n], new_m[n], new_v[n] = _adamw(weights[n], grad_w[n], given["m_" + n], given["v_" + n])
    return (loss, grad_x, *[grad_w[n] for n in TWIN_WEIGHTS], *[delta_w[n] for n in TWIN_WEIGHTS],
            *[new_m[n] for n in TWIN_WEIGHTS], *[new_v[n] for n in TWIN_WEIGHTS])
```

```python
import functools

import jax
import jax.numpy as jnp
from jax import lax
from jax.experimental import pallas as pl
from jax.experimental.pallas import tpu as pltpu

F32 = jnp.float32
BF = jnp.bfloat16
MESH = pl.DeviceIdType.MESH

D = 1024
H = 8
DH = 128
CH = 64
RB = 256
NCK = RB // CH
GRID_W = 64
IN_COLS = 9 * D
NSH = 4
SHC = IN_COLS // NSH
RG_C = 8.0
ALPHA = 2.0 ** 0.25
LN_EPS = 1e-5
RMS_EPS = 1e-6
Q_SCALE = DH ** -0.5
ADAM_LR, ADAM_B1, ADAM_B2, ADAM_EPS, ADAM_WD, ADAM_STEP = 0.001, 0.9, 0.999, 1e-08, 0.01, 10
VMEM_LIMIT = 56 * 1024 * 1024


def _cp(sem=None):
    return pltpu.CompilerParams(dimension_semantics=sem, vmem_limit_bytes=VMEM_LIMIT)


def _sig(x):
    return 1.0 / (1.0 + jnp.exp(-x))


def _dot(a, b):
    return jnp.dot(a.astype(BF), b.astype(BF), preferred_element_type=F32)


def _dot_nt(a, b):
    return lax.dot_general(a.astype(BF), b.astype(BF), (((1,), (1,)), ((), ())), preferred_element_type=F32)


def _dot_tn(a, b):
    return lax.dot_general(a.astype(BF), b.astype(BF), (((0,), (0,)), ((), ())), preferred_element_type=F32)


def _dot_hi(a, b):
    return jnp.dot(a, b, preferred_element_type=F32, precision=lax.Precision.HIGHEST)


def _colsum(v):
    return jnp.sum(v, axis=0, keepdims=True)


def _mod_fwd(cc, w_mod_g, b_mod):
    def body(cc_ref, w_ref, b_ref, o_ref):
        v = cc_ref[...]
        s = v * _sig(v)
        for k in range(NSH):
            cs = slice(k * 768, (k + 1) * 768)
            o_ref[:, cs] = _dot(s, w_ref[k]) + b_ref[:, cs]
    return pl.pallas_call(body, out_shape=jax.ShapeDtypeStruct((16, 3 * D), F32), name="mod_fwd",
                          compiler_params=_cp())(cc, w_mod_g, b_mod)


def _mod_bwd(cc, dmod, w_mod_g):
    def body(cc_ref, dm_ref, w_ref, dw_ref, dcc_ref):
        v = cc_ref[...]
        sg = _sig(v)
        s = v * sg
        ds = jnp.zeros((16, D), F32)
        for k in range(NSH):
            dm = dm_ref[:, k * 768:(k + 1) * 768]
            dw_ref[k] = _dot_tn(s, dm)
            ds = ds + _dot_nt(dm, w_ref[k])
        dcc_ref[...] = ds * (sg * (1.0 + v * (1.0 - sg)))
    return pl.pallas_call(body, out_shape=(jax.ShapeDtypeStruct((NSH, D, 768), F32),
                                           jax.ShapeDtypeStruct((16, D), F32)),
                          name="mod_bwd", compiler_params=_cp())(cc, dmod, w_mod_g)


def _modulate(x_ref, mod_ref, is_lat):
    sh = jnp.where(is_lat, mod_ref[0:1, 0:D], mod_ref[1:2, 0:D])
    sc = jnp.where(is_lat, mod_ref[0:1, D:2 * D], mod_ref[1:2, D:2 * D])
    return x_ref[...] * (1.0 + sc) + sh, sc


def _inproj(x_all, mod, w_in_g, b_in, n_lat_tiles):
    m = x_all.shape[0]
    tm = 256
    nt = m // tm

    def body(x_ref, mod_ref, w_ref, b_ref, z_ref):
        i = pl.program_id(1)
        u, _ = _modulate(x_ref, mod_ref, i < n_lat_tiles)
        z_ref[...] = _dot(u, w_ref[0]) + b_ref[...]

    return pl.pallas_call(
        body, grid=(NSH, nt),
        in_specs=[pl.BlockSpec((tm, D), lambda n, i: (i, 0)),
                  pl.BlockSpec((16, 3 * D), lambda n, i: (0, 0)),
                  pl.BlockSpec((1, D, SHC), lambda n, i: (n, 0, 0)),
                  pl.BlockSpec((1, SHC), lambda n, i: (0, n))],
        out_specs=pl.BlockSpec((tm, SHC), lambda n, i: (i, n)),
        out_shape=jax.ShapeDtypeStruct((m, IN_COLS), F32), name="inproj",
        compiler_params=_cp(("arbitrary", "arbitrary")))(x_all, mod, w_in_g, b_in)


def _gla_masks(d):
    row = lax.broadcasted_iota(jnp.int32, (CH, CH), 0)
    col = lax.broadcasted_iota(jnp.int32, (CH, CH), 1)
    lower, upper = row >= col, row <= col
    mb, mt = (upper, lower) if d else (lower, upper)
    return mb, mb.astype(F32), mt.astype(F32)


def _lower_bound(lbl_ref, d):
    return _sig(lbl_ref[0, d:d + 1, :] - lbl_ref[1, d:d + 1, :])


def _gla_rb(d, nlb, ncb):
    nrb = nlb + ncb
    if d:
        return lambda s: nrb - 1 - s
    return lambda s: jnp.where(s < ncb, nlb + s, s - ncb)


def _gla_fwd(z, lbl, d, nlb, ncb):
    m = z.shape[0]
    nrb = nlb + ncb
    rb = _gla_rb(d, nlb, ncb)

    def body(q_ref, f_ref, v_ref, lbl_ref, o_ref, st_ref, S):
        s = pl.program_id(0)

        @pl.when(s == 0)
        def _():
            S[...] = jnp.zeros_like(S)

        lb = _lower_bound(lbl_ref, d)
        mb, mf, _ = _gla_masks(d)

        def chunk(j, carry):
            jj = (NCK - 1 - j) if d else j
            r0 = pl.multiple_of(jj * CH, CH)
            zq = q_ref[pl.ds(r0, CH), :]
            zf = f_ref[pl.ds(r0, CH), :]
            zv = v_ref[pl.ds(r0, CH), :]
            q = zq * _sig(zq) * Q_SCALE
            f = lb + (1.0 - lb) * _sig(zf)
            k = 1.0 - f
            logf = jnp.log(f)
            for h in range(H):
                hs = slice(h * DH, (h + 1) * DH)
                lf = logf[:, hs]
                g = _dot_hi(mf, lf)
                gl = _colsum(lf)
                qd = q[:, hs] * jnp.exp(g)
                ki = k[:, hs] * jnp.exp(-g)
                ke = k[:, hs] * jnp.exp(gl - g)
                a = jnp.where(mb, _dot_nt(qd, ki), 0.0)
                st = S[h]
                st_ref[jj, h] = st
                o_ref[pl.ds(r0, CH), hs] = _dot(a, zv[:, hs]) + _dot_nt(qd, st)
                S[h] = st * jnp.exp(gl) + _dot_tn(zv[:, hs], ke)
            return carry

        lax.fori_loop(0, NCK, chunk, 0)

    def zspec(cb):
        return pl.BlockSpec((RB, D), lambda s: (rb(s), cb))

    return pl.pallas_call(
        body, grid=(nrb,),
        in_specs=[zspec(0), zspec(1 + d), zspec(3), pl.BlockSpec((2, 2, D), lambda s: (0, 0, 0))],
        out_specs=(pl.BlockSpec((RB, D), lambda s: (rb(s), 0)),
                   pl.BlockSpec((NCK, H, DH, DH), lambda s: (rb(s), 0, 0, 0))),
        out_shape=(jax.ShapeDtypeStruct((m, D), F32),
                   jax.ShapeDtypeStruct((m // CH, H, DH, DH), F32)),
        scratch_shapes=[pltpu.VMEM((H, DH, DH), F32)],
        name=f"gla_fwd{d}", compiler_params=_cp(("arbitrary",)))(z, z, z, lbl)


def _gla_bwd(z, lbl, do_lat, states, d, nlb, ncb, prev=None):
    m = z.shape[0]
    nrb = nlb + ncb
    fwd_rb = _gla_rb(d, nlb, ncb)
    rb = lambda s: fwd_rb(nrb - 1 - s)
    last = prev is not None

    def body(*refs):
        if last:
            q_ref, f_ref, v_ref, lbl_ref, do_ref, st_ref, pq_ref, pv_ref, o0_ref, o1_ref, o2_ref, sum_ref, dS = refs
        else:
            q_ref, f_ref, v_ref, lbl_ref, do_ref, st_ref, o0_ref, o1_ref, o2_ref, sum_ref, dS = refs
        s = pl.program_id(0)
        is_lat = rb(s) < nlb

        @pl.when(s == 0)
        def _():
            dS[...] = jnp.zeros_like(dS)
            sum_ref[...] = jnp.zeros_like(sum_ref)

        lb = _lower_bound(lbl_ref, d)
        mb, mf, mtf = _gla_masks(d)

        def chunk(j, carry):
            jj = j if d else (NCK - 1 - j)
            r0 = pl.multiple_of(jj * CH, CH)
            zq = q_ref[pl.ds(r0, CH), :]
            zf = f_ref[pl.ds(r0, CH), :]
            zv = v_ref[pl.ds(r0, CH), :]
            do = jnp.where(is_lat, do_ref[pl.ds(r0, CH), :], 0.0)
            sq = _sig(zq)
            q = zq * sq * Q_SCALE
            sf = _sig(zf)
            f = lb + (1.0 - lb) * sf
            k = 1.0 - f
            logf = jnp.log(f)
            dqs, dvs, dfs = [], [], []
            for h in range(H):
                hs = slice(h * DH, (h + 1) * DH)
                lf = logf[:, hs]
                g = _dot_hi(mf, lf)
                gl = _colsum(lf)
                eg = jnp.exp(g)
                eig = jnp.exp(-g)
                eeg = jnp.exp(gl - g)
                dec = jnp.exp(gl)
                qd = q[:, hs] * eg
                ki = k[:, hs] * eig
                ke = k[:, hs] * eeg
                v = zv[:, hs]
                doh = do[:, hs]
                a = jnp.where(mb, _dot_nt(qd, ki), 0.0)
                st = st_ref[jj, h]
                dst = dS[h]
                da = jnp.where(mb, _dot_nt(doh, v), 0.0)
                dv = _dot_tn(a, doh) + _dot_nt(ke, dst)
                dqd = _dot(da, ki) + _dot(doh, st)
                dki = _dot_tn(da, qd)
                dke = _dot(v, dst)
                dS[h] = _dot_tn(doh, qd) + dst * dec
                ddec = _colsum(st * dst)
                dq = dqd * eg
                dk = dki * eig + dke * eeg
                dke_ke = dke * ke
                dg = dqd * qd - dki * ki - dke_ke
                dgl = _colsum(dke_ke) + ddec * dec
                dlogf = _dot_hi(mtf, dg) + dgl
                dqs.append(dq)
                dvs.append(dv)
                dfs.append(dlogf / f[:, hs] - dk)
            dq = jnp.concatenate(dqs, axis=1)
            dv = jnp.concatenate(dvs, axis=1)
            df = jnp.concatenate(dfs, axis=1)
            dzf = df * (1.0 - lb) * (sf * (1.0 - sf))
            sum_ref[0:1, :] += _colsum(dzf)
            sum_ref[1:2, :] += _colsum(df * (1.0 - sf))
            rows = pl.ds(r0, CH)
            if last:
                dz0 = (dq + pq_ref[rows, :]) * (Q_SCALE * (sq * (1.0 + zq * (1.0 - sq))))
                dz3 = dv + pv_ref[rows, :]
                sum_ref[2:3, :] += _colsum(dz0)
                sum_ref[3:4, :] += _colsum(dz3)
                o0_ref[rows, :] = dz0.astype(BF)
                o1_ref[rows, :] = dz3.astype(BF)
            else:
                o0_ref[rows, :] = dq
                o1_ref[rows, :] = dv
            o2_ref[rows, :] = dzf.astype(BF)
            return carry

        lax.fori_loop(0, NCK, chunk, 0)

        @pl.when(s == nrb - 1)
        def _():
            sum_ref[1:2, :] = sum_ref[1:2, :] * (lb * (1.0 - lb))

    def zspec(cb):
        return pl.BlockSpec((RB, D), lambda s: (rb(s), cb))

    rowspec = pl.BlockSpec((RB, D), lambda s: (rb(s), 0))
    in_specs = [zspec(0), zspec(1 + d), zspec(3), pl.BlockSpec((2, 2, D), lambda s: (0, 0, 0)),
                pl.BlockSpec((RB, D), lambda s: (jnp.minimum(rb(s), nlb - 1), 0)),
                pl.BlockSpec((NCK, H, DH, DH), lambda s: (rb(s), 0, 0, 0))]
    args = [z, z, z, lbl, do_lat, states]
    if last:
        in_specs += [rowspec, rowspec]
        args += list(prev)
    dt01 = BF if last else F32
    return pl.pallas_call(
        body, grid=(nrb,), in_specs=in_specs,
        out_specs=(rowspec, rowspec, rowspec, pl.BlockSpec((8, D), lambda s: (0, 0))),
        out_shape=(jax.ShapeDtypeStruct((m, D), dt01), jax.ShapeDtypeStruct((m, D), dt01),
                   jax.ShapeDtypeStruct((m, D), BF), jax.ShapeDtypeStruct((8, D), F32)),
        scratch_shapes=[pltpu.VMEM((H, DH, DH), F32)],
        name=f"gla_bwd{d}", compiler_params=_cp(("arbitrary",)))(*args)


def _shift_rows(v, k, r):
    row = lax.broadcasted_iota(jnp.int32, v.shape, 0)
    rolled = pltpu.roll(v, k % r, 0)
    return jnp.where((row >= k) & (row < r + k), rolled, 0.0)


def _conv_fwd(xb, cw_ref, cb_ref, r):
    return (cb_ref[...] + _shift_rows(xb, 1, r) * cw_ref[0:1, :] + xb * cw_ref[1:2, :]
            + _shift_rows(xb, -1, r) * cw_ref[2:3, :] + _shift_rows(xb, -2, r) * cw_ref[3:4, :])


def _softplus_neg(lam):
    y = jnp.exp(-jnp.abs(lam))
    u = 1.0 + y
    tiny = u == 1.0
    l1p = jnp.where(tiny, y, jnp.log(u) * (y / jnp.where(tiny, 1.0, u - 1.0)))
    return jnp.maximum(-lam, 0.0) + l1p


def _gates(xc, wr_ref, br, wi_ref, bi, sp):
    xcb = xc.astype(BF)
    rs, is_ = [], []
    for g in range(H):
        gs = slice(g * DH, (g + 1) * DH)
        rs.append(jnp.dot(xcb[:, gs], wr_ref[g].astype(BF), preferred_element_type=F32))
        is_.append(jnp.dot(xcb[:, gs], wi_ref[g].astype(BF), preferred_element_type=F32))
    r = _sig(jnp.concatenate(rs, axis=1) + br)
    i = _sig(jnp.concatenate(is_, axis=1) + bi)
    log_a = (-RG_C * r) * sp
    a = jnp.exp(log_a)
    t = jnp.tanh(log_a)
    om = (-2.0 * t) / (1.0 - t)
    return r, i, a, om


def _scan_rows(d, nrows, a_s, b_s, h_s, h0):
    nsl = nrows // 8

    def slab(j, h):
        jj = (nsl - 1 - j) if d else j
        r0 = pl.multiple_of(jj * 8, 8)
        for t in (range(7, -1, -1) if d else range(8)):
            h = a_s[pl.ds(r0 + t, 1), :] * h + b_s[pl.ds(r0 + t, 1), :]
            h_s[pl.ds(r0 + t, 1), :] = h
        return h

    return lax.fori_loop(0, nsl, slab, h0)


def _col_of(d, ncols):
    if d:
        return lambda s: ncols - jnp.maximum(s, 1)
    return lambda s: jnp.maximum(s, 1) - 1


def _rglru_fwd(z, cw, cb, wr, br, wi, bi, lam, d, t_lat, t_ctx):
    m = z.shape[0]
    rows = t_lat // GRID_W
    zcol = z.reshape(m // GRID_W, GRID_W * IN_COLS)
    col = _col_of(d, GRID_W)
    cblk = t_lat // t_ctx
    rmax = max(rows, t_ctx)

    def body(zc_ref, zx_ref, cw_ref, cb_ref, wr_ref, br_ref, wi_ref, bi_ref, lam_ref,
             hx_ref, hpx_ref, hpc_ref, a_s, b_s, h_s, hcar):
        s = pl.program_id(0)
        sp = _softplus_neg(lam_ref[d:d + 1, :])
        br_ = br_ref[d:d + 1, :]
        bi_ = bi_ref[d:d + 1, :]

        def run(x_ref, r, h0, h_out, hp_out):
            xb = x_ref[...]
            xc = _conv_fwd(xb, cw_ref, cb_ref, r)
            _, i, a, om = _gates(xc, wr_ref, br_, wi_ref, bi_, sp)
            a_s[0:r, :] = a
            b_s[0:r, :] = jnp.sqrt(om) * (i * xc)
            hl = _scan_rows(d, r, a_s, b_s, h_s, h0)
            hs = h_s[0:r, :]
            row = lax.broadcasted_iota(jnp.int32, (r, D), 0)
            if d:
                hp = jnp.where(row == r - 1, h0, pltpu.roll(hs, r - 1, 0))
            else:
                hp = jnp.where(row == 0, h0, pltpu.roll(hs, 1, 0))
            if h_out is not None:
                h_out[...] = hs
            hp_out[...] = hp
            hcar[0:1, :] = hl

        @pl.when(s == 0)
        def _():
            run(zc_ref, t_ctx, jnp.zeros((1, D), F32), None, hpc_ref)

        @pl.when(s > 0)
        def _():
            run(zx_ref, rows, hcar[0:1, :], hx_ref, hpx_ref)

    full = lambda shp: pl.BlockSpec(shp, lambda s: (0,) * len(shp))
    colspec = pl.BlockSpec((rows, D), lambda s: (0, col(s)))
    outs = pl.pallas_call(
        body, grid=(GRID_W + 1,),
        in_specs=[pl.BlockSpec((t_ctx, D), lambda s: (cblk, 5)),
                  pl.BlockSpec((rows, D), lambda s: (0, col(s) * 9 + 5)),
                  full((4, D)), full((1, D)),
                  pl.BlockSpec((None, H, DH, DH), lambda s: (d, 0, 0, 0)), full((2, D)),
                  pl.BlockSpec((None, H, DH, DH), lambda s: (d, 0, 0, 0)), full((2, D)), full((2, D))],
        out_specs=(colspec, colspec, full((t_ctx, D))),
        out_shape=(jax.ShapeDtypeStruct((rows, GRID_W * D), F32), jax.ShapeDtypeStruct((rows, GRID_W * D), F32),
                   jax.ShapeDtypeStruct((t_ctx, D), F32)),
        scratch_shapes=[pltpu.VMEM((rmax, D), F32), pltpu.VMEM((rmax, D), F32), pltpu.VMEM((rmax, D), F32),
                        pltpu.VMEM((8, D), F32)],
        name=f"rglru_fwd{d}", compiler_params=_cp(("arbitrary",)))(z, zcol, cw, cb, wr, br, wi, bi, lam)
    return outs[0].reshape(t_lat, D), outs[1].reshape(t_lat, D), outs[2]


def _rglru_bwd(z, cw, cb, wr, br, wi, bi, lam, dh_lat, hp_lat, hp_ctx, d, t_lat, t_ctx, prev=None):
    m = z.shape[0]
    rows = t_lat // GRID_W
    zcol = z.reshape(m // GRID_W, GRID_W * IN_COLS)
    fcol = _col_of(d, GRID_W)
    col = lambda s: fcol(GRID_W - jnp.minimum(s, GRID_W - 1))
    cblk = t_lat // t_ctx
    rmax = max(rows, t_ctx)
    last = prev is not None
    dview = lambda v: v.reshape(rows, GRID_W * D)

    def body(*refs):
        (zc_ref, zx_ref, cw_ref, cb_ref, wr_ref, br_ref, wi_ref, bi_ref, lam_ref, dh_ref, hpx_ref, hpc_ref) = refs[:12]
        k = 12
        if last:
            pdx_ref, pdc_ref = refs[12:14]
            k = 14
        ox_ref, oc_ref, dwr_ref, dwi_ref, sum_ref, a_s, b_s, g_s, gcar = refs[k:]
        s = pl.program_id(0)
        lam_d = lam_ref[d:d + 1, :]
        sp = _softplus_neg(lam_d)
        br_ = br_ref[d:d + 1, :]
        bi_ = bi_ref[d:d + 1, :]

        @pl.when(s == 0)
        def _():
            gcar[...] = jnp.zeros_like(gcar)
            dwr_ref[...] = jnp.zeros_like(dwr_ref)
            dwi_ref[...] = jnp.zeros_like(dwi_ref)
            sum_ref[...] = jnp.zeros_like(sum_ref)

        def run(x_ref, r, dout, hp_ref, prev_ref, o_ref):
            xb = x_ref[...]
            xc = _conv_fwd(xb, cw_ref, cb_ref, r)
            rr, i, a, om = _gates(xc, wr_ref, br_, wi_ref, bi_, sp)
            mult = jnp.sqrt(om)
            a_s[0:r, :] = a
            b_s[0:r, :] = a * dout
            c0 = gcar[0:1, :]
            cl = _scan_rows(1 - d, r, a_s, b_s, g_s, c0)
            cs = g_s[0:r, :]
            row = lax.broadcasted_iota(jnp.int32, (r, D), 0)
            if d:
                cn = jnp.where(row == 0, c0, pltpu.roll(cs, 1, 0))
            else:
                cn = jnp.where(row == r - 1, c0, pltpu.roll(cs, r - 1, 0))
            gcar[0:1, :] = cl
            g = dout + cn
            da = g * hp_ref[...]
            ixc = i * xc
            dmult = g * ixc
            dixc = g * mult
            di = dixc * xc
            dxc = dixc * i
            dlog_a = da * a - dmult * ((1.0 - om) / mult)
            dr = dlog_a * (-RG_C * sp)
            sum_ref[2:3, :] += _colsum(dlog_a * rr)
            drp = dr * rr * (1.0 - rr)
            dip = di * i * (1.0 - i)
            sum_ref[0:1, :] += _colsum(drp)
            sum_ref[1:2, :] += _colsum(dip)
            xcb = xc.astype(BF)
            drb = drp.astype(BF)
            dib = dip.astype(BF)
            parts = []
            for gi in range(H):
                gs = slice(gi * DH, (gi + 1) * DH)
                parts.append(_dot_nt(drb[:, gs], wr_ref[gi]) + _dot_nt(dib[:, gs], wi_ref[gi]))
                dwr_ref[gi] += _dot_tn(xcb[:, gs], drb[:, gs])
                dwi_ref[gi] += _dot_tn(xcb[:, gs], dib[:, gs])
            dxc = dxc + jnp.concatenate(parts, axis=1)
            if last:
                dxc = dxc + prev_ref[...]
                dxb = (_shift_rows(dxc, -1, r) * cw_ref[0:1, :] + dxc * cw_ref[1:2, :]
                       + _shift_rows(dxc, 1, r) * cw_ref[2:3, :] + _shift_rows(dxc, 2, r) * cw_ref[3:4, :])
                sum_ref[3:4, :] += _colsum(dxc)
                sum_ref[4:5, :] += _colsum(dxb)
                sum_ref[8:9, :] += _colsum(dxc * _shift_rows(xb, 1, r))
                sum_ref[9:10, :] += _colsum(dxc * xb)
                sum_ref[10:11, :] += _colsum(dxc * _shift_rows(xb, -1, r))
                sum_ref[11:12, :] += _colsum(dxc * _shift_rows(xb, -2, r))
                o_ref[...] = dxb.astype(BF)
            else:
                o_ref[...] = dxc

        @pl.when(s < GRID_W)
        def _():
            run(zx_ref, rows, dh_ref[...], hpx_ref, pdx_ref if last else None, ox_ref)

        @pl.when(s == GRID_W)
        def _():
            run(zc_ref, t_ctx, jnp.zeros((t_ctx, D), F32), hpc_ref, pdc_ref if last else None, oc_ref)
            sum_ref[2:3, :] = sum_ref[2:3, :] * (RG_C * _sig(-lam_d))

    full = lambda shp: pl.BlockSpec(shp, lambda s: (0,) * len(shp))
    colspec = pl.BlockSpec((rows, D), lambda s: (0, col(s)))
    in_specs = [pl.BlockSpec((t_ctx, D), lambda s: (cblk, 5)),
                pl.BlockSpec((rows, D), lambda s: (0, col(s) * 9 + 5)),
                full((4, D)), full((1, D)),
                pl.BlockSpec((None, H, DH, DH), lambda s: (d, 0, 0, 0)), full((2, D)),
                pl.BlockSpec((None, H, DH, DH), lambda s: (d, 0, 0, 0)), full((2, D)), full((2, D)),
                colspec, colspec, full((t_ctx, D))]
    args = [z, zcol, cw, cb, wr, br, wi, bi, lam, dview(dh_lat), dview(hp_lat), hp_ctx]
    if last:
        in_specs += [colspec, full((t_ctx, D))]
        args += [dview(prev[0]), prev[1]]
    odt = BF if last else F32
    outs = pl.pallas_call(
        body, grid=(GRID_W + 1,), in_specs=in_specs,
        out_specs=(colspec, full((t_ctx, D)), full((H, DH, DH)), full((H, DH, DH)), full((16, D))),
        out_shape=(jax.ShapeDtypeStruct((rows, GRID_W * D), odt), jax.ShapeDtypeStruct((t_ctx, D), odt),
                   jax.ShapeDtypeStruct((H, DH, DH), F32), jax.ShapeDtypeStruct((H, DH, DH), F32),
                   jax.ShapeDtypeStruct((16, D), F32)),
        scratch_shapes=[pltpu.VMEM((rmax, D), F32), pltpu.VMEM((rmax, D), F32), pltpu.VMEM((rmax, D), F32),
                        pltpu.VMEM((8, D), F32)],
        name=f"rglru_bwd{d}", compiler_params=_cp(("arbitrary",)))(*args)
    return (outs[0].reshape(t_lat, D), outs[1]) + tuple(outs[2:])


def _merge(o_f, o_b, h_f, h_b, z, x_all, tgt, mod, norm_g, ln_g, ln_b, p_a, p_b, w_out, t_lat):
    tm = 128
    nt = t_lat // tm

    def body(of_ref, ob_ref, hf_ref, hb_ref, z4_ref, z6_ref, z7_ref, z8_ref, x_ref, t_ref, mod_ref, ng_ref,
             lg_ref, lb_ref, pa_ref, pb_ref, wo_ref,
             do_ref, dh_ref, dz4_ref, dz6_ref, dz7_ref, dz8_ref, gx_ref,
             y_ref, dout_ref, oa_ref, dpa_ref, obv_ref, dpb_ref, acc_ref):
        i = pl.program_id(0)

        @pl.when(i == 0)
        def _():
            acc_ref[...] = jnp.zeros_like(acc_ref)

        def per_head(v):
            return jnp.concatenate(
                [jnp.broadcast_to(jnp.mean(v[:, h * DH:(h + 1) * DH], axis=-1, keepdims=True), (tm, DH))
                 for h in range(H)], axis=1)

        gt = mod_ref[0:1, 2 * D:3 * D]
        gfull = jnp.concatenate([ng_ref[...]] * H, axis=1)
        o = of_ref[...] + ob_ref[...]
        rinv = lax.rsqrt(per_head(o * o) + RMS_EPS)
        n = o * rinv
        na = n * gfull
        z4 = z4_ref[...]
        s4 = _sig(z4)
        silu4 = z4 * s4
        oa = na * silu4
        z6 = z6_ref[...]
        s6 = _sig(z6)
        silu6 = z6 * s6
        hsum = hf_ref[...] + hb_ref[...]
        obv = hsum * silu6
        pa = _dot(oa, pa_ref[...])
        pb = _dot(obv, pb_ref[...])
        s7 = _sig(z7_ref[...])
        s8 = _sig(z8_ref[...])
        y = s7 * pa + s8 * pb
        out = _dot(y, wo_ref[...])
        pre = ALPHA * x_ref[...] + gt * out
        mu = jnp.mean(pre, axis=-1, keepdims=True)
        xc = pre - mu
        rstd = lax.rsqrt(jnp.mean(xc * xc, axis=-1, keepdims=True) + LN_EPS)
        xhat = xc * rstd
        lg = lg_ref[...]
        diff = xhat * lg + lb_ref[...] - t_ref[...]
        acc_ref[8:9, :] += _colsum(diff * diff) * (0.5 / D)
        dxn = diff * (1.0 / D)
        acc_ref[1:2, :] += _colsum(dxn * xhat)
        acc_ref[2:3, :] += _colsum(dxn)
        dxhat = dxn * lg
        dpre = rstd * (dxhat - jnp.mean(dxhat, axis=-1, keepdims=True)
                       - xhat * jnp.mean(dxhat * xhat, axis=-1, keepdims=True))
        gx_ref[...] = ALPHA * dpre
        acc_ref[0:1, :] += _colsum(dpre * out)
        dout = dpre * gt
        dy = _dot_nt(dout, wo_ref[...])
        dpa = dy * s7
        dpb = dy * s8
        dz7 = dy * pa * (s7 * (1.0 - s7))
        dz8 = dy * pb * (s8 * (1.0 - s8))
        doa = _dot_nt(dpa, pa_ref[...])
        dob = _dot_nt(dpb, pb_ref[...])
        dh_ref[...] = dob * silu6
        dz6 = dob * hsum * (s6 * (1.0 + z6 * (1.0 - s6)))
        dna = doa * silu4
        dz4 = doa * na * (s4 * (1.0 + z4 * (1.0 - s4)))
        dng = _colsum(dna * n)
        acc_ref[7:8, 0:DH] += sum(dng[:, h * DH:(h + 1) * DH] for h in range(H))
        dn = dna * gfull
        do_ref[...] = rinv * (dn - n * per_head(dn * n))
        acc_ref[3:4, :] += _colsum(dz4)
        acc_ref[4:5, :] += _colsum(dz6)
        acc_ref[5:6, :] += _colsum(dz7)
        acc_ref[6:7, :] += _colsum(dz8)
        dz4_ref[...] = dz4.astype(BF)
        dz6_ref[...] = dz6.astype(BF)
        dz7_ref[...] = dz7.astype(BF)
        dz8_ref[...] = dz8.astype(BF)
        y_ref[...] = y.astype(BF)
        dout_ref[...] = dout.astype(BF)
        oa_ref[...] = oa.astype(BF)
        dpa_ref[...] = dpa.astype(BF)
        obv_ref[...] = obv.astype(BF)
        dpb_ref[...] = dpb.astype(BF)

        @pl.when(i == nt - 1)
        def _():
            acc_ref[9:10, :] = jnp.broadcast_to(jnp.sum(acc_ref[8:9, :], axis=-1, keepdims=True), (1, D))

    row = pl.BlockSpec((tm, D), lambda i: (i, 0))
    zs = lambda cb: pl.BlockSpec((tm, D), lambda i: (i, cb))
    full = lambda shp: pl.BlockSpec(shp, lambda i: (0,) * len(shp))
    f32o = jax.ShapeDtypeStruct((t_lat, D), F32)
    bfo = jax.ShapeDtypeStruct((t_lat, D), BF)
    return pl.pallas_call(
        body, grid=(nt,),
        in_specs=[row, row, row, row, zs(4), zs(6), zs(7), zs(8), row, row, full((16, 3 * D)), full((1, DH)),
                  full((1, D)), full((1, D)), full((D, D)), full((D, D)), full((D, D))],
        out_specs=(row,) * 13 + (full((16, D)),),
        out_shape=(f32o, f32o, bfo, bfo, bfo, bfo, f32o, bfo, bfo, bfo, bfo, bfo, bfo,
                   jax.ShapeDtypeStruct((16, D), F32)),
        name="merge", compiler_params=_cp(("arbitrary",)))(
            o_f, o_b, h_f, h_b, z, z, z, z, x_all, tgt, mod, norm_g, ln_g, ln_b, p_a, p_b, w_out)


def _wgrad(a, b, name):
    tm = 512

    def body(a_ref, b_ref, o_ref):
        @pl.when(pl.program_id(0) == 0)
        def _():
            o_ref[...] = jnp.zeros_like(o_ref)
        o_ref[...] += _dot_tn(a_ref[...], b_ref[...])

    row = pl.BlockSpec((tm, D), lambda i: (i, 0))
    return pl.pallas_call(body, grid=(a.shape[0] // tm,), in_specs=[row, row],
                          out_specs=pl.BlockSpec((D, D), lambda i: (0, 0)),
                          out_shape=jax.ShapeDtypeStruct((D, D), F32), name=name,
                          compiler_params=_cp(("arbitrary",)))(a, b)


def _wgrad_in(x_all, mod, dz, n_lat_tiles):
    m = x_all.shape[0]
    tm = 256

    def body(x_ref, mod_ref, dz_ref, o_ref):
        i = pl.program_id(1)

        @pl.when(i == 0)
        def _():
            o_ref[...] = jnp.zeros_like(o_ref)
        u, _ = _modulate(x_ref, mod_ref, i < n_lat_tiles)
        o_ref[0] += _dot_tn(u, dz_ref[...])

    return pl.pallas_call(
        body, grid=(NSH, m // tm),
        in_specs=[pl.BlockSpec((tm, D), lambda n, i: (i, 0)),
                  pl.BlockSpec((16, 3 * D), lambda n, i: (0, 0)),
                  pl.BlockSpec((tm, SHC), lambda n, i: (i, n))],
        out_specs=pl.BlockSpec((1, D, SHC), lambda n, i: (n, 0, 0)),
        out_shape=jax.ShapeDtypeStruct((NSH, D, SHC), F32), name="wgrad_in",
        compiler_params=_cp(("arbitrary", "arbitrary")))(x_all, mod, dz)


def _du(dz, w_in_g, x_all, mod, gxres, n_lat_tiles):
    m = x_all.shape[0]
    tm = 256
    nt = m // tm
    nct = nt - n_lat_tiles
    rblk = lambda i: jnp.where(i < nct, n_lat_tiles + i, i - nct)
    lblk = lambda i: jnp.maximum(i - nct, 0)

    def body(dz_ref, w_ref, x_ref, mod_ref, gr_ref, gx_ref, dm_ref, acc):
        i = pl.program_id(0)
        n = pl.program_id(1)
        is_lat = i >= nct

        @pl.when((i == 0) & (n == 0))
        def _():
            dm_ref[...] = jnp.zeros_like(dm_ref)

        @pl.when(n == 0)
        def _():
            acc[...] = jnp.zeros_like(acc)

        acc[...] += _dot_nt(dz_ref[...], w_ref[0])

        @pl.when(n == NSH - 1)
        def _():
            du = acc[...]
            sc = jnp.where(is_lat, mod_ref[0:1, D:2 * D], mod_ref[1:2, D:2 * D])
            dsh = _colsum(du)
            dsc = _colsum(du * x_ref[...])

            @pl.when(is_lat)
            def _():
                gx_ref[...] = du * (1.0 + sc) + gr_ref[...]
                dm_ref[0:1, 0:D] += dsh
                dm_ref[0:1, D:2 * D] += dsc

            @pl.when(jnp.logical_not(is_lat))
            def _():
                dm_ref[1:2, 0:D] += dsh
                dm_ref[1:2, D:2 * D] += dsc

    return pl.pallas_call(
        body, grid=(nt, NSH),
        in_specs=[pl.BlockSpec((tm, SHC), lambda i, n: (rblk(i), n)),
                  pl.BlockSpec((1, D, SHC), lambda i, n: (n, 0, 0)),
                  pl.BlockSpec((tm, D), lambda i, n: (rblk(i), 0)),
                  pl.BlockSpec((16, 3 * D), lambda i, n: (0, 0)),
                  pl.BlockSpec((tm, D), lambda i, n: (lblk(i), 0))],
        out_specs=(pl.BlockSpec((tm, D), lambda i, n: (lblk(i), 0)),
                   pl.BlockSpec((8, 2 * D), lambda i, n: (0, 0))),
        out_shape=(jax.ShapeDtypeStruct((n_lat_tiles * tm, D), F32), jax.ShapeDtypeStruct((8, 2 * D), F32)),
        scratch_shapes=[pltpu.VMEM((tm, D), F32)],
        name="du", compiler_params=_cp(("arbitrary", "arbitrary")))(dz, w_in_g, x_all, mod, gxres)


def _row_tile(rows, cols):
    t = 8
    while t * 2 * cols * 4 <= (1 << 20) and rows % (t * 2) == 0:
        t *= 2
    return t


def _addn(arrs, name):
    shape = arrs[0].shape
    cols = shape[-1]
    flat = [a.reshape(-1, cols) for a in arrs]
    rows = flat[0].shape[0]
    tr = _row_tile(rows, cols)

    def body(*refs):
        acc = refs[0][...]
        for r in refs[1:-1]:
            acc = acc + r[...]
        refs[-1][...] = acc

    spec = pl.BlockSpec((tr, cols), lambda i: (i, 0))
    out = pl.pallas_call(body, grid=(rows // tr,), in_specs=[spec] * len(flat), out_specs=spec,
                         out_shape=jax.ShapeDtypeStruct((rows, cols), F32), name=name,
                         compiler_params=_cp(("arbitrary",)))(*flat)
    return out.reshape(shape)


def _adamw(w, g, m, v, name):
    rows, cols = w.shape
    tr = _row_tile(rows, cols)

    def body(w_ref, g_ref, m_ref, v_ref, d_ref, nm_ref, nv_ref):
        gg = g_ref[...]
        m2 = ADAM_B1 * m_ref[...] + (1.0 - ADAM_B1) * gg
        v2 = ADAM_B2 * v_ref[...] + (1.0 - ADAM_B2) * (gg * gg)
        m_hat = m2 / (1.0 - ADAM_B1 ** ADAM_STEP)
        v_hat = v2 / (1.0 - ADAM_B2 ** ADAM_STEP)
        d_ref[...] = -ADAM_LR * (m_hat / (jnp.sqrt(v_hat) + ADAM_EPS) + ADAM_WD * w_ref[...])
        nm_ref[...] = m2
        nv_ref[...] = v2

    spec = pl.BlockSpec((tr, cols), lambda i: (i, 0))
    o = jax.ShapeDtypeStruct((rows, cols), F32)
    return pl.pallas_call(body, grid=(rows // tr,), in_specs=[spec] * 4, out_specs=(spec,) * 3,
                          out_shape=(o, o, o), name=name, compiler_params=_cp(("arbitrary",)))(w, g, m, v)


_ANY = pl.BlockSpec(memory_space=pl.ANY)


def _place():
    return lax.axis_index("x"), lax.axis_index("y"), lax.axis_index("c")


def _allgather_chips(shards):
    n = len(shards)

    def body(*refs):
        ins, outs = refs[:n], refs[n:2 * n]
        send, recv, lsem = refs[2 * n:]
        x, y, c = _place()
        me = 2 * x + y
        peers = ((1 - x, y, c), (x, 1 - y, c), (1 - x, 1 - y, c))
        copies = []
        for j in range(n):
            cp = pltpu.make_async_copy(ins[j], outs[j].at[me], lsem.at[j])
            cp.start()
            copies.append(cp)
            for k, p in enumerate(peers):
                cp = pltpu.make_async_remote_copy(src_ref=ins[j], dst_ref=outs[j].at[me], send_sem=send.at[3 * j + k],
                                                  recv_sem=recv.at[3 * j + k], device_id=p, device_id_type=MESH)
                cp.start()
                copies.append(cp)
        for cp in copies:
            cp.wait()

    return pl.pallas_call(
        body, in_specs=[_ANY] * n, out_specs=[_ANY] * n,
        out_shape=[jax.ShapeDtypeStruct((NSH,) + s.shape, s.dtype) for s in shards],
        scratch_shapes=[pltpu.SemaphoreType.DMA((3 * n,)), pltpu.SemaphoreType.DMA((3 * n,)),
                        pltpu.SemaphoreType.DMA((n,))],
        name="allgather_weights")(*shards)


def _rs_sibling(grads):
    n = len(grads)

    def body(*refs):
        ins, mine, got = refs[:n], refs[n:2 * n], refs[2 * n:3 * n]
        send, recv, lsem = refs[3 * n:]
        x, y, c = _place()
        copies = []
        for j in range(n):
            hr = ins[j].shape[1] // 2
            keep = ins[j].at[:, pl.ds(pl.multiple_of(c * hr, 8), hr), :]
            give = ins[j].at[:, pl.ds(pl.multiple_of((1 - c) * hr, 8), hr), :]
            cp = pltpu.make_async_copy(keep, mine[j], lsem.at[j])
            cp.start()
            copies.append(cp)
            cp = pltpu.make_async_remote_copy(src_ref=give, dst_ref=got[j], send_sem=send.at[j], recv_sem=recv.at[j],
                                              device_id=(x, y, 1 - c), device_id_type=MESH)
            cp.start()
            copies.append(cp)
        for cp in copies:
            cp.wait()

    half = [jax.ShapeDtypeStruct((NSH, g.shape[1] // 2, g.shape[2]), F32) for g in grads]
    outs = pl.pallas_call(
        body, in_specs=[_ANY] * n, out_specs=[_ANY] * (2 * n), out_shape=half + half,
        scratch_shapes=[pltpu.SemaphoreType.DMA((n,)), pltpu.SemaphoreType.DMA((n,)), pltpu.SemaphoreType.DMA((n,))],
        name="rs_sibling")(*grads)
    return outs[:n], outs[n:]


def _rs_chips(sums):
    n = len(sums)

    def body(*refs):
        ins, mine, got = refs[:n], refs[n:2 * n], refs[2 * n:3 * n]
        send, recv, lsem = refs[3 * n:]
        x, y, c = _place()
        me = 2 * x + y
        peers = ((1 - x, y), (x, 1 - y), (1 - x, 1 - y))
        copies = []
        for j in range(n):
            cp = pltpu.make_async_copy(ins[j].at[me], mine[j], lsem.at[j])
            cp.start()
            copies.append(cp)
            for k, (px, py) in enumerate(peers):
                cp = pltpu.make_async_remote_copy(src_ref=ins[j].at[2 * px + py], dst_ref=got[j].at[k],
                                                  send_sem=send.at[3 * j + k], recv_sem=recv.at[3 * j + k],
                                                  device_id=(px, py, c), device_id_type=MESH)
                cp.start()
                copies.append(cp)
        for cp in copies:
            cp.wait()

    one = [jax.ShapeDtypeStruct(g.shape[1:], F32) for g in sums]
    three = [jax.ShapeDtypeStruct((3,) + g.shape[1:], F32) for g in sums]
    outs = pl.pallas_call(
        body, in_specs=[_ANY] * n, out_specs=[_ANY] * (2 * n), out_shape=one + three,
        scratch_shapes=[pltpu.SemaphoreType.DMA((3 * n,)), pltpu.SemaphoreType.DMA((3 * n,)),
                        pltpu.SemaphoreType.DMA((n,))],
        name="rs_chips")(*sums)
    return outs[:n], outs[n:]


def _ag_sibling(halves):
    n = len(halves)

    def body(*refs):
        ins, outs = refs[:n], refs[n:2 * n]
        send, recv, lsem = refs[2 * n:]
        x, y, c = _place()
        copies = []
        for j in range(n):
            hr = ins[j].shape[0]
            rows = outs[j].at[pl.ds(pl.multiple_of(c * hr, 8), hr), :]
            cp = pltpu.make_async_copy(ins[j], rows, lsem.at[j])
            cp.start()
            copies.append(cp)
            cp = pltpu.make_async_remote_copy(src_ref=ins[j], dst_ref=rows, send_sem=send.at[j], recv_sem=recv.at[j],
                                              device_id=(x, y, 1 - c), device_id_type=MESH)
            cp.start()
            copies.append(cp)
        for cp in copies:
            cp.wait()

    return pl.pallas_call(
        body, in_specs=[_ANY] * n, out_specs=[_ANY] * n,
        out_shape=[jax.ShapeDtypeStruct((2 * h.shape[0], h.shape[1]), F32) for h in halves],
        scratch_shapes=[pltpu.SemaphoreType.DMA((n,)), pltpu.SemaphoreType.DMA((n,)), pltpu.SemaphoreType.DMA((n,))],
        name="ag_sibling")(*halves)


def _allreduce_small(buf):
    rows = buf.shape[0]
    pr = rows // 8

    def body(in_ref, out_ref, stage, send1, recv1, send2, recv2):
        x, y, c = _place()
        me = 4 * x + 2 * y + c

        def peer(k):
            kx, ky, kc = (k >> 2) & 1, (k >> 1) & 1, k & 1
            return (x ^ kx, y ^ ky, c ^ kc)

        def piece(ref, idx):
            return ref.at[pl.ds(pl.multiple_of(idx * pr, 8), pr), :]

        copies = []
        for k in range(1, 8):
            px, py, pc = peer(k)
            cp = pltpu.make_async_remote_copy(src_ref=piece(in_ref, 4 * px + 2 * py + pc), dst_ref=stage.at[k],
                                              send_sem=send1.at[k], recv_sem=recv1.at[k],
                                              device_id=(px, py, pc), device_id_type=MESH)
            cp.start()
            copies.append(cp)
        for cp in copies:
            cp.wait()
        acc = piece(in_ref, me)[...]
        for k in range(1, 8):
            acc = acc + stage[k]
        piece(out_ref, me)[...] = acc
        copies = []
        for k in range(1, 8):
            cp = pltpu.make_async_remote_copy(src_ref=piece(out_ref, me), dst_ref=piece(out_ref, me),
                                              send_sem=send2.at[k], recv_sem=recv2.at[k],
                                              device_id=peer(k), device_id_type=MESH)
            cp.start()
            copies.append(cp)
        for cp in copies:
            cp.wait()

    vm = pl.BlockSpec(memory_space=pltpu.VMEM)
    return pl.pallas_call(
        body, in_specs=[vm], out_specs=vm, out_shape=jax.ShapeDtypeStruct((rows, D), F32),
        scratch_shapes=[pltpu.VMEM((8, pr, D), F32)] + [pltpu.SemaphoreType.DMA((8,))] * 4,
        name="allreduce_small", compiler_params=_cp())(buf)


def _reduce_scatter(grads):
    mine, got = _rs_sibling(grads)
    sums = [_addn([a, b], f"rs_add1_{j}") for j, (a, b) in enumerate(zip(mine, got))]
    mine, got = _rs_chips(sums)
    halves = [_addn([a, b[0], b[1], b[2]], f"rs_add2_{j}") for j, (a, b) in enumerate(zip(mine, got))]
    return _ag_sibling(halves)


def _rows(a):
    flat = a.reshape(-1)
    pad = (-flat.shape[0]) % D
    if pad:
        flat = jnp.concatenate([flat, jnp.zeros((pad,), flat.dtype)])
    return flat.reshape(-1, D)


def _pad_rows(a, mult):
    pad = (-a.shape[0]) % mult
    return jnp.concatenate([a, jnp.zeros((pad, a.shape[1]), a.dtype)]) if pad else a


def _local_step(x, c, ctx, c_ctx, tgt, w_mod_g, b_mod, w_in_g, b_in, lbl, norm_g, cw, cb, wr, br, wi, bi, lam,
                p_a, p_b, w_out, ln_g, ln_b):
    t_lat, t_ctx = x.shape[0], ctx.shape[0]
    nlt = t_lat // 256
    nlb, ncb = t_lat // RB, t_ctx // RB
    cc = jnp.zeros((16, D), F32).at[0].set(c).at[1].set(c_ctx)
    mod = _mod_fwd(cc, w_mod_g, b_mod)
    x_all = jnp.concatenate([x, ctx], axis=0)
    z = _inproj(x_all, mod, w_in_g, b_in, nlt)
    o0, st0 = _gla_fwd(z, lbl, 0, nlb, ncb)
    o1, st1 = _gla_fwd(z, lbl, 1, nlb, ncb)
    h0, hp0, hpc0 = _rglru_fwd(z, cw, cb, wr, br, wi, bi, lam, 0, t_lat, t_ctx)
    h1, hp1, hpc1 = _rglru_fwd(z, cw, cb, wr, br, wi, bi, lam, 1, t_lat, t_ctx)
    (do, dh, dz4, dz6, dz7, dz8, gxres, y, dout, oa, dpa, obv, dpb, acc) = _merge(
        o0, o1, h0, h1, z, x_all, tgt, mod, norm_g, ln_g, ln_b, p_a, p_b, w_out, t_lat)
    gp_a = _wgrad(oa, dpa, "wgrad_pa")
    gp_b = _wgrad(obv, dpb, "wgrad_pb")
    gw_out = _wgrad(y, dout, "wgrad_wout")
    dxc_lat, dxc_ctx, dwr0, dwi0, sb0 = _rglru_bwd(z, cw, cb, wr, br, wi, bi, lam, dh, hp0, hpc0, 0, t_lat, t_ctx)
    dz5_lat, dz5_ctx, dwr1, dwi1, sb1 = _rglru_bwd(z, cw, cb, wr, br, wi, bi, lam, dh, hp1, hpc1, 1, t_lat, t_ctx,
                                                   prev=(dxc_lat, dxc_ctx))
    dq0, dv0, dz1, sa0 = _gla_bwd(z, lbl, do, st0, 0, nlb, ncb)
    dz0, dz3, dz2, sa1 = _gla_bwd(z, lbl, do, st1, 1, nlb, ncb, prev=(dq0, dv0))
    zc = jnp.zeros((t_ctx, D), BF)
    pad = lambda a: jnp.concatenate([a, zc], axis=0)
    dz = jnp.concatenate([dz0, dz1, dz2, dz3, pad(dz4), jnp.concatenate([dz5_lat, dz5_ctx], axis=0),
                          pad(dz6), pad(dz7), pad(dz8)], axis=1)
    gw_in = _wgrad_in(x_all, mod, dz, nlt)
    gx, dm = _du(dz, w_in_g, x_all, mod, gxres, nlt)
    dmod = jnp.zeros((16, 3 * D), F32).at[0:2, 0:2 * D].set(dm[0:2]).at[0, 2 * D:].set(acc[0])
    gw_mod, dcc = _mod_bwd(cc, dmod, w_mod_g)
    small = dict(
        c_ctx=dcc[1:2], b_mod=(dmod[0] + dmod[1]).reshape(3, D),
        b_in=jnp.stack([sa1[2], sa0[0], sa1[0], sa1[3], acc[3], sb1[4], acc[4], acc[5], acc[6]]),
        lb_logits=jnp.stack([sa0[1], sa1[1], -sa0[1], -sa1[1]]),
        norm_a_g=acc[7:8], conv_w=sb1[8:12], conv_b=sb1[3:4],
        w_r=jnp.stack([dwr0, dwr1]).reshape(-1, D), w_i=jnp.stack([dwi0, dwi1]).reshape(-1, D),
        b_r=jnp.stack([sb0[0], sb1[0]]), b_i=jnp.stack([sb0[1], sb1[1]]), lam=jnp.stack([sb0[2], sb1[2]]),
        ln_g=acc[1:2], ln_b=acc[2:3])
    big = dict(w_mod=gw_mod, w_in=gw_in, p_a=gp_a.reshape(NSH, D // NSH, D), p_b=gp_b.reshape(NSH, D // NSH, D),
               w_out=gw_out.reshape(NSH, D // NSH, D))
    return acc[9, 0], gx, big, small


_SMALL = ("c_ctx", "b_mod", "b_in", "lb_logits", "norm_a_g", "conv_w", "conv_b", "w_r", "w_i", "b_r", "b_i", "lam",
          "ln_g", "ln_b")
_BIG = ("w_mod", "w_in", "p_a", "p_b", "w_out")
_COL_SHARDED = ("lb_logits", "conv_w", "b_r", "b_i", "lam")
_WEIGHTS = ("c_ctx", "w_mod", "b_mod", "w_in", "b_in", "lb_logits", "norm_a_g", "conv_w", "conv_b", "w_r", "b_r", "w_i",
            "b_i", "lam", "p_a", "p_b", "w_out", "ln_g", "ln_b")


def kernel(x, c, ctx, c_ctx, w_mod, b_mod, w_in, b_in, lb_logits, norm_a_g, conv_w, conv_b, w_r, b_r, w_i, b_i, lam, p_a, p_b, w_out, ln_g, ln_b, loss_target, m_c_ctx, m_w_mod, m_b_mod, m_w_in, m_b_in, m_lb_logits, m_norm_a_g, m_conv_w, m_conv_b, m_w_r, m_b_r, m_w_i, m_b_i, m_lam, m_p_a, m_p_b, m_w_out, m_ln_g, m_ln_b, v_c_ctx, v_w_mod, v_b_mod, v_w_in, v_b_in, v_lb_logits, v_norm_a_g, v_conv_w, v_conv_b, v_w_r, v_b_r, v_w_i, v_b_i, v_lam, v_p_a, v_p_b, v_w_out, v_ln_g, v_ln_b):
    w = dict(c_ctx=c_ctx, w_mod=w_mod, b_mod=b_mod, w_in=w_in, b_in=b_in, lb_logits=lb_logits, norm_a_g=norm_a_g,
             conv_w=conv_w, conv_b=conv_b, w_r=w_r, b_r=b_r, w_i=w_i, b_i=b_i, lam=lam, p_a=p_a, p_b=p_b, w_out=w_out,
             ln_g=ln_g, ln_b=ln_b)
    m = dict(c_ctx=m_c_ctx, w_mod=m_w_mod, b_mod=m_b_mod, w_in=m_w_in, b_in=m_b_in, lb_logits=m_lb_logits,
             norm_a_g=m_norm_a_g, conv_w=m_conv_w, conv_b=m_conv_b, w_r=m_w_r, b_r=m_b_r, w_i=m_w_i, b_i=m_b_i,
             lam=m_lam, p_a=m_p_a, p_b=m_p_b, w_out=m_w_out, ln_g=m_ln_g, ln_b=m_ln_b)
    v = dict(c_ctx=v_c_ctx, w_mod=v_w_mod, b_mod=v_b_mod, w_in=v_w_in, b_in=v_b_in, lb_logits=v_lb_logits,
             norm_a_g=v_norm_a_g, conv_w=v_conv_w, conv_b=v_conv_b, w_r=v_w_r, b_r=v_b_r, w_i=v_w_i, b_i=v_b_i,
             lam=v_lam, p_a=v_p_a, p_b=v_p_b, w_out=v_w_out, ln_g=v_ln_g, ln_b=v_ln_b)
    shard = 2 * lax.axis_index("x") + lax.axis_index("y")
    cs = D // NSH

    narrow = _pad_rows(jnp.concatenate([lb_logits.reshape(4, cs), conv_w[0], b_r[0], b_i[0], lam[0]], axis=0), 8)
    big_shards = [w[k][0].astype(BF) for k in _BIG]
    gathered = _allgather_chips(big_shards + [narrow])
    w_mod_g, w_in_g = gathered[0], gathered[1]
    p_a_g, p_b_g, w_out_g = (g.reshape(D, D) for g in gathered[2:5])
    nar = jnp.transpose(gathered[5], (1, 0, 2)).reshape(-1, D)
    lbl, cw, br, bi, lam_f = nar[0:4].reshape(2, 2, D), nar[4:8], nar[8:10], nar[10:12], nar[12:14]

    loss, gx, big, small = _local_step(
        x[0], c[0], ctx[0], c_ctx, loss_target[0], w_mod_g, b_mod, w_in_g, b_in, lbl, norm_a_g, cw, conv_b,
        w_r[0], br, w_i[0], bi, lam_f, p_a_g, p_b_g, w_out_g, ln_g, ln_b)
    loss = lax.psum(loss, ("x", "y", "c"))

    g_big = dict(zip(_BIG, _reduce_scatter([big[k] for k in _BIG])))
    sizes = [small[k].shape[0] for k in _SMALL]
    red = _allreduce_small(_pad_rows(jnp.concatenate([small[k] for k in _SMALL], axis=0), 64))
    grads = {}
    off = 0
    for k, n in zip(_SMALL, sizes):
        g = red[off:off + n]
        off += n
        if k == "norm_a_g":
            g = g[:, :DH]
        if k in _COL_SHARDED:
            g = lax.dynamic_slice_in_dim(g, shard * cs, cs, axis=1)
        grads[k] = g.reshape(w[k].shape)
    for k in _BIG:
        grads[k] = g_big[k].reshape(w[k].shape)

    delta, new_m, new_v = {}, {}, {}
    for k in _BIG:
        shp = w[k].shape
        two = lambda a: a.reshape(shp[-2], shp[-1])
        d_, m_, v_ = _adamw(two(w[k]), two(grads[k]), two(m[k]), two(v[k]), f"adamw_{k}")
        delta[k], new_m[k], new_v[k] = d_.reshape(shp), m_.reshape(shp), v_.reshape(shp)
    flat = lambda t: _pad_rows(jnp.concatenate([_rows(t[k]) for k in _SMALL], axis=0), 8)
    d_, m_, v_ = _adamw(flat(w), flat(grads), flat(m), flat(v), "adamw_small")
    off = 0
    for k in _SMALL:
        n = w[k].size
        nr = -(-n // D)
        for src, dst in ((d_, delta), (m_, new_m), (v_, new_v)):
            dst[k] = src[off:off + nr].reshape(-1)[:n].reshape(w[k].shape)
        off += nr

    return (loss, gx[None], *[grads[k] for k in _WEIGHTS], *[delta[k] for k in _WEIGHTS],
            *[new_m[k] for k in _WEIGHTS], *[new_v[k] for k in _WEIGHTS])
```

```python
import functools

import jax
import jax.numpy as jnp
from jax import lax
from jax.experimental import pallas as pl
from jax.experimental.pallas import tpu as pltpu

F32 = jnp.float32
BF = jnp.bfloat16
MESH = pl.DeviceIdType.MESH

D = 1024
H = 8
DH = 128
CH = 64
RB = 256
NCK = RB // CH
GRID_W = 64
IN_COLS = 9 * D
NSH = 4
SHC = IN_COLS // NSH
RG_C = 8.0
ALPHA = 2.0 ** 0.25
LN_EPS = 1e-5
RMS_EPS = 1e-6
Q_SCALE = DH ** -0.5
ADAM_LR, ADAM_B1, ADAM_B2, ADAM_EPS, ADAM_WD, ADAM_STEP = 0.001, 0.9, 0.999, 1e-08, 0.01, 10
VMEM_LIMIT = 56 * 1024 * 1024


def _cp(sem=None):
    return pltpu.CompilerParams(dimension_semantics=sem, vmem_limit_bytes=VMEM_LIMIT)


def _sig(x):
    return 1.0 / (1.0 + jnp.exp(-x))


def _dot(a, b):
    return jnp.dot(a.astype(BF), b.astype(BF), preferred_element_type=F32)


def _dot_nt(a, b):
    return lax.dot_general(a.astype(BF), b.astype(BF), (((1,), (1,)), ((), ())), preferred_element_type=F32)


def _dot_tn(a, b):
    return lax.dot_general(a.astype(BF), b.astype(BF), (((0,), (0,)), ((), ())), preferred_element_type=F32)


def _dot_hi(a, b):
    return jnp.dot(a, b, preferred_element_type=F32, precision=lax.Precision.HIGHEST)


def _colsum(v):
    return jnp.sum(v, axis=0, keepdims=True)


def _mod_fwd(cc, w_mod_g, b_mod):
    def body(cc_ref, w_ref, b_ref, o_ref):
        v = cc_ref[...]
        s = v * _sig(v)
        for k in range(NSH):
            cs = slice(k * 768, (k + 1) * 768)
            o_ref[:, cs] = _dot(s, w_ref[k]) + b_ref[:, cs]
    return pl.pallas_call(body, out_shape=jax.ShapeDtypeStruct((16, 3 * D), F32), name="mod_fwd",
                          compiler_params=_cp())(cc, w_mod_g, b_mod)


def _mod_bwd(cc, dmod, w_mod_g):
    def body(cc_ref, dm_ref, w_ref, dw_ref, dcc_ref):
        v = cc_ref[...]
        sg = _sig(v)
        s = v * sg
        ds = jnp.zeros((16, D), F32)
        for k in range(NSH):
            dm = dm_ref[:, k * 768:(k + 1) * 768]
            dw_ref[k] = _dot_tn(s, dm)
            ds = ds + _dot_nt(dm, w_ref[k])
        dcc_ref[...] = ds * (sg * (1.0 + v * (1.0 - sg)))
    return pl.pallas_call(body, out_shape=(jax.ShapeDtypeStruct((NSH, D, 768), F32),
                                           jax.ShapeDtypeStruct((16, D), F32)),
                          name="mod_bwd", compiler_params=_cp())(cc, dmod, w_mod_g)


def _modulate(x_ref, mod_ref, is_lat):
    sh = jnp.where(is_lat, mod_ref[0:1, 0:D], mod_ref[1:2, 0:D])
    sc = jnp.where(is_lat, mod_ref[0:1, D:2 * D], mod_ref[1:2, D:2 * D])
    return x_ref[...] * (1.0 + sc) + sh, sc


def _inproj(x_all, mod, w_in_g, b_in, n_lat_tiles):
    m = x_all.shape[0]
    tm = 256
    nt = m // tm

    def body(x_ref, mod_ref, w_ref, b_ref, z_ref):
        i = pl.program_id(1)
        u, _ = _modulate(x_ref, mod_ref, i < n_lat_tiles)
        z_ref[...] = _dot(u, w_ref[0]) + b_ref[...]

    return pl.pallas_call(
        body, grid=(NSH, nt),
        in_specs=[pl.BlockSpec((tm, D), lambda n, i: (i, 0)),
                  pl.BlockSpec((16, 3 * D), lambda n, i: (0, 0)),
                  pl.BlockSpec((1, D, SHC), lambda n, i: (n, 0, 0)),
                  pl.BlockSpec((1, SHC), lambda n, i: (0, n))],
        out_specs=pl.BlockSpec((tm, SHC), lambda n, i: (i, n)),
        out_shape=jax.ShapeDtypeStruct((m, IN_COLS), F32), name="inproj",
        compiler_params=_cp(("arbitrary", "arbitrary")))(x_all, mod, w_in_g, b_in)


def _gla_masks(d):
    row = lax.broadcasted_iota(jnp.int32, (CH, CH), 0)
    col = lax.broadcasted_iota(jnp.int32, (CH, CH), 1)
    lower, upper = row >= col, row <= col
    mb, mt = (upper, lower) if d else (lower, upper)
    return mb, mb.astype(F32), mt.astype(F32)


def _lower_bound(lbl_ref, d):
    return _sig(lbl_ref[0, d:d + 1, :] - lbl_ref[1, d:d + 1, :])


def _gla_rb(d, nlb, ncb):
    nrb = nlb + ncb
    if d:
        return lambda s: nrb - 1 - s
    return lambda s: jnp.where(s < ncb, nlb + s, s - ncb)


def _gla_fwd(z, lbl, d, nlb, ncb):
    m = z.shape[0]
    nrb = nlb + ncb
    rb = _gla_rb(d, nlb, ncb)

    def body(q_ref, f_ref, v_ref, lbl_ref, o_ref, st_ref, S):
        s = pl.program_id(0)

        @pl.when(s == 0)
        def _():
            S[...] = jnp.zeros_like(S)

        lb = _lower_bound(lbl_ref, d)
        mb, mf, _ = _gla_masks(d)

        def chunk(j, carry):
            jj = (NCK - 1 - j) if d else j
            r0 = pl.multiple_of(jj * CH, CH)
            zq = q_ref[pl.ds(r0, CH), :]
            zf = f_ref[pl.ds(r0, CH), :]
            zv = v_ref[pl.ds(r0, CH), :]
            q = zq * _sig(zq) * Q_SCALE
            f = lb + (1.0 - lb) * _sig(zf)
            k = 1.0 - f
            logf = jnp.log(f)
            for h in range(H):
                hs = slice(h * DH, (h + 1) * DH)
                lf = logf[:, hs]
                g = _dot_hi(mf, lf)
                gl = _colsum(lf)
                qd = q[:, hs] * jnp.exp(g)
                ki = k[:, hs] * jnp.exp(-g)
                ke = k[:, hs] * jnp.exp(gl - g)
                a = jnp.where(mb, _dot_nt(qd, ki), 0.0)
                st = S[h]
                st_ref[jj, h] = st
                o_ref[pl.ds(r0, CH), hs] = _dot(a, zv[:, hs]) + _dot_nt(qd, st)
                S[h] = st * jnp.exp(gl) + _dot_tn(zv[:, hs], ke)
            return carry

        lax.fori_loop(0, NCK, chunk, 0)

    def zspec(cb):
        return pl.BlockSpec((RB, D), lambda s: (rb(s), cb))

    return pl.pallas_call(
        body, grid=(nrb,),
        in_specs=[zspec(0), zspec(1 + d), zspec(3), pl.BlockSpec((2, 2, D), lambda s: (0, 0, 0))],
        out_specs=(pl.BlockSpec((RB, D), lambda s: (rb(s), 0)),
                   pl.BlockSpec((NCK, H, DH, DH), lambda s: (rb(s), 0, 0, 0))),
        out_shape=(jax.ShapeDtypeStruct((m, D), F32),
                   jax.ShapeDtypeStruct((m // CH, H, DH, DH), F32)),
        scratch_shapes=[pltpu.VMEM((H, DH, DH), F32)],
        name=f"gla_fwd{d}", compiler_params=_cp(("arbitrary",)))(z, z, z, lbl)


def _gla_bwd(z, lbl, do_lat, states, d, nlb, ncb, prev=None):
    m = z.shape[0]
    nrb = nlb + ncb
    fwd_rb = _gla_rb(d, nlb, ncb)
    rb = lambda s: fwd_rb(nrb - 1 - s)
    last = prev is not None

    def body(*refs):
        if last:
            q_ref, f_ref, v_ref, lbl_ref, do_ref, st_ref, pq_ref, pv_ref, o0_ref, o1_ref, o2_ref, sum_ref, dS = refs
        else:
            q_ref, f_ref, v_ref, lbl_ref, do_ref, st_ref, o0_ref, o1_ref, o2_ref, sum_ref, dS = refs
        s = pl.program_id(0)
        is_lat = rb(s) < nlb

        @pl.when(s == 0)
        def _():
            dS[...] = jnp.zeros_like(dS)
            sum_ref[...] = jnp.zeros_like(sum_ref)

        lb = _lower_bound(lbl_ref, d)
        mb, mf, mtf = _gla_masks(d)

        def chunk(j, carry):
            jj = j if d else (NCK - 1 - j)
            r0 = pl.multiple_of(jj * CH, CH)
            zq = q_ref[pl.ds(r0, CH), :]
            zf = f_ref[pl.ds(r0, CH), :]
            zv = v_ref[pl.ds(r0, CH), :]
            do = jnp.where(is_lat, do_ref[pl.ds(r0, CH), :], 0.0)
            sq = _sig(zq)
            q = zq * sq * Q_SCALE
            sf = _sig(zf)
            f = lb + (1.0 - lb) * sf
            k = 1.0 - f
            logf = jnp.log(f)
            dqs, dvs, dfs = [], [], []
            for h in range(H):
                hs = slice(h * DH, (h + 1) * DH)
                lf = logf[:, hs]
                g = _dot_hi(mf, lf)
                gl = _colsum(lf)
                eg = jnp.exp(g)
                eig = jnp.exp(-g)
                eeg = jnp.exp(gl - g)
                dec = jnp.exp(gl)
                qd = q[:, hs] * eg
                ki = k[:, hs] * eig
                ke = k[:, hs] * eeg
                v = zv[:, hs]
                doh = do[:, hs]
                a = jnp.where(mb, _dot_nt(qd, ki), 0.0)
                st = st_ref[jj, h]
                dst = dS[h]
                da = jnp.where(mb, _dot_nt(doh, v), 0.0)
                dv = _dot_tn(a, doh) + _dot_nt(ke, dst)
                dqd = _dot(da, ki) + _dot(doh, st)
                dki = _dot_tn(da, qd)
                dke = _dot(v, dst)
                dS[h] = _dot_tn(doh, qd) + dst * dec
                ddec = _colsum(st * dst)
                dq = dqd * eg
                dk = dki * eig + dke * eeg
                dke_ke = dke * ke
                dg = dqd * qd - dki * ki - dke_ke
                dgl = _colsum(dke_ke) + ddec * dec
                dlogf = _dot_hi(mtf, dg) + dgl
                dqs.append(dq)
                dvs.append(dv)
                dfs.append(dlogf / f[:, hs] - dk)
            dq = jnp.concatenate(dqs, axis=1)
            dv = jnp.concatenate(dvs, axis=1)
            df = jnp.concatenate(dfs, axis=1)
            dzf = df * (1.0 - lb) * (sf * (1.0 - sf))
            sum_ref[0:1, :] += _colsum(dzf)
            sum_ref[1:2, :] += _colsum(df * (1.0 - sf))
            rows = pl.ds(r0, CH)
            if last:
                dz0 = (dq + pq_ref[rows, :]) * (Q_SCALE * (sq * (1.0 + zq * (1.0 - sq))))
                dz3 = dv + pv_ref[rows, :]
                sum_ref[2:3, :] += _colsum(dz0)
                sum_ref[3:4, :] += _colsum(dz3)
                o0_ref[rows, :] = dz0.astype(BF)
                o1_ref[rows, :] = dz3.astype(BF)
            else:
                o0_ref[rows, :] = dq
                o1_ref[rows, :] = dv
            o2_ref[rows, :] = dzf.astype(BF)
            return carry

        lax.fori_loop(0, NCK, chunk, 0)

        @pl.when(s == nrb - 1)
        def _():
            sum_ref[1:2, :] = sum_ref[1:2, :] * (lb * (1.0 - lb))

    def zspec(cb):
        return pl.BlockSpec((RB, D), lambda s: (rb(s), cb))

    rowspec = pl.BlockSpec((RB, D), lambda s: (rb(s), 0))
    in_specs = [zspec(0), zspec(1 + d), zspec(3), pl.BlockSpec((2, 2, D), lambda s: (0, 0, 0)),
                pl.BlockSpec((RB, D), lambda s: (jnp.minimum(rb(s), nlb - 1), 0)),
                pl.BlockSpec((NCK, H, DH, DH), lambda s: (rb(s), 0, 0, 0))]
    args = [z, z, z, lbl, do_lat, states]
    if last:
        in_specs += [rowspec, rowspec]
        args += list(prev)
    dt01 = BF if last else F32
    return pl.pallas_call(
        body, grid=(nrb,), in_specs=in_specs,
        out_specs=(rowspec, rowspec, rowspec, pl.BlockSpec((8, D), lambda s: (0, 0))),
        out_shape=(jax.ShapeDtypeStruct((m, D), dt01), jax.ShapeDtypeStruct((m, D), dt01),
                   jax.ShapeDtypeStruct((m, D), BF), jax.ShapeDtypeStruct((8, D), F32)),
        scratch_shapes=[pltpu.VMEM((H, DH, DH), F32)],
        name=f"gla_bwd{d}", compiler_params=_cp(("arbitrary",)))(*args)


def _shift_rows(v, k, r):
    row = lax.broadcasted_iota(jnp.int32, v.shape, 0)
    rolled = pltpu.roll(v, k % r, 0)
    return jnp.where((row >= k) & (row < r + k), rolled, 0.0)


def _conv_fwd(xb, cw_ref, cb_ref, r):
    return (cb_ref[...] + _shift_rows(xb, 1, r) * cw_ref[0:1, :] + xb * cw_ref[1:2, :]
            + _shift_rows(xb, -1, r) * cw_ref[2:3, :] + _shift_rows(xb, -2, r) * cw_ref[3:4, :])


def _softplus_neg(lam):
    y = jnp.exp(-jnp.abs(lam))
    u = 1.0 + y
    tiny = u == 1.0
    l1p = jnp.where(tiny, y, jnp.log(u) * (y / jnp.where(tiny, 1.0, u - 1.0)))
    return jnp.maximum(-lam, 0.0) + l1p


def _gates(xc, wr_ref, br, wi_ref, bi, sp):
    xcb = xc.astype(BF)
    rs, is_ = [], []
    for g in range(H):
        gs = slice(g * DH, (g + 1) * DH)
        rs.append(jnp.dot(xcb[:, gs], wr_ref[g].astype(BF), preferred_element_type=F32))
        is_.append(jnp.dot(xcb[:, gs], wi_ref[g].astype(BF), preferred_element_type=F32))
    r = _sig(jnp.concatenate(rs, axis=1) + br)
    i = _sig(jnp.concatenate(is_, axis=1) + bi)
    log_a = (-RG_C * r) * sp
    a = jnp.exp(log_a)
    t = jnp.tanh(log_a)
    om = (-2.0 * t) / (1.0 - t)
    return r, i, a, om


def _scan_rows(d, nrows, a_s, b_s, h_s, h0):
    nsl = nrows // 8

    def slab(j, h):
        jj = (nsl - 1 - j) if d else j
        r0 = pl.multiple_of(jj * 8, 8)
        for t in (range(7, -1, -1) if d else range(8)):
            h = a_s[pl.ds(r0 + t, 1), :] * h + b_s[pl.ds(r0 + t, 1), :]
            h_s[pl.ds(r0 + t, 1), :] = h
        return h

    return lax.fori_loop(0, nsl, slab, h0)


def _col_of(d, ncols):
    if d:
        return lambda s: ncols - jnp.maximum(s, 1)
    return lambda s: jnp.maximum(s, 1) - 1


def _rglru_fwd(z, cw, cb, wr, br, wi, bi, lam, d, t_lat, t_ctx):
    m = z.shape[0]
    rows = t_lat // GRID_W
    zcol = z.reshape(m // GRID_W, GRID_W * IN_COLS)
    col = _col_of(d, GRID_W)
    cblk = t_lat // t_ctx
    rmax = max(rows, t_ctx)

    def body(zc_ref, zx_ref, cw_ref, cb_ref, wr_ref, br_ref, wi_ref, bi_ref, lam_ref,
             hx_ref, hpx_ref, hpc_ref, a_s, b_s, h_s, hcar):
        s = pl.program_id(0)
        sp = _softplus_neg(lam_ref[d:d + 1, :])
        br_ = br_ref[d:d + 1, :]
        bi_ = bi_ref[d:d + 1, :]

        def run(x_ref, r, h0, h_out, hp_out):
            xb = x_ref[...]
            xc = _conv_fwd(xb, cw_ref, cb_ref, r)
            _, i, a, om = _gates(xc, wr_ref, br_, wi_ref, bi_, sp)
            a_s[0:r, :] = a
            b_s[0:r, :] = jnp.sqrt(om) * (i * xc)
            hl = _scan_rows(d, r, a_s, b_s, h_s, h0)
            hs = h_s[0:r, :]
            row = lax.broadcasted_iota(jnp.int32, (r, D), 0)
            if d:
                hp = jnp.where(row == r - 1, h0, pltpu.roll(hs, r - 1, 0))
            else:
                hp = jnp.where(row == 0, h0, pltpu.roll(hs, 1, 0))
            if h_out is not None:
                h_out[...] = hs
            hp_out[...] = hp
            hcar[0:1, :] = hl

        @pl.when(s == 0)
        def _():
            run(zc_ref, t_ctx, jnp.zeros((1, D), F32), None, hpc_ref)

        @pl.when(s > 0)
        def _():
            run(zx_ref, rows, hcar[0:1, :], hx_ref, hpx_ref)

    full = lambda shp: pl.BlockSpec(shp, lambda s: (0,) * len(shp))
    colspec = pl.BlockSpec((rows, D), lambda s: (0, col(s)))
    outs = pl.pallas_call(
        body, grid=(GRID_W + 1,),
        in_specs=[pl.BlockSpec((t_ctx, D), lambda s: (cblk, 5)),
                  pl.BlockSpec((rows, D), lambda s: (0, col(s) * 9 + 5)),
                  full((4, D)), full((1, D)),
                  pl.BlockSpec((None, H, DH, DH), lambda s: (d, 0, 0, 0)), full((2, D)),
                  pl.BlockSpec((None, H, DH, DH), lambda s: (d, 0, 0, 0)), full((2, D)), full((2, D))],
        out_specs=(colspec, colspec, full((t_ctx, D))),
        out_shape=(jax.ShapeDtypeStruct((rows, GRID_W * D), F32), jax.ShapeDtypeStruct((rows, GRID_W * D), F32),
                   jax.ShapeDtypeStruct((t_ctx, D), F32)),
        scratch_shapes=[pltpu.VMEM((rmax, D), F32), pltpu.VMEM((rmax, D), F32), pltpu.VMEM((rmax, D), F32),
                        pltpu.VMEM((8, D), F32)],
        name=f"rglru_fwd{d}", compiler_params=_cp(("arbitrary",)))(z, zcol, cw, cb, wr, br, wi, bi, lam)
    return outs[0].reshape(t_lat, D), outs[1].reshape(t_lat, D), outs[2]


def _rglru_bwd(z, cw, cb, wr, br, wi, bi, lam, dh_lat, hp_lat, hp_ctx, d, t_lat, t_ctx, prev=None):
    m = z.shape[0]
    rows = t_lat // GRID_W
    zcol = z.reshape(m // GRID_W, GRID_W * IN_COLS)
    fcol = _col_of(d, GRID_W)
    col = lambda s: fcol(GRID_W - jnp.minimum(s, GRID_W - 1))
    cblk = t_lat // t_ctx
    rmax = max(rows, t_ctx)
    last = prev is not None
    dview = lambda v: v.reshape(rows, GRID_W * D)

    def body(*refs):
        (zc_ref, zx_ref, cw_ref, cb_ref, wr_ref, br_ref, wi_ref, bi_ref, lam_ref, dh_ref, hpx_ref, hpc_ref) = refs[:12]
        k = 12
        if last:
            pdx_ref, pdc_ref = refs[12:14]
            k = 14
        ox_ref, oc_ref, dwr_ref, dwi_ref, sum_ref, a_s, b_s, g_s, gcar = refs[k:]
        s = pl.program_id(0)
        lam_d = lam_ref[d:d + 1, :]
        sp = _softplus_neg(lam_d)
        br_ = br_ref[d:d + 1, :]
        bi_ = bi_ref[d:d + 1, :]

        @pl.when(s == 0)
        def _():
            gcar[...] = jnp.zeros_like(gcar)
            dwr_ref[...] = jnp.zeros_like(dwr_ref)
            dwi_ref[...] = jnp.zeros_like(dwi_ref)
            sum_ref[...] = jnp.zeros_like(sum_ref)

        def run(x_ref, r, dout, hp_ref, prev_ref, o_ref):
            xb = x_ref[...]
            xc = _conv_fwd(xb, cw_ref, cb_ref, r)
            rr, i, a, om = _gates(xc, wr_ref, br_, wi_ref, bi_, sp)
            mult = jnp.sqrt(om)
            a_s[0:r, :] = a
            b_s[0:r, :] = a * dout
            c0 = gcar[0:1, :]
            cl = _scan_rows(1 - d, r, a_s, b_s, g_s, c0)
            cs = g_s[0:r, :]
            row = lax.broadcasted_iota(jnp.int32, (r, D), 0)
            if d:
                cn = jnp.where(row == 0, c0, pltpu.roll(cs, 1, 0))
            else:
                cn = jnp.where(row == r - 1, c0, pltpu.roll(cs, r - 1, 0))
            gcar[0:1, :] = cl
            g = dout + cn
            da = g * hp_ref[...]
            ixc = i * xc
            dmult = g * ixc
            dixc = g * mult
            di = dixc * xc
            dxc = dixc * i
            dlog_a = da * a - dmult * ((1.0 - om) / mult)
            dr = dlog_a * (-RG_C * sp)
            sum_ref[2:3, :] += _colsum(dlog_a * rr)
            drp = dr * rr * (1.0 - rr)
            dip = di * i * (1.0 - i)
            sum_ref[0:1, :] += _colsum(drp)
            sum_ref[1:2, :] += _colsum(dip)
            xcb = xc.astype(BF)
            drb = drp.astype(BF)
            dib = dip.astype(BF)
            parts = []
            for gi in range(H):
                gs = slice(gi * DH, (gi + 1) * DH)
                parts.append(_dot_nt(drb[:, gs], wr_ref[gi]) + _dot_nt(dib[:, gs], wi_ref[gi]))
                dwr_ref[gi] += _dot_tn(xcb[:, gs], drb[:, gs])
                dwi_ref[gi] += _dot_tn(xcb[:, gs], dib[:, gs])
            dxc = dxc + jnp.concatenate(parts, axis=1)
            if last:
                dxc = dxc + prev_ref[...]
                dxb = (_shift_rows(dxc, -1, r) * cw_ref[0:1, :] + dxc * cw_ref[1:2, :]
                       + _shift_rows(dxc, 1, r) * cw_ref[2:3, :] + _shift_rows(dxc, 2, r) * cw_ref[3:4, :])
                sum_ref[3:4, :] += _colsum(dxc)
                sum_ref[4:5, :] += _colsum(dxb)
                sum_ref[8:9, :] += _colsum(dxc * _shift_rows(xb, 1, r))
                sum_ref[9:10, :] += _colsum(dxc * xb)
                sum_ref[10:11, :] += _colsum(dxc * _shift_rows(xb, -1, r))
                sum_ref[11:12, :] += _colsum(dxc * _shift_rows(xb, -2, r))
                o_ref[...] = dxb.astype(BF)
            else:
                o_ref[...] = dxc

        @pl.when(s < GRID_W)
        def _():
            run(zx_ref, rows, dh_ref[...], hpx_ref, pdx_ref if last else None, ox_ref)

        @pl.when(s == GRID_W)
        def _():
            run(zc_ref, t_ctx, jnp.zeros((t_ctx, D), F32), hpc_ref, pdc_ref if last else None, oc_ref)
            sum_ref[2:3, :] = sum_ref[2:3, :] * (RG_C * _sig(-lam_d))

    full = lambda shp: pl.BlockSpec(shp, lambda s: (0,) * len(shp))
    colspec = pl.BlockSpec((rows, D), lambda s: (0, col(s)))
    in_specs = [pl.BlockSpec((t_ctx, D), lambda s: (cblk, 5)),
                pl.BlockSpec((rows, D), lambda s: (0, col(s) * 9 + 5)),
                full((4, D)), full((1, D)),
                pl.BlockSpec((None, H, DH, DH), lambda s: (d, 0, 0, 0)), full((2, D)),
                pl.BlockSpec((None, H, DH, DH), lambda s: (d, 0, 0, 0)), full((2, D)), full((2, D)),
                colspec, colspec, full((t_ctx, D))]
    args = [z, zcol, cw, cb, wr, br, wi, bi, lam, dview(dh_lat), dview(hp_lat), hp_ctx]
    if last:
        in_specs += [colspec, full((t_ctx, D))]
        args += [dview(prev[0]), prev[1]]
    odt = BF if last else F32
    outs = pl.pallas_call(
        body, grid=(GRID_W + 1,), in_specs=in_specs,
        out_specs=(colspec, full((t_ctx, D)), full((H, DH, DH)), full((H, DH, DH)), full((16, D))),
        out_shape=(jax.ShapeDtypeStruct((rows, GRID_W * D), odt), jax.ShapeDtypeStruct((t_ctx, D), odt),
                   jax.ShapeDtypeStruct((H, DH, DH), F32), jax.ShapeDtypeStruct((H, DH, DH), F32),
                   jax.ShapeDtypeStruct((16, D), F32)),
        scratch_shapes=[pltpu.VMEM((rmax, D), F32), pltpu.VMEM((rmax, D), F32), pltpu.VMEM((rmax, D), F32),
                        pltpu.VMEM((8, D), F32)],
        name=f"rglru_bwd{d}", compiler_params=_cp(("arbitrary",)))(*args)
    return (outs[0].reshape(t_lat, D), outs[1]) + tuple(outs[2:])


def _merge(o_f, o_b, h_f, h_b, z, x_all, tgt, mod, norm_g, ln_g, ln_b, p_a, p_b, w_out, t_lat):
    tm = 128
    nt = t_lat // tm

    def body(of_ref, ob_ref, hf_ref, hb_ref, z4_ref, z6_ref, z7_ref, z8_ref, x_ref, t_ref, mod_ref, ng_ref,
             lg_ref, lb_ref, pa_ref, pb_ref, wo_ref,
             do_ref, dh_ref, dz4_ref, dz6_ref, dz7_ref, dz8_ref, gx_ref,
             y_ref, dout_ref, oa_ref, dpa_ref, obv_ref, dpb_ref, acc_ref):
        i = pl.program_id(0)

        @pl.when(i == 0)
        def _():
            acc_ref[...] = jnp.zeros_like(acc_ref)

        def per_head(v):
            return jnp.concatenate(
                [jnp.broadcast_to(jnp.mean(v[:, h * DH:(h + 1) * DH], axis=-1, keepdims=True), (tm, DH))
                 for h in range(H)], axis=1)

        gt = mod_ref[0:1, 2 * D:3 * D]
        gfull = jnp.concatenate([ng_ref[...]] * H, axis=1)
        o = of_ref[...] + ob_ref[...]
        rinv = lax.rsqrt(per_head(o * o) + RMS_EPS)
        n = o * rinv
        na = n * gfull
        z4 = z4_ref[...]
        s4 = _sig(z4)
        silu4 = z4 * s4
        oa = na * silu4
        z6 = z6_ref[...]
        s6 = _sig(z6)
        silu6 = z6 * s6
        hsum = hf_ref[...] + hb_ref[...]
        obv = hsum * silu6
        pa = _dot(oa, pa_ref[...])
        pb = _dot(obv, pb_ref[...])
        s7 = _sig(z7_ref[...])
        s8 = _sig(z8_ref[...])
        y = s7 * pa + s8 * pb
        out = _dot(y, wo_ref[...])
        pre = ALPHA * x_ref[...] + gt * out
        mu = jnp.mean(pre, axis=-1, keepdims=True)
        xc = pre - mu
        rstd = lax.rsqrt(jnp.mean(xc * xc, axis=-1, keepdims=True) + LN_EPS)
        xhat = xc * rstd
        lg = lg_ref[...]
        diff = xhat * lg + lb_ref[...] - t_ref[...]
        acc_ref[8:9, :] += _colsum(diff * diff) * (0.5 / D)
        dxn = diff * (1.0 / D)
        acc_ref[1:2, :] += _colsum(dxn * xhat)
        acc_ref[2:3, :] += _colsum(dxn)
        dxhat = dxn * lg
        dpre = rstd * (dxhat - jnp.mean(dxhat, axis=-1, keepdims=True)
                       - xhat * jnp.mean(dxhat * xhat, axis=-1, keepdims=True))
        gx_ref[...] = ALPHA * dpre
        acc_ref[0:1, :] += _colsum(dpre * out)
        dout = dpre * gt
        dy = _dot_nt(dout, wo_ref[...])
        dpa = dy * s7
        dpb = dy * s8
        dz7 = dy * pa * (s7 * (1.0 - s7))
        dz8 = dy * pb * (s8 * (1.0 - s8))
        doa = _dot_nt(dpa, pa_ref[...])
        dob = _dot_nt(dpb, pb_ref[...])
        dh_ref[...] = dob * silu6
        dz6 = dob * hsum * (s6 * (1.0 + z6 * (1.0 - s6)))
        dna = doa * silu4
        dz4 = doa * na * (s4 * (1.0 + z4 * (1.0 - s4)))
        dng = _colsum(dna * n)
        acc_ref[7:8, 0:DH] += sum(dng[:, h * DH:(h + 1) * DH] for h in range(H))
        dn = dna * gfull
        do_ref[...] = rinv * (dn - n * per_head(dn * n))
        acc_ref[3:4, :] += _colsum(dz4)
        acc_ref[4:5, :] += _colsum(dz6)
        acc_ref[5:6, :] += _colsum(dz7)
        acc_ref[6:7, :] += _colsum(dz8)
        dz4_ref[...] = dz4.astype(BF)
        dz6_ref[...] = dz6.astype(BF)
        dz7_ref[...] = dz7.astype(BF)
        dz8_ref[...] = dz8.astype(BF)
        y_ref[...] = y.astype(BF)
        dout_ref[...] = dout.astype(BF)
        oa_ref[...] = oa.astype(BF)
        dpa_ref[...] = dpa.astype(BF)
        obv_ref[...] = obv.astype(BF)
        dpb_ref[...] = dpb.astype(BF)

        @pl.when(i == nt - 1)
        def _():
            acc_ref[9:10, :] = jnp.broadcast_to(jnp.sum(acc_ref[8:9, :], axis=-1, keepdims=True), (1, D))

    row = pl.BlockSpec((tm, D), lambda i: (i, 0))
    zs = lambda cb: pl.BlockSpec((tm, D), lambda i: (i, cb))
    full = lambda shp: pl.BlockSpec(shp, lambda i: (0,) * len(shp))
    f32o = jax.ShapeDtypeStruct((t_lat, D), F32)
    bfo = jax.ShapeDtypeStruct((t_lat, D), BF)
    return pl.pallas_call(
        body, grid=(nt,),
        in_specs=[row, row, row, row, zs(4), zs(6), zs(7), zs(8), row, row, full((16, 3 * D)), full((1, DH)),
                  full((1, D)), full((1, D)), full((D, D)), full((D, D)), full((D, D))],
        out_specs=(row,) * 13 + (full((16, D)),),
        out_shape=(f32o, f32o, bfo, bfo, bfo, bfo, f32o, bfo, bfo, bfo, bfo, bfo, bfo,
                   jax.ShapeDtypeStruct((16, D), F32)),
        name="merge", compiler_params=_cp(("arbitrary",)))(
            o_f, o_b, h_f, h_b, z, z, z, z, x_all, tgt, mod, norm_g, ln_g, ln_b, p_a, p_b, w_out)


def _wgrad(a, b, name):
    tm = 512

    def body(a_ref, b_ref, o_ref):
        @pl.when(pl.program_id(0) == 0)
        def _():
            o_ref[...] = jnp.zeros_like(o_ref)
        o_ref[...] += _dot_tn(a_ref[...], b_ref[...])

    row = pl.BlockSpec((tm, D), lambda i: (i, 0))
    return pl.pallas_call(body, grid=(a.shape[0] // tm,), in_specs=[row, row],
                          out_specs=pl.BlockSpec((D, D), lambda i: (0, 0)),
                          out_shape=jax.ShapeDtypeStruct((D, D), F32), name=name,
                          compiler_params=_cp(("arbitrary",)))(a, b)


def _wgrad_in(x_all, mod, dz, n_lat_tiles):
    m = x_all.shape[0]
    tm = 256

    def body(x_ref, mod_ref, dz_ref, o_ref):
        i = pl.program_id(1)

        @pl.when(i == 0)
        def _():
            o_ref[...] = jnp.zeros_like(o_ref)
        u, _ = _modulate(x_ref, mod_ref, i < n_lat_tiles)
        o_ref[0] += _dot_tn(u, dz_ref[...])

    return pl.pallas_call(
        body, grid=(NSH, m // tm),
        in_specs=[pl.BlockSpec((tm, D), lambda n, i: (i, 0)),
                  pl.BlockSpec((16, 3 * D), lambda n, i: (0, 0)),
                  pl.BlockSpec((tm, SHC), lambda n, i: (i, n))],
        out_specs=pl.BlockSpec((1, D, SHC), lambda n, i: (n, 0, 0)),
        out_shape=jax.ShapeDtypeStruct((NSH, D, SHC), F32), name="wgrad_in",
        compiler_params=_cp(("arbitrary", "arbitrary")))(x_all, mod, dz)


def _du(dz, w_in_g, x_all, mod, gxres, n_lat_tiles):
    m = x_all.shape[0]
    tm = 256
    nt = m // tm
    nct = nt - n_lat_tiles
    rblk = lambda i: jnp.where(i < nct, n_lat_tiles + i, i - nct)
    lblk = lambda i: jnp.maximum(i - nct, 0)

    def body(dz_ref, w_ref, x_ref, mod_ref, gr_ref, gx_ref, dm_ref, acc):
        i = pl.program_id(0)
        n = pl.program_id(1)
        is_lat = i >= nct

        @pl.when((i == 0) & (n == 0))
        def _():
            dm_ref[...] = jnp.zeros_like(dm_ref)

        @pl.when(n == 0)
        def _():
            acc[...] = jnp.zeros_like(acc)

        acc[...] += _dot_nt(dz_ref[...], w_ref[0])

        @pl.when(n == NSH - 1)
        def _():
            du = acc[...]
            sc = jnp.where(is_lat, mod_ref[0:1, D:2 * D], mod_ref[1:2, D:2 * D])
            dsh = _colsum(du)
            dsc = _colsum(du * x_ref[...])

            @pl.when(is_lat)
            def _():
                gx_ref[...] = du * (1.0 + sc) + gr_ref[...]
                dm_ref[0:1, 0:D] += dsh
                dm_ref[0:1, D:2 * D] += dsc

            @pl.when(jnp.logical_not(is_lat))
            def _():
                dm_ref[1:2, 0:D] += dsh
                dm_ref[1:2, D:2 * D] += dsc

    return pl.pallas_call(
        body, grid=(nt, NSH),
        in_specs=[pl.BlockSpec((tm, SHC), lambda i, n: (rblk(i), n)),
                  pl.BlockSpec((1, D, SHC), lambda i, n: (n, 0, 0)),
                  pl.BlockSpec((tm, D), lambda i, n: (rblk(i), 0)),
                  pl.BlockSpec((16, 3 * D), lambda i, n: (0, 0)),
                  pl.BlockSpec((tm, D), lambda i, n: (lblk(i), 0))],
        out_specs=(pl.BlockSpec((tm, D), lambda i, n: (lblk(i), 0)),
                   pl.BlockSpec((8, 2 * D), lambda i, n: (0, 0))),
        out_shape=(jax.ShapeDtypeStruct((n_lat_tiles * tm, D), F32), jax.ShapeDtypeStruct((8, 2 * D), F32)),
        scratch_shapes=[pltpu.VMEM((tm, D), F32)],
        name="du", compiler_params=_cp(("arbitrary", "arbitrary")))(dz, w_in_g, x_all, mod, gxres)


def _row_tile(rows, cols):
    t = 8
    while t * 2 * cols * 4 <= (1 << 20) and rows % (t * 2) == 0:
        t *= 2
    return t


def _addn(arrs, name):
    shape = arrs[0].shape
    cols = shape[-1]
    flat = [a.reshape(-1, cols) for a in arrs]
    rows = flat[0].shape[0]
    tr = _row_tile(rows, cols)

    def body(*refs):
        acc = refs[0][...]
        for r in refs[1:-1]:
            acc = acc + r[...]
        refs[-1][...] = acc

    spec = pl.BlockSpec((tr, cols), lambda i: (i, 0))
    out = pl.pallas_call(body, grid=(rows // tr,), in_specs=[spec] * len(flat), out_specs=spec,
                         out_shape=jax.ShapeDtypeStruct((rows, cols), F32), name=name,
                         compiler_params=_cp(("arbitrary",)))(*flat)
    return out.reshape(shape)


def _adamw(w, g, m, v, name):
    rows, cols = w.shape
    tr = _row_tile(rows, cols)

    def body(w_ref, g_ref, m_ref, v_ref, d_ref, nm_ref, nv_ref):
        gg = g_ref[...]
        m2 = ADAM_B1 * m_ref[...] + (1.0 - ADAM_B1) * gg
        v2 = ADAM_B2 * v_ref[...] + (1.0 - ADAM_B2) * (gg * gg)
        m_hat = m2 / (1.0 - ADAM_B1 ** ADAM_STEP)
        v_hat = v2 / (1.0 - ADAM_B2 ** ADAM_STEP)
        d_ref[...] = -ADAM_LR * (m_hat / (jnp.sqrt(v_hat) + ADAM_EPS) + ADAM_WD * w_ref[...])
        nm_ref[...] = m2
        nv_ref[...] = v2

    spec = pl.BlockSpec((tr, cols), lambda i: (i, 0))
    o = jax.ShapeDtypeStruct((rows, cols), F32)
    return pl.pallas_call(body, grid=(rows // tr,), in_specs=[spec] * 4, out_specs=(spec,) * 3,
                          out_shape=(o, o, o), name=name, compiler_params=_cp(("arbitrary",)))(w, g, m, v)


_ANY = pl.BlockSpec(memory_space=pl.ANY)


def _place():
    return lax.axis_index("x"), lax.axis_index("y"), lax.axis_index("c")


def _allgather_chips(shards):
    n = len(shards)

    def body(*refs):
        ins, outs = refs[:n], refs[n:2 * n]
        send, recv, lsem = refs[2 * n:]
        x, y, c = _place()
        me = 2 * x + y
        peers = ((1 - x, y, c), (x, 1 - y, c), (1 - x, 1 - y, c))
        copies = []
        for j in range(n):
            cp = pltpu.make_async_copy(ins[j], outs[j].at[me], lsem.at[j])
            cp.start()
            copies.append(cp)
            for k, p in enumerate(peers):
                cp = pltpu.make_async_remote_copy(src_ref=ins[j], dst_ref=outs[j].at[me], send_sem=send.at[3 * j + k],
                                                  recv_sem=recv.at[3 * j + k], device_id=p, device_id_type=MESH)
                cp.start()
                copies.append(cp)
        for cp in copies:
            cp.wait()

    return pl.pallas_call(
        body, in_specs=[_ANY] * n, out_specs=[_ANY] * n,
        out_shape=[jax.ShapeDtypeStruct((NSH,) + s.shape, s.dtype) for s in shards],
        scratch_shapes=[pltpu.SemaphoreType.DMA((3 * n,)), pltpu.SemaphoreType.DMA((3 * n,)),
                        pltpu.SemaphoreType.DMA((n,))],
        name="allgather_weights")(*shards)


def _rs_sibling(grads):
    n = len(grads)

    def body(*refs):
        ins, got = refs[:n], refs[n:2 * n]
        send, recv = refs[2 * n:]
        x, y, c = _place()
        copies = []
        for j in range(n):
            hr = ins[j].shape[1] // 2
            for s in range(NSH):
                give = ins[j].at[s, pl.ds(pl.multiple_of((1 - c) * hr, 8), hr), :]
                cp = pltpu.make_async_remote_copy(src_ref=give, dst_ref=got[j].at[s], send_sem=send.at[NSH * j + s],
                                                  recv_sem=recv.at[NSH * j + s], device_id=(x, y, 1 - c),
                                                  device_id_type=MESH)
                cp.start()
                copies.append(cp)
        for cp in copies:
            cp.wait()

    half = [jax.ShapeDtypeStruct((NSH, g.shape[1] // 2, g.shape[2]), F32) for g in grads]
    return pl.pallas_call(
        body, in_specs=[_ANY] * n, out_specs=[_ANY] * n, out_shape=half,
        scratch_shapes=[pltpu.SemaphoreType.DMA((NSH * n,)), pltpu.SemaphoreType.DMA((NSH * n,))],
        name="rs_sibling")(*grads)


def _core_vec():
    return lax.axis_index("c").astype(jnp.int32).reshape(1)


def _rs_add1(g, got, name):
    _, r, cols = g.shape
    hr = r // 2
    tr = _row_tile(hr, cols)
    nb = hr // tr

    def body(c_ref, g_ref, got_ref, o_ref):
        o_ref[...] = (g_ref[...] + got_ref[...]).astype(BF)

    spec = pl.BlockSpec((1, tr, cols), lambda s, i, c_ref: (s, i, 0))
    return pl.pallas_call(
        body, grid_spec=pltpu.PrefetchScalarGridSpec(
            num_scalar_prefetch=1, grid=(NSH, nb),
            in_specs=[pl.BlockSpec((1, tr, cols), lambda s, i, c_ref: (s, c_ref[0] * nb + i, 0)), spec],
            out_specs=spec),
        out_shape=jax.ShapeDtypeStruct((NSH, hr, cols), BF), name=name,
        compiler_params=_cp(("arbitrary", "arbitrary")))(_core_vec(), g, got)


def _rs_add2(sums, got, name):
    _, hr, cols = sums.shape
    tr = _row_tile(hr, cols)
    nb = hr // tr
    place = jnp.stack([2 * lax.axis_index("x") + lax.axis_index("y"), lax.axis_index("c")]).astype(jnp.int32)

    def body(p_ref, s_ref, got_ref, o_ref):
        f = lambda v: v.astype(F32)
        o_ref[...] = f(s_ref[0]) + f(got_ref[0]) + f(got_ref[1]) + f(got_ref[2])

    return pl.pallas_call(
        body, grid_spec=pltpu.PrefetchScalarGridSpec(
            num_scalar_prefetch=1, grid=(nb,),
            in_specs=[pl.BlockSpec((1, tr, cols), lambda i, p_ref: (p_ref[0], i, 0)),
                      pl.BlockSpec((3, tr, cols), lambda i, p_ref: (0, i, 0))],
            out_specs=pl.BlockSpec((tr, cols), lambda i, p_ref: (p_ref[1] * nb + i, 0))),
        out_shape=jax.ShapeDtypeStruct((2 * hr, cols), F32), name=name,
        compiler_params=_cp(("arbitrary",)))(place, sums, got)


def _rs_chips(sums):
    n = len(sums)

    def body(*refs):
        ins, got = refs[:n], refs[n:2 * n]
        send, recv = refs[2 * n:]
        x, y, c = _place()
        peers = ((1 - x, y), (x, 1 - y), (1 - x, 1 - y))
        copies = []
        for j in range(n):
            for k, (px, py) in enumerate(peers):
                cp = pltpu.make_async_remote_copy(src_ref=ins[j].at[2 * px + py], dst_ref=got[j].at[k],
                                                  send_sem=send.at[3 * j + k], recv_sem=recv.at[3 * j + k],
                                                  device_id=(px, py, c), device_id_type=MESH)
                cp.start()
                copies.append(cp)
        for cp in copies:
            cp.wait()

    three = [jax.ShapeDtypeStruct((3,) + g.shape[1:], g.dtype) for g in sums]
    return pl.pallas_call(
        body, in_specs=[_ANY] * n, out_specs=[_ANY] * n, out_shape=three,
        scratch_shapes=[pltpu.SemaphoreType.DMA((3 * n,)), pltpu.SemaphoreType.DMA((3 * n,))],
        name="rs_chips")(*sums)


def _ag_sibling(fulls):
    n = len(fulls)
    nck = 4

    def body(*refs):
        outs = refs[n:2 * n]
        send, recv = refs[2 * n:]
        x, y, c = _place()
        copies = []
        for j in range(n):
            qr = outs[j].shape[0] // (2 * nck)
            for k in range(nck):
                rows = outs[j].at[pl.ds(pl.multiple_of((c * nck + k) * qr, 8), qr), :]
                cp = pltpu.make_async_remote_copy(src_ref=rows, dst_ref=rows, send_sem=send.at[nck * j + k],
                                                  recv_sem=recv.at[nck * j + k], device_id=(x, y, 1 - c),
                                                  device_id_type=MESH)
                cp.start()
                copies.append(cp)
        for cp in copies:
            cp.wait()

    return pl.pallas_call(
        body, in_specs=[_ANY] * n, out_specs=[_ANY] * n,
        out_shape=[jax.ShapeDtypeStruct(f.shape, F32) for f in fulls],
        input_output_aliases={j: j for j in range(n)},
        scratch_shapes=[pltpu.SemaphoreType.DMA((nck * n,)), pltpu.SemaphoreType.DMA((nck * n,))],
        name="ag_sibling")(*fulls)


def _allreduce_small(buf):
    rows = buf.shape[0]
    pr = rows // 8

    def body(in_ref, out_ref, stage, send1, recv1, send2, recv2):
        x, y, c = _place()
        me = 4 * x + 2 * y + c

        def peer(k):
            kx, ky, kc = (k >> 2) & 1, (k >> 1) & 1, k & 1
            return (x ^ kx, y ^ ky, c ^ kc)

        def piece(ref, idx):
            return ref.at[pl.ds(pl.multiple_of(idx * pr, 8), pr), :]

        copies = []
        for k in range(1, 8):
            px, py, pc = peer(k)
            cp = pltpu.make_async_remote_copy(src_ref=piece(in_ref, 4 * px + 2 * py + pc), dst_ref=stage.at[k],
                                              send_sem=send1.at[k], recv_sem=recv1.at[k],
                                              device_id=(px, py, pc), device_id_type=MESH)
            cp.start()
            copies.append(cp)
        for cp in copies:
            cp.wait()
        acc = piece(in_ref, me)[...]
        for k in range(1, 8):
            acc = acc + stage[k]
        piece(out_ref, me)[...] = acc
        copies = []
        for k in range(1, 8):
            cp = pltpu.make_async_remote_copy(src_ref=piece(out_ref, me), dst_ref=piece(out_ref, me),
                                              send_sem=send2.at[k], recv_sem=recv2.at[k],
                                              device_id=peer(k), device_id_type=MESH)
            cp.start()
            copies.append(cp)
        for cp in copies:
            cp.wait()

    vm = pl.BlockSpec(memory_space=pltpu.VMEM)
    return pl.pallas_call(
        body, in_specs=[vm], out_specs=vm, out_shape=jax.ShapeDtypeStruct((rows, D), F32),
        scratch_shapes=[pltpu.VMEM((8, pr, D), F32)] + [pltpu.SemaphoreType.DMA((8,))] * 4,
        name="allreduce_small", compiler_params=_cp())(buf)


def _reduce_scatter(grads):
    got = _rs_sibling(grads)
    sums = [_rs_add1(g, b, f"rs_add1_{j}") for j, (g, b) in enumerate(zip(grads, got))]
    got = _rs_chips(sums)
    fulls = [_rs_add2(a, b, f"rs_add2_{j}") for j, (a, b) in enumerate(zip(sums, got))]
    return _ag_sibling(fulls)


def _rows(a):
    flat = a.reshape(-1)
    pad = (-flat.shape[0]) % D
    if pad:
        flat = jnp.concatenate([flat, jnp.zeros((pad,), flat.dtype)])
    return flat.reshape(-1, D)


def _pad_rows(a, mult):
    pad = (-a.shape[0]) % mult
    return jnp.concatenate([a, jnp.zeros((pad, a.shape[1]), a.dtype)]) if pad else a


def _local_step(x, c, ctx, c_ctx, tgt, w_mod_g, b_mod, w_in_g, b_in, lbl, norm_g, cw, cb, wr, br, wi, bi, lam,
                p_a, p_b, w_out, ln_g, ln_b):
    t_lat, t_ctx = x.shape[0], ctx.shape[0]
    nlt = t_lat // 256
    nlb, ncb = t_lat // RB, t_ctx // RB
    cc = jnp.zeros((16, D), F32).at[0].set(c).at[1].set(c_ctx)
    mod = _mod_fwd(cc, w_mod_g, b_mod)
    x_all = jnp.concatenate([x, ctx], axis=0)
    z = _inproj(x_all, mod, w_in_g, b_in, nlt)
    o0, st0 = _gla_fwd(z, lbl, 0, nlb, ncb)
    o1, st1 = _gla_fwd(z, lbl, 1, nlb, ncb)
    h0, hp0, hpc0 = _rglru_fwd(z, cw, cb, wr, br, wi, bi, lam, 0, t_lat, t_ctx)
    h1, hp1, hpc1 = _rglru_fwd(z, cw, cb, wr, br, wi, bi, lam, 1, t_lat, t_ctx)
    (do, dh, dz4, dz6, dz7, dz8, gxres, y, dout, oa, dpa, obv, dpb, acc) = _merge(
        o0, o1, h0, h1, z, x_all, tgt, mod, norm_g, ln_g, ln_b, p_a, p_b, w_out, t_lat)
    gp_a = _wgrad(oa, dpa, "wgrad_pa")
    gp_b = _wgrad(obv, dpb, "wgrad_pb")
    gw_out = _wgrad(y, dout, "wgrad_wout")
    dxc_lat, dxc_ctx, dwr0, dwi0, sb0 = _rglru_bwd(z, cw, cb, wr, br, wi, bi, lam, dh, hp0, hpc0, 0, t_lat, t_ctx)
    dz5_lat, dz5_ctx, dwr1, dwi1, sb1 = _rglru_bwd(z, cw, cb, wr, br, wi, bi, lam, dh, hp1, hpc1, 1, t_lat, t_ctx,
                                                   prev=(dxc_lat, dxc_ctx))
    dq0, dv0, dz1, sa0 = _gla_bwd(z, lbl, do, st0, 0, nlb, ncb)
    dz0, dz3, dz2, sa1 = _gla_bwd(z, lbl, do, st1, 1, nlb, ncb, prev=(dq0, dv0))
    zc = jnp.zeros((t_ctx, D), BF)
    pad = lambda a: jnp.concatenate([a, zc], axis=0)
    dz = jnp.concatenate([dz0, dz1, dz2, dz3, pad(dz4), jnp.concatenate([dz5_lat, dz5_ctx], axis=0),
                          pad(dz6), pad(dz7), pad(dz8)], axis=1)
    gw_in = _wgrad_in(x_all, mod, dz, nlt)
    gx, dm = _du(dz, w_in_g, x_all, mod, gxres, nlt)
    dmod = jnp.zeros((16, 3 * D), F32).at[0:2, 0:2 * D].set(dm[0:2]).at[0, 2 * D:].set(acc[0])
    gw_mod, dcc = _mod_bwd(cc, dmod, w_mod_g)
    small = dict(
        c_ctx=dcc[1:2], b_mod=(dmod[0] + dmod[1]).reshape(3, D),
        b_in=jnp.stack([sa1[2], sa0[0], sa1[0], sa1[3], acc[3], sb1[4], acc[4], acc[5], acc[6]]),
        lb_logits=jnp.stack([sa0[1], sa1[1], -sa0[1], -sa1[1]]),
        norm_a_g=acc[7:8], conv_w=sb1[8:12], conv_b=sb1[3:4],
        w_r=jnp.stack([dwr0, dwr1]).reshape(-1, D), w_i=jnp.stack([dwi0, dwi1]).reshape(-1, D),
        b_r=jnp.stack([sb0[0], sb1[0]]), b_i=jnp.stack([sb0[1], sb1[1]]), lam=jnp.stack([sb0[2], sb1[2]]),
        ln_g=acc[1:2], ln_b=acc[2:3])
    big = dict(w_mod=gw_mod, w_in=gw_in, p_a=gp_a.reshape(NSH, D // NSH, D), p_b=gp_b.reshape(NSH, D // NSH, D),
               w_out=gw_out.reshape(NSH, D // NSH, D))
    return acc[9, 0], gx, big, small


_SMALL = ("c_ctx", "b_mod", "b_in", "lb_logits", "norm_a_g", "conv_w", "conv_b", "w_r", "w_i", "b_r", "b_i", "lam",
          "ln_g", "ln_b")
_BIG = ("w_mod", "w_in", "p_a", "p_b", "w_out")
_COL_SHARDED = ("lb_logits", "conv_w", "b_r", "b_i", "lam")
_WEIGHTS = ("c_ctx", "w_mod", "b_mod", "w_in", "b_in", "lb_logits", "norm_a_g", "conv_w", "conv_b", "w_r", "b_r", "w_i",
            "b_i", "lam", "p_a", "p_b", "w_out", "ln_g", "ln_b")


def kernel(x, c, ctx, c_ctx, w_mod, b_mod, w_in, b_in, lb_logits, norm_a_g, conv_w, conv_b, w_r, b_r, w_i, b_i, lam, p_a, p_b, w_out, ln_g, ln_b, loss_target, m_c_ctx, m_w_mod, m_b_mod, m_w_in, m_b_in, m_lb_logits, m_norm_a_g, m_conv_w, m_conv_b, m_w_r, m_b_r, m_w_i, m_b_i, m_lam, m_p_a, m_p_b, m_w_out, m_ln_g, m_ln_b, v_c_ctx, v_w_mod, v_b_mod, v_w_in, v_b_in, v_lb_logits, v_norm_a_g, v_conv_w, v_conv_b, v_w_r, v_b_r, v_w_i, v_b_i, v_lam, v_p_a, v_p_b, v_w_out, v_ln_g, v_ln_b):
    w = dict(c_ctx=c_ctx, w_mod=w_mod, b_mod=b_mod, w_in=w_in, b_in=b_in, lb_logits=lb_logits, norm_a_g=norm_a_g,
             conv_w=conv_w, conv_b=conv_b, w_r=w_r, b_r=b_r, w_i=w_i, b_i=b_i, lam=lam, p_a=p_a, p_b=p_b, w_out=w_out,
             ln_g=ln_g, ln_b=ln_b)
    m = dict(c_ctx=m_c_ctx, w_mod=m_w_mod, b_mod=m_b_mod, w_in=m_w_in, b_in=m_b_in, lb_logits=m_lb_logits,
             norm_a_g=m_norm_a_g, conv_w=m_conv_w, conv_b=m_conv_b, w_r=m_w_r, b_r=m_b_r, w_i=m_w_i, b_i=m_b_i,
             lam=m_lam, p_a=m_p_a, p_b=m_p_b, w_out=m_w_out, ln_g=m_ln_g, ln_b=m_ln_b)
    v = dict(c_ctx=v_c_ctx, w_mod=v_w_mod, b_mod=v_b_mod, w_in=v_w_in, b_in=v_b_in, lb_logits=v_lb_logits,
             norm_a_g=v_norm_a_g, conv_w=v_conv_w, conv_b=v_conv_b, w_r=v_w_r, b_r=v_b_r, w_i=v_w_i, b_i=v_b_i,
             lam=v_lam, p_a=v_p_a, p_b=v_p_b, w_out=v_w_out, ln_g=v_ln_g, ln_b=v_ln_b)
    shard = 2 * lax.axis_index("x") + lax.axis_index("y")
    cs = D // NSH

    narrow = _pad_rows(jnp.concatenate([lb_logits.reshape(4, cs), conv_w[0], b_r[0], b_i[0], lam[0]], axis=0), 8)
    big_shards = [w[k][0].astype(BF) for k in _BIG]
    gathered = _allgather_chips(big_shards + [narrow])
    w_mod_g, w_in_g = gathered[0], gathered[1]
    p_a_g, p_b_g, w_out_g = (g.reshape(D, D) for g in gathered[2:5])
    nar = jnp.transpose(gathered[5], (1, 0, 2)).reshape(-1, D)
    lbl, cw, br, bi, lam_f = nar[0:4].reshape(2, 2, D), nar[4:8], nar[8:10], nar[10:12], nar[12:14]

    loss, gx, big, small = _local_step(
        x[0], c[0], ctx[0], c_ctx, loss_target[0], w_mod_g, b_mod, w_in_g, b_in, lbl, norm_a_g, cw, conv_b,
        w_r[0], br, w_i[0], bi, lam_f, p_a_g, p_b_g, w_out_g, ln_g, ln_b)
    loss = lax.psum(loss, ("x", "y", "c"))

    g_big = dict(zip(_BIG, _reduce_scatter([big[k] for k in _BIG])))
    sizes = [small[k].shape[0] for k in _SMALL]
    red = _allreduce_small(_pad_rows(jnp.concatenate([_pad_rows(small[k], 8) for k in _SMALL], axis=0), 64))
    grads = {}
    off = 0
    for k, n in zip(_SMALL, sizes):
        g = red[off:off + n]
        off += n + (-n) % 8
        if k == "norm_a_g":
            g = g[:, :DH]
        if k in _COL_SHARDED:
            g = lax.dynamic_slice_in_dim(g, shard * cs, cs, axis=1)
        grads[k] = g.reshape(w[k].shape)
    for k in _BIG:
        grads[k] = g_big[k].reshape(w[k].shape)

    delta, new_m, new_v = {}, {}, {}
    for k in _BIG:
        shp = w[k].shape
        two = lambda a: a.reshape(shp[-2], shp[-1])
        d_, m_, v_ = _adamw(two(w[k]), two(grads[k]), two(m[k]), two(v[k]), f"adamw_{k}")
        delta[k], new_m[k], new_v[k] = d_.reshape(shp), m_.reshape(shp), v_.reshape(shp)
    flat = lambda t: jnp.concatenate([_pad_rows(_rows(t[k]), 8) for k in _SMALL], axis=0)
    d_, m_, v_ = _adamw(flat(w), flat(grads), flat(m), flat(v), "adamw_small")
    off = 0
    for k in _SMALL:
        n = w[k].size
        nr = -(-n // D)
        for src, dst in ((d_, delta), (m_, new_m), (v_, new_v)):
            dst[k] = src[off:off + nr].reshape(-1)[:n].reshape(w[k].shape)
        off += nr + (-nr) % 8

    return (loss, gx[None], *[grads[k] for k in _WEIGHTS], *[delta[k] for k in _WEIGHTS],
            *[new_m[k] for k in _WEIGHTS], *[new_v[k] for k in _WEIGHTS])
```

```python
import functools

import jax
import jax.numpy as jnp
from jax import lax
from jax.experimental import pallas as pl
from jax.experimental.pallas import tpu as pltpu

F32 = jnp.float32
BF = jnp.bfloat16
MESH = pl.DeviceIdType.MESH

D = 1024
H = 8
DH = 128
CH = 64
RB = 256
NCK = RB // CH
GRID_W = 64
IN_COLS = 9 * D
NSH = 4
SHC = IN_COLS // NSH
RG_C = 8.0
ALPHA = 2.0 ** 0.25
LN_EPS = 1e-5
RMS_EPS = 1e-6
Q_SCALE = DH ** -0.5
ADAM_LR, ADAM_B1, ADAM_B2, ADAM_EPS, ADAM_WD, ADAM_STEP = 0.001, 0.9, 0.999, 1e-08, 0.01, 10
VMEM_LIMIT = 56 * 1024 * 1024


def _cp(sem=None):
    return pltpu.CompilerParams(dimension_semantics=sem, vmem_limit_bytes=VMEM_LIMIT)


def _sig(x):
    return 1.0 / (1.0 + jnp.exp(-x))


def _dot(a, b):
    return jnp.dot(a.astype(BF), b.astype(BF), preferred_element_type=F32)


def _dot_nt(a, b):
    return lax.dot_general(a.astype(BF), b.astype(BF), (((1,), (1,)), ((), ())), preferred_element_type=F32)


def _dot_tn(a, b):
    return lax.dot_general(a.astype(BF), b.astype(BF), (((0,), (0,)), ((), ())), preferred_element_type=F32)


def _dot_hi(a, b):
    return jnp.dot(a, b, preferred_element_type=F32, precision=lax.Precision.HIGHEST)


def _colsum(v):
    return jnp.sum(v, axis=0, keepdims=True)


def _mod_fwd(cc, w_mod_g, b_mod):
    def body(cc_ref, w_ref, b_ref, o_ref):
        v = cc_ref[...]
        s = v * _sig(v)
        for k in range(NSH):
            cs = slice(k * 768, (k + 1) * 768)
            o_ref[:, cs] = _dot(s, w_ref[k]) + b_ref[:, cs]
    return pl.pallas_call(body, out_shape=jax.ShapeDtypeStruct((16, 3 * D), F32), name="mod_fwd",
                          compiler_params=_cp())(cc, w_mod_g, b_mod)


def _mod_bwd(cc, dmod, w_mod_g):
    def body(cc_ref, dm_ref, w_ref, dw_ref, dcc_ref):
        v = cc_ref[...]
        sg = _sig(v)
        s = v * sg
        ds = jnp.zeros((16, D), F32)
        for k in range(NSH):
            dm = dm_ref[:, k * 768:(k + 1) * 768]
            dw_ref[k] = _dot_tn(s, dm)
            ds = ds + _dot_nt(dm, w_ref[k])
        dcc_ref[...] = ds * (sg * (1.0 + v * (1.0 - sg)))
    return pl.pallas_call(body, out_shape=(jax.ShapeDtypeStruct((NSH, D, 768), F32),
                                           jax.ShapeDtypeStruct((16, D), F32)),
                          name="mod_bwd", compiler_params=_cp())(cc, dmod, w_mod_g)


def _modulate(x_ref, mod_ref, is_lat):
    sh = jnp.where(is_lat, mod_ref[0:1, 0:D], mod_ref[1:2, 0:D])
    sc = jnp.where(is_lat, mod_ref[0:1, D:2 * D], mod_ref[1:2, D:2 * D])
    return x_ref[...] * (1.0 + sc) + sh, sc


def _inproj(x_all, mod, w_in_g, b_in, n_lat_tiles):
    m = x_all.shape[0]
    tm = 256
    nt = m // tm

    def body(x_ref, mod_ref, w_ref, b_ref, z_ref):
        i = pl.program_id(1)
        u, _ = _modulate(x_ref, mod_ref, i < n_lat_tiles)
        z_ref[...] = _dot(u, w_ref[0]) + b_ref[...]

    return pl.pallas_call(
        body, grid=(NSH, nt),
        in_specs=[pl.BlockSpec((tm, D), lambda n, i: (i, 0)),
                  pl.BlockSpec((16, 3 * D), lambda n, i: (0, 0)),
                  pl.BlockSpec((1, D, SHC), lambda n, i: (n, 0, 0)),
                  pl.BlockSpec((1, SHC), lambda n, i: (0, n))],
        out_specs=pl.BlockSpec((tm, SHC), lambda n, i: (i, n)),
        out_shape=jax.ShapeDtypeStruct((m, IN_COLS), F32), name="inproj",
        compiler_params=_cp(("arbitrary", "arbitrary")))(x_all, mod, w_in_g, b_in)


def _gla_mask(d):
    row = lax.broadcasted_iota(jnp.int32, (CH, CH), 0)
    col = lax.broadcasted_iota(jnp.int32, (CH, CH), 1)
    return (row <= col) if d else (row >= col)


def _chunk_cumsum(v, rev):
    n = v.shape[0]
    pos = lax.broadcasted_iota(jnp.int32, v.shape, 0) & (CH - 1)
    s = 1
    while s < CH:
        if rev:
            v = v + jnp.where(pos < CH - s, pltpu.roll(v, n - s, 0), 0.0)
        else:
            v = v + jnp.where(pos >= s, pltpu.roll(v, s, 0), 0.0)
        s *= 2
    return v


def _chunk_rows(c):
    return slice(c * CH, (c + 1) * CH)


def _gla_features(zq, zf, lb, d):
    sq = _sig(zq)
    q = zq * sq * Q_SCALE
    sf = _sig(zf)
    f = lb + (1.0 - lb) * sf
    k = 1.0 - f
    g = _chunk_cumsum(jnp.log(f), d)
    last = 0 if d else CH - 1
    gls = [g[c * CH + last:c * CH + last + 1, :] for c in range(NCK)]
    glb = jnp.concatenate([jnp.broadcast_to(gl, (CH, D)) for gl in gls], axis=0)
    eg, eig, eeg = jnp.exp(g), jnp.exp(-g), jnp.exp(glb - g)
    decs = [jnp.exp(gl) for gl in gls]
    return sq, sf, f, q * eg, k * eig, k * eeg, eg, eig, eeg, decs


def _lower_bound(lbl_ref, d):
    return _sig(lbl_ref[0, d:d + 1, :] - lbl_ref[1, d:d + 1, :])


def _gla_rb(d, nlb, ncb):
    nrb = nlb + ncb
    if d:
        return lambda s: nrb - 1 - s
    return lambda s: jnp.where(s < ncb, nlb + s, s - ncb)


def _gla_fwd(z, lbl, d, nlb, ncb):
    m = z.shape[0]
    nrb = nlb + ncb
    rb = _gla_rb(d, nlb, ncb)

    def body(q_ref, f_ref, v_ref, lbl_ref, o_ref, st_ref, S):
        s = pl.program_id(0)

        @pl.when(s == 0)
        def _():
            S[...] = jnp.zeros_like(S)

        lb = _lower_bound(lbl_ref, d)
        mb = _gla_mask(d)
        _, _, _, qd, ki, ke, _, _, _, decs = _gla_features(q_ref[...], f_ref[...], lb, d)
        qd, ki, ke, v = qd.astype(BF), ki.astype(BF), ke.astype(BF), v_ref[...].astype(BF)
        order = range(NCK - 1, -1, -1) if d else range(NCK)
        for h in range(H):
            hs = slice(h * DH, (h + 1) * DH)
            intra, upd = {}, {}
            for c in range(NCK):
                rs = _chunk_rows(c)
                a = jnp.where(mb, _dot_nt(qd[rs, hs], ki[rs, hs]), 0.0)
                intra[c] = _dot(a, v[rs, hs])
                upd[c] = _dot_tn(v[rs, hs], ke[rs, hs])
            st = S[h]
            for c in order:
                rs = _chunk_rows(c)
                st_ref[c, h] = st
                o_ref[rs, hs] = intra[c] + _dot_nt(qd[rs, hs], st)
                st = st * decs[c][:, hs] + upd[c]
            S[h] = st

    def zspec(cb):
        return pl.BlockSpec((RB, D), lambda s: (rb(s), cb))

    return pl.pallas_call(
        body, grid=(nrb,),
        in_specs=[zspec(0), zspec(1 + d), zspec(3), pl.BlockSpec((2, 2, D), lambda s: (0, 0, 0))],
        out_specs=(pl.BlockSpec((RB, D), lambda s: (rb(s), 0)),
                   pl.BlockSpec((NCK, H, DH, DH), lambda s: (rb(s), 0, 0, 0))),
        out_shape=(jax.ShapeDtypeStruct((m, D), F32),
                   jax.ShapeDtypeStruct((m // CH, H, DH, DH), F32)),
        scratch_shapes=[pltpu.VMEM((H, DH, DH), F32)],
        name=f"gla_fwd{d}", compiler_params=_cp(("arbitrary",)))(z, z, z, lbl)


def _gla_bwd(z, lbl, do_lat, states, d, nlb, ncb, prev=None):
    m = z.shape[0]
    nrb = nlb + ncb
    fwd_rb = _gla_rb(d, nlb, ncb)
    rb = lambda s: fwd_rb(nrb - 1 - s)
    last = prev is not None

    def body(*refs):
        if last:
            q_ref, f_ref, v_ref, lbl_ref, do_ref, st_ref, pq_ref, pv_ref, o0_ref, o1_ref, o2_ref, sum_ref, dS = refs
        else:
            q_ref, f_ref, v_ref, lbl_ref, do_ref, st_ref, o0_ref, o1_ref, o2_ref, sum_ref, dS = refs
        s = pl.program_id(0)
        is_lat = rb(s) < nlb

        @pl.when(s == 0)
        def _():
            dS[...] = jnp.zeros_like(dS)
            sum_ref[...] = jnp.zeros_like(sum_ref)

        lb = _lower_bound(lbl_ref, d)
        mb = _gla_mask(d)
        zq = q_ref[...]
        sq, sf, f, qd, ki, ke, eg, eig, eeg, decs = _gla_features(zq, f_ref[...], lb, d)
        qdb, kib, keb, vb = qd.astype(BF), ki.astype(BF), ke.astype(BF), v_ref[...].astype(BF)
        dob = jnp.where(is_lat, do_ref[...], 0.0).astype(BF)
        order = range(NCK) if d else range(NCK - 1, -1, -1)
        dqd_h, dki_h, dke_h, dv_h, ddec_h = [], [], [], [], []
        for h in range(H):
            hs = slice(h * DH, (h + 1) * DH)
            dqd, dki, inc, dvi = {}, {}, {}, {}
            for c in range(NCK):
                rs = _chunk_rows(c)
                a = jnp.where(mb, _dot_nt(qdb[rs, hs], kib[rs, hs]), 0.0)
                da = jnp.where(mb, _dot_nt(dob[rs, hs], vb[rs, hs]), 0.0)
                dqd[c] = _dot(da, kib[rs, hs]) + _dot(dob[rs, hs], st_ref[c, h])
                dki[c] = _dot_tn(da, qdb[rs, hs])
                inc[c] = _dot_tn(dob[rs, hs], qdb[rs, hs])
                dvi[c] = _dot_tn(a, dob[rs, hs])
            dst = dS[h]
            dke, dv, ddec = {}, {}, {}
            for c in order:
                rs = _chunk_rows(c)
                dv[c] = dvi[c] + _dot_nt(keb[rs, hs], dst)
                dke[c] = _dot(vb[rs, hs], dst)
                ddec[c] = _colsum(st_ref[c, h] * dst)
                dst = inc[c] + dst * decs[c][:, hs]
            dS[h] = dst
            cat = lambda t: jnp.concatenate([t[c] for c in range(NCK)], axis=0)
            dqd_h.append(cat(dqd))
            dki_h.append(cat(dki))
            dke_h.append(cat(dke))
            dv_h.append(cat(dv))
            ddec_h.append([ddec[c] for c in range(NCK)])
        lanes = lambda parts: jnp.concatenate(parts, axis=1)
        dqd, dki, dke, dv = lanes(dqd_h), lanes(dki_h), lanes(dke_h), lanes(dv_h)
        dq = dqd * eg
        dk = dki * eig + dke * eeg
        dke_ke = dke * ke
        dg = dqd * qd - dki * ki - dke_ke
        dgl = [_colsum(dke_ke[_chunk_rows(c), :]) + lanes([ddec_h[h][c] for h in range(H)]) * decs[c]
               for c in range(NCK)]
        dglb = jnp.concatenate([jnp.broadcast_to(t, (CH, D)) for t in dgl], axis=0)
        df = (_chunk_cumsum(dg, 1 - d) + dglb) / f - dk
        dzf = df * (1.0 - lb) * (sf * (1.0 - sf))
        sum_ref[0:1, :] += _colsum(dzf)
        sum_ref[1:2, :] += _colsum(df * (1.0 - sf))
        if last:
            dz0 = (dq + pq_ref[...]) * (Q_SCALE * (sq * (1.0 + zq * (1.0 - sq))))
            dz3 = dv + pv_ref[...]
            sum_ref[2:3, :] += _colsum(dz0)
            sum_ref[3:4, :] += _colsum(dz3)
            o0_ref[...] = dz0.astype(BF)
            o1_ref[...] = dz3.astype(BF)
        else:
            o0_ref[...] = dq
            o1_ref[...] = dv
        o2_ref[...] = dzf.astype(BF)

        @pl.when(s == nrb - 1)
        def _():
            sum_ref[1:2, :] = sum_ref[1:2, :] * (lb * (1.0 - lb))

    def zspec(cb):
        return pl.BlockSpec((RB, D), lambda s: (rb(s), cb))

    rowspec = pl.BlockSpec((RB, D), lambda s: (rb(s), 0))
    in_specs = [zspec(0), zspec(1 + d), zspec(3), pl.BlockSpec((2, 2, D), lambda s: (0, 0, 0)),
                pl.BlockSpec((RB, D), lambda s: (jnp.minimum(rb(s), nlb - 1), 0)),
                pl.BlockSpec((NCK, H, DH, DH), lambda s: (rb(s), 0, 0, 0))]
    args = [z, z, z, lbl, do_lat, states]
    if last:
        in_specs += [rowspec, rowspec]
        args += list(prev)
    dt01 = BF if last else F32
    return pl.pallas_call(
        body, grid=(nrb,), in_specs=in_specs,
        out_specs=(rowspec, rowspec, rowspec, pl.BlockSpec((8, D), lambda s: (0, 0))),
        out_shape=(jax.ShapeDtypeStruct((m, D), dt01), jax.ShapeDtypeStruct((m, D), dt01),
                   jax.ShapeDtypeStruct((m, D), BF), jax.ShapeDtypeStruct((8, D), F32)),
        scratch_shapes=[pltpu.VMEM((H, DH, DH), F32)],
        name=f"gla_bwd{d}", compiler_params=_cp(("arbitrary",)))(*args)


def _shift_rows(v, k, r):
    row = lax.broadcasted_iota(jnp.int32, v.shape, 0)
    rolled = pltpu.roll(v, k % r, 0)
    return jnp.where((row >= k) & (row < r + k), rolled, 0.0)


def _conv_fwd(xb, cw_ref, cb_ref, r):
    return (cb_ref[...] + _shift_rows(xb, 1, r) * cw_ref[0:1, :] + xb * cw_ref[1:2, :]
            + _shift_rows(xb, -1, r) * cw_ref[2:3, :] + _shift_rows(xb, -2, r) * cw_ref[3:4, :])


def _softplus_neg(lam):
    y = jnp.exp(-jnp.abs(lam))
    u = 1.0 + y
    tiny = u == 1.0
    l1p = jnp.where(tiny, y, jnp.log(u) * (y / jnp.where(tiny, 1.0, u - 1.0)))
    return jnp.maximum(-lam, 0.0) + l1p


def _gates(xc, wr_ref, br, wi_ref, bi, sp):
    xcb = xc.astype(BF)
    rs, is_ = [], []
    for g in range(H):
        gs = slice(g * DH, (g + 1) * DH)
        rs.append(jnp.dot(xcb[:, gs], wr_ref[g].astype(BF), preferred_element_type=F32))
        is_.append(jnp.dot(xcb[:, gs], wi_ref[g].astype(BF), preferred_element_type=F32))
    r = _sig(jnp.concatenate(rs, axis=1) + br)
    i = _sig(jnp.concatenate(is_, axis=1) + bi)
    log_a = (-RG_C * r) * sp
    a = jnp.exp(log_a)
    t = jnp.tanh(log_a)
    om = (-2.0 * t) / (1.0 - t)
    return r, i, a, om


def _scan_rows(d, nrows, a_s, b_s, h_s, h0):
    nsl = nrows // 8

    def slab(j, h):
        jj = (nsl - 1 - j) if d else j
        r0 = pl.multiple_of(jj * 8, 8)
        for t in (range(7, -1, -1) if d else range(8)):
            h = a_s[pl.ds(r0 + t, 1), :] * h + b_s[pl.ds(r0 + t, 1), :]
            h_s[pl.ds(r0 + t, 1), :] = h
        return h

    return lax.fori_loop(0, nsl, slab, h0)


def _col_of(d, ncols):
    if d:
        return lambda s: ncols - jnp.maximum(s, 1)
    return lambda s: jnp.maximum(s, 1) - 1


def _rglru_fwd(z, cw, cb, wr, br, wi, bi, lam, d, t_lat, t_ctx):
    m = z.shape[0]
    rows = t_lat // GRID_W
    zcol = z.reshape(m // GRID_W, GRID_W * IN_COLS)
    col = _col_of(d, GRID_W)
    cblk = t_lat // t_ctx
    rmax = max(rows, t_ctx)

    def body(zc_ref, zx_ref, cw_ref, cb_ref, wr_ref, br_ref, wi_ref, bi_ref, lam_ref,
             hx_ref, hpx_ref, hpc_ref, a_s, b_s, h_s, hcar):
        s = pl.program_id(0)
        sp = _softplus_neg(lam_ref[d:d + 1, :])
        br_ = br_ref[d:d + 1, :]
        bi_ = bi_ref[d:d + 1, :]

        def run(x_ref, r, h0, h_out, hp_out):
            xb = x_ref[...]
            xc = _conv_fwd(xb, cw_ref, cb_ref, r)
            _, i, a, om = _gates(xc, wr_ref, br_, wi_ref, bi_, sp)
            a_s[0:r, :] = a
            b_s[0:r, :] = jnp.sqrt(om) * (i * xc)
            hl = _scan_rows(d, r, a_s, b_s, h_s, h0)
            hs = h_s[0:r, :]
            row = lax.broadcasted_iota(jnp.int32, (r, D), 0)
            if d:
                hp = jnp.where(row == r - 1, h0, pltpu.roll(hs, r - 1, 0))
            else:
                hp = jnp.where(row == 0, h0, pltpu.roll(hs, 1, 0))
            if h_out is not None:
                h_out[...] = hs
            hp_out[...] = hp
            hcar[0:1, :] = hl

        @pl.when(s == 0)
        def _():
            run(zc_ref, t_ctx, jnp.zeros((1, D), F32), None, hpc_ref)

        @pl.when(s > 0)
        def _():
            run(zx_ref, rows, hcar[0:1, :], hx_ref, hpx_ref)

    full = lambda shp: pl.BlockSpec(shp, lambda s: (0,) * len(shp))
    colspec = pl.BlockSpec((rows, D), lambda s: (0, col(s)))
    outs = pl.pallas_call(
        body, grid=(GRID_W + 1,),
        in_specs=[pl.BlockSpec((t_ctx, D), lambda s: (cblk, 5)),
                  pl.BlockSpec((rows, D), lambda s: (0, col(s) * 9 + 5)),
                  full((4, D)), full((1, D)),
                  pl.BlockSpec((None, H, DH, DH), lambda s: (d, 0, 0, 0)), full((2, D)),
                  pl.BlockSpec((None, H, DH, DH), lambda s: (d, 0, 0, 0)), full((2, D)), full((2, D))],
        out_specs=(colspec, colspec, full((t_ctx, D))),
        out_shape=(jax.ShapeDtypeStruct((rows, GRID_W * D), F32), jax.ShapeDtypeStruct((rows, GRID_W * D), F32),
                   jax.ShapeDtypeStruct((t_ctx, D), F32)),
        scratch_shapes=[pltpu.VMEM((rmax, D), F32), pltpu.VMEM((rmax, D), F32), pltpu.VMEM((rmax, D), F32),
                        pltpu.VMEM((8, D), F32)],
        name=f"rglru_fwd{d}", compiler_params=_cp(("arbitrary",)))(z, zcol, cw, cb, wr, br, wi, bi, lam)
    return outs[0].reshape(t_lat, D), outs[1].reshape(t_lat, D), outs[2]


def _rglru_bwd(z, cw, cb, wr, br, wi, bi, lam, dh_lat, hp_lat, hp_ctx, d, t_lat, t_ctx, prev=None):
    m = z.shape[0]
    rows = t_lat // GRID_W
    zcol = z.reshape(m // GRID_W, GRID_W * IN_COLS)
    fcol = _col_of(d, GRID_W)
    col = lambda s: fcol(GRID_W - jnp.minimum(s, GRID_W - 1))
    cblk = t_lat // t_ctx
    rmax = max(rows, t_ctx)
    last = prev is not None
    dview = lambda v: v.reshape(rows, GRID_W * D)

    def body(*refs):
        (zc_ref, zx_ref, cw_ref, cb_ref, wr_ref, br_ref, wi_ref, bi_ref, lam_ref, dh_ref, hpx_ref, hpc_ref) = refs[:12]
        k = 12
        if last:
            pdx_ref, pdc_ref = refs[12:14]
            k = 14
        ox_ref, oc_ref, dwr_ref, dwi_ref, sum_ref, a_s, b_s, g_s, gcar = refs[k:]
        s = pl.program_id(0)
        lam_d = lam_ref[d:d + 1, :]
        sp = _softplus_neg(lam_d)
        br_ = br_ref[d:d + 1, :]
        bi_ = bi_ref[d:d + 1, :]

        @pl.when(s == 0)
        def _():
            gcar[...] = jnp.zeros_like(gcar)
            dwr_ref[...] = jnp.zeros_like(dwr_ref)
            dwi_ref[...] = jnp.zeros_like(dwi_ref)
            sum_ref[...] = jnp.zeros_like(sum_ref)

        def run(x_ref, r, dout, hp_ref, prev_ref, o_ref):
            xb = x_ref[...]
            xc = _conv_fwd(xb, cw_ref, cb_ref, r)
            rr, i, a, om = _gates(xc, wr_ref, br_, wi_ref, bi_, sp)
            mult = jnp.sqrt(om)
            a_s[0:r, :] = a
            b_s[0:r, :] = a * dout
            c0 = gcar[0:1, :]
            cl = _scan_rows(1 - d, r, a_s, b_s, g_s, c0)
            cs = g_s[0:r, :]
            row = lax.broadcasted_iota(jnp.int32, (r, D), 0)
            if d:
                cn = jnp.where(row == 0, c0, pltpu.roll(cs, 1, 0))
            else:
                cn = jnp.where(row == r - 1, c0, pltpu.roll(cs, r - 1, 0))
            gcar[0:1, :] = cl
            g = dout + cn
            da = g * hp_ref[...]
            ixc = i * xc
            dmult = g * ixc
            dixc = g * mult
            di = dixc * xc
            dxc = dixc * i
            dlog_a = da * a - dmult * ((1.0 - om) / mult)
            dr = dlog_a * (-RG_C * sp)
            sum_ref[2:3, :] += _colsum(dlog_a * rr)
            drp = dr * rr * (1.0 - rr)
            dip = di * i * (1.0 - i)
            sum_ref[0:1, :] += _colsum(drp)
            sum_ref[1:2, :] += _colsum(dip)
            xcb = xc.astype(BF)
            drb = drp.astype(BF)
            dib = dip.astype(BF)
            parts = []
            for gi in range(H):
                gs = slice(gi * DH, (gi + 1) * DH)
                parts.append(_dot_nt(drb[:, gs], wr_ref[gi]) + _dot_nt(dib[:, gs], wi_ref[gi]))
                dwr_ref[gi] += _dot_tn(xcb[:, gs], drb[:, gs])
                dwi_ref[gi] += _dot_tn(xcb[:, gs], dib[:, gs])
            dxc = dxc + jnp.concatenate(parts, axis=1)
            if last:
                dxc = dxc + prev_ref[...]
                dxb = (_shift_rows(dxc, -1, r) * cw_ref[0:1, :] + dxc * cw_ref[1:2, :]
                       + _shift_rows(dxc, 1, r) * cw_ref[2:3, :] + _shift_rows(dxc, 2, r) * cw_ref[3:4, :])
                sum_ref[3:4, :] += _colsum(dxc)
                sum_ref[4:5, :] += _colsum(dxb)
                sum_ref[8:9, :] += _colsum(dxc * _shift_rows(xb, 1, r))
                sum_ref[9:10, :] += _colsum(dxc * xb)
                sum_ref[10:11, :] += _colsum(dxc * _shift_rows(xb, -1, r))
                sum_ref[11:12, :] += _colsum(dxc * _shift_rows(xb, -2, r))
                o_ref[...] = dxb.astype(BF)
            else:
                o_ref[...] = dxc

        @pl.when(s < GRID_W)
        def _():
            run(zx_ref, rows, dh_ref[...], hpx_ref, pdx_ref if last else None, ox_ref)

        @pl.when(s == GRID_W)
        def _():
            run(zc_ref, t_ctx, jnp.zeros((t_ctx, D), F32), hpc_ref, pdc_ref if last else None, oc_ref)
            sum_ref[2:3, :] = sum_ref[2:3, :] * (RG_C * _sig(-lam_d))

    full = lambda shp: pl.BlockSpec(shp, lambda s: (0,) * len(shp))
    colspec = pl.BlockSpec((rows, D), lambda s: (0, col(s)))
    in_specs = [pl.BlockSpec((t_ctx, D), lambda s: (cblk, 5)),
                pl.BlockSpec((rows, D), lambda s: (0, col(s) * 9 + 5)),
                full((4, D)), full((1, D)),
                pl.BlockSpec((None, H, DH, DH), lambda s: (d, 0, 0, 0)), full((2, D)),
                pl.BlockSpec((None, H, DH, DH), lambda s: (d, 0, 0, 0)), full((2, D)), full((2, D)),
                colspec, colspec, full((t_ctx, D))]
    args = [z, zcol, cw, cb, wr, br, wi, bi, lam, dview(dh_lat), dview(hp_lat), hp_ctx]
    if last:
        in_specs += [colspec, full((t_ctx, D))]
        args += [dview(prev[0]), prev[1]]
    odt = BF if last else F32
    outs = pl.pallas_call(
        body, grid=(GRID_W + 1,), in_specs=in_specs,
        out_specs=(colspec, full((t_ctx, D)), full((H, DH, DH)), full((H, DH, DH)), full((16, D))),
        out_shape=(jax.ShapeDtypeStruct((rows, GRID_W * D), odt), jax.ShapeDtypeStruct((t_ctx, D), odt),
                   jax.ShapeDtypeStruct((H, DH, DH), F32), jax.ShapeDtypeStruct((H, DH, DH), F32),
                   jax.ShapeDtypeStruct((16, D), F32)),
        scratch_shapes=[pltpu.VMEM((rmax, D), F32), pltpu.VMEM((rmax, D), F32), pltpu.VMEM((rmax, D), F32),
                        pltpu.VMEM((8, D), F32)],
        name=f"rglru_bwd{d}", compiler_params=_cp(("arbitrary",)))(*args)
    return (outs[0].reshape(t_lat, D), outs[1]) + tuple(outs[2:])


def _merge(o_f, o_b, h_f, h_b, z, x_all, tgt, mod, norm_g, ln_g, ln_b, p_a, p_b, w_out, t_lat):
    tm = 128
    nt = t_lat // tm

    def body(of_ref, ob_ref, hf_ref, hb_ref, z4_ref, z6_ref, z7_ref, z8_ref, x_ref, t_ref, mod_ref, ng_ref,
             lg_ref, lb_ref, pa_ref, pb_ref, wo_ref,
             do_ref, dh_ref, dz4_ref, dz6_ref, dz7_ref, dz8_ref, gx_ref,
             y_ref, dout_ref, oa_ref, dpa_ref, obv_ref, dpb_ref, acc_ref):
        i = pl.program_id(0)

        @pl.when(i == 0)
        def _():
            acc_ref[...] = jnp.zeros_like(acc_ref)

        def per_head(v):
            return jnp.concatenate(
                [jnp.broadcast_to(jnp.mean(v[:, h * DH:(h + 1) * DH], axis=-1, keepdims=True), (tm, DH))
                 for h in range(H)], axis=1)

        gt = mod_ref[0:1, 2 * D:3 * D]
        gfull = jnp.concatenate([ng_ref[...]] * H, axis=1)
        o = of_ref[...] + ob_ref[...]
        rinv = lax.rsqrt(per_head(o * o) + RMS_EPS)
        n = o * rinv
        na = n * gfull
        z4 = z4_ref[...]
        s4 = _sig(z4)
        silu4 = z4 * s4
        oa = na * silu4
        z6 = z6_ref[...]
        s6 = _sig(z6)
        silu6 = z6 * s6
        hsum = hf_ref[...] + hb_ref[...]
        obv = hsum * silu6
        pa = _dot(oa, pa_ref[...])
        pb = _dot(obv, pb_ref[...])
        s7 = _sig(z7_ref[...])
        s8 = _sig(z8_ref[...])
        y = s7 * pa + s8 * pb
        out = _dot(y, wo_ref[...])
        pre = ALPHA * x_ref[...] + gt * out
        mu = jnp.mean(pre, axis=-1, keepdims=True)
        xc = pre - mu
        rstd = lax.rsqrt(jnp.mean(xc * xc, axis=-1, keepdims=True) + LN_EPS)
        xhat = xc * rstd
        lg = lg_ref[...]
        diff = xhat * lg + lb_ref[...] - t_ref[...]
        acc_ref[8:9, :] += _colsum(diff * diff) * (0.5 / D)
        dxn = diff * (1.0 / D)
        acc_ref[1:2, :] += _colsum(dxn * xhat)
        acc_ref[2:3, :] += _colsum(dxn)
        dxhat = dxn * lg
        dpre = rstd * (dxhat - jnp.mean(dxhat, axis=-1, keepdims=True)
                       - xhat * jnp.mean(dxhat * xhat, axis=-1, keepdims=True))
        gx_ref[...] = ALPHA * dpre
        acc_ref[0:1, :] += _colsum(dpre * out)
        dout = dpre * gt
        dy = _dot_nt(dout, wo_ref[...])
        dpa = dy * s7
        dpb = dy * s8
        dz7 = dy * pa * (s7 * (1.0 - s7))
        dz8 = dy * pb * (s8 * (1.0 - s8))
        doa = _dot_nt(dpa, pa_ref[...])
        dob = _dot_nt(dpb, pb_ref[...])
        dh_ref[...] = dob * silu6
        dz6 = dob * hsum * (s6 * (1.0 + z6 * (1.0 - s6)))
        dna = doa * silu4
        dz4 = doa * na * (s4 * (1.0 + z4 * (1.0 - s4)))
        dng = _colsum(dna * n)
        acc_ref[7:8, 0:DH] += sum(dng[:, h * DH:(h + 1) * DH] for h in range(H))
        dn = dna * gfull
        do_ref[...] = rinv * (dn - n * per_head(dn * n))
        acc_ref[3:4, :] += _colsum(dz4)
        acc_ref[4:5, :] += _colsum(dz6)
        acc_ref[5:6, :] += _colsum(dz7)
        acc_ref[6:7, :] += _colsum(dz8)
        dz4_ref[...] = dz4.astype(BF)
        dz6_ref[...] = dz6.astype(BF)
        dz7_ref[...] = dz7.astype(BF)
        dz8_ref[...] = dz8.astype(BF)
        y_ref[...] = y.astype(BF)
        dout_ref[...] = dout.astype(BF)
        oa_ref[...] = oa.astype(BF)
        dpa_ref[...] = dpa.astype(BF)
        obv_ref[...] = obv.astype(BF)
        dpb_ref[...] = dpb.astype(BF)

        @pl.when(i == nt - 1)
        def _():
            acc_ref[9:10, :] = jnp.broadcast_to(jnp.sum(acc_ref[8:9, :], axis=-1, keepdims=True), (1, D))

    row = pl.BlockSpec((tm, D), lambda i: (i, 0))
    zs = lambda cb: pl.BlockSpec((tm, D), lambda i: (i, cb))
    full = lambda shp: pl.BlockSpec(shp, lambda i: (0,) * len(shp))
    f32o = jax.ShapeDtypeStruct((t_lat, D), F32)
    bfo = jax.ShapeDtypeStruct((t_lat, D), BF)
    return pl.pallas_call(
        body, grid=(nt,),
        in_specs=[row, row, row, row, zs(4), zs(6), zs(7), zs(8), row, row, full((16, 3 * D)), full((1, DH)),
                  full((1, D)), full((1, D)), full((D, D)), full((D, D)), full((D, D))],
        out_specs=(row,) * 13 + (full((16, D)),),
        out_shape=(f32o, f32o, bfo, bfo, bfo, bfo, f32o, bfo, bfo, bfo, bfo, bfo, bfo,
                   jax.ShapeDtypeStruct((16, D), F32)),
        name="merge", compiler_params=_cp(("arbitrary",)))(
            o_f, o_b, h_f, h_b, z, z, z, z, x_all, tgt, mod, norm_g, ln_g, ln_b, p_a, p_b, w_out)


def _wgrad(a, b, name):
    tm = 512

    def body(a_ref, b_ref, o_ref):
        @pl.when(pl.program_id(0) == 0)
        def _():
            o_ref[...] = jnp.zeros_like(o_ref)
        o_ref[...] += _dot_tn(a_ref[...], b_ref[...])

    row = pl.BlockSpec((tm, D), lambda i: (i, 0))
    return pl.pallas_call(body, grid=(a.shape[0] // tm,), in_specs=[row, row],
                          out_specs=pl.BlockSpec((D, D), lambda i: (0, 0)),
                          out_shape=jax.ShapeDtypeStruct((D, D), F32), name=name,
                          compiler_params=_cp(("arbitrary",)))(a, b)


def _wgrad_in(x_all, mod, dz, n_lat_tiles):
    m = x_all.shape[0]
    tm = 256

    def body(x_ref, mod_ref, dz_ref, o_ref):
        i = pl.program_id(1)

        @pl.when(i == 0)
        def _():
            o_ref[...] = jnp.zeros_like(o_ref)
        u, _ = _modulate(x_ref, mod_ref, i < n_lat_tiles)
        o_ref[0] += _dot_tn(u, dz_ref[...])

    return pl.pallas_call(
        body, grid=(NSH, m // tm),
        in_specs=[pl.BlockSpec((tm, D), lambda n, i: (i, 0)),
                  pl.BlockSpec((16, 3 * D), lambda n, i: (0, 0)),
                  pl.BlockSpec((tm, SHC), lambda n, i: (i, n))],
        out_specs=pl.BlockSpec((1, D, SHC), lambda n, i: (n, 0, 0)),
        out_shape=jax.ShapeDtypeStruct((NSH, D, SHC), F32), name="wgrad_in",
        compiler_params=_cp(("arbitrary", "arbitrary")))(x_all, mod, dz)


def _du(dz, w_in_g, x_all, mod, gxres, n_lat_tiles):
    m = x_all.shape[0]
    tm = 256
    nt = m // tm
    nct = nt - n_lat_tiles
    rblk = lambda i: jnp.where(i < nct, n_lat_tiles + i, i - nct)
    lblk = lambda i: jnp.maximum(i - nct, 0)

    def body(dz_ref, w_ref, x_ref, mod_ref, gr_ref, gx_ref, dm_ref, acc):
        i = pl.program_id(0)
        n = pl.program_id(1)
        is_lat = i >= nct

        @pl.when((i == 0) & (n == 0))
        def _():
            dm_ref[...] = jnp.zeros_like(dm_ref)

        @pl.when(n == 0)
        def _():
            acc[...] = jnp.zeros_like(acc)

        acc[...] += _dot_nt(dz_ref[...], w_ref[0])

        @pl.when(n == NSH - 1)
        def _():
            du = acc[...]
            sc = jnp.where(is_lat, mod_ref[0:1, D:2 * D], mod_ref[1:2, D:2 * D])
            dsh = _colsum(du)
            dsc = _colsum(du * x_ref[...])

            @pl.when(is_lat)
            def _():
                gx_ref[...] = du * (1.0 + sc) + gr_ref[...]
                dm_ref[0:1, 0:D] += dsh
                dm_ref[0:1, D:2 * D] += dsc

            @pl.when(jnp.logical_not(is_lat))
            def _():
                dm_ref[1:2, 0:D] += dsh
                dm_ref[1:2, D:2 * D] += dsc

    return pl.pallas_call(
        body, grid=(nt, NSH),
        in_specs=[pl.BlockSpec((tm, SHC), lambda i, n: (rblk(i), n)),
                  pl.BlockSpec((1, D, SHC), lambda i, n: (n, 0, 0)),
                  pl.BlockSpec((tm, D), lambda i, n: (rblk(i), 0)),
                  pl.BlockSpec((16, 3 * D), lambda i, n: (0, 0)),
                  pl.BlockSpec((tm, D), lambda i, n: (lblk(i), 0))],
        out_specs=(pl.BlockSpec((tm, D), lambda i, n: (lblk(i), 0)),
                   pl.BlockSpec((8, 2 * D), lambda i, n: (0, 0))),
        out_shape=(jax.ShapeDtypeStruct((n_lat_tiles * tm, D), F32), jax.ShapeDtypeStruct((8, 2 * D), F32)),
        scratch_shapes=[pltpu.VMEM((tm, D), F32)],
        name="du", compiler_params=_cp(("arbitrary", "arbitrary")))(dz, w_in_g, x_all, mod, gxres)


def _row_tile(rows, cols):
    t = 8
    while t * 2 * cols * 4 <= (1 << 20) and rows % (t * 2) == 0:
        t *= 2
    return t


def _addn(arrs, name):
    shape = arrs[0].shape
    cols = shape[-1]
    flat = [a.reshape(-1, cols) for a in arrs]
    rows = flat[0].shape[0]
    tr = _row_tile(rows, cols)

    def body(*refs):
        acc = refs[0][...]
        for r in refs[1:-1]:
            acc = acc + r[...]
        refs[-1][...] = acc

    spec = pl.BlockSpec((tr, cols), lambda i: (i, 0))
    out = pl.pallas_call(body, grid=(rows // tr,), in_specs=[spec] * len(flat), out_specs=spec,
                         out_shape=jax.ShapeDtypeStruct((rows, cols), F32), name=name,
                         compiler_params=_cp(("arbitrary",)))(*flat)
    return out.reshape(shape)


def _adamw(w, g, m, v, name):
    rows, cols = w.shape
    tr = _row_tile(rows, cols)

    def body(w_ref, g_ref, m_ref, v_ref, d_ref, nm_ref, nv_ref):
        gg = g_ref[...]
        m2 = ADAM_B1 * m_ref[...] + (1.0 - ADAM_B1) * gg
        v2 = ADAM_B2 * v_ref[...] + (1.0 - ADAM_B2) * (gg * gg)
        m_hat = m2 / (1.0 - ADAM_B1 ** ADAM_STEP)
        v_hat = v2 / (1.0 - ADAM_B2 ** ADAM_STEP)
        d_ref[...] = -ADAM_LR * (m_hat / (jnp.sqrt(v_hat) + ADAM_EPS) + ADAM_WD * w_ref[...])
        nm_ref[...] = m2
        nv_ref[...] = v2

    spec = pl.BlockSpec((tr, cols), lambda i: (i, 0))
    o = jax.ShapeDtypeStruct((rows, cols), F32)
    return pl.pallas_call(body, grid=(rows // tr,), in_specs=[spec] * 4, out_specs=(spec,) * 3,
                          out_shape=(o, o, o), name=name, compiler_params=_cp(("arbitrary",)))(w, g, m, v)


_ANY = pl.BlockSpec(memory_space=pl.ANY)


def _place():
    return lax.axis_index("x"), lax.axis_index("y"), lax.axis_index("c")


def _allgather_chips(shards):
    n = len(shards)

    def body(*refs):
        ins, outs = refs[:n], refs[n:2 * n]
        send, recv, lsem = refs[2 * n:]
        x, y, c = _place()
        me = 2 * x + y
        peers = ((1 - x, y, c), (x, 1 - y, c), (1 - x, 1 - y, c))
        copies = []
        for j in range(n):
            cp = pltpu.make_async_copy(ins[j], outs[j].at[me], lsem.at[j])
            cp.start()
            copies.append(cp)
            for k, p in enumerate(peers):
                cp = pltpu.make_async_remote_copy(src_ref=ins[j], dst_ref=outs[j].at[me], send_sem=send.at[3 * j + k],
                                                  recv_sem=recv.at[3 * j + k], device_id=p, device_id_type=MESH)
                cp.start()
                copies.append(cp)
        for cp in copies:
            cp.wait()

    return pl.pallas_call(
        body, in_specs=[_ANY] * n, out_specs=[_ANY] * n,
        out_shape=[jax.ShapeDtypeStruct((NSH,) + s.shape, s.dtype) for s in shards],
        scratch_shapes=[pltpu.SemaphoreType.DMA((3 * n,)), pltpu.SemaphoreType.DMA((3 * n,)),
                        pltpu.SemaphoreType.DMA((n,))],
        name="allgather_weights")(*shards)


def _rs_sibling(grads):
    n = len(grads)

    def body(*refs):
        ins, got = refs[:n], refs[n:2 * n]
        send, recv = refs[2 * n:]
        x, y, c = _place()
        copies = []
        for j in range(n):
            hr = ins[j].shape[1] // 2
            for s in range(NSH):
                give = ins[j].at[s, pl.ds(pl.multiple_of((1 - c) * hr, 8), hr), :]
                cp = pltpu.make_async_remote_copy(src_ref=give, dst_ref=got[j].at[s], send_sem=send.at[NSH * j + s],
                                                  recv_sem=recv.at[NSH * j + s], device_id=(x, y, 1 - c),
                                                  device_id_type=MESH)
                cp.start()
                copies.append(cp)
        for cp in copies:
            cp.wait()

    half = [jax.ShapeDtypeStruct((NSH, g.shape[1] // 2, g.shape[2]), F32) for g in grads]
    return pl.pallas_call(
        body, in_specs=[_ANY] * n, out_specs=[_ANY] * n, out_shape=half,
        scratch_shapes=[pltpu.SemaphoreType.DMA((NSH * n,)), pltpu.SemaphoreType.DMA((NSH * n,))],
        name="rs_sibling")(*grads)


def _core_vec():
    return lax.axis_index("c").astype(jnp.int32).reshape(1)


def _rs_add1(g, got, name):
    _, r, cols = g.shape
    hr = r // 2
    tr = _row_tile(hr, cols)
    nb = hr // tr

    def body(c_ref, g_ref, got_ref, o_ref):
        o_ref[...] = (g_ref[...] + got_ref[...]).astype(BF)

    spec = pl.BlockSpec((1, tr, cols), lambda s, i, c_ref: (s, i, 0))
    return pl.pallas_call(
        body, grid_spec=pltpu.PrefetchScalarGridSpec(
            num_scalar_prefetch=1, grid=(NSH, nb),
            in_specs=[pl.BlockSpec((1, tr, cols), lambda s, i, c_ref: (s, c_ref[0] * nb + i, 0)), spec],
            out_specs=spec),
        out_shape=jax.ShapeDtypeStruct((NSH, hr, cols), BF), name=name,
        compiler_params=_cp(("arbitrary", "arbitrary")))(_core_vec(), g, got)


def _rs_add2(sums, got, name):
    _, hr, cols = sums.shape
    tr = _row_tile(hr, cols)
    nb = hr // tr
    place = jnp.stack([2 * lax.axis_index("x") + lax.axis_index("y"), lax.axis_index("c")]).astype(jnp.int32)

    def body(p_ref, s_ref, got_ref, o_ref):
        f = lambda v: v.astype(F32)
        o_ref[...] = f(s_ref[0]) + f(got_ref[0]) + f(got_ref[1]) + f(got_ref[2])

    return pl.pallas_call(
        body, grid_spec=pltpu.PrefetchScalarGridSpec(
            num_scalar_prefetch=1, grid=(nb,),
            in_specs=[pl.BlockSpec((1, tr, cols), lambda i, p_ref: (p_ref[0], i, 0)),
                      pl.BlockSpec((3, tr, cols), lambda i, p_ref: (0, i, 0))],
            out_specs=pl.BlockSpec((tr, cols), lambda i, p_ref: (p_ref[1] * nb + i, 0))),
        out_shape=jax.ShapeDtypeStruct((2 * hr, cols), F32), name=name,
        compiler_params=_cp(("arbitrary",)))(place, sums, got)


def _rs_chips(sums):
    n = len(sums)

    def body(*refs):
        ins, got = refs[:n], refs[n:2 * n]
        send, recv = refs[2 * n:]
        x, y, c = _place()
        peers = ((1 - x, y), (x, 1 - y), (1 - x, 1 - y))
        copies = []
        for j in range(n):
            for k, (px, py) in enumerate(peers):
                cp = pltpu.make_async_remote_copy(src_ref=ins[j].at[2 * px + py], dst_ref=got[j].at[k],
                                                  send_sem=send.at[3 * j + k], recv_sem=recv.at[3 * j + k],
                                                  device_id=(px, py, c), device_id_type=MESH)
                cp.start()
                copies.append(cp)
        for cp in copies:
            cp.wait()

    three = [jax.ShapeDtypeStruct((3,) + g.shape[1:], g.dtype) for g in sums]
    return pl.pallas_call(
        body, in_specs=[_ANY] * n, out_specs=[_ANY] * n, out_shape=three,
        scratch_shapes=[pltpu.SemaphoreType.DMA((3 * n,)), pltpu.SemaphoreType.DMA((3 * n,))],
        name="rs_chips")(*sums)


def _ag_sibling(fulls):
    n = len(fulls)
    nck = 4

    def body(*refs):
        outs = refs[n:2 * n]
        send, recv = refs[2 * n:]
        x, y, c = _place()
        copies = []
        for j in range(n):
            qr = outs[j].shape[0] // (2 * nck)
            for k in range(nck):
                rows = outs[j].at[pl.ds(pl.multiple_of((c * nck + k) * qr, 8), qr), :]
                cp = pltpu.make_async_remote_copy(src_ref=rows, dst_ref=rows, send_sem=send.at[nck * j + k],
                                                  recv_sem=recv.at[nck * j + k], device_id=(x, y, 1 - c),
                                                  device_id_type=MESH)
                cp.start()
                copies.append(cp)
        for cp in copies:
            cp.wait()

    return pl.pallas_call(
        body, in_specs=[_ANY] * n, out_specs=[_ANY] * n,
        out_shape=[jax.ShapeDtypeStruct(f.shape, F32) for f in fulls],
        input_output_aliases={j: j for j in range(n)},
        scratch_shapes=[pltpu.SemaphoreType.DMA((nck * n,)), pltpu.SemaphoreType.DMA((nck * n,))],
        name="ag_sibling")(*fulls)


def _allreduce_small(buf):
    rows = buf.shape[0]
    pr = rows // 8

    def body(in_ref, out_ref, stage, send1, recv1, send2, recv2):
        x, y, c = _place()
        me = 4 * x + 2 * y + c

        def peer(k):
            kx, ky, kc = (k >> 2) & 1, (k >> 1) & 1, k & 1
            return (x ^ kx, y ^ ky, c ^ kc)

        def piece(ref, idx):
            return ref.at[pl.ds(pl.multiple_of(idx * pr, 8), pr), :]

        copies = []
        for k in range(1, 8):
            px, py, pc = peer(k)
            cp = pltpu.make_async_remote_copy(src_ref=piece(in_ref, 4 * px + 2 * py + pc), dst_ref=stage.at[k],
                                              send_sem=send1.at[k], recv_sem=recv1.at[k],
                                              device_id=(px, py, pc), device_id_type=MESH)
            cp.start()
            copies.append(cp)
        for cp in copies:
            cp.wait()
        acc = piece(in_ref, me)[...]
        for k in range(1, 8):
            acc = acc + stage[k]
        piece(out_ref, me)[...] = acc
        copies = []
        for k in range(1, 8):
            cp = pltpu.make_async_remote_copy(src_ref=piece(out_ref, me), dst_ref=piece(out_ref, me),
                                              send_sem=send2.at[k], recv_sem=recv2.at[k],
                                              device_id=peer(k), device_id_type=MESH)
            cp.start()
            copies.append(cp)
        for cp in copies:
            cp.wait()

    vm = pl.BlockSpec(memory_space=pltpu.VMEM)
    return pl.pallas_call(
        body, in_specs=[vm], out_specs=vm, out_shape=jax.ShapeDtypeStruct((rows, D), F32),
        scratch_shapes=[pltpu.VMEM((8, pr, D), F32)] + [pltpu.SemaphoreType.DMA((8,))] * 4,
        name="allreduce_small", compiler_params=_cp())(buf)


def _reduce_scatter(grads):
    got = _rs_sibling(grads)
    sums = [_rs_add1(g, b, f"rs_add1_{j}") for j, (g, b) in enumerate(zip(grads, got))]
    got = _rs_chips(sums)
    fulls = [_rs_add2(a, b, f"rs_add2_{j}") for j, (a, b) in enumerate(zip(sums, got))]
    return _ag_sibling(fulls)


def _rows(a):
    flat = a.reshape(-1)
    pad = (-flat.shape[0]) % D
    if pad:
        flat = jnp.concatenate([flat, jnp.zeros((pad,), flat.dtype)])
    return flat.reshape(-1, D)


def _pad_rows(a, mult):
    pad = (-a.shape[0]) % mult
    return jnp.concatenate([a, jnp.zeros((pad, a.shape[1]), a.dtype)]) if pad else a


def _local_step(x, c, ctx, c_ctx, tgt, w_mod_g, b_mod, w_in_g, b_in, lbl, norm_g, cw, cb, wr, br, wi, bi, lam,
                p_a, p_b, w_out, ln_g, ln_b):
    t_lat, t_ctx = x.shape[0], ctx.shape[0]
    nlt = t_lat // 256
    nlb, ncb = t_lat // RB, t_ctx // RB
    cc = jnp.zeros((16, D), F32).at[0].set(c).at[1].set(c_ctx)
    mod = _mod_fwd(cc, w_mod_g, b_mod)
    x_all = jnp.concatenate([x, ctx], axis=0)
    z = _inproj(x_all, mod, w_in_g, b_in, nlt)
    o0, st0 = _gla_fwd(z, lbl, 0, nlb, ncb)
    o1, st1 = _gla_fwd(z, lbl, 1, nlb, ncb)
    h0, hp0, hpc0 = _rglru_fwd(z, cw, cb, wr, br, wi, bi, lam, 0, t_lat, t_ctx)
    h1, hp1, hpc1 = _rglru_fwd(z, cw, cb, wr, br, wi, bi, lam, 1, t_lat, t_ctx)
    (do, dh, dz4, dz6, dz7, dz8, gxres, y, dout, oa, dpa, obv, dpb, acc) = _merge(
        o0, o1, h0, h1, z, x_all, tgt, mod, norm_g, ln_g, ln_b, p_a, p_b, w_out, t_lat)
    gp_a = _wgrad(oa, dpa, "wgrad_pa")
    gp_b = _wgrad(obv, dpb, "wgrad_pb")
    gw_out = _wgrad(y, dout, "wgrad_wout")
    dxc_lat, dxc_ctx, dwr0, dwi0, sb0 = _rglru_bwd(z, cw, cb, wr, br, wi, bi, lam, dh, hp0, hpc0, 0, t_lat, t_ctx)
    dz5_lat, dz5_ctx, dwr1, dwi1, sb1 = _rglru_bwd(z, cw, cb, wr, br, wi, bi, lam, dh, hp1, hpc1, 1, t_lat, t_ctx,
                                                   prev=(dxc_lat, dxc_ctx))
    dq0, dv0, dz1, sa0 = _gla_bwd(z, lbl, do, st0, 0, nlb, ncb)
    dz0, dz3, dz2, sa1 = _gla_bwd(z, lbl, do, st1, 1, nlb, ncb, prev=(dq0, dv0))
    zc = jnp.zeros((t_ctx, D), BF)
    pad = lambda a: jnp.concatenate([a, zc], axis=0)
    dz = jnp.concatenate([dz0, dz1, dz2, dz3, pad(dz4), jnp.concatenate([dz5_lat, dz5_ctx], axis=0),
                          pad(dz6), pad(dz7), pad(dz8)], axis=1)
    gw_in = _wgrad_in(x_all, mod, dz, nlt)
    gx, dm = _du(dz, w_in_g, x_all, mod, gxres, nlt)
    dmod = jnp.zeros((16, 3 * D), F32).at[0:2, 0:2 * D].set(dm[0:2]).at[0, 2 * D:].set(acc[0])
    gw_mod, dcc = _mod_bwd(cc, dmod, w_mod_g)
    small = dict(
        c_ctx=dcc[1:2], b_mod=(dmod[0] + dmod[1]).reshape(3, D),
        b_in=jnp.stack([sa1[2], sa0[0], sa1[0], sa1[3], acc[3], sb1[4], acc[4], acc[5], acc[6]]),
        lb_logits=jnp.stack([sa0[1], sa1[1], -sa0[1], -sa1[1]]),
        norm_a_g=acc[7:8], conv_w=sb1[8:12], conv_b=sb1[3:4],
        w_r=jnp.stack([dwr0, dwr1]).reshape(-1, D), w_i=jnp.stack([dwi0, dwi1]).reshape(-1, D),
        b_r=jnp.stack([sb0[0], sb1[0]]), b_i=jnp.stack([sb0[1], sb1[1]]), lam=jnp.stack([sb0[2], sb1[2]]),
        ln_g=acc[1:2], ln_b=acc[2:3])
    big = dict(w_mod=gw_mod, w_in=gw_in, p_a=gp_a.reshape(NSH, D // NSH, D), p_b=gp_b.reshape(NSH, D // NSH, D),
               w_out=gw_out.reshape(NSH, D // NSH, D))
    return acc[9, 0], gx, big, small


_SMALL = ("c_ctx", "b_mod", "b_in", "lb_logits", "norm_a_g", "conv_w", "conv_b", "w_r", "w_i", "b_r", "b_i", "lam",
          "ln_g", "ln_b")
_BIG = ("w_mod", "w_in", "p_a", "p_b", "w_out")
_COL_SHARDED = ("lb_logits", "conv_w", "b_r", "b_i", "lam")
_WEIGHTS = ("c_ctx", "w_mod", "b_mod", "w_in", "b_in", "lb_logits", "norm_a_g", "conv_w", "conv_b", "w_r", "b_r", "w_i",
            "b_i", "lam", "p_a", "p_b", "w_out", "ln_g", "ln_b")


def kernel(x, c, ctx, c_ctx, w_mod, b_mod, w_in, b_in, lb_logits, norm_a_g, conv_w, conv_b, w_r, b_r, w_i, b_i, lam, p_a, p_b, w_out, ln_g, ln_b, loss_target, m_c_ctx, m_w_mod, m_b_mod, m_w_in, m_b_in, m_lb_logits, m_norm_a_g, m_conv_w, m_conv_b, m_w_r, m_b_r, m_w_i, m_b_i, m_lam, m_p_a, m_p_b, m_w_out, m_ln_g, m_ln_b, v_c_ctx, v_w_mod, v_b_mod, v_w_in, v_b_in, v_lb_logits, v_norm_a_g, v_conv_w, v_conv_b, v_w_r, v_b_r, v_w_i, v_b_i, v_lam, v_p_a, v_p_b, v_w_out, v_ln_g, v_ln_b):
    w = dict(c_ctx=c_ctx, w_mod=w_mod, b_mod=b_mod, w_in=w_in, b_in=b_in, lb_logits=lb_logits, norm_a_g=norm_a_g,
             conv_w=conv_w, conv_b=conv_b, w_r=w_r, b_r=b_r, w_i=w_i, b_i=b_i, lam=lam, p_a=p_a, p_b=p_b, w_out=w_out,
             ln_g=ln_g, ln_b=ln_b)
    m = dict(c_ctx=m_c_ctx, w_mod=m_w_mod, b_mod=m_b_mod, w_in=m_w_in, b_in=m_b_in, lb_logits=m_lb_logits,
             norm_a_g=m_norm_a_g, conv_w=m_conv_w, conv_b=m_conv_b, w_r=m_w_r, b_r=m_b_r, w_i=m_w_i, b_i=m_b_i,
             lam=m_lam, p_a=m_p_a, p_b=m_p_b, w_out=m_w_out, ln_g=m_ln_g, ln_b=m_ln_b)
    v = dict(c_ctx=v_c_ctx, w_mod=v_w_mod, b_mod=v_b_mod, w_in=v_w_in, b_in=v_b_in, lb_logits=v_lb_logits,
             norm_a_g=v_norm_a_g, conv_w=v_conv_w, conv_b=v_conv_b, w_r=v_w_r, b_r=v_b_r, w_i=v_w_i, b_i=v_b_i,
             lam=v_lam, p_a=v_p_a, p_b=v_p_b, w_out=v_w_out, ln_g=v_ln_g, ln_b=v_ln_b)
    shard = 2 * lax.axis_index("x") + lax.axis_index("y")
    cs = D // NSH

    narrow = _pad_rows(jnp.concatenate([lb_logits.reshape(4, cs), conv_w[0], b_r[0], b_i[0], lam[0]], axis=0), 8)
    big_shards = [w[k][0].astype(BF) for k in _BIG]
    gathered = _allgather_chips(big_shards + [narrow])
    w_mod_g, w_in_g = gathered[0], gathered[1]
    p_a_g, p_b_g, w_out_g = (g.reshape(D, D) for g in gathered[2:5])
    nar = jnp.transpose(gathered[5], (1, 0, 2)).reshape(-1, D)
    lbl, cw, br, bi, lam_f = nar[0:4].reshape(2, 2, D), nar[4:8], nar[8:10], nar[10:12], nar[12:14]

    loss, gx, big, small = _local_step(
        x[0], c[0], ctx[0], c_ctx, loss_target[0], w_mod_g, b_mod, w_in_g, b_in, lbl, norm_a_g, cw, conv_b,
        w_r[0], br, w_i[0], bi, lam_f, p_a_g, p_b_g, w_out_g, ln_g, ln_b)
    loss = lax.psum(loss, ("x", "y", "c"))

    g_big = dict(zip(_BIG, _reduce_scatter([big[k] for k in _BIG])))
    sizes = [small[k].shape[0] for k in _SMALL]
    red = _allreduce_small(_pad_rows(jnp.concatenate([_pad_rows(small[k], 8) for k in _SMALL], axis=0), 64))
    grads = {}
    off = 0
    for k, n in zip(_SMALL, sizes):
        g = red[off:off + n]
        off += n + (-n) % 8
        if k == "norm_a_g":
            g = g[:, :DH]
        if k in _COL_SHARDED:
            g = lax.dynamic_slice_in_dim(g, shard * cs, cs, axis=1)
        grads[k] = g.reshape(w[k].shape)
    for k in _BIG:
        grads[k] = g_big[k].reshape(w[k].shape)

    delta, new_m, new_v = {}, {}, {}
    for k in _BIG:
        shp = w[k].shape
        two = lambda a: a.reshape(shp[-2], shp[-1])
        d_, m_, v_ = _adamw(two(w[k]), two(grads[k]), two(m[k]), two(v[k]), f"adamw_{k}")
        delta[k], new_m[k], new_v[k] = d_.reshape(shp), m_.reshape(shp), v_.reshape(shp)
    flat = lambda t: jnp.concatenate([_pad_rows(_rows(t[k]), 8) for k in _SMALL], axis=0)
    d_, m_, v_ = _adamw(flat(w), flat(grads), flat(m), flat(v), "adamw_small")
    off = 0
    for k in _SMALL:
        n = w[k].size
        nr = -(-n // D)
        for src, dst in ((d_, delta), (m_, new_m), (v_, new_v)):
            dst[k] = src[off:off + nr].reshape(-1)[:n].reshape(w[k].shape)
        off += nr + (-nr) % 8

    return (loss, gx[None], *[grads[k] for k in _WEIGHTS], *[delta[k] for k in _WEIGHTS],
            *[new_m[k] for k in _WEIGHTS], *[new_v[k] for k in _WEIGHTS])
```

```python
import functools

import jax
import jax.numpy as jnp
from jax import lax
from jax.experimental import pallas as pl
from jax.experimental.pallas import tpu as pltpu

F32 = jnp.float32
BF = jnp.bfloat16
MESH = pl.DeviceIdType.MESH

D = 1024
H = 8
DH = 128
CH = 64
RB = 256
NCK = RB // CH
GRID_W = 64
CB = 8
RCH = 16
IN_COLS = 9 * D
NSH = 4
SHC = IN_COLS // NSH
RG_C = 8.0
ALPHA = 2.0 ** 0.25
LN_EPS = 1e-5
RMS_EPS = 1e-6
Q_SCALE = DH ** -0.5
ADAM_LR, ADAM_B1, ADAM_B2, ADAM_EPS, ADAM_WD, ADAM_STEP = 0.001, 0.9, 0.999, 1e-08, 0.01, 10
VMEM_LIMIT = 56 * 1024 * 1024


def _cp(sem=None):
    return pltpu.CompilerParams(dimension_semantics=sem, vmem_limit_bytes=VMEM_LIMIT)


def _sig(x):
    return 1.0 / (1.0 + jnp.exp(-x))


def _dot(a, b):
    return jnp.dot(a.astype(BF), b.astype(BF), preferred_element_type=F32)


def _dot_nt(a, b):
    return lax.dot_general(a.astype(BF), b.astype(BF), (((1,), (1,)), ((), ())), preferred_element_type=F32)


def _dot_tn(a, b):
    return lax.dot_general(a.astype(BF), b.astype(BF), (((0,), (0,)), ((), ())), preferred_element_type=F32)


def _dot_hi(a, b):
    return jnp.dot(a, b, preferred_element_type=F32, precision=lax.Precision.HIGHEST)


def _colsum(v):
    return jnp.sum(v, axis=0, keepdims=True)


def _mod_fwd(cc, w_mod_g, b_mod):
    def body(cc_ref, w_ref, b_ref, o_ref):
        v = cc_ref[...]
        s = v * _sig(v)
        for k in range(NSH):
            cs = slice(k * 768, (k + 1) * 768)
            o_ref[:, cs] = _dot(s, w_ref[k]) + b_ref[:, cs]
    return pl.pallas_call(body, out_shape=jax.ShapeDtypeStruct((16, 3 * D), F32), name="mod_fwd",
                          compiler_params=_cp())(cc, w_mod_g, b_mod)


def _mod_bwd(cc, dmod, w_mod_g):
    def body(cc_ref, dm_ref, w_ref, dw_ref, dcc_ref):
        v = cc_ref[...]
        sg = _sig(v)
        s = v * sg
        ds = jnp.zeros((16, D), F32)
        for k in range(NSH):
            dm = dm_ref[:, k * 768:(k + 1) * 768]
            dw_ref[k] = _dot_tn(s, dm)
            ds = ds + _dot_nt(dm, w_ref[k])
        dcc_ref[...] = ds * (sg * (1.0 + v * (1.0 - sg)))
    return pl.pallas_call(body, out_shape=(jax.ShapeDtypeStruct((NSH, D, 768), F32),
                                           jax.ShapeDtypeStruct((16, D), F32)),
                          name="mod_bwd", compiler_params=_cp())(cc, dmod, w_mod_g)


def _modulate(x_ref, mod_ref, is_lat):
    sh = jnp.where(is_lat, mod_ref[0:1, 0:D], mod_ref[1:2, 0:D])
    sc = jnp.where(is_lat, mod_ref[0:1, D:2 * D], mod_ref[1:2, D:2 * D])
    return x_ref[...] * (1.0 + sc) + sh, sc


def _inproj(x_all, mod, w_in_g, b_in, n_lat_tiles):
    m = x_all.shape[0]
    tm = 256
    nt = m // tm

    def body(x_ref, mod_ref, w_ref, b_ref, z_ref):
        i = pl.program_id(1)
        u, _ = _modulate(x_ref, mod_ref, i < n_lat_tiles)
        z_ref[...] = _dot(u, w_ref[0]) + b_ref[...]

    return pl.pallas_call(
        body, grid=(NSH, nt),
        in_specs=[pl.BlockSpec((tm, D), lambda n, i: (i, 0)),
                  pl.BlockSpec((16, 3 * D), lambda n, i: (0, 0)),
                  pl.BlockSpec((1, D, SHC), lambda n, i: (n, 0, 0)),
                  pl.BlockSpec((1, SHC), lambda n, i: (0, n))],
        out_specs=pl.BlockSpec((tm, SHC), lambda n, i: (i, n)),
        out_shape=jax.ShapeDtypeStruct((m, IN_COLS), F32), name="inproj",
        compiler_params=_cp(("arbitrary", "arbitrary")))(x_all, mod, w_in_g, b_in)


def _gla_mask(d):
    row = lax.broadcasted_iota(jnp.int32, (CH, CH), 0)
    col = lax.broadcasted_iota(jnp.int32, (CH, CH), 1)
    return (row <= col) if d else (row >= col)


def _chunk_cumsum(v, rev):
    n = v.shape[0]
    pos = lax.broadcasted_iota(jnp.int32, v.shape, 0) & (CH - 1)
    s = 1
    while s < CH:
        if rev:
            v = v + jnp.where(pos < CH - s, pltpu.roll(v, n - s, 0), 0.0)
        else:
            v = v + jnp.where(pos >= s, pltpu.roll(v, s, 0), 0.0)
        s *= 2
    return v


def _chunk_rows(c):
    return slice(c * CH, (c + 1) * CH)


def _gla_features(zq, zf, lb, d):
    sq = _sig(zq)
    q = zq * sq * Q_SCALE
    sf = _sig(zf)
    f = lb + (1.0 - lb) * sf
    k = 1.0 - f
    g = _chunk_cumsum(jnp.log(f), d)
    last = 0 if d else CH - 1
    gls = [g[c * CH + last:c * CH + last + 1, :] for c in range(NCK)]
    glb = jnp.concatenate([jnp.broadcast_to(gl, (CH, D)) for gl in gls], axis=0)
    eg, eig, eeg = jnp.exp(g), jnp.exp(-g), jnp.exp(glb - g)
    decs = [jnp.exp(gl) for gl in gls]
    return sq, sf, f, q * eg, k * eig, k * eeg, eg, eig, eeg, decs


def _lower_bound(lbl_ref, d):
    return _sig(lbl_ref[0, d:d + 1, :] - lbl_ref[1, d:d + 1, :])


def _gla_rb(d, nlb, ncb):
    nrb = nlb + ncb
    if d:
        return lambda s: nrb - 1 - s
    return lambda s: jnp.where(s < ncb, nlb + s, s - ncb)


def _gla_fwd(z, lbl, d, nlb, ncb):
    m = z.shape[0]
    nrb = nlb + ncb
    rb = _gla_rb(d, nlb, ncb)

    def body(q_ref, f_ref, v_ref, lbl_ref, o_ref, st_ref, S):
        s = pl.program_id(0)

        @pl.when(s == 0)
        def _():
            S[...] = jnp.zeros_like(S)

        lb = _lower_bound(lbl_ref, d)
        mb = _gla_mask(d)
        _, _, _, qd, ki, ke, _, _, _, decs = _gla_features(q_ref[...], f_ref[...], lb, d)
        qd, ki, ke, v = qd.astype(BF), ki.astype(BF), ke.astype(BF), v_ref[...].astype(BF)
        order = range(NCK - 1, -1, -1) if d else range(NCK)
        for h in range(H):
            hs = slice(h * DH, (h + 1) * DH)
            intra, upd = {}, {}
            for c in range(NCK):
                rs = _chunk_rows(c)
                a = jnp.where(mb, _dot_nt(qd[rs, hs], ki[rs, hs]), 0.0)
                intra[c] = _dot(a, v[rs, hs])
                upd[c] = _dot_tn(v[rs, hs], ke[rs, hs])
            st = S[h]
            for c in order:
                rs = _chunk_rows(c)
                st_ref[c, h] = st
                o_ref[rs, hs] = intra[c] + _dot_nt(qd[rs, hs], st)
                st = st * decs[c][:, hs] + upd[c]
            S[h] = st

    def zspec(cb):
        return pl.BlockSpec((RB, D), lambda s: (rb(s), cb))

    return pl.pallas_call(
        body, grid=(nrb,),
        in_specs=[zspec(0), zspec(1 + d), zspec(3), pl.BlockSpec((2, 2, D), lambda s: (0, 0, 0))],
        out_specs=(pl.BlockSpec((RB, D), lambda s: (rb(s), 0)),
                   pl.BlockSpec((NCK, H, DH, DH), lambda s: (rb(s), 0, 0, 0))),
        out_shape=(jax.ShapeDtypeStruct((m, D), F32),
                   jax.ShapeDtypeStruct((m // CH, H, DH, DH), F32)),
        scratch_shapes=[pltpu.VMEM((H, DH, DH), F32)],
        name=f"gla_fwd{d}", compiler_params=_cp(("arbitrary",)))(z, z, z, lbl)


def _gla_bwd(z, lbl, do_lat, states, d, nlb, ncb, prev=None):
    m = z.shape[0]
    nrb = nlb + ncb
    fwd_rb = _gla_rb(d, nlb, ncb)
    rb = lambda s: fwd_rb(nrb - 1 - s)
    last = prev is not None

    def body(*refs):
        if last:
            q_ref, f_ref, v_ref, lbl_ref, do_ref, st_ref, pq_ref, pv_ref, o0_ref, o1_ref, o2_ref, sum_ref, dS = refs
        else:
            q_ref, f_ref, v_ref, lbl_ref, do_ref, st_ref, o0_ref, o1_ref, o2_ref, sum_ref, dS = refs
        s = pl.program_id(0)
        is_lat = rb(s) < nlb

        @pl.when(s == 0)
        def _():
            dS[...] = jnp.zeros_like(dS)
            sum_ref[...] = jnp.zeros_like(sum_ref)

        lb = _lower_bound(lbl_ref, d)
        mb = _gla_mask(d)
        zq = q_ref[...]
        sq, sf, f, qd, ki, ke, eg, eig, eeg, decs = _gla_features(zq, f_ref[...], lb, d)
        qdb, kib, keb, vb = qd.astype(BF), ki.astype(BF), ke.astype(BF), v_ref[...].astype(BF)
        dob = jnp.where(is_lat, do_ref[...], 0.0).astype(BF)
        order = range(NCK) if d else range(NCK - 1, -1, -1)
        dqd_h, dki_h, dke_h, dv_h, ddec_h = [], [], [], [], []
        for h in range(H):
            hs = slice(h * DH, (h + 1) * DH)
            dqd, dki, inc, dvi = {}, {}, {}, {}
            for c in range(NCK):
                rs = _chunk_rows(c)
                a = jnp.where(mb, _dot_nt(qdb[rs, hs], kib[rs, hs]), 0.0)
                da = jnp.where(mb, _dot_nt(dob[rs, hs], vb[rs, hs]), 0.0)
                dqd[c] = _dot(da, kib[rs, hs]) + _dot(dob[rs, hs], st_ref[c, h])
                dki[c] = _dot_tn(da, qdb[rs, hs])
                inc[c] = _dot_tn(dob[rs, hs], qdb[rs, hs])
                dvi[c] = _dot_tn(a, dob[rs, hs])
            dst = dS[h]
            dke, dv, ddec = {}, {}, {}
            for c in order:
                rs = _chunk_rows(c)
                dv[c] = dvi[c] + _dot_nt(keb[rs, hs], dst)
                dke[c] = _dot(vb[rs, hs], dst)
                ddec[c] = _colsum(st_ref[c, h] * dst)
                dst = inc[c] + dst * decs[c][:, hs]
            dS[h] = dst
            cat = lambda t: jnp.concatenate([t[c] for c in range(NCK)], axis=0)
            dqd_h.append(cat(dqd))
            dki_h.append(cat(dki))
            dke_h.append(cat(dke))
            dv_h.append(cat(dv))
            ddec_h.append([ddec[c] for c in range(NCK)])
        lanes = lambda parts: jnp.concatenate(parts, axis=1)
        dqd, dki, dke, dv = lanes(dqd_h), lanes(dki_h), lanes(dke_h), lanes(dv_h)
        dq = dqd * eg
        dk = dki * eig + dke * eeg
        dke_ke = dke * ke
        dg = dqd * qd - dki * ki - dke_ke
        dgl = [_colsum(dke_ke[_chunk_rows(c), :]) + lanes([ddec_h[h][c] for h in range(H)]) * decs[c]
               for c in range(NCK)]
        dglb = jnp.concatenate([jnp.broadcast_to(t, (CH, D)) for t in dgl], axis=0)
        df = (_chunk_cumsum(dg, 1 - d) + dglb) / f - dk
        dzf = df * (1.0 - lb) * (sf * (1.0 - sf))
        sum_ref[0:1, :] += _colsum(dzf)
        sum_ref[1:2, :] += _colsum(df * (1.0 - sf))
        if last:
            dz0 = (dq + pq_ref[...]) * (Q_SCALE * (sq * (1.0 + zq * (1.0 - sq))))
            dz3 = dv + pv_ref[...]
            sum_ref[2:3, :] += _colsum(dz0)
            sum_ref[3:4, :] += _colsum(dz3)
            o0_ref[...] = dz0.astype(BF)
            o1_ref[...] = dz3.astype(BF)
        else:
            o0_ref[...] = dq
            o1_ref[...] = dv
        o2_ref[...] = dzf.astype(BF)

        @pl.when(s == nrb - 1)
        def _():
            sum_ref[1:2, :] = sum_ref[1:2, :] * (lb * (1.0 - lb))

    def zspec(cb):
        return pl.BlockSpec((RB, D), lambda s: (rb(s), cb))

    rowspec = pl.BlockSpec((RB, D), lambda s: (rb(s), 0))
    in_specs = [zspec(0), zspec(1 + d), zspec(3), pl.BlockSpec((2, 2, D), lambda s: (0, 0, 0)),
                pl.BlockSpec((RB, D), lambda s: (jnp.minimum(rb(s), nlb - 1), 0)),
                pl.BlockSpec((NCK, H, DH, DH), lambda s: (rb(s), 0, 0, 0))]
    args = [z, z, z, lbl, do_lat, states]
    if last:
        in_specs += [rowspec, rowspec]
        args += list(prev)
    dt01 = BF if last else F32
    return pl.pallas_call(
        body, grid=(nrb,), in_specs=in_specs,
        out_specs=(rowspec, rowspec, rowspec, pl.BlockSpec((8, D), lambda s: (0, 0))),
        out_shape=(jax.ShapeDtypeStruct((m, D), dt01), jax.ShapeDtypeStruct((m, D), dt01),
                   jax.ShapeDtypeStruct((m, D), BF), jax.ShapeDtypeStruct((8, D), F32)),
        scratch_shapes=[pltpu.VMEM((H, DH, DH), F32)],
        name=f"gla_bwd{d}", compiler_params=_cp(("arbitrary",)))(*args)


def _shift_rows(v, k, r):
    row = lax.broadcasted_iota(jnp.int32, v.shape, 0)
    rolled = pltpu.roll(v, k % r, 0)
    return jnp.where((row >= k) & (row < r + k), rolled, 0.0)


def _conv_fwd(xb, cw_ref, cb_ref, r):
    return (cb_ref[...] + _shift_rows(xb, 1, r) * cw_ref[0:1, :] + xb * cw_ref[1:2, :]
            + _shift_rows(xb, -1, r) * cw_ref[2:3, :] + _shift_rows(xb, -2, r) * cw_ref[3:4, :])


def _window(ref, lo, n, rows):
    parts = []
    if lo < 0:
        parts.append(jnp.zeros((-lo,) + tuple(ref.shape[1:]), F32))
    parts.append(ref[max(lo, 0):min(lo + n, rows)])
    if lo + n > rows:
        parts.append(jnp.zeros((lo + n - rows,) + tuple(ref.shape[1:]), F32))
    return parts[0] if len(parts) == 1 else jnp.concatenate(parts, axis=0)


def _conv_cols(x_ref, cw_ref, cb_ref, r0, n, rows):
    w = _window(x_ref, r0 - 1, n + 3, rows)
    return (cb_ref[...] + w[0:n] * cw_ref[0:1, :] + w[1:n + 1] * cw_ref[1:2, :] + w[2:n + 2] * cw_ref[2:3, :]
            + w[3:n + 3] * cw_ref[3:4, :])


def _gate_a(xc, wr_ref, br, sp):
    xcb = xc.astype(BF)
    rs = [jnp.dot(xcb[:, g * DH:(g + 1) * DH], wr_ref[g].astype(BF), preferred_element_type=F32) for g in range(H)]
    return jnp.exp((-RG_C * _sig(jnp.concatenate(rs, axis=1) + br)) * sp)


def _softplus_neg(lam):
    y = jnp.exp(-jnp.abs(lam))
    u = 1.0 + y
    tiny = u == 1.0
    l1p = jnp.where(tiny, y, jnp.log(u) * (y / jnp.where(tiny, 1.0, u - 1.0)))
    return jnp.maximum(-lam, 0.0) + l1p


def _gates(xc, wr_ref, br, wi_ref, bi, sp):
    xcb = xc.astype(BF)
    rs, is_ = [], []
    for g in range(H):
        gs = slice(g * DH, (g + 1) * DH)
        rs.append(jnp.dot(xcb[:, gs], wr_ref[g].astype(BF), preferred_element_type=F32))
        is_.append(jnp.dot(xcb[:, gs], wi_ref[g].astype(BF), preferred_element_type=F32))
    r = _sig(jnp.concatenate(rs, axis=1) + br)
    i = _sig(jnp.concatenate(is_, axis=1) + bi)
    log_a = (-RG_C * r) * sp
    a = jnp.exp(log_a)
    t = jnp.tanh(log_a)
    om = (-2.0 * t) / (1.0 - t)
    return r, i, a, om


def _scan_rows(d, nrows, a_s, b_s, h_s, h0):
    nsl = nrows // 8

    def slab(j, h):
        jj = (nsl - 1 - j) if d else j
        r0 = pl.multiple_of(jj * 8, 8)
        for t in (range(7, -1, -1) if d else range(8)):
            h = a_s[pl.ds(r0 + t, 1), :] * h + b_s[pl.ds(r0 + t, 1), :]
            h_s[pl.ds(r0 + t, 1), :] = h
        return h

    return lax.fori_loop(0, nsl, slab, h0)


def _col_of(d, ncols):
    if d:
        return lambda s: ncols - jnp.maximum(s, 1)
    return lambda s: jnp.maximum(s, 1) - 1


def _rglru_fwd(z, cw, cb, wr, br, wi, bi, lam, d, t_lat, t_ctx):
    m = z.shape[0]
    rows = t_lat // GRID_W
    z3 = z.reshape(m // GRID_W, GRID_W, IN_COLS)
    nblk = GRID_W // CB
    blk = _col_of(d, nblk)
    cblk = t_lat // t_ctx
    rc = min(RCH, rows)

    def body(zc_ref, zx_ref, cw_ref, cb_ref, wr_ref, br_ref, wi_ref, bi_ref, lam_ref,
             hx_ref, hpx_ref, hpc_ref, a_s, b_s, h_s, hcar, a3, b3, cin_s):
        s = pl.program_id(0)
        sp = _softplus_neg(lam_ref[d:d + 1, :])
        br_ = br_ref[d:d + 1, :]
        bi_ = bi_ref[d:d + 1, :]

        @pl.when(s == 0)
        def _():
            xc = _conv_fwd(zc_ref[...], cw_ref, cb_ref, t_ctx)
            _, i, a, om = _gates(xc, wr_ref, br_, wi_ref, bi_, sp)
            a_s[...] = a
            b_s[...] = jnp.sqrt(om) * (i * xc)
            h0 = jnp.zeros((1, D), F32)
            hcar[0:1, :] = _scan_rows(d, t_ctx, a_s, b_s, h_s, h0)
            hs = h_s[...]
            row = lax.broadcasted_iota(jnp.int32, (t_ctx, D), 0)
            if d:
                hpc_ref[...] = jnp.where(row == t_ctx - 1, h0, pltpu.roll(hs, t_ctx - 1, 0))
            else:
                hpc_ref[...] = jnp.where(row == 0, h0, pltpu.roll(hs, 1, 0))

        @pl.when(s > 0)
        def _():
            for r0 in range(0, rows, rc):
                xc = _conv_cols(zx_ref, cw_ref, cb_ref, r0, rc, rows).reshape(rc * CB, D)
                _, i, a, om = _gates(xc, wr_ref, br_, wi_ref, bi_, sp)
                a3[r0:r0 + rc] = a.reshape(rc, CB, D)
                b3[r0:r0 + rc] = (jnp.sqrt(om) * (i * xc)).reshape(rc, CB, D)

            def local(t, carry):
                hl, p = carry
                r = (rows - 1 - t) if d else t
                a = a3[r]
                hl = a * hl + b3[r]
                p = a * p
                b3[r] = hl
                a3[r] = p
                return hl, p

            hl, p = lax.fori_loop(0, rows, local, (jnp.zeros((CB, D), F32), jnp.ones((CB, D), F32)))
            cin = hcar[0:1, :]
            for j in (range(CB - 1, -1, -1) if d else range(CB)):
                cin_s[j:j + 1, :] = cin
                cin = hl[j:j + 1, :] + p[j:j + 1, :] * cin
            hcar[0:1, :] = cin
            c_in = cin_s[...]

            def fix(t, prev):
                r = (rows - 1 - t) if d else t
                h = b3[r] + a3[r] * c_in
                hx_ref[r] = h
                hpx_ref[r] = prev
                return h

            lax.fori_loop(0, rows, fix, c_in)

    full = lambda shp: pl.BlockSpec(shp, lambda s: (0,) * len(shp))
    colspec = pl.BlockSpec((rows, CB, D), lambda s: (0, blk(s), 0))
    outs = pl.pallas_call(
        body, grid=(nblk + 1,),
        in_specs=[pl.BlockSpec((t_ctx, D), lambda s: (cblk, 5)),
                  pl.BlockSpec((rows, CB, D), lambda s: (0, blk(s), 5)),
                  full((4, D)), full((1, D)),
                  pl.BlockSpec((None, H, DH, DH), lambda s: (d, 0, 0, 0)), full((2, D)),
                  pl.BlockSpec((None, H, DH, DH), lambda s: (d, 0, 0, 0)), full((2, D)), full((2, D))],
        out_specs=(colspec, colspec, full((t_ctx, D))),
        out_shape=(jax.ShapeDtypeStruct((rows, GRID_W, D), F32), jax.ShapeDtypeStruct((rows, GRID_W, D), F32),
                   jax.ShapeDtypeStruct((t_ctx, D), F32)),
        scratch_shapes=[pltpu.VMEM((t_ctx, D), F32), pltpu.VMEM((t_ctx, D), F32), pltpu.VMEM((t_ctx, D), F32),
                        pltpu.VMEM((8, D), F32), pltpu.VMEM((rows, CB, D), F32), pltpu.VMEM((rows, CB, D), F32),
                        pltpu.VMEM((CB, D), F32)],
        name=f"rglru_fwd{d}", compiler_params=_cp(("arbitrary",)))(z, z3, cw, cb, wr, br, wi, bi, lam)
    return outs[0].reshape(t_lat, D), outs[1].reshape(t_lat, D), outs[2]


def _rglru_bwd(z, cw, cb, wr, br, wi, bi, lam, dh_lat, hp_lat, hp_ctx, d, t_lat, t_ctx, prev=None):
    m = z.shape[0]
    rows = t_lat // GRID_W
    z3 = z.reshape(m // GRID_W, GRID_W, IN_COLS)
    nblk = GRID_W // CB
    fblk = _col_of(d, nblk)
    blk = lambda s: fblk(nblk - jnp.minimum(s, nblk - 1))
    cblk = t_lat // t_ctx
    rc = min(RCH, rows)
    last = prev is not None
    view3 = lambda v: v.reshape(rows, GRID_W, D)

    def body(*refs):
        (zc_ref, zx_ref, cw_ref, cb_ref, wr_ref, br_ref, wi_ref, bi_ref, lam_ref, dh_ref, hpx_ref, hpc_ref) = refs[:12]
        k = 12
        if last:
            pdx_ref, pdc_ref = refs[12:14]
            k = 14
        ox_ref, oc_ref, dwr_ref, dwi_ref, sum_ref, a_s, b_s, gcar, a3, b3, cin_s = refs[k:]
        s = pl.program_id(0)
        lam_d = lam_ref[d:d + 1, :]
        sp = _softplus_neg(lam_d)
        br_ = br_ref[d:d + 1, :]
        bi_ = bi_ref[d:d + 1, :]
        flat = lambda v: v.reshape(-1, D)

        @pl.when(s == 0)
        def _():
            gcar[...] = jnp.zeros_like(gcar)
            dwr_ref[...] = jnp.zeros_like(dwr_ref)
            dwi_ref[...] = jnp.zeros_like(dwi_ref)
            sum_ref[...] = jnp.zeros_like(sum_ref)


        def conv_sums(dxc, dxb, xm1, x0, xp1, xp2):
            sum_ref[3:4, :] += _colsum(flat(dxc))
            sum_ref[4:5, :] += _colsum(flat(dxb))
            sum_ref[8:9, :] += _colsum(flat(dxc * xm1))
            sum_ref[9:10, :] += _colsum(flat(dxc * x0))
            sum_ref[10:11, :] += _colsum(flat(dxc * xp1))
            sum_ref[11:12, :] += _colsum(flat(dxc * xp2))

        def gate_grads(g, hp, xc, rr, i, a, om):
            mult = jnp.sqrt(om)
            da = g * hp
            ixc = i * xc
            dmult = g * ixc
            dixc = g * mult
            di = dixc * xc
            dxc = dixc * i
            dlog_a = da * a - dmult * ((1.0 - om) / mult)
            dr = dlog_a * (-RG_C * sp)
            sum_ref[2:3, :] += _colsum(dlog_a * rr)
            drp = dr * rr * (1.0 - rr)
            dip = di * i * (1.0 - i)
            sum_ref[0:1, :] += _colsum(drp)
            sum_ref[1:2, :] += _colsum(dip)
            xcb = xc.astype(BF)
            drb = drp.astype(BF)
            dib = dip.astype(BF)
            parts = []
            for gi in range(H):
                gs = slice(gi * DH, (gi + 1) * DH)
                parts.append(_dot_nt(drb[:, gs], wr_ref[gi]) + _dot_nt(dib[:, gs], wi_ref[gi]))
                dwr_ref[gi] += _dot_tn(xcb[:, gs], drb[:, gs])
                dwi_ref[gi] += _dot_tn(xcb[:, gs], dib[:, gs])
            return dxc + jnp.concatenate(parts, axis=1)

        @pl.when(s < nblk)
        def _():
            for r0 in range(0, rows, rc):
                xc = flat(_conv_cols(zx_ref, cw_ref, cb_ref, r0, rc, rows))
                a = _gate_a(xc, wr_ref, br_, sp).reshape(rc, CB, D)
                a3[r0:r0 + rc] = a
                b3[r0:r0 + rc] = a * dh_ref[r0:r0 + rc]

            def local(t, carry):
                c, q = carry
                r = t if d else (rows - 1 - t)
                a = a3[r]
                c = a * c + b3[r]
                q = a * q
                b3[r] = c
                a3[r] = q
                return c, q

            c, q = lax.fori_loop(0, rows, local, (jnp.zeros((CB, D), F32), jnp.ones((CB, D), F32)))
            cin = gcar[0:1, :]
            for j in (range(CB) if d else range(CB - 1, -1, -1)):
                cin_s[j:j + 1, :] = cin
                cin = c[j:j + 1, :] + q[j:j + 1, :] * cin
            gcar[0:1, :] = cin
            c_in = cin_s[...]
            for r0 in (range(rows - rc, -1, -rc) if d else range(0, rows, rc)):
                if d:
                    lo = max(r0 - 1, 0)
                    cn = b3[lo:r0 + rc - 1] + a3[lo:r0 + rc - 1] * c_in
                    if r0 == 0:
                        cn = jnp.concatenate([c_in[None], cn], axis=0)
                else:
                    hi = min(r0 + rc + 1, rows)
                    cn = b3[r0 + 1:hi] + a3[r0 + 1:hi] * c_in
                    if hi == rows:
                        cn = jnp.concatenate([cn, c_in[None]], axis=0)
                g = flat(dh_ref[r0:r0 + rc] + cn)
                xc = flat(_conv_cols(zx_ref, cw_ref, cb_ref, r0, rc, rows))
                rr, i, a, om = _gates(xc, wr_ref, br_, wi_ref, bi_, sp)
                b3[r0:r0 + rc] = gate_grads(g, flat(hpx_ref[r0:r0 + rc]), xc, rr, i, a, om).reshape(rc, CB, D)
            if last:
                for r0 in range(0, rows, rc):
                    b3[r0:r0 + rc] = b3[r0:r0 + rc] + pdx_ref[r0:r0 + rc]
                for r0 in range(0, rows, rc):
                    w = _window(b3, r0 - 2, rc + 3, rows)
                    xw = _window(zx_ref, r0 - 1, rc + 3, rows)
                    dxc = w[2:rc + 2]
                    dxb = (w[3:rc + 3] * cw_ref[0:1, :] + dxc * cw_ref[1:2, :] + w[1:rc + 1] * cw_ref[2:3, :]
                           + w[0:rc] * cw_ref[3:4, :])
                    conv_sums(dxc, dxb, xw[0:rc], xw[1:rc + 1], xw[2:rc + 2], xw[3:rc + 3])
                    ox_ref[r0:r0 + rc] = dxb
            else:
                for r0 in range(0, rows, rc):
                    ox_ref[r0:r0 + rc] = b3[r0:r0 + rc]

        @pl.when(s == nblk)
        def _():
            r = t_ctx
            xb = zc_ref[...]
            xc = _conv_fwd(xb, cw_ref, cb_ref, r)
            rr, i, a, om = _gates(xc, wr_ref, br_, wi_ref, bi_, sp)
            a_s[...] = a
            b_s[...] = jnp.zeros((r, D), F32)
            c0 = gcar[0:1, :]
            _scan_rows(1 - d, r, a_s, b_s, b_s, c0)
            cs = b_s[...]
            row = lax.broadcasted_iota(jnp.int32, (r, D), 0)
            if d:
                g = jnp.where(row == 0, c0, pltpu.roll(cs, 1, 0))
            else:
                g = jnp.where(row == r - 1, c0, pltpu.roll(cs, r - 1, 0))
            dxc = gate_grads(g, hpc_ref[...], xc, rr, i, a, om)
            if last:
                dxc = dxc + pdc_ref[...]
                dxb = (_shift_rows(dxc, -1, r) * cw_ref[0:1, :] + dxc * cw_ref[1:2, :]
                       + _shift_rows(dxc, 1, r) * cw_ref[2:3, :] + _shift_rows(dxc, 2, r) * cw_ref[3:4, :])
                conv_sums(dxc, dxb, _shift_rows(xb, 1, r), xb, _shift_rows(xb, -1, r), _shift_rows(xb, -2, r))
                oc_ref[...] = dxb
            else:
                oc_ref[...] = dxc
            sum_ref[2:3, :] = sum_ref[2:3, :] * (RG_C * _sig(-lam_d))

    full = lambda shp: pl.BlockSpec(shp, lambda s: (0,) * len(shp))
    once = lambda shp: pl.BlockSpec(shp, lambda s: (0,) * len(shp), pipeline_mode=pl.Buffered(1))
    colspec = pl.BlockSpec((rows, CB, D), lambda s: (0, blk(s), 0))
    colonce = pl.BlockSpec((rows, CB, D), lambda s: (0, blk(s), 0), pipeline_mode=pl.Buffered(1))
    in_specs = [pl.BlockSpec((t_ctx, D), lambda s: (cblk, 5), pipeline_mode=pl.Buffered(1)),
                pl.BlockSpec((rows, CB, D), lambda s: (0, blk(s), 5), pipeline_mode=pl.Buffered(1)),
                full((4, D)), full((1, D)),
                pl.BlockSpec((None, H, DH, DH), lambda s: (d, 0, 0, 0)), full((2, D)),
                pl.BlockSpec((None, H, DH, DH), lambda s: (d, 0, 0, 0)), full((2, D)), full((2, D)),
                colonce, colonce, once((t_ctx, D))]
    args = [z, z3, cw, cb, wr, br, wi, bi, lam, view3(dh_lat), view3(hp_lat), hp_ctx]
    if last:
        in_specs += [colonce, once((t_ctx, D))]
        args += [view3(prev[0]), prev[1]]
    outs = pl.pallas_call(
        body, grid=(nblk + 1,), in_specs=in_specs,
        out_specs=(colspec, full((t_ctx, D)), full((H, DH, DH)), full((H, DH, DH)), full((16, D))),
        out_shape=(jax.ShapeDtypeStruct((rows, GRID_W, D), F32), jax.ShapeDtypeStruct((t_ctx, D), F32),
                   jax.ShapeDtypeStruct((H, DH, DH), F32), jax.ShapeDtypeStruct((H, DH, DH), F32),
                   jax.ShapeDtypeStruct((16, D), F32)),
        scratch_shapes=[pltpu.VMEM((t_ctx, D), F32), pltpu.VMEM((t_ctx, D), F32), pltpu.VMEM((8, D), F32),
                        pltpu.VMEM((rows, CB, D), F32), pltpu.VMEM((rows, CB, D), F32), pltpu.VMEM((CB, D), F32)],
        name=f"rglru_bwd{d}", compiler_params=_cp(("arbitrary",)))(*args)
    return (outs[0].reshape(t_lat, D), outs[1]) + tuple(outs[2:])


def _merge(o_f, o_b, h_f, h_b, z, x_all, tgt, mod, norm_g, ln_g, ln_b, p_a, p_b, w_out, t_lat):
    tm = 128
    nt = t_lat // tm

    def body(of_ref, ob_ref, hf_ref, hb_ref, z4_ref, z6_ref, z7_ref, z8_ref, x_ref, t_ref, mod_ref, ng_ref,
             lg_ref, lb_ref, pa_ref, pb_ref, wo_ref,
             do_ref, dh_ref, dz4_ref, dz6_ref, dz7_ref, dz8_ref, gx_ref,
             y_ref, dout_ref, oa_ref, dpa_ref, obv_ref, dpb_ref, acc_ref):
        i = pl.program_id(0)

        @pl.when(i == 0)
        def _():
            acc_ref[...] = jnp.zeros_like(acc_ref)

        def per_head(v):
            return jnp.concatenate(
                [jnp.broadcast_to(jnp.mean(v[:, h * DH:(h + 1) * DH], axis=-1, keepdims=True), (tm, DH))
                 for h in range(H)], axis=1)

        gt = mod_ref[0:1, 2 * D:3 * D]
        gfull = jnp.concatenate([ng_ref[...]] * H, axis=1)
        o = of_ref[...] + ob_ref[...]
        rinv = lax.rsqrt(per_head(o * o) + RMS_EPS)
        n = o * rinv
        na = n * gfull
        z4 = z4_ref[...]
        s4 = _sig(z4)
        silu4 = z4 * s4
        oa = na * silu4
        z6 = z6_ref[...]
        s6 = _sig(z6)
        silu6 = z6 * s6
        hsum = hf_ref[...] + hb_ref[...]
        obv = hsum * silu6
        pa = _dot(oa, pa_ref[...])
        pb = _dot(obv, pb_ref[...])
        s7 = _sig(z7_ref[...])
        s8 = _sig(z8_ref[...])
        y = s7 * pa + s8 * pb
        out = _dot(y, wo_ref[...])
        pre = ALPHA * x_ref[...] + gt * out
        mu = jnp.mean(pre, axis=-1, keepdims=True)
        xc = pre - mu
        rstd = lax.rsqrt(jnp.mean(xc * xc, axis=-1, keepdims=True) + LN_EPS)
        xhat = xc * rstd
        lg = lg_ref[...]
        diff = xhat * lg + lb_ref[...] - t_ref[...]
        acc_ref[8:9, :] += _colsum(diff * diff) * (0.5 / D)
        dxn = diff * (1.0 / D)
        acc_ref[1:2, :] += _colsum(dxn * xhat)
        acc_ref[2:3, :] += _colsum(dxn)
        dxhat = dxn * lg
        dpre = rstd * (dxhat - jnp.mean(dxhat, axis=-1, keepdims=True)
                       - xhat * jnp.mean(dxhat * xhat, axis=-1, keepdims=True))
        gx_ref[...] = ALPHA * dpre
        acc_ref[0:1, :] += _colsum(dpre * out)
        dout = dpre * gt
        dy = _dot_nt(dout, wo_ref[...])
        dpa = dy * s7
        dpb = dy * s8
        dz7 = dy * pa * (s7 * (1.0 - s7))
        dz8 = dy * pb * (s8 * (1.0 - s8))
        doa = _dot_nt(dpa, pa_ref[...])
        dob = _dot_nt(dpb, pb_ref[...])
        dh_ref[...] = dob * silu6
        dz6 = dob * hsum * (s6 * (1.0 + z6 * (1.0 - s6)))
        dna = doa * silu4
        dz4 = doa * na * (s4 * (1.0 + z4 * (1.0 - s4)))
        dng = _colsum(dna * n)
        acc_ref[7:8, 0:DH] += sum(dng[:, h * DH:(h + 1) * DH] for h in range(H))
        dn = dna * gfull
        do_ref[...] = rinv * (dn - n * per_head(dn * n))
        acc_ref[3:4, :] += _colsum(dz4)
        acc_ref[4:5, :] += _colsum(dz6)
        acc_ref[5:6, :] += _colsum(dz7)
        acc_ref[6:7, :] += _colsum(dz8)
        dz4_ref[...] = dz4.astype(BF)
        dz6_ref[...] = dz6.astype(BF)
        dz7_ref[...] = dz7.astype(BF)
        dz8_ref[...] = dz8.astype(BF)
        y_ref[...] = y.astype(BF)
        dout_ref[...] = dout.astype(BF)
        oa_ref[...] = oa.astype(BF)
        dpa_ref[...] = dpa.astype(BF)
        obv_ref[...] = obv.astype(BF)
        dpb_ref[...] = dpb.astype(BF)

        @pl.when(i == nt - 1)
        def _():
            acc_ref[9:10, :] = jnp.broadcast_to(jnp.sum(acc_ref[8:9, :], axis=-1, keepdims=True), (1, D))

    row = pl.BlockSpec((tm, D), lambda i: (i, 0))
    zs = lambda cb: pl.BlockSpec((tm, D), lambda i: (i, cb))
    full = lambda shp: pl.BlockSpec(shp, lambda i: (0,) * len(shp))
    f32o = jax.ShapeDtypeStruct((t_lat, D), F32)
    bfo = jax.ShapeDtypeStruct((t_lat, D), BF)
    return pl.pallas_call(
        body, grid=(nt,),
        in_specs=[row, row, row, row, zs(4), zs(6), zs(7), zs(8), row, row, full((16, 3 * D)), full((1, DH)),
                  full((1, D)), full((1, D)), full((D, D)), full((D, D)), full((D, D))],
        out_specs=(row,) * 13 + (full((16, D)),),
        out_shape=(f32o, f32o, bfo, bfo, bfo, bfo, f32o, bfo, bfo, bfo, bfo, bfo, bfo,
                   jax.ShapeDtypeStruct((16, D), F32)),
        name="merge", compiler_params=_cp(("arbitrary",)))(
            o_f, o_b, h_f, h_b, z, z, z, z, x_all, tgt, mod, norm_g, ln_g, ln_b, p_a, p_b, w_out)


def _wgrad(a, b, name):
    tm = 512

    def body(a_ref, b_ref, o_ref):
        @pl.when(pl.program_id(0) == 0)
        def _():
            o_ref[...] = jnp.zeros_like(o_ref)
        o_ref[...] += _dot_tn(a_ref[...], b_ref[...])

    row = pl.BlockSpec((tm, D), lambda i: (i, 0))
    return pl.pallas_call(body, grid=(a.shape[0] // tm,), in_specs=[row, row],
                          out_specs=pl.BlockSpec((D, D), lambda i: (0, 0)),
                          out_shape=jax.ShapeDtypeStruct((D, D), F32), name=name,
                          compiler_params=_cp(("arbitrary",)))(a, b)


def _wgrad_in(x_all, mod, dz, n_lat_tiles):
    m = x_all.shape[0]
    tm = 256

    def body(x_ref, mod_ref, dz_ref, o_ref):
        i = pl.program_id(1)

        @pl.when(i == 0)
        def _():
            o_ref[...] = jnp.zeros_like(o_ref)
        u, _ = _modulate(x_ref, mod_ref, i < n_lat_tiles)
        o_ref[0] += _dot_tn(u, dz_ref[...])

    return pl.pallas_call(
        body, grid=(NSH, m // tm),
        in_specs=[pl.BlockSpec((tm, D), lambda n, i: (i, 0)),
                  pl.BlockSpec((16, 3 * D), lambda n, i: (0, 0)),
                  pl.BlockSpec((tm, SHC), lambda n, i: (i, n))],
        out_specs=pl.BlockSpec((1, D, SHC), lambda n, i: (n, 0, 0)),
        out_shape=jax.ShapeDtypeStruct((NSH, D, SHC), F32), name="wgrad_in",
        compiler_params=_cp(("arbitrary", "arbitrary")))(x_all, mod, dz)


def _du(dz, w_in_g, x_all, mod, gxres, n_lat_tiles):
    m = x_all.shape[0]
    tm = 256
    nt = m // tm
    nct = nt - n_lat_tiles
    rblk = lambda i: jnp.where(i < nct, n_lat_tiles + i, i - nct)
    lblk = lambda i: jnp.maximum(i - nct, 0)

    def body(dz_ref, w_ref, x_ref, mod_ref, gr_ref, gx_ref, dm_ref, acc):
        i = pl.program_id(0)
        n = pl.program_id(1)
        is_lat = i >= nct

        @pl.when((i == 0) & (n == 0))
        def _():
            dm_ref[...] = jnp.zeros_like(dm_ref)

        @pl.when(n == 0)
        def _():
            acc[...] = jnp.zeros_like(acc)

        acc[...] += _dot_nt(dz_ref[...], w_ref[0])

        @pl.when(n == NSH - 1)
        def _():
            du = acc[...]
            sc = jnp.where(is_lat, mod_ref[0:1, D:2 * D], mod_ref[1:2, D:2 * D])
            dsh = _colsum(du)
            dsc = _colsum(du * x_ref[...])

            @pl.when(is_lat)
            def _():
                gx_ref[...] = du * (1.0 + sc) + gr_ref[...]
                dm_ref[0:1, 0:D] += dsh
                dm_ref[0:1, D:2 * D] += dsc

            @pl.when(jnp.logical_not(is_lat))
            def _():
                dm_ref[1:2, 0:D] += dsh
                dm_ref[1:2, D:2 * D] += dsc

    return pl.pallas_call(
        body, grid=(nt, NSH),
        in_specs=[pl.BlockSpec((tm, SHC), lambda i, n: (rblk(i), n)),
                  pl.BlockSpec((1, D, SHC), lambda i, n: (n, 0, 0)),
                  pl.BlockSpec((tm, D), lambda i, n: (rblk(i), 0)),
                  pl.BlockSpec((16, 3 * D), lambda i, n: (0, 0)),
                  pl.BlockSpec((tm, D), lambda i, n: (lblk(i), 0))],
        out_specs=(pl.BlockSpec((tm, D), lambda i, n: (lblk(i), 0)),
                   pl.BlockSpec((8, 2 * D), lambda i, n: (0, 0))),
        out_shape=(jax.ShapeDtypeStruct((n_lat_tiles * tm, D), F32), jax.ShapeDtypeStruct((8, 2 * D), F32)),
        scratch_shapes=[pltpu.VMEM((tm, D), F32)],
        name="du", compiler_params=_cp(("arbitrary", "arbitrary")))(dz, w_in_g, x_all, mod, gxres)


def _row_tile(rows, cols):
    t = 8
    while t * 2 * cols * 4 <= (1 << 20) and rows % (t * 2) == 0:
        t *= 2
    return t


def _addn(arrs, name):
    shape = arrs[0].shape
    cols = shape[-1]
    flat = [a.reshape(-1, cols) for a in arrs]
    rows = flat[0].shape[0]
    tr = _row_tile(rows, cols)

    def body(*refs):
        acc = refs[0][...]
        for r in refs[1:-1]:
            acc = acc + r[...]
        refs[-1][...] = acc

    spec = pl.BlockSpec((tr, cols), lambda i: (i, 0))
    out = pl.pallas_call(body, grid=(rows // tr,), in_specs=[spec] * len(flat), out_specs=spec,
                         out_shape=jax.ShapeDtypeStruct((rows, cols), F32), name=name,
                         compiler_params=_cp(("arbitrary",)))(*flat)
    return out.reshape(shape)


def _adamw(w, g, m, v, name):
    rows, cols = w.shape
    tr = _row_tile(rows, cols)

    def body(w_ref, g_ref, m_ref, v_ref, d_ref, nm_ref, nv_ref):
        gg = g_ref[...]
        m2 = ADAM_B1 * m_ref[...] + (1.0 - ADAM_B1) * gg
        v2 = ADAM_B2 * v_ref[...] + (1.0 - ADAM_B2) * (gg * gg)
        m_hat = m2 / (1.0 - ADAM_B1 ** ADAM_STEP)
        v_hat = v2 / (1.0 - ADAM_B2 ** ADAM_STEP)
        d_ref[...] = -ADAM_LR * (m_hat / (jnp.sqrt(v_hat) + ADAM_EPS) + ADAM_WD * w_ref[...])
        nm_ref[...] = m2
        nv_ref[...] = v2

    spec = pl.BlockSpec((tr, cols), lambda i: (i, 0))
    o = jax.ShapeDtypeStruct((rows, cols), F32)
    return pl.pallas_call(body, grid=(rows // tr,), in_specs=[spec] * 4, out_specs=(spec,) * 3,
                          out_shape=(o, o, o), name=name, compiler_params=_cp(("arbitrary",)))(w, g, m, v)


_ANY = pl.BlockSpec(memory_space=pl.ANY)


def _place():
    return lax.axis_index("x"), lax.axis_index("y"), lax.axis_index("c")


def _allgather_chips(shards):
    n = len(shards)

    def body(*refs):
        ins, outs = refs[:n], refs[n:2 * n]
        send, recv, lsem = refs[2 * n:]
        x, y, c = _place()
        me = 2 * x + y
        peers = ((1 - x, y, c), (x, 1 - y, c), (1 - x, 1 - y, c))
        copies = []
        for j in range(n):
            cp = pltpu.make_async_copy(ins[j], outs[j].at[me], lsem.at[j])
            cp.start()
            copies.append(cp)
            for k, p in enumerate(peers):
                cp = pltpu.make_async_remote_copy(src_ref=ins[j], dst_ref=outs[j].at[me], send_sem=send.at[3 * j + k],
                                                  recv_sem=recv.at[3 * j + k], device_id=p, device_id_type=MESH)
                cp.start()
                copies.append(cp)
        for cp in copies:
            cp.wait()

    return pl.pallas_call(
        body, in_specs=[_ANY] * n, out_specs=[_ANY] * n,
        out_shape=[jax.ShapeDtypeStruct((NSH,) + s.shape, s.dtype) for s in shards],
        scratch_shapes=[pltpu.SemaphoreType.DMA((3 * n,)), pltpu.SemaphoreType.DMA((3 * n,)),
                        pltpu.SemaphoreType.DMA((n,))],
        name="allgather_weights")(*shards)


def _rs_sibling(grads):
    n = len(grads)

    def body(*refs):
        ins, got = refs[:n], refs[n:2 * n]
        send, recv = refs[2 * n:]
        x, y, c = _place()
        copies = []
        for j in range(n):
            hr = ins[j].shape[1] // 2
            for s in range(NSH):
                give = ins[j].at[s, pl.ds(pl.multiple_of((1 - c) * hr, 8), hr), :]
                cp = pltpu.make_async_remote_copy(src_ref=give, dst_ref=got[j].at[s], send_sem=send.at[NSH * j + s],
                                                  recv_sem=recv.at[NSH * j + s], device_id=(x, y, 1 - c),
                                                  device_id_type=MESH)
                cp.start()
                copies.append(cp)
        for cp in copies:
            cp.wait()

    half = [jax.ShapeDtypeStruct((NSH, g.shape[1] // 2, g.shape[2]), F32) for g in grads]
    return pl.pallas_call(
        body, in_specs=[_ANY] * n, out_specs=[_ANY] * n, out_shape=half,
        scratch_shapes=[pltpu.SemaphoreType.DMA((NSH * n,)), pltpu.SemaphoreType.DMA((NSH * n,))],
        name="rs_sibling")(*grads)


def _core_vec():
    return lax.axis_index("c").astype(jnp.int32).reshape(1)


def _rs_add1(g, got, name):
    _, r, cols = g.shape
    hr = r // 2
    tr = _row_tile(hr, cols)
    nb = hr // tr

    def body(c_ref, g_ref, got_ref, o_ref):
        o_ref[...] = (g_ref[...] + got_ref[...]).astype(BF)

    spec = pl.BlockSpec((1, tr, cols), lambda s, i, c_ref: (s, i, 0))
    return pl.pallas_call(
        body, grid_spec=pltpu.PrefetchScalarGridSpec(
            num_scalar_prefetch=1, grid=(NSH, nb),
            in_specs=[pl.BlockSpec((1, tr, cols), lambda s, i, c_ref: (s, c_ref[0] * nb + i, 0)), spec],
            out_specs=spec),
        out_shape=jax.ShapeDtypeStruct((NSH, hr, cols), BF), name=name,
        compiler_params=_cp(("arbitrary", "arbitrary")))(_core_vec(), g, got)


def _rs_add2(sums, got, name):
    _, hr, cols = sums.shape
    tr = _row_tile(hr, cols)
    nb = hr // tr
    place = jnp.stack([2 * lax.axis_index("x") + lax.axis_index("y"), lax.axis_index("c")]).astype(jnp.int32)

    def body(p_ref, s_ref, got_ref, o_ref):
        f = lambda v: v.astype(F32)
        o_ref[...] = f(s_ref[0]) + f(got_ref[0]) + f(got_ref[1]) + f(got_ref[2])

    return pl.pallas_call(
        body, grid_spec=pltpu.PrefetchScalarGridSpec(
            num_scalar_prefetch=1, grid=(nb,),
            in_specs=[pl.BlockSpec((1, tr, cols), lambda i, p_ref: (p_ref[0], i, 0)),
                      pl.BlockSpec((3, tr, cols), lambda i, p_ref: (0, i, 0))],
            out_specs=pl.BlockSpec((tr, cols), lambda i, p_ref: (p_ref[1] * nb + i, 0))),
        out_shape=jax.ShapeDtypeStruct((2 * hr, cols), F32), name=name,
        compiler_params=_cp(("arbitrary",)))(place, sums, got)


def _rs_chips(sums):
    n = len(sums)

    def body(*refs):
        ins, got = refs[:n], refs[n:2 * n]
        send, recv = refs[2 * n:]
        x, y, c = _place()
        peers = ((1 - x, y), (x, 1 - y), (1 - x, 1 - y))
        copies = []
        for j in range(n):
            for k, (px, py) in enumerate(peers):
                cp = pltpu.make_async_remote_copy(src_ref=ins[j].at[2 * px + py], dst_ref=got[j].at[k],
                                                  send_sem=send.at[3 * j + k], recv_sem=recv.at[3 * j + k],
                                                  device_id=(px, py, c), device_id_type=MESH)
                cp.start()
                copies.append(cp)
        for cp in copies:
            cp.wait()

    three = [jax.ShapeDtypeStruct((3,) + g.shape[1:], g.dtype) for g in sums]
    return pl.pallas_call(
        body, in_specs=[_ANY] * n, out_specs=[_ANY] * n, out_shape=three,
        scratch_shapes=[pltpu.SemaphoreType.DMA((3 * n,)), pltpu.SemaphoreType.DMA((3 * n,))],
        name="rs_chips")(*sums)


def _ag_sibling(fulls):
    n = len(fulls)
    nck = 4

    def body(*refs):
        outs = refs[n:2 * n]
        send, recv = refs[2 * n:]
        x, y, c = _place()
        copies = []
        for j in range(n):
            qr = outs[j].shape[0] // (2 * nck)
            for k in range(nck):
                rows = outs[j].at[pl.ds(pl.multiple_of((c * nck + k) * qr, 8), qr), :]
                cp = pltpu.make_async_remote_copy(src_ref=rows, dst_ref=rows, send_sem=send.at[nck * j + k],
                                                  recv_sem=recv.at[nck * j + k], device_id=(x, y, 1 - c),
                                                  device_id_type=MESH)
                cp.start()
                copies.append(cp)
        for cp in copies:
            cp.wait()

    return pl.pallas_call(
        body, in_specs=[_ANY] * n, out_specs=[_ANY] * n,
        out_shape=[jax.ShapeDtypeStruct(f.shape, F32) for f in fulls],
        input_output_aliases={j: j for j in range(n)},
        scratch_shapes=[pltpu.SemaphoreType.DMA((nck * n,)), pltpu.SemaphoreType.DMA((nck * n,))],
        name="ag_sibling")(*fulls)


def _allreduce_small(buf):
    rows = buf.shape[0]
    pr = rows // 8

    def body(in_ref, out_ref, stage, send1, recv1, send2, recv2):
        x, y, c = _place()
        me = 4 * x + 2 * y + c

        def peer(k):
            kx, ky, kc = (k >> 2) & 1, (k >> 1) & 1, k & 1
            return (x ^ kx, y ^ ky, c ^ kc)

        def piece(ref, idx):
            return ref.at[pl.ds(pl.multiple_of(idx * pr, 8), pr), :]

        copies = []
        for k in range(1, 8):
            px, py, pc = peer(k)
            cp = pltpu.make_async_remote_copy(src_ref=piece(in_ref, 4 * px + 2 * py + pc), dst_ref=stage.at[k],
                                              send_sem=send1.at[k], recv_sem=recv1.at[k],
                                              device_id=(px, py, pc), device_id_type=MESH)
            cp.start()
            copies.append(cp)
        for cp in copies:
            cp.wait()
        acc = piece(in_ref, me)[...]
        for k in range(1, 8):
            acc = acc + stage[k]
        piece(out_ref, me)[...] = acc
        copies = []
        for k in range(1, 8):
            cp = pltpu.make_async_remote_copy(src_ref=piece(out_ref, me), dst_ref=piece(out_ref, me),
                                              send_sem=send2.at[k], recv_sem=recv2.at[k],
                                              device_id=peer(k), device_id_type=MESH)
            cp.start()
            copies.append(cp)
        for cp in copies:
            cp.wait()

    vm = pl.BlockSpec(memory_space=pltpu.VMEM)
    return pl.pallas_call(
        body, in_specs=[vm], out_specs=vm, out_shape=jax.ShapeDtypeStruct((rows, D), F32),
        scratch_shapes=[pltpu.VMEM((8, pr, D), F32)] + [pltpu.SemaphoreType.DMA((8,))] * 4,
        name="allreduce_small", compiler_params=_cp())(buf)


def _reduce_scatter(grads):
    got = _rs_sibling(grads)
    sums = [_rs_add1(g, b, f"rs_add1_{j}") for j, (g, b) in enumerate(zip(grads, got))]
    got = _rs_chips(sums)
    fulls = [_rs_add2(a, b, f"rs_add2_{j}") for j, (a, b) in enumerate(zip(sums, got))]
    return _ag_sibling(fulls)


def _rows(a):
    flat = a.reshape(-1)
    pad = (-flat.shape[0]) % D
    if pad:
        flat = jnp.concatenate([flat, jnp.zeros((pad,), flat.dtype)])
    return flat.reshape(-1, D)


def _pad_rows(a, mult):
    pad = (-a.shape[0]) % mult
    return jnp.concatenate([a, jnp.zeros((pad, a.shape[1]), a.dtype)]) if pad else a


def _local_step(x, c, ctx, c_ctx, tgt, w_mod_g, b_mod, w_in_g, b_in, lbl, norm_g, cw, cb, wr, br, wi, bi, lam,
                p_a, p_b, w_out, ln_g, ln_b):
    t_lat, t_ctx = x.shape[0], ctx.shape[0]
    nlt = t_lat // 256
    nlb, ncb = t_lat // RB, t_ctx // RB
    cc = jnp.zeros((16, D), F32).at[0].set(c).at[1].set(c_ctx)
    mod = _mod_fwd(cc, w_mod_g, b_mod)
    x_all = jnp.concatenate([x, ctx], axis=0)
    z = _inproj(x_all, mod, w_in_g, b_in, nlt)
    o0, st0 = _gla_fwd(z, lbl, 0, nlb, ncb)
    o1, st1 = _gla_fwd(z, lbl, 1, nlb, ncb)
    h0, hp0, hpc0 = _rglru_fwd(z, cw, cb, wr, br, wi, bi, lam, 0, t_lat, t_ctx)
    h1, hp1, hpc1 = _rglru_fwd(z, cw, cb, wr, br, wi, bi, lam, 1, t_lat, t_ctx)
    (do, dh, dz4, dz6, dz7, dz8, gxres, y, dout, oa, dpa, obv, dpb, acc) = _merge(
        o0, o1, h0, h1, z, x_all, tgt, mod, norm_g, ln_g, ln_b, p_a, p_b, w_out, t_lat)
    gp_a = _wgrad(oa, dpa, "wgrad_pa")
    gp_b = _wgrad(obv, dpb, "wgrad_pb")
    gw_out = _wgrad(y, dout, "wgrad_wout")
    dxc_lat, dxc_ctx, dwr0, dwi0, sb0 = _rglru_bwd(z, cw, cb, wr, br, wi, bi, lam, dh, hp0, hpc0, 0, t_lat, t_ctx)
    dz5_lat, dz5_ctx, dwr1, dwi1, sb1 = _rglru_bwd(z, cw, cb, wr, br, wi, bi, lam, dh, hp1, hpc1, 1, t_lat, t_ctx,
                                                   prev=(dxc_lat, dxc_ctx))
    dq0, dv0, dz1, sa0 = _gla_bwd(z, lbl, do, st0, 0, nlb, ncb)
    dz0, dz3, dz2, sa1 = _gla_bwd(z, lbl, do, st1, 1, nlb, ncb, prev=(dq0, dv0))
    zc = jnp.zeros((t_ctx, D), BF)
    pad = lambda a: jnp.concatenate([a, zc], axis=0)
    dz = jnp.concatenate([dz0, dz1, dz2, dz3, pad(dz4), jnp.concatenate([dz5_lat, dz5_ctx], axis=0).astype(BF),
                          pad(dz6), pad(dz7), pad(dz8)], axis=1)
    gw_in = _wgrad_in(x_all, mod, dz, nlt)
    gx, dm = _du(dz, w_in_g, x_all, mod, gxres, nlt)
    dmod = jnp.zeros((16, 3 * D), F32).at[0:2, 0:2 * D].set(dm[0:2]).at[0, 2 * D:].set(acc[0])
    gw_mod, dcc = _mod_bwd(cc, dmod, w_mod_g)
    small = dict(
        c_ctx=dcc[1:2], b_mod=(dmod[0] + dmod[1]).reshape(3, D),
        b_in=jnp.stack([sa1[2], sa0[0], sa1[0], sa1[3], acc[3], sb1[4], acc[4], acc[5], acc[6]]),
        lb_logits=jnp.stack([sa0[1], sa1[1], -sa0[1], -sa1[1]]),
        norm_a_g=acc[7:8], conv_w=sb1[8:12], conv_b=sb1[3:4],
        w_r=jnp.stack([dwr0, dwr1]).reshape(-1, D), w_i=jnp.stack([dwi0, dwi1]).reshape(-1, D),
        b_r=jnp.stack([sb0[0], sb1[0]]), b_i=jnp.stack([sb0[1], sb1[1]]), lam=jnp.stack([sb0[2], sb1[2]]),
        ln_g=acc[1:2], ln_b=acc[2:3])
    big = dict(w_mod=gw_mod, w_in=gw_in, p_a=gp_a.reshape(NSH, D // NSH, D), p_b=gp_b.reshape(NSH, D // NSH, D),
               w_out=gw_out.reshape(NSH, D // NSH, D))
    return acc[9, 0], gx, big, small


_SMALL = ("c_ctx", "b_mod", "b_in", "lb_logits", "norm_a_g", "conv_w", "conv_b", "w_r", "w_i", "b_r", "b_i", "lam",
          "ln_g", "ln_b")
_BIG = ("w_mod", "w_in", "p_a", "p_b", "w_out")
_COL_SHARDED = ("lb_logits", "conv_w", "b_r", "b_i", "lam")
_WEIGHTS = ("c_ctx", "w_mod", "b_mod", "w_in", "b_in", "lb_logits", "norm_a_g", "conv_w", "conv_b", "w_r", "b_r", "w_i",
            "b_i", "lam", "p_a", "p_b", "w_out", "ln_g", "ln_b")


def kernel(x, c, ctx, c_ctx, w_mod, b_mod, w_in, b_in, lb_logits, norm_a_g, conv_w, conv_b, w_r, b_r, w_i, b_i, lam, p_a, p_b, w_out, ln_g, ln_b, loss_target, m_c_ctx, m_w_mod, m_b_mod, m_w_in, m_b_in, m_lb_logits, m_norm_a_g, m_conv_w, m_conv_b, m_w_r, m_b_r, m_w_i, m_b_i, m_lam, m_p_a, m_p_b, m_w_out, m_ln_g, m_ln_b, v_c_ctx, v_w_mod, v_b_mod, v_w_in, v_b_in, v_lb_logits, v_norm_a_g, v_conv_w, v_conv_b, v_w_r, v_b_r, v_w_i, v_b_i, v_lam, v_p_a, v_p_b, v_w_out, v_ln_g, v_ln_b):
    w = dict(c_ctx=c_ctx, w_mod=w_mod, b_mod=b_mod, w_in=w_in, b_in=b_in, lb_logits=lb_logits, norm_a_g=norm_a_g,
             conv_w=conv_w, conv_b=conv_b, w_r=w_r, b_r=b_r, w_i=w_i, b_i=b_i, lam=lam, p_a=p_a, p_b=p_b, w_out=w_out,
             ln_g=ln_g, ln_b=ln_b)
    m = dict(c_ctx=m_c_ctx, w_mod=m_w_mod, b_mod=m_b_mod, w_in=m_w_in, b_in=m_b_in, lb_logits=m_lb_logits,
             norm_a_g=m_norm_a_g, conv_w=m_conv_w, conv_b=m_conv_b, w_r=m_w_r, b_r=m_b_r, w_i=m_w_i, b_i=m_b_i,
             lam=m_lam, p_a=m_p_a, p_b=m_p_b, w_out=m_w_out, ln_g=m_ln_g, ln_b=m_ln_b)
    v = dict(c_ctx=v_c_ctx, w_mod=v_w_mod, b_mod=v_b_mod, w_in=v_w_in, b_in=v_b_in, lb_logits=v_lb_logits,
             norm_a_g=v_norm_a_g, conv_w=v_conv_w, conv_b=v_conv_b, w_r=v_w_r, b_r=v_b_r, w_i=v_w_i, b_i=v_b_i,
             lam=v_lam, p_a=v_p_a, p_b=v_p_b, w_out=v_w_out, ln_g=v_ln_g, ln_b=v_ln_b)
    shard = 2 * lax.axis_index("x") + lax.axis_index("y")
    cs = D // NSH

    narrow = _pad_rows(jnp.concatenate([lb_logits.reshape(4, cs), conv_w[0], b_r[0], b_i[0], lam[0]], axis=0), 8)
    big_shards = [w[k][0].astype(BF) for k in _BIG]
    gathered = _allgather_chips(big_shards + [narrow])
    w_mod_g, w_in_g = gathered[0], gathered[1]
    p_a_g, p_b_g, w_out_g = (g.reshape(D, D) for g in gathered[2:5])
    nar = jnp.transpose(gathered[5], (1, 0, 2)).reshape(-1, D)
    lbl, cw, br, bi, lam_f = nar[0:4].reshape(2, 2, D), nar[4:8], nar[8:10], nar[10:12], nar[12:14]

    loss, gx, big, small = _local_step(
        x[0], c[0], ctx[0], c_ctx, loss_target[0], w_mod_g, b_mod, w_in_g, b_in, lbl, norm_a_g, cw, conv_b,
        w_r[0], br, w_i[0], bi, lam_f, p_a_g, p_b_g, w_out_g, ln_g, ln_b)
    loss = lax.psum(loss, ("x", "y", "c"))

    g_big = dict(zip(_BIG, _reduce_scatter([big[k] for k in _BIG])))
    sizes = [small[k].shape[0] for k in _SMALL]
    red = _allreduce_small(_pad_rows(jnp.concatenate([_pad_rows(small[k], 8) for k in _SMALL], axis=0), 64))
    grads = {}
    off = 0
    for k, n in zip(_SMALL, sizes):
        g = red[off:off + n]
        off += n + (-n) % 8
        if k == "norm_a_g":
            g = g[:, :DH]
        if k in _COL_SHARDED:
            g = lax.dynamic_slice_in_dim(g, shard * cs, cs, axis=1)
        grads[k] = g.reshape(w[k].shape)
    for k in _BIG:
        grads[k] = g_big[k].reshape(w[k].shape)

    delta, new_m, new_v = {}, {}, {}
    for k in _BIG:
        shp = w[k].shape
        two = lambda a: a.reshape(shp[-2], shp[-1])
        d_, m_, v_ = _adamw(two(w[k]), two(grads[k]), two(m[k]), two(v[k]), f"adamw_{k}")
        delta[k], new_m[k], new_v[k] = d_.reshape(shp), m_.reshape(shp), v_.reshape(shp)
    flat = lambda t: jnp.concatenate([_pad_rows(_rows(t[k]), 8) for k in _SMALL], axis=0)
    d_, m_, v_ = _adamw(flat(w), flat(grads), flat(m), flat(v), "adamw_small")
    off = 0
    for k in _SMALL:
        n = w[k].size
        nr = -(-n // D)
        for src, dst in ((d_, delta), (m_, new_m), (v_, new_v)):
            dst[k] = src[off:off + nr].reshape(-1)[:n].reshape(w[k].shape)
        off += nr + (-nr) % 8

    return (loss, gx[None], *[grads[k] for k in _WEIGHTS], *[delta[k] for k in _WEIGHTS],
            *[new_m[k] for k in _WEIGHTS], *[new_v[k] for k in _WEIGHTS])
```

```python
import functools

import jax
import jax.numpy as jnp
from jax import lax
from jax.experimental import pallas as pl
from jax.experimental.pallas import tpu as pltpu

F32 = jnp.float32
BF = jnp.bfloat16
MESH = pl.DeviceIdType.MESH

D = 1024
H = 8
DH = 128
CH = 64
RB = 256
NCK = RB // CH
GRID_W = 64
CB = 8
RCH = 16
IN_COLS = 9 * D
NSH = 4
SHC = IN_COLS // NSH
RG_C = 8.0
ALPHA = 2.0 ** 0.25
LN_EPS = 1e-5
RMS_EPS = 1e-6
Q_SCALE = DH ** -0.5
ADAM_LR, ADAM_B1, ADAM_B2, ADAM_EPS, ADAM_WD, ADAM_STEP = 0.001, 0.9, 0.999, 1e-08, 0.01, 10
VMEM_LIMIT = 56 * 1024 * 1024
VMEM_LIMIT_MERGE = 60 * 1024 * 1024


def _cp(sem=None, vmem=VMEM_LIMIT):
    return pltpu.CompilerParams(dimension_semantics=sem, vmem_limit_bytes=vmem)


def _sig(x):
    return 0.5 * jnp.tanh(0.5 * x) + 0.5


def _dot(a, b):
    return jnp.dot(a.astype(BF), b.astype(BF), preferred_element_type=F32)


def _dot_nt(a, b):
    return lax.dot_general(a.astype(BF), b.astype(BF), (((1,), (1,)), ((), ())), preferred_element_type=F32)


def _dot_tn(a, b):
    return lax.dot_general(a.astype(BF), b.astype(BF), (((0,), (0,)), ((), ())), preferred_element_type=F32)


def _dot_hi(a, b):
    return jnp.dot(a, b, preferred_element_type=F32, precision=lax.Precision.HIGHEST)


def _colsum(v):
    return jnp.sum(v, axis=0, keepdims=True)


def _mod_fwd(cc, w_mod_g, b_mod):
    def body(cc_ref, w_ref, b_ref, o_ref):
        v = cc_ref[...]
        s = v * _sig(v)
        for k in range(NSH):
            cs = slice(k * 768, (k + 1) * 768)
            o_ref[:, cs] = _dot(s, w_ref[k]) + b_ref[:, cs]
    return pl.pallas_call(body, out_shape=jax.ShapeDtypeStruct((16, 3 * D), F32), name="mod_fwd",
                          compiler_params=_cp())(cc, w_mod_g, b_mod)


def _mod_bwd(cc, dmod, w_mod_g):
    def body(cc_ref, dm_ref, w_ref, dw_ref, dcc_ref):
        v = cc_ref[...]
        sg = _sig(v)
        s = v * sg
        ds = jnp.zeros((16, D), F32)
        for k in range(NSH):
            dm = dm_ref[:, k * 768:(k + 1) * 768]
            dw_ref[k] = _dot_tn(s, dm)
            ds = ds + _dot_nt(dm, w_ref[k])
        dcc_ref[...] = ds * (sg * (1.0 + v * (1.0 - sg)))
    return pl.pallas_call(body, out_shape=(jax.ShapeDtypeStruct((NSH, D, 768), F32),
                                           jax.ShapeDtypeStruct((16, D), F32)),
                          name="mod_bwd", compiler_params=_cp())(cc, dmod, w_mod_g)


def _modulate(x_ref, mod_ref, is_lat):
    sh = jnp.where(is_lat, mod_ref[0:1, 0:D], mod_ref[1:2, 0:D])
    sc = jnp.where(is_lat, mod_ref[0:1, D:2 * D], mod_ref[1:2, D:2 * D])
    return x_ref[...] * (1.0 + sc) + sh, sc


def _inproj(x_all, mod, w_in_g, b_in, n_lat_tiles):
    m = x_all.shape[0]
    tm = 256
    nt = m // tm

    def body(x_ref, mod_ref, w_ref, b_ref, z_ref):
        i = pl.program_id(1)
        u, _ = _modulate(x_ref, mod_ref, i < n_lat_tiles)
        z_ref[...] = _dot(u, w_ref[0]) + b_ref[...]

    return pl.pallas_call(
        body, grid=(NSH, nt),
        in_specs=[pl.BlockSpec((tm, D), lambda n, i: (i, 0)),
                  pl.BlockSpec((16, 3 * D), lambda n, i: (0, 0)),
                  pl.BlockSpec((1, D, SHC), lambda n, i: (n, 0, 0)),
                  pl.BlockSpec((1, SHC), lambda n, i: (0, n))],
        out_specs=pl.BlockSpec((tm, SHC), lambda n, i: (i, n)),
        out_shape=jax.ShapeDtypeStruct((m, IN_COLS), F32), name="inproj",
        compiler_params=_cp(("arbitrary", "arbitrary")))(x_all, mod, w_in_g, b_in)


def _gla_mask(d):
    row = lax.broadcasted_iota(jnp.int32, (CH, CH), 0)
    col = lax.broadcasted_iota(jnp.int32, (CH, CH), 1)
    return (row <= col) if d else (row >= col)


def _chunk_cumsum(v, rev):
    n = v.shape[0]
    pos = lax.broadcasted_iota(jnp.int32, v.shape, 0) & (CH - 1)
    s = 1
    while s < CH:
        if rev:
            v = v + jnp.where(pos < CH - s, pltpu.roll(v, n - s, 0), 0.0)
        else:
            v = v + jnp.where(pos >= s, pltpu.roll(v, s, 0), 0.0)
        s *= 2
    return v


def _chunk_rows(c):
    return slice(c * CH, (c + 1) * CH)


def _gla_features(zq, zf, lb, d):
    sq = _sig(zq)
    q = zq * sq * Q_SCALE
    sf = _sig(zf)
    f = lb + (1.0 - lb) * sf
    k = 1.0 - f
    g = _chunk_cumsum(jnp.log(f), d)
    last = 0 if d else CH - 1
    gls = [g[c * CH + last:c * CH + last + 1, :] for c in range(NCK)]
    glb = jnp.concatenate([jnp.broadcast_to(gl, (CH, D)) for gl in gls], axis=0)
    eg, eig, eeg = jnp.exp(g), jnp.exp(-g), jnp.exp(glb - g)
    decs = [jnp.exp(gl) for gl in gls]
    return sq, sf, f, q * eg, k * eig, k * eeg, eg, eig, eeg, decs


def _lower_bound(lbl_ref, d):
    return _sig(lbl_ref[0, d:d + 1, :] - lbl_ref[1, d:d + 1, :])


def _gla_rb(d, nlb, ncb):
    nrb = nlb + ncb
    if d:
        return lambda s: nrb - 1 - s
    return lambda s: jnp.where(s < ncb, nlb + s, s - ncb)


def _gla_fwd(z, lbl, d, nlb, ncb):
    m = z.shape[0]
    nrb = nlb + ncb
    rb = _gla_rb(d, nlb, ncb)

    def body(q_ref, f_ref, v_ref, lbl_ref, o_ref, st_ref, S):
        s = pl.program_id(0)

        @pl.when(s == 0)
        def _():
            S[...] = jnp.zeros_like(S)

        lb = _lower_bound(lbl_ref, d)
        mb = _gla_mask(d)
        _, _, _, qd, ki, ke, _, _, _, decs = _gla_features(q_ref[...], f_ref[...], lb, d)
        qd, ki, ke, v = qd.astype(BF), ki.astype(BF), ke.astype(BF), v_ref[...].astype(BF)
        order = range(NCK - 1, -1, -1) if d else range(NCK)
        for h in range(H):
            hs = slice(h * DH, (h + 1) * DH)
            intra, upd = {}, {}
            for c in range(NCK):
                rs = _chunk_rows(c)
                a = jnp.where(mb, _dot_nt(qd[rs, hs], ki[rs, hs]), 0.0)
                intra[c] = _dot(a, v[rs, hs])
                upd[c] = _dot_tn(v[rs, hs], ke[rs, hs])
            st = S[h]
            for c in order:
                rs = _chunk_rows(c)
                st_ref[c, h] = st
                o_ref[rs, hs] = intra[c] + _dot_nt(qd[rs, hs], st)
                st = st * decs[c][:, hs] + upd[c]
            S[h] = st

    def zspec(cb):
        return pl.BlockSpec((RB, D), lambda s: (rb(s), cb))

    return pl.pallas_call(
        body, grid=(nrb,),
        in_specs=[zspec(0), zspec(1 + d), zspec(3), pl.BlockSpec((2, 2, D), lambda s: (0, 0, 0))],
        out_specs=(pl.BlockSpec((RB, D), lambda s: (rb(s), 0)),
                   pl.BlockSpec((NCK, H, DH, DH), lambda s: (rb(s), 0, 0, 0))),
        out_shape=(jax.ShapeDtypeStruct((m, D), F32),
                   jax.ShapeDtypeStruct((m // CH, H, DH, DH), F32)),
        scratch_shapes=[pltpu.VMEM((H, DH, DH), F32)],
        name=f"gla_fwd{d}", compiler_params=_cp(("arbitrary",)))(z, z, z, lbl)


def _gla_bwd(z, lbl, do_lat, states, d, nlb, ncb, prev=None):
    m = z.shape[0]
    nrb = nlb + ncb
    fwd_rb = _gla_rb(d, nlb, ncb)
    rb = lambda s: fwd_rb(nrb - 1 - s)
    last = prev is not None

    def body(*refs):
        if last:
            q_ref, f_ref, v_ref, lbl_ref, do_ref, st_ref, pq_ref, pv_ref, o0_ref, o1_ref, o2_ref, sum_ref, dS = refs
        else:
            q_ref, f_ref, v_ref, lbl_ref, do_ref, st_ref, o0_ref, o1_ref, o2_ref, sum_ref, dS = refs
        s = pl.program_id(0)
        is_lat = rb(s) < nlb

        @pl.when(s == 0)
        def _():
            dS[...] = jnp.zeros_like(dS)
            sum_ref[...] = jnp.zeros_like(sum_ref)

        lb = _lower_bound(lbl_ref, d)
        mb = _gla_mask(d)
        zq = q_ref[...]
        sq, sf, f, qd, ki, ke, eg, eig, eeg, decs = _gla_features(zq, f_ref[...], lb, d)
        qdb, kib, keb, vb = qd.astype(BF), ki.astype(BF), ke.astype(BF), v_ref[...].astype(BF)
        dob = jnp.where(is_lat, do_ref[...], 0.0).astype(BF)
        order = range(NCK) if d else range(NCK - 1, -1, -1)
        dqd_h, dki_h, dke_h, dv_h, ddec_h = [], [], [], [], []
        for h in range(H):
            hs = slice(h * DH, (h + 1) * DH)
            dqd, dki, inc, dvi = {}, {}, {}, {}
            for c in range(NCK):
                rs = _chunk_rows(c)
                a = jnp.where(mb, _dot_nt(qdb[rs, hs], kib[rs, hs]), 0.0)
                da = jnp.where(mb, _dot_nt(dob[rs, hs], vb[rs, hs]), 0.0)
                dqd[c] = _dot(da, kib[rs, hs]) + _dot(dob[rs, hs], st_ref[c, h])
                dki[c] = _dot_tn(da, qdb[rs, hs])
                inc[c] = _dot_tn(dob[rs, hs], qdb[rs, hs])
                dvi[c] = _dot_tn(a, dob[rs, hs])
            dst = dS[h]
            dke, dv, ddec = {}, {}, {}
            for c in order:
                rs = _chunk_rows(c)
                dv[c] = dvi[c] + _dot_nt(keb[rs, hs], dst)
                dke[c] = _dot(vb[rs, hs], dst)
                ddec[c] = _colsum(st_ref[c, h] * dst)
                dst = inc[c] + dst * decs[c][:, hs]
            dS[h] = dst
            cat = lambda t: jnp.concatenate([t[c] for c in range(NCK)], axis=0)
            dqd_h.append(cat(dqd))
            dki_h.append(cat(dki))
            dke_h.append(cat(dke))
            dv_h.append(cat(dv))
            ddec_h.append([ddec[c] for c in range(NCK)])
        lanes = lambda parts: jnp.concatenate(parts, axis=1)
        dqd, dki, dke, dv = lanes(dqd_h), lanes(dki_h), lanes(dke_h), lanes(dv_h)
        dq = dqd * eg
        dk = dki * eig + dke * eeg
        dke_ke = dke * ke
        dg = dqd * qd - dki * ki - dke_ke
        dgl = [_colsum(dke_ke[_chunk_rows(c), :]) + lanes([ddec_h[h][c] for h in range(H)]) * decs[c]
               for c in range(NCK)]
        dglb = jnp.concatenate([jnp.broadcast_to(t, (CH, D)) for t in dgl], axis=0)
        df = (_chunk_cumsum(dg, 1 - d) + dglb) / f - dk
        dzf = df * (1.0 - lb) * (sf * (1.0 - sf))
        sum_ref[0:1, :] += _colsum(dzf)
        sum_ref[1:2, :] += _colsum(df * (1.0 - sf))
        if last:
            dz0 = (dq + pq_ref[...]) * (Q_SCALE * (sq * (1.0 + zq * (1.0 - sq))))
            dz3 = dv + pv_ref[...]
            sum_ref[2:3, :] += _colsum(dz0)
            sum_ref[3:4, :] += _colsum(dz3)
            o0_ref[...] = dz0.astype(BF)
            o1_ref[...] = dz3.astype(BF)
        else:
            o0_ref[...] = dq
            o1_ref[...] = dv
        o2_ref[...] = dzf.astype(BF)

        @pl.when(s == nrb - 1)
        def _():
            sum_ref[1:2, :] = sum_ref[1:2, :] * (lb * (1.0 - lb))

    def zspec(cb):
        return pl.BlockSpec((RB, D), lambda s: (rb(s), cb))

    rowspec = pl.BlockSpec((RB, D), lambda s: (rb(s), 0))
    in_specs = [zspec(0), zspec(1 + d), zspec(3), pl.BlockSpec((2, 2, D), lambda s: (0, 0, 0)),
                pl.BlockSpec((RB, D), lambda s: (jnp.minimum(rb(s), nlb - 1), 0)),
                pl.BlockSpec((NCK, H, DH, DH), lambda s: (rb(s), 0, 0, 0))]
    args = [z, z, z, lbl, do_lat, states]
    if last:
        in_specs += [rowspec, rowspec]
        args += list(prev)
    dt01 = BF if last else F32
    return pl.pallas_call(
        body, grid=(nrb,), in_specs=in_specs,
        out_specs=(rowspec, rowspec, rowspec, pl.BlockSpec((8, D), lambda s: (0, 0))),
        out_shape=(jax.ShapeDtypeStruct((m, D), dt01), jax.ShapeDtypeStruct((m, D), dt01),
                   jax.ShapeDtypeStruct((m, D), BF), jax.ShapeDtypeStruct((8, D), F32)),
        scratch_shapes=[pltpu.VMEM((H, DH, DH), F32)],
        name=f"gla_bwd{d}", compiler_params=_cp(("arbitrary",)))(*args)


def _shift_rows(v, k, r):
    row = lax.broadcasted_iota(jnp.int32, v.shape, 0)
    rolled = pltpu.roll(v, k % r, 0)
    return jnp.where((row >= k) & (row < r + k), rolled, 0.0)


def _conv_fwd(xb, cw_ref, cb_ref, r):
    return (cb_ref[...] + _shift_rows(xb, 1, r) * cw_ref[0:1, :] + xb * cw_ref[1:2, :]
            + _shift_rows(xb, -1, r) * cw_ref[2:3, :] + _shift_rows(xb, -2, r) * cw_ref[3:4, :])


def _window(ref, lo, n, rows):
    parts = []
    if lo < 0:
        parts.append(jnp.zeros((-lo,) + tuple(ref.shape[1:]), F32))
    parts.append(ref[max(lo, 0):min(lo + n, rows)])
    if lo + n > rows:
        parts.append(jnp.zeros((lo + n - rows,) + tuple(ref.shape[1:]), F32))
    return parts[0] if len(parts) == 1 else jnp.concatenate(parts, axis=0)


def _conv_cols(x_ref, cw_ref, cb_ref, r0, n, rows):
    w = _window(x_ref, r0 - 1, n + 3, rows)
    return (cb_ref[...] + w[0:n] * cw_ref[0:1, :] + w[1:n + 1] * cw_ref[1:2, :] + w[2:n + 2] * cw_ref[2:3, :]
            + w[3:n + 3] * cw_ref[3:4, :])


def _gate_a(xc, wr_ref, br, sp):
    xcb = xc.astype(BF)
    rs = [jnp.dot(xcb[:, g * DH:(g + 1) * DH], wr_ref[g].astype(BF), preferred_element_type=F32) for g in range(H)]
    return jnp.exp((-RG_C * _sig(jnp.concatenate(rs, axis=1) + br)) * sp)


def _softplus_neg(lam):
    y = jnp.exp(-jnp.abs(lam))
    u = 1.0 + y
    tiny = u == 1.0
    l1p = jnp.where(tiny, y, jnp.log(u) * (y / jnp.where(tiny, 1.0, u - 1.0)))
    return jnp.maximum(-lam, 0.0) + l1p


def _gates(xc, wr_ref, br, wi_ref, bi, sp):
    xcb = xc.astype(BF)
    rs, is_ = [], []
    for g in range(H):
        gs = slice(g * DH, (g + 1) * DH)
        rs.append(jnp.dot(xcb[:, gs], wr_ref[g].astype(BF), preferred_element_type=F32))
        is_.append(jnp.dot(xcb[:, gs], wi_ref[g].astype(BF), preferred_element_type=F32))
    r = _sig(jnp.concatenate(rs, axis=1) + br)
    i = _sig(jnp.concatenate(is_, axis=1) + bi)
    log_a = (-RG_C * r) * sp
    a = jnp.exp(log_a)
    t = jnp.tanh(log_a)
    om = (-2.0 * t) / (1.0 - t)
    return r, i, a, om


def _scan_rows(d, nrows, a_s, b_s, h_s, h0):
    nsl = nrows // 8

    def slab(j, h):
        jj = (nsl - 1 - j) if d else j
        r0 = pl.multiple_of(jj * 8, 8)
        for t in (range(7, -1, -1) if d else range(8)):
            h = a_s[pl.ds(r0 + t, 1), :] * h + b_s[pl.ds(r0 + t, 1), :]
            h_s[pl.ds(r0 + t, 1), :] = h
        return h

    return lax.fori_loop(0, nsl, slab, h0)


def _col_of(d, ncols):
    if d:
        return lambda s: ncols - jnp.maximum(s, 1)
    return lambda s: jnp.maximum(s, 1) - 1


def _rglru_fwd(z, cw, cb, wr, br, wi, bi, lam, d, t_lat, t_ctx):
    m = z.shape[0]
    rows = t_lat // GRID_W
    z3 = z.reshape(m // GRID_W, GRID_W, IN_COLS)
    nblk = GRID_W // CB
    blk = _col_of(d, nblk)
    cblk = t_lat // t_ctx
    rc = min(RCH, rows)

    def body(zc_ref, zx_ref, cw_ref, cb_ref, wr_ref, br_ref, wi_ref, bi_ref, lam_ref,
             hx_ref, hpx_ref, hpc_ref, a_s, b_s, h_s, hcar, a3, b3, cin_s):
        s = pl.program_id(0)
        sp = _softplus_neg(lam_ref[d:d + 1, :])
        br_ = br_ref[d:d + 1, :]
        bi_ = bi_ref[d:d + 1, :]

        @pl.when(s == 0)
        def _():
            xc = _conv_fwd(zc_ref[...], cw_ref, cb_ref, t_ctx)
            _, i, a, om = _gates(xc, wr_ref, br_, wi_ref, bi_, sp)
            a_s[...] = a
            b_s[...] = jnp.sqrt(om) * (i * xc)
            h0 = jnp.zeros((1, D), F32)
            hcar[0:1, :] = _scan_rows(d, t_ctx, a_s, b_s, h_s, h0)
            hs = h_s[...]
            row = lax.broadcasted_iota(jnp.int32, (t_ctx, D), 0)
            if d:
                hpc_ref[...] = jnp.where(row == t_ctx - 1, h0, pltpu.roll(hs, t_ctx - 1, 0))
            else:
                hpc_ref[...] = jnp.where(row == 0, h0, pltpu.roll(hs, 1, 0))

        @pl.when(s > 0)
        def _():
            for r0 in range(0, rows, rc):
                xc = _conv_cols(zx_ref, cw_ref, cb_ref, r0, rc, rows).reshape(rc * CB, D)
                _, i, a, om = _gates(xc, wr_ref, br_, wi_ref, bi_, sp)
                a3[r0:r0 + rc] = a.reshape(rc, CB, D)
                b3[r0:r0 + rc] = (jnp.sqrt(om) * (i * xc)).reshape(rc, CB, D)

            def local(t, carry):
                hl, p = carry
                r = (rows - 1 - t) if d else t
                a = a3[r]
                hl = a * hl + b3[r]
                p = a * p
                b3[r] = hl
                a3[r] = p
                return hl, p

            hl, p = lax.fori_loop(0, rows, local, (jnp.zeros((CB, D), F32), jnp.ones((CB, D), F32)))
            cin = hcar[0:1, :]
            for j in (range(CB - 1, -1, -1) if d else range(CB)):
                cin_s[j:j + 1, :] = cin
                cin = hl[j:j + 1, :] + p[j:j + 1, :] * cin
            hcar[0:1, :] = cin
            c_in = cin_s[...]

            def fix(t, prev):
                r = (rows - 1 - t) if d else t
                h = b3[r] + a3[r] * c_in
                hx_ref[r] = h
                hpx_ref[r] = prev
                return h

            lax.fori_loop(0, rows, fix, c_in)

    full = lambda shp: pl.BlockSpec(shp, lambda s: (0,) * len(shp))
    colspec = pl.BlockSpec((rows, CB, D), lambda s: (0, blk(s), 0))
    outs = pl.pallas_call(
        body, grid=(nblk + 1,),
        in_specs=[pl.BlockSpec((t_ctx, D), lambda s: (cblk, 5)),
                  pl.BlockSpec((rows, CB, D), lambda s: (0, blk(s), 5)),
                  full((4, D)), full((1, D)),
                  pl.BlockSpec((None, H, DH, DH), lambda s: (d, 0, 0, 0)), full((2, D)),
                  pl.BlockSpec((None, H, DH, DH), lambda s: (d, 0, 0, 0)), full((2, D)), full((2, D))],
        out_specs=(colspec, colspec, full((t_ctx, D))),
        out_shape=(jax.ShapeDtypeStruct((rows, GRID_W, D), F32), jax.ShapeDtypeStruct((rows, GRID_W, D), F32),
                   jax.ShapeDtypeStruct((t_ctx, D), F32)),
        scratch_shapes=[pltpu.VMEM((t_ctx, D), F32), pltpu.VMEM((t_ctx, D), F32), pltpu.VMEM((t_ctx, D), F32),
                        pltpu.VMEM((8, D), F32), pltpu.VMEM((rows, CB, D), F32), pltpu.VMEM((rows, CB, D), F32),
                        pltpu.VMEM((CB, D), F32)],
        name=f"rglru_fwd{d}", compiler_params=_cp(("arbitrary",)))(z, z3, cw, cb, wr, br, wi, bi, lam)
    return outs[0].reshape(t_lat, D), outs[1].reshape(t_lat, D), outs[2]


def _rglru_bwd(z, cw, cb, wr, br, wi, bi, lam, dh_lat, hp_lat, hp_ctx, d, t_lat, t_ctx, prev=None):
    m = z.shape[0]
    rows = t_lat // GRID_W
    z3 = z.reshape(m // GRID_W, GRID_W, IN_COLS)
    nblk = GRID_W // CB
    fblk = _col_of(d, nblk)
    blk = lambda s: fblk(nblk - jnp.minimum(s, nblk - 1))
    cblk = t_lat // t_ctx
    rc = min(RCH, rows)
    last = prev is not None
    view3 = lambda v: v.reshape(rows, GRID_W, D)

    def body(*refs):
        (zc_ref, zx_ref, cw_ref, cb_ref, wr_ref, br_ref, wi_ref, bi_ref, lam_ref, dh_ref, hpx_ref, hpc_ref) = refs[:12]
        k = 12
        if last:
            pdx_ref, pdc_ref = refs[12:14]
            k = 14
        ox_ref, oc_ref, dwr_ref, dwi_ref, sum_ref, a_s, b_s, gcar, a3, b3, cin_s = refs[k:]
        s = pl.program_id(0)
        lam_d = lam_ref[d:d + 1, :]
        sp = _softplus_neg(lam_d)
        br_ = br_ref[d:d + 1, :]
        bi_ = bi_ref[d:d + 1, :]
        flat = lambda v: v.reshape(-1, D)

        @pl.when(s == 0)
        def _():
            gcar[...] = jnp.zeros_like(gcar)
            dwr_ref[...] = jnp.zeros_like(dwr_ref)
            dwi_ref[...] = jnp.zeros_like(dwi_ref)
            sum_ref[...] = jnp.zeros_like(sum_ref)


        def conv_sums(dxc, dxb, xm1, x0, xp1, xp2):
            sum_ref[3:4, :] += _colsum(flat(dxc))
            sum_ref[4:5, :] += _colsum(flat(dxb))
            sum_ref[8:9, :] += _colsum(flat(dxc * xm1))
            sum_ref[9:10, :] += _colsum(flat(dxc * x0))
            sum_ref[10:11, :] += _colsum(flat(dxc * xp1))
            sum_ref[11:12, :] += _colsum(flat(dxc * xp2))

        def gate_grads(g, hp, xc, rr, i, a, om):
            mult = jnp.sqrt(om)
            da = g * hp
            ixc = i * xc
            dmult = g * ixc
            dixc = g * mult
            di = dixc * xc
            dxc = dixc * i
            dlog_a = da * a - dmult * ((1.0 - om) / mult)
            dr = dlog_a * (-RG_C * sp)
            sum_ref[2:3, :] += _colsum(dlog_a * rr)
            drp = dr * rr * (1.0 - rr)
            dip = di * i * (1.0 - i)
            sum_ref[0:1, :] += _colsum(drp)
            sum_ref[1:2, :] += _colsum(dip)
            xcb = xc.astype(BF)
            drb = drp.astype(BF)
            dib = dip.astype(BF)
            parts = []
            for gi in range(H):
                gs = slice(gi * DH, (gi + 1) * DH)
                parts.append(_dot_nt(drb[:, gs], wr_ref[gi]) + _dot_nt(dib[:, gs], wi_ref[gi]))
                dwr_ref[gi] += _dot_tn(xcb[:, gs], drb[:, gs])
                dwi_ref[gi] += _dot_tn(xcb[:, gs], dib[:, gs])
            return dxc + jnp.concatenate(parts, axis=1)

        @pl.when(s < nblk)
        def _():
            for r0 in range(0, rows, rc):
                xc = flat(_conv_cols(zx_ref, cw_ref, cb_ref, r0, rc, rows))
                a = _gate_a(xc, wr_ref, br_, sp).reshape(rc, CB, D)
                a3[r0:r0 + rc] = a
                b3[r0:r0 + rc] = a * dh_ref[r0:r0 + rc]

            def local(t, carry):
                c, q = carry
                r = t if d else (rows - 1 - t)
                a = a3[r]
                c = a * c + b3[r]
                q = a * q
                b3[r] = c
                a3[r] = q
                return c, q

            c, q = lax.fori_loop(0, rows, local, (jnp.zeros((CB, D), F32), jnp.ones((CB, D), F32)))
            cin = gcar[0:1, :]
            for j in (range(CB) if d else range(CB - 1, -1, -1)):
                cin_s[j:j + 1, :] = cin
                cin = c[j:j + 1, :] + q[j:j + 1, :] * cin
            gcar[0:1, :] = cin
            c_in = cin_s[...]
            for r0 in (range(rows - rc, -1, -rc) if d else range(0, rows, rc)):
                if d:
                    lo = max(r0 - 1, 0)
                    cn = b3[lo:r0 + rc - 1] + a3[lo:r0 + rc - 1] * c_in
                    if r0 == 0:
                        cn = jnp.concatenate([c_in[None], cn], axis=0)
                else:
                    hi = min(r0 + rc + 1, rows)
                    cn = b3[r0 + 1:hi] + a3[r0 + 1:hi] * c_in
                    if hi == rows:
                        cn = jnp.concatenate([cn, c_in[None]], axis=0)
                g = flat(dh_ref[r0:r0 + rc] + cn)
                xc = flat(_conv_cols(zx_ref, cw_ref, cb_ref, r0, rc, rows))
                rr, i, a, om = _gates(xc, wr_ref, br_, wi_ref, bi_, sp)
                b3[r0:r0 + rc] = gate_grads(g, flat(hpx_ref[r0:r0 + rc]), xc, rr, i, a, om).reshape(rc, CB, D)
            if last:
                for r0 in range(0, rows, rc):
                    b3[r0:r0 + rc] = b3[r0:r0 + rc] + pdx_ref[r0:r0 + rc]
                for r0 in range(0, rows, rc):
                    w = _window(b3, r0 - 2, rc + 3, rows)
                    xw = _window(zx_ref, r0 - 1, rc + 3, rows)
                    dxc = w[2:rc + 2]
                    dxb = (w[3:rc + 3] * cw_ref[0:1, :] + dxc * cw_ref[1:2, :] + w[1:rc + 1] * cw_ref[2:3, :]
                           + w[0:rc] * cw_ref[3:4, :])
                    conv_sums(dxc, dxb, xw[0:rc], xw[1:rc + 1], xw[2:rc + 2], xw[3:rc + 3])
                    ox_ref[r0:r0 + rc] = dxb
            else:
                for r0 in range(0, rows, rc):
                    ox_ref[r0:r0 + rc] = b3[r0:r0 + rc]

        @pl.when(s == nblk)
        def _():
            r = t_ctx
            xb = zc_ref[...]
            xc = _conv_fwd(xb, cw_ref, cb_ref, r)
            rr, i, a, om = _gates(xc, wr_ref, br_, wi_ref, bi_, sp)
            a_s[...] = a
            b_s[...] = jnp.zeros((r, D), F32)
            c0 = gcar[0:1, :]
            _scan_rows(1 - d, r, a_s, b_s, b_s, c0)
            cs = b_s[...]
            row = lax.broadcasted_iota(jnp.int32, (r, D), 0)
            if d:
                g = jnp.where(row == 0, c0, pltpu.roll(cs, 1, 0))
            else:
                g = jnp.where(row == r - 1, c0, pltpu.roll(cs, r - 1, 0))
            dxc = gate_grads(g, hpc_ref[...], xc, rr, i, a, om)
            if last:
                dxc = dxc + pdc_ref[...]
                dxb = (_shift_rows(dxc, -1, r) * cw_ref[0:1, :] + dxc * cw_ref[1:2, :]
                       + _shift_rows(dxc, 1, r) * cw_ref[2:3, :] + _shift_rows(dxc, 2, r) * cw_ref[3:4, :])
                conv_sums(dxc, dxb, _shift_rows(xb, 1, r), xb, _shift_rows(xb, -1, r), _shift_rows(xb, -2, r))
                oc_ref[...] = dxb
            else:
                oc_ref[...] = dxc
            sum_ref[2:3, :] = sum_ref[2:3, :] * (RG_C * _sig(-lam_d))

    full = lambda shp: pl.BlockSpec(shp, lambda s: (0,) * len(shp))
    once = lambda shp: pl.BlockSpec(shp, lambda s: (0,) * len(shp), pipeline_mode=pl.Buffered(1))
    colspec = pl.BlockSpec((rows, CB, D), lambda s: (0, blk(s), 0))
    colonce = pl.BlockSpec((rows, CB, D), lambda s: (0, blk(s), 0), pipeline_mode=pl.Buffered(1))
    in_specs = [pl.BlockSpec((t_ctx, D), lambda s: (cblk, 5), pipeline_mode=pl.Buffered(1)),
                pl.BlockSpec((rows, CB, D), lambda s: (0, blk(s), 5), pipeline_mode=pl.Buffered(1)),
                full((4, D)), full((1, D)),
                pl.BlockSpec((None, H, DH, DH), lambda s: (d, 0, 0, 0)), full((2, D)),
                pl.BlockSpec((None, H, DH, DH), lambda s: (d, 0, 0, 0)), full((2, D)), full((2, D)),
                colonce, colonce, once((t_ctx, D))]
    args = [z, z3, cw, cb, wr, br, wi, bi, lam, view3(dh_lat), view3(hp_lat), hp_ctx]
    if last:
        in_specs += [colonce, once((t_ctx, D))]
        args += [view3(prev[0]), prev[1]]
    outs = pl.pallas_call(
        body, grid=(nblk + 1,), in_specs=in_specs,
        out_specs=(colspec, full((t_ctx, D)), full((H, DH, DH)), full((H, DH, DH)), full((16, D))),
        out_shape=(jax.ShapeDtypeStruct((rows, GRID_W, D), F32), jax.ShapeDtypeStruct((t_ctx, D), F32),
                   jax.ShapeDtypeStruct((H, DH, DH), F32), jax.ShapeDtypeStruct((H, DH, DH), F32),
                   jax.ShapeDtypeStruct((16, D), F32)),
        scratch_shapes=[pltpu.VMEM((t_ctx, D), F32), pltpu.VMEM((t_ctx, D), F32), pltpu.VMEM((8, D), F32),
                        pltpu.VMEM((rows, CB, D), F32), pltpu.VMEM((rows, CB, D), F32), pltpu.VMEM((CB, D), F32)],
        name=f"rglru_bwd{d}", compiler_params=_cp(("arbitrary",)))(*args)
    return (outs[0].reshape(t_lat, D), outs[1]) + tuple(outs[2:])


def _merge(o_f, o_b, h_f, h_b, z, x_all, tgt, mod, norm_g, ln_g, ln_b, p_a, p_b, w_out, t_lat):
    tm = 256
    nt = t_lat // tm

    def body(of_ref, ob_ref, hf_ref, hb_ref, z4_ref, z6_ref, z7_ref, z8_ref, x_ref, t_ref, mod_ref, ng_ref,
             lg_ref, lb_ref, pa_ref, pb_ref, wo_ref,
             do_ref, dh_ref, dz4_ref, dz6_ref, dz7_ref, dz8_ref, gx_ref,
             y_ref, dout_ref, oa_ref, dpa_ref, obv_ref, dpb_ref, acc_ref):
        i = pl.program_id(0)

        @pl.when(i == 0)
        def _():
            acc_ref[...] = jnp.zeros_like(acc_ref)

        def per_head(v):
            return jnp.concatenate(
                [jnp.broadcast_to(jnp.mean(v[:, h * DH:(h + 1) * DH], axis=-1, keepdims=True), (tm, DH))
                 for h in range(H)], axis=1)

        gt = mod_ref[0:1, 2 * D:3 * D]
        gfull = jnp.concatenate([ng_ref[...]] * H, axis=1)
        o = of_ref[...] + ob_ref[...]
        rinv = lax.rsqrt(per_head(o * o) + RMS_EPS)
        n = o * rinv
        na = n * gfull
        z4 = z4_ref[...]
        s4 = _sig(z4)
        silu4 = z4 * s4
        oa = na * silu4
        z6 = z6_ref[...]
        s6 = _sig(z6)
        silu6 = z6 * s6
        hsum = hf_ref[...] + hb_ref[...]
        obv = hsum * silu6
        pa = _dot(oa, pa_ref[...])
        pb = _dot(obv, pb_ref[...])
        s7 = _sig(z7_ref[...])
        s8 = _sig(z8_ref[...])
        y = s7 * pa + s8 * pb
        out = _dot(y, wo_ref[...])
        pre = ALPHA * x_ref[...] + gt * out
        mu = jnp.mean(pre, axis=-1, keepdims=True)
        xc = pre - mu
        rstd = lax.rsqrt(jnp.mean(xc * xc, axis=-1, keepdims=True) + LN_EPS)
        xhat = xc * rstd
        lg = lg_ref[...]
        diff = xhat * lg + lb_ref[...] - t_ref[...]
        acc_ref[8:9, :] += _colsum(diff * diff) * (0.5 / D)
        dxn = diff * (1.0 / D)
        acc_ref[1:2, :] += _colsum(dxn * xhat)
        acc_ref[2:3, :] += _colsum(dxn)
        dxhat = dxn * lg
        dpre = rstd * (dxhat - jnp.mean(dxhat, axis=-1, keepdims=True)
                       - xhat * jnp.mean(dxhat * xhat, axis=-1, keepdims=True))
        gx_ref[...] = ALPHA * dpre
        acc_ref[0:1, :] += _colsum(dpre * out)
        dout = dpre * gt
        dy = _dot_nt(dout, wo_ref[...])
        dpa = dy * s7
        dpb = dy * s8
        dz7 = dy * pa * (s7 * (1.0 - s7))
        dz8 = dy * pb * (s8 * (1.0 - s8))
        doa = _dot_nt(dpa, pa_ref[...])
        dob = _dot_nt(dpb, pb_ref[...])
        dh_ref[...] = dob * silu6
        dz6 = dob * hsum * (s6 * (1.0 + z6 * (1.0 - s6)))
        dna = doa * silu4
        dz4 = doa * na * (s4 * (1.0 + z4 * (1.0 - s4)))
        dng = _colsum(dna * n)
        acc_ref[7:8, 0:DH] += sum(dng[:, h * DH:(h + 1) * DH] for h in range(H))
        dn = dna * gfull
        do_ref[...] = rinv * (dn - n * per_head(dn * n))
        acc_ref[3:4, :] += _colsum(dz4)
        acc_ref[4:5, :] += _colsum(dz6)
        acc_ref[5:6, :] += _colsum(dz7)
        acc_ref[6:7, :] += _colsum(dz8)
        dz4_ref[...] = dz4.astype(BF)
        dz6_ref[...] = dz6.astype(BF)
        dz7_ref[...] = dz7.astype(BF)
        dz8_ref[...] = dz8.astype(BF)
        y_ref[...] = y.astype(BF)
        dout_ref[...] = dout.astype(BF)
        oa_ref[...] = oa.astype(BF)
        dpa_ref[...] = dpa.astype(BF)
        obv_ref[...] = obv.astype(BF)
        dpb_ref[...] = dpb.astype(BF)

        @pl.when(i == nt - 1)
        def _():
            acc_ref[9:10, :] = jnp.broadcast_to(jnp.sum(acc_ref[8:9, :], axis=-1, keepdims=True), (1, D))

    row = pl.BlockSpec((tm, D), lambda i: (i, 0))
    zs = lambda cb: pl.BlockSpec((tm, D), lambda i: (i, cb))
    full = lambda shp: pl.BlockSpec(shp, lambda i: (0,) * len(shp))
    wfull = pl.BlockSpec((D, D), lambda i: (0, 0), pipeline_mode=pl.Buffered(1))
    f32o = jax.ShapeDtypeStruct((t_lat, D), F32)
    bfo = jax.ShapeDtypeStruct((t_lat, D), BF)
    return pl.pallas_call(
        body, grid=(nt,),
        in_specs=[row, row, row, row, zs(4), zs(6), zs(7), zs(8), row, row, full((16, 3 * D)), full((1, DH)),
                  full((1, D)), full((1, D)), wfull, wfull, wfull],
        out_specs=(row,) * 13 + (full((16, D)),),
        out_shape=(f32o, f32o, bfo, bfo, bfo, bfo, f32o, bfo, bfo, bfo, bfo, bfo, bfo,
                   jax.ShapeDtypeStruct((16, D), F32)),
        name="merge", compiler_params=_cp(("arbitrary",), VMEM_LIMIT_MERGE))(
            o_f, o_b, h_f, h_b, z, z, z, z, x_all, tgt, mod, norm_g, ln_g, ln_b, p_a, p_b, w_out)


def _wgrad(a, b, name):
    tm = 1024

    def body(a_ref, b_ref, o_ref):
        @pl.when(pl.program_id(0) == 0)
        def _():
            o_ref[...] = jnp.zeros_like(o_ref)
        o_ref[...] += _dot_tn(a_ref[...], b_ref[...])

    row = pl.BlockSpec((tm, D), lambda i: (i, 0))
    return pl.pallas_call(body, grid=(a.shape[0] // tm,), in_specs=[row, row],
                          out_specs=pl.BlockSpec((D, D), lambda i: (0, 0)),
                          out_shape=jax.ShapeDtypeStruct((D, D), F32), name=name,
                          compiler_params=_cp(("arbitrary",)))(a, b)


def _wgrad_in(x_all, mod, dz, n_lat_tiles):
    m = x_all.shape[0]
    assert m % 128 == 0
    tm = m // 8
    n_lat = n_lat_tiles * 256

    def body(x_ref, mod_ref, dz_ref, o_ref):
        i = pl.program_id(1)

        @pl.when(i == 0)
        def _():
            o_ref[...] = jnp.zeros_like(o_ref)
        is_lat = (i * tm + lax.broadcasted_iota(jnp.int32, (tm, 1), 0)) < n_lat
        sh = jnp.where(is_lat, mod_ref[0:1, 0:D], mod_ref[1:2, 0:D])
        sc = jnp.where(is_lat, mod_ref[0:1, D:2 * D], mod_ref[1:2, D:2 * D])
        o_ref[0] += _dot_tn(x_ref[...] * (1.0 + sc) + sh, dz_ref[...])

    return pl.pallas_call(
        body, grid=(NSH, m // tm),
        in_specs=[pl.BlockSpec((tm, D), lambda n, i: (i, 0)),
                  pl.BlockSpec((16, 3 * D), lambda n, i: (0, 0)),
                  pl.BlockSpec((tm, SHC), lambda n, i: (i, n))],
        out_specs=pl.BlockSpec((1, D, SHC), lambda n, i: (n, 0, 0)),
        out_shape=jax.ShapeDtypeStruct((NSH, D, SHC), F32), name="wgrad_in",
        compiler_params=_cp(("arbitrary", "arbitrary")))(x_all, mod, dz)


def _du(dz, w_in_g, x_all, mod, gxres, n_lat_tiles):
    m = x_all.shape[0]
    tm = 256
    nt = m // tm
    nct = nt - n_lat_tiles
    rblk = lambda i: jnp.where(i < nct, n_lat_tiles + i, i - nct)
    lblk = lambda i: jnp.maximum(i - nct, 0)

    def body(dz_ref, w_ref, x_ref, mod_ref, gr_ref, gx_ref, dm_ref):
        i = pl.program_id(0)
        is_lat = i >= nct

        @pl.when(i == 0)
        def _():
            dm_ref[...] = jnp.zeros_like(dm_ref)

        du = _dot_nt(dz_ref[:, 0:SHC], w_ref[0])
        for n in range(1, NSH):
            du = du + _dot_nt(dz_ref[:, n * SHC:(n + 1) * SHC], w_ref[n])
        sc = jnp.where(is_lat, mod_ref[0:1, D:2 * D], mod_ref[1:2, D:2 * D])
        dsh = _colsum(du)
        dsc = _colsum(du * x_ref[...])

        @pl.when(is_lat)
        def _():
            gx_ref[...] = du * (1.0 + sc) + gr_ref[...]
            dm_ref[0:1, 0:D] += dsh
            dm_ref[0:1, D:2 * D] += dsc

        @pl.when(jnp.logical_not(is_lat))
        def _():
            dm_ref[1:2, 0:D] += dsh
            dm_ref[1:2, D:2 * D] += dsc

    return pl.pallas_call(
        body, grid=(nt,),
        in_specs=[pl.BlockSpec((tm, IN_COLS), lambda i: (rblk(i), 0)),
                  pl.BlockSpec((NSH, D, SHC), lambda i: (0, 0, 0), pipeline_mode=pl.Buffered(1)),
                  pl.BlockSpec((tm, D), lambda i: (rblk(i), 0)),
                  pl.BlockSpec((16, 3 * D), lambda i: (0, 0)),
                  pl.BlockSpec((tm, D), lambda i: (lblk(i), 0))],
        out_specs=(pl.BlockSpec((tm, D), lambda i: (lblk(i), 0)),
                   pl.BlockSpec((8, 2 * D), lambda i: (0, 0))),
        out_shape=(jax.ShapeDtypeStruct((n_lat_tiles * tm, D), F32), jax.ShapeDtypeStruct((8, 2 * D), F32)),
        name="du", compiler_params=_cp(("arbitrary",)))(dz, w_in_g, x_all, mod, gxres)


def _row_tile(rows, cols):
    t = 8
    while t * 2 * cols * 4 <= (1 << 20) and rows % (t * 2) == 0:
        t *= 2
    return t


def _addn(arrs, name):
    shape = arrs[0].shape
    cols = shape[-1]
    flat = [a.reshape(-1, cols) for a in arrs]
    rows = flat[0].shape[0]
    tr = _row_tile(rows, cols)

    def body(*refs):
        acc = refs[0][...]
        for r in refs[1:-1]:
            acc = acc + r[...]
        refs[-1][...] = acc

    spec = pl.BlockSpec((tr, cols), lambda i: (i, 0))
    out = pl.pallas_call(body, grid=(rows // tr,), in_specs=[spec] * len(flat), out_specs=spec,
                         out_shape=jax.ShapeDtypeStruct((rows, cols), F32), name=name,
                         compiler_params=_cp(("arbitrary",)))(*flat)
    return out.reshape(shape)


def _adamw(w, g, m, v, name):
    rows, cols = w.shape
    tr = _row_tile(rows, cols)

    def body(w_ref, g_ref, m_ref, v_ref, d_ref, nm_ref, nv_ref):
        gg = g_ref[...]
        m2 = ADAM_B1 * m_ref[...] + (1.0 - ADAM_B1) * gg
        v2 = ADAM_B2 * v_ref[...] + (1.0 - ADAM_B2) * (gg * gg)
        m_hat = m2 / (1.0 - ADAM_B1 ** ADAM_STEP)
        v_hat = v2 / (1.0 - ADAM_B2 ** ADAM_STEP)
        d_ref[...] = -ADAM_LR * (m_hat / (jnp.sqrt(v_hat) + ADAM_EPS) + ADAM_WD * w_ref[...])
        nm_ref[...] = m2
        nv_ref[...] = v2

    spec = pl.BlockSpec((tr, cols), lambda i: (i, 0))
    o = jax.ShapeDtypeStruct((rows, cols), F32)
    return pl.pallas_call(body, grid=(rows // tr,), in_specs=[spec] * 4, out_specs=(spec,) * 3,
                          out_shape=(o, o, o), name=name, compiler_params=_cp(("arbitrary",)))(w, g, m, v)


_ANY = pl.BlockSpec(memory_space=pl.ANY)


def _place():
    return lax.axis_index("x"), lax.axis_index("y"), lax.axis_index("c")


def _allgather_chips(shards):
    n = len(shards)

    def body(*refs):
        ins, outs = refs[:n], refs[n:2 * n]
        send, recv, lsem = refs[2 * n:]
        x, y, c = _place()
        me = 2 * x + y
        peers = ((1 - x, y, c), (x, 1 - y, c), (1 - x, 1 - y, c))
        copies = []
        for j in range(n):
            cp = pltpu.make_async_copy(ins[j], outs[j].at[me], lsem.at[j])
            cp.start()
            copies.append(cp)
            for k, p in enumerate(peers):
                cp = pltpu.make_async_remote_copy(src_ref=ins[j], dst_ref=outs[j].at[me], send_sem=send.at[3 * j + k],
                                                  recv_sem=recv.at[3 * j + k], device_id=p, device_id_type=MESH)
                cp.start()
                copies.append(cp)
        for cp in copies:
            cp.wait()

    return pl.pallas_call(
        body, in_specs=[_ANY] * n, out_specs=[_ANY] * n,
        out_shape=[jax.ShapeDtypeStruct((NSH,) + s.shape, s.dtype) for s in shards],
        scratch_shapes=[pltpu.SemaphoreType.DMA((3 * n,)), pltpu.SemaphoreType.DMA((3 * n,)),
                        pltpu.SemaphoreType.DMA((n,))],
        name="allgather_weights")(*shards)


def _rs_sibling(grads):
    n = len(grads)

    def body(*refs):
        ins, got = refs[:n], refs[n:2 * n]
        send, recv = refs[2 * n:]
        x, y, c = _place()
        copies = []
        for j in range(n):
            hr = ins[j].shape[1] // 2
            for s in range(NSH):
                give = ins[j].at[s, pl.ds(pl.multiple_of((1 - c) * hr, 8), hr), :]
                cp = pltpu.make_async_remote_copy(src_ref=give, dst_ref=got[j].at[s], send_sem=send.at[NSH * j + s],
                                                  recv_sem=recv.at[NSH * j + s], device_id=(x, y, 1 - c),
                                                  device_id_type=MESH)
                cp.start()
                copies.append(cp)
        for cp in copies:
            cp.wait()

    half = [jax.ShapeDtypeStruct((NSH, g.shape[1] // 2, g.shape[2]), F32) for g in grads]
    return pl.pallas_call(
        body, in_specs=[_ANY] * n, out_specs=[_ANY] * n, out_shape=half,
        scratch_shapes=[pltpu.SemaphoreType.DMA((NSH * n,)), pltpu.SemaphoreType.DMA((NSH * n,))],
        name="rs_sibling")(*grads)


def _core_vec():
    return lax.axis_index("c").astype(jnp.int32).reshape(1)


def _rs_add1(g, got, name):
    _, r, cols = g.shape
    hr = r // 2
    tr = _row_tile(hr, cols)
    nb = hr // tr

    def body(c_ref, g_ref, got_ref, o_ref):
        o_ref[...] = (g_ref[...] + got_ref[...]).astype(BF)

    spec = pl.BlockSpec((1, tr, cols), lambda s, i, c_ref: (s, i, 0))
    return pl.pallas_call(
        body, grid_spec=pltpu.PrefetchScalarGridSpec(
            num_scalar_prefetch=1, grid=(NSH, nb),
            in_specs=[pl.BlockSpec((1, tr, cols), lambda s, i, c_ref: (s, c_ref[0] * nb + i, 0)), spec],
            out_specs=spec),
        out_shape=jax.ShapeDtypeStruct((NSH, hr, cols), BF), name=name,
        compiler_params=_cp(("arbitrary", "arbitrary")))(_core_vec(), g, got)


def _rs_add2(sums, got, name):
    _, hr, cols = sums.shape
    tr = _row_tile(hr, cols)
    nb = hr // tr
    place = jnp.stack([2 * lax.axis_index("x") + lax.axis_index("y"), lax.axis_index("c")]).astype(jnp.int32)

    def body(p_ref, s_ref, got_ref, o_ref):
        f = lambda v: v.astype(F32)
        o_ref[...] = f(s_ref[0]) + f(got_ref[0]) + f(got_ref[1]) + f(got_ref[2])

    return pl.pallas_call(
        body, grid_spec=pltpu.PrefetchScalarGridSpec(
            num_scalar_prefetch=1, grid=(nb,),
            in_specs=[pl.BlockSpec((1, tr, cols), lambda i, p_ref: (p_ref[0], i, 0)),
                      pl.BlockSpec((3, tr, cols), lambda i, p_ref: (0, i, 0))],
            out_specs=pl.BlockSpec((tr, cols), lambda i, p_ref: (p_ref[1] * nb + i, 0))),
        out_shape=jax.ShapeDtypeStruct((2 * hr, cols), F32), name=name,
        compiler_params=_cp(("arbitrary",)))(place, sums, got)


def _rs_chips(sums):
    n = len(sums)

    def body(*refs):
        ins, got = refs[:n], refs[n:2 * n]
        send, recv = refs[2 * n:]
        x, y, c = _place()
        peers = ((1 - x, y), (x, 1 - y), (1 - x, 1 - y))
        copies = []
        for j in range(n):
            for k, (px, py) in enumerate(peers):
                cp = pltpu.make_async_remote_copy(src_ref=ins[j].at[2 * px + py], dst_ref=got[j].at[k],
                                                  send_sem=send.at[3 * j + k], recv_sem=recv.at[3 * j + k],
                                                  device_id=(px, py, c), device_id_type=MESH)
                cp.start()
                copies.append(cp)
        for cp in copies:
            cp.wait()

    three = [jax.ShapeDtypeStruct((3,) + g.shape[1:], g.dtype) for g in sums]
    return pl.pallas_call(
        body, in_specs=[_ANY] * n, out_specs=[_ANY] * n, out_shape=three,
        scratch_shapes=[pltpu.SemaphoreType.DMA((3 * n,)), pltpu.SemaphoreType.DMA((3 * n,))],
        name="rs_chips")(*sums)


def _ag_sibling(fulls):
    n = len(fulls)
    nck = 4

    def body(*refs):
        outs = refs[n:2 * n]
        send, recv = refs[2 * n:]
        x, y, c = _place()
        copies = []
        for j in range(n):
            qr = outs[j].shape[0] // (2 * nck)
            for k in range(nck):
                rows = outs[j].at[pl.ds(pl.multiple_of((c * nck + k) * qr, 8), qr), :]
                cp = pltpu.make_async_remote_copy(src_ref=rows, dst_ref=rows, send_sem=send.at[nck * j + k],
                                                  recv_sem=recv.at[nck * j + k], device_id=(x, y, 1 - c),
                                                  device_id_type=MESH)
                cp.start()
                copies.append(cp)
        for cp in copies:
            cp.wait()

    return pl.pallas_call(
        body, in_specs=[_ANY] * n, out_specs=[_ANY] * n,
        out_shape=[jax.ShapeDtypeStruct(f.shape, F32) for f in fulls],
        input_output_aliases={j: j for j in range(n)},
        scratch_shapes=[pltpu.SemaphoreType.DMA((nck * n,)), pltpu.SemaphoreType.DMA((nck * n,))],
        name="ag_sibling")(*fulls)


def _allreduce_small(buf):
    rows = buf.shape[0]
    pr = rows // 8

    def body(in_ref, out_ref, stage, send1, recv1, send2, recv2):
        x, y, c = _place()
        me = 4 * x + 2 * y + c

        def peer(k):
            kx, ky, kc = (k >> 2) & 1, (k >> 1) & 1, k & 1
            return (x ^ kx, y ^ ky, c ^ kc)

        def piece(ref, idx):
            return ref.at[pl.ds(pl.multiple_of(idx * pr, 8), pr), :]

        copies = []
        for k in range(1, 8):
            px, py, pc = peer(k)
            cp = pltpu.make_async_remote_copy(src_ref=piece(in_ref, 4 * px + 2 * py + pc), dst_ref=stage.at[k],
                                              send_sem=send1.at[k], recv_sem=recv1.at[k],
                                              device_id=(px, py, pc), device_id_type=MESH)
            cp.start()
            copies.append(cp)
        for cp in copies:
            cp.wait()
        acc = piece(in_ref, me)[...]
        for k in range(1, 8):
            acc = acc + stage[k]
        piece(out_ref, me)[...] = acc
        copies = []
        for k in range(1, 8):
            cp = pltpu.make_async_remote_copy(src_ref=piece(out_ref, me), dst_ref=piece(out_ref, me),
                                              send_sem=send2.at[k], recv_sem=recv2.at[k],
                                              device_id=peer(k), device_id_type=MESH)
            cp.start()
            copies.append(cp)
        for cp in copies:
            cp.wait()

    vm = pl.BlockSpec(memory_space=pltpu.VMEM)
    return pl.pallas_call(
        body, in_specs=[vm], out_specs=vm, out_shape=jax.ShapeDtypeStruct((rows, D), F32),
        scratch_shapes=[pltpu.VMEM((8, pr, D), F32)] + [pltpu.SemaphoreType.DMA((8,))] * 4,
        name="allreduce_small", compiler_params=_cp())(buf)


def _reduce_scatter(grads):
    got = _rs_sibling(grads)
    sums = [_rs_add1(g, b, f"rs_add1_{j}") for j, (g, b) in enumerate(zip(grads, got))]
    got = _rs_chips(sums)
    fulls = [_rs_add2(a, b, f"rs_add2_{j}") for j, (a, b) in enumerate(zip(sums, got))]
    return _ag_sibling(fulls)


def _rows(a):
    flat = a.reshape(-1)
    pad = (-flat.shape[0]) % D
    if pad:
        flat = jnp.concatenate([flat, jnp.zeros((pad,), flat.dtype)])
    return flat.reshape(-1, D)


def _pad_rows(a, mult):
    pad = (-a.shape[0]) % mult
    return jnp.concatenate([a, jnp.zeros((pad, a.shape[1]), a.dtype)]) if pad else a


def _local_step(x, c, ctx, c_ctx, tgt, w_mod_g, b_mod, w_in_g, b_in, lbl, norm_g, cw, cb, wr, br, wi, bi, lam,
                p_a, p_b, w_out, ln_g, ln_b):
    t_lat, t_ctx = x.shape[0], ctx.shape[0]
    nlt = t_lat // 256
    nlb, ncb = t_lat // RB, t_ctx // RB
    cc = jnp.zeros((16, D), F32).at[0].set(c).at[1].set(c_ctx)
    mod = _mod_fwd(cc, w_mod_g, b_mod)
    x_all = jnp.concatenate([x, ctx], axis=0)
    z = _inproj(x_all, mod, w_in_g, b_in, nlt)
    o0, st0 = _gla_fwd(z, lbl, 0, nlb, ncb)
    o1, st1 = _gla_fwd(z, lbl, 1, nlb, ncb)
    h0, hp0, hpc0 = _rglru_fwd(z, cw, cb, wr, br, wi, bi, lam, 0, t_lat, t_ctx)
    h1, hp1, hpc1 = _rglru_fwd(z, cw, cb, wr, br, wi, bi, lam, 1, t_lat, t_ctx)
    (do, dh, dz4, dz6, dz7, dz8, gxres, y, dout, oa, dpa, obv, dpb, acc) = _merge(
        o0, o1, h0, h1, z, x_all, tgt, mod, norm_g, ln_g, ln_b, p_a, p_b, w_out, t_lat)
    gp_a = _wgrad(oa, dpa, "wgrad_pa")
    gp_b = _wgrad(obv, dpb, "wgrad_pb")
    gw_out = _wgrad(y, dout, "wgrad_wout")
    dxc_lat, dxc_ctx, dwr0, dwi0, sb0 = _rglru_bwd(z, cw, cb, wr, br, wi, bi, lam, dh, hp0, hpc0, 0, t_lat, t_ctx)
    dz5_lat, dz5_ctx, dwr1, dwi1, sb1 = _rglru_bwd(z, cw, cb, wr, br, wi, bi, lam, dh, hp1, hpc1, 1, t_lat, t_ctx,
                                                   prev=(dxc_lat, dxc_ctx))
    dq0, dv0, dz1, sa0 = _gla_bwd(z, lbl, do, st0, 0, nlb, ncb)
    dz0, dz3, dz2, sa1 = _gla_bwd(z, lbl, do, st1, 1, nlb, ncb, prev=(dq0, dv0))
    zc = jnp.zeros((t_ctx, D), BF)
    pad = lambda a: jnp.concatenate([a, zc], axis=0)
    dz = jnp.concatenate([dz0, dz1, dz2, dz3, pad(dz4), jnp.concatenate([dz5_lat, dz5_ctx], axis=0).astype(BF),
                          pad(dz6), pad(dz7), pad(dz8)], axis=1)
    gw_in = _wgrad_in(x_all, mod, dz, nlt)
    gx, dm = _du(dz, w_in_g, x_all, mod, gxres, nlt)
    dmod = jnp.zeros((16, 3 * D), F32).at[0:2, 0:2 * D].set(dm[0:2]).at[0, 2 * D:].set(acc[0])
    gw_mod, dcc = _mod_bwd(cc, dmod, w_mod_g)
    small = dict(
        c_ctx=dcc[1:2], b_mod=(dmod[0] + dmod[1]).reshape(3, D),
        b_in=jnp.stack([sa1[2], sa0[0], sa1[0], sa1[3], acc[3], sb1[4], acc[4], acc[5], acc[6]]),
        lb_logits=jnp.stack([sa0[1], sa1[1], -sa0[1], -sa1[1]]),
        norm_a_g=acc[7:8], conv_w=sb1[8:12], conv_b=sb1[3:4],
        w_r=jnp.stack([dwr0, dwr1]).reshape(-1, D), w_i=jnp.stack([dwi0, dwi1]).reshape(-1, D),
        b_r=jnp.stack([sb0[0], sb1[0]]), b_i=jnp.stack([sb0[1], sb1[1]]), lam=jnp.stack([sb0[2], sb1[2]]),
        ln_g=acc[1:2], ln_b=acc[2:3])
    big = dict(w_mod=gw_mod, w_in=gw_in, p_a=gp_a.reshape(NSH, D // NSH, D), p_b=gp_b.reshape(NSH, D // NSH, D),
               w_out=gw_out.reshape(NSH, D // NSH, D))
    return acc[9, 0], gx, big, small


_SMALL = ("c_ctx", "b_mod", "b_in", "lb_logits", "norm_a_g", "conv_w", "conv_b", "w_r", "w_i", "b_r", "b_i", "lam",
          "ln_g", "ln_b")
_BIG = ("w_mod", "w_in", "p_a", "p_b", "w_out")
_COL_SHARDED = ("lb_logits", "conv_w", "b_r", "b_i", "lam")
_WEIGHTS = ("c_ctx", "w_mod", "b_mod", "w_in", "b_in", "lb_logits", "norm_a_g", "conv_w", "conv_b", "w_r", "b_r", "w_i",
            "b_i", "lam", "p_a", "p_b", "w_out", "ln_g", "ln_b")


def kernel(x, c, ctx, c_ctx, w_mod, b_mod, w_in, b_in, lb_logits, norm_a_g, conv_w, conv_b, w_r, b_r, w_i, b_i, lam, p_a, p_b, w_out, ln_g, ln_b, loss_target, m_c_ctx, m_w_mod, m_b_mod, m_w_in, m_b_in, m_lb_logits, m_norm_a_g, m_conv_w, m_conv_b, m_w_r, m_b_r, m_w_i, m_b_i, m_lam, m_p_a, m_p_b, m_w_out, m_ln_g, m_ln_b, v_c_ctx, v_w_mod, v_b_mod, v_w_in, v_b_in, v_lb_logits, v_norm_a_g, v_conv_w, v_conv_b, v_w_r, v_b_r, v_w_i, v_b_i, v_lam, v_p_a, v_p_b, v_w_out, v_ln_g, v_ln_b):
    w = dict(c_ctx=c_ctx, w_mod=w_mod, b_mod=b_mod, w_in=w_in, b_in=b_in, lb_logits=lb_logits, norm_a_g=norm_a_g,
             conv_w=conv_w, conv_b=conv_b, w_r=w_r, b_r=b_r, w_i=w_i, b_i=b_i, lam=lam, p_a=p_a, p_b=p_b, w_out=w_out,
             ln_g=ln_g, ln_b=ln_b)
    m = dict(c_ctx=m_c_ctx, w_mod=m_w_mod, b_mod=m_b_mod, w_in=m_w_in, b_in=m_b_in, lb_logits=m_lb_logits,
             norm_a_g=m_norm_a_g, conv_w=m_conv_w, conv_b=m_conv_b, w_r=m_w_r, b_r=m_b_r, w_i=m_w_i, b_i=m_b_i,
             lam=m_lam, p_a=m_p_a, p_b=m_p_b, w_out=m_w_out, ln_g=m_ln_g, ln_b=m_ln_b)
    v = dict(c_ctx=v_c_ctx, w_mod=v_w_mod, b_mod=v_b_mod, w_in=v_w_in, b_in=v_b_in, lb_logits=v_lb_logits,
             norm_a_g=v_norm_a_g, conv_w=v_conv_w, conv_b=v_conv_b, w_r=v_w_r, b_r=v_b_r, w_i=v_w_i, b_i=v_b_i,
             lam=v_lam, p_a=v_p_a, p_b=v_p_b, w_out=v_w_out, ln_g=v_ln_g, ln_b=v_ln_b)
    shard = 2 * lax.axis_index("x") + lax.axis_index("y")
    cs = D // NSH

    narrow = _pad_rows(jnp.concatenate([lb_logits.reshape(4, cs), conv_w[0], b_r[0], b_i[0], lam[0]], axis=0), 8)
    big_shards = [w[k][0].astype(BF) for k in _BIG]
    gathered = _allgather_chips(big_shards + [narrow])
    w_mod_g, w_in_g = gathered[0], gathered[1]
    p_a_g, p_b_g, w_out_g = (g.reshape(D, D) for g in gathered[2:5])
    nar = jnp.transpose(gathered[5], (1, 0, 2)).reshape(-1, D)
    lbl, cw, br, bi, lam_f = nar[0:4].reshape(2, 2, D), nar[4:8], nar[8:10], nar[10:12], nar[12:14]

    loss, gx, big, small = _local_step(
        x[0], c[0], ctx[0], c_ctx, loss_target[0], w_mod_g, b_mod, w_in_g, b_in, lbl, norm_a_g, cw, conv_b,
        w_r[0], br, w_i[0], bi, lam_f, p_a_g, p_b_g, w_out_g, ln_g, ln_b)
    loss = lax.psum(loss, ("x", "y", "c"))

    g_big = dict(zip(_BIG, _reduce_scatter([big[k] for k in _BIG])))
    sizes = [small[k].shape[0] for k in _SMALL]
    red = _allreduce_small(_pad_rows(jnp.concatenate([_pad_rows(small[k], 8) for k in _SMALL], axis=0), 64))
    grads = {}
    off = 0
    for k, n in zip(_SMALL, sizes):
        g = red[off:off + n]
        off += n + (-n) % 8
        if k == "norm_a_g":
            g = g[:, :DH]
        if k in _COL_SHARDED:
            g = lax.dynamic_slice_in_dim(g, shard * cs, cs, axis=1)
        grads[k] = g.reshape(w[k].shape)
    for k in _BIG:
        grads[k] = g_big[k].reshape(w[k].shape)

    delta, new_m, new_v = {}, {}, {}
    for k in _BIG:
        shp = w[k].shape
        two = lambda a: a.reshape(shp[-2], shp[-1])
        d_, m_, v_ = _adamw(two(w[k]), two(grads[k]), two(m[k]), two(v[k]), f"adamw_{k}")
        delta[k], new_m[k], new_v[k] = d_.reshape(shp), m_.reshape(shp), v_.reshape(shp)
    flat = lambda t: jnp.concatenate([_pad_rows(_rows(t[k]), 8) for k in _SMALL], axis=0)
    d_, m_, v_ = _adamw(flat(w), flat(grads), flat(m), flat(v), "adamw_small")
    off = 0
    for k in _SMALL:
        n = w[k].size
        nr = -(-n // D)
        for src, dst in ((d_, delta), (m_, new_m), (v_, new_v)):
            dst[k] = src[off:off + nr].reshape(-1)[:n].reshape(w[k].shape)
        off += nr + (-nr) % 8

    return (loss, gx[None], *[grads[k] for k in _WEIGHTS], *[delta[k] for k in _WEIGHTS],
            *[new_m[k] for k in _WEIGHTS], *[new_v[k] for k in _WEIGHTS])
```

```python
import functools

import jax
import jax.numpy as jnp
from jax import lax
from jax.experimental import pallas as pl
from jax.experimental.pallas import tpu as pltpu

F32 = jnp.float32
BF = jnp.bfloat16
MESH = pl.DeviceIdType.MESH

D = 1024
H = 8
DH = 128
CH = 64
RB = 256
NCK = RB // CH
GRID_W = 64
CB = 8
RCH = 16
IN_COLS = 9 * D
NSH = 4
SHC = IN_COLS // NSH
RG_C = 8.0
ALPHA = 2.0 ** 0.25
LN_EPS = 1e-5
RMS_EPS = 1e-6
Q_SCALE = DH ** -0.5
ADAM_LR, ADAM_B1, ADAM_B2, ADAM_EPS, ADAM_WD, ADAM_STEP = 0.001, 0.9, 0.999, 1e-08, 0.01, 10
VMEM_LIMIT = 56 * 1024 * 1024
VMEM_LIMIT_MERGE = 60 * 1024 * 1024


def _cp(sem=None, vmem=VMEM_LIMIT):
    return pltpu.CompilerParams(dimension_semantics=sem, vmem_limit_bytes=vmem)


def _sig(x):
    return 0.5 * jnp.tanh(0.5 * x) + 0.5


def _dot(a, b):
    return jnp.dot(a.astype(BF), b.astype(BF), preferred_element_type=F32)


def _dot_nt(a, b):
    return lax.dot_general(a.astype(BF), b.astype(BF), (((1,), (1,)), ((), ())), preferred_element_type=F32)


def _dot_tn(a, b):
    return lax.dot_general(a.astype(BF), b.astype(BF), (((0,), (0,)), ((), ())), preferred_element_type=F32)


def _dot_hi(a, b):
    return jnp.dot(a, b, preferred_element_type=F32, precision=lax.Precision.HIGHEST)


def _colsum(v):
    return jnp.sum(v, axis=0, keepdims=True)


def _mod_fwd(cc, w_mod_g, b_mod):
    def body(cc_ref, w_ref, b_ref, o_ref):
        v = cc_ref[...]
        s = v * _sig(v)
        for k in range(NSH):
            cs = slice(k * 768, (k + 1) * 768)
            o_ref[:, cs] = _dot(s, w_ref[k]) + b_ref[:, cs]
    return pl.pallas_call(body, out_shape=jax.ShapeDtypeStruct((16, 3 * D), F32), name="mod_fwd",
                          compiler_params=_cp())(cc, w_mod_g, b_mod)


def _mod_bwd(cc, dmod, w_mod_g):
    def body(cc_ref, dm_ref, w_ref, dcc_ref):
        v = cc_ref[...]
        sg = _sig(v)
        ds = jnp.zeros((16, D), F32)
        for k in range(NSH):
            ds = ds + _dot_nt(dm_ref[:, k * 768:(k + 1) * 768], w_ref[k])
        dcc_ref[...] = ds * (sg * (1.0 + v * (1.0 - sg)))
    return pl.pallas_call(body, out_shape=jax.ShapeDtypeStruct((16, D), F32),
                          name="mod_bwd", compiler_params=_cp())(cc, dmod, w_mod_g)


def _wmod_grad(cc, dmod_cols):
    def body(cc_ref, dm_ref, dw_ref):
        v = cc_ref[...]
        dw_ref[...] = _dot_tn(v * _sig(v), dm_ref[...])
    return pl.pallas_call(body, out_shape=jax.ShapeDtypeStruct((D, dmod_cols.shape[1]), F32),
                          name="wmod_grad", compiler_params=_cp())(cc, dmod_cols)


def _modulate(x_ref, mod_ref, is_lat):
    sh = jnp.where(is_lat, mod_ref[0:1, 0:D], mod_ref[1:2, 0:D])
    sc = jnp.where(is_lat, mod_ref[0:1, D:2 * D], mod_ref[1:2, D:2 * D])
    return x_ref[...] * (1.0 + sc) + sh, sc


def _inproj(x_all, mod, w_in_g, b_in, n_lat_tiles):
    m = x_all.shape[0]
    tm = 256
    nt = m // tm

    def body(x_ref, mod_ref, w_ref, b_ref, z_ref):
        i = pl.program_id(1)
        u, _ = _modulate(x_ref, mod_ref, i < n_lat_tiles)
        z_ref[...] = _dot(u, w_ref[0]) + b_ref[...]

    return pl.pallas_call(
        body, grid=(NSH, nt),
        in_specs=[pl.BlockSpec((tm, D), lambda n, i: (i, 0)),
                  pl.BlockSpec((16, 3 * D), lambda n, i: (0, 0)),
                  pl.BlockSpec((1, D, SHC), lambda n, i: (n, 0, 0)),
                  pl.BlockSpec((1, SHC), lambda n, i: (0, n))],
        out_specs=pl.BlockSpec((tm, SHC), lambda n, i: (i, n)),
        out_shape=jax.ShapeDtypeStruct((m, IN_COLS), F32), name="inproj",
        compiler_params=_cp(("arbitrary", "arbitrary")))(x_all, mod, w_in_g, b_in)


def _gla_mask(d):
    row = lax.broadcasted_iota(jnp.int32, (CH, CH), 0)
    col = lax.broadcasted_iota(jnp.int32, (CH, CH), 1)
    return (row <= col) if d else (row >= col)


def _chunk_cumsum(v, rev):
    n = v.shape[0]
    pos = lax.broadcasted_iota(jnp.int32, v.shape, 0) & (CH - 1)
    s = 1
    while s < CH:
        if rev:
            v = v + jnp.where(pos < CH - s, pltpu.roll(v, n - s, 0), 0.0)
        else:
            v = v + jnp.where(pos >= s, pltpu.roll(v, s, 0), 0.0)
        s *= 2
    return v


def _chunk_rows(c):
    return slice(c * CH, (c + 1) * CH)


def _gla_features(zq, zf, lb, d):
    sq = _sig(zq)
    q = zq * sq * Q_SCALE
    sf = _sig(zf)
    f = lb + (1.0 - lb) * sf
    k = 1.0 - f
    g = _chunk_cumsum(jnp.log(f), d)
    last = 0 if d else CH - 1
    gls = [g[c * CH + last:c * CH + last + 1, :] for c in range(NCK)]
    glb = jnp.concatenate([jnp.broadcast_to(gl, (CH, D)) for gl in gls], axis=0)
    eg, eig, eeg = jnp.exp(g), jnp.exp(-g), jnp.exp(glb - g)
    decs = [jnp.exp(gl) for gl in gls]
    return sq, sf, f, q * eg, k * eig, k * eeg, eg, eig, eeg, decs


def _lower_bound(lbl_ref, d):
    return _sig(lbl_ref[0, d:d + 1, :] - lbl_ref[1, d:d + 1, :])


def _gla_rb(d, nlb, ncb):
    nrb = nlb + ncb
    if d:
        return lambda s: nrb - 1 - s
    return lambda s: jnp.where(s < ncb, nlb + s, s - ncb)


def _gla_fwd(z, lbl, d, nlb, ncb):
    m = z.shape[0]
    nrb = nlb + ncb
    rb = _gla_rb(d, nlb, ncb)

    def body(q_ref, f_ref, v_ref, lbl_ref, o_ref, st_ref, S):
        s = pl.program_id(0)

        @pl.when(s == 0)
        def _():
            S[...] = jnp.zeros_like(S)

        lb = _lower_bound(lbl_ref, d)
        mb = _gla_mask(d)
        _, _, _, qd, ki, ke, _, _, _, decs = _gla_features(q_ref[...], f_ref[...], lb, d)
        qd, ki, ke, v = qd.astype(BF), ki.astype(BF), ke.astype(BF), v_ref[...].astype(BF)
        order = range(NCK - 1, -1, -1) if d else range(NCK)
        for h in range(H):
            hs = slice(h * DH, (h + 1) * DH)
            intra, upd = {}, {}
            for c in range(NCK):
                rs = _chunk_rows(c)
                a = jnp.where(mb, _dot_nt(qd[rs, hs], ki[rs, hs]), 0.0)
                intra[c] = _dot(a, v[rs, hs])
                upd[c] = _dot_tn(v[rs, hs], ke[rs, hs])
            st = S[h]
            for c in order:
                rs = _chunk_rows(c)
                st_ref[c, h] = st
                o_ref[rs, hs] = intra[c] + _dot_nt(qd[rs, hs], st)
                st = st * decs[c][:, hs] + upd[c]
            S[h] = st

    def zspec(cb):
        return pl.BlockSpec((RB, D), lambda s: (rb(s), cb))

    return pl.pallas_call(
        body, grid=(nrb,),
        in_specs=[zspec(0), zspec(1 + d), zspec(3), pl.BlockSpec((2, 2, D), lambda s: (0, 0, 0))],
        out_specs=(pl.BlockSpec((RB, D), lambda s: (rb(s), 0)),
                   pl.BlockSpec((NCK, H, DH, DH), lambda s: (rb(s), 0, 0, 0))),
        out_shape=(jax.ShapeDtypeStruct((m, D), F32),
                   jax.ShapeDtypeStruct((m // CH, H, DH, DH), F32)),
        scratch_shapes=[pltpu.VMEM((H, DH, DH), F32)],
        name=f"gla_fwd{d}", compiler_params=_cp(("arbitrary",)))(z, z, z, lbl)


def _gla_bwd(z, lbl, do_lat, states, d, nlb, ncb, prev=None):
    m = z.shape[0]
    nrb = nlb + ncb
    fwd_rb = _gla_rb(d, nlb, ncb)
    rb = lambda s: fwd_rb(nrb - 1 - s)
    last = prev is not None

    def body(*refs):
        if last:
            q_ref, f_ref, v_ref, lbl_ref, do_ref, st_ref, pq_ref, pv_ref, o0_ref, o1_ref, o2_ref, sum_ref, dS = refs
        else:
            q_ref, f_ref, v_ref, lbl_ref, do_ref, st_ref, o0_ref, o1_ref, o2_ref, sum_ref, dS = refs
        s = pl.program_id(0)
        is_lat = rb(s) < nlb

        @pl.when(s == 0)
        def _():
            dS[...] = jnp.zeros_like(dS)
            sum_ref[...] = jnp.zeros_like(sum_ref)

        lb = _lower_bound(lbl_ref, d)
        mb = _gla_mask(d)
        zq = q_ref[...]
        sq, sf, f, qd, ki, ke, eg, eig, eeg, decs = _gla_features(zq, f_ref[...], lb, d)
        qdb, kib, keb, vb = qd.astype(BF), ki.astype(BF), ke.astype(BF), v_ref[...].astype(BF)
        dob = jnp.where(is_lat, do_ref[...], 0.0).astype(BF)
        order = range(NCK) if d else range(NCK - 1, -1, -1)
        dqd_h, dki_h, dke_h, dv_h, ddec_h = [], [], [], [], []
        for h in range(H):
            hs = slice(h * DH, (h + 1) * DH)
            dqd, dki, inc, dvi = {}, {}, {}, {}
            for c in range(NCK):
                rs = _chunk_rows(c)
                a = jnp.where(mb, _dot_nt(qdb[rs, hs], kib[rs, hs]), 0.0)
                da = jnp.where(mb, _dot_nt(dob[rs, hs], vb[rs, hs]), 0.0)
                dqd[c] = _dot(da, kib[rs, hs]) + _dot(dob[rs, hs], st_ref[c, h])
                dki[c] = _dot_tn(da, qdb[rs, hs])
                inc[c] = _dot_tn(dob[rs, hs], qdb[rs, hs])
                dvi[c] = _dot_tn(a, dob[rs, hs])
            dst = dS[h]
            dke, dv, ddec = {}, {}, {}
            for c in order:
                rs = _chunk_rows(c)
                dv[c] = dvi[c] + _dot_nt(keb[rs, hs], dst)
                dke[c] = _dot(vb[rs, hs], dst)
                ddec[c] = _colsum(st_ref[c, h] * dst)
                dst = inc[c] + dst * decs[c][:, hs]
            dS[h] = dst
            cat = lambda t: jnp.concatenate([t[c] for c in range(NCK)], axis=0)
            dqd_h.append(cat(dqd))
            dki_h.append(cat(dki))
            dke_h.append(cat(dke))
            dv_h.append(cat(dv))
            ddec_h.append([ddec[c] for c in range(NCK)])
        lanes = lambda parts: jnp.concatenate(parts, axis=1)
        dqd, dki, dke, dv = lanes(dqd_h), lanes(dki_h), lanes(dke_h), lanes(dv_h)
        dq = dqd * eg
        dk = dki * eig + dke * eeg
        dke_ke = dke * ke
        dg = dqd * qd - dki * ki - dke_ke
        dgl = [_colsum(dke_ke[_chunk_rows(c), :]) + lanes([ddec_h[h][c] for h in range(H)]) * decs[c]
               for c in range(NCK)]
        dglb = jnp.concatenate([jnp.broadcast_to(t, (CH, D)) for t in dgl], axis=0)
        df = (_chunk_cumsum(dg, 1 - d) + dglb) / f - dk
        dzf = df * (1.0 - lb) * (sf * (1.0 - sf))
        sum_ref[0:1, :] += _colsum(dzf)
        sum_ref[1:2, :] += _colsum(df * (1.0 - sf))
        if last:
            dz0 = (dq + pq_ref[...]) * (Q_SCALE * (sq * (1.0 + zq * (1.0 - sq))))
            dz3 = dv + pv_ref[...]
            sum_ref[2:3, :] += _colsum(dz0)
            sum_ref[3:4, :] += _colsum(dz3)
            o0_ref[...] = dz0.astype(BF)
            o1_ref[...] = dz3.astype(BF)
        else:
            o0_ref[...] = dq
            o1_ref[...] = dv
        o2_ref[...] = dzf.astype(BF)

        @pl.when(s == nrb - 1)
        def _():
            sum_ref[1:2, :] = sum_ref[1:2, :] * (lb * (1.0 - lb))

    def zspec(cb):
        return pl.BlockSpec((RB, D), lambda s: (rb(s), cb))

    rowspec = pl.BlockSpec((RB, D), lambda s: (rb(s), 0))
    in_specs = [zspec(0), zspec(1 + d), zspec(3), pl.BlockSpec((2, 2, D), lambda s: (0, 0, 0)),
                pl.BlockSpec((RB, D), lambda s: (jnp.minimum(rb(s), nlb - 1), 0)),
                pl.BlockSpec((NCK, H, DH, DH), lambda s: (rb(s), 0, 0, 0))]
    args = [z, z, z, lbl, do_lat, states]
    if last:
        in_specs += [rowspec, rowspec]
        args += list(prev)
    dt01 = BF if last else F32
    return pl.pallas_call(
        body, grid=(nrb,), in_specs=in_specs,
        out_specs=(rowspec, rowspec, rowspec, pl.BlockSpec((8, D), lambda s: (0, 0))),
        out_shape=(jax.ShapeDtypeStruct((m, D), dt01), jax.ShapeDtypeStruct((m, D), dt01),
                   jax.ShapeDtypeStruct((m, D), BF), jax.ShapeDtypeStruct((8, D), F32)),
        scratch_shapes=[pltpu.VMEM((H, DH, DH), F32)],
        name=f"gla_bwd{d}", compiler_params=_cp(("arbitrary",)))(*args)


def _shift_rows(v, k, r):
    row = lax.broadcasted_iota(jnp.int32, v.shape, 0)
    rolled = pltpu.roll(v, k % r, 0)
    return jnp.where((row >= k) & (row < r + k), rolled, 0.0)


def _conv_fwd(xb, cw_ref, cb_ref, r):
    return (cb_ref[...] + _shift_rows(xb, 1, r) * cw_ref[0:1, :] + xb * cw_ref[1:2, :]
            + _shift_rows(xb, -1, r) * cw_ref[2:3, :] + _shift_rows(xb, -2, r) * cw_ref[3:4, :])


def _window(ref, lo, n, rows):
    parts = []
    if lo < 0:
        parts.append(jnp.zeros((-lo,) + tuple(ref.shape[1:]), F32))
    parts.append(ref[max(lo, 0):min(lo + n, rows)])
    if lo + n > rows:
        parts.append(jnp.zeros((lo + n - rows,) + tuple(ref.shape[1:]), F32))
    return parts[0] if len(parts) == 1 else jnp.concatenate(parts, axis=0)


def _conv_cols(x_ref, cw_ref, cb_ref, r0, n, rows):
    w = _window(x_ref, r0 - 1, n + 3, rows)
    return (cb_ref[...] + w[0:n] * cw_ref[0:1, :] + w[1:n + 1] * cw_ref[1:2, :] + w[2:n + 2] * cw_ref[2:3, :]
            + w[3:n + 3] * cw_ref[3:4, :])


def _gate_a(xc, wr_ref, br, sp):
    xcb = xc.astype(BF)
    rs = [jnp.dot(xcb[:, g * DH:(g + 1) * DH], wr_ref[g].astype(BF), preferred_element_type=F32) for g in range(H)]
    return jnp.exp((-RG_C * _sig(jnp.concatenate(rs, axis=1) + br)) * sp)


def _softplus_neg(lam):
    y = jnp.exp(-jnp.abs(lam))
    u = 1.0 + y
    tiny = u == 1.0
    l1p = jnp.where(tiny, y, jnp.log(u) * (y / jnp.where(tiny, 1.0, u - 1.0)))
    return jnp.maximum(-lam, 0.0) + l1p


def _gates(xc, wr_ref, br, wi_ref, bi, sp):
    xcb = xc.astype(BF)
    rs, is_ = [], []
    for g in range(H):
        gs = slice(g * DH, (g + 1) * DH)
        rs.append(jnp.dot(xcb[:, gs], wr_ref[g].astype(BF), preferred_element_type=F32))
        is_.append(jnp.dot(xcb[:, gs], wi_ref[g].astype(BF), preferred_element_type=F32))
    r = _sig(jnp.concatenate(rs, axis=1) + br)
    i = _sig(jnp.concatenate(is_, axis=1) + bi)
    log_a = (-RG_C * r) * sp
    a = jnp.exp(log_a)
    t = jnp.tanh(log_a)
    om = (-2.0 * t) / (1.0 - t)
    return r, i, a, om


def _scan_rows(d, nrows, a_s, b_s, h_s, h0):
    nsl = nrows // 8

    def slab(j, h):
        jj = (nsl - 1 - j) if d else j
        r0 = pl.multiple_of(jj * 8, 8)
        for t in (range(7, -1, -1) if d else range(8)):
            h = a_s[pl.ds(r0 + t, 1), :] * h + b_s[pl.ds(r0 + t, 1), :]
            h_s[pl.ds(r0 + t, 1), :] = h
        return h

    return lax.fori_loop(0, nsl, slab, h0)


def _col_of(d, ncols):
    if d:
        return lambda s: ncols - jnp.maximum(s, 1)
    return lambda s: jnp.maximum(s, 1) - 1


def _rglru_fwd(z, cw, cb, wr, br, wi, bi, lam, d, t_lat, t_ctx):
    m = z.shape[0]
    rows = t_lat // GRID_W
    z3 = z.reshape(m // GRID_W, GRID_W, IN_COLS)
    nblk = GRID_W // CB
    blk = _col_of(d, nblk)
    cblk = t_lat // t_ctx
    rc = min(RCH, rows)

    def body(zc_ref, zx_ref, cw_ref, cb_ref, wr_ref, br_ref, wi_ref, bi_ref, lam_ref,
             hx_ref, hpx_ref, hpc_ref, a_s, b_s, h_s, hcar, a3, b3, cin_s):
        s = pl.program_id(0)
        sp = _softplus_neg(lam_ref[d:d + 1, :])
        br_ = br_ref[d:d + 1, :]
        bi_ = bi_ref[d:d + 1, :]

        @pl.when(s == 0)
        def _():
            xc = _conv_fwd(zc_ref[...], cw_ref, cb_ref, t_ctx)
            _, i, a, om = _gates(xc, wr_ref, br_, wi_ref, bi_, sp)
            a_s[...] = a
            b_s[...] = jnp.sqrt(om) * (i * xc)
            h0 = jnp.zeros((1, D), F32)
            hcar[0:1, :] = _scan_rows(d, t_ctx, a_s, b_s, h_s, h0)
            hs = h_s[...]
            row = lax.broadcasted_iota(jnp.int32, (t_ctx, D), 0)
            if d:
                hpc_ref[...] = jnp.where(row == t_ctx - 1, h0, pltpu.roll(hs, t_ctx - 1, 0))
            else:
                hpc_ref[...] = jnp.where(row == 0, h0, pltpu.roll(hs, 1, 0))

        @pl.when(s > 0)
        def _():
            for r0 in range(0, rows, rc):
                xc = _conv_cols(zx_ref, cw_ref, cb_ref, r0, rc, rows).reshape(rc * CB, D)
                _, i, a, om = _gates(xc, wr_ref, br_, wi_ref, bi_, sp)
                a3[r0:r0 + rc] = a.reshape(rc, CB, D)
                b3[r0:r0 + rc] = (jnp.sqrt(om) * (i * xc)).reshape(rc, CB, D)

            def local(t, carry):
                hl, p = carry
                r = (rows - 1 - t) if d else t
                a = a3[r]
                hl = a * hl + b3[r]
                p = a * p
                b3[r] = hl
                a3[r] = p
                return hl, p

            hl, p = lax.fori_loop(0, rows, local, (jnp.zeros((CB, D), F32), jnp.ones((CB, D), F32)))
            cin = hcar[0:1, :]
            for j in (range(CB - 1, -1, -1) if d else range(CB)):
                cin_s[j:j + 1, :] = cin
                cin = hl[j:j + 1, :] + p[j:j + 1, :] * cin
            hcar[0:1, :] = cin
            c_in = cin_s[...]

            def fix(t, prev):
                r = (rows - 1 - t) if d else t
                h = b3[r] + a3[r] * c_in
                hx_ref[r] = h
                hpx_ref[r] = prev
                return h

            lax.fori_loop(0, rows, fix, c_in)

    full = lambda shp: pl.BlockSpec(shp, lambda s: (0,) * len(shp))
    colspec = pl.BlockSpec((rows, CB, D), lambda s: (0, blk(s), 0))
    outs = pl.pallas_call(
        body, grid=(nblk + 1,),
        in_specs=[pl.BlockSpec((t_ctx, D), lambda s: (cblk, 5)),
                  pl.BlockSpec((rows, CB, D), lambda s: (0, blk(s), 5)),
                  full((4, D)), full((1, D)),
                  pl.BlockSpec((None, H, DH, DH), lambda s: (d, 0, 0, 0)), full((2, D)),
                  pl.BlockSpec((None, H, DH, DH), lambda s: (d, 0, 0, 0)), full((2, D)), full((2, D))],
        out_specs=(colspec, colspec, full((t_ctx, D))),
        out_shape=(jax.ShapeDtypeStruct((rows, GRID_W, D), F32), jax.ShapeDtypeStruct((rows, GRID_W, D), F32),
                   jax.ShapeDtypeStruct((t_ctx, D), F32)),
        scratch_shapes=[pltpu.VMEM((t_ctx, D), F32), pltpu.VMEM((t_ctx, D), F32), pltpu.VMEM((t_ctx, D), F32),
                        pltpu.VMEM((8, D), F32), pltpu.VMEM((rows, CB, D), F32), pltpu.VMEM((rows, CB, D), F32),
                        pltpu.VMEM((CB, D), F32)],
        name=f"rglru_fwd{d}", compiler_params=_cp(("arbitrary",)))(z, z3, cw, cb, wr, br, wi, bi, lam)
    return outs[0].reshape(t_lat, D), outs[1].reshape(t_lat, D), outs[2]


def _rglru_bwd(z, cw, cb, wr, br, wi, bi, lam, dh_lat, hp_lat, hp_ctx, d, t_lat, t_ctx, prev=None):
    m = z.shape[0]
    rows = t_lat // GRID_W
    z3 = z.reshape(m // GRID_W, GRID_W, IN_COLS)
    nblk = GRID_W // CB
    fblk = _col_of(d, nblk)
    blk = lambda s: fblk(nblk - jnp.minimum(s, nblk - 1))
    cblk = t_lat // t_ctx
    rc = min(RCH, rows)
    last = prev is not None
    view3 = lambda v: v.reshape(rows, GRID_W, D)

    def body(*refs):
        (zc_ref, zx_ref, cw_ref, cb_ref, wr_ref, br_ref, wi_ref, bi_ref, lam_ref, dh_ref, hpx_ref, hpc_ref) = refs[:12]
        k = 12
        if last:
            pdx_ref, pdc_ref = refs[12:14]
            k = 14
        ox_ref, oc_ref, dwr_ref, dwi_ref, sum_ref, a_s, b_s, gcar, a3, b3, cin_s = refs[k:]
        s = pl.program_id(0)
        lam_d = lam_ref[d:d + 1, :]
        sp = _softplus_neg(lam_d)
        br_ = br_ref[d:d + 1, :]
        bi_ = bi_ref[d:d + 1, :]
        flat = lambda v: v.reshape(-1, D)

        @pl.when(s == 0)
        def _():
            gcar[...] = jnp.zeros_like(gcar)
            dwr_ref[...] = jnp.zeros_like(dwr_ref)
            dwi_ref[...] = jnp.zeros_like(dwi_ref)
            sum_ref[...] = jnp.zeros_like(sum_ref)


        def conv_sums(dxc, dxb, xm1, x0, xp1, xp2):
            sum_ref[3:4, :] += _colsum(flat(dxc))
            sum_ref[4:5, :] += _colsum(flat(dxb))
            sum_ref[8:9, :] += _colsum(flat(dxc * xm1))
            sum_ref[9:10, :] += _colsum(flat(dxc * x0))
            sum_ref[10:11, :] += _colsum(flat(dxc * xp1))
            sum_ref[11:12, :] += _colsum(flat(dxc * xp2))

        def gate_grads(g, hp, xc, rr, i, a, om):
            mult = jnp.sqrt(om)
            da = g * hp
            ixc = i * xc
            dmult = g * ixc
            dixc = g * mult
            di = dixc * xc
            dxc = dixc * i
            dlog_a = da * a - dmult * ((1.0 - om) / mult)
            dr = dlog_a * (-RG_C * sp)
            sum_ref[2:3, :] += _colsum(dlog_a * rr)
            drp = dr * rr * (1.0 - rr)
            dip = di * i * (1.0 - i)
            sum_ref[0:1, :] += _colsum(drp)
            sum_ref[1:2, :] += _colsum(dip)
            xcb = xc.astype(BF)
            drb = drp.astype(BF)
            dib = dip.astype(BF)
            parts = []
            for gi in range(H):
                gs = slice(gi * DH, (gi + 1) * DH)
                parts.append(_dot_nt(drb[:, gs], wr_ref[gi]) + _dot_nt(dib[:, gs], wi_ref[gi]))
                dwr_ref[gi] += _dot_tn(xcb[:, gs], drb[:, gs])
                dwi_ref[gi] += _dot_tn(xcb[:, gs], dib[:, gs])
            return dxc + jnp.concatenate(parts, axis=1)

        @pl.when(s < nblk)
        def _():
            for r0 in range(0, rows, rc):
                xc = flat(_conv_cols(zx_ref, cw_ref, cb_ref, r0, rc, rows))
                a = _gate_a(xc, wr_ref, br_, sp).reshape(rc, CB, D)
                a3[r0:r0 + rc] = a
                b3[r0:r0 + rc] = a * dh_ref[r0:r0 + rc]

            def local(t, carry):
                c, q = carry
                r = t if d else (rows - 1 - t)
                a = a3[r]
                c = a * c + b3[r]
                q = a * q
                b3[r] = c
                a3[r] = q
                return c, q

            c, q = lax.fori_loop(0, rows, local, (jnp.zeros((CB, D), F32), jnp.ones((CB, D), F32)))
            cin = gcar[0:1, :]
            for j in (range(CB) if d else range(CB - 1, -1, -1)):
                cin_s[j:j + 1, :] = cin
                cin = c[j:j + 1, :] + q[j:j + 1, :] * cin
            gcar[0:1, :] = cin
            c_in = cin_s[...]
            for r0 in (range(rows - rc, -1, -rc) if d else range(0, rows, rc)):
                if d:
                    lo = max(r0 - 1, 0)
                    cn = b3[lo:r0 + rc - 1] + a3[lo:r0 + rc - 1] * c_in
                    if r0 == 0:
                        cn = jnp.concatenate([c_in[None], cn], axis=0)
                else:
                    hi = min(r0 + rc + 1, rows)
                    cn = b3[r0 + 1:hi] + a3[r0 + 1:hi] * c_in
                    if hi == rows:
                        cn = jnp.concatenate([cn, c_in[None]], axis=0)
                g = flat(dh_ref[r0:r0 + rc] + cn)
                xc = flat(_conv_cols(zx_ref, cw_ref, cb_ref, r0, rc, rows))
                rr, i, a, om = _gates(xc, wr_ref, br_, wi_ref, bi_, sp)
                b3[r0:r0 + rc] = gate_grads(g, flat(hpx_ref[r0:r0 + rc]), xc, rr, i, a, om).reshape(rc, CB, D)
            if last:
                for r0 in range(0, rows, rc):
                    b3[r0:r0 + rc] = b3[r0:r0 + rc] + pdx_ref[r0:r0 + rc]
                for r0 in range(0, rows, rc):
                    w = _window(b3, r0 - 2, rc + 3, rows)
                    xw = _window(zx_ref, r0 - 1, rc + 3, rows)
                    dxc = w[2:rc + 2]
                    dxb = (w[3:rc + 3] * cw_ref[0:1, :] + dxc * cw_ref[1:2, :] + w[1:rc + 1] * cw_ref[2:3, :]
                           + w[0:rc] * cw_ref[3:4, :])
                    conv_sums(dxc, dxb, xw[0:rc], xw[1:rc + 1], xw[2:rc + 2], xw[3:rc + 3])
                    ox_ref[r0:r0 + rc] = dxb
            else:
                for r0 in range(0, rows, rc):
                    ox_ref[r0:r0 + rc] = b3[r0:r0 + rc]

        @pl.when(s == nblk)
        def _():
            r = t_ctx
            xb = zc_ref[...]
            xc = _conv_fwd(xb, cw_ref, cb_ref, r)
            rr, i, a, om = _gates(xc, wr_ref, br_, wi_ref, bi_, sp)
            a_s[...] = a
            b_s[...] = jnp.zeros((r, D), F32)
            c0 = gcar[0:1, :]
            _scan_rows(1 - d, r, a_s, b_s, b_s, c0)
            cs = b_s[...]
            row = lax.broadcasted_iota(jnp.int32, (r, D), 0)
            if d:
                g = jnp.where(row == 0, c0, pltpu.roll(cs, 1, 0))
            else:
                g = jnp.where(row == r - 1, c0, pltpu.roll(cs, r - 1, 0))
            dxc = gate_grads(g, hpc_ref[...], xc, rr, i, a, om)
            if last:
                dxc = dxc + pdc_ref[...]
                dxb = (_shift_rows(dxc, -1, r) * cw_ref[0:1, :] + dxc * cw_ref[1:2, :]
                       + _shift_rows(dxc, 1, r) * cw_ref[2:3, :] + _shift_rows(dxc, 2, r) * cw_ref[3:4, :])
                conv_sums(dxc, dxb, _shift_rows(xb, 1, r), xb, _shift_rows(xb, -1, r), _shift_rows(xb, -2, r))
                oc_ref[...] = dxb
            else:
                oc_ref[...] = dxc
            sum_ref[2:3, :] = sum_ref[2:3, :] * (RG_C * _sig(-lam_d))

    full = lambda shp: pl.BlockSpec(shp, lambda s: (0,) * len(shp))
    once = lambda shp: pl.BlockSpec(shp, lambda s: (0,) * len(shp), pipeline_mode=pl.Buffered(1))
    colspec = pl.BlockSpec((rows, CB, D), lambda s: (0, blk(s), 0))
    colonce = pl.BlockSpec((rows, CB, D), lambda s: (0, blk(s), 0), pipeline_mode=pl.Buffered(1))
    in_specs = [pl.BlockSpec((t_ctx, D), lambda s: (cblk, 5), pipeline_mode=pl.Buffered(1)),
                pl.BlockSpec((rows, CB, D), lambda s: (0, blk(s), 5), pipeline_mode=pl.Buffered(1)),
                full((4, D)), full((1, D)),
                pl.BlockSpec((None, H, DH, DH), lambda s: (d, 0, 0, 0)), full((2, D)),
                pl.BlockSpec((None, H, DH, DH), lambda s: (d, 0, 0, 0)), full((2, D)), full((2, D)),
                colonce, colonce, once((t_ctx, D))]
    args = [z, z3, cw, cb, wr, br, wi, bi, lam, view3(dh_lat), view3(hp_lat), hp_ctx]
    if last:
        in_specs += [colonce, once((t_ctx, D))]
        args += [view3(prev[0]), prev[1]]
    outs = pl.pallas_call(
        body, grid=(nblk + 1,), in_specs=in_specs,
        out_specs=(colspec, full((t_ctx, D)), full((H, DH, DH)), full((H, DH, DH)), full((16, D))),
        out_shape=(jax.ShapeDtypeStruct((rows, GRID_W, D), F32), jax.ShapeDtypeStruct((t_ctx, D), F32),
                   jax.ShapeDtypeStruct((H, DH, DH), F32), jax.ShapeDtypeStruct((H, DH, DH), F32),
                   jax.ShapeDtypeStruct((16, D), F32)),
        scratch_shapes=[pltpu.VMEM((t_ctx, D), F32), pltpu.VMEM((t_ctx, D), F32), pltpu.VMEM((8, D), F32),
                        pltpu.VMEM((rows, CB, D), F32), pltpu.VMEM((rows, CB, D), F32), pltpu.VMEM((CB, D), F32)],
        name=f"rglru_bwd{d}", compiler_params=_cp(("arbitrary",)))(*args)
    return (outs[0].reshape(t_lat, D), outs[1]) + tuple(outs[2:])


def _merge(o_f, o_b, h_f, h_b, z, x_all, tgt, mod, norm_g, ln_g, ln_b, p_a, p_b, w_out, t_lat):
    tm = 256
    nt = t_lat // tm

    def body(of_ref, ob_ref, hf_ref, hb_ref, z4_ref, z6_ref, z7_ref, z8_ref, x_ref, t_ref, mod_ref, ng_ref,
             lg_ref, lb_ref, pa_ref, pb_ref, wo_ref,
             do_ref, dh_ref, dz4_ref, dz6_ref, dz7_ref, dz8_ref, gx_ref,
             y_ref, dout_ref, oa_ref, dpa_ref, obv_ref, dpb_ref, acc_ref):
        i = pl.program_id(0)

        @pl.when(i == 0)
        def _():
            acc_ref[...] = jnp.zeros_like(acc_ref)

        def per_head(v):
            return jnp.concatenate(
                [jnp.broadcast_to(jnp.mean(v[:, h * DH:(h + 1) * DH], axis=-1, keepdims=True), (tm, DH))
                 for h in range(H)], axis=1)

        gt = mod_ref[0:1, 2 * D:3 * D]
        gfull = jnp.concatenate([ng_ref[...]] * H, axis=1)
        o = of_ref[...] + ob_ref[...]
        rinv = lax.rsqrt(per_head(o * o) + RMS_EPS)
        n = o * rinv
        na = n * gfull
        z4 = z4_ref[...]
        s4 = _sig(z4)
        silu4 = z4 * s4
        oa = na * silu4
        z6 = z6_ref[...]
        s6 = _sig(z6)
        silu6 = z6 * s6
        hsum = hf_ref[...] + hb_ref[...]
        obv = hsum * silu6
        pa = _dot(oa, pa_ref[...])
        pb = _dot(obv, pb_ref[...])
        s7 = _sig(z7_ref[...])
        s8 = _sig(z8_ref[...])
        y = s7 * pa + s8 * pb
        out = _dot(y, wo_ref[...])
        pre = ALPHA * x_ref[...] + gt * out
        mu = jnp.mean(pre, axis=-1, keepdims=True)
        xc = pre - mu
        rstd = lax.rsqrt(jnp.mean(xc * xc, axis=-1, keepdims=True) + LN_EPS)
        xhat = xc * rstd
        lg = lg_ref[...]
        diff = xhat * lg + lb_ref[...] - t_ref[...]
        acc_ref[8:9, :] += _colsum(diff * diff) * (0.5 / D)
        dxn = diff * (1.0 / D)
        acc_ref[1:2, :] += _colsum(dxn * xhat)
        acc_ref[2:3, :] += _colsum(dxn)
        dxhat = dxn * lg
        dpre = rstd * (dxhat - jnp.mean(dxhat, axis=-1, keepdims=True)
                       - xhat * jnp.mean(dxhat * xhat, axis=-1, keepdims=True))
        gx_ref[...] = ALPHA * dpre
        acc_ref[0:1, :] += _colsum(dpre * out)
        dout = dpre * gt
        dy = _dot_nt(dout, wo_ref[...])
        dpa = dy * s7
        dpb = dy * s8
        dz7 = dy * pa * (s7 * (1.0 - s7))
        dz8 = dy * pb * (s8 * (1.0 - s8))
        doa = _dot_nt(dpa, pa_ref[...])
        dob = _dot_nt(dpb, pb_ref[...])
        dh_ref[...] = dob * silu6
        dz6 = dob * hsum * (s6 * (1.0 + z6 * (1.0 - s6)))
        dna = doa * silu4
        dz4 = doa * na * (s4 * (1.0 + z4 * (1.0 - s4)))
        dng = _colsum(dna * n)
        acc_ref[7:8, 0:DH] += sum(dng[:, h * DH:(h + 1) * DH] for h in range(H))
        dn = dna * gfull
        do_ref[...] = rinv * (dn - n * per_head(dn * n))
        acc_ref[3:4, :] += _colsum(dz4)
        acc_ref[4:5, :] += _colsum(dz6)
        acc_ref[5:6, :] += _colsum(dz7)
        acc_ref[6:7, :] += _colsum(dz8)
        dz4_ref[...] = dz4.astype(BF)
        dz6_ref[...] = dz6.astype(BF)
        dz7_ref[...] = dz7.astype(BF)
        dz8_ref[...] = dz8.astype(BF)
        y_ref[...] = y.astype(BF)
        dout_ref[...] = dout.astype(BF)
        oa_ref[...] = oa.astype(BF)
        dpa_ref[...] = dpa.astype(BF)
        obv_ref[...] = obv.astype(BF)
        dpb_ref[...] = dpb.astype(BF)

        @pl.when(i == nt - 1)
        def _():
            acc_ref[9:10, :] = jnp.broadcast_to(jnp.sum(acc_ref[8:9, :], axis=-1, keepdims=True), (1, D))

    row = pl.BlockSpec((tm, D), lambda i: (i, 0))
    zs = lambda cb: pl.BlockSpec((tm, D), lambda i: (i, cb))
    full = lambda shp: pl.BlockSpec(shp, lambda i: (0,) * len(shp))
    wfull = pl.BlockSpec((D, D), lambda i: (0, 0), pipeline_mode=pl.Buffered(1))
    f32o = jax.ShapeDtypeStruct((t_lat, D), F32)
    bfo = jax.ShapeDtypeStruct((t_lat, D), BF)
    return pl.pallas_call(
        body, grid=(nt,),
        in_specs=[row, row, row, row, zs(4), zs(6), zs(7), zs(8), row, row, full((16, 3 * D)), full((1, DH)),
                  full((1, D)), full((1, D)), wfull, wfull, wfull],
        out_specs=(row,) * 13 + (full((16, D)),),
        out_shape=(f32o, f32o, bfo, bfo, bfo, bfo, f32o, bfo, bfo, bfo, bfo, bfo, bfo,
                   jax.ShapeDtypeStruct((16, D), F32)),
        name="merge", compiler_params=_cp(("arbitrary",), VMEM_LIMIT_MERGE))(
            o_f, o_b, h_f, h_b, z, z, z, z, x_all, tgt, mod, norm_g, ln_g, ln_b, p_a, p_b, w_out)


def _wgrad(a, b, name):
    tm = 1024

    def body(a_ref, b_ref, o_ref):
        @pl.when(pl.program_id(0) == 0)
        def _():
            o_ref[...] = jnp.zeros_like(o_ref)
        o_ref[...] += _dot_tn(a_ref[...], b_ref[...])

    row = pl.BlockSpec((tm, D), lambda i: (i, 0))
    return pl.pallas_call(body, grid=(a.shape[0] // tm,), in_specs=[row, row],
                          out_specs=pl.BlockSpec((D, D), lambda i: (0, 0)),
                          out_shape=jax.ShapeDtypeStruct((D, D), F32), name=name,
                          compiler_params=_cp(("arbitrary",)))(a, b)


def _wgrad_in(x_all, mod, dz, n_lat_tiles):
    m = x_all.shape[0]
    assert m % 128 == 0
    tm = m // 8
    n_lat = n_lat_tiles * 256

    def body(x_ref, mod_ref, dz_ref, o_ref):
        i = pl.program_id(1)

        @pl.when(i == 0)
        def _():
            o_ref[...] = jnp.zeros_like(o_ref)
        is_lat = (i * tm + lax.broadcasted_iota(jnp.int32, (tm, 1), 0)) < n_lat
        sh = jnp.where(is_lat, mod_ref[0:1, 0:D], mod_ref[1:2, 0:D])
        sc = jnp.where(is_lat, mod_ref[0:1, D:2 * D], mod_ref[1:2, D:2 * D])
        o_ref[0] += _dot_tn(x_ref[...] * (1.0 + sc) + sh, dz_ref[...])

    return pl.pallas_call(
        body, grid=(NSH, m // tm),
        in_specs=[pl.BlockSpec((tm, D), lambda n, i: (i, 0)),
                  pl.BlockSpec((16, 3 * D), lambda n, i: (0, 0)),
                  pl.BlockSpec((tm, SHC), lambda n, i: (i, n))],
        out_specs=pl.BlockSpec((1, D, SHC), lambda n, i: (n, 0, 0)),
        out_shape=jax.ShapeDtypeStruct((NSH, D, SHC), F32), name="wgrad_in",
        compiler_params=_cp(("arbitrary", "arbitrary")))(x_all, mod, dz)


def _du(dz, w_in_g, x_all, mod, gxres, n_lat_tiles, sums=()):
    m = x_all.shape[0]
    tm = 256
    nt = m // tm
    nct = nt - n_lat_tiles
    ns = len(sums)
    rblk = lambda i: jnp.where(i < nct, n_lat_tiles + i, i - nct)
    lblk = lambda i: jnp.maximum(i - nct, 0)

    def body(*refs):
        dz_ref, w_ref, x_ref, mod_ref, gr_ref = refs[:5]
        sum_refs = refs[5:5 + ns]
        gx_ref, dm_ref = refs[5 + ns:7 + ns]
        got_refs = refs[7 + ns:7 + 2 * ns]
        sems = refs[7 + 2 * ns:]
        i = pl.program_id(0)
        is_lat = i >= nct

        @pl.when(i == 0)
        def _():
            dm_ref[...] = jnp.zeros_like(dm_ref)
            if ns:
                for cp in _rs_chip_copies(sum_refs, got_refs, *sems):
                    cp.start()

        du = _dot_nt(dz_ref[:, 0:SHC], w_ref[0])
        for n in range(1, NSH):
            du = du + _dot_nt(dz_ref[:, n * SHC:(n + 1) * SHC], w_ref[n])
        sc = jnp.where(is_lat, mod_ref[0:1, D:2 * D], mod_ref[1:2, D:2 * D])
        dsh = _colsum(du)
        dsc = _colsum(du * x_ref[...])

        @pl.when(is_lat)
        def _():
            gx_ref[...] = du * (1.0 + sc) + gr_ref[...]
            dm_ref[0:1, 0:D] += dsh
            dm_ref[0:1, D:2 * D] += dsc

        @pl.when(jnp.logical_not(is_lat))
        def _():
            dm_ref[1:2, 0:D] += dsh
            dm_ref[1:2, D:2 * D] += dsc

        if ns:
            @pl.when(i == nt - 1)
            def _():
                for cp in _rs_chip_copies(sum_refs, got_refs, *sems):
                    cp.wait()

    outs = pl.pallas_call(
        body, grid=(nt,),
        in_specs=[pl.BlockSpec((tm, IN_COLS), lambda i: (rblk(i), 0)),
                  pl.BlockSpec((NSH, D, SHC), lambda i: (0, 0, 0), pipeline_mode=pl.Buffered(1)),
                  pl.BlockSpec((tm, D), lambda i: (rblk(i), 0)),
                  pl.BlockSpec((16, 3 * D), lambda i: (0, 0)),
                  pl.BlockSpec((tm, D), lambda i: (lblk(i), 0))] + [_ANY] * ns,
        out_specs=[pl.BlockSpec((tm, D), lambda i: (lblk(i), 0)),
                   pl.BlockSpec((8, 2 * D), lambda i: (0, 0))] + [_ANY] * ns,
        out_shape=[jax.ShapeDtypeStruct((n_lat_tiles * tm, D), F32), jax.ShapeDtypeStruct((8, 2 * D), F32)]
        + [jax.ShapeDtypeStruct((3,) + g.shape[1:], g.dtype) for g in sums],
        scratch_shapes=[pltpu.SemaphoreType.DMA((3 * ns,)), pltpu.SemaphoreType.DMA((3 * ns,))] if ns else [],
        name="du", compiler_params=_cp(("arbitrary",)))(dz, w_in_g, x_all, mod, gxres, *sums)
    return outs[0], outs[1], list(outs[2:])


def _row_tile(rows, cols):
    t = 8
    while t * 2 * cols * 4 <= (1 << 20) and rows % (t * 2) == 0:
        t *= 2
    return t


def _addn(arrs, name):
    shape = arrs[0].shape
    cols = shape[-1]
    flat = [a.reshape(-1, cols) for a in arrs]
    rows = flat[0].shape[0]
    tr = _row_tile(rows, cols)

    def body(*refs):
        acc = refs[0][...]
        for r in refs[1:-1]:
            acc = acc + r[...]
        refs[-1][...] = acc

    spec = pl.BlockSpec((tr, cols), lambda i: (i, 0))
    out = pl.pallas_call(body, grid=(rows // tr,), in_specs=[spec] * len(flat), out_specs=spec,
                         out_shape=jax.ShapeDtypeStruct((rows, cols), F32), name=name,
                         compiler_params=_cp(("arbitrary",)))(*flat)
    return out.reshape(shape)


def _adamw(w, g, m, v, name):
    rows, cols = w.shape
    tr = _row_tile(rows, cols)

    def body(w_ref, g_ref, m_ref, v_ref, d_ref, nm_ref, nv_ref):
        gg = g_ref[...]
        m2 = ADAM_B1 * m_ref[...] + (1.0 - ADAM_B1) * gg
        v2 = ADAM_B2 * v_ref[...] + (1.0 - ADAM_B2) * (gg * gg)
        m_hat = m2 / (1.0 - ADAM_B1 ** ADAM_STEP)
        v_hat = v2 / (1.0 - ADAM_B2 ** ADAM_STEP)
        d_ref[...] = -ADAM_LR * (m_hat / (jnp.sqrt(v_hat) + ADAM_EPS) + ADAM_WD * w_ref[...])
        nm_ref[...] = m2
        nv_ref[...] = v2

    spec = pl.BlockSpec((tr, cols), lambda i: (i, 0))
    o = jax.ShapeDtypeStruct((rows, cols), F32)
    return pl.pallas_call(body, grid=(rows // tr,), in_specs=[spec] * 4, out_specs=(spec,) * 3,
                          out_shape=(o, o, o), name=name, compiler_params=_cp(("arbitrary",)))(w, g, m, v)


_ANY = pl.BlockSpec(memory_space=pl.ANY)


def _place():
    return lax.axis_index("x"), lax.axis_index("y"), lax.axis_index("c")


def _allgather_chips(shards):
    n = len(shards)

    def body(*refs):
        ins, outs = refs[:n], refs[n:2 * n]
        send, recv, lsem = refs[2 * n:]
        x, y, c = _place()
        me = 2 * x + y
        peers = ((1 - x, y, c), (x, 1 - y, c), (1 - x, 1 - y, c))
        copies = []
        for j in range(n):
            cp = pltpu.make_async_copy(ins[j], outs[j].at[me], lsem.at[j])
            cp.start()
            copies.append(cp)
            for k, p in enumerate(peers):
                cp = pltpu.make_async_remote_copy(src_ref=ins[j], dst_ref=outs[j].at[me], send_sem=send.at[3 * j + k],
                                                  recv_sem=recv.at[3 * j + k], device_id=p, device_id_type=MESH)
                cp.start()
                copies.append(cp)
        for cp in copies:
            cp.wait()

    return pl.pallas_call(
        body, in_specs=[_ANY] * n, out_specs=[_ANY] * n,
        out_shape=[jax.ShapeDtypeStruct((NSH,) + s.shape, s.dtype) for s in shards],
        scratch_shapes=[pltpu.SemaphoreType.DMA((3 * n,)), pltpu.SemaphoreType.DMA((3 * n,)),
                        pltpu.SemaphoreType.DMA((n,))],
        name="allgather_weights")(*shards)


def _rs_sibling(grads):
    n = len(grads)

    def body(*refs):
        ins, got = refs[:n], refs[n:2 * n]
        send, recv = refs[2 * n:]
        x, y, c = _place()
        copies = []
        for j in range(n):
            hr = ins[j].shape[1] // 2
            for s in range(NSH):
                give = ins[j].at[s, pl.ds(pl.multiple_of((1 - c) * hr, 8), hr), :]
                cp = pltpu.make_async_remote_copy(src_ref=give, dst_ref=got[j].at[s], send_sem=send.at[NSH * j + s],
                                                  recv_sem=recv.at[NSH * j + s], device_id=(x, y, 1 - c),
                                                  device_id_type=MESH)
                cp.start()
                copies.append(cp)
        for cp in copies:
            cp.wait()

    half = [jax.ShapeDtypeStruct((NSH, g.shape[1] // 2, g.shape[2]), F32) for g in grads]
    return pl.pallas_call(
        body, in_specs=[_ANY] * n, out_specs=[_ANY] * n, out_shape=half,
        scratch_shapes=[pltpu.SemaphoreType.DMA((NSH * n,)), pltpu.SemaphoreType.DMA((NSH * n,))],
        name="rs_sibling")(*grads)


def _core_vec():
    return lax.axis_index("c").astype(jnp.int32).reshape(1)


def _rs_add1(g, got, name):
    _, r, cols = g.shape
    hr = r // 2
    tr = _row_tile(hr, cols)
    nb = hr // tr

    def body(c_ref, g_ref, got_ref, o_ref):
        o_ref[...] = (g_ref[...] + got_ref[...]).astype(BF)

    spec = pl.BlockSpec((1, tr, cols), lambda s, i, c_ref: (s, i, 0))
    return pl.pallas_call(
        body, grid_spec=pltpu.PrefetchScalarGridSpec(
            num_scalar_prefetch=1, grid=(NSH, nb),
            in_specs=[pl.BlockSpec((1, tr, cols), lambda s, i, c_ref: (s, c_ref[0] * nb + i, 0)), spec],
            out_specs=spec),
        out_shape=jax.ShapeDtypeStruct((NSH, hr, cols), BF), name=name,
        compiler_params=_cp(("arbitrary", "arbitrary")))(_core_vec(), g, got)


def _rs_add2(sums, got, name):
    _, hr, cols = sums.shape
    tr = _row_tile(hr, cols)
    nb = hr // tr
    place = jnp.stack([2 * lax.axis_index("x") + lax.axis_index("y"), lax.axis_index("c")]).astype(jnp.int32)

    def body(p_ref, s_ref, got_ref, o_ref):
        f = lambda v: v.astype(F32)
        o_ref[...] = f(s_ref[0]) + f(got_ref[0]) + f(got_ref[1]) + f(got_ref[2])

    return pl.pallas_call(
        body, grid_spec=pltpu.PrefetchScalarGridSpec(
            num_scalar_prefetch=1, grid=(nb,),
            in_specs=[pl.BlockSpec((1, tr, cols), lambda i, p_ref: (p_ref[0], i, 0)),
                      pl.BlockSpec((3, tr, cols), lambda i, p_ref: (0, i, 0))],
            out_specs=pl.BlockSpec((tr, cols), lambda i, p_ref: (p_ref[1] * nb + i, 0))),
        out_shape=jax.ShapeDtypeStruct((2 * hr, cols), F32), name=name,
        compiler_params=_cp(("arbitrary",)))(place, sums, got)


def _rs_chip_copies(ins, got, send, recv):
    x, y, c = _place()
    peers = ((1 - x, y), (x, 1 - y), (1 - x, 1 - y))
    return [pltpu.make_async_remote_copy(src_ref=ins[j].at[2 * px + py], dst_ref=got[j].at[k],
                                         send_sem=send.at[3 * j + k], recv_sem=recv.at[3 * j + k],
                                         device_id=(px, py, c), device_id_type=MESH)
            for j in range(len(ins)) for k, (px, py) in enumerate(peers)]


def _ag_sibling(fulls):
    n = len(fulls)
    nck = 4

    def body(*refs):
        outs = refs[n:2 * n]
        send, recv = refs[2 * n:]
        x, y, c = _place()
        copies = []
        for j in range(n):
            qr = outs[j].shape[0] // (2 * nck)
            for k in range(nck):
                rows = outs[j].at[pl.ds(pl.multiple_of((c * nck + k) * qr, 8), qr), :]
                cp = pltpu.make_async_remote_copy(src_ref=rows, dst_ref=rows, send_sem=send.at[nck * j + k],
                                                  recv_sem=recv.at[nck * j + k], device_id=(x, y, 1 - c),
                                                  device_id_type=MESH)
                cp.start()
                copies.append(cp)
        for cp in copies:
            cp.wait()

    return pl.pallas_call(
        body, in_specs=[_ANY] * n, out_specs=[_ANY] * n,
        out_shape=[jax.ShapeDtypeStruct(f.shape, F32) for f in fulls],
        input_output_aliases={j: j for j in range(n)},
        scratch_shapes=[pltpu.SemaphoreType.DMA((nck * n,)), pltpu.SemaphoreType.DMA((nck * n,))],
        name="ag_sibling")(*fulls)


def _allreduce_small(buf):
    rows = buf.shape[0]
    pr = rows // 8

    def body(in_ref, out_ref, stage, send1, recv1, send2, recv2):
        x, y, c = _place()
        me = 4 * x + 2 * y + c

        def peer(k):
            kx, ky, kc = (k >> 2) & 1, (k >> 1) & 1, k & 1
            return (x ^ kx, y ^ ky, c ^ kc)

        def piece(ref, idx):
            return ref.at[pl.ds(pl.multiple_of(idx * pr, 8), pr), :]

        copies = []
        for k in range(1, 8):
            px, py, pc = peer(k)
            cp = pltpu.make_async_remote_copy(src_ref=piece(in_ref, 4 * px + 2 * py + pc), dst_ref=stage.at[k],
                                              send_sem=send1.at[k], recv_sem=recv1.at[k],
                                              device_id=(px, py, pc), device_id_type=MESH)
            cp.start()
            copies.append(cp)
        for cp in copies:
            cp.wait()
        acc = piece(in_ref, me)[...]
        for k in range(1, 8):
            acc = acc + stage[k]
        piece(out_ref, me)[...] = acc
        copies = []
        for k in range(1, 8):
            cp = pltpu.make_async_remote_copy(src_ref=piece(out_ref, me), dst_ref=piece(out_ref, me),
                                              send_sem=send2.at[k], recv_sem=recv2.at[k],
                                              device_id=peer(k), device_id_type=MESH)
            cp.start()
            copies.append(cp)
        for cp in copies:
            cp.wait()

    vm = pl.BlockSpec(memory_space=pltpu.VMEM)
    return pl.pallas_call(
        body, in_specs=[vm], out_specs=vm, out_shape=jax.ShapeDtypeStruct((rows, D), F32),
        scratch_shapes=[pltpu.VMEM((8, pr, D), F32)] + [pltpu.SemaphoreType.DMA((8,))] * 4,
        name="allreduce_small", compiler_params=_cp())(buf)


def _rows(a):
    flat = a.reshape(-1)
    pad = (-flat.shape[0]) % D
    if pad:
        flat = jnp.concatenate([flat, jnp.zeros((pad,), flat.dtype)])
    return flat.reshape(-1, D)


def _pad_rows(a, mult):
    pad = (-a.shape[0]) % mult
    return jnp.concatenate([a, jnp.zeros((pad, a.shape[1]), a.dtype)]) if pad else a


def _local_step(x, c, ctx, c_ctx, tgt, w_mod_g, b_mod, w_in_g, b_in, lbl, norm_g, cw, cb, wr, br, wi, bi, lam,
                p_a, p_b, w_out, ln_g, ln_b, distributed=False):
    t_lat, t_ctx = x.shape[0], ctx.shape[0]
    nlt = t_lat // 256
    nlb, ncb = t_lat // RB, t_ctx // RB
    cc = jnp.zeros((16, D), F32).at[0].set(c).at[1].set(c_ctx)
    mod = _mod_fwd(cc, w_mod_g, b_mod)
    x_all = jnp.concatenate([x, ctx], axis=0)
    z = _inproj(x_all, mod, w_in_g, b_in, nlt)
    o0, st0 = _gla_fwd(z, lbl, 0, nlb, ncb)
    o1, st1 = _gla_fwd(z, lbl, 1, nlb, ncb)
    h0, hp0, hpc0 = _rglru_fwd(z, cw, cb, wr, br, wi, bi, lam, 0, t_lat, t_ctx)
    h1, hp1, hpc1 = _rglru_fwd(z, cw, cb, wr, br, wi, bi, lam, 1, t_lat, t_ctx)
    (do, dh, dz4, dz6, dz7, dz8, gxres, y, dout, oa, dpa, obv, dpb, acc) = _merge(
        o0, o1, h0, h1, z, x_all, tgt, mod, norm_g, ln_g, ln_b, p_a, p_b, w_out, t_lat)
    gp_a = _wgrad(oa, dpa, "wgrad_pa")
    gp_b = _wgrad(obv, dpb, "wgrad_pb")
    gw_out = _wgrad(y, dout, "wgrad_wout")
    dxc_lat, dxc_ctx, dwr0, dwi0, sb0 = _rglru_bwd(z, cw, cb, wr, br, wi, bi, lam, dh, hp0, hpc0, 0, t_lat, t_ctx)
    dz5_lat, dz5_ctx, dwr1, dwi1, sb1 = _rglru_bwd(z, cw, cb, wr, br, wi, bi, lam, dh, hp1, hpc1, 1, t_lat, t_ctx,
                                                   prev=(dxc_lat, dxc_ctx))
    dq0, dv0, dz1, sa0 = _gla_bwd(z, lbl, do, st0, 0, nlb, ncb)
    dz0, dz3, dz2, sa1 = _gla_bwd(z, lbl, do, st1, 1, nlb, ncb, prev=(dq0, dv0))
    zc = jnp.zeros((t_ctx, D), BF)
    pad = lambda a: jnp.concatenate([a, zc], axis=0)
    dz = jnp.concatenate([dz0, dz1, dz2, dz3, pad(dz4), jnp.concatenate([dz5_lat, dz5_ctx], axis=0).astype(BF),
                          pad(dz6), pad(dz7), pad(dz8)], axis=1)
    big = dict(w_in=_wgrad_in(x_all, mod, dz, nlt), p_a=gp_a.reshape(NSH, D // NSH, D),
               p_b=gp_b.reshape(NSH, D // NSH, D), w_out=gw_out.reshape(NSH, D // NSH, D))
    if distributed:
        grads = [big[k] for k in _RS]
        sums = [_rs_add1(g, b, f"rs_add1_{j}") for j, (g, b) in enumerate(zip(grads, _rs_sibling(grads)))]
        gx, dm, got = _du(dz, w_in_g, x_all, mod, gxres, nlt, sums)
        fulls = [_rs_add2(a, b, f"rs_add2_{j}") for j, (a, b) in enumerate(zip(sums, got))]
        big = dict(zip(_RS, _ag_sibling(fulls)))
    else:
        gx, dm, _ = _du(dz, w_in_g, x_all, mod, gxres, nlt)
    dmod = jnp.zeros((16, 3 * D), F32).at[0:2, 0:2 * D].set(dm[0:2]).at[0, 2 * D:].set(acc[0])
    dcc = _mod_bwd(cc, dmod, w_mod_g)
    small = dict(
        c_ctx=dcc[1:2], b_mod=(dmod[0] + dmod[1]).reshape(3, D),
        b_in=jnp.stack([sa1[2], sa0[0], sa1[0], sa1[3], acc[3], sb1[4], acc[4], acc[5], acc[6]]),
        lb_logits=jnp.stack([sa0[1], sa1[1], -sa0[1], -sa1[1]]),
        norm_a_g=acc[7:8], conv_w=sb1[8:12], conv_b=sb1[3:4],
        w_r=jnp.stack([dwr0, dwr1]).reshape(-1, D), w_i=jnp.stack([dwi0, dwi1]).reshape(-1, D),
        b_r=jnp.stack([sb0[0], sb1[0]]), b_i=jnp.stack([sb0[1], sb1[1]]), lam=jnp.stack([sb0[2], sb1[2]]),
        ln_g=acc[1:2], ln_b=acc[2:3])
    return acc[9, 0], gx, big, small, dmod


_RS = ("w_in", "p_a", "p_b", "w_out")
_SMALL =("c_ctx", "b_mod", "b_in", "lb_logits", "norm_a_g", "conv_w", "conv_b", "w_r", "w_i", "b_r", "b_i", "lam",
          "ln_g", "ln_b")
_BIG = ("w_mod", "w_in", "p_a", "p_b", "w_out")
_COL_SHARDED = ("lb_logits", "conv_w", "b_r", "b_i", "lam")
_WEIGHTS = ("c_ctx", "w_mod", "b_mod", "w_in", "b_in", "lb_logits", "norm_a_g", "conv_w", "conv_b", "w_r", "b_r", "w_i",
            "b_i", "lam", "p_a", "p_b", "w_out", "ln_g", "ln_b")


def kernel(x, c, ctx, c_ctx, w_mod, b_mod, w_in, b_in, lb_logits, norm_a_g, conv_w, conv_b, w_r, b_r, w_i, b_i, lam, p_a, p_b, w_out, ln_g, ln_b, loss_target, m_c_ctx, m_w_mod, m_b_mod, m_w_in, m_b_in, m_lb_logits, m_norm_a_g, m_conv_w, m_conv_b, m_w_r, m_b_r, m_w_i, m_b_i, m_lam, m_p_a, m_p_b, m_w_out, m_ln_g, m_ln_b, v_c_ctx, v_w_mod, v_b_mod, v_w_in, v_b_in, v_lb_logits, v_norm_a_g, v_conv_w, v_conv_b, v_w_r, v_b_r, v_w_i, v_b_i, v_lam, v_p_a, v_p_b, v_w_out, v_ln_g, v_ln_b):
    w = dict(c_ctx=c_ctx, w_mod=w_mod, b_mod=b_mod, w_in=w_in, b_in=b_in, lb_logits=lb_logits, norm_a_g=norm_a_g,
             conv_w=conv_w, conv_b=conv_b, w_r=w_r, b_r=b_r, w_i=w_i, b_i=b_i, lam=lam, p_a=p_a, p_b=p_b, w_out=w_out,
             ln_g=ln_g, ln_b=ln_b)
    m = dict(c_ctx=m_c_ctx, w_mod=m_w_mod, b_mod=m_b_mod, w_in=m_w_in, b_in=m_b_in, lb_logits=m_lb_logits,
             norm_a_g=m_norm_a_g, conv_w=m_conv_w, conv_b=m_conv_b, w_r=m_w_r, b_r=m_b_r, w_i=m_w_i, b_i=m_b_i,
             lam=m_lam, p_a=m_p_a, p_b=m_p_b, w_out=m_w_out, ln_g=m_ln_g, ln_b=m_ln_b)
    v = dict(c_ctx=v_c_ctx, w_mod=v_w_mod, b_mod=v_b_mod, w_in=v_w_in, b_in=v_b_in, lb_logits=v_lb_logits,
             norm_a_g=v_norm_a_g, conv_w=v_conv_w, conv_b=v_conv_b, w_r=v_w_r, b_r=v_b_r, w_i=v_w_i, b_i=v_b_i,
             lam=v_lam, p_a=v_p_a, p_b=v_p_b, w_out=v_w_out, ln_g=v_ln_g, ln_b=v_ln_b)
    shard = 2 * lax.axis_index("x") + lax.axis_index("y")
    cs = D // NSH

    narrow = _pad_rows(jnp.concatenate([lb_logits.reshape(4, cs), conv_w[0], b_r[0], b_i[0], lam[0]], axis=0), 8)
    big_shards = [w[k][0].astype(BF) for k in _BIG]
    gathered = _allgather_chips(big_shards + [narrow])
    w_mod_g, w_in_g = gathered[0], gathered[1]
    p_a_g, p_b_g, w_out_g = (g.reshape(D, D) for g in gathered[2:5])
    nar = jnp.transpose(gathered[5], (1, 0, 2)).reshape(-1, D)
    lbl, cw, br, bi, lam_f = nar[0:4].reshape(2, 2, D), nar[4:8], nar[8:10], nar[10:12], nar[12:14]

    loss, gx, big, small, dmod = _local_step(
        x[0], c[0], ctx[0], c_ctx, loss_target[0], w_mod_g, b_mod, w_in_g, b_in, lbl, norm_a_g, cw, conv_b,
        w_r[0], br, w_i[0], bi, lam_f, p_a_g, p_b_g, w_out_g, ln_g, ln_b, distributed=True)
    loss = lax.psum(loss, ("x", "y", "c"))

    me = 2 * shard + lax.axis_index("c")
    cc_rows = jnp.zeros((8, D), F32).at[me].set(c[0])
    dmod_rows = jnp.zeros((16, 3 * D), F32).at[me].set(dmod[0]).at[8].set(dmod[1]).reshape(48, D)
    sizes = [small[k].shape[0] for k in _SMALL]
    red = _allreduce_small(_pad_rows(jnp.concatenate([_pad_rows(small[k], 8) for k in _SMALL] + [cc_rows, dmod_rows],
                                                     axis=0), 64))
    grads = {}
    off = 0
    for k, n in zip(_SMALL, sizes):
        g = red[off:off + n]
        off += n + (-n) % 8
        if k == "norm_a_g":
            g = g[:, :DH]
        if k in _COL_SHARDED:
            g = lax.dynamic_slice_in_dim(g, shard * cs, cs, axis=1)
        grads[k] = g.reshape(w[k].shape)
    for k in _RS:
        grads[k] = big[k].reshape(w[k].shape)
    cc_all = jnp.zeros((16, D), F32).at[0:8].set(red[off:off + 8]).at[8].set(c_ctx)
    dmod_all = red[off + 8:off + 56].reshape(16, 3 * D)
    mc = 3 * D // NSH
    grads["w_mod"] = _wmod_grad(cc_all, lax.dynamic_slice_in_dim(dmod_all, shard * mc, mc, axis=1)).reshape(
        w["w_mod"].shape)

    delta, new_m, new_v = {}, {}, {}
    for k in _BIG:
        shp = w[k].shape
        two = lambda a: a.reshape(shp[-2], shp[-1])
        d_, m_, v_ = _adamw(two(w[k]), two(grads[k]), two(m[k]), two(v[k]), f"adamw_{k}")
        delta[k], new_m[k], new_v[k] = d_.reshape(shp), m_.reshape(shp), v_.reshape(shp)
    flat = lambda t: jnp.concatenate([_pad_rows(_rows(t[k]), 8) for k in _SMALL], axis=0)
    d_, m_, v_ = _adamw(flat(w), flat(grads), flat(m), flat(v), "adamw_small")
    off = 0
    for k in _SMALL:
        n = w[k].size
        nr = -(-n // D)
        for src, dst in ((d_, delta), (m_, new_m), (v_, new_v)):
            dst[k] = src[off:off + nr].reshape(-1)[:n].reshape(w[k].shape)
        off += nr + (-nr) % 8

    return (loss, gx[None], *[grads[k] for k in _WEIGHTS], *[delta[k] for k in _WEIGHTS],
            *[new_m[k] for k in _WEIGHTS], *[new_v[k] for k in _WEIGHTS])
```

```python
import functools

import jax
import jax.numpy as jnp
from jax import lax
from jax.experimental import pallas as pl
from jax.experimental.pallas import tpu as pltpu

F32 = jnp.float32
BF = jnp.bfloat16
MESH = pl.DeviceIdType.MESH

D = 1024
H = 8
DH = 128
CH = 64
RB = 256
NCK = RB // CH
GRID_W = 64
CB = 8
RCH = 16
IN_COLS = 9 * D
NSH = 4
SHC = IN_COLS // NSH
RG_C = 8.0
ALPHA = 2.0 ** 0.25
LN_EPS = 1e-5
RMS_EPS = 1e-6
Q_SCALE = DH ** -0.5
ADAM_LR, ADAM_B1, ADAM_B2, ADAM_EPS, ADAM_WD, ADAM_STEP = 0.001, 0.9, 0.999, 1e-08, 0.01, 10
VMEM_LIMIT = 56 * 1024 * 1024
VMEM_LIMIT_MERGE = 60 * 1024 * 1024


def _cp(sem=None, vmem=VMEM_LIMIT):
    return pltpu.CompilerParams(dimension_semantics=sem, vmem_limit_bytes=vmem)


def _sig(x):
    return 0.5 * jnp.tanh(0.5 * x) + 0.5


def _dot(a, b):
    return jnp.dot(a.astype(BF), b.astype(BF), preferred_element_type=F32)


def _dot_nt(a, b):
    return lax.dot_general(a.astype(BF), b.astype(BF), (((1,), (1,)), ((), ())), preferred_element_type=F32)


def _dot_tn(a, b):
    return lax.dot_general(a.astype(BF), b.astype(BF), (((0,), (0,)), ((), ())), preferred_element_type=F32)


def _colsum(v):
    return jnp.sum(v, axis=0, keepdims=True)


def _mod_bwd(cc, dmod, w_mod_g):
    def body(cc_ref, dm_ref, w_ref, dcc_ref):
        v = cc_ref[...]
        sg = _sig(v)
        ds = jnp.zeros((16, D), F32)
        for k in range(NSH):
            ds = ds + _dot_nt(dm_ref[:, k * 768:(k + 1) * 768], w_ref[k])
        dcc_ref[...] = ds * (sg * (1.0 + v * (1.0 - sg)))
    return pl.pallas_call(body, out_shape=jax.ShapeDtypeStruct((16, D), F32),
                          name="mod_bwd", compiler_params=_cp())(cc, dmod, w_mod_g)


def _wmod_grad(cc, dmod_cols):
    def body(cc_ref, dm_ref, dw_ref):
        v = cc_ref[...]
        dw_ref[...] = _dot_tn(v * _sig(v), dm_ref[...])
    return pl.pallas_call(body, out_shape=jax.ShapeDtypeStruct((D, dmod_cols.shape[1]), F32),
                          name="wmod_grad", compiler_params=_cp())(cc, dmod_cols)


def _modulate(x_ref, mod_ref, is_lat):
    sh = jnp.where(is_lat, mod_ref[0:1, 0:D], mod_ref[1:2, 0:D])
    sc = jnp.where(is_lat, mod_ref[0:1, D:2 * D], mod_ref[1:2, D:2 * D])
    return x_ref[...] * (1.0 + sc) + sh, sc


def _gla_mask(d):
    row = lax.broadcasted_iota(jnp.int32, (CH, CH), 0)
    col = lax.broadcasted_iota(jnp.int32, (CH, CH), 1)
    return (row <= col) if d else (row >= col)


def _chunk_cumsum(v, rev):
    n = v.shape[0]
    pos = lax.broadcasted_iota(jnp.int32, v.shape, 0) & (CH - 1)
    s = 1
    while s < CH:
        if rev:
            v = v + jnp.where(pos < CH - s, pltpu.roll(v, n - s, 0), 0.0)
        else:
            v = v + jnp.where(pos >= s, pltpu.roll(v, s, 0), 0.0)
        s *= 2
    return v


def _chunk_rows(c):
    return slice(c * CH, (c + 1) * CH)


def _gla_features(zq, zf, lb, d):
    sq = _sig(zq)
    q = zq * sq * Q_SCALE
    sf = _sig(zf)
    f = lb + (1.0 - lb) * sf
    k = 1.0 - f
    g = _chunk_cumsum(jnp.log(f), d)
    last = 0 if d else CH - 1
    gls = [g[c * CH + last:c * CH + last + 1, :] for c in range(NCK)]
    glb = jnp.concatenate([jnp.broadcast_to(gl, (CH, D)) for gl in gls], axis=0)
    eg, eig, eeg = jnp.exp(g), jnp.exp(-g), jnp.exp(glb - g)
    decs = [jnp.exp(gl) for gl in gls]
    return sq, sf, f, q * eg, k * eig, k * eeg, eg, eig, eeg, decs


def _lower_bound(lbl_ref, d):
    return _sig(lbl_ref[0, d:d + 1, :] - lbl_ref[1, d:d + 1, :])


def _gla_rb(d, nlb, ncb):
    nrb = nlb + ncb
    if d:
        return lambda s: nrb - 1 - s
    return lambda s: jnp.where(s < ncb, nlb + s, s - ncb)


def _gla_fwd(z, lbl, d, nlb, ncb, gather=()):
    m = z.shape[0]
    nrb = nlb + ncb
    rb = _gla_rb(d, nlb, ncb)
    ng = len(gather)

    def body(*refs):
        q_ref, f_ref, v_ref, lbl_ref = refs[:4]
        ag_in = refs[4:4 + ng]
        o_ref, st_ref = refs[4 + ng:6 + ng]
        ag_out = refs[6 + ng:6 + 2 * ng]
        S = refs[6 + 2 * ng]
        ag_sems = refs[7 + 2 * ng:]
        s = pl.program_id(0)

        @pl.when(s == 0)
        def _():
            S[...] = jnp.zeros_like(S)
            if ng:
                local, remote = _ag_copies(ag_in, ag_out, *ag_sems)
                for cp in remote + local:
                    cp.start()

        lb = _lower_bound(lbl_ref, d)
        mb = _gla_mask(d)
        _, _, _, qd, ki, ke, _, _, _, decs = _gla_features(q_ref[...], f_ref[...], lb, d)
        qd, ki, ke, v = qd.astype(BF), ki.astype(BF), ke.astype(BF), v_ref[...].astype(BF)
        order = range(NCK - 1, -1, -1) if d else range(NCK)
        for h in range(H):
            hs = slice(h * DH, (h + 1) * DH)
            intra, upd = {}, {}
            for c in range(NCK):
                rs = _chunk_rows(c)
                a = jnp.where(mb, _dot_nt(qd[rs, hs], ki[rs, hs]), 0.0)
                intra[c] = _dot(a, v[rs, hs])
                upd[c] = _dot_tn(v[rs, hs], ke[rs, hs])
            st = S[h]
            for c in order:
                rs = _chunk_rows(c)
                st_ref[c, h] = st
                o_ref[rs, hs] = intra[c] + _dot_nt(qd[rs, hs], st)
                st = st * decs[c][:, hs] + upd[c]
            S[h] = st

        if ng:
            @pl.when(s == nrb - 1)
            def _():
                local, remote = _ag_copies(ag_in, ag_out, *ag_sems)
                for cp in remote + local:
                    cp.wait()

    def zspec(cb):
        return pl.BlockSpec((RB, D), lambda s: (rb(s), cb))

    outs = pl.pallas_call(
        body, grid=(nrb,),
        in_specs=[zspec(0), zspec(1 + d), zspec(3), pl.BlockSpec((2, 2, D), lambda s: (0, 0, 0))] + [_ANY] * ng,
        out_specs=[pl.BlockSpec((RB, D), lambda s: (rb(s), 0)),
                   pl.BlockSpec((NCK, H, DH, DH), lambda s: (rb(s), 0, 0, 0))] + [_ANY] * ng,
        out_shape=[jax.ShapeDtypeStruct((m, D), F32), jax.ShapeDtypeStruct((m // CH, H, DH, DH), F32)]
        + [jax.ShapeDtypeStruct((NSH,) + g.shape, g.dtype) for g in gather],
        scratch_shapes=[pltpu.VMEM((H, DH, DH), F32)] + (_ag_sems(ng) if ng else []),
        name=f"gla_fwd{d}", compiler_params=_cp(("arbitrary",)))(z, z, z, lbl, *gather)
    return outs[0], outs[1], list(outs[2:])


def _gla_bwd(z, lbl, do_lat, states, d, nlb, ncb, prev=None):
    m = z.shape[0]
    nrb = nlb + ncb
    fwd_rb = _gla_rb(d, nlb, ncb)
    rb = lambda s: fwd_rb(nrb - 1 - s)
    last = prev is not None

    def body(*refs):
        if last:
            q_ref, f_ref, v_ref, lbl_ref, do_ref, st_ref, pq_ref, pv_ref, o0_ref, o1_ref, o2_ref, sum_ref, dS = refs
        else:
            q_ref, f_ref, v_ref, lbl_ref, do_ref, st_ref, o0_ref, o1_ref, o2_ref, sum_ref, dS = refs
        s = pl.program_id(0)
        is_lat = rb(s) < nlb

        @pl.when(s == 0)
        def _():
            dS[...] = jnp.zeros_like(dS)
            sum_ref[...] = jnp.zeros_like(sum_ref)

        lb = _lower_bound(lbl_ref, d)
        mb = _gla_mask(d)
        zq = q_ref[...]
        sq, sf, f, qd, ki, ke, eg, eig, eeg, decs = _gla_features(zq, f_ref[...], lb, d)
        qdb, kib, keb, vb = qd.astype(BF), ki.astype(BF), ke.astype(BF), v_ref[...].astype(BF)
        dob = jnp.where(is_lat, do_ref[...], 0.0).astype(BF)
        order = range(NCK) if d else range(NCK - 1, -1, -1)
        dqd_h, dki_h, dke_h, dv_h, ddec_h = [], [], [], [], []
        for h in range(H):
            hs = slice(h * DH, (h + 1) * DH)
            dqd, dki, inc, dvi = {}, {}, {}, {}
            for c in range(NCK):
                rs = _chunk_rows(c)
                a = jnp.where(mb, _dot_nt(qdb[rs, hs], kib[rs, hs]), 0.0)
                da = jnp.where(mb, _dot_nt(dob[rs, hs], vb[rs, hs]), 0.0)
                dqd[c] = _dot(da, kib[rs, hs]) + _dot(dob[rs, hs], st_ref[c, h])
                dki[c] = _dot_tn(da, qdb[rs, hs])
                inc[c] = _dot_tn(dob[rs, hs], qdb[rs, hs])
                dvi[c] = _dot_tn(a, dob[rs, hs])
            dst = dS[h]
            dke, dv, ddec = {}, {}, {}
            for c in order:
                rs = _chunk_rows(c)
                dv[c] = dvi[c] + _dot_nt(keb[rs, hs], dst)
                dke[c] = _dot(vb[rs, hs], dst)
                ddec[c] = _colsum(st_ref[c, h] * dst)
                dst = inc[c] + dst * decs[c][:, hs]
            dS[h] = dst
            cat = lambda t: jnp.concatenate([t[c] for c in range(NCK)], axis=0)
            dqd_h.append(cat(dqd))
            dki_h.append(cat(dki))
            dke_h.append(cat(dke))
            dv_h.append(cat(dv))
            ddec_h.append([ddec[c] for c in range(NCK)])
        lanes = lambda parts: jnp.concatenate(parts, axis=1)
        dqd, dki, dke, dv = lanes(dqd_h), lanes(dki_h), lanes(dke_h), lanes(dv_h)
        dq = dqd * eg
        dk = dki * eig + dke * eeg
        dke_ke = dke * ke
        dg = dqd * qd - dki * ki - dke_ke
        dgl = [_colsum(dke_ke[_chunk_rows(c), :]) + lanes([ddec_h[h][c] for h in range(H)]) * decs[c]
               for c in range(NCK)]
        dglb = jnp.concatenate([jnp.broadcast_to(t, (CH, D)) for t in dgl], axis=0)
        df = (_chunk_cumsum(dg, 1 - d) + dglb) / f - dk
        dzf = df * (1.0 - lb) * (sf * (1.0 - sf))
        sum_ref[0:1, :] += _colsum(dzf)
        sum_ref[1:2, :] += _colsum(df * (1.0 - sf))
        if last:
            dz0 = (dq + pq_ref[...]) * (Q_SCALE * (sq * (1.0 + zq * (1.0 - sq))))
            dz3 = dv + pv_ref[...]
            sum_ref[2:3, :] += _colsum(dz0)
            sum_ref[3:4, :] += _colsum(dz3)
            o0_ref[...] = dz0.astype(BF)
            o1_ref[...] = dz3.astype(BF)
        else:
            o0_ref[...] = dq
            o1_ref[...] = dv
        o2_ref[...] = dzf.astype(BF)

        @pl.when(s == nrb - 1)
        def _():
            sum_ref[1:2, :] = sum_ref[1:2, :] * (lb * (1.0 - lb))

    def zspec(cb):
        return pl.BlockSpec((RB, D), lambda s: (rb(s), cb))

    rowspec = pl.BlockSpec((RB, D), lambda s: (rb(s), 0))
    in_specs = [zspec(0), zspec(1 + d), zspec(3), pl.BlockSpec((2, 2, D), lambda s: (0, 0, 0)),
                pl.BlockSpec((RB, D), lambda s: (jnp.minimum(rb(s), nlb - 1), 0)),
                pl.BlockSpec((NCK, H, DH, DH), lambda s: (rb(s), 0, 0, 0))]
    args = [z, z, z, lbl, do_lat, states]
    if last:
        in_specs += [rowspec, rowspec]
        args += list(prev)
    dt01 = BF if last else F32
    return pl.pallas_call(
        body, grid=(nrb,), in_specs=in_specs,
        out_specs=(rowspec, rowspec, rowspec, pl.BlockSpec((8, D), lambda s: (0, 0))),
        out_shape=(jax.ShapeDtypeStruct((m, D), dt01), jax.ShapeDtypeStruct((m, D), dt01),
                   jax.ShapeDtypeStruct((m, D), BF), jax.ShapeDtypeStruct((8, D), F32)),
        scratch_shapes=[pltpu.VMEM((H, DH, DH), F32)],
        name=f"gla_bwd{d}", compiler_params=_cp(("arbitrary",)))(*args)


def _shift_rows(v, k, r):
    row = lax.broadcasted_iota(jnp.int32, v.shape, 0)
    rolled = pltpu.roll(v, k % r, 0)
    return jnp.where((row >= k) & (row < r + k), rolled, 0.0)


def _conv_fwd(xb, cw_ref, cb_ref, r):
    return (cb_ref[...] + _shift_rows(xb, 1, r) * cw_ref[0:1, :] + xb * cw_ref[1:2, :]
            + _shift_rows(xb, -1, r) * cw_ref[2:3, :] + _shift_rows(xb, -2, r) * cw_ref[3:4, :])


def _window(ref, lo, n, rows):
    parts = []
    if lo < 0:
        parts.append(jnp.zeros((-lo,) + tuple(ref.shape[1:]), F32))
    parts.append(ref[max(lo, 0):min(lo + n, rows)])
    if lo + n > rows:
        parts.append(jnp.zeros((lo + n - rows,) + tuple(ref.shape[1:]), F32))
    return parts[0] if len(parts) == 1 else jnp.concatenate(parts, axis=0)


def _conv_cols(x_ref, cw_ref, cb_ref, r0, n, rows):
    w = _window(x_ref, r0 - 1, n + 3, rows)
    return (cb_ref[...] + w[0:n] * cw_ref[0:1, :] + w[1:n + 1] * cw_ref[1:2, :] + w[2:n + 2] * cw_ref[2:3, :]
            + w[3:n + 3] * cw_ref[3:4, :])


def _gate_a(xc, wr_ref, br, sp):
    xcb = xc.astype(BF)
    rs = [jnp.dot(xcb[:, g * DH:(g + 1) * DH], wr_ref[g].astype(BF), preferred_element_type=F32) for g in range(H)]
    return jnp.exp((-RG_C * _sig(jnp.concatenate(rs, axis=1) + br)) * sp)


def _softplus_neg(lam):
    y = jnp.exp(-jnp.abs(lam))
    u = 1.0 + y
    tiny = u == 1.0
    l1p = jnp.where(tiny, y, jnp.log(u) * (y / jnp.where(tiny, 1.0, u - 1.0)))
    return jnp.maximum(-lam, 0.0) + l1p


def _gates(xc, wr_ref, br, wi_ref, bi, sp):
    xcb = xc.astype(BF)
    rs, is_ = [], []
    for g in range(H):
        gs = slice(g * DH, (g + 1) * DH)
        rs.append(jnp.dot(xcb[:, gs], wr_ref[g].astype(BF), preferred_element_type=F32))
        is_.append(jnp.dot(xcb[:, gs], wi_ref[g].astype(BF), preferred_element_type=F32))
    r = _sig(jnp.concatenate(rs, axis=1) + br)
    i = _sig(jnp.concatenate(is_, axis=1) + bi)
    log_a = (-RG_C * r) * sp
    a = jnp.exp(log_a)
    t = jnp.tanh(log_a)
    om = (-2.0 * t) / (1.0 - t)
    return r, i, a, om


def _scan_rows(d, nrows, a_s, b_s, h_s, h0):
    nsl = nrows // 8

    def slab(j, h):
        jj = (nsl - 1 - j) if d else j
        r0 = pl.multiple_of(jj * 8, 8)
        for t in (range(7, -1, -1) if d else range(8)):
            h = a_s[pl.ds(r0 + t, 1), :] * h + b_s[pl.ds(r0 + t, 1), :]
            h_s[pl.ds(r0 + t, 1), :] = h
        return h

    return lax.fori_loop(0, nsl, slab, h0)


def _col_of(d, ncols):
    if d:
        return lambda s: ncols - jnp.maximum(s, 1)
    return lambda s: jnp.maximum(s, 1) - 1


def _rglru_fwd(z, cw, cb, wr, br, wi, bi, lam, d, t_lat, t_ctx):
    m = z.shape[0]
    rows = t_lat // GRID_W
    z3 = z.reshape(m // GRID_W, GRID_W, IN_COLS)
    nblk = GRID_W // CB
    blk = _col_of(d, nblk)
    cblk = t_lat // t_ctx
    rc = min(RCH, rows)

    def body(zc_ref, zx_ref, cw_ref, cb_ref, wr_ref, br_ref, wi_ref, bi_ref, lam_ref,
             hx_ref, hpx_ref, hpc_ref, a_s, b_s, h_s, hcar, a3, b3, cin_s):
        s = pl.program_id(0)
        sp = _softplus_neg(lam_ref[d:d + 1, :])
        br_ = br_ref[d:d + 1, :]
        bi_ = bi_ref[d:d + 1, :]

        @pl.when(s == 0)
        def _():
            xc = _conv_fwd(zc_ref[...], cw_ref, cb_ref, t_ctx)
            _, i, a, om = _gates(xc, wr_ref, br_, wi_ref, bi_, sp)
            a_s[...] = a
            b_s[...] = jnp.sqrt(om) * (i * xc)
            h0 = jnp.zeros((1, D), F32)
            hcar[0:1, :] = _scan_rows(d, t_ctx, a_s, b_s, h_s, h0)
            hs = h_s[...]
            row = lax.broadcasted_iota(jnp.int32, (t_ctx, D), 0)
            if d:
                hpc_ref[...] = jnp.where(row == t_ctx - 1, h0, pltpu.roll(hs, t_ctx - 1, 0))
            else:
                hpc_ref[...] = jnp.where(row == 0, h0, pltpu.roll(hs, 1, 0))

        @pl.when(s > 0)
        def _():
            for r0 in range(0, rows, rc):
                xc = _conv_cols(zx_ref, cw_ref, cb_ref, r0, rc, rows).reshape(rc * CB, D)
                _, i, a, om = _gates(xc, wr_ref, br_, wi_ref, bi_, sp)
                a3[r0:r0 + rc] = a.reshape(rc, CB, D)
                b3[r0:r0 + rc] = (jnp.sqrt(om) * (i * xc)).reshape(rc, CB, D)

            def local(t, carry):
                hl, p = carry
                r = (rows - 1 - t) if d else t
                a = a3[r]
                hl = a * hl + b3[r]
                p = a * p
                b3[r] = hl
                a3[r] = p
                return hl, p

            hl, p = lax.fori_loop(0, rows, local, (jnp.zeros((CB, D), F32), jnp.ones((CB, D), F32)))
            cin = hcar[0:1, :]
            for j in (range(CB - 1, -1, -1) if d else range(CB)):
                cin_s[j:j + 1, :] = cin
                cin = hl[j:j + 1, :] + p[j:j + 1, :] * cin
            hcar[0:1, :] = cin
            c_in = cin_s[...]

            def fix(t, prev):
                r = (rows - 1 - t) if d else t
                h = b3[r] + a3[r] * c_in
                hx_ref[r] = h
                hpx_ref[r] = prev
                return h

            lax.fori_loop(0, rows, fix, c_in)

    full = lambda shp: pl.BlockSpec(shp, lambda s: (0,) * len(shp))
    colspec = pl.BlockSpec((rows, CB, D), lambda s: (0, blk(s), 0))
    outs = pl.pallas_call(
        body, grid=(nblk + 1,),
        in_specs=[pl.BlockSpec((t_ctx, D), lambda s: (cblk, 5)),
                  pl.BlockSpec((rows, CB, D), lambda s: (0, blk(s), 5)),
                  full((4, D)), full((1, D)),
                  pl.BlockSpec((None, H, DH, DH), lambda s: (d, 0, 0, 0)), full((2, D)),
                  pl.BlockSpec((None, H, DH, DH), lambda s: (d, 0, 0, 0)), full((2, D)), full((2, D))],
        out_specs=(colspec, colspec, full((t_ctx, D))),
        out_shape=(jax.ShapeDtypeStruct((rows, GRID_W, D), F32), jax.ShapeDtypeStruct((rows, GRID_W, D), F32),
                   jax.ShapeDtypeStruct((t_ctx, D), F32)),
        scratch_shapes=[pltpu.VMEM((t_ctx, D), F32), pltpu.VMEM((t_ctx, D), F32), pltpu.VMEM((t_ctx, D), F32),
                        pltpu.VMEM((8, D), F32), pltpu.VMEM((rows, CB, D), F32), pltpu.VMEM((rows, CB, D), F32),
                        pltpu.VMEM((CB, D), F32)],
        name=f"rglru_fwd{d}", compiler_params=_cp(("arbitrary",)))(z, z3, cw, cb, wr, br, wi, bi, lam)
    return outs[0].reshape(t_lat, D), outs[1].reshape(t_lat, D), outs[2]


def _rglru_bwd(z, cw, cb, wr, br, wi, bi, lam, dh_lat, hp_lat, hp_ctx, d, t_lat, t_ctx, prev=None):
    m = z.shape[0]
    rows = t_lat // GRID_W
    z3 = z.reshape(m // GRID_W, GRID_W, IN_COLS)
    nblk = GRID_W // CB
    fblk = _col_of(d, nblk)
    blk = lambda s: fblk(nblk - jnp.minimum(s, nblk - 1))
    cblk = t_lat // t_ctx
    rc = min(RCH, rows)
    last = prev is not None
    view3 = lambda v: v.reshape(rows, GRID_W, D)

    def body(*refs):
        (zc_ref, zx_ref, cw_ref, cb_ref, wr_ref, br_ref, wi_ref, bi_ref, lam_ref, dh_ref, hpx_ref, hpc_ref) = refs[:12]
        k = 12
        if last:
            pdx_ref, pdc_ref = refs[12:14]
            k = 14
        ox_ref, oc_ref, dwr_ref, dwi_ref, sum_ref, a_s, b_s, gcar, a3, b3, cin_s = refs[k:]
        s = pl.program_id(0)
        lam_d = lam_ref[d:d + 1, :]
        sp = _softplus_neg(lam_d)
        br_ = br_ref[d:d + 1, :]
        bi_ = bi_ref[d:d + 1, :]
        flat = lambda v: v.reshape(-1, D)

        @pl.when(s == 0)
        def _():
            gcar[...] = jnp.zeros_like(gcar)
            dwr_ref[...] = jnp.zeros_like(dwr_ref)
            dwi_ref[...] = jnp.zeros_like(dwi_ref)
            sum_ref[...] = jnp.zeros_like(sum_ref)


        def conv_sums(dxc, dxb, xm1, x0, xp1, xp2):
            sum_ref[3:4, :] += _colsum(flat(dxc))
            sum_ref[4:5, :] += _colsum(flat(dxb))
            sum_ref[8:9, :] += _colsum(flat(dxc * xm1))
            sum_ref[9:10, :] += _colsum(flat(dxc * x0))
            sum_ref[10:11, :] += _colsum(flat(dxc * xp1))
            sum_ref[11:12, :] += _colsum(flat(dxc * xp2))

        def gate_grads(g, hp, xc, rr, i, a, om):
            mult = jnp.sqrt(om)
            da = g * hp
            ixc = i * xc
            dmult = g * ixc
            dixc = g * mult
            di = dixc * xc
            dxc = dixc * i
            dlog_a = da * a - dmult * ((1.0 - om) / mult)
            dr = dlog_a * (-RG_C * sp)
            sum_ref[2:3, :] += _colsum(dlog_a * rr)
            drp = dr * rr * (1.0 - rr)
            dip = di * i * (1.0 - i)
            sum_ref[0:1, :] += _colsum(drp)
            sum_ref[1:2, :] += _colsum(dip)
            xcb = xc.astype(BF)
            drb = drp.astype(BF)
            dib = dip.astype(BF)
            parts = []
            for gi in range(H):
                gs = slice(gi * DH, (gi + 1) * DH)
                parts.append(_dot_nt(drb[:, gs], wr_ref[gi]) + _dot_nt(dib[:, gs], wi_ref[gi]))
                dwr_ref[gi] += _dot_tn(xcb[:, gs], drb[:, gs])
                dwi_ref[gi] += _dot_tn(xcb[:, gs], dib[:, gs])
            return dxc + jnp.concatenate(parts, axis=1)

        @pl.when(s < nblk)
        def _():
            for r0 in range(0, rows, rc):
                xc = flat(_conv_cols(zx_ref, cw_ref, cb_ref, r0, rc, rows))
                a = _gate_a(xc, wr_ref, br_, sp).reshape(rc, CB, D)
                a3[r0:r0 + rc] = a
                b3[r0:r0 + rc] = a * dh_ref[r0:r0 + rc]

            def local(t, carry):
                c, q = carry
                r = t if d else (rows - 1 - t)
                a = a3[r]
                c = a * c + b3[r]
                q = a * q
                b3[r] = c
                a3[r] = q
                return c, q

            c, q = lax.fori_loop(0, rows, local, (jnp.zeros((CB, D), F32), jnp.ones((CB, D), F32)))
            cin = gcar[0:1, :]
            for j in (range(CB) if d else range(CB - 1, -1, -1)):
                cin_s[j:j + 1, :] = cin
                cin = c[j:j + 1, :] + q[j:j + 1, :] * cin
            gcar[0:1, :] = cin
            c_in = cin_s[...]
            for r0 in (range(rows - rc, -1, -rc) if d else range(0, rows, rc)):
                if d:
                    lo = max(r0 - 1, 0)
                    cn = b3[lo:r0 + rc - 1] + a3[lo:r0 + rc - 1] * c_in
                    if r0 == 0:
                        cn = jnp.concatenate([c_in[None], cn], axis=0)
                else:
                    hi = min(r0 + rc + 1, rows)
                    cn = b3[r0 + 1:hi] + a3[r0 + 1:hi] * c_in
                    if hi == rows:
                        cn = jnp.concatenate([cn, c_in[None]], axis=0)
                g = flat(dh_ref[r0:r0 + rc] + cn)
                xc = flat(_conv_cols(zx_ref, cw_ref, cb_ref, r0, rc, rows))
                rr, i, a, om = _gates(xc, wr_ref, br_, wi_ref, bi_, sp)
                b3[r0:r0 + rc] = gate_grads(g, flat(hpx_ref[r0:r0 + rc]), xc, rr, i, a, om).reshape(rc, CB, D)
            if last:
                for r0 in range(0, rows, rc):
                    b3[r0:r0 + rc] = b3[r0:r0 + rc] + pdx_ref[r0:r0 + rc]
                for r0 in range(0, rows, rc):
                    w = _window(b3, r0 - 2, rc + 3, rows)
                    xw = _window(zx_ref, r0 - 1, rc + 3, rows)
                    dxc = w[2:rc + 2]
                    dxb = (w[3:rc + 3] * cw_ref[0:1, :] + dxc * cw_ref[1:2, :] + w[1:rc + 1] * cw_ref[2:3, :]
                           + w[0:rc] * cw_ref[3:4, :])
                    conv_sums(dxc, dxb, xw[0:rc], xw[1:rc + 1], xw[2:rc + 2], xw[3:rc + 3])
                    ox_ref[r0:r0 + rc] = dxb
            else:
                for r0 in range(0, rows, rc):
                    ox_ref[r0:r0 + rc] = b3[r0:r0 + rc]

        @pl.when(s == nblk)
        def _():
            r = t_ctx
            xb = zc_ref[...]
            xc = _conv_fwd(xb, cw_ref, cb_ref, r)
            rr, i, a, om = _gates(xc, wr_ref, br_, wi_ref, bi_, sp)
            a_s[...] = a
            b_s[...] = jnp.zeros((r, D), F32)
            c0 = gcar[0:1, :]
            _scan_rows(1 - d, r, a_s, b_s, b_s, c0)
            cs = b_s[...]
            row = lax.broadcasted_iota(jnp.int32, (r, D), 0)
            if d:
                g = jnp.where(row == 0, c0, pltpu.roll(cs, 1, 0))
            else:
                g = jnp.where(row == r - 1, c0, pltpu.roll(cs, r - 1, 0))
            dxc = gate_grads(g, hpc_ref[...], xc, rr, i, a, om)
            if last:
                dxc = dxc + pdc_ref[...]
                dxb = (_shift_rows(dxc, -1, r) * cw_ref[0:1, :] + dxc * cw_ref[1:2, :]
                       + _shift_rows(dxc, 1, r) * cw_ref[2:3, :] + _shift_rows(dxc, 2, r) * cw_ref[3:4, :])
                conv_sums(dxc, dxb, _shift_rows(xb, 1, r), xb, _shift_rows(xb, -1, r), _shift_rows(xb, -2, r))
                oc_ref[...] = dxb
            else:
                oc_ref[...] = dxc
            sum_ref[2:3, :] = sum_ref[2:3, :] * (RG_C * _sig(-lam_d))

    full = lambda shp: pl.BlockSpec(shp, lambda s: (0,) * len(shp))
    once = lambda shp: pl.BlockSpec(shp, lambda s: (0,) * len(shp), pipeline_mode=pl.Buffered(1))
    colspec = pl.BlockSpec((rows, CB, D), lambda s: (0, blk(s), 0))
    colonce = pl.BlockSpec((rows, CB, D), lambda s: (0, blk(s), 0), pipeline_mode=pl.Buffered(1))
    in_specs = [pl.BlockSpec((t_ctx, D), lambda s: (cblk, 5), pipeline_mode=pl.Buffered(1)),
                pl.BlockSpec((rows, CB, D), lambda s: (0, blk(s), 5), pipeline_mode=pl.Buffered(1)),
                full((4, D)), full((1, D)),
                pl.BlockSpec((None, H, DH, DH), lambda s: (d, 0, 0, 0)), full((2, D)),
                pl.BlockSpec((None, H, DH, DH), lambda s: (d, 0, 0, 0)), full((2, D)), full((2, D)),
                colonce, colonce, once((t_ctx, D))]
    args = [z, z3, cw, cb, wr, br, wi, bi, lam, view3(dh_lat), view3(hp_lat), hp_ctx]
    if last:
        in_specs += [colonce, once((t_ctx, D))]
        args += [view3(prev[0]), prev[1]]
    outs = pl.pallas_call(
        body, grid=(nblk + 1,), in_specs=in_specs,
        out_specs=(colspec, full((t_ctx, D)), full((H, DH, DH)), full((H, DH, DH)), full((16, D))),
        out_shape=(jax.ShapeDtypeStruct((rows, GRID_W, D), F32), jax.ShapeDtypeStruct((t_ctx, D), F32),
                   jax.ShapeDtypeStruct((H, DH, DH), F32), jax.ShapeDtypeStruct((H, DH, DH), F32),
                   jax.ShapeDtypeStruct((16, D), F32)),
        scratch_shapes=[pltpu.VMEM((t_ctx, D), F32), pltpu.VMEM((t_ctx, D), F32), pltpu.VMEM((8, D), F32),
                        pltpu.VMEM((rows, CB, D), F32), pltpu.VMEM((rows, CB, D), F32), pltpu.VMEM((CB, D), F32)],
        name=f"rglru_bwd{d}", compiler_params=_cp(("arbitrary",)))(*args)
    return (outs[0].reshape(t_lat, D), outs[1]) + tuple(outs[2:])


def _merge(o_f, o_b, h_f, h_b, z, x_all, tgt, mod, norm_g, ln_g, ln_b, p_a, p_b, w_out, t_lat):
    tm = 256
    nt = t_lat // tm

    def body(of_ref, ob_ref, hf_ref, hb_ref, z4_ref, z6_ref, z7_ref, z8_ref, x_ref, t_ref, mod_ref, ng_ref,
             lg_ref, lb_ref, pa_ref, pb_ref, wo_ref,
             do_ref, dh_ref, dz4_ref, dz6_ref, dz7_ref, dz8_ref, gx_ref,
             y_ref, dout_ref, oa_ref, dpa_ref, obv_ref, dpb_ref, acc_ref):
        i = pl.program_id(0)

        @pl.when(i == 0)
        def _():
            acc_ref[...] = jnp.zeros_like(acc_ref)

        def per_head(v):
            return jnp.concatenate(
                [jnp.broadcast_to(jnp.mean(v[:, h * DH:(h + 1) * DH], axis=-1, keepdims=True), (tm, DH))
                 for h in range(H)], axis=1)

        gt = mod_ref[0:1, 2 * D:3 * D]
        gfull = jnp.concatenate([ng_ref[...]] * H, axis=1)
        o = of_ref[...] + ob_ref[...]
        rinv = lax.rsqrt(per_head(o * o) + RMS_EPS)
        n = o * rinv
        na = n * gfull
        z4 = z4_ref[...]
        s4 = _sig(z4)
        silu4 = z4 * s4
        oa = na * silu4
        z6 = z6_ref[...]
        s6 = _sig(z6)
        silu6 = z6 * s6
        hsum = hf_ref[...] + hb_ref[...]
        obv = hsum * silu6
        pa = _dot(oa, pa_ref[...])
        pb = _dot(obv, pb_ref[...])
        s7 = _sig(z7_ref[...])
        s8 = _sig(z8_ref[...])
        y = s7 * pa + s8 * pb
        out = _dot(y, wo_ref[...])
        pre = ALPHA * x_ref[...] + gt * out
        mu = jnp.mean(pre, axis=-1, keepdims=True)
        xc = pre - mu
        rstd = lax.rsqrt(jnp.mean(xc * xc, axis=-1, keepdims=True) + LN_EPS)
        xhat = xc * rstd
        lg = lg_ref[...]
        diff = xhat * lg + lb_ref[...] - t_ref[...]
        acc_ref[8:9, :] += _colsum(diff * diff) * (0.5 / D)
        dxn = diff * (1.0 / D)
        acc_ref[1:2, :] += _colsum(dxn * xhat)
        acc_ref[2:3, :] += _colsum(dxn)
        dxhat = dxn * lg
        dpre = rstd * (dxhat - jnp.mean(dxhat, axis=-1, keepdims=True)
                       - xhat * jnp.mean(dxhat * xhat, axis=-1, keepdims=True))
        gx_ref[...] = ALPHA * dpre
        acc_ref[0:1, :] += _colsum(dpre * out)
        dout = dpre * gt
        dy = _dot_nt(dout, wo_ref[...])
        dpa = dy * s7
        dpb = dy * s8
        dz7 = dy * pa * (s7 * (1.0 - s7))
        dz8 = dy * pb * (s8 * (1.0 - s8))
        doa = _dot_nt(dpa, pa_ref[...])
        dob = _dot_nt(dpb, pb_ref[...])
        dh_ref[...] = dob * silu6
        dz6 = dob * hsum * (s6 * (1.0 + z6 * (1.0 - s6)))
        dna = doa * silu4
        dz4 = doa * na * (s4 * (1.0 + z4 * (1.0 - s4)))
        dng = _colsum(dna * n)
        acc_ref[7:8, 0:DH] += sum(dng[:, h * DH:(h + 1) * DH] for h in range(H))
        dn = dna * gfull
        do_ref[...] = rinv * (dn - n * per_head(dn * n))
        acc_ref[3:4, :] += _colsum(dz4)
        acc_ref[4:5, :] += _colsum(dz6)
        acc_ref[5:6, :] += _colsum(dz7)
        acc_ref[6:7, :] += _colsum(dz8)
        dz4_ref[...] = dz4.astype(BF)
        dz6_ref[...] = dz6.astype(BF)
        dz7_ref[...] = dz7.astype(BF)
        dz8_ref[...] = dz8.astype(BF)
        y_ref[...] = y.astype(BF)
        dout_ref[...] = dout.astype(BF)
        oa_ref[...] = oa.astype(BF)
        dpa_ref[...] = dpa.astype(BF)
        obv_ref[...] = obv.astype(BF)
        dpb_ref[...] = dpb.astype(BF)

        @pl.when(i == nt - 1)
        def _():
            acc_ref[9:10, :] = jnp.broadcast_to(jnp.sum(acc_ref[8:9, :], axis=-1, keepdims=True), (1, D))

    row = pl.BlockSpec((tm, D), lambda i: (i, 0))
    zs = lambda cb: pl.BlockSpec((tm, D), lambda i: (i, cb))
    full = lambda shp: pl.BlockSpec(shp, lambda i: (0,) * len(shp))
    wfull = pl.BlockSpec((D, D), lambda i: (0, 0), pipeline_mode=pl.Buffered(1))
    f32o = jax.ShapeDtypeStruct((t_lat, D), F32)
    bfo = jax.ShapeDtypeStruct((t_lat, D), BF)
    return pl.pallas_call(
        body, grid=(nt,),
        in_specs=[row, row, row, row, zs(4), zs(6), zs(7), zs(8), row, row, full((16, 3 * D)), full((1, DH)),
                  full((1, D)), full((1, D)), wfull, wfull, wfull],
        out_specs=(row,) * 13 + (full((16, D)),),
        out_shape=(f32o, f32o, bfo, bfo, bfo, bfo, f32o, bfo, bfo, bfo, bfo, bfo, bfo,
                   jax.ShapeDtypeStruct((16, D), F32)),
        name="merge", compiler_params=_cp(("arbitrary",), VMEM_LIMIT_MERGE))(
            o_f, o_b, h_f, h_b, z, z, z, z, x_all, tgt, mod, norm_g, ln_g, ln_b, p_a, p_b, w_out)


def _wgrad(a, b, name):
    tm = 1024

    def body(a_ref, b_ref, o_ref):
        @pl.when(pl.program_id(0) == 0)
        def _():
            o_ref[...] = jnp.zeros_like(o_ref)
        o_ref[...] += _dot_tn(a_ref[...], b_ref[...])

    row = pl.BlockSpec((tm, D), lambda i: (i, 0))
    return pl.pallas_call(body, grid=(a.shape[0] // tm,), in_specs=[row, row],
                          out_specs=pl.BlockSpec((D, D), lambda i: (0, 0)),
                          out_shape=jax.ShapeDtypeStruct((D, D), F32), name=name,
                          compiler_params=_cp(("arbitrary",)))(a, b)


def _wgrad_in(x_all, mod, dz, n_lat_tiles):
    m = x_all.shape[0]
    assert m % 128 == 0
    tm = m // 8
    n_lat = n_lat_tiles * 256

    def body(x_ref, mod_ref, dz_ref, o_ref):
        i = pl.program_id(1)

        @pl.when(i == 0)
        def _():
            o_ref[...] = jnp.zeros_like(o_ref)
        is_lat = (i * tm + lax.broadcasted_iota(jnp.int32, (tm, 1), 0)) < n_lat
        sh = jnp.where(is_lat, mod_ref[0:1, 0:D], mod_ref[1:2, 0:D])
        sc = jnp.where(is_lat, mod_ref[0:1, D:2 * D], mod_ref[1:2, D:2 * D])
        o_ref[0] += _dot_tn(x_ref[...] * (1.0 + sc) + sh, dz_ref[...])

    return pl.pallas_call(
        body, grid=(NSH, m // tm),
        in_specs=[pl.BlockSpec((tm, D), lambda n, i: (i, 0)),
                  pl.BlockSpec((16, 3 * D), lambda n, i: (0, 0)),
                  pl.BlockSpec((tm, SHC), lambda n, i: (i, n))],
        out_specs=pl.BlockSpec((1, D, SHC), lambda n, i: (n, 0, 0)),
        out_shape=jax.ShapeDtypeStruct((NSH, D, SHC), F32), name="wgrad_in",
        compiler_params=_cp(("arbitrary", "arbitrary")))(x_all, mod, dz)


def _du(dz, w_in_g, x_all, mod, gxres, n_lat_tiles, sums=()):
    m = x_all.shape[0]
    tm = 256
    nt = m // tm
    nct = nt - n_lat_tiles
    ns = len(sums)
    rblk = lambda i: jnp.where(i < nct, n_lat_tiles + i, i - nct)
    lblk = lambda i: jnp.maximum(i - nct, 0)

    def body(*refs):
        dz_ref, w_ref, x_ref, mod_ref, gr_ref = refs[:5]
        sum_refs = refs[5:5 + ns]
        gx_ref, dm_ref = refs[5 + ns:7 + ns]
        got_refs = refs[7 + ns:7 + 2 * ns]
        sems = refs[7 + 2 * ns:]
        i = pl.program_id(0)
        is_lat = i >= nct

        @pl.when(i == 0)
        def _():
            dm_ref[...] = jnp.zeros_like(dm_ref)
            if ns:
                for cp in _rs_chip_copies(sum_refs, got_refs, *sems):
                    cp.start()

        du = _dot_nt(dz_ref[:, 0:SHC], w_ref[0])
        for n in range(1, NSH):
            du = du + _dot_nt(dz_ref[:, n * SHC:(n + 1) * SHC], w_ref[n])
        sc = jnp.where(is_lat, mod_ref[0:1, D:2 * D], mod_ref[1:2, D:2 * D])
        dsh = _colsum(du)
        dsc = _colsum(du * x_ref[...])

        @pl.when(is_lat)
        def _():
            gx_ref[...] = du * (1.0 + sc) + gr_ref[...]
            dm_ref[0:1, 0:D] += dsh
            dm_ref[0:1, D:2 * D] += dsc

        @pl.when(jnp.logical_not(is_lat))
        def _():
            dm_ref[1:2, 0:D] += dsh
            dm_ref[1:2, D:2 * D] += dsc

        if ns:
            @pl.when(i == nt - 1)
            def _():
                for cp in _rs_chip_copies(sum_refs, got_refs, *sems):
                    cp.wait()

    outs = pl.pallas_call(
        body, grid=(nt,),
        in_specs=[pl.BlockSpec((tm, IN_COLS), lambda i: (rblk(i), 0)),
                  pl.BlockSpec((NSH, D, SHC), lambda i: (0, 0, 0), pipeline_mode=pl.Buffered(1)),
                  pl.BlockSpec((tm, D), lambda i: (rblk(i), 0)),
                  pl.BlockSpec((16, 3 * D), lambda i: (0, 0)),
                  pl.BlockSpec((tm, D), lambda i: (lblk(i), 0))] + [_ANY] * ns,
        out_specs=[pl.BlockSpec((tm, D), lambda i: (lblk(i), 0)),
                   pl.BlockSpec((8, 2 * D), lambda i: (0, 0))] + [_ANY] * ns,
        out_shape=[jax.ShapeDtypeStruct((n_lat_tiles * tm, D), F32), jax.ShapeDtypeStruct((8, 2 * D), F32)]
        + [jax.ShapeDtypeStruct((3,) + g.shape[1:], g.dtype) for g in sums],
        scratch_shapes=[pltpu.SemaphoreType.DMA((3 * ns,)), pltpu.SemaphoreType.DMA((3 * ns,))] if ns else [],
        name="du", compiler_params=_cp(("arbitrary",)))(dz, w_in_g, x_all, mod, gxres, *sums)
    return outs[0], outs[1], list(outs[2:])


def _row_tile(rows, cols):
    t = 8
    while t * 2 * cols * 4 <= (1 << 20) and rows % (t * 2) == 0:
        t *= 2
    return t


def _adamw(w, g, m, v, name):
    rows, cols = w.shape
    tr = _row_tile(rows, cols)

    def body(w_ref, g_ref, m_ref, v_ref, d_ref, nm_ref, nv_ref):
        gg = g_ref[...]
        m2 = ADAM_B1 * m_ref[...] + (1.0 - ADAM_B1) * gg
        v2 = ADAM_B2 * v_ref[...] + (1.0 - ADAM_B2) * (gg * gg)
        m_hat = m2 / (1.0 - ADAM_B1 ** ADAM_STEP)
        v_hat = v2 / (1.0 - ADAM_B2 ** ADAM_STEP)
        d_ref[...] = -ADAM_LR * (m_hat / (jnp.sqrt(v_hat) + ADAM_EPS) + ADAM_WD * w_ref[...])
        nm_ref[...] = m2
        nv_ref[...] = v2

    spec = pl.BlockSpec((tr, cols), lambda i: (i, 0))
    o = jax.ShapeDtypeStruct((rows, cols), F32)
    return pl.pallas_call(body, grid=(rows // tr,), in_specs=[spec] * 4, out_specs=(spec,) * 3,
                          out_shape=(o, o, o), name=name, compiler_params=_cp(("arbitrary",)))(w, g, m, v)


_ANY = pl.BlockSpec(memory_space=pl.ANY)


def _place():
    return lax.axis_index("x"), lax.axis_index("y"), lax.axis_index("c")


def _ag_copies(ins, outs, send, recv, lsem):
    x, y, c = _place()
    me = 2 * x + y
    peers = ((1 - x, y, c), (x, 1 - y, c), (1 - x, 1 - y, c))
    local = [pltpu.make_async_copy(ins[j], outs[j].at[me], lsem.at[j]) for j in range(len(ins))]
    remote = [pltpu.make_async_remote_copy(src_ref=ins[j], dst_ref=outs[j].at[me], send_sem=send.at[3 * j + k],
                                           recv_sem=recv.at[3 * j + k], device_id=p, device_id_type=MESH)
              for j in range(len(ins)) for k, p in enumerate(peers)]
    return local, remote


def _ag_sems(n):
    return [pltpu.SemaphoreType.DMA((3 * n,)), pltpu.SemaphoreType.DMA((3 * n,)), pltpu.SemaphoreType.DMA((n,))]


def _mod_tp(c8, c_ctx, w_mod_sh, b_mod_sh):
    mc = w_mod_sh.shape[1]

    def body(c8_ref, cctx_ref, w_ref, b_ref, mod_ref, cc_ref, cc_s, part_s, send1, recv1, send3, recv3):
        x, y, c = _place()
        me = 4 * x + 2 * y + c
        ms = 2 * x + y
        copies = []
        for k in range(1, 8):
            peer = (x ^ ((k >> 2) & 1), y ^ ((k >> 1) & 1), c ^ (k & 1))
            cp = pltpu.make_async_remote_copy(src_ref=c8_ref, dst_ref=cc_s.at[me], send_sem=send1.at[k],
                                              recv_sem=recv1.at[k], device_id=peer, device_id_type=MESH)
            cp.start()
            copies.append(cp)
        cc_s[me] = c8_ref[...]
        for cp in copies:
            cp.wait()
        cc_ref[...] = jnp.zeros_like(cc_ref)
        for j in range(8):
            cc_ref[j:j + 1, :] = cc_s[j, 0:1, :]
        cc_ref[8:9, :] = cctx_ref[...]
        v = cc_ref[...]
        part_s[ms] = _dot(v * _sig(v), w_ref[...]) + b_ref[...]
        copies = []
        for k in range(1, 4):
            peer = (x ^ ((k >> 1) & 1), y ^ (k & 1), c)
            cp = pltpu.make_async_remote_copy(src_ref=part_s.at[ms], dst_ref=part_s.at[ms], send_sem=send3.at[k],
                                              recv_sem=recv3.at[k], device_id=peer, device_id_type=MESH)
            cp.start()
            copies.append(cp)
        for cp in copies:
            cp.wait()
        for s in range(NSH):
            mod_ref[:, s * mc:(s + 1) * mc] = part_s[s]

    vm = pl.BlockSpec(memory_space=pltpu.VMEM)
    return pl.pallas_call(
        body, in_specs=[vm] * 4, out_specs=(vm, vm),
        out_shape=(jax.ShapeDtypeStruct((16, NSH * mc), F32), jax.ShapeDtypeStruct((16, D), F32)),
        scratch_shapes=[pltpu.VMEM((8, 8, D), F32), pltpu.VMEM((NSH, 16, mc), F32),
                        pltpu.SemaphoreType.DMA((8,)), pltpu.SemaphoreType.DMA((8,)),
                        pltpu.SemaphoreType.DMA((4,)), pltpu.SemaphoreType.DMA((4,))],
        name="mod_tp", compiler_params=_cp())(c8, c_ctx, w_mod_sh, b_mod_sh)


def _inproj_ag(x_all, mod, w_in_sh, b_in, narrow_sh, n_lat_tiles):
    m = x_all.shape[0]
    tm = 256
    nt = m // tm
    x_, y_ = lax.axis_index("x"), lax.axis_index("y")
    sids = jnp.stack([2 * x_ + y_, 2 * (1 - x_) + y_, 2 * x_ + 1 - y_, 2 * (1 - x_) + 1 - y_]).astype(jnp.int32)

    def body(sid_ref, x_ref, mod_ref, b_ref, wsh_ref, nsh_ref, z_ref, wg_ref, ng_ref, w_s, send, recv, lsem, wsem):
        n = pl.program_id(0)
        i = pl.program_id(1)
        ag = ((wsh_ref, nsh_ref), (wg_ref, ng_ref), send, recv, lsem)

        def load(src):
            cp = pltpu.make_async_copy(src, w_s, wsem)
            cp.start()
            cp.wait()

        @pl.when((n == 0) & (i == 0))
        def _():
            local, remote = _ag_copies(*ag)
            for cp in remote + local:
                cp.start()
            load(wsh_ref)

        for k in range(NSH - 1):
            @pl.when((n == k + 1) & (i == 0))
            def _():
                _ag_copies(*ag)[1][k].wait_recv()
                load(wg_ref.at[sid_ref[k + 1]])

        u, _ = _modulate(x_ref, mod_ref, i < n_lat_tiles)
        z_ref[...] = _dot(u, w_s[...]) + b_ref[...]

        @pl.when((n == NSH - 1) & (i == nt - 1))
        def _():
            local, remote = _ag_copies(*ag)
            for cp in remote:
                cp.wait_send()
            for cp in remote[NSH - 1:]:
                cp.wait_recv()
            for cp in local:
                cp.wait()

    outs = pl.pallas_call(
        body, grid_spec=pltpu.PrefetchScalarGridSpec(
            num_scalar_prefetch=1, grid=(NSH, nt),
            in_specs=[pl.BlockSpec((tm, D), lambda n, i, sid: (i, 0)),
                      pl.BlockSpec((16, 3 * D), lambda n, i, sid: (0, 0)),
                      pl.BlockSpec((1, SHC), lambda n, i, sid: (0, sid[n])), _ANY, _ANY],
            out_specs=[pl.BlockSpec((tm, SHC), lambda n, i, sid: (i, sid[n])), _ANY, _ANY],
            scratch_shapes=[pltpu.VMEM((D, SHC), BF)] + _ag_sems(2) + [pltpu.SemaphoreType.DMA]),
        out_shape=[jax.ShapeDtypeStruct((m, IN_COLS), F32), jax.ShapeDtypeStruct((NSH,) + w_in_sh.shape, BF),
                   jax.ShapeDtypeStruct((NSH,) + narrow_sh.shape, narrow_sh.dtype)],
        name="inproj_ag", compiler_params=_cp(("arbitrary", "arbitrary")))(sids, x_all, mod, b_in, w_in_sh, narrow_sh)
    return outs


def _rs_sibling(grads):
    n = len(grads)

    def body(*refs):
        ins, got = refs[:n], refs[n:2 * n]
        send, recv = refs[2 * n:]
        x, y, c = _place()
        copies = []
        for j in range(n):
            hr = ins[j].shape[1] // 2
            for s in range(NSH):
                give = ins[j].at[s, pl.ds(pl.multiple_of((1 - c) * hr, 8), hr), :]
                cp = pltpu.make_async_remote_copy(src_ref=give, dst_ref=got[j].at[s], send_sem=send.at[NSH * j + s],
                                                  recv_sem=recv.at[NSH * j + s], device_id=(x, y, 1 - c),
                                                  device_id_type=MESH)
                cp.start()
                copies.append(cp)
        for cp in copies:
            cp.wait()

    half = [jax.ShapeDtypeStruct((NSH, g.shape[1] // 2, g.shape[2]), F32) for g in grads]
    return pl.pallas_call(
        body, in_specs=[_ANY] * n, out_specs=[_ANY] * n, out_shape=half,
        scratch_shapes=[pltpu.SemaphoreType.DMA((NSH * n,)), pltpu.SemaphoreType.DMA((NSH * n,))],
        name="rs_sibling")(*grads)


def _core_vec():
    return lax.axis_index("c").astype(jnp.int32).reshape(1)


def _rs_add1(g, got, name):
    _, r, cols = g.shape
    hr = r // 2
    tr = _row_tile(hr, cols)
    nb = hr // tr

    def body(c_ref, g_ref, got_ref, o_ref):
        o_ref[...] = (g_ref[...] + got_ref[...]).astype(BF)

    spec = pl.BlockSpec((1, tr, cols), lambda s, i, c_ref: (s, i, 0))
    return pl.pallas_call(
        body, grid_spec=pltpu.PrefetchScalarGridSpec(
            num_scalar_prefetch=1, grid=(NSH, nb),
            in_specs=[pl.BlockSpec((1, tr, cols), lambda s, i, c_ref: (s, c_ref[0] * nb + i, 0)), spec],
            out_specs=spec),
        out_shape=jax.ShapeDtypeStruct((NSH, hr, cols), BF), name=name,
        compiler_params=_cp(("arbitrary", "arbitrary")))(_core_vec(), g, got)


def _rs_add2(sums, got, name):
    _, hr, cols = sums.shape
    tr = _row_tile(hr, cols)
    nb = hr // tr
    place = jnp.stack([2 * lax.axis_index("x") + lax.axis_index("y"), lax.axis_index("c")]).astype(jnp.int32)

    def body(p_ref, s_ref, got_ref, o_ref):
        f = lambda v: v.astype(F32)
        o_ref[...] = f(s_ref[0]) + f(got_ref[0]) + f(got_ref[1]) + f(got_ref[2])

    return pl.pallas_call(
        body, grid_spec=pltpu.PrefetchScalarGridSpec(
            num_scalar_prefetch=1, grid=(nb,),
            in_specs=[pl.BlockSpec((1, tr, cols), lambda i, p_ref: (p_ref[0], i, 0)),
                      pl.BlockSpec((3, tr, cols), lambda i, p_ref: (0, i, 0))],
            out_specs=pl.BlockSpec((tr, cols), lambda i, p_ref: (p_ref[1] * nb + i, 0))),
        out_shape=jax.ShapeDtypeStruct((2 * hr, cols), F32), name=name,
        compiler_params=_cp(("arbitrary",)))(place, sums, got)


def _rs_chip_copies(ins, got, send, recv):
    x, y, c = _place()
    peers = ((1 - x, y), (x, 1 - y), (1 - x, 1 - y))
    return [pltpu.make_async_remote_copy(src_ref=ins[j].at[2 * px + py], dst_ref=got[j].at[k],
                                         send_sem=send.at[3 * j + k], recv_sem=recv.at[3 * j + k],
                                         device_id=(px, py, c), device_id_type=MESH)
            for j in range(len(ins)) for k, (px, py) in enumerate(peers)]


def _ag_sibling(fulls):
    n = len(fulls)
    nck = 4

    def body(*refs):
        outs = refs[n:2 * n]
        send, recv = refs[2 * n:]
        x, y, c = _place()
        copies = []
        for j in range(n):
            qr = outs[j].shape[0] // (2 * nck)
            for k in range(nck):
                rows = outs[j].at[pl.ds(pl.multiple_of((c * nck + k) * qr, 8), qr), :]
                cp = pltpu.make_async_remote_copy(src_ref=rows, dst_ref=rows, send_sem=send.at[nck * j + k],
                                                  recv_sem=recv.at[nck * j + k], device_id=(x, y, 1 - c),
                                                  device_id_type=MESH)
                cp.start()
                copies.append(cp)
        for cp in copies:
            cp.wait()

    return pl.pallas_call(
        body, in_specs=[_ANY] * n, out_specs=[_ANY] * n,
        out_shape=[jax.ShapeDtypeStruct(f.shape, F32) for f in fulls],
        input_output_aliases={j: j for j in range(n)},
        scratch_shapes=[pltpu.SemaphoreType.DMA((nck * n,)), pltpu.SemaphoreType.DMA((nck * n,))],
        name="ag_sibling")(*fulls)


def _allreduce_small(buf):
    rows = buf.shape[0]
    pr = rows // 8

    def body(in_ref, out_ref, stage, send1, recv1, send2, recv2):
        x, y, c = _place()
        me = 4 * x + 2 * y + c

        def peer(k):
            kx, ky, kc = (k >> 2) & 1, (k >> 1) & 1, k & 1
            return (x ^ kx, y ^ ky, c ^ kc)

        def piece(ref, idx):
            return ref.at[pl.ds(pl.multiple_of(idx * pr, 8), pr), :]

        copies = []
        for k in range(1, 8):
            px, py, pc = peer(k)
            cp = pltpu.make_async_remote_copy(src_ref=piece(in_ref, 4 * px + 2 * py + pc), dst_ref=stage.at[k],
                                              send_sem=send1.at[k], recv_sem=recv1.at[k],
                                              device_id=(px, py, pc), device_id_type=MESH)
            cp.start()
            copies.append(cp)
        for cp in copies:
            cp.wait()
        acc = piece(in_ref, me)[...]
        for k in range(1, 8):
            acc = acc + stage[k]
        piece(out_ref, me)[...] = acc
        copies = []
        for k in range(1, 8):
            cp = pltpu.make_async_remote_copy(src_ref=piece(out_ref, me), dst_ref=piece(out_ref, me),
                                              send_sem=send2.at[k], recv_sem=recv2.at[k],
                                              device_id=peer(k), device_id_type=MESH)
            cp.start()
            copies.append(cp)
        for cp in copies:
            cp.wait()

    vm = pl.BlockSpec(memory_space=pltpu.VMEM)
    return pl.pallas_call(
        body, in_specs=[vm], out_specs=vm, out_shape=jax.ShapeDtypeStruct((rows, D), F32),
        scratch_shapes=[pltpu.VMEM((8, pr, D), F32)] + [pltpu.SemaphoreType.DMA((8,))] * 4,
        name="allreduce_small", compiler_params=_cp())(buf)


def _rows(a):
    flat = a.reshape(-1)
    pad = (-flat.shape[0]) % D
    if pad:
        flat = jnp.concatenate([flat, jnp.zeros((pad,), flat.dtype)])
    return flat.reshape(-1, D)


def _pad_rows(a, mult):
    pad = (-a.shape[0]) % mult
    return jnp.concatenate([a, jnp.zeros((pad, a.shape[1]), a.dtype)]) if pad else a


def _local_step(x, c, ctx, c_ctx, tgt, me, shard, sh, b_mod, b_in, norm_g, cb, wr, wi, ln_g, ln_b):
    t_lat, t_ctx = x.shape[0], ctx.shape[0]
    nlt = t_lat // 256
    nlb, ncb = t_lat // RB, t_ctx // RB
    mc = 3 * D // NSH
    mod_all, cc_all = _mod_tp(jnp.zeros((8, D), F32).at[0].set(c), c_ctx.reshape(1, D), sh["w_mod"],
                              lax.dynamic_slice_in_dim(b_mod, shard * mc, mc, axis=1))
    mod = jnp.zeros((16, 3 * D), F32).at[0].set(mod_all[me]).at[1].set(mod_all[8])
    cc = jnp.zeros((16, D), F32).at[0].set(c).at[1].set(c_ctx)
    x_all = jnp.concatenate([x, ctx], axis=0)
    z, w_in_g, nar = _inproj_ag(x_all, mod, sh["w_in"], b_in, sh["narrow"], nlt)
    nar = jnp.transpose(nar, (1, 0, 2)).reshape(-1, D)
    lbl, cw, br, bi, lam = nar[0:4].reshape(2, 2, D), nar[4:8], nar[8:10], nar[10:12], nar[12:14]
    o0, st0, (w_mod_g, p_a, p_b, w_out) = _gla_fwd(z, lbl, 0, nlb, ncb,
                                                   gather=[sh[k] for k in ("w_mod", "p_a", "p_b", "w_out")])
    p_a, p_b, w_out = p_a.reshape(D, D), p_b.reshape(D, D), w_out.reshape(D, D)
    o1, st1, _ = _gla_fwd(z, lbl, 1, nlb, ncb)
    h0, hp0, hpc0 = _rglru_fwd(z, cw, cb, wr, br, wi, bi, lam, 0, t_lat, t_ctx)
    h1, hp1, hpc1 = _rglru_fwd(z, cw, cb, wr, br, wi, bi, lam, 1, t_lat, t_ctx)
    (do, dh, dz4, dz6, dz7, dz8, gxres, y, dout, oa, dpa, obv, dpb, acc) = _merge(
        o0, o1, h0, h1, z, x_all, tgt, mod, norm_g, ln_g, ln_b, p_a, p_b, w_out, t_lat)
    gp_a = _wgrad(oa, dpa, "wgrad_pa")
    gp_b = _wgrad(obv, dpb, "wgrad_pb")
    gw_out = _wgrad(y, dout, "wgrad_wout")
    dxc_lat, dxc_ctx, dwr0, dwi0, sb0 = _rglru_bwd(z, cw, cb, wr, br, wi, bi, lam, dh, hp0, hpc0, 0, t_lat, t_ctx)
    dz5_lat, dz5_ctx, dwr1, dwi1, sb1 = _rglru_bwd(z, cw, cb, wr, br, wi, bi, lam, dh, hp1, hpc1, 1, t_lat, t_ctx,
                                                   prev=(dxc_lat, dxc_ctx))
    dq0, dv0, dz1, sa0 = _gla_bwd(z, lbl, do, st0, 0, nlb, ncb)
    dz0, dz3, dz2, sa1 = _gla_bwd(z, lbl, do, st1, 1, nlb, ncb, prev=(dq0, dv0))
    zc = jnp.zeros((t_ctx, D), BF)
    pad = lambda a: jnp.concatenate([a, zc], axis=0)
    dz = jnp.concatenate([dz0, dz1, dz2, dz3, pad(dz4), jnp.concatenate([dz5_lat, dz5_ctx], axis=0).astype(BF),
                          pad(dz6), pad(dz7), pad(dz8)], axis=1)
    big = dict(w_in=_wgrad_in(x_all, mod, dz, nlt), p_a=gp_a.reshape(NSH, D // NSH, D),
               p_b=gp_b.reshape(NSH, D // NSH, D), w_out=gw_out.reshape(NSH, D // NSH, D))
    grads = [big[k] for k in _RS]
    sums = [_rs_add1(g, b, f"rs_add1_{j}") for j, (g, b) in enumerate(zip(grads, _rs_sibling(grads)))]
    gx, dm, got = _du(dz, w_in_g, x_all, mod, gxres, nlt, sums)
    fulls = [_rs_add2(a, b, f"rs_add2_{j}") for j, (a, b) in enumerate(zip(sums, got))]
    big = dict(zip(_RS, _ag_sibling(fulls)))
    dmod = jnp.zeros((16, 3 * D), F32).at[0:2, 0:2 * D].set(dm[0:2]).at[0, 2 * D:].set(acc[0])
    dcc = _mod_bwd(cc, dmod, w_mod_g)
    small = dict(
        c_ctx=dcc[1:2], b_mod=(dmod[0] + dmod[1]).reshape(3, D),
        b_in=jnp.stack([sa1[2], sa0[0], sa1[0], sa1[3], acc[3], sb1[4], acc[4], acc[5], acc[6]]),
        lb_logits=jnp.stack([sa0[1], sa1[1], -sa0[1], -sa1[1]]),
        norm_a_g=acc[7:8], conv_w=sb1[8:12], conv_b=sb1[3:4],
        w_r=jnp.stack([dwr0, dwr1]).reshape(-1, D), w_i=jnp.stack([dwi0, dwi1]).reshape(-1, D),
        b_r=jnp.stack([sb0[0], sb1[0]]), b_i=jnp.stack([sb0[1], sb1[1]]), lam=jnp.stack([sb0[2], sb1[2]]),
        ln_g=acc[1:2], ln_b=acc[2:3])
    return acc[9, 0], gx, big, small, dmod, cc_all


_RS =("w_in", "p_a", "p_b", "w_out")
_SMALL =("c_ctx", "b_mod", "b_in", "lb_logits", "norm_a_g", "conv_w", "conv_b", "w_r", "w_i", "b_r", "b_i", "lam",
          "ln_g", "ln_b")
_BIG = ("w_mod", "w_in", "p_a", "p_b", "w_out")
_COL_SHARDED = ("lb_logits", "conv_w", "b_r", "b_i", "lam")
_WEIGHTS = ("c_ctx", "w_mod", "b_mod", "w_in", "b_in", "lb_logits", "norm_a_g", "conv_w", "conv_b", "w_r", "b_r", "w_i",
            "b_i", "lam", "p_a", "p_b", "w_out", "ln_g", "ln_b")


def kernel(x, c, ctx, c_ctx, w_mod, b_mod, w_in, b_in, lb_logits, norm_a_g, conv_w, conv_b, w_r, b_r, w_i, b_i, lam, p_a, p_b, w_out, ln_g, ln_b, loss_target, m_c_ctx, m_w_mod, m_b_mod, m_w_in, m_b_in, m_lb_logits, m_norm_a_g, m_conv_w, m_conv_b, m_w_r, m_b_r, m_w_i, m_b_i, m_lam, m_p_a, m_p_b, m_w_out, m_ln_g, m_ln_b, v_c_ctx, v_w_mod, v_b_mod, v_w_in, v_b_in, v_lb_logits, v_norm_a_g, v_conv_w, v_conv_b, v_w_r, v_b_r, v_w_i, v_b_i, v_lam, v_p_a, v_p_b, v_w_out, v_ln_g, v_ln_b):
    w = dict(c_ctx=c_ctx, w_mod=w_mod, b_mod=b_mod, w_in=w_in, b_in=b_in, lb_logits=lb_logits, norm_a_g=norm_a_g,
             conv_w=conv_w, conv_b=conv_b, w_r=w_r, b_r=b_r, w_i=w_i, b_i=b_i, lam=lam, p_a=p_a, p_b=p_b, w_out=w_out,
             ln_g=ln_g, ln_b=ln_b)
    m = dict(c_ctx=m_c_ctx, w_mod=m_w_mod, b_mod=m_b_mod, w_in=m_w_in, b_in=m_b_in, lb_logits=m_lb_logits,
             norm_a_g=m_norm_a_g, conv_w=m_conv_w, conv_b=m_conv_b, w_r=m_w_r, b_r=m_b_r, w_i=m_w_i, b_i=m_b_i,
             lam=m_lam, p_a=m_p_a, p_b=m_p_b, w_out=m_w_out, ln_g=m_ln_g, ln_b=m_ln_b)
    v = dict(c_ctx=v_c_ctx, w_mod=v_w_mod, b_mod=v_b_mod, w_in=v_w_in, b_in=v_b_in, lb_logits=v_lb_logits,
             norm_a_g=v_norm_a_g, conv_w=v_conv_w, conv_b=v_conv_b, w_r=v_w_r, b_r=v_b_r, w_i=v_w_i, b_i=v_b_i,
             lam=v_lam, p_a=v_p_a, p_b=v_p_b, w_out=v_w_out, ln_g=v_ln_g, ln_b=v_ln_b)
    shard = 2 * lax.axis_index("x") + lax.axis_index("y")
    cs = D // NSH

    sh = {k: w[k][0].astype(BF) for k in _BIG}
    sh["narrow"] = _pad_rows(jnp.concatenate([lb_logits.reshape(4, cs), conv_w[0], b_r[0], b_i[0], lam[0]], axis=0), 8)
    me = 2 * shard + lax.axis_index("c")
    loss, gx, big, small, dmod, cc_all = _local_step(
        x[0], c[0], ctx[0], c_ctx, loss_target[0], me, shard, sh, b_mod, b_in, norm_a_g, conv_b, w_r[0], w_i[0],
        ln_g, ln_b)
    loss = lax.psum(loss, ("x", "y", "c"))

    dmod_rows = jnp.zeros((16, 3 * D), F32).at[me].set(dmod[0]).at[8].set(dmod[1]).reshape(48, D)
    sizes = [small[k].shape[0] for k in _SMALL]
    red = _allreduce_small(_pad_rows(jnp.concatenate([_pad_rows(small[k], 8) for k in _SMALL] + [dmod_rows],
                                                     axis=0), 64))
    grads = {}
    off = 0
    for k, n in zip(_SMALL, sizes):
        g = red[off:off + n]
        off += n + (-n) % 8
        if k == "norm_a_g":
            g = g[:, :DH]
        if k in _COL_SHARDED:
            g = lax.dynamic_slice_in_dim(g, shard * cs, cs, axis=1)
        grads[k] = g.reshape(w[k].shape)
    for k in _RS:
        grads[k] = big[k].reshape(w[k].shape)
    dmod_all = red[off:off + 48].reshape(16, 3 * D)
    mc = 3 * D // NSH
    grads["w_mod"] = _wmod_grad(cc_all, lax.dynamic_slice_in_dim(dmod_all, shard * mc, mc, axis=1)).reshape(
        w["w_mod"].shape)

    delta, new_m, new_v = {}, {}, {}
    for k in _BIG:
        shp = w[k].shape
        two = lambda a: a.reshape(shp[-2], shp[-1])
        d_, m_, v_ = _adamw(two(w[k]), two(grads[k]), two(m[k]), two(v[k]), f"adamw_{k}")
        delta[k], new_m[k], new_v[k] = d_.reshape(shp), m_.reshape(shp), v_.reshape(shp)
    flat = lambda t: jnp.concatenate([_pad_rows(_rows(t[k]), 8) for k in _SMALL], axis=0)
    d_, m_, v_ = _adamw(flat(w), flat(grads), flat(m), flat(v), "adamw_small")
    off = 0
    for k in _SMALL:
        n = w[k].size
        nr = -(-n // D)
        for src, dst in ((d_, delta), (m_, new_m), (v_, new_v)):
            dst[k] = src[off:off + nr].reshape(-1)[:n].reshape(w[k].shape)
        off += nr + (-nr) % 8

    return (loss, gx[None], *[grads[k] for k in _WEIGHTS], *[delta[k] for k in _WEIGHTS],
            *[new_m[k] for k in _WEIGHTS], *[new_v[k] for k in _WEIGHTS])
```

```python
import functools

import jax
import jax.numpy as jnp
from jax import lax
from jax.experimental import pallas as pl
from jax.experimental.pallas import tpu as pltpu

F32 = jnp.float32
BF = jnp.bfloat16
MESH = pl.DeviceIdType.MESH

D = 1024
H = 8
DH = 128
CH = 64
RB = 256
NCK = RB // CH
GRID_W = 64
CB = 8
RCH = 16
IN_COLS = 9 * D
NSH = 4
SHC = IN_COLS // NSH
RG_C = 8.0
ALPHA = 2.0 ** 0.25
LN_EPS = 1e-5
RMS_EPS = 1e-6
Q_SCALE = DH ** -0.5
ADAM_LR, ADAM_B1, ADAM_B2, ADAM_EPS, ADAM_WD, ADAM_STEP = 0.001, 0.9, 0.999, 1e-08, 0.01, 10
VMEM_LIMIT = 56 * 1024 * 1024
VMEM_LIMIT_MERGE = 60 * 1024 * 1024


def _cp(sem=None, vmem=VMEM_LIMIT):
    return pltpu.CompilerParams(dimension_semantics=sem, vmem_limit_bytes=vmem)


def _sig(x):
    return 0.5 * jnp.tanh(0.5 * x) + 0.5


def _dot(a, b):
    return jnp.dot(a.astype(BF), b.astype(BF), preferred_element_type=F32)


def _dot_nt(a, b):
    return lax.dot_general(a.astype(BF), b.astype(BF), (((1,), (1,)), ((), ())), preferred_element_type=F32)


def _dot_tn(a, b):
    return lax.dot_general(a.astype(BF), b.astype(BF), (((0,), (0,)), ((), ())), preferred_element_type=F32)


def _colsum(v):
    return jnp.sum(v, axis=0, keepdims=True)


def _mod_bwd(cc, dmod, w_mod_g):
    def body(cc_ref, dm_ref, w_ref, dcc_ref):
        v = cc_ref[...]
        sg = _sig(v)
        ds = jnp.zeros((16, D), F32)
        for k in range(NSH):
            ds = ds + _dot_nt(dm_ref[:, k * 768:(k + 1) * 768], w_ref[k])
        dcc_ref[...] = ds * (sg * (1.0 + v * (1.0 - sg)))
    return pl.pallas_call(body, out_shape=jax.ShapeDtypeStruct((16, D), F32),
                          name="mod_bwd", compiler_params=_cp())(cc, dmod, w_mod_g)


def _wmod_grad(cc, dmod_cols):
    def body(cc_ref, dm_ref, dw_ref):
        v = cc_ref[...]
        dw_ref[...] = _dot_tn(v * _sig(v), dm_ref[...])
    return pl.pallas_call(body, out_shape=jax.ShapeDtypeStruct((D, dmod_cols.shape[1]), F32),
                          name="wmod_grad", compiler_params=_cp())(cc, dmod_cols)


def _modulate(x_ref, mod_ref, is_lat):
    sh = jnp.where(is_lat, mod_ref[0:1, 0:D], mod_ref[1:2, 0:D])
    sc = jnp.where(is_lat, mod_ref[0:1, D:2 * D], mod_ref[1:2, D:2 * D])
    return x_ref[...] * (1.0 + sc) + sh, sc


def _gla_mask(d):
    row = lax.broadcasted_iota(jnp.int32, (CH, CH), 0)
    col = lax.broadcasted_iota(jnp.int32, (CH, CH), 1)
    return (row <= col) if d else (row >= col)


def _chunk_cumsum(v, rev):
    n = v.shape[0]
    pos = lax.broadcasted_iota(jnp.int32, v.shape, 0) & (CH - 1)
    s = 1
    while s < CH:
        if rev:
            v = v + jnp.where(pos < CH - s, pltpu.roll(v, n - s, 0), 0.0)
        else:
            v = v + jnp.where(pos >= s, pltpu.roll(v, s, 0), 0.0)
        s *= 2
    return v


def _chunk_rows(c):
    return slice(c * CH, (c + 1) * CH)


def _gla_features(zq, zf, lb, d):
    sq = _sig(zq)
    q = zq * sq * Q_SCALE
    sf = _sig(zf)
    f = lb + (1.0 - lb) * sf
    k = 1.0 - f
    g = _chunk_cumsum(jnp.log(f), d)
    last = 0 if d else CH - 1
    gls = [g[c * CH + last:c * CH + last + 1, :] for c in range(NCK)]
    glb = jnp.concatenate([jnp.broadcast_to(gl, (CH, D)) for gl in gls], axis=0)
    eg, eig, eeg = jnp.exp(g), jnp.exp(-g), jnp.exp(glb - g)
    decs = [jnp.exp(gl) for gl in gls]
    return sq, sf, f, q * eg, k * eig, k * eeg, eg, eig, eeg, decs


def _lower_bound(lbl_ref, d):
    return _sig(lbl_ref[0, d:d + 1, :] - lbl_ref[1, d:d + 1, :])


def _gla_rb(d, nlb, ncb):
    nrb = nlb + ncb
    if d:
        return lambda s: nrb - 1 - s
    return lambda s: jnp.where(s < ncb, nlb + s, s - ncb)


def _gla_fwd(z, lbl, d, nlb, ncb, gather=()):
    m = z.shape[0]
    nrb = nlb + ncb
    rb = _gla_rb(d, nlb, ncb)
    ng = len(gather)

    def body(*refs):
        q_ref, f_ref, v_ref, lbl_ref = refs[:4]
        ag_in = refs[4:4 + ng]
        o_ref, st_ref = refs[4 + ng:6 + ng]
        ag_out = refs[6 + ng:6 + 2 * ng]
        S = refs[6 + 2 * ng]
        ag_sems = refs[7 + 2 * ng:]
        s = pl.program_id(0)

        @pl.when(s == 0)
        def _():
            S[...] = jnp.zeros_like(S)
            if ng:
                local, chip, _ = _ag_copies(ag_in, ag_out, *ag_sems)
                for cp in chip + local:
                    cp.start()

        lb = _lower_bound(lbl_ref, d)
        mb = _gla_mask(d)
        _, _, _, qd, ki, ke, _, _, _, decs = _gla_features(q_ref[...], f_ref[...], lb, d)
        qd, ki, ke, v = qd.astype(BF), ki.astype(BF), ke.astype(BF), v_ref[...].astype(BF)
        order = range(NCK - 1, -1, -1) if d else range(NCK)
        for h in range(H):
            hs = slice(h * DH, (h + 1) * DH)
            intra, upd = {}, {}
            for c in range(NCK):
                rs = _chunk_rows(c)
                a = jnp.where(mb, _dot_nt(qd[rs, hs], ki[rs, hs]), 0.0)
                intra[c] = _dot(a, v[rs, hs])
                upd[c] = _dot_tn(v[rs, hs], ke[rs, hs])
            st = S[h]
            for c in order:
                rs = _chunk_rows(c)
                st_ref[c, h] = st
                o_ref[rs, hs] = intra[c] + _dot_nt(qd[rs, hs], st)
                st = st * decs[c][:, hs] + upd[c]
            S[h] = st

        if ng:
            @pl.when(s == nrb - 1)
            def _():
                _ag_finish(*_ag_copies(ag_in, ag_out, *ag_sems))

    def zspec(cb):
        return pl.BlockSpec((RB, D), lambda s: (rb(s), cb))

    outs = pl.pallas_call(
        body, grid=(nrb,),
        in_specs=[zspec(0), zspec(1 + d), zspec(3), pl.BlockSpec((2, 2, D), lambda s: (0, 0, 0))] + [_ANY] * ng,
        out_specs=[pl.BlockSpec((RB, D), lambda s: (rb(s), 0)),
                   pl.BlockSpec((NCK, H, DH, DH), lambda s: (rb(s), 0, 0, 0))] + [_ANY] * ng,
        out_shape=[jax.ShapeDtypeStruct((m, D), F32), jax.ShapeDtypeStruct((m // CH, H, DH, DH), F32)]
        + [jax.ShapeDtypeStruct((NSH,) + g.shape, g.dtype) for g in gather],
        scratch_shapes=[pltpu.VMEM((H, DH, DH), F32)] + (_ag_sems(ng) if ng else []),
        name=f"gla_fwd{d}", compiler_params=_cp(("arbitrary",)))(z, z, z, lbl, *gather)
    return outs[0], outs[1], list(outs[2:])


def _gla_bwd(z, lbl, do_lat, states, d, nlb, ncb, prev=None):
    m = z.shape[0]
    nrb = nlb + ncb
    fwd_rb = _gla_rb(d, nlb, ncb)
    rb = lambda s: fwd_rb(nrb - 1 - s)
    last = prev is not None

    def body(*refs):
        if last:
            q_ref, f_ref, v_ref, lbl_ref, do_ref, st_ref, pq_ref, pv_ref, o0_ref, o1_ref, o2_ref, sum_ref, dS = refs
        else:
            q_ref, f_ref, v_ref, lbl_ref, do_ref, st_ref, o0_ref, o1_ref, o2_ref, sum_ref, dS = refs
        s = pl.program_id(0)
        is_lat = rb(s) < nlb

        @pl.when(s == 0)
        def _():
            dS[...] = jnp.zeros_like(dS)
            sum_ref[...] = jnp.zeros_like(sum_ref)

        lb = _lower_bound(lbl_ref, d)
        mb = _gla_mask(d)
        zq = q_ref[...]
        sq, sf, f, qd, ki, ke, eg, eig, eeg, decs = _gla_features(zq, f_ref[...], lb, d)
        qdb, kib, keb, vb = qd.astype(BF), ki.astype(BF), ke.astype(BF), v_ref[...].astype(BF)
        dob = jnp.where(is_lat, do_ref[...], 0.0).astype(BF)
        order = range(NCK) if d else range(NCK - 1, -1, -1)
        dqd_h, dki_h, dke_h, dv_h, ddec_h = [], [], [], [], []
        for h in range(H):
            hs = slice(h * DH, (h + 1) * DH)
            dqd, dki, inc, dvi = {}, {}, {}, {}
            for c in range(NCK):
                rs = _chunk_rows(c)
                a = jnp.where(mb, _dot_nt(qdb[rs, hs], kib[rs, hs]), 0.0)
                da = jnp.where(mb, _dot_nt(dob[rs, hs], vb[rs, hs]), 0.0)
                dqd[c] = _dot(da, kib[rs, hs]) + _dot(dob[rs, hs], st_ref[c, h])
                dki[c] = _dot_tn(da, qdb[rs, hs])
                inc[c] = _dot_tn(dob[rs, hs], qdb[rs, hs])
                dvi[c] = _dot_tn(a, dob[rs, hs])
            dst = dS[h]
            dke, dv, ddec = {}, {}, {}
            for c in order:
                rs = _chunk_rows(c)
                dv[c] = dvi[c] + _dot_nt(keb[rs, hs], dst)
                dke[c] = _dot(vb[rs, hs], dst)
                ddec[c] = _colsum(st_ref[c, h] * dst)
                dst = inc[c] + dst * decs[c][:, hs]
            dS[h] = dst
            cat = lambda t: jnp.concatenate([t[c] for c in range(NCK)], axis=0)
            dqd_h.append(cat(dqd))
            dki_h.append(cat(dki))
            dke_h.append(cat(dke))
            dv_h.append(cat(dv))
            ddec_h.append([ddec[c] for c in range(NCK)])
        lanes = lambda parts: jnp.concatenate(parts, axis=1)
        dqd, dki, dke, dv = lanes(dqd_h), lanes(dki_h), lanes(dke_h), lanes(dv_h)
        dq = dqd * eg
        dk = dki * eig + dke * eeg
        dke_ke = dke * ke
        dg = dqd * qd - dki * ki - dke_ke
        dgl = [_colsum(dke_ke[_chunk_rows(c), :]) + lanes([ddec_h[h][c] for h in range(H)]) * decs[c]
               for c in range(NCK)]
        dglb = jnp.concatenate([jnp.broadcast_to(t, (CH, D)) for t in dgl], axis=0)
        df = (_chunk_cumsum(dg, 1 - d) + dglb) / f - dk
        dzf = df * (1.0 - lb) * (sf * (1.0 - sf))
        sum_ref[0:1, :] += _colsum(dzf)
        sum_ref[1:2, :] += _colsum(df * (1.0 - sf))
        if last:
            dz0 = (dq + pq_ref[...]) * (Q_SCALE * (sq * (1.0 + zq * (1.0 - sq))))
            dz3 = dv + pv_ref[...]
            sum_ref[2:3, :] += _colsum(dz0)
            sum_ref[3:4, :] += _colsum(dz3)
            o0_ref[...] = dz0.astype(BF)
            o1_ref[...] = dz3.astype(BF)
        else:
            o0_ref[...] = dq
            o1_ref[...] = dv
        o2_ref[...] = dzf.astype(BF)

        @pl.when(s == nrb - 1)
        def _():
            sum_ref[1:2, :] = sum_ref[1:2, :] * (lb * (1.0 - lb))

    def zspec(cb):
        return pl.BlockSpec((RB, D), lambda s: (rb(s), cb))

    rowspec = pl.BlockSpec((RB, D), lambda s: (rb(s), 0))
    in_specs = [zspec(0), zspec(1 + d), zspec(3), pl.BlockSpec((2, 2, D), lambda s: (0, 0, 0)),
                pl.BlockSpec((RB, D), lambda s: (jnp.minimum(rb(s), nlb - 1), 0)),
                pl.BlockSpec((NCK, H, DH, DH), lambda s: (rb(s), 0, 0, 0))]
    args = [z, z, z, lbl, do_lat, states]
    if last:
        in_specs += [rowspec, rowspec]
        args += list(prev)
    dt01 = BF if last else F32
    return pl.pallas_call(
        body, grid=(nrb,), in_specs=in_specs,
        out_specs=(rowspec, rowspec, rowspec, pl.BlockSpec((8, D), lambda s: (0, 0))),
        out_shape=(jax.ShapeDtypeStruct((m, D), dt01), jax.ShapeDtypeStruct((m, D), dt01),
                   jax.ShapeDtypeStruct((m, D), BF), jax.ShapeDtypeStruct((8, D), F32)),
        scratch_shapes=[pltpu.VMEM((H, DH, DH), F32)],
        name=f"gla_bwd{d}", compiler_params=_cp(("arbitrary",)))(*args)


def _shift_rows(v, k, r):
    row = lax.broadcasted_iota(jnp.int32, v.shape, 0)
    rolled = pltpu.roll(v, k % r, 0)
    return jnp.where((row >= k) & (row < r + k), rolled, 0.0)


def _conv_fwd(xb, cw_ref, cb_ref, r):
    return (cb_ref[...] + _shift_rows(xb, 1, r) * cw_ref[0:1, :] + xb * cw_ref[1:2, :]
            + _shift_rows(xb, -1, r) * cw_ref[2:3, :] + _shift_rows(xb, -2, r) * cw_ref[3:4, :])


def _window(ref, lo, n, rows):
    parts = []
    if lo < 0:
        parts.append(jnp.zeros((-lo,) + tuple(ref.shape[1:]), F32))
    parts.append(ref[max(lo, 0):min(lo + n, rows)])
    if lo + n > rows:
        parts.append(jnp.zeros((lo + n - rows,) + tuple(ref.shape[1:]), F32))
    return parts[0] if len(parts) == 1 else jnp.concatenate(parts, axis=0)


def _conv_cols(x_ref, cw_ref, cb_ref, r0, n, rows):
    w = _window(x_ref, r0 - 1, n + 3, rows)
    return (cb_ref[...] + w[0:n] * cw_ref[0:1, :] + w[1:n + 1] * cw_ref[1:2, :] + w[2:n + 2] * cw_ref[2:3, :]
            + w[3:n + 3] * cw_ref[3:4, :])


def _gate_a(xc, wr_ref, br, sp):
    xcb = xc.astype(BF)
    rs = [jnp.dot(xcb[:, g * DH:(g + 1) * DH], wr_ref[g].astype(BF), preferred_element_type=F32) for g in range(H)]
    return jnp.exp((-RG_C * _sig(jnp.concatenate(rs, axis=1) + br)) * sp)


def _softplus_neg(lam):
    y = jnp.exp(-jnp.abs(lam))
    u = 1.0 + y
    tiny = u == 1.0
    l1p = jnp.where(tiny, y, jnp.log(u) * (y / jnp.where(tiny, 1.0, u - 1.0)))
    return jnp.maximum(-lam, 0.0) + l1p


def _gates(xc, wr_ref, br, wi_ref, bi, sp):
    xcb = xc.astype(BF)
    rs, is_ = [], []
    for g in range(H):
        gs = slice(g * DH, (g + 1) * DH)
        rs.append(jnp.dot(xcb[:, gs], wr_ref[g].astype(BF), preferred_element_type=F32))
        is_.append(jnp.dot(xcb[:, gs], wi_ref[g].astype(BF), preferred_element_type=F32))
    r = _sig(jnp.concatenate(rs, axis=1) + br)
    i = _sig(jnp.concatenate(is_, axis=1) + bi)
    log_a = (-RG_C * r) * sp
    a = jnp.exp(log_a)
    t = jnp.tanh(log_a)
    om = (-2.0 * t) / (1.0 - t)
    return r, i, a, om


def _scan_rows(d, nrows, a_s, b_s, h_s, h0):
    nsl = nrows // 8

    def slab(j, h):
        jj = (nsl - 1 - j) if d else j
        r0 = pl.multiple_of(jj * 8, 8)
        for t in (range(7, -1, -1) if d else range(8)):
            h = a_s[pl.ds(r0 + t, 1), :] * h + b_s[pl.ds(r0 + t, 1), :]
            h_s[pl.ds(r0 + t, 1), :] = h
        return h

    return lax.fori_loop(0, nsl, slab, h0)


def _col_of(d, ncols):
    if d:
        return lambda s: ncols - jnp.maximum(s, 1)
    return lambda s: jnp.maximum(s, 1) - 1


def _rglru_fwd(z, cw, cb, wr, br, wi, bi, lam, d, t_lat, t_ctx):
    m = z.shape[0]
    rows = t_lat // GRID_W
    z3 = z.reshape(m // GRID_W, GRID_W, IN_COLS)
    nblk = GRID_W // CB
    blk = _col_of(d, nblk)
    cblk = t_lat // t_ctx
    rc = min(RCH, rows)

    def body(zc_ref, zx_ref, cw_ref, cb_ref, wr_ref, br_ref, wi_ref, bi_ref, lam_ref,
             hx_ref, hpx_ref, hpc_ref, a_s, b_s, h_s, hcar, a3, b3, cin_s):
        s = pl.program_id(0)
        sp = _softplus_neg(lam_ref[d:d + 1, :])
        br_ = br_ref[d:d + 1, :]
        bi_ = bi_ref[d:d + 1, :]

        @pl.when(s == 0)
        def _():
            xc = _conv_fwd(zc_ref[...], cw_ref, cb_ref, t_ctx)
            _, i, a, om = _gates(xc, wr_ref, br_, wi_ref, bi_, sp)
            a_s[...] = a
            b_s[...] = jnp.sqrt(om) * (i * xc)
            h0 = jnp.zeros((1, D), F32)
            hcar[0:1, :] = _scan_rows(d, t_ctx, a_s, b_s, h_s, h0)
            hs = h_s[...]
            row = lax.broadcasted_iota(jnp.int32, (t_ctx, D), 0)
            if d:
                hpc_ref[...] = jnp.where(row == t_ctx - 1, h0, pltpu.roll(hs, t_ctx - 1, 0))
            else:
                hpc_ref[...] = jnp.where(row == 0, h0, pltpu.roll(hs, 1, 0))

        @pl.when(s > 0)
        def _():
            for r0 in range(0, rows, rc):
                xc = _conv_cols(zx_ref, cw_ref, cb_ref, r0, rc, rows).reshape(rc * CB, D)
                _, i, a, om = _gates(xc, wr_ref, br_, wi_ref, bi_, sp)
                a3[r0:r0 + rc] = a.reshape(rc, CB, D)
                b3[r0:r0 + rc] = (jnp.sqrt(om) * (i * xc)).reshape(rc, CB, D)

            def local(t, carry):
                hl, p = carry
                r = (rows - 1 - t) if d else t
                a = a3[r]
                hl = a * hl + b3[r]
                p = a * p
                b3[r] = hl
                a3[r] = p
                return hl, p

            hl, p = lax.fori_loop(0, rows, local, (jnp.zeros((CB, D), F32), jnp.ones((CB, D), F32)))
            cin = hcar[0:1, :]
            for j in (range(CB - 1, -1, -1) if d else range(CB)):
                cin_s[j:j + 1, :] = cin
                cin = hl[j:j + 1, :] + p[j:j + 1, :] * cin
            hcar[0:1, :] = cin
            c_in = cin_s[...]

            def fix(t, prev):
                r = (rows - 1 - t) if d else t
                h = b3[r] + a3[r] * c_in
                hx_ref[r] = h
                hpx_ref[r] = prev
                return h

            lax.fori_loop(0, rows, fix, c_in)

    full = lambda shp: pl.BlockSpec(shp, lambda s: (0,) * len(shp))
    colspec = pl.BlockSpec((rows, CB, D), lambda s: (0, blk(s), 0))
    outs = pl.pallas_call(
        body, grid=(nblk + 1,),
        in_specs=[pl.BlockSpec((t_ctx, D), lambda s: (cblk, 5)),
                  pl.BlockSpec((rows, CB, D), lambda s: (0, blk(s), 5)),
                  full((4, D)), full((1, D)),
                  pl.BlockSpec((None, H, DH, DH), lambda s: (d, 0, 0, 0)), full((2, D)),
                  pl.BlockSpec((None, H, DH, DH), lambda s: (d, 0, 0, 0)), full((2, D)), full((2, D))],
        out_specs=(colspec, colspec, full((t_ctx, D))),
        out_shape=(jax.ShapeDtypeStruct((rows, GRID_W, D), F32), jax.ShapeDtypeStruct((rows, GRID_W, D), F32),
                   jax.ShapeDtypeStruct((t_ctx, D), F32)),
        scratch_shapes=[pltpu.VMEM((t_ctx, D), F32), pltpu.VMEM((t_ctx, D), F32), pltpu.VMEM((t_ctx, D), F32),
                        pltpu.VMEM((8, D), F32), pltpu.VMEM((rows, CB, D), F32), pltpu.VMEM((rows, CB, D), F32),
                        pltpu.VMEM((CB, D), F32)],
        name=f"rglru_fwd{d}", compiler_params=_cp(("arbitrary",)))(z, z3, cw, cb, wr, br, wi, bi, lam)
    return outs[0].reshape(t_lat, D), outs[1].reshape(t_lat, D), outs[2]


def _rglru_bwd(z, cw, cb, wr, br, wi, bi, lam, dh_lat, hp_lat, hp_ctx, d, t_lat, t_ctx, prev=None):
    m = z.shape[0]
    rows = t_lat // GRID_W
    z3 = z.reshape(m // GRID_W, GRID_W, IN_COLS)
    nblk = GRID_W // CB
    fblk = _col_of(d, nblk)
    blk = lambda s: fblk(nblk - jnp.minimum(s, nblk - 1))
    cblk = t_lat // t_ctx
    rc = min(RCH, rows)
    last = prev is not None
    view3 = lambda v: v.reshape(rows, GRID_W, D)

    def body(*refs):
        (zc_ref, zx_ref, cw_ref, cb_ref, wr_ref, br_ref, wi_ref, bi_ref, lam_ref, dh_ref, hpx_ref, hpc_ref) = refs[:12]
        k = 12
        if last:
            pdx_ref, pdc_ref = refs[12:14]
            k = 14
        ox_ref, oc_ref, dwr_ref, dwi_ref, sum_ref, a_s, b_s, gcar, a3, b3, cin_s = refs[k:]
        s = pl.program_id(0)
        lam_d = lam_ref[d:d + 1, :]
        sp = _softplus_neg(lam_d)
        br_ = br_ref[d:d + 1, :]
        bi_ = bi_ref[d:d + 1, :]
        flat = lambda v: v.reshape(-1, D)

        @pl.when(s == 0)
        def _():
            gcar[...] = jnp.zeros_like(gcar)
            dwr_ref[...] = jnp.zeros_like(dwr_ref)
            dwi_ref[...] = jnp.zeros_like(dwi_ref)
            sum_ref[...] = jnp.zeros_like(sum_ref)


        def conv_sums(dxc, dxb, xm1, x0, xp1, xp2):
            sum_ref[3:4, :] += _colsum(flat(dxc))
            sum_ref[4:5, :] += _colsum(flat(dxb))
            sum_ref[8:9, :] += _colsum(flat(dxc * xm1))
            sum_ref[9:10, :] += _colsum(flat(dxc * x0))
            sum_ref[10:11, :] += _colsum(flat(dxc * xp1))
            sum_ref[11:12, :] += _colsum(flat(dxc * xp2))

        def gate_grads(g, hp, xc, rr, i, a, om):
            mult = jnp.sqrt(om)
            da = g * hp
            ixc = i * xc
            dmult = g * ixc
            dixc = g * mult
            di = dixc * xc
            dxc = dixc * i
            dlog_a = da * a - dmult * ((1.0 - om) / mult)
            dr = dlog_a * (-RG_C * sp)
            sum_ref[2:3, :] += _colsum(dlog_a * rr)
            drp = dr * rr * (1.0 - rr)
            dip = di * i * (1.0 - i)
            sum_ref[0:1, :] += _colsum(drp)
            sum_ref[1:2, :] += _colsum(dip)
            xcb = xc.astype(BF)
            drb = drp.astype(BF)
            dib = dip.astype(BF)
            parts = []
            for gi in range(H):
                gs = slice(gi * DH, (gi + 1) * DH)
                parts.append(_dot_nt(drb[:, gs], wr_ref[gi]) + _dot_nt(dib[:, gs], wi_ref[gi]))
                dwr_ref[gi] += _dot_tn(xcb[:, gs], drb[:, gs])
                dwi_ref[gi] += _dot_tn(xcb[:, gs], dib[:, gs])
            return dxc + jnp.concatenate(parts, axis=1)

        @pl.when(s < nblk)
        def _():
            for r0 in range(0, rows, rc):
                xc = flat(_conv_cols(zx_ref, cw_ref, cb_ref, r0, rc, rows))
                a = _gate_a(xc, wr_ref, br_, sp).reshape(rc, CB, D)
                a3[r0:r0 + rc] = a
                b3[r0:r0 + rc] = a * dh_ref[r0:r0 + rc]

            def local(t, carry):
                c, q = carry
                r = t if d else (rows - 1 - t)
                a = a3[r]
                c = a * c + b3[r]
                q = a * q
                b3[r] = c
                a3[r] = q
                return c, q

            c, q = lax.fori_loop(0, rows, local, (jnp.zeros((CB, D), F32), jnp.ones((CB, D), F32)))
            cin = gcar[0:1, :]
            for j in (range(CB) if d else range(CB - 1, -1, -1)):
                cin_s[j:j + 1, :] = cin
                cin = c[j:j + 1, :] + q[j:j + 1, :] * cin
            gcar[0:1, :] = cin
            c_in = cin_s[...]
            for r0 in (range(rows - rc, -1, -rc) if d else range(0, rows, rc)):
                if d:
                    lo = max(r0 - 1, 0)
                    cn = b3[lo:r0 + rc - 1] + a3[lo:r0 + rc - 1] * c_in
                    if r0 == 0:
                        cn = jnp.concatenate([c_in[None], cn], axis=0)
                else:
                    hi = min(r0 + rc + 1, rows)
                    cn = b3[r0 + 1:hi] + a3[r0 + 1:hi] * c_in
                    if hi == rows:
                        cn = jnp.concatenate([cn, c_in[None]], axis=0)
                g = flat(dh_ref[r0:r0 + rc] + cn)
                xc = flat(_conv_cols(zx_ref, cw_ref, cb_ref, r0, rc, rows))
                rr, i, a, om = _gates(xc, wr_ref, br_, wi_ref, bi_, sp)
                b3[r0:r0 + rc] = gate_grads(g, flat(hpx_ref[r0:r0 + rc]), xc, rr, i, a, om).reshape(rc, CB, D)
            if last:
                for r0 in range(0, rows, rc):
                    b3[r0:r0 + rc] = b3[r0:r0 + rc] + pdx_ref[r0:r0 + rc]
                for r0 in range(0, rows, rc):
                    w = _window(b3, r0 - 2, rc + 3, rows)
                    xw = _window(zx_ref, r0 - 1, rc + 3, rows)
                    dxc = w[2:rc + 2]
                    dxb = (w[3:rc + 3] * cw_ref[0:1, :] + dxc * cw_ref[1:2, :] + w[1:rc + 1] * cw_ref[2:3, :]
                           + w[0:rc] * cw_ref[3:4, :])
                    conv_sums(dxc, dxb, xw[0:rc], xw[1:rc + 1], xw[2:rc + 2], xw[3:rc + 3])
                    ox_ref[r0:r0 + rc] = dxb
            else:
                for r0 in range(0, rows, rc):
                    ox_ref[r0:r0 + rc] = b3[r0:r0 + rc]

        @pl.when(s == nblk)
        def _():
            r = t_ctx
            xb = zc_ref[...]
            xc = _conv_fwd(xb, cw_ref, cb_ref, r)
            rr, i, a, om = _gates(xc, wr_ref, br_, wi_ref, bi_, sp)
            a_s[...] = a
            b_s[...] = jnp.zeros((r, D), F32)
            c0 = gcar[0:1, :]
            _scan_rows(1 - d, r, a_s, b_s, b_s, c0)
            cs = b_s[...]
            row = lax.broadcasted_iota(jnp.int32, (r, D), 0)
            if d:
                g = jnp.where(row == 0, c0, pltpu.roll(cs, 1, 0))
            else:
                g = jnp.where(row == r - 1, c0, pltpu.roll(cs, r - 1, 0))
            dxc = gate_grads(g, hpc_ref[...], xc, rr, i, a, om)
            if last:
                dxc = dxc + pdc_ref[...]
                dxb = (_shift_rows(dxc, -1, r) * cw_ref[0:1, :] + dxc * cw_ref[1:2, :]
                       + _shift_rows(dxc, 1, r) * cw_ref[2:3, :] + _shift_rows(dxc, 2, r) * cw_ref[3:4, :])
                conv_sums(dxc, dxb, _shift_rows(xb, 1, r), xb, _shift_rows(xb, -1, r), _shift_rows(xb, -2, r))
                oc_ref[...] = dxb
            else:
                oc_ref[...] = dxc
            sum_ref[2:3, :] = sum_ref[2:3, :] * (RG_C * _sig(-lam_d))

    full = lambda shp: pl.BlockSpec(shp, lambda s: (0,) * len(shp))
    once = lambda shp: pl.BlockSpec(shp, lambda s: (0,) * len(shp), pipeline_mode=pl.Buffered(1))
    colspec = pl.BlockSpec((rows, CB, D), lambda s: (0, blk(s), 0))
    colonce = pl.BlockSpec((rows, CB, D), lambda s: (0, blk(s), 0), pipeline_mode=pl.Buffered(1))
    in_specs = [pl.BlockSpec((t_ctx, D), lambda s: (cblk, 5), pipeline_mode=pl.Buffered(1)),
                pl.BlockSpec((rows, CB, D), lambda s: (0, blk(s), 5), pipeline_mode=pl.Buffered(1)),
                full((4, D)), full((1, D)),
                pl.BlockSpec((None, H, DH, DH), lambda s: (d, 0, 0, 0)), full((2, D)),
                pl.BlockSpec((None, H, DH, DH), lambda s: (d, 0, 0, 0)), full((2, D)), full((2, D)),
                colonce, colonce, once((t_ctx, D))]
    args = [z, z3, cw, cb, wr, br, wi, bi, lam, view3(dh_lat), view3(hp_lat), hp_ctx]
    if last:
        in_specs += [colonce, once((t_ctx, D))]
        args += [view3(prev[0]), prev[1]]
    outs = pl.pallas_call(
        body, grid=(nblk + 1,), in_specs=in_specs,
        out_specs=(colspec, full((t_ctx, D)), full((H, DH, DH)), full((H, DH, DH)), full((16, D))),
        out_shape=(jax.ShapeDtypeStruct((rows, GRID_W, D), F32), jax.ShapeDtypeStruct((t_ctx, D), F32),
                   jax.ShapeDtypeStruct((H, DH, DH), F32), jax.ShapeDtypeStruct((H, DH, DH), F32),
                   jax.ShapeDtypeStruct((16, D), F32)),
        scratch_shapes=[pltpu.VMEM((t_ctx, D), F32), pltpu.VMEM((t_ctx, D), F32), pltpu.VMEM((8, D), F32),
                        pltpu.VMEM((rows, CB, D), F32), pltpu.VMEM((rows, CB, D), F32), pltpu.VMEM((CB, D), F32)],
        name=f"rglru_bwd{d}", compiler_params=_cp(("arbitrary",)))(*args)
    return (outs[0].reshape(t_lat, D), outs[1]) + tuple(outs[2:])


def _merge(o_f, o_b, h_f, h_b, z, x_all, tgt, mod, norm_g, ln_g, ln_b, p_a, p_b, w_out, t_lat):
    tm = 256
    nt = t_lat // tm

    def body(of_ref, ob_ref, hf_ref, hb_ref, z4_ref, z6_ref, z7_ref, z8_ref, x_ref, t_ref, mod_ref, ng_ref,
             lg_ref, lb_ref, pa_ref, pb_ref, wo_ref,
             do_ref, dh_ref, dz4_ref, dz6_ref, dz7_ref, dz8_ref, gx_ref,
             y_ref, dout_ref, oa_ref, dpa_ref, obv_ref, dpb_ref, acc_ref):
        i = pl.program_id(0)

        @pl.when(i == 0)
        def _():
            acc_ref[...] = jnp.zeros_like(acc_ref)

        def per_head(v):
            return jnp.concatenate(
                [jnp.broadcast_to(jnp.mean(v[:, h * DH:(h + 1) * DH], axis=-1, keepdims=True), (tm, DH))
                 for h in range(H)], axis=1)

        gt = mod_ref[0:1, 2 * D:3 * D]
        gfull = jnp.concatenate([ng_ref[...]] * H, axis=1)
        o = of_ref[...] + ob_ref[...]
        rinv = lax.rsqrt(per_head(o * o) + RMS_EPS)
        n = o * rinv
        na = n * gfull
        z4 = z4_ref[...]
        s4 = _sig(z4)
        silu4 = z4 * s4
        oa = na * silu4
        z6 = z6_ref[...]
        s6 = _sig(z6)
        silu6 = z6 * s6
        hsum = hf_ref[...] + hb_ref[...]
        obv = hsum * silu6
        pa = _dot(oa, pa_ref[...])
        pb = _dot(obv, pb_ref[...])
        s7 = _sig(z7_ref[...])
        s8 = _sig(z8_ref[...])
        y = s7 * pa + s8 * pb
        out = _dot(y, wo_ref[...])
        pre = ALPHA * x_ref[...] + gt * out
        mu = jnp.mean(pre, axis=-1, keepdims=True)
        xc = pre - mu
        rstd = lax.rsqrt(jnp.mean(xc * xc, axis=-1, keepdims=True) + LN_EPS)
        xhat = xc * rstd
        lg = lg_ref[...]
        diff = xhat * lg + lb_ref[...] - t_ref[...]
        acc_ref[8:9, :] += _colsum(diff * diff) * (0.5 / D)
        dxn = diff * (1.0 / D)
        acc_ref[1:2, :] += _colsum(dxn * xhat)
        acc_ref[2:3, :] += _colsum(dxn)
        dxhat = dxn * lg
        dpre = rstd * (dxhat - jnp.mean(dxhat, axis=-1, keepdims=True)
                       - xhat * jnp.mean(dxhat * xhat, axis=-1, keepdims=True))
        gx_ref[...] = ALPHA * dpre
        acc_ref[0:1, :] += _colsum(dpre * out)
        dout = dpre * gt
        dy = _dot_nt(dout, wo_ref[...])
        dpa = dy * s7
        dpb = dy * s8
        dz7 = dy * pa * (s7 * (1.0 - s7))
        dz8 = dy * pb * (s8 * (1.0 - s8))
        doa = _dot_nt(dpa, pa_ref[...])
        dob = _dot_nt(dpb, pb_ref[...])
        dh_ref[...] = dob * silu6
        dz6 = dob * hsum * (s6 * (1.0 + z6 * (1.0 - s6)))
        dna = doa * silu4
        dz4 = doa * na * (s4 * (1.0 + z4 * (1.0 - s4)))
        dng = _colsum(dna * n)
        acc_ref[7:8, 0:DH] += sum(dng[:, h * DH:(h + 1) * DH] for h in range(H))
        dn = dna * gfull
        do_ref[...] = rinv * (dn - n * per_head(dn * n))
        acc_ref[3:4, :] += _colsum(dz4)
        acc_ref[4:5, :] += _colsum(dz6)
        acc_ref[5:6, :] += _colsum(dz7)
        acc_ref[6:7, :] += _colsum(dz8)
        dz4_ref[...] = dz4.astype(BF)
        dz6_ref[...] = dz6.astype(BF)
        dz7_ref[...] = dz7.astype(BF)
        dz8_ref[...] = dz8.astype(BF)
        y_ref[...] = y.astype(BF)
        dout_ref[...] = dout.astype(BF)
        oa_ref[...] = oa.astype(BF)
        dpa_ref[...] = dpa.astype(BF)
        obv_ref[...] = obv.astype(BF)
        dpb_ref[...] = dpb.astype(BF)

        @pl.when(i == nt - 1)
        def _():
            acc_ref[9:10, :] = jnp.broadcast_to(jnp.sum(acc_ref[8:9, :], axis=-1, keepdims=True), (1, D))

    row = pl.BlockSpec((tm, D), lambda i: (i, 0))
    zs = lambda cb: pl.BlockSpec((tm, D), lambda i: (i, cb))
    full = lambda shp: pl.BlockSpec(shp, lambda i: (0,) * len(shp))
    wfull = pl.BlockSpec((D, D), lambda i: (0, 0), pipeline_mode=pl.Buffered(1))
    f32o = jax.ShapeDtypeStruct((t_lat, D), F32)
    bfo = jax.ShapeDtypeStruct((t_lat, D), BF)
    return pl.pallas_call(
        body, grid=(nt,),
        in_specs=[row, row, row, row, zs(4), zs(6), zs(7), zs(8), row, row, full((16, 3 * D)), full((1, DH)),
                  full((1, D)), full((1, D)), wfull, wfull, wfull],
        out_specs=(row,) * 13 + (full((16, D)),),
        out_shape=(f32o, f32o, bfo, bfo, bfo, bfo, f32o, bfo, bfo, bfo, bfo, bfo, bfo,
                   jax.ShapeDtypeStruct((16, D), F32)),
        name="merge", compiler_params=_cp(("arbitrary",), VMEM_LIMIT_MERGE))(
            o_f, o_b, h_f, h_b, z, z, z, z, x_all, tgt, mod, norm_g, ln_g, ln_b, p_a, p_b, w_out)


def _wgrad(a, b, name):
    tm = 1024

    def body(a_ref, b_ref, o_ref):
        @pl.when(pl.program_id(0) == 0)
        def _():
            o_ref[...] = jnp.zeros_like(o_ref)
        o_ref[...] += _dot_tn(a_ref[...], b_ref[...])

    row = pl.BlockSpec((tm, D), lambda i: (i, 0))
    return pl.pallas_call(body, grid=(a.shape[0] // tm,), in_specs=[row, row],
                          out_specs=pl.BlockSpec((D, D), lambda i: (0, 0)),
                          out_shape=jax.ShapeDtypeStruct((D, D), F32), name=name,
                          compiler_params=_cp(("arbitrary",)))(a, b)


def _wgrad_in(x_all, mod, dz, n_lat_tiles):
    m = x_all.shape[0]
    assert m % 128 == 0
    tm = m // 8
    n_lat = n_lat_tiles * 256

    def body(x_ref, mod_ref, dz_ref, o_ref):
        i = pl.program_id(1)

        @pl.when(i == 0)
        def _():
            o_ref[...] = jnp.zeros_like(o_ref)
        is_lat = (i * tm + lax.broadcasted_iota(jnp.int32, (tm, 1), 0)) < n_lat
        sh = jnp.where(is_lat, mod_ref[0:1, 0:D], mod_ref[1:2, 0:D])
        sc = jnp.where(is_lat, mod_ref[0:1, D:2 * D], mod_ref[1:2, D:2 * D])
        o_ref[0] += _dot_tn(x_ref[...] * (1.0 + sc) + sh, dz_ref[...])

    return pl.pallas_call(
        body, grid=(NSH, m // tm),
        in_specs=[pl.BlockSpec((tm, D), lambda n, i: (i, 0)),
                  pl.BlockSpec((16, 3 * D), lambda n, i: (0, 0)),
                  pl.BlockSpec((tm, SHC), lambda n, i: (i, n))],
        out_specs=pl.BlockSpec((1, D, SHC), lambda n, i: (n, 0, 0)),
        out_shape=jax.ShapeDtypeStruct((NSH, D, SHC), F32), name="wgrad_in",
        compiler_params=_cp(("arbitrary", "arbitrary")))(x_all, mod, dz)


def _du(dz, w_in_g, x_all, mod, gxres, n_lat_tiles, sums=()):
    m = x_all.shape[0]
    tm = 256
    nt = m // tm
    nct = nt - n_lat_tiles
    ns = len(sums)
    rblk = lambda i: jnp.where(i < nct, n_lat_tiles + i, i - nct)
    lblk = lambda i: jnp.maximum(i - nct, 0)

    def body(*refs):
        dz_ref, w_ref, x_ref, mod_ref, gr_ref = refs[:5]
        sum_refs = refs[5:5 + ns]
        gx_ref, dm_ref = refs[5 + ns:7 + ns]
        got_refs = refs[7 + ns:7 + 2 * ns]
        sems = refs[7 + 2 * ns:]
        i = pl.program_id(0)
        is_lat = i >= nct

        @pl.when(i == 0)
        def _():
            dm_ref[...] = jnp.zeros_like(dm_ref)
            if ns:
                for cp in _rs_chip_copies(sum_refs, got_refs, *sems):
                    cp.start()

        du = _dot_nt(dz_ref[:, 0:SHC], w_ref[0])
        for n in range(1, NSH):
            du = du + _dot_nt(dz_ref[:, n * SHC:(n + 1) * SHC], w_ref[n])
        sc = jnp.where(is_lat, mod_ref[0:1, D:2 * D], mod_ref[1:2, D:2 * D])
        dsh = _colsum(du)
        dsc = _colsum(du * x_ref[...])

        @pl.when(is_lat)
        def _():
            gx_ref[...] = du * (1.0 + sc) + gr_ref[...]
            dm_ref[0:1, 0:D] += dsh
            dm_ref[0:1, D:2 * D] += dsc

        @pl.when(jnp.logical_not(is_lat))
        def _():
            dm_ref[1:2, 0:D] += dsh
            dm_ref[1:2, D:2 * D] += dsc

        if ns:
            @pl.when(i == nt - 1)
            def _():
                for cp in _rs_chip_copies(sum_refs, got_refs, *sems):
                    cp.wait()

    outs = pl.pallas_call(
        body, grid=(nt,),
        in_specs=[pl.BlockSpec((tm, IN_COLS), lambda i: (rblk(i), 0)),
                  pl.BlockSpec((NSH, D, SHC), lambda i: (0, 0, 0), pipeline_mode=pl.Buffered(1)),
                  pl.BlockSpec((tm, D), lambda i: (rblk(i), 0)),
                  pl.BlockSpec((16, 3 * D), lambda i: (0, 0)),
                  pl.BlockSpec((tm, D), lambda i: (lblk(i), 0))] + [_ANY] * ns,
        out_specs=[pl.BlockSpec((tm, D), lambda i: (lblk(i), 0)),
                   pl.BlockSpec((8, 2 * D), lambda i: (0, 0))] + [_ANY] * ns,
        out_shape=[jax.ShapeDtypeStruct((n_lat_tiles * tm, D), F32), jax.ShapeDtypeStruct((8, 2 * D), F32)]
        + [jax.ShapeDtypeStruct((3,) + g.shape[1:], g.dtype) for g in sums],
        scratch_shapes=[pltpu.SemaphoreType.DMA((3 * ns,)), pltpu.SemaphoreType.DMA((3 * ns,))] if ns else [],
        name="du", compiler_params=_cp(("arbitrary",)))(dz, w_in_g, x_all, mod, gxres, *sums)
    return outs[0], outs[1], list(outs[2:])


def _row_tile(rows, cols):
    t = 8
    while t * 2 * cols * 4 <= (1 << 20) and rows % (t * 2) == 0:
        t *= 2
    return t


def _adamw(w, g, m, v, name):
    rows, cols = w.shape
    tr = _row_tile(rows, cols)

    def body(w_ref, g_ref, m_ref, v_ref, d_ref, nm_ref, nv_ref):
        gg = g_ref[...]
        m2 = ADAM_B1 * m_ref[...] + (1.0 - ADAM_B1) * gg
        v2 = ADAM_B2 * v_ref[...] + (1.0 - ADAM_B2) * (gg * gg)
        m_hat = m2 / (1.0 - ADAM_B1 ** ADAM_STEP)
        v_hat = v2 / (1.0 - ADAM_B2 ** ADAM_STEP)
        d_ref[...] = -ADAM_LR * (m_hat / (jnp.sqrt(v_hat) + ADAM_EPS) + ADAM_WD * w_ref[...])
        nm_ref[...] = m2
        nv_ref[...] = v2

    spec = pl.BlockSpec((tr, cols), lambda i: (i, 0))
    o = jax.ShapeDtypeStruct((rows, cols), F32)
    return pl.pallas_call(body, grid=(rows // tr,), in_specs=[spec] * 4, out_specs=(spec,) * 3,
                          out_shape=(o, o, o), name=name, compiler_params=_cp(("arbitrary",)))(w, g, m, v)


_ANY = pl.BlockSpec(memory_space=pl.ANY)


def _place():
    return lax.axis_index("x"), lax.axis_index("y"), lax.axis_index("c")


def _ag_copies(ins, outs, send, recv, fsend, frecv, lsem):
    x, y, c = _place()
    me = 2 * x + y
    chips = ((1 - x, y), (x, 1 - y), (1 - x, 1 - y))
    local, chip, hand = [], [], []
    for j in range(len(ins)):
        hr = ins[j].shape[0] // 2
        half = pl.ds(pl.multiple_of(c * hr, 8), hr)
        local.append(pltpu.make_async_copy(ins[j], outs[j].at[me], lsem.at[j]))
        for k, (px, py) in enumerate(chips):
            chip.append(pltpu.make_async_remote_copy(
                src_ref=ins[j].at[half, :], dst_ref=outs[j].at[me, half, :], send_sem=send.at[3 * j + k],
                recv_sem=recv.at[3 * j + k], device_id=(px, py, c), device_id_type=MESH))
            got = outs[j].at[2 * px + py, half, :]
            hand.append(pltpu.make_async_remote_copy(
                src_ref=got, dst_ref=got, send_sem=fsend.at[3 * j + k], recv_sem=frecv.at[3 * j + k],
                device_id=(x, y, 1 - c), device_id_type=MESH))
    return local, chip, hand


def _ag_sems(n):
    return [pltpu.SemaphoreType.DMA((3 * n,))] * 4 + [pltpu.SemaphoreType.DMA((n,))]


def _ag_finish(local, chip, hand, done=0):
    for k in range(done, len(chip)):
        chip[k].wait_recv()
        hand[k].start()
    for cp in chip:
        cp.wait_send()
    for k in range(done):
        hand[k].wait_send()
    for k in range(done, len(chip)):
        hand[k].wait_send()
        hand[k].wait_recv()
    for cp in local:
        cp.wait()


def _mod_tp(c8, c_ctx, w_mod_sh, b_mod_sh):
    mc = w_mod_sh.shape[1]

    def body(c8_ref, cctx_ref, w_ref, b_ref, mod_ref, cc_ref, cc_s, part_s, send1, recv1, send3, recv3):
        x, y, c = _place()
        me = 4 * x + 2 * y + c
        ms = 2 * x + y
        copies = []
        for k in range(1, 8):
            peer = (x ^ ((k >> 2) & 1), y ^ ((k >> 1) & 1), c ^ (k & 1))
            cp = pltpu.make_async_remote_copy(src_ref=c8_ref, dst_ref=cc_s.at[me], send_sem=send1.at[k],
                                              recv_sem=recv1.at[k], device_id=peer, device_id_type=MESH)
            cp.start()
            copies.append(cp)
        cc_s[me] = c8_ref[...]
        for cp in copies:
            cp.wait()
        cc_ref[...] = jnp.zeros_like(cc_ref)
        for j in range(8):
            cc_ref[j:j + 1, :] = cc_s[j, 0:1, :]
        cc_ref[8:9, :] = cctx_ref[...]
        v = cc_ref[...]
        part_s[ms] = _dot(v * _sig(v), w_ref[...]) + b_ref[...]
        copies = []
        for k in range(1, 4):
            peer = (x ^ ((k >> 1) & 1), y ^ (k & 1), c)
            cp = pltpu.make_async_remote_copy(src_ref=part_s.at[ms], dst_ref=part_s.at[ms], send_sem=send3.at[k],
                                              recv_sem=recv3.at[k], device_id=peer, device_id_type=MESH)
            cp.start()
            copies.append(cp)
        for cp in copies:
            cp.wait()
        for s in range(NSH):
            mod_ref[:, s * mc:(s + 1) * mc] = part_s[s]

    vm = pl.BlockSpec(memory_space=pltpu.VMEM)
    return pl.pallas_call(
        body, in_specs=[vm] * 4, out_specs=(vm, vm),
        out_shape=(jax.ShapeDtypeStruct((16, NSH * mc), F32), jax.ShapeDtypeStruct((16, D), F32)),
        scratch_shapes=[pltpu.VMEM((8, 8, D), F32), pltpu.VMEM((NSH, 16, mc), F32),
                        pltpu.SemaphoreType.DMA((8,)), pltpu.SemaphoreType.DMA((8,)),
                        pltpu.SemaphoreType.DMA((4,)), pltpu.SemaphoreType.DMA((4,))],
        name="mod_tp", compiler_params=_cp())(c8, c_ctx, w_mod_sh, b_mod_sh)


def _inproj_ag(x_all, mod, w_in_sh, b_in, narrow_sh, n_lat_tiles):
    m = x_all.shape[0]
    tm = 256
    nt = m // tm
    x_, y_ = lax.axis_index("x"), lax.axis_index("y")
    sids = jnp.stack([2 * x_ + y_, 2 * (1 - x_) + y_, 2 * x_ + 1 - y_, 2 * (1 - x_) + 1 - y_]).astype(jnp.int32)

    def body(sid_ref, x_ref, mod_ref, b_ref, wsh_ref, nsh_ref, z_ref, wg_ref, ng_ref, w_s, *sems):
        n = pl.program_id(0)
        i = pl.program_id(1)
        ag = ((wsh_ref, nsh_ref), (wg_ref, ng_ref)) + tuple(sems[:5])
        wsem = sems[5]

        def load(src):
            cp = pltpu.make_async_copy(src, w_s, wsem)
            cp.start()
            cp.wait()

        @pl.when((n == 0) & (i == 0))
        def _():
            local, chip, _ = _ag_copies(*ag)
            for cp in chip + local:
                cp.start()
            load(wsh_ref)

        for k in range(NSH - 1):
            @pl.when((n == k + 1) & (i == 0))
            def _():
                _, chip, hand = _ag_copies(*ag)
                chip[k].wait_recv()
                hand[k].start()
                hand[k].wait_recv()
                load(wg_ref.at[sid_ref[k + 1]])

        u, _ = _modulate(x_ref, mod_ref, i < n_lat_tiles)
        z_ref[...] = _dot(u, w_s[...]) + b_ref[...]

        @pl.when((n == NSH - 1) & (i == nt - 1))
        def _():
            _ag_finish(*_ag_copies(*ag), done=NSH - 1)

    outs = pl.pallas_call(
        body, grid_spec=pltpu.PrefetchScalarGridSpec(
            num_scalar_prefetch=1, grid=(NSH, nt),
            in_specs=[pl.BlockSpec((tm, D), lambda n, i, sid: (i, 0)),
                      pl.BlockSpec((16, 3 * D), lambda n, i, sid: (0, 0)),
                      pl.BlockSpec((1, SHC), lambda n, i, sid: (0, sid[n])), _ANY, _ANY],
            out_specs=[pl.BlockSpec((tm, SHC), lambda n, i, sid: (i, sid[n])), _ANY, _ANY],
            scratch_shapes=[pltpu.VMEM((D, SHC), BF)] + _ag_sems(2) + [pltpu.SemaphoreType.DMA]),
        out_shape=[jax.ShapeDtypeStruct((m, IN_COLS), F32), jax.ShapeDtypeStruct((NSH,) + w_in_sh.shape, BF),
                   jax.ShapeDtypeStruct((NSH,) + narrow_sh.shape, narrow_sh.dtype)],
        name="inproj_ag", compiler_params=_cp(("arbitrary", "arbitrary")))(sids, x_all, mod, b_in, w_in_sh, narrow_sh)
    return outs


def _rs_sibling(grads):
    n = len(grads)

    def body(*refs):
        ins, got = refs[:n], refs[n:2 * n]
        send, recv = refs[2 * n:]
        x, y, c = _place()
        copies = []
        for j in range(n):
            hr = ins[j].shape[1] // 2
            for s in range(NSH):
                give = ins[j].at[s, pl.ds(pl.multiple_of((1 - c) * hr, 8), hr), :]
                cp = pltpu.make_async_remote_copy(src_ref=give, dst_ref=got[j].at[s], send_sem=send.at[NSH * j + s],
                                                  recv_sem=recv.at[NSH * j + s], device_id=(x, y, 1 - c),
                                                  device_id_type=MESH)
                cp.start()
                copies.append(cp)
        for cp in copies:
            cp.wait()

    half = [jax.ShapeDtypeStruct((NSH, g.shape[1] // 2, g.shape[2]), F32) for g in grads]
    return pl.pallas_call(
        body, in_specs=[_ANY] * n, out_specs=[_ANY] * n, out_shape=half,
        scratch_shapes=[pltpu.SemaphoreType.DMA((NSH * n,)), pltpu.SemaphoreType.DMA((NSH * n,))],
        name="rs_sibling")(*grads)


def _core_vec():
    return lax.axis_index("c").astype(jnp.int32).reshape(1)


def _rs_add1(g, got, name):
    _, r, cols = g.shape
    hr = r // 2
    tr = _row_tile(hr, cols)
    nb = hr // tr

    def body(c_ref, g_ref, got_ref, o_ref):
        o_ref[...] = (g_ref[...] + got_ref[...]).astype(BF)

    spec = pl.BlockSpec((1, tr, cols), lambda s, i, c_ref: (s, i, 0))
    return pl.pallas_call(
        body, grid_spec=pltpu.PrefetchScalarGridSpec(
            num_scalar_prefetch=1, grid=(NSH, nb),
            in_specs=[pl.BlockSpec((1, tr, cols), lambda s, i, c_ref: (s, c_ref[0] * nb + i, 0)), spec],
            out_specs=spec),
        out_shape=jax.ShapeDtypeStruct((NSH, hr, cols), BF), name=name,
        compiler_params=_cp(("arbitrary", "arbitrary")))(_core_vec(), g, got)


def _rs_add2(sums, got, name):
    _, hr, cols = sums.shape
    tr = _row_tile(hr, cols)
    nb = hr // tr
    place = jnp.stack([2 * lax.axis_index("x") + lax.axis_index("y"), lax.axis_index("c")]).astype(jnp.int32)

    def body(p_ref, s_ref, got_ref, o_ref):
        f = lambda v: v.astype(F32)
        o_ref[...] = f(s_ref[0]) + f(got_ref[0]) + f(got_ref[1]) + f(got_ref[2])

    return pl.pallas_call(
        body, grid_spec=pltpu.PrefetchScalarGridSpec(
            num_scalar_prefetch=1, grid=(nb,),
            in_specs=[pl.BlockSpec((1, tr, cols), lambda i, p_ref: (p_ref[0], i, 0)),
                      pl.BlockSpec((3, tr, cols), lambda i, p_ref: (0, i, 0))],
            out_specs=pl.BlockSpec((tr, cols), lambda i, p_ref: (p_ref[1] * nb + i, 0))),
        out_shape=jax.ShapeDtypeStruct((2 * hr, cols), F32), name=name,
        compiler_params=_cp(("arbitrary",)))(place, sums, got)


def _rs_chip_copies(ins, got, send, recv):
    x, y, c = _place()
    peers = ((1 - x, y), (x, 1 - y), (1 - x, 1 - y))
    return [pltpu.make_async_remote_copy(src_ref=ins[j].at[2 * px + py], dst_ref=got[j].at[k],
                                         send_sem=send.at[3 * j + k], recv_sem=recv.at[3 * j + k],
                                         device_id=(px, py, c), device_id_type=MESH)
            for j in range(len(ins)) for k, (px, py) in enumerate(peers)]


def _ag_sibling(fulls):
    n = len(fulls)
    nck = 4

    def body(*refs):
        outs = refs[n:2 * n]
        send, recv = refs[2 * n:]
        x, y, c = _place()
        copies = []
        for j in range(n):
            qr = outs[j].shape[0] // (2 * nck)
            for k in range(nck):
                rows = outs[j].at[pl.ds(pl.multiple_of((c * nck + k) * qr, 8), qr), :]
                cp = pltpu.make_async_remote_copy(src_ref=rows, dst_ref=rows, send_sem=send.at[nck * j + k],
                                                  recv_sem=recv.at[nck * j + k], device_id=(x, y, 1 - c),
                                                  device_id_type=MESH)
                cp.start()
                copies.append(cp)
        for cp in copies:
            cp.wait()

    return pl.pallas_call(
        body, in_specs=[_ANY] * n, out_specs=[_ANY] * n,
        out_shape=[jax.ShapeDtypeStruct(f.shape, F32) for f in fulls],
        input_output_aliases={j: j for j in range(n)},
        scratch_shapes=[pltpu.SemaphoreType.DMA((nck * n,)), pltpu.SemaphoreType.DMA((nck * n,))],
        name="ag_sibling")(*fulls)


def _allreduce_small(buf):
    rows = buf.shape[0]
    pr = rows // 8

    def body(in_ref, out_ref, stage, send1, recv1, send2, recv2):
        x, y, c = _place()
        me = 4 * x + 2 * y + c

        def peer(k):
            kx, ky, kc = (k >> 2) & 1, (k >> 1) & 1, k & 1
            return (x ^ kx, y ^ ky, c ^ kc)

        def piece(ref, idx):
            return ref.at[pl.ds(pl.multiple_of(idx * pr, 8), pr), :]

        copies = []
        for k in range(1, 8):
            px, py, pc = peer(k)
            cp = pltpu.make_async_remote_copy(src_ref=piece(in_ref, 4 * px + 2 * py + pc), dst_ref=stage.at[k],
                                              send_sem=send1.at[k], recv_sem=recv1.at[k],
                                              device_id=(px, py, pc), device_id_type=MESH)
            cp.start()
            copies.append(cp)
        for cp in copies:
            cp.wait()
        acc = piece(in_ref, me)[...]
        for k in range(1, 8):
            acc = acc + stage[k]
        piece(out_ref, me)[...] = acc
        copies = []
        for k in range(1, 8):
            cp = pltpu.make_async_remote_copy(src_ref=piece(out_ref, me), dst_ref=piece(out_ref, me),
                                              send_sem=send2.at[k], recv_sem=recv2.at[k],
                                              device_id=peer(k), device_id_type=MESH)
            cp.start()
            copies.append(cp)
        for cp in copies:
            cp.wait()

    vm = pl.BlockSpec(memory_space=pltpu.VMEM)
    return pl.pallas_call(
        body, in_specs=[vm], out_specs=vm, out_shape=jax.ShapeDtypeStruct((rows, D), F32),
        scratch_shapes=[pltpu.VMEM((8, pr, D), F32)] + [pltpu.SemaphoreType.DMA((8,))] * 4,
        name="allreduce_small", compiler_params=_cp())(buf)


def _rows(a):
    flat = a.reshape(-1)
    pad = (-flat.shape[0]) % D
    if pad:
        flat = jnp.concatenate([flat, jnp.zeros((pad,), flat.dtype)])
    return flat.reshape(-1, D)


def _pad_rows(a, mult):
    pad = (-a.shape[0]) % mult
    return jnp.concatenate([a, jnp.zeros((pad, a.shape[1]), a.dtype)]) if pad else a


def _local_step(x, c, ctx, c_ctx, tgt, me, shard, sh, b_mod, b_in, norm_g, cb, wr, wi, ln_g, ln_b):
    t_lat, t_ctx = x.shape[0], ctx.shape[0]
    nlt = t_lat // 256
    nlb, ncb = t_lat // RB, t_ctx // RB
    mc = 3 * D // NSH
    mod_all, cc_all = _mod_tp(jnp.zeros((8, D), F32).at[0].set(c), c_ctx.reshape(1, D), sh["w_mod"],
                              lax.dynamic_slice_in_dim(b_mod, shard * mc, mc, axis=1))
    mod = jnp.zeros((16, 3 * D), F32).at[0].set(mod_all[me]).at[1].set(mod_all[8])
    cc = jnp.zeros((16, D), F32).at[0].set(c).at[1].set(c_ctx)
    x_all = jnp.concatenate([x, ctx], axis=0)
    z, w_in_g, nar = _inproj_ag(x_all, mod, sh["w_in"], b_in, sh["narrow"], nlt)
    nar = jnp.transpose(nar, (1, 0, 2)).reshape(-1, D)
    lbl, cw, br, bi, lam = nar[0:4].reshape(2, 2, D), nar[4:8], nar[8:10], nar[10:12], nar[12:14]
    o0, st0, (w_mod_g, p_a, p_b, w_out) = _gla_fwd(z, lbl, 0, nlb, ncb,
                                                   gather=[sh[k] for k in ("w_mod", "p_a", "p_b", "w_out")])
    p_a, p_b, w_out = p_a.reshape(D, D), p_b.reshape(D, D), w_out.reshape(D, D)
    o1, st1, _ = _gla_fwd(z, lbl, 1, nlb, ncb)
    h0, hp0, hpc0 = _rglru_fwd(z, cw, cb, wr, br, wi, bi, lam, 0, t_lat, t_ctx)
    h1, hp1, hpc1 = _rglru_fwd(z, cw, cb, wr, br, wi, bi, lam, 1, t_lat, t_ctx)
    (do, dh, dz4, dz6, dz7, dz8, gxres, y, dout, oa, dpa, obv, dpb, acc) = _merge(
        o0, o1, h0, h1, z, x_all, tgt, mod, norm_g, ln_g, ln_b, p_a, p_b, w_out, t_lat)
    gp_a = _wgrad(oa, dpa, "wgrad_pa")
    gp_b = _wgrad(obv, dpb, "wgrad_pb")
    gw_out = _wgrad(y, dout, "wgrad_wout")
    dxc_lat, dxc_ctx, dwr0, dwi0, sb0 = _rglru_bwd(z, cw, cb, wr, br, wi, bi, lam, dh, hp0, hpc0, 0, t_lat, t_ctx)
    dz5_lat, dz5_ctx, dwr1, dwi1, sb1 = _rglru_bwd(z, cw, cb, wr, br, wi, bi, lam, dh, hp1, hpc1, 1, t_lat, t_ctx,
                                                   prev=(dxc_lat, dxc_ctx))
    dq0, dv0, dz1, sa0 = _gla_bwd(z, lbl, do, st0, 0, nlb, ncb)
    dz0, dz3, dz2, sa1 = _gla_bwd(z, lbl, do, st1, 1, nlb, ncb, prev=(dq0, dv0))
    zc = jnp.zeros((t_ctx, D), BF)
    pad = lambda a: jnp.concatenate([a, zc], axis=0)
    dz = jnp.concatenate([dz0, dz1, dz2, dz3, pad(dz4), jnp.concatenate([dz5_lat, dz5_ctx], axis=0).astype(BF),
                          pad(dz6), pad(dz7), pad(dz8)], axis=1)
    big = dict(w_in=_wgrad_in(x_all, mod, dz, nlt), p_a=gp_a.reshape(NSH, D // NSH, D),
               p_b=gp_b.reshape(NSH, D // NSH, D), w_out=gw_out.reshape(NSH, D // NSH, D))
    grads = [big[k] for k in _RS]
    sums = [_rs_add1(g, b, f"rs_add1_{j}") for j, (g, b) in enumerate(zip(grads, _rs_sibling(grads)))]
    gx, dm, got = _du(dz, w_in_g, x_all, mod, gxres, nlt, sums)
    fulls = [_rs_add2(a, b, f"rs_add2_{j}") for j, (a, b) in enumerate(zip(sums, got))]
    big = dict(zip(_RS, _ag_sibling(fulls)))
    dmod = jnp.zeros((16, 3 * D), F32).at[0:2, 0:2 * D].set(dm[0:2]).at[0, 2 * D:].set(acc[0])
    dcc = _mod_bwd(cc, dmod, w_mod_g)
    small = dict(
        c_ctx=dcc[1:2], b_mod=(dmod[0] + dmod[1]).reshape(3, D),
        b_in=jnp.stack([sa1[2], sa0[0], sa1[0], sa1[3], acc[3], sb1[4], acc[4], acc[5], acc[6]]),
        lb_logits=jnp.stack([sa0[1], sa1[1], -sa0[1], -sa1[1]]),
        norm_a_g=acc[7:8], conv_w=sb1[8:12], conv_b=sb1[3:4],
        w_r=jnp.stack([dwr0, dwr1]).reshape(-1, D), w_i=jnp.stack([dwi0, dwi1]).reshape(-1, D),
        b_r=jnp.stack([sb0[0], sb1[0]]), b_i=jnp.stack([sb0[1], sb1[1]]), lam=jnp.stack([sb0[2], sb1[2]]),
        ln_g=acc[1:2], ln_b=acc[2:3])
    return acc[9, 0], gx, big, small, dmod, cc_all


_RS =("w_in", "p_a", "p_b", "w_out")
_SMALL =("c_ctx", "b_mod", "b_in", "lb_logits", "norm_a_g", "conv_w", "conv_b", "w_r", "w_i", "b_r", "b_i", "lam",
          "ln_g", "ln_b")
_BIG = ("w_mod", "w_in", "p_a", "p_b", "w_out")
_COL_SHARDED = ("lb_logits", "conv_w", "b_r", "b_i", "lam")
_WEIGHTS = ("c_ctx", "w_mod", "b_mod", "w_in", "b_in", "lb_logits", "norm_a_g", "conv_w", "conv_b", "w_r", "b_r", "w_i",
            "b_i", "lam", "p_a", "p_b", "w_out", "ln_g", "ln_b")


def kernel(x, c, ctx, c_ctx, w_mod, b_mod, w_in, b_in, lb_logits, norm_a_g, conv_w, conv_b, w_r, b_r, w_i, b_i, lam, p_a, p_b, w_out, ln_g, ln_b, loss_target, m_c_ctx, m_w_mod, m_b_mod, m_w_in, m_b_in, m_lb_logits, m_norm_a_g, m_conv_w, m_conv_b, m_w_r, m_b_r, m_w_i, m_b_i, m_lam, m_p_a, m_p_b, m_w_out, m_ln_g, m_ln_b, v_c_ctx, v_w_mod, v_b_mod, v_w_in, v_b_in, v_lb_logits, v_norm_a_g, v_conv_w, v_conv_b, v_w_r, v_b_r, v_w_i, v_b_i, v_lam, v_p_a, v_p_b, v_w_out, v_ln_g, v_ln_b):
    w = dict(c_ctx=c_ctx, w_mod=w_mod, b_mod=b_mod, w_in=w_in, b_in=b_in, lb_logits=lb_logits, norm_a_g=norm_a_g,
             conv_w=conv_w, conv_b=conv_b, w_r=w_r, b_r=b_r, w_i=w_i, b_i=b_i, lam=lam, p_a=p_a, p_b=p_b, w_out=w_out,
             ln_g=ln_g, ln_b=ln_b)
    m = dict(c_ctx=m_c_ctx, w_mod=m_w_mod, b_mod=m_b_mod, w_in=m_w_in, b_in=m_b_in, lb_logits=m_lb_logits,
             norm_a_g=m_norm_a_g, conv_w=m_conv_w, conv_b=m_conv_b, w_r=m_w_r, b_r=m_b_r, w_i=m_w_i, b_i=m_b_i,
             lam=m_lam, p_a=m_p_a, p_b=m_p_b, w_out=m_w_out, ln_g=m_ln_g, ln_b=m_ln_b)
    v = dict(c_ctx=v_c_ctx, w_mod=v_w_mod, b_mod=v_b_mod, w_in=v_w_in, b_in=v_b_in, lb_logits=v_lb_logits,
             norm_a_g=v_norm_a_g, conv_w=v_conv_w, conv_b=v_conv_b, w_r=v_w_r, b_r=v_b_r, w_i=v_w_i, b_i=v_b_i,
             lam=v_lam, p_a=v_p_a, p_b=v_p_b, w_out=v_w_out, ln_g=v_ln_g, ln_b=v_ln_b)
    shard = 2 * lax.axis_index("x") + lax.axis_index("y")
    cs = D // NSH

    sh = {k: w[k][0].astype(BF) for k in _BIG}
    sh["narrow"] = _pad_rows(jnp.concatenate([lb_logits.reshape(4, cs), conv_w[0], b_r[0], b_i[0], lam[0]], axis=0), 8)
    me = 2 * shard + lax.axis_index("c")
    loss, gx, big, small, dmod, cc_all = _local_step(
        x[0], c[0], ctx[0], c_ctx, loss_target[0], me, shard, sh, b_mod, b_in, norm_a_g, conv_b, w_r[0], w_i[0],
        ln_g, ln_b)
    loss = lax.psum(loss, ("x", "y", "c"))

    dmod_rows = jnp.zeros((16, 3 * D), F32).at[me].set(dmod[0]).at[8].set(dmod[1]).reshape(48, D)
    sizes = [small[k].shape[0] for k in _SMALL]
    red = _allreduce_small(_pad_rows(jnp.concatenate([_pad_rows(small[k], 8) for k in _SMALL] + [dmod_rows],
                                                     axis=0), 64))
    grads = {}
    off = 0
    for k, n in zip(_SMALL, sizes):
        g = red[off:off + n]
        off += n + (-n) % 8
        if k == "norm_a_g":
            g = g[:, :DH]
        if k in _COL_SHARDED:
            g = lax.dynamic_slice_in_dim(g, shard * cs, cs, axis=1)
        grads[k] = g.reshape(w[k].shape)
    for k in _RS:
        grads[k] = big[k].reshape(w[k].shape)
    dmod_all = red[off:off + 48].reshape(16, 3 * D)
    mc = 3 * D // NSH
    grads["w_mod"] = _wmod_grad(cc_all, lax.dynamic_slice_in_dim(dmod_all, shard * mc, mc, axis=1)).reshape(
        w["w_mod"].shape)

    delta, new_m, new_v = {}, {}, {}
    for k in _BIG:
        shp = w[k].shape
        two = lambda a: a.reshape(shp[-2], shp[-1])
        d_, m_, v_ = _adamw(two(w[k]), two(grads[k]), two(m[k]), two(v[k]), f"adamw_{k}")
        delta[k], new_m[k], new_v[k] = d_.reshape(shp), m_.reshape(shp), v_.reshape(shp)
    flat = lambda t: jnp.concatenate([_pad_rows(_rows(t[k]), 8) for k in _SMALL], axis=0)
    d_, m_, v_ = _adamw(flat(w), flat(grads), flat(m), flat(v), "adamw_small")
    off = 0
    for k in _SMALL:
        n = w[k].size
        nr = -(-n // D)
        for src, dst in ((d_, delta), (m_, new_m), (v_, new_v)):
            dst[k] = src[off:off + nr].reshape(-1)[:n].reshape(w[k].shape)
        off += nr + (-nr) % 8

    return (loss, gx[None], *[grads[k] for k in _WEIGHTS], *[delta[k] for k in _WEIGHTS],
            *[new_m[k] for k in _WEIGHTS], *[new_v[k] for k in _WEIGHTS])
```

```python
import functools

import jax
import jax.numpy as jnp
from jax import lax
from jax.experimental import pallas as pl
from jax.experimental.pallas import tpu as pltpu

F32 = jnp.float32
BF = jnp.bfloat16
MESH = pl.DeviceIdType.MESH

D = 1024
H = 8
DH = 128
CH = 64
RB = 256
NCK = RB // CH
GRID_W = 64
CB = 8
RCH = 16
IN_COLS = 9 * D
NSH = 4
SHC = IN_COLS // NSH
RG_C = 8.0
ALPHA = 2.0 ** 0.25
LN_EPS = 1e-5
RMS_EPS = 1e-6
Q_SCALE = DH ** -0.5
ADAM_LR, ADAM_B1, ADAM_B2, ADAM_EPS, ADAM_WD, ADAM_STEP = 0.001, 0.9, 0.999, 1e-08, 0.01, 10
VMEM_LIMIT = 56 * 1024 * 1024
VMEM_LIMIT_MERGE = 60 * 1024 * 1024


def _cp(sem=None, vmem=VMEM_LIMIT):
    return pltpu.CompilerParams(dimension_semantics=sem, vmem_limit_bytes=vmem)


def _sig(x):
    return 0.5 * jnp.tanh(0.5 * x) + 0.5


def _dot(a, b):
    return jnp.dot(a.astype(BF), b.astype(BF), preferred_element_type=F32)


def _dot_nt(a, b):
    return lax.dot_general(a.astype(BF), b.astype(BF), (((1,), (1,)), ((), ())), preferred_element_type=F32)


def _dot_tn(a, b):
    return lax.dot_general(a.astype(BF), b.astype(BF), (((0,), (0,)), ((), ())), preferred_element_type=F32)


def _colsum(v):
    return jnp.sum(v, axis=0, keepdims=True)


def _mod_bwd(cc, dmod, w_mod_g):
    def body(cc_ref, dm_ref, w_ref, dcc_ref):
        v = cc_ref[...]
        sg = _sig(v)
        ds = jnp.zeros((16, D), F32)
        for k in range(NSH):
            ds = ds + _dot_nt(dm_ref[:, k * 768:(k + 1) * 768], w_ref[k])
        dcc_ref[...] = ds * (sg * (1.0 + v * (1.0 - sg)))
    return pl.pallas_call(body, out_shape=jax.ShapeDtypeStruct((16, D), F32),
                          name="mod_bwd", compiler_params=_cp())(cc, dmod, w_mod_g)


def _wmod_grad(cc, dmod_cols):
    def body(cc_ref, dm_ref, dw_ref):
        v = cc_ref[...]
        dw_ref[...] = _dot_tn(v * _sig(v), dm_ref[...])
    return pl.pallas_call(body, out_shape=jax.ShapeDtypeStruct((D, dmod_cols.shape[1]), F32),
                          name="wmod_grad", compiler_params=_cp())(cc, dmod_cols)


def _modulate(x_ref, mod_ref, is_lat):
    sh = jnp.where(is_lat, mod_ref[0:1, 0:D], mod_ref[1:2, 0:D])
    sc = jnp.where(is_lat, mod_ref[0:1, D:2 * D], mod_ref[1:2, D:2 * D])
    return x_ref[...] * (1.0 + sc) + sh, sc


def _gla_mask(d):
    row = lax.broadcasted_iota(jnp.int32, (CH, CH), 0)
    col = lax.broadcasted_iota(jnp.int32, (CH, CH), 1)
    return (row <= col) if d else (row >= col)


def _chunk_cumsum(v, rev):
    n = v.shape[0]
    pos = lax.broadcasted_iota(jnp.int32, v.shape, 0) & (CH - 1)
    s = 1
    while s < CH:
        if rev:
            v = v + jnp.where(pos < CH - s, pltpu.roll(v, n - s, 0), 0.0)
        else:
            v = v + jnp.where(pos >= s, pltpu.roll(v, s, 0), 0.0)
        s *= 2
    return v


def _chunk_rows(c):
    return slice(c * CH, (c + 1) * CH)


def _gla_features(zq, zf, lb, d):
    sq = _sig(zq)
    q = zq * sq * Q_SCALE
    sf = _sig(zf)
    f = lb + (1.0 - lb) * sf
    k = 1.0 - f
    g = _chunk_cumsum(jnp.log(f), d)
    last = 0 if d else CH - 1
    gls = [g[c * CH + last:c * CH + last + 1, :] for c in range(NCK)]
    glb = jnp.concatenate([jnp.broadcast_to(gl, (CH, D)) for gl in gls], axis=0)
    eg, eig, eeg = jnp.exp(g), jnp.exp(-g), jnp.exp(glb - g)
    decs = [jnp.exp(gl) for gl in gls]
    return sq, sf, f, q * eg, k * eig, k * eeg, eg, eig, eeg, decs


def _lower_bound(lbl_ref, d):
    return _sig(lbl_ref[0, d:d + 1, :] - lbl_ref[1, d:d + 1, :])


def _gla_rb(d, nlb, ncb):
    nrb = nlb + ncb
    if d:
        return lambda s: nrb - 1 - s
    return lambda s: jnp.where(s < ncb, nlb + s, s - ncb)


def _gla_fwd(z, lbl, d, nlb, ncb, gather=()):
    m = z.shape[0]
    nrb = nlb + ncb
    rb = _gla_rb(d, nlb, ncb)
    ng = len(gather)

    def body(*refs):
        q_ref, f_ref, v_ref, lbl_ref = refs[:4]
        ag_in = refs[4:4 + ng]
        o_ref, st_ref = refs[4 + ng:6 + ng]
        ag_out = refs[6 + ng:6 + 2 * ng]
        S = refs[6 + 2 * ng]
        ag_sems = refs[7 + 2 * ng:]
        s = pl.program_id(0)

        @pl.when(s == 0)
        def _():
            S[...] = jnp.zeros_like(S)
            if ng:
                local, chip, _ = _ag_copies(ag_in, ag_out, *ag_sems)
                for cp in chip + local:
                    cp.start()

        lb = _lower_bound(lbl_ref, d)
        mb = _gla_mask(d)
        _, _, _, qd, ki, ke, _, _, _, decs = _gla_features(q_ref[...], f_ref[...], lb, d)
        qd, ki, ke, v = qd.astype(BF), ki.astype(BF), ke.astype(BF), v_ref[...].astype(BF)
        order = range(NCK - 1, -1, -1) if d else range(NCK)
        for h in range(H):
            hs = slice(h * DH, (h + 1) * DH)
            intra, upd = {}, {}
            for c in range(NCK):
                rs = _chunk_rows(c)
                a = jnp.where(mb, _dot_nt(qd[rs, hs], ki[rs, hs]), 0.0)
                intra[c] = _dot(a, v[rs, hs])
                upd[c] = _dot_tn(v[rs, hs], ke[rs, hs])
            st = S[h]
            for c in order:
                rs = _chunk_rows(c)
                st_ref[c, h] = st
                o_ref[rs, hs] = intra[c] + _dot_nt(qd[rs, hs], st)
                st = st * decs[c][:, hs] + upd[c]
            S[h] = st

        if ng:
            @pl.when(s == nrb - 1)
            def _():
                _ag_finish(*_ag_copies(ag_in, ag_out, *ag_sems))

    def zspec(cb):
        return pl.BlockSpec((RB, D), lambda s: (rb(s), cb))

    outs = pl.pallas_call(
        body, grid=(nrb,),
        in_specs=[zspec(0), zspec(1 + d), zspec(3), pl.BlockSpec((2, 2, D), lambda s: (0, 0, 0))] + [_ANY] * ng,
        out_specs=[pl.BlockSpec((RB, D), lambda s: (rb(s), 0)),
                   pl.BlockSpec((NCK, H, DH, DH), lambda s: (rb(s), 0, 0, 0))] + [_ANY] * ng,
        out_shape=[jax.ShapeDtypeStruct((m, D), F32), jax.ShapeDtypeStruct((m // CH, H, DH, DH), F32)]
        + [jax.ShapeDtypeStruct((NSH,) + g.shape, g.dtype) for g in gather],
        scratch_shapes=[pltpu.VMEM((H, DH, DH), F32)] + (_ag_sems(ng) if ng else []),
        name=f"gla_fwd{d}", compiler_params=_cp(("arbitrary",)))(z, z, z, lbl, *gather)
    return outs[0], outs[1], list(outs[2:])


def _gla_bwd(z, lbl, do_lat, states, d, nlb, ncb, prev=None):
    m = z.shape[0]
    nrb = nlb + ncb
    fwd_rb = _gla_rb(d, nlb, ncb)
    rb = lambda s: fwd_rb(nrb - 1 - s)
    last = prev is not None

    def body(*refs):
        if last:
            q_ref, f_ref, v_ref, lbl_ref, do_ref, st_ref, pq_ref, pv_ref, o0_ref, o1_ref, o2_ref, sum_ref, dS = refs
        else:
            q_ref, f_ref, v_ref, lbl_ref, do_ref, st_ref, o0_ref, o1_ref, o2_ref, sum_ref, dS = refs
        s = pl.program_id(0)
        is_lat = rb(s) < nlb

        @pl.when(s == 0)
        def _():
            dS[...] = jnp.zeros_like(dS)
            sum_ref[...] = jnp.zeros_like(sum_ref)

        lb = _lower_bound(lbl_ref, d)
        mb = _gla_mask(d)
        zq = q_ref[...]
        sq, sf, f, qd, ki, ke, eg, eig, eeg, decs = _gla_features(zq, f_ref[...], lb, d)
        qdb, kib, keb, vb = qd.astype(BF), ki.astype(BF), ke.astype(BF), v_ref[...].astype(BF)
        dob = jnp.where(is_lat, do_ref[...], 0.0).astype(BF)
        order = range(NCK) if d else range(NCK - 1, -1, -1)
        dqd_h, dki_h, dke_h, dv_h, ddec_h = [], [], [], [], []
        for h in range(H):
            hs = slice(h * DH, (h + 1) * DH)
            dqd, dki, inc, dvi = {}, {}, {}, {}
            for c in range(NCK):
                rs = _chunk_rows(c)
                a = jnp.where(mb, _dot_nt(qdb[rs, hs], kib[rs, hs]), 0.0)
                da = jnp.where(mb, _dot_nt(dob[rs, hs], vb[rs, hs]), 0.0)
                dqd[c] = _dot(da, kib[rs, hs]) + _dot(dob[rs, hs], st_ref[c, h])
                dki[c] = _dot_tn(da, qdb[rs, hs])
                inc[c] = _dot_tn(dob[rs, hs], qdb[rs, hs])
                dvi[c] = _dot_tn(a, dob[rs, hs])
            dst = dS[h]
            dke, dv, ddec = {}, {}, {}
            for c in order:
                rs = _chunk_rows(c)
                dv[c] = dvi[c] + _dot_nt(keb[rs, hs], dst)
                dke[c] = _dot(vb[rs, hs], dst)
                ddec[c] = _colsum(st_ref[c, h] * dst)
                dst = inc[c] + dst * decs[c][:, hs]
            dS[h] = dst
            cat = lambda t: jnp.concatenate([t[c] for c in range(NCK)], axis=0)
            dqd_h.append(cat(dqd))
            dki_h.append(cat(dki))
            dke_h.append(cat(dke))
            dv_h.append(cat(dv))
            ddec_h.append([ddec[c] for c in range(NCK)])
        lanes = lambda parts: jnp.concatenate(parts, axis=1)
        dqd, dki, dke, dv = lanes(dqd_h), lanes(dki_h), lanes(dke_h), lanes(dv_h)
        dq = dqd * eg
        dk = dki * eig + dke * eeg
        dke_ke = dke * ke
        dg = dqd * qd - dki * ki - dke_ke
        dgl = [_colsum(dke_ke[_chunk_rows(c), :]) + lanes([ddec_h[h][c] for h in range(H)]) * decs[c]
               for c in range(NCK)]
        dglb = jnp.concatenate([jnp.broadcast_to(t, (CH, D)) for t in dgl], axis=0)
        df = (_chunk_cumsum(dg, 1 - d) + dglb) / f - dk
        dzf = df * (1.0 - lb) * (sf * (1.0 - sf))
        sum_ref[0:1, :] += _colsum(dzf)
        sum_ref[1:2, :] += _colsum(df * (1.0 - sf))
        if last:
            dz0 = (dq + pq_ref[...]) * (Q_SCALE * (sq * (1.0 + zq * (1.0 - sq))))
            dz3 = dv + pv_ref[...]
            sum_ref[2:3, :] += _colsum(dz0)
            sum_ref[3:4, :] += _colsum(dz3)
            o0_ref[...] = dz0.astype(BF)
            o1_ref[...] = dz3.astype(BF)
        else:
            o0_ref[...] = dq
            o1_ref[...] = dv
        o2_ref[...] = dzf.astype(BF)

        @pl.when(s == nrb - 1)
        def _():
            sum_ref[1:2, :] = sum_ref[1:2, :] * (lb * (1.0 - lb))

    def zspec(cb):
        return pl.BlockSpec((RB, D), lambda s: (rb(s), cb))

    rowspec = pl.BlockSpec((RB, D), lambda s: (rb(s), 0))
    in_specs = [zspec(0), zspec(1 + d), zspec(3), pl.BlockSpec((2, 2, D), lambda s: (0, 0, 0)),
                pl.BlockSpec((RB, D), lambda s: (jnp.minimum(rb(s), nlb - 1), 0)),
                pl.BlockSpec((NCK, H, DH, DH), lambda s: (rb(s), 0, 0, 0))]
    args = [z, z, z, lbl, do_lat, states]
    if last:
        in_specs += [rowspec, rowspec]
        args += list(prev)
    dt01 = BF if last else F32
    return pl.pallas_call(
        body, grid=(nrb,), in_specs=in_specs,
        out_specs=(rowspec, rowspec, rowspec, pl.BlockSpec((8, D), lambda s: (0, 0))),
        out_shape=(jax.ShapeDtypeStruct((m, D), dt01), jax.ShapeDtypeStruct((m, D), dt01),
                   jax.ShapeDtypeStruct((m, D), BF), jax.ShapeDtypeStruct((8, D), F32)),
        scratch_shapes=[pltpu.VMEM((H, DH, DH), F32)],
        name=f"gla_bwd{d}", compiler_params=_cp(("arbitrary",)))(*args)


def _shift_rows(v, k, r):
    row = lax.broadcasted_iota(jnp.int32, v.shape, 0)
    rolled = pltpu.roll(v, k % r, 0)
    return jnp.where((row >= k) & (row < r + k), rolled, 0.0)


def _conv_fwd(xb, cw_ref, cb_ref, r):
    return (cb_ref[...] + _shift_rows(xb, 1, r) * cw_ref[0:1, :] + xb * cw_ref[1:2, :]
            + _shift_rows(xb, -1, r) * cw_ref[2:3, :] + _shift_rows(xb, -2, r) * cw_ref[3:4, :])


def _window(ref, lo, n, rows):
    parts = []
    if lo < 0:
        parts.append(jnp.zeros((-lo,) + tuple(ref.shape[1:]), F32))
    parts.append(ref[max(lo, 0):min(lo + n, rows)])
    if lo + n > rows:
        parts.append(jnp.zeros((lo + n - rows,) + tuple(ref.shape[1:]), F32))
    return parts[0] if len(parts) == 1 else jnp.concatenate(parts, axis=0)


def _conv_cols(x_ref, cw_ref, cb_ref, r0, n, rows):
    w = _window(x_ref, r0 - 1, n + 3, rows)
    return (cb_ref[...] + w[0:n] * cw_ref[0:1, :] + w[1:n + 1] * cw_ref[1:2, :] + w[2:n + 2] * cw_ref[2:3, :]
            + w[3:n + 3] * cw_ref[3:4, :])


def _gate_a(xc, wr_ref, br, sp):
    xcb = xc.astype(BF)
    rs = [jnp.dot(xcb[:, g * DH:(g + 1) * DH], wr_ref[g].astype(BF), preferred_element_type=F32) for g in range(H)]
    return jnp.exp((-RG_C * _sig(jnp.concatenate(rs, axis=1) + br)) * sp)


def _softplus_neg(lam):
    y = jnp.exp(-jnp.abs(lam))
    u = 1.0 + y
    tiny = u == 1.0
    l1p = jnp.where(tiny, y, jnp.log(u) * (y / jnp.where(tiny, 1.0, u - 1.0)))
    return jnp.maximum(-lam, 0.0) + l1p


def _gates(xc, wr_ref, br, wi_ref, bi, sp):
    xcb = xc.astype(BF)
    rs, is_ = [], []
    for g in range(H):
        gs = slice(g * DH, (g + 1) * DH)
        rs.append(jnp.dot(xcb[:, gs], wr_ref[g].astype(BF), preferred_element_type=F32))
        is_.append(jnp.dot(xcb[:, gs], wi_ref[g].astype(BF), preferred_element_type=F32))
    r = _sig(jnp.concatenate(rs, axis=1) + br)
    i = _sig(jnp.concatenate(is_, axis=1) + bi)
    log_a = (-RG_C * r) * sp
    a = jnp.exp(log_a)
    t = jnp.tanh(log_a)
    om = (-2.0 * t) / (1.0 - t)
    return r, i, a, om


def _scan_rows(d, nrows, a_s, b_s, h_s, h0):
    nsl = nrows // 8

    def slab(j, h):
        jj = (nsl - 1 - j) if d else j
        r0 = pl.multiple_of(jj * 8, 8)
        for t in (range(7, -1, -1) if d else range(8)):
            h = a_s[pl.ds(r0 + t, 1), :] * h + b_s[pl.ds(r0 + t, 1), :]
            h_s[pl.ds(r0 + t, 1), :] = h
        return h

    return lax.fori_loop(0, nsl, slab, h0)


def _col_of(d, ncols):
    if d:
        return lambda s: ncols - jnp.maximum(s, 1)
    return lambda s: jnp.maximum(s, 1) - 1


def _rglru_fwd(z, cw, cb, wr, br, wi, bi, lam, d, t_lat, t_ctx):
    m = z.shape[0]
    rows = t_lat // GRID_W
    z3 = z.reshape(m // GRID_W, GRID_W, IN_COLS)
    nblk = GRID_W // CB
    blk = _col_of(d, nblk)
    cblk = t_lat // t_ctx
    rc = min(RCH, rows)

    def body(zc_ref, zx_ref, cw_ref, cb_ref, wr_ref, br_ref, wi_ref, bi_ref, lam_ref,
             hx_ref, hpx_ref, hpc_ref, a_s, b_s, h_s, hcar, a3, b3, cin_s):
        s = pl.program_id(0)
        sp = _softplus_neg(lam_ref[d:d + 1, :])
        br_ = br_ref[d:d + 1, :]
        bi_ = bi_ref[d:d + 1, :]

        @pl.when(s == 0)
        def _():
            xc = _conv_fwd(zc_ref[...], cw_ref, cb_ref, t_ctx)
            _, i, a, om = _gates(xc, wr_ref, br_, wi_ref, bi_, sp)
            a_s[...] = a
            b_s[...] = jnp.sqrt(om) * (i * xc)
            h0 = jnp.zeros((1, D), F32)
            hcar[0:1, :] = _scan_rows(d, t_ctx, a_s, b_s, h_s, h0)
            hs = h_s[...]
            row = lax.broadcasted_iota(jnp.int32, (t_ctx, D), 0)
            if d:
                hpc_ref[...] = jnp.where(row == t_ctx - 1, h0, pltpu.roll(hs, t_ctx - 1, 0))
            else:
                hpc_ref[...] = jnp.where(row == 0, h0, pltpu.roll(hs, 1, 0))

        @pl.when(s > 0)
        def _():
            for r0 in range(0, rows, rc):
                xc = _conv_cols(zx_ref, cw_ref, cb_ref, r0, rc, rows).reshape(rc * CB, D)
                _, i, a, om = _gates(xc, wr_ref, br_, wi_ref, bi_, sp)
                a3[r0:r0 + rc] = a.reshape(rc, CB, D)
                b3[r0:r0 + rc] = (jnp.sqrt(om) * (i * xc)).reshape(rc, CB, D)

            def local(t, carry):
                hl, p = carry
                r = (rows - 1 - t) if d else t
                a = a3[r]
                hl = a * hl + b3[r]
                p = a * p
                b3[r] = hl
                a3[r] = p
                return hl, p

            hl, p = lax.fori_loop(0, rows, local, (jnp.zeros((CB, D), F32), jnp.ones((CB, D), F32)))
            cin = hcar[0:1, :]
            for j in (range(CB - 1, -1, -1) if d else range(CB)):
                cin_s[j:j + 1, :] = cin
                cin = hl[j:j + 1, :] + p[j:j + 1, :] * cin
            hcar[0:1, :] = cin
            c_in = cin_s[...]

            def fix(t, prev):
                r = (rows - 1 - t) if d else t
                h = b3[r] + a3[r] * c_in
                hx_ref[r] = h
                hpx_ref[r] = prev
                return h

            lax.fori_loop(0, rows, fix, c_in)

    full = lambda shp: pl.BlockSpec(shp, lambda s: (0,) * len(shp))
    colspec = pl.BlockSpec((rows, CB, D), lambda s: (0, blk(s), 0))
    outs = pl.pallas_call(
        body, grid=(nblk + 1,),
        in_specs=[pl.BlockSpec((t_ctx, D), lambda s: (cblk, 5)),
                  pl.BlockSpec((rows, CB, D), lambda s: (0, blk(s), 5)),
                  full((4, D)), full((1, D)),
                  pl.BlockSpec((None, H, DH, DH), lambda s: (d, 0, 0, 0)), full((2, D)),
                  pl.BlockSpec((None, H, DH, DH), lambda s: (d, 0, 0, 0)), full((2, D)), full((2, D))],
        out_specs=(colspec, colspec, full((t_ctx, D))),
        out_shape=(jax.ShapeDtypeStruct((rows, GRID_W, D), F32), jax.ShapeDtypeStruct((rows, GRID_W, D), F32),
                   jax.ShapeDtypeStruct((t_ctx, D), F32)),
        scratch_shapes=[pltpu.VMEM((t_ctx, D), F32), pltpu.VMEM((t_ctx, D), F32), pltpu.VMEM((t_ctx, D), F32),
                        pltpu.VMEM((8, D), F32), pltpu.VMEM((rows, CB, D), F32), pltpu.VMEM((rows, CB, D), F32),
                        pltpu.VMEM((CB, D), F32)],
        name=f"rglru_fwd{d}", compiler_params=_cp(("arbitrary",)))(z, z3, cw, cb, wr, br, wi, bi, lam)
    return outs[0].reshape(t_lat, D), outs[1].reshape(t_lat, D), outs[2]


def _rglru_bwd(z, cw, cb, wr, br, wi, bi, lam, dh_lat, hp_lat, hp_ctx, d, t_lat, t_ctx, prev=None):
    m = z.shape[0]
    rows = t_lat // GRID_W
    z3 = z.reshape(m // GRID_W, GRID_W, IN_COLS)
    nblk = GRID_W // CB
    fblk = _col_of(d, nblk)
    blk = lambda s: fblk(nblk - jnp.minimum(s, nblk - 1))
    cblk = t_lat // t_ctx
    rc = min(RCH, rows)
    last = prev is not None
    view3 = lambda v: v.reshape(rows, GRID_W, D)

    def body(*refs):
        (zc_ref, zx_ref, cw_ref, cb_ref, wr_ref, br_ref, wi_ref, bi_ref, lam_ref, dh_ref, hpx_ref, hpc_ref) = refs[:12]
        k = 12
        if last:
            pdx_ref, pdc_ref = refs[12:14]
            k = 14
        ox_ref, oc_ref, dwr_ref, dwi_ref, sum_ref, a_s, b_s, gcar, a3, b3, cin_s = refs[k:]
        s = pl.program_id(0)
        lam_d = lam_ref[d:d + 1, :]
        sp = _softplus_neg(lam_d)
        br_ = br_ref[d:d + 1, :]
        bi_ = bi_ref[d:d + 1, :]
        flat = lambda v: v.reshape(-1, D)

        @pl.when(s == 0)
        def _():
            gcar[...] = jnp.zeros_like(gcar)
            dwr_ref[...] = jnp.zeros_like(dwr_ref)
            dwi_ref[...] = jnp.zeros_like(dwi_ref)
            sum_ref[...] = jnp.zeros_like(sum_ref)


        def conv_sums(dxc, dxb, xm1, x0, xp1, xp2):
            sum_ref[3:4, :] += _colsum(flat(dxc))
            sum_ref[4:5, :] += _colsum(flat(dxb))
            sum_ref[8:9, :] += _colsum(flat(dxc * xm1))
            sum_ref[9:10, :] += _colsum(flat(dxc * x0))
            sum_ref[10:11, :] += _colsum(flat(dxc * xp1))
            sum_ref[11:12, :] += _colsum(flat(dxc * xp2))

        def gate_grads(g, hp, xc, rr, i, a, om):
            mult = jnp.sqrt(om)
            da = g * hp
            ixc = i * xc
            dmult = g * ixc
            dixc = g * mult
            di = dixc * xc
            dxc = dixc * i
            dlog_a = da * a - dmult * ((1.0 - om) / mult)
            dr = dlog_a * (-RG_C * sp)
            sum_ref[2:3, :] += _colsum(dlog_a * rr)
            drp = dr * rr * (1.0 - rr)
            dip = di * i * (1.0 - i)
            sum_ref[0:1, :] += _colsum(drp)
            sum_ref[1:2, :] += _colsum(dip)
            xcb = xc.astype(BF)
            drb = drp.astype(BF)
            dib = dip.astype(BF)
            parts = []
            for gi in range(H):
                gs = slice(gi * DH, (gi + 1) * DH)
                parts.append(_dot_nt(drb[:, gs], wr_ref[gi]) + _dot_nt(dib[:, gs], wi_ref[gi]))
                dwr_ref[gi] += _dot_tn(xcb[:, gs], drb[:, gs])
                dwi_ref[gi] += _dot_tn(xcb[:, gs], dib[:, gs])
            return dxc + jnp.concatenate(parts, axis=1)

        @pl.when(s < nblk)
        def _():
            for r0 in range(0, rows, rc):
                xc = flat(_conv_cols(zx_ref, cw_ref, cb_ref, r0, rc, rows))
                a = _gate_a(xc, wr_ref, br_, sp).reshape(rc, CB, D)
                a3[r0:r0 + rc] = a
                b3[r0:r0 + rc] = a * dh_ref[r0:r0 + rc]

            def local(t, carry):
                c, q = carry
                r = t if d else (rows - 1 - t)
                a = a3[r]
                c = a * c + b3[r]
                q = a * q
                b3[r] = c
                a3[r] = q
                return c, q

            c, q = lax.fori_loop(0, rows, local, (jnp.zeros((CB, D), F32), jnp.ones((CB, D), F32)))
            cin = gcar[0:1, :]
            for j in (range(CB) if d else range(CB - 1, -1, -1)):
                cin_s[j:j + 1, :] = cin
                cin = c[j:j + 1, :] + q[j:j + 1, :] * cin
            gcar[0:1, :] = cin
            c_in = cin_s[...]
            for r0 in (range(rows - rc, -1, -rc) if d else range(0, rows, rc)):
                if d:
                    lo = max(r0 - 1, 0)
                    cn = b3[lo:r0 + rc - 1] + a3[lo:r0 + rc - 1] * c_in
                    if r0 == 0:
                        cn = jnp.concatenate([c_in[None], cn], axis=0)
                else:
                    hi = min(r0 + rc + 1, rows)
                    cn = b3[r0 + 1:hi] + a3[r0 + 1:hi] * c_in
                    if hi == rows:
                        cn = jnp.concatenate([cn, c_in[None]], axis=0)
                g = flat(dh_ref[r0:r0 + rc] + cn)
                xc = flat(_conv_cols(zx_ref, cw_ref, cb_ref, r0, rc, rows))
                rr, i, a, om = _gates(xc, wr_ref, br_, wi_ref, bi_, sp)
                b3[r0:r0 + rc] = gate_grads(g, flat(hpx_ref[r0:r0 + rc]), xc, rr, i, a, om).reshape(rc, CB, D)
            if last:
                for r0 in range(0, rows, rc):
                    b3[r0:r0 + rc] = b3[r0:r0 + rc] + pdx_ref[r0:r0 + rc]
                for r0 in range(0, rows, rc):
                    w = _window(b3, r0 - 2, rc + 3, rows)
                    xw = _window(zx_ref, r0 - 1, rc + 3, rows)
                    dxc = w[2:rc + 2]
                    dxb = (w[3:rc + 3] * cw_ref[0:1, :] + dxc * cw_ref[1:2, :] + w[1:rc + 1] * cw_ref[2:3, :]
                           + w[0:rc] * cw_ref[3:4, :])
                    conv_sums(dxc, dxb, xw[0:rc], xw[1:rc + 1], xw[2:rc + 2], xw[3:rc + 3])
                    ox_ref[r0:r0 + rc] = dxb
            else:
                for r0 in range(0, rows, rc):
                    ox_ref[r0:r0 + rc] = b3[r0:r0 + rc]

        @pl.when(s == nblk)
        def _():
            r = t_ctx
            xb = zc_ref[...]
            xc = _conv_fwd(xb, cw_ref, cb_ref, r)
            rr, i, a, om = _gates(xc, wr_ref, br_, wi_ref, bi_, sp)
            a_s[...] = a
            b_s[...] = jnp.zeros((r, D), F32)
            c0 = gcar[0:1, :]
            _scan_rows(1 - d, r, a_s, b_s, b_s, c0)
            cs = b_s[...]
            row = lax.broadcasted_iota(jnp.int32, (r, D), 0)
            if d:
                g = jnp.where(row == 0, c0, pltpu.roll(cs, 1, 0))
            else:
                g = jnp.where(row == r - 1, c0, pltpu.roll(cs, r - 1, 0))
            dxc = gate_grads(g, hpc_ref[...], xc, rr, i, a, om)
            if last:
                dxc = dxc + pdc_ref[...]
                dxb = (_shift_rows(dxc, -1, r) * cw_ref[0:1, :] + dxc * cw_ref[1:2, :]
                       + _shift_rows(dxc, 1, r) * cw_ref[2:3, :] + _shift_rows(dxc, 2, r) * cw_ref[3:4, :])
                conv_sums(dxc, dxb, _shift_rows(xb, 1, r), xb, _shift_rows(xb, -1, r), _shift_rows(xb, -2, r))
                oc_ref[...] = dxb
            else:
                oc_ref[...] = dxc
            sum_ref[2:3, :] = sum_ref[2:3, :] * (RG_C * _sig(-lam_d))

    full = lambda shp: pl.BlockSpec(shp, lambda s: (0,) * len(shp))
    once = lambda shp: pl.BlockSpec(shp, lambda s: (0,) * len(shp), pipeline_mode=pl.Buffered(1))
    colspec = pl.BlockSpec((rows, CB, D), lambda s: (0, blk(s), 0))
    colonce = pl.BlockSpec((rows, CB, D), lambda s: (0, blk(s), 0), pipeline_mode=pl.Buffered(1))
    in_specs = [pl.BlockSpec((t_ctx, D), lambda s: (cblk, 5), pipeline_mode=pl.Buffered(1)),
                pl.BlockSpec((rows, CB, D), lambda s: (0, blk(s), 5), pipeline_mode=pl.Buffered(1)),
                full((4, D)), full((1, D)),
                pl.BlockSpec((None, H, DH, DH), lambda s: (d, 0, 0, 0)), full((2, D)),
                pl.BlockSpec((None, H, DH, DH), lambda s: (d, 0, 0, 0)), full((2, D)), full((2, D)),
                colonce, colonce, once((t_ctx, D))]
    args = [z, z3, cw, cb, wr, br, wi, bi, lam, view3(dh_lat), view3(hp_lat), hp_ctx]
    if last:
        in_specs += [colonce, once((t_ctx, D))]
        args += [view3(prev[0]), prev[1]]
    outs = pl.pallas_call(
        body, grid=(nblk + 1,), in_specs=in_specs,
        out_specs=(colspec, full((t_ctx, D)), full((H, DH, DH)), full((H, DH, DH)), full((16, D))),
        out_shape=(jax.ShapeDtypeStruct((rows, GRID_W, D), F32), jax.ShapeDtypeStruct((t_ctx, D), F32),
                   jax.ShapeDtypeStruct((H, DH, DH), F32), jax.ShapeDtypeStruct((H, DH, DH), F32),
                   jax.ShapeDtypeStruct((16, D), F32)),
        scratch_shapes=[pltpu.VMEM((t_ctx, D), F32), pltpu.VMEM((t_ctx, D), F32), pltpu.VMEM((8, D), F32),
                        pltpu.VMEM((rows, CB, D), F32), pltpu.VMEM((rows, CB, D), F32), pltpu.VMEM((CB, D), F32)],
        name=f"rglru_bwd{d}", compiler_params=_cp(("arbitrary",)))(*args)
    return (outs[0].reshape(t_lat, D), outs[1]) + tuple(outs[2:])


def _merge(o_f, o_b, h_f, h_b, z, x_all, tgt, mod, norm_g, ln_g, ln_b, p_a, p_b, w_out, t_lat):
    tm = 256
    nt = t_lat // tm

    def body(of_ref, ob_ref, hf_ref, hb_ref, z4_ref, z6_ref, z7_ref, z8_ref, x_ref, t_ref, mod_ref, ng_ref,
             lg_ref, lb_ref, pa_ref, pb_ref, wo_ref,
             do_ref, dh_ref, dz4_ref, dz6_ref, dz7_ref, dz8_ref, gx_ref,
             y_ref, dout_ref, oa_ref, dpa_ref, obv_ref, dpb_ref, acc_ref):
        i = pl.program_id(0)
        lat = i < nt
        latf = lat.astype(F32)

        @pl.when(i == 0)
        def _():
            acc_ref[...] = jnp.zeros_like(acc_ref)

        def per_head(v):
            return jnp.concatenate(
                [jnp.broadcast_to(jnp.mean(v[:, h * DH:(h + 1) * DH], axis=-1, keepdims=True), (tm, DH))
                 for h in range(H)], axis=1)

        gt = mod_ref[0:1, 2 * D:3 * D]
        gfull = jnp.concatenate([ng_ref[...]] * H, axis=1)
        o = of_ref[...] + ob_ref[...]
        rinv = lax.rsqrt(per_head(o * o) + RMS_EPS)
        n = o * rinv
        na = n * gfull
        z4 = z4_ref[...]
        s4 = _sig(z4)
        silu4 = z4 * s4
        oa = na * silu4
        z6 = z6_ref[...]
        s6 = _sig(z6)
        silu6 = z6 * s6
        hsum = hf_ref[...] + hb_ref[...]
        obv = hsum * silu6
        pa = _dot(oa, pa_ref[...])
        pb = _dot(obv, pb_ref[...])
        s7 = _sig(z7_ref[...])
        s8 = _sig(z8_ref[...])
        y = s7 * pa + s8 * pb
        out = _dot(y, wo_ref[...])
        pre = ALPHA * x_ref[...] + gt * out
        mu = jnp.mean(pre, axis=-1, keepdims=True)
        xc = pre - mu
        rstd = lax.rsqrt(jnp.mean(xc * xc, axis=-1, keepdims=True) + LN_EPS)
        xhat = xc * rstd
        lg = lg_ref[...]
        diff = xhat * lg + lb_ref[...] - t_ref[...]
        acc_ref[8:9, :] += _colsum(diff * diff) * (0.5 / D * latf)
        dxn = diff * (1.0 / D)
        acc_ref[1:2, :] += _colsum(dxn * xhat) * latf
        acc_ref[2:3, :] += _colsum(dxn) * latf
        dxhat = dxn * lg
        dpre = rstd * (dxhat - jnp.mean(dxhat, axis=-1, keepdims=True)
                       - xhat * jnp.mean(dxhat * xhat, axis=-1, keepdims=True))
        gx_ref[...] = ALPHA * dpre
        acc_ref[0:1, :] += _colsum(dpre * out) * latf
        dout = dpre * gt
        dy = _dot_nt(dout, wo_ref[...])
        dpa = dy * s7
        dpb = dy * s8
        dz7 = dy * pa * (s7 * (1.0 - s7))
        dz8 = dy * pb * (s8 * (1.0 - s8))
        doa = _dot_nt(dpa, pa_ref[...])
        dob = _dot_nt(dpb, pb_ref[...])
        dh_ref[...] = dob * silu6
        dz6 = dob * hsum * (s6 * (1.0 + z6 * (1.0 - s6)))
        dna = doa * silu4
        dz4 = doa * na * (s4 * (1.0 + z4 * (1.0 - s4)))
        dng = _colsum(dna * n)
        acc_ref[7:8, 0:DH] += sum(dng[:, h * DH:(h + 1) * DH] for h in range(H)) * latf
        dn = dna * gfull
        do_ref[...] = rinv * (dn - n * per_head(dn * n))
        acc_ref[3:4, :] += _colsum(dz4) * latf
        acc_ref[4:5, :] += _colsum(dz6) * latf
        acc_ref[5:6, :] += _colsum(dz7) * latf
        acc_ref[6:7, :] += _colsum(dz8) * latf
        dz4_ref[...] = (dz4 * latf).astype(BF)
        dz6_ref[...] = (dz6 * latf).astype(BF)
        dz7_ref[...] = (dz7 * latf).astype(BF)
        dz8_ref[...] = (dz8 * latf).astype(BF)
        y_ref[...] = y.astype(BF)
        dout_ref[...] = dout.astype(BF)
        oa_ref[...] = oa.astype(BF)
        dpa_ref[...] = dpa.astype(BF)
        obv_ref[...] = obv.astype(BF)
        dpb_ref[...] = dpb.astype(BF)

        @pl.when(i == nt - 1)
        def _():
            acc_ref[9:10, :] = jnp.broadcast_to(jnp.sum(acc_ref[8:9, :], axis=-1, keepdims=True), (1, D))

    m = x_all.shape[0]
    lrow = lambda i: jnp.minimum(i, nt - 1)
    row = pl.BlockSpec((tm, D), lambda i: (lrow(i), 0))
    allrow = pl.BlockSpec((tm, D), lambda i: (i, 0))
    zs = lambda cb: pl.BlockSpec((tm, D), lambda i: (lrow(i), cb))
    full = lambda shp: pl.BlockSpec(shp, lambda i: (0,) * len(shp))
    wfull = pl.BlockSpec((D, D), lambda i: (0, 0), pipeline_mode=pl.Buffered(1))
    f32o = jax.ShapeDtypeStruct((t_lat, D), F32)
    bfo = jax.ShapeDtypeStruct((t_lat, D), BF)
    bfall = jax.ShapeDtypeStruct((m, D), BF)
    return pl.pallas_call(
        body, grid=(m // tm,),
        in_specs=[row, row, row, row, zs(4), zs(6), zs(7), zs(8), row, row, full((16, 3 * D)), full((1, DH)),
                  full((1, D)), full((1, D)), wfull, wfull, wfull],
        out_specs=(row, row) + (allrow,) * 4 + (row,) * 7 + (full((16, D)),),
        out_shape=(f32o, f32o, bfall, bfall, bfall, bfall, f32o, bfo, bfo, bfo, bfo, bfo, bfo,
                   jax.ShapeDtypeStruct((16, D), F32)),
        name="merge", compiler_params=_cp(("arbitrary",), VMEM_LIMIT_MERGE))(
            o_f, o_b, h_f, h_b, z, z, z, z, x_all, tgt, mod, norm_g, ln_g, ln_b, p_a, p_b, w_out)


def _wgrad(a, b, name):
    tm = 1024

    def body(a_ref, b_ref, o_ref):
        @pl.when(pl.program_id(0) == 0)
        def _():
            o_ref[...] = jnp.zeros_like(o_ref)
        o_ref[...] += _dot_tn(a_ref[...], b_ref[...])

    row = pl.BlockSpec((tm, D), lambda i: (i, 0))
    return pl.pallas_call(body, grid=(a.shape[0] // tm,), in_specs=[row, row],
                          out_specs=pl.BlockSpec((D, D), lambda i: (0, 0)),
                          out_shape=jax.ShapeDtypeStruct((D, D), F32), name=name,
                          compiler_params=_cp(("arbitrary",)))(a, b)


def _wgrad_in(x_all, mod, dz, n_lat_tiles):
    m = x_all.shape[0]
    assert m % 128 == 0
    tm = m // 8
    n_lat = n_lat_tiles * 256

    def body(x_ref, mod_ref, dz_ref, o_ref):
        i = pl.program_id(1)

        @pl.when(i == 0)
        def _():
            o_ref[...] = jnp.zeros_like(o_ref)
        is_lat = (i * tm + lax.broadcasted_iota(jnp.int32, (tm, 1), 0)) < n_lat
        sh = jnp.where(is_lat, mod_ref[0:1, 0:D], mod_ref[1:2, 0:D])
        sc = jnp.where(is_lat, mod_ref[0:1, D:2 * D], mod_ref[1:2, D:2 * D])
        o_ref[0] += _dot_tn(x_ref[...] * (1.0 + sc) + sh, dz_ref[...])

    return pl.pallas_call(
        body, grid=(NSH, m // tm),
        in_specs=[pl.BlockSpec((tm, D), lambda n, i: (i, 0)),
                  pl.BlockSpec((16, 3 * D), lambda n, i: (0, 0)),
                  pl.BlockSpec((tm, SHC), lambda n, i: (i, n))],
        out_specs=pl.BlockSpec((1, D, SHC), lambda n, i: (n, 0, 0)),
        out_shape=jax.ShapeDtypeStruct((NSH, D, SHC), F32), name="wgrad_in",
        compiler_params=_cp(("arbitrary", "arbitrary")))(x_all, mod, dz)


def _du(dz, w_in_g, x_all, mod, gxres, n_lat_tiles, sums=()):
    m = x_all.shape[0]
    tm = 256
    nt = m // tm
    nct = nt - n_lat_tiles
    ns = len(sums)
    rblk = lambda i: jnp.where(i < nct, n_lat_tiles + i, i - nct)
    lblk = lambda i: jnp.maximum(i - nct, 0)

    def body(*refs):
        dz_ref, w_ref, x_ref, mod_ref, gr_ref = refs[:5]
        sum_refs = refs[5:5 + ns]
        gx_ref, dm_ref = refs[5 + ns:7 + ns]
        got_refs = refs[7 + ns:7 + 2 * ns]
        sems = refs[7 + 2 * ns:]
        i = pl.program_id(0)
        is_lat = i >= nct

        @pl.when(i == 0)
        def _():
            dm_ref[...] = jnp.zeros_like(dm_ref)
            if ns:
                for cp in _rs_chip_copies(sum_refs, got_refs, *sems):
                    cp.start()

        du = _dot_nt(dz_ref[:, 0:SHC], w_ref[0])
        for n in range(1, NSH):
            du = du + _dot_nt(dz_ref[:, n * SHC:(n + 1) * SHC], w_ref[n])
        sc = jnp.where(is_lat, mod_ref[0:1, D:2 * D], mod_ref[1:2, D:2 * D])
        dsh = _colsum(du)
        dsc = _colsum(du * x_ref[...])

        @pl.when(is_lat)
        def _():
            gx_ref[...] = du * (1.0 + sc) + gr_ref[...]
            dm_ref[0:1, 0:D] += dsh
            dm_ref[0:1, D:2 * D] += dsc

        @pl.when(jnp.logical_not(is_lat))
        def _():
            dm_ref[1:2, 0:D] += dsh
            dm_ref[1:2, D:2 * D] += dsc

        if ns:
            @pl.when(i == nt - 1)
            def _():
                for cp in _rs_chip_copies(sum_refs, got_refs, *sems):
                    cp.wait()

    outs = pl.pallas_call(
        body, grid=(nt,),
        in_specs=[pl.BlockSpec((tm, IN_COLS), lambda i: (rblk(i), 0)),
                  pl.BlockSpec((NSH, D, SHC), lambda i: (0, 0, 0), pipeline_mode=pl.Buffered(1)),
                  pl.BlockSpec((tm, D), lambda i: (rblk(i), 0)),
                  pl.BlockSpec((16, 3 * D), lambda i: (0, 0)),
                  pl.BlockSpec((tm, D), lambda i: (lblk(i), 0))] + [_ANY] * ns,
        out_specs=[pl.BlockSpec((tm, D), lambda i: (lblk(i), 0)),
                   pl.BlockSpec((8, 2 * D), lambda i: (0, 0))] + [_ANY] * ns,
        out_shape=[jax.ShapeDtypeStruct((n_lat_tiles * tm, D), F32), jax.ShapeDtypeStruct((8, 2 * D), F32)]
        + [jax.ShapeDtypeStruct((3,) + g.shape[1:], g.dtype) for g in sums],
        scratch_shapes=[pltpu.SemaphoreType.DMA((3 * ns,)), pltpu.SemaphoreType.DMA((3 * ns,))] if ns else [],
        name="du", compiler_params=_cp(("arbitrary",)))(dz, w_in_g, x_all, mod, gxres, *sums)
    return outs[0], outs[1], list(outs[2:])


def _row_tile(rows, cols):
    t = 8
    while t * 2 * cols * 4 <= (1 << 20) and rows % (t * 2) == 0:
        t *= 2
    return t


def _adamw_update(w_ref, g_ref, m_ref, v_ref, d_ref, nm_ref, nv_ref):
    gg = g_ref[...]
    m2 = ADAM_B1 * m_ref[...] + (1.0 - ADAM_B1) * gg
    v2 = ADAM_B2 * v_ref[...] + (1.0 - ADAM_B2) * (gg * gg)
    m_hat = m2 / (1.0 - ADAM_B1 ** ADAM_STEP)
    v_hat = v2 / (1.0 - ADAM_B2 ** ADAM_STEP)
    d_ref[...] = -ADAM_LR * (m_hat / (jnp.sqrt(v_hat) + ADAM_EPS) + ADAM_WD * w_ref[...])
    nm_ref[...] = m2
    nv_ref[...] = v2


def _adamw_many(ws, gs, ms, vs):
    n = len(ws)

    def body(*refs):
        for j in range(n):
            _adamw_update(*refs[4 * j:4 * j + 4], *refs[4 * n + 3 * j:4 * n + 3 * j + 3])

    args = [a for quad in zip(ws, gs, ms, vs) for a in quad]
    outs = pl.pallas_call(body, out_shape=[jax.ShapeDtypeStruct(w.shape, F32) for w in ws for _ in range(3)],
                          name="adamw_small", compiler_params=_cp())(*args)
    return outs[0::3], outs[1::3], outs[2::3]


def _adamw(w, g, m, v, name):
    rows, cols = w.shape
    tr = _row_tile(rows, cols)

    def body(*refs):
        _adamw_update(*refs)

    spec = pl.BlockSpec((tr, cols), lambda i: (i, 0))
    o = jax.ShapeDtypeStruct((rows, cols), F32)
    return pl.pallas_call(body, grid=(rows // tr,), in_specs=[spec] * 4, out_specs=(spec,) * 3,
                          out_shape=(o, o, o), name=name, compiler_params=_cp(("arbitrary",)))(w, g, m, v)


_ANY = pl.BlockSpec(memory_space=pl.ANY)


def _place():
    return lax.axis_index("x"), lax.axis_index("y"), lax.axis_index("c")


def _ag_copies(ins, outs, send, recv, fsend, frecv, lsem):
    x, y, c = _place()
    me = 2 * x + y
    chips = ((1 - x, y), (x, 1 - y), (1 - x, 1 - y))
    local, chip, hand = [], [], []
    for j in range(len(ins)):
        hr = ins[j].shape[0] // 2
        half = pl.ds(pl.multiple_of(c * hr, 8), hr)
        local.append(pltpu.make_async_copy(ins[j], outs[j].at[me], lsem.at[j]))
        for k, (px, py) in enumerate(chips):
            chip.append(pltpu.make_async_remote_copy(
                src_ref=ins[j].at[half, :], dst_ref=outs[j].at[me, half, :], send_sem=send.at[3 * j + k],
                recv_sem=recv.at[3 * j + k], device_id=(px, py, c), device_id_type=MESH))
            got = outs[j].at[2 * px + py, half, :]
            hand.append(pltpu.make_async_remote_copy(
                src_ref=got, dst_ref=got, send_sem=fsend.at[3 * j + k], recv_sem=frecv.at[3 * j + k],
                device_id=(x, y, 1 - c), device_id_type=MESH))
    return local, chip, hand


def _ag_sems(n):
    return [pltpu.SemaphoreType.DMA((3 * n,))] * 4 + [pltpu.SemaphoreType.DMA((n,))]


def _ag_finish(local, chip, hand, done=0):
    for k in range(done, len(chip)):
        chip[k].wait_recv()
        hand[k].start()
    for cp in chip:
        cp.wait_send()
    for k in range(done):
        hand[k].wait_send()
    for k in range(done, len(chip)):
        hand[k].wait_send()
        hand[k].wait_recv()
    for cp in local:
        cp.wait()


def _mod_tp(c8, c_ctx, w_mod_sh, b_mod_sh):
    mc = w_mod_sh.shape[1]

    def body(c8_ref, cctx_ref, w_ref, b_ref, mod_ref, cc_ref, cc_s, part_s, send1, recv1, send3, recv3):
        x, y, c = _place()
        me = 4 * x + 2 * y + c
        ms = 2 * x + y
        copies = []
        for k in range(1, 8):
            peer = (x ^ ((k >> 2) & 1), y ^ ((k >> 1) & 1), c ^ (k & 1))
            cp = pltpu.make_async_remote_copy(src_ref=c8_ref, dst_ref=cc_s.at[me], send_sem=send1.at[k],
                                              recv_sem=recv1.at[k], device_id=peer, device_id_type=MESH)
            cp.start()
            copies.append(cp)
        cc_s[me] = c8_ref[...]
        for cp in copies:
            cp.wait()
        cc_ref[...] = jnp.zeros_like(cc_ref)
        for j in range(8):
            cc_ref[j:j + 1, :] = cc_s[j, 0:1, :]
        cc_ref[8:9, :] = cctx_ref[...]
        v = cc_ref[...]
        part_s[ms] = _dot(v * _sig(v), w_ref[...]) + b_ref[...]
        copies = []
        for k in range(1, 4):
            peer = (x ^ ((k >> 1) & 1), y ^ (k & 1), c)
            cp = pltpu.make_async_remote_copy(src_ref=part_s.at[ms], dst_ref=part_s.at[ms], send_sem=send3.at[k],
                                              recv_sem=recv3.at[k], device_id=peer, device_id_type=MESH)
            cp.start()
            copies.append(cp)
        for cp in copies:
            cp.wait()
        for s in range(NSH):
            mod_ref[:, s * mc:(s + 1) * mc] = part_s[s]

    vm = pl.BlockSpec(memory_space=pltpu.VMEM)
    return pl.pallas_call(
        body, in_specs=[vm] * 4, out_specs=(vm, vm),
        out_shape=(jax.ShapeDtypeStruct((16, NSH * mc), F32), jax.ShapeDtypeStruct((16, D), F32)),
        scratch_shapes=[pltpu.VMEM((8, 8, D), F32), pltpu.VMEM((NSH, 16, mc), F32),
                        pltpu.SemaphoreType.DMA((8,)), pltpu.SemaphoreType.DMA((8,)),
                        pltpu.SemaphoreType.DMA((4,)), pltpu.SemaphoreType.DMA((4,))],
        name="mod_tp", compiler_params=_cp())(c8, c_ctx, w_mod_sh, b_mod_sh)


def _inproj_ag(x_all, mod, w_in_sh, b_in, narrow_sh, n_lat_tiles):
    m = x_all.shape[0]
    tm = 256
    nt = m // tm
    x_, y_ = lax.axis_index("x"), lax.axis_index("y")
    sids = jnp.stack([2 * x_ + y_, 2 * (1 - x_) + y_, 2 * x_ + 1 - y_, 2 * (1 - x_) + 1 - y_]).astype(jnp.int32)

    def body(sid_ref, x_ref, mod_ref, b_ref, wsh_ref, nsh_ref, z_ref, wg_ref, ng_ref, w_s, *sems):
        n = pl.program_id(0)
        i = pl.program_id(1)
        ag = ((wsh_ref, nsh_ref), (wg_ref, ng_ref)) + tuple(sems[:5])
        wsem = sems[5]

        def load(src):
            cp = pltpu.make_async_copy(src, w_s, wsem)
            cp.start()
            cp.wait()

        @pl.when((n == 0) & (i == 0))
        def _():
            local, chip, _ = _ag_copies(*ag)
            for cp in chip + local:
                cp.start()
            load(wsh_ref)

        for k in range(NSH - 1):
            @pl.when((n == k + 1) & (i == 0))
            def _():
                _, chip, hand = _ag_copies(*ag)
                chip[k].wait_recv()
                hand[k].start()
                hand[k].wait_recv()
                load(wg_ref.at[sid_ref[k + 1]])

        u, _ = _modulate(x_ref, mod_ref, i < n_lat_tiles)
        z_ref[...] = _dot(u, w_s[...]) + b_ref[...]

        @pl.when((n == NSH - 1) & (i == nt - 1))
        def _():
            _ag_finish(*_ag_copies(*ag), done=NSH - 1)

    outs = pl.pallas_call(
        body, grid_spec=pltpu.PrefetchScalarGridSpec(
            num_scalar_prefetch=1, grid=(NSH, nt),
            in_specs=[pl.BlockSpec((tm, D), lambda n, i, sid: (i, 0)),
                      pl.BlockSpec((16, 3 * D), lambda n, i, sid: (0, 0)),
                      pl.BlockSpec((1, SHC), lambda n, i, sid: (0, sid[n])), _ANY, _ANY],
            out_specs=[pl.BlockSpec((tm, SHC), lambda n, i, sid: (i, sid[n])), _ANY, _ANY],
            scratch_shapes=[pltpu.VMEM((D, SHC), BF)] + _ag_sems(2) + [pltpu.SemaphoreType.DMA]),
        out_shape=[jax.ShapeDtypeStruct((m, IN_COLS), F32), jax.ShapeDtypeStruct((NSH,) + w_in_sh.shape, BF),
                   jax.ShapeDtypeStruct((NSH,) + narrow_sh.shape, narrow_sh.dtype)],
        name="inproj_ag", compiler_params=_cp(("arbitrary", "arbitrary")))(sids, x_all, mod, b_in, w_in_sh, narrow_sh)
    return outs


def _rs_sibling(grads):
    n = len(grads)

    def body(*refs):
        ins, got = refs[:n], refs[n:2 * n]
        send, recv = refs[2 * n:]
        x, y, c = _place()
        copies = []
        for j in range(n):
            hr = ins[j].shape[1] // 2
            for s in range(NSH):
                give = ins[j].at[s, pl.ds(pl.multiple_of((1 - c) * hr, 8), hr), :]
                cp = pltpu.make_async_remote_copy(src_ref=give, dst_ref=got[j].at[s], send_sem=send.at[NSH * j + s],
                                                  recv_sem=recv.at[NSH * j + s], device_id=(x, y, 1 - c),
                                                  device_id_type=MESH)
                cp.start()
                copies.append(cp)
        for cp in copies:
            cp.wait()

    half = [jax.ShapeDtypeStruct((NSH, g.shape[1] // 2, g.shape[2]), F32) for g in grads]
    return pl.pallas_call(
        body, in_specs=[_ANY] * n, out_specs=[_ANY] * n, out_shape=half,
        scratch_shapes=[pltpu.SemaphoreType.DMA((NSH * n,)), pltpu.SemaphoreType.DMA((NSH * n,))],
        name="rs_sibling")(*grads)


def _core_vec():
    return lax.axis_index("c").astype(jnp.int32).reshape(1)


def _rs_add1(g, got, name):
    _, r, cols = g.shape
    hr = r // 2
    tr = _row_tile(hr, cols)
    nb = hr // tr

    def body(c_ref, g_ref, got_ref, o_ref):
        o_ref[...] = (g_ref[...] + got_ref[...]).astype(BF)

    spec = pl.BlockSpec((1, tr, cols), lambda s, i, c_ref: (s, i, 0))
    return pl.pallas_call(
        body, grid_spec=pltpu.PrefetchScalarGridSpec(
            num_scalar_prefetch=1, grid=(NSH, nb),
            in_specs=[pl.BlockSpec((1, tr, cols), lambda s, i, c_ref: (s, c_ref[0] * nb + i, 0)), spec],
            out_specs=spec),
        out_shape=jax.ShapeDtypeStruct((NSH, hr, cols), BF), name=name,
        compiler_params=_cp(("arbitrary", "arbitrary")))(_core_vec(), g, got)


def _rs_add2(sums, got, name):
    _, hr, cols = sums.shape
    tr = _row_tile(hr, cols)
    nb = hr // tr
    place = jnp.stack([2 * lax.axis_index("x") + lax.axis_index("y"), lax.axis_index("c")]).astype(jnp.int32)

    def body(p_ref, s_ref, got_ref, o_ref):
        f = lambda v: v.astype(F32)
        o_ref[...] = f(s_ref[0]) + f(got_ref[0]) + f(got_ref[1]) + f(got_ref[2])

    return pl.pallas_call(
        body, grid_spec=pltpu.PrefetchScalarGridSpec(
            num_scalar_prefetch=1, grid=(nb,),
            in_specs=[pl.BlockSpec((1, tr, cols), lambda i, p_ref: (p_ref[0], i, 0)),
                      pl.BlockSpec((3, tr, cols), lambda i, p_ref: (0, i, 0))],
            out_specs=pl.BlockSpec((tr, cols), lambda i, p_ref: (p_ref[1] * nb + i, 0))),
        out_shape=jax.ShapeDtypeStruct((2 * hr, cols), F32), name=name,
        compiler_params=_cp(("arbitrary",)))(place, sums, got)


def _rs_chip_copies(ins, got, send, recv):
    x, y, c = _place()
    peers = ((1 - x, y), (x, 1 - y), (1 - x, 1 - y))
    return [pltpu.make_async_remote_copy(src_ref=ins[j].at[2 * px + py], dst_ref=got[j].at[k],
                                         send_sem=send.at[3 * j + k], recv_sem=recv.at[3 * j + k],
                                         device_id=(px, py, c), device_id_type=MESH)
            for j in range(len(ins)) for k, (px, py) in enumerate(peers)]


def _ag_sibling(fulls):
    n = len(fulls)
    nck = 4

    def body(*refs):
        outs = refs[n:2 * n]
        send, recv = refs[2 * n:]
        x, y, c = _place()
        copies = []
        for j in range(n):
            qr = outs[j].shape[0] // (2 * nck)
            for k in range(nck):
                rows = outs[j].at[pl.ds(pl.multiple_of((c * nck + k) * qr, 8), qr), :]
                cp = pltpu.make_async_remote_copy(src_ref=rows, dst_ref=rows, send_sem=send.at[nck * j + k],
                                                  recv_sem=recv.at[nck * j + k], device_id=(x, y, 1 - c),
                                                  device_id_type=MESH)
                cp.start()
                copies.append(cp)
        for cp in copies:
            cp.wait()

    return pl.pallas_call(
        body, in_specs=[_ANY] * n, out_specs=[_ANY] * n,
        out_shape=[jax.ShapeDtypeStruct(f.shape, F32) for f in fulls],
        input_output_aliases={j: j for j in range(n)},
        scratch_shapes=[pltpu.SemaphoreType.DMA((nck * n,)), pltpu.SemaphoreType.DMA((nck * n,))],
        name="ag_sibling")(*fulls)


def _allreduce_small(buf):
    rows = buf.shape[0]
    pr = rows // 8

    def body(in_ref, out_ref, stage, send1, recv1, send2, recv2):
        x, y, c = _place()
        me = 4 * x + 2 * y + c

        def peer(k):
            kx, ky, kc = (k >> 2) & 1, (k >> 1) & 1, k & 1
            return (x ^ kx, y ^ ky, c ^ kc)

        def piece(ref, idx):
            return ref.at[pl.ds(pl.multiple_of(idx * pr, 8), pr), :]

        copies = []
        for k in range(1, 8):
            px, py, pc = peer(k)
            cp = pltpu.make_async_remote_copy(src_ref=piece(in_ref, 4 * px + 2 * py + pc), dst_ref=stage.at[k],
                                              send_sem=send1.at[k], recv_sem=recv1.at[k],
                                              device_id=(px, py, pc), device_id_type=MESH)
            cp.start()
            copies.append(cp)
        for cp in copies:
            cp.wait()
        acc = piece(in_ref, me)[...]
        for k in range(1, 8):
            acc = acc + stage[k]
        piece(out_ref, me)[...] = acc
        copies = []
        for k in range(1, 8):
            cp = pltpu.make_async_remote_copy(src_ref=piece(out_ref, me), dst_ref=piece(out_ref, me),
                                              send_sem=send2.at[k], recv_sem=recv2.at[k],
                                              device_id=peer(k), device_id_type=MESH)
            cp.start()
            copies.append(cp)
        for cp in copies:
            cp.wait()

    vm = pl.BlockSpec(memory_space=pltpu.VMEM)
    return pl.pallas_call(
        body, in_specs=[vm], out_specs=vm, out_shape=jax.ShapeDtypeStruct((rows, D), F32),
        scratch_shapes=[pltpu.VMEM((8, pr, D), F32)] + [pltpu.SemaphoreType.DMA((8,))] * 4,
        name="allreduce_small", compiler_params=_cp())(buf)


def _rows(a):
    flat = a.reshape(-1)
    pad = (-flat.shape[0]) % D
    if pad:
        flat = jnp.concatenate([flat, jnp.zeros((pad,), flat.dtype)])
    return flat.reshape(-1, D)


def _pad_rows(a, mult):
    pad = (-a.shape[0]) % mult
    return jnp.concatenate([a, jnp.zeros((pad, a.shape[1]), a.dtype)]) if pad else a


def _local_step(x, c, ctx, c_ctx, tgt, me, shard, sh, b_mod, b_in, norm_g, cb, wr, wi, ln_g, ln_b):
    t_lat, t_ctx = x.shape[0], ctx.shape[0]
    nlt = t_lat // 256
    nlb, ncb = t_lat // RB, t_ctx // RB
    mc = 3 * D // NSH
    mod_all, cc_all = _mod_tp(jnp.zeros((8, D), F32).at[0].set(c), c_ctx.reshape(1, D), sh["w_mod"],
                              lax.dynamic_slice_in_dim(b_mod, shard * mc, mc, axis=1))
    mod = jnp.zeros((16, 3 * D), F32).at[0].set(mod_all[me]).at[1].set(mod_all[8])
    cc = jnp.zeros((16, D), F32).at[0].set(c).at[1].set(c_ctx)
    x_all = jnp.concatenate([x, ctx], axis=0)
    z, w_in_g, nar = _inproj_ag(x_all, mod, sh["w_in"], b_in, sh["narrow"], nlt)
    nar = jnp.transpose(nar, (1, 0, 2)).reshape(-1, D)
    lbl, cw, br, bi, lam = nar[0:4].reshape(2, 2, D), nar[4:8], nar[8:10], nar[10:12], nar[12:14]
    o0, st0, (w_mod_g, p_a, p_b, w_out) = _gla_fwd(z, lbl, 0, nlb, ncb,
                                                   gather=[sh[k] for k in ("w_mod", "p_a", "p_b", "w_out")])
    p_a, p_b, w_out = p_a.reshape(D, D), p_b.reshape(D, D), w_out.reshape(D, D)
    o1, st1, _ = _gla_fwd(z, lbl, 1, nlb, ncb)
    h0, hp0, hpc0 = _rglru_fwd(z, cw, cb, wr, br, wi, bi, lam, 0, t_lat, t_ctx)
    h1, hp1, hpc1 = _rglru_fwd(z, cw, cb, wr, br, wi, bi, lam, 1, t_lat, t_ctx)
    (do, dh, dz4, dz6, dz7, dz8, gxres, y, dout, oa, dpa, obv, dpb, acc) = _merge(
        o0, o1, h0, h1, z, x_all, tgt, mod, norm_g, ln_g, ln_b, p_a, p_b, w_out, t_lat)
    gp_a = _wgrad(oa, dpa, "wgrad_pa")
    gp_b = _wgrad(obv, dpb, "wgrad_pb")
    gw_out = _wgrad(y, dout, "wgrad_wout")
    dxc_lat, dxc_ctx, dwr0, dwi0, sb0 = _rglru_bwd(z, cw, cb, wr, br, wi, bi, lam, dh, hp0, hpc0, 0, t_lat, t_ctx)
    dz5_lat, dz5_ctx, dwr1, dwi1, sb1 = _rglru_bwd(z, cw, cb, wr, br, wi, bi, lam, dh, hp1, hpc1, 1, t_lat, t_ctx,
                                                   prev=(dxc_lat, dxc_ctx))
    dq0, dv0, dz1, sa0 = _gla_bwd(z, lbl, do, st0, 0, nlb, ncb)
    dz0, dz3, dz2, sa1 = _gla_bwd(z, lbl, do, st1, 1, nlb, ncb, prev=(dq0, dv0))
    dz = jnp.concatenate([dz0, dz1, dz2, dz3, dz4, jnp.concatenate([dz5_lat, dz5_ctx], axis=0).astype(BF),
                          dz6, dz7, dz8], axis=1)
    big = dict(w_in=_wgrad_in(x_all, mod, dz, nlt), p_a=gp_a.reshape(NSH, D // NSH, D),
               p_b=gp_b.reshape(NSH, D // NSH, D), w_out=gw_out.reshape(NSH, D // NSH, D))
    grads = [big[k] for k in _RS]
    sums = [_rs_add1(g, b, f"rs_add1_{j}") for j, (g, b) in enumerate(zip(grads, _rs_sibling(grads)))]
    gx, dm, got = _du(dz, w_in_g, x_all, mod, gxres, nlt, sums)
    fulls = [_rs_add2(a, b, f"rs_add2_{j}") for j, (a, b) in enumerate(zip(sums, got))]
    big = dict(zip(_RS, _ag_sibling(fulls)))
    dmod = jnp.zeros((16, 3 * D), F32).at[0:2, 0:2 * D].set(dm[0:2]).at[0, 2 * D:].set(acc[0])
    dcc = _mod_bwd(cc, dmod, w_mod_g)
    small = dict(
        c_ctx=dcc[1:2], b_mod=(dmod[0] + dmod[1]).reshape(3, D),
        b_in=jnp.stack([sa1[2], sa0[0], sa1[0], sa1[3], acc[3], sb1[4], acc[4], acc[5], acc[6]]),
        lb_logits=jnp.stack([sa0[1], sa1[1], -sa0[1], -sa1[1]]),
        norm_a_g=acc[7:8], conv_w=sb1[8:12], conv_b=sb1[3:4],
        w_r=jnp.stack([dwr0, dwr1]).reshape(-1, D), w_i=jnp.stack([dwi0, dwi1]).reshape(-1, D),
        b_r=jnp.stack([sb0[0], sb1[0]]), b_i=jnp.stack([sb0[1], sb1[1]]), lam=jnp.stack([sb0[2], sb1[2]]),
        ln_g=acc[1:2], ln_b=acc[2:3])
    return acc[9, 0], gx, big, small, dmod, cc_all


_RS =("w_in", "p_a", "p_b", "w_out")
_SMALL =("c_ctx", "b_mod", "b_in", "lb_logits", "norm_a_g", "conv_w", "conv_b", "w_r", "w_i", "b_r", "b_i", "lam",
          "ln_g", "ln_b")
_BIG = ("w_mod", "w_in", "p_a", "p_b", "w_out")
_COL_SHARDED = ("lb_logits", "conv_w", "b_r", "b_i", "lam")
_WEIGHTS = ("c_ctx", "w_mod", "b_mod", "w_in", "b_in", "lb_logits", "norm_a_g", "conv_w", "conv_b", "w_r", "b_r", "w_i",
            "b_i", "lam", "p_a", "p_b", "w_out", "ln_g", "ln_b")


def kernel(x, c, ctx, c_ctx, w_mod, b_mod, w_in, b_in, lb_logits, norm_a_g, conv_w, conv_b, w_r, b_r, w_i, b_i, lam, p_a, p_b, w_out, ln_g, ln_b, loss_target, m_c_ctx, m_w_mod, m_b_mod, m_w_in, m_b_in, m_lb_logits, m_norm_a_g, m_conv_w, m_conv_b, m_w_r, m_b_r, m_w_i, m_b_i, m_lam, m_p_a, m_p_b, m_w_out, m_ln_g, m_ln_b, v_c_ctx, v_w_mod, v_b_mod, v_w_in, v_b_in, v_lb_logits, v_norm_a_g, v_conv_w, v_conv_b, v_w_r, v_b_r, v_w_i, v_b_i, v_lam, v_p_a, v_p_b, v_w_out, v_ln_g, v_ln_b):
    w = dict(c_ctx=c_ctx, w_mod=w_mod, b_mod=b_mod, w_in=w_in, b_in=b_in, lb_logits=lb_logits, norm_a_g=norm_a_g,
             conv_w=conv_w, conv_b=conv_b, w_r=w_r, b_r=b_r, w_i=w_i, b_i=b_i, lam=lam, p_a=p_a, p_b=p_b, w_out=w_out,
             ln_g=ln_g, ln_b=ln_b)
    m = dict(c_ctx=m_c_ctx, w_mod=m_w_mod, b_mod=m_b_mod, w_in=m_w_in, b_in=m_b_in, lb_logits=m_lb_logits,
             norm_a_g=m_norm_a_g, conv_w=m_conv_w, conv_b=m_conv_b, w_r=m_w_r, b_r=m_b_r, w_i=m_w_i, b_i=m_b_i,
             lam=m_lam, p_a=m_p_a, p_b=m_p_b, w_out=m_w_out, ln_g=m_ln_g, ln_b=m_ln_b)
    v = dict(c_ctx=v_c_ctx, w_mod=v_w_mod, b_mod=v_b_mod, w_in=v_w_in, b_in=v_b_in, lb_logits=v_lb_logits,
             norm_a_g=v_norm_a_g, conv_w=v_conv_w, conv_b=v_conv_b, w_r=v_w_r, b_r=v_b_r, w_i=v_w_i, b_i=v_b_i,
             lam=v_lam, p_a=v_p_a, p_b=v_p_b, w_out=v_w_out, ln_g=v_ln_g, ln_b=v_ln_b)
    shard = 2 * lax.axis_index("x") + lax.axis_index("y")
    cs = D // NSH

    sh = {k: w[k][0].astype(BF) for k in _BIG}
    sh["narrow"] = _pad_rows(jnp.concatenate([lb_logits.reshape(4, cs), conv_w[0], b_r[0], b_i[0], lam[0]], axis=0), 8)
    me = 2 * shard + lax.axis_index("c")
    loss, gx, big, small, dmod, cc_all = _local_step(
        x[0], c[0], ctx[0], c_ctx, loss_target[0], me, shard, sh, b_mod, b_in, norm_a_g, conv_b, w_r[0], w_i[0],
        ln_g, ln_b)
    loss = lax.psum(loss, ("x", "y", "c"))

    dmod_rows = jnp.zeros((16, 3 * D), F32).at[me].set(dmod[0]).at[8].set(dmod[1]).reshape(48, D)
    sizes = [small[k].shape[0] for k in _SMALL]
    red = _allreduce_small(_pad_rows(jnp.concatenate([_pad_rows(small[k], 8) for k in _SMALL] + [dmod_rows],
                                                     axis=0), 64))
    grads = {}
    off = 0
    for k, n in zip(_SMALL, sizes):
        g = red[off:off + n]
        off += n + (-n) % 8
        if k == "norm_a_g":
            g = g[:, :DH]
        if k in _COL_SHARDED:
            g = lax.dynamic_slice_in_dim(g, shard * cs, cs, axis=1)
        grads[k] = g.reshape(w[k].shape)
    for k in _RS:
        grads[k] = big[k].reshape(w[k].shape)
    dmod_all = red[off:off + 48].reshape(16, 3 * D)
    mc = 3 * D // NSH
    grads["w_mod"] = _wmod_grad(cc_all, lax.dynamic_slice_in_dim(dmod_all, shard * mc, mc, axis=1)).reshape(
        w["w_mod"].shape)

    delta, new_m, new_v = {}, {}, {}
    for k in _BIG:
        shp = w[k].shape
        two = lambda a: a.reshape(shp[-2], shp[-1])
        d_, m_, v_ = _adamw(two(w[k]), two(grads[k]), two(m[k]), two(v[k]), f"adamw_{k}")
        delta[k], new_m[k], new_v[k] = d_.reshape(shp), m_.reshape(shp), v_.reshape(shp)
    d_, m_, v_ = _adamw_many(*[[t[k] for k in _SMALL] for t in (w, grads, m, v)])
    delta.update(zip(_SMALL, d_))
    new_m.update(zip(_SMALL, m_))
    new_v.update(zip(_SMALL, v_))

    return (loss, gx[None], *[grads[k] for k in _WEIGHTS], *[delta[k] for k in _WEIGHTS],
            *[new_m[k] for k in _WEIGHTS], *[new_v[k] for k in _WEIGHTS])
```

```python
import functools

import jax
import jax.numpy as jnp
from jax import lax
from jax.experimental import pallas as pl
from jax.experimental.pallas import tpu as pltpu

F32 = jnp.float32
BF = jnp.bfloat16
MESH = pl.DeviceIdType.MESH

D = 1024
H = 8
DH = 128
CH = 64
RB = 256
NCK = RB // CH
GRID_W = 64
CB = 8
RCH = 16
IN_COLS = 9 * D
NSH = 4
SHC = IN_COLS // NSH
RG_C = 8.0
ALPHA = 2.0 ** 0.25
LN_EPS = 1e-5
RMS_EPS = 1e-6
Q_SCALE = DH ** -0.5
ADAM_LR, ADAM_B1, ADAM_B2, ADAM_EPS, ADAM_WD, ADAM_STEP = 0.001, 0.9, 0.999, 1e-08, 0.01, 10
VMEM_LIMIT = 56 * 1024 * 1024
VMEM_LIMIT_MERGE = 60 * 1024 * 1024


def _cp(sem=None, vmem=VMEM_LIMIT):
    return pltpu.CompilerParams(dimension_semantics=sem, vmem_limit_bytes=vmem)


def _sig(x):
    return 0.5 * jnp.tanh(0.5 * x) + 0.5


def _dot(a, b):
    return jnp.dot(a.astype(BF), b.astype(BF), preferred_element_type=F32)


def _dot_nt(a, b):
    return lax.dot_general(a.astype(BF), b.astype(BF), (((1,), (1,)), ((), ())), preferred_element_type=F32)


def _dot_tn(a, b):
    return lax.dot_general(a.astype(BF), b.astype(BF), (((0,), (0,)), ((), ())), preferred_element_type=F32)


def _colsum(v):
    return jnp.sum(v, axis=0, keepdims=True)


def _mod_bwd(cc, dmod, w_mod_g):
    def body(cc_ref, dm_ref, w_ref, dcc_ref):
        v = cc_ref[...]
        sg = _sig(v)
        ds = jnp.zeros((16, D), F32)
        for k in range(NSH):
            ds = ds + _dot_nt(dm_ref[:, k * 768:(k + 1) * 768], w_ref[k])
        dcc_ref[...] = ds * (sg * (1.0 + v * (1.0 - sg)))
    return pl.pallas_call(body, out_shape=jax.ShapeDtypeStruct((16, D), F32),
                          name="mod_bwd", compiler_params=_cp())(cc, dmod, w_mod_g)


def _wmod_grad(cc, dmod_cols):
    def body(cc_ref, dm_ref, dw_ref):
        v = cc_ref[...]
        dw_ref[...] = _dot_tn(v * _sig(v), dm_ref[...])
    return pl.pallas_call(body, out_shape=jax.ShapeDtypeStruct((D, dmod_cols.shape[1]), F32),
                          name="wmod_grad", compiler_params=_cp())(cc, dmod_cols)


def _modulate(x_ref, mod_ref, is_lat):
    sh = jnp.where(is_lat, mod_ref[0:1, 0:D], mod_ref[1:2, 0:D])
    sc = jnp.where(is_lat, mod_ref[0:1, D:2 * D], mod_ref[1:2, D:2 * D])
    return x_ref[...] * (1.0 + sc) + sh, sc


def _gla_mask(d, n):
    row = lax.broadcasted_iota(jnp.int32, (n, n), 0)
    col = lax.broadcasted_iota(jnp.int32, (n, n), 1)
    same = (row // CH) == (col // CH)
    return same & ((row <= col) if d else (row >= col))


def _chunk_cumsum(v, rev):
    n = v.shape[0]
    pos = lax.broadcasted_iota(jnp.int32, v.shape, 0) & (CH - 1)
    s = 1
    while s < CH:
        if rev:
            v = v + jnp.where(pos < CH - s, pltpu.roll(v, n - s, 0), 0.0)
        else:
            v = v + jnp.where(pos >= s, pltpu.roll(v, s, 0), 0.0)
        s *= 2
    return v


def _chunk_rows(c):
    return slice(c * CH, (c + 1) * CH)


def _gla_features(zq, zf, lb, d):
    sq = _sig(zq)
    q = zq * sq * Q_SCALE
    sf = _sig(zf)
    f = lb + (1.0 - lb) * sf
    k = 1.0 - f
    g = _chunk_cumsum(jnp.log(f), d)
    last = 0 if d else CH - 1
    gls = [g[c * CH + last:c * CH + last + 1, :] for c in range(NCK)]
    glb = jnp.concatenate([jnp.broadcast_to(gl, (CH, D)) for gl in gls], axis=0)
    eg, eig, eeg = jnp.exp(g), jnp.exp(-g), jnp.exp(glb - g)
    decs = [jnp.exp(gl) for gl in gls]
    return sq, sf, f, q * eg, k * eig, k * eeg, eg, eig, eeg, decs


def _lower_bound(lbl_ref, d):
    return _sig(lbl_ref[0, d:d + 1, :] - lbl_ref[1, d:d + 1, :])


def _gla_rb(d, nlb, ncb):
    nrb = nlb + ncb
    if d:
        return lambda s: nrb - 1 - s
    return lambda s: jnp.where(s < ncb, nlb + s, s - ncb)


def _gla_fwd(z, lbl, d, nlb, ncb, gather=()):
    m = z.shape[0]
    nrb = nlb + ncb
    rb = _gla_rb(d, nlb, ncb)
    ng = len(gather)

    def body(*refs):
        q_ref, f_ref, v_ref, lbl_ref = refs[:4]
        ag_in = refs[4:4 + ng]
        o_ref, st_ref = refs[4 + ng:6 + ng]
        ag_out = refs[6 + ng:6 + 2 * ng]
        S = refs[6 + 2 * ng]
        ag_sems = refs[7 + 2 * ng:]
        s = pl.program_id(0)

        @pl.when(s == 0)
        def _():
            S[...] = jnp.zeros_like(S)
            if ng:
                local, chip, _ = _ag_copies(ag_in, ag_out, *ag_sems)
                for cp in chip + local:
                    cp.start()

        lb = _lower_bound(lbl_ref, d)
        mb = _gla_mask(d, CH)
        _, _, _, qd, ki, ke, _, _, _, decs = _gla_features(q_ref[...], f_ref[...], lb, d)
        qd, ki, ke, v = qd.astype(BF), ki.astype(BF), ke.astype(BF), v_ref[...].astype(BF)
        order = range(NCK - 1, -1, -1) if d else range(NCK)
        for h in range(H):
            hs = slice(h * DH, (h + 1) * DH)
            intra, upd = {}, {}
            for c in range(NCK):
                rs = _chunk_rows(c)
                a = jnp.where(mb, _dot_nt(qd[rs, hs], ki[rs, hs]), 0.0)
                intra[c] = _dot(a, v[rs, hs])
                upd[c] = _dot_tn(v[rs, hs], ke[rs, hs])
            st = S[h]
            for c in order:
                rs = _chunk_rows(c)
                st_ref[c, h] = st
                o_ref[rs, hs] = intra[c] + _dot_nt(qd[rs, hs], st)
                st = st * decs[c][:, hs] + upd[c]
            S[h] = st

        if ng:
            @pl.when(s == nrb - 1)
            def _():
                _ag_finish(*_ag_copies(ag_in, ag_out, *ag_sems))

    def zspec(cb):
        return pl.BlockSpec((RB, D), lambda s: (rb(s), cb))

    outs = pl.pallas_call(
        body, grid=(nrb,),
        in_specs=[zspec(0), zspec(1 + d), zspec(3), pl.BlockSpec((2, 2, D), lambda s: (0, 0, 0))] + [_ANY] * ng,
        out_specs=[pl.BlockSpec((RB, D), lambda s: (rb(s), 0)),
                   pl.BlockSpec((NCK, H, DH, DH), lambda s: (rb(s), 0, 0, 0))] + [_ANY] * ng,
        out_shape=[jax.ShapeDtypeStruct((m, D), F32), jax.ShapeDtypeStruct((m // CH, H, DH, DH), F32)]
        + [jax.ShapeDtypeStruct((NSH,) + g.shape, g.dtype) for g in gather],
        scratch_shapes=[pltpu.VMEM((H, DH, DH), F32)] + (_ag_sems(ng) if ng else []),
        name=f"gla_fwd{d}", compiler_params=_cp(("arbitrary",)))(z, z, z, lbl, *gather)
    return outs[0], outs[1], list(outs[2:])


def _gla_bwd(z, lbl, do_lat, states, d, nlb, ncb, prev=None):
    m = z.shape[0]
    nrb = nlb + ncb
    fwd_rb = _gla_rb(d, nlb, ncb)
    rb = lambda s: fwd_rb(nrb - 1 - s)
    last = prev is not None

    def body(*refs):
        if last:
            q_ref, f_ref, v_ref, lbl_ref, do_ref, st_ref, pq_ref, pv_ref, o0_ref, o1_ref, o2_ref, sum_ref, dS = refs
        else:
            q_ref, f_ref, v_ref, lbl_ref, do_ref, st_ref, o0_ref, o1_ref, o2_ref, sum_ref, dS = refs
        s = pl.program_id(0)
        is_lat = rb(s) < nlb

        @pl.when(s == 0)
        def _():
            dS[...] = jnp.zeros_like(dS)
            sum_ref[...] = jnp.zeros_like(sum_ref)

        lb = _lower_bound(lbl_ref, d)
        mb = _gla_mask(d, RB)
        zq = q_ref[...]
        sq, sf, f, qd, ki, ke, eg, eig, eeg, decs = _gla_features(zq, f_ref[...], lb, d)
        qdb, kib, keb, vb = qd.astype(BF), ki.astype(BF), ke.astype(BF), v_ref[...].astype(BF)
        dob = jnp.where(is_lat, do_ref[...], 0.0).astype(BF)
        order = range(NCK) if d else range(NCK - 1, -1, -1)
        dqd_h, dki_h, dke_h, dv_h, ddec_h = [], [], [], [], []
        for h in range(H):
            hs = slice(h * DH, (h + 1) * DH)
            a = jnp.where(mb, _dot_nt(qdb[:, hs], kib[:, hs]), 0.0).astype(BF)
            da = jnp.where(mb, _dot_nt(dob[:, hs], vb[:, hs]), 0.0).astype(BF)
            dqd_i = _dot(da, kib[:, hs])
            dki_h.append(_dot_tn(da, qdb[:, hs]))
            dvi = _dot_tn(a, dob[:, hs])
            dqd, inc = {}, {}
            for c in range(NCK):
                rs = _chunk_rows(c)
                dqd[c] = dqd_i[rs, :] + _dot(dob[rs, hs], st_ref[c, h])
                inc[c] = _dot_tn(dob[rs, hs], qdb[rs, hs])
            dst = dS[h]
            dke, dv, ddec = {}, {}, {}
            for c in order:
                rs = _chunk_rows(c)
                dv[c] = dvi[rs, :] + _dot_nt(keb[rs, hs], dst)
                dke[c] = _dot(vb[rs, hs], dst)
                ddec[c] = _colsum(st_ref[c, h] * dst)
                dst = inc[c] + dst * decs[c][:, hs]
            dS[h] = dst
            cat = lambda t: jnp.concatenate([t[c] for c in range(NCK)], axis=0)
            dqd_h.append(cat(dqd))
            dke_h.append(cat(dke))
            dv_h.append(cat(dv))
            ddec_h.append([ddec[c] for c in range(NCK)])
        lanes = lambda parts: jnp.concatenate(parts, axis=1)
        dqd, dki, dke, dv = lanes(dqd_h), lanes(dki_h), lanes(dke_h), lanes(dv_h)
        dq = dqd * eg
        dk = dki * eig + dke * eeg
        dke_ke = dke * ke
        dg = dqd * qd - dki * ki - dke_ke
        dgl = [_colsum(dke_ke[_chunk_rows(c), :]) + lanes([ddec_h[h][c] for h in range(H)]) * decs[c]
               for c in range(NCK)]
        dglb = jnp.concatenate([jnp.broadcast_to(t, (CH, D)) for t in dgl], axis=0)
        df = (_chunk_cumsum(dg, 1 - d) + dglb) / f - dk
        dzf = df * (1.0 - lb) * (sf * (1.0 - sf))
        sum_ref[0:1, :] += _colsum(dzf)
        sum_ref[1:2, :] += _colsum(df * (1.0 - sf))
        if last:
            dz0 = (dq + pq_ref[...]) * (Q_SCALE * (sq * (1.0 + zq * (1.0 - sq))))
            dz3 = dv + pv_ref[...]
            sum_ref[2:3, :] += _colsum(dz0)
            sum_ref[3:4, :] += _colsum(dz3)
            o0_ref[...] = dz0.astype(BF)
            o1_ref[...] = dz3.astype(BF)
        else:
            o0_ref[...] = dq
            o1_ref[...] = dv
        o2_ref[...] = dzf.astype(BF)

        @pl.when(s == nrb - 1)
        def _():
            sum_ref[1:2, :] = sum_ref[1:2, :] * (lb * (1.0 - lb))

    def zspec(cb):
        return pl.BlockSpec((RB, D), lambda s: (rb(s), cb))

    rowspec = pl.BlockSpec((RB, D), lambda s: (rb(s), 0))
    in_specs = [zspec(0), zspec(1 + d), zspec(3), pl.BlockSpec((2, 2, D), lambda s: (0, 0, 0)),
                pl.BlockSpec((RB, D), lambda s: (jnp.minimum(rb(s), nlb - 1), 0)),
                pl.BlockSpec((NCK, H, DH, DH), lambda s: (rb(s), 0, 0, 0))]
    args = [z, z, z, lbl, do_lat, states]
    if last:
        in_specs += [rowspec, rowspec]
        args += list(prev)
    dt01 = BF if last else F32
    return pl.pallas_call(
        body, grid=(nrb,), in_specs=in_specs,
        out_specs=(rowspec, rowspec, rowspec, pl.BlockSpec((8, D), lambda s: (0, 0))),
        out_shape=(jax.ShapeDtypeStruct((m, D), dt01), jax.ShapeDtypeStruct((m, D), dt01),
                   jax.ShapeDtypeStruct((m, D), BF), jax.ShapeDtypeStruct((8, D), F32)),
        scratch_shapes=[pltpu.VMEM((H, DH, DH), F32)],
        name=f"gla_bwd{d}", compiler_params=_cp(("arbitrary",)))(*args)


def _shift_rows(v, k, r):
    row = lax.broadcasted_iota(jnp.int32, v.shape, 0)
    rolled = pltpu.roll(v, k % r, 0)
    return jnp.where((row >= k) & (row < r + k), rolled, 0.0)


def _conv_fwd(xb, cw_ref, cb_ref, r):
    return (cb_ref[...] + _shift_rows(xb, 1, r) * cw_ref[0:1, :] + xb * cw_ref[1:2, :]
            + _shift_rows(xb, -1, r) * cw_ref[2:3, :] + _shift_rows(xb, -2, r) * cw_ref[3:4, :])


def _window(ref, lo, n, rows):
    parts = []
    if lo < 0:
        parts.append(jnp.zeros((-lo,) + tuple(ref.shape[1:]), F32))
    parts.append(ref[max(lo, 0):min(lo + n, rows)])
    if lo + n > rows:
        parts.append(jnp.zeros((lo + n - rows,) + tuple(ref.shape[1:]), F32))
    return parts[0] if len(parts) == 1 else jnp.concatenate(parts, axis=0)


def _conv_cols(x_ref, cw_ref, cb_ref, r0, n, rows):
    w = _window(x_ref, r0 - 1, n + 3, rows)
    return (cb_ref[...] + w[0:n] * cw_ref[0:1, :] + w[1:n + 1] * cw_ref[1:2, :] + w[2:n + 2] * cw_ref[2:3, :]
            + w[3:n + 3] * cw_ref[3:4, :])


def _softplus_neg(lam):
    y = jnp.exp(-jnp.abs(lam))
    u = 1.0 + y
    tiny = u == 1.0
    l1p = jnp.where(tiny, y, jnp.log(u) * (y / jnp.where(tiny, 1.0, u - 1.0)))
    return jnp.maximum(-lam, 0.0) + l1p


def _gates(xc, wr_ref, br, wi_ref, bi, sp):
    xcb = xc.astype(BF)
    rs, is_ = [], []
    for g in range(H):
        gs = slice(g * DH, (g + 1) * DH)
        rs.append(jnp.dot(xcb[:, gs], wr_ref[g].astype(BF), preferred_element_type=F32))
        is_.append(jnp.dot(xcb[:, gs], wi_ref[g].astype(BF), preferred_element_type=F32))
    r = _sig(jnp.concatenate(rs, axis=1) + br)
    i = _sig(jnp.concatenate(is_, axis=1) + bi)
    log_a = (-RG_C * r) * sp
    a = jnp.exp(log_a)
    t = jnp.tanh(log_a)
    om = (-2.0 * t) / (1.0 - t)
    return r, i, a, om


def _scan_rows(d, nrows, a_s, b_s, h_s, h0):
    nsl = nrows // 8

    def slab(j, h):
        jj = (nsl - 1 - j) if d else j
        r0 = pl.multiple_of(jj * 8, 8)
        for t in (range(7, -1, -1) if d else range(8)):
            h = a_s[pl.ds(r0 + t, 1), :] * h + b_s[pl.ds(r0 + t, 1), :]
            h_s[pl.ds(r0 + t, 1), :] = h
        return h

    return lax.fori_loop(0, nsl, slab, h0)


def _col_of(d, ncols):
    if d:
        return lambda s: ncols - jnp.maximum(s, 1)
    return lambda s: jnp.maximum(s, 1) - 1


def _rglru_fwd(z, cw, cb, wr, br, wi, bi, lam, d, t_lat, t_ctx):
    m = z.shape[0]
    rows = t_lat // GRID_W
    z3 = z.reshape(m // GRID_W, GRID_W, IN_COLS)
    nblk = GRID_W // CB
    blk = _col_of(d, nblk)
    cblk = t_lat // t_ctx
    rc = min(RCH, rows)

    def body(zc_ref, zx_ref, cw_ref, cb_ref, wr_ref, br_ref, wi_ref, bi_ref, lam_ref,
             hx_ref, hpx_ref, ax_ref, hpc_ref, a_s, b_s, h_s, hcar, a3, b3, cin_s):
        s = pl.program_id(0)
        sp = _softplus_neg(lam_ref[d:d + 1, :])
        br_ = br_ref[d:d + 1, :]
        bi_ = bi_ref[d:d + 1, :]

        @pl.when(s == 0)
        def _():
            xc = _conv_fwd(zc_ref[...], cw_ref, cb_ref, t_ctx)
            _, i, a, om = _gates(xc, wr_ref, br_, wi_ref, bi_, sp)
            a_s[...] = a
            b_s[...] = jnp.sqrt(om) * (i * xc)
            h0 = jnp.zeros((1, D), F32)
            hcar[0:1, :] = _scan_rows(d, t_ctx, a_s, b_s, h_s, h0)
            hs = h_s[...]
            row = lax.broadcasted_iota(jnp.int32, (t_ctx, D), 0)
            if d:
                hpc_ref[...] = jnp.where(row == t_ctx - 1, h0, pltpu.roll(hs, t_ctx - 1, 0))
            else:
                hpc_ref[...] = jnp.where(row == 0, h0, pltpu.roll(hs, 1, 0))

        @pl.when(s > 0)
        def _():
            for r0 in range(0, rows, rc):
                xc = _conv_cols(zx_ref, cw_ref, cb_ref, r0, rc, rows).reshape(rc * CB, D)
                _, i, a, om = _gates(xc, wr_ref, br_, wi_ref, bi_, sp)
                a3[r0:r0 + rc] = a.reshape(rc, CB, D)
                ax_ref[r0:r0 + rc] = a.reshape(rc, CB, D)
                b3[r0:r0 + rc] = (jnp.sqrt(om) * (i * xc)).reshape(rc, CB, D)

            def local(t, carry):
                hl, p = carry
                r = (rows - 1 - t) if d else t
                a = a3[r]
                hl = a * hl + b3[r]
                p = a * p
                b3[r] = hl
                a3[r] = p
                return hl, p

            hl, p = lax.fori_loop(0, rows, local, (jnp.zeros((CB, D), F32), jnp.ones((CB, D), F32)))
            cin = hcar[0:1, :]
            for j in (range(CB - 1, -1, -1) if d else range(CB)):
                cin_s[j:j + 1, :] = cin
                cin = hl[j:j + 1, :] + p[j:j + 1, :] * cin
            hcar[0:1, :] = cin
            c_in = cin_s[...]

            def fix(t, prev):
                r = (rows - 1 - t) if d else t
                h = b3[r] + a3[r] * c_in
                hx_ref[r] = h
                hpx_ref[r] = prev
                return h

            lax.fori_loop(0, rows, fix, c_in)

    full = lambda shp: pl.BlockSpec(shp, lambda s: (0,) * len(shp))
    colspec = pl.BlockSpec((rows, CB, D), lambda s: (0, blk(s), 0))
    outs = pl.pallas_call(
        body, grid=(nblk + 1,),
        in_specs=[pl.BlockSpec((t_ctx, D), lambda s: (cblk, 5)),
                  pl.BlockSpec((rows, CB, D), lambda s: (0, blk(s), 5)),
                  full((4, D)), full((1, D)),
                  pl.BlockSpec((None, H, DH, DH), lambda s: (d, 0, 0, 0)), full((2, D)),
                  pl.BlockSpec((None, H, DH, DH), lambda s: (d, 0, 0, 0)), full((2, D)), full((2, D))],
        out_specs=(colspec, colspec, colspec, full((t_ctx, D))),
        out_shape=(jax.ShapeDtypeStruct((rows, GRID_W, D), F32),) * 3 + (jax.ShapeDtypeStruct((t_ctx, D), F32),),
        scratch_shapes=[pltpu.VMEM((t_ctx, D), F32), pltpu.VMEM((t_ctx, D), F32), pltpu.VMEM((t_ctx, D), F32),
                        pltpu.VMEM((8, D), F32), pltpu.VMEM((rows, CB, D), F32), pltpu.VMEM((rows, CB, D), F32),
                        pltpu.VMEM((CB, D), F32)],
        name=f"rglru_fwd{d}", compiler_params=_cp(("arbitrary",)))(z, z3, cw, cb, wr, br, wi, bi, lam)
    return outs[0].reshape(t_lat, D), outs[1].reshape(t_lat, D), outs[2].reshape(t_lat, D), outs[3]


def _rglru_bwd(z, cw, cb, wr, br, wi, bi, lam, dh_lat, hp_lat, a_lat, hp_ctx, d, t_lat, t_ctx, prev=None):
    m = z.shape[0]
    rows = t_lat // GRID_W
    z3 = z.reshape(m // GRID_W, GRID_W, IN_COLS)
    nblk = GRID_W // CB
    fblk = _col_of(d, nblk)
    blk = lambda s: fblk(nblk - jnp.minimum(s, nblk - 1))
    cblk = t_lat // t_ctx
    rc = min(RCH, rows)
    last = prev is not None
    view3 = lambda v: v.reshape(rows, GRID_W, D)

    def body(*refs):
        (zc_ref, zx_ref, cw_ref, cb_ref, wr_ref, br_ref, wi_ref, bi_ref, lam_ref, dh_ref, hpx_ref, ax_ref,
         hpc_ref) = refs[:13]
        k = 13
        if last:
            pdx_ref, pdc_ref = refs[13:15]
            k = 15
        ox_ref, oc_ref, dwr_ref, dwi_ref, sum_ref, a_s, b_s, gcar, a3, b3, cin_s = refs[k:]
        s = pl.program_id(0)
        lam_d = lam_ref[d:d + 1, :]
        sp = _softplus_neg(lam_d)
        br_ = br_ref[d:d + 1, :]
        bi_ = bi_ref[d:d + 1, :]
        flat = lambda v: v.reshape(-1, D)

        @pl.when(s == 0)
        def _():
            gcar[...] = jnp.zeros_like(gcar)
            dwr_ref[...] = jnp.zeros_like(dwr_ref)
            dwi_ref[...] = jnp.zeros_like(dwi_ref)
            sum_ref[...] = jnp.zeros_like(sum_ref)


        def conv_sums(dxc, dxb, xm1, x0, xp1, xp2):
            sum_ref[3:4, :] += _colsum(flat(dxc))
            sum_ref[4:5, :] += _colsum(flat(dxb))
            sum_ref[8:9, :] += _colsum(flat(dxc * xm1))
            sum_ref[9:10, :] += _colsum(flat(dxc * x0))
            sum_ref[10:11, :] += _colsum(flat(dxc * xp1))
            sum_ref[11:12, :] += _colsum(flat(dxc * xp2))

        def gate_grads(g, hp, xc, rr, i, a, om):
            mult = jnp.sqrt(om)
            da = g * hp
            ixc = i * xc
            dmult = g * ixc
            dixc = g * mult
            di = dixc * xc
            dxc = dixc * i
            dlog_a = da * a - dmult * ((1.0 - om) / mult)
            dr = dlog_a * (-RG_C * sp)
            sum_ref[2:3, :] += _colsum(dlog_a * rr)
            drp = dr * rr * (1.0 - rr)
            dip = di * i * (1.0 - i)
            sum_ref[0:1, :] += _colsum(drp)
            sum_ref[1:2, :] += _colsum(dip)
            xcb = xc.astype(BF)
            drb = drp.astype(BF)
            dib = dip.astype(BF)
            parts = []
            for gi in range(H):
                gs = slice(gi * DH, (gi + 1) * DH)
                parts.append(_dot_nt(drb[:, gs], wr_ref[gi]) + _dot_nt(dib[:, gs], wi_ref[gi]))
                dwr_ref[gi] += _dot_tn(xcb[:, gs], drb[:, gs])
                dwi_ref[gi] += _dot_tn(xcb[:, gs], dib[:, gs])
            return dxc + jnp.concatenate(parts, axis=1)

        @pl.when(s < nblk)
        def _():
            def local(t, carry):
                c, q = carry
                r = t if d else (rows - 1 - t)
                a = ax_ref[r]
                c = a * (c + dh_ref[r])
                q = a * q
                b3[r] = c
                a3[r] = q
                return c, q

            c, q = lax.fori_loop(0, rows, local, (jnp.zeros((CB, D), F32), jnp.ones((CB, D), F32)))
            cin = gcar[0:1, :]
            for j in (range(CB) if d else range(CB - 1, -1, -1)):
                cin_s[j:j + 1, :] = cin
                cin = c[j:j + 1, :] + q[j:j + 1, :] * cin
            gcar[0:1, :] = cin
            c_in = cin_s[...]
            for r0 in (range(rows - rc, -1, -rc) if d else range(0, rows, rc)):
                if d:
                    lo = max(r0 - 1, 0)
                    cn = b3[lo:r0 + rc - 1] + a3[lo:r0 + rc - 1] * c_in
                    if r0 == 0:
                        cn = jnp.concatenate([c_in[None], cn], axis=0)
                else:
                    hi = min(r0 + rc + 1, rows)
                    cn = b3[r0 + 1:hi] + a3[r0 + 1:hi] * c_in
                    if hi == rows:
                        cn = jnp.concatenate([cn, c_in[None]], axis=0)
                g = flat(dh_ref[r0:r0 + rc] + cn)
                xc = flat(_conv_cols(zx_ref, cw_ref, cb_ref, r0, rc, rows))
                rr, i, a, om = _gates(xc, wr_ref, br_, wi_ref, bi_, sp)
                b3[r0:r0 + rc] = gate_grads(g, flat(hpx_ref[r0:r0 + rc]), xc, rr, i, a, om).reshape(rc, CB, D)
            if last:
                for r0 in range(0, rows, rc):
                    b3[r0:r0 + rc] = b3[r0:r0 + rc] + pdx_ref[r0:r0 + rc]
                for r0 in range(0, rows, rc):
                    w = _window(b3, r0 - 2, rc + 3, rows)
                    xw = _window(zx_ref, r0 - 1, rc + 3, rows)
                    dxc = w[2:rc + 2]
                    dxb = (w[3:rc + 3] * cw_ref[0:1, :] + dxc * cw_ref[1:2, :] + w[1:rc + 1] * cw_ref[2:3, :]
                           + w[0:rc] * cw_ref[3:4, :])
                    conv_sums(dxc, dxb, xw[0:rc], xw[1:rc + 1], xw[2:rc + 2], xw[3:rc + 3])
                    ox_ref[r0:r0 + rc] = dxb
            else:
                for r0 in range(0, rows, rc):
                    ox_ref[r0:r0 + rc] = b3[r0:r0 + rc]

        @pl.when(s == nblk)
        def _():
            r = t_ctx
            xb = zc_ref[...]
            xc = _conv_fwd(xb, cw_ref, cb_ref, r)
            rr, i, a, om = _gates(xc, wr_ref, br_, wi_ref, bi_, sp)
            a_s[...] = a
            b_s[...] = jnp.zeros((r, D), F32)
            c0 = gcar[0:1, :]
            _scan_rows(1 - d, r, a_s, b_s, b_s, c0)
            cs = b_s[...]
            row = lax.broadcasted_iota(jnp.int32, (r, D), 0)
            if d:
                g = jnp.where(row == 0, c0, pltpu.roll(cs, 1, 0))
            else:
                g = jnp.where(row == r - 1, c0, pltpu.roll(cs, r - 1, 0))
            dxc = gate_grads(g, hpc_ref[...], xc, rr, i, a, om)
            if last:
                dxc = dxc + pdc_ref[...]
                dxb = (_shift_rows(dxc, -1, r) * cw_ref[0:1, :] + dxc * cw_ref[1:2, :]
                       + _shift_rows(dxc, 1, r) * cw_ref[2:3, :] + _shift_rows(dxc, 2, r) * cw_ref[3:4, :])
                conv_sums(dxc, dxb, _shift_rows(xb, 1, r), xb, _shift_rows(xb, -1, r), _shift_rows(xb, -2, r))
                oc_ref[...] = dxb
            else:
                oc_ref[...] = dxc
            sum_ref[2:3, :] = sum_ref[2:3, :] * (RG_C * _sig(-lam_d))

    full = lambda shp: pl.BlockSpec(shp, lambda s: (0,) * len(shp))
    once = lambda shp: pl.BlockSpec(shp, lambda s: (0,) * len(shp), pipeline_mode=pl.Buffered(1))
    colspec = pl.BlockSpec((rows, CB, D), lambda s: (0, blk(s), 0))
    colonce = pl.BlockSpec((rows, CB, D), lambda s: (0, blk(s), 0), pipeline_mode=pl.Buffered(1))
    in_specs = [pl.BlockSpec((t_ctx, D), lambda s: (cblk, 5), pipeline_mode=pl.Buffered(1)),
                pl.BlockSpec((rows, CB, D), lambda s: (0, blk(s), 5), pipeline_mode=pl.Buffered(1)),
                full((4, D)), full((1, D)),
                pl.BlockSpec((None, H, DH, DH), lambda s: (d, 0, 0, 0)), full((2, D)),
                pl.BlockSpec((None, H, DH, DH), lambda s: (d, 0, 0, 0)), full((2, D)), full((2, D)),
                colonce, colonce, colonce, once((t_ctx, D))]
    args = [z, z3, cw, cb, wr, br, wi, bi, lam, view3(dh_lat), view3(hp_lat), view3(a_lat), hp_ctx]
    if last:
        in_specs += [colonce, once((t_ctx, D))]
        args += [view3(prev[0]), prev[1]]
    outs = pl.pallas_call(
        body, grid=(nblk + 1,), in_specs=in_specs,
        out_specs=(colspec, full((t_ctx, D)), full((H, DH, DH)), full((H, DH, DH)), full((16, D))),
        out_shape=(jax.ShapeDtypeStruct((rows, GRID_W, D), F32), jax.ShapeDtypeStruct((t_ctx, D), F32),
                   jax.ShapeDtypeStruct((H, DH, DH), F32), jax.ShapeDtypeStruct((H, DH, DH), F32),
                   jax.ShapeDtypeStruct((16, D), F32)),
        scratch_shapes=[pltpu.VMEM((t_ctx, D), F32), pltpu.VMEM((t_ctx, D), F32), pltpu.VMEM((8, D), F32),
                        pltpu.VMEM((rows, CB, D), F32), pltpu.VMEM((rows, CB, D), F32), pltpu.VMEM((CB, D), F32)],
        name=f"rglru_bwd{d}", compiler_params=_cp(("arbitrary",)))(*args)
    return (outs[0].reshape(t_lat, D), outs[1]) + tuple(outs[2:])


def _merge(o_f, o_b, h_f, h_b, z, x_all, tgt, mod, norm_g, ln_g, ln_b, p_a, p_b, w_out, t_lat):
    tm = 256
    nt = t_lat // tm

    def body(of_ref, ob_ref, hf_ref, hb_ref, z4_ref, z6_ref, z7_ref, z8_ref, x_ref, t_ref, mod_ref, ng_ref,
             lg_ref, lb_ref, pa_ref, pb_ref, wo_ref,
             do_ref, dh_ref, dz4_ref, dz6_ref, dz7_ref, dz8_ref, gx_ref,
             y_ref, dout_ref, oa_ref, dpa_ref, obv_ref, dpb_ref, acc_ref):
        i = pl.program_id(0)
        lat = i < nt
        latf = lat.astype(F32)

        @pl.when(i == 0)
        def _():
            acc_ref[...] = jnp.zeros_like(acc_ref)

        def per_head(v):
            return jnp.concatenate(
                [jnp.broadcast_to(jnp.mean(v[:, h * DH:(h + 1) * DH], axis=-1, keepdims=True), (tm, DH))
                 for h in range(H)], axis=1)

        gt = mod_ref[0:1, 2 * D:3 * D]
        gfull = jnp.concatenate([ng_ref[...]] * H, axis=1)
        o = of_ref[...] + ob_ref[...]
        rinv = lax.rsqrt(per_head(o * o) + RMS_EPS)
        n = o * rinv
        na = n * gfull
        z4 = z4_ref[...]
        s4 = _sig(z4)
        silu4 = z4 * s4
        oa = na * silu4
        z6 = z6_ref[...]
        s6 = _sig(z6)
        silu6 = z6 * s6
        hsum = hf_ref[...] + hb_ref[...]
        obv = hsum * silu6
        pa = _dot(oa, pa_ref[...])
        pb = _dot(obv, pb_ref[...])
        s7 = _sig(z7_ref[...])
        s8 = _sig(z8_ref[...])
        y = s7 * pa + s8 * pb
        out = _dot(y, wo_ref[...])
        pre = ALPHA * x_ref[...] + gt * out
        mu = jnp.mean(pre, axis=-1, keepdims=True)
        xc = pre - mu
        rstd = lax.rsqrt(jnp.mean(xc * xc, axis=-1, keepdims=True) + LN_EPS)
        xhat = xc * rstd
        lg = lg_ref[...]
        diff = xhat * lg + lb_ref[...] - t_ref[...]
        acc_ref[8:9, :] += _colsum(diff * diff) * (0.5 / D * latf)
        dxn = diff * (1.0 / D)
        acc_ref[1:2, :] += _colsum(dxn * xhat) * latf
        acc_ref[2:3, :] += _colsum(dxn) * latf
        dxhat = dxn * lg
        dpre = rstd * (dxhat - jnp.mean(dxhat, axis=-1, keepdims=True)
                       - xhat * jnp.mean(dxhat * xhat, axis=-1, keepdims=True))
        gx_ref[...] = ALPHA * dpre
        acc_ref[0:1, :] += _colsum(dpre * out) * latf
        dout = dpre * gt
        dy = _dot_nt(dout, wo_ref[...])
        dpa = dy * s7
        dpb = dy * s8
        dz7 = dy * pa * (s7 * (1.0 - s7))
        dz8 = dy * pb * (s8 * (1.0 - s8))
        doa = _dot_nt(dpa, pa_ref[...])
        dob = _dot_nt(dpb, pb_ref[...])
        dh_ref[...] = dob * silu6
        dz6 = dob * hsum * (s6 * (1.0 + z6 * (1.0 - s6)))
        dna = doa * silu4
        dz4 = doa * na * (s4 * (1.0 + z4 * (1.0 - s4)))
        dng = _colsum(dna * n)
        acc_ref[7:8, 0:DH] += sum(dng[:, h * DH:(h + 1) * DH] for h in range(H)) * latf
        dn = dna * gfull
        do_ref[...] = rinv * (dn - n * per_head(dn * n))
        acc_ref[3:4, :] += _colsum(dz4) * latf
        acc_ref[4:5, :] += _colsum(dz6) * latf
        acc_ref[5:6, :] += _colsum(dz7) * latf
        acc_ref[6:7, :] += _colsum(dz8) * latf
        dz4_ref[...] = (dz4 * latf).astype(BF)
        dz6_ref[...] = (dz6 * latf).astype(BF)
        dz7_ref[...] = (dz7 * latf).astype(BF)
        dz8_ref[...] = (dz8 * latf).astype(BF)
        y_ref[...] = y.astype(BF)
        dout_ref[...] = dout.astype(BF)
        oa_ref[...] = oa.astype(BF)
        dpa_ref[...] = dpa.astype(BF)
        obv_ref[...] = obv.astype(BF)
        dpb_ref[...] = dpb.astype(BF)

        @pl.when(i == nt - 1)
        def _():
            acc_ref[9:10, :] = jnp.broadcast_to(jnp.sum(acc_ref[8:9, :], axis=-1, keepdims=True), (1, D))

    m = x_all.shape[0]
    lrow = lambda i: jnp.minimum(i, nt - 1)
    row = pl.BlockSpec((tm, D), lambda i: (lrow(i), 0))
    allrow = pl.BlockSpec((tm, D), lambda i: (i, 0))
    zs = lambda cb: pl.BlockSpec((tm, D), lambda i: (lrow(i), cb))
    full = lambda shp: pl.BlockSpec(shp, lambda i: (0,) * len(shp))
    wfull = pl.BlockSpec((D, D), lambda i: (0, 0), pipeline_mode=pl.Buffered(1))
    f32o = jax.ShapeDtypeStruct((t_lat, D), F32)
    bfo = jax.ShapeDtypeStruct((t_lat, D), BF)
    bfall = jax.ShapeDtypeStruct((m, D), BF)
    return pl.pallas_call(
        body, grid=(m // tm,),
        in_specs=[row, row, row, row, zs(4), zs(6), zs(7), zs(8), row, row, full((16, 3 * D)), full((1, DH)),
                  full((1, D)), full((1, D)), wfull, wfull, wfull],
        out_specs=(row, row) + (allrow,) * 4 + (row,) * 7 + (full((16, D)),),
        out_shape=(f32o, f32o, bfall, bfall, bfall, bfall, f32o, bfo, bfo, bfo, bfo, bfo, bfo,
                   jax.ShapeDtypeStruct((16, D), F32)),
        name="merge", compiler_params=_cp(("arbitrary",), VMEM_LIMIT_MERGE))(
            o_f, o_b, h_f, h_b, z, z, z, z, x_all, tgt, mod, norm_g, ln_g, ln_b, p_a, p_b, w_out)


def _wgrad(a, b, name):
    tm = 1024

    def body(a_ref, b_ref, o_ref):
        @pl.when(pl.program_id(0) == 0)
        def _():
            o_ref[...] = jnp.zeros_like(o_ref)
        o_ref[...] += _dot_tn(a_ref[...], b_ref[...])

    row = pl.BlockSpec((tm, D), lambda i: (i, 0))
    return pl.pallas_call(body, grid=(a.shape[0] // tm,), in_specs=[row, row],
                          out_specs=pl.BlockSpec((D, D), lambda i: (0, 0)),
                          out_shape=jax.ShapeDtypeStruct((D, D), F32), name=name,
                          compiler_params=_cp(("arbitrary",)))(a, b)


def _wgrad_in(x_all, mod, dz, n_lat_tiles):
    m = x_all.shape[0]
    assert m % 128 == 0
    tm = m // 8
    n_lat = n_lat_tiles * 256

    def body(x_ref, mod_ref, dz_ref, o_ref):
        i = pl.program_id(1)

        @pl.when(i == 0)
        def _():
            o_ref[...] = jnp.zeros_like(o_ref)
        is_lat = (i * tm + lax.broadcasted_iota(jnp.int32, (tm, 1), 0)) < n_lat
        sh = jnp.where(is_lat, mod_ref[0:1, 0:D], mod_ref[1:2, 0:D])
        sc = jnp.where(is_lat, mod_ref[0:1, D:2 * D], mod_ref[1:2, D:2 * D])
        o_ref[0] += _dot_tn(x_ref[...] * (1.0 + sc) + sh, dz_ref[...])

    return pl.pallas_call(
        body, grid=(NSH, m // tm),
        in_specs=[pl.BlockSpec((tm, D), lambda n, i: (i, 0)),
                  pl.BlockSpec((16, 3 * D), lambda n, i: (0, 0)),
                  pl.BlockSpec((tm, SHC), lambda n, i: (i, n))],
        out_specs=pl.BlockSpec((1, D, SHC), lambda n, i: (n, 0, 0)),
        out_shape=jax.ShapeDtypeStruct((NSH, D, SHC), F32), name="wgrad_in",
        compiler_params=_cp(("arbitrary", "arbitrary")))(x_all, mod, dz)


def _du(dz, w_in_g, x_all, mod, gxres, n_lat_tiles, sums=()):
    m = x_all.shape[0]
    tm = 256
    nt = m // tm
    nct = nt - n_lat_tiles
    ns = len(sums)
    rblk = lambda i: jnp.where(i < nct, n_lat_tiles + i, i - nct)
    lblk = lambda i: jnp.maximum(i - nct, 0)

    def body(*refs):
        dz_ref, w_ref, x_ref, mod_ref, gr_ref = refs[:5]
        sum_refs = refs[5:5 + ns]
        gx_ref, dm_ref = refs[5 + ns:7 + ns]
        got_refs = refs[7 + ns:7 + 2 * ns]
        sems = refs[7 + 2 * ns:]
        i = pl.program_id(0)
        is_lat = i >= nct

        @pl.when(i == 0)
        def _():
            dm_ref[...] = jnp.zeros_like(dm_ref)
            if ns:
                for cp in _rs_chip_copies(sum_refs, got_refs, *sems):
                    cp.start()

        du = _dot_nt(dz_ref[:, 0:SHC], w_ref[0])
        for n in range(1, NSH):
            du = du + _dot_nt(dz_ref[:, n * SHC:(n + 1) * SHC], w_ref[n])
        sc = jnp.where(is_lat, mod_ref[0:1, D:2 * D], mod_ref[1:2, D:2 * D])
        dsh = _colsum(du)
        dsc = _colsum(du * x_ref[...])

        @pl.when(is_lat)
        def _():
            gx_ref[...] = du * (1.0 + sc) + gr_ref[...]
            dm_ref[0:1, 0:D] += dsh
            dm_ref[0:1, D:2 * D] += dsc

        @pl.when(jnp.logical_not(is_lat))
        def _():
            dm_ref[1:2, 0:D] += dsh
            dm_ref[1:2, D:2 * D] += dsc

        if ns:
            @pl.when(i == nt - 1)
            def _():
                for cp in _rs_chip_copies(sum_refs, got_refs, *sems):
                    cp.wait()

    outs = pl.pallas_call(
        body, grid=(nt,),
        in_specs=[pl.BlockSpec((tm, IN_COLS), lambda i: (rblk(i), 0)),
                  pl.BlockSpec((NSH, D, SHC), lambda i: (0, 0, 0), pipeline_mode=pl.Buffered(1)),
                  pl.BlockSpec((tm, D), lambda i: (rblk(i), 0)),
                  pl.BlockSpec((16, 3 * D), lambda i: (0, 0)),
                  pl.BlockSpec((tm, D), lambda i: (lblk(i), 0))] + [_ANY] * ns,
        out_specs=[pl.BlockSpec((tm, D), lambda i: (lblk(i), 0)),
                   pl.BlockSpec((8, 2 * D), lambda i: (0, 0))] + [_ANY] * ns,
        out_shape=[jax.ShapeDtypeStruct((n_lat_tiles * tm, D), F32), jax.ShapeDtypeStruct((8, 2 * D), F32)]
        + [jax.ShapeDtypeStruct((3,) + g.shape[1:], g.dtype) for g in sums],
        scratch_shapes=[pltpu.SemaphoreType.DMA((3 * ns,)), pltpu.SemaphoreType.DMA((3 * ns,))] if ns else [],
        name="du", compiler_params=_cp(("arbitrary",)))(dz, w_in_g, x_all, mod, gxres, *sums)
    return outs[0], outs[1], list(outs[2:])


def _row_tile(rows, cols):
    t = 8
    while t * 2 * cols * 4 <= (1 << 20) and rows % (t * 2) == 0:
        t *= 2
    return t


def _adamw_update(w_ref, g_ref, m_ref, v_ref, d_ref, nm_ref, nv_ref):
    gg = g_ref[...]
    m2 = ADAM_B1 * m_ref[...] + (1.0 - ADAM_B1) * gg
    v2 = ADAM_B2 * v_ref[...] + (1.0 - ADAM_B2) * (gg * gg)
    m_hat = m2 / (1.0 - ADAM_B1 ** ADAM_STEP)
    v_hat = v2 / (1.0 - ADAM_B2 ** ADAM_STEP)
    d_ref[...] = -ADAM_LR * (m_hat / (jnp.sqrt(v_hat) + ADAM_EPS) + ADAM_WD * w_ref[...])
    nm_ref[...] = m2
    nv_ref[...] = v2


def _adamw_many(ws, gs, ms, vs):
    n = len(ws)

    def body(*refs):
        for j in range(n):
            _adamw_update(*refs[4 * j:4 * j + 4], *refs[4 * n + 3 * j:4 * n + 3 * j + 3])

    args = [a for quad in zip(ws, gs, ms, vs) for a in quad]
    outs = pl.pallas_call(body, out_shape=[jax.ShapeDtypeStruct(w.shape, F32) for w in ws for _ in range(3)],
                          name="adamw_small", compiler_params=_cp())(*args)
    return outs[0::3], outs[1::3], outs[2::3]


def _adamw(w, g, m, v, name):
    rows, cols = w.shape
    tr = _row_tile(rows, cols)

    def body(*refs):
        _adamw_update(*refs)

    spec = pl.BlockSpec((tr, cols), lambda i: (i, 0))
    o = jax.ShapeDtypeStruct((rows, cols), F32)
    return pl.pallas_call(body, grid=(rows // tr,), in_specs=[spec] * 4, out_specs=(spec,) * 3,
                          out_shape=(o, o, o), name=name, compiler_params=_cp(("arbitrary",)))(w, g, m, v)


_ANY = pl.BlockSpec(memory_space=pl.ANY)


def _place():
    return lax.axis_index("x"), lax.axis_index("y"), lax.axis_index("c")


def _ag_copies(ins, outs, send, recv, fsend, frecv, lsem):
    x, y, c = _place()
    me = 2 * x + y
    chips = ((1 - x, y), (x, 1 - y), (1 - x, 1 - y))
    local, chip, hand = [], [], []
    for j in range(len(ins)):
        hr = ins[j].shape[0] // 2
        half = pl.ds(pl.multiple_of(c * hr, 8), hr)
        local.append(pltpu.make_async_copy(ins[j], outs[j].at[me], lsem.at[j]))
        for k, (px, py) in enumerate(chips):
            chip.append(pltpu.make_async_remote_copy(
                src_ref=ins[j].at[half, :], dst_ref=outs[j].at[me, half, :], send_sem=send.at[3 * j + k],
                recv_sem=recv.at[3 * j + k], device_id=(px, py, c), device_id_type=MESH))
            got = outs[j].at[2 * px + py, half, :]
            hand.append(pltpu.make_async_remote_copy(
                src_ref=got, dst_ref=got, send_sem=fsend.at[3 * j + k], recv_sem=frecv.at[3 * j + k],
                device_id=(x, y, 1 - c), device_id_type=MESH))
    return local, chip, hand


def _ag_sems(n):
    return [pltpu.SemaphoreType.DMA((3 * n,))] * 4 + [pltpu.SemaphoreType.DMA((n,))]


def _ag_finish(local, chip, hand, done=0):
    for k in range(done, len(chip)):
        chip[k].wait_recv()
        hand[k].start()
    for cp in chip:
        cp.wait_send()
    for k in range(done):
        hand[k].wait_send()
    for k in range(done, len(chip)):
        hand[k].wait_send()
        hand[k].wait_recv()
    for cp in local:
        cp.wait()


def _mod_tp(c8, c_ctx, w_mod_sh, b_mod_sh):
    mc = w_mod_sh.shape[1]

    def body(c8_ref, cctx_ref, w_ref, b_ref, mod_ref, cc_ref, cc_s, part_s, send1, recv1, send3, recv3):
        x, y, c = _place()
        me = 4 * x + 2 * y + c
        ms = 2 * x + y
        copies = []
        for k in range(1, 8):
            peer = (x ^ ((k >> 2) & 1), y ^ ((k >> 1) & 1), c ^ (k & 1))
            cp = pltpu.make_async_remote_copy(src_ref=c8_ref, dst_ref=cc_s.at[me], send_sem=send1.at[k],
                                              recv_sem=recv1.at[k], device_id=peer, device_id_type=MESH)
            cp.start()
            copies.append(cp)
        cc_s[me] = c8_ref[...]
        for cp in copies:
            cp.wait()
        cc_ref[...] = jnp.zeros_like(cc_ref)
        for j in range(8):
            cc_ref[j:j + 1, :] = cc_s[j, 0:1, :]
        cc_ref[8:9, :] = cctx_ref[...]
        v = cc_ref[...]
        part_s[ms] = _dot(v * _sig(v), w_ref[...]) + b_ref[...]
        copies = []
        for k in range(1, 4):
            peer = (x ^ ((k >> 1) & 1), y ^ (k & 1), c)
            cp = pltpu.make_async_remote_copy(src_ref=part_s.at[ms], dst_ref=part_s.at[ms], send_sem=send3.at[k],
                                              recv_sem=recv3.at[k], device_id=peer, device_id_type=MESH)
            cp.start()
            copies.append(cp)
        for cp in copies:
            cp.wait()
        for s in range(NSH):
            mod_ref[:, s * mc:(s + 1) * mc] = part_s[s]

    vm = pl.BlockSpec(memory_space=pltpu.VMEM)
    return pl.pallas_call(
        body, in_specs=[vm] * 4, out_specs=(vm, vm),
        out_shape=(jax.ShapeDtypeStruct((16, NSH * mc), F32), jax.ShapeDtypeStruct((16, D), F32)),
        scratch_shapes=[pltpu.VMEM((8, 8, D), F32), pltpu.VMEM((NSH, 16, mc), F32),
                        pltpu.SemaphoreType.DMA((8,)), pltpu.SemaphoreType.DMA((8,)),
                        pltpu.SemaphoreType.DMA((4,)), pltpu.SemaphoreType.DMA((4,))],
        name="mod_tp", compiler_params=_cp())(c8, c_ctx, w_mod_sh, b_mod_sh)


def _inproj_ag(x_all, mod, w_in_sh, b_in, narrow_sh, n_lat_tiles):
    m = x_all.shape[0]
    tm = 256
    nt = m // tm
    x_, y_ = lax.axis_index("x"), lax.axis_index("y")
    sids = jnp.stack([2 * x_ + y_, 2 * (1 - x_) + y_, 2 * x_ + 1 - y_, 2 * (1 - x_) + 1 - y_]).astype(jnp.int32)

    def body(sid_ref, x_ref, mod_ref, b_ref, wsh_ref, nsh_ref, z_ref, wg_ref, ng_ref, w_s, *sems):
        n = pl.program_id(0)
        i = pl.program_id(1)
        ag = ((wsh_ref, nsh_ref), (wg_ref, ng_ref)) + tuple(sems[:5])
        wsem = sems[5]

        def load(src):
            cp = pltpu.make_async_copy(src, w_s, wsem)
            cp.start()
            cp.wait()

        @pl.when((n == 0) & (i == 0))
        def _():
            local, chip, _ = _ag_copies(*ag)
            for cp in chip + local:
                cp.start()
            load(wsh_ref)

        for k in range(NSH - 1):
            @pl.when((n == k + 1) & (i == 0))
            def _():
                _, chip, hand = _ag_copies(*ag)
                chip[k].wait_recv()
                hand[k].start()
                hand[k].wait_recv()
                load(wg_ref.at[sid_ref[k + 1]])

        u, _ = _modulate(x_ref, mod_ref, i < n_lat_tiles)
        z_ref[...] = _dot(u, w_s[...]) + b_ref[...]

        @pl.when((n == NSH - 1) & (i == nt - 1))
        def _():
            _ag_finish(*_ag_copies(*ag), done=NSH - 1)

    outs = pl.pallas_call(
        body, grid_spec=pltpu.PrefetchScalarGridSpec(
            num_scalar_prefetch=1, grid=(NSH, nt),
            in_specs=[pl.BlockSpec((tm, D), lambda n, i, sid: (i, 0)),
                      pl.BlockSpec((16, 3 * D), lambda n, i, sid: (0, 0)),
                      pl.BlockSpec((1, SHC), lambda n, i, sid: (0, sid[n])), _ANY, _ANY],
            out_specs=[pl.BlockSpec((tm, SHC), lambda n, i, sid: (i, sid[n])), _ANY, _ANY],
            scratch_shapes=[pltpu.VMEM((D, SHC), BF)] + _ag_sems(2) + [pltpu.SemaphoreType.DMA]),
        out_shape=[jax.ShapeDtypeStruct((m, IN_COLS), F32), jax.ShapeDtypeStruct((NSH,) + w_in_sh.shape, BF),
                   jax.ShapeDtypeStruct((NSH,) + narrow_sh.shape, narrow_sh.dtype)],
        name="inproj_ag", compiler_params=_cp(("arbitrary", "arbitrary")))(sids, x_all, mod, b_in, w_in_sh, narrow_sh)
    return outs


def _rs_sibling(grads):
    n = len(grads)

    def body(*refs):
        ins, got = refs[:n], refs[n:2 * n]
        send, recv = refs[2 * n:]
        x, y, c = _place()
        copies = []
        for j in range(n):
            hr = ins[j].shape[1] // 2
            for s in range(NSH):
                give = ins[j].at[s, pl.ds(pl.multiple_of((1 - c) * hr, 8), hr), :]
                cp = pltpu.make_async_remote_copy(src_ref=give, dst_ref=got[j].at[s], send_sem=send.at[NSH * j + s],
                                                  recv_sem=recv.at[NSH * j + s], device_id=(x, y, 1 - c),
                                                  device_id_type=MESH)
                cp.start()
                copies.append(cp)
        for cp in copies:
            cp.wait()

    half = [jax.ShapeDtypeStruct((NSH, g.shape[1] // 2, g.shape[2]), F32) for g in grads]
    return pl.pallas_call(
        body, in_specs=[_ANY] * n, out_specs=[_ANY] * n, out_shape=half,
        scratch_shapes=[pltpu.SemaphoreType.DMA((NSH * n,)), pltpu.SemaphoreType.DMA((NSH * n,))],
        name="rs_sibling")(*grads)


def _core_vec():
    return lax.axis_index("c").astype(jnp.int32).reshape(1)


def _rs_add1(g, got, name):
    _, r, cols = g.shape
    hr = r // 2
    tr = _row_tile(hr, cols)
    nb = hr // tr

    def body(c_ref, g_ref, got_ref, o_ref):
        o_ref[...] = (g_ref[...] + got_ref[...]).astype(BF)

    spec = pl.BlockSpec((1, tr, cols), lambda s, i, c_ref: (s, i, 0))
    return pl.pallas_call(
        body, grid_spec=pltpu.PrefetchScalarGridSpec(
            num_scalar_prefetch=1, grid=(NSH, nb),
            in_specs=[pl.BlockSpec((1, tr, cols), lambda s, i, c_ref: (s, c_ref[0] * nb + i, 0)), spec],
            out_specs=spec),
        out_shape=jax.ShapeDtypeStruct((NSH, hr, cols), BF), name=name,
        compiler_params=_cp(("arbitrary", "arbitrary")))(_core_vec(), g, got)


def _rs_add2(sums, got, name):
    _, hr, cols = sums.shape
    tr = _row_tile(hr, cols)
    nb = hr // tr
    place = jnp.stack([2 * lax.axis_index("x") + lax.axis_index("y"), lax.axis_index("c")]).astype(jnp.int32)

    def body(p_ref, s_ref, got_ref, o_ref):
        f = lambda v: v.astype(F32)
        o_ref[...] = f(s_ref[0]) + f(got_ref[0]) + f(got_ref[1]) + f(got_ref[2])

    return pl.pallas_call(
        body, grid_spec=pltpu.PrefetchScalarGridSpec(
            num_scalar_prefetch=1, grid=(nb,),
            in_specs=[pl.BlockSpec((1, tr, cols), lambda i, p_ref: (p_ref[0], i, 0)),
                      pl.BlockSpec((3, tr, cols), lambda i, p_ref: (0, i, 0))],
            out_specs=pl.BlockSpec((tr, cols), lambda i, p_ref: (p_ref[1] * nb + i, 0))),
        out_shape=jax.ShapeDtypeStruct((2 * hr, cols), F32), name=name,
        compiler_params=_cp(("arbitrary",)))(place, sums, got)


def _rs_chip_copies(ins, got, send, recv):
    x, y, c = _place()
    peers = ((1 - x, y), (x, 1 - y), (1 - x, 1 - y))
    return [pltpu.make_async_remote_copy(src_ref=ins[j].at[2 * px + py], dst_ref=got[j].at[k],
                                         send_sem=send.at[3 * j + k], recv_sem=recv.at[3 * j + k],
                                         device_id=(px, py, c), device_id_type=MESH)
            for j in range(len(ins)) for k, (px, py) in enumerate(peers)]


def _ag_sibling(fulls):
    n = len(fulls)
    nck = 4

    def body(*refs):
        outs = refs[n:2 * n]
        send, recv = refs[2 * n:]
        x, y, c = _place()
        copies = []
        for j in range(n):
            qr = outs[j].shape[0] // (2 * nck)
            for k in range(nck):
                rows = outs[j].at[pl.ds(pl.multiple_of((c * nck + k) * qr, 8), qr), :]
                cp = pltpu.make_async_remote_copy(src_ref=rows, dst_ref=rows, send_sem=send.at[nck * j + k],
                                                  recv_sem=recv.at[nck * j + k], device_id=(x, y, 1 - c),
                                                  device_id_type=MESH)
                cp.start()
                copies.append(cp)
        for cp in copies:
            cp.wait()

    return pl.pallas_call(
        body, in_specs=[_ANY] * n, out_specs=[_ANY] * n,
        out_shape=[jax.ShapeDtypeStruct(f.shape, F32) for f in fulls],
        input_output_aliases={j: j for j in range(n)},
        scratch_shapes=[pltpu.SemaphoreType.DMA((nck * n,)), pltpu.SemaphoreType.DMA((nck * n,))],
        name="ag_sibling")(*fulls)


def _allreduce_small(buf):
    rows = buf.shape[0]
    pr = rows // 8

    def body(in_ref, out_ref, stage, send1, recv1, send2, recv2):
        x, y, c = _place()
        me = 4 * x + 2 * y + c

        def peer(k):
            kx, ky, kc = (k >> 2) & 1, (k >> 1) & 1, k & 1
            return (x ^ kx, y ^ ky, c ^ kc)

        def piece(ref, idx):
            return ref.at[pl.ds(pl.multiple_of(idx * pr, 8), pr), :]

        copies = []
        for k in range(1, 8):
            px, py, pc = peer(k)
            cp = pltpu.make_async_remote_copy(src_ref=piece(in_ref, 4 * px + 2 * py + pc), dst_ref=stage.at[k],
                                              send_sem=send1.at[k], recv_sem=recv1.at[k],
                                              device_id=(px, py, pc), device_id_type=MESH)
            cp.start()
            copies.append(cp)
        for cp in copies:
            cp.wait()
        acc = piece(in_ref, me)[...]
        for k in range(1, 8):
            acc = acc + stage[k]
        piece(out_ref, me)[...] = acc
        copies = []
        for k in range(1, 8):
            cp = pltpu.make_async_remote_copy(src_ref=piece(out_ref, me), dst_ref=piece(out_ref, me),
                                              send_sem=send2.at[k], recv_sem=recv2.at[k],
                                              device_id=peer(k), device_id_type=MESH)
            cp.start()
            copies.append(cp)
        for cp in copies:
            cp.wait()

    vm = pl.BlockSpec(memory_space=pltpu.VMEM)
    return pl.pallas_call(
        body, in_specs=[vm], out_specs=vm, out_shape=jax.ShapeDtypeStruct((rows, D), F32),
        scratch_shapes=[pltpu.VMEM((8, pr, D), F32)] + [pltpu.SemaphoreType.DMA((8,))] * 4,
        name="allreduce_small", compiler_params=_cp())(buf)


def _rows(a):
    flat = a.reshape(-1)
    pad = (-flat.shape[0]) % D
    if pad:
        flat = jnp.concatenate([flat, jnp.zeros((pad,), flat.dtype)])
    return flat.reshape(-1, D)


def _pad_rows(a, mult):
    pad = (-a.shape[0]) % mult
    return jnp.concatenate([a, jnp.zeros((pad, a.shape[1]), a.dtype)]) if pad else a


def _local_step(x, c, ctx, c_ctx, tgt, me, shard, sh, b_mod, b_in, norm_g, cb, wr, wi, ln_g, ln_b):
    t_lat, t_ctx = x.shape[0], ctx.shape[0]
    nlt = t_lat // 256
    nlb, ncb = t_lat // RB, t_ctx // RB
    mc = 3 * D // NSH
    mod_all, cc_all = _mod_tp(jnp.zeros((8, D), F32).at[0].set(c), c_ctx.reshape(1, D), sh["w_mod"],
                              lax.dynamic_slice_in_dim(b_mod, shard * mc, mc, axis=1))
    mod = jnp.zeros((16, 3 * D), F32).at[0].set(mod_all[me]).at[1].set(mod_all[8])
    cc = jnp.zeros((16, D), F32).at[0].set(c).at[1].set(c_ctx)
    x_all = jnp.concatenate([x, ctx], axis=0)
    z, w_in_g, nar = _inproj_ag(x_all, mod, sh["w_in"], b_in, sh["narrow"], nlt)
    nar = jnp.transpose(nar, (1, 0, 2)).reshape(-1, D)
    lbl, cw, br, bi, lam = nar[0:4].reshape(2, 2, D), nar[4:8], nar[8:10], nar[10:12], nar[12:14]
    o0, st0, (w_mod_g, p_a, p_b, w_out) = _gla_fwd(z, lbl, 0, nlb, ncb,
                                                   gather=[sh[k] for k in ("w_mod", "p_a", "p_b", "w_out")])
    p_a, p_b, w_out = p_a.reshape(D, D), p_b.reshape(D, D), w_out.reshape(D, D)
    o1, st1, _ = _gla_fwd(z, lbl, 1, nlb, ncb)
    h0, hp0, a0, hpc0 = _rglru_fwd(z, cw, cb, wr, br, wi, bi, lam, 0, t_lat, t_ctx)
    h1, hp1, a1, hpc1 = _rglru_fwd(z, cw, cb, wr, br, wi, bi, lam, 1, t_lat, t_ctx)
    (do, dh, dz4, dz6, dz7, dz8, gxres, y, dout, oa, dpa, obv, dpb, acc) = _merge(
        o0, o1, h0, h1, z, x_all, tgt, mod, norm_g, ln_g, ln_b, p_a, p_b, w_out, t_lat)
    gp_a = _wgrad(oa, dpa, "wgrad_pa")
    gp_b = _wgrad(obv, dpb, "wgrad_pb")
    gw_out = _wgrad(y, dout, "wgrad_wout")
    dxc_lat, dxc_ctx, dwr0, dwi0, sb0 = _rglru_bwd(z, cw, cb, wr, br, wi, bi, lam, dh, hp0, a0, hpc0, 0, t_lat, t_ctx)
    dz5_lat, dz5_ctx, dwr1, dwi1, sb1 = _rglru_bwd(z, cw, cb, wr, br, wi, bi, lam, dh, hp1, a1, hpc1, 1, t_lat, t_ctx,
                                                   prev=(dxc_lat, dxc_ctx))
    dq0, dv0, dz1, sa0 = _gla_bwd(z, lbl, do, st0, 0, nlb, ncb)
    dz0, dz3, dz2, sa1 = _gla_bwd(z, lbl, do, st1, 1, nlb, ncb, prev=(dq0, dv0))
    dz = jnp.concatenate([dz0, dz1, dz2, dz3, dz4, jnp.concatenate([dz5_lat, dz5_ctx], axis=0).astype(BF),
                          dz6, dz7, dz8], axis=1)
    big = dict(w_in=_wgrad_in(x_all, mod, dz, nlt), p_a=gp_a.reshape(NSH, D // NSH, D),
               p_b=gp_b.reshape(NSH, D // NSH, D), w_out=gw_out.reshape(NSH, D // NSH, D))
    grads = [big[k] for k in _RS]
    sums = [_rs_add1(g, b, f"rs_add1_{j}") for j, (g, b) in enumerate(zip(grads, _rs_sibling(grads)))]
    gx, dm, got = _du(dz, w_in_g, x_all, mod, gxres, nlt, sums)
    fulls = [_rs_add2(a, b, f"rs_add2_{j}") for j, (a, b) in enumerate(zip(sums, got))]
    big = dict(zip(_RS, _ag_sibling(fulls)))
    dmod = jnp.zeros((16, 3 * D), F32).at[0:2, 0:2 * D].set(dm[0:2]).at[0, 2 * D:].set(acc[0])
    dcc = _mod_bwd(cc, dmod, w_mod_g)
    small = dict(
        c_ctx=dcc[1:2], b_mod=(dmod[0] + dmod[1]).reshape(3, D),
        b_in=jnp.stack([sa1[2], sa0[0], sa1[0], sa1[3], acc[3], sb1[4], acc[4], acc[5], acc[6]]),
        lb_logits=jnp.stack([sa0[1], sa1[1], -sa0[1], -sa1[1]]),
        norm_a_g=acc[7:8], conv_w=sb1[8:12], conv_b=sb1[3:4],
        w_r=jnp.stack([dwr0, dwr1]).reshape(-1, D), w_i=jnp.stack([dwi0, dwi1]).reshape(-1, D),
        b_r=jnp.stack([sb0[0], sb1[0]]), b_i=jnp.stack([sb0[1], sb1[1]]), lam=jnp.stack([sb0[2], sb1[2]]),
        ln_g=acc[1:2], ln_b=acc[2:3])
    return acc[9, 0], gx, big, small, dmod, cc_all


_RS =("w_in", "p_a", "p_b", "w_out")
_SMALL =("c_ctx", "b_mod", "b_in", "lb_logits", "norm_a_g", "conv_w", "conv_b", "w_r", "w_i", "b_r", "b_i", "lam",
          "ln_g", "ln_b")
_BIG = ("w_mod", "w_in", "p_a", "p_b", "w_out")
_COL_SHARDED = ("lb_logits", "conv_w", "b_r", "b_i", "lam")
_WEIGHTS = ("c_ctx", "w_mod", "b_mod", "w_in", "b_in", "lb_logits", "norm_a_g", "conv_w", "conv_b", "w_r", "b_r", "w_i",
            "b_i", "lam", "p_a", "p_b", "w_out", "ln_g", "ln_b")


def kernel(x, c, ctx, c_ctx, w_mod, b_mod, w_in, b_in, lb_logits, norm_a_g, conv_w, conv_b, w_r, b_r, w_i, b_i, lam, p_a, p_b, w_out, ln_g, ln_b, loss_target, m_c_ctx, m_w_mod, m_b_mod, m_w_in, m_b_in, m_lb_logits, m_norm_a_g, m_conv_w, m_conv_b, m_w_r, m_b_r, m_w_i, m_b_i, m_lam, m_p_a, m_p_b, m_w_out, m_ln_g, m_ln_b, v_c_ctx, v_w_mod, v_b_mod, v_w_in, v_b_in, v_lb_logits, v_norm_a_g, v_conv_w, v_conv_b, v_w_r, v_b_r, v_w_i, v_b_i, v_lam, v_p_a, v_p_b, v_w_out, v_ln_g, v_ln_b):
    w = dict(c_ctx=c_ctx, w_mod=w_mod, b_mod=b_mod, w_in=w_in, b_in=b_in, lb_logits=lb_logits, norm_a_g=norm_a_g,
             conv_w=conv_w, conv_b=conv_b, w_r=w_r, b_r=b_r, w_i=w_i, b_i=b_i, lam=lam, p_a=p_a, p_b=p_b, w_out=w_out,
             ln_g=ln_g, ln_b=ln_b)
    m = dict(c_ctx=m_c_ctx, w_mod=m_w_mod, b_mod=m_b_mod, w_in=m_w_in, b_in=m_b_in, lb_logits=m_lb_logits,
             norm_a_g=m_norm_a_g, conv_w=m_conv_w, conv_b=m_conv_b, w_r=m_w_r, b_r=m_b_r, w_i=m_w_i, b_i=m_b_i,
             lam=m_lam, p_a=m_p_a, p_b=m_p_b, w_out=m_w_out, ln_g=m_ln_g, ln_b=m_ln_b)
    v = dict(c_ctx=v_c_ctx, w_mod=v_w_mod, b_mod=v_b_mod, w_in=v_w_in, b_in=v_b_in, lb_logits=v_lb_logits,
             norm_a_g=v_norm_a_g, conv_w=v_conv_w, conv_b=v_conv_b, w_r=v_w_r, b_r=v_b_r, w_i=v_w_i, b_i=v_b_i,
             lam=v_lam, p_a=v_p_a, p_b=v_p_b, w_out=v_w_out, ln_g=v_ln_g, ln_b=v_ln_b)
    shard = 2 * lax.axis_index("x") + lax.axis_index("y")
    cs = D // NSH

    sh = {k: w[k][0].astype(BF) for k in _BIG}
    sh["narrow"] = _pad_rows(jnp.concatenate([lb_logits.reshape(4, cs), conv_w[0], b_r[0], b_i[0], lam[0]], axis=0), 8)
    me = 2 * shard + lax.axis_index("c")
    loss, gx, big, small, dmod, cc_all = _local_step(
        x[0], c[0], ctx[0], c_ctx, loss_target[0], me, shard, sh, b_mod, b_in, norm_a_g, conv_b, w_r[0], w_i[0],
        ln_g, ln_b)
    loss = lax.psum(loss, ("x", "y", "c"))

    dmod_rows = jnp.zeros((16, 3 * D), F32).at[me].set(dmod[0]).at[8].set(dmod[1]).reshape(48, D)
    sizes = [small[k].shape[0] for k in _SMALL]
    red = _allreduce_small(_pad_rows(jnp.concatenate([_pad_rows(small[k], 8) for k in _SMALL] + [dmod_rows],
                                                     axis=0), 64))
    grads = {}
    off = 0
    for k, n in zip(_SMALL, sizes):
        g = red[off:off + n]
        off += n + (-n) % 8
        if k == "norm_a_g":
            g = g[:, :DH]
        if k in _COL_SHARDED:
            g = lax.dynamic_slice_in_dim(g, shard * cs, cs, axis=1)
        grads[k] = g.reshape(w[k].shape)
    for k in _RS:
        grads[k] = big[k].reshape(w[k].shape)
    dmod_all = red[off:off + 48].reshape(16, 3 * D)
    mc = 3 * D // NSH
    grads["w_mod"] = _wmod_grad(cc_all, lax.dynamic_slice_in_dim(dmod_all, shard * mc, mc, axis=1)).reshape(
        w["w_mod"].shape)

    delta, new_m, new_v = {}, {}, {}
    for k in _BIG:
        shp = w[k].shape
        two = lambda a: a.reshape(shp[-2], shp[-1])
        d_, m_, v_ = _adamw(two(w[k]), two(grads[k]), two(m[k]), two(v[k]), f"adamw_{k}")
        delta[k], new_m[k], new_v[k] = d_.reshape(shp), m_.reshape(shp), v_.reshape(shp)
    d_, m_, v_ = _adamw_many(*[[t[k] for k in _SMALL] for t in (w, grads, m, v)])
    delta.update(zip(_SMALL, d_))
    new_m.update(zip(_SMALL, m_))
    new_v.update(zip(_SMALL, v_))

    return (loss, gx[None], *[grads[k] for k in _WEIGHTS], *[delta[k] for k in _WEIGHTS],
            *[new_m[k] for k in _WEIGHTS], *[new_v[k] for k in _WEIGHTS])
```

```python
import functools

import jax
import jax.numpy as jnp
from jax import lax
from jax.experimental import pallas as pl
from jax.experimental.pallas import tpu as pltpu

F32 = jnp.float32
BF = jnp.bfloat16
MESH = pl.DeviceIdType.MESH

D = 1024
H = 8
DH = 128
CH = 64
RB = 256
NCK = RB // CH
GRID_W = 64
CB = 8
RCH = 16
IN_COLS = 9 * D
NSH = 4
SHC = IN_COLS // NSH
RG_C = 8.0
ALPHA = 2.0 ** 0.25
LN_EPS = 1e-5
RMS_EPS = 1e-6
Q_SCALE = DH ** -0.5
ADAM_LR, ADAM_B1, ADAM_B2, ADAM_EPS, ADAM_WD, ADAM_STEP = 0.001, 0.9, 0.999, 1e-08, 0.01, 10
VMEM_LIMIT = 56 * 1024 * 1024
VMEM_LIMIT_MERGE = 60 * 1024 * 1024


def _cp(sem=None, vmem=VMEM_LIMIT):
    return pltpu.CompilerParams(dimension_semantics=sem, vmem_limit_bytes=vmem)


def _sig(x):
    return 0.5 * jnp.tanh(0.5 * x) + 0.5


def _dot(a, b):
    return jnp.dot(a.astype(BF), b.astype(BF), preferred_element_type=F32)


def _dot_nt(a, b):
    return lax.dot_general(a.astype(BF), b.astype(BF), (((1,), (1,)), ((), ())), preferred_element_type=F32)


def _dot_tn(a, b):
    return lax.dot_general(a.astype(BF), b.astype(BF), (((0,), (0,)), ((), ())), preferred_element_type=F32)


def _colsum(v):
    return jnp.sum(v, axis=0, keepdims=True)


def _mod_bwd(cc, dmod, w_mod_g):
    def body(cc_ref, dm_ref, w_ref, dcc_ref):
        v = cc_ref[...]
        sg = _sig(v)
        ds = jnp.zeros((16, D), F32)
        for k in range(NSH):
            ds = ds + _dot_nt(dm_ref[:, k * 768:(k + 1) * 768], w_ref[k])
        dcc_ref[...] = ds * (sg * (1.0 + v * (1.0 - sg)))
    return pl.pallas_call(body, out_shape=jax.ShapeDtypeStruct((16, D), F32),
                          name="mod_bwd", compiler_params=_cp())(cc, dmod, w_mod_g)


def _wmod_grad(cc, dmod_cols):
    def body(cc_ref, dm_ref, dw_ref):
        v = cc_ref[...]
        dw_ref[...] = _dot_tn(v * _sig(v), dm_ref[...])
    return pl.pallas_call(body, out_shape=jax.ShapeDtypeStruct((D, dmod_cols.shape[1]), F32),
                          name="wmod_grad", compiler_params=_cp())(cc, dmod_cols)


def _modulate(x_ref, mod_ref, is_lat):
    sh = jnp.where(is_lat, mod_ref[0:1, 0:D], mod_ref[1:2, 0:D])
    sc = jnp.where(is_lat, mod_ref[0:1, D:2 * D], mod_ref[1:2, D:2 * D])
    return x_ref[...] * (1.0 + sc) + sh, sc


def _gla_mask(d, n):
    row = lax.broadcasted_iota(jnp.int32, (n, n), 0)
    col = lax.broadcasted_iota(jnp.int32, (n, n), 1)
    same = (row // CH) == (col // CH)
    return same & ((row <= col) if d else (row >= col))


def _chunk_cumsum(v, rev):
    n = v.shape[0]
    pos = lax.broadcasted_iota(jnp.int32, v.shape, 0) & (CH - 1)
    s = 1
    while s < CH:
        if rev:
            v = v + jnp.where(pos < CH - s, pltpu.roll(v, n - s, 0), 0.0)
        else:
            v = v + jnp.where(pos >= s, pltpu.roll(v, s, 0), 0.0)
        s *= 2
    return v


def _chunk_rows(c):
    return slice(c * CH, (c + 1) * CH)


def _gla_features(zq, zf, lb, d):
    sq = _sig(zq)
    q = zq * sq * Q_SCALE
    sf = _sig(zf)
    f = lb + (1.0 - lb) * sf
    k = 1.0 - f
    g = _chunk_cumsum(jnp.log(f), d)
    last = 0 if d else CH - 1
    gls = [g[c * CH + last:c * CH + last + 1, :] for c in range(NCK)]
    glb = jnp.concatenate([jnp.broadcast_to(gl, (CH, D)) for gl in gls], axis=0)
    eg, eig, eeg = jnp.exp(g), jnp.exp(-g), jnp.exp(glb - g)
    decs = [jnp.exp(gl) for gl in gls]
    return sq, sf, f, q * eg, k * eig, k * eeg, eg, eig, eeg, decs


def _lower_bound(lbl_ref, d):
    return _sig(lbl_ref[0, d:d + 1, :] - lbl_ref[1, d:d + 1, :])


def _gla_rb(d, nlb, ncb):
    nrb = nlb + ncb
    if d:
        return lambda s: nrb - 1 - s
    return lambda s: jnp.where(s < ncb, nlb + s, s - ncb)


def _gla_fwd(z, lbl, d, nlb, ncb, gather=()):
    m = z.shape[0]
    nrb = nlb + ncb
    rb = _gla_rb(d, nlb, ncb)
    ng = len(gather)

    def body(*refs):
        q_ref, f_ref, v_ref, lbl_ref = refs[:4]
        ag_in = refs[4:4 + ng]
        o_ref, st_ref = refs[4 + ng:6 + ng]
        ag_out = refs[6 + ng:6 + 2 * ng]
        S = refs[6 + 2 * ng]
        ag_sems = refs[7 + 2 * ng:]
        s = pl.program_id(0)

        @pl.when(s == 0)
        def _():
            S[...] = jnp.zeros_like(S)
            if ng:
                local, chip, _ = _ag_copies(ag_in, ag_out, *ag_sems)
                for cp in chip + local:
                    cp.start()

        lb = _lower_bound(lbl_ref, d)
        mb = _gla_mask(d, CH)
        _, _, _, qd, ki, ke, _, _, _, decs = _gla_features(q_ref[...], f_ref[...], lb, d)
        qd, ki, ke, v = qd.astype(BF), ki.astype(BF), ke.astype(BF), v_ref[...].astype(BF)
        order = range(NCK - 1, -1, -1) if d else range(NCK)
        for h in range(H):
            hs = slice(h * DH, (h + 1) * DH)
            intra, upd = {}, {}
            for c in range(NCK):
                rs = _chunk_rows(c)
                a = jnp.where(mb, _dot_nt(qd[rs, hs], ki[rs, hs]), 0.0)
                intra[c] = _dot(a, v[rs, hs])
                upd[c] = _dot_tn(v[rs, hs], ke[rs, hs])
            st = S[h]
            for c in order:
                rs = _chunk_rows(c)
                st_ref[c, h] = st
                o_ref[rs, hs] = intra[c] + _dot_nt(qd[rs, hs], st)
                st = st * decs[c][:, hs] + upd[c]
            S[h] = st

        if ng:
            @pl.when(s == nrb - 1)
            def _():
                _ag_finish(*_ag_copies(ag_in, ag_out, *ag_sems))

    def zspec(cb):
        return pl.BlockSpec((RB, D), lambda s: (rb(s), cb))

    outs = pl.pallas_call(
        body, grid=(nrb,),
        in_specs=[zspec(0), zspec(1 + d), zspec(3), pl.BlockSpec((2, 2, D), lambda s: (0, 0, 0))] + [_ANY] * ng,
        out_specs=[pl.BlockSpec((RB, D), lambda s: (rb(s), 0)),
                   pl.BlockSpec((NCK, H, DH, DH), lambda s: (rb(s), 0, 0, 0))] + [_ANY] * ng,
        out_shape=[jax.ShapeDtypeStruct((m, D), F32), jax.ShapeDtypeStruct((m // CH, H, DH, DH), F32)]
        + [jax.ShapeDtypeStruct((NSH,) + g.shape, g.dtype) for g in gather],
        scratch_shapes=[pltpu.VMEM((H, DH, DH), F32)] + (_ag_sems(ng) if ng else []),
        name=f"gla_fwd{d}", compiler_params=_cp(("arbitrary",)))(z, z, z, lbl, *gather)
    return outs[0], outs[1], list(outs[2:])


def _gla_bwd(z, lbl, do_lat, states, d, nlb, ncb, prev=None):
    m = z.shape[0]
    nrb = nlb + ncb
    fwd_rb = _gla_rb(d, nlb, ncb)
    rb = lambda s: fwd_rb(nrb - 1 - s)
    last = prev is not None

    def body(*refs):
        if last:
            q_ref, f_ref, v_ref, lbl_ref, do_ref, st_ref, pq_ref, pv_ref, o0_ref, o1_ref, o2_ref, sum_ref, dS = refs
        else:
            q_ref, f_ref, v_ref, lbl_ref, do_ref, st_ref, o0_ref, o1_ref, o2_ref, sum_ref, dS = refs
        s = pl.program_id(0)
        is_lat = rb(s) < nlb

        @pl.when(s == 0)
        def _():
            dS[...] = jnp.zeros_like(dS)
            sum_ref[...] = jnp.zeros_like(sum_ref)

        lb = _lower_bound(lbl_ref, d)
        mb = _gla_mask(d, RB)
        zq = q_ref[...]
        sq, sf, f, qd, ki, ke, eg, eig, eeg, decs = _gla_features(zq, f_ref[...], lb, d)
        qdb, kib, keb, vb = qd.astype(BF), ki.astype(BF), ke.astype(BF), v_ref[...].astype(BF)
        dob = jnp.where(is_lat, do_ref[...], 0.0).astype(BF)
        order = range(NCK) if d else range(NCK - 1, -1, -1)
        dqd_h, dki_h, dke_h, dv_h, ddec_h = [], [], [], [], []
        for h in range(H):
            hs = slice(h * DH, (h + 1) * DH)
            a = jnp.where(mb, _dot_nt(qdb[:, hs], kib[:, hs]), 0.0).astype(BF)
            da = jnp.where(mb, _dot_nt(dob[:, hs], vb[:, hs]), 0.0).astype(BF)
            dqd_i = _dot(da, kib[:, hs])
            dki_h.append(_dot_tn(da, qdb[:, hs]))
            dvi = _dot_tn(a, dob[:, hs])
            dqd, inc = {}, {}
            for c in range(NCK):
                rs = _chunk_rows(c)
                dqd[c] = dqd_i[rs, :] + _dot(dob[rs, hs], st_ref[c, h])
                inc[c] = _dot_tn(dob[rs, hs], qdb[rs, hs])
            dst = dS[h]
            dke, dv, ddec = {}, {}, {}
            for c in order:
                rs = _chunk_rows(c)
                dv[c] = dvi[rs, :] + _dot_nt(keb[rs, hs], dst)
                dke[c] = _dot(vb[rs, hs], dst)
                ddec[c] = _colsum(st_ref[c, h] * dst)
                dst = inc[c] + dst * decs[c][:, hs]
            dS[h] = dst
            cat = lambda t: jnp.concatenate([t[c] for c in range(NCK)], axis=0)
            dqd_h.append(cat(dqd))
            dke_h.append(cat(dke))
            dv_h.append(cat(dv))
            ddec_h.append([ddec[c] for c in range(NCK)])
        lanes = lambda parts: jnp.concatenate(parts, axis=1)
        dqd, dki, dke, dv = lanes(dqd_h), lanes(dki_h), lanes(dke_h), lanes(dv_h)
        dq = dqd * eg
        dk = dki * eig + dke * eeg
        dke_ke = dke * ke
        dg = dqd * qd - dki * ki - dke_ke
        dgl = [_colsum(dke_ke[_chunk_rows(c), :]) + lanes([ddec_h[h][c] for h in range(H)]) * decs[c]
               for c in range(NCK)]
        dglb = jnp.concatenate([jnp.broadcast_to(t, (CH, D)) for t in dgl], axis=0)
        df = (_chunk_cumsum(dg, 1 - d) + dglb) / f - dk
        dzf = df * (1.0 - lb) * (sf * (1.0 - sf))
        sum_ref[0:1, :] += _colsum(dzf)
        sum_ref[1:2, :] += _colsum(df * (1.0 - sf))
        if last:
            dz0 = (dq + pq_ref[...]) * (Q_SCALE * (sq * (1.0 + zq * (1.0 - sq))))
            dz3 = dv + pv_ref[...]
            sum_ref[2:3, :] += _colsum(dz0)
            sum_ref[3:4, :] += _colsum(dz3)
            o0_ref[...] = dz0.astype(BF)
            o1_ref[...] = dz3.astype(BF)
        else:
            o0_ref[...] = dq
            o1_ref[...] = dv
        o2_ref[...] = dzf.astype(BF)

        @pl.when(s == nrb - 1)
        def _():
            sum_ref[1:2, :] = sum_ref[1:2, :] * (lb * (1.0 - lb))

    def zspec(cb):
        return pl.BlockSpec((RB, D), lambda s: (rb(s), cb))

    rowspec = pl.BlockSpec((RB, D), lambda s: (rb(s), 0))
    in_specs = [zspec(0), zspec(1 + d), zspec(3), pl.BlockSpec((2, 2, D), lambda s: (0, 0, 0)),
                pl.BlockSpec((RB, D), lambda s: (jnp.minimum(rb(s), nlb - 1), 0)),
                pl.BlockSpec((NCK, H, DH, DH), lambda s: (rb(s), 0, 0, 0))]
    args = [z, z, z, lbl, do_lat, states]
    if last:
        in_specs += [rowspec, rowspec]
        args += list(prev)
    dt01 = BF if last else F32
    return pl.pallas_call(
        body, grid=(nrb,), in_specs=in_specs,
        out_specs=(rowspec, rowspec, rowspec, pl.BlockSpec((8, D), lambda s: (0, 0))),
        out_shape=(jax.ShapeDtypeStruct((m, D), dt01), jax.ShapeDtypeStruct((m, D), dt01),
                   jax.ShapeDtypeStruct((m, D), BF), jax.ShapeDtypeStruct((8, D), F32)),
        scratch_shapes=[pltpu.VMEM((H, DH, DH), F32)],
        name=f"gla_bwd{d}", compiler_params=_cp(("arbitrary",)))(*args)


def _shift_rows(v, k, r):
    row = lax.broadcasted_iota(jnp.int32, v.shape, 0)
    rolled = pltpu.roll(v, k % r, 0)
    return jnp.where((row >= k) & (row < r + k), rolled, 0.0)


def _conv_fwd(xb, cw_ref, cb_ref, r):
    return (cb_ref[...] + _shift_rows(xb, 1, r) * cw_ref[0:1, :] + xb * cw_ref[1:2, :]
            + _shift_rows(xb, -1, r) * cw_ref[2:3, :] + _shift_rows(xb, -2, r) * cw_ref[3:4, :])


def _window(ref, lo, n, rows):
    parts = []
    if lo < 0:
        parts.append(jnp.zeros((-lo,) + tuple(ref.shape[1:]), F32))
    parts.append(ref[max(lo, 0):min(lo + n, rows)])
    if lo + n > rows:
        parts.append(jnp.zeros((lo + n - rows,) + tuple(ref.shape[1:]), F32))
    return parts[0] if len(parts) == 1 else jnp.concatenate(parts, axis=0)


def _conv_cols(x_ref, cw_ref, cb_ref, r0, n, rows):
    w = _window(x_ref, r0 - 1, n + 3, rows)
    return (cb_ref[...] + w[0:n] * cw_ref[0:1, :] + w[1:n + 1] * cw_ref[1:2, :] + w[2:n + 2] * cw_ref[2:3, :]
            + w[3:n + 3] * cw_ref[3:4, :])


def _softplus_neg(lam):
    y = jnp.exp(-jnp.abs(lam))
    u = 1.0 + y
    tiny = u == 1.0
    l1p = jnp.where(tiny, y, jnp.log(u) * (y / jnp.where(tiny, 1.0, u - 1.0)))
    return jnp.maximum(-lam, 0.0) + l1p


def _gates(xc, wr_ref, br, wi_ref, bi, sp):
    xcb = xc.astype(BF)
    rs, is_ = [], []
    for g in range(H):
        gs = slice(g * DH, (g + 1) * DH)
        rs.append(jnp.dot(xcb[:, gs], wr_ref[g].astype(BF), preferred_element_type=F32))
        is_.append(jnp.dot(xcb[:, gs], wi_ref[g].astype(BF), preferred_element_type=F32))
    r = _sig(jnp.concatenate(rs, axis=1) + br)
    i = _sig(jnp.concatenate(is_, axis=1) + bi)
    log_a = (-RG_C * r) * sp
    a = jnp.exp(log_a)
    t = jnp.tanh(log_a)
    om = (-2.0 * t) / (1.0 - t)
    return r, i, a, om


def _scan_rows(d, nrows, a_s, b_s, h_s, h0):
    nsl = nrows // 8

    def slab(j, h):
        jj = (nsl - 1 - j) if d else j
        r0 = pl.multiple_of(jj * 8, 8)
        for t in (range(7, -1, -1) if d else range(8)):
            h = a_s[pl.ds(r0 + t, 1), :] * h + b_s[pl.ds(r0 + t, 1), :]
            h_s[pl.ds(r0 + t, 1), :] = h
        return h

    return lax.fori_loop(0, nsl, slab, h0)


def _col_of(d, ncols):
    if d:
        return lambda s: ncols - jnp.maximum(s, 1)
    return lambda s: jnp.maximum(s, 1) - 1


def _rglru_fwd(z, cw, cb, wr, br, wi, bi, lam, d, t_lat, t_ctx):
    m = z.shape[0]
    rows = t_lat // GRID_W
    z3 = z.reshape(m // GRID_W, GRID_W, IN_COLS)
    nblk = GRID_W // CB
    blk = _col_of(d, nblk)
    cblk = t_lat // t_ctx
    rc = min(RCH, rows)

    def body(zc_ref, zx_ref, cw_ref, cb_ref, wr_ref, br_ref, wi_ref, bi_ref, lam_ref,
             hx_ref, hpx_ref, ax_ref, hpc_ref, a_s, b_s, h_s, hcar, a3, b3, cin_s):
        s = pl.program_id(0)
        sp = _softplus_neg(lam_ref[d:d + 1, :])
        br_ = br_ref[d:d + 1, :]
        bi_ = bi_ref[d:d + 1, :]

        @pl.when(s == 0)
        def _():
            xc = _conv_fwd(zc_ref[...], cw_ref, cb_ref, t_ctx)
            _, i, a, om = _gates(xc, wr_ref, br_, wi_ref, bi_, sp)
            a_s[...] = a
            b_s[...] = jnp.sqrt(om) * (i * xc)
            h0 = jnp.zeros((1, D), F32)
            hcar[0:1, :] = _scan_rows(d, t_ctx, a_s, b_s, h_s, h0)
            hs = h_s[...]
            row = lax.broadcasted_iota(jnp.int32, (t_ctx, D), 0)
            if d:
                hpc_ref[...] = jnp.where(row == t_ctx - 1, h0, pltpu.roll(hs, t_ctx - 1, 0))
            else:
                hpc_ref[...] = jnp.where(row == 0, h0, pltpu.roll(hs, 1, 0))

        @pl.when(s > 0)
        def _():
            for r0 in range(0, rows, rc):
                xc = _conv_cols(zx_ref, cw_ref, cb_ref, r0, rc, rows).reshape(rc * CB, D)
                _, i, a, om = _gates(xc, wr_ref, br_, wi_ref, bi_, sp)
                a3[r0:r0 + rc] = a.reshape(rc, CB, D)
                ax_ref[r0:r0 + rc] = a.reshape(rc, CB, D)
                b3[r0:r0 + rc] = (jnp.sqrt(om) * (i * xc)).reshape(rc, CB, D)

            def local(t, carry):
                hl, p = carry
                r = (rows - 1 - t) if d else t
                a = a3[r]
                hl = a * hl + b3[r]
                p = a * p
                b3[r] = hl
                a3[r] = p
                return hl, p

            hl, p = lax.fori_loop(0, rows, local, (jnp.zeros((CB, D), F32), jnp.ones((CB, D), F32)))
            cin = hcar[0:1, :]
            for j in (range(CB - 1, -1, -1) if d else range(CB)):
                cin_s[j:j + 1, :] = cin
                cin = hl[j:j + 1, :] + p[j:j + 1, :] * cin
            hcar[0:1, :] = cin
            c_in = cin_s[...]

            def fix(t, prev):
                r = (rows - 1 - t) if d else t
                h = b3[r] + a3[r] * c_in
                hx_ref[r] = h
                hpx_ref[r] = prev
                return h

            lax.fori_loop(0, rows, fix, c_in)

    full = lambda shp: pl.BlockSpec(shp, lambda s: (0,) * len(shp))
    colspec = pl.BlockSpec((rows, CB, D), lambda s: (0, blk(s), 0))
    outs = pl.pallas_call(
        body, grid=(nblk + 1,),
        in_specs=[pl.BlockSpec((t_ctx, D), lambda s: (cblk, 5)),
                  pl.BlockSpec((rows, CB, D), lambda s: (0, blk(s), 5)),
                  full((4, D)), full((1, D)),
                  pl.BlockSpec((None, H, DH, DH), lambda s: (d, 0, 0, 0)), full((2, D)),
                  pl.BlockSpec((None, H, DH, DH), lambda s: (d, 0, 0, 0)), full((2, D)), full((2, D))],
        out_specs=(colspec, colspec, colspec, full((t_ctx, D))),
        out_shape=(jax.ShapeDtypeStruct((rows, GRID_W, D), F32),) * 3 + (jax.ShapeDtypeStruct((t_ctx, D), F32),),
        scratch_shapes=[pltpu.VMEM((t_ctx, D), F32), pltpu.VMEM((t_ctx, D), F32), pltpu.VMEM((t_ctx, D), F32),
                        pltpu.VMEM((8, D), F32), pltpu.VMEM((rows, CB, D), F32), pltpu.VMEM((rows, CB, D), F32),
                        pltpu.VMEM((CB, D), F32)],
        name=f"rglru_fwd{d}", compiler_params=_cp(("arbitrary",)))(z, z3, cw, cb, wr, br, wi, bi, lam)
    return outs[0].reshape(t_lat, D), outs[1].reshape(t_lat, D), outs[2].reshape(t_lat, D), outs[3]


def _rglru_bwd(z, cw, cb, wr, br, wi, bi, lam, dh_lat, hp_lat, a_lat, hp_ctx, d, t_lat, t_ctx, prev=None):
    m = z.shape[0]
    rows = t_lat // GRID_W
    z3 = z.reshape(m // GRID_W, GRID_W, IN_COLS)
    nblk = GRID_W // CB
    fblk = _col_of(d, nblk)
    blk = lambda s: fblk(nblk - jnp.minimum(s, nblk - 1))
    cblk = t_lat // t_ctx
    rc = min(RCH, rows)
    last = prev is not None
    view3 = lambda v: v.reshape(rows, GRID_W, D)

    def body(*refs):
        (zc_ref, zx_ref, cw_ref, cb_ref, wr_ref, br_ref, wi_ref, bi_ref, lam_ref, dh_ref, hpx_ref, ax_ref,
         hpc_ref) = refs[:13]
        k = 13
        if last:
            pdx_ref, pdc_ref = refs[13:15]
            k = 15
        ox_ref, oc_ref, dwr_ref, dwi_ref, sum_ref, a_s, b_s, gcar, a3, b3, cin_s = refs[k:]
        s = pl.program_id(0)
        lam_d = lam_ref[d:d + 1, :]
        sp = _softplus_neg(lam_d)
        br_ = br_ref[d:d + 1, :]
        bi_ = bi_ref[d:d + 1, :]
        flat = lambda v: v.reshape(-1, D)

        @pl.when(s == 0)
        def _():
            gcar[...] = jnp.zeros_like(gcar)
            dwr_ref[...] = jnp.zeros_like(dwr_ref)
            dwi_ref[...] = jnp.zeros_like(dwi_ref)
            sum_ref[...] = jnp.zeros_like(sum_ref)


        def conv_sums(dxc, dxb, xm1, x0, xp1, xp2):
            sum_ref[3:4, :] += _colsum(flat(dxc))
            sum_ref[4:5, :] += _colsum(flat(dxb))
            sum_ref[8:9, :] += _colsum(flat(dxc * xm1))
            sum_ref[9:10, :] += _colsum(flat(dxc * x0))
            sum_ref[10:11, :] += _colsum(flat(dxc * xp1))
            sum_ref[11:12, :] += _colsum(flat(dxc * xp2))

        def gate_grads(g, hp, xc, rr, i, a, om):
            mult = jnp.sqrt(om)
            da = g * hp
            ixc = i * xc
            dmult = g * ixc
            dixc = g * mult
            di = dixc * xc
            dxc = dixc * i
            dlog_a = da * a - dmult * ((1.0 - om) / mult)
            dr = dlog_a * (-RG_C * sp)
            sum_ref[2:3, :] += _colsum(dlog_a * rr)
            drp = dr * rr * (1.0 - rr)
            dip = di * i * (1.0 - i)
            sum_ref[0:1, :] += _colsum(drp)
            sum_ref[1:2, :] += _colsum(dip)
            xcb = xc.astype(BF)
            drb = drp.astype(BF)
            dib = dip.astype(BF)
            parts = []
            for gi in range(H):
                gs = slice(gi * DH, (gi + 1) * DH)
                parts.append(_dot_nt(drb[:, gs], wr_ref[gi]) + _dot_nt(dib[:, gs], wi_ref[gi]))
                dwr_ref[gi] += _dot_tn(xcb[:, gs], drb[:, gs])
                dwi_ref[gi] += _dot_tn(xcb[:, gs], dib[:, gs])
            return dxc + jnp.concatenate(parts, axis=1)

        @pl.when(s < nblk)
        def _():
            def local(t, carry):
                c, q = carry
                r = t if d else (rows - 1 - t)
                a = ax_ref[r]
                c = a * (c + dh_ref[r])
                q = a * q
                b3[r] = c
                a3[r] = q
                return c, q

            c, q = lax.fori_loop(0, rows, local, (jnp.zeros((CB, D), F32), jnp.ones((CB, D), F32)))
            cin = gcar[0:1, :]
            for j in (range(CB) if d else range(CB - 1, -1, -1)):
                cin_s[j:j + 1, :] = cin
                cin = c[j:j + 1, :] + q[j:j + 1, :] * cin
            gcar[0:1, :] = cin
            c_in = cin_s[...]
            for r0 in (range(rows - rc, -1, -rc) if d else range(0, rows, rc)):
                if d:
                    lo = max(r0 - 1, 0)
                    cn = b3[lo:r0 + rc - 1] + a3[lo:r0 + rc - 1] * c_in
                    if r0 == 0:
                        cn = jnp.concatenate([c_in[None], cn], axis=0)
                else:
                    hi = min(r0 + rc + 1, rows)
                    cn = b3[r0 + 1:hi] + a3[r0 + 1:hi] * c_in
                    if hi == rows:
                        cn = jnp.concatenate([cn, c_in[None]], axis=0)
                g = flat(dh_ref[r0:r0 + rc] + cn)
                xc = flat(_conv_cols(zx_ref, cw_ref, cb_ref, r0, rc, rows))
                rr, i, a, om = _gates(xc, wr_ref, br_, wi_ref, bi_, sp)
                b3[r0:r0 + rc] = gate_grads(g, flat(hpx_ref[r0:r0 + rc]), xc, rr, i, a, om).reshape(rc, CB, D)
            if last:
                for r0 in range(0, rows, rc):
                    b3[r0:r0 + rc] = b3[r0:r0 + rc] + pdx_ref[r0:r0 + rc]
                for r0 in range(0, rows, rc):
                    w = _window(b3, r0 - 2, rc + 3, rows)
                    xw = _window(zx_ref, r0 - 1, rc + 3, rows)
                    dxc = w[2:rc + 2]
                    dxb = (w[3:rc + 3] * cw_ref[0:1, :] + dxc * cw_ref[1:2, :] + w[1:rc + 1] * cw_ref[2:3, :]
                           + w[0:rc] * cw_ref[3:4, :])
                    conv_sums(dxc, dxb, xw[0:rc], xw[1:rc + 1], xw[2:rc + 2], xw[3:rc + 3])
                    ox_ref[r0:r0 + rc] = dxb
            else:
                for r0 in range(0, rows, rc):
                    ox_ref[r0:r0 + rc] = b3[r0:r0 + rc]

        @pl.when(s == nblk)
        def _():
            r = t_ctx
            xb = zc_ref[...]
            xc = _conv_fwd(xb, cw_ref, cb_ref, r)
            rr, i, a, om = _gates(xc, wr_ref, br_, wi_ref, bi_, sp)
            a_s[...] = a
            b_s[...] = jnp.zeros((r, D), F32)
            c0 = gcar[0:1, :]
            _scan_rows(1 - d, r, a_s, b_s, b_s, c0)
            cs = b_s[...]
            row = lax.broadcasted_iota(jnp.int32, (r, D), 0)
            if d:
                g = jnp.where(row == 0, c0, pltpu.roll(cs, 1, 0))
            else:
                g = jnp.where(row == r - 1, c0, pltpu.roll(cs, r - 1, 0))
            dxc = gate_grads(g, hpc_ref[...], xc, rr, i, a, om)
            if last:
                dxc = dxc + pdc_ref[...]
                dxb = (_shift_rows(dxc, -1, r) * cw_ref[0:1, :] + dxc * cw_ref[1:2, :]
                       + _shift_rows(dxc, 1, r) * cw_ref[2:3, :] + _shift_rows(dxc, 2, r) * cw_ref[3:4, :])
                conv_sums(dxc, dxb, _shift_rows(xb, 1, r), xb, _shift_rows(xb, -1, r), _shift_rows(xb, -2, r))
                oc_ref[...] = dxb
            else:
                oc_ref[...] = dxc
            sum_ref[2:3, :] = sum_ref[2:3, :] * (RG_C * _sig(-lam_d))

    full = lambda shp: pl.BlockSpec(shp, lambda s: (0,) * len(shp))
    once = lambda shp: pl.BlockSpec(shp, lambda s: (0,) * len(shp), pipeline_mode=pl.Buffered(1))
    colspec = pl.BlockSpec((rows, CB, D), lambda s: (0, blk(s), 0))
    colonce = pl.BlockSpec((rows, CB, D), lambda s: (0, blk(s), 0), pipeline_mode=pl.Buffered(1))
    in_specs = [pl.BlockSpec((t_ctx, D), lambda s: (cblk, 5), pipeline_mode=pl.Buffered(1)),
                pl.BlockSpec((rows, CB, D), lambda s: (0, blk(s), 5), pipeline_mode=pl.Buffered(1)),
                full((4, D)), full((1, D)),
                pl.BlockSpec((None, H, DH, DH), lambda s: (d, 0, 0, 0)), full((2, D)),
                pl.BlockSpec((None, H, DH, DH), lambda s: (d, 0, 0, 0)), full((2, D)), full((2, D)),
                colonce, colonce, colonce, once((t_ctx, D))]
    args = [z, z3, cw, cb, wr, br, wi, bi, lam, view3(dh_lat), view3(hp_lat), view3(a_lat), hp_ctx]
    if last:
        in_specs += [colonce, once((t_ctx, D))]
        args += [view3(prev[0]), prev[1]]
    outs = pl.pallas_call(
        body, grid=(nblk + 1,), in_specs=in_specs,
        out_specs=(colspec, full((t_ctx, D)), full((H, DH, DH)), full((H, DH, DH)), full((16, D))),
        out_shape=(jax.ShapeDtypeStruct((rows, GRID_W, D), F32), jax.ShapeDtypeStruct((t_ctx, D), F32),
                   jax.ShapeDtypeStruct((H, DH, DH), F32), jax.ShapeDtypeStruct((H, DH, DH), F32),
                   jax.ShapeDtypeStruct((16, D), F32)),
        scratch_shapes=[pltpu.VMEM((t_ctx, D), F32), pltpu.VMEM((t_ctx, D), F32), pltpu.VMEM((8, D), F32),
                        pltpu.VMEM((rows, CB, D), F32), pltpu.VMEM((rows, CB, D), F32), pltpu.VMEM((CB, D), F32)],
        name=f"rglru_bwd{d}", compiler_params=_cp(("arbitrary",)))(*args)
    return (outs[0].reshape(t_lat, D), outs[1]) + tuple(outs[2:])


def _merge(o_f, o_b, h_f, h_b, z, x_all, tgt, mod, norm_g, ln_g, ln_b, p_a, p_b, w_out, t_lat):
    tm = 256
    nt = t_lat // tm

    def body(of_ref, ob_ref, hf_ref, hb_ref, z4_ref, z6_ref, z7_ref, z8_ref, x_ref, t_ref, mod_ref, ng_ref,
             lg_ref, lb_ref, pa_ref, pb_ref, wo_ref,
             do_ref, dh_ref, dz4_ref, dz6_ref, dz7_ref, dz8_ref, gx_ref,
             y_ref, dout_ref, oa_ref, dpa_ref, obv_ref, dpb_ref, acc_ref):
        i = pl.program_id(0)
        lat = i < nt
        latf = lat.astype(F32)

        @pl.when(i == 0)
        def _():
            acc_ref[...] = jnp.zeros_like(acc_ref)

        def per_head(v):
            return jnp.concatenate(
                [jnp.broadcast_to(jnp.mean(v[:, h * DH:(h + 1) * DH], axis=-1, keepdims=True), (tm, DH))
                 for h in range(H)], axis=1)

        gt = mod_ref[0:1, 2 * D:3 * D]
        gfull = jnp.concatenate([ng_ref[...]] * H, axis=1)
        o = of_ref[...] + ob_ref[...]
        rinv = lax.rsqrt(per_head(o * o) + RMS_EPS)
        n = o * rinv
        na = n * gfull
        z4 = z4_ref[...]
        s4 = _sig(z4)
        silu4 = z4 * s4
        oa = na * silu4
        z6 = z6_ref[...]
        s6 = _sig(z6)
        silu6 = z6 * s6
        hsum = hf_ref[...] + hb_ref[...]
        obv = hsum * silu6
        pa = _dot(oa, pa_ref[...])
        pb = _dot(obv, pb_ref[...])
        s7 = _sig(z7_ref[...])
        s8 = _sig(z8_ref[...])
        y = s7 * pa + s8 * pb
        out = _dot(y, wo_ref[...])
        pre = ALPHA * x_ref[...] + gt * out
        mu = jnp.mean(pre, axis=-1, keepdims=True)
        xc = pre - mu
        rstd = lax.rsqrt(jnp.mean(xc * xc, axis=-1, keepdims=True) + LN_EPS)
        xhat = xc * rstd
        lg = lg_ref[...]
        diff = xhat * lg + lb_ref[...] - t_ref[...]
        acc_ref[8:9, :] += _colsum(diff * diff) * (0.5 / D * latf)
        dxn = diff * (1.0 / D)
        acc_ref[1:2, :] += _colsum(dxn * xhat) * latf
        acc_ref[2:3, :] += _colsum(dxn) * latf
        dxhat = dxn * lg
        dpre = rstd * (dxhat - jnp.mean(dxhat, axis=-1, keepdims=True)
                       - xhat * jnp.mean(dxhat * xhat, axis=-1, keepdims=True))
        gx_ref[...] = ALPHA * dpre
        acc_ref[0:1, :] += _colsum(dpre * out) * latf
        dout = dpre * gt
        dy = _dot_nt(dout, wo_ref[...])
        dpa = dy * s7
        dpb = dy * s8
        dz7 = dy * pa * (s7 * (1.0 - s7))
        dz8 = dy * pb * (s8 * (1.0 - s8))
        doa = _dot_nt(dpa, pa_ref[...])
        dob = _dot_nt(dpb, pb_ref[...])
        dh_ref[...] = dob * silu6
        dz6 = dob * hsum * (s6 * (1.0 + z6 * (1.0 - s6)))
        dna = doa * silu4
        dz4 = doa * na * (s4 * (1.0 + z4 * (1.0 - s4)))
        dng = _colsum(dna * n)
        acc_ref[7:8, 0:DH] += sum(dng[:, h * DH:(h + 1) * DH] for h in range(H)) * latf
        dn = dna * gfull
        do_ref[...] = rinv * (dn - n * per_head(dn * n))
        acc_ref[3:4, :] += _colsum(dz4) * latf
        acc_ref[4:5, :] += _colsum(dz6) * latf
        acc_ref[5:6, :] += _colsum(dz7) * latf
        acc_ref[6:7, :] += _colsum(dz8) * latf
        dz4_ref[...] = (dz4 * latf).astype(BF)
        dz6_ref[...] = (dz6 * latf).astype(BF)
        dz7_ref[...] = (dz7 * latf).astype(BF)
        dz8_ref[...] = (dz8 * latf).astype(BF)
        y_ref[...] = y.astype(BF)
        dout_ref[...] = dout.astype(BF)
        oa_ref[...] = oa.astype(BF)
        dpa_ref[...] = dpa.astype(BF)
        obv_ref[...] = obv.astype(BF)
        dpb_ref[...] = dpb.astype(BF)

        @pl.when(i == nt - 1)
        def _():
            acc_ref[9:10, :] = jnp.broadcast_to(jnp.sum(acc_ref[8:9, :], axis=-1, keepdims=True), (1, D))

    m = x_all.shape[0]
    lrow = lambda i: jnp.minimum(i, nt - 1)
    row = pl.BlockSpec((tm, D), lambda i: (lrow(i), 0))
    allrow = pl.BlockSpec((tm, D), lambda i: (i, 0))
    zs = lambda cb: pl.BlockSpec((tm, D), lambda i: (lrow(i), cb))
    full = lambda shp: pl.BlockSpec(shp, lambda i: (0,) * len(shp))
    wfull = pl.BlockSpec((D, D), lambda i: (0, 0), pipeline_mode=pl.Buffered(1))
    f32o = jax.ShapeDtypeStruct((t_lat, D), F32)
    bfo = jax.ShapeDtypeStruct((t_lat, D), BF)
    bfall = jax.ShapeDtypeStruct((m, D), BF)
    return pl.pallas_call(
        body, grid=(m // tm,),
        in_specs=[row, row, row, row, zs(4), zs(6), zs(7), zs(8), row, row, full((16, 3 * D)), full((1, DH)),
                  full((1, D)), full((1, D)), wfull, wfull, wfull],
        out_specs=(row, row) + (allrow,) * 4 + (row,) * 7 + (full((16, D)),),
        out_shape=(f32o, f32o, bfall, bfall, bfall, bfall, f32o, bfo, bfo, bfo, bfo, bfo, bfo,
                   jax.ShapeDtypeStruct((16, D), F32)),
        name="merge", compiler_params=_cp(("arbitrary",), VMEM_LIMIT_MERGE))(
            o_f, o_b, h_f, h_b, z, z, z, z, x_all, tgt, mod, norm_g, ln_g, ln_b, p_a, p_b, w_out)


def _wgrad(a, b, name):
    tm = 1024

    def body(a_ref, b_ref, o_ref):
        @pl.when(pl.program_id(0) == 0)
        def _():
            o_ref[...] = jnp.zeros_like(o_ref)
        o_ref[...] += _dot_tn(a_ref[...], b_ref[...])

    row = pl.BlockSpec((tm, D), lambda i: (i, 0))
    return pl.pallas_call(body, grid=(a.shape[0] // tm,), in_specs=[row, row],
                          out_specs=pl.BlockSpec((D, D), lambda i: (0, 0)),
                          out_shape=jax.ShapeDtypeStruct((D, D), F32), name=name,
                          compiler_params=_cp(("arbitrary",)))(a, b)


def _wgrad_in(u_all, dz):
    m = u_all.shape[0]
    assert m % 128 == 0
    tm = m // 8

    def body(u_ref, dz_ref, o_ref):
        @pl.when(pl.program_id(1) == 0)
        def _():
            o_ref[...] = jnp.zeros_like(o_ref)
        o_ref[0] += _dot_tn(u_ref[...], dz_ref[...])

    return pl.pallas_call(
        body, grid=(NSH, m // tm),
        in_specs=[pl.BlockSpec((tm, D), lambda n, i: (i, 0)),
                  pl.BlockSpec((tm, SHC), lambda n, i: (i, n))],
        out_specs=pl.BlockSpec((1, D, SHC), lambda n, i: (n, 0, 0)),
        out_shape=jax.ShapeDtypeStruct((NSH, D, SHC), F32), name="wgrad_in",
        compiler_params=_cp(("arbitrary", "arbitrary")))(u_all, dz)


def _du(dz, w_in_g, x_all, mod, gxres, n_lat_tiles, sums=()):
    m = x_all.shape[0]
    tm = 256
    nt = m // tm
    nct = nt - n_lat_tiles
    ns = len(sums)
    rblk = lambda i: jnp.where(i < nct, n_lat_tiles + i, i - nct)
    lblk = lambda i: jnp.maximum(i - nct, 0)

    def body(*refs):
        dz_ref, w_ref, x_ref, mod_ref, gr_ref = refs[:5]
        sum_refs = refs[5:5 + ns]
        gx_ref, dm_ref = refs[5 + ns:7 + ns]
        got_refs = refs[7 + ns:7 + 2 * ns]
        sems = refs[7 + 2 * ns:]
        i = pl.program_id(0)
        is_lat = i >= nct

        @pl.when(i == 0)
        def _():
            dm_ref[...] = jnp.zeros_like(dm_ref)
            if ns:
                for cp in _rs_chip_copies(sum_refs, got_refs, *sems):
                    cp.start()

        du = _dot_nt(dz_ref[:, 0:SHC], w_ref[0])
        for n in range(1, NSH):
            du = du + _dot_nt(dz_ref[:, n * SHC:(n + 1) * SHC], w_ref[n])
        sc = jnp.where(is_lat, mod_ref[0:1, D:2 * D], mod_ref[1:2, D:2 * D])
        dsh = _colsum(du)
        dsc = _colsum(du * x_ref[...])

        @pl.when(is_lat)
        def _():
            gx_ref[...] = du * (1.0 + sc) + gr_ref[...]
            dm_ref[0:1, 0:D] += dsh
            dm_ref[0:1, D:2 * D] += dsc

        @pl.when(jnp.logical_not(is_lat))
        def _():
            dm_ref[1:2, 0:D] += dsh
            dm_ref[1:2, D:2 * D] += dsc

        if ns:
            @pl.when(i == nt - 1)
            def _():
                for cp in _rs_chip_copies(sum_refs, got_refs, *sems):
                    cp.wait()

    outs = pl.pallas_call(
        body, grid=(nt,),
        in_specs=[pl.BlockSpec((tm, IN_COLS), lambda i: (rblk(i), 0)),
                  pl.BlockSpec((NSH, D, SHC), lambda i: (0, 0, 0), pipeline_mode=pl.Buffered(1)),
                  pl.BlockSpec((tm, D), lambda i: (rblk(i), 0)),
                  pl.BlockSpec((16, 3 * D), lambda i: (0, 0)),
                  pl.BlockSpec((tm, D), lambda i: (lblk(i), 0))] + [_ANY] * ns,
        out_specs=[pl.BlockSpec((tm, D), lambda i: (lblk(i), 0)),
                   pl.BlockSpec((8, 2 * D), lambda i: (0, 0))] + [_ANY] * ns,
        out_shape=[jax.ShapeDtypeStruct((n_lat_tiles * tm, D), F32), jax.ShapeDtypeStruct((8, 2 * D), F32)]
        + [jax.ShapeDtypeStruct((3,) + g.shape[1:], g.dtype) for g in sums],
        scratch_shapes=[pltpu.SemaphoreType.DMA((3 * ns,)), pltpu.SemaphoreType.DMA((3 * ns,))] if ns else [],
        name="du", compiler_params=_cp(("arbitrary",)))(dz, w_in_g, x_all, mod, gxres, *sums)
    return outs[0], outs[1], list(outs[2:])


def _row_tile(rows, cols):
    t = 8
    while t * 2 * cols * 4 <= (1 << 20) and rows % (t * 2) == 0:
        t *= 2
    return t


def _adamw_update(w_ref, g_ref, m_ref, v_ref, d_ref, nm_ref, nv_ref):
    gg = g_ref[...]
    m2 = ADAM_B1 * m_ref[...] + (1.0 - ADAM_B1) * gg
    v2 = ADAM_B2 * v_ref[...] + (1.0 - ADAM_B2) * (gg * gg)
    m_hat = m2 / (1.0 - ADAM_B1 ** ADAM_STEP)
    v_hat = v2 / (1.0 - ADAM_B2 ** ADAM_STEP)
    d_ref[...] = -ADAM_LR * (m_hat / (jnp.sqrt(v_hat) + ADAM_EPS) + ADAM_WD * w_ref[...])
    nm_ref[...] = m2
    nv_ref[...] = v2


def _adamw_many(ws, gs, ms, vs):
    n = len(ws)

    def body(*refs):
        for j in range(n):
            _adamw_update(*refs[4 * j:4 * j + 4], *refs[4 * n + 3 * j:4 * n + 3 * j + 3])

    args = [a for quad in zip(ws, gs, ms, vs) for a in quad]
    outs = pl.pallas_call(body, out_shape=[jax.ShapeDtypeStruct(w.shape, F32) for w in ws for _ in range(3)],
                          name="adamw_small", compiler_params=_cp())(*args)
    return outs[0::3], outs[1::3], outs[2::3]


def _adamw(w, g, m, v, name):
    rows, cols = w.shape
    tr = _row_tile(rows, cols)

    def body(*refs):
        _adamw_update(*refs)

    spec = pl.BlockSpec((tr, cols), lambda i: (i, 0))
    o = jax.ShapeDtypeStruct((rows, cols), F32)
    return pl.pallas_call(body, grid=(rows // tr,), in_specs=[spec] * 4, out_specs=(spec,) * 3,
                          out_shape=(o, o, o), name=name, compiler_params=_cp(("arbitrary",)))(w, g, m, v)


_ANY = pl.BlockSpec(memory_space=pl.ANY)


def _place():
    return lax.axis_index("x"), lax.axis_index("y"), lax.axis_index("c")


def _ag_copies(ins, outs, send, recv, fsend, frecv, lsem):
    x, y, c = _place()
    me = 2 * x + y
    chips = ((1 - x, y), (x, 1 - y), (1 - x, 1 - y))
    local, chip, hand = [], [], []
    for j in range(len(ins)):
        hr = ins[j].shape[0] // 2
        half = pl.ds(pl.multiple_of(c * hr, 8), hr)
        local.append(pltpu.make_async_copy(ins[j], outs[j].at[me], lsem.at[j]))
        for k, (px, py) in enumerate(chips):
            chip.append(pltpu.make_async_remote_copy(
                src_ref=ins[j].at[half, :], dst_ref=outs[j].at[me, half, :], send_sem=send.at[3 * j + k],
                recv_sem=recv.at[3 * j + k], device_id=(px, py, c), device_id_type=MESH))
            got = outs[j].at[2 * px + py, half, :]
            hand.append(pltpu.make_async_remote_copy(
                src_ref=got, dst_ref=got, send_sem=fsend.at[3 * j + k], recv_sem=frecv.at[3 * j + k],
                device_id=(x, y, 1 - c), device_id_type=MESH))
    return local, chip, hand


def _ag_sems(n):
    return [pltpu.SemaphoreType.DMA((3 * n,))] * 4 + [pltpu.SemaphoreType.DMA((n,))]


def _ag_finish(local, chip, hand, done=0):
    for k in range(done, len(chip)):
        chip[k].wait_recv()
        hand[k].start()
    for cp in chip:
        cp.wait_send()
    for k in range(done):
        hand[k].wait_send()
    for k in range(done, len(chip)):
        hand[k].wait_send()
        hand[k].wait_recv()
    for cp in local:
        cp.wait()


def _mod_tp(c8, c_ctx, w_mod_sh, b_mod_sh):
    mc = w_mod_sh.shape[1]

    def body(c8_ref, cctx_ref, w_ref, b_ref, mod_ref, cc_ref, cc_s, part_s, send1, recv1, send3, recv3):
        x, y, c = _place()
        me = 4 * x + 2 * y + c
        ms = 2 * x + y
        copies = []
        for k in range(1, 8):
            peer = (x ^ ((k >> 2) & 1), y ^ ((k >> 1) & 1), c ^ (k & 1))
            cp = pltpu.make_async_remote_copy(src_ref=c8_ref, dst_ref=cc_s.at[me], send_sem=send1.at[k],
                                              recv_sem=recv1.at[k], device_id=peer, device_id_type=MESH)
            cp.start()
            copies.append(cp)
        cc_s[me] = c8_ref[...]
        for cp in copies:
            cp.wait()
        cc_ref[...] = jnp.zeros_like(cc_ref)
        for j in range(8):
            cc_ref[j:j + 1, :] = cc_s[j, 0:1, :]
        cc_ref[8:9, :] = cctx_ref[...]
        v = cc_ref[...]
        part_s[ms] = _dot(v * _sig(v), w_ref[...]) + b_ref[...]
        copies = []
        for k in range(1, 4):
            peer = (x ^ ((k >> 1) & 1), y ^ (k & 1), c)
            cp = pltpu.make_async_remote_copy(src_ref=part_s.at[ms], dst_ref=part_s.at[ms], send_sem=send3.at[k],
                                              recv_sem=recv3.at[k], device_id=peer, device_id_type=MESH)
            cp.start()
            copies.append(cp)
        for cp in copies:
            cp.wait()
        for s in range(NSH):
            mod_ref[:, s * mc:(s + 1) * mc] = part_s[s]

    vm = pl.BlockSpec(memory_space=pltpu.VMEM)
    return pl.pallas_call(
        body, in_specs=[vm] * 4, out_specs=(vm, vm),
        out_shape=(jax.ShapeDtypeStruct((16, NSH * mc), F32), jax.ShapeDtypeStruct((16, D), F32)),
        scratch_shapes=[pltpu.VMEM((8, 8, D), F32), pltpu.VMEM((NSH, 16, mc), F32),
                        pltpu.SemaphoreType.DMA((8,)), pltpu.SemaphoreType.DMA((8,)),
                        pltpu.SemaphoreType.DMA((4,)), pltpu.SemaphoreType.DMA((4,))],
        name="mod_tp", compiler_params=_cp())(c8, c_ctx, w_mod_sh, b_mod_sh)


def _inproj_ag(x_all, mod, w_in_sh, b_in, narrow_sh, n_lat_tiles):
    m = x_all.shape[0]
    tm = 256
    nt = m // tm
    x_, y_ = lax.axis_index("x"), lax.axis_index("y")
    sids = jnp.stack([2 * x_ + y_, 2 * (1 - x_) + y_, 2 * x_ + 1 - y_, 2 * (1 - x_) + 1 - y_]).astype(jnp.int32)

    def body(sid_ref, x_ref, mod_ref, b_ref, wsh_ref, nsh_ref, z_ref, u_ref, wg_ref, ng_ref, w_s, u_s, *sems):
        n = pl.program_id(0)
        i = pl.program_id(1)
        rows = pl.ds(pl.multiple_of(i * tm, tm), tm)
        ag = ((wsh_ref, nsh_ref), (wg_ref, ng_ref)) + tuple(sems[:5])
        wsem = sems[5]

        def load(src):
            cp = pltpu.make_async_copy(src, w_s, wsem)
            cp.start()
            cp.wait()

        @pl.when((n == 0) & (i == 0))
        def _():
            local, chip, _ = _ag_copies(*ag)
            for cp in chip + local:
                cp.start()
            load(wsh_ref)

        for k in range(NSH - 1):
            @pl.when((n == k + 1) & (i == 0))
            def _():
                _, chip, hand = _ag_copies(*ag)
                chip[k].wait_recv()
                hand[k].start()
                hand[k].wait_recv()
                load(wg_ref.at[sid_ref[k + 1]])

        @pl.when(n == 0)
        def _():
            u, _ = _modulate(x_ref, mod_ref, i < n_lat_tiles)
            u_s[rows, :] = u.astype(BF)
            u_ref[...] = u.astype(BF)

        z_ref[...] = _dot(u_s[rows, :], w_s[...]) + b_ref[...]

        @pl.when((n == NSH - 1) & (i == nt - 1))
        def _():
            _ag_finish(*_ag_copies(*ag), done=NSH - 1)

    first = lambda n, i: jnp.where(n == 0, i, nt - 1)
    outs = pl.pallas_call(
        body, grid_spec=pltpu.PrefetchScalarGridSpec(
            num_scalar_prefetch=1, grid=(NSH, nt),
            in_specs=[pl.BlockSpec((tm, D), lambda n, i, sid: (first(n, i), 0)),
                      pl.BlockSpec((16, 3 * D), lambda n, i, sid: (0, 0)),
                      pl.BlockSpec((1, SHC), lambda n, i, sid: (0, sid[n])), _ANY, _ANY],
            out_specs=[pl.BlockSpec((tm, SHC), lambda n, i, sid: (i, sid[n])),
                       pl.BlockSpec((tm, D), lambda n, i, sid: (first(n, i), 0)), _ANY, _ANY],
            scratch_shapes=[pltpu.VMEM((D, SHC), BF), pltpu.VMEM((m, D), BF)] + _ag_sems(2)
            + [pltpu.SemaphoreType.DMA]),
        out_shape=[jax.ShapeDtypeStruct((m, IN_COLS), F32), jax.ShapeDtypeStruct((m, D), BF),
                   jax.ShapeDtypeStruct((NSH,) + w_in_sh.shape, BF),
                   jax.ShapeDtypeStruct((NSH,) + narrow_sh.shape, narrow_sh.dtype)],
        name="inproj_ag", compiler_params=_cp(("arbitrary", "arbitrary")))(sids, x_all, mod, b_in, w_in_sh, narrow_sh)
    return outs


def _rs_sibling(grads):
    n = len(grads)

    def body(*refs):
        ins, got = refs[:n], refs[n:2 * n]
        send, recv = refs[2 * n:]
        x, y, c = _place()
        copies = []
        for j in range(n):
            hr = ins[j].shape[1] // 2
            for s in range(NSH):
                give = ins[j].at[s, pl.ds(pl.multiple_of((1 - c) * hr, 8), hr), :]
                cp = pltpu.make_async_remote_copy(src_ref=give, dst_ref=got[j].at[s], send_sem=send.at[NSH * j + s],
                                                  recv_sem=recv.at[NSH * j + s], device_id=(x, y, 1 - c),
                                                  device_id_type=MESH)
                cp.start()
                copies.append(cp)
        for cp in copies:
            cp.wait()

    half = [jax.ShapeDtypeStruct((NSH, g.shape[1] // 2, g.shape[2]), F32) for g in grads]
    return pl.pallas_call(
        body, in_specs=[_ANY] * n, out_specs=[_ANY] * n, out_shape=half,
        scratch_shapes=[pltpu.SemaphoreType.DMA((NSH * n,)), pltpu.SemaphoreType.DMA((NSH * n,))],
        name="rs_sibling")(*grads)


def _core_vec():
    return lax.axis_index("c").astype(jnp.int32).reshape(1)


def _rs_add1(g, got, name):
    _, r, cols = g.shape
    hr = r // 2
    tr = _row_tile(hr, cols)
    nb = hr // tr

    def body(c_ref, g_ref, got_ref, o_ref):
        o_ref[...] = (g_ref[...] + got_ref[...]).astype(BF)

    spec = pl.BlockSpec((1, tr, cols), lambda s, i, c_ref: (s, i, 0))
    return pl.pallas_call(
        body, grid_spec=pltpu.PrefetchScalarGridSpec(
            num_scalar_prefetch=1, grid=(NSH, nb),
            in_specs=[pl.BlockSpec((1, tr, cols), lambda s, i, c_ref: (s, c_ref[0] * nb + i, 0)), spec],
            out_specs=spec),
        out_shape=jax.ShapeDtypeStruct((NSH, hr, cols), BF), name=name,
        compiler_params=_cp(("arbitrary", "arbitrary")))(_core_vec(), g, got)


def _rs_add2(sums, got, name):
    _, hr, cols = sums.shape
    tr = _row_tile(hr, cols)
    nb = hr // tr
    place = jnp.stack([2 * lax.axis_index("x") + lax.axis_index("y"), lax.axis_index("c")]).astype(jnp.int32)

    def body(p_ref, s_ref, got_ref, o_ref):
        f = lambda v: v.astype(F32)
        o_ref[...] = f(s_ref[0]) + f(got_ref[0]) + f(got_ref[1]) + f(got_ref[2])

    return pl.pallas_call(
        body, grid_spec=pltpu.PrefetchScalarGridSpec(
            num_scalar_prefetch=1, grid=(nb,),
            in_specs=[pl.BlockSpec((1, tr, cols), lambda i, p_ref: (p_ref[0], i, 0)),
                      pl.BlockSpec((3, tr, cols), lambda i, p_ref: (0, i, 0))],
            out_specs=pl.BlockSpec((tr, cols), lambda i, p_ref: (p_ref[1] * nb + i, 0))),
        out_shape=jax.ShapeDtypeStruct((2 * hr, cols), F32), name=name,
        compiler_params=_cp(("arbitrary",)))(place, sums, got)


def _rs_chip_copies(ins, got, send, recv):
    x, y, c = _place()
    peers = ((1 - x, y), (x, 1 - y), (1 - x, 1 - y))
    return [pltpu.make_async_remote_copy(src_ref=ins[j].at[2 * px + py], dst_ref=got[j].at[k],
                                         send_sem=send.at[3 * j + k], recv_sem=recv.at[3 * j + k],
                                         device_id=(px, py, c), device_id_type=MESH)
            for j in range(len(ins)) for k, (px, py) in enumerate(peers)]


def _ag_sibling(fulls):
    n = len(fulls)
    nck = 4

    def body(*refs):
        outs = refs[n:2 * n]
        send, recv = refs[2 * n:]
        x, y, c = _place()
        copies = []
        for j in range(n):
            qr = outs[j].shape[0] // (2 * nck)
            for k in range(nck):
                rows = outs[j].at[pl.ds(pl.multiple_of((c * nck + k) * qr, 8), qr), :]
                cp = pltpu.make_async_remote_copy(src_ref=rows, dst_ref=rows, send_sem=send.at[nck * j + k],
                                                  recv_sem=recv.at[nck * j + k], device_id=(x, y, 1 - c),
                                                  device_id_type=MESH)
                cp.start()
                copies.append(cp)
        for cp in copies:
            cp.wait()

    return pl.pallas_call(
        body, in_specs=[_ANY] * n, out_specs=[_ANY] * n,
        out_shape=[jax.ShapeDtypeStruct(f.shape, F32) for f in fulls],
        input_output_aliases={j: j for j in range(n)},
        scratch_shapes=[pltpu.SemaphoreType.DMA((nck * n,)), pltpu.SemaphoreType.DMA((nck * n,))],
        name="ag_sibling")(*fulls)


def _allreduce_small(buf):
    rows = buf.shape[0]
    pr = rows // 8

    def body(in_ref, out_ref, stage, send1, recv1, send2, recv2):
        x, y, c = _place()
        me = 4 * x + 2 * y + c

        def peer(k):
            kx, ky, kc = (k >> 2) & 1, (k >> 1) & 1, k & 1
            return (x ^ kx, y ^ ky, c ^ kc)

        def piece(ref, idx):
            return ref.at[pl.ds(pl.multiple_of(idx * pr, 8), pr), :]

        copies = []
        for k in range(1, 8):
            px, py, pc = peer(k)
            cp = pltpu.make_async_remote_copy(src_ref=piece(in_ref, 4 * px + 2 * py + pc), dst_ref=stage.at[k],
                                              send_sem=send1.at[k], recv_sem=recv1.at[k],
                                              device_id=(px, py, pc), device_id_type=MESH)
            cp.start()
            copies.append(cp)
        for cp in copies:
            cp.wait()
        acc = piece(in_ref, me)[...]
        for k in range(1, 8):
            acc = acc + stage[k]
        piece(out_ref, me)[...] = acc
        copies = []
        for k in range(1, 8):
            cp = pltpu.make_async_remote_copy(src_ref=piece(out_ref, me), dst_ref=piece(out_ref, me),
                                              send_sem=send2.at[k], recv_sem=recv2.at[k],
                                              device_id=peer(k), device_id_type=MESH)
            cp.start()
            copies.append(cp)
        for cp in copies:
            cp.wait()

    vm = pl.BlockSpec(memory_space=pltpu.VMEM)
    return pl.pallas_call(
        body, in_specs=[vm], out_specs=vm, out_shape=jax.ShapeDtypeStruct((rows, D), F32),
        scratch_shapes=[pltpu.VMEM((8, pr, D), F32)] + [pltpu.SemaphoreType.DMA((8,))] * 4,
        name="allreduce_small", compiler_params=_cp())(buf)


def _rows(a):
    flat = a.reshape(-1)
    pad = (-flat.shape[0]) % D
    if pad:
        flat = jnp.concatenate([flat, jnp.zeros((pad,), flat.dtype)])
    return flat.reshape(-1, D)


def _pad_rows(a, mult):
    pad = (-a.shape[0]) % mult
    return jnp.concatenate([a, jnp.zeros((pad, a.shape[1]), a.dtype)]) if pad else a


def _local_step(x, c, ctx, c_ctx, tgt, me, shard, sh, b_mod, b_in, norm_g, cb, wr, wi, ln_g, ln_b):
    t_lat, t_ctx = x.shape[0], ctx.shape[0]
    nlt = t_lat // 256
    nlb, ncb = t_lat // RB, t_ctx // RB
    mc = 3 * D // NSH
    mod_all, cc_all = _mod_tp(jnp.zeros((8, D), F32).at[0].set(c), c_ctx.reshape(1, D), sh["w_mod"],
                              lax.dynamic_slice_in_dim(b_mod, shard * mc, mc, axis=1))
    mod = jnp.zeros((16, 3 * D), F32).at[0].set(mod_all[me]).at[1].set(mod_all[8])
    cc = jnp.zeros((16, D), F32).at[0].set(c).at[1].set(c_ctx)
    x_all = jnp.concatenate([x, ctx], axis=0)
    z, u_all, w_in_g, nar = _inproj_ag(x_all, mod, sh["w_in"], b_in, sh["narrow"], nlt)
    nar = jnp.transpose(nar, (1, 0, 2)).reshape(-1, D)
    lbl, cw, br, bi, lam = nar[0:4].reshape(2, 2, D), nar[4:8], nar[8:10], nar[10:12], nar[12:14]
    o0, st0, (w_mod_g, p_a, p_b, w_out) = _gla_fwd(z, lbl, 0, nlb, ncb,
                                                   gather=[sh[k] for k in ("w_mod", "p_a", "p_b", "w_out")])
    p_a, p_b, w_out = p_a.reshape(D, D), p_b.reshape(D, D), w_out.reshape(D, D)
    o1, st1, _ = _gla_fwd(z, lbl, 1, nlb, ncb)
    h0, hp0, a0, hpc0 = _rglru_fwd(z, cw, cb, wr, br, wi, bi, lam, 0, t_lat, t_ctx)
    h1, hp1, a1, hpc1 = _rglru_fwd(z, cw, cb, wr, br, wi, bi, lam, 1, t_lat, t_ctx)
    (do, dh, dz4, dz6, dz7, dz8, gxres, y, dout, oa, dpa, obv, dpb, acc) = _merge(
        o0, o1, h0, h1, z, x_all, tgt, mod, norm_g, ln_g, ln_b, p_a, p_b, w_out, t_lat)
    gp_a = _wgrad(oa, dpa, "wgrad_pa")
    gp_b = _wgrad(obv, dpb, "wgrad_pb")
    gw_out = _wgrad(y, dout, "wgrad_wout")
    dxc_lat, dxc_ctx, dwr0, dwi0, sb0 = _rglru_bwd(z, cw, cb, wr, br, wi, bi, lam, dh, hp0, a0, hpc0, 0, t_lat, t_ctx)
    dz5_lat, dz5_ctx, dwr1, dwi1, sb1 = _rglru_bwd(z, cw, cb, wr, br, wi, bi, lam, dh, hp1, a1, hpc1, 1, t_lat, t_ctx,
                                                   prev=(dxc_lat, dxc_ctx))
    dq0, dv0, dz1, sa0 = _gla_bwd(z, lbl, do, st0, 0, nlb, ncb)
    dz0, dz3, dz2, sa1 = _gla_bwd(z, lbl, do, st1, 1, nlb, ncb, prev=(dq0, dv0))
    dz = jnp.concatenate([dz0, dz1, dz2, dz3, dz4, jnp.concatenate([dz5_lat, dz5_ctx], axis=0).astype(BF),
                          dz6, dz7, dz8], axis=1)
    big = dict(w_in=_wgrad_in(u_all, dz), p_a=gp_a.reshape(NSH, D // NSH, D),
               p_b=gp_b.reshape(NSH, D // NSH, D), w_out=gw_out.reshape(NSH, D // NSH, D))
    grads = [big[k] for k in _RS]
    sums = [_rs_add1(g, b, f"rs_add1_{j}") for j, (g, b) in enumerate(zip(grads, _rs_sibling(grads)))]
    gx, dm, got = _du(dz, w_in_g, x_all, mod, gxres, nlt, sums)
    fulls = [_rs_add2(a, b, f"rs_add2_{j}") for j, (a, b) in enumerate(zip(sums, got))]
    big = dict(zip(_RS, _ag_sibling(fulls)))
    dmod = jnp.zeros((16, 3 * D), F32).at[0:2, 0:2 * D].set(dm[0:2]).at[0, 2 * D:].set(acc[0])
    dcc = _mod_bwd(cc, dmod, w_mod_g)
    small = dict(
        c_ctx=dcc[1:2], b_mod=(dmod[0] + dmod[1]).reshape(3, D),
        b_in=jnp.stack([sa1[2], sa0[0], sa1[0], sa1[3], acc[3], sb1[4], acc[4], acc[5], acc[6]]),
        lb_logits=jnp.stack([sa0[1], sa1[1], -sa0[1], -sa1[1]]),
        norm_a_g=acc[7:8], conv_w=sb1[8:12], conv_b=sb1[3:4],
        w_r=jnp.stack([dwr0, dwr1]).reshape(-1, D), w_i=jnp.stack([dwi0, dwi1]).reshape(-1, D),
        b_r=jnp.stack([sb0[0], sb1[0]]), b_i=jnp.stack([sb0[1], sb1[1]]), lam=jnp.stack([sb0[2], sb1[2]]),
        ln_g=acc[1:2], ln_b=acc[2:3])
    return acc[9, 0], gx, big, small, dmod, cc_all


_RS =("w_in", "p_a", "p_b", "w_out")
_SMALL =("c_ctx", "b_mod", "b_in", "lb_logits", "norm_a_g", "conv_w", "conv_b", "w_r", "w_i", "b_r", "b_i", "lam",
          "ln_g", "ln_b")
_BIG = ("w_mod", "w_in", "p_a", "p_b", "w_out")
_COL_SHARDED = ("lb_logits", "conv_w", "b_r", "b_i", "lam")
_WEIGHTS = ("c_ctx", "w_mod", "b_mod", "w_in", "b_in", "lb_logits", "norm_a_g", "conv_w", "conv_b", "w_r", "b_r", "w_i",
            "b_i", "lam", "p_a", "p_b", "w_out", "ln_g", "ln_b")


def kernel(x, c, ctx, c_ctx, w_mod, b_mod, w_in, b_in, lb_logits, norm_a_g, conv_w, conv_b, w_r, b_r, w_i, b_i, lam, p_a, p_b, w_out, ln_g, ln_b, loss_target, m_c_ctx, m_w_mod, m_b_mod, m_w_in, m_b_in, m_lb_logits, m_norm_a_g, m_conv_w, m_conv_b, m_w_r, m_b_r, m_w_i, m_b_i, m_lam, m_p_a, m_p_b, m_w_out, m_ln_g, m_ln_b, v_c_ctx, v_w_mod, v_b_mod, v_w_in, v_b_in, v_lb_logits, v_norm_a_g, v_conv_w, v_conv_b, v_w_r, v_b_r, v_w_i, v_b_i, v_lam, v_p_a, v_p_b, v_w_out, v_ln_g, v_ln_b):
    w = dict(c_ctx=c_ctx, w_mod=w_mod, b_mod=b_mod, w_in=w_in, b_in=b_in, lb_logits=lb_logits, norm_a_g=norm_a_g,
             conv_w=conv_w, conv_b=conv_b, w_r=w_r, b_r=b_r, w_i=w_i, b_i=b_i, lam=lam, p_a=p_a, p_b=p_b, w_out=w_out,
             ln_g=ln_g, ln_b=ln_b)
    m = dict(c_ctx=m_c_ctx, w_mod=m_w_mod, b_mod=m_b_mod, w_in=m_w_in, b_in=m_b_in, lb_logits=m_lb_logits,
             norm_a_g=m_norm_a_g, conv_w=m_conv_w, conv_b=m_conv_b, w_r=m_w_r, b_r=m_b_r, w_i=m_w_i, b_i=m_b_i,
             lam=m_lam, p_a=m_p_a, p_b=m_p_b, w_out=m_w_out, ln_g=m_ln_g, ln_b=m_ln_b)
    v = dict(c_ctx=v_c_ctx, w_mod=v_w_mod, b_mod=v_b_mod, w_in=v_w_in, b_in=v_b_in, lb_logits=v_lb_logits,
             norm_a_g=v_norm_a_g, conv_w=v_conv_w, conv_b=v_conv_b, w_r=v_w_r, b_r=v_b_r, w_i=v_w_i, b_i=v_b_i,
             lam=v_lam, p_a=v_p_a, p_b=v_p_b, w_out=v_w_out, ln_g=v_ln_g, ln_b=v_ln_b)
    shard = 2 * lax.axis_index("x") + lax.axis_index("y")
    cs = D // NSH

    sh = {k: w[k][0].astype(BF) for k in _BIG}
    sh["narrow"] = _pad_rows(jnp.concatenate([lb_logits.reshape(4, cs), conv_w[0], b_r[0], b_i[0], lam[0]], axis=0), 8)
    me = 2 * shard + lax.axis_index("c")
    loss, gx, big, small, dmod, cc_all = _local_step(
        x[0], c[0], ctx[0], c_ctx, loss_target[0], me, shard, sh, b_mod, b_in, norm_a_g, conv_b, w_r[0], w_i[0],
        ln_g, ln_b)
    loss = lax.psum(loss, ("x", "y", "c"))

    dmod_rows = jnp.zeros((16, 3 * D), F32).at[me].set(dmod[0]).at[8].set(dmod[1]).reshape(48, D)
    sizes = [small[k].shape[0] for k in _SMALL]
    red = _allreduce_small(_pad_rows(jnp.concatenate([_pad_rows(small[k], 8) for k in _SMALL] + [dmod_rows],
                                                     axis=0), 64))
    grads = {}
    off = 0
    for k, n in zip(_SMALL, sizes):
        g = red[off:off + n]
        off += n + (-n) % 8
        if k == "norm_a_g":
            g = g[:, :DH]
        if k in _COL_SHARDED:
            g = lax.dynamic_slice_in_dim(g, shard * cs, cs, axis=1)
        grads[k] = g.reshape(w[k].shape)
    for k in _RS:
        grads[k] = big[k].reshape(w[k].shape)
    dmod_all = red[off:off + 48].reshape(16, 3 * D)
    mc = 3 * D // NSH
    grads["w_mod"] = _wmod_grad(cc_all, lax.dynamic_slice_in_dim(dmod_all, shard * mc, mc, axis=1)).reshape(
        w["w_mod"].shape)

    delta, new_m, new_v = {}, {}, {}
    for k in _BIG:
        shp = w[k].shape
        two = lambda a: a.reshape(shp[-2], shp[-1])
        d_, m_, v_ = _adamw(two(w[k]), two(grads[k]), two(m[k]), two(v[k]), f"adamw_{k}")
        delta[k], new_m[k], new_v[k] = d_.reshape(shp), m_.reshape(shp), v_.reshape(shp)
    d_, m_, v_ = _adamw_many(*[[t[k] for k in _SMALL] for t in (w, grads, m, v)])
    delta.update(zip(_SMALL, d_))
    new_m.update(zip(_SMALL, m_))
    new_v.update(zip(_SMALL, v_))

    return (loss, gx[None], *[grads[k] for k in _WEIGHTS], *[delta[k] for k in _WEIGHTS],
            *[new_m[k] for k in _WEIGHTS], *[new_v[k] for k in _WEIGHTS])
```

```python
import functools

import jax
import jax.numpy as jnp
from jax import lax
from jax.experimental import pallas as pl
from jax.experimental.pallas import tpu as pltpu

F32 = jnp.float32
BF = jnp.bfloat16
MESH = pl.DeviceIdType.MESH

D = 1024
H = 8
DH = 128
CH = 64
RB = 256
NCK = RB // CH
GRID_W = 64
CB = 8
RCH = 16
IN_COLS = 9 * D
NSH = 4
SHC = IN_COLS // NSH
RG_C = 8.0
ALPHA = 2.0 ** 0.25
LN_EPS = 1e-5
RMS_EPS = 1e-6
Q_SCALE = DH ** -0.5
ADAM_LR, ADAM_B1, ADAM_B2, ADAM_EPS, ADAM_WD, ADAM_STEP = 0.001, 0.9, 0.999, 1e-08, 0.01, 10
VMEM_LIMIT = 56 * 1024 * 1024
VMEM_LIMIT_MERGE = 60 * 1024 * 1024


def _cp(sem=None, vmem=VMEM_LIMIT):
    return pltpu.CompilerParams(dimension_semantics=sem, vmem_limit_bytes=vmem)


def _sig(x):
    return 0.5 * jnp.tanh(0.5 * x) + 0.5


def _dot(a, b):
    return jnp.dot(a.astype(BF), b.astype(BF), preferred_element_type=F32)


def _dot_nt(a, b):
    return lax.dot_general(a.astype(BF), b.astype(BF), (((1,), (1,)), ((), ())), preferred_element_type=F32)


def _dot_tn(a, b):
    return lax.dot_general(a.astype(BF), b.astype(BF), (((0,), (0,)), ((), ())), preferred_element_type=F32)


def _colsum(v):
    return jnp.sum(v, axis=0, keepdims=True)


def _mod_bwd(cc, dmod, w_mod_g):
    def body(cc_ref, dm_ref, w_ref, dcc_ref):
        v = cc_ref[...]
        sg = _sig(v)
        ds = jnp.zeros((16, D), F32)
        for k in range(NSH):
            ds = ds + _dot_nt(dm_ref[:, k * 768:(k + 1) * 768], w_ref[k])
        dcc_ref[...] = ds * (sg * (1.0 + v * (1.0 - sg)))
    return pl.pallas_call(body, out_shape=jax.ShapeDtypeStruct((16, D), F32),
                          name="mod_bwd", compiler_params=_cp())(cc, dmod, w_mod_g)


def _wmod_grad(cc, dmod_cols):
    def body(cc_ref, dm_ref, dw_ref):
        v = cc_ref[...]
        dw_ref[...] = _dot_tn(v * _sig(v), dm_ref[...])
    return pl.pallas_call(body, out_shape=jax.ShapeDtypeStruct((D, dmod_cols.shape[1]), F32),
                          name="wmod_grad", compiler_params=_cp())(cc, dmod_cols)


def _modulate(x_ref, mod_ref, is_lat):
    sh = jnp.where(is_lat, mod_ref[0:1, 0:D], mod_ref[1:2, 0:D])
    sc = jnp.where(is_lat, mod_ref[0:1, D:2 * D], mod_ref[1:2, D:2 * D])
    return x_ref[...] * (1.0 + sc) + sh, sc


def _gla_mask(d, n):
    row = lax.broadcasted_iota(jnp.int32, (n, n), 0)
    col = lax.broadcasted_iota(jnp.int32, (n, n), 1)
    same = (row // CH) == (col // CH)
    return same & ((row <= col) if d else (row >= col))


def _chunk_cumsum(v, rev):
    n = v.shape[0]
    pos = lax.broadcasted_iota(jnp.int32, v.shape, 0) & (CH - 1)
    s = 1
    while s < CH:
        if rev:
            v = v + jnp.where(pos < CH - s, pltpu.roll(v, n - s, 0), 0.0)
        else:
            v = v + jnp.where(pos >= s, pltpu.roll(v, s, 0), 0.0)
        s *= 2
    return v


def _chunk_rows(c):
    return slice(c * CH, (c + 1) * CH)


def _gla_features(zq, zf, lb, d):
    sq = _sig(zq)
    q = zq * sq * Q_SCALE
    sf = _sig(zf)
    f = lb + (1.0 - lb) * sf
    k = 1.0 - f
    g = _chunk_cumsum(jnp.log(f), d)
    last = 0 if d else CH - 1
    gls = [g[c * CH + last:c * CH + last + 1, :] for c in range(NCK)]
    glb = jnp.concatenate([jnp.broadcast_to(gl, (CH, D)) for gl in gls], axis=0)
    eg, eig, eeg = jnp.exp(g), jnp.exp(-g), jnp.exp(glb - g)
    decs = [jnp.exp(gl) for gl in gls]
    return sq, sf, f, q * eg, k * eig, k * eeg, eg, eig, eeg, decs


def _lower_bound(lbl_ref, d):
    return _sig(lbl_ref[0, d:d + 1, :] - lbl_ref[1, d:d + 1, :])


def _gla_rb(d, nlb, ncb):
    nrb = nlb + ncb
    if d:
        return lambda s: nrb - 1 - s
    return lambda s: jnp.where(s < ncb, nlb + s, s - ncb)


def _gla_fwd(z, lbl, d, nlb, ncb, gather=()):
    m = z.shape[0]
    nrb = nlb + ncb
    rb = _gla_rb(d, nlb, ncb)
    ng = len(gather)

    def body(*refs):
        q_ref, f_ref, v_ref, lbl_ref = refs[:4]
        ag_in = refs[4:4 + ng]
        o_ref, st_ref = refs[4 + ng:6 + ng]
        ag_out = refs[6 + ng:6 + 2 * ng]
        S = refs[6 + 2 * ng]
        ag_sems = refs[7 + 2 * ng:]
        s = pl.program_id(0)

        @pl.when(s == 0)
        def _():
            S[...] = jnp.zeros_like(S)
            if ng:
                local, chip, _ = _ag_copies(ag_in, ag_out, *ag_sems)
                for cp in chip + local:
                    cp.start()

        lb = _lower_bound(lbl_ref, d)
        mb = _gla_mask(d, CH)
        _, _, _, qd, ki, ke, _, _, _, decs = _gla_features(q_ref[...], f_ref[...], lb, d)
        qd, ki, ke, v = qd.astype(BF), ki.astype(BF), ke.astype(BF), v_ref[...].astype(BF)
        order = range(NCK - 1, -1, -1) if d else range(NCK)
        for h in range(H):
            hs = slice(h * DH, (h + 1) * DH)
            intra, upd = {}, {}
            for c in range(NCK):
                rs = _chunk_rows(c)
                a = jnp.where(mb, _dot_nt(qd[rs, hs], ki[rs, hs]), 0.0)
                intra[c] = _dot(a, v[rs, hs])
                upd[c] = _dot_tn(v[rs, hs], ke[rs, hs])
            st = S[h]
            for c in order:
                rs = _chunk_rows(c)
                st_ref[c, h] = st
                o_ref[rs, hs] = intra[c] + _dot_nt(qd[rs, hs], st)
                st = st * decs[c][:, hs] + upd[c]
            S[h] = st

        if ng:
            @pl.when(s == nrb - 1)
            def _():
                _ag_finish(*_ag_copies(ag_in, ag_out, *ag_sems))

    def zspec(cb):
        return pl.BlockSpec((RB, D), lambda s: (rb(s), cb))

    outs = pl.pallas_call(
        body, grid=(nrb,),
        in_specs=[zspec(0), zspec(1 + d), zspec(3), pl.BlockSpec((2, 2, D), lambda s: (0, 0, 0))] + [_ANY] * ng,
        out_specs=[pl.BlockSpec((RB, D), lambda s: (rb(s), 0)),
                   pl.BlockSpec((NCK, H, DH, DH), lambda s: (rb(s), 0, 0, 0))] + [_ANY] * ng,
        out_shape=[jax.ShapeDtypeStruct((m, D), F32), jax.ShapeDtypeStruct((m // CH, H, DH, DH), F32)]
        + [jax.ShapeDtypeStruct((NSH,) + g.shape, g.dtype) for g in gather],
        scratch_shapes=[pltpu.VMEM((H, DH, DH), F32)] + (_ag_sems(ng) if ng else []),
        name=f"gla_fwd{d}", compiler_params=_cp(("arbitrary",)))(z, z, z, lbl, *gather)
    return outs[0], outs[1], list(outs[2:])


def _gla_bwd(z, lbl, do_lat, states, d, nlb, ncb, prev=None):
    m = z.shape[0]
    nrb = nlb + ncb
    fwd_rb = _gla_rb(d, nlb, ncb)
    rb = lambda s: fwd_rb(nrb - 1 - s)
    last = prev is not None

    def body(*refs):
        if last:
            q_ref, f_ref, v_ref, lbl_ref, do_ref, st_ref, pq_ref, pv_ref, o0_ref, o1_ref, o2_ref, sum_ref, dS = refs
        else:
            q_ref, f_ref, v_ref, lbl_ref, do_ref, st_ref, o0_ref, o1_ref, o2_ref, sum_ref, dS = refs
        s = pl.program_id(0)
        is_lat = rb(s) < nlb

        @pl.when(s == 0)
        def _():
            dS[...] = jnp.zeros_like(dS)
            sum_ref[...] = jnp.zeros_like(sum_ref)

        lb = _lower_bound(lbl_ref, d)
        mb = _gla_mask(d, RB)
        zq = q_ref[...]
        sq, sf, f, qd, ki, ke, eg, eig, eeg, decs = _gla_features(zq, f_ref[...], lb, d)
        qdb, kib, keb, vb = qd.astype(BF), ki.astype(BF), ke.astype(BF), v_ref[...].astype(BF)
        dob = jnp.where(is_lat, do_ref[...], 0.0).astype(BF)
        order = range(NCK) if d else range(NCK - 1, -1, -1)
        dqd_h, dki_h, dke_h, dv_h, ddec_h = [], [], [], [], []
        for h in range(H):
            hs = slice(h * DH, (h + 1) * DH)
            a = jnp.where(mb, _dot_nt(qdb[:, hs], kib[:, hs]), 0.0).astype(BF)
            da = jnp.where(mb, _dot_nt(dob[:, hs], vb[:, hs]), 0.0).astype(BF)
            dqd_i = _dot(da, kib[:, hs])
            dki_h.append(_dot_tn(da, qdb[:, hs]))
            dvi = _dot_tn(a, dob[:, hs])
            dqd, inc = {}, {}
            for c in range(NCK):
                rs = _chunk_rows(c)
                dqd[c] = dqd_i[rs, :] + _dot(dob[rs, hs], st_ref[c, h])
                inc[c] = _dot_tn(dob[rs, hs], qdb[rs, hs])
            dst = dS[h]
            dke, dv, ddec = {}, {}, {}
            for c in order:
                rs = _chunk_rows(c)
                dv[c] = dvi[rs, :] + _dot_nt(keb[rs, hs], dst)
                dke[c] = _dot(vb[rs, hs], dst)
                ddec[c] = _colsum(st_ref[c, h] * dst)
                dst = inc[c] + dst * decs[c][:, hs]
            dS[h] = dst
            cat = lambda t: jnp.concatenate([t[c] for c in range(NCK)], axis=0)
            dqd_h.append(cat(dqd))
            dke_h.append(cat(dke))
            dv_h.append(cat(dv))
            ddec_h.append([ddec[c] for c in range(NCK)])
        lanes = lambda parts: jnp.concatenate(parts, axis=1)
        dqd, dki, dke, dv = lanes(dqd_h), lanes(dki_h), lanes(dke_h), lanes(dv_h)
        dq = dqd * eg
        dk = dki * eig + dke * eeg
        dke_ke = dke * ke
        dg = dqd * qd - dki * ki - dke_ke
        dgl = [_colsum(dke_ke[_chunk_rows(c), :]) + lanes([ddec_h[h][c] for h in range(H)]) * decs[c]
               for c in range(NCK)]
        dglb = jnp.concatenate([jnp.broadcast_to(t, (CH, D)) for t in dgl], axis=0)
        df = (_chunk_cumsum(dg, 1 - d) + dglb) / f - dk
        dzf = df * (1.0 - lb) * (sf * (1.0 - sf))
        sum_ref[0:1, :] += _colsum(dzf)
        sum_ref[1:2, :] += _colsum(df * (1.0 - sf))
        if last:
            dz0 = (dq + pq_ref[...]) * (Q_SCALE * (sq * (1.0 + zq * (1.0 - sq))))
            dz3 = dv + pv_ref[...]
            sum_ref[2:3, :] += _colsum(dz0)
            sum_ref[3:4, :] += _colsum(dz3)
            o0_ref[...] = dz0.astype(BF)
            o1_ref[...] = dz3.astype(BF)
        else:
            o0_ref[...] = dq
            o1_ref[...] = dv
        o2_ref[...] = dzf.astype(BF)

        @pl.when(s == nrb - 1)
        def _():
            sum_ref[1:2, :] = sum_ref[1:2, :] * (lb * (1.0 - lb))

    def zspec(cb):
        return pl.BlockSpec((RB, D), lambda s: (rb(s), cb))

    rowspec = pl.BlockSpec((RB, D), lambda s: (rb(s), 0))
    in_specs = [zspec(0), zspec(1 + d), zspec(3), pl.BlockSpec((2, 2, D), lambda s: (0, 0, 0)),
                pl.BlockSpec((RB, D), lambda s: (jnp.minimum(rb(s), nlb - 1), 0)),
                pl.BlockSpec((NCK, H, DH, DH), lambda s: (rb(s), 0, 0, 0))]
    args = [z, z, z, lbl, do_lat, states]
    if last:
        in_specs += [rowspec, rowspec]
        args += list(prev)
    dt01 = BF if last else F32
    return pl.pallas_call(
        body, grid=(nrb,), in_specs=in_specs,
        out_specs=(rowspec, rowspec, rowspec, pl.BlockSpec((8, D), lambda s: (0, 0))),
        out_shape=(jax.ShapeDtypeStruct((m, D), dt01), jax.ShapeDtypeStruct((m, D), dt01),
                   jax.ShapeDtypeStruct((m, D), BF), jax.ShapeDtypeStruct((8, D), F32)),
        scratch_shapes=[pltpu.VMEM((H, DH, DH), F32)],
        name=f"gla_bwd{d}", compiler_params=_cp(("arbitrary",)))(*args)


def _shift_rows(v, k, r):
    row = lax.broadcasted_iota(jnp.int32, v.shape, 0)
    rolled = pltpu.roll(v, k % r, 0)
    return jnp.where((row >= k) & (row < r + k), rolled, 0.0)


def _conv_fwd(xb, cw_ref, cb_ref, r):
    return (cb_ref[...] + _shift_rows(xb, 1, r) * cw_ref[0:1, :] + xb * cw_ref[1:2, :]
            + _shift_rows(xb, -1, r) * cw_ref[2:3, :] + _shift_rows(xb, -2, r) * cw_ref[3:4, :])


def _window(ref, lo, n, rows):
    parts = []
    if lo < 0:
        parts.append(jnp.zeros((-lo,) + tuple(ref.shape[1:]), F32))
    parts.append(ref[max(lo, 0):min(lo + n, rows)])
    if lo + n > rows:
        parts.append(jnp.zeros((lo + n - rows,) + tuple(ref.shape[1:]), F32))
    return parts[0] if len(parts) == 1 else jnp.concatenate(parts, axis=0)


def _conv_cols(x_ref, cw_ref, cb_ref, r0, n, rows):
    w = _window(x_ref, r0 - 1, n + 3, rows)
    return (cb_ref[...] + w[0:n] * cw_ref[0:1, :] + w[1:n + 1] * cw_ref[1:2, :] + w[2:n + 2] * cw_ref[2:3, :]
            + w[3:n + 3] * cw_ref[3:4, :])


def _softplus_neg(lam):
    y = jnp.exp(-jnp.abs(lam))
    u = 1.0 + y
    tiny = u == 1.0
    l1p = jnp.where(tiny, y, jnp.log(u) * (y / jnp.where(tiny, 1.0, u - 1.0)))
    return jnp.maximum(-lam, 0.0) + l1p


def _gates(xc, wr_ref, br, wi_ref, bi, sp):
    xcb = xc.astype(BF)
    rs, is_ = [], []
    for g in range(H):
        gs = slice(g * DH, (g + 1) * DH)
        rs.append(jnp.dot(xcb[:, gs], wr_ref[g].astype(BF), preferred_element_type=F32))
        is_.append(jnp.dot(xcb[:, gs], wi_ref[g].astype(BF), preferred_element_type=F32))
    r = _sig(jnp.concatenate(rs, axis=1) + br)
    i = _sig(jnp.concatenate(is_, axis=1) + bi)
    log_a = (-RG_C * r) * sp
    a = jnp.exp(log_a)
    t = jnp.tanh(log_a)
    om = (-2.0 * t) / (1.0 - t)
    return r, i, a, om


def _scan_rows(d, nrows, a_s, b_s, h_s, h0):
    nsl = nrows // 8

    def slab(j, h):
        jj = (nsl - 1 - j) if d else j
        r0 = pl.multiple_of(jj * 8, 8)
        for t in (range(7, -1, -1) if d else range(8)):
            h = a_s[pl.ds(r0 + t, 1), :] * h + b_s[pl.ds(r0 + t, 1), :]
            h_s[pl.ds(r0 + t, 1), :] = h
        return h

    return lax.fori_loop(0, nsl, slab, h0)


def _col_of(d, ncols):
    if d:
        return lambda s: ncols - jnp.maximum(s, 1)
    return lambda s: jnp.maximum(s, 1) - 1


def _rglru_fwd(z, cw, cb, wr, br, wi, bi, lam, d, t_lat, t_ctx):
    m = z.shape[0]
    rows = t_lat // GRID_W
    z3 = z.reshape(m // GRID_W, GRID_W, IN_COLS)
    nblk = GRID_W // CB
    blk = _col_of(d, nblk)
    cblk = t_lat // t_ctx
    rc = min(RCH, rows)

    def body(zc_ref, zx_ref, cw_ref, cb_ref, wr_ref, br_ref, wi_ref, bi_ref, lam_ref,
             hx_ref, hpx_ref, ax_ref, hpc_ref, a_s, b_s, h_s, hcar, a3, b3, cin_s):
        s = pl.program_id(0)
        sp = _softplus_neg(lam_ref[d:d + 1, :])
        br_ = br_ref[d:d + 1, :]
        bi_ = bi_ref[d:d + 1, :]

        @pl.when(s == 0)
        def _():
            xc = _conv_fwd(zc_ref[...], cw_ref, cb_ref, t_ctx)
            _, i, a, om = _gates(xc, wr_ref, br_, wi_ref, bi_, sp)
            a_s[...] = a
            b_s[...] = jnp.sqrt(om) * (i * xc)
            h0 = jnp.zeros((1, D), F32)
            hcar[0:1, :] = _scan_rows(d, t_ctx, a_s, b_s, h_s, h0)
            hs = h_s[...]
            row = lax.broadcasted_iota(jnp.int32, (t_ctx, D), 0)
            if d:
                hpc_ref[...] = jnp.where(row == t_ctx - 1, h0, pltpu.roll(hs, t_ctx - 1, 0))
            else:
                hpc_ref[...] = jnp.where(row == 0, h0, pltpu.roll(hs, 1, 0))

        @pl.when(s > 0)
        def _():
            for r0 in range(0, rows, rc):
                xc = _conv_cols(zx_ref, cw_ref, cb_ref, r0, rc, rows).reshape(rc * CB, D)
                _, i, a, om = _gates(xc, wr_ref, br_, wi_ref, bi_, sp)
                a3[r0:r0 + rc] = a.reshape(rc, CB, D)
                ax_ref[r0:r0 + rc] = a.reshape(rc, CB, D)
                b3[r0:r0 + rc] = (jnp.sqrt(om) * (i * xc)).reshape(rc, CB, D)

            def local(t, carry):
                hl, p = carry
                r = (rows - 1 - t) if d else t
                a = a3[r]
                hl = a * hl + b3[r]
                p = a * p
                b3[r] = hl
                a3[r] = p
                return hl, p

            hl, p = lax.fori_loop(0, rows, local, (jnp.zeros((CB, D), F32), jnp.ones((CB, D), F32)))
            cin = hcar[0:1, :]
            for j in (range(CB - 1, -1, -1) if d else range(CB)):
                cin_s[j:j + 1, :] = cin
                cin = hl[j:j + 1, :] + p[j:j + 1, :] * cin
            hcar[0:1, :] = cin
            c_in = cin_s[...]

            def fix(t, prev):
                r = (rows - 1 - t) if d else t
                h = b3[r] + a3[r] * c_in
                hx_ref[r] = h
                hpx_ref[r] = prev
                return h

            lax.fori_loop(0, rows, fix, c_in)

    full = lambda shp: pl.BlockSpec(shp, lambda s: (0,) * len(shp))
    colspec = pl.BlockSpec((rows, CB, D), lambda s: (0, blk(s), 0))
    outs = pl.pallas_call(
        body, grid=(nblk + 1,),
        in_specs=[pl.BlockSpec((t_ctx, D), lambda s: (cblk, 5)),
                  pl.BlockSpec((rows, CB, D), lambda s: (0, blk(s), 5)),
                  full((4, D)), full((1, D)),
                  pl.BlockSpec((None, H, DH, DH), lambda s: (d, 0, 0, 0)), full((2, D)),
                  pl.BlockSpec((None, H, DH, DH), lambda s: (d, 0, 0, 0)), full((2, D)), full((2, D))],
        out_specs=(colspec, colspec, colspec, full((t_ctx, D))),
        out_shape=(jax.ShapeDtypeStruct((rows, GRID_W, D), F32),) * 3 + (jax.ShapeDtypeStruct((t_ctx, D), F32),),
        scratch_shapes=[pltpu.VMEM((t_ctx, D), F32), pltpu.VMEM((t_ctx, D), F32), pltpu.VMEM((t_ctx, D), F32),
                        pltpu.VMEM((8, D), F32), pltpu.VMEM((rows, CB, D), F32), pltpu.VMEM((rows, CB, D), F32),
                        pltpu.VMEM((CB, D), F32)],
        name=f"rglru_fwd{d}", compiler_params=_cp(("arbitrary",)))(z, z3, cw, cb, wr, br, wi, bi, lam)
    return outs[0].reshape(t_lat, D), outs[1].reshape(t_lat, D), outs[2].reshape(t_lat, D), outs[3]


def _rglru_bwd(z, cw, cb, wr, br, wi, bi, lam, dh_lat, hp_lat, a_lat, hp_ctx, d, t_lat, t_ctx, prev=None):
    m = z.shape[0]
    rows = t_lat // GRID_W
    z3 = z.reshape(m // GRID_W, GRID_W, IN_COLS)
    nblk = GRID_W // CB
    fblk = _col_of(d, nblk)
    blk = lambda s: fblk(nblk - jnp.minimum(s, nblk - 1))
    cblk = t_lat // t_ctx
    rc = min(RCH, rows)
    last = prev is not None
    view3 = lambda v: v.reshape(rows, GRID_W, D)

    def body(*refs):
        (zc_ref, zx_ref, cw_ref, cb_ref, wr_ref, br_ref, wi_ref, bi_ref, lam_ref, dh_ref, hpx_ref, ax_ref,
         hpc_ref) = refs[:13]
        k = 13
        if last:
            pdx_ref, pdc_ref = refs[13:15]
            k = 15
        ox_ref, oc_ref, dwr_ref, dwi_ref, sum_ref, a_s, b_s, gcar, a3, b3, cin_s = refs[k:]
        s = pl.program_id(0)
        lam_d = lam_ref[d:d + 1, :]
        sp = _softplus_neg(lam_d)
        br_ = br_ref[d:d + 1, :]
        bi_ = bi_ref[d:d + 1, :]
        flat = lambda v: v.reshape(-1, D)

        @pl.when(s == 0)
        def _():
            gcar[...] = jnp.zeros_like(gcar)
            dwr_ref[...] = jnp.zeros_like(dwr_ref)
            dwi_ref[...] = jnp.zeros_like(dwi_ref)
            sum_ref[...] = jnp.zeros_like(sum_ref)


        def conv_sums(dxc, dxb, xm1, x0, xp1, xp2):
            sum_ref[3:4, :] += _colsum(flat(dxc))
            sum_ref[4:5, :] += _colsum(flat(dxb))
            sum_ref[8:9, :] += _colsum(flat(dxc * xm1))
            sum_ref[9:10, :] += _colsum(flat(dxc * x0))
            sum_ref[10:11, :] += _colsum(flat(dxc * xp1))
            sum_ref[11:12, :] += _colsum(flat(dxc * xp2))

        def gate_grads(g, hp, xc, rr, i, a, om):
            mult = jnp.sqrt(om)
            da = g * hp
            ixc = i * xc
            dmult = g * ixc
            dixc = g * mult
            di = dixc * xc
            dxc = dixc * i
            dlog_a = da * a - dmult * ((1.0 - om) / mult)
            dr = dlog_a * (-RG_C * sp)
            sum_ref[2:3, :] += _colsum(dlog_a * rr)
            drp = dr * rr * (1.0 - rr)
            dip = di * i * (1.0 - i)
            sum_ref[0:1, :] += _colsum(drp)
            sum_ref[1:2, :] += _colsum(dip)
            xcb = xc.astype(BF)
            drb = drp.astype(BF)
            dib = dip.astype(BF)
            parts = []
            for gi in range(H):
                gs = slice(gi * DH, (gi + 1) * DH)
                parts.append(_dot_nt(drb[:, gs], wr_ref[gi]) + _dot_nt(dib[:, gs], wi_ref[gi]))
                dwr_ref[gi] += _dot_tn(xcb[:, gs], drb[:, gs])
                dwi_ref[gi] += _dot_tn(xcb[:, gs], dib[:, gs])
            return dxc + jnp.concatenate(parts, axis=1)

        @pl.when(s < nblk)
        def _():
            def local(t, carry):
                c, q = carry
                r = t if d else (rows - 1 - t)
                a = ax_ref[r]
                c = a * (c + dh_ref[r])
                q = a * q
                b3[r] = c
                a3[r] = q
                return c, q

            c, q = lax.fori_loop(0, rows, local, (jnp.zeros((CB, D), F32), jnp.ones((CB, D), F32)))
            cin = gcar[0:1, :]
            for j in (range(CB) if d else range(CB - 1, -1, -1)):
                cin_s[j:j + 1, :] = cin
                cin = c[j:j + 1, :] + q[j:j + 1, :] * cin
            gcar[0:1, :] = cin
            c_in = cin_s[...]
            for r0 in (range(rows - rc, -1, -rc) if d else range(0, rows, rc)):
                if d:
                    lo = max(r0 - 1, 0)
                    cn = b3[lo:r0 + rc - 1] + a3[lo:r0 + rc - 1] * c_in
                    if r0 == 0:
                        cn = jnp.concatenate([c_in[None], cn], axis=0)
                else:
                    hi = min(r0 + rc + 1, rows)
                    cn = b3[r0 + 1:hi] + a3[r0 + 1:hi] * c_in
                    if hi == rows:
                        cn = jnp.concatenate([cn, c_in[None]], axis=0)
                g = flat(dh_ref[r0:r0 + rc] + cn)
                xc = flat(_conv_cols(zx_ref, cw_ref, cb_ref, r0, rc, rows))
                rr, i, a, om = _gates(xc, wr_ref, br_, wi_ref, bi_, sp)
                b3[r0:r0 + rc] = gate_grads(g, flat(hpx_ref[r0:r0 + rc]), xc, rr, i, a, om).reshape(rc, CB, D)
            if last:
                for r0 in range(0, rows, rc):
                    b3[r0:r0 + rc] = b3[r0:r0 + rc] + pdx_ref[r0:r0 + rc]
                for r0 in range(0, rows, rc):
                    w = _window(b3, r0 - 2, rc + 3, rows)
                    xw = _window(zx_ref, r0 - 1, rc + 3, rows)
                    dxc = w[2:rc + 2]
                    dxb = (w[3:rc + 3] * cw_ref[0:1, :] + dxc * cw_ref[1:2, :] + w[1:rc + 1] * cw_ref[2:3, :]
                           + w[0:rc] * cw_ref[3:4, :])
                    conv_sums(dxc, dxb, xw[0:rc], xw[1:rc + 1], xw[2:rc + 2], xw[3:rc + 3])
                    ox_ref[r0:r0 + rc] = dxb
            else:
                for r0 in range(0, rows, rc):
                    ox_ref[r0:r0 + rc] = b3[r0:r0 + rc]

        @pl.when(s == nblk)
        def _():
            r = t_ctx
            xb = zc_ref[...]
            xc = _conv_fwd(xb, cw_ref, cb_ref, r)
            rr, i, a, om = _gates(xc, wr_ref, br_, wi_ref, bi_, sp)
            a_s[...] = a
            b_s[...] = jnp.zeros((r, D), F32)
            c0 = gcar[0:1, :]
            _scan_rows(1 - d, r, a_s, b_s, b_s, c0)
            cs = b_s[...]
            row = lax.broadcasted_iota(jnp.int32, (r, D), 0)
            if d:
                g = jnp.where(row == 0, c0, pltpu.roll(cs, 1, 0))
            else:
                g = jnp.where(row == r - 1, c0, pltpu.roll(cs, r - 1, 0))
            dxc = gate_grads(g, hpc_ref[...], xc, rr, i, a, om)
            if last:
                dxc = dxc + pdc_ref[...]
                dxb = (_shift_rows(dxc, -1, r) * cw_ref[0:1, :] + dxc * cw_ref[1:2, :]
                       + _shift_rows(dxc, 1, r) * cw_ref[2:3, :] + _shift_rows(dxc, 2, r) * cw_ref[3:4, :])
                conv_sums(dxc, dxb, _shift_rows(xb, 1, r), xb, _shift_rows(xb, -1, r), _shift_rows(xb, -2, r))
                oc_ref[...] = dxb
            else:
                oc_ref[...] = dxc
            sum_ref[2:3, :] = sum_ref[2:3, :] * (RG_C * _sig(-lam_d))

    full = lambda shp: pl.BlockSpec(shp, lambda s: (0,) * len(shp))
    once = lambda shp: pl.BlockSpec(shp, lambda s: (0,) * len(shp), pipeline_mode=pl.Buffered(1))
    colspec = pl.BlockSpec((rows, CB, D), lambda s: (0, blk(s), 0))
    colonce = pl.BlockSpec((rows, CB, D), lambda s: (0, blk(s), 0), pipeline_mode=pl.Buffered(1))
    in_specs = [pl.BlockSpec((t_ctx, D), lambda s: (cblk, 5), pipeline_mode=pl.Buffered(1)),
                pl.BlockSpec((rows, CB, D), lambda s: (0, blk(s), 5), pipeline_mode=pl.Buffered(1)),
                full((4, D)), full((1, D)),
                pl.BlockSpec((None, H, DH, DH), lambda s: (d, 0, 0, 0)), full((2, D)),
                pl.BlockSpec((None, H, DH, DH), lambda s: (d, 0, 0, 0)), full((2, D)), full((2, D)),
                colonce, colonce, colonce, once((t_ctx, D))]
    args = [z, z3, cw, cb, wr, br, wi, bi, lam, view3(dh_lat), view3(hp_lat), view3(a_lat), hp_ctx]
    if last:
        in_specs += [colonce, once((t_ctx, D))]
        args += [view3(prev[0]), prev[1]]
    outs = pl.pallas_call(
        body, grid=(nblk + 1,), in_specs=in_specs,
        out_specs=(colspec, full((t_ctx, D)), full((H, DH, DH)), full((H, DH, DH)), full((16, D))),
        out_shape=(jax.ShapeDtypeStruct((rows, GRID_W, D), F32), jax.ShapeDtypeStruct((t_ctx, D), F32),
                   jax.ShapeDtypeStruct((H, DH, DH), F32), jax.ShapeDtypeStruct((H, DH, DH), F32),
                   jax.ShapeDtypeStruct((16, D), F32)),
        scratch_shapes=[pltpu.VMEM((t_ctx, D), F32), pltpu.VMEM((t_ctx, D), F32), pltpu.VMEM((8, D), F32),
                        pltpu.VMEM((rows, CB, D), F32), pltpu.VMEM((rows, CB, D), F32), pltpu.VMEM((CB, D), F32)],
        name=f"rglru_bwd{d}", compiler_params=_cp(("arbitrary",)))(*args)
    return (outs[0].reshape(t_lat, D), outs[1]) + tuple(outs[2:])


def _merge(o_f, o_b, h_f, h_b, z, x_all, tgt, mod, norm_g, ln_g, ln_b, p_a, p_b, w_out, t_lat):
    tm = 256
    nt = t_lat // tm

    def body(of_ref, ob_ref, hf_ref, hb_ref, z4_ref, z6_ref, z7_ref, z8_ref, x_ref, t_ref, mod_ref, ng_ref,
             lg_ref, lb_ref, pa_ref, pb_ref, wo_ref,
             do_ref, dh_ref, dz4_ref, dz6_ref, dz7_ref, dz8_ref, gx_ref,
             y_ref, dout_ref, oa_ref, dpa_ref, obv_ref, dpb_ref, acc_ref):
        i = pl.program_id(0)
        lat = i < nt
        latf = lat.astype(F32)

        @pl.when(i == 0)
        def _():
            acc_ref[...] = jnp.zeros_like(acc_ref)

        def per_head(v):
            return jnp.concatenate(
                [jnp.broadcast_to(jnp.mean(v[:, h * DH:(h + 1) * DH], axis=-1, keepdims=True), (tm, DH))
                 for h in range(H)], axis=1)

        gt = mod_ref[0:1, 2 * D:3 * D]
        gfull = jnp.concatenate([ng_ref[...]] * H, axis=1)
        o = of_ref[...] + ob_ref[...]
        rinv = lax.rsqrt(per_head(o * o) + RMS_EPS)
        n = o * rinv
        na = n * gfull
        z4 = z4_ref[...]
        s4 = _sig(z4)
        silu4 = z4 * s4
        oa = na * silu4
        z6 = z6_ref[...]
        s6 = _sig(z6)
        silu6 = z6 * s6
        hsum = hf_ref[...] + hb_ref[...]
        obv = hsum * silu6
        pa = _dot(oa, pa_ref[...])
        pb = _dot(obv, pb_ref[...])
        s7 = _sig(z7_ref[...])
        s8 = _sig(z8_ref[...])
        y = s7 * pa + s8 * pb
        out = _dot(y, wo_ref[...])
        pre = ALPHA * x_ref[...] + gt * out
        mu = jnp.mean(pre, axis=-1, keepdims=True)
        xc = pre - mu
        rstd = lax.rsqrt(jnp.mean(xc * xc, axis=-1, keepdims=True) + LN_EPS)
        xhat = xc * rstd
        lg = lg_ref[...]
        diff = xhat * lg + lb_ref[...] - t_ref[...]
        acc_ref[8:9, :] += _colsum(diff * diff) * (0.5 / D * latf)
        dxn = diff * (1.0 / D)
        acc_ref[1:2, :] += _colsum(dxn * xhat) * latf
        acc_ref[2:3, :] += _colsum(dxn) * latf
        dxhat = dxn * lg
        dpre = rstd * (dxhat - jnp.mean(dxhat, axis=-1, keepdims=True)
                       - xhat * jnp.mean(dxhat * xhat, axis=-1, keepdims=True))
        gx_ref[...] = ALPHA * dpre
        acc_ref[0:1, :] += _colsum(dpre * out) * latf
        dout = dpre * gt
        dy = _dot_nt(dout, wo_ref[...])
        dpa = dy * s7
        dpb = dy * s8
        dz7 = dy * pa * (s7 * (1.0 - s7))
        dz8 = dy * pb * (s8 * (1.0 - s8))
        doa = _dot_nt(dpa, pa_ref[...])
        dob = _dot_nt(dpb, pb_ref[...])
        dh_ref[...] = dob * silu6
        dz6 = dob * hsum * (s6 * (1.0 + z6 * (1.0 - s6)))
        dna = doa * silu4
        dz4 = doa * na * (s4 * (1.0 + z4 * (1.0 - s4)))
        dng = _colsum(dna * n)
        acc_ref[7:8, 0:DH] += sum(dng[:, h * DH:(h + 1) * DH] for h in range(H)) * latf
        dn = dna * gfull
        do_ref[...] = rinv * (dn - n * per_head(dn * n))
        acc_ref[3:4, :] += _colsum(dz4) * latf
        acc_ref[4:5, :] += _colsum(dz6) * latf
        acc_ref[5:6, :] += _colsum(dz7) * latf
        acc_ref[6:7, :] += _colsum(dz8) * latf
        dz4_ref[...] = (dz4 * latf).astype(BF)
        dz6_ref[...] = (dz6 * latf).astype(BF)
        dz7_ref[...] = (dz7 * latf).astype(BF)
        dz8_ref[...] = (dz8 * latf).astype(BF)
        y_ref[...] = y.astype(BF)
        dout_ref[...] = dout.astype(BF)
        oa_ref[...] = oa.astype(BF)
        dpa_ref[...] = dpa.astype(BF)
        obv_ref[...] = obv.astype(BF)
        dpb_ref[...] = dpb.astype(BF)

        @pl.when(i == nt - 1)
        def _():
            acc_ref[9:10, :] = jnp.broadcast_to(jnp.sum(acc_ref[8:9, :], axis=-1, keepdims=True), (1, D))

    m = x_all.shape[0]
    lrow = lambda i: jnp.minimum(i, nt - 1)
    row = pl.BlockSpec((tm, D), lambda i: (lrow(i), 0))
    allrow = pl.BlockSpec((tm, D), lambda i: (i, 0))
    zs = lambda cb: pl.BlockSpec((tm, D), lambda i: (lrow(i), cb))
    full = lambda shp: pl.BlockSpec(shp, lambda i: (0,) * len(shp))
    wfull = pl.BlockSpec((D, D), lambda i: (0, 0), pipeline_mode=pl.Buffered(1))
    f32o = jax.ShapeDtypeStruct((t_lat, D), F32)
    bfo = jax.ShapeDtypeStruct((t_lat, D), BF)
    bfall = jax.ShapeDtypeStruct((m, D), BF)
    return pl.pallas_call(
        body, grid=(m // tm,),
        in_specs=[row, row, row, row, zs(4), zs(6), zs(7), zs(8), row, row, full((16, 3 * D)), full((1, DH)),
                  full((1, D)), full((1, D)), wfull, wfull, wfull],
        out_specs=(row, row) + (allrow,) * 4 + (row,) * 7 + (full((16, D)),),
        out_shape=(f32o, f32o, bfall, bfall, bfall, bfall, f32o, bfo, bfo, bfo, bfo, bfo, bfo,
                   jax.ShapeDtypeStruct((16, D), F32)),
        name="merge", compiler_params=_cp(("arbitrary",), VMEM_LIMIT_MERGE))(
            o_f, o_b, h_f, h_b, z, z, z, z, x_all, tgt, mod, norm_g, ln_g, ln_b, p_a, p_b, w_out)


def _wgrad(a, b, name):
    tm = 1024

    def body(a_ref, b_ref, o_ref):
        @pl.when(pl.program_id(0) == 0)
        def _():
            o_ref[...] = jnp.zeros_like(o_ref)
        o_ref[...] += _dot_tn(a_ref[...], b_ref[...])

    row = pl.BlockSpec((tm, D), lambda i: (i, 0))
    return pl.pallas_call(body, grid=(a.shape[0] // tm,), in_specs=[row, row],
                          out_specs=pl.BlockSpec((D, D), lambda i: (0, 0)),
                          out_shape=jax.ShapeDtypeStruct((D, D), F32), name=name,
                          compiler_params=_cp(("arbitrary",)))(a, b)


def _wgrad_in(u_all, dz):
    m = u_all.shape[0]
    assert m % 128 == 0
    tm = m // 8

    def body(u_ref, dz_ref, o_ref):
        @pl.when(pl.program_id(1) == 0)
        def _():
            o_ref[...] = jnp.zeros_like(o_ref)
        o_ref[0] += _dot_tn(u_ref[...], dz_ref[...])

    return pl.pallas_call(
        body, grid=(NSH, m // tm),
        in_specs=[pl.BlockSpec((tm, D), lambda n, i: (i, 0)),
                  pl.BlockSpec((tm, SHC), lambda n, i: (i, n))],
        out_specs=pl.BlockSpec((1, D, SHC), lambda n, i: (n, 0, 0)),
        out_shape=jax.ShapeDtypeStruct((NSH, D, SHC), F32), name="wgrad_in",
        compiler_params=_cp(("arbitrary", "arbitrary")))(u_all, dz)


def _du(dz, w_in_g, x_all, mod, gxres, n_lat_tiles, sums=()):
    m = x_all.shape[0]
    tm = 256
    nt = m // tm
    nct = nt - n_lat_tiles
    ns = len(sums)
    rblk = lambda i: jnp.where(i < nct, n_lat_tiles + i, i - nct)
    lblk = lambda i: jnp.maximum(i - nct, 0)

    def body(*refs):
        dz_ref, w_ref, x_ref, mod_ref, gr_ref = refs[:5]
        sum_refs = refs[5:5 + ns]
        gx_ref, dm_ref = refs[5 + ns:7 + ns]
        got_refs = refs[7 + ns:7 + 2 * ns]
        sems = refs[7 + 2 * ns:]
        i = pl.program_id(0)
        is_lat = i >= nct

        @pl.when(i == 0)
        def _():
            dm_ref[...] = jnp.zeros_like(dm_ref)
            if ns:
                for cp in _rs_chip_copies(sum_refs, got_refs, *sems):
                    cp.start()

        du = _dot_nt(dz_ref[:, 0:SHC], w_ref[0])
        for n in range(1, NSH):
            du = du + _dot_nt(dz_ref[:, n * SHC:(n + 1) * SHC], w_ref[n])
        sc = jnp.where(is_lat, mod_ref[0:1, D:2 * D], mod_ref[1:2, D:2 * D])
        dsh = _colsum(du)
        dsc = _colsum(du * x_ref[...])

        @pl.when(is_lat)
        def _():
            gx_ref[...] = du * (1.0 + sc) + gr_ref[...]
            dm_ref[0:1, 0:D] += dsh
            dm_ref[0:1, D:2 * D] += dsc

        @pl.when(jnp.logical_not(is_lat))
        def _():
            dm_ref[1:2, 0:D] += dsh
            dm_ref[1:2, D:2 * D] += dsc

        if ns:
            @pl.when(i == nt - 1)
            def _():
                for cp in _rs_chip_copies(sum_refs, got_refs, *sems):
                    cp.wait()

    outs = pl.pallas_call(
        body, grid=(nt,),
        in_specs=[pl.BlockSpec((tm, IN_COLS), lambda i: (rblk(i), 0)),
                  pl.BlockSpec((NSH, D, SHC), lambda i: (0, 0, 0), pipeline_mode=pl.Buffered(1)),
                  pl.BlockSpec((tm, D), lambda i: (rblk(i), 0)),
                  pl.BlockSpec((16, 3 * D), lambda i: (0, 0)),
                  pl.BlockSpec((tm, D), lambda i: (lblk(i), 0))] + [_ANY] * ns,
        out_specs=[pl.BlockSpec((tm, D), lambda i: (lblk(i), 0)),
                   pl.BlockSpec((8, 2 * D), lambda i: (0, 0))] + [_ANY] * ns,
        out_shape=[jax.ShapeDtypeStruct((n_lat_tiles * tm, D), F32), jax.ShapeDtypeStruct((8, 2 * D), F32)]
        + [jax.ShapeDtypeStruct((3,) + g.shape[1:], g.dtype) for g in sums],
        scratch_shapes=[pltpu.SemaphoreType.DMA((3 * ns,)), pltpu.SemaphoreType.DMA((3 * ns,))] if ns else [],
        name="du", compiler_params=_cp(("arbitrary",)))(dz, w_in_g, x_all, mod, gxres, *sums)
    return outs[0], outs[1], list(outs[2:])


def _row_tile(rows, cols):
    t = 8
    while t * 2 * cols * 4 <= (1 << 20) and rows % (t * 2) == 0:
        t *= 2
    return t


def _adamw_update(w_ref, g_ref, m_ref, v_ref, d_ref, nm_ref, nv_ref):
    gg = g_ref[...]
    m2 = ADAM_B1 * m_ref[...] + (1.0 - ADAM_B1) * gg
    v2 = ADAM_B2 * v_ref[...] + (1.0 - ADAM_B2) * (gg * gg)
    m_hat = m2 / (1.0 - ADAM_B1 ** ADAM_STEP)
    v_hat = v2 / (1.0 - ADAM_B2 ** ADAM_STEP)
    d_ref[...] = -ADAM_LR * (m_hat / (jnp.sqrt(v_hat) + ADAM_EPS) + ADAM_WD * w_ref[...])
    nm_ref[...] = m2
    nv_ref[...] = v2


def _adamw_many(ws, gs, ms, vs):
    n = len(ws)

    def body(*refs):
        for j in range(n):
            _adamw_update(*refs[4 * j:4 * j + 4], *refs[4 * n + 3 * j:4 * n + 3 * j + 3])

    args = [a for quad in zip(ws, gs, ms, vs) for a in quad]
    outs = pl.pallas_call(body, out_shape=[jax.ShapeDtypeStruct(w.shape, F32) for w in ws for _ in range(3)],
                          name="adamw_small", compiler_params=_cp())(*args)
    return outs[0::3], outs[1::3], outs[2::3]


def _adamw(w, g, m, v, name):
    rows, cols = w.shape
    tr = _row_tile(rows, cols)

    def body(*refs):
        _adamw_update(*refs)

    spec = pl.BlockSpec((tr, cols), lambda i: (i, 0))
    o = jax.ShapeDtypeStruct((rows, cols), F32)
    return pl.pallas_call(body, grid=(rows // tr,), in_specs=[spec] * 4, out_specs=(spec,) * 3,
                          out_shape=(o, o, o), name=name, compiler_params=_cp(("arbitrary",)))(w, g, m, v)


_ANY = pl.BlockSpec(memory_space=pl.ANY)


def _place():
    return lax.axis_index("x"), lax.axis_index("y"), lax.axis_index("c")


def _ag_copies(ins, outs, send, recv, fsend, frecv, lsem):
    x, y, c = _place()
    me = 2 * x + y
    chips = ((1 - x, y), (x, 1 - y), (1 - x, 1 - y))
    local, chip, hand = [], [], []
    for j in range(len(ins)):
        hr = ins[j].shape[0] // 2
        half = pl.ds(pl.multiple_of(c * hr, 8), hr)
        local.append(pltpu.make_async_copy(ins[j], outs[j].at[me], lsem.at[j]))
        for k, (px, py) in enumerate(chips):
            chip.append(pltpu.make_async_remote_copy(
                src_ref=ins[j].at[half, :], dst_ref=outs[j].at[me, half, :], send_sem=send.at[3 * j + k],
                recv_sem=recv.at[3 * j + k], device_id=(px, py, c), device_id_type=MESH))
            got = outs[j].at[2 * px + py, half, :]
            hand.append(pltpu.make_async_remote_copy(
                src_ref=got, dst_ref=got, send_sem=fsend.at[3 * j + k], recv_sem=frecv.at[3 * j + k],
                device_id=(x, y, 1 - c), device_id_type=MESH))
    return local, chip, hand


def _ag_sems(n):
    return [pltpu.SemaphoreType.DMA((3 * n,))] * 4 + [pltpu.SemaphoreType.DMA((n,))]


def _ag_finish(local, chip, hand, done=0):
    for k in range(done, len(chip)):
        chip[k].wait_recv()
        hand[k].start()
    for cp in chip:
        cp.wait_send()
    for k in range(done):
        hand[k].wait_send()
    for k in range(done, len(chip)):
        hand[k].wait_send()
        hand[k].wait_recv()
    for cp in local:
        cp.wait()


def _mod_tp(c8, c_ctx, w_mod_sh, b_mod_sh):
    mc = w_mod_sh.shape[1]

    def body(c8_ref, cctx_ref, w_ref, b_ref, mod_ref, cc_ref, cc_s, part_s, send1, recv1, send3, recv3):
        x, y, c = _place()
        me = 4 * x + 2 * y + c
        ms = 2 * x + y
        copies = []
        for k in range(1, 8):
            peer = (x ^ ((k >> 2) & 1), y ^ ((k >> 1) & 1), c ^ (k & 1))
            cp = pltpu.make_async_remote_copy(src_ref=c8_ref, dst_ref=cc_s.at[me], send_sem=send1.at[k],
                                              recv_sem=recv1.at[k], device_id=peer, device_id_type=MESH)
            cp.start()
            copies.append(cp)
        cc_s[me] = c8_ref[...]
        for cp in copies:
            cp.wait()
        cc_ref[...] = jnp.zeros_like(cc_ref)
        for j in range(8):
            cc_ref[j:j + 1, :] = cc_s[j, 0:1, :]
        cc_ref[8:9, :] = cctx_ref[...]
        v = cc_ref[...]
        part_s[ms] = _dot(v * _sig(v), w_ref[...]) + b_ref[...]
        copies = []
        for k in range(1, 4):
            peer = (x ^ ((k >> 1) & 1), y ^ (k & 1), c)
            cp = pltpu.make_async_remote_copy(src_ref=part_s.at[ms], dst_ref=part_s.at[ms], send_sem=send3.at[k],
                                              recv_sem=recv3.at[k], device_id=peer, device_id_type=MESH)
            cp.start()
            copies.append(cp)
        for cp in copies:
            cp.wait()
        for s in range(NSH):
            mod_ref[:, s * mc:(s + 1) * mc] = part_s[s]

    vm = pl.BlockSpec(memory_space=pltpu.VMEM)
    return pl.pallas_call(
        body, in_specs=[vm] * 4, out_specs=(vm, vm),
        out_shape=(jax.ShapeDtypeStruct((16, NSH * mc), F32), jax.ShapeDtypeStruct((16, D), F32)),
        scratch_shapes=[pltpu.VMEM((8, 8, D), F32), pltpu.VMEM((NSH, 16, mc), F32),
                        pltpu.SemaphoreType.DMA((8,)), pltpu.SemaphoreType.DMA((8,)),
                        pltpu.SemaphoreType.DMA((4,)), pltpu.SemaphoreType.DMA((4,))],
        name="mod_tp", compiler_params=_cp())(c8, c_ctx, w_mod_sh, b_mod_sh)


def _inproj_ag(x_all, mod, w_in_sh, b_in, narrow_sh, n_lat_tiles):
    m = x_all.shape[0]
    assert m % (11 * 16) == 0
    tm = m // 11
    nt = m // tm
    n_lat = n_lat_tiles * 256
    x_, y_ = lax.axis_index("x"), lax.axis_index("y")
    sids = jnp.stack([2 * x_ + y_, 2 * (1 - x_) + y_, 2 * x_ + 1 - y_, 2 * (1 - x_) + 1 - y_]).astype(jnp.int32)

    def body(sid_ref, x_ref, mod_ref, b_ref, wsh_ref, nsh_ref, z_ref, u_ref, wg_ref, ng_ref, w_s, u_s, *sems):
        n = pl.program_id(0)
        i = pl.program_id(1)
        rows = pl.ds(pl.multiple_of(i * tm, tm), tm)
        ag = ((wsh_ref, nsh_ref), (wg_ref, ng_ref)) + tuple(sems[:5])
        wsem = sems[5]

        def load(src):
            cp = pltpu.make_async_copy(src, w_s, wsem)
            cp.start()
            cp.wait()

        @pl.when((n == 0) & (i == 0))
        def _():
            local, chip, _ = _ag_copies(*ag)
            for cp in chip + local:
                cp.start()
            load(wsh_ref)

        for k in range(NSH - 1):
            @pl.when((n == k + 1) & (i == 0))
            def _():
                _, chip, hand = _ag_copies(*ag)
                chip[k].wait_recv()
                hand[k].start()
                hand[k].wait_recv()
                load(wg_ref.at[sid_ref[k + 1]])

        @pl.when(n == 0)
        def _():
            is_lat = (i * tm + lax.broadcasted_iota(jnp.int32, (tm, 1), 0)) < n_lat
            u, _ = _modulate(x_ref, mod_ref, is_lat)
            u_s[rows, :] = u.astype(BF)
            u_ref[...] = u.astype(BF)

        z_ref[...] = _dot(u_s[rows, :], w_s[...]) + b_ref[...]

        @pl.when((n == NSH - 1) & (i == nt - 1))
        def _():
            _ag_finish(*_ag_copies(*ag), done=NSH - 1)

    first = lambda n, i: jnp.where(n == 0, i, nt - 1)
    outs = pl.pallas_call(
        body, grid_spec=pltpu.PrefetchScalarGridSpec(
            num_scalar_prefetch=1, grid=(NSH, nt),
            in_specs=[pl.BlockSpec((tm, D), lambda n, i, sid: (first(n, i), 0)),
                      pl.BlockSpec((16, 3 * D), lambda n, i, sid: (0, 0)),
                      pl.BlockSpec((1, SHC), lambda n, i, sid: (0, sid[n])), _ANY, _ANY],
            out_specs=[pl.BlockSpec((tm, SHC), lambda n, i, sid: (i, sid[n])),
                       pl.BlockSpec((tm, D), lambda n, i, sid: (first(n, i), 0)), _ANY, _ANY],
            scratch_shapes=[pltpu.VMEM((D, SHC), BF), pltpu.VMEM((m, D), BF)] + _ag_sems(2)
            + [pltpu.SemaphoreType.DMA]),
        out_shape=[jax.ShapeDtypeStruct((m, IN_COLS), F32), jax.ShapeDtypeStruct((m, D), BF),
                   jax.ShapeDtypeStruct((NSH,) + w_in_sh.shape, BF),
                   jax.ShapeDtypeStruct((NSH,) + narrow_sh.shape, narrow_sh.dtype)],
        name="inproj_ag", compiler_params=_cp(("arbitrary", "arbitrary")))(sids, x_all, mod, b_in, w_in_sh, narrow_sh)
    return outs


def _rs_sibling(grads):
    n = len(grads)

    def body(*refs):
        ins, got = refs[:n], refs[n:2 * n]
        send, recv = refs[2 * n:]
        x, y, c = _place()
        copies = []
        for j in range(n):
            hr = ins[j].shape[1] // 2
            for s in range(NSH):
                give = ins[j].at[s, pl.ds(pl.multiple_of((1 - c) * hr, 8), hr), :]
                cp = pltpu.make_async_remote_copy(src_ref=give, dst_ref=got[j].at[s], send_sem=send.at[NSH * j + s],
                                                  recv_sem=recv.at[NSH * j + s], device_id=(x, y, 1 - c),
                                                  device_id_type=MESH)
                cp.start()
                copies.append(cp)
        for cp in copies:
            cp.wait()

    half = [jax.ShapeDtypeStruct((NSH, g.shape[1] // 2, g.shape[2]), F32) for g in grads]
    return pl.pallas_call(
        body, in_specs=[_ANY] * n, out_specs=[_ANY] * n, out_shape=half,
        scratch_shapes=[pltpu.SemaphoreType.DMA((NSH * n,)), pltpu.SemaphoreType.DMA((NSH * n,))],
        name="rs_sibling")(*grads)


def _core_vec():
    return lax.axis_index("c").astype(jnp.int32).reshape(1)


def _rs_add1(g, got, name):
    _, r, cols = g.shape
    hr = r // 2
    tr = _row_tile(hr, cols)
    nb = hr // tr

    def body(c_ref, g_ref, got_ref, o_ref):
        o_ref[...] = (g_ref[...] + got_ref[...]).astype(BF)

    spec = pl.BlockSpec((1, tr, cols), lambda s, i, c_ref: (s, i, 0))
    return pl.pallas_call(
        body, grid_spec=pltpu.PrefetchScalarGridSpec(
            num_scalar_prefetch=1, grid=(NSH, nb),
            in_specs=[pl.BlockSpec((1, tr, cols), lambda s, i, c_ref: (s, c_ref[0] * nb + i, 0)), spec],
            out_specs=spec),
        out_shape=jax.ShapeDtypeStruct((NSH, hr, cols), BF), name=name,
        compiler_params=_cp(("arbitrary", "arbitrary")))(_core_vec(), g, got)


def _rs_add2(sums, got, name):
    _, hr, cols = sums.shape
    tr = _row_tile(hr, cols)
    nb = hr // tr
    place = jnp.stack([2 * lax.axis_index("x") + lax.axis_index("y"), lax.axis_index("c")]).astype(jnp.int32)

    def body(p_ref, s_ref, got_ref, o_ref):
        f = lambda v: v.astype(F32)
        o_ref[...] = f(s_ref[0]) + f(got_ref[0]) + f(got_ref[1]) + f(got_ref[2])

    return pl.pallas_call(
        body, grid_spec=pltpu.PrefetchScalarGridSpec(
            num_scalar_prefetch=1, grid=(nb,),
            in_specs=[pl.BlockSpec((1, tr, cols), lambda i, p_ref: (p_ref[0], i, 0)),
                      pl.BlockSpec((3, tr, cols), lambda i, p_ref: (0, i, 0))],
            out_specs=pl.BlockSpec((tr, cols), lambda i, p_ref: (p_ref[1] * nb + i, 0))),
        out_shape=jax.ShapeDtypeStruct((2 * hr, cols), F32), name=name,
        compiler_params=_cp(("arbitrary",)))(place, sums, got)


def _rs_chip_copies(ins, got, send, recv):
    x, y, c = _place()
    peers = ((1 - x, y), (x, 1 - y), (1 - x, 1 - y))
    return [pltpu.make_async_remote_copy(src_ref=ins[j].at[2 * px + py], dst_ref=got[j].at[k],
                                         send_sem=send.at[3 * j + k], recv_sem=recv.at[3 * j + k],
                                         device_id=(px, py, c), device_id_type=MESH)
            for j in range(len(ins)) for k, (px, py) in enumerate(peers)]


def _ag_sibling(fulls):
    n = len(fulls)
    nck = 4

    def body(*refs):
        outs = refs[n:2 * n]
        send, recv = refs[2 * n:]
        x, y, c = _place()
        copies = []
        for j in range(n):
            qr = outs[j].shape[0] // (2 * nck)
            for k in range(nck):
                rows = outs[j].at[pl.ds(pl.multiple_of((c * nck + k) * qr, 8), qr), :]
                cp = pltpu.make_async_remote_copy(src_ref=rows, dst_ref=rows, send_sem=send.at[nck * j + k],
                                                  recv_sem=recv.at[nck * j + k], device_id=(x, y, 1 - c),
                                                  device_id_type=MESH)
                cp.start()
                copies.append(cp)
        for cp in copies:
            cp.wait()

    return pl.pallas_call(
        body, in_specs=[_ANY] * n, out_specs=[_ANY] * n,
        out_shape=[jax.ShapeDtypeStruct(f.shape, F32) for f in fulls],
        input_output_aliases={j: j for j in range(n)},
        scratch_shapes=[pltpu.SemaphoreType.DMA((nck * n,)), pltpu.SemaphoreType.DMA((nck * n,))],
        name="ag_sibling")(*fulls)


def _allreduce_small(buf):
    rows = buf.shape[0]
    pr = rows // 8

    def body(in_ref, out_ref, stage, send1, recv1, send2, recv2):
        x, y, c = _place()
        me = 4 * x + 2 * y + c

        def peer(k):
            kx, ky, kc = (k >> 2) & 1, (k >> 1) & 1, k & 1
            return (x ^ kx, y ^ ky, c ^ kc)

        def piece(ref, idx):
            return ref.at[pl.ds(pl.multiple_of(idx * pr, 8), pr), :]

        copies = []
        for k in range(1, 8):
            px, py, pc = peer(k)
            cp = pltpu.make_async_remote_copy(src_ref=piece(in_ref, 4 * px + 2 * py + pc), dst_ref=stage.at[k],
                                              send_sem=send1.at[k], recv_sem=recv1.at[k],
                                              device_id=(px, py, pc), device_id_type=MESH)
            cp.start()
            copies.append(cp)
        for cp in copies:
            cp.wait()
        acc = piece(in_ref, me)[...]
        for k in range(1, 8):
            acc = acc + stage[k]
        piece(out_ref, me)[...] = acc
        copies = []
        for k in range(1, 8):
            cp = pltpu.make_async_remote_copy(src_ref=piece(out_ref, me), dst_ref=piece(out_ref, me),
                                              send_sem=send2.at[k], recv_sem=recv2.at[k],
                                              device_id=peer(k), device_id_type=MESH)
            cp.start()
            copies.append(cp)
        for cp in copies:
            cp.wait()

    vm = pl.BlockSpec(memory_space=pltpu.VMEM)
    return pl.pallas_call(
        body, in_specs=[vm], out_specs=vm, out_shape=jax.ShapeDtypeStruct((rows, D), F32),
        scratch_shapes=[pltpu.VMEM((8, pr, D), F32)] + [pltpu.SemaphoreType.DMA((8,))] * 4,
        name="allreduce_small", compiler_params=_cp())(buf)


def _rows(a):
    flat = a.reshape(-1)
    pad = (-flat.shape[0]) % D
    if pad:
        flat = jnp.concatenate([flat, jnp.zeros((pad,), flat.dtype)])
    return flat.reshape(-1, D)


def _pad_rows(a, mult):
    pad = (-a.shape[0]) % mult
    return jnp.concatenate([a, jnp.zeros((pad, a.shape[1]), a.dtype)]) if pad else a


def _local_step(x, c, ctx, c_ctx, tgt, me, shard, sh, b_mod, b_in, norm_g, cb, wr, wi, ln_g, ln_b):
    t_lat, t_ctx = x.shape[0], ctx.shape[0]
    nlt = t_lat // 256
    nlb, ncb = t_lat // RB, t_ctx // RB
    mc = 3 * D // NSH
    mod_all, cc_all = _mod_tp(jnp.zeros((8, D), F32).at[0].set(c), c_ctx.reshape(1, D), sh["w_mod"],
                              lax.dynamic_slice_in_dim(b_mod, shard * mc, mc, axis=1))
    mod = jnp.zeros((16, 3 * D), F32).at[0].set(mod_all[me]).at[1].set(mod_all[8])
    cc = jnp.zeros((16, D), F32).at[0].set(c).at[1].set(c_ctx)
    x_all = jnp.concatenate([x, ctx], axis=0)
    z, u_all, w_in_g, nar = _inproj_ag(x_all, mod, sh["w_in"], b_in, sh["narrow"], nlt)
    nar = jnp.transpose(nar, (1, 0, 2)).reshape(-1, D)
    lbl, cw, br, bi, lam = nar[0:4].reshape(2, 2, D), nar[4:8], nar[8:10], nar[10:12], nar[12:14]
    o0, st0, (w_mod_g, p_a, p_b, w_out) = _gla_fwd(z, lbl, 0, nlb, ncb,
                                                   gather=[sh[k] for k in ("w_mod", "p_a", "p_b", "w_out")])
    p_a, p_b, w_out = p_a.reshape(D, D), p_b.reshape(D, D), w_out.reshape(D, D)
    o1, st1, _ = _gla_fwd(z, lbl, 1, nlb, ncb)
    h0, hp0, a0, hpc0 = _rglru_fwd(z, cw, cb, wr, br, wi, bi, lam, 0, t_lat, t_ctx)
    h1, hp1, a1, hpc1 = _rglru_fwd(z, cw, cb, wr, br, wi, bi, lam, 1, t_lat, t_ctx)
    (do, dh, dz4, dz6, dz7, dz8, gxres, y, dout, oa, dpa, obv, dpb, acc) = _merge(
        o0, o1, h0, h1, z, x_all, tgt, mod, norm_g, ln_g, ln_b, p_a, p_b, w_out, t_lat)
    gp_a = _wgrad(oa, dpa, "wgrad_pa")
    gp_b = _wgrad(obv, dpb, "wgrad_pb")
    gw_out = _wgrad(y, dout, "wgrad_wout")
    dxc_lat, dxc_ctx, dwr0, dwi0, sb0 = _rglru_bwd(z, cw, cb, wr, br, wi, bi, lam, dh, hp0, a0, hpc0, 0, t_lat, t_ctx)
    dz5_lat, dz5_ctx, dwr1, dwi1, sb1 = _rglru_bwd(z, cw, cb, wr, br, wi, bi, lam, dh, hp1, a1, hpc1, 1, t_lat, t_ctx,
                                                   prev=(dxc_lat, dxc_ctx))
    dq0, dv0, dz1, sa0 = _gla_bwd(z, lbl, do, st0, 0, nlb, ncb)
    dz0, dz3, dz2, sa1 = _gla_bwd(z, lbl, do, st1, 1, nlb, ncb, prev=(dq0, dv0))
    dz = jnp.concatenate([dz0, dz1, dz2, dz3, dz4, jnp.concatenate([dz5_lat, dz5_ctx], axis=0).astype(BF),
                          dz6, dz7, dz8], axis=1)
    big = dict(w_in=_wgrad_in(u_all, dz), p_a=gp_a.reshape(NSH, D // NSH, D),
               p_b=gp_b.reshape(NSH, D // NSH, D), w_out=gw_out.reshape(NSH, D // NSH, D))
    grads = [big[k] for k in _RS]
    sums = [_rs_add1(g, b, f"rs_add1_{j}") for j, (g, b) in enumerate(zip(grads, _rs_sibling(grads)))]
    gx, dm, got = _du(dz, w_in_g, x_all, mod, gxres, nlt, sums)
    fulls = [_rs_add2(a, b, f"rs_add2_{j}") for j, (a, b) in enumerate(zip(sums, got))]
    big = dict(zip(_RS, _ag_sibling(fulls)))
    dmod = jnp.zeros((16, 3 * D), F32).at[0:2, 0:2 * D].set(dm[0:2]).at[0, 2 * D:].set(acc[0])
    dcc = _mod_bwd(cc, dmod, w_mod_g)
    small = dict(
        c_ctx=dcc[1:2], b_mod=(dmod[0] + dmod[1]).reshape(3, D),
        b_in=jnp.stack([sa1[2], sa0[0], sa1[0], sa1[3], acc[3], sb1[4], acc[4], acc[5], acc[6]]),
        lb_logits=jnp.stack([sa0[1], sa1[1], -sa0[1], -sa1[1]]),
        norm_a_g=acc[7:8], conv_w=sb1[8:12], conv_b=sb1[3:4],
        w_r=jnp.stack([dwr0, dwr1]).reshape(-1, D), w_i=jnp.stack([dwi0, dwi1]).reshape(-1, D),
        b_r=jnp.stack([sb0[0], sb1[0]]), b_i=jnp.stack([sb0[1], sb1[1]]), lam=jnp.stack([sb0[2], sb1[2]]),
        ln_g=acc[1:2], ln_b=acc[2:3])
    return acc[9, 0], gx, big, small, dmod, cc_all


_RS =("w_in", "p_a", "p_b", "w_out")
_SMALL =("c_ctx", "b_mod", "b_in", "lb_logits", "norm_a_g", "conv_w", "conv_b", "w_r", "w_i", "b_r", "b_i", "lam",
          "ln_g", "ln_b")
_BIG = ("w_mod", "w_in", "p_a", "p_b", "w_out")
_COL_SHARDED = ("lb_logits", "conv_w", "b_r", "b_i", "lam")
_WEIGHTS = ("c_ctx", "w_mod", "b_mod", "w_in", "b_in", "lb_logits", "norm_a_g", "conv_w", "conv_b", "w_r", "b_r", "w_i",
            "b_i", "lam", "p_a", "p_b", "w_out", "ln_g", "ln_b")


def kernel(x, c, ctx, c_ctx, w_mod, b_mod, w_in, b_in, lb_logits, norm_a_g, conv_w, conv_b, w_r, b_r, w_i, b_i, lam, p_a, p_b, w_out, ln_g, ln_b, loss_target, m_c_ctx, m_w_mod, m_b_mod, m_w_in, m_b_in, m_lb_logits, m_norm_a_g, m_conv_w, m_conv_b, m_w_r, m_b_r, m_w_i, m_b_i, m_lam, m_p_a, m_p_b, m_w_out, m_ln_g, m_ln_b, v_c_ctx, v_w_mod, v_b_mod, v_w_in, v_b_in, v_lb_logits, v_norm_a_g, v_conv_w, v_conv_b, v_w_r, v_b_r, v_w_i, v_b_i, v_lam, v_p_a, v_p_b, v_w_out, v_ln_g, v_ln_b):
    w = dict(c_ctx=c_ctx, w_mod=w_mod, b_mod=b_mod, w_in=w_in, b_in=b_in, lb_logits=lb_logits, norm_a_g=norm_a_g,
             conv_w=conv_w, conv_b=conv_b, w_r=w_r, b_r=b_r, w_i=w_i, b_i=b_i, lam=lam, p_a=p_a, p_b=p_b, w_out=w_out,
             ln_g=ln_g, ln_b=ln_b)
    m = dict(c_ctx=m_c_ctx, w_mod=m_w_mod, b_mod=m_b_mod, w_in=m_w_in, b_in=m_b_in, lb_logits=m_lb_logits,
             norm_a_g=m_norm_a_g, conv_w=m_conv_w, conv_b=m_conv_b, w_r=m_w_r, b_r=m_b_r, w_i=m_w_i, b_i=m_b_i,
             lam=m_lam, p_a=m_p_a, p_b=m_p_b, w_out=m_w_out, ln_g=m_ln_g, ln_b=m_ln_b)
    v = dict(c_ctx=v_c_ctx, w_mod=v_w_mod, b_mod=v_b_mod, w_in=v_w_in, b_in=v_b_in, lb_logits=v_lb_logits,
             norm_a_g=v_norm_a_g, conv_w=v_conv_w, conv_b=v_conv_b, w_r=v_w_r, b_r=v_b_r, w_i=v_w_i, b_i=v_b_i,
             lam=v_lam, p_a=v_p_a, p_b=v_p_b, w_out=v_w_out, ln_g=v_ln_g, ln_b=v_ln_b)
    shard = 2 * lax.axis_index("x") + lax.axis_index("y")
    cs = D // NSH

    sh = {k: w[k][0].astype(BF) for k in _BIG}
    sh["narrow"] = _pad_rows(jnp.concatenate([lb_logits.reshape(4, cs), conv_w[0], b_r[0], b_i[0], lam[0]], axis=0), 8)
    me = 2 * shard + lax.axis_index("c")
    loss, gx, big, small, dmod, cc_all = _local_step(
        x[0], c[0], ctx[0], c_ctx, loss_target[0], me, shard, sh, b_mod, b_in, norm_a_g, conv_b, w_r[0], w_i[0],
        ln_g, ln_b)
    loss = lax.psum(loss, ("x", "y", "c"))

    dmod_rows = jnp.zeros((16, 3 * D), F32).at[me].set(dmod[0]).at[8].set(dmod[1]).reshape(48, D)
    sizes = [small[k].shape[0] for k in _SMALL]
    red = _allreduce_small(_pad_rows(jnp.concatenate([_pad_rows(small[k], 8) for k in _SMALL] + [dmod_rows],
                                                     axis=0), 64))
    grads = {}
    off = 0
    for k, n in zip(_SMALL, sizes):
        g = red[off:off + n]
        off += n + (-n) % 8
        if k == "norm_a_g":
            g = g[:, :DH]
        if k in _COL_SHARDED:
            g = lax.dynamic_slice_in_dim(g, shard * cs, cs, axis=1)
        grads[k] = g.reshape(w[k].shape)
    for k in _RS:
        grads[k] = big[k].reshape(w[k].shape)
    dmod_all = red[off:off + 48].reshape(16, 3 * D)
    mc = 3 * D // NSH
    grads["w_mod"] = _wmod_grad(cc_all, lax.dynamic_slice_in_dim(dmod_all, shard * mc, mc, axis=1)).reshape(
        w["w_mod"].shape)

    delta, new_m, new_v = {}, {}, {}
    for k in _BIG:
        shp = w[k].shape
        two = lambda a: a.reshape(shp[-2], shp[-1])
        d_, m_, v_ = _adamw(two(w[k]), two(grads[k]), two(m[k]), two(v[k]), f"adamw_{k}")
        delta[k], new_m[k], new_v[k] = d_.reshape(shp), m_.reshape(shp), v_.reshape(shp)
    d_, m_, v_ = _adamw_many(*[[t[k] for k in _SMALL] for t in (w, grads, m, v)])
    delta.update(zip(_SMALL, d_))
    new_m.update(zip(_SMALL, m_))
    new_v.update(zip(_SMALL, v_))

    return (loss, gx[None], *[grads[k] for k in _WEIGHTS], *[delta[k] for k in _WEIGHTS],
            *[new_m[k] for k in _WEIGHTS], *[new_v[k] for k in _WEIGHTS])
```

```python
import functools

import jax
import jax.numpy as jnp
from jax import lax
from jax.experimental import pallas as pl
from jax.experimental.pallas import tpu as pltpu

F32 = jnp.float32
BF = jnp.bfloat16
MESH = pl.DeviceIdType.MESH

D = 1024
H = 8
DH = 128
CH = 64
RB = 256
NCK = RB // CH
GRID_W = 64
CB = 8
RCH = 16
IN_COLS = 9 * D
NSH = 4
SHC = IN_COLS // NSH
CUT2 = SHC - 2 * D
CUT4 = 2 * SHC - 4 * D
CUT6 = 3 * SHC - 6 * D
RG_C = 8.0
ALPHA = 2.0 ** 0.25
LN_EPS = 1e-5
RMS_EPS = 1e-6
Q_SCALE = DH ** -0.5
ADAM_LR, ADAM_B1, ADAM_B2, ADAM_EPS, ADAM_WD, ADAM_STEP = 0.001, 0.9, 0.999, 1e-08, 0.01, 10
VMEM_LIMIT = 56 * 1024 * 1024
VMEM_LIMIT_MERGE = 60 * 1024 * 1024


def _cp(sem=None, vmem=VMEM_LIMIT):
    return pltpu.CompilerParams(dimension_semantics=sem, vmem_limit_bytes=vmem)


def _sig(x):
    return 0.5 * jnp.tanh(0.5 * x) + 0.5


def _dot(a, b):
    return jnp.dot(a.astype(BF), b.astype(BF), preferred_element_type=F32)


def _dot_nt(a, b):
    return lax.dot_general(a.astype(BF), b.astype(BF), (((1,), (1,)), ((), ())), preferred_element_type=F32)


def _dot_tn(a, b):
    return lax.dot_general(a.astype(BF), b.astype(BF), (((0,), (0,)), ((), ())), preferred_element_type=F32)


def _colsum(v):
    return jnp.sum(v, axis=0, keepdims=True)


def _mod_bwd(cc, dmod, w_mod_g):
    def body(cc_ref, dm_ref, w_ref, dcc_ref):
        v = cc_ref[...]
        sg = _sig(v)
        ds = jnp.zeros((16, D), F32)
        for k in range(NSH):
            ds = ds + _dot_nt(dm_ref[:, k * 768:(k + 1) * 768], w_ref[k])
        dcc_ref[...] = ds * (sg * (1.0 + v * (1.0 - sg)))
    return pl.pallas_call(body, out_shape=jax.ShapeDtypeStruct((16, D), F32),
                          name="mod_bwd", compiler_params=_cp())(cc, dmod, w_mod_g)


def _wmod_grad(cc, dmod_cols):
    def body(cc_ref, dm_ref, dw_ref):
        v = cc_ref[...]
        dw_ref[...] = _dot_tn(v * _sig(v), dm_ref[...])
    return pl.pallas_call(body, out_shape=jax.ShapeDtypeStruct((D, dmod_cols.shape[1]), F32),
                          name="wmod_grad", compiler_params=_cp())(cc, dmod_cols)


def _modulate(x_ref, mod_ref, is_lat):
    sh = jnp.where(is_lat, mod_ref[0:1, 0:D], mod_ref[1:2, 0:D])
    sc = jnp.where(is_lat, mod_ref[0:1, D:2 * D], mod_ref[1:2, D:2 * D])
    return x_ref[...] * (1.0 + sc) + sh, sc


def _gla_mask(d, n):
    row = lax.broadcasted_iota(jnp.int32, (n, n), 0)
    col = lax.broadcasted_iota(jnp.int32, (n, n), 1)
    same = (row // CH) == (col // CH)
    return same & ((row <= col) if d else (row >= col))


def _chunk_cumsum(v, rev):
    n = v.shape[0]
    pos = lax.broadcasted_iota(jnp.int32, v.shape, 0) & (CH - 1)
    s = 1
    while s < CH:
        if rev:
            v = v + jnp.where(pos < CH - s, pltpu.roll(v, n - s, 0), 0.0)
        else:
            v = v + jnp.where(pos >= s, pltpu.roll(v, s, 0), 0.0)
        s *= 2
    return v


def _chunk_rows(c):
    return slice(c * CH, (c + 1) * CH)


def _gla_features(zq, zf, lb, d):
    sq = _sig(zq)
    q = zq * sq * Q_SCALE
    sf = _sig(zf)
    f = lb + (1.0 - lb) * sf
    k = 1.0 - f
    g = _chunk_cumsum(jnp.log(f), d)
    last = 0 if d else CH - 1
    gls = [g[c * CH + last:c * CH + last + 1, :] for c in range(NCK)]
    glb = jnp.concatenate([jnp.broadcast_to(gl, (CH, D)) for gl in gls], axis=0)
    eg, eig, eeg = jnp.exp(g), jnp.exp(-g), jnp.exp(glb - g)
    decs = [jnp.exp(gl) for gl in gls]
    return sq, sf, f, q * eg, k * eig, k * eeg, eg, eig, eeg, decs


def _lower_bound(lbl_ref, d):
    return _sig(lbl_ref[0, d:d + 1, :] - lbl_ref[1, d:d + 1, :])


def _gla_rb(d, nlb, ncb):
    nrb = nlb + ncb
    if d:
        return lambda s: nrb - 1 - s
    return lambda s: jnp.where(s < ncb, nlb + s, s - ncb)


def _gla_fwd(z, lbl, d, nlb, ncb, gather=()):
    m = z.shape[0]
    nrb = nlb + ncb
    rb = _gla_rb(d, nlb, ncb)
    ng = len(gather)

    def body(*refs):
        q_ref, f_ref, v_ref, lbl_ref = refs[:4]
        ag_in = refs[4:4 + ng]
        o_ref, st_ref = refs[4 + ng:6 + ng]
        ag_out = refs[6 + ng:6 + 2 * ng]
        S = refs[6 + 2 * ng]
        ag_sems = refs[7 + 2 * ng:]
        s = pl.program_id(0)

        @pl.when(s == 0)
        def _():
            S[...] = jnp.zeros_like(S)
            if ng:
                local, chip, _ = _ag_copies(ag_in, ag_out, *ag_sems)
                for cp in chip + local:
                    cp.start()

        lb = _lower_bound(lbl_ref, d)
        mb = _gla_mask(d, CH)
        _, _, _, qd, ki, ke, _, _, _, decs = _gla_features(q_ref[...], f_ref[...], lb, d)
        qd, ki, ke, v = qd.astype(BF), ki.astype(BF), ke.astype(BF), v_ref[...].astype(BF)
        order = range(NCK - 1, -1, -1) if d else range(NCK)
        for h in range(H):
            hs = slice(h * DH, (h + 1) * DH)
            intra, upd = {}, {}
            for c in range(NCK):
                rs = _chunk_rows(c)
                a = jnp.where(mb, _dot_nt(qd[rs, hs], ki[rs, hs]), 0.0)
                intra[c] = _dot(a, v[rs, hs])
                upd[c] = _dot_tn(v[rs, hs], ke[rs, hs])
            st = S[h]
            for c in order:
                rs = _chunk_rows(c)
                st_ref[c, h] = st
                o_ref[rs, hs] = intra[c] + _dot_nt(qd[rs, hs], st)
                st = st * decs[c][:, hs] + upd[c]
            S[h] = st

        if ng:
            @pl.when(s == nrb - 1)
            def _():
                _ag_finish(*_ag_copies(ag_in, ag_out, *ag_sems))

    def zspec(cb):
        return pl.BlockSpec((RB, D), lambda s: (rb(s), cb))

    outs = pl.pallas_call(
        body, grid=(nrb,),
        in_specs=[zspec(0), zspec(1 + d), zspec(3), pl.BlockSpec((2, 2, D), lambda s: (0, 0, 0))] + [_ANY] * ng,
        out_specs=[pl.BlockSpec((RB, D), lambda s: (rb(s), 0)),
                   pl.BlockSpec((NCK, H, DH, DH), lambda s: (rb(s), 0, 0, 0))] + [_ANY] * ng,
        out_shape=[jax.ShapeDtypeStruct((m, D), F32), jax.ShapeDtypeStruct((m // CH, H, DH, DH), F32)]
        + [jax.ShapeDtypeStruct((NSH,) + g.shape, g.dtype) for g in gather],
        scratch_shapes=[pltpu.VMEM((H, DH, DH), F32)] + (_ag_sems(ng) if ng else []),
        name=f"gla_fwd{d}", compiler_params=_cp(("arbitrary",)))(z, z, z, lbl, *gather)
    return outs[0], outs[1], list(outs[2:])


def _gla_bwd(z, lbl, do_lat, states, d, nlb, ncb, prev=None):
    m = z.shape[0]
    nrb = nlb + ncb
    fwd_rb = _gla_rb(d, nlb, ncb)
    rb = lambda s: fwd_rb(nrb - 1 - s)
    last = prev is not None

    def body(*refs):
        if last:
            (q_ref, f_ref, v_ref, lbl_ref, do_ref, st_ref, pq_ref, pv_ref, dz1_ref, dz4a_ref, sh0_ref, sh1_ref, sum_ref,
             dS) = refs
        else:
            q_ref, f_ref, v_ref, lbl_ref, do_ref, st_ref, o0_ref, o1_ref, o2_ref, sum_ref, dS = refs
        s = pl.program_id(0)
        is_lat = rb(s) < nlb

        @pl.when(s == 0)
        def _():
            dS[...] = jnp.zeros_like(dS)
            sum_ref[...] = jnp.zeros_like(sum_ref)

        lb = _lower_bound(lbl_ref, d)
        mb = _gla_mask(d, RB)
        zq = q_ref[...]
        sq, sf, f, qd, ki, ke, eg, eig, eeg, decs = _gla_features(zq, f_ref[...], lb, d)
        qdb, kib, keb, vb = qd.astype(BF), ki.astype(BF), ke.astype(BF), v_ref[...].astype(BF)
        dob = jnp.where(is_lat, do_ref[...], 0.0).astype(BF)
        order = range(NCK) if d else range(NCK - 1, -1, -1)
        dqd_h, dki_h, dke_h, dv_h, ddec_h = [], [], [], [], []
        for h in range(H):
            hs = slice(h * DH, (h + 1) * DH)
            a = jnp.where(mb, _dot_nt(qdb[:, hs], kib[:, hs]), 0.0).astype(BF)
            da = jnp.where(mb, _dot_nt(dob[:, hs], vb[:, hs]), 0.0).astype(BF)
            dqd_i = _dot(da, kib[:, hs])
            dki_h.append(_dot_tn(da, qdb[:, hs]))
            dvi = _dot_tn(a, dob[:, hs])
            dqd, inc = {}, {}
            for c in range(NCK):
                rs = _chunk_rows(c)
                dqd[c] = dqd_i[rs, :] + _dot(dob[rs, hs], st_ref[c, h])
                inc[c] = _dot_tn(dob[rs, hs], qdb[rs, hs])
            dst = dS[h]
            dke, dv, ddec = {}, {}, {}
            for c in order:
                rs = _chunk_rows(c)
                dv[c] = dvi[rs, :] + _dot_nt(keb[rs, hs], dst)
                dke[c] = _dot(vb[rs, hs], dst)
                ddec[c] = _colsum(st_ref[c, h] * dst)
                dst = inc[c] + dst * decs[c][:, hs]
            dS[h] = dst
            cat = lambda t: jnp.concatenate([t[c] for c in range(NCK)], axis=0)
            dqd_h.append(cat(dqd))
            dke_h.append(cat(dke))
            dv_h.append(cat(dv))
            ddec_h.append([ddec[c] for c in range(NCK)])
        lanes = lambda parts: jnp.concatenate(parts, axis=1)
        dqd, dki, dke, dv = lanes(dqd_h), lanes(dki_h), lanes(dke_h), lanes(dv_h)
        dq = dqd * eg
        dk = dki * eig + dke * eeg
        dke_ke = dke * ke
        dg = dqd * qd - dki * ki - dke_ke
        dgl = [_colsum(dke_ke[_chunk_rows(c), :]) + lanes([ddec_h[h][c] for h in range(H)]) * decs[c]
               for c in range(NCK)]
        dglb = jnp.concatenate([jnp.broadcast_to(t, (CH, D)) for t in dgl], axis=0)
        df = (_chunk_cumsum(dg, 1 - d) + dglb) / f - dk
        dzf = df * (1.0 - lb) * (sf * (1.0 - sf))
        sum_ref[0:1, :] += _colsum(dzf)
        sum_ref[1:2, :] += _colsum(df * (1.0 - sf))
        if last:
            dz0 = (dq + pq_ref[...]) * (Q_SCALE * (sq * (1.0 + zq * (1.0 - sq))))
            dz3 = dv + pv_ref[...]
            sum_ref[2:3, :] += _colsum(dz0)
            sum_ref[3:4, :] += _colsum(dz3)
            dz2 = dzf.astype(BF)
            sh0_ref[:, 0:D] = dz0.astype(BF)
            sh0_ref[:, D:2 * D] = dz1_ref[...]
            sh0_ref[:, 2 * D:] = dz2[:, :CUT2]
            sh1_ref[:, 0:D - CUT2] = dz2[:, CUT2:]
            sh1_ref[:, D - CUT2:2 * D - CUT2] = dz3.astype(BF)
            sh1_ref[:, 2 * D - CUT2:] = dz4a_ref[...]
        else:
            o0_ref[...] = dq
            o1_ref[...] = dv
            o2_ref[...] = dzf.astype(BF)

        @pl.when(s == nrb - 1)
        def _():
            sum_ref[1:2, :] = sum_ref[1:2, :] * (lb * (1.0 - lb))

    def zspec(cb):
        return pl.BlockSpec((RB, D), lambda s: (rb(s), cb))

    rowspec = pl.BlockSpec((RB, D), lambda s: (rb(s), 0))
    in_specs = [zspec(0), zspec(1 + d), zspec(3), pl.BlockSpec((2, 2, D), lambda s: (0, 0, 0)),
                pl.BlockSpec((RB, D), lambda s: (jnp.minimum(rb(s), nlb - 1), 0)),
                pl.BlockSpec((NCK, H, DH, DH), lambda s: (rb(s), 0, 0, 0))]
    args = [z, z, z, lbl, do_lat, states]
    sumspec = pl.BlockSpec((8, D), lambda s: (0, 0))
    if last:
        in_specs += [rowspec, rowspec, rowspec, pl.BlockSpec((RB, CUT4), lambda s: (rb(s), 0))]
        args += list(prev)
        shspec = pl.BlockSpec((RB, SHC), lambda s: (rb(s), 0))
        out_specs = (shspec, shspec, sumspec)
        out_shape = (jax.ShapeDtypeStruct((m, SHC), BF), jax.ShapeDtypeStruct((m, SHC), BF))
    else:
        out_specs = (rowspec, rowspec, rowspec, sumspec)
        out_shape = (jax.ShapeDtypeStruct((m, D), F32), jax.ShapeDtypeStruct((m, D), F32),
                     jax.ShapeDtypeStruct((m, D), BF))
    return pl.pallas_call(
        body, grid=(nrb,), in_specs=in_specs, out_specs=out_specs,
        out_shape=out_shape + (jax.ShapeDtypeStruct((8, D), F32),),
        scratch_shapes=[pltpu.VMEM((H, DH, DH), F32)],
        name=f"gla_bwd{d}", compiler_params=_cp(("arbitrary",)))(*args)


def _shift_rows(v, k, r):
    row = lax.broadcasted_iota(jnp.int32, v.shape, 0)
    rolled = pltpu.roll(v, k % r, 0)
    return jnp.where((row >= k) & (row < r + k), rolled, 0.0)


def _conv_fwd(xb, cw_ref, cb_ref, r):
    return (cb_ref[...] + _shift_rows(xb, 1, r) * cw_ref[0:1, :] + xb * cw_ref[1:2, :]
            + _shift_rows(xb, -1, r) * cw_ref[2:3, :] + _shift_rows(xb, -2, r) * cw_ref[3:4, :])


def _window(ref, lo, n, rows):
    parts = []
    if lo < 0:
        parts.append(jnp.zeros((-lo,) + tuple(ref.shape[1:]), F32))
    parts.append(ref[max(lo, 0):min(lo + n, rows)])
    if lo + n > rows:
        parts.append(jnp.zeros((lo + n - rows,) + tuple(ref.shape[1:]), F32))
    return parts[0] if len(parts) == 1 else jnp.concatenate(parts, axis=0)


def _conv_cols(x_ref, cw_ref, cb_ref, r0, n, rows):
    w = _window(x_ref, r0 - 1, n + 3, rows)
    return (cb_ref[...] + w[0:n] * cw_ref[0:1, :] + w[1:n + 1] * cw_ref[1:2, :] + w[2:n + 2] * cw_ref[2:3, :]
            + w[3:n + 3] * cw_ref[3:4, :])


def _softplus_neg(lam):
    y = jnp.exp(-jnp.abs(lam))
    u = 1.0 + y
    tiny = u == 1.0
    l1p = jnp.where(tiny, y, jnp.log(u) * (y / jnp.where(tiny, 1.0, u - 1.0)))
    return jnp.maximum(-lam, 0.0) + l1p


def _gates(xc, wr_ref, br, wi_ref, bi, sp):
    xcb = xc.astype(BF)
    rs, is_ = [], []
    for g in range(H):
        gs = slice(g * DH, (g + 1) * DH)
        rs.append(jnp.dot(xcb[:, gs], wr_ref[g].astype(BF), preferred_element_type=F32))
        is_.append(jnp.dot(xcb[:, gs], wi_ref[g].astype(BF), preferred_element_type=F32))
    r = _sig(jnp.concatenate(rs, axis=1) + br)
    i = _sig(jnp.concatenate(is_, axis=1) + bi)
    log_a = (-RG_C * r) * sp
    a = jnp.exp(log_a)
    t = jnp.tanh(log_a)
    om = (-2.0 * t) / (1.0 - t)
    return r, i, a, om


def _scan_rows(d, nrows, a_s, b_s, h_s, h0):
    nsl = nrows // 8

    def slab(j, h):
        jj = (nsl - 1 - j) if d else j
        r0 = pl.multiple_of(jj * 8, 8)
        for t in (range(7, -1, -1) if d else range(8)):
            h = a_s[pl.ds(r0 + t, 1), :] * h + b_s[pl.ds(r0 + t, 1), :]
            h_s[pl.ds(r0 + t, 1), :] = h
        return h

    return lax.fori_loop(0, nsl, slab, h0)


def _col_of(d, ncols):
    if d:
        return lambda s: ncols - jnp.maximum(s, 1)
    return lambda s: jnp.maximum(s, 1) - 1


def _rglru_fwd(z, cw, cb, wr, br, wi, bi, lam, d, t_lat, t_ctx):
    m = z.shape[0]
    rows = t_lat // GRID_W
    z3 = z.reshape(m // GRID_W, GRID_W, IN_COLS)
    nblk = GRID_W // CB
    blk = _col_of(d, nblk)
    cblk = t_lat // t_ctx
    rc = min(RCH, rows)

    def body(zc_ref, zx_ref, cw_ref, cb_ref, wr_ref, br_ref, wi_ref, bi_ref, lam_ref,
             hx_ref, hpx_ref, ax_ref, hpc_ref, a_s, b_s, h_s, hcar, a3, b3, cin_s):
        s = pl.program_id(0)
        sp = _softplus_neg(lam_ref[d:d + 1, :])
        br_ = br_ref[d:d + 1, :]
        bi_ = bi_ref[d:d + 1, :]

        @pl.when(s == 0)
        def _():
            xc = _conv_fwd(zc_ref[...], cw_ref, cb_ref, t_ctx)
            _, i, a, om = _gates(xc, wr_ref, br_, wi_ref, bi_, sp)
            a_s[...] = a
            b_s[...] = jnp.sqrt(om) * (i * xc)
            h0 = jnp.zeros((1, D), F32)
            hcar[0:1, :] = _scan_rows(d, t_ctx, a_s, b_s, h_s, h0)
            hs = h_s[...]
            row = lax.broadcasted_iota(jnp.int32, (t_ctx, D), 0)
            if d:
                hpc_ref[...] = jnp.where(row == t_ctx - 1, h0, pltpu.roll(hs, t_ctx - 1, 0))
            else:
                hpc_ref[...] = jnp.where(row == 0, h0, pltpu.roll(hs, 1, 0))

        @pl.when(s > 0)
        def _():
            for r0 in range(0, rows, rc):
                xc = _conv_cols(zx_ref, cw_ref, cb_ref, r0, rc, rows).reshape(rc * CB, D)
                _, i, a, om = _gates(xc, wr_ref, br_, wi_ref, bi_, sp)
                a3[r0:r0 + rc] = a.reshape(rc, CB, D)
                ax_ref[r0:r0 + rc] = a.reshape(rc, CB, D)
                b3[r0:r0 + rc] = (jnp.sqrt(om) * (i * xc)).reshape(rc, CB, D)

            def local(t, carry):
                hl, p = carry
                r = (rows - 1 - t) if d else t
                a = a3[r]
                hl = a * hl + b3[r]
                p = a * p
                b3[r] = hl
                a3[r] = p
                return hl, p

            hl, p = lax.fori_loop(0, rows, local, (jnp.zeros((CB, D), F32), jnp.ones((CB, D), F32)))
            cin = hcar[0:1, :]
            for j in (range(CB - 1, -1, -1) if d else range(CB)):
                cin_s[j:j + 1, :] = cin
                cin = hl[j:j + 1, :] + p[j:j + 1, :] * cin
            hcar[0:1, :] = cin
            c_in = cin_s[...]

            def fix(t, prev):
                r = (rows - 1 - t) if d else t
                h = b3[r] + a3[r] * c_in
                hx_ref[r] = h
                hpx_ref[r] = prev
                return h

            lax.fori_loop(0, rows, fix, c_in)

    full = lambda shp: pl.BlockSpec(shp, lambda s: (0,) * len(shp))
    colspec = pl.BlockSpec((rows, CB, D), lambda s: (0, blk(s), 0))
    outs = pl.pallas_call(
        body, grid=(nblk + 1,),
        in_specs=[pl.BlockSpec((t_ctx, D), lambda s: (cblk, 5)),
                  pl.BlockSpec((rows, CB, D), lambda s: (0, blk(s), 5)),
                  full((4, D)), full((1, D)),
                  pl.BlockSpec((None, H, DH, DH), lambda s: (d, 0, 0, 0)), full((2, D)),
                  pl.BlockSpec((None, H, DH, DH), lambda s: (d, 0, 0, 0)), full((2, D)), full((2, D))],
        out_specs=(colspec, colspec, colspec, full((t_ctx, D))),
        out_shape=(jax.ShapeDtypeStruct((rows, GRID_W, D), F32),) * 3 + (jax.ShapeDtypeStruct((t_ctx, D), F32),),
        scratch_shapes=[pltpu.VMEM((t_ctx, D), F32), pltpu.VMEM((t_ctx, D), F32), pltpu.VMEM((t_ctx, D), F32),
                        pltpu.VMEM((8, D), F32), pltpu.VMEM((rows, CB, D), F32), pltpu.VMEM((rows, CB, D), F32),
                        pltpu.VMEM((CB, D), F32)],
        name=f"rglru_fwd{d}", compiler_params=_cp(("arbitrary",)))(z, z3, cw, cb, wr, br, wi, bi, lam)
    return outs[0].reshape(t_lat, D), outs[1].reshape(t_lat, D), outs[2].reshape(t_lat, D), outs[3]


def _rglru_bwd(z, cw, cb, wr, br, wi, bi, lam, dh_lat, hp_lat, a_lat, hp_ctx, d, t_lat, t_ctx, prev=None):
    m = z.shape[0]
    rows = t_lat // GRID_W
    z3 = z.reshape(m // GRID_W, GRID_W, IN_COLS)
    nblk = GRID_W // CB
    fblk = _col_of(d, nblk)
    blk = lambda s: fblk(nblk - jnp.minimum(s, nblk - 1))
    cblk = t_lat // t_ctx
    rc = min(RCH, rows)
    last = prev is not None
    view3 = lambda v: v.reshape(rows, GRID_W, D)

    def body(*refs):
        (zc_ref, zx_ref, cw_ref, cb_ref, wr_ref, br_ref, wi_ref, bi_ref, lam_ref, dh_ref, hpx_ref, ax_ref,
         hpc_ref) = refs[:13]
        k = 13
        if last:
            pdx_ref, pdc_ref = refs[13:15]
            k = 15
        ox_ref, oc_ref, dwr_ref, dwi_ref, sum_ref, a_s, b_s, gcar, a3, b3, cin_s = refs[k:]
        s = pl.program_id(0)
        lam_d = lam_ref[d:d + 1, :]
        sp = _softplus_neg(lam_d)
        br_ = br_ref[d:d + 1, :]
        bi_ = bi_ref[d:d + 1, :]
        flat = lambda v: v.reshape(-1, D)

        @pl.when(s == 0)
        def _():
            gcar[...] = jnp.zeros_like(gcar)
            dwr_ref[...] = jnp.zeros_like(dwr_ref)
            dwi_ref[...] = jnp.zeros_like(dwi_ref)
            sum_ref[...] = jnp.zeros_like(sum_ref)


        def conv_sums(dxc, dxb, xm1, x0, xp1, xp2):
            sum_ref[3:4, :] += _colsum(flat(dxc))
            sum_ref[4:5, :] += _colsum(flat(dxb))
            sum_ref[8:9, :] += _colsum(flat(dxc * xm1))
            sum_ref[9:10, :] += _colsum(flat(dxc * x0))
            sum_ref[10:11, :] += _colsum(flat(dxc * xp1))
            sum_ref[11:12, :] += _colsum(flat(dxc * xp2))

        def gate_grads(g, hp, xc, rr, i, a, om):
            mult = jnp.sqrt(om)
            da = g * hp
            ixc = i * xc
            dmult = g * ixc
            dixc = g * mult
            di = dixc * xc
            dxc = dixc * i
            dlog_a = da * a - dmult * ((1.0 - om) / mult)
            dr = dlog_a * (-RG_C * sp)
            sum_ref[2:3, :] += _colsum(dlog_a * rr)
            drp = dr * rr * (1.0 - rr)
            dip = di * i * (1.0 - i)
            sum_ref[0:1, :] += _colsum(drp)
            sum_ref[1:2, :] += _colsum(dip)
            xcb = xc.astype(BF)
            drb = drp.astype(BF)
            dib = dip.astype(BF)
            parts = []
            for gi in range(H):
                gs = slice(gi * DH, (gi + 1) * DH)
                parts.append(_dot_nt(drb[:, gs], wr_ref[gi]) + _dot_nt(dib[:, gs], wi_ref[gi]))
                dwr_ref[gi] += _dot_tn(xcb[:, gs], drb[:, gs])
                dwi_ref[gi] += _dot_tn(xcb[:, gs], dib[:, gs])
            return dxc + jnp.concatenate(parts, axis=1)

        @pl.when(s < nblk)
        def _():
            def local(t, carry):
                c, q = carry
                r = t if d else (rows - 1 - t)
                a = ax_ref[r]
                c = a * (c + dh_ref[r])
                q = a * q
                b3[r] = c
                a3[r] = q
                return c, q

            c, q = lax.fori_loop(0, rows, local, (jnp.zeros((CB, D), F32), jnp.ones((CB, D), F32)))
            cin = gcar[0:1, :]
            for j in (range(CB) if d else range(CB - 1, -1, -1)):
                cin_s[j:j + 1, :] = cin
                cin = c[j:j + 1, :] + q[j:j + 1, :] * cin
            gcar[0:1, :] = cin
            c_in = cin_s[...]
            for r0 in (range(rows - rc, -1, -rc) if d else range(0, rows, rc)):
                if d:
                    lo = max(r0 - 1, 0)
                    cn = b3[lo:r0 + rc - 1] + a3[lo:r0 + rc - 1] * c_in
                    if r0 == 0:
                        cn = jnp.concatenate([c_in[None], cn], axis=0)
                else:
                    hi = min(r0 + rc + 1, rows)
                    cn = b3[r0 + 1:hi] + a3[r0 + 1:hi] * c_in
                    if hi == rows:
                        cn = jnp.concatenate([cn, c_in[None]], axis=0)
                g = flat(dh_ref[r0:r0 + rc] + cn)
                xc = flat(_conv_cols(zx_ref, cw_ref, cb_ref, r0, rc, rows))
                rr, i, a, om = _gates(xc, wr_ref, br_, wi_ref, bi_, sp)
                b3[r0:r0 + rc] = gate_grads(g, flat(hpx_ref[r0:r0 + rc]), xc, rr, i, a, om).reshape(rc, CB, D)
            if last:
                for r0 in range(0, rows, rc):
                    b3[r0:r0 + rc] = b3[r0:r0 + rc] + pdx_ref[r0:r0 + rc]
                for r0 in range(0, rows, rc):
                    w = _window(b3, r0 - 2, rc + 3, rows)
                    xw = _window(zx_ref, r0 - 1, rc + 3, rows)
                    dxc = w[2:rc + 2]
                    dxb = (w[3:rc + 3] * cw_ref[0:1, :] + dxc * cw_ref[1:2, :] + w[1:rc + 1] * cw_ref[2:3, :]
                           + w[0:rc] * cw_ref[3:4, :])
                    conv_sums(dxc, dxb, xw[0:rc], xw[1:rc + 1], xw[2:rc + 2], xw[3:rc + 3])
                    ox_ref[r0:r0 + rc] = dxb
            else:
                for r0 in range(0, rows, rc):
                    ox_ref[r0:r0 + rc] = b3[r0:r0 + rc]

        @pl.when(s == nblk)
        def _():
            r = t_ctx
            xb = zc_ref[...]
            xc = _conv_fwd(xb, cw_ref, cb_ref, r)
            rr, i, a, om = _gates(xc, wr_ref, br_, wi_ref, bi_, sp)
            a_s[...] = a
            b_s[...] = jnp.zeros((r, D), F32)
            c0 = gcar[0:1, :]
            _scan_rows(1 - d, r, a_s, b_s, b_s, c0)
            cs = b_s[...]
            row = lax.broadcasted_iota(jnp.int32, (r, D), 0)
            if d:
                g = jnp.where(row == 0, c0, pltpu.roll(cs, 1, 0))
            else:
                g = jnp.where(row == r - 1, c0, pltpu.roll(cs, r - 1, 0))
            dxc = gate_grads(g, hpc_ref[...], xc, rr, i, a, om)
            if last:
                dxc = dxc + pdc_ref[...]
                dxb = (_shift_rows(dxc, -1, r) * cw_ref[0:1, :] + dxc * cw_ref[1:2, :]
                       + _shift_rows(dxc, 1, r) * cw_ref[2:3, :] + _shift_rows(dxc, 2, r) * cw_ref[3:4, :])
                conv_sums(dxc, dxb, _shift_rows(xb, 1, r), xb, _shift_rows(xb, -1, r), _shift_rows(xb, -2, r))
                oc_ref[...] = dxb
            else:
                oc_ref[...] = dxc
            sum_ref[2:3, :] = sum_ref[2:3, :] * (RG_C * _sig(-lam_d))

    full = lambda shp: pl.BlockSpec(shp, lambda s: (0,) * len(shp))
    once = lambda shp: pl.BlockSpec(shp, lambda s: (0,) * len(shp), pipeline_mode=pl.Buffered(1))
    colspec = pl.BlockSpec((rows, CB, D), lambda s: (0, blk(s), 0))
    colonce = pl.BlockSpec((rows, CB, D), lambda s: (0, blk(s), 0), pipeline_mode=pl.Buffered(1))
    in_specs = [pl.BlockSpec((t_ctx, D), lambda s: (cblk, 5), pipeline_mode=pl.Buffered(1)),
                pl.BlockSpec((rows, CB, D), lambda s: (0, blk(s), 5), pipeline_mode=pl.Buffered(1)),
                full((4, D)), full((1, D)),
                pl.BlockSpec((None, H, DH, DH), lambda s: (d, 0, 0, 0)), full((2, D)),
                pl.BlockSpec((None, H, DH, DH), lambda s: (d, 0, 0, 0)), full((2, D)), full((2, D)),
                colonce, colonce, colonce, once((t_ctx, D))]
    args = [z, z3, cw, cb, wr, br, wi, bi, lam, view3(dh_lat), view3(hp_lat), view3(a_lat), hp_ctx]
    if last:
        in_specs += [colonce, once((t_ctx, D))]
        args += [view3(prev[0]), prev[1]]
    outs = pl.pallas_call(
        body, grid=(nblk + 1,), in_specs=in_specs,
        out_specs=(colspec, full((t_ctx, D)), full((H, DH, DH)), full((H, DH, DH)), full((16, D))),
        out_shape=(jax.ShapeDtypeStruct((rows, GRID_W, D), F32), jax.ShapeDtypeStruct((t_ctx, D), F32),
                   jax.ShapeDtypeStruct((H, DH, DH), F32), jax.ShapeDtypeStruct((H, DH, DH), F32),
                   jax.ShapeDtypeStruct((16, D), F32)),
        scratch_shapes=[pltpu.VMEM((t_ctx, D), F32), pltpu.VMEM((t_ctx, D), F32), pltpu.VMEM((8, D), F32),
                        pltpu.VMEM((rows, CB, D), F32), pltpu.VMEM((rows, CB, D), F32), pltpu.VMEM((CB, D), F32)],
        name=f"rglru_bwd{d}", compiler_params=_cp(("arbitrary",)))(*args)
    return (outs[0].reshape(t_lat, D), outs[1]) + tuple(outs[2:])


def _merge(o_f, o_b, h_f, h_b, z, x_all, tgt, mod, norm_g, ln_g, ln_b, p_a, p_b, w_out, t_lat):
    tm = 256
    nt = t_lat // tm

    def body(of_ref, ob_ref, hf_ref, hb_ref, z4_ref, z6_ref, z7_ref, z8_ref, x_ref, t_ref, mod_ref, ng_ref,
             lg_ref, lb_ref, pa_ref, pb_ref, wo_ref,
             do_ref, dh_ref, dz4a_ref, dz4b_ref, dz6a_ref, sh3_ref, gx_ref,
             y_ref, dout_ref, oa_ref, dpa_ref, obv_ref, dpb_ref, acc_ref):
        i = pl.program_id(0)
        lat = i < nt
        latf = lat.astype(F32)

        @pl.when(i == 0)
        def _():
            acc_ref[...] = jnp.zeros_like(acc_ref)

        def per_head(v):
            return jnp.concatenate(
                [jnp.broadcast_to(jnp.mean(v[:, h * DH:(h + 1) * DH], axis=-1, keepdims=True), (tm, DH))
                 for h in range(H)], axis=1)

        gt = mod_ref[0:1, 2 * D:3 * D]
        gfull = jnp.concatenate([ng_ref[...]] * H, axis=1)
        o = of_ref[...] + ob_ref[...]
        rinv = lax.rsqrt(per_head(o * o) + RMS_EPS)
        n = o * rinv
        na = n * gfull
        z4 = z4_ref[...]
        s4 = _sig(z4)
        silu4 = z4 * s4
        oa = na * silu4
        z6 = z6_ref[...]
        s6 = _sig(z6)
        silu6 = z6 * s6
        hsum = hf_ref[...] + hb_ref[...]
        obv = hsum * silu6
        pa = _dot(oa, pa_ref[...])
        pb = _dot(obv, pb_ref[...])
        s7 = _sig(z7_ref[...])
        s8 = _sig(z8_ref[...])
        y = s7 * pa + s8 * pb
        out = _dot(y, wo_ref[...])
        pre = ALPHA * x_ref[...] + gt * out
        mu = jnp.mean(pre, axis=-1, keepdims=True)
        xc = pre - mu
        rstd = lax.rsqrt(jnp.mean(xc * xc, axis=-1, keepdims=True) + LN_EPS)
        xhat = xc * rstd
        lg = lg_ref[...]
        diff = xhat * lg + lb_ref[...] - t_ref[...]
        acc_ref[8:9, :] += _colsum(diff * diff) * (0.5 / D * latf)
        dxn = diff * (1.0 / D)
        acc_ref[1:2, :] += _colsum(dxn * xhat) * latf
        acc_ref[2:3, :] += _colsum(dxn) * latf
        dxhat = dxn * lg
        dpre = rstd * (dxhat - jnp.mean(dxhat, axis=-1, keepdims=True)
                       - xhat * jnp.mean(dxhat * xhat, axis=-1, keepdims=True))
        gx_ref[...] = ALPHA * dpre
        acc_ref[0:1, :] += _colsum(dpre * out) * latf
        dout = dpre * gt
        dy = _dot_nt(dout, wo_ref[...])
        dpa = dy * s7
        dpb = dy * s8
        dz7 = dy * pa * (s7 * (1.0 - s7))
        dz8 = dy * pb * (s8 * (1.0 - s8))
        doa = _dot_nt(dpa, pa_ref[...])
        dob = _dot_nt(dpb, pb_ref[...])
        dh_ref[...] = dob * silu6
        dz6 = dob * hsum * (s6 * (1.0 + z6 * (1.0 - s6)))
        dna = doa * silu4
        dz4 = doa * na * (s4 * (1.0 + z4 * (1.0 - s4)))
        dng = _colsum(dna * n)
        acc_ref[7:8, 0:DH] += sum(dng[:, h * DH:(h + 1) * DH] for h in range(H)) * latf
        dn = dna * gfull
        do_ref[...] = rinv * (dn - n * per_head(dn * n))
        acc_ref[3:4, :] += _colsum(dz4) * latf
        acc_ref[4:5, :] += _colsum(dz6) * latf
        acc_ref[5:6, :] += _colsum(dz7) * latf
        acc_ref[6:7, :] += _colsum(dz8) * latf
        dz4b, dz6b = (dz4 * latf).astype(BF), (dz6 * latf).astype(BF)
        dz4a_ref[...] = dz4b[:, :CUT4]
        dz4b_ref[...] = dz4b[:, CUT4:]
        dz6a_ref[...] = dz6b[:, :CUT6]
        sh3_ref[:, 0:D - CUT6] = dz6b[:, CUT6:]
        sh3_ref[:, D - CUT6:2 * D - CUT6] = (dz7 * latf).astype(BF)
        sh3_ref[:, 2 * D - CUT6:] = (dz8 * latf).astype(BF)
        y_ref[...] = y.astype(BF)
        dout_ref[...] = dout.astype(BF)
        oa_ref[...] = oa.astype(BF)
        dpa_ref[...] = dpa.astype(BF)
        obv_ref[...] = obv.astype(BF)
        dpb_ref[...] = dpb.astype(BF)

        @pl.when(i == nt - 1)
        def _():
            acc_ref[9:10, :] = jnp.broadcast_to(jnp.sum(acc_ref[8:9, :], axis=-1, keepdims=True), (1, D))

    m = x_all.shape[0]
    lrow = lambda i: jnp.minimum(i, nt - 1)
    row = pl.BlockSpec((tm, D), lambda i: (lrow(i), 0))
    allrow = lambda cols: pl.BlockSpec((tm, cols), lambda i: (i, 0))
    zs = lambda cb: pl.BlockSpec((tm, D), lambda i: (lrow(i), cb))
    full = lambda shp: pl.BlockSpec(shp, lambda i: (0,) * len(shp))
    wfull = pl.BlockSpec((D, D), lambda i: (0, 0), pipeline_mode=pl.Buffered(1))
    f32o = jax.ShapeDtypeStruct((t_lat, D), F32)
    bfo = jax.ShapeDtypeStruct((t_lat, D), BF)
    bfall = lambda cols: jax.ShapeDtypeStruct((m, cols), BF)
    return pl.pallas_call(
        body, grid=(m // tm,),
        in_specs=[row, row, row, row, zs(4), zs(6), zs(7), zs(8), row, row, full((16, 3 * D)), full((1, DH)),
                  full((1, D)), full((1, D)), wfull, wfull, wfull],
        out_specs=(row, row) + tuple(allrow(c) for c in (CUT4, D - CUT4, CUT6, SHC)) + (row,) * 7 + (full((16, D)),),
        out_shape=(f32o, f32o, bfall(CUT4), bfall(D - CUT4), bfall(CUT6), bfall(SHC), f32o, bfo, bfo, bfo, bfo, bfo, bfo,
                   jax.ShapeDtypeStruct((16, D), F32)),
        name="merge", compiler_params=_cp(("arbitrary",), VMEM_LIMIT_MERGE))(
            o_f, o_b, h_f, h_b, z, z, z, z, x_all, tgt, mod, norm_g, ln_g, ln_b, p_a, p_b, w_out)


def _wgrad(a, b, name):
    tm = 1024

    def body(a_ref, b_ref, o_ref):
        @pl.when(pl.program_id(0) == 0)
        def _():
            o_ref[...] = jnp.zeros_like(o_ref)
        o_ref[...] += _dot_tn(a_ref[...], b_ref[...])

    row = pl.BlockSpec((tm, D), lambda i: (i, 0))
    return pl.pallas_call(body, grid=(a.shape[0] // tm,), in_specs=[row, row],
                          out_specs=pl.BlockSpec((D, D), lambda i: (0, 0)),
                          out_shape=jax.ShapeDtypeStruct((D, D), F32), name=name,
                          compiler_params=_cp(("arbitrary",)))(a, b)


def _pack_shard2(dz4b, dz5_lat, dz5_ctx, dz6a):
    m, t_lat, t_ctx = dz4b.shape[0], dz5_lat.shape[0], dz5_ctx.shape[0]
    tm = t_ctx
    nlt = t_lat // tm
    w4 = D - CUT4

    def body(a_ref, bl_ref, bc_ref, c_ref, o_ref):
        i = pl.program_id(0)
        o_ref[:, 0:w4] = a_ref[...]
        o_ref[:, w4:w4 + D] = jnp.where(i < nlt, bl_ref[...], bc_ref[...]).astype(BF)
        o_ref[:, w4 + D:] = c_ref[...]

    return pl.pallas_call(
        body, grid=(m // tm,),
        in_specs=[pl.BlockSpec((tm, w4), lambda i: (i, 0)),
                  pl.BlockSpec((tm, D), lambda i: (jnp.minimum(i, nlt - 1), 0)),
                  pl.BlockSpec((tm, D), lambda i: (0, 0)),
                  pl.BlockSpec((tm, CUT6), lambda i: (i, 0))],
        out_specs=pl.BlockSpec((tm, SHC), lambda i: (i, 0)),
        out_shape=jax.ShapeDtypeStruct((m, SHC), BF), name="pack_shard2",
        compiler_params=_cp(("arbitrary",)))(dz4b, dz5_lat, dz5_ctx, dz6a)


def _wgrad_in(u_all, dz_shards):
    m = u_all.shape[0]
    assert m % 256 == 0
    tm = m // 16

    def body(u_ref, *refs):
        o_ref = refs[NSH]
        n = pl.program_id(0)

        @pl.when(pl.program_id(1) == 0)
        def _():
            o_ref[...] = jnp.zeros_like(o_ref)

        for k in range(NSH):
            @pl.when(n == k)
            def _():
                o_ref[0] += _dot_tn(u_ref[...], refs[k][...])

    shard = lambda k: pl.BlockSpec((tm, SHC), lambda n, i: (jnp.where(n == k, i, 0), 0))
    return pl.pallas_call(
        body, grid=(NSH, m // tm),
        in_specs=[pl.BlockSpec((tm, D), lambda n, i: (i, 0))] + [shard(k) for k in range(NSH)],
        out_specs=pl.BlockSpec((1, D, SHC), lambda n, i: (n, 0, 0)),
        out_shape=jax.ShapeDtypeStruct((NSH, D, SHC), F32), name="wgrad_in",
        compiler_params=_cp(("arbitrary", "arbitrary")))(u_all, *dz_shards)


def _du(dz_shards, w_in_g, x_all, mod, gxres, n_lat_tiles, sums=()):
    m = x_all.shape[0]
    tm = 256
    nt = m // tm
    nct = nt - n_lat_tiles
    ns = len(sums)
    rblk = lambda i: jnp.where(i < nct, n_lat_tiles + i, i - nct)
    lblk = lambda i: jnp.maximum(i - nct, 0)

    def body(*refs):
        dz_refs, refs = refs[:NSH], refs[NSH - 1:]
        _, w_ref, x_ref, mod_ref, gr_ref = refs[:5]
        sum_refs = refs[5:5 + ns]
        gx_ref, dm_ref = refs[5 + ns:7 + ns]
        got_refs = refs[7 + ns:7 + 2 * ns]
        sems = refs[7 + 2 * ns:]
        i = pl.program_id(0)
        is_lat = i >= nct

        @pl.when(i == 0)
        def _():
            dm_ref[...] = jnp.zeros_like(dm_ref)
            if ns:
                for cp in _rs_chip_copies(sum_refs, got_refs, *sems):
                    cp.start()

        du = _dot_nt(dz_refs[0][...], w_ref[0])
        for n in range(1, NSH):
            du = du + _dot_nt(dz_refs[n][...], w_ref[n])
        sc = jnp.where(is_lat, mod_ref[0:1, D:2 * D], mod_ref[1:2, D:2 * D])
        dsh = _colsum(du)
        dsc = _colsum(du * x_ref[...])

        @pl.when(is_lat)
        def _():
            gx_ref[...] = du * (1.0 + sc) + gr_ref[...]
            dm_ref[0:1, 0:D] += dsh
            dm_ref[0:1, D:2 * D] += dsc

        @pl.when(jnp.logical_not(is_lat))
        def _():
            dm_ref[1:2, 0:D] += dsh
            dm_ref[1:2, D:2 * D] += dsc

        if ns:
            @pl.when(i == nt - 1)
            def _():
                for cp in _rs_chip_copies(sum_refs, got_refs, *sems):
                    cp.wait()

    outs = pl.pallas_call(
        body, grid=(nt,),
        in_specs=[pl.BlockSpec((tm, SHC), lambda i: (rblk(i), 0))] * NSH + [
                  pl.BlockSpec((NSH, D, SHC), lambda i: (0, 0, 0), pipeline_mode=pl.Buffered(1)),
                  pl.BlockSpec((tm, D), lambda i: (rblk(i), 0)),
                  pl.BlockSpec((16, 3 * D), lambda i: (0, 0)),
                  pl.BlockSpec((tm, D), lambda i: (lblk(i), 0))] + [_ANY] * ns,
        out_specs=[pl.BlockSpec((tm, D), lambda i: (lblk(i), 0)),
                   pl.BlockSpec((8, 2 * D), lambda i: (0, 0))] + [_ANY] * ns,
        out_shape=[jax.ShapeDtypeStruct((n_lat_tiles * tm, D), F32), jax.ShapeDtypeStruct((8, 2 * D), F32)]
        + [jax.ShapeDtypeStruct((3,) + g.shape[1:], g.dtype) for g in sums],
        scratch_shapes=[pltpu.SemaphoreType.DMA((3 * ns,)), pltpu.SemaphoreType.DMA((3 * ns,))] if ns else [],
        name="du", compiler_params=_cp(("arbitrary",)))(*dz_shards, w_in_g, x_all, mod, gxres, *sums)
    return outs[0], outs[1], list(outs[2:])


def _row_tile(rows, cols):
    t = 8
    while t * 2 * cols * 4 <= (1 << 20) and rows % (t * 2) == 0:
        t *= 2
    return t


def _adamw_update(w_ref, g_ref, m_ref, v_ref, d_ref, nm_ref, nv_ref):
    gg = g_ref[...]
    m2 = ADAM_B1 * m_ref[...] + (1.0 - ADAM_B1) * gg
    v2 = ADAM_B2 * v_ref[...] + (1.0 - ADAM_B2) * (gg * gg)
    m_hat = m2 / (1.0 - ADAM_B1 ** ADAM_STEP)
    v_hat = v2 / (1.0 - ADAM_B2 ** ADAM_STEP)
    d_ref[...] = -ADAM_LR * (m_hat / (jnp.sqrt(v_hat) + ADAM_EPS) + ADAM_WD * w_ref[...])
    nm_ref[...] = m2
    nv_ref[...] = v2


def _adamw_many(ws, gs, ms, vs):
    n = len(ws)

    def body(*refs):
        for j in range(n):
            _adamw_update(*refs[4 * j:4 * j + 4], *refs[4 * n + 3 * j:4 * n + 3 * j + 3])

    args = [a for quad in zip(ws, gs, ms, vs) for a in quad]
    outs = pl.pallas_call(body, out_shape=[jax.ShapeDtypeStruct(w.shape, F32) for w in ws for _ in range(3)],
                          name="adamw_small", compiler_params=_cp())(*args)
    return outs[0::3], outs[1::3], outs[2::3]


def _adamw(w, g, m, v, name):
    rows, cols = w.shape
    tr = _row_tile(rows, cols)

    def body(*refs):
        _adamw_update(*refs)

    spec = pl.BlockSpec((tr, cols), lambda i: (i, 0))
    o = jax.ShapeDtypeStruct((rows, cols), F32)
    return pl.pallas_call(body, grid=(rows // tr,), in_specs=[spec] * 4, out_specs=(spec,) * 3,
                          out_shape=(o, o, o), name=name, compiler_params=_cp(("arbitrary",)))(w, g, m, v)


_ANY = pl.BlockSpec(memory_space=pl.ANY)


def _place():
    return lax.axis_index("x"), lax.axis_index("y"), lax.axis_index("c")


def _ag_copies(ins, outs, send, recv, fsend, frecv, lsem):
    x, y, c = _place()
    me = 2 * x + y
    chips = ((1 - x, y), (x, 1 - y), (1 - x, 1 - y))
    local, chip, hand = [], [], []
    for j in range(len(ins)):
        hr = ins[j].shape[0] // 2
        half = pl.ds(pl.multiple_of(c * hr, 8), hr)
        local.append(pltpu.make_async_copy(ins[j], outs[j].at[me], lsem.at[j]))
        for k, (px, py) in enumerate(chips):
            chip.append(pltpu.make_async_remote_copy(
                src_ref=ins[j].at[half, :], dst_ref=outs[j].at[me, half, :], send_sem=send.at[3 * j + k],
                recv_sem=recv.at[3 * j + k], device_id=(px, py, c), device_id_type=MESH))
            got = outs[j].at[2 * px + py, half, :]
            hand.append(pltpu.make_async_remote_copy(
                src_ref=got, dst_ref=got, send_sem=fsend.at[3 * j + k], recv_sem=frecv.at[3 * j + k],
                device_id=(x, y, 1 - c), device_id_type=MESH))
    return local, chip, hand


def _ag_sems(n):
    return [pltpu.SemaphoreType.DMA((3 * n,))] * 4 + [pltpu.SemaphoreType.DMA((n,))]


def _ag_finish(local, chip, hand, done=0):
    for k in range(done, len(chip)):
        chip[k].wait_recv()
        hand[k].start()
    for cp in chip:
        cp.wait_send()
    for k in range(done):
        hand[k].wait_send()
    for k in range(done, len(chip)):
        hand[k].wait_send()
        hand[k].wait_recv()
    for cp in local:
        cp.wait()


def _mod_tp(c8, c_ctx, w_mod_sh, b_mod_sh):
    mc = w_mod_sh.shape[1]

    def body(c8_ref, cctx_ref, w_ref, b_ref, mod_ref, cc_ref, cc_s, part_s, send1, recv1, send3, recv3):
        x, y, c = _place()
        me = 4 * x + 2 * y + c
        ms = 2 * x + y
        copies = []
        for k in range(1, 8):
            peer = (x ^ ((k >> 2) & 1), y ^ ((k >> 1) & 1), c ^ (k & 1))
            cp = pltpu.make_async_remote_copy(src_ref=c8_ref, dst_ref=cc_s.at[me], send_sem=send1.at[k],
                                              recv_sem=recv1.at[k], device_id=peer, device_id_type=MESH)
            cp.start()
            copies.append(cp)
        cc_s[me] = c8_ref[...]
        for cp in copies:
            cp.wait()
        cc_ref[...] = jnp.zeros_like(cc_ref)
        for j in range(8):
            cc_ref[j:j + 1, :] = cc_s[j, 0:1, :]
        cc_ref[8:9, :] = cctx_ref[...]
        v = cc_ref[...]
        part_s[ms] = _dot(v * _sig(v), w_ref[...]) + b_ref[...]
        copies = []
        for k in range(1, 4):
            peer = (x ^ ((k >> 1) & 1), y ^ (k & 1), c)
            cp = pltpu.make_async_remote_copy(src_ref=part_s.at[ms], dst_ref=part_s.at[ms], send_sem=send3.at[k],
                                              recv_sem=recv3.at[k], device_id=peer, device_id_type=MESH)
            cp.start()
            copies.append(cp)
        for cp in copies:
            cp.wait()
        for s in range(NSH):
            mod_ref[:, s * mc:(s + 1) * mc] = part_s[s]

    vm = pl.BlockSpec(memory_space=pltpu.VMEM)
    return pl.pallas_call(
        body, in_specs=[vm] * 4, out_specs=(vm, vm),
        out_shape=(jax.ShapeDtypeStruct((16, NSH * mc), F32), jax.ShapeDtypeStruct((16, D), F32)),
        scratch_shapes=[pltpu.VMEM((8, 8, D), F32), pltpu.VMEM((NSH, 16, mc), F32),
                        pltpu.SemaphoreType.DMA((8,)), pltpu.SemaphoreType.DMA((8,)),
                        pltpu.SemaphoreType.DMA((4,)), pltpu.SemaphoreType.DMA((4,))],
        name="mod_tp", compiler_params=_cp())(c8, c_ctx, w_mod_sh, b_mod_sh)


def _inproj_ag(x_all, mod, w_in_sh, b_in, narrow_sh, n_lat_tiles):
    m = x_all.shape[0]
    assert m % (11 * 16) == 0
    tm = m // 11
    nt = m // tm
    n_lat = n_lat_tiles * 256
    x_, y_ = lax.axis_index("x"), lax.axis_index("y")
    sids = jnp.stack([2 * x_ + y_, 2 * (1 - x_) + y_, 2 * x_ + 1 - y_, 2 * (1 - x_) + 1 - y_]).astype(jnp.int32)

    def body(sid_ref, x_ref, mod_ref, b_ref, wsh_ref, nsh_ref, z_ref, u_ref, wg_ref, ng_ref, w_s, u_s, *sems):
        n = pl.program_id(0)
        i = pl.program_id(1)
        rows = pl.ds(pl.multiple_of(i * tm, tm), tm)
        ag = ((wsh_ref, nsh_ref), (wg_ref, ng_ref)) + tuple(sems[:5])
        wsem = sems[5]

        def load(src):
            cp = pltpu.make_async_copy(src, w_s, wsem)
            cp.start()
            cp.wait()

        @pl.when((n == 0) & (i == 0))
        def _():
            local, chip, _ = _ag_copies(*ag)
            for cp in chip + local:
                cp.start()
            load(wsh_ref)

        for k in range(NSH - 1):
            @pl.when((n == k + 1) & (i == 0))
            def _():
                _, chip, hand = _ag_copies(*ag)
                chip[k].wait_recv()
                hand[k].start()
                hand[k].wait_recv()
                load(wg_ref.at[sid_ref[k + 1]])

        @pl.when(n == 0)
        def _():
            is_lat = (i * tm + lax.broadcasted_iota(jnp.int32, (tm, 1), 0)) < n_lat
            u, _ = _modulate(x_ref, mod_ref, is_lat)
            u_s[rows, :] = u.astype(BF)
            u_ref[...] = u.astype(BF)

        z_ref[...] = _dot(u_s[rows, :], w_s[...]) + b_ref[...]

        @pl.when((n == NSH - 1) & (i == nt - 1))
        def _():
            _ag_finish(*_ag_copies(*ag), done=NSH - 1)

    first = lambda n, i: jnp.where(n == 0, i, nt - 1)
    outs = pl.pallas_call(
        body, grid_spec=pltpu.PrefetchScalarGridSpec(
            num_scalar_prefetch=1, grid=(NSH, nt),
            in_specs=[pl.BlockSpec((tm, D), lambda n, i, sid: (first(n, i), 0)),
                      pl.BlockSpec((16, 3 * D), lambda n, i, sid: (0, 0)),
                      pl.BlockSpec((1, SHC), lambda n, i, sid: (0, sid[n])), _ANY, _ANY],
            out_specs=[pl.BlockSpec((tm, SHC), lambda n, i, sid: (i, sid[n])),
                       pl.BlockSpec((tm, D), lambda n, i, sid: (first(n, i), 0)), _ANY, _ANY],
            scratch_shapes=[pltpu.VMEM((D, SHC), BF), pltpu.VMEM((m, D), BF)] + _ag_sems(2)
            + [pltpu.SemaphoreType.DMA]),
        out_shape=[jax.ShapeDtypeStruct((m, IN_COLS), F32), jax.ShapeDtypeStruct((m, D), BF),
                   jax.ShapeDtypeStruct((NSH,) + w_in_sh.shape, BF),
                   jax.ShapeDtypeStruct((NSH,) + narrow_sh.shape, narrow_sh.dtype)],
        name="inproj_ag", compiler_params=_cp(("arbitrary", "arbitrary")))(sids, x_all, mod, b_in, w_in_sh, narrow_sh)
    return outs


def _rs_sibling(grads):
    n = len(grads)

    def body(*refs):
        ins, got = refs[:n], refs[n:2 * n]
        send, recv = refs[2 * n:]
        x, y, c = _place()
        copies = []
        for j in range(n):
            hr = ins[j].shape[1] // 2
            for s in range(NSH):
                give = ins[j].at[s, pl.ds(pl.multiple_of((1 - c) * hr, 8), hr), :]
                cp = pltpu.make_async_remote_copy(src_ref=give, dst_ref=got[j].at[s], send_sem=send.at[NSH * j + s],
                                                  recv_sem=recv.at[NSH * j + s], device_id=(x, y, 1 - c),
                                                  device_id_type=MESH)
                cp.start()
                copies.append(cp)
        for cp in copies:
            cp.wait()

    half = [jax.ShapeDtypeStruct((NSH, g.shape[1] // 2, g.shape[2]), F32) for g in grads]
    return pl.pallas_call(
        body, in_specs=[_ANY] * n, out_specs=[_ANY] * n, out_shape=half,
        scratch_shapes=[pltpu.SemaphoreType.DMA((NSH * n,)), pltpu.SemaphoreType.DMA((NSH * n,))],
        name="rs_sibling")(*grads)


def _core_vec():
    return lax.axis_index("c").astype(jnp.int32).reshape(1)


def _rs_add1(g, got, name):
    _, r, cols = g.shape
    hr = r // 2
    tr = _row_tile(hr, cols)
    nb = hr // tr

    def body(c_ref, g_ref, got_ref, o_ref):
        o_ref[...] = (g_ref[...] + got_ref[...]).astype(BF)

    spec = pl.BlockSpec((1, tr, cols), lambda s, i, c_ref: (s, i, 0))
    return pl.pallas_call(
        body, grid_spec=pltpu.PrefetchScalarGridSpec(
            num_scalar_prefetch=1, grid=(NSH, nb),
            in_specs=[pl.BlockSpec((1, tr, cols), lambda s, i, c_ref: (s, c_ref[0] * nb + i, 0)), spec],
            out_specs=spec),
        out_shape=jax.ShapeDtypeStruct((NSH, hr, cols), BF), name=name,
        compiler_params=_cp(("arbitrary", "arbitrary")))(_core_vec(), g, got)


def _rs_add2(sums, got, name):
    _, hr, cols = sums.shape
    tr = _row_tile(hr, cols)
    nb = hr // tr
    place = jnp.stack([2 * lax.axis_index("x") + lax.axis_index("y"), lax.axis_index("c")]).astype(jnp.int32)

    def body(p_ref, s_ref, got_ref, o_ref):
        f = lambda v: v.astype(F32)
        o_ref[...] = f(s_ref[0]) + f(got_ref[0]) + f(got_ref[1]) + f(got_ref[2])

    return pl.pallas_call(
        body, grid_spec=pltpu.PrefetchScalarGridSpec(
            num_scalar_prefetch=1, grid=(nb,),
            in_specs=[pl.BlockSpec((1, tr, cols), lambda i, p_ref: (p_ref[0], i, 0)),
                      pl.BlockSpec((3, tr, cols), lambda i, p_ref: (0, i, 0))],
            out_specs=pl.BlockSpec((tr, cols), lambda i, p_ref: (p_ref[1] * nb + i, 0))),
        out_shape=jax.ShapeDtypeStruct((2 * hr, cols), F32), name=name,
        compiler_params=_cp(("arbitrary",)))(place, sums, got)


def _rs_chip_copies(ins, got, send, recv):
    x, y, c = _place()
    peers = ((1 - x, y), (x, 1 - y), (1 - x, 1 - y))
    return [pltpu.make_async_remote_copy(src_ref=ins[j].at[2 * px + py], dst_ref=got[j].at[k],
                                         send_sem=send.at[3 * j + k], recv_sem=recv.at[3 * j + k],
                                         device_id=(px, py, c), device_id_type=MESH)
            for j in range(len(ins)) for k, (px, py) in enumerate(peers)]


def _ag_sibling(fulls):
    n = len(fulls)
    nck = 4

    def body(*refs):
        outs = refs[n:2 * n]
        send, recv = refs[2 * n:]
        x, y, c = _place()
        copies = []
        for j in range(n):
            qr = outs[j].shape[0] // (2 * nck)
            for k in range(nck):
                rows = outs[j].at[pl.ds(pl.multiple_of((c * nck + k) * qr, 8), qr), :]
                cp = pltpu.make_async_remote_copy(src_ref=rows, dst_ref=rows, send_sem=send.at[nck * j + k],
                                                  recv_sem=recv.at[nck * j + k], device_id=(x, y, 1 - c),
                                                  device_id_type=MESH)
                cp.start()
                copies.append(cp)
        for cp in copies:
            cp.wait()

    return pl.pallas_call(
        body, in_specs=[_ANY] * n, out_specs=[_ANY] * n,
        out_shape=[jax.ShapeDtypeStruct(f.shape, F32) for f in fulls],
        input_output_aliases={j: j for j in range(n)},
        scratch_shapes=[pltpu.SemaphoreType.DMA((nck * n,)), pltpu.SemaphoreType.DMA((nck * n,))],
        name="ag_sibling")(*fulls)


def _allreduce_small(buf):
    rows = buf.shape[0]
    pr = rows // 8

    def body(in_ref, out_ref, stage, send1, recv1, send2, recv2):
        x, y, c = _place()
        me = 4 * x + 2 * y + c

        def peer(k):
            kx, ky, kc = (k >> 2) & 1, (k >> 1) & 1, k & 1
            return (x ^ kx, y ^ ky, c ^ kc)

        def piece(ref, idx):
            return ref.at[pl.ds(pl.multiple_of(idx * pr, 8), pr), :]

        copies = []
        for k in range(1, 8):
            px, py, pc = peer(k)
            cp = pltpu.make_async_remote_copy(src_ref=piece(in_ref, 4 * px + 2 * py + pc), dst_ref=stage.at[k],
                                              send_sem=send1.at[k], recv_sem=recv1.at[k],
                                              device_id=(px, py, pc), device_id_type=MESH)
            cp.start()
            copies.append(cp)
        for cp in copies:
            cp.wait()
        acc = piece(in_ref, me)[...]
        for k in range(1, 8):
            acc = acc + stage[k]
        piece(out_ref, me)[...] = acc
        copies = []
        for k in range(1, 8):
            cp = pltpu.make_async_remote_copy(src_ref=piece(out_ref, me), dst_ref=piece(out_ref, me),
                                              send_sem=send2.at[k], recv_sem=recv2.at[k],
                                              device_id=peer(k), device_id_type=MESH)
            cp.start()
            copies.append(cp)
        for cp in copies:
            cp.wait()

    vm = pl.BlockSpec(memory_space=pltpu.VMEM)
    return pl.pallas_call(
        body, in_specs=[vm], out_specs=vm, out_shape=jax.ShapeDtypeStruct((rows, D), F32),
        scratch_shapes=[pltpu.VMEM((8, pr, D), F32)] + [pltpu.SemaphoreType.DMA((8,))] * 4,
        name="allreduce_small", compiler_params=_cp())(buf)


def _rows(a):
    flat = a.reshape(-1)
    pad = (-flat.shape[0]) % D
    if pad:
        flat = jnp.concatenate([flat, jnp.zeros((pad,), flat.dtype)])
    return flat.reshape(-1, D)


def _pad_rows(a, mult):
    pad = (-a.shape[0]) % mult
    return jnp.concatenate([a, jnp.zeros((pad, a.shape[1]), a.dtype)]) if pad else a


def _local_step(x, c, ctx, c_ctx, tgt, me, shard, sh, b_mod, b_in, norm_g, cb, wr, wi, ln_g, ln_b):
    t_lat, t_ctx = x.shape[0], ctx.shape[0]
    nlt = t_lat // 256
    nlb, ncb = t_lat // RB, t_ctx // RB
    mc = 3 * D // NSH
    mod_all, cc_all = _mod_tp(jnp.zeros((8, D), F32).at[0].set(c), c_ctx.reshape(1, D), sh["w_mod"],
                              lax.dynamic_slice_in_dim(b_mod, shard * mc, mc, axis=1))
    mod = jnp.zeros((16, 3 * D), F32).at[0].set(mod_all[me]).at[1].set(mod_all[8])
    cc = jnp.zeros((16, D), F32).at[0].set(c).at[1].set(c_ctx)
    x_all = jnp.concatenate([x, ctx], axis=0)
    z, u_all, w_in_g, nar = _inproj_ag(x_all, mod, sh["w_in"], b_in, sh["narrow"], nlt)
    nar = jnp.transpose(nar, (1, 0, 2)).reshape(-1, D)
    lbl, cw, br, bi, lam = nar[0:4].reshape(2, 2, D), nar[4:8], nar[8:10], nar[10:12], nar[12:14]
    o0, st0, (w_mod_g, p_a, p_b, w_out) = _gla_fwd(z, lbl, 0, nlb, ncb,
                                                   gather=[sh[k] for k in ("w_mod", "p_a", "p_b", "w_out")])
    p_a, p_b, w_out = p_a.reshape(D, D), p_b.reshape(D, D), w_out.reshape(D, D)
    o1, st1, _ = _gla_fwd(z, lbl, 1, nlb, ncb)
    h0, hp0, a0, hpc0 = _rglru_fwd(z, cw, cb, wr, br, wi, bi, lam, 0, t_lat, t_ctx)
    h1, hp1, a1, hpc1 = _rglru_fwd(z, cw, cb, wr, br, wi, bi, lam, 1, t_lat, t_ctx)
    (do, dh, dz4a, dz4b, dz6a, dz_sh3, gxres, y, dout, oa, dpa, obv, dpb, acc) = _merge(
        o0, o1, h0, h1, z, x_all, tgt, mod, norm_g, ln_g, ln_b, p_a, p_b, w_out, t_lat)
    gp_a = _wgrad(oa, dpa, "wgrad_pa")
    gp_b = _wgrad(obv, dpb, "wgrad_pb")
    gw_out = _wgrad(y, dout, "wgrad_wout")
    dxc_lat, dxc_ctx, dwr0, dwi0, sb0 = _rglru_bwd(z, cw, cb, wr, br, wi, bi, lam, dh, hp0, a0, hpc0, 0, t_lat, t_ctx)
    dz5_lat, dz5_ctx, dwr1, dwi1, sb1 = _rglru_bwd(z, cw, cb, wr, br, wi, bi, lam, dh, hp1, a1, hpc1, 1, t_lat, t_ctx,
                                                   prev=(dxc_lat, dxc_ctx))
    dq0, dv0, dz1, sa0 = _gla_bwd(z, lbl, do, st0, 0, nlb, ncb)
    dz_sh0, dz_sh1, sa1 = _gla_bwd(z, lbl, do, st1, 1, nlb, ncb, prev=(dq0, dv0, dz1, dz4a))
    dz = (dz_sh0, dz_sh1, _pack_shard2(dz4b, dz5_lat, dz5_ctx, dz6a), dz_sh3)
    big = dict(w_in=_wgrad_in(u_all, dz), p_a=gp_a.reshape(NSH, D // NSH, D),
               p_b=gp_b.reshape(NSH, D // NSH, D), w_out=gw_out.reshape(NSH, D // NSH, D))
    grads = [big[k] for k in _RS]
    sums = [_rs_add1(g, b, f"rs_add1_{j}") for j, (g, b) in enumerate(zip(grads, _rs_sibling(grads)))]
    gx, dm, got = _du(dz, w_in_g, x_all, mod, gxres, nlt, sums)
    fulls = [_rs_add2(a, b, f"rs_add2_{j}") for j, (a, b) in enumerate(zip(sums, got))]
    big = dict(zip(_RS, _ag_sibling(fulls)))
    dmod = jnp.zeros((16, 3 * D), F32).at[0:2, 0:2 * D].set(dm[0:2]).at[0, 2 * D:].set(acc[0])
    dcc = _mod_bwd(cc, dmod, w_mod_g)
    small = dict(
        c_ctx=dcc[1:2], b_mod=(dmod[0] + dmod[1]).reshape(3, D),
        b_in=jnp.stack([sa1[2], sa0[0], sa1[0], sa1[3], acc[3], sb1[4], acc[4], acc[5], acc[6]]),
        lb_logits=jnp.stack([sa0[1], sa1[1], -sa0[1], -sa1[1]]),
        norm_a_g=acc[7:8], conv_w=sb1[8:12], conv_b=sb1[3:4],
        w_r=jnp.stack([dwr0, dwr1]).reshape(-1, D), w_i=jnp.stack([dwi0, dwi1]).reshape(-1, D),
        b_r=jnp.stack([sb0[0], sb1[0]]), b_i=jnp.stack([sb0[1], sb1[1]]), lam=jnp.stack([sb0[2], sb1[2]]),
        ln_g=acc[1:2], ln_b=acc[2:3])
    return acc[9, 0], gx, big, small, dmod, cc_all


_RS =("w_in", "p_a", "p_b", "w_out")
_SMALL =("c_ctx", "b_mod", "b_in", "lb_logits", "norm_a_g", "conv_w", "conv_b", "w_r", "w_i", "b_r", "b_i", "lam",
          "ln_g", "ln_b")
_BIG = ("w_mod", "w_in", "p_a", "p_b", "w_out")
_COL_SHARDED = ("lb_logits", "conv_w", "b_r", "b_i", "lam")
_WEIGHTS = ("c_ctx", "w_mod", "b_mod", "w_in", "b_in", "lb_logits", "norm_a_g", "conv_w", "conv_b", "w_r", "b_r", "w_i",
            "b_i", "lam", "p_a", "p_b", "w_out", "ln_g", "ln_b")


def kernel(x, c, ctx, c_ctx, w_mod, b_mod, w_in, b_in, lb_logits, norm_a_g, conv_w, conv_b, w_r, b_r, w_i, b_i, lam, p_a, p_b, w_out, ln_g, ln_b, loss_target, m_c_ctx, m_w_mod, m_b_mod, m_w_in, m_b_in, m_lb_logits, m_norm_a_g, m_conv_w, m_conv_b, m_w_r, m_b_r, m_w_i, m_b_i, m_lam, m_p_a, m_p_b, m_w_out, m_ln_g, m_ln_b, v_c_ctx, v_w_mod, v_b_mod, v_w_in, v_b_in, v_lb_logits, v_norm_a_g, v_conv_w, v_conv_b, v_w_r, v_b_r, v_w_i, v_b_i, v_lam, v_p_a, v_p_b, v_w_out, v_ln_g, v_ln_b):
    w = dict(c_ctx=c_ctx, w_mod=w_mod, b_mod=b_mod, w_in=w_in, b_in=b_in, lb_logits=lb_logits, norm_a_g=norm_a_g,
             conv_w=conv_w, conv_b=conv_b, w_r=w_r, b_r=b_r, w_i=w_i, b_i=b_i, lam=lam, p_a=p_a, p_b=p_b, w_out=w_out,
             ln_g=ln_g, ln_b=ln_b)
    m = dict(c_ctx=m_c_ctx, w_mod=m_w_mod, b_mod=m_b_mod, w_in=m_w_in, b_in=m_b_in, lb_logits=m_lb_logits,
             norm_a_g=m_norm_a_g, conv_w=m_conv_w, conv_b=m_conv_b, w_r=m_w_r, b_r=m_b_r, w_i=m_w_i, b_i=m_b_i,
             lam=m_lam, p_a=m_p_a, p_b=m_p_b, w_out=m_w_out, ln_g=m_ln_g, ln_b=m_ln_b)
    v = dict(c_ctx=v_c_ctx, w_mod=v_w_mod, b_mod=v_b_mod, w_in=v_w_in, b_in=v_b_in, lb_logits=v_lb_logits,
             norm_a_g=v_norm_a_g, conv_w=v_conv_w, conv_b=v_conv_b, w_r=v_w_r, b_r=v_b_r, w_i=v_w_i, b_i=v_b_i,
             lam=v_lam, p_a=v_p_a, p_b=v_p_b, w_out=v_w_out, ln_g=v_ln_g, ln_b=v_ln_b)
    shard = 2 * lax.axis_index("x") + lax.axis_index("y")
    cs = D // NSH

    sh = {k: w[k][0].astype(BF) for k in _BIG}
    sh["narrow"] = _pad_rows(jnp.concatenate([lb_logits.reshape(4, cs), conv_w[0], b_r[0], b_i[0], lam[0]], axis=0), 8)
    me = 2 * shard + lax.axis_index("c")
    loss, gx, big, small, dmod, cc_all = _local_step(
        x[0], c[0], ctx[0], c_ctx, loss_target[0], me, shard, sh, b_mod, b_in, norm_a_g, conv_b, w_r[0], w_i[0],
        ln_g, ln_b)
    loss = lax.psum(loss, ("x", "y", "c"))

    dmod_rows = jnp.zeros((16, 3 * D), F32).at[me].set(dmod[0]).at[8].set(dmod[1]).reshape(48, D)
    sizes = [small[k].shape[0] for k in _SMALL]
    red = _allreduce_small(_pad_rows(jnp.concatenate([_pad_rows(small[k], 8) for k in _SMALL] + [dmod_rows],
                                                     axis=0), 64))
    grads = {}
    off = 0
    for k, n in zip(_SMALL, sizes):
        g = red[off:off + n]
        off += n + (-n) % 8
        if k == "norm_a_g":
            g = g[:, :DH]
        if k in _COL_SHARDED:
            g = lax.dynamic_slice_in_dim(g, shard * cs, cs, axis=1)
        grads[k] = g.reshape(w[k].shape)
    for k in _RS:
        grads[k] = big[k].reshape(w[k].shape)
    dmod_all = red[off:off + 48].reshape(16, 3 * D)
    mc = 3 * D // NSH
    grads["w_mod"] = _wmod_grad(cc_all, lax.dynamic_slice_in_dim(dmod_all, shard * mc, mc, axis=1)).reshape(
        w["w_mod"].shape)

    delta, new_m, new_v = {}, {}, {}
    for k in _BIG:
        shp = w[k].shape
        two = lambda a: a.reshape(shp[-2], shp[-1])
        d_, m_, v_ = _adamw(two(w[k]), two(grads[k]), two(m[k]), two(v[k]), f"adamw_{k}")
        delta[k], new_m[k], new_v[k] = d_.reshape(shp), m_.reshape(shp), v_.reshape(shp)
    d_, m_, v_ = _adamw_many(*[[t[k] for k in _SMALL] for t in (w, grads, m, v)])
    delta.update(zip(_SMALL, d_))
    new_m.update(zip(_SMALL, m_))
    new_v.update(zip(_SMALL, v_))

    return (loss, gx[None], *[grads[k] for k in _WEIGHTS], *[delta[k] for k in _WEIGHTS],
            *[new_m[k] for k in _WEIGHTS], *[new_v[k] for k in _WEIGHTS])
```

```python
import functools

import jax
import jax.numpy as jnp
from jax import lax
from jax.experimental import pallas as pl
from jax.experimental.pallas import tpu as pltpu

F32 = jnp.float32
BF = jnp.bfloat16
MESH = pl.DeviceIdType.MESH

D = 1024
H = 8
DH = 128
CH = 64
RB = 256
NCK = RB // CH
GRID_W = 64
CB = 8
RCH = 16
IN_COLS = 9 * D
NSH = 4
SHC = IN_COLS // NSH
CUT2 = SHC - 2 * D
CUT4 = 2 * SHC - 4 * D
CUT6 = 3 * SHC - 6 * D
RG_C = 8.0
ALPHA = 2.0 ** 0.25
LN_EPS = 1e-5
RMS_EPS = 1e-6
Q_SCALE = DH ** -0.5
ADAM_LR, ADAM_B1, ADAM_B2, ADAM_EPS, ADAM_WD, ADAM_STEP = 0.001, 0.9, 0.999, 1e-08, 0.01, 10
VMEM_LIMIT = 56 * 1024 * 1024
VMEM_LIMIT_MERGE = 60 * 1024 * 1024


def _cp(sem=None, vmem=VMEM_LIMIT):
    return pltpu.CompilerParams(dimension_semantics=sem, vmem_limit_bytes=vmem)


def _sig(x):
    return 0.5 * jnp.tanh(0.5 * x) + 0.5


def _dot(a, b):
    return jnp.dot(a.astype(BF), b.astype(BF), preferred_element_type=F32)


def _dot_nt(a, b):
    return lax.dot_general(a.astype(BF), b.astype(BF), (((1,), (1,)), ((), ())), preferred_element_type=F32)


def _dot_tn(a, b):
    return lax.dot_general(a.astype(BF), b.astype(BF), (((0,), (0,)), ((), ())), preferred_element_type=F32)


def _colsum(v):
    return jnp.sum(v, axis=0, keepdims=True)


def _mod_bwd(cc, dmod, w_mod_g):
    def body(cc_ref, dm_ref, w_ref, dcc_ref):
        v = cc_ref[...]
        sg = _sig(v)
        ds = jnp.zeros((16, D), F32)
        for k in range(NSH):
            ds = ds + _dot_nt(dm_ref[:, k * 768:(k + 1) * 768], w_ref[k])
        dcc_ref[...] = ds * (sg * (1.0 + v * (1.0 - sg)))
    return pl.pallas_call(body, out_shape=jax.ShapeDtypeStruct((16, D), F32),
                          name="mod_bwd", compiler_params=_cp())(cc, dmod, w_mod_g)


def _wmod_grad(cc, dmod_cols):
    def body(cc_ref, dm_ref, dw_ref):
        v = cc_ref[...]
        dw_ref[...] = _dot_tn(v * _sig(v), dm_ref[...])
    return pl.pallas_call(body, out_shape=jax.ShapeDtypeStruct((D, dmod_cols.shape[1]), F32),
                          name="wmod_grad", compiler_params=_cp())(cc, dmod_cols)


def _modulate(x_ref, mod_ref, is_lat):
    sh = jnp.where(is_lat, mod_ref[0:1, 0:D], mod_ref[1:2, 0:D])
    sc = jnp.where(is_lat, mod_ref[0:1, D:2 * D], mod_ref[1:2, D:2 * D])
    return x_ref[...] * (1.0 + sc) + sh, sc


def _gla_mask(d, n):
    row = lax.broadcasted_iota(jnp.int32, (n, n), 0)
    col = lax.broadcasted_iota(jnp.int32, (n, n), 1)
    same = (row // CH) == (col // CH)
    return same & ((row <= col) if d else (row >= col))


def _chunk_cumsum(v, rev):
    n = v.shape[0]
    pos = lax.broadcasted_iota(jnp.int32, v.shape, 0) & (CH - 1)
    s = 1
    while s < CH:
        if rev:
            v = v + jnp.where(pos < CH - s, pltpu.roll(v, n - s, 0), 0.0)
        else:
            v = v + jnp.where(pos >= s, pltpu.roll(v, s, 0), 0.0)
        s *= 2
    return v


def _chunk_rows(c):
    return slice(c * CH, (c + 1) * CH)


def _gla_features(zq, zf, lb, d):
    sq = _sig(zq)
    q = zq * sq * Q_SCALE
    sf = _sig(zf)
    f = lb + (1.0 - lb) * sf
    k = 1.0 - f
    g = _chunk_cumsum(jnp.log(f), d)
    last = 0 if d else CH - 1
    gls = [g[c * CH + last:c * CH + last + 1, :] for c in range(NCK)]
    glb = jnp.concatenate([jnp.broadcast_to(gl, (CH, D)) for gl in gls], axis=0)
    eg, eig, eeg = jnp.exp(g), jnp.exp(-g), jnp.exp(glb - g)
    decs = [jnp.exp(gl) for gl in gls]
    return sq, sf, f, q * eg, k * eig, k * eeg, eg, eig, eeg, decs


def _lower_bound(lbl_ref, d):
    return _sig(lbl_ref[0, d:d + 1, :] - lbl_ref[1, d:d + 1, :])


def _gla_rb(d, nlb, ncb):
    nrb = nlb + ncb
    if d:
        return lambda s: nrb - 1 - s
    return lambda s: jnp.where(s < ncb, nlb + s, s - ncb)


def _gla_fwd(z, lbl, d, nlb, ncb, gather=()):
    m = z.shape[0]
    nrb = nlb + ncb
    rb = _gla_rb(d, nlb, ncb)
    ng = len(gather)

    def body(*refs):
        q_ref, f_ref, v_ref, lbl_ref = refs[:4]
        ag_in = refs[4:4 + ng]
        o_ref, st_ref = refs[4 + ng:6 + ng]
        ag_out = refs[6 + ng:6 + 2 * ng]
        S = refs[6 + 2 * ng]
        ag_sems = refs[7 + 2 * ng:]
        s = pl.program_id(0)

        @pl.when(s == 0)
        def _():
            S[...] = jnp.zeros_like(S)
            if ng:
                local, chip, _ = _ag_copies(ag_in, ag_out, *ag_sems)
                for cp in chip + local:
                    cp.start()

        lb = _lower_bound(lbl_ref, d)
        mb = _gla_mask(d, CH)
        _, _, _, qd, ki, ke, _, _, _, decs = _gla_features(q_ref[...], f_ref[...], lb, d)
        qd, ki, ke, v = qd.astype(BF), ki.astype(BF), ke.astype(BF), v_ref[...].astype(BF)
        order = range(NCK - 1, -1, -1) if d else range(NCK)
        for h in range(H):
            hs = slice(h * DH, (h + 1) * DH)
            intra, upd = {}, {}
            for c in range(NCK):
                rs = _chunk_rows(c)
                a = jnp.where(mb, _dot_nt(qd[rs, hs], ki[rs, hs]), 0.0)
                intra[c] = _dot(a, v[rs, hs])
                upd[c] = _dot_tn(v[rs, hs], ke[rs, hs])
            st = S[h]
            for c in order:
                rs = _chunk_rows(c)
                st_ref[c, h] = st
                o_ref[rs, hs] = intra[c] + _dot_nt(qd[rs, hs], st)
                st = st * decs[c][:, hs] + upd[c]
            S[h] = st

        if ng:
            @pl.when(s == nrb - 1)
            def _():
                _ag_finish(*_ag_copies(ag_in, ag_out, *ag_sems))

    def zspec(cb):
        return pl.BlockSpec((RB, D), lambda s: (rb(s), cb))

    outs = pl.pallas_call(
        body, grid=(nrb,),
        in_specs=[zspec(0), zspec(1 + d), zspec(3), pl.BlockSpec((2, 2, D), lambda s: (0, 0, 0))] + [_ANY] * ng,
        out_specs=[pl.BlockSpec((RB, D), lambda s: (rb(s), 0)),
                   pl.BlockSpec((NCK, H, DH, DH), lambda s: (rb(s), 0, 0, 0))] + [_ANY] * ng,
        out_shape=[jax.ShapeDtypeStruct((m, D), F32), jax.ShapeDtypeStruct((m // CH, H, DH, DH), F32)]
        + [jax.ShapeDtypeStruct((NSH,) + g.shape, g.dtype) for g in gather],
        scratch_shapes=[pltpu.VMEM((H, DH, DH), F32)] + (_ag_sems(ng) if ng else []),
        name=f"gla_fwd{d}", compiler_params=_cp(("arbitrary",)))(z, z, z, lbl, *gather)
    return outs[0], outs[1], list(outs[2:])


def _gla_bwd(z, lbl, do_lat, states, d, nlb, ncb, prev=None):
    m = z.shape[0]
    nrb = nlb + ncb
    fwd_rb = _gla_rb(d, nlb, ncb)
    rb = lambda s: fwd_rb(nrb - 1 - s)
    last = prev is not None

    def body(*refs):
        if last:
            (q_ref, f_ref, v_ref, lbl_ref, do_ref, st_ref, pq_ref, pv_ref, dz1_ref, dz4a_ref, sh0_ref, sh1_ref, sum_ref,
             dS) = refs
        else:
            q_ref, f_ref, v_ref, lbl_ref, do_ref, st_ref, o0_ref, o1_ref, o2_ref, sum_ref, dS = refs
        s = pl.program_id(0)
        is_lat = rb(s) < nlb

        @pl.when(s == 0)
        def _():
            dS[...] = jnp.zeros_like(dS)
            sum_ref[...] = jnp.zeros_like(sum_ref)

        lb = _lower_bound(lbl_ref, d)
        mb = _gla_mask(d, RB)
        zq = q_ref[...]
        sq, sf, f, qd, ki, ke, eg, eig, eeg, decs = _gla_features(zq, f_ref[...], lb, d)
        qdb, kib, keb, vb = qd.astype(BF), ki.astype(BF), ke.astype(BF), v_ref[...].astype(BF)
        dob = jnp.where(is_lat, do_ref[...], 0.0).astype(BF)
        order = range(NCK) if d else range(NCK - 1, -1, -1)
        dqd_h, dki_h, dke_h, dv_h, ddec_h = [], [], [], [], []
        for h in range(H):
            hs = slice(h * DH, (h + 1) * DH)
            a = jnp.where(mb, _dot_nt(qdb[:, hs], kib[:, hs]), 0.0).astype(BF)
            da = jnp.where(mb, _dot_nt(dob[:, hs], vb[:, hs]), 0.0).astype(BF)
            dqd_i = _dot(da, kib[:, hs])
            dki_h.append(_dot_tn(da, qdb[:, hs]))
            dvi = _dot_tn(a, dob[:, hs])
            dqd, inc = {}, {}
            for c in range(NCK):
                rs = _chunk_rows(c)
                dqd[c] = dqd_i[rs, :] + _dot(dob[rs, hs], st_ref[c, h])
                inc[c] = _dot_tn(dob[rs, hs], qdb[rs, hs])
            dst = dS[h]
            dke, dv, ddec = {}, {}, {}
            for c in order:
                rs = _chunk_rows(c)
                dv[c] = dvi[rs, :] + _dot_nt(keb[rs, hs], dst)
                dke[c] = _dot(vb[rs, hs], dst)
                ddec[c] = _colsum(st_ref[c, h] * dst)
                dst = inc[c] + dst * decs[c][:, hs]
            dS[h] = dst
            cat = lambda t: jnp.concatenate([t[c] for c in range(NCK)], axis=0)
            dqd_h.append(cat(dqd))
            dke_h.append(cat(dke))
            dv_h.append(cat(dv))
            ddec_h.append([ddec[c] for c in range(NCK)])
        lanes = lambda parts: jnp.concatenate(parts, axis=1)
        dqd, dki, dke, dv = lanes(dqd_h), lanes(dki_h), lanes(dke_h), lanes(dv_h)
        dq = dqd * eg
        dk = dki * eig + dke * eeg
        dke_ke = dke * ke
        dg = dqd * qd - dki * ki - dke_ke
        dgl = [_colsum(dke_ke[_chunk_rows(c), :]) + lanes([ddec_h[h][c] for h in range(H)]) * decs[c]
               for c in range(NCK)]
        dglb = jnp.concatenate([jnp.broadcast_to(t, (CH, D)) for t in dgl], axis=0)
        df = (_chunk_cumsum(dg, 1 - d) + dglb) / f - dk
        dzf = df * (1.0 - lb) * (sf * (1.0 - sf))
        sum_ref[0:1, :] += _colsum(dzf)
        sum_ref[1:2, :] += _colsum(df * (1.0 - sf))
        if last:
            dz0 = (dq + pq_ref[...]) * (Q_SCALE * (sq * (1.0 + zq * (1.0 - sq))))
            dz3 = dv + pv_ref[...]
            sum_ref[2:3, :] += _colsum(dz0)
            sum_ref[3:4, :] += _colsum(dz3)
            dz2 = dzf.astype(BF)
            sh0_ref[:, 0:D] = dz0.astype(BF)
            sh0_ref[:, D:2 * D] = dz1_ref[...]
            sh0_ref[:, 2 * D:] = dz2[:, :CUT2]
            sh1_ref[:, 0:D - CUT2] = dz2[:, CUT2:]
            sh1_ref[:, D - CUT2:2 * D - CUT2] = dz3.astype(BF)
            sh1_ref[:, 2 * D - CUT2:] = dz4a_ref[...]
        else:
            o0_ref[...] = dq
            o1_ref[...] = dv
            o2_ref[...] = dzf.astype(BF)

        @pl.when(s == nrb - 1)
        def _():
            sum_ref[1:2, :] = sum_ref[1:2, :] * (lb * (1.0 - lb))

    def zspec(cb):
        return pl.BlockSpec((RB, D), lambda s: (rb(s), cb))

    rowspec = pl.BlockSpec((RB, D), lambda s: (rb(s), 0))
    in_specs = [zspec(0), zspec(1 + d), zspec(3), pl.BlockSpec((2, 2, D), lambda s: (0, 0, 0)),
                pl.BlockSpec((RB, D), lambda s: (jnp.minimum(rb(s), nlb - 1), 0)),
                pl.BlockSpec((NCK, H, DH, DH), lambda s: (rb(s), 0, 0, 0))]
    args = [z, z, z, lbl, do_lat, states]
    sumspec = pl.BlockSpec((8, D), lambda s: (0, 0))
    if last:
        in_specs += [rowspec, rowspec, rowspec, pl.BlockSpec((RB, CUT4), lambda s: (rb(s), 0))]
        args += list(prev)
        shspec = pl.BlockSpec((RB, SHC), lambda s: (rb(s), 0))
        out_specs = (shspec, shspec, sumspec)
        out_shape = (jax.ShapeDtypeStruct((m, SHC), BF), jax.ShapeDtypeStruct((m, SHC), BF))
    else:
        out_specs = (rowspec, rowspec, rowspec, sumspec)
        out_shape = (jax.ShapeDtypeStruct((m, D), F32), jax.ShapeDtypeStruct((m, D), F32),
                     jax.ShapeDtypeStruct((m, D), BF))
    return pl.pallas_call(
        body, grid=(nrb,), in_specs=in_specs, out_specs=out_specs,
        out_shape=out_shape + (jax.ShapeDtypeStruct((8, D), F32),),
        scratch_shapes=[pltpu.VMEM((H, DH, DH), F32)],
        name=f"gla_bwd{d}", compiler_params=_cp(("arbitrary",)))(*args)


def _shift_rows(v, k, r):
    row = lax.broadcasted_iota(jnp.int32, v.shape, 0)
    rolled = pltpu.roll(v, k % r, 0)
    return jnp.where((row >= k) & (row < r + k), rolled, 0.0)


def _conv_fwd(xb, cw_ref, cb_ref, r):
    return (cb_ref[...] + _shift_rows(xb, 1, r) * cw_ref[0:1, :] + xb * cw_ref[1:2, :]
            + _shift_rows(xb, -1, r) * cw_ref[2:3, :] + _shift_rows(xb, -2, r) * cw_ref[3:4, :])


def _window(ref, lo, n, rows):
    parts = []
    if lo < 0:
        parts.append(jnp.zeros((-lo,) + tuple(ref.shape[1:]), F32))
    parts.append(ref[max(lo, 0):min(lo + n, rows)])
    if lo + n > rows:
        parts.append(jnp.zeros((lo + n - rows,) + tuple(ref.shape[1:]), F32))
    return parts[0] if len(parts) == 1 else jnp.concatenate(parts, axis=0)


def _conv_cols(x_ref, cw_ref, cb_ref, r0, n, rows):
    w = _window(x_ref, r0 - 1, n + 3, rows)
    return (cb_ref[...] + w[0:n] * cw_ref[0:1, :] + w[1:n + 1] * cw_ref[1:2, :] + w[2:n + 2] * cw_ref[2:3, :]
            + w[3:n + 3] * cw_ref[3:4, :])


def _softplus_neg(lam):
    y = jnp.exp(-jnp.abs(lam))
    u = 1.0 + y
    tiny = u == 1.0
    l1p = jnp.where(tiny, y, jnp.log(u) * (y / jnp.where(tiny, 1.0, u - 1.0)))
    return jnp.maximum(-lam, 0.0) + l1p


def _gates(xc, wr_ref, br, wi_ref, bi, sp):
    xcb = xc.astype(BF)
    rs, is_ = [], []
    for g in range(H):
        gs = slice(g * DH, (g + 1) * DH)
        rs.append(jnp.dot(xcb[:, gs], wr_ref[g].astype(BF), preferred_element_type=F32))
        is_.append(jnp.dot(xcb[:, gs], wi_ref[g].astype(BF), preferred_element_type=F32))
    r = _sig(jnp.concatenate(rs, axis=1) + br)
    i = _sig(jnp.concatenate(is_, axis=1) + bi)
    log_a = (-RG_C * r) * sp
    a = jnp.exp(log_a)
    t = jnp.tanh(log_a)
    om = (-2.0 * t) / (1.0 - t)
    return r, i, a, om


def _scan_rows(d, nrows, a_s, b_s, h_s, h0):
    nsl = nrows // 8

    def slab(j, h):
        jj = (nsl - 1 - j) if d else j
        r0 = pl.multiple_of(jj * 8, 8)
        for t in (range(7, -1, -1) if d else range(8)):
            h = a_s[pl.ds(r0 + t, 1), :] * h + b_s[pl.ds(r0 + t, 1), :]
            h_s[pl.ds(r0 + t, 1), :] = h
        return h

    return lax.fori_loop(0, nsl, slab, h0)


def _col_of(d, ncols):
    if d:
        return lambda s: ncols - jnp.maximum(s, 1)
    return lambda s: jnp.maximum(s, 1) - 1


def _rglru_fwd(z, cw, cb, wr, br, wi, bi, lam, d, t_lat, t_ctx):
    m = z.shape[0]
    rows = t_lat // GRID_W
    z3 = z.reshape(m // GRID_W, GRID_W, IN_COLS)
    nblk = GRID_W // CB
    blk = _col_of(d, nblk)
    cblk = t_lat // t_ctx
    rc = min(RCH, rows)

    def body(zc_ref, zx_ref, cw_ref, cb_ref, wr_ref, br_ref, wi_ref, bi_ref, lam_ref,
             hx_ref, hpx_ref, ax_ref, hpc_ref, a_s, b_s, h_s, hcar, a3, b3, cin_s):
        s = pl.program_id(0)
        sp = _softplus_neg(lam_ref[d:d + 1, :])
        br_ = br_ref[d:d + 1, :]
        bi_ = bi_ref[d:d + 1, :]

        @pl.when(s == 0)
        def _():
            xc = _conv_fwd(zc_ref[...], cw_ref, cb_ref, t_ctx)
            _, i, a, om = _gates(xc, wr_ref, br_, wi_ref, bi_, sp)
            a_s[...] = a
            b_s[...] = jnp.sqrt(om) * (i * xc)
            h0 = jnp.zeros((1, D), F32)
            hcar[0:1, :] = _scan_rows(d, t_ctx, a_s, b_s, h_s, h0)
            hs = h_s[...]
            row = lax.broadcasted_iota(jnp.int32, (t_ctx, D), 0)
            if d:
                hpc_ref[...] = jnp.where(row == t_ctx - 1, h0, pltpu.roll(hs, t_ctx - 1, 0))
            else:
                hpc_ref[...] = jnp.where(row == 0, h0, pltpu.roll(hs, 1, 0))

        @pl.when(s > 0)
        def _():
            for r0 in range(0, rows, rc):
                xc = _conv_cols(zx_ref, cw_ref, cb_ref, r0, rc, rows).reshape(rc * CB, D)
                _, i, a, om = _gates(xc, wr_ref, br_, wi_ref, bi_, sp)
                a3[r0:r0 + rc] = a.reshape(rc, CB, D)
                ax_ref[r0:r0 + rc] = a.reshape(rc, CB, D)
                b3[r0:r0 + rc] = (jnp.sqrt(om) * (i * xc)).reshape(rc, CB, D)

            def local(t, carry):
                hl, p = carry
                r = (rows - 1 - t) if d else t
                a = a3[r]
                hl = a * hl + b3[r]
                p = a * p
                b3[r] = hl
                a3[r] = p
                return hl, p

            hl, p = lax.fori_loop(0, rows, local, (jnp.zeros((CB, D), F32), jnp.ones((CB, D), F32)))
            cin = hcar[0:1, :]
            for j in (range(CB - 1, -1, -1) if d else range(CB)):
                cin_s[j:j + 1, :] = cin
                cin = hl[j:j + 1, :] + p[j:j + 1, :] * cin
            hcar[0:1, :] = cin
            c_in = cin_s[...]

            def fix(t, prev):
                r = (rows - 1 - t) if d else t
                h = b3[r] + a3[r] * c_in
                hx_ref[r] = h
                hpx_ref[r] = prev
                return h

            lax.fori_loop(0, rows, fix, c_in)

    full = lambda shp: pl.BlockSpec(shp, lambda s: (0,) * len(shp))
    colspec = pl.BlockSpec((rows, CB, D), lambda s: (0, blk(s), 0))
    outs = pl.pallas_call(
        body, grid=(nblk + 1,),
        in_specs=[pl.BlockSpec((t_ctx, D), lambda s: (cblk, 5)),
                  pl.BlockSpec((rows, CB, D), lambda s: (0, blk(s), 5)),
                  full((4, D)), full((1, D)),
                  pl.BlockSpec((None, H, DH, DH), lambda s: (d, 0, 0, 0)), full((2, D)),
                  pl.BlockSpec((None, H, DH, DH), lambda s: (d, 0, 0, 0)), full((2, D)), full((2, D))],
        out_specs=(colspec, colspec, colspec, full((t_ctx, D))),
        out_shape=(jax.ShapeDtypeStruct((rows, GRID_W, D), F32),) * 3 + (jax.ShapeDtypeStruct((t_ctx, D), F32),),
        scratch_shapes=[pltpu.VMEM((t_ctx, D), F32), pltpu.VMEM((t_ctx, D), F32), pltpu.VMEM((t_ctx, D), F32),
                        pltpu.VMEM((8, D), F32), pltpu.VMEM((rows, CB, D), F32), pltpu.VMEM((rows, CB, D), F32),
                        pltpu.VMEM((CB, D), F32)],
        name=f"rglru_fwd{d}", compiler_params=_cp(("arbitrary",)))(z, z3, cw, cb, wr, br, wi, bi, lam)
    return outs[0].reshape(t_lat, D), outs[1].reshape(t_lat, D), outs[2].reshape(t_lat, D), outs[3]


def _rglru_bwd(z, cw, cb, wr, br, wi, bi, lam, dh_lat, hp_lat, a_lat, hp_ctx, d, t_lat, t_ctx, prev=None):
    m = z.shape[0]
    rows = t_lat // GRID_W
    z3 = z.reshape(m // GRID_W, GRID_W, IN_COLS)
    nblk = GRID_W // CB
    fblk = _col_of(d, nblk)
    blk = lambda s: fblk(nblk - jnp.minimum(s, nblk - 1))
    cblk = t_lat // t_ctx
    rc = min(RCH, rows)
    last = prev is not None
    view3 = lambda v: v.reshape(rows, GRID_W, D)

    def body(*refs):
        (zc_ref, zx_ref, cw_ref, cb_ref, wr_ref, br_ref, wi_ref, bi_ref, lam_ref, dh_ref, hpx_ref, ax_ref,
         hpc_ref) = refs[:13]
        k = 13
        if last:
            pdx_ref, pdc_ref = refs[13:15]
            k = 15
        ox_ref, oc_ref, dwr_ref, dwi_ref, sum_ref, a_s, b_s, gcar, a3, b3, cin_s = refs[k:]
        s = pl.program_id(0)
        lam_d = lam_ref[d:d + 1, :]
        sp = _softplus_neg(lam_d)
        br_ = br_ref[d:d + 1, :]
        bi_ = bi_ref[d:d + 1, :]
        flat = lambda v: v.reshape(-1, D)

        @pl.when(s == 0)
        def _():
            gcar[...] = jnp.zeros_like(gcar)
            dwr_ref[...] = jnp.zeros_like(dwr_ref)
            dwi_ref[...] = jnp.zeros_like(dwi_ref)
            sum_ref[...] = jnp.zeros_like(sum_ref)


        def conv_sums(dxc, dxb, xm1, x0, xp1, xp2):
            sum_ref[3:4, :] += _colsum(flat(dxc))
            sum_ref[4:5, :] += _colsum(flat(dxb))
            sum_ref[8:9, :] += _colsum(flat(dxc * xm1))
            sum_ref[9:10, :] += _colsum(flat(dxc * x0))
            sum_ref[10:11, :] += _colsum(flat(dxc * xp1))
            sum_ref[11:12, :] += _colsum(flat(dxc * xp2))

        def gate_grads(g, hp, xc, rr, i, a, om):
            mult = jnp.sqrt(om)
            da = g * hp
            ixc = i * xc
            dmult = g * ixc
            dixc = g * mult
            di = dixc * xc
            dxc = dixc * i
            dlog_a = da * a - dmult * ((1.0 - om) / mult)
            dr = dlog_a * (-RG_C * sp)
            sum_ref[2:3, :] += _colsum(dlog_a * rr)
            drp = dr * rr * (1.0 - rr)
            dip = di * i * (1.0 - i)
            sum_ref[0:1, :] += _colsum(drp)
            sum_ref[1:2, :] += _colsum(dip)
            xcb = xc.astype(BF)
            drb = drp.astype(BF)
            dib = dip.astype(BF)
            parts = []
            for gi in range(H):
                gs = slice(gi * DH, (gi + 1) * DH)
                parts.append(_dot_nt(drb[:, gs], wr_ref[gi]) + _dot_nt(dib[:, gs], wi_ref[gi]))
                dwr_ref[gi] += _dot_tn(xcb[:, gs], drb[:, gs])
                dwi_ref[gi] += _dot_tn(xcb[:, gs], dib[:, gs])
            return dxc + jnp.concatenate(parts, axis=1)

        @pl.when(s < nblk)
        def _():
            def local(t, carry):
                c, q = carry
                r = t if d else (rows - 1 - t)
                a = ax_ref[r]
                c = a * (c + dh_ref[r])
                q = a * q
                b3[r] = c
                a3[r] = q
                return c, q

            c, q = lax.fori_loop(0, rows, local, (jnp.zeros((CB, D), F32), jnp.ones((CB, D), F32)))
            cin = gcar[0:1, :]
            for j in (range(CB) if d else range(CB - 1, -1, -1)):
                cin_s[j:j + 1, :] = cin
                cin = c[j:j + 1, :] + q[j:j + 1, :] * cin
            gcar[0:1, :] = cin
            c_in = cin_s[...]
            for r0 in (range(rows - rc, -1, -rc) if d else range(0, rows, rc)):
                if d:
                    lo = max(r0 - 1, 0)
                    cn = b3[lo:r0 + rc - 1] + a3[lo:r0 + rc - 1] * c_in
                    if r0 == 0:
                        cn = jnp.concatenate([c_in[None], cn], axis=0)
                else:
                    hi = min(r0 + rc + 1, rows)
                    cn = b3[r0 + 1:hi] + a3[r0 + 1:hi] * c_in
                    if hi == rows:
                        cn = jnp.concatenate([cn, c_in[None]], axis=0)
                g = flat(dh_ref[r0:r0 + rc] + cn)
                xc = flat(_conv_cols(zx_ref, cw_ref, cb_ref, r0, rc, rows))
                rr, i, a, om = _gates(xc, wr_ref, br_, wi_ref, bi_, sp)
                b3[r0:r0 + rc] = gate_grads(g, flat(hpx_ref[r0:r0 + rc]), xc, rr, i, a, om).reshape(rc, CB, D)
            if last:
                for r0 in range(0, rows, rc):
                    b3[r0:r0 + rc] = b3[r0:r0 + rc] + pdx_ref[r0:r0 + rc]
                for r0 in range(0, rows, rc):
                    w = _window(b3, r0 - 2, rc + 3, rows)
                    xw = _window(zx_ref, r0 - 1, rc + 3, rows)
                    dxc = w[2:rc + 2]
                    dxb = (w[3:rc + 3] * cw_ref[0:1, :] + dxc * cw_ref[1:2, :] + w[1:rc + 1] * cw_ref[2:3, :]
                           + w[0:rc] * cw_ref[3:4, :])
                    conv_sums(dxc, dxb, xw[0:rc], xw[1:rc + 1], xw[2:rc + 2], xw[3:rc + 3])
                    ox_ref[r0:r0 + rc] = dxb
            else:
                for r0 in range(0, rows, rc):
                    ox_ref[r0:r0 + rc] = b3[r0:r0 + rc]

        @pl.when(s == nblk)
        def _():
            r = t_ctx
            xb = zc_ref[...]
            xc = _conv_fwd(xb, cw_ref, cb_ref, r)
            rr, i, a, om = _gates(xc, wr_ref, br_, wi_ref, bi_, sp)
            a_s[...] = a
            b_s[...] = jnp.zeros((r, D), F32)
            c0 = gcar[0:1, :]
            _scan_rows(1 - d, r, a_s, b_s, b_s, c0)
            cs = b_s[...]
            row = lax.broadcasted_iota(jnp.int32, (r, D), 0)
            if d:
                g = jnp.where(row == 0, c0, pltpu.roll(cs, 1, 0))
            else:
                g = jnp.where(row == r - 1, c0, pltpu.roll(cs, r - 1, 0))
            dxc = gate_grads(g, hpc_ref[...], xc, rr, i, a, om)
            if last:
                dxc = dxc + pdc_ref[...]
                dxb = (_shift_rows(dxc, -1, r) * cw_ref[0:1, :] + dxc * cw_ref[1:2, :]
                       + _shift_rows(dxc, 1, r) * cw_ref[2:3, :] + _shift_rows(dxc, 2, r) * cw_ref[3:4, :])
                conv_sums(dxc, dxb, _shift_rows(xb, 1, r), xb, _shift_rows(xb, -1, r), _shift_rows(xb, -2, r))
                oc_ref[...] = dxb
            else:
                oc_ref[...] = dxc
            sum_ref[2:3, :] = sum_ref[2:3, :] * (RG_C * _sig(-lam_d))

    full = lambda shp: pl.BlockSpec(shp, lambda s: (0,) * len(shp))
    once = lambda shp: pl.BlockSpec(shp, lambda s: (0,) * len(shp), pipeline_mode=pl.Buffered(1))
    colspec = pl.BlockSpec((rows, CB, D), lambda s: (0, blk(s), 0))
    colonce = pl.BlockSpec((rows, CB, D), lambda s: (0, blk(s), 0), pipeline_mode=pl.Buffered(1))
    in_specs = [pl.BlockSpec((t_ctx, D), lambda s: (cblk, 5), pipeline_mode=pl.Buffered(1)),
                pl.BlockSpec((rows, CB, D), lambda s: (0, blk(s), 5), pipeline_mode=pl.Buffered(1)),
                full((4, D)), full((1, D)),
                pl.BlockSpec((None, H, DH, DH), lambda s: (d, 0, 0, 0)), full((2, D)),
                pl.BlockSpec((None, H, DH, DH), lambda s: (d, 0, 0, 0)), full((2, D)), full((2, D)),
                colonce, colonce, colonce, once((t_ctx, D))]
    args = [z, z3, cw, cb, wr, br, wi, bi, lam, view3(dh_lat), view3(hp_lat), view3(a_lat), hp_ctx]
    if last:
        in_specs += [colonce, once((t_ctx, D))]
        args += [view3(prev[0]), prev[1]]
    outs = pl.pallas_call(
        body, grid=(nblk + 1,), in_specs=in_specs,
        out_specs=(colspec, full((t_ctx, D)), full((H, DH, DH)), full((H, DH, DH)), full((16, D))),
        out_shape=(jax.ShapeDtypeStruct((rows, GRID_W, D), F32), jax.ShapeDtypeStruct((t_ctx, D), F32),
                   jax.ShapeDtypeStruct((H, DH, DH), F32), jax.ShapeDtypeStruct((H, DH, DH), F32),
                   jax.ShapeDtypeStruct((16, D), F32)),
        scratch_shapes=[pltpu.VMEM((t_ctx, D), F32), pltpu.VMEM((t_ctx, D), F32), pltpu.VMEM((8, D), F32),
                        pltpu.VMEM((rows, CB, D), F32), pltpu.VMEM((rows, CB, D), F32), pltpu.VMEM((CB, D), F32)],
        name=f"rglru_bwd{d}", compiler_params=_cp(("arbitrary",)))(*args)
    return (outs[0].reshape(t_lat, D), outs[1]) + tuple(outs[2:])


def _merge(o_f, o_b, h_f, h_b, z, x_all, tgt, mod, norm_g, ln_g, ln_b, p_a, p_b, w_out, t_lat):
    tm = 256
    nt = t_lat // tm

    def body(of_ref, ob_ref, hf_ref, hb_ref, z4_ref, z6_ref, z7_ref, z8_ref, x_ref, t_ref, mod_ref, ng_ref,
             lg_ref, lb_ref, pa_ref, pb_ref, wo_ref,
             do_ref, dh_ref, dz4a_ref, dz4b_ref, dz6a_ref, sh3_ref, gx_ref,
             y_ref, dout_ref, oa_ref, dpa_ref, obv_ref, dpb_ref, acc_ref):
        i = pl.program_id(0)
        lat = i < nt
        latf = lat.astype(F32)

        @pl.when(i == 0)
        def _():
            acc_ref[...] = jnp.zeros_like(acc_ref)

        def per_head(v):
            return jnp.concatenate(
                [jnp.broadcast_to(jnp.mean(v[:, h * DH:(h + 1) * DH], axis=-1, keepdims=True), (tm, DH))
                 for h in range(H)], axis=1)

        gt = mod_ref[0:1, 2 * D:3 * D]
        gfull = jnp.concatenate([ng_ref[...]] * H, axis=1)
        o = of_ref[...] + ob_ref[...]
        rinv = lax.rsqrt(per_head(o * o) + RMS_EPS)
        n = o * rinv
        na = n * gfull
        z4 = z4_ref[...]
        s4 = _sig(z4)
        silu4 = z4 * s4
        oa = na * silu4
        z6 = z6_ref[...]
        s6 = _sig(z6)
        silu6 = z6 * s6
        hsum = hf_ref[...] + hb_ref[...]
        obv = hsum * silu6
        pa = _dot(oa, pa_ref[...])
        pb = _dot(obv, pb_ref[...])
        s7 = _sig(z7_ref[...])
        s8 = _sig(z8_ref[...])
        y = s7 * pa + s8 * pb
        out = _dot(y, wo_ref[...])
        pre = ALPHA * x_ref[...] + gt * out
        mu = jnp.mean(pre, axis=-1, keepdims=True)
        xc = pre - mu
        rstd = lax.rsqrt(jnp.mean(xc * xc, axis=-1, keepdims=True) + LN_EPS)
        xhat = xc * rstd
        lg = lg_ref[...]
        diff = xhat * lg + lb_ref[...] - t_ref[...]
        acc_ref[8:9, :] += _colsum(diff * diff) * (0.5 / D * latf)
        dxn = diff * (1.0 / D)
        acc_ref[1:2, :] += _colsum(dxn * xhat) * latf
        acc_ref[2:3, :] += _colsum(dxn) * latf
        dxhat = dxn * lg
        dpre = rstd * (dxhat - jnp.mean(dxhat, axis=-1, keepdims=True)
                       - xhat * jnp.mean(dxhat * xhat, axis=-1, keepdims=True))
        gx_ref[...] = ALPHA * dpre
        acc_ref[0:1, :] += _colsum(dpre * out) * latf
        dout = dpre * gt
        dy = _dot_nt(dout, wo_ref[...])
        dpa = dy * s7
        dpb = dy * s8
        dz7 = dy * pa * (s7 * (1.0 - s7))
        dz8 = dy * pb * (s8 * (1.0 - s8))
        doa = _dot_nt(dpa, pa_ref[...])
        dob = _dot_nt(dpb, pb_ref[...])
        dh_ref[...] = dob * silu6
        dz6 = dob * hsum * (s6 * (1.0 + z6 * (1.0 - s6)))
        dna = doa * silu4
        dz4 = doa * na * (s4 * (1.0 + z4 * (1.0 - s4)))
        dng = _colsum(dna * n)
        acc_ref[7:8, 0:DH] += sum(dng[:, h * DH:(h + 1) * DH] for h in range(H)) * latf
        dn = dna * gfull
        do_ref[...] = rinv * (dn - n * per_head(dn * n))
        acc_ref[3:4, :] += _colsum(dz4) * latf
        acc_ref[4:5, :] += _colsum(dz6) * latf
        acc_ref[5:6, :] += _colsum(dz7) * latf
        acc_ref[6:7, :] += _colsum(dz8) * latf
        dz4b, dz6b = (dz4 * latf).astype(BF), (dz6 * latf).astype(BF)
        dz4a_ref[...] = dz4b[:, :CUT4]
        dz4b_ref[...] = dz4b[:, CUT4:]
        dz6a_ref[...] = dz6b[:, :CUT6]
        sh3_ref[:, 0:D - CUT6] = dz6b[:, CUT6:]
        sh3_ref[:, D - CUT6:2 * D - CUT6] = (dz7 * latf).astype(BF)
        sh3_ref[:, 2 * D - CUT6:] = (dz8 * latf).astype(BF)
        y_ref[...] = y.astype(BF)
        dout_ref[...] = dout.astype(BF)
        oa_ref[...] = oa.astype(BF)
        dpa_ref[...] = dpa.astype(BF)
        obv_ref[...] = obv.astype(BF)
        dpb_ref[...] = dpb.astype(BF)

        @pl.when(i == nt - 1)
        def _():
            acc_ref[9:10, :] = jnp.broadcast_to(jnp.sum(acc_ref[8:9, :], axis=-1, keepdims=True), (1, D))

    m = x_all.shape[0]
    lrow = lambda i: jnp.minimum(i, nt - 1)
    row = pl.BlockSpec((tm, D), lambda i: (lrow(i), 0))
    allrow = lambda cols: pl.BlockSpec((tm, cols), lambda i: (i, 0))
    zs = lambda cb: pl.BlockSpec((tm, D), lambda i: (lrow(i), cb))
    full = lambda shp: pl.BlockSpec(shp, lambda i: (0,) * len(shp))
    wfull = pl.BlockSpec((D, D), lambda i: (0, 0), pipeline_mode=pl.Buffered(1))
    f32o = jax.ShapeDtypeStruct((t_lat, D), F32)
    bfo = jax.ShapeDtypeStruct((t_lat, D), BF)
    bfall = lambda cols: jax.ShapeDtypeStruct((m, cols), BF)
    return pl.pallas_call(
        body, grid=(m // tm,),
        in_specs=[row, row, row, row, zs(4), zs(6), zs(7), zs(8), row, row, full((16, 3 * D)), full((1, DH)),
                  full((1, D)), full((1, D)), wfull, wfull, wfull],
        out_specs=(row, row) + tuple(allrow(c) for c in (CUT4, D - CUT4, CUT6, SHC)) + (row,) * 7 + (full((16, D)),),
        out_shape=(f32o, f32o, bfall(CUT4), bfall(D - CUT4), bfall(CUT6), bfall(SHC), f32o, bfo, bfo, bfo, bfo, bfo, bfo,
                   jax.ShapeDtypeStruct((16, D), F32)),
        name="merge", compiler_params=_cp(("arbitrary",), VMEM_LIMIT_MERGE))(
            o_f, o_b, h_f, h_b, z, z, z, z, x_all, tgt, mod, norm_g, ln_g, ln_b, p_a, p_b, w_out)


def _wgrad(a, b, name):
    tm = 1024

    def body(a_ref, b_ref, o_ref):
        @pl.when(pl.program_id(0) == 0)
        def _():
            o_ref[...] = jnp.zeros_like(o_ref)
        o_ref[...] += _dot_tn(a_ref[...], b_ref[...])

    row = pl.BlockSpec((tm, D), lambda i: (i, 0))
    return pl.pallas_call(body, grid=(a.shape[0] // tm,), in_specs=[row, row],
                          out_specs=pl.BlockSpec((D, D), lambda i: (0, 0)),
                          out_shape=jax.ShapeDtypeStruct((D, D), F32), name=name,
                          compiler_params=_cp(("arbitrary",)))(a, b)


def _pack_shard2(dz4b, dz5_lat, dz5_ctx, dz6a):
    m, t_lat, t_ctx = dz4b.shape[0], dz5_lat.shape[0], dz5_ctx.shape[0]
    tm = t_ctx
    nlt = t_lat // tm
    w4 = D - CUT4

    def body(a_ref, bl_ref, bc_ref, c_ref, o_ref):
        i = pl.program_id(0)
        o_ref[:, 0:w4] = a_ref[...]
        o_ref[:, w4:w4 + D] = jnp.where(i < nlt, bl_ref[...], bc_ref[...]).astype(BF)
        o_ref[:, w4 + D:] = c_ref[...]

    return pl.pallas_call(
        body, grid=(m // tm,),
        in_specs=[pl.BlockSpec((tm, w4), lambda i: (i, 0)),
                  pl.BlockSpec((tm, D), lambda i: (jnp.minimum(i, nlt - 1), 0)),
                  pl.BlockSpec((tm, D), lambda i: (0, 0)),
                  pl.BlockSpec((tm, CUT6), lambda i: (i, 0))],
        out_specs=pl.BlockSpec((tm, SHC), lambda i: (i, 0)),
        out_shape=jax.ShapeDtypeStruct((m, SHC), BF), name="pack_shard2",
        compiler_params=_cp(("arbitrary",)))(dz4b, dz5_lat, dz5_ctx, dz6a)


def _wgrad_in(u_all, dz_shards):
    m = u_all.shape[0]
    assert m % 128 == 0
    tm = m // 8
    out = None
    for k, dz_k in enumerate(dz_shards):
        def body(u_ref, dz_ref, *rest):
            o_ref = rest[-1]

            @pl.when(pl.program_id(0) == 0)
            def _():
                o_ref[...] = jnp.zeros_like(o_ref)
            o_ref[0] += _dot_tn(u_ref[...], dz_ref[...])

        out = pl.pallas_call(
            body, grid=(m // tm,),
            in_specs=[pl.BlockSpec((tm, D), lambda i: (i, 0)), pl.BlockSpec((tm, SHC), lambda i: (i, 0))]
            + ([] if out is None else [_ANY]),
            out_specs=pl.BlockSpec((1, D, SHC), lambda i, k=k: (k, 0, 0)),
            out_shape=jax.ShapeDtypeStruct((NSH, D, SHC), F32),
            input_output_aliases={} if out is None else {2: 0},
            name=f"wgrad_in{k}", compiler_params=_cp(("arbitrary",)))(u_all, dz_k, *(() if out is None else (out,)))
    return out


def _du(dz_shards, w_in_g, x_all, mod, gxres, n_lat_tiles, sums=()):
    m = x_all.shape[0]
    tm = 256
    nt = m // tm
    nct = nt - n_lat_tiles
    ns = len(sums)
    rblk = lambda i: jnp.where(i < nct, n_lat_tiles + i, i - nct)
    lblk = lambda i: jnp.maximum(i - nct, 0)

    def body(*refs):
        dz_refs, refs = refs[:NSH], refs[NSH - 1:]
        _, w_ref, x_ref, mod_ref, gr_ref = refs[:5]
        sum_refs = refs[5:5 + ns]
        gx_ref, dm_ref = refs[5 + ns:7 + ns]
        got_refs = refs[7 + ns:7 + 2 * ns]
        sems = refs[7 + 2 * ns:]
        i = pl.program_id(0)
        is_lat = i >= nct

        @pl.when(i == 0)
        def _():
            dm_ref[...] = jnp.zeros_like(dm_ref)
            if ns:
                for cp in _rs_chip_copies(sum_refs, got_refs, *sems):
                    cp.start()

        du = _dot_nt(dz_refs[0][...], w_ref[0])
        for n in range(1, NSH):
            du = du + _dot_nt(dz_refs[n][...], w_ref[n])
        sc = jnp.where(is_lat, mod_ref[0:1, D:2 * D], mod_ref[1:2, D:2 * D])
        dsh = _colsum(du)
        dsc = _colsum(du * x_ref[...])

        @pl.when(is_lat)
        def _():
            gx_ref[...] = du * (1.0 + sc) + gr_ref[...]
            dm_ref[0:1, 0:D] += dsh
            dm_ref[0:1, D:2 * D] += dsc

        @pl.when(jnp.logical_not(is_lat))
        def _():
            dm_ref[1:2, 0:D] += dsh
            dm_ref[1:2, D:2 * D] += dsc

        if ns:
            @pl.when(i == nt - 1)
            def _():
                for cp in _rs_chip_copies(sum_refs, got_refs, *sems):
                    cp.wait()

    outs = pl.pallas_call(
        body, grid=(nt,),
        in_specs=[pl.BlockSpec((tm, SHC), lambda i: (rblk(i), 0))] * NSH + [
                  pl.BlockSpec((NSH, D, SHC), lambda i: (0, 0, 0), pipeline_mode=pl.Buffered(1)),
                  pl.BlockSpec((tm, D), lambda i: (rblk(i), 0)),
                  pl.BlockSpec((16, 3 * D), lambda i: (0, 0)),
                  pl.BlockSpec((tm, D), lambda i: (lblk(i), 0))] + [_ANY] * ns,
        out_specs=[pl.BlockSpec((tm, D), lambda i: (lblk(i), 0)),
                   pl.BlockSpec((8, 2 * D), lambda i: (0, 0))] + [_ANY] * ns,
        out_shape=[jax.ShapeDtypeStruct((n_lat_tiles * tm, D), F32), jax.ShapeDtypeStruct((8, 2 * D), F32)]
        + [jax.ShapeDtypeStruct((3,) + g.shape[1:], g.dtype) for g in sums],
        scratch_shapes=[pltpu.SemaphoreType.DMA((3 * ns,)), pltpu.SemaphoreType.DMA((3 * ns,))] if ns else [],
        name="du", compiler_params=_cp(("arbitrary",)))(*dz_shards, w_in_g, x_all, mod, gxres, *sums)
    return outs[0], outs[1], list(outs[2:])


def _row_tile(rows, cols):
    t = 8
    while t * 2 * cols * 4 <= (1 << 20) and rows % (t * 2) == 0:
        t *= 2
    return t


def _adamw_update(w_ref, g_ref, m_ref, v_ref, d_ref, nm_ref, nv_ref):
    gg = g_ref[...]
    m2 = ADAM_B1 * m_ref[...] + (1.0 - ADAM_B1) * gg
    v2 = ADAM_B2 * v_ref[...] + (1.0 - ADAM_B2) * (gg * gg)
    m_hat = m2 / (1.0 - ADAM_B1 ** ADAM_STEP)
    v_hat = v2 / (1.0 - ADAM_B2 ** ADAM_STEP)
    d_ref[...] = -ADAM_LR * (m_hat / (jnp.sqrt(v_hat) + ADAM_EPS) + ADAM_WD * w_ref[...])
    nm_ref[...] = m2
    nv_ref[...] = v2


def _adamw_many(ws, gs, ms, vs):
    n = len(ws)

    def body(*refs):
        for j in range(n):
            _adamw_update(*refs[4 * j:4 * j + 4], *refs[4 * n + 3 * j:4 * n + 3 * j + 3])

    args = [a for quad in zip(ws, gs, ms, vs) for a in quad]
    outs = pl.pallas_call(body, out_shape=[jax.ShapeDtypeStruct(w.shape, F32) for w in ws for _ in range(3)],
                          name="adamw_small", compiler_params=_cp())(*args)
    return outs[0::3], outs[1::3], outs[2::3]


def _adamw(w, g, m, v, name):
    rows, cols = w.shape
    tr = _row_tile(rows, cols)

    def body(*refs):
        _adamw_update(*refs)

    spec = pl.BlockSpec((tr, cols), lambda i: (i, 0))
    o = jax.ShapeDtypeStruct((rows, cols), F32)
    return pl.pallas_call(body, grid=(rows // tr,), in_specs=[spec] * 4, out_specs=(spec,) * 3,
                          out_shape=(o, o, o), name=name, compiler_params=_cp(("arbitrary",)))(w, g, m, v)


_ANY = pl.BlockSpec(memory_space=pl.ANY)


def _place():
    return lax.axis_index("x"), lax.axis_index("y"), lax.axis_index("c")


def _ag_copies(ins, outs, send, recv, fsend, frecv, lsem):
    x, y, c = _place()
    me = 2 * x + y
    chips = ((1 - x, y), (x, 1 - y), (1 - x, 1 - y))
    local, chip, hand = [], [], []
    for j in range(len(ins)):
        hr = ins[j].shape[0] // 2
        half = pl.ds(pl.multiple_of(c * hr, 8), hr)
        local.append(pltpu.make_async_copy(ins[j], outs[j].at[me], lsem.at[j]))
        for k, (px, py) in enumerate(chips):
            chip.append(pltpu.make_async_remote_copy(
                src_ref=ins[j].at[half, :], dst_ref=outs[j].at[me, half, :], send_sem=send.at[3 * j + k],
                recv_sem=recv.at[3 * j + k], device_id=(px, py, c), device_id_type=MESH))
            got = outs[j].at[2 * px + py, half, :]
            hand.append(pltpu.make_async_remote_copy(
                src_ref=got, dst_ref=got, send_sem=fsend.at[3 * j + k], recv_sem=frecv.at[3 * j + k],
                device_id=(x, y, 1 - c), device_id_type=MESH))
    return local, chip, hand


def _ag_sems(n):
    return [pltpu.SemaphoreType.DMA((3 * n,))] * 4 + [pltpu.SemaphoreType.DMA((n,))]


def _ag_finish(local, chip, hand, done=0):
    for k in range(done, len(chip)):
        chip[k].wait_recv()
        hand[k].start()
    for cp in chip:
        cp.wait_send()
    for k in range(done):
        hand[k].wait_send()
    for k in range(done, len(chip)):
        hand[k].wait_send()
        hand[k].wait_recv()
    for cp in local:
        cp.wait()


def _mod_tp(c8, c_ctx, w_mod_sh, b_mod_sh):
    mc = w_mod_sh.shape[1]

    def body(c8_ref, cctx_ref, w_ref, b_ref, mod_ref, cc_ref, cc_s, part_s, send1, recv1, send3, recv3):
        x, y, c = _place()
        me = 4 * x + 2 * y + c
        ms = 2 * x + y
        copies = []
        for k in range(1, 8):
            peer = (x ^ ((k >> 2) & 1), y ^ ((k >> 1) & 1), c ^ (k & 1))
            cp = pltpu.make_async_remote_copy(src_ref=c8_ref, dst_ref=cc_s.at[me], send_sem=send1.at[k],
                                              recv_sem=recv1.at[k], device_id=peer, device_id_type=MESH)
            cp.start()
            copies.append(cp)
        cc_s[me] = c8_ref[...]
        for cp in copies:
            cp.wait()
        cc_ref[...] = jnp.zeros_like(cc_ref)
        for j in range(8):
            cc_ref[j:j + 1, :] = cc_s[j, 0:1, :]
        cc_ref[8:9, :] = cctx_ref[...]
        v = cc_ref[...]
        part_s[ms] = _dot(v * _sig(v), w_ref[...]) + b_ref[...]
        copies = []
        for k in range(1, 4):
            peer = (x ^ ((k >> 1) & 1), y ^ (k & 1), c)
            cp = pltpu.make_async_remote_copy(src_ref=part_s.at[ms], dst_ref=part_s.at[ms], send_sem=send3.at[k],
                                              recv_sem=recv3.at[k], device_id=peer, device_id_type=MESH)
            cp.start()
            copies.append(cp)
        for cp in copies:
            cp.wait()
        for s in range(NSH):
            mod_ref[:, s * mc:(s + 1) * mc] = part_s[s]

    vm = pl.BlockSpec(memory_space=pltpu.VMEM)
    return pl.pallas_call(
        body, in_specs=[vm] * 4, out_specs=(vm, vm),
        out_shape=(jax.ShapeDtypeStruct((16, NSH * mc), F32), jax.ShapeDtypeStruct((16, D), F32)),
        scratch_shapes=[pltpu.VMEM((8, 8, D), F32), pltpu.VMEM((NSH, 16, mc), F32),
                        pltpu.SemaphoreType.DMA((8,)), pltpu.SemaphoreType.DMA((8,)),
                        pltpu.SemaphoreType.DMA((4,)), pltpu.SemaphoreType.DMA((4,))],
        name="mod_tp", compiler_params=_cp())(c8, c_ctx, w_mod_sh, b_mod_sh)


def _inproj_ag(x_all, mod, w_in_sh, b_in, narrow_sh, n_lat_tiles):
    m = x_all.shape[0]
    assert m % (11 * 16) == 0
    tm = m // 11
    nt = m // tm
    n_lat = n_lat_tiles * 256
    x_, y_ = lax.axis_index("x"), lax.axis_index("y")
    sids = jnp.stack([2 * x_ + y_, 2 * (1 - x_) + y_, 2 * x_ + 1 - y_, 2 * (1 - x_) + 1 - y_]).astype(jnp.int32)

    def body(sid_ref, x_ref, mod_ref, b_ref, wsh_ref, nsh_ref, z_ref, u_ref, wg_ref, ng_ref, w_s, u_s, *sems):
        n = pl.program_id(0)
        i = pl.program_id(1)
        rows = pl.ds(pl.multiple_of(i * tm, tm), tm)
        ag = ((wsh_ref, nsh_ref), (wg_ref, ng_ref)) + tuple(sems[:5])
        wsem = sems[5]

        def load(src):
            cp = pltpu.make_async_copy(src, w_s, wsem)
            cp.start()
            cp.wait()

        @pl.when((n == 0) & (i == 0))
        def _():
            local, chip, _ = _ag_copies(*ag)
            for cp in chip + local:
                cp.start()
            load(wsh_ref)

        for k in range(NSH - 1):
            @pl.when((n == k + 1) & (i == 0))
            def _():
                _, chip, hand = _ag_copies(*ag)
                chip[k].wait_recv()
                hand[k].start()
                hand[k].wait_recv()
                load(wg_ref.at[sid_ref[k + 1]])

        @pl.when(n == 0)
        def _():
            is_lat = (i * tm + lax.broadcasted_iota(jnp.int32, (tm, 1), 0)) < n_lat
            u, _ = _modulate(x_ref, mod_ref, is_lat)
            u_s[rows, :] = u.astype(BF)
            u_ref[...] = u.astype(BF)

        z_ref[...] = _dot(u_s[rows, :], w_s[...]) + b_ref[...]

        @pl.when((n == NSH - 1) & (i == nt - 1))
        def _():
            _ag_finish(*_ag_copies(*ag), done=NSH - 1)

    first = lambda n, i: jnp.where(n == 0, i, nt - 1)
    outs = pl.pallas_call(
        body, grid_spec=pltpu.PrefetchScalarGridSpec(
            num_scalar_prefetch=1, grid=(NSH, nt),
            in_specs=[pl.BlockSpec((tm, D), lambda n, i, sid: (first(n, i), 0)),
                      pl.BlockSpec((16, 3 * D), lambda n, i, sid: (0, 0)),
                      pl.BlockSpec((1, SHC), lambda n, i, sid: (0, sid[n])), _ANY, _ANY],
            out_specs=[pl.BlockSpec((tm, SHC), lambda n, i, sid: (i, sid[n])),
                       pl.BlockSpec((tm, D), lambda n, i, sid: (first(n, i), 0)), _ANY, _ANY],
            scratch_shapes=[pltpu.VMEM((D, SHC), BF), pltpu.VMEM((m, D), BF)] + _ag_sems(2)
            + [pltpu.SemaphoreType.DMA]),
        out_shape=[jax.ShapeDtypeStruct((m, IN_COLS), F32), jax.ShapeDtypeStruct((m, D), BF),
                   jax.ShapeDtypeStruct((NSH,) + w_in_sh.shape, BF),
                   jax.ShapeDtypeStruct((NSH,) + narrow_sh.shape, narrow_sh.dtype)],
        name="inproj_ag", compiler_params=_cp(("arbitrary", "arbitrary")))(sids, x_all, mod, b_in, w_in_sh, narrow_sh)
    return outs


def _rs_sibling(grads):
    n = len(grads)

    def body(*refs):
        ins, got = refs[:n], refs[n:2 * n]
        send, recv = refs[2 * n:]
        x, y, c = _place()
        copies = []
        for j in range(n):
            hr = ins[j].shape[1] // 2
            for s in range(NSH):
                give = ins[j].at[s, pl.ds(pl.multiple_of((1 - c) * hr, 8), hr), :]
                cp = pltpu.make_async_remote_copy(src_ref=give, dst_ref=got[j].at[s], send_sem=send.at[NSH * j + s],
                                                  recv_sem=recv.at[NSH * j + s], device_id=(x, y, 1 - c),
                                                  device_id_type=MESH)
                cp.start()
                copies.append(cp)
        for cp in copies:
            cp.wait()

    half = [jax.ShapeDtypeStruct((NSH, g.shape[1] // 2, g.shape[2]), F32) for g in grads]
    return pl.pallas_call(
        body, in_specs=[_ANY] * n, out_specs=[_ANY] * n, out_shape=half,
        scratch_shapes=[pltpu.SemaphoreType.DMA((NSH * n,)), pltpu.SemaphoreType.DMA((NSH * n,))],
        name="rs_sibling")(*grads)


def _core_vec():
    return lax.axis_index("c").astype(jnp.int32).reshape(1)


def _rs_add1(g, got, name):
    _, r, cols = g.shape
    hr = r // 2
    tr = _row_tile(hr, cols)
    nb = hr // tr

    def body(c_ref, g_ref, got_ref, o_ref):
        o_ref[...] = (g_ref[...] + got_ref[...]).astype(BF)

    spec = pl.BlockSpec((1, tr, cols), lambda s, i, c_ref: (s, i, 0))
    return pl.pallas_call(
        body, grid_spec=pltpu.PrefetchScalarGridSpec(
            num_scalar_prefetch=1, grid=(NSH, nb),
            in_specs=[pl.BlockSpec((1, tr, cols), lambda s, i, c_ref: (s, c_ref[0] * nb + i, 0)), spec],
            out_specs=spec),
        out_shape=jax.ShapeDtypeStruct((NSH, hr, cols), BF), name=name,
        compiler_params=_cp(("arbitrary", "arbitrary")))(_core_vec(), g, got)


def _rs_add2(sums, got, name):
    _, hr, cols = sums.shape
    tr = _row_tile(hr, cols)
    nb = hr // tr
    place = jnp.stack([2 * lax.axis_index("x") + lax.axis_index("y"), lax.axis_index("c")]).astype(jnp.int32)

    def body(p_ref, s_ref, got_ref, o_ref):
        f = lambda v: v.astype(F32)
        o_ref[...] = f(s_ref[0]) + f(got_ref[0]) + f(got_ref[1]) + f(got_ref[2])

    return pl.pallas_call(
        body, grid_spec=pltpu.PrefetchScalarGridSpec(
            num_scalar_prefetch=1, grid=(nb,),
            in_specs=[pl.BlockSpec((1, tr, cols), lambda i, p_ref: (p_ref[0], i, 0)),
                      pl.BlockSpec((3, tr, cols), lambda i, p_ref: (0, i, 0))],
            out_specs=pl.BlockSpec((tr, cols), lambda i, p_ref: (p_ref[1] * nb + i, 0))),
        out_shape=jax.ShapeDtypeStruct((2 * hr, cols), F32), name=name,
        compiler_params=_cp(("arbitrary",)))(place, sums, got)


def _rs_chip_copies(ins, got, send, recv):
    x, y, c = _place()
    peers = ((1 - x, y), (x, 1 - y), (1 - x, 1 - y))
    return [pltpu.make_async_remote_copy(src_ref=ins[j].at[2 * px + py], dst_ref=got[j].at[k],
                                         send_sem=send.at[3 * j + k], recv_sem=recv.at[3 * j + k],
                                         device_id=(px, py, c), device_id_type=MESH)
            for j in range(len(ins)) for k, (px, py) in enumerate(peers)]


def _ag_sibling(fulls):
    n = len(fulls)
    nck = 4

    def body(*refs):
        outs = refs[n:2 * n]
        send, recv = refs[2 * n:]
        x, y, c = _place()
        copies = []
        for j in range(n):
            qr = outs[j].shape[0] // (2 * nck)
            for k in range(nck):
                rows = outs[j].at[pl.ds(pl.multiple_of((c * nck + k) * qr, 8), qr), :]
                cp = pltpu.make_async_remote_copy(src_ref=rows, dst_ref=rows, send_sem=send.at[nck * j + k],
                                                  recv_sem=recv.at[nck * j + k], device_id=(x, y, 1 - c),
                                                  device_id_type=MESH)
                cp.start()
                copies.append(cp)
        for cp in copies:
            cp.wait()

    return pl.pallas_call(
        body, in_specs=[_ANY] * n, out_specs=[_ANY] * n,
        out_shape=[jax.ShapeDtypeStruct(f.shape, F32) for f in fulls],
        input_output_aliases={j: j for j in range(n)},
        scratch_shapes=[pltpu.SemaphoreType.DMA((nck * n,)), pltpu.SemaphoreType.DMA((nck * n,))],
        name="ag_sibling")(*fulls)


def _allreduce_small(buf):
    rows = buf.shape[0]
    pr = rows // 8

    def body(in_ref, out_ref, stage, send1, recv1, send2, recv2):
        x, y, c = _place()
        me = 4 * x + 2 * y + c

        def peer(k):
            kx, ky, kc = (k >> 2) & 1, (k >> 1) & 1, k & 1
            return (x ^ kx, y ^ ky, c ^ kc)

        def piece(ref, idx):
            return ref.at[pl.ds(pl.multiple_of(idx * pr, 8), pr), :]

        copies = []
        for k in range(1, 8):
            px, py, pc = peer(k)
            cp = pltpu.make_async_remote_copy(src_ref=piece(in_ref, 4 * px + 2 * py + pc), dst_ref=stage.at[k],
                                              send_sem=send1.at[k], recv_sem=recv1.at[k],
                                              device_id=(px, py, pc), device_id_type=MESH)
            cp.start()
            copies.append(cp)
        for cp in copies:
            cp.wait()
        acc = piece(in_ref, me)[...]
        for k in range(1, 8):
            acc = acc + stage[k]
        piece(out_ref, me)[...] = acc
        copies = []
        for k in range(1, 8):
            cp = pltpu.make_async_remote_copy(src_ref=piece(out_ref, me), dst_ref=piece(out_ref, me),
                                              send_sem=send2.at[k], recv_sem=recv2.at[k],
                                              device_id=peer(k), device_id_type=MESH)
            cp.start()
            copies.append(cp)
        for cp in copies:
            cp.wait()

    vm = pl.BlockSpec(memory_space=pltpu.VMEM)
    return pl.pallas_call(
        body, in_specs=[vm], out_specs=vm, out_shape=jax.ShapeDtypeStruct((rows, D), F32),
        scratch_shapes=[pltpu.VMEM((8, pr, D), F32)] + [pltpu.SemaphoreType.DMA((8,))] * 4,
        name="allreduce_small", compiler_params=_cp())(buf)


def _rows(a):
    flat = a.reshape(-1)
    pad = (-flat.shape[0]) % D
    if pad:
        flat = jnp.concatenate([flat, jnp.zeros((pad,), flat.dtype)])
    return flat.reshape(-1, D)


def _pad_rows(a, mult):
    pad = (-a.shape[0]) % mult
    return jnp.concatenate([a, jnp.zeros((pad, a.shape[1]), a.dtype)]) if pad else a


def _local_step(x, c, ctx, c_ctx, tgt, me, shard, sh, b_mod, b_in, norm_g, cb, wr, wi, ln_g, ln_b):
    t_lat, t_ctx = x.shape[0], ctx.shape[0]
    nlt = t_lat // 256
    nlb, ncb = t_lat // RB, t_ctx // RB
    mc = 3 * D // NSH
    mod_all, cc_all = _mod_tp(jnp.zeros((8, D), F32).at[0].set(c), c_ctx.reshape(1, D), sh["w_mod"],
                              lax.dynamic_slice_in_dim(b_mod, shard * mc, mc, axis=1))
    mod = jnp.zeros((16, 3 * D), F32).at[0].set(mod_all[me]).at[1].set(mod_all[8])
    cc = jnp.zeros((16, D), F32).at[0].set(c).at[1].set(c_ctx)
    x_all = jnp.concatenate([x, ctx], axis=0)
    z, u_all, w_in_g, nar = _inproj_ag(x_all, mod, sh["w_in"], b_in, sh["narrow"], nlt)
    nar = jnp.transpose(nar, (1, 0, 2)).reshape(-1, D)
    lbl, cw, br, bi, lam = nar[0:4].reshape(2, 2, D), nar[4:8], nar[8:10], nar[10:12], nar[12:14]
    o0, st0, (w_mod_g, p_a, p_b, w_out) = _gla_fwd(z, lbl, 0, nlb, ncb,
                                                   gather=[sh[k] for k in ("w_mod", "p_a", "p_b", "w_out")])
    p_a, p_b, w_out = p_a.reshape(D, D), p_b.reshape(D, D), w_out.reshape(D, D)
    o1, st1, _ = _gla_fwd(z, lbl, 1, nlb, ncb)
    h0, hp0, a0, hpc0 = _rglru_fwd(z, cw, cb, wr, br, wi, bi, lam, 0, t_lat, t_ctx)
    h1, hp1, a1, hpc1 = _rglru_fwd(z, cw, cb, wr, br, wi, bi, lam, 1, t_lat, t_ctx)
    (do, dh, dz4a, dz4b, dz6a, dz_sh3, gxres, y, dout, oa, dpa, obv, dpb, acc) = _merge(
        o0, o1, h0, h1, z, x_all, tgt, mod, norm_g, ln_g, ln_b, p_a, p_b, w_out, t_lat)
    gp_a = _wgrad(oa, dpa, "wgrad_pa")
    gp_b = _wgrad(obv, dpb, "wgrad_pb")
    gw_out = _wgrad(y, dout, "wgrad_wout")
    dxc_lat, dxc_ctx, dwr0, dwi0, sb0 = _rglru_bwd(z, cw, cb, wr, br, wi, bi, lam, dh, hp0, a0, hpc0, 0, t_lat, t_ctx)
    dz5_lat, dz5_ctx, dwr1, dwi1, sb1 = _rglru_bwd(z, cw, cb, wr, br, wi, bi, lam, dh, hp1, a1, hpc1, 1, t_lat, t_ctx,
                                                   prev=(dxc_lat, dxc_ctx))
    dq0, dv0, dz1, sa0 = _gla_bwd(z, lbl, do, st0, 0, nlb, ncb)
    dz_sh0, dz_sh1, sa1 = _gla_bwd(z, lbl, do, st1, 1, nlb, ncb, prev=(dq0, dv0, dz1, dz4a))
    dz = (dz_sh0, dz_sh1, _pack_shard2(dz4b, dz5_lat, dz5_ctx, dz6a), dz_sh3)
    big = dict(w_in=_wgrad_in(u_all, dz), p_a=gp_a.reshape(NSH, D // NSH, D),
               p_b=gp_b.reshape(NSH, D // NSH, D), w_out=gw_out.reshape(NSH, D // NSH, D))
    grads = [big[k] for k in _RS]
    sums = [_rs_add1(g, b, f"rs_add1_{j}") for j, (g, b) in enumerate(zip(grads, _rs_sibling(grads)))]
    gx, dm, got = _du(dz, w_in_g, x_all, mod, gxres, nlt, sums)
    fulls = [_rs_add2(a, b, f"rs_add2_{j}") for j, (a, b) in enumerate(zip(sums, got))]
    big = dict(zip(_RS, _ag_sibling(fulls)))
    dmod = jnp.zeros((16, 3 * D), F32).at[0:2, 0:2 * D].set(dm[0:2]).at[0, 2 * D:].set(acc[0])
    dcc = _mod_bwd(cc, dmod, w_mod_g)
    small = dict(
        c_ctx=dcc[1:2], b_mod=(dmod[0] + dmod[1]).reshape(3, D),
        b_in=jnp.stack([sa1[2], sa0[0], sa1[0], sa1[3], acc[3], sb1[4], acc[4], acc[5], acc[6]]),
        lb_logits=jnp.stack([sa0[1], sa1[1], -sa0[1], -sa1[1]]),
        norm_a_g=acc[7:8], conv_w=sb1[8:12], conv_b=sb1[3:4],
        w_r=jnp.stack([dwr0, dwr1]).reshape(-1, D), w_i=jnp.stack([dwi0, dwi1]).reshape(-1, D),
        b_r=jnp.stack([sb0[0], sb1[0]]), b_i=jnp.stack([sb0[1], sb1[1]]), lam=jnp.stack([sb0[2], sb1[2]]),
        ln_g=acc[1:2], ln_b=acc[2:3])
    return acc[9, 0], gx, big, small, dmod, cc_all


_RS =("w_in", "p_a", "p_b", "w_out")
_SMALL =("c_ctx", "b_mod", "b_in", "lb_logits", "norm_a_g", "conv_w", "conv_b", "w_r", "w_i", "b_r", "b_i", "lam",
          "ln_g", "ln_b")
_BIG = ("w_mod", "w_in", "p_a", "p_b", "w_out")
_COL_SHARDED = ("lb_logits", "conv_w", "b_r", "b_i", "lam")
_WEIGHTS = ("c_ctx", "w_mod", "b_mod", "w_in", "b_in", "lb_logits", "norm_a_g", "conv_w", "conv_b", "w_r", "b_r", "w_i",
            "b_i", "lam", "p_a", "p_b", "w_out", "ln_g", "ln_b")


def kernel(x, c, ctx, c_ctx, w_mod, b_mod, w_in, b_in, lb_logits, norm_a_g, conv_w, conv_b, w_r, b_r, w_i, b_i, lam, p_a, p_b, w_out, ln_g, ln_b, loss_target, m_c_ctx, m_w_mod, m_b_mod, m_w_in, m_b_in, m_lb_logits, m_norm_a_g, m_conv_w, m_conv_b, m_w_r, m_b_r, m_w_i, m_b_i, m_lam, m_p_a, m_p_b, m_w_out, m_ln_g, m_ln_b, v_c_ctx, v_w_mod, v_b_mod, v_w_in, v_b_in, v_lb_logits, v_norm_a_g, v_conv_w, v_conv_b, v_w_r, v_b_r, v_w_i, v_b_i, v_lam, v_p_a, v_p_b, v_w_out, v_ln_g, v_ln_b):
    w = dict(c_ctx=c_ctx, w_mod=w_mod, b_mod=b_mod, w_in=w_in, b_in=b_in, lb_logits=lb_logits, norm_a_g=norm_a_g,
             conv_w=conv_w, conv_b=conv_b, w_r=w_r, b_r=b_r, w_i=w_i, b_i=b_i, lam=lam, p_a=p_a, p_b=p_b, w_out=w_out,
             ln_g=ln_g, ln_b=ln_b)
    m = dict(c_ctx=m_c_ctx, w_mod=m_w_mod, b_mod=m_b_mod, w_in=m_w_in, b_in=m_b_in, lb_logits=m_lb_logits,
             norm_a_g=m_norm_a_g, conv_w=m_conv_w, conv_b=m_conv_b, w_r=m_w_r, b_r=m_b_r, w_i=m_w_i, b_i=m_b_i,
             lam=m_lam, p_a=m_p_a, p_b=m_p_b, w_out=m_w_out, ln_g=m_ln_g, ln_b=m_ln_b)
    v = dict(c_ctx=v_c_ctx, w_mod=v_w_mod, b_mod=v_b_mod, w_in=v_w_in, b_in=v_b_in, lb_logits=v_lb_logits,
             norm_a_g=v_norm_a_g, conv_w=v_conv_w, conv_b=v_conv_b, w_r=v_w_r, b_r=v_b_r, w_i=v_w_i, b_i=v_b_i,
             lam=v_lam, p_a=v_p_a, p_b=v_p_b, w_out=v_w_out, ln_g=v_ln_g, ln_b=v_ln_b)
    shard = 2 * lax.axis_index("x") + lax.axis_index("y")
    cs = D // NSH

    sh = {k: w[k][0].astype(BF) for k in _BIG}
    sh["narrow"] = _pad_rows(jnp.concatenate([lb_logits.reshape(4, cs), conv_w[0], b_r[0], b_i[0], lam[0]], axis=0), 8)
    me = 2 * shard + lax.axis_index("c")
    loss, gx, big, small, dmod, cc_all = _local_step(
        x[0], c[0], ctx[0], c_ctx, loss_target[0], me, shard, sh, b_mod, b_in, norm_a_g, conv_b, w_r[0], w_i[0],
        ln_g, ln_b)
    loss = lax.psum(loss, ("x", "y", "c"))

    dmod_rows = jnp.zeros((16, 3 * D), F32).at[me].set(dmod[0]).at[8].set(dmod[1]).reshape(48, D)
    sizes = [small[k].shape[0] for k in _SMALL]
    red = _allreduce_small(_pad_rows(jnp.concatenate([_pad_rows(small[k], 8) for k in _SMALL] + [dmod_rows],
                                                     axis=0), 64))
    grads = {}
    off = 0
    for k, n in zip(_SMALL, sizes):
        g = red[off:off + n]
        off += n + (-n) % 8
        if k == "norm_a_g":
            g = g[:, :DH]
        if k in _COL_SHARDED:
            g = lax.dynamic_slice_in_dim(g, shard * cs, cs, axis=1)
        grads[k] = g.reshape(w[k].shape)
    for k in _RS:
        grads[k] = big[k].reshape(w[k].shape)
    dmod_all = red[off:off + 48].reshape(16, 3 * D)
    mc = 3 * D // NSH
    grads["w_mod"] = _wmod_grad(cc_all, lax.dynamic_slice_in_dim(dmod_all, shard * mc, mc, axis=1)).reshape(
        w["w_mod"].shape)

    delta, new_m, new_v = {}, {}, {}
    for k in _BIG:
        shp = w[k].shape
        two = lambda a: a.reshape(shp[-2], shp[-1])
        d_, m_, v_ = _adamw(two(w[k]), two(grads[k]), two(m[k]), two(v[k]), f"adamw_{k}")
        delta[k], new_m[k], new_v[k] = d_.reshape(shp), m_.reshape(shp), v_.reshape(shp)
    d_, m_, v_ = _adamw_many(*[[t[k] for k in _SMALL] for t in (w, grads, m, v)])
    delta.update(zip(_SMALL, d_))
    new_m.update(zip(_SMALL, m_))
    new_v.update(zip(_SMALL, v_))

    return (loss, gx[None], *[grads[k] for k in _WEIGHTS], *[delta[k] for k in _WEIGHTS],
            *[new_m[k] for k in _WEIGHTS], *[new_v[k] for k in _WEIGHTS])
```

```python
import functools

import jax
import jax.numpy as jnp
from jax import lax
from jax.experimental import pallas as pl
from jax.experimental.pallas import tpu as pltpu

F32 = jnp.float32
BF = jnp.bfloat16
MESH = pl.DeviceIdType.MESH

D = 1024
H = 8
DH = 128
CH = 64
RB = 256
NCK = RB // CH
GRID_W = 64
CB = 8
RCH = 16
IN_COLS = 9 * D
NSH = 4
SHC = IN_COLS // NSH
CUT2 = SHC - 2 * D
CUT4 = 2 * SHC - 4 * D
CUT6 = 3 * SHC - 6 * D
RG_C = 8.0
ALPHA = 2.0 ** 0.25
LN_EPS = 1e-5
RMS_EPS = 1e-6
Q_SCALE = DH ** -0.5
ADAM_LR, ADAM_B1, ADAM_B2, ADAM_EPS, ADAM_WD, ADAM_STEP = 0.001, 0.9, 0.999, 1e-08, 0.01, 10
VMEM_LIMIT = 56 * 1024 * 1024
VMEM_LIMIT_MERGE = 60 * 1024 * 1024


def _cp(sem=None, vmem=VMEM_LIMIT):
    return pltpu.CompilerParams(dimension_semantics=sem, vmem_limit_bytes=vmem)


def _sig(x):
    return 0.5 * jnp.tanh(0.5 * x) + 0.5


def _dot(a, b):
    return jnp.dot(a.astype(BF), b.astype(BF), preferred_element_type=F32)


def _dot_nt(a, b):
    return lax.dot_general(a.astype(BF), b.astype(BF), (((1,), (1,)), ((), ())), preferred_element_type=F32)


def _dot_tn(a, b):
    return lax.dot_general(a.astype(BF), b.astype(BF), (((0,), (0,)), ((), ())), preferred_element_type=F32)


def _colsum(v):
    return jnp.sum(v, axis=0, keepdims=True)


def _mod_bwd(cc, dmod, w_mod_g):
    def body(cc_ref, dm_ref, w_ref, dcc_ref):
        v = cc_ref[...]
        sg = _sig(v)
        ds = jnp.zeros((16, D), F32)
        for k in range(NSH):
            ds = ds + _dot_nt(dm_ref[:, k * 768:(k + 1) * 768], w_ref[k])
        dcc_ref[...] = ds * (sg * (1.0 + v * (1.0 - sg)))
    return pl.pallas_call(body, out_shape=jax.ShapeDtypeStruct((16, D), F32),
                          name="mod_bwd", compiler_params=_cp())(cc, dmod, w_mod_g)


def _wmod_grad(cc, dmod_cols):
    def body(cc_ref, dm_ref, dw_ref):
        v = cc_ref[...]
        dw_ref[...] = _dot_tn(v * _sig(v), dm_ref[...])
    return pl.pallas_call(body, out_shape=jax.ShapeDtypeStruct((D, dmod_cols.shape[1]), F32),
                          name="wmod_grad", compiler_params=_cp())(cc, dmod_cols)


def _modulate(x_ref, mod_ref, is_lat):
    sh = jnp.where(is_lat, mod_ref[0:1, 0:D], mod_ref[1:2, 0:D])
    sc = jnp.where(is_lat, mod_ref[0:1, D:2 * D], mod_ref[1:2, D:2 * D])
    return x_ref[...] * (1.0 + sc) + sh, sc


def _gla_mask(d, n):
    row = lax.broadcasted_iota(jnp.int32, (n, n), 0)
    col = lax.broadcasted_iota(jnp.int32, (n, n), 1)
    same = (row // CH) == (col // CH)
    return same & ((row <= col) if d else (row >= col))


def _chunk_cumsum(v, rev):
    n = v.shape[0]
    pos = lax.broadcasted_iota(jnp.int32, v.shape, 0) & (CH - 1)
    s = 1
    while s < CH:
        if rev:
            v = v + jnp.where(pos < CH - s, pltpu.roll(v, n - s, 0), 0.0)
        else:
            v = v + jnp.where(pos >= s, pltpu.roll(v, s, 0), 0.0)
        s *= 2
    return v


def _chunk_rows(c):
    return slice(c * CH, (c + 1) * CH)


def _gla_features(zq, zf, lb, d):
    sq = _sig(zq)
    q = zq * sq * Q_SCALE
    sf = _sig(zf)
    f = lb + (1.0 - lb) * sf
    k = 1.0 - f
    g = _chunk_cumsum(jnp.log(f), d)
    last = 0 if d else CH - 1
    gls = [g[c * CH + last:c * CH + last + 1, :] for c in range(NCK)]
    glb = jnp.concatenate([jnp.broadcast_to(gl, (CH, D)) for gl in gls], axis=0)
    eg, eig, eeg = jnp.exp(g), jnp.exp(-g), jnp.exp(glb - g)
    decs = [jnp.exp(gl) for gl in gls]
    return sq, sf, f, q * eg, k * eig, k * eeg, eg, eig, eeg, decs


def _lower_bound(lbl_ref, d):
    return _sig(lbl_ref[0, d:d + 1, :] - lbl_ref[1, d:d + 1, :])


def _gla_rb(d, nlb, ncb):
    nrb = nlb + ncb
    if d:
        return lambda s: nrb - 1 - s
    return lambda s: jnp.where(s < ncb, nlb + s, s - ncb)


def _gla_fwd(z, lbl, d, nlb, ncb, gather=()):
    m = z.shape[0]
    nrb = nlb + ncb
    rb = _gla_rb(d, nlb, ncb)
    ng = len(gather)

    def body(*refs):
        q_ref, f_ref, v_ref, lbl_ref = refs[:4]
        ag_in = refs[4:4 + ng]
        o_ref, st_ref = refs[4 + ng:6 + ng]
        ag_out = refs[6 + ng:6 + 2 * ng]
        S = refs[6 + 2 * ng]
        ag_sems = refs[7 + 2 * ng:]
        s = pl.program_id(0)

        @pl.when(s == 0)
        def _():
            S[...] = jnp.zeros_like(S)
            if ng:
                local, chip, _ = _ag_copies(ag_in, ag_out, *ag_sems)
                for cp in chip + local:
                    cp.start()

        lb = _lower_bound(lbl_ref, d)
        mb = _gla_mask(d, CH)
        _, _, _, qd, ki, ke, _, _, _, decs = _gla_features(q_ref[...], f_ref[...], lb, d)
        qd, ki, ke, v = qd.astype(BF), ki.astype(BF), ke.astype(BF), v_ref[...].astype(BF)
        order = range(NCK - 1, -1, -1) if d else range(NCK)
        for h in range(H):
            hs = slice(h * DH, (h + 1) * DH)
            intra, upd = {}, {}
            for c in range(NCK):
                rs = _chunk_rows(c)
                a = jnp.where(mb, _dot_nt(qd[rs, hs], ki[rs, hs]), 0.0)
                intra[c] = _dot(a, v[rs, hs])
                upd[c] = _dot_tn(v[rs, hs], ke[rs, hs])
            st = S[h]
            for c in order:
                rs = _chunk_rows(c)
                st_ref[c, h] = st
                o_ref[rs, hs] = intra[c] + _dot_nt(qd[rs, hs], st)
                st = st * decs[c][:, hs] + upd[c]
            S[h] = st

        if ng:
            @pl.when(s == nrb - 1)
            def _():
                _ag_finish(*_ag_copies(ag_in, ag_out, *ag_sems))

    def zspec(cb):
        return pl.BlockSpec((RB, D), lambda s: (rb(s), cb))

    outs = pl.pallas_call(
        body, grid=(nrb,),
        in_specs=[zspec(0), zspec(1 + d), zspec(3), pl.BlockSpec((2, 2, D), lambda s: (0, 0, 0))] + [_ANY] * ng,
        out_specs=[pl.BlockSpec((RB, D), lambda s: (rb(s), 0)),
                   pl.BlockSpec((NCK, H, DH, DH), lambda s: (rb(s), 0, 0, 0))] + [_ANY] * ng,
        out_shape=[jax.ShapeDtypeStruct((m, D), F32), jax.ShapeDtypeStruct((m // CH, H, DH, DH), F32)]
        + [jax.ShapeDtypeStruct((NSH,) + g.shape, g.dtype) for g in gather],
        scratch_shapes=[pltpu.VMEM((H, DH, DH), F32)] + (_ag_sems(ng) if ng else []),
        name=f"gla_fwd{d}", compiler_params=_cp(("arbitrary",)))(z, z, z, lbl, *gather)
    return outs[0], outs[1], list(outs[2:])


def _gla_bwd(z, lbl, do_lat, states, d, nlb, ncb, prev=None):
    m = z.shape[0]
    nrb = nlb + ncb
    fwd_rb = _gla_rb(d, nlb, ncb)
    rb = lambda s: fwd_rb(nrb - 1 - s)
    last = prev is not None

    def body(*refs):
        if last:
            (q_ref, f_ref, v_ref, lbl_ref, do_ref, st_ref, pq_ref, pv_ref, dz1_ref, dz4a_ref, sh0_ref, sh1_ref, sum_ref,
             dS) = refs
        else:
            q_ref, f_ref, v_ref, lbl_ref, do_ref, st_ref, o0_ref, o1_ref, o2_ref, sum_ref, dS = refs
        s = pl.program_id(0)
        is_lat = rb(s) < nlb

        @pl.when(s == 0)
        def _():
            dS[...] = jnp.zeros_like(dS)
            sum_ref[...] = jnp.zeros_like(sum_ref)

        lb = _lower_bound(lbl_ref, d)
        mb = _gla_mask(d, RB)
        zq = q_ref[...]
        sq, sf, f, qd, ki, ke, eg, eig, eeg, decs = _gla_features(zq, f_ref[...], lb, d)
        qdb, kib, keb, vb = qd.astype(BF), ki.astype(BF), ke.astype(BF), v_ref[...].astype(BF)
        dob = jnp.where(is_lat, do_ref[...], 0.0).astype(BF)
        order = range(NCK) if d else range(NCK - 1, -1, -1)
        dqd_h, dki_h, dke_h, dv_h, ddec_h = [], [], [], [], []
        for h in range(H):
            hs = slice(h * DH, (h + 1) * DH)
            a = jnp.where(mb, _dot_nt(qdb[:, hs], kib[:, hs]), 0.0).astype(BF)
            da = jnp.where(mb, _dot_nt(dob[:, hs], vb[:, hs]), 0.0).astype(BF)
            dqd_i = _dot(da, kib[:, hs])
            dki_h.append(_dot_tn(da, qdb[:, hs]))
            dvi = _dot_tn(a, dob[:, hs])
            dqd, inc = {}, {}
            for c in range(NCK):
                rs = _chunk_rows(c)
                dqd[c] = dqd_i[rs, :] + _dot(dob[rs, hs], st_ref[c, h])
                inc[c] = _dot_tn(dob[rs, hs], qdb[rs, hs])
            dst = dS[h]
            dke, dv, ddec = {}, {}, {}
            for c in order:
                rs = _chunk_rows(c)
                dv[c] = dvi[rs, :] + _dot_nt(keb[rs, hs], dst)
                dke[c] = _dot(vb[rs, hs], dst)
                ddec[c] = _colsum(st_ref[c, h] * dst)
                dst = inc[c] + dst * decs[c][:, hs]
            dS[h] = dst
            cat = lambda t: jnp.concatenate([t[c] for c in range(NCK)], axis=0)
            dqd_h.append(cat(dqd))
            dke_h.append(cat(dke))
            dv_h.append(cat(dv))
            ddec_h.append([ddec[c] for c in range(NCK)])
        lanes = lambda parts: jnp.concatenate(parts, axis=1)
        dqd, dki, dke, dv = lanes(dqd_h), lanes(dki_h), lanes(dke_h), lanes(dv_h)
        dq = dqd * eg
        dk = dki * eig + dke * eeg
        dke_ke = dke * ke
        dg = dqd * qd - dki * ki - dke_ke
        dgl = [_colsum(dke_ke[_chunk_rows(c), :]) + lanes([ddec_h[h][c] for h in range(H)]) * decs[c]
               for c in range(NCK)]
        dglb = jnp.concatenate([jnp.broadcast_to(t, (CH, D)) for t in dgl], axis=0)
        df = (_chunk_cumsum(dg, 1 - d) + dglb) / f - dk
        dzf = df * (1.0 - lb) * (sf * (1.0 - sf))
        sum_ref[0:1, :] += _colsum(dzf)
        sum_ref[1:2, :] += _colsum(df * (1.0 - sf))
        if last:
            dz0 = (dq + pq_ref[...]) * (Q_SCALE * (sq * (1.0 + zq * (1.0 - sq))))
            dz3 = dv + pv_ref[...]
            sum_ref[2:3, :] += _colsum(dz0)
            sum_ref[3:4, :] += _colsum(dz3)
            dz2 = dzf.astype(BF)
            sh0_ref[:, 0:D] = dz0.astype(BF)
            sh0_ref[:, D:2 * D] = dz1_ref[...]
            sh0_ref[:, 2 * D:] = dz2[:, :CUT2]
            sh1_ref[:, 0:D - CUT2] = dz2[:, CUT2:]
            sh1_ref[:, D - CUT2:2 * D - CUT2] = dz3.astype(BF)
            sh1_ref[:, 2 * D - CUT2:] = dz4a_ref[...]
        else:
            o0_ref[...] = dq
            o1_ref[...] = dv
            o2_ref[...] = dzf.astype(BF)

        @pl.when(s == nrb - 1)
        def _():
            sum_ref[1:2, :] = sum_ref[1:2, :] * (lb * (1.0 - lb))

    def zspec(cb):
        return pl.BlockSpec((RB, D), lambda s: (rb(s), cb))

    rowspec = pl.BlockSpec((RB, D), lambda s: (rb(s), 0))
    in_specs = [zspec(0), zspec(1 + d), zspec(3), pl.BlockSpec((2, 2, D), lambda s: (0, 0, 0)),
                pl.BlockSpec((RB, D), lambda s: (jnp.minimum(rb(s), nlb - 1), 0)),
                pl.BlockSpec((NCK, H, DH, DH), lambda s: (rb(s), 0, 0, 0))]
    args = [z, z, z, lbl, do_lat, states]
    sumspec = pl.BlockSpec((8, D), lambda s: (0, 0))
    if last:
        in_specs += [rowspec, rowspec, rowspec, pl.BlockSpec((RB, CUT4), lambda s: (rb(s), 0))]
        args += list(prev)
        shspec = pl.BlockSpec((RB, SHC), lambda s: (rb(s), 0))
        out_specs = (shspec, shspec, sumspec)
        out_shape = (jax.ShapeDtypeStruct((m, SHC), BF), jax.ShapeDtypeStruct((m, SHC), BF))
    else:
        out_specs = (rowspec, rowspec, rowspec, sumspec)
        out_shape = (jax.ShapeDtypeStruct((m, D), F32), jax.ShapeDtypeStruct((m, D), F32),
                     jax.ShapeDtypeStruct((m, D), BF))
    return pl.pallas_call(
        body, grid=(nrb,), in_specs=in_specs, out_specs=out_specs,
        out_shape=out_shape + (jax.ShapeDtypeStruct((8, D), F32),),
        scratch_shapes=[pltpu.VMEM((H, DH, DH), F32)],
        name=f"gla_bwd{d}", compiler_params=_cp(("arbitrary",)))(*args)


def _shift_rows(v, k, r):
    row = lax.broadcasted_iota(jnp.int32, v.shape, 0)
    rolled = pltpu.roll(v, k % r, 0)
    return jnp.where((row >= k) & (row < r + k), rolled, 0.0)


def _conv_fwd(xb, cw_ref, cb_ref, r):
    return (cb_ref[...] + _shift_rows(xb, 1, r) * cw_ref[0:1, :] + xb * cw_ref[1:2, :]
            + _shift_rows(xb, -1, r) * cw_ref[2:3, :] + _shift_rows(xb, -2, r) * cw_ref[3:4, :])


def _window(ref, lo, n, rows):
    parts = []
    if lo < 0:
        parts.append(jnp.zeros((-lo,) + tuple(ref.shape[1:]), F32))
    parts.append(ref[max(lo, 0):min(lo + n, rows)])
    if lo + n > rows:
        parts.append(jnp.zeros((lo + n - rows,) + tuple(ref.shape[1:]), F32))
    return parts[0] if len(parts) == 1 else jnp.concatenate(parts, axis=0)


def _conv_cols(x_ref, cw_ref, cb_ref, r0, n, rows):
    w = _window(x_ref, r0 - 1, n + 3, rows)
    return (cb_ref[...] + w[0:n] * cw_ref[0:1, :] + w[1:n + 1] * cw_ref[1:2, :] + w[2:n + 2] * cw_ref[2:3, :]
            + w[3:n + 3] * cw_ref[3:4, :])


def _softplus_neg(lam):
    y = jnp.exp(-jnp.abs(lam))
    u = 1.0 + y
    tiny = u == 1.0
    l1p = jnp.where(tiny, y, jnp.log(u) * (y / jnp.where(tiny, 1.0, u - 1.0)))
    return jnp.maximum(-lam, 0.0) + l1p


def _gates(xc, wr_ref, br, wi_ref, bi, sp):
    xcb = xc.astype(BF)
    rs, is_ = [], []
    for g in range(H):
        gs = slice(g * DH, (g + 1) * DH)
        rs.append(jnp.dot(xcb[:, gs], wr_ref[g].astype(BF), preferred_element_type=F32))
        is_.append(jnp.dot(xcb[:, gs], wi_ref[g].astype(BF), preferred_element_type=F32))
    r = _sig(jnp.concatenate(rs, axis=1) + br)
    i = _sig(jnp.concatenate(is_, axis=1) + bi)
    log_a = (-RG_C * r) * sp
    a = jnp.exp(log_a)
    t = jnp.tanh(log_a)
    om = (-2.0 * t) / (1.0 - t)
    return r, i, a, om


def _scan_rows(d, nrows, a_s, b_s, h_s, h0):
    nsl = nrows // 8

    def slab(j, h):
        jj = (nsl - 1 - j) if d else j
        r0 = pl.multiple_of(jj * 8, 8)
        for t in (range(7, -1, -1) if d else range(8)):
            h = a_s[pl.ds(r0 + t, 1), :] * h + b_s[pl.ds(r0 + t, 1), :]
            h_s[pl.ds(r0 + t, 1), :] = h
        return h

    return lax.fori_loop(0, nsl, slab, h0)


def _col_of(d, ncols):
    if d:
        return lambda s: ncols - jnp.maximum(s, 1)
    return lambda s: jnp.maximum(s, 1) - 1


def _rglru_fwd(z, cw, cb, wr, br, wi, bi, lam, d, t_lat, t_ctx):
    m = z.shape[0]
    rows = t_lat // GRID_W
    z3 = z.reshape(m // GRID_W, GRID_W, IN_COLS)
    nblk = GRID_W // CB
    blk = _col_of(d, nblk)
    cblk = t_lat // t_ctx
    rc = min(RCH, rows)

    def body(zc_ref, zx_ref, cw_ref, cb_ref, wr_ref, br_ref, wi_ref, bi_ref, lam_ref,
             hx_ref, hpx_ref, ax_ref, hpc_ref, a_s, b_s, h_s, hcar, a3, b3, cin_s):
        s = pl.program_id(0)
        sp = _softplus_neg(lam_ref[d:d + 1, :])
        br_ = br_ref[d:d + 1, :]
        bi_ = bi_ref[d:d + 1, :]

        @pl.when(s == 0)
        def _():
            xc = _conv_fwd(zc_ref[...], cw_ref, cb_ref, t_ctx)
            _, i, a, om = _gates(xc, wr_ref, br_, wi_ref, bi_, sp)
            a_s[...] = a
            b_s[...] = jnp.sqrt(om) * (i * xc)
            h0 = jnp.zeros((1, D), F32)
            hcar[0:1, :] = _scan_rows(d, t_ctx, a_s, b_s, h_s, h0)
            hs = h_s[...]
            row = lax.broadcasted_iota(jnp.int32, (t_ctx, D), 0)
            if d:
                hpc_ref[...] = jnp.where(row == t_ctx - 1, h0, pltpu.roll(hs, t_ctx - 1, 0))
            else:
                hpc_ref[...] = jnp.where(row == 0, h0, pltpu.roll(hs, 1, 0))

        @pl.when(s > 0)
        def _():
            for r0 in range(0, rows, rc):
                xc = _conv_cols(zx_ref, cw_ref, cb_ref, r0, rc, rows).reshape(rc * CB, D)
                _, i, a, om = _gates(xc, wr_ref, br_, wi_ref, bi_, sp)
                a3[r0:r0 + rc] = a.reshape(rc, CB, D)
                ax_ref[r0:r0 + rc] = a.reshape(rc, CB, D)
                b3[r0:r0 + rc] = (jnp.sqrt(om) * (i * xc)).reshape(rc, CB, D)

            def local(t, carry):
                hl, p = carry
                r = (rows - 1 - t) if d else t
                a = a3[r]
                hl = a * hl + b3[r]
                p = a * p
                b3[r] = hl
                a3[r] = p
                return hl, p

            hl, p = lax.fori_loop(0, rows, local, (jnp.zeros((CB, D), F32), jnp.ones((CB, D), F32)))
            cin = hcar[0:1, :]
            for j in (range(CB - 1, -1, -1) if d else range(CB)):
                cin_s[j:j + 1, :] = cin
                cin = hl[j:j + 1, :] + p[j:j + 1, :] * cin
            hcar[0:1, :] = cin
            c_in = cin_s[...]

            def fix(t, prev):
                r = (rows - 1 - t) if d else t
                h = b3[r] + a3[r] * c_in
                hx_ref[r] = h
                hpx_ref[r] = prev
                return h

            lax.fori_loop(0, rows, fix, c_in)

    full = lambda shp: pl.BlockSpec(shp, lambda s: (0,) * len(shp))
    colspec = pl.BlockSpec((rows, CB, D), lambda s: (0, blk(s), 0))
    outs = pl.pallas_call(
        body, grid=(nblk + 1,),
        in_specs=[pl.BlockSpec((t_ctx, D), lambda s: (cblk, 5)),
                  pl.BlockSpec((rows, CB, D), lambda s: (0, blk(s), 5)),
                  full((4, D)), full((1, D)),
                  pl.BlockSpec((None, H, DH, DH), lambda s: (d, 0, 0, 0)), full((2, D)),
                  pl.BlockSpec((None, H, DH, DH), lambda s: (d, 0, 0, 0)), full((2, D)), full((2, D))],
        out_specs=(colspec, colspec, colspec, full((t_ctx, D))),
        out_shape=(jax.ShapeDtypeStruct((rows, GRID_W, D), F32),) * 3 + (jax.ShapeDtypeStruct((t_ctx, D), F32),),
        scratch_shapes=[pltpu.VMEM((t_ctx, D), F32), pltpu.VMEM((t_ctx, D), F32), pltpu.VMEM((t_ctx, D), F32),
                        pltpu.VMEM((8, D), F32), pltpu.VMEM((rows, CB, D), F32), pltpu.VMEM((rows, CB, D), F32),
                        pltpu.VMEM((CB, D), F32)],
        name=f"rglru_fwd{d}", compiler_params=_cp(("arbitrary",)))(z, z3, cw, cb, wr, br, wi, bi, lam)
    return outs[0].reshape(t_lat, D), outs[1].reshape(t_lat, D), outs[2].reshape(t_lat, D), outs[3]


def _rglru_bwd(z, cw, cb, wr, br, wi, bi, lam, dh_lat, hp_lat, a_lat, hp_ctx, d, t_lat, t_ctx, prev=None):
    m = z.shape[0]
    rows = t_lat // GRID_W
    z3 = z.reshape(m // GRID_W, GRID_W, IN_COLS)
    nblk = GRID_W // CB
    fblk = _col_of(d, nblk)
    blk = lambda s: fblk(nblk - jnp.minimum(s, nblk - 1))
    cblk = t_lat // t_ctx
    rc = min(RCH, rows)
    last = prev is not None
    view3 = lambda v: v.reshape(rows, GRID_W, D)

    def body(*refs):
        (zc_ref, zx_ref, cw_ref, cb_ref, wr_ref, br_ref, wi_ref, bi_ref, lam_ref, dh_ref, hpx_ref, ax_ref,
         hpc_ref) = refs[:13]
        k = 13
        if last:
            pdx_ref, pdc_ref = refs[13:15]
            k = 15
        ox_ref, oc_ref, dwr_ref, dwi_ref, sum_ref, a_s, b_s, gcar, a3, b3, cin_s = refs[k:]
        s = pl.program_id(0)
        lam_d = lam_ref[d:d + 1, :]
        sp = _softplus_neg(lam_d)
        br_ = br_ref[d:d + 1, :]
        bi_ = bi_ref[d:d + 1, :]
        flat = lambda v: v.reshape(-1, D)

        @pl.when(s == 0)
        def _():
            gcar[...] = jnp.zeros_like(gcar)
            dwr_ref[...] = jnp.zeros_like(dwr_ref)
            dwi_ref[...] = jnp.zeros_like(dwi_ref)
            sum_ref[...] = jnp.zeros_like(sum_ref)


        def conv_sums(dxc, dxb, xm1, x0, xp1, xp2):
            sum_ref[3:4, :] += _colsum(flat(dxc))
            sum_ref[4:5, :] += _colsum(flat(dxb))
            sum_ref[8:9, :] += _colsum(flat(dxc * xm1))
            sum_ref[9:10, :] += _colsum(flat(dxc * x0))
            sum_ref[10:11, :] += _colsum(flat(dxc * xp1))
            sum_ref[11:12, :] += _colsum(flat(dxc * xp2))

        def gate_grads(g, hp, xc, rr, i, a, om):
            mult = jnp.sqrt(om)
            da = g * hp
            ixc = i * xc
            dmult = g * ixc
            dixc = g * mult
            di = dixc * xc
            dxc = dixc * i
            dlog_a = da * a - dmult * ((1.0 - om) / mult)
            dr = dlog_a * (-RG_C * sp)
            sum_ref[2:3, :] += _colsum(dlog_a * rr)
            drp = dr * rr * (1.0 - rr)
            dip = di * i * (1.0 - i)
            sum_ref[0:1, :] += _colsum(drp)
            sum_ref[1:2, :] += _colsum(dip)
            xcb = xc.astype(BF)
            drb = drp.astype(BF)
            dib = dip.astype(BF)
            parts = []
            for gi in range(H):
                gs = slice(gi * DH, (gi + 1) * DH)
                parts.append(_dot_nt(drb[:, gs], wr_ref[gi]) + _dot_nt(dib[:, gs], wi_ref[gi]))
                dwr_ref[gi] += _dot_tn(xcb[:, gs], drb[:, gs])
                dwi_ref[gi] += _dot_tn(xcb[:, gs], dib[:, gs])
            return dxc + jnp.concatenate(parts, axis=1)

        @pl.when(s < nblk)
        def _():
            def local(t, carry):
                c, q = carry
                r = t if d else (rows - 1 - t)
                a = ax_ref[r]
                c = a * (c + dh_ref[r])
                q = a * q
                b3[r] = c
                a3[r] = q
                return c, q

            c, q = lax.fori_loop(0, rows, local, (jnp.zeros((CB, D), F32), jnp.ones((CB, D), F32)))
            cin = gcar[0:1, :]
            for j in (range(CB) if d else range(CB - 1, -1, -1)):
                cin_s[j:j + 1, :] = cin
                cin = c[j:j + 1, :] + q[j:j + 1, :] * cin
            gcar[0:1, :] = cin
            c_in = cin_s[...]
            for r0 in (range(rows - rc, -1, -rc) if d else range(0, rows, rc)):
                if d:
                    lo = max(r0 - 1, 0)
                    cn = b3[lo:r0 + rc - 1] + a3[lo:r0 + rc - 1] * c_in
                    if r0 == 0:
                        cn = jnp.concatenate([c_in[None], cn], axis=0)
                else:
                    hi = min(r0 + rc + 1, rows)
                    cn = b3[r0 + 1:hi] + a3[r0 + 1:hi] * c_in
                    if hi == rows:
                        cn = jnp.concatenate([cn, c_in[None]], axis=0)
                g = flat(dh_ref[r0:r0 + rc] + cn)
                xc = flat(_conv_cols(zx_ref, cw_ref, cb_ref, r0, rc, rows))
                rr, i, a, om = _gates(xc, wr_ref, br_, wi_ref, bi_, sp)
                b3[r0:r0 + rc] = gate_grads(g, flat(hpx_ref[r0:r0 + rc]), xc, rr, i, a, om).reshape(rc, CB, D)
            if last:
                for r0 in range(0, rows, rc):
                    b3[r0:r0 + rc] = b3[r0:r0 + rc] + pdx_ref[r0:r0 + rc]
                for r0 in range(0, rows, rc):
                    w = _window(b3, r0 - 2, rc + 3, rows)
                    xw = _window(zx_ref, r0 - 1, rc + 3, rows)
                    dxc = w[2:rc + 2]
                    dxb = (w[3:rc + 3] * cw_ref[0:1, :] + dxc * cw_ref[1:2, :] + w[1:rc + 1] * cw_ref[2:3, :]
                           + w[0:rc] * cw_ref[3:4, :])
                    conv_sums(dxc, dxb, xw[0:rc], xw[1:rc + 1], xw[2:rc + 2], xw[3:rc + 3])
                    ox_ref[r0:r0 + rc] = dxb
            else:
                for r0 in range(0, rows, rc):
                    ox_ref[r0:r0 + rc] = b3[r0:r0 + rc]

        @pl.when(s == nblk)
        def _():
            r = t_ctx
            xb = zc_ref[...]
            xc = _conv_fwd(xb, cw_ref, cb_ref, r)
            rr, i, a, om = _gates(xc, wr_ref, br_, wi_ref, bi_, sp)
            a_s[...] = a
            b_s[...] = jnp.zeros((r, D), F32)
            c0 = gcar[0:1, :]
            _scan_rows(1 - d, r, a_s, b_s, b_s, c0)
            cs = b_s[...]
            row = lax.broadcasted_iota(jnp.int32, (r, D), 0)
            if d:
                g = jnp.where(row == 0, c0, pltpu.roll(cs, 1, 0))
            else:
                g = jnp.where(row == r - 1, c0, pltpu.roll(cs, r - 1, 0))
            dxc = gate_grads(g, hpc_ref[...], xc, rr, i, a, om)
            if last:
                dxc = dxc + pdc_ref[...]
                dxb = (_shift_rows(dxc, -1, r) * cw_ref[0:1, :] + dxc * cw_ref[1:2, :]
                       + _shift_rows(dxc, 1, r) * cw_ref[2:3, :] + _shift_rows(dxc, 2, r) * cw_ref[3:4, :])
                conv_sums(dxc, dxb, _shift_rows(xb, 1, r), xb, _shift_rows(xb, -1, r), _shift_rows(xb, -2, r))
                oc_ref[...] = dxb
            else:
                oc_ref[...] = dxc
            sum_ref[2:3, :] = sum_ref[2:3, :] * (RG_C * _sig(-lam_d))

    full = lambda shp: pl.BlockSpec(shp, lambda s: (0,) * len(shp))
    once = lambda shp: pl.BlockSpec(shp, lambda s: (0,) * len(shp), pipeline_mode=pl.Buffered(1))
    colspec = pl.BlockSpec((rows, CB, D), lambda s: (0, blk(s), 0))
    colonce = pl.BlockSpec((rows, CB, D), lambda s: (0, blk(s), 0), pipeline_mode=pl.Buffered(1))
    in_specs = [pl.BlockSpec((t_ctx, D), lambda s: (cblk, 5), pipeline_mode=pl.Buffered(1)),
                pl.BlockSpec((rows, CB, D), lambda s: (0, blk(s), 5), pipeline_mode=pl.Buffered(1)),
                full((4, D)), full((1, D)),
                pl.BlockSpec((None, H, DH, DH), lambda s: (d, 0, 0, 0)), full((2, D)),
                pl.BlockSpec((None, H, DH, DH), lambda s: (d, 0, 0, 0)), full((2, D)), full((2, D)),
                colonce, colonce, colonce, once((t_ctx, D))]
    args = [z, z3, cw, cb, wr, br, wi, bi, lam, view3(dh_lat), view3(hp_lat), view3(a_lat), hp_ctx]
    if last:
        in_specs += [colonce, once((t_ctx, D))]
        args += [view3(prev[0]), prev[1]]
    outs = pl.pallas_call(
        body, grid=(nblk + 1,), in_specs=in_specs,
        out_specs=(colspec, full((t_ctx, D)), full((H, DH, DH)), full((H, DH, DH)), full((16, D))),
        out_shape=(jax.ShapeDtypeStruct((rows, GRID_W, D), F32), jax.ShapeDtypeStruct((t_ctx, D), F32),
                   jax.ShapeDtypeStruct((H, DH, DH), F32), jax.ShapeDtypeStruct((H, DH, DH), F32),
                   jax.ShapeDtypeStruct((16, D), F32)),
        scratch_shapes=[pltpu.VMEM((t_ctx, D), F32), pltpu.VMEM((t_ctx, D), F32), pltpu.VMEM((8, D), F32),
                        pltpu.VMEM((rows, CB, D), F32), pltpu.VMEM((rows, CB, D), F32), pltpu.VMEM((CB, D), F32)],
        name=f"rglru_bwd{d}", compiler_params=_cp(("arbitrary",)))(*args)
    return (outs[0].reshape(t_lat, D), outs[1]) + tuple(outs[2:])


def _merge(o_f, o_b, h_f, h_b, z, x_all, tgt, mod, norm_g, ln_g, ln_b, p_a, p_b, w_out, t_lat):
    tm = 256
    nt = t_lat // tm

    def body(of_ref, ob_ref, hf_ref, hb_ref, z4_ref, z6_ref, z7_ref, z8_ref, x_ref, t_ref, mod_ref, ng_ref,
             lg_ref, lb_ref, pa_ref, pb_ref, wo_ref,
             do_ref, dh_ref, dz4a_ref, dz4b_ref, dz6a_ref, sh3_ref, gx_ref,
             y_ref, dout_ref, oa_ref, dpa_ref, obv_ref, dpb_ref, acc_ref):
        i = pl.program_id(0)
        lat = i < nt
        latf = lat.astype(F32)

        @pl.when(i == 0)
        def _():
            acc_ref[...] = jnp.zeros_like(acc_ref)

        def per_head(v):
            return jnp.concatenate(
                [jnp.broadcast_to(jnp.mean(v[:, h * DH:(h + 1) * DH], axis=-1, keepdims=True), (tm, DH))
                 for h in range(H)], axis=1)

        gt = mod_ref[0:1, 2 * D:3 * D]
        gfull = jnp.concatenate([ng_ref[...]] * H, axis=1)
        o = of_ref[...] + ob_ref[...]
        rinv = lax.rsqrt(per_head(o * o) + RMS_EPS)
        n = o * rinv
        na = n * gfull
        z4 = z4_ref[...]
        s4 = _sig(z4)
        silu4 = z4 * s4
        oa = na * silu4
        z6 = z6_ref[...]
        s6 = _sig(z6)
        silu6 = z6 * s6
        hsum = hf_ref[...] + hb_ref[...]
        obv = hsum * silu6
        pa = _dot(oa, pa_ref[...])
        pb = _dot(obv, pb_ref[...])
        s7 = _sig(z7_ref[...])
        s8 = _sig(z8_ref[...])
        y = s7 * pa + s8 * pb
        out = _dot(y, wo_ref[...])
        pre = ALPHA * x_ref[...] + gt * out
        mu = jnp.mean(pre, axis=-1, keepdims=True)
        xc = pre - mu
        rstd = lax.rsqrt(jnp.mean(xc * xc, axis=-1, keepdims=True) + LN_EPS)
        xhat = xc * rstd
        lg = lg_ref[...]
        diff = xhat * lg + lb_ref[...] - t_ref[...]
        acc_ref[8:9, :] += _colsum(diff * diff) * (0.5 / D * latf)
        dxn = diff * (1.0 / D)
        acc_ref[1:2, :] += _colsum(dxn * xhat) * latf
        acc_ref[2:3, :] += _colsum(dxn) * latf
        dxhat = dxn * lg
        dpre = rstd * (dxhat - jnp.mean(dxhat, axis=-1, keepdims=True)
                       - xhat * jnp.mean(dxhat * xhat, axis=-1, keepdims=True))
        gx_ref[...] = ALPHA * dpre
        acc_ref[0:1, :] += _colsum(dpre * out) * latf
        dout = dpre * gt
        dy = _dot_nt(dout, wo_ref[...])
        dpa = dy * s7
        dpb = dy * s8
        dz7 = dy * pa * (s7 * (1.0 - s7))
        dz8 = dy * pb * (s8 * (1.0 - s8))
        doa = _dot_nt(dpa, pa_ref[...])
        dob = _dot_nt(dpb, pb_ref[...])
        dh_ref[...] = dob * silu6
        dz6 = dob * hsum * (s6 * (1.0 + z6 * (1.0 - s6)))
        dna = doa * silu4
        dz4 = doa * na * (s4 * (1.0 + z4 * (1.0 - s4)))
        dng = _colsum(dna * n)
        acc_ref[7:8, 0:DH] += sum(dng[:, h * DH:(h + 1) * DH] for h in range(H)) * latf
        dn = dna * gfull
        do_ref[...] = rinv * (dn - n * per_head(dn * n))
        acc_ref[3:4, :] += _colsum(dz4) * latf
        acc_ref[4:5, :] += _colsum(dz6) * latf
        acc_ref[5:6, :] += _colsum(dz7) * latf
        acc_ref[6:7, :] += _colsum(dz8) * latf
        dz4b, dz6b = (dz4 * latf).astype(BF), (dz6 * latf).astype(BF)
        dz4a_ref[...] = dz4b[:, :CUT4]
        dz4b_ref[...] = dz4b[:, CUT4:]
        dz6a_ref[...] = dz6b[:, :CUT6]
        sh3_ref[:, 0:D - CUT6] = dz6b[:, CUT6:]
        sh3_ref[:, D - CUT6:2 * D - CUT6] = (dz7 * latf).astype(BF)
        sh3_ref[:, 2 * D - CUT6:] = (dz8 * latf).astype(BF)
        y_ref[...] = y.astype(BF)
        dout_ref[...] = dout.astype(BF)
        oa_ref[...] = oa.astype(BF)
        dpa_ref[...] = dpa.astype(BF)
        obv_ref[...] = obv.astype(BF)
        dpb_ref[...] = dpb.astype(BF)

        @pl.when(i == nt - 1)
        def _():
            acc_ref[9:10, :] = jnp.broadcast_to(jnp.sum(acc_ref[8:9, :], axis=-1, keepdims=True), (1, D))

    m = x_all.shape[0]
    lrow = lambda i: jnp.minimum(i, nt - 1)
    row = pl.BlockSpec((tm, D), lambda i: (lrow(i), 0))
    allrow = lambda cols: pl.BlockSpec((tm, cols), lambda i: (i, 0))
    zs = lambda cb: pl.BlockSpec((tm, D), lambda i: (lrow(i), cb))
    full = lambda shp: pl.BlockSpec(shp, lambda i: (0,) * len(shp))
    wfull = pl.BlockSpec((D, D), lambda i: (0, 0), pipeline_mode=pl.Buffered(1))
    f32o = jax.ShapeDtypeStruct((t_lat, D), F32)
    bfo = jax.ShapeDtypeStruct((t_lat, D), BF)
    bfall = lambda cols: jax.ShapeDtypeStruct((m, cols), BF)
    return pl.pallas_call(
        body, grid=(m // tm,),
        in_specs=[row, row, row, row, zs(4), zs(6), zs(7), zs(8), row, row, full((16, 3 * D)), full((1, DH)),
                  full((1, D)), full((1, D)), wfull, wfull, wfull],
        out_specs=(row, row) + tuple(allrow(c) for c in (CUT4, D - CUT4, CUT6, SHC)) + (row,) * 7 + (full((16, D)),),
        out_shape=(f32o, f32o, bfall(CUT4), bfall(D - CUT4), bfall(CUT6), bfall(SHC), f32o, bfo, bfo, bfo, bfo, bfo, bfo,
                   jax.ShapeDtypeStruct((16, D), F32)),
        name="merge", compiler_params=_cp(("arbitrary",), VMEM_LIMIT_MERGE))(
            o_f, o_b, h_f, h_b, z, z, z, z, x_all, tgt, mod, norm_g, ln_g, ln_b, p_a, p_b, w_out)


def _wgrad(a, b, name):
    tm = 1024

    def body(a_ref, b_ref, o_ref):
        @pl.when(pl.program_id(0) == 0)
        def _():
            o_ref[...] = jnp.zeros_like(o_ref)
        o_ref[...] += _dot_tn(a_ref[...], b_ref[...])

    row = pl.BlockSpec((tm, D), lambda i: (i, 0))
    return pl.pallas_call(body, grid=(a.shape[0] // tm,), in_specs=[row, row],
                          out_specs=pl.BlockSpec((D, D), lambda i: (0, 0)),
                          out_shape=jax.ShapeDtypeStruct((D, D), F32), name=name,
                          compiler_params=_cp(("arbitrary",)))(a, b)


def _pack_shard2(dz4b, dz5_lat, dz5_ctx, dz6a):
    m, t_lat, t_ctx = dz4b.shape[0], dz5_lat.shape[0], dz5_ctx.shape[0]
    tm = t_ctx
    nlt = t_lat // tm
    w4 = D - CUT4

    def body(a_ref, bl_ref, bc_ref, c_ref, o_ref):
        i = pl.program_id(0)
        o_ref[:, 0:w4] = a_ref[...]
        o_ref[:, w4:w4 + D] = jnp.where(i < nlt, bl_ref[...], bc_ref[...]).astype(BF)
        o_ref[:, w4 + D:] = c_ref[...]

    return pl.pallas_call(
        body, grid=(m // tm,),
        in_specs=[pl.BlockSpec((tm, w4), lambda i: (i, 0)),
                  pl.BlockSpec((tm, D), lambda i: (jnp.minimum(i, nlt - 1), 0)),
                  pl.BlockSpec((tm, D), lambda i: (0, 0)),
                  pl.BlockSpec((tm, CUT6), lambda i: (i, 0))],
        out_specs=pl.BlockSpec((tm, SHC), lambda i: (i, 0)),
        out_shape=jax.ShapeDtypeStruct((m, SHC), BF), name="pack_shard2",
        compiler_params=_cp(("arbitrary",)))(dz4b, dz5_lat, dz5_ctx, dz6a)


def _wgrad_in(u_all, dz_shards, sums=()):
    m = u_all.shape[0]
    assert m % 128 == 0
    tm = m // 8
    nt = m // tm
    ns = len(sums)
    spec_u, spec_dz = pl.BlockSpec((tm, D), lambda i: (i, 0)), pl.BlockSpec((tm, SHC), lambda i: (i, 0))
    oshape = jax.ShapeDtypeStruct((NSH, D, SHC), F32)

    def first(u_ref, dz_ref, *rest):
        sum_refs, o_ref, got_refs, sems = rest[:ns], rest[ns], rest[ns + 1:2 * ns + 1], rest[2 * ns + 1:]
        i = pl.program_id(0)

        @pl.when(i == 0)
        def _():
            o_ref[...] = jnp.zeros_like(o_ref)
            for cp in _rs_chip_copies(sum_refs, got_refs, *sems):
                cp.start()
        o_ref[0] += _dot_tn(u_ref[...], dz_ref[...])

        @pl.when(i == nt - 1)
        def _():
            for cp in _rs_chip_copies(sum_refs, got_refs, *sems):
                cp.wait()

    outs = pl.pallas_call(
        first, grid=(nt,), in_specs=[spec_u, spec_dz] + [_ANY] * ns,
        out_specs=[pl.BlockSpec((1, D, SHC), lambda i: (0, 0, 0))] + [_ANY] * ns,
        out_shape=[oshape] + [jax.ShapeDtypeStruct((3,) + g.shape[1:], g.dtype) for g in sums],
        scratch_shapes=[pltpu.SemaphoreType.DMA((3 * ns,)), pltpu.SemaphoreType.DMA((3 * ns,))],
        name="wgrad_in0", compiler_params=_cp(("arbitrary",)))(u_all, dz_shards[0], *sums)
    out, got = outs[0], list(outs[1:])
    for k in range(1, NSH):
        def body(u_ref, dz_ref, _, o_ref):
            @pl.when(pl.program_id(0) == 0)
            def _():
                o_ref[...] = jnp.zeros_like(o_ref)
            o_ref[0] += _dot_tn(u_ref[...], dz_ref[...])

        out = pl.pallas_call(
            body, grid=(nt,), in_specs=[spec_u, spec_dz, _ANY],
            out_specs=pl.BlockSpec((1, D, SHC), lambda i, k=k: (k, 0, 0)), out_shape=oshape,
            input_output_aliases={2: 0}, name=f"wgrad_in{k}",
            compiler_params=_cp(("arbitrary",)))(u_all, dz_shards[k], out)
    return out, got


def _du(dz_shards, w_in_g, x_all, mod, gxres, n_lat_tiles, sums=()):
    m = x_all.shape[0]
    tm = 256
    nt = m // tm
    nct = nt - n_lat_tiles
    ns = len(sums)
    rblk = lambda i: jnp.where(i < nct, n_lat_tiles + i, i - nct)
    lblk = lambda i: jnp.maximum(i - nct, 0)

    def body(*refs):
        dz_refs, refs = refs[:NSH], refs[NSH - 1:]
        _, w_ref, x_ref, mod_ref, gr_ref = refs[:5]
        sum_refs = refs[5:5 + ns]
        gx_ref, dm_ref = refs[5 + ns:7 + ns]
        got_refs = refs[7 + ns:7 + 2 * ns]
        sems = refs[7 + 2 * ns:]
        i = pl.program_id(0)
        is_lat = i >= nct

        @pl.when(i == 0)
        def _():
            dm_ref[...] = jnp.zeros_like(dm_ref)
            if ns:
                for cp in _rs_chip_copies(sum_refs, got_refs, *sems):
                    cp.start()

        du = _dot_nt(dz_refs[0][...], w_ref[0])
        for n in range(1, NSH):
            du = du + _dot_nt(dz_refs[n][...], w_ref[n])
        sc = jnp.where(is_lat, mod_ref[0:1, D:2 * D], mod_ref[1:2, D:2 * D])
        dsh = _colsum(du)
        dsc = _colsum(du * x_ref[...])

        @pl.when(is_lat)
        def _():
            gx_ref[...] = du * (1.0 + sc) + gr_ref[...]
            dm_ref[0:1, 0:D] += dsh
            dm_ref[0:1, D:2 * D] += dsc

        @pl.when(jnp.logical_not(is_lat))
        def _():
            dm_ref[1:2, 0:D] += dsh
            dm_ref[1:2, D:2 * D] += dsc

        if ns:
            @pl.when(i == nt - 1)
            def _():
                for cp in _rs_chip_copies(sum_refs, got_refs, *sems):
                    cp.wait()

    outs = pl.pallas_call(
        body, grid=(nt,),
        in_specs=[pl.BlockSpec((tm, SHC), lambda i: (rblk(i), 0))] * NSH + [
                  pl.BlockSpec((NSH, D, SHC), lambda i: (0, 0, 0), pipeline_mode=pl.Buffered(1)),
                  pl.BlockSpec((tm, D), lambda i: (rblk(i), 0)),
                  pl.BlockSpec((16, 3 * D), lambda i: (0, 0)),
                  pl.BlockSpec((tm, D), lambda i: (lblk(i), 0))] + [_ANY] * ns,
        out_specs=[pl.BlockSpec((tm, D), lambda i: (lblk(i), 0)),
                   pl.BlockSpec((8, 2 * D), lambda i: (0, 0))] + [_ANY] * ns,
        out_shape=[jax.ShapeDtypeStruct((n_lat_tiles * tm, D), F32), jax.ShapeDtypeStruct((8, 2 * D), F32)]
        + [jax.ShapeDtypeStruct((3,) + g.shape[1:], g.dtype) for g in sums],
        scratch_shapes=[pltpu.SemaphoreType.DMA((3 * ns,)), pltpu.SemaphoreType.DMA((3 * ns,))] if ns else [],
        name="du", compiler_params=_cp(("arbitrary",)))(*dz_shards, w_in_g, x_all, mod, gxres, *sums)
    return outs[0], outs[1], list(outs[2:])


def _row_tile(rows, cols):
    t = 8
    while t * 2 * cols * 4 <= (1 << 20) and rows % (t * 2) == 0:
        t *= 2
    return t


def _adamw_update(w_ref, g_ref, m_ref, v_ref, d_ref, nm_ref, nv_ref):
    gg = g_ref[...]
    m2 = ADAM_B1 * m_ref[...] + (1.0 - ADAM_B1) * gg
    v2 = ADAM_B2 * v_ref[...] + (1.0 - ADAM_B2) * (gg * gg)
    m_hat = m2 / (1.0 - ADAM_B1 ** ADAM_STEP)
    v_hat = v2 / (1.0 - ADAM_B2 ** ADAM_STEP)
    d_ref[...] = -ADAM_LR * (m_hat / (jnp.sqrt(v_hat) + ADAM_EPS) + ADAM_WD * w_ref[...])
    nm_ref[...] = m2
    nv_ref[...] = v2


def _adamw_many(ws, gs, ms, vs):
    n = len(ws)

    def body(*refs):
        for j in range(n):
            _adamw_update(*refs[4 * j:4 * j + 4], *refs[4 * n + 3 * j:4 * n + 3 * j + 3])

    args = [a for quad in zip(ws, gs, ms, vs) for a in quad]
    outs = pl.pallas_call(body, out_shape=[jax.ShapeDtypeStruct(w.shape, F32) for w in ws for _ in range(3)],
                          name="adamw_small", compiler_params=_cp())(*args)
    return outs[0::3], outs[1::3], outs[2::3]


def _adamw(w, g, m, v, name):
    rows, cols = w.shape
    tr = _row_tile(rows, cols)

    def body(*refs):
        _adamw_update(*refs)

    spec = pl.BlockSpec((tr, cols), lambda i: (i, 0))
    o = jax.ShapeDtypeStruct((rows, cols), F32)
    return pl.pallas_call(body, grid=(rows // tr,), in_specs=[spec] * 4, out_specs=(spec,) * 3,
                          out_shape=(o, o, o), name=name, compiler_params=_cp(("arbitrary",)))(w, g, m, v)


_ANY = pl.BlockSpec(memory_space=pl.ANY)


def _place():
    return lax.axis_index("x"), lax.axis_index("y"), lax.axis_index("c")


def _ag_copies(ins, outs, send, recv, fsend, frecv, lsem):
    x, y, c = _place()
    me = 2 * x + y
    chips = ((1 - x, y), (x, 1 - y), (1 - x, 1 - y))
    local, chip, hand = [], [], []
    for j in range(len(ins)):
        hr = ins[j].shape[0] // 2
        half = pl.ds(pl.multiple_of(c * hr, 8), hr)
        local.append(pltpu.make_async_copy(ins[j], outs[j].at[me], lsem.at[j]))
        for k, (px, py) in enumerate(chips):
            chip.append(pltpu.make_async_remote_copy(
                src_ref=ins[j].at[half, :], dst_ref=outs[j].at[me, half, :], send_sem=send.at[3 * j + k],
                recv_sem=recv.at[3 * j + k], device_id=(px, py, c), device_id_type=MESH))
            got = outs[j].at[2 * px + py, half, :]
            hand.append(pltpu.make_async_remote_copy(
                src_ref=got, dst_ref=got, send_sem=fsend.at[3 * j + k], recv_sem=frecv.at[3 * j + k],
                device_id=(x, y, 1 - c), device_id_type=MESH))
    return local, chip, hand


def _ag_sems(n):
    return [pltpu.SemaphoreType.DMA((3 * n,))] * 4 + [pltpu.SemaphoreType.DMA((n,))]


def _ag_finish(local, chip, hand, done=0):
    for k in range(done, len(chip)):
        chip[k].wait_recv()
        hand[k].start()
    for cp in chip:
        cp.wait_send()
    for k in range(done):
        hand[k].wait_send()
    for k in range(done, len(chip)):
        hand[k].wait_send()
        hand[k].wait_recv()
    for cp in local:
        cp.wait()


def _mod_tp(c8, c_ctx, w_mod_sh, b_mod_sh):
    mc = w_mod_sh.shape[1]

    def body(c8_ref, cctx_ref, w_ref, b_ref, mod_ref, cc_ref, cc_s, part_s, send1, recv1, send3, recv3):
        x, y, c = _place()
        me = 4 * x + 2 * y + c
        ms = 2 * x + y
        copies = []
        for k in range(1, 8):
            peer = (x ^ ((k >> 2) & 1), y ^ ((k >> 1) & 1), c ^ (k & 1))
            cp = pltpu.make_async_remote_copy(src_ref=c8_ref, dst_ref=cc_s.at[me], send_sem=send1.at[k],
                                              recv_sem=recv1.at[k], device_id=peer, device_id_type=MESH)
            cp.start()
            copies.append(cp)
        cc_s[me] = c8_ref[...]
        for cp in copies:
            cp.wait()
        cc_ref[...] = jnp.zeros_like(cc_ref)
        for j in range(8):
            cc_ref[j:j + 1, :] = cc_s[j, 0:1, :]
        cc_ref[8:9, :] = cctx_ref[...]
        v = cc_ref[...]
        part_s[ms] = _dot(v * _sig(v), w_ref[...]) + b_ref[...]
        copies = []
        for k in range(1, 4):
            peer = (x ^ ((k >> 1) & 1), y ^ (k & 1), c)
            cp = pltpu.make_async_remote_copy(src_ref=part_s.at[ms], dst_ref=part_s.at[ms], send_sem=send3.at[k],
                                              recv_sem=recv3.at[k], device_id=peer, device_id_type=MESH)
            cp.start()
            copies.append(cp)
        for cp in copies:
            cp.wait()
        for s in range(NSH):
            mod_ref[:, s * mc:(s + 1) * mc] = part_s[s]

    vm = pl.BlockSpec(memory_space=pltpu.VMEM)
    return pl.pallas_call(
        body, in_specs=[vm] * 4, out_specs=(vm, vm),
        out_shape=(jax.ShapeDtypeStruct((16, NSH * mc), F32), jax.ShapeDtypeStruct((16, D), F32)),
        scratch_shapes=[pltpu.VMEM((8, 8, D), F32), pltpu.VMEM((NSH, 16, mc), F32),
                        pltpu.SemaphoreType.DMA((8,)), pltpu.SemaphoreType.DMA((8,)),
                        pltpu.SemaphoreType.DMA((4,)), pltpu.SemaphoreType.DMA((4,))],
        name="mod_tp", compiler_params=_cp())(c8, c_ctx, w_mod_sh, b_mod_sh)


def _inproj_ag(x_all, mod, w_in_sh, b_in, narrow_sh, n_lat_tiles):
    m = x_all.shape[0]
    assert m % (11 * 16) == 0
    tm = m // 11
    nt = m // tm
    n_lat = n_lat_tiles * 256
    x_, y_ = lax.axis_index("x"), lax.axis_index("y")
    sids = jnp.stack([2 * x_ + y_, 2 * (1 - x_) + y_, 2 * x_ + 1 - y_, 2 * (1 - x_) + 1 - y_]).astype(jnp.int32)

    def body(sid_ref, x_ref, mod_ref, b_ref, wsh_ref, nsh_ref, z_ref, u_ref, wg_ref, ng_ref, w_s, u_s, *sems):
        n = pl.program_id(0)
        i = pl.program_id(1)
        rows = pl.ds(pl.multiple_of(i * tm, tm), tm)
        ag = ((wsh_ref, nsh_ref), (wg_ref, ng_ref)) + tuple(sems[:5])
        wsem = sems[5]

        def load(src):
            cp = pltpu.make_async_copy(src, w_s, wsem)
            cp.start()
            cp.wait()

        @pl.when((n == 0) & (i == 0))
        def _():
            local, chip, _ = _ag_copies(*ag)
            for cp in chip + local:
                cp.start()
            load(wsh_ref)

        for k in range(NSH - 1):
            @pl.when((n == k + 1) & (i == 0))
            def _():
                _, chip, hand = _ag_copies(*ag)
                chip[k].wait_recv()
                hand[k].start()
                hand[k].wait_recv()
                load(wg_ref.at[sid_ref[k + 1]])

        @pl.when(n == 0)
        def _():
            is_lat = (i * tm + lax.broadcasted_iota(jnp.int32, (tm, 1), 0)) < n_lat
            u, _ = _modulate(x_ref, mod_ref, is_lat)
            u_s[rows, :] = u.astype(BF)
            u_ref[...] = u.astype(BF)

        z_ref[...] = _dot(u_s[rows, :], w_s[...]) + b_ref[...]

        @pl.when((n == NSH - 1) & (i == nt - 1))
        def _():
            _ag_finish(*_ag_copies(*ag), done=NSH - 1)

    first = lambda n, i: jnp.where(n == 0, i, nt - 1)
    outs = pl.pallas_call(
        body, grid_spec=pltpu.PrefetchScalarGridSpec(
            num_scalar_prefetch=1, grid=(NSH, nt),
            in_specs=[pl.BlockSpec((tm, D), lambda n, i, sid: (first(n, i), 0)),
                      pl.BlockSpec((16, 3 * D), lambda n, i, sid: (0, 0)),
                      pl.BlockSpec((1, SHC), lambda n, i, sid: (0, sid[n])), _ANY, _ANY],
            out_specs=[pl.BlockSpec((tm, SHC), lambda n, i, sid: (i, sid[n])),
                       pl.BlockSpec((tm, D), lambda n, i, sid: (first(n, i), 0)), _ANY, _ANY],
            scratch_shapes=[pltpu.VMEM((D, SHC), BF), pltpu.VMEM((m, D), BF)] + _ag_sems(2)
            + [pltpu.SemaphoreType.DMA]),
        out_shape=[jax.ShapeDtypeStruct((m, IN_COLS), F32), jax.ShapeDtypeStruct((m, D), BF),
                   jax.ShapeDtypeStruct((NSH,) + w_in_sh.shape, BF),
                   jax.ShapeDtypeStruct((NSH,) + narrow_sh.shape, narrow_sh.dtype)],
        name="inproj_ag", compiler_params=_cp(("arbitrary", "arbitrary")))(sids, x_all, mod, b_in, w_in_sh, narrow_sh)
    return outs


def _rs_sibling(grads, name):
    n = len(grads)

    def body(*refs):
        ins, got = refs[:n], refs[n:2 * n]
        send, recv = refs[2 * n:]
        x, y, c = _place()
        copies = []
        for j in range(n):
            hr = ins[j].shape[1] // 2
            for s in range(NSH):
                give = ins[j].at[s, pl.ds(pl.multiple_of((1 - c) * hr, 8), hr), :]
                cp = pltpu.make_async_remote_copy(src_ref=give, dst_ref=got[j].at[s], send_sem=send.at[NSH * j + s],
                                                  recv_sem=recv.at[NSH * j + s], device_id=(x, y, 1 - c),
                                                  device_id_type=MESH)
                cp.start()
                copies.append(cp)
        for cp in copies:
            cp.wait()

    half = [jax.ShapeDtypeStruct((NSH, g.shape[1] // 2, g.shape[2]), F32) for g in grads]
    return pl.pallas_call(
        body, in_specs=[_ANY] * n, out_specs=[_ANY] * n, out_shape=half,
        scratch_shapes=[pltpu.SemaphoreType.DMA((NSH * n,)), pltpu.SemaphoreType.DMA((NSH * n,))],
        name=name)(*grads)


def _core_vec():
    return lax.axis_index("c").astype(jnp.int32).reshape(1)


def _rs_add1(g, got, name):
    _, r, cols = g.shape
    hr = r // 2
    tr = _row_tile(hr, cols)
    nb = hr // tr

    def body(c_ref, g_ref, got_ref, o_ref):
        o_ref[...] = (g_ref[...] + got_ref[...]).astype(BF)

    spec = pl.BlockSpec((1, tr, cols), lambda s, i, c_ref: (s, i, 0))
    return pl.pallas_call(
        body, grid_spec=pltpu.PrefetchScalarGridSpec(
            num_scalar_prefetch=1, grid=(NSH, nb),
            in_specs=[pl.BlockSpec((1, tr, cols), lambda s, i, c_ref: (s, c_ref[0] * nb + i, 0)), spec],
            out_specs=spec),
        out_shape=jax.ShapeDtypeStruct((NSH, hr, cols), BF), name=name,
        compiler_params=_cp(("arbitrary", "arbitrary")))(_core_vec(), g, got)


def _rs_add2(sums, got, name):
    _, hr, cols = sums.shape
    tr = _row_tile(hr, cols)
    nb = hr // tr
    place = jnp.stack([2 * lax.axis_index("x") + lax.axis_index("y"), lax.axis_index("c")]).astype(jnp.int32)

    def body(p_ref, s_ref, got_ref, o_ref):
        f = lambda v: v.astype(F32)
        o_ref[...] = f(s_ref[0]) + f(got_ref[0]) + f(got_ref[1]) + f(got_ref[2])

    return pl.pallas_call(
        body, grid_spec=pltpu.PrefetchScalarGridSpec(
            num_scalar_prefetch=1, grid=(nb,),
            in_specs=[pl.BlockSpec((1, tr, cols), lambda i, p_ref: (p_ref[0], i, 0)),
                      pl.BlockSpec((3, tr, cols), lambda i, p_ref: (0, i, 0))],
            out_specs=pl.BlockSpec((tr, cols), lambda i, p_ref: (p_ref[1] * nb + i, 0))),
        out_shape=jax.ShapeDtypeStruct((2 * hr, cols), F32), name=name,
        compiler_params=_cp(("arbitrary",)))(place, sums, got)


def _rs_chip_copies(ins, got, send, recv):
    x, y, c = _place()
    peers = ((1 - x, y), (x, 1 - y), (1 - x, 1 - y))
    return [pltpu.make_async_remote_copy(src_ref=ins[j].at[2 * px + py], dst_ref=got[j].at[k],
                                         send_sem=send.at[3 * j + k], recv_sem=recv.at[3 * j + k],
                                         device_id=(px, py, c), device_id_type=MESH)
            for j in range(len(ins)) for k, (px, py) in enumerate(peers)]


def _ag_sibling(fulls):
    n = len(fulls)
    nck = 4

    def body(*refs):
        outs = refs[n:2 * n]
        send, recv = refs[2 * n:]
        x, y, c = _place()
        copies = []
        for j in range(n):
            qr = outs[j].shape[0] // (2 * nck)
            for k in range(nck):
                rows = outs[j].at[pl.ds(pl.multiple_of((c * nck + k) * qr, 8), qr), :]
                cp = pltpu.make_async_remote_copy(src_ref=rows, dst_ref=rows, send_sem=send.at[nck * j + k],
                                                  recv_sem=recv.at[nck * j + k], device_id=(x, y, 1 - c),
                                                  device_id_type=MESH)
                cp.start()
                copies.append(cp)
        for cp in copies:
            cp.wait()

    return pl.pallas_call(
        body, in_specs=[_ANY] * n, out_specs=[_ANY] * n,
        out_shape=[jax.ShapeDtypeStruct(f.shape, F32) for f in fulls],
        input_output_aliases={j: j for j in range(n)},
        scratch_shapes=[pltpu.SemaphoreType.DMA((nck * n,)), pltpu.SemaphoreType.DMA((nck * n,))],
        name="ag_sibling")(*fulls)


def _allreduce_small(buf):
    rows = buf.shape[0]
    pr = rows // 8

    def body(in_ref, out_ref, stage, send1, recv1, send2, recv2):
        x, y, c = _place()
        me = 4 * x + 2 * y + c

        def peer(k):
            kx, ky, kc = (k >> 2) & 1, (k >> 1) & 1, k & 1
            return (x ^ kx, y ^ ky, c ^ kc)

        def piece(ref, idx):
            return ref.at[pl.ds(pl.multiple_of(idx * pr, 8), pr), :]

        copies = []
        for k in range(1, 8):
            px, py, pc = peer(k)
            cp = pltpu.make_async_remote_copy(src_ref=piece(in_ref, 4 * px + 2 * py + pc), dst_ref=stage.at[k],
                                              send_sem=send1.at[k], recv_sem=recv1.at[k],
                                              device_id=(px, py, pc), device_id_type=MESH)
            cp.start()
            copies.append(cp)
        for cp in copies:
            cp.wait()
        acc = piece(in_ref, me)[...]
        for k in range(1, 8):
            acc = acc + stage[k]
        piece(out_ref, me)[...] = acc
        copies = []
        for k in range(1, 8):
            cp = pltpu.make_async_remote_copy(src_ref=piece(out_ref, me), dst_ref=piece(out_ref, me),
                                              send_sem=send2.at[k], recv_sem=recv2.at[k],
                                              device_id=peer(k), device_id_type=MESH)
            cp.start()
            copies.append(cp)
        for cp in copies:
            cp.wait()

    vm = pl.BlockSpec(memory_space=pltpu.VMEM)
    return pl.pallas_call(
        body, in_specs=[vm], out_specs=vm, out_shape=jax.ShapeDtypeStruct((rows, D), F32),
        scratch_shapes=[pltpu.VMEM((8, pr, D), F32)] + [pltpu.SemaphoreType.DMA((8,))] * 4,
        name="allreduce_small", compiler_params=_cp())(buf)


def _rows(a):
    flat = a.reshape(-1)
    pad = (-flat.shape[0]) % D
    if pad:
        flat = jnp.concatenate([flat, jnp.zeros((pad,), flat.dtype)])
    return flat.reshape(-1, D)


def _pad_rows(a, mult):
    pad = (-a.shape[0]) % mult
    return jnp.concatenate([a, jnp.zeros((pad, a.shape[1]), a.dtype)]) if pad else a


def _local_step(x, c, ctx, c_ctx, tgt, me, shard, sh, b_mod, b_in, norm_g, cb, wr, wi, ln_g, ln_b):
    t_lat, t_ctx = x.shape[0], ctx.shape[0]
    nlt = t_lat // 256
    nlb, ncb = t_lat // RB, t_ctx // RB
    mc = 3 * D // NSH
    mod_all, cc_all = _mod_tp(jnp.zeros((8, D), F32).at[0].set(c), c_ctx.reshape(1, D), sh["w_mod"],
                              lax.dynamic_slice_in_dim(b_mod, shard * mc, mc, axis=1))
    mod = jnp.zeros((16, 3 * D), F32).at[0].set(mod_all[me]).at[1].set(mod_all[8])
    cc = jnp.zeros((16, D), F32).at[0].set(c).at[1].set(c_ctx)
    x_all = jnp.concatenate([x, ctx], axis=0)
    z, u_all, w_in_g, nar = _inproj_ag(x_all, mod, sh["w_in"], b_in, sh["narrow"], nlt)
    nar = jnp.transpose(nar, (1, 0, 2)).reshape(-1, D)
    lbl, cw, br, bi, lam = nar[0:4].reshape(2, 2, D), nar[4:8], nar[8:10], nar[10:12], nar[12:14]
    o0, st0, (w_mod_g, p_a, p_b, w_out) = _gla_fwd(z, lbl, 0, nlb, ncb,
                                                   gather=[sh[k] for k in ("w_mod", "p_a", "p_b", "w_out")])
    p_a, p_b, w_out = p_a.reshape(D, D), p_b.reshape(D, D), w_out.reshape(D, D)
    o1, st1, _ = _gla_fwd(z, lbl, 1, nlb, ncb)
    h0, hp0, a0, hpc0 = _rglru_fwd(z, cw, cb, wr, br, wi, bi, lam, 0, t_lat, t_ctx)
    h1, hp1, a1, hpc1 = _rglru_fwd(z, cw, cb, wr, br, wi, bi, lam, 1, t_lat, t_ctx)
    (do, dh, dz4a, dz4b, dz6a, dz_sh3, gxres, y, dout, oa, dpa, obv, dpb, acc) = _merge(
        o0, o1, h0, h1, z, x_all, tgt, mod, norm_g, ln_g, ln_b, p_a, p_b, w_out, t_lat)
    gp_a = _wgrad(oa, dpa, "wgrad_pa")
    gp_b = _wgrad(obv, dpb, "wgrad_pb")
    gw_out = _wgrad(y, dout, "wgrad_wout")
    dxc_lat, dxc_ctx, dwr0, dwi0, sb0 = _rglru_bwd(z, cw, cb, wr, br, wi, bi, lam, dh, hp0, a0, hpc0, 0, t_lat, t_ctx)
    dz5_lat, dz5_ctx, dwr1, dwi1, sb1 = _rglru_bwd(z, cw, cb, wr, br, wi, bi, lam, dh, hp1, a1, hpc1, 1, t_lat, t_ctx,
                                                   prev=(dxc_lat, dxc_ctx))
    dq0, dv0, dz1, sa0 = _gla_bwd(z, lbl, do, st0, 0, nlb, ncb)
    dz_sh0, dz_sh1, sa1 = _gla_bwd(z, lbl, do, st1, 1, nlb, ncb, prev=(dq0, dv0, dz1, dz4a))
    dz = (dz_sh0, dz_sh1, _pack_shard2(dz4b, dz5_lat, dz5_ctx, dz6a), dz_sh3)
    g_p = [g.reshape(NSH, D // NSH, D) for g in (gp_a, gp_b, gw_out)]
    sums_p = [_rs_add1(g, b, f"rs_add1_{j + 1}") for j, (g, b) in enumerate(zip(g_p, _rs_sibling(g_p, "rs_sibling_p")))]
    gw_in, got_p = _wgrad_in(u_all, dz, sums_p)
    sums_w = [_rs_add1(gw_in, _rs_sibling([gw_in], "rs_sibling_w")[0], "rs_add1_0")]
    gx, dm, got_w = _du(dz, w_in_g, x_all, mod, gxres, nlt, sums_w)
    fulls = [_rs_add2(a, b, f"rs_add2_{j}") for j, (a, b) in enumerate(zip(sums_w + sums_p, got_w + got_p))]
    big = dict(zip(_RS, _ag_sibling(fulls)))
    dmod = jnp.zeros((16, 3 * D), F32).at[0:2, 0:2 * D].set(dm[0:2]).at[0, 2 * D:].set(acc[0])
    dcc = _mod_bwd(cc, dmod, w_mod_g)
    small = dict(
        c_ctx=dcc[1:2], b_mod=(dmod[0] + dmod[1]).reshape(3, D),
        b_in=jnp.stack([sa1[2], sa0[0], sa1[0], sa1[3], acc[3], sb1[4], acc[4], acc[5], acc[6]]),
        lb_logits=jnp.stack([sa0[1], sa1[1], -sa0[1], -sa1[1]]),
        norm_a_g=acc[7:8], conv_w=sb1[8:12], conv_b=sb1[3:4],
        w_r=jnp.stack([dwr0, dwr1]).reshape(-1, D), w_i=jnp.stack([dwi0, dwi1]).reshape(-1, D),
        b_r=jnp.stack([sb0[0], sb1[0]]), b_i=jnp.stack([sb0[1], sb1[1]]), lam=jnp.stack([sb0[2], sb1[2]]),
        ln_g=acc[1:2], ln_b=acc[2:3])
    return acc[9, 0], gx, big, small, dmod, cc_all


_RS =("w_in", "p_a", "p_b", "w_out")
_SMALL =("c_ctx", "b_mod", "b_in", "lb_logits", "norm_a_g", "conv_w", "conv_b", "w_r", "w_i", "b_r", "b_i", "lam",
          "ln_g", "ln_b")
_BIG = ("w_mod", "w_in", "p_a", "p_b", "w_out")
_COL_SHARDED = ("lb_logits", "conv_w", "b_r", "b_i", "lam")
_WEIGHTS = ("c_ctx", "w_mod", "b_mod", "w_in", "b_in", "lb_logits", "norm_a_g", "conv_w", "conv_b", "w_r", "b_r", "w_i",
            "b_i", "lam", "p_a", "p_b", "w_out", "ln_g", "ln_b")


def kernel(x, c, ctx, c_ctx, w_mod, b_mod, w_in, b_in, lb_logits, norm_a_g, conv_w, conv_b, w_r, b_r, w_i, b_i, lam, p_a, p_b, w_out, ln_g, ln_b, loss_target, m_c_ctx, m_w_mod, m_b_mod, m_w_in, m_b_in, m_lb_logits, m_norm_a_g, m_conv_w, m_conv_b, m_w_r, m_b_r, m_w_i, m_b_i, m_lam, m_p_a, m_p_b, m_w_out, m_ln_g, m_ln_b, v_c_ctx, v_w_mod, v_b_mod, v_w_in, v_b_in, v_lb_logits, v_norm_a_g, v_conv_w, v_conv_b, v_w_r, v_b_r, v_w_i, v_b_i, v_lam, v_p_a, v_p_b, v_w_out, v_ln_g, v_ln_b):
    w = dict(c_ctx=c_ctx, w_mod=w_mod, b_mod=b_mod, w_in=w_in, b_in=b_in, lb_logits=lb_logits, norm_a_g=norm_a_g,
             conv_w=conv_w, conv_b=conv_b, w_r=w_r, b_r=b_r, w_i=w_i, b_i=b_i, lam=lam, p_a=p_a, p_b=p_b, w_out=w_out,
             ln_g=ln_g, ln_b=ln_b)
    m = dict(c_ctx=m_c_ctx, w_mod=m_w_mod, b_mod=m_b_mod, w_in=m_w_in, b_in=m_b_in, lb_logits=m_lb_logits,
             norm_a_g=m_norm_a_g, conv_w=m_conv_w, conv_b=m_conv_b, w_r=m_w_r, b_r=m_b_r, w_i=m_w_i, b_i=m_b_i,
             lam=m_lam, p_a=m_p_a, p_b=m_p_b, w_out=m_w_out, ln_g=m_ln_g, ln_b=m_ln_b)
    v = dict(c_ctx=v_c_ctx, w_mod=v_w_mod, b_mod=v_b_mod, w_in=v_w_in, b_in=v_b_in, lb_logits=v_lb_logits,
             norm_a_g=v_norm_a_g, conv_w=v_conv_w, conv_b=v_conv_b, w_r=v_w_r, b_r=v_b_r, w_i=v_w_i, b_i=v_b_i,
             lam=v_lam, p_a=v_p_a, p_b=v_p_b, w_out=v_w_out, ln_g=v_ln_g, ln_b=v_ln_b)
    shard = 2 * lax.axis_index("x") + lax.axis_index("y")
    cs = D // NSH

    sh = {k: w[k][0].astype(BF) for k in _BIG}
    sh["narrow"] = _pad_rows(jnp.concatenate([lb_logits.reshape(4, cs), conv_w[0], b_r[0], b_i[0], lam[0]], axis=0), 8)
    me = 2 * shard + lax.axis_index("c")
    loss, gx, big, small, dmod, cc_all = _local_step(
        x[0], c[0], ctx[0], c_ctx, loss_target[0], me, shard, sh, b_mod, b_in, norm_a_g, conv_b, w_r[0], w_i[0],
        ln_g, ln_b)
    loss = lax.psum(loss, ("x", "y", "c"))

    dmod_rows = jnp.zeros((16, 3 * D), F32).at[me].set(dmod[0]).at[8].set(dmod[1]).reshape(48, D)
    sizes = [small[k].shape[0] for k in _SMALL]
    red = _allreduce_small(_pad_rows(jnp.concatenate([_pad_rows(small[k], 8) for k in _SMALL] + [dmod_rows],
                                                     axis=0), 64))
    grads = {}
    off = 0
    for k, n in zip(_SMALL, sizes):
        g = red[off:off + n]
        off += n + (-n) % 8
        if k == "norm_a_g":
            g = g[:, :DH]
        if k in _COL_SHARDED:
            g = lax.dynamic_slice_in_dim(g, shard * cs, cs, axis=1)
        grads[k] = g.reshape(w[k].shape)
    for k in _RS:
        grads[k] = big[k].reshape(w[k].shape)
    dmod_all = red[off:off + 48].reshape(16, 3 * D)
    mc = 3 * D // NSH
    grads["w_mod"] = _wmod_grad(cc_all, lax.dynamic_slice_in_dim(dmod_all, shard * mc, mc, axis=1)).reshape(
        w["w_mod"].shape)

    delta, new_m, new_v = {}, {}, {}
    for k in _BIG:
        shp = w[k].shape
        two = lambda a: a.reshape(shp[-2], shp[-1])
        d_, m_, v_ = _adamw(two(w[k]), two(grads[k]), two(m[k]), two(v[k]), f"adamw_{k}")
        delta[k], new_m[k], new_v[k] = d_.reshape(shp), m_.reshape(shp), v_.reshape(shp)
    d_, m_, v_ = _adamw_many(*[[t[k] for k in _SMALL] for t in (w, grads, m, v)])
    delta.update(zip(_SMALL, d_))
    new_m.update(zip(_SMALL, m_))
    new_v.update(zip(_SMALL, v_))

    return (loss, gx[None], *[grads[k] for k in _WEIGHTS], *[delta[k] for k in _WEIGHTS],
            *[new_m[k] for k in _WEIGHTS], *[new_v[k] for k in _WEIGHTS])
```

```python
import functools

import jax
import jax.numpy as jnp
from jax import lax
from jax.experimental import pallas as pl
from jax.experimental.pallas import tpu as pltpu

F32 = jnp.float32
BF = jnp.bfloat16
MESH = pl.DeviceIdType.MESH

D = 1024
H = 8
DH = 128
CH = 64
RB = 256
NCK = RB // CH
GRID_W = 64
CB = 8
RCH = 16
IN_COLS = 9 * D
NSH = 4
SHC = IN_COLS // NSH
CUT2 = SHC - 2 * D
CUT4 = 2 * SHC - 4 * D
CUT6 = 3 * SHC - 6 * D
RG_C = 8.0
ALPHA = 2.0 ** 0.25
LN_EPS = 1e-5
RMS_EPS = 1e-6
Q_SCALE = DH ** -0.5
ADAM_LR, ADAM_B1, ADAM_B2, ADAM_EPS, ADAM_WD, ADAM_STEP = 0.001, 0.9, 0.999, 1e-08, 0.01, 10
VMEM_LIMIT = 56 * 1024 * 1024
VMEM_LIMIT_MERGE = 60 * 1024 * 1024
VMEM_LIMIT_RGLRU_BWD = 63 * 1024 * 1024


def _cp(sem=None, vmem=VMEM_LIMIT):
    return pltpu.CompilerParams(dimension_semantics=sem, vmem_limit_bytes=vmem)


def _sig(x):
    return 0.5 * jnp.tanh(0.5 * x) + 0.5


def _dot(a, b):
    return jnp.dot(a.astype(BF), b.astype(BF), preferred_element_type=F32)


def _dot_nt(a, b):
    return lax.dot_general(a.astype(BF), b.astype(BF), (((1,), (1,)), ((), ())), preferred_element_type=F32)


def _dot_tn(a, b):
    return lax.dot_general(a.astype(BF), b.astype(BF), (((0,), (0,)), ((), ())), preferred_element_type=F32)


def _colsum(v):
    return jnp.sum(v, axis=0, keepdims=True)


def _mod_bwd(cc, dmod, w_mod_g):
    def body(cc_ref, dm_ref, w_ref, dcc_ref):
        v = cc_ref[...]
        sg = _sig(v)
        ds = jnp.zeros((16, D), F32)
        for k in range(NSH):
            ds = ds + _dot_nt(dm_ref[:, k * 768:(k + 1) * 768], w_ref[k])
        dcc_ref[...] = ds * (sg * (1.0 + v * (1.0 - sg)))
    return pl.pallas_call(body, out_shape=jax.ShapeDtypeStruct((16, D), F32),
                          name="mod_bwd", compiler_params=_cp())(cc, dmod, w_mod_g)


def _wmod_grad(cc, dmod_cols):
    def body(cc_ref, dm_ref, dw_ref):
        v = cc_ref[...]
        dw_ref[...] = _dot_tn(v * _sig(v), dm_ref[...])
    return pl.pallas_call(body, out_shape=jax.ShapeDtypeStruct((D, dmod_cols.shape[1]), F32),
                          name="wmod_grad", compiler_params=_cp())(cc, dmod_cols)


def _modulate(x_ref, mod_ref, is_lat):
    sh = jnp.where(is_lat, mod_ref[0:1, 0:D], mod_ref[1:2, 0:D])
    sc = jnp.where(is_lat, mod_ref[0:1, D:2 * D], mod_ref[1:2, D:2 * D])
    return x_ref[...] * (1.0 + sc) + sh, sc


def _gla_mask(d, n):
    row = lax.broadcasted_iota(jnp.int32, (n, n), 0)
    col = lax.broadcasted_iota(jnp.int32, (n, n), 1)
    same = (row // CH) == (col // CH)
    return same & ((row <= col) if d else (row >= col))


def _chunk_cumsum(v, rev):
    n = v.shape[0]
    pos = lax.broadcasted_iota(jnp.int32, v.shape, 0) & (CH - 1)
    s = 1
    while s < CH:
        if rev:
            v = v + jnp.where(pos < CH - s, pltpu.roll(v, n - s, 0), 0.0)
        else:
            v = v + jnp.where(pos >= s, pltpu.roll(v, s, 0), 0.0)
        s *= 2
    return v


def _chunk_rows(c):
    return slice(c * CH, (c + 1) * CH)


def _gla_features(zq, zf, lb, d):
    sq = _sig(zq)
    q = zq * sq * Q_SCALE
    sf = _sig(zf)
    f = lb + (1.0 - lb) * sf
    k = 1.0 - f
    g = _chunk_cumsum(jnp.log(f), d)
    last = 0 if d else CH - 1
    gls = [g[c * CH + last:c * CH + last + 1, :] for c in range(NCK)]
    glb = jnp.concatenate([jnp.broadcast_to(gl, (CH, D)) for gl in gls], axis=0)
    eg, eig, eeg = jnp.exp(g), jnp.exp(-g), jnp.exp(glb - g)
    decs = [jnp.exp(gl) for gl in gls]
    return sq, sf, f, q * eg, k * eig, k * eeg, eg, eig, eeg, decs


def _lower_bound(lbl_ref, d):
    return _sig(lbl_ref[0, d:d + 1, :] - lbl_ref[1, d:d + 1, :])


def _gla_rb(d, nlb, ncb):
    nrb = nlb + ncb
    if d:
        return lambda s: nrb - 1 - s
    return lambda s: jnp.where(s < ncb, nlb + s, s - ncb)


def _gla_fwd(z, lbl, d, nlb, ncb, gather=()):
    m = z.shape[0]
    nrb = nlb + ncb
    rb = _gla_rb(d, nlb, ncb)
    ng = len(gather)

    def body(*refs):
        q_ref, f_ref, v_ref, lbl_ref = refs[:4]
        ag_in = refs[4:4 + ng]
        o_ref, st_ref = refs[4 + ng:6 + ng]
        ag_out = refs[6 + ng:6 + 2 * ng]
        S = refs[6 + 2 * ng]
        ag_sems = refs[7 + 2 * ng:]
        s = pl.program_id(0)

        @pl.when(s == 0)
        def _():
            S[...] = jnp.zeros_like(S)
            if ng:
                local, chip, _ = _ag_copies(ag_in, ag_out, *ag_sems)
                for cp in chip + local:
                    cp.start()

        lb = _lower_bound(lbl_ref, d)
        mb = _gla_mask(d, CH)
        _, _, _, qd, ki, ke, _, _, _, decs = _gla_features(q_ref[...], f_ref[...], lb, d)
        qd, ki, ke, v = qd.astype(BF), ki.astype(BF), ke.astype(BF), v_ref[...].astype(BF)
        order = range(NCK - 1, -1, -1) if d else range(NCK)
        for h in range(H):
            hs = slice(h * DH, (h + 1) * DH)
            intra, upd = {}, {}
            for c in range(NCK):
                rs = _chunk_rows(c)
                a = jnp.where(mb, _dot_nt(qd[rs, hs], ki[rs, hs]), 0.0)
                intra[c] = _dot(a, v[rs, hs])
                upd[c] = _dot_tn(v[rs, hs], ke[rs, hs])
            st = S[h]
            for c in order:
                rs = _chunk_rows(c)
                st_ref[c, h] = st
                o_ref[rs, hs] = intra[c] + _dot_nt(qd[rs, hs], st)
                st = st * decs[c][:, hs] + upd[c]
            S[h] = st

        if ng:
            @pl.when(s == nrb - 1)
            def _():
                _ag_finish(*_ag_copies(ag_in, ag_out, *ag_sems))

    def zspec(cb):
        return pl.BlockSpec((RB, D), lambda s: (rb(s), cb))

    outs = pl.pallas_call(
        body, grid=(nrb,),
        in_specs=[zspec(0), zspec(1 + d), zspec(3), pl.BlockSpec((2, 2, D), lambda s: (0, 0, 0))] + [_ANY] * ng,
        out_specs=[pl.BlockSpec((RB, D), lambda s: (rb(s), 0)),
                   pl.BlockSpec((NCK, H, DH, DH), lambda s: (rb(s), 0, 0, 0))] + [_ANY] * ng,
        out_shape=[jax.ShapeDtypeStruct((m, D), F32), jax.ShapeDtypeStruct((m // CH, H, DH, DH), F32)]
        + [jax.ShapeDtypeStruct((NSH,) + g.shape, g.dtype) for g in gather],
        scratch_shapes=[pltpu.VMEM((H, DH, DH), F32)] + (_ag_sems(ng) if ng else []),
        name=f"gla_fwd{d}", compiler_params=_cp(("arbitrary",)))(z, z, z, lbl, *gather)
    return outs[0], outs[1], list(outs[2:])


def _gla_bwd(z, lbl, do_lat, states, d, nlb, ncb, prev=None):
    m = z.shape[0]
    nrb = nlb + ncb
    fwd_rb = _gla_rb(d, nlb, ncb)
    rb = lambda s: fwd_rb(nrb - 1 - s)
    last = prev is not None

    def body(*refs):
        if last:
            (q_ref, f_ref, v_ref, lbl_ref, do_ref, st_ref, pq_ref, pv_ref, dz1_ref, dz4a_ref, sh0_ref, sh1_ref, sum_ref,
             dS) = refs
        else:
            q_ref, f_ref, v_ref, lbl_ref, do_ref, st_ref, o0_ref, o1_ref, o2_ref, sum_ref, dS = refs
        s = pl.program_id(0)
        is_lat = rb(s) < nlb

        @pl.when(s == 0)
        def _():
            dS[...] = jnp.zeros_like(dS)
            sum_ref[...] = jnp.zeros_like(sum_ref)

        lb = _lower_bound(lbl_ref, d)
        mb = _gla_mask(d, RB)
        zq = q_ref[...]
        sq, sf, f, qd, ki, ke, eg, eig, eeg, decs = _gla_features(zq, f_ref[...], lb, d)
        qdb, kib, keb, vb = qd.astype(BF), ki.astype(BF), ke.astype(BF), v_ref[...].astype(BF)
        dob = jnp.where(is_lat, do_ref[...], 0.0).astype(BF)
        order = range(NCK) if d else range(NCK - 1, -1, -1)
        dqd_h, dki_h, dke_h, dv_h, ddec_h = [], [], [], [], []
        for h in range(H):
            hs = slice(h * DH, (h + 1) * DH)
            a = jnp.where(mb, _dot_nt(qdb[:, hs], kib[:, hs]), 0.0).astype(BF)
            da = jnp.where(mb, _dot_nt(dob[:, hs], vb[:, hs]), 0.0).astype(BF)
            dqd_i = _dot(da, kib[:, hs])
            dki_h.append(_dot_tn(da, qdb[:, hs]))
            dvi = _dot_tn(a, dob[:, hs])
            dqd, inc = {}, {}
            for c in range(NCK):
                rs = _chunk_rows(c)
                dqd[c] = dqd_i[rs, :] + _dot(dob[rs, hs], st_ref[c, h])
                inc[c] = _dot_tn(dob[rs, hs], qdb[rs, hs])
            dst = dS[h]
            dke, dv, ddec = {}, {}, {}
            for c in order:
                rs = _chunk_rows(c)
                dv[c] = dvi[rs, :] + _dot_nt(keb[rs, hs], dst)
                dke[c] = _dot(vb[rs, hs], dst)
                ddec[c] = _colsum(st_ref[c, h] * dst)
                dst = inc[c] + dst * decs[c][:, hs]
            dS[h] = dst
            cat = lambda t: jnp.concatenate([t[c] for c in range(NCK)], axis=0)
            dqd_h.append(cat(dqd))
            dke_h.append(cat(dke))
            dv_h.append(cat(dv))
            ddec_h.append([ddec[c] for c in range(NCK)])
        lanes = lambda parts: jnp.concatenate(parts, axis=1)
        dqd, dki, dke, dv = lanes(dqd_h), lanes(dki_h), lanes(dke_h), lanes(dv_h)
        dq = dqd * eg
        dk = dki * eig + dke * eeg
        dke_ke = dke * ke
        dg = dqd * qd - dki * ki - dke_ke
        dgl = [_colsum(dke_ke[_chunk_rows(c), :]) + lanes([ddec_h[h][c] for h in range(H)]) * decs[c]
               for c in range(NCK)]
        dglb = jnp.concatenate([jnp.broadcast_to(t, (CH, D)) for t in dgl], axis=0)
        df = (_chunk_cumsum(dg, 1 - d) + dglb) / f - dk
        dzf = df * (1.0 - lb) * (sf * (1.0 - sf))
        sum_ref[0:1, :] += _colsum(dzf)
        sum_ref[1:2, :] += _colsum(df * (1.0 - sf))
        if last:
            dz0 = (dq + pq_ref[...]) * (Q_SCALE * (sq * (1.0 + zq * (1.0 - sq))))
            dz3 = dv + pv_ref[...]
            sum_ref[2:3, :] += _colsum(dz0)
            sum_ref[3:4, :] += _colsum(dz3)
            dz2 = dzf.astype(BF)
            sh0_ref[:, 0:D] = dz0.astype(BF)
            sh0_ref[:, D:2 * D] = dz1_ref[...]
            sh0_ref[:, 2 * D:] = dz2[:, :CUT2]
            sh1_ref[:, 0:D - CUT2] = dz2[:, CUT2:]
            sh1_ref[:, D - CUT2:2 * D - CUT2] = dz3.astype(BF)
            sh1_ref[:, 2 * D - CUT2:] = dz4a_ref[...]
        else:
            o0_ref[...] = dq
            o1_ref[...] = dv
            o2_ref[...] = dzf.astype(BF)

        @pl.when(s == nrb - 1)
        def _():
            sum_ref[1:2, :] = sum_ref[1:2, :] * (lb * (1.0 - lb))

    def zspec(cb):
        return pl.BlockSpec((RB, D), lambda s: (rb(s), cb))

    rowspec = pl.BlockSpec((RB, D), lambda s: (rb(s), 0))
    in_specs = [zspec(0), zspec(1 + d), zspec(3), pl.BlockSpec((2, 2, D), lambda s: (0, 0, 0)),
                pl.BlockSpec((RB, D), lambda s: (jnp.minimum(rb(s), nlb - 1), 0)),
                pl.BlockSpec((NCK, H, DH, DH), lambda s: (rb(s), 0, 0, 0))]
    args = [z, z, z, lbl, do_lat, states]
    sumspec = pl.BlockSpec((8, D), lambda s: (0, 0))
    if last:
        in_specs += [rowspec, rowspec, rowspec, pl.BlockSpec((RB, CUT4), lambda s: (rb(s), 0))]
        args += list(prev)
        shspec = pl.BlockSpec((RB, SHC), lambda s: (rb(s), 0))
        out_specs = (shspec, shspec, sumspec)
        out_shape = (jax.ShapeDtypeStruct((m, SHC), BF), jax.ShapeDtypeStruct((m, SHC), BF))
    else:
        out_specs = (rowspec, rowspec, rowspec, sumspec)
        out_shape = (jax.ShapeDtypeStruct((m, D), F32), jax.ShapeDtypeStruct((m, D), F32),
                     jax.ShapeDtypeStruct((m, D), BF))
    return pl.pallas_call(
        body, grid=(nrb,), in_specs=in_specs, out_specs=out_specs,
        out_shape=out_shape + (jax.ShapeDtypeStruct((8, D), F32),),
        scratch_shapes=[pltpu.VMEM((H, DH, DH), F32)],
        name=f"gla_bwd{d}", compiler_params=_cp(("arbitrary",)))(*args)


def _shift_rows(v, k, r):
    row = lax.broadcasted_iota(jnp.int32, v.shape, 0)
    rolled = pltpu.roll(v, k % r, 0)
    return jnp.where((row >= k) & (row < r + k), rolled, 0.0)


def _conv_fwd(xb, cw_ref, cb_ref, r):
    return (cb_ref[...] + _shift_rows(xb, 1, r) * cw_ref[0:1, :] + xb * cw_ref[1:2, :]
            + _shift_rows(xb, -1, r) * cw_ref[2:3, :] + _shift_rows(xb, -2, r) * cw_ref[3:4, :])


def _window(ref, lo, n, rows):
    parts = []
    if lo < 0:
        parts.append(jnp.zeros((-lo,) + tuple(ref.shape[1:]), F32))
    parts.append(ref[max(lo, 0):min(lo + n, rows)])
    if lo + n > rows:
        parts.append(jnp.zeros((lo + n - rows,) + tuple(ref.shape[1:]), F32))
    return parts[0] if len(parts) == 1 else jnp.concatenate(parts, axis=0)


def _conv_cols(x_ref, cw_ref, cb_ref, r0, n, rows):
    w = _window(x_ref, r0 - 1, n + 3, rows)
    return (cb_ref[...] + w[0:n] * cw_ref[0:1, :] + w[1:n + 1] * cw_ref[1:2, :] + w[2:n + 2] * cw_ref[2:3, :]
            + w[3:n + 3] * cw_ref[3:4, :])


def _softplus_neg(lam):
    y = jnp.exp(-jnp.abs(lam))
    u = 1.0 + y
    tiny = u == 1.0
    l1p = jnp.where(tiny, y, jnp.log(u) * (y / jnp.where(tiny, 1.0, u - 1.0)))
    return jnp.maximum(-lam, 0.0) + l1p


def _gates(xc, wr_ref, br, wi_ref, bi, sp):
    xcb = xc.astype(BF)
    rs, is_ = [], []
    for g in range(H):
        gs = slice(g * DH, (g + 1) * DH)
        rs.append(jnp.dot(xcb[:, gs], wr_ref[g].astype(BF), preferred_element_type=F32))
        is_.append(jnp.dot(xcb[:, gs], wi_ref[g].astype(BF), preferred_element_type=F32))
    r = _sig(jnp.concatenate(rs, axis=1) + br)
    i = _sig(jnp.concatenate(is_, axis=1) + bi)
    log_a = (-RG_C * r) * sp
    a = jnp.exp(log_a)
    t = jnp.tanh(log_a)
    om = (-2.0 * t) / (1.0 - t)
    return r, i, a, om


def _scan_rows(d, nrows, a_s, b_s, h_s, h0):
    nsl = nrows // 8

    def slab(j, h):
        jj = (nsl - 1 - j) if d else j
        r0 = pl.multiple_of(jj * 8, 8)
        for t in (range(7, -1, -1) if d else range(8)):
            h = a_s[pl.ds(r0 + t, 1), :] * h + b_s[pl.ds(r0 + t, 1), :]
            h_s[pl.ds(r0 + t, 1), :] = h
        return h

    return lax.fori_loop(0, nsl, slab, h0)


def _col_of(d, ncols):
    if d:
        return lambda s: ncols - jnp.maximum(s, 1)
    return lambda s: jnp.maximum(s, 1) - 1


def _rglru_fwd(z, cw, cb, wr, br, wi, bi, lam, d, t_lat, t_ctx):
    m = z.shape[0]
    rows = t_lat // GRID_W
    z3 = z.reshape(m // GRID_W, GRID_W, IN_COLS)
    nblk = GRID_W // CB
    blk = _col_of(d, nblk)
    cblk = t_lat // t_ctx
    rc = min(RCH, rows)

    def body(zc_ref, zx_ref, cw_ref, cb_ref, wr_ref, br_ref, wi_ref, bi_ref, lam_ref,
             hx_ref, hpx_ref, ax_ref, hpc_ref, a_s, b_s, h_s, hcar, a3, b3, cin_s):
        s = pl.program_id(0)
        sp = _softplus_neg(lam_ref[d:d + 1, :])
        br_ = br_ref[d:d + 1, :]
        bi_ = bi_ref[d:d + 1, :]

        @pl.when(s == 0)
        def _():
            xc = _conv_fwd(zc_ref[...], cw_ref, cb_ref, t_ctx)
            _, i, a, om = _gates(xc, wr_ref, br_, wi_ref, bi_, sp)
            a_s[...] = a
            b_s[...] = jnp.sqrt(om) * (i * xc)
            h0 = jnp.zeros((1, D), F32)
            hcar[0:1, :] = _scan_rows(d, t_ctx, a_s, b_s, h_s, h0)
            hs = h_s[...]
            row = lax.broadcasted_iota(jnp.int32, (t_ctx, D), 0)
            if d:
                hpc_ref[...] = jnp.where(row == t_ctx - 1, h0, pltpu.roll(hs, t_ctx - 1, 0))
            else:
                hpc_ref[...] = jnp.where(row == 0, h0, pltpu.roll(hs, 1, 0))

        @pl.when(s > 0)
        def _():
            for r0 in range(0, rows, rc):
                xc = _conv_cols(zx_ref, cw_ref, cb_ref, r0, rc, rows).reshape(rc * CB, D)
                _, i, a, om = _gates(xc, wr_ref, br_, wi_ref, bi_, sp)
                a3[r0:r0 + rc] = a.reshape(rc, CB, D)
                ax_ref[r0:r0 + rc] = a.reshape(rc, CB, D)
                b3[r0:r0 + rc] = (jnp.sqrt(om) * (i * xc)).reshape(rc, CB, D)

            def local(t, carry):
                hl, p = carry
                r = (rows - 1 - t) if d else t
                a = a3[r]
                hl = a * hl + b3[r]
                p = a * p
                b3[r] = hl
                a3[r] = p
                return hl, p

            hl, p = lax.fori_loop(0, rows, local, (jnp.zeros((CB, D), F32), jnp.ones((CB, D), F32)))
            cin = hcar[0:1, :]
            for j in (range(CB - 1, -1, -1) if d else range(CB)):
                cin_s[j:j + 1, :] = cin
                cin = hl[j:j + 1, :] + p[j:j + 1, :] * cin
            hcar[0:1, :] = cin
            c_in = cin_s[...]

            def fix(t, prev):
                r = (rows - 1 - t) if d else t
                h = b3[r] + a3[r] * c_in
                hx_ref[r] = h
                hpx_ref[r] = prev
                return h

            lax.fori_loop(0, rows, fix, c_in)

    full = lambda shp: pl.BlockSpec(shp, lambda s: (0,) * len(shp))
    colspec = pl.BlockSpec((rows, CB, D), lambda s: (0, blk(s), 0))
    outs = pl.pallas_call(
        body, grid=(nblk + 1,),
        in_specs=[pl.BlockSpec((t_ctx, D), lambda s: (cblk, 5)),
                  pl.BlockSpec((rows, CB, D), lambda s: (0, blk(s), 5)),
                  full((4, D)), full((1, D)),
                  pl.BlockSpec((None, H, DH, DH), lambda s: (d, 0, 0, 0)), full((2, D)),
                  pl.BlockSpec((None, H, DH, DH), lambda s: (d, 0, 0, 0)), full((2, D)), full((2, D))],
        out_specs=(colspec, colspec, colspec, full((t_ctx, D))),
        out_shape=(jax.ShapeDtypeStruct((rows, GRID_W, D), F32),) * 3 + (jax.ShapeDtypeStruct((t_ctx, D), F32),),
        scratch_shapes=[pltpu.VMEM((t_ctx, D), F32), pltpu.VMEM((t_ctx, D), F32), pltpu.VMEM((t_ctx, D), F32),
                        pltpu.VMEM((8, D), F32), pltpu.VMEM((rows, CB, D), F32), pltpu.VMEM((rows, CB, D), F32),
                        pltpu.VMEM((CB, D), F32)],
        name=f"rglru_fwd{d}", compiler_params=_cp(("arbitrary",)))(z, z3, cw, cb, wr, br, wi, bi, lam)
    return outs[0].reshape(t_lat, D), outs[1].reshape(t_lat, D), outs[2].reshape(t_lat, D), outs[3]


def _rglru_bwd(z, cw, cb, wr, br, wi, bi, lam, dh_lat, hp_lat, a_lat, hp_ctx, d, t_lat, t_ctx, prev=None):
    m = z.shape[0]
    rows = t_lat // GRID_W
    z3 = z.reshape(m // GRID_W, GRID_W, IN_COLS)
    nblk = GRID_W // CB
    fblk = _col_of(d, nblk)
    blk = lambda s: fblk(nblk - jnp.minimum(s, nblk - 1))
    cblk = t_lat // t_ctx
    rc = min(RCH, rows)
    last = prev is not None
    view3 = lambda v: v.reshape(rows, GRID_W, D)

    def body(*refs):
        (zc_ref, zx_ref, cw_ref, cb_ref, wr_ref, br_ref, wi_ref, bi_ref, lam_ref, dh_ref, hpx_ref, ax_ref,
         hpc_ref) = refs[:13]
        k = 13
        if last:
            pdx_ref, pdc_ref = refs[13:15]
            k = 15
        ox_ref, oc_ref, dwr_ref, dwi_ref, sum_ref, a_s, b_s, gcar, a3, b3, cin_s = refs[k:]
        s = pl.program_id(0)
        lam_d = lam_ref[d:d + 1, :]
        sp = _softplus_neg(lam_d)
        br_ = br_ref[d:d + 1, :]
        bi_ = bi_ref[d:d + 1, :]
        flat = lambda v: v.reshape(-1, D)

        @pl.when(s == 0)
        def _():
            gcar[...] = jnp.zeros_like(gcar)
            dwr_ref[...] = jnp.zeros_like(dwr_ref)
            dwi_ref[...] = jnp.zeros_like(dwi_ref)
            sum_ref[...] = jnp.zeros_like(sum_ref)


        def conv_sums(dxc, dxb, xm1, x0, xp1, xp2):
            sum_ref[3:4, :] += _colsum(flat(dxc))
            sum_ref[4:5, :] += _colsum(flat(dxb))
            sum_ref[8:9, :] += _colsum(flat(dxc * xm1))
            sum_ref[9:10, :] += _colsum(flat(dxc * x0))
            sum_ref[10:11, :] += _colsum(flat(dxc * xp1))
            sum_ref[11:12, :] += _colsum(flat(dxc * xp2))

        def gate_grads(g, hp, xc, rr, i, a, om):
            mult = jnp.sqrt(om)
            da = g * hp
            ixc = i * xc
            dmult = g * ixc
            dixc = g * mult
            di = dixc * xc
            dxc = dixc * i
            dlog_a = da * a - dmult * ((1.0 - om) / mult)
            dr = dlog_a * (-RG_C * sp)
            sum_ref[2:3, :] += _colsum(dlog_a * rr)
            drp = dr * rr * (1.0 - rr)
            dip = di * i * (1.0 - i)
            sum_ref[0:1, :] += _colsum(drp)
            sum_ref[1:2, :] += _colsum(dip)
            xcb = xc.astype(BF)
            drb = drp.astype(BF)
            dib = dip.astype(BF)
            parts = []
            for gi in range(H):
                gs = slice(gi * DH, (gi + 1) * DH)
                parts.append(_dot_nt(drb[:, gs], wr_ref[gi]) + _dot_nt(dib[:, gs], wi_ref[gi]))
                dwr_ref[gi] += _dot_tn(xcb[:, gs], drb[:, gs])
                dwi_ref[gi] += _dot_tn(xcb[:, gs], dib[:, gs])
            return dxc + jnp.concatenate(parts, axis=1)

        @pl.when(s < nblk)
        def _():
            def local(t, carry):
                c, q = carry
                r = t if d else (rows - 1 - t)
                a = ax_ref[r]
                c = a * (c + dh_ref[r])
                q = a * q
                b3[r] = c
                a3[r] = q
                return c, q

            c, q = lax.fori_loop(0, rows, local, (jnp.zeros((CB, D), F32), jnp.ones((CB, D), F32)))
            cin = gcar[0:1, :]
            for j in (range(CB) if d else range(CB - 1, -1, -1)):
                cin_s[j:j + 1, :] = cin
                cin = c[j:j + 1, :] + q[j:j + 1, :] * cin
            gcar[0:1, :] = cin
            c_in = cin_s[...]
            for r0 in (range(rows - rc, -1, -rc) if d else range(0, rows, rc)):
                if d:
                    lo = max(r0 - 1, 0)
                    cn = b3[lo:r0 + rc - 1] + a3[lo:r0 + rc - 1] * c_in
                    if r0 == 0:
                        cn = jnp.concatenate([c_in[None], cn], axis=0)
                else:
                    hi = min(r0 + rc + 1, rows)
                    cn = b3[r0 + 1:hi] + a3[r0 + 1:hi] * c_in
                    if hi == rows:
                        cn = jnp.concatenate([cn, c_in[None]], axis=0)
                g = flat(dh_ref[r0:r0 + rc] + cn)
                xc = flat(_conv_cols(zx_ref, cw_ref, cb_ref, r0, rc, rows))
                rr, i, a, om = _gates(xc, wr_ref, br_, wi_ref, bi_, sp)
                b3[r0:r0 + rc] = gate_grads(g, flat(hpx_ref[r0:r0 + rc]), xc, rr, i, a, om).reshape(rc, CB, D)
            if last:
                for r0 in range(0, rows, rc):
                    b3[r0:r0 + rc] = b3[r0:r0 + rc] + pdx_ref[r0:r0 + rc]
                for r0 in range(0, rows, rc):
                    w = _window(b3, r0 - 2, rc + 3, rows)
                    xw = _window(zx_ref, r0 - 1, rc + 3, rows)
                    dxc = w[2:rc + 2]
                    dxb = (w[3:rc + 3] * cw_ref[0:1, :] + dxc * cw_ref[1:2, :] + w[1:rc + 1] * cw_ref[2:3, :]
                           + w[0:rc] * cw_ref[3:4, :])
                    conv_sums(dxc, dxb, xw[0:rc], xw[1:rc + 1], xw[2:rc + 2], xw[3:rc + 3])
                    ox_ref[r0:r0 + rc] = dxb
            else:
                for r0 in range(0, rows, rc):
                    ox_ref[r0:r0 + rc] = b3[r0:r0 + rc]

        @pl.when(s == nblk)
        def _():
            r = t_ctx
            xb = zc_ref[...]
            xc = _conv_fwd(xb, cw_ref, cb_ref, r)
            rr, i, a, om = _gates(xc, wr_ref, br_, wi_ref, bi_, sp)
            a_s[...] = a
            b_s[...] = jnp.zeros((r, D), F32)
            c0 = gcar[0:1, :]
            _scan_rows(1 - d, r, a_s, b_s, b_s, c0)
            cs = b_s[...]
            row = lax.broadcasted_iota(jnp.int32, (r, D), 0)
            if d:
                g = jnp.where(row == 0, c0, pltpu.roll(cs, 1, 0))
            else:
                g = jnp.where(row == r - 1, c0, pltpu.roll(cs, r - 1, 0))
            dxc = gate_grads(g, hpc_ref[...], xc, rr, i, a, om)
            if last:
                dxc = dxc + pdc_ref[...]
                dxb = (_shift_rows(dxc, -1, r) * cw_ref[0:1, :] + dxc * cw_ref[1:2, :]
                       + _shift_rows(dxc, 1, r) * cw_ref[2:3, :] + _shift_rows(dxc, 2, r) * cw_ref[3:4, :])
                conv_sums(dxc, dxb, _shift_rows(xb, 1, r), xb, _shift_rows(xb, -1, r), _shift_rows(xb, -2, r))
                oc_ref[...] = dxb
            else:
                oc_ref[...] = dxc
            sum_ref[2:3, :] = sum_ref[2:3, :] * (RG_C * _sig(-lam_d))

    full = lambda shp: pl.BlockSpec(shp, lambda s: (0,) * len(shp))
    once = lambda shp: pl.BlockSpec(shp, lambda s: (0,) * len(shp), pipeline_mode=pl.Buffered(1))
    colspec = pl.BlockSpec((rows, CB, D), lambda s: (0, blk(s), 0))
    colonce = pl.BlockSpec((rows, CB, D), lambda s: (0, blk(s), 0), pipeline_mode=pl.Buffered(1))
    in_specs = [pl.BlockSpec((t_ctx, D), lambda s: (cblk, 5), pipeline_mode=pl.Buffered(1)),
                pl.BlockSpec((rows, CB, D), lambda s: (0, blk(s), 5), pipeline_mode=pl.Buffered(1)),
                full((4, D)), full((1, D)),
                pl.BlockSpec((None, H, DH, DH), lambda s: (d, 0, 0, 0)), full((2, D)),
                pl.BlockSpec((None, H, DH, DH), lambda s: (d, 0, 0, 0)), full((2, D)), full((2, D)),
                colspec, colspec, colonce, once((t_ctx, D))]
    args = [z, z3, cw, cb, wr, br, wi, bi, lam, view3(dh_lat), view3(hp_lat), view3(a_lat), hp_ctx]
    if last:
        in_specs += [colonce, once((t_ctx, D))]
        args += [view3(prev[0]), prev[1]]
    outs = pl.pallas_call(
        body, grid=(nblk + 1,), in_specs=in_specs,
        out_specs=(colspec, full((t_ctx, D)), full((H, DH, DH)), full((H, DH, DH)), full((16, D))),
        out_shape=(jax.ShapeDtypeStruct((rows, GRID_W, D), F32), jax.ShapeDtypeStruct((t_ctx, D), F32),
                   jax.ShapeDtypeStruct((H, DH, DH), F32), jax.ShapeDtypeStruct((H, DH, DH), F32),
                   jax.ShapeDtypeStruct((16, D), F32)),
        scratch_shapes=[pltpu.VMEM((t_ctx, D), F32), pltpu.VMEM((t_ctx, D), F32), pltpu.VMEM((8, D), F32),
                        pltpu.VMEM((rows, CB, D), F32), pltpu.VMEM((rows, CB, D), F32), pltpu.VMEM((CB, D), F32)],
        name=f"rglru_bwd{d}", compiler_params=_cp(("arbitrary",), VMEM_LIMIT_RGLRU_BWD))(*args)
    return (outs[0].reshape(t_lat, D), outs[1]) + tuple(outs[2:])


def _merge(o_f, o_b, h_f, h_b, z, x_all, tgt, mod, norm_g, ln_g, ln_b, p_a, p_b, w_out, t_lat):
    tm = 256
    nt = t_lat // tm

    def body(of_ref, ob_ref, hf_ref, hb_ref, z4_ref, z6_ref, z7_ref, z8_ref, x_ref, t_ref, mod_ref, ng_ref,
             lg_ref, lb_ref, pa_ref, pb_ref, wo_ref,
             do_ref, dh_ref, dz4a_ref, dz4b_ref, dz6a_ref, sh3_ref, gx_ref,
             y_ref, dout_ref, oa_ref, dpa_ref, obv_ref, dpb_ref, acc_ref):
        i = pl.program_id(0)
        lat = i < nt
        latf = lat.astype(F32)

        @pl.when(i == 0)
        def _():
            acc_ref[...] = jnp.zeros_like(acc_ref)

        def per_head(v):
            return jnp.concatenate(
                [jnp.broadcast_to(jnp.mean(v[:, h * DH:(h + 1) * DH], axis=-1, keepdims=True), (tm, DH))
                 for h in range(H)], axis=1)

        gt = mod_ref[0:1, 2 * D:3 * D]
        gfull = jnp.concatenate([ng_ref[...]] * H, axis=1)
        o = of_ref[...] + ob_ref[...]
        rinv = lax.rsqrt(per_head(o * o) + RMS_EPS)
        n = o * rinv
        na = n * gfull
        z4 = z4_ref[...]
        s4 = _sig(z4)
        silu4 = z4 * s4
        oa = na * silu4
        z6 = z6_ref[...]
        s6 = _sig(z6)
        silu6 = z6 * s6
        hsum = hf_ref[...] + hb_ref[...]
        obv = hsum * silu6
        pa = _dot(oa, pa_ref[...])
        pb = _dot(obv, pb_ref[...])
        s7 = _sig(z7_ref[...])
        s8 = _sig(z8_ref[...])
        y = s7 * pa + s8 * pb
        out = _dot(y, wo_ref[...])
        pre = ALPHA * x_ref[...] + gt * out
        mu = jnp.mean(pre, axis=-1, keepdims=True)
        xc = pre - mu
        rstd = lax.rsqrt(jnp.mean(xc * xc, axis=-1, keepdims=True) + LN_EPS)
        xhat = xc * rstd
        lg = lg_ref[...]
        diff = xhat * lg + lb_ref[...] - t_ref[...]
        acc_ref[8:9, :] += _colsum(diff * diff) * (0.5 / D * latf)
        dxn = diff * (1.0 / D)
        acc_ref[1:2, :] += _colsum(dxn * xhat) * latf
        acc_ref[2:3, :] += _colsum(dxn) * latf
        dxhat = dxn * lg
        dpre = rstd * (dxhat - jnp.mean(dxhat, axis=-1, keepdims=True)
                       - xhat * jnp.mean(dxhat * xhat, axis=-1, keepdims=True))
        gx_ref[...] = ALPHA * dpre
        acc_ref[0:1, :] += _colsum(dpre * out) * latf
        dout = dpre * gt
        dy = _dot_nt(dout, wo_ref[...])
        dpa = dy * s7
        dpb = dy * s8
        dz7 = dy * pa * (s7 * (1.0 - s7))
        dz8 = dy * pb * (s8 * (1.0 - s8))
        doa = _dot_nt(dpa, pa_ref[...])
        dob = _dot_nt(dpb, pb_ref[...])
        dh_ref[...] = dob * silu6
        dz6 = dob * hsum * (s6 * (1.0 + z6 * (1.0 - s6)))
        dna = doa * silu4
        dz4 = doa * na * (s4 * (1.0 + z4 * (1.0 - s4)))
        dng = _colsum(dna * n)
        acc_ref[7:8, 0:DH] += sum(dng[:, h * DH:(h + 1) * DH] for h in range(H)) * latf
        dn = dna * gfull
        do_ref[...] = rinv * (dn - n * per_head(dn * n))
        acc_ref[3:4, :] += _colsum(dz4) * latf
        acc_ref[4:5, :] += _colsum(dz6) * latf
        acc_ref[5:6, :] += _colsum(dz7) * latf
        acc_ref[6:7, :] += _colsum(dz8) * latf
        dz4b, dz6b = (dz4 * latf).astype(BF), (dz6 * latf).astype(BF)
        dz4a_ref[...] = dz4b[:, :CUT4]
        dz4b_ref[...] = dz4b[:, CUT4:]
        dz6a_ref[...] = dz6b[:, :CUT6]
        sh3_ref[:, 0:D - CUT6] = dz6b[:, CUT6:]
        sh3_ref[:, D - CUT6:2 * D - CUT6] = (dz7 * latf).astype(BF)
        sh3_ref[:, 2 * D - CUT6:] = (dz8 * latf).astype(BF)
        y_ref[...] = y.astype(BF)
        dout_ref[...] = dout.astype(BF)
        oa_ref[...] = oa.astype(BF)
        dpa_ref[...] = dpa.astype(BF)
        obv_ref[...] = obv.astype(BF)
        dpb_ref[...] = dpb.astype(BF)

        @pl.when(i == nt - 1)
        def _():
            acc_ref[9:10, :] = jnp.broadcast_to(jnp.sum(acc_ref[8:9, :], axis=-1, keepdims=True), (1, D))

    m = x_all.shape[0]
    lrow = lambda i: jnp.minimum(i, nt - 1)
    row = pl.BlockSpec((tm, D), lambda i: (lrow(i), 0))
    allrow = lambda cols: pl.BlockSpec((tm, cols), lambda i: (i, 0))
    zs = lambda cb: pl.BlockSpec((tm, D), lambda i: (lrow(i), cb))
    full = lambda shp: pl.BlockSpec(shp, lambda i: (0,) * len(shp))
    wfull = pl.BlockSpec((D, D), lambda i: (0, 0), pipeline_mode=pl.Buffered(1))
    f32o = jax.ShapeDtypeStruct((t_lat, D), F32)
    bfo = jax.ShapeDtypeStruct((t_lat, D), BF)
    bfall = lambda cols: jax.ShapeDtypeStruct((m, cols), BF)
    return pl.pallas_call(
        body, grid=(m // tm,),
        in_specs=[row, row, row, row, zs(4), zs(6), zs(7), zs(8), row, row, full((16, 3 * D)), full((1, DH)),
                  full((1, D)), full((1, D)), wfull, wfull, wfull],
        out_specs=(row, row) + tuple(allrow(c) for c in (CUT4, D - CUT4, CUT6, SHC)) + (row,) * 7 + (full((16, D)),),
        out_shape=(f32o, f32o, bfall(CUT4), bfall(D - CUT4), bfall(CUT6), bfall(SHC), f32o, bfo, bfo, bfo, bfo, bfo, bfo,
                   jax.ShapeDtypeStruct((16, D), F32)),
        name="merge", compiler_params=_cp(("arbitrary",), VMEM_LIMIT_MERGE))(
            o_f, o_b, h_f, h_b, z, z, z, z, x_all, tgt, mod, norm_g, ln_g, ln_b, p_a, p_b, w_out)


def _wgrad(a, b, name):
    tm = 1024

    def body(a_ref, b_ref, o_ref):
        @pl.when(pl.program_id(0) == 0)
        def _():
            o_ref[...] = jnp.zeros_like(o_ref)
        o_ref[...] += _dot_tn(a_ref[...], b_ref[...])

    row = pl.BlockSpec((tm, D), lambda i: (i, 0))
    return pl.pallas_call(body, grid=(a.shape[0] // tm,), in_specs=[row, row],
                          out_specs=pl.BlockSpec((D, D), lambda i: (0, 0)),
                          out_shape=jax.ShapeDtypeStruct((D, D), F32), name=name,
                          compiler_params=_cp(("arbitrary",)))(a, b)


def _pack_shard2(dz4b, dz5_lat, dz5_ctx, dz6a):
    m, t_lat, t_ctx = dz4b.shape[0], dz5_lat.shape[0], dz5_ctx.shape[0]
    tm = t_ctx
    nlt = t_lat // tm
    w4 = D - CUT4

    def body(a_ref, bl_ref, bc_ref, c_ref, o_ref):
        i = pl.program_id(0)
        o_ref[:, 0:w4] = a_ref[...]
        o_ref[:, w4:w4 + D] = jnp.where(i < nlt, bl_ref[...], bc_ref[...]).astype(BF)
        o_ref[:, w4 + D:] = c_ref[...]

    return pl.pallas_call(
        body, grid=(m // tm,),
        in_specs=[pl.BlockSpec((tm, w4), lambda i: (i, 0)),
                  pl.BlockSpec((tm, D), lambda i: (jnp.minimum(i, nlt - 1), 0)),
                  pl.BlockSpec((tm, D), lambda i: (0, 0)),
                  pl.BlockSpec((tm, CUT6), lambda i: (i, 0))],
        out_specs=pl.BlockSpec((tm, SHC), lambda i: (i, 0)),
        out_shape=jax.ShapeDtypeStruct((m, SHC), BF), name="pack_shard2",
        compiler_params=_cp(("arbitrary",)))(dz4b, dz5_lat, dz5_ctx, dz6a)


def _wgrad_in(u_all, dz_shards):
    m = u_all.shape[0]
    assert m % 128 == 0
    tm = m // 8
    out = None
    for k, dz_k in enumerate(dz_shards):
        def body(u_ref, dz_ref, *rest):
            o_ref = rest[-1]

            @pl.when(pl.program_id(0) == 0)
            def _():
                o_ref[...] = jnp.zeros_like(o_ref)
            o_ref[0] += _dot_tn(u_ref[...], dz_ref[...])

        out = pl.pallas_call(
            body, grid=(m // tm,),
            in_specs=[pl.BlockSpec((tm, D), lambda i: (i, 0)), pl.BlockSpec((tm, SHC), lambda i: (i, 0))]
            + ([] if out is None else [_ANY]),
            out_specs=pl.BlockSpec((1, D, SHC), lambda i, k=k: (k, 0, 0)),
            out_shape=jax.ShapeDtypeStruct((NSH, D, SHC), F32),
            input_output_aliases={} if out is None else {2: 0},
            name=f"wgrad_in{k}", compiler_params=_cp(("arbitrary",)))(u_all, dz_k, *(() if out is None else (out,)))
    return out


def _du(dz_shards, w_in_g, x_all, mod, gxres, n_lat_tiles, sums=()):
    m = x_all.shape[0]
    tm = 256
    nt = m // tm
    nct = nt - n_lat_tiles
    ns = len(sums)
    rblk = lambda i: jnp.where(i < nct, n_lat_tiles + i, i - nct)
    lblk = lambda i: jnp.maximum(i - nct, 0)

    def body(*refs):
        dz_refs, refs = refs[:NSH], refs[NSH - 1:]
        _, w_ref, x_ref, mod_ref, gr_ref = refs[:5]
        sum_refs = refs[5:5 + ns]
        gx_ref, dm_ref = refs[5 + ns:7 + ns]
        got_refs = refs[7 + ns:7 + 2 * ns]
        sems = refs[7 + 2 * ns:]
        i = pl.program_id(0)
        is_lat = i >= nct

        @pl.when(i == 0)
        def _():
            dm_ref[...] = jnp.zeros_like(dm_ref)
            if ns:
                for cp in _rs_chip_copies(sum_refs, got_refs, *sems):
                    cp.start()

        du = _dot_nt(dz_refs[0][...], w_ref[0])
        for n in range(1, NSH):
            du = du + _dot_nt(dz_refs[n][...], w_ref[n])
        sc = jnp.where(is_lat, mod_ref[0:1, D:2 * D], mod_ref[1:2, D:2 * D])
        dsh = _colsum(du)
        dsc = _colsum(du * x_ref[...])

        @pl.when(is_lat)
        def _():
            gx_ref[...] = du * (1.0 + sc) + gr_ref[...]
            dm_ref[0:1, 0:D] += dsh
            dm_ref[0:1, D:2 * D] += dsc

        @pl.when(jnp.logical_not(is_lat))
        def _():
            dm_ref[1:2, 0:D] += dsh
            dm_ref[1:2, D:2 * D] += dsc

        if ns:
            @pl.when(i == nt - 1)
            def _():
                for cp in _rs_chip_copies(sum_refs, got_refs, *sems):
                    cp.wait()

    outs = pl.pallas_call(
        body, grid=(nt,),
        in_specs=[pl.BlockSpec((tm, SHC), lambda i: (rblk(i), 0))] * NSH + [
                  pl.BlockSpec((NSH, D, SHC), lambda i: (0, 0, 0), pipeline_mode=pl.Buffered(1)),
                  pl.BlockSpec((tm, D), lambda i: (rblk(i), 0)),
                  pl.BlockSpec((16, 3 * D), lambda i: (0, 0)),
                  pl.BlockSpec((tm, D), lambda i: (lblk(i), 0))] + [_ANY] * ns,
        out_specs=[pl.BlockSpec((tm, D), lambda i: (lblk(i), 0)),
                   pl.BlockSpec((8, 2 * D), lambda i: (0, 0))] + [_ANY] * ns,
        out_shape=[jax.ShapeDtypeStruct((n_lat_tiles * tm, D), F32), jax.ShapeDtypeStruct((8, 2 * D), F32)]
        + [jax.ShapeDtypeStruct((3,) + g.shape[1:], g.dtype) for g in sums],
        scratch_shapes=[pltpu.SemaphoreType.DMA((3 * ns,)), pltpu.SemaphoreType.DMA((3 * ns,))] if ns else [],
        name="du", compiler_params=_cp(("arbitrary",)))(*dz_shards, w_in_g, x_all, mod, gxres, *sums)
    return outs[0], outs[1], list(outs[2:])


def _row_tile(rows, cols):
    t = 8
    while t * 2 * cols * 4 <= (1 << 20) and rows % (t * 2) == 0:
        t *= 2
    return t


def _adamw_update(w_ref, g_ref, m_ref, v_ref, d_ref, nm_ref, nv_ref):
    gg = g_ref[...]
    m2 = ADAM_B1 * m_ref[...] + (1.0 - ADAM_B1) * gg
    v2 = ADAM_B2 * v_ref[...] + (1.0 - ADAM_B2) * (gg * gg)
    m_hat = m2 / (1.0 - ADAM_B1 ** ADAM_STEP)
    v_hat = v2 / (1.0 - ADAM_B2 ** ADAM_STEP)
    d_ref[...] = -ADAM_LR * (m_hat / (jnp.sqrt(v_hat) + ADAM_EPS) + ADAM_WD * w_ref[...])
    nm_ref[...] = m2
    nv_ref[...] = v2


def _adamw_many(ws, gs, ms, vs):
    n = len(ws)

    def body(*refs):
        for j in range(n):
            _adamw_update(*refs[4 * j:4 * j + 4], *refs[4 * n + 3 * j:4 * n + 3 * j + 3])

    args = [a for quad in zip(ws, gs, ms, vs) for a in quad]
    outs = pl.pallas_call(body, out_shape=[jax.ShapeDtypeStruct(w.shape, F32) for w in ws for _ in range(3)],
                          name="adamw_small", compiler_params=_cp())(*args)
    return outs[0::3], outs[1::3], outs[2::3]


def _adamw(w, g, m, v, name):
    rows, cols = w.shape
    tr = _row_tile(rows, cols)

    def body(*refs):
        _adamw_update(*refs)

    spec = pl.BlockSpec((tr, cols), lambda i: (i, 0))
    o = jax.ShapeDtypeStruct((rows, cols), F32)
    return pl.pallas_call(body, grid=(rows // tr,), in_specs=[spec] * 4, out_specs=(spec,) * 3,
                          out_shape=(o, o, o), name=name, compiler_params=_cp(("arbitrary",)))(w, g, m, v)


_ANY = pl.BlockSpec(memory_space=pl.ANY)


def _place():
    return lax.axis_index("x"), lax.axis_index("y"), lax.axis_index("c")


def _ag_copies(ins, outs, send, recv, fsend, frecv, lsem):
    x, y, c = _place()
    me = 2 * x + y
    chips = ((1 - x, y), (x, 1 - y), (1 - x, 1 - y))
    local, chip, hand = [], [], []
    for j in range(len(ins)):
        hr = ins[j].shape[0] // 2
        half = pl.ds(pl.multiple_of(c * hr, 8), hr)
        local.append(pltpu.make_async_copy(ins[j], outs[j].at[me], lsem.at[j]))
        for k, (px, py) in enumerate(chips):
            chip.append(pltpu.make_async_remote_copy(
                src_ref=ins[j].at[half, :], dst_ref=outs[j].at[me, half, :], send_sem=send.at[3 * j + k],
                recv_sem=recv.at[3 * j + k], device_id=(px, py, c), device_id_type=MESH))
            got = outs[j].at[2 * px + py, half, :]
            hand.append(pltpu.make_async_remote_copy(
                src_ref=got, dst_ref=got, send_sem=fsend.at[3 * j + k], recv_sem=frecv.at[3 * j + k],
                device_id=(x, y, 1 - c), device_id_type=MESH))
    return local, chip, hand


def _ag_sems(n):
    return [pltpu.SemaphoreType.DMA((3 * n,))] * 4 + [pltpu.SemaphoreType.DMA((n,))]


def _ag_finish(local, chip, hand, done=0):
    for k in range(done, len(chip)):
        chip[k].wait_recv()
        hand[k].start()
    for cp in chip:
        cp.wait_send()
    for k in range(done):
        hand[k].wait_send()
    for k in range(done, len(chip)):
        hand[k].wait_send()
        hand[k].wait_recv()
    for cp in local:
        cp.wait()


def _mod_tp(c8, c_ctx, w_mod_sh, b_mod_sh):
    mc = w_mod_sh.shape[1]

    def body(c8_ref, cctx_ref, w_ref, b_ref, mod_ref, cc_ref, cc_s, part_s, send1, recv1, send3, recv3):
        x, y, c = _place()
        me = 4 * x + 2 * y + c
        ms = 2 * x + y
        copies = []
        for k in range(1, 8):
            peer = (x ^ ((k >> 2) & 1), y ^ ((k >> 1) & 1), c ^ (k & 1))
            cp = pltpu.make_async_remote_copy(src_ref=c8_ref, dst_ref=cc_s.at[me], send_sem=send1.at[k],
                                              recv_sem=recv1.at[k], device_id=peer, device_id_type=MESH)
            cp.start()
            copies.append(cp)
        cc_s[me] = c8_ref[...]
        for cp in copies:
            cp.wait()
        cc_ref[...] = jnp.zeros_like(cc_ref)
        for j in range(8):
            cc_ref[j:j + 1, :] = cc_s[j, 0:1, :]
        cc_ref[8:9, :] = cctx_ref[...]
        v = cc_ref[...]
        part_s[ms] = _dot(v * _sig(v), w_ref[...]) + b_ref[...]
        copies = []
        for k in range(1, 4):
            peer = (x ^ ((k >> 1) & 1), y ^ (k & 1), c)
            cp = pltpu.make_async_remote_copy(src_ref=part_s.at[ms], dst_ref=part_s.at[ms], send_sem=send3.at[k],
                                              recv_sem=recv3.at[k], device_id=peer, device_id_type=MESH)
            cp.start()
            copies.append(cp)
        for cp in copies:
            cp.wait()
        for s in range(NSH):
            mod_ref[:, s * mc:(s + 1) * mc] = part_s[s]

    vm = pl.BlockSpec(memory_space=pltpu.VMEM)
    return pl.pallas_call(
        body, in_specs=[vm] * 4, out_specs=(vm, vm),
        out_shape=(jax.ShapeDtypeStruct((16, NSH * mc), F32), jax.ShapeDtypeStruct((16, D), F32)),
        scratch_shapes=[pltpu.VMEM((8, 8, D), F32), pltpu.VMEM((NSH, 16, mc), F32),
                        pltpu.SemaphoreType.DMA((8,)), pltpu.SemaphoreType.DMA((8,)),
                        pltpu.SemaphoreType.DMA((4,)), pltpu.SemaphoreType.DMA((4,))],
        name="mod_tp", compiler_params=_cp())(c8, c_ctx, w_mod_sh, b_mod_sh)


def _inproj_ag(x_all, mod, w_in_sh, b_in, narrow_sh, n_lat_tiles):
    m = x_all.shape[0]
    assert m % (11 * 16) == 0
    tm = m // 11
    nt = m // tm
    n_lat = n_lat_tiles * 256
    x_, y_ = lax.axis_index("x"), lax.axis_index("y")
    sids = jnp.stack([2 * x_ + y_, 2 * (1 - x_) + y_, 2 * x_ + 1 - y_, 2 * (1 - x_) + 1 - y_]).astype(jnp.int32)

    def body(sid_ref, x_ref, mod_ref, b_ref, wsh_ref, nsh_ref, z_ref, u_ref, wg_ref, ng_ref, w_s, u_s, *sems):
        n = pl.program_id(0)
        i = pl.program_id(1)
        rows = pl.ds(pl.multiple_of(i * tm, tm), tm)
        ag = ((wsh_ref, nsh_ref), (wg_ref, ng_ref)) + tuple(sems[:5])
        wsem = sems[5]

        def load(src):
            cp = pltpu.make_async_copy(src, w_s, wsem)
            cp.start()
            cp.wait()

        @pl.when((n == 0) & (i == 0))
        def _():
            local, chip, _ = _ag_copies(*ag)
            for cp in chip + local:
                cp.start()
            load(wsh_ref)

        for k in range(NSH - 1):
            @pl.when((n == k + 1) & (i == 0))
            def _():
                _, chip, hand = _ag_copies(*ag)
                chip[k].wait_recv()
                hand[k].start()
                hand[k].wait_recv()
                load(wg_ref.at[sid_ref[k + 1]])

        @pl.when(n == 0)
        def _():
            is_lat = (i * tm + lax.broadcasted_iota(jnp.int32, (tm, 1), 0)) < n_lat
            u, _ = _modulate(x_ref, mod_ref, is_lat)
            u_s[rows, :] = u.astype(BF)
            u_ref[...] = u.astype(BF)

        z_ref[...] = _dot(u_s[rows, :], w_s[...]) + b_ref[...]

        @pl.when((n == NSH - 1) & (i == nt - 1))
        def _():
            _ag_finish(*_ag_copies(*ag), done=NSH - 1)

    first = lambda n, i: jnp.where(n == 0, i, nt - 1)
    outs = pl.pallas_call(
        body, grid_spec=pltpu.PrefetchScalarGridSpec(
            num_scalar_prefetch=1, grid=(NSH, nt),
            in_specs=[pl.BlockSpec((tm, D), lambda n, i, sid: (first(n, i), 0)),
                      pl.BlockSpec((16, 3 * D), lambda n, i, sid: (0, 0)),
                      pl.BlockSpec((1, SHC), lambda n, i, sid: (0, sid[n])), _ANY, _ANY],
            out_specs=[pl.BlockSpec((tm, SHC), lambda n, i, sid: (i, sid[n])),
                       pl.BlockSpec((tm, D), lambda n, i, sid: (first(n, i), 0)), _ANY, _ANY],
            scratch_shapes=[pltpu.VMEM((D, SHC), BF), pltpu.VMEM((m, D), BF)] + _ag_sems(2)
            + [pltpu.SemaphoreType.DMA]),
        out_shape=[jax.ShapeDtypeStruct((m, IN_COLS), F32), jax.ShapeDtypeStruct((m, D), BF),
                   jax.ShapeDtypeStruct((NSH,) + w_in_sh.shape, BF),
                   jax.ShapeDtypeStruct((NSH,) + narrow_sh.shape, narrow_sh.dtype)],
        name="inproj_ag", compiler_params=_cp(("arbitrary", "arbitrary")))(sids, x_all, mod, b_in, w_in_sh, narrow_sh)
    return outs


def _rs_sibling(grads, name):
    n = len(grads)

    def body(*refs):
        ins, got = refs[:n], refs[n:2 * n]
        send, recv = refs[2 * n:]
        x, y, c = _place()
        copies = []
        for j in range(n):
            hr = ins[j].shape[1] // 2
            for s in range(NSH):
                give = ins[j].at[s, pl.ds(pl.multiple_of((1 - c) * hr, 8), hr), :]
                cp = pltpu.make_async_remote_copy(src_ref=give, dst_ref=got[j].at[s], send_sem=send.at[NSH * j + s],
                                                  recv_sem=recv.at[NSH * j + s], device_id=(x, y, 1 - c),
                                                  device_id_type=MESH)
                cp.start()
                copies.append(cp)
        for cp in copies:
            cp.wait()

    half = [jax.ShapeDtypeStruct((NSH, g.shape[1] // 2, g.shape[2]), F32) for g in grads]
    return pl.pallas_call(
        body, in_specs=[_ANY] * n, out_specs=[_ANY] * n, out_shape=half,
        scratch_shapes=[pltpu.SemaphoreType.DMA((NSH * n,)), pltpu.SemaphoreType.DMA((NSH * n,))],
        name=name)(*grads)


def _core_vec():
    return lax.axis_index("c").astype(jnp.int32).reshape(1)


def _rs_add1(g, got, name):
    _, r, cols = g.shape
    hr = r // 2
    tr = _row_tile(hr, cols)
    nb = hr // tr

    def body(c_ref, g_ref, got_ref, o_ref):
        o_ref[...] = (g_ref[...] + got_ref[...]).astype(BF)

    spec = pl.BlockSpec((1, tr, cols), lambda s, i, c_ref: (s, i, 0))
    return pl.pallas_call(
        body, grid_spec=pltpu.PrefetchScalarGridSpec(
            num_scalar_prefetch=1, grid=(NSH, nb),
            in_specs=[pl.BlockSpec((1, tr, cols), lambda s, i, c_ref: (s, c_ref[0] * nb + i, 0)), spec],
            out_specs=spec),
        out_shape=jax.ShapeDtypeStruct((NSH, hr, cols), BF), name=name,
        compiler_params=_cp(("arbitrary", "arbitrary")))(_core_vec(), g, got)


def _rs_add2(sums, got, name):
    _, hr, cols = sums.shape
    tr = _row_tile(hr, cols)
    nb = hr // tr
    place = jnp.stack([2 * lax.axis_index("x") + lax.axis_index("y"), lax.axis_index("c")]).astype(jnp.int32)

    def body(p_ref, s_ref, got_ref, o_ref):
        f = lambda v: v.astype(F32)
        o_ref[...] = f(s_ref[0]) + f(got_ref[0]) + f(got_ref[1]) + f(got_ref[2])

    return pl.pallas_call(
        body, grid_spec=pltpu.PrefetchScalarGridSpec(
            num_scalar_prefetch=1, grid=(nb,),
            in_specs=[pl.BlockSpec((1, tr, cols), lambda i, p_ref: (p_ref[0], i, 0)),
                      pl.BlockSpec((3, tr, cols), lambda i, p_ref: (0, i, 0))],
            out_specs=pl.BlockSpec((tr, cols), lambda i, p_ref: (p_ref[1] * nb + i, 0))),
        out_shape=jax.ShapeDtypeStruct((2 * hr, cols), F32), name=name,
        compiler_params=_cp(("arbitrary",)))(place, sums, got)


def _rs_chip_copies(ins, got, send, recv):
    x, y, c = _place()
    peers = ((1 - x, y), (x, 1 - y), (1 - x, 1 - y))
    return [pltpu.make_async_remote_copy(src_ref=ins[j].at[2 * px + py], dst_ref=got[j].at[k],
                                         send_sem=send.at[3 * j + k], recv_sem=recv.at[3 * j + k],
                                         device_id=(px, py, c), device_id_type=MESH)
            for j in range(len(ins)) for k, (px, py) in enumerate(peers)]


def _ag_sibling(fulls):
    n = len(fulls)
    nck = 4

    def body(*refs):
        outs = refs[n:2 * n]
        send, recv = refs[2 * n:]
        x, y, c = _place()
        copies = []
        for j in range(n):
            qr = outs[j].shape[0] // (2 * nck)
            for k in range(nck):
                rows = outs[j].at[pl.ds(pl.multiple_of((c * nck + k) * qr, 8), qr), :]
                cp = pltpu.make_async_remote_copy(src_ref=rows, dst_ref=rows, send_sem=send.at[nck * j + k],
                                                  recv_sem=recv.at[nck * j + k], device_id=(x, y, 1 - c),
                                                  device_id_type=MESH)
                cp.start()
                copies.append(cp)
        for cp in copies:
            cp.wait()

    return pl.pallas_call(
        body, in_specs=[_ANY] * n, out_specs=[_ANY] * n,
        out_shape=[jax.ShapeDtypeStruct(f.shape, F32) for f in fulls],
        input_output_aliases={j: j for j in range(n)},
        scratch_shapes=[pltpu.SemaphoreType.DMA((nck * n,)), pltpu.SemaphoreType.DMA((nck * n,))],
        name="ag_sibling")(*fulls)


def _allreduce_small(buf):
    rows = buf.shape[0]
    pr = rows // 8

    def body(in_ref, out_ref, stage, send1, recv1, send2, recv2):
        x, y, c = _place()
        me = 4 * x + 2 * y + c

        def peer(k):
            kx, ky, kc = (k >> 2) & 1, (k >> 1) & 1, k & 1
            return (x ^ kx, y ^ ky, c ^ kc)

        def piece(ref, idx):
            return ref.at[pl.ds(pl.multiple_of(idx * pr, 8), pr), :]

        copies = []
        for k in range(1, 8):
            px, py, pc = peer(k)
            cp = pltpu.make_async_remote_copy(src_ref=piece(in_ref, 4 * px + 2 * py + pc), dst_ref=stage.at[k],
                                              send_sem=send1.at[k], recv_sem=recv1.at[k],
                                              device_id=(px, py, pc), device_id_type=MESH)
            cp.start()
            copies.append(cp)
        for cp in copies:
            cp.wait()
        acc = piece(in_ref, me)[...]
        for k in range(1, 8):
            acc = acc + stage[k]
        piece(out_ref, me)[...] = acc
        copies = []
        for k in range(1, 8):
            cp = pltpu.make_async_remote_copy(src_ref=piece(out_ref, me), dst_ref=piece(out_ref, me),
                                              send_sem=send2.at[k], recv_sem=recv2.at[k],
                                              device_id=peer(k), device_id_type=MESH)
            cp.start()
            copies.append(cp)
        for cp in copies:
            cp.wait()

    vm = pl.BlockSpec(memory_space=pltpu.VMEM)
    return pl.pallas_call(
        body, in_specs=[vm], out_specs=vm, out_shape=jax.ShapeDtypeStruct((rows, D), F32),
        scratch_shapes=[pltpu.VMEM((8, pr, D), F32)] + [pltpu.SemaphoreType.DMA((8,))] * 4,
        name="allreduce_small", compiler_params=_cp())(buf)


def _rows(a):
    flat = a.reshape(-1)
    pad = (-flat.shape[0]) % D
    if pad:
        flat = jnp.concatenate([flat, jnp.zeros((pad,), flat.dtype)])
    return flat.reshape(-1, D)


def _pad_rows(a, mult):
    pad = (-a.shape[0]) % mult
    return jnp.concatenate([a, jnp.zeros((pad, a.shape[1]), a.dtype)]) if pad else a


def _local_step(x, c, ctx, c_ctx, tgt, me, shard, sh, b_mod, b_in, norm_g, cb, wr, wi, ln_g, ln_b):
    t_lat, t_ctx = x.shape[0], ctx.shape[0]
    nlt = t_lat // 256
    nlb, ncb = t_lat // RB, t_ctx // RB
    mc = 3 * D // NSH
    mod_all, cc_all = _mod_tp(jnp.zeros((8, D), F32).at[0].set(c), c_ctx.reshape(1, D), sh["w_mod"],
                              lax.dynamic_slice_in_dim(b_mod, shard * mc, mc, axis=1))
    mod = jnp.zeros((16, 3 * D), F32).at[0].set(mod_all[me]).at[1].set(mod_all[8])
    cc = jnp.zeros((16, D), F32).at[0].set(c).at[1].set(c_ctx)
    x_all = jnp.concatenate([x, ctx], axis=0)
    z, u_all, w_in_g, nar = _inproj_ag(x_all, mod, sh["w_in"], b_in, sh["narrow"], nlt)
    nar = jnp.transpose(nar, (1, 0, 2)).reshape(-1, D)
    lbl, cw, br, bi, lam = nar[0:4].reshape(2, 2, D), nar[4:8], nar[8:10], nar[10:12], nar[12:14]
    o0, st0, (w_mod_g, p_a, p_b, w_out) = _gla_fwd(z, lbl, 0, nlb, ncb,
                                                   gather=[sh[k] for k in ("w_mod", "p_a", "p_b", "w_out")])
    p_a, p_b, w_out = p_a.reshape(D, D), p_b.reshape(D, D), w_out.reshape(D, D)
    o1, st1, _ = _gla_fwd(z, lbl, 1, nlb, ncb)
    h0, hp0, a0, hpc0 = _rglru_fwd(z, cw, cb, wr, br, wi, bi, lam, 0, t_lat, t_ctx)
    h1, hp1, a1, hpc1 = _rglru_fwd(z, cw, cb, wr, br, wi, bi, lam, 1, t_lat, t_ctx)
    (do, dh, dz4a, dz4b, dz6a, dz_sh3, gxres, y, dout, oa, dpa, obv, dpb, acc) = _merge(
        o0, o1, h0, h1, z, x_all, tgt, mod, norm_g, ln_g, ln_b, p_a, p_b, w_out, t_lat)
    gp_a = _wgrad(oa, dpa, "wgrad_pa")
    gp_b = _wgrad(obv, dpb, "wgrad_pb")
    gw_out = _wgrad(y, dout, "wgrad_wout")
    dxc_lat, dxc_ctx, dwr0, dwi0, sb0 = _rglru_bwd(z, cw, cb, wr, br, wi, bi, lam, dh, hp0, a0, hpc0, 0, t_lat, t_ctx)
    dz5_lat, dz5_ctx, dwr1, dwi1, sb1 = _rglru_bwd(z, cw, cb, wr, br, wi, bi, lam, dh, hp1, a1, hpc1, 1, t_lat, t_ctx,
                                                   prev=(dxc_lat, dxc_ctx))
    dq0, dv0, dz1, sa0 = _gla_bwd(z, lbl, do, st0, 0, nlb, ncb)
    dz_sh0, dz_sh1, sa1 = _gla_bwd(z, lbl, do, st1, 1, nlb, ncb, prev=(dq0, dv0, dz1, dz4a))
    dz = (dz_sh0, dz_sh1, _pack_shard2(dz4b, dz5_lat, dz5_ctx, dz6a), dz_sh3)
    grads = [_wgrad_in(u_all, dz)] + [g.reshape(NSH, D // NSH, D) for g in (gp_a, gp_b, gw_out)]
    sums = [_rs_add1(g, b, f"rs_add1_{j}") for j, (g, b) in enumerate(zip(grads, _rs_sibling(grads, "rs_sibling")))]
    gx, dm, got = _du(dz, w_in_g, x_all, mod, gxres, nlt, sums)
    fulls = [_rs_add2(a, b, f"rs_add2_{j}") for j, (a, b) in enumerate(zip(sums, got))]
    big = dict(zip(_RS, _ag_sibling(fulls)))
    dmod = jnp.zeros((16, 3 * D), F32).at[0:2, 0:2 * D].set(dm[0:2]).at[0, 2 * D:].set(acc[0])
    dcc = _mod_bwd(cc, dmod, w_mod_g)
    small = dict(
        c_ctx=dcc[1:2], b_mod=(dmod[0] + dmod[1]).reshape(3, D),
        b_in=jnp.stack([sa1[2], sa0[0], sa1[0], sa1[3], acc[3], sb1[4], acc[4], acc[5], acc[6]]),
        lb_logits=jnp.stack([sa0[1], sa1[1], -sa0[1], -sa1[1]]),
        norm_a_g=acc[7:8], conv_w=sb1[8:12], conv_b=sb1[3:4],
        w_r=jnp.stack([dwr0, dwr1]).reshape(-1, D), w_i=jnp.stack([dwi0, dwi1]).reshape(-1, D),
        b_r=jnp.stack([sb0[0], sb1[0]]), b_i=jnp.stack([sb0[1], sb1[1]]), lam=jnp.stack([sb0[2], sb1[2]]),
        ln_g=acc[1:2], ln_b=acc[2:3])
    return acc[9, 0], gx, big, small, dmod, cc_all


_RS =("w_in", "p_a", "p_b", "w_out")
_SMALL =("c_ctx", "b_mod", "b_in", "lb_logits", "norm_a_g", "conv_w", "conv_b", "w_r", "w_i", "b_r", "b_i", "lam",
          "ln_g", "ln_b")
_BIG = ("w_mod", "w_in", "p_a", "p_b", "w_out")
_COL_SHARDED = ("lb_logits", "conv_w", "b_r", "b_i", "lam")
_WEIGHTS = ("c_ctx", "w_mod", "b_mod", "w_in", "b_in", "lb_logits", "norm_a_g", "conv_w", "conv_b", "w_r", "b_r", "w_i",
            "b_i", "lam", "p_a", "p_b", "w_out", "ln_g", "ln_b")


def kernel(x, c, ctx, c_ctx, w_mod, b_mod, w_in, b_in, lb_logits, norm_a_g, conv_w, conv_b, w_r, b_r, w_i, b_i, lam, p_a, p_b, w_out, ln_g, ln_b, loss_target, m_c_ctx, m_w_mod, m_b_mod, m_w_in, m_b_in, m_lb_logits, m_norm_a_g, m_conv_w, m_conv_b, m_w_r, m_b_r, m_w_i, m_b_i, m_lam, m_p_a, m_p_b, m_w_out, m_ln_g, m_ln_b, v_c_ctx, v_w_mod, v_b_mod, v_w_in, v_b_in, v_lb_logits, v_norm_a_g, v_conv_w, v_conv_b, v_w_r, v_b_r, v_w_i, v_b_i, v_lam, v_p_a, v_p_b, v_w_out, v_ln_g, v_ln_b):
    w = dict(c_ctx=c_ctx, w_mod=w_mod, b_mod=b_mod, w_in=w_in, b_in=b_in, lb_logits=lb_logits, norm_a_g=norm_a_g,
             conv_w=conv_w, conv_b=conv_b, w_r=w_r, b_r=b_r, w_i=w_i, b_i=b_i, lam=lam, p_a=p_a, p_b=p_b, w_out=w_out,
             ln_g=ln_g, ln_b=ln_b)
    m = dict(c_ctx=m_c_ctx, w_mod=m_w_mod, b_mod=m_b_mod, w_in=m_w_in, b_in=m_b_in, lb_logits=m_lb_logits,
             norm_a_g=m_norm_a_g, conv_w=m_conv_w, conv_b=m_conv_b, w_r=m_w_r, b_r=m_b_r, w_i=m_w_i, b_i=m_b_i,
             lam=m_lam, p_a=m_p_a, p_b=m_p_b, w_out=m_w_out, ln_g=m_ln_g, ln_b=m_ln_b)
    v = dict(c_ctx=v_c_ctx, w_mod=v_w_mod, b_mod=v_b_mod, w_in=v_w_in, b_in=v_b_in, lb_logits=v_lb_logits,
             norm_a_g=v_norm_a_g, conv_w=v_conv_w, conv_b=v_conv_b, w_r=v_w_r, b_r=v_b_r, w_i=v_w_i, b_i=v_b_i,
             lam=v_lam, p_a=v_p_a, p_b=v_p_b, w_out=v_w_out, ln_g=v_ln_g, ln_b=v_ln_b)
    shard = 2 * lax.axis_index("x") + lax.axis_index("y")
    cs = D // NSH

    sh = {k: w[k][0].astype(BF) for k in _BIG}
    sh["narrow"] = _pad_rows(jnp.concatenate([lb_logits.reshape(4, cs), conv_w[0], b_r[0], b_i[0], lam[0]], axis=0), 8)
    me = 2 * shard + lax.axis_index("c")
    loss, gx, big, small, dmod, cc_all = _local_step(
        x[0], c[0], ctx[0], c_ctx, loss_target[0], me, shard, sh, b_mod, b_in, norm_a_g, conv_b, w_r[0], w_i[0],
        ln_g, ln_b)
    loss = lax.psum(loss, ("x", "y", "c"))

    dmod_rows = jnp.zeros((16, 3 * D), F32).at[me].set(dmod[0]).at[8].set(dmod[1]).reshape(48, D)
    sizes = [small[k].shape[0] for k in _SMALL]
    red = _allreduce_small(_pad_rows(jnp.concatenate([_pad_rows(small[k], 8) for k in _SMALL] + [dmod_rows],
                                                     axis=0), 64))
    grads = {}
    off = 0
    for k, n in zip(_SMALL, sizes):
        g = red[off:off + n]
        off += n + (-n) % 8
        if k == "norm_a_g":
            g = g[:, :DH]
        if k in _COL_SHARDED:
            g = lax.dynamic_slice_in_dim(g, shard * cs, cs, axis=1)
        grads[k] = g.reshape(w[k].shape)
    for k in _RS:
        grads[k] = big[k].reshape(w[k].shape)
    dmod_all = red[off:off + 48].reshape(16, 3 * D)
    mc = 3 * D // NSH
    grads["w_mod"] = _wmod_grad(cc_all, lax.dynamic_slice_in_dim(dmod_all, shard * mc, mc, axis=1)).reshape(
        w["w_mod"].shape)

    delta, new_m, new_v = {}, {}, {}
    for k in _BIG:
        shp = w[k].shape
        two = lambda a: a.reshape(shp[-2], shp[-1])
        d_, m_, v_ = _adamw(two(w[k]), two(grads[k]), two(m[k]), two(v[k]), f"adamw_{k}")
        delta[k], new_m[k], new_v[k] = d_.reshape(shp), m_.reshape(shp), v_.reshape(shp)
    d_, m_, v_ = _adamw_many(*[[t[k] for k in _SMALL] for t in (w, grads, m, v)])
    delta.update(zip(_SMALL, d_))
    new_m.update(zip(_SMALL, m_))
    new_v.update(zip(_SMALL, v_))

    return (loss, gx[None], *[grads[k] for k in _WEIGHTS], *[delta[k] for k in _WEIGHTS],
            *[new_m[k] for k in _WEIGHTS], *[new_v[k] for k in _WEIGHTS])
```

```python
import functools

import jax
import jax.numpy as jnp
from jax import lax
from jax.experimental import pallas as pl
from jax.experimental.pallas import tpu as pltpu

F32 = jnp.float32
BF = jnp.bfloat16
MESH = pl.DeviceIdType.MESH

D = 1024
H = 8
DH = 128
CH = 64
RB = 256
NCK = RB // CH
GRID_W = 64
CB = 8
RCH = 16
IN_COLS = 9 * D
NSH = 4
SHC = IN_COLS // NSH
CUT2 = SHC - 2 * D
CUT4 = 2 * SHC - 4 * D
CUT6 = 3 * SHC - 6 * D
RG_C = 8.0
ALPHA = 2.0 ** 0.25
LN_EPS = 1e-5
RMS_EPS = 1e-6
Q_SCALE = DH ** -0.5
ADAM_LR, ADAM_B1, ADAM_B2, ADAM_EPS, ADAM_WD, ADAM_STEP = 0.001, 0.9, 0.999, 1e-08, 0.01, 10
VMEM_LIMIT = 56 * 1024 * 1024
VMEM_LIMIT_MERGE = 60 * 1024 * 1024
VMEM_LIMIT_RGLRU_BWD = 63 * 1024 * 1024


def _cp(sem=None, vmem=VMEM_LIMIT):
    return pltpu.CompilerParams(dimension_semantics=sem, vmem_limit_bytes=vmem)


def _sig(x):
    return 0.5 * jnp.tanh(0.5 * x) + 0.5


def _dot(a, b):
    return jnp.dot(a.astype(BF), b.astype(BF), preferred_element_type=F32)


def _dot_nt(a, b):
    return lax.dot_general(a.astype(BF), b.astype(BF), (((1,), (1,)), ((), ())), preferred_element_type=F32)


def _dot_tn(a, b):
    return lax.dot_general(a.astype(BF), b.astype(BF), (((0,), (0,)), ((), ())), preferred_element_type=F32)


def _colsum(v):
    return jnp.sum(v, axis=0, keepdims=True)


def _mod_bwd(cc, dmod, w_mod_g):
    def body(cc_ref, dm_ref, w_ref, dcc_ref):
        v = cc_ref[...]
        sg = _sig(v)
        ds = jnp.zeros((16, D), F32)
        for k in range(NSH):
            ds = ds + _dot_nt(dm_ref[:, k * 768:(k + 1) * 768], w_ref[k])
        dcc_ref[...] = ds * (sg * (1.0 + v * (1.0 - sg)))
    return pl.pallas_call(body, out_shape=jax.ShapeDtypeStruct((16, D), F32),
                          name="mod_bwd", compiler_params=_cp())(cc, dmod, w_mod_g)


def _wmod_grad(cc, dmod_cols):
    def body(cc_ref, dm_ref, dw_ref):
        v = cc_ref[...]
        dw_ref[...] = _dot_tn(v * _sig(v), dm_ref[...])
    return pl.pallas_call(body, out_shape=jax.ShapeDtypeStruct((D, dmod_cols.shape[1]), F32),
                          name="wmod_grad", compiler_params=_cp())(cc, dmod_cols)


def _modulate(x_ref, mod_ref, is_lat):
    sh = jnp.where(is_lat, mod_ref[0:1, 0:D], mod_ref[1:2, 0:D])
    sc = jnp.where(is_lat, mod_ref[0:1, D:2 * D], mod_ref[1:2, D:2 * D])
    return x_ref[...] * (1.0 + sc) + sh, sc


def _gla_mask(d, n):
    row = lax.broadcasted_iota(jnp.int32, (n, n), 0)
    col = lax.broadcasted_iota(jnp.int32, (n, n), 1)
    same = (row // CH) == (col // CH)
    return same & ((row <= col) if d else (row >= col))


def _chunk_cumsum(v, rev):
    n = v.shape[0]
    pos = lax.broadcasted_iota(jnp.int32, v.shape, 0) & (CH - 1)
    s = 1
    while s < CH:
        if rev:
            v = v + jnp.where(pos < CH - s, pltpu.roll(v, n - s, 0), 0.0)
        else:
            v = v + jnp.where(pos >= s, pltpu.roll(v, s, 0), 0.0)
        s *= 2
    return v


def _chunk_rows(c):
    return slice(c * CH, (c + 1) * CH)


def _gla_features(zq, zf, lb, d):
    sq = _sig(zq)
    q = zq * sq * Q_SCALE
    sf = _sig(zf)
    f = lb + (1.0 - lb) * sf
    k = 1.0 - f
    g = _chunk_cumsum(jnp.log(f), d)
    last = 0 if d else CH - 1
    gls = [g[c * CH + last:c * CH + last + 1, :] for c in range(NCK)]
    glb = jnp.concatenate([jnp.broadcast_to(gl, (CH, D)) for gl in gls], axis=0)
    eg, eig, eeg = jnp.exp(g), jnp.exp(-g), jnp.exp(glb - g)
    decs = [jnp.exp(gl) for gl in gls]
    return sq, sf, f, q * eg, k * eig, k * eeg, eg, eig, eeg, decs


def _lower_bound(lbl_ref, d):
    return _sig(lbl_ref[0, d:d + 1, :] - lbl_ref[1, d:d + 1, :])


def _gla_rb(d, nlb, ncb):
    nrb = nlb + ncb
    if d:
        return lambda s: nrb - 1 - s
    return lambda s: jnp.where(s < ncb, nlb + s, s - ncb)


def _gla_fwd(z, lbl, d, nlb, ncb, gather=()):
    m = z.shape[0]
    nrb = nlb + ncb
    rb = _gla_rb(d, nlb, ncb)
    ng = len(gather)

    def body(*refs):
        q_ref, f_ref, v_ref, lbl_ref = refs[:4]
        ag_in = refs[4:4 + ng]
        o_ref, st_ref = refs[4 + ng:6 + ng]
        ag_out = refs[6 + ng:6 + 2 * ng]
        S = refs[6 + 2 * ng]
        ag_sems = refs[7 + 2 * ng:]
        s = pl.program_id(0)

        @pl.when(s == 0)
        def _():
            S[...] = jnp.zeros_like(S)
            if ng:
                local, chip, _ = _ag_copies(ag_in, ag_out, *ag_sems)
                for cp in chip + local:
                    cp.start()

        lb = _lower_bound(lbl_ref, d)
        mb = _gla_mask(d, CH)
        _, _, _, qd, ki, ke, _, _, _, decs = _gla_features(q_ref[...], f_ref[...], lb, d)
        qd, ki, ke, v = qd.astype(BF), ki.astype(BF), ke.astype(BF), v_ref[...].astype(BF)
        order = range(NCK - 1, -1, -1) if d else range(NCK)
        for h in range(H):
            hs = slice(h * DH, (h + 1) * DH)
            intra, upd = {}, {}
            for c in range(NCK):
                rs = _chunk_rows(c)
                a = jnp.where(mb, _dot_nt(qd[rs, hs], ki[rs, hs]), 0.0)
                intra[c] = _dot(a, v[rs, hs])
                upd[c] = _dot_tn(v[rs, hs], ke[rs, hs])
            st = S[h]
            for c in order:
                rs = _chunk_rows(c)
                st_ref[c, h] = st
                o_ref[rs, hs] = intra[c] + _dot_nt(qd[rs, hs], st)
                st = st * decs[c][:, hs] + upd[c]
            S[h] = st

        if ng:
            @pl.when(s == nrb - 1)
            def _():
                _ag_finish(*_ag_copies(ag_in, ag_out, *ag_sems))

    def zspec(cb):
        return pl.BlockSpec((RB, D), lambda s: (rb(s), cb))

    outs = pl.pallas_call(
        body, grid=(nrb,),
        in_specs=[zspec(0), zspec(1 + d), zspec(3), pl.BlockSpec((2, 2, D), lambda s: (0, 0, 0))] + [_ANY] * ng,
        out_specs=[pl.BlockSpec((RB, D), lambda s: (rb(s), 0)),
                   pl.BlockSpec((NCK, H, DH, DH), lambda s: (rb(s), 0, 0, 0))] + [_ANY] * ng,
        out_shape=[jax.ShapeDtypeStruct((m, D), F32), jax.ShapeDtypeStruct((m // CH, H, DH, DH), F32)]
        + [jax.ShapeDtypeStruct((NSH,) + g.shape, g.dtype) for g in gather],
        scratch_shapes=[pltpu.VMEM((H, DH, DH), F32)] + (_ag_sems(ng) if ng else []),
        name=f"gla_fwd{d}", compiler_params=_cp(("arbitrary",)))(z, z, z, lbl, *gather)
    return outs[0], outs[1], list(outs[2:])


def _gla_bwd(z, lbl, do_lat, states, d, nlb, ncb, prev=None):
    m = z.shape[0]
    nrb = nlb + ncb
    fwd_rb = _gla_rb(d, nlb, ncb)
    rb = lambda s: fwd_rb(nrb - 1 - s)
    last = prev is not None

    def body(*refs):
        if last:
            (q_ref, f_ref, v_ref, lbl_ref, do_ref, st_ref, pq_ref, pv_ref, dz1_ref, dz4a_ref, sh0_ref, sh1_ref, sum_ref,
             dS) = refs
        else:
            q_ref, f_ref, v_ref, lbl_ref, do_ref, st_ref, o0_ref, o1_ref, o2_ref, sum_ref, dS = refs
        s = pl.program_id(0)
        is_lat = rb(s) < nlb

        @pl.when(s == 0)
        def _():
            dS[...] = jnp.zeros_like(dS)
            sum_ref[...] = jnp.zeros_like(sum_ref)

        lb = _lower_bound(lbl_ref, d)
        mb = _gla_mask(d, RB)
        zq = q_ref[...]
        sq, sf, f, qd, ki, ke, eg, eig, eeg, decs = _gla_features(zq, f_ref[...], lb, d)
        qdb, kib, keb, vb = qd.astype(BF), ki.astype(BF), ke.astype(BF), v_ref[...].astype(BF)
        dob = jnp.where(is_lat, do_ref[...], 0.0).astype(BF)
        order = range(NCK) if d else range(NCK - 1, -1, -1)
        dqd_h, dki_h, dke_h, dv_h, ddec_h = [], [], [], [], []
        for h in range(H):
            hs = slice(h * DH, (h + 1) * DH)
            a = jnp.where(mb, _dot_nt(qdb[:, hs], kib[:, hs]), 0.0).astype(BF)
            da = jnp.where(mb, _dot_nt(dob[:, hs], vb[:, hs]), 0.0).astype(BF)
            dqd_i = _dot(da, kib[:, hs])
            dki_h.append(_dot_tn(da, qdb[:, hs]))
            dvi = _dot_tn(a, dob[:, hs])
            dqd, inc = {}, {}
            for c in range(NCK):
                rs = _chunk_rows(c)
                dqd[c] = dqd_i[rs, :] + _dot(dob[rs, hs], st_ref[c, h])
                inc[c] = _dot_tn(dob[rs, hs], qdb[rs, hs])
            dst = dS[h]
            dke, dv, ddec = {}, {}, {}
            for c in order:
                rs = _chunk_rows(c)
                dv[c] = dvi[rs, :] + _dot_nt(keb[rs, hs], dst)
                dke[c] = _dot(vb[rs, hs], dst)
                ddec[c] = _colsum(st_ref[c, h] * dst)
                dst = inc[c] + dst * decs[c][:, hs]
            dS[h] = dst
            cat = lambda t: jnp.concatenate([t[c] for c in range(NCK)], axis=0)
            dqd_h.append(cat(dqd))
            dke_h.append(cat(dke))
            dv_h.append(cat(dv))
            ddec_h.append([ddec[c] for c in range(NCK)])
        lanes = lambda parts: jnp.concatenate(parts, axis=1)
        dqd, dki, dke, dv = lanes(dqd_h), lanes(dki_h), lanes(dke_h), lanes(dv_h)
        dq = dqd * eg
        dk = dki * eig + dke * eeg
        dke_ke = dke * ke
        dg = dqd * qd - dki * ki - dke_ke
        dgl = [_colsum(dke_ke[_chunk_rows(c), :]) + lanes([ddec_h[h][c] for h in range(H)]) * decs[c]
               for c in range(NCK)]
        dglb = jnp.concatenate([jnp.broadcast_to(t, (CH, D)) for t in dgl], axis=0)
        df = (_chunk_cumsum(dg, 1 - d) + dglb) / f - dk
        dzf = df * (1.0 - lb) * (sf * (1.0 - sf))
        sum_ref[0:1, :] += _colsum(dzf)
        sum_ref[1:2, :] += _colsum(df * (1.0 - sf))
        if last:
            dz0 = (dq + pq_ref[...]) * (Q_SCALE * (sq * (1.0 + zq * (1.0 - sq))))
            dz3 = dv + pv_ref[...]
            sum_ref[2:3, :] += _colsum(dz0)
            sum_ref[3:4, :] += _colsum(dz3)
            dz2 = dzf.astype(BF)
            sh0_ref[:, 0:D] = dz0.astype(BF)
            sh0_ref[:, D:2 * D] = dz1_ref[...]
            sh0_ref[:, 2 * D:] = dz2[:, :CUT2]
            sh1_ref[:, 0:D - CUT2] = dz2[:, CUT2:]
            sh1_ref[:, D - CUT2:2 * D - CUT2] = dz3.astype(BF)
            sh1_ref[:, 2 * D - CUT2:] = dz4a_ref[...]
        else:
            o0_ref[...] = dq
            o1_ref[...] = dv
            o2_ref[...] = dzf.astype(BF)

        @pl.when(s == nrb - 1)
        def _():
            sum_ref[1:2, :] = sum_ref[1:2, :] * (lb * (1.0 - lb))

    def zspec(cb):
        return pl.BlockSpec((RB, D), lambda s: (rb(s), cb))

    rowspec = pl.BlockSpec((RB, D), lambda s: (rb(s), 0))
    in_specs = [zspec(0), zspec(1 + d), zspec(3), pl.BlockSpec((2, 2, D), lambda s: (0, 0, 0)),
                pl.BlockSpec((RB, D), lambda s: (jnp.minimum(rb(s), nlb - 1), 0)),
                pl.BlockSpec((NCK, H, DH, DH), lambda s: (rb(s), 0, 0, 0))]
    args = [z, z, z, lbl, do_lat, states]
    sumspec = pl.BlockSpec((8, D), lambda s: (0, 0))
    if last:
        in_specs += [rowspec, rowspec, rowspec, pl.BlockSpec((RB, CUT4), lambda s: (rb(s), 0))]
        args += list(prev)
        shspec = pl.BlockSpec((RB, SHC), lambda s: (rb(s), 0))
        out_specs = (shspec, shspec, sumspec)
        out_shape = (jax.ShapeDtypeStruct((m, SHC), BF), jax.ShapeDtypeStruct((m, SHC), BF))
    else:
        out_specs = (rowspec, rowspec, rowspec, sumspec)
        out_shape = (jax.ShapeDtypeStruct((m, D), F32), jax.ShapeDtypeStruct((m, D), F32),
                     jax.ShapeDtypeStruct((m, D), BF))
    return pl.pallas_call(
        body, grid=(nrb,), in_specs=in_specs, out_specs=out_specs,
        out_shape=out_shape + (jax.ShapeDtypeStruct((8, D), F32),),
        scratch_shapes=[pltpu.VMEM((H, DH, DH), F32)],
        name=f"gla_bwd{d}", compiler_params=_cp(("arbitrary",)))(*args)


def _shift_rows(v, k, r):
    row = lax.broadcasted_iota(jnp.int32, v.shape, 0)
    rolled = pltpu.roll(v, k % r, 0)
    return jnp.where((row >= k) & (row < r + k), rolled, 0.0)


def _conv_fwd(xb, cw_ref, cb_ref, r):
    return (cb_ref[...] + _shift_rows(xb, 1, r) * cw_ref[0:1, :] + xb * cw_ref[1:2, :]
            + _shift_rows(xb, -1, r) * cw_ref[2:3, :] + _shift_rows(xb, -2, r) * cw_ref[3:4, :])


def _window(ref, lo, n, rows):
    parts = []
    if lo < 0:
        parts.append(jnp.zeros((-lo,) + tuple(ref.shape[1:]), F32))
    parts.append(ref[max(lo, 0):min(lo + n, rows)])
    if lo + n > rows:
        parts.append(jnp.zeros((lo + n - rows,) + tuple(ref.shape[1:]), F32))
    return parts[0] if len(parts) == 1 else jnp.concatenate(parts, axis=0)


def _conv_cols(x_ref, cw_ref, cb_ref, r0, n, rows):
    w = _window(x_ref, r0 - 1, n + 3, rows)
    return (cb_ref[...] + w[0:n] * cw_ref[0:1, :] + w[1:n + 1] * cw_ref[1:2, :] + w[2:n + 2] * cw_ref[2:3, :]
            + w[3:n + 3] * cw_ref[3:4, :])


def _softplus_neg(lam):
    y = jnp.exp(-jnp.abs(lam))
    u = 1.0 + y
    tiny = u == 1.0
    l1p = jnp.where(tiny, y, jnp.log(u) * (y / jnp.where(tiny, 1.0, u - 1.0)))
    return jnp.maximum(-lam, 0.0) + l1p


def _gates(xc, wr_ref, br, wi_ref, bi, sp):
    xcb = xc.astype(BF)
    rs, is_ = [], []
    for g in range(H):
        gs = slice(g * DH, (g + 1) * DH)
        rs.append(jnp.dot(xcb[:, gs], wr_ref[g].astype(BF), preferred_element_type=F32))
        is_.append(jnp.dot(xcb[:, gs], wi_ref[g].astype(BF), preferred_element_type=F32))
    r = _sig(jnp.concatenate(rs, axis=1) + br)
    i = _sig(jnp.concatenate(is_, axis=1) + bi)
    log_a = (-RG_C * r) * sp
    a = jnp.exp(log_a)
    t = jnp.tanh(log_a)
    om = (-2.0 * t) / (1.0 - t)
    return r, i, a, om


def _scan_rows(d, nrows, a_s, b_s, h_s, h0):
    nsl = nrows // 8

    def slab(j, h):
        jj = (nsl - 1 - j) if d else j
        r0 = pl.multiple_of(jj * 8, 8)
        for t in (range(7, -1, -1) if d else range(8)):
            h = a_s[pl.ds(r0 + t, 1), :] * h + b_s[pl.ds(r0 + t, 1), :]
            h_s[pl.ds(r0 + t, 1), :] = h
        return h

    return lax.fori_loop(0, nsl, slab, h0)


def _col_of(d, ncols):
    if d:
        return lambda s: ncols - jnp.maximum(s, 1)
    return lambda s: jnp.maximum(s, 1) - 1


def _rglru_fwd(z, cw, cb, wr, br, wi, bi, lam, d, t_lat, t_ctx):
    m = z.shape[0]
    rows = t_lat // GRID_W
    z3 = z.reshape(m // GRID_W, GRID_W, IN_COLS)
    nblk = GRID_W // CB
    blk = _col_of(d, nblk)
    cblk = t_lat // t_ctx
    rc = min(RCH, rows)

    def body(zc_ref, zx_ref, cw_ref, cb_ref, wr_ref, br_ref, wi_ref, bi_ref, lam_ref,
             hx_ref, hpx_ref, ax_ref, hpc_ref, a_s, b_s, h_s, hcar, a3, b3, cin_s):
        s = pl.program_id(0)
        sp = _softplus_neg(lam_ref[d:d + 1, :])
        br_ = br_ref[d:d + 1, :]
        bi_ = bi_ref[d:d + 1, :]

        @pl.when(s == 0)
        def _():
            xc = _conv_fwd(zc_ref[...], cw_ref, cb_ref, t_ctx)
            _, i, a, om = _gates(xc, wr_ref, br_, wi_ref, bi_, sp)
            a_s[...] = a
            b_s[...] = jnp.sqrt(om) * (i * xc)
            h0 = jnp.zeros((1, D), F32)
            hcar[0:1, :] = _scan_rows(d, t_ctx, a_s, b_s, h_s, h0)
            hs = h_s[...]
            row = lax.broadcasted_iota(jnp.int32, (t_ctx, D), 0)
            if d:
                hpc_ref[...] = jnp.where(row == t_ctx - 1, h0, pltpu.roll(hs, t_ctx - 1, 0))
            else:
                hpc_ref[...] = jnp.where(row == 0, h0, pltpu.roll(hs, 1, 0))

        @pl.when(s > 0)
        def _():
            for r0 in range(0, rows, rc):
                xc = _conv_cols(zx_ref, cw_ref, cb_ref, r0, rc, rows).reshape(rc * CB, D)
                _, i, a, om = _gates(xc, wr_ref, br_, wi_ref, bi_, sp)
                a3[r0:r0 + rc] = a.reshape(rc, CB, D)
                ax_ref[r0:r0 + rc] = a.reshape(rc, CB, D)
                b3[r0:r0 + rc] = (jnp.sqrt(om) * (i * xc)).reshape(rc, CB, D)

            def local(t, carry):
                hl, p = carry
                r = (rows - 1 - t) if d else t
                a = a3[r]
                hl = a * hl + b3[r]
                p = a * p
                b3[r] = hl
                a3[r] = p
                return hl, p

            hl, p = lax.fori_loop(0, rows, local, (jnp.zeros((CB, D), F32), jnp.ones((CB, D), F32)))
            cin = hcar[0:1, :]
            for j in (range(CB - 1, -1, -1) if d else range(CB)):
                cin_s[j:j + 1, :] = cin
                cin = hl[j:j + 1, :] + p[j:j + 1, :] * cin
            hcar[0:1, :] = cin
            c_in = cin_s[...]

            def fix(t, prev):
                r = (rows - 1 - t) if d else t
                h = b3[r] + a3[r] * c_in
                hx_ref[r] = h
                hpx_ref[r] = prev
                return h

            lax.fori_loop(0, rows, fix, c_in)

    full = lambda shp: pl.BlockSpec(shp, lambda s: (0,) * len(shp))
    colspec = pl.BlockSpec((rows, CB, D), lambda s: (0, blk(s), 0))
    outs = pl.pallas_call(
        body, grid=(nblk + 1,),
        in_specs=[pl.BlockSpec((t_ctx, D), lambda s: (cblk, 5)),
                  pl.BlockSpec((rows, CB, D), lambda s: (0, blk(s), 5)),
                  full((4, D)), full((1, D)),
                  pl.BlockSpec((None, H, DH, DH), lambda s: (d, 0, 0, 0)), full((2, D)),
                  pl.BlockSpec((None, H, DH, DH), lambda s: (d, 0, 0, 0)), full((2, D)), full((2, D))],
        out_specs=(colspec, colspec, colspec, full((t_ctx, D))),
        out_shape=(jax.ShapeDtypeStruct((rows, GRID_W, D), F32),) * 3 + (jax.ShapeDtypeStruct((t_ctx, D), F32),),
        scratch_shapes=[pltpu.VMEM((t_ctx, D), F32), pltpu.VMEM((t_ctx, D), F32), pltpu.VMEM((t_ctx, D), F32),
                        pltpu.VMEM((8, D), F32), pltpu.VMEM((rows, CB, D), F32), pltpu.VMEM((rows, CB, D), F32),
                        pltpu.VMEM((CB, D), F32)],
        name=f"rglru_fwd{d}", compiler_params=_cp(("arbitrary",)))(z, z3, cw, cb, wr, br, wi, bi, lam)
    return outs[0].reshape(t_lat, D), outs[1].reshape(t_lat, D), outs[2].reshape(t_lat, D), outs[3]


def _rglru_bwd(z, cw, cb, wr, br, wi, bi, lam, dh_lat, hp_lat, a_lat, hp_ctx, d, t_lat, t_ctx, prev=None):
    m = z.shape[0]
    rows = t_lat // GRID_W
    z3 = z.reshape(m // GRID_W, GRID_W, IN_COLS)
    nblk = GRID_W // CB
    fblk = _col_of(d, nblk)
    blk = lambda s: fblk(nblk - jnp.minimum(s, nblk - 1))
    cblk = t_lat // t_ctx
    rc = min(RCH, rows)
    last = prev is not None
    view3 = lambda v: v.reshape(rows, GRID_W, D)

    def body(*refs):
        (zc_ref, z3_any, cw_ref, cb_ref, wr_ref, br_ref, wi_ref, bi_ref, lam_ref, dh_ref, hp_any, ax_ref,
         hpc_ref) = refs[:13]
        k = 13
        if last:
            pd_any, pdc_ref = refs[13:15]
            k = 15
        ox_ref, oc_ref, dwr_ref, dwi_ref, sum_ref, a_s, b_s, gcar, a3, b3, cin_s, zx_ref, hpx_ref = refs[k:k + 13]
        pdx_ref, lsem = (refs[k + 13], refs[k + 14]) if last else (None, refs[k + 13])
        s = pl.program_id(0)

        def late_loads():
            cols = pl.ds(pl.multiple_of(blk(s) * CB, CB), CB)
            cps = [pltpu.make_async_copy(z3_any.at[pl.ds(0, rows), cols, pl.ds(5 * D, D)], zx_ref, lsem.at[0]),
                   pltpu.make_async_copy(hp_any.at[:, cols, :], hpx_ref, lsem.at[1])]
            if last:
                cps.append(pltpu.make_async_copy(pd_any.at[:, cols, :], pdx_ref, lsem.at[2]))
            return cps
        lam_d = lam_ref[d:d + 1, :]
        sp = _softplus_neg(lam_d)
        br_ = br_ref[d:d + 1, :]
        bi_ = bi_ref[d:d + 1, :]
        flat = lambda v: v.reshape(-1, D)

        @pl.when(s == 0)
        def _():
            gcar[...] = jnp.zeros_like(gcar)
            dwr_ref[...] = jnp.zeros_like(dwr_ref)
            dwi_ref[...] = jnp.zeros_like(dwi_ref)
            sum_ref[...] = jnp.zeros_like(sum_ref)


        def conv_sums(dxc, dxb, xm1, x0, xp1, xp2):
            sum_ref[3:4, :] += _colsum(flat(dxc))
            sum_ref[4:5, :] += _colsum(flat(dxb))
            sum_ref[8:9, :] += _colsum(flat(dxc * xm1))
            sum_ref[9:10, :] += _colsum(flat(dxc * x0))
            sum_ref[10:11, :] += _colsum(flat(dxc * xp1))
            sum_ref[11:12, :] += _colsum(flat(dxc * xp2))

        def gate_grads(g, hp, xc, rr, i, a, om):
            mult = jnp.sqrt(om)
            da = g * hp
            ixc = i * xc
            dmult = g * ixc
            dixc = g * mult
            di = dixc * xc
            dxc = dixc * i
            dlog_a = da * a - dmult * ((1.0 - om) / mult)
            dr = dlog_a * (-RG_C * sp)
            sum_ref[2:3, :] += _colsum(dlog_a * rr)
            drp = dr * rr * (1.0 - rr)
            dip = di * i * (1.0 - i)
            sum_ref[0:1, :] += _colsum(drp)
            sum_ref[1:2, :] += _colsum(dip)
            xcb = xc.astype(BF)
            drb = drp.astype(BF)
            dib = dip.astype(BF)
            parts = []
            for gi in range(H):
                gs = slice(gi * DH, (gi + 1) * DH)
                parts.append(_dot_nt(drb[:, gs], wr_ref[gi]) + _dot_nt(dib[:, gs], wi_ref[gi]))
                dwr_ref[gi] += _dot_tn(xcb[:, gs], drb[:, gs])
                dwi_ref[gi] += _dot_tn(xcb[:, gs], dib[:, gs])
            return dxc + jnp.concatenate(parts, axis=1)

        @pl.when(s < nblk)
        def _():
            loads = late_loads()
            for cp in loads:
                cp.start()

            def local(t, carry):
                c, q = carry
                r = t if d else (rows - 1 - t)
                a = ax_ref[r]
                c = a * (c + dh_ref[r])
                q = a * q
                b3[r] = c
                a3[r] = q
                return c, q

            c, q = lax.fori_loop(0, rows, local, (jnp.zeros((CB, D), F32), jnp.ones((CB, D), F32)))
            cin = gcar[0:1, :]
            for j in (range(CB) if d else range(CB - 1, -1, -1)):
                cin_s[j:j + 1, :] = cin
                cin = c[j:j + 1, :] + q[j:j + 1, :] * cin
            gcar[0:1, :] = cin
            c_in = cin_s[...]
            loads[0].wait()
            loads[1].wait()
            for r0 in (range(rows - rc, -1, -rc) if d else range(0, rows, rc)):
                if d:
                    lo = max(r0 - 1, 0)
                    cn = b3[lo:r0 + rc - 1] + a3[lo:r0 + rc - 1] * c_in
                    if r0 == 0:
                        cn = jnp.concatenate([c_in[None], cn], axis=0)
                else:
                    hi = min(r0 + rc + 1, rows)
                    cn = b3[r0 + 1:hi] + a3[r0 + 1:hi] * c_in
                    if hi == rows:
                        cn = jnp.concatenate([cn, c_in[None]], axis=0)
                g = flat(dh_ref[r0:r0 + rc] + cn)
                xc = flat(_conv_cols(zx_ref, cw_ref, cb_ref, r0, rc, rows))
                rr, i, a, om = _gates(xc, wr_ref, br_, wi_ref, bi_, sp)
                b3[r0:r0 + rc] = gate_grads(g, flat(hpx_ref[r0:r0 + rc]), xc, rr, i, a, om).reshape(rc, CB, D)
            if last:
                loads[2].wait()
                for r0 in range(0, rows, rc):
                    b3[r0:r0 + rc] = b3[r0:r0 + rc] + pdx_ref[r0:r0 + rc]
                for r0 in range(0, rows, rc):
                    w = _window(b3, r0 - 2, rc + 3, rows)
                    xw = _window(zx_ref, r0 - 1, rc + 3, rows)
                    dxc = w[2:rc + 2]
                    dxb = (w[3:rc + 3] * cw_ref[0:1, :] + dxc * cw_ref[1:2, :] + w[1:rc + 1] * cw_ref[2:3, :]
                           + w[0:rc] * cw_ref[3:4, :])
                    conv_sums(dxc, dxb, xw[0:rc], xw[1:rc + 1], xw[2:rc + 2], xw[3:rc + 3])
                    ox_ref[r0:r0 + rc] = dxb
            else:
                for r0 in range(0, rows, rc):
                    ox_ref[r0:r0 + rc] = b3[r0:r0 + rc]

        @pl.when(s == nblk)
        def _():
            r = t_ctx
            xb = zc_ref[...]
            xc = _conv_fwd(xb, cw_ref, cb_ref, r)
            rr, i, a, om = _gates(xc, wr_ref, br_, wi_ref, bi_, sp)
            a_s[...] = a
            b_s[...] = jnp.zeros((r, D), F32)
            c0 = gcar[0:1, :]
            _scan_rows(1 - d, r, a_s, b_s, b_s, c0)
            cs = b_s[...]
            row = lax.broadcasted_iota(jnp.int32, (r, D), 0)
            if d:
                g = jnp.where(row == 0, c0, pltpu.roll(cs, 1, 0))
            else:
                g = jnp.where(row == r - 1, c0, pltpu.roll(cs, r - 1, 0))
            dxc = gate_grads(g, hpc_ref[...], xc, rr, i, a, om)
            if last:
                dxc = dxc + pdc_ref[...]
                dxb = (_shift_rows(dxc, -1, r) * cw_ref[0:1, :] + dxc * cw_ref[1:2, :]
                       + _shift_rows(dxc, 1, r) * cw_ref[2:3, :] + _shift_rows(dxc, 2, r) * cw_ref[3:4, :])
                conv_sums(dxc, dxb, _shift_rows(xb, 1, r), xb, _shift_rows(xb, -1, r), _shift_rows(xb, -2, r))
                oc_ref[...] = dxb
            else:
                oc_ref[...] = dxc
            sum_ref[2:3, :] = sum_ref[2:3, :] * (RG_C * _sig(-lam_d))

    full = lambda shp: pl.BlockSpec(shp, lambda s: (0,) * len(shp))
    once = lambda shp: pl.BlockSpec(shp, lambda s: (0,) * len(shp), pipeline_mode=pl.Buffered(1))
    colspec = pl.BlockSpec((rows, CB, D), lambda s: (0, blk(s), 0))
    colonce = pl.BlockSpec((rows, CB, D), lambda s: (0, blk(s), 0), pipeline_mode=pl.Buffered(1))
    in_specs = [pl.BlockSpec((t_ctx, D), lambda s: (cblk, 5), pipeline_mode=pl.Buffered(1)), _ANY,
                full((4, D)), full((1, D)),
                pl.BlockSpec((None, H, DH, DH), lambda s: (d, 0, 0, 0)), full((2, D)),
                pl.BlockSpec((None, H, DH, DH), lambda s: (d, 0, 0, 0)), full((2, D)), full((2, D)),
                colspec, _ANY, colspec, once((t_ctx, D))]
    args = [z, z3, cw, cb, wr, br, wi, bi, lam, view3(dh_lat), view3(hp_lat), view3(a_lat), hp_ctx]
    if last:
        in_specs += [_ANY, once((t_ctx, D))]
        args += [view3(prev[0]), prev[1]]
    block = pltpu.VMEM((rows, CB, D), F32)
    outs = pl.pallas_call(
        body, grid=(nblk + 1,), in_specs=in_specs,
        out_specs=(colspec, full((t_ctx, D)), full((H, DH, DH)), full((H, DH, DH)), full((16, D))),
        out_shape=(jax.ShapeDtypeStruct((rows, GRID_W, D), F32), jax.ShapeDtypeStruct((t_ctx, D), F32),
                   jax.ShapeDtypeStruct((H, DH, DH), F32), jax.ShapeDtypeStruct((H, DH, DH), F32),
                   jax.ShapeDtypeStruct((16, D), F32)),
        scratch_shapes=[pltpu.VMEM((t_ctx, D), F32), pltpu.VMEM((t_ctx, D), F32), pltpu.VMEM((8, D), F32),
                        block, block, pltpu.VMEM((CB, D), F32), block, block] + ([block] if last else [])
        + [pltpu.SemaphoreType.DMA((3,))],
        name=f"rglru_bwd{d}", compiler_params=_cp(("arbitrary",), VMEM_LIMIT_RGLRU_BWD))(*args)
    return (outs[0].reshape(t_lat, D), outs[1]) + tuple(outs[2:])


def _merge(o_f, o_b, h_f, h_b, z, x_all, tgt, mod, norm_g, ln_g, ln_b, p_a, p_b, w_out, t_lat):
    tm = 256
    nt = t_lat // tm

    def body(of_ref, ob_ref, hf_ref, hb_ref, z4_ref, z6_ref, z7_ref, z8_ref, x_ref, t_ref, mod_ref, ng_ref,
             lg_ref, lb_ref, pa_ref, pb_ref, wo_ref,
             do_ref, dh_ref, dz4a_ref, dz4b_ref, dz6a_ref, sh3_ref, gx_ref,
             y_ref, dout_ref, oa_ref, dpa_ref, obv_ref, dpb_ref, acc_ref):
        i = pl.program_id(0)
        lat = i < nt
        latf = lat.astype(F32)

        @pl.when(i == 0)
        def _():
            acc_ref[...] = jnp.zeros_like(acc_ref)

        def per_head(v):
            return jnp.concatenate(
                [jnp.broadcast_to(jnp.mean(v[:, h * DH:(h + 1) * DH], axis=-1, keepdims=True), (tm, DH))
                 for h in range(H)], axis=1)

        gt = mod_ref[0:1, 2 * D:3 * D]
        gfull = jnp.concatenate([ng_ref[...]] * H, axis=1)
        o = of_ref[...] + ob_ref[...]
        rinv = lax.rsqrt(per_head(o * o) + RMS_EPS)
        n = o * rinv
        na = n * gfull
        z4 = z4_ref[...]
        s4 = _sig(z4)
        silu4 = z4 * s4
        oa = na * silu4
        z6 = z6_ref[...]
        s6 = _sig(z6)
        silu6 = z6 * s6
        hsum = hf_ref[...] + hb_ref[...]
        obv = hsum * silu6
        pa = _dot(oa, pa_ref[...])
        pb = _dot(obv, pb_ref[...])
        s7 = _sig(z7_ref[...])
        s8 = _sig(z8_ref[...])
        y = s7 * pa + s8 * pb
        out = _dot(y, wo_ref[...])
        pre = ALPHA * x_ref[...] + gt * out
        mu = jnp.mean(pre, axis=-1, keepdims=True)
        xc = pre - mu
        rstd = lax.rsqrt(jnp.mean(xc * xc, axis=-1, keepdims=True) + LN_EPS)
        xhat = xc * rstd
        lg = lg_ref[...]
        diff = xhat * lg + lb_ref[...] - t_ref[...]
        acc_ref[8:9, :] += _colsum(diff * diff) * (0.5 / D * latf)
        dxn = diff * (1.0 / D)
        acc_ref[1:2, :] += _colsum(dxn * xhat) * latf
        acc_ref[2:3, :] += _colsum(dxn) * latf
        dxhat = dxn * lg
        dpre = rstd * (dxhat - jnp.mean(dxhat, axis=-1, keepdims=True)
                       - xhat * jnp.mean(dxhat * xhat, axis=-1, keepdims=True))
        gx_ref[...] = ALPHA * dpre
        acc_ref[0:1, :] += _colsum(dpre * out) * latf
        dout = dpre * gt
        dy = _dot_nt(dout, wo_ref[...])
        dpa = dy * s7
        dpb = dy * s8
        dz7 = dy * pa * (s7 * (1.0 - s7))
        dz8 = dy * pb * (s8 * (1.0 - s8))
        doa = _dot_nt(dpa, pa_ref[...])
        dob = _dot_nt(dpb, pb_ref[...])
        dh_ref[...] = dob * silu6
        dz6 = dob * hsum * (s6 * (1.0 + z6 * (1.0 - s6)))
        dna = doa * silu4
        dz4 = doa * na * (s4 * (1.0 + z4 * (1.0 - s4)))
        dng = _colsum(dna * n)
        acc_ref[7:8, 0:DH] += sum(dng[:, h * DH:(h + 1) * DH] for h in range(H)) * latf
        dn = dna * gfull
        do_ref[...] = rinv * (dn - n * per_head(dn * n))
        acc_ref[3:4, :] += _colsum(dz4) * latf
        acc_ref[4:5, :] += _colsum(dz6) * latf
        acc_ref[5:6, :] += _colsum(dz7) * latf
        acc_ref[6:7, :] += _colsum(dz8) * latf
        dz4b, dz6b = (dz4 * latf).astype(BF), (dz6 * latf).astype(BF)
        dz4a_ref[...] = dz4b[:, :CUT4]
        dz4b_ref[...] = dz4b[:, CUT4:]
        dz6a_ref[...] = dz6b[:, :CUT6]
        sh3_ref[:, 0:D - CUT6] = dz6b[:, CUT6:]
        sh3_ref[:, D - CUT6:2 * D - CUT6] = (dz7 * latf).astype(BF)
        sh3_ref[:, 2 * D - CUT6:] = (dz8 * latf).astype(BF)
        y_ref[...] = y.astype(BF)
        dout_ref[...] = dout.astype(BF)
        oa_ref[...] = oa.astype(BF)
        dpa_ref[...] = dpa.astype(BF)
        obv_ref[...] = obv.astype(BF)
        dpb_ref[...] = dpb.astype(BF)

        @pl.when(i == nt - 1)
        def _():
            acc_ref[9:10, :] = jnp.broadcast_to(jnp.sum(acc_ref[8:9, :], axis=-1, keepdims=True), (1, D))

    m = x_all.shape[0]
    lrow = lambda i: jnp.minimum(i, nt - 1)
    row = pl.BlockSpec((tm, D), lambda i: (lrow(i), 0))
    allrow = lambda cols: pl.BlockSpec((tm, cols), lambda i: (i, 0))
    zs = lambda cb: pl.BlockSpec((tm, D), lambda i: (lrow(i), cb))
    full = lambda shp: pl.BlockSpec(shp, lambda i: (0,) * len(shp))
    wfull = pl.BlockSpec((D, D), lambda i: (0, 0), pipeline_mode=pl.Buffered(1))
    f32o = jax.ShapeDtypeStruct((t_lat, D), F32)
    bfo = jax.ShapeDtypeStruct((t_lat, D), BF)
    bfall = lambda cols: jax.ShapeDtypeStruct((m, cols), BF)
    return pl.pallas_call(
        body, grid=(m // tm,),
        in_specs=[row, row, row, row, zs(4), zs(6), zs(7), zs(8), row, row, full((16, 3 * D)), full((1, DH)),
                  full((1, D)), full((1, D)), wfull, wfull, wfull],
        out_specs=(row, row) + tuple(allrow(c) for c in (CUT4, D - CUT4, CUT6, SHC)) + (row,) * 7 + (full((16, D)),),
        out_shape=(f32o, f32o, bfall(CUT4), bfall(D - CUT4), bfall(CUT6), bfall(SHC), f32o, bfo, bfo, bfo, bfo, bfo, bfo,
                   jax.ShapeDtypeStruct((16, D), F32)),
        name="merge", compiler_params=_cp(("arbitrary",), VMEM_LIMIT_MERGE))(
            o_f, o_b, h_f, h_b, z, z, z, z, x_all, tgt, mod, norm_g, ln_g, ln_b, p_a, p_b, w_out)


def _wgrad(a, b, name):
    tm = 1024

    def body(a_ref, b_ref, o_ref):
        @pl.when(pl.program_id(0) == 0)
        def _():
            o_ref[...] = jnp.zeros_like(o_ref)
        o_ref[...] += _dot_tn(a_ref[...], b_ref[...])

    row = pl.BlockSpec((tm, D), lambda i: (i, 0))
    return pl.pallas_call(body, grid=(a.shape[0] // tm,), in_specs=[row, row],
                          out_specs=pl.BlockSpec((D, D), lambda i: (0, 0)),
                          out_shape=jax.ShapeDtypeStruct((D, D), F32), name=name,
                          compiler_params=_cp(("arbitrary",)))(a, b)


def _pack_shard2(dz4b, dz5_lat, dz5_ctx, dz6a):
    m, t_lat, t_ctx = dz4b.shape[0], dz5_lat.shape[0], dz5_ctx.shape[0]
    tm = t_ctx
    nlt = t_lat // tm
    w4 = D - CUT4

    def body(a_ref, bl_ref, bc_ref, c_ref, o_ref):
        i = pl.program_id(0)
        o_ref[:, 0:w4] = a_ref[...]
        o_ref[:, w4:w4 + D] = jnp.where(i < nlt, bl_ref[...], bc_ref[...]).astype(BF)
        o_ref[:, w4 + D:] = c_ref[...]

    return pl.pallas_call(
        body, grid=(m // tm,),
        in_specs=[pl.BlockSpec((tm, w4), lambda i: (i, 0)),
                  pl.BlockSpec((tm, D), lambda i: (jnp.minimum(i, nlt - 1), 0)),
                  pl.BlockSpec((tm, D), lambda i: (0, 0)),
                  pl.BlockSpec((tm, CUT6), lambda i: (i, 0))],
        out_specs=pl.BlockSpec((tm, SHC), lambda i: (i, 0)),
        out_shape=jax.ShapeDtypeStruct((m, SHC), BF), name="pack_shard2",
        compiler_params=_cp(("arbitrary",)))(dz4b, dz5_lat, dz5_ctx, dz6a)


def _wgrad_in(u_all, dz_shards):
    m = u_all.shape[0]
    assert m % 128 == 0
    tm = m // 8
    out = None
    for k, dz_k in enumerate(dz_shards):
        def body(u_ref, dz_ref, *rest):
            o_ref = rest[-1]

            @pl.when(pl.program_id(0) == 0)
            def _():
                o_ref[...] = jnp.zeros_like(o_ref)
            o_ref[0] += _dot_tn(u_ref[...], dz_ref[...])

        out = pl.pallas_call(
            body, grid=(m // tm,),
            in_specs=[pl.BlockSpec((tm, D), lambda i: (i, 0)), pl.BlockSpec((tm, SHC), lambda i: (i, 0))]
            + ([] if out is None else [_ANY]),
            out_specs=pl.BlockSpec((1, D, SHC), lambda i, k=k: (k, 0, 0)),
            out_shape=jax.ShapeDtypeStruct((NSH, D, SHC), F32),
            input_output_aliases={} if out is None else {2: 0},
            name=f"wgrad_in{k}", compiler_params=_cp(("arbitrary",)))(u_all, dz_k, *(() if out is None else (out,)))
    return out


def _du(dz_shards, w_in_g, x_all, mod, gxres, n_lat_tiles, sums=()):
    m = x_all.shape[0]
    tm = 256
    nt = m // tm
    nct = nt - n_lat_tiles
    ns = len(sums)
    rblk = lambda i: jnp.where(i < nct, n_lat_tiles + i, i - nct)
    lblk = lambda i: jnp.maximum(i - nct, 0)

    def body(*refs):
        dz_refs, refs = refs[:NSH], refs[NSH - 1:]
        _, w_ref, x_ref, mod_ref, gr_ref = refs[:5]
        sum_refs = refs[5:5 + ns]
        gx_ref, dm_ref = refs[5 + ns:7 + ns]
        got_refs = refs[7 + ns:7 + 2 * ns]
        sems = refs[7 + 2 * ns:]
        i = pl.program_id(0)
        is_lat = i >= nct

        @pl.when(i == 0)
        def _():
            dm_ref[...] = jnp.zeros_like(dm_ref)
            if ns:
                for cp in _rs_chip_copies(sum_refs, got_refs, *sems):
                    cp.start()

        du = _dot_nt(dz_refs[0][...], w_ref[0])
        for n in range(1, NSH):
            du = du + _dot_nt(dz_refs[n][...], w_ref[n])
        sc = jnp.where(is_lat, mod_ref[0:1, D:2 * D], mod_ref[1:2, D:2 * D])
        dsh = _colsum(du)
        dsc = _colsum(du * x_ref[...])

        @pl.when(is_lat)
        def _():
            gx_ref[...] = du * (1.0 + sc) + gr_ref[...]
            dm_ref[0:1, 0:D] += dsh
            dm_ref[0:1, D:2 * D] += dsc

        @pl.when(jnp.logical_not(is_lat))
        def _():
            dm_ref[1:2, 0:D] += dsh
            dm_ref[1:2, D:2 * D] += dsc

        if ns:
            @pl.when(i == nt - 1)
            def _():
                for cp in _rs_chip_copies(sum_refs, got_refs, *sems):
                    cp.wait()

    outs = pl.pallas_call(
        body, grid=(nt,),
        in_specs=[pl.BlockSpec((tm, SHC), lambda i: (rblk(i), 0))] * NSH + [
                  pl.BlockSpec((NSH, D, SHC), lambda i: (0, 0, 0), pipeline_mode=pl.Buffered(1)),
                  pl.BlockSpec((tm, D), lambda i: (rblk(i), 0)),
                  pl.BlockSpec((16, 3 * D), lambda i: (0, 0)),
                  pl.BlockSpec((tm, D), lambda i: (lblk(i), 0))] + [_ANY] * ns,
        out_specs=[pl.BlockSpec((tm, D), lambda i: (lblk(i), 0)),
                   pl.BlockSpec((8, 2 * D), lambda i: (0, 0))] + [_ANY] * ns,
        out_shape=[jax.ShapeDtypeStruct((n_lat_tiles * tm, D), F32), jax.ShapeDtypeStruct((8, 2 * D), F32)]
        + [jax.ShapeDtypeStruct((3,) + g.shape[1:], g.dtype) for g in sums],
        scratch_shapes=[pltpu.SemaphoreType.DMA((3 * ns,)), pltpu.SemaphoreType.DMA((3 * ns,))] if ns else [],
        name="du", compiler_params=_cp(("arbitrary",)))(*dz_shards, w_in_g, x_all, mod, gxres, *sums)
    return outs[0], outs[1], list(outs[2:])


def _row_tile(rows, cols):
    t = 8
    while t * 2 * cols * 4 <= (1 << 20) and rows % (t * 2) == 0:
        t *= 2
    return t


def _adamw_update(w_ref, g_ref, m_ref, v_ref, d_ref, nm_ref, nv_ref):
    gg = g_ref[...]
    m2 = ADAM_B1 * m_ref[...] + (1.0 - ADAM_B1) * gg
    v2 = ADAM_B2 * v_ref[...] + (1.0 - ADAM_B2) * (gg * gg)
    m_hat = m2 / (1.0 - ADAM_B1 ** ADAM_STEP)
    v_hat = v2 / (1.0 - ADAM_B2 ** ADAM_STEP)
    d_ref[...] = -ADAM_LR * (m_hat / (jnp.sqrt(v_hat) + ADAM_EPS) + ADAM_WD * w_ref[...])
    nm_ref[...] = m2
    nv_ref[...] = v2


def _adamw_many(ws, gs, ms, vs):
    n = len(ws)

    def body(*refs):
        for j in range(n):
            _adamw_update(*refs[4 * j:4 * j + 4], *refs[4 * n + 3 * j:4 * n + 3 * j + 3])

    args = [a for quad in zip(ws, gs, ms, vs) for a in quad]
    outs = pl.pallas_call(body, out_shape=[jax.ShapeDtypeStruct(w.shape, F32) for w in ws for _ in range(3)],
                          name="adamw_small", compiler_params=_cp())(*args)
    return outs[0::3], outs[1::3], outs[2::3]


def _adamw(w, g, m, v, name):
    rows, cols = w.shape
    tr = _row_tile(rows, cols)

    def body(*refs):
        _adamw_update(*refs)

    spec = pl.BlockSpec((tr, cols), lambda i: (i, 0))
    o = jax.ShapeDtypeStruct((rows, cols), F32)
    return pl.pallas_call(body, grid=(rows // tr,), in_specs=[spec] * 4, out_specs=(spec,) * 3,
                          out_shape=(o, o, o), name=name, compiler_params=_cp(("arbitrary",)))(w, g, m, v)


_ANY = pl.BlockSpec(memory_space=pl.ANY)


def _place():
    return lax.axis_index("x"), lax.axis_index("y"), lax.axis_index("c")


def _ag_copies(ins, outs, send, recv, fsend, frecv, lsem):
    x, y, c = _place()
    me = 2 * x + y
    chips = ((1 - x, y), (x, 1 - y), (1 - x, 1 - y))
    local, chip, hand = [], [], []
    for j in range(len(ins)):
        hr = ins[j].shape[0] // 2
        half = pl.ds(pl.multiple_of(c * hr, 8), hr)
        local.append(pltpu.make_async_copy(ins[j], outs[j].at[me], lsem.at[j]))
        for k, (px, py) in enumerate(chips):
            chip.append(pltpu.make_async_remote_copy(
                src_ref=ins[j].at[half, :], dst_ref=outs[j].at[me, half, :], send_sem=send.at[3 * j + k],
                recv_sem=recv.at[3 * j + k], device_id=(px, py, c), device_id_type=MESH))
            got = outs[j].at[2 * px + py, half, :]
            hand.append(pltpu.make_async_remote_copy(
                src_ref=got, dst_ref=got, send_sem=fsend.at[3 * j + k], recv_sem=frecv.at[3 * j + k],
                device_id=(x, y, 1 - c), device_id_type=MESH))
    return local, chip, hand


def _ag_sems(n):
    return [pltpu.SemaphoreType.DMA((3 * n,))] * 4 + [pltpu.SemaphoreType.DMA((n,))]


def _ag_finish(local, chip, hand, done=0):
    for k in range(done, len(chip)):
        chip[k].wait_recv()
        hand[k].start()
    for cp in chip:
        cp.wait_send()
    for k in range(done):
        hand[k].wait_send()
    for k in range(done, len(chip)):
        hand[k].wait_send()
        hand[k].wait_recv()
    for cp in local:
        cp.wait()


def _mod_tp(c8, c_ctx, w_mod_sh, b_mod_sh):
    mc = w_mod_sh.shape[1]

    def body(c8_ref, cctx_ref, w_ref, b_ref, mod_ref, cc_ref, cc_s, part_s, send1, recv1, send3, recv3):
        x, y, c = _place()
        me = 4 * x + 2 * y + c
        ms = 2 * x + y
        copies = []
        for k in range(1, 8):
            peer = (x ^ ((k >> 2) & 1), y ^ ((k >> 1) & 1), c ^ (k & 1))
            cp = pltpu.make_async_remote_copy(src_ref=c8_ref, dst_ref=cc_s.at[me], send_sem=send1.at[k],
                                              recv_sem=recv1.at[k], device_id=peer, device_id_type=MESH)
            cp.start()
            copies.append(cp)
        cc_s[me] = c8_ref[...]
        for cp in copies:
            cp.wait()
        cc_ref[...] = jnp.zeros_like(cc_ref)
        for j in range(8):
            cc_ref[j:j + 1, :] = cc_s[j, 0:1, :]
        cc_ref[8:9, :] = cctx_ref[...]
        v = cc_ref[...]
        part_s[ms] = _dot(v * _sig(v), w_ref[...]) + b_ref[...]
        copies = []
        for k in range(1, 4):
            peer = (x ^ ((k >> 1) & 1), y ^ (k & 1), c)
            cp = pltpu.make_async_remote_copy(src_ref=part_s.at[ms], dst_ref=part_s.at[ms], send_sem=send3.at[k],
                                              recv_sem=recv3.at[k], device_id=peer, device_id_type=MESH)
            cp.start()
            copies.append(cp)
        for cp in copies:
            cp.wait()
        for s in range(NSH):
            mod_ref[:, s * mc:(s + 1) * mc] = part_s[s]

    vm = pl.BlockSpec(memory_space=pltpu.VMEM)
    return pl.pallas_call(
        body, in_specs=[vm] * 4, out_specs=(vm, vm),
        out_shape=(jax.ShapeDtypeStruct((16, NSH * mc), F32), jax.ShapeDtypeStruct((16, D), F32)),
        scratch_shapes=[pltpu.VMEM((8, 8, D), F32), pltpu.VMEM((NSH, 16, mc), F32),
                        pltpu.SemaphoreType.DMA((8,)), pltpu.SemaphoreType.DMA((8,)),
                        pltpu.SemaphoreType.DMA((4,)), pltpu.SemaphoreType.DMA((4,))],
        name="mod_tp", compiler_params=_cp())(c8, c_ctx, w_mod_sh, b_mod_sh)


def _inproj_ag(x_all, mod, w_in_sh, b_in, narrow_sh, n_lat_tiles):
    m = x_all.shape[0]
    assert m % (11 * 16) == 0
    tm = m // 11
    nt = m // tm
    n_lat = n_lat_tiles * 256
    x_, y_ = lax.axis_index("x"), lax.axis_index("y")
    sids = jnp.stack([2 * x_ + y_, 2 * (1 - x_) + y_, 2 * x_ + 1 - y_, 2 * (1 - x_) + 1 - y_]).astype(jnp.int32)

    def body(sid_ref, x_ref, mod_ref, b_ref, wsh_ref, nsh_ref, z_ref, u_ref, wg_ref, ng_ref, w_s, u_s, *sems):
        n = pl.program_id(0)
        i = pl.program_id(1)
        rows = pl.ds(pl.multiple_of(i * tm, tm), tm)
        ag = ((wsh_ref, nsh_ref), (wg_ref, ng_ref)) + tuple(sems[:5])
        wsem = sems[5]

        def load(src):
            cp = pltpu.make_async_copy(src, w_s, wsem)
            cp.start()
            cp.wait()

        @pl.when((n == 0) & (i == 0))
        def _():
            local, chip, _ = _ag_copies(*ag)
            for cp in chip + local:
                cp.start()
            load(wsh_ref)

        for k in range(NSH - 1):
            @pl.when((n == k + 1) & (i == 0))
            def _():
                _, chip, hand = _ag_copies(*ag)
                chip[k].wait_recv()
                hand[k].start()
                hand[k].wait_recv()
                load(wg_ref.at[sid_ref[k + 1]])

        @pl.when(n == 0)
        def _():
            is_lat = (i * tm + lax.broadcasted_iota(jnp.int32, (tm, 1), 0)) < n_lat
            u, _ = _modulate(x_ref, mod_ref, is_lat)
            u_s[rows, :] = u.astype(BF)
            u_ref[...] = u.astype(BF)

        z_ref[...] = _dot(u_s[rows, :], w_s[...]) + b_ref[...]

        @pl.when((n == NSH - 1) & (i == nt - 1))
        def _():
            _ag_finish(*_ag_copies(*ag), done=NSH - 1)

    first = lambda n, i: jnp.where(n == 0, i, nt - 1)
    outs = pl.pallas_call(
        body, grid_spec=pltpu.PrefetchScalarGridSpec(
            num_scalar_prefetch=1, grid=(NSH, nt),
            in_specs=[pl.BlockSpec((tm, D), lambda n, i, sid: (first(n, i), 0)),
                      pl.BlockSpec((16, 3 * D), lambda n, i, sid: (0, 0)),
                      pl.BlockSpec((1, SHC), lambda n, i, sid: (0, sid[n])), _ANY, _ANY],
            out_specs=[pl.BlockSpec((tm, SHC), lambda n, i, sid: (i, sid[n])),
                       pl.BlockSpec((tm, D), lambda n, i, sid: (first(n, i), 0)), _ANY, _ANY],
            scratch_shapes=[pltpu.VMEM((D, SHC), BF), pltpu.VMEM((m, D), BF)] + _ag_sems(2)
            + [pltpu.SemaphoreType.DMA]),
        out_shape=[jax.ShapeDtypeStruct((m, IN_COLS), F32), jax.ShapeDtypeStruct((m, D), BF),
                   jax.ShapeDtypeStruct((NSH,) + w_in_sh.shape, BF),
                   jax.ShapeDtypeStruct((NSH,) + narrow_sh.shape, narrow_sh.dtype)],
        name="inproj_ag", compiler_params=_cp(("arbitrary", "arbitrary")))(sids, x_all, mod, b_in, w_in_sh, narrow_sh)
    return outs


def _rs_sibling(grads, name):
    n = len(grads)

    def body(*refs):
        ins, got = refs[:n], refs[n:2 * n]
        send, recv = refs[2 * n:]
        x, y, c = _place()
        copies = []
        for j in range(n):
            hr = ins[j].shape[1] // 2
            for s in range(NSH):
                give = ins[j].at[s, pl.ds(pl.multiple_of((1 - c) * hr, 8), hr), :]
                cp = pltpu.make_async_remote_copy(src_ref=give, dst_ref=got[j].at[s], send_sem=send.at[NSH * j + s],
                                                  recv_sem=recv.at[NSH * j + s], device_id=(x, y, 1 - c),
                                                  device_id_type=MESH)
                cp.start()
                copies.append(cp)
        for cp in copies:
            cp.wait()

    half = [jax.ShapeDtypeStruct((NSH, g.shape[1] // 2, g.shape[2]), F32) for g in grads]
    return pl.pallas_call(
        body, in_specs=[_ANY] * n, out_specs=[_ANY] * n, out_shape=half,
        scratch_shapes=[pltpu.SemaphoreType.DMA((NSH * n,)), pltpu.SemaphoreType.DMA((NSH * n,))],
        name=name)(*grads)


def _core_vec():
    return lax.axis_index("c").astype(jnp.int32).reshape(1)


def _rs_add1(g, got, name):
    _, r, cols = g.shape
    hr = r // 2
    tr = _row_tile(hr, cols)
    nb = hr // tr

    def body(c_ref, g_ref, got_ref, o_ref):
        o_ref[...] = (g_ref[...] + got_ref[...]).astype(BF)

    spec = pl.BlockSpec((1, tr, cols), lambda s, i, c_ref: (s, i, 0))
    return pl.pallas_call(
        body, grid_spec=pltpu.PrefetchScalarGridSpec(
            num_scalar_prefetch=1, grid=(NSH, nb),
            in_specs=[pl.BlockSpec((1, tr, cols), lambda s, i, c_ref: (s, c_ref[0] * nb + i, 0)), spec],
            out_specs=spec),
        out_shape=jax.ShapeDtypeStruct((NSH, hr, cols), BF), name=name,
        compiler_params=_cp(("arbitrary", "arbitrary")))(_core_vec(), g, got)


def _rs_add2(sums, got, name):
    _, hr, cols = sums.shape
    tr = _row_tile(hr, cols)
    nb = hr // tr
    place = jnp.stack([2 * lax.axis_index("x") + lax.axis_index("y"), lax.axis_index("c")]).astype(jnp.int32)

    def body(p_ref, s_ref, got_ref, o_ref):
        f = lambda v: v.astype(F32)
        o_ref[...] = f(s_ref[0]) + f(got_ref[0]) + f(got_ref[1]) + f(got_ref[2])

    return pl.pallas_call(
        body, grid_spec=pltpu.PrefetchScalarGridSpec(
            num_scalar_prefetch=1, grid=(nb,),
            in_specs=[pl.BlockSpec((1, tr, cols), lambda i, p_ref: (p_ref[0], i, 0)),
                      pl.BlockSpec((3, tr, cols), lambda i, p_ref: (0, i, 0))],
            out_specs=pl.BlockSpec((tr, cols), lambda i, p_ref: (p_ref[1] * nb + i, 0))),
        out_shape=jax.ShapeDtypeStruct((2 * hr, cols), F32), name=name,
        compiler_params=_cp(("arbitrary",)))(place, sums, got)


def _rs_chip_copies(ins, got, send, recv):
    x, y, c = _place()
    peers = ((1 - x, y), (x, 1 - y), (1 - x, 1 - y))
    return [pltpu.make_async_remote_copy(src_ref=ins[j].at[2 * px + py], dst_ref=got[j].at[k],
                                         send_sem=send.at[3 * j + k], recv_sem=recv.at[3 * j + k],
                                         device_id=(px, py, c), device_id_type=MESH)
            for j in range(len(ins)) for k, (px, py) in enumerate(peers)]


def _ag_sibling(fulls):
    n = len(fulls)
    nck = 4

    def body(*refs):
        outs = refs[n:2 * n]
        send, recv = refs[2 * n:]
        x, y, c = _place()
        copies = []
        for j in range(n):
            qr = outs[j].shape[0] // (2 * nck)
            for k in range(nck):
                rows = outs[j].at[pl.ds(pl.multiple_of((c * nck + k) * qr, 8), qr), :]
                cp = pltpu.make_async_remote_copy(src_ref=rows, dst_ref=rows, send_sem=send.at[nck * j + k],
                                                  recv_sem=recv.at[nck * j + k], device_id=(x, y, 1 - c),
                                                  device_id_type=MESH)
                cp.start()
                copies.append(cp)
        for cp in copies:
            cp.wait()

    return pl.pallas_call(
        body, in_specs=[_ANY] * n, out_specs=[_ANY] * n,
        out_shape=[jax.ShapeDtypeStruct(f.shape, F32) for f in fulls],
        input_output_aliases={j: j for j in range(n)},
        scratch_shapes=[pltpu.SemaphoreType.DMA((nck * n,)), pltpu.SemaphoreType.DMA((nck * n,))],
        name="ag_sibling")(*fulls)


def _allreduce_small(buf):
    rows = buf.shape[0]
    pr = rows // 8

    def body(in_ref, out_ref, stage, send1, recv1, send2, recv2):
        x, y, c = _place()
        me = 4 * x + 2 * y + c

        def peer(k):
            kx, ky, kc = (k >> 2) & 1, (k >> 1) & 1, k & 1
            return (x ^ kx, y ^ ky, c ^ kc)

        def piece(ref, idx):
            return ref.at[pl.ds(pl.multiple_of(idx * pr, 8), pr), :]

        copies = []
        for k in range(1, 8):
            px, py, pc = peer(k)
            cp = pltpu.make_async_remote_copy(src_ref=piece(in_ref, 4 * px + 2 * py + pc), dst_ref=stage.at[k],
                                              send_sem=send1.at[k], recv_sem=recv1.at[k],
                                              device_id=(px, py, pc), device_id_type=MESH)
            cp.start()
            copies.append(cp)
        for cp in copies:
            cp.wait()
        acc = piece(in_ref, me)[...]
        for k in range(1, 8):
            acc = acc + stage[k]
        piece(out_ref, me)[...] = acc
        copies = []
        for k in range(1, 8):
            cp = pltpu.make_async_remote_copy(src_ref=piece(out_ref, me), dst_ref=piece(out_ref, me),
                                              send_sem=send2.at[k], recv_sem=recv2.at[k],
                                              device_id=peer(k), device_id_type=MESH)
            cp.start()
            copies.append(cp)
        for cp in copies:
            cp.wait()

    vm = pl.BlockSpec(memory_space=pltpu.VMEM)
    return pl.pallas_call(
        body, in_specs=[vm], out_specs=vm, out_shape=jax.ShapeDtypeStruct((rows, D), F32),
        scratch_shapes=[pltpu.VMEM((8, pr, D), F32)] + [pltpu.SemaphoreType.DMA((8,))] * 4,
        name="allreduce_small", compiler_params=_cp())(buf)


def _rows(a):
    flat = a.reshape(-1)
    pad = (-flat.shape[0]) % D
    if pad:
        flat = jnp.concatenate([flat, jnp.zeros((pad,), flat.dtype)])
    return flat.reshape(-1, D)


def _pad_rows(a, mult):
    pad = (-a.shape[0]) % mult
    return jnp.concatenate([a, jnp.zeros((pad, a.shape[1]), a.dtype)]) if pad else a


def _local_step(x, c, ctx, c_ctx, tgt, me, shard, sh, b_mod, b_in, norm_g, cb, wr, wi, ln_g, ln_b):
    t_lat, t_ctx = x.shape[0], ctx.shape[0]
    nlt = t_lat // 256
    nlb, ncb = t_lat // RB, t_ctx // RB
    mc = 3 * D // NSH
    mod_all, cc_all = _mod_tp(jnp.zeros((8, D), F32).at[0].set(c), c_ctx.reshape(1, D), sh["w_mod"],
                              lax.dynamic_slice_in_dim(b_mod, shard * mc, mc, axis=1))
    mod = jnp.zeros((16, 3 * D), F32).at[0].set(mod_all[me]).at[1].set(mod_all[8])
    cc = jnp.zeros((16, D), F32).at[0].set(c).at[1].set(c_ctx)
    x_all = jnp.concatenate([x, ctx], axis=0)
    z, u_all, w_in_g, nar = _inproj_ag(x_all, mod, sh["w_in"], b_in, sh["narrow"], nlt)
    nar = jnp.transpose(nar, (1, 0, 2)).reshape(-1, D)
    lbl, cw, br, bi, lam = nar[0:4].reshape(2, 2, D), nar[4:8], nar[8:10], nar[10:12], nar[12:14]
    o0, st0, (w_mod_g, p_a, p_b, w_out) = _gla_fwd(z, lbl, 0, nlb, ncb,
                                                   gather=[sh[k] for k in ("w_mod", "p_a", "p_b", "w_out")])
    p_a, p_b, w_out = p_a.reshape(D, D), p_b.reshape(D, D), w_out.reshape(D, D)
    o1, st1, _ = _gla_fwd(z, lbl, 1, nlb, ncb)
    h0, hp0, a0, hpc0 = _rglru_fwd(z, cw, cb, wr, br, wi, bi, lam, 0, t_lat, t_ctx)
    h1, hp1, a1, hpc1 = _rglru_fwd(z, cw, cb, wr, br, wi, bi, lam, 1, t_lat, t_ctx)
    (do, dh, dz4a, dz4b, dz6a, dz_sh3, gxres, y, dout, oa, dpa, obv, dpb, acc) = _merge(
        o0, o1, h0, h1, z, x_all, tgt, mod, norm_g, ln_g, ln_b, p_a, p_b, w_out, t_lat)
    gp_a = _wgrad(oa, dpa, "wgrad_pa")
    gp_b = _wgrad(obv, dpb, "wgrad_pb")
    gw_out = _wgrad(y, dout, "wgrad_wout")
    dxc_lat, dxc_ctx, dwr0, dwi0, sb0 = _rglru_bwd(z, cw, cb, wr, br, wi, bi, lam, dh, hp0, a0, hpc0, 0, t_lat, t_ctx)
    dz5_lat, dz5_ctx, dwr1, dwi1, sb1 = _rglru_bwd(z, cw, cb, wr, br, wi, bi, lam, dh, hp1, a1, hpc1, 1, t_lat, t_ctx,
                                                   prev=(dxc_lat, dxc_ctx))
    dq0, dv0, dz1, sa0 = _gla_bwd(z, lbl, do, st0, 0, nlb, ncb)
    dz_sh0, dz_sh1, sa1 = _gla_bwd(z, lbl, do, st1, 1, nlb, ncb, prev=(dq0, dv0, dz1, dz4a))
    dz = (dz_sh0, dz_sh1, _pack_shard2(dz4b, dz5_lat, dz5_ctx, dz6a), dz_sh3)
    grads = [_wgrad_in(u_all, dz)] + [g.reshape(NSH, D // NSH, D) for g in (gp_a, gp_b, gw_out)]
    sums = [_rs_add1(g, b, f"rs_add1_{j}") for j, (g, b) in enumerate(zip(grads, _rs_sibling(grads, "rs_sibling")))]
    gx, dm, got = _du(dz, w_in_g, x_all, mod, gxres, nlt, sums)
    fulls = [_rs_add2(a, b, f"rs_add2_{j}") for j, (a, b) in enumerate(zip(sums, got))]
    big = dict(zip(_RS, _ag_sibling(fulls)))
    dmod = jnp.zeros((16, 3 * D), F32).at[0:2, 0:2 * D].set(dm[0:2]).at[0, 2 * D:].set(acc[0])
    dcc = _mod_bwd(cc, dmod, w_mod_g)
    small = dict(
        c_ctx=dcc[1:2], b_mod=(dmod[0] + dmod[1]).reshape(3, D),
        b_in=jnp.stack([sa1[2], sa0[0], sa1[0], sa1[3], acc[3], sb1[4], acc[4], acc[5], acc[6]]),
        lb_logits=jnp.stack([sa0[1], sa1[1], -sa0[1], -sa1[1]]),
        norm_a_g=acc[7:8], conv_w=sb1[8:12], conv_b=sb1[3:4],
        w_r=jnp.stack([dwr0, dwr1]).reshape(-1, D), w_i=jnp.stack([dwi0, dwi1]).reshape(-1, D),
        b_r=jnp.stack([sb0[0], sb1[0]]), b_i=jnp.stack([sb0[1], sb1[1]]), lam=jnp.stack([sb0[2], sb1[2]]),
        ln_g=acc[1:2], ln_b=acc[2:3])
    return acc[9, 0], gx, big, small, dmod, cc_all


_RS =("w_in", "p_a", "p_b", "w_out")
_SMALL =("c_ctx", "b_mod", "b_in", "lb_logits", "norm_a_g", "conv_w", "conv_b", "w_r", "w_i", "b_r", "b_i", "lam",
          "ln_g", "ln_b")
_BIG = ("w_mod", "w_in", "p_a", "p_b", "w_out")
_COL_SHARDED = ("lb_logits", "conv_w", "b_r", "b_i", "lam")
_WEIGHTS = ("c_ctx", "w_mod", "b_mod", "w_in", "b_in", "lb_logits", "norm_a_g", "conv_w", "conv_b", "w_r", "b_r", "w_i",
            "b_i", "lam", "p_a", "p_b", "w_out", "ln_g", "ln_b")


def kernel(x, c, ctx, c_ctx, w_mod, b_mod, w_in, b_in, lb_logits, norm_a_g, conv_w, conv_b, w_r, b_r, w_i, b_i, lam, p_a, p_b, w_out, ln_g, ln_b, loss_target, m_c_ctx, m_w_mod, m_b_mod, m_w_in, m_b_in, m_lb_logits, m_norm_a_g, m_conv_w, m_conv_b, m_w_r, m_b_r, m_w_i, m_b_i, m_lam, m_p_a, m_p_b, m_w_out, m_ln_g, m_ln_b, v_c_ctx, v_w_mod, v_b_mod, v_w_in, v_b_in, v_lb_logits, v_norm_a_g, v_conv_w, v_conv_b, v_w_r, v_b_r, v_w_i, v_b_i, v_lam, v_p_a, v_p_b, v_w_out, v_ln_g, v_ln_b):
    w = dict(c_ctx=c_ctx, w_mod=w_mod, b_mod=b_mod, w_in=w_in, b_in=b_in, lb_logits=lb_logits, norm_a_g=norm_a_g,
             conv_w=conv_w, conv_b=conv_b, w_r=w_r, b_r=b_r, w_i=w_i, b_i=b_i, lam=lam, p_a=p_a, p_b=p_b, w_out=w_out,
             ln_g=ln_g, ln_b=ln_b)
    m = dict(c_ctx=m_c_ctx, w_mod=m_w_mod, b_mod=m_b_mod, w_in=m_w_in, b_in=m_b_in, lb_logits=m_lb_logits,
             norm_a_g=m_norm_a_g, conv_w=m_conv_w, conv_b=m_conv_b, w_r=m_w_r, b_r=m_b_r, w_i=m_w_i, b_i=m_b_i,
             lam=m_lam, p_a=m_p_a, p_b=m_p_b, w_out=m_w_out, ln_g=m_ln_g, ln_b=m_ln_b)
    v = dict(c_ctx=v_c_ctx, w_mod=v_w_mod, b_mod=v_b_mod, w_in=v_w_in, b_in=v_b_in, lb_logits=v_lb_logits,
             norm_a_g=v_norm_a_g, conv_w=v_conv_w, conv_b=v_conv_b, w_r=v_w_r, b_r=v_b_r, w_i=v_w_i, b_i=v_b_i,
             lam=v_lam, p_a=v_p_a, p_b=v_p_b, w_out=v_w_out, ln_g=v_ln_g, ln_b=v_ln_b)
    shard = 2 * lax.axis_index("x") + lax.axis_index("y")
    cs = D // NSH

    sh = {k: w[k][0].astype(BF) for k in _BIG}
    sh["narrow"] = _pad_rows(jnp.concatenate([lb_logits.reshape(4, cs), conv_w[0], b_r[0], b_i[0], lam[0]], axis=0), 8)
    me = 2 * shard + lax.axis_index("c")
    loss, gx, big, small, dmod, cc_all = _local_step(
        x[0], c[0], ctx[0], c_ctx, loss_target[0], me, shard, sh, b_mod, b_in, norm_a_g, conv_b, w_r[0], w_i[0],
        ln_g, ln_b)
    loss = lax.psum(loss, ("x", "y", "c"))

    dmod_rows = jnp.zeros((16, 3 * D), F32).at[me].set(dmod[0]).at[8].set(dmod[1]).reshape(48, D)
    sizes = [small[k].shape[0] for k in _SMALL]
    red = _allreduce_small(_pad_rows(jnp.concatenate([_pad_rows(small[k], 8) for k in _SMALL] + [dmod_rows],
                                                     axis=0), 64))
    grads = {}
    off = 0
    for k, n in zip(_SMALL, sizes):
        g = red[off:off + n]
        off += n + (-n) % 8
        if k == "norm_a_g":
            g = g[:, :DH]
        if k in _COL_SHARDED:
            g = lax.dynamic_slice_in_dim(g, shard * cs, cs, axis=1)
        grads[k] = g.reshape(w[k].shape)
    for k in _RS:
        grads[k] = big[k].reshape(w[k].shape)
    dmod_all = red[off:off + 48].reshape(16, 3 * D)
    mc = 3 * D // NSH
    grads["w_mod"] = _wmod_grad(cc_all, lax.dynamic_slice_in_dim(dmod_all, shard * mc, mc, axis=1)).reshape(
        w["w_mod"].shape)

    delta, new_m, new_v = {}, {}, {}
    for k in _BIG:
        shp = w[k].shape
        two = lambda a: a.reshape(shp[-2], shp[-1])
        d_, m_, v_ = _adamw(two(w[k]), two(grads[k]), two(m[k]), two(v[k]), f"adamw_{k}")
        delta[k], new_m[k], new_v[k] = d_.reshape(shp), m_.reshape(shp), v_.reshape(shp)
    d_, m_, v_ = _adamw_many(*[[t[k] for k in _SMALL] for t in (w, grads, m, v)])
    delta.update(zip(_SMALL, d_))
    new_m.update(zip(_SMALL, m_))
    new_v.update(zip(_SMALL, v_))

    return (loss, gx[None], *[grads[k] for k in _WEIGHTS], *[delta[k] for k in _WEIGHTS],
            *[new_m[k] for k in _WEIGHTS], *[new_v[k] for k in _WEIGHTS])
```

```python
import functools

import jax
import jax.numpy as jnp
from jax import lax
from jax.experimental import pallas as pl
from jax.experimental.pallas import tpu as pltpu

F32 = jnp.float32
BF = jnp.bfloat16
MESH = pl.DeviceIdType.MESH

D = 1024
H = 8
DH = 128
CH = 64
RB = 256
NCK = RB // CH
GRID_W = 64
CB = 8
RCH = 16
IN_COLS = 9 * D
NSH = 4
SHC = IN_COLS // NSH
CUT2 = SHC - 2 * D
CUT4 = 2 * SHC - 4 * D
CUT6 = 3 * SHC - 6 * D
RG_C = 8.0
ALPHA = 2.0 ** 0.25
LN_EPS = 1e-5
RMS_EPS = 1e-6
Q_SCALE = DH ** -0.5
ADAM_LR, ADAM_B1, ADAM_B2, ADAM_EPS, ADAM_WD, ADAM_STEP = 0.001, 0.9, 0.999, 1e-08, 0.01, 10
VMEM_LIMIT = 56 * 1024 * 1024
VMEM_LIMIT_MERGE = 60 * 1024 * 1024
VMEM_LIMIT_RGLRU_BWD = 63 * 1024 * 1024


def _cp(sem=None, vmem=VMEM_LIMIT):
    return pltpu.CompilerParams(dimension_semantics=sem, vmem_limit_bytes=vmem)


def _sig(x):
    return 0.5 * jnp.tanh(0.5 * x) + 0.5


def _dot(a, b):
    return jnp.dot(a.astype(BF), b.astype(BF), preferred_element_type=F32)


def _dot_nt(a, b):
    return lax.dot_general(a.astype(BF), b.astype(BF), (((1,), (1,)), ((), ())), preferred_element_type=F32)


def _dot_tn(a, b):
    return lax.dot_general(a.astype(BF), b.astype(BF), (((0,), (0,)), ((), ())), preferred_element_type=F32)


def _colsum(v):
    return jnp.sum(v, axis=0, keepdims=True)


def _mod_bwd(cc, dmod, w_mod_g):
    def body(cc_ref, dm_ref, w_ref, dcc_ref):
        v = cc_ref[...]
        sg = _sig(v)
        ds = jnp.zeros((16, D), F32)
        for k in range(NSH):
            ds = ds + _dot_nt(dm_ref[:, k * 768:(k + 1) * 768], w_ref[k])
        dcc_ref[...] = ds * (sg * (1.0 + v * (1.0 - sg)))
    return pl.pallas_call(body, out_shape=jax.ShapeDtypeStruct((16, D), F32),
                          name="mod_bwd", compiler_params=_cp())(cc, dmod, w_mod_g)


def _wmod_grad(cc, dmod_cols):
    def body(cc_ref, dm_ref, dw_ref):
        v = cc_ref[...]
        dw_ref[...] = _dot_tn(v * _sig(v), dm_ref[...])
    return pl.pallas_call(body, out_shape=jax.ShapeDtypeStruct((D, dmod_cols.shape[1]), F32),
                          name="wmod_grad", compiler_params=_cp())(cc, dmod_cols)


def _modulate(x_ref, mod_ref, is_lat):
    sh = jnp.where(is_lat, mod_ref[0:1, 0:D], mod_ref[1:2, 0:D])
    sc = jnp.where(is_lat, mod_ref[0:1, D:2 * D], mod_ref[1:2, D:2 * D])
    return x_ref[...] * (1.0 + sc) + sh, sc


def _gla_mask(d, n):
    row = lax.broadcasted_iota(jnp.int32, (n, n), 0)
    col = lax.broadcasted_iota(jnp.int32, (n, n), 1)
    same = (row // CH) == (col // CH)
    return same & ((row <= col) if d else (row >= col))


def _chunk_cumsum(v, rev):
    n = v.shape[0]
    pos = lax.broadcasted_iota(jnp.int32, v.shape, 0) & (CH - 1)
    s = 1
    while s < CH:
        if rev:
            v = v + jnp.where(pos < CH - s, pltpu.roll(v, n - s, 0), 0.0)
        else:
            v = v + jnp.where(pos >= s, pltpu.roll(v, s, 0), 0.0)
        s *= 2
    return v


def _chunk_rows(c):
    return slice(c * CH, (c + 1) * CH)


def _gla_features(zq, zf, lb, d):
    sq = _sig(zq)
    q = zq * sq * Q_SCALE
    sf = _sig(zf)
    f = lb + (1.0 - lb) * sf
    k = 1.0 - f
    g = _chunk_cumsum(jnp.log(f), d)
    last = 0 if d else CH - 1
    gls = [g[c * CH + last:c * CH + last + 1, :] for c in range(NCK)]
    glb = jnp.concatenate([jnp.broadcast_to(gl, (CH, D)) for gl in gls], axis=0)
    eg, eig, eeg = jnp.exp(g), jnp.exp(-g), jnp.exp(glb - g)
    decs = [jnp.exp(gl) for gl in gls]
    return sq, sf, f, q * eg, k * eig, k * eeg, eg, eig, eeg, decs


def _lower_bound(lbl_ref, d):
    return _sig(lbl_ref[0, d:d + 1, :] - lbl_ref[1, d:d + 1, :])


def _gla_rb(d, nlb, ncb):
    nrb = nlb + ncb
    if d:
        return lambda s: nrb - 1 - s
    return lambda s: jnp.where(s < ncb, nlb + s, s - ncb)


def _gla_fwd(z, lbl, d, nlb, ncb, gather=()):
    m = z.shape[0]
    nrb = nlb + ncb
    rb = _gla_rb(d, nlb, ncb)
    ng = len(gather)

    def body(*refs):
        q_ref, f_ref, v_ref, lbl_ref = refs[:4]
        ag_in = refs[4:4 + ng]
        o_ref, st_ref = refs[4 + ng:6 + ng]
        ag_out = refs[6 + ng:6 + 2 * ng]
        S = refs[6 + 2 * ng]
        ag_sems = refs[7 + 2 * ng:]
        s = pl.program_id(0)

        @pl.when(s == 0)
        def _():
            S[...] = jnp.zeros_like(S)
            if ng:
                local, chip, _ = _ag_copies(ag_in, ag_out, *ag_sems)
                for cp in chip + local:
                    cp.start()

        lb = _lower_bound(lbl_ref, d)
        mb = _gla_mask(d, CH)
        _, _, _, qd, ki, ke, _, _, _, decs = _gla_features(q_ref[...], f_ref[...], lb, d)
        qd, ki, ke, v = qd.astype(BF), ki.astype(BF), ke.astype(BF), v_ref[...].astype(BF)
        order = range(NCK - 1, -1, -1) if d else range(NCK)
        for h in range(H):
            hs = slice(h * DH, (h + 1) * DH)
            intra, upd = {}, {}
            for c in range(NCK):
                rs = _chunk_rows(c)
                a = jnp.where(mb, _dot_nt(qd[rs, hs], ki[rs, hs]), 0.0)
                intra[c] = _dot(a, v[rs, hs])
                upd[c] = _dot_tn(v[rs, hs], ke[rs, hs])
            st = S[h]
            for c in order:
                rs = _chunk_rows(c)
                st_ref[c, h] = st
                o_ref[rs, hs] = intra[c] + _dot_nt(qd[rs, hs], st)
                st = st * decs[c][:, hs] + upd[c]
            S[h] = st

        if ng:
            @pl.when(s == nrb - 1)
            def _():
                _ag_finish(*_ag_copies(ag_in, ag_out, *ag_sems))

    def zspec(cb):
        return pl.BlockSpec((RB, D), lambda s: (rb(s), cb))

    outs = pl.pallas_call(
        body, grid=(nrb,),
        in_specs=[zspec(0), zspec(1 + d), zspec(3), pl.BlockSpec((2, 2, D), lambda s: (0, 0, 0))] + [_ANY] * ng,
        out_specs=[pl.BlockSpec((RB, D), lambda s: (rb(s), 0)),
                   pl.BlockSpec((NCK, H, DH, DH), lambda s: (rb(s), 0, 0, 0))] + [_ANY] * ng,
        out_shape=[jax.ShapeDtypeStruct((m, D), F32), jax.ShapeDtypeStruct((m // CH, H, DH, DH), F32)]
        + [jax.ShapeDtypeStruct((NSH,) + g.shape, g.dtype) for g in gather],
        scratch_shapes=[pltpu.VMEM((H, DH, DH), F32)] + (_ag_sems(ng) if ng else []),
        name=f"gla_fwd{d}", compiler_params=_cp(("arbitrary",)))(z, z, z, lbl, *gather)
    return outs[0], outs[1], list(outs[2:])


def _gla_bwd(z, lbl, do_lat, states, d, nlb, ncb, prev=None):
    m = z.shape[0]
    nrb = nlb + ncb
    fwd_rb = _gla_rb(d, nlb, ncb)
    rb = lambda s: fwd_rb(nrb - 1 - s)
    last = prev is not None

    def body(*refs):
        if last:
            (q_ref, f_ref, v_ref, lbl_ref, do_ref, st_ref, pq_ref, pv_ref, dz1_ref, dz4a_ref, sh0_ref, sh1_ref, sum_ref,
             dS) = refs
        else:
            q_ref, f_ref, v_ref, lbl_ref, do_ref, st_ref, o0_ref, o1_ref, o2_ref, sum_ref, dS = refs
        s = pl.program_id(0)
        is_lat = rb(s) < nlb

        @pl.when(s == 0)
        def _():
            dS[...] = jnp.zeros_like(dS)
            sum_ref[...] = jnp.zeros_like(sum_ref)

        lb = _lower_bound(lbl_ref, d)
        mb = _gla_mask(d, RB)
        zq = q_ref[...]
        sq, sf, f, qd, ki, ke, eg, eig, eeg, decs = _gla_features(zq, f_ref[...], lb, d)
        qdb, kib, keb, vb = qd.astype(BF), ki.astype(BF), ke.astype(BF), v_ref[...].astype(BF)
        dob = jnp.where(is_lat, do_ref[...], 0.0).astype(BF)
        order = range(NCK) if d else range(NCK - 1, -1, -1)
        dqd_h, dki_h, dke_h, dv_h, ddec_h = [], [], [], [], []
        for h in range(H):
            hs = slice(h * DH, (h + 1) * DH)
            a = jnp.where(mb, _dot_nt(qdb[:, hs], kib[:, hs]), 0.0).astype(BF)
            da = jnp.where(mb, _dot_nt(dob[:, hs], vb[:, hs]), 0.0).astype(BF)
            dqd_i = _dot(da, kib[:, hs])
            dki_h.append(_dot_tn(da, qdb[:, hs]))
            dvi = _dot_tn(a, dob[:, hs])
            dqd, inc = {}, {}
            for c in range(NCK):
                rs = _chunk_rows(c)
                dqd[c] = dqd_i[rs, :] + _dot(dob[rs, hs], st_ref[c, h])
                inc[c] = _dot_tn(dob[rs, hs], qdb[rs, hs])
            dst = dS[h]
            dke, dv, ddec = {}, {}, {}
            for c in order:
                rs = _chunk_rows(c)
                dv[c] = dvi[rs, :] + _dot_nt(keb[rs, hs], dst)
                dke[c] = _dot(vb[rs, hs], dst)
                ddec[c] = _colsum(st_ref[c, h] * dst)
                dst = inc[c] + dst * decs[c][:, hs]
            dS[h] = dst
            cat = lambda t: jnp.concatenate([t[c] for c in range(NCK)], axis=0)
            dqd_h.append(cat(dqd))
            dke_h.append(cat(dke))
            dv_h.append(cat(dv))
            ddec_h.append([ddec[c] for c in range(NCK)])
        lanes = lambda parts: jnp.concatenate(parts, axis=1)
        dqd, dki, dke, dv = lanes(dqd_h), lanes(dki_h), lanes(dke_h), lanes(dv_h)
        dq = dqd * eg
        dk = dki * eig + dke * eeg
        dke_ke = dke * ke
        dg = dqd * qd - dki * ki - dke_ke
        dgl = [_colsum(dke_ke[_chunk_rows(c), :]) + lanes([ddec_h[h][c] for h in range(H)]) * decs[c]
               for c in range(NCK)]
        dglb = jnp.concatenate([jnp.broadcast_to(t, (CH, D)) for t in dgl], axis=0)
        df = (_chunk_cumsum(dg, 1 - d) + dglb) / f - dk
        dzf = df * (1.0 - lb) * (sf * (1.0 - sf))
        sum_ref[0:1, :] += _colsum(dzf)
        sum_ref[1:2, :] += _colsum(df * (1.0 - sf))
        if last:
            dz0 = (dq + pq_ref[...]) * (Q_SCALE * (sq * (1.0 + zq * (1.0 - sq))))
            dz3 = dv + pv_ref[...]
            sum_ref[2:3, :] += _colsum(dz0)
            sum_ref[3:4, :] += _colsum(dz3)
            dz2 = dzf.astype(BF)
            sh0_ref[:, 0:D] = dz0.astype(BF)
            sh0_ref[:, D:2 * D] = dz1_ref[...]
            sh0_ref[:, 2 * D:] = dz2[:, :CUT2]
            sh1_ref[:, 0:D - CUT2] = dz2[:, CUT2:]
            sh1_ref[:, D - CUT2:2 * D - CUT2] = dz3.astype(BF)
            sh1_ref[:, 2 * D - CUT2:] = dz4a_ref[...]
        else:
            o0_ref[...] = dq
            o1_ref[...] = dv
            o2_ref[...] = dzf.astype(BF)

        @pl.when(s == nrb - 1)
        def _():
            sum_ref[1:2, :] = sum_ref[1:2, :] * (lb * (1.0 - lb))

    def zspec(cb):
        return pl.BlockSpec((RB, D), lambda s: (rb(s), cb))

    rowspec = pl.BlockSpec((RB, D), lambda s: (rb(s), 0))
    in_specs = [zspec(0), zspec(1 + d), zspec(3), pl.BlockSpec((2, 2, D), lambda s: (0, 0, 0)),
                pl.BlockSpec((RB, D), lambda s: (jnp.minimum(rb(s), nlb - 1), 0)),
                pl.BlockSpec((NCK, H, DH, DH), lambda s: (rb(s), 0, 0, 0))]
    args = [z, z, z, lbl, do_lat, states]
    sumspec = pl.BlockSpec((8, D), lambda s: (0, 0))
    if last:
        in_specs += [rowspec, rowspec, rowspec, pl.BlockSpec((RB, CUT4), lambda s: (rb(s), 0))]
        args += list(prev)
        shspec = pl.BlockSpec((RB, SHC), lambda s: (rb(s), 0))
        out_specs = (shspec, shspec, sumspec)
        out_shape = (jax.ShapeDtypeStruct((m, SHC), BF), jax.ShapeDtypeStruct((m, SHC), BF))
    else:
        out_specs = (rowspec, rowspec, rowspec, sumspec)
        out_shape = (jax.ShapeDtypeStruct((m, D), F32), jax.ShapeDtypeStruct((m, D), F32),
                     jax.ShapeDtypeStruct((m, D), BF))
    return pl.pallas_call(
        body, grid=(nrb,), in_specs=in_specs, out_specs=out_specs,
        out_shape=out_shape + (jax.ShapeDtypeStruct((8, D), F32),),
        scratch_shapes=[pltpu.VMEM((H, DH, DH), F32)],
        name=f"gla_bwd{d}", compiler_params=_cp(("arbitrary",)))(*args)


def _shift_rows(v, k, r):
    row = lax.broadcasted_iota(jnp.int32, v.shape, 0)
    rolled = pltpu.roll(v, k % r, 0)
    return jnp.where((row >= k) & (row < r + k), rolled, 0.0)


def _conv_fwd(xb, cw_ref, cb_ref, r):
    return (cb_ref[...] + _shift_rows(xb, 1, r) * cw_ref[0:1, :] + xb * cw_ref[1:2, :]
            + _shift_rows(xb, -1, r) * cw_ref[2:3, :] + _shift_rows(xb, -2, r) * cw_ref[3:4, :])


def _window(ref, lo, n, rows):
    parts = []
    if lo < 0:
        parts.append(jnp.zeros((-lo,) + tuple(ref.shape[1:]), F32))
    parts.append(ref[max(lo, 0):min(lo + n, rows)])
    if lo + n > rows:
        parts.append(jnp.zeros((lo + n - rows,) + tuple(ref.shape[1:]), F32))
    return parts[0] if len(parts) == 1 else jnp.concatenate(parts, axis=0)


def _conv_cols(x_ref, cw_ref, cb_ref, r0, n, rows):
    w = _window(x_ref, r0 - 1, n + 3, rows)
    return (cb_ref[...] + w[0:n] * cw_ref[0:1, :] + w[1:n + 1] * cw_ref[1:2, :] + w[2:n + 2] * cw_ref[2:3, :]
            + w[3:n + 3] * cw_ref[3:4, :])


def _softplus_neg(lam):
    y = jnp.exp(-jnp.abs(lam))
    u = 1.0 + y
    tiny = u == 1.0
    l1p = jnp.where(tiny, y, jnp.log(u) * (y / jnp.where(tiny, 1.0, u - 1.0)))
    return jnp.maximum(-lam, 0.0) + l1p


def _gates(xc, wr_ref, br, wi_ref, bi, sp):
    xcb = xc.astype(BF)
    rs, is_ = [], []
    for g in range(H):
        gs = slice(g * DH, (g + 1) * DH)
        rs.append(jnp.dot(xcb[:, gs], wr_ref[g].astype(BF), preferred_element_type=F32))
        is_.append(jnp.dot(xcb[:, gs], wi_ref[g].astype(BF), preferred_element_type=F32))
    r = _sig(jnp.concatenate(rs, axis=1) + br)
    i = _sig(jnp.concatenate(is_, axis=1) + bi)
    log_a = (-RG_C * r) * sp
    a = jnp.exp(log_a)
    t = jnp.tanh(log_a)
    om = (-2.0 * t) / (1.0 - t)
    return r, i, a, om


def _scan_rows(d, nrows, a_s, b_s, h_s, h0):
    nsl = nrows // 8

    def slab(j, h):
        jj = (nsl - 1 - j) if d else j
        r0 = pl.multiple_of(jj * 8, 8)
        for t in (range(7, -1, -1) if d else range(8)):
            h = a_s[pl.ds(r0 + t, 1), :] * h + b_s[pl.ds(r0 + t, 1), :]
            h_s[pl.ds(r0 + t, 1), :] = h
        return h

    return lax.fori_loop(0, nsl, slab, h0)


def _col_of(d, ncols):
    if d:
        return lambda s: ncols - jnp.maximum(s, 1)
    return lambda s: jnp.maximum(s, 1) - 1


def _rglru_fwd(z, cw, cb, wr, br, wi, bi, lam, d, t_lat, t_ctx):
    m = z.shape[0]
    rows = t_lat // GRID_W
    z3 = z.reshape(m // GRID_W, GRID_W, IN_COLS)
    nblk = GRID_W // CB
    blk = _col_of(d, nblk)
    cblk = t_lat // t_ctx
    rc = min(RCH, rows)

    def body(zc_ref, zx_ref, cw_ref, cb_ref, wr_ref, br_ref, wi_ref, bi_ref, lam_ref,
             hx_ref, hpx_ref, ax_ref, hpc_ref, a_s, b_s, h_s, hcar, a3, b3, cin_s):
        s = pl.program_id(0)
        sp = _softplus_neg(lam_ref[d:d + 1, :])
        br_ = br_ref[d:d + 1, :]
        bi_ = bi_ref[d:d + 1, :]

        @pl.when(s == 0)
        def _():
            xc = _conv_fwd(zc_ref[...], cw_ref, cb_ref, t_ctx)
            _, i, a, om = _gates(xc, wr_ref, br_, wi_ref, bi_, sp)
            a_s[...] = a
            b_s[...] = jnp.sqrt(om) * (i * xc)
            h0 = jnp.zeros((1, D), F32)
            hcar[0:1, :] = _scan_rows(d, t_ctx, a_s, b_s, h_s, h0)
            hs = h_s[...]
            row = lax.broadcasted_iota(jnp.int32, (t_ctx, D), 0)
            if d:
                hpc_ref[...] = jnp.where(row == t_ctx - 1, h0, pltpu.roll(hs, t_ctx - 1, 0))
            else:
                hpc_ref[...] = jnp.where(row == 0, h0, pltpu.roll(hs, 1, 0))

        @pl.when(s > 0)
        def _():
            for r0 in range(0, rows, rc):
                xc = _conv_cols(zx_ref, cw_ref, cb_ref, r0, rc, rows).reshape(rc * CB, D)
                _, i, a, om = _gates(xc, wr_ref, br_, wi_ref, bi_, sp)
                a3[r0:r0 + rc] = a.reshape(rc, CB, D)
                ax_ref[r0:r0 + rc] = a.reshape(rc, CB, D)
                b3[r0:r0 + rc] = (jnp.sqrt(om) * (i * xc)).reshape(rc, CB, D)

            def local(t, carry):
                hl, p = carry
                r = (rows - 1 - t) if d else t
                a = a3[r]
                hl = a * hl + b3[r]
                p = a * p
                b3[r] = hl
                a3[r] = p
                return hl, p

            hl, p = lax.fori_loop(0, rows, local, (jnp.zeros((CB, D), F32), jnp.ones((CB, D), F32)))
            cin = hcar[0:1, :]
            for j in (range(CB - 1, -1, -1) if d else range(CB)):
                cin_s[j:j + 1, :] = cin
                cin = hl[j:j + 1, :] + p[j:j + 1, :] * cin
            hcar[0:1, :] = cin
            c_in = cin_s[...]

            def fix(t, prev):
                r = (rows - 1 - t) if d else t
                h = b3[r] + a3[r] * c_in
                hx_ref[r] = h
                hpx_ref[r] = prev
                return h

            lax.fori_loop(0, rows, fix, c_in)

    full = lambda shp: pl.BlockSpec(shp, lambda s: (0,) * len(shp))
    colspec = pl.BlockSpec((rows, CB, D), lambda s: (0, blk(s), 0))
    outs = pl.pallas_call(
        body, grid=(nblk + 1,),
        in_specs=[pl.BlockSpec((t_ctx, D), lambda s: (cblk, 5)),
                  pl.BlockSpec((rows, CB, D), lambda s: (0, blk(s), 5)),
                  full((4, D)), full((1, D)),
                  pl.BlockSpec((None, H, DH, DH), lambda s: (d, 0, 0, 0)), full((2, D)),
                  pl.BlockSpec((None, H, DH, DH), lambda s: (d, 0, 0, 0)), full((2, D)), full((2, D))],
        out_specs=(colspec, colspec, colspec, full((t_ctx, D))),
        out_shape=(jax.ShapeDtypeStruct((rows, GRID_W, D), F32),) * 3 + (jax.ShapeDtypeStruct((t_ctx, D), F32),),
        scratch_shapes=[pltpu.VMEM((t_ctx, D), F32), pltpu.VMEM((t_ctx, D), F32), pltpu.VMEM((t_ctx, D), F32),
                        pltpu.VMEM((8, D), F32), pltpu.VMEM((rows, CB, D), F32), pltpu.VMEM((rows, CB, D), F32),
                        pltpu.VMEM((CB, D), F32)],
        name=f"rglru_fwd{d}", compiler_params=_cp(("arbitrary",)))(z, z3, cw, cb, wr, br, wi, bi, lam)
    return outs[0].reshape(t_lat, D), outs[1].reshape(t_lat, D), outs[2].reshape(t_lat, D), outs[3]


def _rglru_bwd(z, cw, cb, wr, br, wi, bi, lam, dh_lat, hp_lat, a_lat, hp_ctx, d, t_lat, t_ctx, prev=None):
    m = z.shape[0]
    rows = t_lat // GRID_W
    z3 = z.reshape(m // GRID_W, GRID_W, IN_COLS)
    nblk = GRID_W // CB
    fblk = _col_of(d, nblk)
    blk = lambda s: fblk(nblk - jnp.minimum(s, nblk - 1))
    cblk = t_lat // t_ctx
    rc = min(RCH, rows)
    last = prev is not None
    view3 = lambda v: v.reshape(rows, GRID_W, D)

    def body(*refs):
        (zc_ref, zx_ref, cw_ref, cb_ref, wr_ref, br_ref, wi_ref, bi_ref, lam_ref, dh_ref, hpx_ref, ax_ref,
         hpc_ref) = refs[:13]
        k = 13
        if last:
            pdx_ref, pdc_ref = refs[13:15]
            k = 15
        ox_ref, oc_ref, dwr_ref, dwi_ref, sum_ref, a_s, b_s, gcar, a3, b3, cin_s = refs[k:]
        s = pl.program_id(0)
        lam_d = lam_ref[d:d + 1, :]
        sp = _softplus_neg(lam_d)
        br_ = br_ref[d:d + 1, :]
        bi_ = bi_ref[d:d + 1, :]
        flat = lambda v: v.reshape(-1, D)

        @pl.when(s == 0)
        def _():
            gcar[...] = jnp.zeros_like(gcar)
            dwr_ref[...] = jnp.zeros_like(dwr_ref)
            dwi_ref[...] = jnp.zeros_like(dwi_ref)
            sum_ref[...] = jnp.zeros_like(sum_ref)


        def conv_sums(dxc, dxb, xm1, x0, xp1, xp2):
            sum_ref[3:4, :] += _colsum(flat(dxc))
            sum_ref[4:5, :] += _colsum(flat(dxb))
            sum_ref[8:9, :] += _colsum(flat(dxc * xm1))
            sum_ref[9:10, :] += _colsum(flat(dxc * x0))
            sum_ref[10:11, :] += _colsum(flat(dxc * xp1))
            sum_ref[11:12, :] += _colsum(flat(dxc * xp2))

        def gate_grads(g, hp, xc, rr, i, a, om):
            mult = jnp.sqrt(om)
            da = g * hp
            ixc = i * xc
            dmult = g * ixc
            dixc = g * mult
            di = dixc * xc
            dxc = dixc * i
            dlog_a = da * a - dmult * ((1.0 - om) / mult)
            dr = dlog_a * (-RG_C * sp)
            sum_ref[2:3, :] += _colsum(dlog_a * rr)
            drp = dr * rr * (1.0 - rr)
            dip = di * i * (1.0 - i)
            sum_ref[0:1, :] += _colsum(drp)
            sum_ref[1:2, :] += _colsum(dip)
            xcb = xc.astype(BF)
            drb = drp.astype(BF)
            dib = dip.astype(BF)
            parts = []
            for gi in range(H):
                gs = slice(gi * DH, (gi + 1) * DH)
                parts.append(_dot_nt(drb[:, gs], wr_ref[gi]) + _dot_nt(dib[:, gs], wi_ref[gi]))
                dwr_ref[gi] += _dot_tn(xcb[:, gs], drb[:, gs])
                dwi_ref[gi] += _dot_tn(xcb[:, gs], dib[:, gs])
            return dxc + jnp.concatenate(parts, axis=1)

        @pl.when(s < nblk)
        def _():
            def local(t, carry):
                c, q = carry
                r = t if d else (rows - 1 - t)
                a = ax_ref[r]
                c = a * (c + dh_ref[r])
                q = a * q
                b3[r] = c
                a3[r] = q
                return c, q

            c, q = lax.fori_loop(0, rows, local, (jnp.zeros((CB, D), F32), jnp.ones((CB, D), F32)))
            cin = gcar[0:1, :]
            for j in (range(CB) if d else range(CB - 1, -1, -1)):
                cin_s[j:j + 1, :] = cin
                cin = c[j:j + 1, :] + q[j:j + 1, :] * cin
            gcar[0:1, :] = cin
            c_in = cin_s[...]
            for r0 in (range(rows - rc, -1, -rc) if d else range(0, rows, rc)):
                if d:
                    lo = max(r0 - 1, 0)
                    cn = b3[lo:r0 + rc - 1] + a3[lo:r0 + rc - 1] * c_in
                    if r0 == 0:
                        cn = jnp.concatenate([c_in[None], cn], axis=0)
                else:
                    hi = min(r0 + rc + 1, rows)
                    cn = b3[r0 + 1:hi] + a3[r0 + 1:hi] * c_in
                    if hi == rows:
                        cn = jnp.concatenate([cn, c_in[None]], axis=0)
                g = flat(dh_ref[r0:r0 + rc] + cn)
                xc = flat(_conv_cols(zx_ref, cw_ref, cb_ref, r0, rc, rows))
                rr, i, a, om = _gates(xc, wr_ref, br_, wi_ref, bi_, sp)
                b3[r0:r0 + rc] = gate_grads(g, flat(hpx_ref[r0:r0 + rc]), xc, rr, i, a, om).reshape(rc, CB, D)
            if last:
                for r0 in range(0, rows, rc):
                    b3[r0:r0 + rc] = b3[r0:r0 + rc] + pdx_ref[r0:r0 + rc]
                for r0 in range(0, rows, rc):
                    w = _window(b3, r0 - 2, rc + 3, rows)
                    xw = _window(zx_ref, r0 - 1, rc + 3, rows)
                    dxc = w[2:rc + 2]
                    dxb = (w[3:rc + 3] * cw_ref[0:1, :] + dxc * cw_ref[1:2, :] + w[1:rc + 1] * cw_ref[2:3, :]
                           + w[0:rc] * cw_ref[3:4, :])
                    conv_sums(dxc, dxb, xw[0:rc], xw[1:rc + 1], xw[2:rc + 2], xw[3:rc + 3])
                    ox_ref[r0:r0 + rc] = dxb
            else:
                for r0 in range(0, rows, rc):
                    ox_ref[r0:r0 + rc] = b3[r0:r0 + rc]

        @pl.when(s == nblk)
        def _():
            r = t_ctx
            xb = zc_ref[...]
            xc = _conv_fwd(xb, cw_ref, cb_ref, r)
            rr, i, a, om = _gates(xc, wr_ref, br_, wi_ref, bi_, sp)
            a_s[...] = a
            b_s[...] = jnp.zeros((r, D), F32)
            c0 = gcar[0:1, :]
            _scan_rows(1 - d, r, a_s, b_s, b_s, c0)
            cs = b_s[...]
            row = lax.broadcasted_iota(jnp.int32, (r, D), 0)
            if d:
                g = jnp.where(row == 0, c0, pltpu.roll(cs, 1, 0))
            else:
                g = jnp.where(row == r - 1, c0, pltpu.roll(cs, r - 1, 0))
            dxc = gate_grads(g, hpc_ref[...], xc, rr, i, a, om)
            if last:
                dxc = dxc + pdc_ref[...]
                dxb = (_shift_rows(dxc, -1, r) * cw_ref[0:1, :] + dxc * cw_ref[1:2, :]
                       + _shift_rows(dxc, 1, r) * cw_ref[2:3, :] + _shift_rows(dxc, 2, r) * cw_ref[3:4, :])
                conv_sums(dxc, dxb, _shift_rows(xb, 1, r), xb, _shift_rows(xb, -1, r), _shift_rows(xb, -2, r))
                oc_ref[...] = dxb
            else:
                oc_ref[...] = dxc
            sum_ref[2:3, :] = sum_ref[2:3, :] * (RG_C * _sig(-lam_d))

    full = lambda shp: pl.BlockSpec(shp, lambda s: (0,) * len(shp))
    once = lambda shp: pl.BlockSpec(shp, lambda s: (0,) * len(shp), pipeline_mode=pl.Buffered(1))
    colspec = pl.BlockSpec((rows, CB, D), lambda s: (0, blk(s), 0))
    colonce = pl.BlockSpec((rows, CB, D), lambda s: (0, blk(s), 0), pipeline_mode=pl.Buffered(1))
    in_specs = [pl.BlockSpec((t_ctx, D), lambda s: (cblk, 5), pipeline_mode=pl.Buffered(1)),
                pl.BlockSpec((rows, CB, D), lambda s: (0, blk(s), 5), pipeline_mode=pl.Buffered(1)),
                full((4, D)), full((1, D)),
                pl.BlockSpec((None, H, DH, DH), lambda s: (d, 0, 0, 0)), full((2, D)),
                pl.BlockSpec((None, H, DH, DH), lambda s: (d, 0, 0, 0)), full((2, D)), full((2, D)),
                colspec, colspec, colonce, once((t_ctx, D))]
    args = [z, z3, cw, cb, wr, br, wi, bi, lam, view3(dh_lat), view3(hp_lat), view3(a_lat), hp_ctx]
    if last:
        in_specs += [colonce, once((t_ctx, D))]
        args += [view3(prev[0]), prev[1]]
    outs = pl.pallas_call(
        body, grid=(nblk + 1,), in_specs=in_specs,
        out_specs=(colspec, full((t_ctx, D)), full((H, DH, DH)), full((H, DH, DH)), full((16, D))),
        out_shape=(jax.ShapeDtypeStruct((rows, GRID_W, D), F32), jax.ShapeDtypeStruct((t_ctx, D), F32),
                   jax.ShapeDtypeStruct((H, DH, DH), F32), jax.ShapeDtypeStruct((H, DH, DH), F32),
                   jax.ShapeDtypeStruct((16, D), F32)),
        scratch_shapes=[pltpu.VMEM((t_ctx, D), F32), pltpu.VMEM((t_ctx, D), F32), pltpu.VMEM((8, D), F32),
                        pltpu.VMEM((rows, CB, D), F32), pltpu.VMEM((rows, CB, D), F32), pltpu.VMEM((CB, D), F32)],
        name=f"rglru_bwd{d}", compiler_params=_cp(("arbitrary",), VMEM_LIMIT_RGLRU_BWD))(*args)
    return (outs[0].reshape(t_lat, D), outs[1]) + tuple(outs[2:])


def _merge(o_f, o_b, h_f, h_b, z, x_all, tgt, mod, norm_g, ln_g, ln_b, p_a, p_b, w_out, t_lat):
    tm = 256
    nt = t_lat // tm

    def body(of_ref, ob_ref, hf_ref, hb_ref, z4_ref, z6_ref, z7_ref, z8_ref, x_ref, t_ref, mod_ref, ng_ref,
             lg_ref, lb_ref, pa_ref, pb_ref, wo_ref,
             do_ref, dh_ref, dz4a_ref, dz4b_ref, dz6a_ref, sh3_ref, gx_ref,
             y_ref, dout_ref, oa_ref, dpa_ref, obv_ref, dpb_ref, acc_ref):
        i = pl.program_id(0)
        lat = i < nt
        latf = lat.astype(F32)

        @pl.when(i == 0)
        def _():
            acc_ref[...] = jnp.zeros_like(acc_ref)

        def per_head(v):
            return jnp.concatenate(
                [jnp.broadcast_to(jnp.mean(v[:, h * DH:(h + 1) * DH], axis=-1, keepdims=True), (tm, DH))
                 for h in range(H)], axis=1)

        gt = mod_ref[0:1, 2 * D:3 * D]
        gfull = jnp.concatenate([ng_ref[...]] * H, axis=1)
        o = of_ref[...] + ob_ref[...]
        rinv = lax.rsqrt(per_head(o * o) + RMS_EPS)
        n = o * rinv
        na = n * gfull
        z4 = z4_ref[...]
        s4 = _sig(z4)
        silu4 = z4 * s4
        oa = na * silu4
        z6 = z6_ref[...]
        s6 = _sig(z6)
        silu6 = z6 * s6
        hsum = hf_ref[...] + hb_ref[...]
        obv = hsum * silu6
        pa = _dot(oa, pa_ref[...])
        pb = _dot(obv, pb_ref[...])
        s7 = _sig(z7_ref[...])
        s8 = _sig(z8_ref[...])
        y = s7 * pa + s8 * pb
        out = _dot(y, wo_ref[...])
        pre = ALPHA * x_ref[...] + gt * out
        mu = jnp.mean(pre, axis=-1, keepdims=True)
        xc = pre - mu
        rstd = lax.rsqrt(jnp.mean(xc * xc, axis=-1, keepdims=True) + LN_EPS)
        xhat = xc * rstd
        lg = lg_ref[...]
        diff = xhat * lg + lb_ref[...] - t_ref[...]
        acc_ref[8:9, :] += _colsum(diff * diff) * (0.5 / D * latf)
        dxn = diff * (1.0 / D)
        acc_ref[1:2, :] += _colsum(dxn * xhat) * latf
        acc_ref[2:3, :] += _colsum(dxn) * latf
        dxhat = dxn * lg
        dpre = rstd * (dxhat - jnp.mean(dxhat, axis=-1, keepdims=True)
                       - xhat * jnp.mean(dxhat * xhat, axis=-1, keepdims=True))
        gx_ref[...] = ALPHA * dpre
        acc_ref[0:1, :] += _colsum(dpre * out) * latf
        dout = dpre * gt
        dy = _dot_nt(dout, wo_ref[...])
        dpa = dy * s7
        dpb = dy * s8
        dz7 = dy * pa * (s7 * (1.0 - s7))
        dz8 = dy * pb * (s8 * (1.0 - s8))
        doa = _dot_nt(dpa, pa_ref[...])
        dob = _dot_nt(dpb, pb_ref[...])
        dh_ref[...] = dob * silu6
        dz6 = dob * hsum * (s6 * (1.0 + z6 * (1.0 - s6)))
        dna = doa * silu4
        dz4 = doa * na * (s4 * (1.0 + z4 * (1.0 - s4)))
        dng = _colsum(dna * n)
        acc_ref[7:8, 0:DH] += sum(dng[:, h * DH:(h + 1) * DH] for h in range(H)) * latf
        dn = dna * gfull
        do_ref[...] = rinv * (dn - n * per_head(dn * n))
        acc_ref[3:4, :] += _colsum(dz4) * latf
        acc_ref[4:5, :] += _colsum(dz6) * latf
        acc_ref[5:6, :] += _colsum(dz7) * latf
        acc_ref[6:7, :] += _colsum(dz8) * latf
        dz4b, dz6b = (dz4 * latf).astype(BF), (dz6 * latf).astype(BF)
        dz4a_ref[...] = dz4b[:, :CUT4]
        dz4b_ref[...] = dz4b[:, CUT4:]
        dz6a_ref[...] = dz6b[:, :CUT6]
        sh3_ref[:, 0:D - CUT6] = dz6b[:, CUT6:]
        sh3_ref[:, D - CUT6:2 * D - CUT6] = (dz7 * latf).astype(BF)
        sh3_ref[:, 2 * D - CUT6:] = (dz8 * latf).astype(BF)
        y_ref[...] = y.astype(BF)
        dout_ref[...] = dout.astype(BF)
        oa_ref[...] = oa.astype(BF)
        dpa_ref[...] = dpa.astype(BF)
        obv_ref[...] = obv.astype(BF)
        dpb_ref[...] = dpb.astype(BF)

        @pl.when(i == nt - 1)
        def _():
            acc_ref[9:10, :] = jnp.broadcast_to(jnp.sum(acc_ref[8:9, :], axis=-1, keepdims=True), (1, D))

    m = x_all.shape[0]
    lrow = lambda i: jnp.minimum(i, nt - 1)
    row = pl.BlockSpec((tm, D), lambda i: (lrow(i), 0))
    allrow = lambda cols: pl.BlockSpec((tm, cols), lambda i: (i, 0))
    zs = lambda cb: pl.BlockSpec((tm, D), lambda i: (lrow(i), cb))
    full = lambda shp: pl.BlockSpec(shp, lambda i: (0,) * len(shp))
    wfull = pl.BlockSpec((D, D), lambda i: (0, 0), pipeline_mode=pl.Buffered(1))
    f32o = jax.ShapeDtypeStruct((t_lat, D), F32)
    bfo = jax.ShapeDtypeStruct((t_lat, D), BF)
    bfall = lambda cols: jax.ShapeDtypeStruct((m, cols), BF)
    return pl.pallas_call(
        body, grid=(m // tm,),
        in_specs=[row, row, row, row, zs(4), zs(6), zs(7), zs(8), row, row, full((16, 3 * D)), full((1, DH)),
                  full((1, D)), full((1, D)), wfull, wfull, wfull],
        out_specs=(row, row) + tuple(allrow(c) for c in (CUT4, D - CUT4, CUT6, SHC)) + (row,) * 7 + (full((16, D)),),
        out_shape=(f32o, f32o, bfall(CUT4), bfall(D - CUT4), bfall(CUT6), bfall(SHC), f32o, bfo, bfo, bfo, bfo, bfo, bfo,
                   jax.ShapeDtypeStruct((16, D), F32)),
        name="merge", compiler_params=_cp(("arbitrary",), VMEM_LIMIT_MERGE))(
            o_f, o_b, h_f, h_b, z, z, z, z, x_all, tgt, mod, norm_g, ln_g, ln_b, p_a, p_b, w_out)


def _wgrad(a, b, name):
    tm = 1024

    def body(a_ref, b_ref, o_ref):
        @pl.when(pl.program_id(0) == 0)
        def _():
            o_ref[...] = jnp.zeros_like(o_ref)
        o_ref[...] += _dot_tn(a_ref[...], b_ref[...])

    row = pl.BlockSpec((tm, D), lambda i: (i, 0))
    return pl.pallas_call(body, grid=(a.shape[0] // tm,), in_specs=[row, row],
                          out_specs=pl.BlockSpec((D, D), lambda i: (0, 0)),
                          out_shape=jax.ShapeDtypeStruct((D, D), F32), name=name,
                          compiler_params=_cp(("arbitrary",)))(a, b)


def _pack_shard2(dz4b, dz5_lat, dz5_ctx, dz6a):
    m, t_lat, t_ctx = dz4b.shape[0], dz5_lat.shape[0], dz5_ctx.shape[0]
    tm = t_ctx
    nlt = t_lat // tm
    w4 = D - CUT4

    def body(a_ref, bl_ref, bc_ref, c_ref, o_ref):
        i = pl.program_id(0)
        o_ref[:, 0:w4] = a_ref[...]
        o_ref[:, w4:w4 + D] = jnp.where(i < nlt, bl_ref[...], bc_ref[...]).astype(BF)
        o_ref[:, w4 + D:] = c_ref[...]

    return pl.pallas_call(
        body, grid=(m // tm,),
        in_specs=[pl.BlockSpec((tm, w4), lambda i: (i, 0)),
                  pl.BlockSpec((tm, D), lambda i: (jnp.minimum(i, nlt - 1), 0)),
                  pl.BlockSpec((tm, D), lambda i: (0, 0)),
                  pl.BlockSpec((tm, CUT6), lambda i: (i, 0))],
        out_specs=pl.BlockSpec((tm, SHC), lambda i: (i, 0)),
        out_shape=jax.ShapeDtypeStruct((m, SHC), BF), name="pack_shard2",
        compiler_params=_cp(("arbitrary",)))(dz4b, dz5_lat, dz5_ctx, dz6a)


def _wgrad_in(u_all, dz_shards):
    m = u_all.shape[1]
    assert m % (6 * 128) == 0
    tm = m // 6
    out = None
    for k, dz_k in enumerate(dz_shards):
        def body(u_ref, dz_ref, *rest):
            o_ref = rest[-1]

            @pl.when(pl.program_id(0) == 0)
            def _():
                o_ref[...] = jnp.zeros_like(o_ref)
            o_ref[0] += _dot(u_ref[...], dz_ref[...])

        out = pl.pallas_call(
            body, grid=(m // tm,),
            in_specs=[pl.BlockSpec((D, tm), lambda i: (0, i)), pl.BlockSpec((tm, SHC), lambda i: (i, 0))]
            + ([] if out is None else [_ANY]),
            out_specs=pl.BlockSpec((1, D, SHC), lambda i, k=k: (k, 0, 0)),
            out_shape=jax.ShapeDtypeStruct((NSH, D, SHC), F32),
            input_output_aliases={} if out is None else {2: 0},
            name=f"wgrad_in{k}", compiler_params=_cp(("arbitrary",)))(u_all, dz_k, *(() if out is None else (out,)))
    return out


def _du(dz_shards, w_in_g, x_all, mod, gxres, n_lat_tiles, sums=()):
    m = x_all.shape[0]
    tm = 256
    nt = m // tm
    nct = nt - n_lat_tiles
    ns = len(sums)
    rblk = lambda i: jnp.where(i < nct, n_lat_tiles + i, i - nct)
    lblk = lambda i: jnp.maximum(i - nct, 0)

    def body(*refs):
        dz_refs, refs = refs[:NSH], refs[NSH - 1:]
        _, w_ref, x_ref, mod_ref, gr_ref = refs[:5]
        sum_refs = refs[5:5 + ns]
        gx_ref, dm_ref = refs[5 + ns:7 + ns]
        got_refs = refs[7 + ns:7 + 2 * ns]
        sems = refs[7 + 2 * ns:]
        i = pl.program_id(0)
        is_lat = i >= nct

        @pl.when(i == 0)
        def _():
            dm_ref[...] = jnp.zeros_like(dm_ref)
            if ns:
                for cp in _rs_chip_copies(sum_refs, got_refs, *sems):
                    cp.start()

        du = _dot_nt(dz_refs[0][...], w_ref[0])
        for n in range(1, NSH):
            du = du + _dot_nt(dz_refs[n][...], w_ref[n])
        sc = jnp.where(is_lat, mod_ref[0:1, D:2 * D], mod_ref[1:2, D:2 * D])
        dsh = _colsum(du)
        dsc = _colsum(du * x_ref[...])

        @pl.when(is_lat)
        def _():
            gx_ref[...] = du * (1.0 + sc) + gr_ref[...]
            dm_ref[0:1, 0:D] += dsh
            dm_ref[0:1, D:2 * D] += dsc

        @pl.when(jnp.logical_not(is_lat))
        def _():
            dm_ref[1:2, 0:D] += dsh
            dm_ref[1:2, D:2 * D] += dsc

        if ns:
            @pl.when(i == nt - 1)
            def _():
                for cp in _rs_chip_copies(sum_refs, got_refs, *sems):
                    cp.wait()

    outs = pl.pallas_call(
        body, grid=(nt,),
        in_specs=[pl.BlockSpec((tm, SHC), lambda i: (rblk(i), 0))] * NSH + [
                  pl.BlockSpec((NSH, D, SHC), lambda i: (0, 0, 0), pipeline_mode=pl.Buffered(1)),
                  pl.BlockSpec((tm, D), lambda i: (rblk(i), 0)),
                  pl.BlockSpec((16, 3 * D), lambda i: (0, 0)),
                  pl.BlockSpec((tm, D), lambda i: (lblk(i), 0))] + [_ANY] * ns,
        out_specs=[pl.BlockSpec((tm, D), lambda i: (lblk(i), 0)),
                   pl.BlockSpec((8, 2 * D), lambda i: (0, 0))] + [_ANY] * ns,
        out_shape=[jax.ShapeDtypeStruct((n_lat_tiles * tm, D), F32), jax.ShapeDtypeStruct((8, 2 * D), F32)]
        + [jax.ShapeDtypeStruct((3,) + g.shape[1:], g.dtype) for g in sums],
        scratch_shapes=[pltpu.SemaphoreType.DMA((3 * ns,)), pltpu.SemaphoreType.DMA((3 * ns,))] if ns else [],
        name="du", compiler_params=_cp(("arbitrary",)))(*dz_shards, w_in_g, x_all, mod, gxres, *sums)
    return outs[0], outs[1], list(outs[2:])


def _row_tile(rows, cols):
    t = 8
    while t * 2 * cols * 4 <= (1 << 20) and rows % (t * 2) == 0:
        t *= 2
    return t


def _adamw_update(w_ref, g_ref, m_ref, v_ref, d_ref, nm_ref, nv_ref):
    gg = g_ref[...]
    m2 = ADAM_B1 * m_ref[...] + (1.0 - ADAM_B1) * gg
    v2 = ADAM_B2 * v_ref[...] + (1.0 - ADAM_B2) * (gg * gg)
    m_hat = m2 / (1.0 - ADAM_B1 ** ADAM_STEP)
    v_hat = v2 / (1.0 - ADAM_B2 ** ADAM_STEP)
    d_ref[...] = -ADAM_LR * (m_hat / (jnp.sqrt(v_hat) + ADAM_EPS) + ADAM_WD * w_ref[...])
    nm_ref[...] = m2
    nv_ref[...] = v2


def _adamw_many(ws, gs, ms, vs):
    n = len(ws)

    def body(*refs):
        for j in range(n):
            _adamw_update(*refs[4 * j:4 * j + 4], *refs[4 * n + 3 * j:4 * n + 3 * j + 3])

    args = [a for quad in zip(ws, gs, ms, vs) for a in quad]
    outs = pl.pallas_call(body, out_shape=[jax.ShapeDtypeStruct(w.shape, F32) for w in ws for _ in range(3)],
                          name="adamw_small", compiler_params=_cp())(*args)
    return outs[0::3], outs[1::3], outs[2::3]


def _adamw(w, g, m, v, name):
    rows, cols = w.shape
    tr = _row_tile(rows, cols)

    def body(*refs):
        _adamw_update(*refs)

    spec = pl.BlockSpec((tr, cols), lambda i: (i, 0))
    o = jax.ShapeDtypeStruct((rows, cols), F32)
    return pl.pallas_call(body, grid=(rows // tr,), in_specs=[spec] * 4, out_specs=(spec,) * 3,
                          out_shape=(o, o, o), name=name, compiler_params=_cp(("arbitrary",)))(w, g, m, v)


_ANY = pl.BlockSpec(memory_space=pl.ANY)


def _place():
    return lax.axis_index("x"), lax.axis_index("y"), lax.axis_index("c")


def _ag_copies(ins, outs, send, recv, fsend, frecv, lsem):
    x, y, c = _place()
    me = 2 * x + y
    chips = ((1 - x, y), (x, 1 - y), (1 - x, 1 - y))
    local, chip, hand = [], [], []
    for j in range(len(ins)):
        hr = ins[j].shape[0] // 2
        half = pl.ds(pl.multiple_of(c * hr, 8), hr)
        local.append(pltpu.make_async_copy(ins[j], outs[j].at[me], lsem.at[j]))
        for k, (px, py) in enumerate(chips):
            chip.append(pltpu.make_async_remote_copy(
                src_ref=ins[j].at[half, :], dst_ref=outs[j].at[me, half, :], send_sem=send.at[3 * j + k],
                recv_sem=recv.at[3 * j + k], device_id=(px, py, c), device_id_type=MESH))
            got = outs[j].at[2 * px + py, half, :]
            hand.append(pltpu.make_async_remote_copy(
                src_ref=got, dst_ref=got, send_sem=fsend.at[3 * j + k], recv_sem=frecv.at[3 * j + k],
                device_id=(x, y, 1 - c), device_id_type=MESH))
    return local, chip, hand


def _ag_sems(n):
    return [pltpu.SemaphoreType.DMA((3 * n,))] * 4 + [pltpu.SemaphoreType.DMA((n,))]


def _ag_finish(local, chip, hand, done=0):
    for k in range(done, len(chip)):
        chip[k].wait_recv()
        hand[k].start()
    for cp in chip:
        cp.wait_send()
    for k in range(done):
        hand[k].wait_send()
    for k in range(done, len(chip)):
        hand[k].wait_send()
        hand[k].wait_recv()
    for cp in local:
        cp.wait()


def _mod_tp(c8, c_ctx, w_mod_sh, b_mod_sh):
    mc = w_mod_sh.shape[1]

    def body(c8_ref, cctx_ref, w_ref, b_ref, mod_ref, cc_ref, cc_s, part_s, send1, recv1, send3, recv3):
        x, y, c = _place()
        me = 4 * x + 2 * y + c
        ms = 2 * x + y
        copies = []
        for k in range(1, 8):
            peer = (x ^ ((k >> 2) & 1), y ^ ((k >> 1) & 1), c ^ (k & 1))
            cp = pltpu.make_async_remote_copy(src_ref=c8_ref, dst_ref=cc_s.at[me], send_sem=send1.at[k],
                                              recv_sem=recv1.at[k], device_id=peer, device_id_type=MESH)
            cp.start()
            copies.append(cp)
        cc_s[me] = c8_ref[...]
        for cp in copies:
            cp.wait()
        cc_ref[...] = jnp.zeros_like(cc_ref)
        for j in range(8):
            cc_ref[j:j + 1, :] = cc_s[j, 0:1, :]
        cc_ref[8:9, :] = cctx_ref[...]
        v = cc_ref[...]
        part_s[ms] = _dot(v * _sig(v), w_ref[...]) + b_ref[...]
        copies = []
        for k in range(1, 4):
            peer = (x ^ ((k >> 1) & 1), y ^ (k & 1), c)
            cp = pltpu.make_async_remote_copy(src_ref=part_s.at[ms], dst_ref=part_s.at[ms], send_sem=send3.at[k],
                                              recv_sem=recv3.at[k], device_id=peer, device_id_type=MESH)
            cp.start()
            copies.append(cp)
        for cp in copies:
            cp.wait()
        for s in range(NSH):
            mod_ref[:, s * mc:(s + 1) * mc] = part_s[s]

    vm = pl.BlockSpec(memory_space=pltpu.VMEM)
    return pl.pallas_call(
        body, in_specs=[vm] * 4, out_specs=(vm, vm),
        out_shape=(jax.ShapeDtypeStruct((16, NSH * mc), F32), jax.ShapeDtypeStruct((16, D), F32)),
        scratch_shapes=[pltpu.VMEM((8, 8, D), F32), pltpu.VMEM((NSH, 16, mc), F32),
                        pltpu.SemaphoreType.DMA((8,)), pltpu.SemaphoreType.DMA((8,)),
                        pltpu.SemaphoreType.DMA((4,)), pltpu.SemaphoreType.DMA((4,))],
        name="mod_tp", compiler_params=_cp())(c8, c_ctx, w_mod_sh, b_mod_sh)


def _inproj_ag(x_all, mod, w_in_sh, b_in, narrow_sh, n_lat_tiles):
    m = x_all.shape[0]
    assert m % (11 * 16) == 0
    tm = m // 11
    nt = m // tm
    n_lat = n_lat_tiles * 256
    x_, y_ = lax.axis_index("x"), lax.axis_index("y")
    sids = jnp.stack([2 * x_ + y_, 2 * (1 - x_) + y_, 2 * x_ + 1 - y_, 2 * (1 - x_) + 1 - y_]).astype(jnp.int32)

    def body(sid_ref, x_ref, mod_ref, b_ref, wsh_ref, nsh_ref, z_ref, u_ref, wg_ref, ng_ref, w_s, u_s, *sems):
        n = pl.program_id(0)
        i = pl.program_id(1)
        rows = pl.ds(pl.multiple_of(i * tm, tm), tm)
        ag = ((wsh_ref, nsh_ref), (wg_ref, ng_ref)) + tuple(sems[:5])
        wsem = sems[5]

        def load(src):
            cp = pltpu.make_async_copy(src, w_s, wsem)
            cp.start()
            cp.wait()

        @pl.when((n == 0) & (i == 0))
        def _():
            local, chip, _ = _ag_copies(*ag)
            for cp in chip + local:
                cp.start()
            load(wsh_ref)

        for k in range(NSH - 1):
            @pl.when((n == k + 1) & (i == 0))
            def _():
                _, chip, hand = _ag_copies(*ag)
                chip[k].wait_recv()
                hand[k].start()
                hand[k].wait_recv()
                load(wg_ref.at[sid_ref[k + 1]])

        @pl.when(n == 0)
        def _():
            is_lat = (i * tm + lax.broadcasted_iota(jnp.int32, (tm, 1), 0)) < n_lat
            u, _ = _modulate(x_ref, mod_ref, is_lat)
            u_s[rows, :] = u.astype(BF)
            u_ref[...] = u.T.astype(BF)

        z_ref[...] = _dot(u_s[rows, :], w_s[...]) + b_ref[...]

        @pl.when((n == NSH - 1) & (i == nt - 1))
        def _():
            _ag_finish(*_ag_copies(*ag), done=NSH - 1)

    first = lambda n, i: jnp.where(n == 0, i, nt - 1)
    outs = pl.pallas_call(
        body, grid_spec=pltpu.PrefetchScalarGridSpec(
            num_scalar_prefetch=1, grid=(NSH, nt),
            in_specs=[pl.BlockSpec((tm, D), lambda n, i, sid: (first(n, i), 0)),
                      pl.BlockSpec((16, 3 * D), lambda n, i, sid: (0, 0)),
                      pl.BlockSpec((1, SHC), lambda n, i, sid: (0, sid[n])), _ANY, _ANY],
            out_specs=[pl.BlockSpec((tm, SHC), lambda n, i, sid: (i, sid[n])),
                       pl.BlockSpec((D, tm), lambda n, i, sid: (0, first(n, i))), _ANY, _ANY],
            scratch_shapes=[pltpu.VMEM((D, SHC), BF), pltpu.VMEM((m, D), BF)] + _ag_sems(2)
            + [pltpu.SemaphoreType.DMA]),
        out_shape=[jax.ShapeDtypeStruct((m, IN_COLS), F32), jax.ShapeDtypeStruct((D, m), BF),
                   jax.ShapeDtypeStruct((NSH,) + w_in_sh.shape, BF),
                   jax.ShapeDtypeStruct((NSH,) + narrow_sh.shape, narrow_sh.dtype)],
        name="inproj_ag", compiler_params=_cp(("arbitrary", "arbitrary")))(sids, x_all, mod, b_in, w_in_sh, narrow_sh)
    return outs


def _rs_sibling(grads, name):
    n = len(grads)

    def body(*refs):
        ins, got = refs[:n], refs[n:2 * n]
        send, recv = refs[2 * n:]
        x, y, c = _place()
        copies = []
        for j in range(n):
            hr = ins[j].shape[1] // 2
            for s in range(NSH):
                give = ins[j].at[s, pl.ds(pl.multiple_of((1 - c) * hr, 8), hr), :]
                cp = pltpu.make_async_remote_copy(src_ref=give, dst_ref=got[j].at[s], send_sem=send.at[NSH * j + s],
                                                  recv_sem=recv.at[NSH * j + s], device_id=(x, y, 1 - c),
                                                  device_id_type=MESH)
                cp.start()
                copies.append(cp)
        for cp in copies:
            cp.wait()

    half = [jax.ShapeDtypeStruct((NSH, g.shape[1] // 2, g.shape[2]), F32) for g in grads]
    return pl.pallas_call(
        body, in_specs=[_ANY] * n, out_specs=[_ANY] * n, out_shape=half,
        scratch_shapes=[pltpu.SemaphoreType.DMA((NSH * n,)), pltpu.SemaphoreType.DMA((NSH * n,))],
        name=name)(*grads)


def _core_vec():
    return lax.axis_index("c").astype(jnp.int32).reshape(1)


def _rs_add1(g, got, name):
    _, r, cols = g.shape
    hr = r // 2
    tr = _row_tile(hr, cols)
    nb = hr // tr

    def body(c_ref, g_ref, got_ref, o_ref):
        o_ref[...] = (g_ref[...] + got_ref[...]).astype(BF)

    spec = pl.BlockSpec((1, tr, cols), lambda s, i, c_ref: (s, i, 0))
    return pl.pallas_call(
        body, grid_spec=pltpu.PrefetchScalarGridSpec(
            num_scalar_prefetch=1, grid=(NSH, nb),
            in_specs=[pl.BlockSpec((1, tr, cols), lambda s, i, c_ref: (s, c_ref[0] * nb + i, 0)), spec],
            out_specs=spec),
        out_shape=jax.ShapeDtypeStruct((NSH, hr, cols), BF), name=name,
        compiler_params=_cp(("arbitrary", "arbitrary")))(_core_vec(), g, got)


def _rs_add2(sums, got, name):
    _, hr, cols = sums.shape
    tr = _row_tile(hr, cols)
    nb = hr // tr
    place = jnp.stack([2 * lax.axis_index("x") + lax.axis_index("y"), lax.axis_index("c")]).astype(jnp.int32)

    def body(p_ref, s_ref, got_ref, o_ref):
        f = lambda v: v.astype(F32)
        o_ref[...] = f(s_ref[0]) + f(got_ref[0]) + f(got_ref[1]) + f(got_ref[2])

    return pl.pallas_call(
        body, grid_spec=pltpu.PrefetchScalarGridSpec(
            num_scalar_prefetch=1, grid=(nb,),
            in_specs=[pl.BlockSpec((1, tr, cols), lambda i, p_ref: (p_ref[0], i, 0)),
                      pl.BlockSpec((3, tr, cols), lambda i, p_ref: (0, i, 0))],
            out_specs=pl.BlockSpec((tr, cols), lambda i, p_ref: (p_ref[1] * nb + i, 0))),
        out_shape=jax.ShapeDtypeStruct((2 * hr, cols), F32), name=name,
        compiler_params=_cp(("arbitrary",)))(place, sums, got)


def _rs_chip_copies(ins, got, send, recv):
    x, y, c = _place()
    peers = ((1 - x, y), (x, 1 - y), (1 - x, 1 - y))
    return [pltpu.make_async_remote_copy(src_ref=ins[j].at[2 * px + py], dst_ref=got[j].at[k],
                                         send_sem=send.at[3 * j + k], recv_sem=recv.at[3 * j + k],
                                         device_id=(px, py, c), device_id_type=MESH)
            for j in range(len(ins)) for k, (px, py) in enumerate(peers)]


def _ag_sibling(fulls):
    n = len(fulls)
    nck = 4

    def body(*refs):
        outs = refs[n:2 * n]
        send, recv = refs[2 * n:]
        x, y, c = _place()
        copies = []
        for j in range(n):
            qr = outs[j].shape[0] // (2 * nck)
            for k in range(nck):
                rows = outs[j].at[pl.ds(pl.multiple_of((c * nck + k) * qr, 8), qr), :]
                cp = pltpu.make_async_remote_copy(src_ref=rows, dst_ref=rows, send_sem=send.at[nck * j + k],
                                                  recv_sem=recv.at[nck * j + k], device_id=(x, y, 1 - c),
                                                  device_id_type=MESH)
                cp.start()
                copies.append(cp)
        for cp in copies:
            cp.wait()

    return pl.pallas_call(
        body, in_specs=[_ANY] * n, out_specs=[_ANY] * n,
        out_shape=[jax.ShapeDtypeStruct(f.shape, F32) for f in fulls],
        input_output_aliases={j: j for j in range(n)},
        scratch_shapes=[pltpu.SemaphoreType.DMA((nck * n,)), pltpu.SemaphoreType.DMA((nck * n,))],
        name="ag_sibling")(*fulls)


def _allreduce_small(buf):
    rows = buf.shape[0]
    pr = rows // 8

    def body(in_ref, out_ref, stage, send1, recv1, send2, recv2):
        x, y, c = _place()
        me = 4 * x + 2 * y + c

        def peer(k):
            kx, ky, kc = (k >> 2) & 1, (k >> 1) & 1, k & 1
            return (x ^ kx, y ^ ky, c ^ kc)

        def piece(ref, idx):
            return ref.at[pl.ds(pl.multiple_of(idx * pr, 8), pr), :]

        copies = []
        for k in range(1, 8):
            px, py, pc = peer(k)
            cp = pltpu.make_async_remote_copy(src_ref=piece(in_ref, 4 * px + 2 * py + pc), dst_ref=stage.at[k],
                                              send_sem=send1.at[k], recv_sem=recv1.at[k],
                                              device_id=(px, py, pc), device_id_type=MESH)
            cp.start()
            copies.append(cp)
        for cp in copies:
            cp.wait()
        acc = piece(in_ref, me)[...]
        for k in range(1, 8):
            acc = acc + stage[k]
        piece(out_ref, me)[...] = acc
        copies = []
        for k in range(1, 8):
            cp = pltpu.make_async_remote_copy(src_ref=piece(out_ref, me), dst_ref=piece(out_ref, me),
                                              send_sem=send2.at[k], recv_sem=recv2.at[k],
                                              device_id=peer(k), device_id_type=MESH)
            cp.start()
            copies.append(cp)
        for cp in copies:
            cp.wait()

    vm = pl.BlockSpec(memory_space=pltpu.VMEM)
    return pl.pallas_call(
        body, in_specs=[vm], out_specs=vm, out_shape=jax.ShapeDtypeStruct((rows, D), F32),
        scratch_shapes=[pltpu.VMEM((8, pr, D), F32)] + [pltpu.SemaphoreType.DMA((8,))] * 4,
        name="allreduce_small", compiler_params=_cp())(buf)


def _rows(a):
    flat = a.reshape(-1)
    pad = (-flat.shape[0]) % D
    if pad:
        flat = jnp.concatenate([flat, jnp.zeros((pad,), flat.dtype)])
    return flat.reshape(-1, D)


def _pad_rows(a, mult):
    pad = (-a.shape[0]) % mult
    return jnp.concatenate([a, jnp.zeros((pad, a.shape[1]), a.dtype)]) if pad else a


def _local_step(x, c, ctx, c_ctx, tgt, me, shard, sh, b_mod, b_in, norm_g, cb, wr, wi, ln_g, ln_b):
    t_lat, t_ctx = x.shape[0], ctx.shape[0]
    nlt = t_lat // 256
    nlb, ncb = t_lat // RB, t_ctx // RB
    mc = 3 * D // NSH
    mod_all, cc_all = _mod_tp(jnp.zeros((8, D), F32).at[0].set(c), c_ctx.reshape(1, D), sh["w_mod"],
                              lax.dynamic_slice_in_dim(b_mod, shard * mc, mc, axis=1))
    mod = jnp.zeros((16, 3 * D), F32).at[0].set(mod_all[me]).at[1].set(mod_all[8])
    cc = jnp.zeros((16, D), F32).at[0].set(c).at[1].set(c_ctx)
    x_all = jnp.concatenate([x, ctx], axis=0)
    z, u_all, w_in_g, nar = _inproj_ag(x_all, mod, sh["w_in"], b_in, sh["narrow"], nlt)
    nar = jnp.transpose(nar, (1, 0, 2)).reshape(-1, D)
    lbl, cw, br, bi, lam = nar[0:4].reshape(2, 2, D), nar[4:8], nar[8:10], nar[10:12], nar[12:14]
    o0, st0, (w_mod_g, p_a, p_b, w_out) = _gla_fwd(z, lbl, 0, nlb, ncb,
                                                   gather=[sh[k] for k in ("w_mod", "p_a", "p_b", "w_out")])
    p_a, p_b, w_out = p_a.reshape(D, D), p_b.reshape(D, D), w_out.reshape(D, D)
    o1, st1, _ = _gla_fwd(z, lbl, 1, nlb, ncb)
    h0, hp0, a0, hpc0 = _rglru_fwd(z, cw, cb, wr, br, wi, bi, lam, 0, t_lat, t_ctx)
    h1, hp1, a1, hpc1 = _rglru_fwd(z, cw, cb, wr, br, wi, bi, lam, 1, t_lat, t_ctx)
    (do, dh, dz4a, dz4b, dz6a, dz_sh3, gxres, y, dout, oa, dpa, obv, dpb, acc) = _merge(
        o0, o1, h0, h1, z, x_all, tgt, mod, norm_g, ln_g, ln_b, p_a, p_b, w_out, t_lat)
    gp_a = _wgrad(oa, dpa, "wgrad_pa")
    gp_b = _wgrad(obv, dpb, "wgrad_pb")
    gw_out = _wgrad(y, dout, "wgrad_wout")
    dxc_lat, dxc_ctx, dwr0, dwi0, sb0 = _rglru_bwd(z, cw, cb, wr, br, wi, bi, lam, dh, hp0, a0, hpc0, 0, t_lat, t_ctx)
    dz5_lat, dz5_ctx, dwr1, dwi1, sb1 = _rglru_bwd(z, cw, cb, wr, br, wi, bi, lam, dh, hp1, a1, hpc1, 1, t_lat, t_ctx,
                                                   prev=(dxc_lat, dxc_ctx))
    dq0, dv0, dz1, sa0 = _gla_bwd(z, lbl, do, st0, 0, nlb, ncb)
    dz_sh0, dz_sh1, sa1 = _gla_bwd(z, lbl, do, st1, 1, nlb, ncb, prev=(dq0, dv0, dz1, dz4a))
    dz = (dz_sh0, dz_sh1, _pack_shard2(dz4b, dz5_lat, dz5_ctx, dz6a), dz_sh3)
    grads = [_wgrad_in(u_all, dz)] + [g.reshape(NSH, D // NSH, D) for g in (gp_a, gp_b, gw_out)]
    sums = [_rs_add1(g, b, f"rs_add1_{j}") for j, (g, b) in enumerate(zip(grads, _rs_sibling(grads, "rs_sibling")))]
    gx, dm, got = _du(dz, w_in_g, x_all, mod, gxres, nlt, sums)
    fulls = [_rs_add2(a, b, f"rs_add2_{j}") for j, (a, b) in enumerate(zip(sums, got))]
    big = dict(zip(_RS, _ag_sibling(fulls)))
    dmod = jnp.zeros((16, 3 * D), F32).at[0:2, 0:2 * D].set(dm[0:2]).at[0, 2 * D:].set(acc[0])
    dcc = _mod_bwd(cc, dmod, w_mod_g)
    small = dict(
        c_ctx=dcc[1:2], b_mod=(dmod[0] + dmod[1]).reshape(3, D),
        b_in=jnp.stack([sa1[2], sa0[0], sa1[0], sa1[3], acc[3], sb1[4], acc[4], acc[5], acc[6]]),
        lb_logits=jnp.stack([sa0[1], sa1[1], -sa0[1], -sa1[1]]),
        norm_a_g=acc[7:8], conv_w=sb1[8:12], conv_b=sb1[3:4],
        w_r=jnp.stack([dwr0, dwr1]).reshape(-1, D), w_i=jnp.stack([dwi0, dwi1]).reshape(-1, D),
        b_r=jnp.stack([sb0[0], sb1[0]]), b_i=jnp.stack([sb0[1], sb1[1]]), lam=jnp.stack([sb0[2], sb1[2]]),
        ln_g=acc[1:2], ln_b=acc[2:3])
    return acc[9, 0], gx, big, small, dmod, cc_all


_RS =("w_in", "p_a", "p_b", "w_out")
_SMALL =("c_ctx", "b_mod", "b_in", "lb_logits", "norm_a_g", "conv_w", "conv_b", "w_r", "w_i", "b_r", "b_i", "lam",
          "ln_g", "ln_b")
_BIG = ("w_mod", "w_in", "p_a", "p_b", "w_out")
_COL_SHARDED = ("lb_logits", "conv_w", "b_r", "b_i", "lam")
_WEIGHTS = ("c_ctx", "w_mod", "b_mod", "w_in", "b_in", "lb_logits", "norm_a_g", "conv_w", "conv_b", "w_r", "b_r", "w_i",
            "b_i", "lam", "p_a", "p_b", "w_out", "ln_g", "ln_b")


def kernel(x, c, ctx, c_ctx, w_mod, b_mod, w_in, b_in, lb_logits, norm_a_g, conv_w, conv_b, w_r, b_r, w_i, b_i, lam, p_a, p_b, w_out, ln_g, ln_b, loss_target, m_c_ctx, m_w_mod, m_b_mod, m_w_in, m_b_in, m_lb_logits, m_norm_a_g, m_conv_w, m_conv_b, m_w_r, m_b_r, m_w_i, m_b_i, m_lam, m_p_a, m_p_b, m_w_out, m_ln_g, m_ln_b, v_c_ctx, v_w_mod, v_b_mod, v_w_in, v_b_in, v_lb_logits, v_norm_a_g, v_conv_w, v_conv_b, v_w_r, v_b_r, v_w_i, v_b_i, v_lam, v_p_a, v_p_b, v_w_out, v_ln_g, v_ln_b):
    w = dict(c_ctx=c_ctx, w_mod=w_mod, b_mod=b_mod, w_in=w_in, b_in=b_in, lb_logits=lb_logits, norm_a_g=norm_a_g,
             conv_w=conv_w, conv_b=conv_b, w_r=w_r, b_r=b_r, w_i=w_i, b_i=b_i, lam=lam, p_a=p_a, p_b=p_b, w_out=w_out,
             ln_g=ln_g, ln_b=ln_b)
    m = dict(c_ctx=m_c_ctx, w_mod=m_w_mod, b_mod=m_b_mod, w_in=m_w_in, b_in=m_b_in, lb_logits=m_lb_logits,
             norm_a_g=m_norm_a_g, conv_w=m_conv_w, conv_b=m_conv_b, w_r=m_w_r, b_r=m_b_r, w_i=m_w_i, b_i=m_b_i,
             lam=m_lam, p_a=m_p_a, p_b=m_p_b, w_out=m_w_out, ln_g=m_ln_g, ln_b=m_ln_b)
    v = dict(c_ctx=v_c_ctx, w_mod=v_w_mod, b_mod=v_b_mod, w_in=v_w_in, b_in=v_b_in, lb_logits=v_lb_logits,
             norm_a_g=v_norm_a_g, conv_w=v_conv_w, conv_b=v_conv_b, w_r=v_w_r, b_r=v_b_r, w_i=v_w_i, b_i=v_b_i,
             lam=v_lam, p_a=v_p_a, p_b=v_p_b, w_out=v_w_out, ln_g=v_ln_g, ln_b=v_ln_b)
    shard = 2 * lax.axis_index("x") + lax.axis_index("y")
    cs = D // NSH

    sh = {k: w[k][0].astype(BF) for k in _BIG}
    sh["narrow"] = _pad_rows(jnp.concatenate([lb_logits.reshape(4, cs), conv_w[0], b_r[0], b_i[0], lam[0]], axis=0), 8)
    me = 2 * shard + lax.axis_index("c")
    loss, gx, big, small, dmod, cc_all = _local_step(
        x[0], c[0], ctx[0], c_ctx, loss_target[0], me, shard, sh, b_mod, b_in, norm_a_g, conv_b, w_r[0], w_i[0],
        ln_g, ln_b)
    loss = lax.psum(loss, ("x", "y", "c"))

    dmod_rows = jnp.zeros((16, 3 * D), F32).at[me].set(dmod[0]).at[8].set(dmod[1]).reshape(48, D)
    sizes = [small[k].shape[0] for k in _SMALL]
    red = _allreduce_small(_pad_rows(jnp.concatenate([_pad_rows(small[k], 8) for k in _SMALL] + [dmod_rows],
                                                     axis=0), 64))
    grads = {}
    off = 0
    for k, n in zip(_SMALL, sizes):
        g = red[off:off + n]
        off += n + (-n) % 8
        if k == "norm_a_g":
            g = g[:, :DH]
        if k in _COL_SHARDED:
            g = lax.dynamic_slice_in_dim(g, shard * cs, cs, axis=1)
        grads[k] = g.reshape(w[k].shape)
    for k in _RS:
        grads[k] = big[k].reshape(w[k].shape)
    dmod_all = red[off:off + 48].reshape(16, 3 * D)
    mc = 3 * D // NSH
    grads["w_mod"] = _wmod_grad(cc_all, lax.dynamic_slice_in_dim(dmod_all, shard * mc, mc, axis=1)).reshape(
        w["w_mod"].shape)

    delta, new_m, new_v = {}, {}, {}
    for k in _BIG:
        shp = w[k].shape
        two = lambda a: a.reshape(shp[-2], shp[-1])
        d_, m_, v_ = _adamw(two(w[k]), two(grads[k]), two(m[k]), two(v[k]), f"adamw_{k}")
        delta[k], new_m[k], new_v[k] = d_.reshape(shp), m_.reshape(shp), v_.reshape(shp)
    d_, m_, v_ = _adamw_many(*[[t[k] for k in _SMALL] for t in (w, grads, m, v)])
    delta.update(zip(_SMALL, d_))
    new_m.update(zip(_SMALL, m_))
    new_v.update(zip(_SMALL, v_))

    return (loss, gx[None], *[grads[k] for k in _WEIGHTS], *[delta[k] for k in _WEIGHTS],
            *[new_m[k] for k in _WEIGHTS], *[new_v[k] for k in _WEIGHTS])
```

```python
import functools

import jax
import jax.numpy as jnp
from jax import lax
from jax.experimental import pallas as pl
from jax.experimental.pallas import tpu as pltpu

F32 = jnp.float32
BF = jnp.bfloat16
MESH = pl.DeviceIdType.MESH

D = 1024
H = 8
DH = 128
CH = 64
RB = 256
NCK = RB // CH
GRID_W = 64
CB = 8
RCH = 16
IN_COLS = 9 * D
NSH = 4
SHC = IN_COLS // NSH
CUT2 = SHC - 2 * D
CUT4 = 2 * SHC - 4 * D
CUT6 = 3 * SHC - 6 * D
RG_C = 8.0
ALPHA = 2.0 ** 0.25
LN_EPS = 1e-5
RMS_EPS = 1e-6
Q_SCALE = DH ** -0.5
ADAM_LR, ADAM_B1, ADAM_B2, ADAM_EPS, ADAM_WD, ADAM_STEP = 0.001, 0.9, 0.999, 1e-08, 0.01, 10
VMEM_LIMIT = 56 * 1024 * 1024
VMEM_LIMIT_MERGE = 60 * 1024 * 1024
VMEM_LIMIT_RGLRU_BWD = 63 * 1024 * 1024


def _cp(sem=None, vmem=VMEM_LIMIT):
    return pltpu.CompilerParams(dimension_semantics=sem, vmem_limit_bytes=vmem)


def _sig(x):
    return 0.5 * jnp.tanh(0.5 * x) + 0.5


def _dot(a, b):
    return jnp.dot(a.astype(BF), b.astype(BF), preferred_element_type=F32)


def _dot_nt(a, b):
    return lax.dot_general(a.astype(BF), b.astype(BF), (((1,), (1,)), ((), ())), preferred_element_type=F32)


def _dot_tn(a, b):
    return lax.dot_general(a.astype(BF), b.astype(BF), (((0,), (0,)), ((), ())), preferred_element_type=F32)


def _colsum(v):
    return jnp.sum(v, axis=0, keepdims=True)


def _mod_bwd(cc, dmod, w_mod_g):
    def body(cc_ref, dm_ref, w_ref, dcc_ref):
        v = cc_ref[...]
        sg = _sig(v)
        ds = jnp.zeros((16, D), F32)
        for k in range(NSH):
            ds = ds + _dot_nt(dm_ref[:, k * 768:(k + 1) * 768], w_ref[k])
        dcc_ref[...] = ds * (sg * (1.0 + v * (1.0 - sg)))
    return pl.pallas_call(body, out_shape=jax.ShapeDtypeStruct((16, D), F32),
                          name="mod_bwd", compiler_params=_cp())(cc, dmod, w_mod_g)


def _wmod_grad(cc, dmod_cols):
    def body(cc_ref, dm_ref, dw_ref):
        v = cc_ref[...]
        dw_ref[...] = _dot_tn(v * _sig(v), dm_ref[...])
    return pl.pallas_call(body, out_shape=jax.ShapeDtypeStruct((D, dmod_cols.shape[1]), F32),
                          name="wmod_grad", compiler_params=_cp())(cc, dmod_cols)


def _modulate(x_ref, mod_ref, is_lat):
    sh = jnp.where(is_lat, mod_ref[0:1, 0:D], mod_ref[1:2, 0:D])
    sc = jnp.where(is_lat, mod_ref[0:1, D:2 * D], mod_ref[1:2, D:2 * D])
    return x_ref[...] * (1.0 + sc) + sh, sc


def _gla_mask(d, n):
    row = lax.broadcasted_iota(jnp.int32, (n, n), 0)
    col = lax.broadcasted_iota(jnp.int32, (n, n), 1)
    same = (row // CH) == (col // CH)
    return same & ((row <= col) if d else (row >= col))


def _chunk_cumsum(v, rev):
    n = v.shape[0]
    pos = lax.broadcasted_iota(jnp.int32, v.shape, 0) & (CH - 1)
    s = 1
    while s < CH:
        if rev:
            v = v + jnp.where(pos < CH - s, pltpu.roll(v, n - s, 0), 0.0)
        else:
            v = v + jnp.where(pos >= s, pltpu.roll(v, s, 0), 0.0)
        s *= 2
    return v


def _chunk_rows(c):
    return slice(c * CH, (c + 1) * CH)


def _gla_features(zq, zf, lb, d):
    sq = _sig(zq)
    q = zq * sq * Q_SCALE
    sf = _sig(zf)
    f = lb + (1.0 - lb) * sf
    k = 1.0 - f
    g = _chunk_cumsum(jnp.log(f), d)
    last = 0 if d else CH - 1
    gls = [g[c * CH + last:c * CH + last + 1, :] for c in range(NCK)]
    glb = jnp.concatenate([jnp.broadcast_to(gl, (CH, D)) for gl in gls], axis=0)
    eg, eig, eeg = jnp.exp(g), jnp.exp(-g), jnp.exp(glb - g)
    decs = [jnp.exp(gl) for gl in gls]
    return sq, sf, f, q * eg, k * eig, k * eeg, eg, eig, eeg, decs


def _lower_bound(lbl_ref, d):
    return _sig(lbl_ref[0, d:d + 1, :] - lbl_ref[1, d:d + 1, :])


def _gla_rb(d, nlb, ncb):
    nrb = nlb + ncb
    if d:
        return lambda s: nrb - 1 - s
    return lambda s: jnp.where(s < ncb, nlb + s, s - ncb)


def _gla_fwd(z, lbl, d, nlb, ncb, gather=()):
    m = z.shape[0]
    nrb = nlb + ncb
    rb = _gla_rb(d, nlb, ncb)
    ng = len(gather)

    def body(*refs):
        q_ref, f_ref, v_ref, lbl_ref = refs[:4]
        ag_in = refs[4:4 + ng]
        o_ref, st_ref = refs[4 + ng:6 + ng]
        ag_out = refs[6 + ng:6 + 2 * ng]
        S = refs[6 + 2 * ng]
        ag_sems = refs[7 + 2 * ng:]
        s = pl.program_id(0)

        @pl.when(s == 0)
        def _():
            S[...] = jnp.zeros_like(S)
            if ng:
                local, chip, _ = _ag_copies(ag_in, ag_out, *ag_sems)
                for cp in chip + local:
                    cp.start()

        lb = _lower_bound(lbl_ref, d)
        mb = _gla_mask(d, CH)
        _, _, _, qd, ki, ke, _, _, _, decs = _gla_features(q_ref[...], f_ref[...], lb, d)
        qd, ki, ke, v = qd.astype(BF), ki.astype(BF), ke.astype(BF), v_ref[...].astype(BF)
        order = range(NCK - 1, -1, -1) if d else range(NCK)
        for h in range(H):
            hs = slice(h * DH, (h + 1) * DH)
            intra, upd = {}, {}
            for c in range(NCK):
                rs = _chunk_rows(c)
                a = jnp.where(mb, _dot_nt(qd[rs, hs], ki[rs, hs]), 0.0)
                intra[c] = _dot(a, v[rs, hs])
                upd[c] = _dot_tn(v[rs, hs], ke[rs, hs])
            st = S[h]
            for c in order:
                rs = _chunk_rows(c)
                st_ref[c, h] = st
                o_ref[rs, hs] = intra[c] + _dot_nt(qd[rs, hs], st)
                st = st * decs[c][:, hs] + upd[c]
            S[h] = st

        if ng:
            @pl.when(s == nrb - 1)
            def _():
                _ag_finish(*_ag_copies(ag_in, ag_out, *ag_sems))

    def zspec(cb):
        return pl.BlockSpec((RB, D), lambda s: (rb(s), cb))

    outs = pl.pallas_call(
        body, grid=(nrb,),
        in_specs=[zspec(0), zspec(1 + d), zspec(3), pl.BlockSpec((2, 2, D), lambda s: (0, 0, 0))] + [_ANY] * ng,
        out_specs=[pl.BlockSpec((RB, D), lambda s: (rb(s), 0)),
                   pl.BlockSpec((NCK, H, DH, DH), lambda s: (rb(s), 0, 0, 0))] + [_ANY] * ng,
        out_shape=[jax.ShapeDtypeStruct((m, D), F32), jax.ShapeDtypeStruct((m // CH, H, DH, DH), F32)]
        + [jax.ShapeDtypeStruct((NSH,) + g.shape, g.dtype) for g in gather],
        scratch_shapes=[pltpu.VMEM((H, DH, DH), F32)] + (_ag_sems(ng) if ng else []),
        name=f"gla_fwd{d}", compiler_params=_cp(("arbitrary",)))(z, z, z, lbl, *gather)
    return outs[0], outs[1], list(outs[2:])


def _gla_bwd(z, lbl, do_lat, states, d, nlb, ncb, prev=None):
    m = z.shape[0]
    nrb = nlb + ncb
    fwd_rb = _gla_rb(d, nlb, ncb)
    rb = lambda s: fwd_rb(nrb - 1 - s)
    last = prev is not None

    def body(*refs):
        if last:
            (q_ref, f_ref, v_ref, lbl_ref, do_ref, st_ref, pq_ref, pv_ref, dz1_ref, dz4a_ref, sh0_ref, sh1_ref, sum_ref,
             dS) = refs
        else:
            q_ref, f_ref, v_ref, lbl_ref, do_ref, st_ref, o0_ref, o1_ref, o2_ref, sum_ref, dS = refs
        s = pl.program_id(0)
        is_lat = rb(s) < nlb

        @pl.when(s == 0)
        def _():
            dS[...] = jnp.zeros_like(dS)
            sum_ref[...] = jnp.zeros_like(sum_ref)

        lb = _lower_bound(lbl_ref, d)
        mb = _gla_mask(d, RB)
        zq = q_ref[...]
        sq, sf, f, qd, ki, ke, eg, eig, eeg, decs = _gla_features(zq, f_ref[...], lb, d)
        qdb, kib, keb, vb = qd.astype(BF), ki.astype(BF), ke.astype(BF), v_ref[...].astype(BF)
        dob = jnp.where(is_lat, do_ref[...], 0.0).astype(BF)
        order = range(NCK) if d else range(NCK - 1, -1, -1)
        dqd_h, dki_h, dke_h, dv_h, ddec_h = [], [], [], [], []
        for h in range(H):
            hs = slice(h * DH, (h + 1) * DH)
            a = jnp.where(mb, _dot_nt(qdb[:, hs], kib[:, hs]), 0.0).astype(BF)
            da = jnp.where(mb, _dot_nt(dob[:, hs], vb[:, hs]), 0.0).astype(BF)
            dqd_i = _dot(da, kib[:, hs])
            dki_h.append(_dot_tn(da, qdb[:, hs]))
            dvi = _dot_tn(a, dob[:, hs])
            dqd, inc = {}, {}
            for c in range(NCK):
                rs = _chunk_rows(c)
                dqd[c] = dqd_i[rs, :] + _dot(dob[rs, hs], st_ref[c, h])
                inc[c] = _dot_tn(dob[rs, hs], qdb[rs, hs])
            dst = dS[h]
            dke, dv, ddec = {}, {}, {}
            for c in order:
                rs = _chunk_rows(c)
                dv[c] = dvi[rs, :] + _dot_nt(keb[rs, hs], dst)
                dke[c] = _dot(vb[rs, hs], dst)
                ddec[c] = _colsum(st_ref[c, h] * dst)
                dst = inc[c] + dst * decs[c][:, hs]
            dS[h] = dst
            cat = lambda t: jnp.concatenate([t[c] for c in range(NCK)], axis=0)
            dqd_h.append(cat(dqd))
            dke_h.append(cat(dke))
            dv_h.append(cat(dv))
            ddec_h.append([ddec[c] for c in range(NCK)])
        lanes = lambda parts: jnp.concatenate(parts, axis=1)
        dqd, dki, dke, dv = lanes(dqd_h), lanes(dki_h), lanes(dke_h), lanes(dv_h)
        dq = dqd * eg
        dk = dki * eig + dke * eeg
        dke_ke = dke * ke
        dg = dqd * qd - dki * ki - dke_ke
        dgl = [_colsum(dke_ke[_chunk_rows(c), :]) + lanes([ddec_h[h][c] for h in range(H)]) * decs[c]
               for c in range(NCK)]
        dglb = jnp.concatenate([jnp.broadcast_to(t, (CH, D)) for t in dgl], axis=0)
        df = (_chunk_cumsum(dg, 1 - d) + dglb) / f - dk
        dzf = df * (1.0 - lb) * (sf * (1.0 - sf))
        sum_ref[0:1, :] += _colsum(dzf)
        sum_ref[1:2, :] += _colsum(df * (1.0 - sf))
        if last:
            dz0 = (dq + pq_ref[...]) * (Q_SCALE * (sq * (1.0 + zq * (1.0 - sq))))
            dz3 = dv + pv_ref[...]
            sum_ref[2:3, :] += _colsum(dz0)
            sum_ref[3:4, :] += _colsum(dz3)
            dz2 = dzf.astype(BF)
            sh0_ref[:, 0:D] = dz0.astype(BF)
            sh0_ref[:, D:2 * D] = dz1_ref[...]
            sh0_ref[:, 2 * D:] = dz2[:, :CUT2]
            sh1_ref[:, 0:D - CUT2] = dz2[:, CUT2:]
            sh1_ref[:, D - CUT2:2 * D - CUT2] = dz3.astype(BF)
            sh1_ref[:, 2 * D - CUT2:] = dz4a_ref[...]
        else:
            o0_ref[...] = dq
            o1_ref[...] = dv
            o2_ref[...] = dzf.astype(BF)

        @pl.when(s == nrb - 1)
        def _():
            sum_ref[1:2, :] = sum_ref[1:2, :] * (lb * (1.0 - lb))

    def zspec(cb):
        return pl.BlockSpec((RB, D), lambda s: (rb(s), cb))

    rowspec = pl.BlockSpec((RB, D), lambda s: (rb(s), 0))
    in_specs = [zspec(0), zspec(1 + d), zspec(3), pl.BlockSpec((2, 2, D), lambda s: (0, 0, 0)),
                pl.BlockSpec((RB, D), lambda s: (jnp.minimum(rb(s), nlb - 1), 0)),
                pl.BlockSpec((NCK, H, DH, DH), lambda s: (rb(s), 0, 0, 0))]
    args = [z, z, z, lbl, do_lat, states]
    sumspec = pl.BlockSpec((8, D), lambda s: (0, 0))
    if last:
        in_specs += [rowspec, rowspec, rowspec, pl.BlockSpec((RB, CUT4), lambda s: (rb(s), 0))]
        args += list(prev)
        shspec = pl.BlockSpec((RB, SHC), lambda s: (rb(s), 0))
        out_specs = (shspec, shspec, sumspec)
        out_shape = (jax.ShapeDtypeStruct((m, SHC), BF), jax.ShapeDtypeStruct((m, SHC), BF))
    else:
        out_specs = (rowspec, rowspec, rowspec, sumspec)
        out_shape = (jax.ShapeDtypeStruct((m, D), F32), jax.ShapeDtypeStruct((m, D), F32),
                     jax.ShapeDtypeStruct((m, D), BF))
    return pl.pallas_call(
        body, grid=(nrb,), in_specs=in_specs, out_specs=out_specs,
        out_shape=out_shape + (jax.ShapeDtypeStruct((8, D), F32),),
        scratch_shapes=[pltpu.VMEM((H, DH, DH), F32)],
        name=f"gla_bwd{d}", compiler_params=_cp(("arbitrary",)))(*args)


def _shift_rows(v, k, r):
    row = lax.broadcasted_iota(jnp.int32, v.shape, 0)
    rolled = pltpu.roll(v, k % r, 0)
    return jnp.where((row >= k) & (row < r + k), rolled, 0.0)


def _conv_fwd(xb, cw_ref, cb_ref, r):
    return (cb_ref[...] + _shift_rows(xb, 1, r) * cw_ref[0:1, :] + xb * cw_ref[1:2, :]
            + _shift_rows(xb, -1, r) * cw_ref[2:3, :] + _shift_rows(xb, -2, r) * cw_ref[3:4, :])


def _window(ref, lo, n, rows):
    parts = []
    if lo < 0:
        parts.append(jnp.zeros((-lo,) + tuple(ref.shape[1:]), F32))
    parts.append(ref[max(lo, 0):min(lo + n, rows)])
    if lo + n > rows:
        parts.append(jnp.zeros((lo + n - rows,) + tuple(ref.shape[1:]), F32))
    return parts[0] if len(parts) == 1 else jnp.concatenate(parts, axis=0)


def _conv_cols(x_ref, cw_ref, cb_ref, r0, n, rows):
    w = _window(x_ref, r0 - 1, n + 3, rows)
    return (cb_ref[...] + w[0:n] * cw_ref[0:1, :] + w[1:n + 1] * cw_ref[1:2, :] + w[2:n + 2] * cw_ref[2:3, :]
            + w[3:n + 3] * cw_ref[3:4, :])


def _softplus_neg(lam):
    y = jnp.exp(-jnp.abs(lam))
    u = 1.0 + y
    tiny = u == 1.0
    l1p = jnp.where(tiny, y, jnp.log(u) * (y / jnp.where(tiny, 1.0, u - 1.0)))
    return jnp.maximum(-lam, 0.0) + l1p


def _gates(xc, wr_ref, br, wi_ref, bi, sp):
    xcb = xc.astype(BF)
    rs, is_ = [], []
    for g in range(H):
        gs = slice(g * DH, (g + 1) * DH)
        rs.append(jnp.dot(xcb[:, gs], wr_ref[g].astype(BF), preferred_element_type=F32))
        is_.append(jnp.dot(xcb[:, gs], wi_ref[g].astype(BF), preferred_element_type=F32))
    r = _sig(jnp.concatenate(rs, axis=1) + br)
    i = _sig(jnp.concatenate(is_, axis=1) + bi)
    log_a = (-RG_C * r) * sp
    a = jnp.exp(log_a)
    t = jnp.tanh(log_a)
    om = (-2.0 * t) / (1.0 - t)
    return r, i, a, om


def _scan_rows(d, nrows, a_s, b_s, h_s, h0):
    nsl = nrows // 8

    def slab(j, h):
        jj = (nsl - 1 - j) if d else j
        r0 = pl.multiple_of(jj * 8, 8)
        for t in (range(7, -1, -1) if d else range(8)):
            h = a_s[pl.ds(r0 + t, 1), :] * h + b_s[pl.ds(r0 + t, 1), :]
            h_s[pl.ds(r0 + t, 1), :] = h
        return h

    return lax.fori_loop(0, nsl, slab, h0)


def _col_of(d, ncols):
    if d:
        return lambda s: ncols - jnp.maximum(s, 1)
    return lambda s: jnp.maximum(s, 1) - 1


def _rglru_fwd(z, cw, cb, wr, br, wi, bi, lam, d, t_lat, t_ctx):
    m = z.shape[0]
    rows = t_lat // GRID_W
    z3 = z.reshape(m // GRID_W, GRID_W, IN_COLS)
    nblk = GRID_W // CB
    blk = _col_of(d, nblk)
    cblk = t_lat // t_ctx
    rc = min(RCH, rows)

    def body(zc_ref, zx_ref, cw_ref, cb_ref, wr_ref, br_ref, wi_ref, bi_ref, lam_ref,
             hx_ref, hpx_ref, ax_ref, hpc_ref, a_s, b_s, h_s, hcar, a3, b3, cin_s):
        s = pl.program_id(0)
        sp = _softplus_neg(lam_ref[d:d + 1, :])
        br_ = br_ref[d:d + 1, :]
        bi_ = bi_ref[d:d + 1, :]

        @pl.when(s == 0)
        def _():
            xc = _conv_fwd(zc_ref[...], cw_ref, cb_ref, t_ctx)
            _, i, a, om = _gates(xc, wr_ref, br_, wi_ref, bi_, sp)
            a_s[...] = a
            b_s[...] = jnp.sqrt(om) * (i * xc)
            h0 = jnp.zeros((1, D), F32)
            hcar[0:1, :] = _scan_rows(d, t_ctx, a_s, b_s, h_s, h0)
            hs = h_s[...]
            row = lax.broadcasted_iota(jnp.int32, (t_ctx, D), 0)
            if d:
                hpc_ref[...] = jnp.where(row == t_ctx - 1, h0, pltpu.roll(hs, t_ctx - 1, 0))
            else:
                hpc_ref[...] = jnp.where(row == 0, h0, pltpu.roll(hs, 1, 0))

        @pl.when(s > 0)
        def _():
            for r0 in range(0, rows, rc):
                xc = _conv_cols(zx_ref, cw_ref, cb_ref, r0, rc, rows).reshape(rc * CB, D)
                _, i, a, om = _gates(xc, wr_ref, br_, wi_ref, bi_, sp)
                a3[r0:r0 + rc] = a.reshape(rc, CB, D)
                ax_ref[r0:r0 + rc] = a.reshape(rc, CB, D)
                b3[r0:r0 + rc] = (jnp.sqrt(om) * (i * xc)).reshape(rc, CB, D)

            def local(t, carry):
                hl, p = carry
                r = (rows - 1 - t) if d else t
                a = a3[r]
                hl = a * hl + b3[r]
                p = a * p
                b3[r] = hl
                a3[r] = p
                return hl, p

            hl, p = lax.fori_loop(0, rows, local, (jnp.zeros((CB, D), F32), jnp.ones((CB, D), F32)))
            cin = hcar[0:1, :]
            for j in (range(CB - 1, -1, -1) if d else range(CB)):
                cin_s[j:j + 1, :] = cin
                cin = hl[j:j + 1, :] + p[j:j + 1, :] * cin
            hcar[0:1, :] = cin
            c_in = cin_s[...]

            def fix(t, prev):
                r = (rows - 1 - t) if d else t
                h = b3[r] + a3[r] * c_in
                hx_ref[r] = h
                hpx_ref[r] = prev
                return h

            lax.fori_loop(0, rows, fix, c_in)

    full = lambda shp: pl.BlockSpec(shp, lambda s: (0,) * len(shp))
    colspec = pl.BlockSpec((rows, CB, D), lambda s: (0, blk(s), 0))
    outs = pl.pallas_call(
        body, grid=(nblk + 1,),
        in_specs=[pl.BlockSpec((t_ctx, D), lambda s: (cblk, 5)),
                  pl.BlockSpec((rows, CB, D), lambda s: (0, blk(s), 5)),
                  full((4, D)), full((1, D)),
                  pl.BlockSpec((None, H, DH, DH), lambda s: (d, 0, 0, 0)), full((2, D)),
                  pl.BlockSpec((None, H, DH, DH), lambda s: (d, 0, 0, 0)), full((2, D)), full((2, D))],
        out_specs=(colspec, colspec, colspec, full((t_ctx, D))),
        out_shape=(jax.ShapeDtypeStruct((rows, GRID_W, D), F32),) * 3 + (jax.ShapeDtypeStruct((t_ctx, D), F32),),
        scratch_shapes=[pltpu.VMEM((t_ctx, D), F32), pltpu.VMEM((t_ctx, D), F32), pltpu.VMEM((t_ctx, D), F32),
                        pltpu.VMEM((8, D), F32), pltpu.VMEM((rows, CB, D), F32), pltpu.VMEM((rows, CB, D), F32),
                        pltpu.VMEM((CB, D), F32)],
        name=f"rglru_fwd{d}", compiler_params=_cp(("arbitrary",)))(z, z3, cw, cb, wr, br, wi, bi, lam)
    return outs[0].reshape(t_lat, D), outs[1].reshape(t_lat, D), outs[2].reshape(t_lat, D), outs[3]


def _rglru_bwd(z, cw, cb, wr, br, wi, bi, lam, dh_lat, hp_lat, a_lat, hp_ctx, d, t_lat, t_ctx, prev=None):
    m = z.shape[0]
    rows = t_lat // GRID_W
    z3 = z.reshape(m // GRID_W, GRID_W, IN_COLS)
    nblk = GRID_W // CB
    fblk = _col_of(d, nblk)
    blk = lambda s: fblk(nblk - jnp.minimum(s, nblk - 1))
    cblk = t_lat // t_ctx
    rc = min(RCH, rows)
    last = prev is not None
    view3 = lambda v: v.reshape(rows, GRID_W, D)

    def body(*refs):
        (zc_ref, zx_ref, cw_ref, cb_ref, wr_ref, br_ref, wi_ref, bi_ref, lam_ref, dh_ref, hpx_ref, ax_ref,
         hpc_ref) = refs[:13]
        k = 13
        if last:
            pdx_ref, pdc_ref = refs[13:15]
            k = 15
        ox_ref, oc_ref, dwr_ref, dwi_ref, sum_ref, a_s, b_s, gcar, a3, b3, cin_s = refs[k:]
        s = pl.program_id(0)
        lam_d = lam_ref[d:d + 1, :]
        sp = _softplus_neg(lam_d)
        br_ = br_ref[d:d + 1, :]
        bi_ = bi_ref[d:d + 1, :]
        flat = lambda v: v.reshape(-1, D)

        @pl.when(s == 0)
        def _():
            gcar[...] = jnp.zeros_like(gcar)
            dwr_ref[...] = jnp.zeros_like(dwr_ref)
            dwi_ref[...] = jnp.zeros_like(dwi_ref)
            sum_ref[...] = jnp.zeros_like(sum_ref)


        def conv_sums(dxc, dxb, xm1, x0, xp1, xp2):
            sum_ref[3:4, :] += _colsum(flat(dxc))
            sum_ref[4:5, :] += _colsum(flat(dxb))
            sum_ref[8:9, :] += _colsum(flat(dxc * xm1))
            sum_ref[9:10, :] += _colsum(flat(dxc * x0))
            sum_ref[10:11, :] += _colsum(flat(dxc * xp1))
            sum_ref[11:12, :] += _colsum(flat(dxc * xp2))

        def gate_grads(g, hp, xc, rr, i, a, om):
            mult = jnp.sqrt(om)
            da = g * hp
            ixc = i * xc
            dmult = g * ixc
            dixc = g * mult
            di = dixc * xc
            dxc = dixc * i
            dlog_a = da * a - dmult * ((1.0 - om) / mult)
            dr = dlog_a * (-RG_C * sp)
            sum_ref[2:3, :] += _colsum(dlog_a * rr)
            drp = dr * rr * (1.0 - rr)
            dip = di * i * (1.0 - i)
            sum_ref[0:1, :] += _colsum(drp)
            sum_ref[1:2, :] += _colsum(dip)
            xcb = xc.astype(BF)
            drb = drp.astype(BF)
            dib = dip.astype(BF)
            parts = []
            for gi in range(H):
                gs = slice(gi * DH, (gi + 1) * DH)
                parts.append(_dot_nt(drb[:, gs], wr_ref[gi]) + _dot_nt(dib[:, gs], wi_ref[gi]))
                dwr_ref[gi] += _dot_tn(xcb[:, gs], drb[:, gs])
                dwi_ref[gi] += _dot_tn(xcb[:, gs], dib[:, gs])
            return dxc + jnp.concatenate(parts, axis=1)

        @pl.when(s < nblk)
        def _():
            def local(t, carry):
                c, q = carry
                r = t if d else (rows - 1 - t)
                a = ax_ref[r]
                c = a * (c + dh_ref[r])
                q = a * q
                b3[r] = c
                a3[r] = q
                return c, q

            c, q = lax.fori_loop(0, rows, local, (jnp.zeros((CB, D), F32), jnp.ones((CB, D), F32)))
            cin = gcar[0:1, :]
            for j in (range(CB) if d else range(CB - 1, -1, -1)):
                cin_s[j:j + 1, :] = cin
                cin = c[j:j + 1, :] + q[j:j + 1, :] * cin
            gcar[0:1, :] = cin
            c_in = cin_s[...]
            for r0 in (range(rows - rc, -1, -rc) if d else range(0, rows, rc)):
                if d:
                    lo = max(r0 - 1, 0)
                    cn = b3[lo:r0 + rc - 1] + a3[lo:r0 + rc - 1] * c_in
                    if r0 == 0:
                        cn = jnp.concatenate([c_in[None], cn], axis=0)
                else:
                    hi = min(r0 + rc + 1, rows)
                    cn = b3[r0 + 1:hi] + a3[r0 + 1:hi] * c_in
                    if hi == rows:
                        cn = jnp.concatenate([cn, c_in[None]], axis=0)
                g = flat(dh_ref[r0:r0 + rc] + cn)
                xc = flat(_conv_cols(zx_ref, cw_ref, cb_ref, r0, rc, rows))
                rr, i, a, om = _gates(xc, wr_ref, br_, wi_ref, bi_, sp)
                b3[r0:r0 + rc] = gate_grads(g, flat(hpx_ref[r0:r0 + rc]), xc, rr, i, a, om).reshape(rc, CB, D)
            if last:
                for r0 in range(0, rows, rc):
                    b3[r0:r0 + rc] = b3[r0:r0 + rc] + pdx_ref[r0:r0 + rc]
                for r0 in range(0, rows, rc):
                    w = _window(b3, r0 - 2, rc + 3, rows)
                    xw = _window(zx_ref, r0 - 1, rc + 3, rows)
                    dxc = w[2:rc + 2]
                    dxb = (w[3:rc + 3] * cw_ref[0:1, :] + dxc * cw_ref[1:2, :] + w[1:rc + 1] * cw_ref[2:3, :]
                           + w[0:rc] * cw_ref[3:4, :])
                    conv_sums(dxc, dxb, xw[0:rc], xw[1:rc + 1], xw[2:rc + 2], xw[3:rc + 3])
                    ox_ref[r0:r0 + rc] = dxb
            else:
                for r0 in range(0, rows, rc):
                    ox_ref[r0:r0 + rc] = b3[r0:r0 + rc]

        @pl.when(s == nblk)
        def _():
            r = t_ctx
            xb = zc_ref[...]
            xc = _conv_fwd(xb, cw_ref, cb_ref, r)
            rr, i, a, om = _gates(xc, wr_ref, br_, wi_ref, bi_, sp)
            a_s[...] = a
            b_s[...] = jnp.zeros((r, D), F32)
            c0 = gcar[0:1, :]
            _scan_rows(1 - d, r, a_s, b_s, b_s, c0)
            cs = b_s[...]
            row = lax.broadcasted_iota(jnp.int32, (r, D), 0)
            if d:
                g = jnp.where(row == 0, c0, pltpu.roll(cs, 1, 0))
            else:
                g = jnp.where(row == r - 1, c0, pltpu.roll(cs, r - 1, 0))
            dxc = gate_grads(g, hpc_ref[...], xc, rr, i, a, om)
            if last:
                dxc = dxc + pdc_ref[...]
                dxb = (_shift_rows(dxc, -1, r) * cw_ref[0:1, :] + dxc * cw_ref[1:2, :]
                       + _shift_rows(dxc, 1, r) * cw_ref[2:3, :] + _shift_rows(dxc, 2, r) * cw_ref[3:4, :])
                conv_sums(dxc, dxb, _shift_rows(xb, 1, r), xb, _shift_rows(xb, -1, r), _shift_rows(xb, -2, r))
                oc_ref[...] = dxb
            else:
                oc_ref[...] = dxc
            sum_ref[2:3, :] = sum_ref[2:3, :] * (RG_C * _sig(-lam_d))

    full = lambda shp: pl.BlockSpec(shp, lambda s: (0,) * len(shp))
    once = lambda shp: pl.BlockSpec(shp, lambda s: (0,) * len(shp), pipeline_mode=pl.Buffered(1))
    colspec = pl.BlockSpec((rows, CB, D), lambda s: (0, blk(s), 0))
    colonce = pl.BlockSpec((rows, CB, D), lambda s: (0, blk(s), 0), pipeline_mode=pl.Buffered(1))
    in_specs = [pl.BlockSpec((t_ctx, D), lambda s: (cblk, 5), pipeline_mode=pl.Buffered(1)),
                pl.BlockSpec((rows, CB, D), lambda s: (0, blk(s), 5), pipeline_mode=pl.Buffered(1)),
                full((4, D)), full((1, D)),
                pl.BlockSpec((None, H, DH, DH), lambda s: (d, 0, 0, 0)), full((2, D)),
                pl.BlockSpec((None, H, DH, DH), lambda s: (d, 0, 0, 0)), full((2, D)), full((2, D)),
                colspec, colspec, colonce, once((t_ctx, D))]
    args = [z, z3, cw, cb, wr, br, wi, bi, lam, view3(dh_lat), view3(hp_lat), view3(a_lat), hp_ctx]
    if last:
        in_specs += [colonce, once((t_ctx, D))]
        args += [view3(prev[0]), prev[1]]
    outs = pl.pallas_call(
        body, grid=(nblk + 1,), in_specs=in_specs,
        out_specs=(colspec, full((t_ctx, D)), full((H, DH, DH)), full((H, DH, DH)), full((16, D))),
        out_shape=(jax.ShapeDtypeStruct((rows, GRID_W, D), F32), jax.ShapeDtypeStruct((t_ctx, D), F32),
                   jax.ShapeDtypeStruct((H, DH, DH), F32), jax.ShapeDtypeStruct((H, DH, DH), F32),
                   jax.ShapeDtypeStruct((16, D), F32)),
        scratch_shapes=[pltpu.VMEM((t_ctx, D), F32), pltpu.VMEM((t_ctx, D), F32), pltpu.VMEM((8, D), F32),
                        pltpu.VMEM((rows, CB, D), F32), pltpu.VMEM((rows, CB, D), F32), pltpu.VMEM((CB, D), F32)],
        name=f"rglru_bwd{d}", compiler_params=_cp(("arbitrary",), VMEM_LIMIT_RGLRU_BWD))(*args)
    return (outs[0].reshape(t_lat, D), outs[1]) + tuple(outs[2:])


def _merge(o_f, o_b, h_f, h_b, z, x_all, tgt, mod, norm_g, ln_g, ln_b, p_a, p_b, w_out, t_lat):
    tm = 256
    nt = t_lat // tm

    def body(of_ref, ob_ref, hf_ref, hb_ref, z4_ref, z6_ref, z7_ref, z8_ref, x_ref, t_ref, mod_ref, ng_ref,
             lg_ref, lb_ref, pa_ref, pb_ref, wo_ref,
             do_ref, dh_ref, dz4a_ref, dz4b_ref, dz6a_ref, sh3_ref, gx_ref,
             y_ref, dout_ref, oa_ref, dpa_ref, obv_ref, dpb_ref, acc_ref):
        i = pl.program_id(0)
        lat = i < nt
        latf = lat.astype(F32)

        @pl.when(i == 0)
        def _():
            acc_ref[...] = jnp.zeros_like(acc_ref)

        def per_head(v):
            return jnp.concatenate(
                [jnp.broadcast_to(jnp.mean(v[:, h * DH:(h + 1) * DH], axis=-1, keepdims=True), (tm, DH))
                 for h in range(H)], axis=1)

        gt = mod_ref[0:1, 2 * D:3 * D]
        gfull = jnp.concatenate([ng_ref[...]] * H, axis=1)
        o = of_ref[...] + ob_ref[...]
        rinv = lax.rsqrt(per_head(o * o) + RMS_EPS)
        n = o * rinv
        na = n * gfull
        z4 = z4_ref[...]
        s4 = _sig(z4)
        silu4 = z4 * s4
        oa = na * silu4
        z6 = z6_ref[...]
        s6 = _sig(z6)
        silu6 = z6 * s6
        hsum = hf_ref[...] + hb_ref[...]
        obv = hsum * silu6
        pa = _dot(oa, pa_ref[...])
        pb = _dot(obv, pb_ref[...])
        s7 = _sig(z7_ref[...])
        s8 = _sig(z8_ref[...])
        y = s7 * pa + s8 * pb
        out = _dot(y, wo_ref[...])
        pre = ALPHA * x_ref[...] + gt * out
        mu = jnp.mean(pre, axis=-1, keepdims=True)
        xc = pre - mu
        rstd = lax.rsqrt(jnp.mean(xc * xc, axis=-1, keepdims=True) + LN_EPS)
        xhat = xc * rstd
        lg = lg_ref[...]
        diff = xhat * lg + lb_ref[...] - t_ref[...]
        acc_ref[8:9, :] += _colsum(diff * diff) * (0.5 / D * latf)
        dxn = diff * (1.0 / D)
        acc_ref[1:2, :] += _colsum(dxn * xhat) * latf
        acc_ref[2:3, :] += _colsum(dxn) * latf
        dxhat = dxn * lg
        dpre = rstd * (dxhat - jnp.mean(dxhat, axis=-1, keepdims=True)
                       - xhat * jnp.mean(dxhat * xhat, axis=-1, keepdims=True))
        gx_ref[...] = ALPHA * dpre
        acc_ref[0:1, :] += _colsum(dpre * out) * latf
        dout = dpre * gt
        dy = _dot_nt(dout, wo_ref[...])
        dpa = dy * s7
        dpb = dy * s8
        dz7 = dy * pa * (s7 * (1.0 - s7))
        dz8 = dy * pb * (s8 * (1.0 - s8))
        doa = _dot_nt(dpa, pa_ref[...])
        dob = _dot_nt(dpb, pb_ref[...])
        dh_ref[...] = dob * silu6
        dz6 = dob * hsum * (s6 * (1.0 + z6 * (1.0 - s6)))
        dna = doa * silu4
        dz4 = doa * na * (s4 * (1.0 + z4 * (1.0 - s4)))
        dng = _colsum(dna * n)
        acc_ref[7:8, 0:DH] += sum(dng[:, h * DH:(h + 1) * DH] for h in range(H)) * latf
        dn = dna * gfull
        do_ref[...] = rinv * (dn - n * per_head(dn * n))
        acc_ref[3:4, :] += _colsum(dz4) * latf
        acc_ref[4:5, :] += _colsum(dz6) * latf
        acc_ref[5:6, :] += _colsum(dz7) * latf
        acc_ref[6:7, :] += _colsum(dz8) * latf
        dz4b, dz6b = (dz4 * latf).astype(BF), (dz6 * latf).astype(BF)
        dz4a_ref[...] = dz4b[:, :CUT4]
        dz4b_ref[...] = dz4b[:, CUT4:]
        dz6a_ref[...] = dz6b[:, :CUT6]
        sh3_ref[:, 0:D - CUT6] = dz6b[:, CUT6:]
        sh3_ref[:, D - CUT6:2 * D - CUT6] = (dz7 * latf).astype(BF)
        sh3_ref[:, 2 * D - CUT6:] = (dz8 * latf).astype(BF)
        y_ref[...] = y.astype(BF)
        dout_ref[...] = dout.astype(BF)
        oa_ref[...] = oa.astype(BF)
        dpa_ref[...] = dpa.astype(BF)
        obv_ref[...] = obv.astype(BF)
        dpb_ref[...] = dpb.astype(BF)

        @pl.when(i == nt - 1)
        def _():
            acc_ref[9:10, :] = jnp.broadcast_to(jnp.sum(acc_ref[8:9, :], axis=-1, keepdims=True), (1, D))

    m = x_all.shape[0]
    lrow = lambda i: jnp.minimum(i, nt - 1)
    row = pl.BlockSpec((tm, D), lambda i: (lrow(i), 0))
    allrow = lambda cols: pl.BlockSpec((tm, cols), lambda i: (i, 0))
    zs = lambda cb: pl.BlockSpec((tm, D), lambda i: (lrow(i), cb))
    full = lambda shp: pl.BlockSpec(shp, lambda i: (0,) * len(shp))
    wfull = pl.BlockSpec((D, D), lambda i: (0, 0), pipeline_mode=pl.Buffered(1))
    f32o = jax.ShapeDtypeStruct((t_lat, D), F32)
    bfo = jax.ShapeDtypeStruct((t_lat, D), BF)
    bfall = lambda cols: jax.ShapeDtypeStruct((m, cols), BF)
    return pl.pallas_call(
        body, grid=(m // tm,),
        in_specs=[row, row, row, row, zs(4), zs(6), zs(7), zs(8), row, row, full((16, 3 * D)), full((1, DH)),
                  full((1, D)), full((1, D)), wfull, wfull, wfull],
        out_specs=(row, row) + tuple(allrow(c) for c in (CUT4, D - CUT4, CUT6, SHC)) + (row,) * 7 + (full((16, D)),),
        out_shape=(f32o, f32o, bfall(CUT4), bfall(D - CUT4), bfall(CUT6), bfall(SHC), f32o, bfo, bfo, bfo, bfo, bfo, bfo,
                   jax.ShapeDtypeStruct((16, D), F32)),
        name="merge", compiler_params=_cp(("arbitrary",), VMEM_LIMIT_MERGE))(
            o_f, o_b, h_f, h_b, z, z, z, z, x_all, tgt, mod, norm_g, ln_g, ln_b, p_a, p_b, w_out)


def _wgrad(a, b, name):
    tm = 1024

    def body(a_ref, b_ref, o_ref):
        @pl.when(pl.program_id(0) == 0)
        def _():
            o_ref[...] = jnp.zeros_like(o_ref)
        o_ref[...] += _dot_tn(a_ref[...], b_ref[...])

    row = pl.BlockSpec((tm, D), lambda i: (i, 0))
    return pl.pallas_call(body, grid=(a.shape[0] // tm,), in_specs=[row, row],
                          out_specs=pl.BlockSpec((D, D), lambda i: (0, 0)),
                          out_shape=jax.ShapeDtypeStruct((D, D), F32), name=name,
                          compiler_params=_cp(("arbitrary",)))(a, b)


def _pack_shard2(dz4b, dz5_lat, dz5_ctx, dz6a):
    m, t_lat, t_ctx = dz4b.shape[0], dz5_lat.shape[0], dz5_ctx.shape[0]
    tm = t_ctx
    nlt = t_lat // tm
    w4 = D - CUT4

    def body(a_ref, bl_ref, bc_ref, c_ref, o_ref):
        i = pl.program_id(0)
        o_ref[:, 0:w4] = a_ref[...]
        o_ref[:, w4:w4 + D] = jnp.where(i < nlt, bl_ref[...], bc_ref[...]).astype(BF)
        o_ref[:, w4 + D:] = c_ref[...]

    return pl.pallas_call(
        body, grid=(m // tm,),
        in_specs=[pl.BlockSpec((tm, w4), lambda i: (i, 0)),
                  pl.BlockSpec((tm, D), lambda i: (jnp.minimum(i, nlt - 1), 0)),
                  pl.BlockSpec((tm, D), lambda i: (0, 0)),
                  pl.BlockSpec((tm, CUT6), lambda i: (i, 0))],
        out_specs=pl.BlockSpec((tm, SHC), lambda i: (i, 0)),
        out_shape=jax.ShapeDtypeStruct((m, SHC), BF), name="pack_shard2",
        compiler_params=_cp(("arbitrary",)))(dz4b, dz5_lat, dz5_ctx, dz6a)


def _wgrad_in(u_all, dz_shards):
    m = u_all.shape[1]
    assert m % (6 * 128) == 0
    tm = m // 6
    out = None
    for k, dz_k in enumerate(dz_shards):
        def body(u_ref, dz_ref, *rest):
            o_ref = rest[-1]

            @pl.when(pl.program_id(0) == 0)
            def _():
                o_ref[...] = jnp.zeros_like(o_ref)
            o_ref[0] += _dot(u_ref[...], dz_ref[...])

        out = pl.pallas_call(
            body, grid=(m // tm,),
            in_specs=[pl.BlockSpec((D, tm), lambda i: (0, i)), pl.BlockSpec((tm, SHC), lambda i: (i, 0))]
            + ([] if out is None else [_ANY]),
            out_specs=pl.BlockSpec((1, D, SHC), lambda i, k=k: (k, 0, 0)),
            out_shape=jax.ShapeDtypeStruct((NSH, D, SHC), F32),
            input_output_aliases={} if out is None else {2: 0},
            name=f"wgrad_in{k}", compiler_params=_cp(("arbitrary",)))(u_all, dz_k, *(() if out is None else (out,)))
    return out


def _du(dz_shards, w_in_g, x_all, mod, gxres, n_lat_tiles, sums=()):
    m = x_all.shape[0]
    tm = 256
    nt = m // tm
    nct = nt - n_lat_tiles
    ns = len(sums)
    rblk = lambda i: jnp.where(i < nct, n_lat_tiles + i, i - nct)
    lblk = lambda i: jnp.maximum(i - nct, 0)

    def body(*refs):
        dz_refs, refs = refs[:NSH], refs[NSH - 1:]
        _, w_ref, x_ref, mod_ref, gr_ref = refs[:5]
        sum_refs = refs[5:5 + ns]
        gx_ref, dm_ref = refs[5 + ns:7 + ns]
        got_refs = refs[7 + ns:7 + 2 * ns]
        sems = refs[7 + 2 * ns:]
        i = pl.program_id(0)
        is_lat = i >= nct

        @pl.when(i == 0)
        def _():
            dm_ref[...] = jnp.zeros_like(dm_ref)
            if ns:
                for cp in _rs_chip_copies(sum_refs, got_refs, *sems):
                    cp.start()

        du = _dot_nt(dz_refs[0][...], w_ref[0])
        for n in range(1, NSH):
            du = du + _dot_nt(dz_refs[n][...], w_ref[n])
        sc = jnp.where(is_lat, mod_ref[0:1, D:2 * D], mod_ref[1:2, D:2 * D])
        dsh = _colsum(du)
        dsc = _colsum(du * x_ref[...])

        @pl.when(is_lat)
        def _():
            gx_ref[...] = du * (1.0 + sc) + gr_ref[...]
            dm_ref[0:1, 0:D] += dsh
            dm_ref[0:1, D:2 * D] += dsc

        @pl.when(jnp.logical_not(is_lat))
        def _():
            dm_ref[1:2, 0:D] += dsh
            dm_ref[1:2, D:2 * D] += dsc

        if ns:
            @pl.when(i == nt - 1)
            def _():
                for cp in _rs_chip_copies(sum_refs, got_refs, *sems):
                    cp.wait()

    outs = pl.pallas_call(
        body, grid=(nt,),
        in_specs=[pl.BlockSpec((tm, SHC), lambda i: (rblk(i), 0))] * NSH + [
                  pl.BlockSpec((NSH, D, SHC), lambda i: (0, 0, 0), pipeline_mode=pl.Buffered(1)),
                  pl.BlockSpec((tm, D), lambda i: (rblk(i), 0)),
                  pl.BlockSpec((16, 3 * D), lambda i: (0, 0)),
                  pl.BlockSpec((tm, D), lambda i: (lblk(i), 0))] + [_ANY] * ns,
        out_specs=[pl.BlockSpec((tm, D), lambda i: (lblk(i), 0)),
                   pl.BlockSpec((8, 2 * D), lambda i: (0, 0))] + [_ANY] * ns,
        out_shape=[jax.ShapeDtypeStruct((n_lat_tiles * tm, D), F32), jax.ShapeDtypeStruct((8, 2 * D), F32)]
        + [jax.ShapeDtypeStruct((3,) + g.shape[1:], g.dtype) for g in sums],
        scratch_shapes=[pltpu.SemaphoreType.DMA((3 * ns,)), pltpu.SemaphoreType.DMA((3 * ns,))] if ns else [],
        name="du", compiler_params=_cp(("arbitrary",)))(*dz_shards, w_in_g, x_all, mod, gxres, *sums)
    return outs[0], outs[1], list(outs[2:])


def _row_tile(rows, cols):
    t = 8
    while t * 2 * cols * 4 <= (1 << 20) and rows % (t * 2) == 0:
        t *= 2
    return t


def _adamw_update(w_ref, g_ref, m_ref, v_ref, d_ref, nm_ref, nv_ref):
    gg = g_ref[...]
    m2 = ADAM_B1 * m_ref[...] + (1.0 - ADAM_B1) * gg
    v2 = ADAM_B2 * v_ref[...] + (1.0 - ADAM_B2) * (gg * gg)
    m_hat = m2 / (1.0 - ADAM_B1 ** ADAM_STEP)
    v_hat = v2 / (1.0 - ADAM_B2 ** ADAM_STEP)
    d_ref[...] = -ADAM_LR * (m_hat / (jnp.sqrt(v_hat) + ADAM_EPS) + ADAM_WD * w_ref[...])
    nm_ref[...] = m2
    nv_ref[...] = v2


def _adamw_many(ws, gs, ms, vs):
    n = len(ws)

    def body(*refs):
        for j in range(n):
            _adamw_update(*refs[4 * j:4 * j + 4], *refs[4 * n + 3 * j:4 * n + 3 * j + 3])

    args = [a for quad in zip(ws, gs, ms, vs) for a in quad]
    outs = pl.pallas_call(body, out_shape=[jax.ShapeDtypeStruct(w.shape, F32) for w in ws for _ in range(3)],
                          name="adamw_small", compiler_params=_cp())(*args)
    return outs[0::3], outs[1::3], outs[2::3]


def _adamw(w, g, m, v, name):
    rows, cols = w.shape
    tr = _row_tile(rows, cols)

    def body(*refs):
        _adamw_update(*refs)

    spec = pl.BlockSpec((tr, cols), lambda i: (i, 0))
    o = jax.ShapeDtypeStruct((rows, cols), F32)
    return pl.pallas_call(body, grid=(rows // tr,), in_specs=[spec] * 4, out_specs=(spec,) * 3,
                          out_shape=(o, o, o), name=name, compiler_params=_cp(("arbitrary",)))(w, g, m, v)


_ANY = pl.BlockSpec(memory_space=pl.ANY)


def _place():
    return lax.axis_index("x"), lax.axis_index("y"), lax.axis_index("c")


def _ag_copies(ins, outs, send, recv, fsend, frecv, lsem):
    x, y, c = _place()
    me = 2 * x + y
    chips = ((1 - x, y), (x, 1 - y), (1 - x, 1 - y))
    local, chip, hand = [], [], []
    for j in range(len(ins)):
        hr = ins[j].shape[0] // 2
        half = pl.ds(pl.multiple_of(c * hr, 8), hr)
        local.append(pltpu.make_async_copy(ins[j], outs[j].at[me], lsem.at[j]))
        for k, (px, py) in enumerate(chips):
            chip.append(pltpu.make_async_remote_copy(
                src_ref=ins[j].at[half, :], dst_ref=outs[j].at[me, half, :], send_sem=send.at[3 * j + k],
                recv_sem=recv.at[3 * j + k], device_id=(px, py, c), device_id_type=MESH))
            got = outs[j].at[2 * px + py, half, :]
            hand.append(pltpu.make_async_remote_copy(
                src_ref=got, dst_ref=got, send_sem=fsend.at[3 * j + k], recv_sem=frecv.at[3 * j + k],
                device_id=(x, y, 1 - c), device_id_type=MESH))
    return local, chip, hand


def _ag_sems(n):
    return [pltpu.SemaphoreType.DMA((3 * n,))] * 4 + [pltpu.SemaphoreType.DMA((n,))]


def _ag_finish(local, chip, hand, done=0):
    for k in range(done, len(chip)):
        chip[k].wait_recv()
        hand[k].start()
    for cp in chip:
        cp.wait_send()
    for k in range(done):
        hand[k].wait_send()
    for k in range(done, len(chip)):
        hand[k].wait_send()
        hand[k].wait_recv()
    for cp in local:
        cp.wait()


def _mod_tp(c8, c_ctx, w_mod_sh, b_mod_sh):
    mc = w_mod_sh.shape[1]

    def body(c8_ref, cctx_ref, w_ref, b_ref, mod_ref, cc_ref, cc_s, part_s, send1, recv1, send3, recv3):
        x, y, c = _place()
        me = 4 * x + 2 * y + c
        ms = 2 * x + y
        copies = []
        for k in range(1, 8):
            peer = (x ^ ((k >> 2) & 1), y ^ ((k >> 1) & 1), c ^ (k & 1))
            cp = pltpu.make_async_remote_copy(src_ref=c8_ref, dst_ref=cc_s.at[me], send_sem=send1.at[k],
                                              recv_sem=recv1.at[k], device_id=peer, device_id_type=MESH)
            cp.start()
            copies.append(cp)
        cc_s[me] = c8_ref[...]
        for cp in copies:
            cp.wait()
        cc_ref[...] = jnp.zeros_like(cc_ref)
        for j in range(8):
            cc_ref[j:j + 1, :] = cc_s[j, 0:1, :]
        cc_ref[8:9, :] = cctx_ref[...]
        v = cc_ref[...]
        part_s[ms] = _dot(v * _sig(v), w_ref[...]) + b_ref[...]
        copies = []
        for k in range(1, 4):
            peer = (x ^ ((k >> 1) & 1), y ^ (k & 1), c)
            cp = pltpu.make_async_remote_copy(src_ref=part_s.at[ms], dst_ref=part_s.at[ms], send_sem=send3.at[k],
                                              recv_sem=recv3.at[k], device_id=peer, device_id_type=MESH)
            cp.start()
            copies.append(cp)
        for cp in copies:
            cp.wait()
        for s in range(NSH):
            mod_ref[:, s * mc:(s + 1) * mc] = part_s[s]

    vm = pl.BlockSpec(memory_space=pltpu.VMEM)
    return pl.pallas_call(
        body, in_specs=[vm] * 4, out_specs=(vm, vm),
        out_shape=(jax.ShapeDtypeStruct((16, NSH * mc), F32), jax.ShapeDtypeStruct((16, D), F32)),
        scratch_shapes=[pltpu.VMEM((8, 8, D), F32), pltpu.VMEM((NSH, 16, mc), F32),
                        pltpu.SemaphoreType.DMA((8,)), pltpu.SemaphoreType.DMA((8,)),
                        pltpu.SemaphoreType.DMA((4,)), pltpu.SemaphoreType.DMA((4,))],
        name="mod_tp", compiler_params=_cp())(c8, c_ctx, w_mod_sh, b_mod_sh)


def _inproj_ag(x_all, mod, w_in_sh, b_in, narrow_sh, n_lat_tiles):
    m = x_all.shape[0]
    assert m % (11 * 16) == 0
    tm = m // 11
    nt = m // tm
    n_lat = n_lat_tiles * 256
    x_, y_ = lax.axis_index("x"), lax.axis_index("y")
    sids = jnp.stack([2 * x_ + y_, 2 * (1 - x_) + y_, 2 * x_ + 1 - y_, 2 * (1 - x_) + 1 - y_]).astype(jnp.int32)

    def body(sid_ref, x_ref, mod_ref, b_ref, wsh_ref, nsh_ref, z_ref, u_ref, wg_ref, ng_ref, w_s, u_s, *sems):
        n = pl.program_id(0)
        i = pl.program_id(1)
        rows = pl.ds(pl.multiple_of(i * tm, tm), tm)
        ag = ((wsh_ref, nsh_ref), (wg_ref, ng_ref)) + tuple(sems[:5])
        wsem = sems[5]

        def load(src):
            cp = pltpu.make_async_copy(src, w_s, wsem)
            cp.start()
            cp.wait()

        @pl.when((n == 0) & (i == 0))
        def _():
            local, chip, _ = _ag_copies(*ag)
            for cp in chip + local:
                cp.start()
            load(wsh_ref)

        for k in range(NSH - 1):
            @pl.when((n == k + 1) & (i == 0))
            def _():
                _, chip, hand = _ag_copies(*ag)
                chip[k].wait_recv()
                hand[k].start()
                hand[k].wait_recv()
                load(wg_ref.at[sid_ref[k + 1]])

        @pl.when(n == 0)
        def _():
            is_lat = (i * tm + lax.broadcasted_iota(jnp.int32, (tm, 1), 0)) < n_lat
            u, _ = _modulate(x_ref, mod_ref, is_lat)
            u_s[rows, :] = u.astype(BF)
            u_ref[...] = u.T.astype(BF)

        z_ref[...] = _dot(u_s[rows, :], w_s[...]) + b_ref[...]

        @pl.when((n == NSH - 1) & (i == nt - 1))
        def _():
            _ag_finish(*_ag_copies(*ag), done=NSH - 1)

    first = lambda n, i: jnp.where(n == 0, i, nt - 1)
    outs = pl.pallas_call(
        body, grid_spec=pltpu.PrefetchScalarGridSpec(
            num_scalar_prefetch=1, grid=(NSH, nt),
            in_specs=[pl.BlockSpec((tm, D), lambda n, i, sid: (first(n, i), 0)),
                      pl.BlockSpec((16, 3 * D), lambda n, i, sid: (0, 0)),
                      pl.BlockSpec((1, SHC), lambda n, i, sid: (0, sid[n])), _ANY, _ANY],
            out_specs=[pl.BlockSpec((tm, SHC), lambda n, i, sid: (i, sid[n])),
                       pl.BlockSpec((D, tm), lambda n, i, sid: (0, first(n, i))), _ANY, _ANY],
            scratch_shapes=[pltpu.VMEM((D, SHC), BF), pltpu.VMEM((m, D), BF)] + _ag_sems(2)
            + [pltpu.SemaphoreType.DMA]),
        out_shape=[jax.ShapeDtypeStruct((m, IN_COLS), F32), jax.ShapeDtypeStruct((D, m), BF),
                   jax.ShapeDtypeStruct((NSH,) + w_in_sh.shape, BF),
                   jax.ShapeDtypeStruct((NSH,) + narrow_sh.shape, narrow_sh.dtype)],
        name="inproj_ag", compiler_params=_cp(("arbitrary", "arbitrary")))(sids, x_all, mod, b_in, w_in_sh, narrow_sh)
    return outs


def _rs_sibling(grads, name):
    n = len(grads)

    def body(*refs):
        ins, got = refs[:n], refs[n:2 * n]
        send, recv = refs[2 * n:]
        x, y, c = _place()
        copies = []
        for j in range(n):
            hr = ins[j].shape[1] // 2
            for s in range(NSH):
                give = ins[j].at[s, pl.ds(pl.multiple_of((1 - c) * hr, 8), hr), :]
                cp = pltpu.make_async_remote_copy(src_ref=give, dst_ref=got[j].at[s], send_sem=send.at[NSH * j + s],
                                                  recv_sem=recv.at[NSH * j + s], device_id=(x, y, 1 - c),
                                                  device_id_type=MESH)
                cp.start()
                copies.append(cp)
        for cp in copies:
            cp.wait()

    half = [jax.ShapeDtypeStruct((NSH, g.shape[1] // 2, g.shape[2]), F32) for g in grads]
    return pl.pallas_call(
        body, in_specs=[_ANY] * n, out_specs=[_ANY] * n, out_shape=half,
        scratch_shapes=[pltpu.SemaphoreType.DMA((NSH * n,)), pltpu.SemaphoreType.DMA((NSH * n,))],
        name=name)(*grads)


def _core_vec():
    return lax.axis_index("c").astype(jnp.int32).reshape(1)


def _rs_add1(g, got, name):
    _, r, cols = g.shape
    hr = r // 2
    tr = _row_tile(hr, cols)
    nb = hr // tr

    def body(c_ref, g_ref, got_ref, o_ref):
        o_ref[...] = (g_ref[...] + got_ref[...]).astype(BF)

    spec = pl.BlockSpec((1, tr, cols), lambda s, i, c_ref: (s, i, 0))
    return pl.pallas_call(
        body, grid_spec=pltpu.PrefetchScalarGridSpec(
            num_scalar_prefetch=1, grid=(NSH, nb),
            in_specs=[pl.BlockSpec((1, tr, cols), lambda s, i, c_ref: (s, c_ref[0] * nb + i, 0)), spec],
            out_specs=spec),
        out_shape=jax.ShapeDtypeStruct((NSH, hr, cols), BF), name=name,
        compiler_params=_cp(("arbitrary", "arbitrary")))(_core_vec(), g, got)


def _rs_add2(sums, got, name):
    _, hr, cols = sums.shape
    tr = _row_tile(hr, cols)
    nb = hr // tr
    place = jnp.stack([2 * lax.axis_index("x") + lax.axis_index("y"), lax.axis_index("c")]).astype(jnp.int32)

    def body(p_ref, s_ref, got_ref, o_ref):
        f = lambda v: v.astype(F32)
        o_ref[...] = f(s_ref[0]) + f(got_ref[0]) + f(got_ref[1]) + f(got_ref[2])

    return pl.pallas_call(
        body, grid_spec=pltpu.PrefetchScalarGridSpec(
            num_scalar_prefetch=1, grid=(nb,),
            in_specs=[pl.BlockSpec((1, tr, cols), lambda i, p_ref: (p_ref[0], i, 0)),
                      pl.BlockSpec((3, tr, cols), lambda i, p_ref: (0, i, 0))],
            out_specs=pl.BlockSpec((tr, cols), lambda i, p_ref: (p_ref[1] * nb + i, 0))),
        out_shape=jax.ShapeDtypeStruct((2 * hr, cols), F32), name=name,
        compiler_params=_cp(("arbitrary",)))(place, sums, got)


def _rs_chip_copies(ins, got, send, recv):
    x, y, c = _place()
    peers = ((1 - x, y), (x, 1 - y), (1 - x, 1 - y))
    return [pltpu.make_async_remote_copy(src_ref=ins[j].at[2 * px + py], dst_ref=got[j].at[k],
                                         send_sem=send.at[3 * j + k], recv_sem=recv.at[3 * j + k],
                                         device_id=(px, py, c), device_id_type=MESH)
            for j in range(len(ins)) for k, (px, py) in enumerate(peers)]


def _ag_sibling(fulls):
    n = len(fulls)
    nck = 4

    def body(*refs):
        outs = refs[n:2 * n]
        send, recv = refs[2 * n:]
        x, y, c = _place()
        copies = []
        for j in range(n):
            qr = outs[j].shape[0] // (2 * nck)
            for k in range(nck):
                rows = outs[j].at[pl.ds(pl.multiple_of((c * nck + k) * qr, 8), qr), :]
                cp = pltpu.make_async_remote_copy(src_ref=rows, dst_ref=rows, send_sem=send.at[nck * j + k],
                                                  recv_sem=recv.at[nck * j + k], device_id=(x, y, 1 - c),
                                                  device_id_type=MESH)
                cp.start()
                copies.append(cp)
        for cp in copies:
            cp.wait()

    return pl.pallas_call(
        body, in_specs=[_ANY] * n, out_specs=[_ANY] * n,
        out_shape=[jax.ShapeDtypeStruct(f.shape, F32) for f in fulls],
        input_output_aliases={j: j for j in range(n)},
        scratch_shapes=[pltpu.SemaphoreType.DMA((nck * n,)), pltpu.SemaphoreType.DMA((nck * n,))],
        name="ag_sibling")(*fulls)


def _allreduce_small(buf):
    rows = buf.shape[0]
    pr = rows // 8

    def body(in_ref, out_ref, part, stage_a, stage_b, *sems):
        (sa, ra, sb, rb, sc, rc, sd, rd) = sems
        x, y, c = _place()
        s = 2 * x + y
        sib = (x, y, 1 - c)
        chips = ((1 - x, y), (x, 1 - y), (1 - x, 1 - y))

        def piece(ref, chip, core):
            return ref.at[pl.ds(pl.multiple_of((2 * chip + core) * pr, 8), pr), :]

        def run(copies):
            for cp in copies:
                cp.start()
            for cp in copies:
                cp.wait()

        run([pltpu.make_async_remote_copy(src_ref=piece(in_ref, j, 1 - c), dst_ref=stage_a.at[j], send_sem=sa.at[j],
                                          recv_sem=ra.at[j], device_id=sib, device_id_type=MESH) for j in range(NSH)])
        for j in range(NSH):
            part[j] = piece(in_ref, j, c)[...] + stage_a[j]
        run([pltpu.make_async_remote_copy(src_ref=part.at[2 * px + py], dst_ref=stage_b.at[k], send_sem=sb.at[k],
                                          recv_sem=rb.at[k], device_id=(px, py, c), device_id_type=MESH)
             for k, (px, py) in enumerate(chips)])
        piece(out_ref, s, c)[...] = part[s] + stage_b[0] + stage_b[1] + stage_b[2]
        run([pltpu.make_async_remote_copy(src_ref=piece(out_ref, s, c), dst_ref=piece(out_ref, s, c),
                                          send_sem=sc.at[k], recv_sem=rc.at[k], device_id=(px, py, c),
                                          device_id_type=MESH) for k, (px, py) in enumerate(chips)])
        run([pltpu.make_async_remote_copy(src_ref=piece(out_ref, j, c), dst_ref=piece(out_ref, j, c), send_sem=sd.at[j],
                                          recv_sem=rd.at[j], device_id=sib, device_id_type=MESH) for j in range(NSH)])

    vm = pl.BlockSpec(memory_space=pltpu.VMEM)
    return pl.pallas_call(
        body, in_specs=[vm], out_specs=vm, out_shape=jax.ShapeDtypeStruct((rows, D), F32),
        scratch_shapes=[pltpu.VMEM((NSH, pr, D), F32), pltpu.VMEM((NSH, pr, D), F32), pltpu.VMEM((3, pr, D), F32)]
        + [pltpu.SemaphoreType.DMA((NSH,))] * 8,
        name="allreduce_small", compiler_params=_cp())(buf)


def _rows(a):
    flat = a.reshape(-1)
    pad = (-flat.shape[0]) % D
    if pad:
        flat = jnp.concatenate([flat, jnp.zeros((pad,), flat.dtype)])
    return flat.reshape(-1, D)


def _pad_rows(a, mult):
    pad = (-a.shape[0]) % mult
    return jnp.concatenate([a, jnp.zeros((pad, a.shape[1]), a.dtype)]) if pad else a


def _local_step(x, c, ctx, c_ctx, tgt, me, shard, sh, b_mod, b_in, norm_g, cb, wr, wi, ln_g, ln_b):
    t_lat, t_ctx = x.shape[0], ctx.shape[0]
    nlt = t_lat // 256
    nlb, ncb = t_lat // RB, t_ctx // RB
    mc = 3 * D // NSH
    mod_all, cc_all = _mod_tp(jnp.zeros((8, D), F32).at[0].set(c), c_ctx.reshape(1, D), sh["w_mod"],
                              lax.dynamic_slice_in_dim(b_mod, shard * mc, mc, axis=1))
    mod = jnp.zeros((16, 3 * D), F32).at[0].set(mod_all[me]).at[1].set(mod_all[8])
    cc = jnp.zeros((16, D), F32).at[0].set(c).at[1].set(c_ctx)
    x_all = jnp.concatenate([x, ctx], axis=0)
    z, u_all, w_in_g, nar = _inproj_ag(x_all, mod, sh["w_in"], b_in, sh["narrow"], nlt)
    nar = jnp.transpose(nar, (1, 0, 2)).reshape(-1, D)
    lbl, cw, br, bi, lam = nar[0:4].reshape(2, 2, D), nar[4:8], nar[8:10], nar[10:12], nar[12:14]
    o0, st0, (w_mod_g, p_a, p_b, w_out) = _gla_fwd(z, lbl, 0, nlb, ncb,
                                                   gather=[sh[k] for k in ("w_mod", "p_a", "p_b", "w_out")])
    p_a, p_b, w_out = p_a.reshape(D, D), p_b.reshape(D, D), w_out.reshape(D, D)
    o1, st1, _ = _gla_fwd(z, lbl, 1, nlb, ncb)
    h0, hp0, a0, hpc0 = _rglru_fwd(z, cw, cb, wr, br, wi, bi, lam, 0, t_lat, t_ctx)
    h1, hp1, a1, hpc1 = _rglru_fwd(z, cw, cb, wr, br, wi, bi, lam, 1, t_lat, t_ctx)
    (do, dh, dz4a, dz4b, dz6a, dz_sh3, gxres, y, dout, oa, dpa, obv, dpb, acc) = _merge(
        o0, o1, h0, h1, z, x_all, tgt, mod, norm_g, ln_g, ln_b, p_a, p_b, w_out, t_lat)
    gp_a = _wgrad(oa, dpa, "wgrad_pa")
    gp_b = _wgrad(obv, dpb, "wgrad_pb")
    gw_out = _wgrad(y, dout, "wgrad_wout")
    dxc_lat, dxc_ctx, dwr0, dwi0, sb0 = _rglru_bwd(z, cw, cb, wr, br, wi, bi, lam, dh, hp0, a0, hpc0, 0, t_lat, t_ctx)
    dz5_lat, dz5_ctx, dwr1, dwi1, sb1 = _rglru_bwd(z, cw, cb, wr, br, wi, bi, lam, dh, hp1, a1, hpc1, 1, t_lat, t_ctx,
                                                   prev=(dxc_lat, dxc_ctx))
    dq0, dv0, dz1, sa0 = _gla_bwd(z, lbl, do, st0, 0, nlb, ncb)
    dz_sh0, dz_sh1, sa1 = _gla_bwd(z, lbl, do, st1, 1, nlb, ncb, prev=(dq0, dv0, dz1, dz4a))
    dz = (dz_sh0, dz_sh1, _pack_shard2(dz4b, dz5_lat, dz5_ctx, dz6a), dz_sh3)
    grads = [_wgrad_in(u_all, dz)] + [g.reshape(NSH, D // NSH, D) for g in (gp_a, gp_b, gw_out)]
    sums = [_rs_add1(g, b, f"rs_add1_{j}") for j, (g, b) in enumerate(zip(grads, _rs_sibling(grads, "rs_sibling")))]
    gx, dm, got = _du(dz, w_in_g, x_all, mod, gxres, nlt, sums)
    fulls = [_rs_add2(a, b, f"rs_add2_{j}") for j, (a, b) in enumerate(zip(sums, got))]
    big = dict(zip(_RS, _ag_sibling(fulls)))
    dmod = jnp.zeros((16, 3 * D), F32).at[0:2, 0:2 * D].set(dm[0:2]).at[0, 2 * D:].set(acc[0])
    dcc = _mod_bwd(cc, dmod, w_mod_g)
    small = dict(
        c_ctx=dcc[1:2], b_mod=(dmod[0] + dmod[1]).reshape(3, D),
        b_in=jnp.stack([sa1[2], sa0[0], sa1[0], sa1[3], acc[3], sb1[4], acc[4], acc[5], acc[6]]),
        lb_logits=jnp.stack([sa0[1], sa1[1], -sa0[1], -sa1[1]]),
        norm_a_g=acc[7:8], conv_w=sb1[8:12], conv_b=sb1[3:4],
        w_r=jnp.stack([dwr0, dwr1]).reshape(-1, D), w_i=jnp.stack([dwi0, dwi1]).reshape(-1, D),
        b_r=jnp.stack([sb0[0], sb1[0]]), b_i=jnp.stack([sb0[1], sb1[1]]), lam=jnp.stack([sb0[2], sb1[2]]),
        ln_g=acc[1:2], ln_b=acc[2:3])
    return acc[9, 0], gx, big, small, dmod, cc_all


_RS =("w_in", "p_a", "p_b", "w_out")
_SMALL =("c_ctx", "b_mod", "b_in", "lb_logits", "norm_a_g", "conv_w", "conv_b", "w_r", "w_i", "b_r", "b_i", "lam",
          "ln_g", "ln_b")
_BIG = ("w_mod", "w_in", "p_a", "p_b", "w_out")
_COL_SHARDED = ("lb_logits", "conv_w", "b_r", "b_i", "lam")
_WEIGHTS = ("c_ctx", "w_mod", "b_mod", "w_in", "b_in", "lb_logits", "norm_a_g", "conv_w", "conv_b", "w_r", "b_r", "w_i",
            "b_i", "lam", "p_a", "p_b", "w_out", "ln_g", "ln_b")


def kernel(x, c, ctx, c_ctx, w_mod, b_mod, w_in, b_in, lb_logits, norm_a_g, conv_w, conv_b, w_r, b_r, w_i, b_i, lam, p_a, p_b, w_out, ln_g, ln_b, loss_target, m_c_ctx, m_w_mod, m_b_mod, m_w_in, m_b_in, m_lb_logits, m_norm_a_g, m_conv_w, m_conv_b, m_w_r, m_b_r, m_w_i, m_b_i, m_lam, m_p_a, m_p_b, m_w_out, m_ln_g, m_ln_b, v_c_ctx, v_w_mod, v_b_mod, v_w_in, v_b_in, v_lb_logits, v_norm_a_g, v_conv_w, v_conv_b, v_w_r, v_b_r, v_w_i, v_b_i, v_lam, v_p_a, v_p_b, v_w_out, v_ln_g, v_ln_b):
    w = dict(c_ctx=c_ctx, w_mod=w_mod, b_mod=b_mod, w_in=w_in, b_in=b_in, lb_logits=lb_logits, norm_a_g=norm_a_g,
             conv_w=conv_w, conv_b=conv_b, w_r=w_r, b_r=b_r, w_i=w_i, b_i=b_i, lam=lam, p_a=p_a, p_b=p_b, w_out=w_out,
             ln_g=ln_g, ln_b=ln_b)
    m = dict(c_ctx=m_c_ctx, w_mod=m_w_mod, b_mod=m_b_mod, w_in=m_w_in, b_in=m_b_in, lb_logits=m_lb_logits,
             norm_a_g=m_norm_a_g, conv_w=m_conv_w, conv_b=m_conv_b, w_r=m_w_r, b_r=m_b_r, w_i=m_w_i, b_i=m_b_i,
             lam=m_lam, p_a=m_p_a, p_b=m_p_b, w_out=m_w_out, ln_g=m_ln_g, ln_b=m_ln_b)
    v = dict(c_ctx=v_c_ctx, w_mod=v_w_mod, b_mod=v_b_mod, w_in=v_w_in, b_in=v_b_in, lb_logits=v_lb_logits,
             norm_a_g=v_norm_a_g, conv_w=v_conv_w, conv_b=v_conv_b, w_r=v_w_r, b_r=v_b_r, w_i=v_w_i, b_i=v_b_i,
             lam=v_lam, p_a=v_p_a, p_b=v_p_b, w_out=v_w_out, ln_g=v_ln_g, ln_b=v_ln_b)
    shard = 2 * lax.axis_index("x") + lax.axis_index("y")
    cs = D // NSH

    sh = {k: w[k][0].astype(BF) for k in _BIG}
    sh["narrow"] = _pad_rows(jnp.concatenate([lb_logits.reshape(4, cs), conv_w[0], b_r[0], b_i[0], lam[0]], axis=0), 8)
    me = 2 * shard + lax.axis_index("c")
    loss, gx, big, small, dmod, cc_all = _local_step(
        x[0], c[0], ctx[0], c_ctx, loss_target[0], me, shard, sh, b_mod, b_in, norm_a_g, conv_b, w_r[0], w_i[0],
        ln_g, ln_b)
    loss = lax.psum(loss, ("x", "y", "c"))

    dmod_rows = jnp.zeros((16, 3 * D), F32).at[me].set(dmod[0]).at[8].set(dmod[1]).reshape(48, D)
    sizes = [small[k].shape[0] for k in _SMALL]
    red = _allreduce_small(_pad_rows(jnp.concatenate([_pad_rows(small[k], 8) for k in _SMALL] + [dmod_rows],
                                                     axis=0), 64))
    grads = {}
    off = 0
    for k, n in zip(_SMALL, sizes):
        g = red[off:off + n]
        off += n + (-n) % 8
        if k == "norm_a_g":
            g = g[:, :DH]
        if k in _COL_SHARDED:
            g = lax.dynamic_slice_in_dim(g, shard * cs, cs, axis=1)
        grads[k] = g.reshape(w[k].shape)
    for k in _RS:
        grads[k] = big[k].reshape(w[k].shape)
    dmod_all = red[off:off + 48].reshape(16, 3 * D)
    mc = 3 * D // NSH
    grads["w_mod"] = _wmod_grad(cc_all, lax.dynamic_slice_in_dim(dmod_all, shard * mc, mc, axis=1)).reshape(
        w["w_mod"].shape)

    delta, new_m, new_v = {}, {}, {}
    for k in _BIG:
        shp = w[k].shape
        two = lambda a: a.reshape(shp[-2], shp[-1])
        d_, m_, v_ = _adamw(two(w[k]), two(grads[k]), two(m[k]), two(v[k]), f"adamw_{k}")
        delta[k], new_m[k], new_v[k] = d_.reshape(shp), m_.reshape(shp), v_.reshape(shp)
    d_, m_, v_ = _adamw_many(*[[t[k] for k in _SMALL] for t in (w, grads, m, v)])
    delta.update(zip(_SMALL, d_))
    new_m.update(zip(_SMALL, m_))
    new_v.update(zip(_SMALL, v_))

    return (loss, gx[None], *[grads[k] for k in _WEIGHTS], *[delta[k] for k in _WEIGHTS],
            *[new_m[k] for k in _WEIGHTS], *[new_v[k] for k in _WEIGHTS])
```

```python
import functools

import jax
import jax.numpy as jnp
from jax import lax
from jax.experimental import pallas as pl
from jax.experimental.pallas import tpu as pltpu

F32 = jnp.float32
BF = jnp.bfloat16
MESH = pl.DeviceIdType.MESH

D = 1024
H = 8
DH = 128
CH = 64
RB = 256
NCK = RB // CH
GRID_W = 64
CB = 8
RCH = 16
IN_COLS = 9 * D
NSH = 4
SHC = IN_COLS // NSH
CUT2 = SHC - 2 * D
CUT4 = 2 * SHC - 4 * D
CUT6 = 3 * SHC - 6 * D
RG_C = 8.0
ALPHA = 2.0 ** 0.25
LN_EPS = 1e-5
RMS_EPS = 1e-6
Q_SCALE = DH ** -0.5
ADAM_LR, ADAM_B1, ADAM_B2, ADAM_EPS, ADAM_WD, ADAM_STEP = 0.001, 0.9, 0.999, 1e-08, 0.01, 10
VMEM_LIMIT = 56 * 1024 * 1024
VMEM_LIMIT_MERGE = 60 * 1024 * 1024
VMEM_LIMIT_RGLRU_BWD = 63 * 1024 * 1024


def _cp(sem=None, vmem=VMEM_LIMIT):
    return pltpu.CompilerParams(dimension_semantics=sem, vmem_limit_bytes=vmem)


def _sig(x):
    return 0.5 * jnp.tanh(0.5 * x) + 0.5


def _dot(a, b):
    return jnp.dot(a.astype(BF), b.astype(BF), preferred_element_type=F32)


def _dot_nt(a, b):
    return lax.dot_general(a.astype(BF), b.astype(BF), (((1,), (1,)), ((), ())), preferred_element_type=F32)


def _dot_tn(a, b):
    return lax.dot_general(a.astype(BF), b.astype(BF), (((0,), (0,)), ((), ())), preferred_element_type=F32)


def _colsum(v):
    return jnp.sum(v, axis=0, keepdims=True)


def _mod_bwd(cc, dmod, w_mod_g):
    def body(cc_ref, dm_ref, w_ref, dcc_ref):
        v = cc_ref[...]
        sg = _sig(v)
        ds = jnp.zeros((16, D), F32)
        for k in range(NSH):
            ds = ds + _dot_nt(dm_ref[:, k * 768:(k + 1) * 768], w_ref[k])
        dcc_ref[...] = ds * (sg * (1.0 + v * (1.0 - sg)))
    return pl.pallas_call(body, out_shape=jax.ShapeDtypeStruct((16, D), F32),
                          name="mod_bwd", compiler_params=_cp())(cc, dmod, w_mod_g)


def _wmod_grad(cc, dmod_cols):
    def body(cc_ref, dm_ref, dw_ref):
        v = cc_ref[...]
        dw_ref[...] = _dot_tn(v * _sig(v), dm_ref[...])
    return pl.pallas_call(body, out_shape=jax.ShapeDtypeStruct((D, dmod_cols.shape[1]), F32),
                          name="wmod_grad", compiler_params=_cp())(cc, dmod_cols)


def _modulate(xv, mod_ref, is_lat):
    sh = jnp.where(is_lat, mod_ref[0:1, 0:D], mod_ref[1:2, 0:D])
    sc = jnp.where(is_lat, mod_ref[0:1, D:2 * D], mod_ref[1:2, D:2 * D])
    return xv * (1.0 + sc) + sh


def _gla_mask(d, n):
    row = lax.broadcasted_iota(jnp.int32, (n, n), 0)
    col = lax.broadcasted_iota(jnp.int32, (n, n), 1)
    same = (row // CH) == (col // CH)
    return same & ((row <= col) if d else (row >= col))


def _chunk_cumsum(v, rev):
    n = v.shape[0]
    pos = lax.broadcasted_iota(jnp.int32, v.shape, 0) & (CH - 1)
    s = 1
    while s < CH:
        if rev:
            v = v + jnp.where(pos < CH - s, pltpu.roll(v, n - s, 0), 0.0)
        else:
            v = v + jnp.where(pos >= s, pltpu.roll(v, s, 0), 0.0)
        s *= 2
    return v


def _chunk_rows(c):
    return slice(c * CH, (c + 1) * CH)


def _gla_features(zq, zf, lb, d):
    sq = _sig(zq)
    q = zq * sq * Q_SCALE
    sf = _sig(zf)
    f = lb + (1.0 - lb) * sf
    k = 1.0 - f
    g = _chunk_cumsum(jnp.log(f), d)
    last = 0 if d else CH - 1
    gls = [g[c * CH + last:c * CH + last + 1, :] for c in range(NCK)]
    glb = jnp.concatenate([jnp.broadcast_to(gl, (CH, D)) for gl in gls], axis=0)
    eg, eig, eeg = jnp.exp(g), jnp.exp(-g), jnp.exp(glb - g)
    decs = [jnp.exp(gl) for gl in gls]
    return sq, sf, f, q * eg, k * eig, k * eeg, eg, eig, eeg, decs


def _lower_bound(lbl_ref, d):
    return _sig(lbl_ref[0, d:d + 1, :] - lbl_ref[1, d:d + 1, :])


def _gla_rb(d, nlb, ncb):
    nrb = nlb + ncb
    if d:
        return lambda s: nrb - 1 - s
    return lambda s: jnp.where(s < ncb, nlb + s, s - ncb)


def _gla_fwd(z, lbl, d, nlb, ncb, gather=()):
    m = z.shape[0]
    nrb = nlb + ncb
    rb = _gla_rb(d, nlb, ncb)
    ng = len(gather)

    def body(*refs):
        q_ref, f_ref, v_ref, lbl_ref = refs[:4]
        ag_in = refs[4:4 + ng]
        o_ref, st_ref = refs[4 + ng:6 + ng]
        ag_out = refs[6 + ng:6 + 2 * ng]
        S = refs[6 + 2 * ng]
        ag_sems = refs[7 + 2 * ng:]
        s = pl.program_id(0)

        @pl.when(s == 0)
        def _():
            S[...] = jnp.zeros_like(S)
            if ng:
                local, chip, _ = _ag_copies(ag_in, ag_out, *ag_sems)
                for cp in chip + local:
                    cp.start()

        lb = _lower_bound(lbl_ref, d)
        mb = _gla_mask(d, CH)
        _, _, _, qd, ki, ke, _, _, _, decs = _gla_features(q_ref[...], f_ref[...], lb, d)
        qd, ki, ke, v = qd.astype(BF), ki.astype(BF), ke.astype(BF), v_ref[...].astype(BF)
        order = range(NCK - 1, -1, -1) if d else range(NCK)
        for h in range(H):
            hs = slice(h * DH, (h + 1) * DH)
            intra, upd = {}, {}
            for c in range(NCK):
                rs = _chunk_rows(c)
                a = jnp.where(mb, _dot_nt(qd[rs, hs], ki[rs, hs]), 0.0)
                intra[c] = _dot(a, v[rs, hs])
                upd[c] = _dot_tn(v[rs, hs], ke[rs, hs])
            st = S[h]
            for c in order:
                rs = _chunk_rows(c)
                st_ref[c, h] = st
                o_ref[rs, hs] = intra[c] + _dot_nt(qd[rs, hs], st)
                st = st * decs[c][:, hs] + upd[c]
            S[h] = st

        if ng:
            @pl.when(s == nrb - 1)
            def _():
                _ag_finish(*_ag_copies(ag_in, ag_out, *ag_sems))

    def zspec(cb):
        return pl.BlockSpec((RB, D), lambda s: (rb(s), cb))

    outs = pl.pallas_call(
        body, grid=(nrb,),
        in_specs=[zspec(0), zspec(1 + d), zspec(3), pl.BlockSpec((2, 2, D), lambda s: (0, 0, 0))] + [_ANY] * ng,
        out_specs=[pl.BlockSpec((RB, D), lambda s: (rb(s), 0)),
                   pl.BlockSpec((NCK, H, DH, DH), lambda s: (rb(s), 0, 0, 0))] + [_ANY] * ng,
        out_shape=[jax.ShapeDtypeStruct((m, D), F32), jax.ShapeDtypeStruct((m // CH, H, DH, DH), F32)]
        + [jax.ShapeDtypeStruct((NSH,) + g.shape, g.dtype) for g in gather],
        scratch_shapes=[pltpu.VMEM((H, DH, DH), F32)] + (_ag_sems(ng) if ng else []),
        name=f"gla_fwd{d}", compiler_params=_cp(("arbitrary",)))(z, z, z, lbl, *gather)
    return outs[0], outs[1], list(outs[2:])


def _gla_bwd(z, lbl, do_lat, states, d, nlb, ncb, prev=None):
    m = z.shape[0]
    nrb = nlb + ncb
    fwd_rb = _gla_rb(d, nlb, ncb)
    rb = lambda s: fwd_rb(nrb - 1 - s)
    last = prev is not None

    def body(*refs):
        if last:
            (q_ref, f_ref, v_ref, lbl_ref, do_ref, st_ref, pq_ref, pv_ref, dz1_ref, dz4a_ref, sh0_ref, sh1_ref, sum_ref,
             dS) = refs
        else:
            q_ref, f_ref, v_ref, lbl_ref, do_ref, st_ref, o0_ref, o1_ref, o2_ref, sum_ref, dS = refs
        s = pl.program_id(0)
        is_lat = rb(s) < nlb

        @pl.when(s == 0)
        def _():
            dS[...] = jnp.zeros_like(dS)
            sum_ref[...] = jnp.zeros_like(sum_ref)

        lb = _lower_bound(lbl_ref, d)
        mb = _gla_mask(d, RB)
        zq = q_ref[...]
        sq, sf, f, qd, ki, ke, eg, eig, eeg, decs = _gla_features(zq, f_ref[...], lb, d)
        qdb, kib, keb, vb = qd.astype(BF), ki.astype(BF), ke.astype(BF), v_ref[...].astype(BF)
        dob = jnp.where(is_lat, do_ref[...], 0.0).astype(BF)
        order = range(NCK) if d else range(NCK - 1, -1, -1)
        dqd_h, dki_h, dke_h, dv_h, ddec_h = [], [], [], [], []
        for h in range(H):
            hs = slice(h * DH, (h + 1) * DH)
            a = jnp.where(mb, _dot_nt(qdb[:, hs], kib[:, hs]), 0.0).astype(BF)
            da = jnp.where(mb, _dot_nt(dob[:, hs], vb[:, hs]), 0.0).astype(BF)
            dqd_i = _dot(da, kib[:, hs])
            dki_h.append(_dot_tn(da, qdb[:, hs]))
            dvi = _dot_tn(a, dob[:, hs])
            dqd, inc = {}, {}
            for c in range(NCK):
                rs = _chunk_rows(c)
                dqd[c] = dqd_i[rs, :] + _dot(dob[rs, hs], st_ref[c, h])
                inc[c] = _dot_tn(dob[rs, hs], qdb[rs, hs])
            dst = dS[h]
            dke, dv, ddec = {}, {}, {}
            for c in order:
                rs = _chunk_rows(c)
                dv[c] = dvi[rs, :] + _dot_nt(keb[rs, hs], dst)
                dke[c] = _dot(vb[rs, hs], dst)
                ddec[c] = _colsum(st_ref[c, h] * dst)
                dst = inc[c] + dst * decs[c][:, hs]
            dS[h] = dst
            cat = lambda t: jnp.concatenate([t[c] for c in range(NCK)], axis=0)
            dqd_h.append(cat(dqd))
            dke_h.append(cat(dke))
            dv_h.append(cat(dv))
            ddec_h.append([ddec[c] for c in range(NCK)])
        lanes = lambda parts: jnp.concatenate(parts, axis=1)
        dqd, dki, dke, dv = lanes(dqd_h), lanes(dki_h), lanes(dke_h), lanes(dv_h)
        dq = dqd * eg
        dk = dki * eig + dke * eeg
        dke_ke = dke * ke
        dg = dqd * qd - dki * ki - dke_ke
        dgl = [_colsum(dke_ke[_chunk_rows(c), :]) + lanes([ddec_h[h][c] for h in range(H)]) * decs[c]
               for c in range(NCK)]
        dglb = jnp.concatenate([jnp.broadcast_to(t, (CH, D)) for t in dgl], axis=0)
        df = (_chunk_cumsum(dg, 1 - d) + dglb) / f - dk
        dzf = df * (1.0 - lb) * (sf * (1.0 - sf))
        sum_ref[0:1, :] += _colsum(dzf)
        sum_ref[1:2, :] += _colsum(df * (1.0 - sf))
        if last:
            dz0 = (dq + pq_ref[...]) * (Q_SCALE * (sq * (1.0 + zq * (1.0 - sq))))
            dz3 = dv + pv_ref[...]
            sum_ref[2:3, :] += _colsum(dz0)
            sum_ref[3:4, :] += _colsum(dz3)
            dz2 = dzf.astype(BF)
            sh0_ref[:, 0:D] = dz0.astype(BF)
            sh0_ref[:, D:2 * D] = dz1_ref[...]
            sh0_ref[:, 2 * D:] = dz2[:, :CUT2]
            sh1_ref[:, 0:D - CUT2] = dz2[:, CUT2:]
            sh1_ref[:, D - CUT2:2 * D - CUT2] = dz3.astype(BF)
            sh1_ref[:, 2 * D - CUT2:] = dz4a_ref[...]
        else:
            o0_ref[...] = dq
            o1_ref[...] = dv
            o2_ref[...] = dzf.astype(BF)

        @pl.when(s == nrb - 1)
        def _():
            sum_ref[1:2, :] = sum_ref[1:2, :] * (lb * (1.0 - lb))

    def zspec(cb):
        return pl.BlockSpec((RB, D), lambda s: (rb(s), cb))

    rowspec = pl.BlockSpec((RB, D), lambda s: (rb(s), 0))
    in_specs = [zspec(0), zspec(1 + d), zspec(3), pl.BlockSpec((2, 2, D), lambda s: (0, 0, 0)),
                pl.BlockSpec((RB, D), lambda s: (jnp.minimum(rb(s), nlb - 1), 0)),
                pl.BlockSpec((NCK, H, DH, DH), lambda s: (rb(s), 0, 0, 0))]
    args = [z, z, z, lbl, do_lat, states]
    sumspec = pl.BlockSpec((8, D), lambda s: (0, 0))
    if last:
        in_specs += [rowspec, rowspec, rowspec, pl.BlockSpec((RB, CUT4), lambda s: (rb(s), 0))]
        args += list(prev)
        shspec = pl.BlockSpec((RB, SHC), lambda s: (rb(s), 0))
        out_specs = (shspec, shspec, sumspec)
        out_shape = (jax.ShapeDtypeStruct((m, SHC), BF), jax.ShapeDtypeStruct((m, SHC), BF))
    else:
        out_specs = (rowspec, rowspec, rowspec, sumspec)
        out_shape = (jax.ShapeDtypeStruct((m, D), F32), jax.ShapeDtypeStruct((m, D), F32),
                     jax.ShapeDtypeStruct((m, D), BF))
    return pl.pallas_call(
        body, grid=(nrb,), in_specs=in_specs, out_specs=out_specs,
        out_shape=out_shape + (jax.ShapeDtypeStruct((8, D), F32),),
        scratch_shapes=[pltpu.VMEM((H, DH, DH), F32)],
        name=f"gla_bwd{d}", compiler_params=_cp(("arbitrary",)))(*args)


def _shift_rows(v, k, r):
    row = lax.broadcasted_iota(jnp.int32, v.shape, 0)
    rolled = pltpu.roll(v, k % r, 0)
    return jnp.where((row >= k) & (row < r + k), rolled, 0.0)


def _conv_fwd(xb, cw_ref, cb_ref, r):
    return (cb_ref[...] + _shift_rows(xb, 1, r) * cw_ref[0:1, :] + xb * cw_ref[1:2, :]
            + _shift_rows(xb, -1, r) * cw_ref[2:3, :] + _shift_rows(xb, -2, r) * cw_ref[3:4, :])


def _window(ref, lo, n, rows):
    parts = []
    if lo < 0:
        parts.append(jnp.zeros((-lo,) + tuple(ref.shape[1:]), F32))
    parts.append(ref[max(lo, 0):min(lo + n, rows)])
    if lo + n > rows:
        parts.append(jnp.zeros((lo + n - rows,) + tuple(ref.shape[1:]), F32))
    return parts[0] if len(parts) == 1 else jnp.concatenate(parts, axis=0)


def _conv_cols(x_ref, cw_ref, cb_ref, r0, n, rows):
    w = _window(x_ref, r0 - 1, n + 3, rows)
    return (cb_ref[...] + w[0:n] * cw_ref[0:1, :] + w[1:n + 1] * cw_ref[1:2, :] + w[2:n + 2] * cw_ref[2:3, :]
            + w[3:n + 3] * cw_ref[3:4, :])


def _softplus_neg(lam):
    y = jnp.exp(-jnp.abs(lam))
    u = 1.0 + y
    tiny = u == 1.0
    l1p = jnp.where(tiny, y, jnp.log(u) * (y / jnp.where(tiny, 1.0, u - 1.0)))
    return jnp.maximum(-lam, 0.0) + l1p


def _gates(xc, wr_ref, br, wi_ref, bi, sp):
    xcb = xc.astype(BF)
    rs, is_ = [], []
    for g in range(H):
        gs = slice(g * DH, (g + 1) * DH)
        rs.append(jnp.dot(xcb[:, gs], wr_ref[g].astype(BF), preferred_element_type=F32))
        is_.append(jnp.dot(xcb[:, gs], wi_ref[g].astype(BF), preferred_element_type=F32))
    r = _sig(jnp.concatenate(rs, axis=1) + br)
    i = _sig(jnp.concatenate(is_, axis=1) + bi)
    log_a = (-RG_C * r) * sp
    a = jnp.exp(log_a)
    t = jnp.tanh(log_a)
    om = (-2.0 * t) / (1.0 - t)
    return r, i, a, om


def _scan_rows(d, nrows, a_s, b_s, h_s, h0):
    nsl = nrows // 8

    def slab(j, h):
        jj = (nsl - 1 - j) if d else j
        r0 = pl.multiple_of(jj * 8, 8)
        for t in (range(7, -1, -1) if d else range(8)):
            h = a_s[pl.ds(r0 + t, 1), :] * h + b_s[pl.ds(r0 + t, 1), :]
            h_s[pl.ds(r0 + t, 1), :] = h
        return h

    return lax.fori_loop(0, nsl, slab, h0)


def _col_of(d, ncols):
    if d:
        return lambda s: ncols - jnp.maximum(s, 1)
    return lambda s: jnp.maximum(s, 1) - 1


def _rglru_fwd(z, cw, cb, wr, br, wi, bi, lam, d, t_lat, t_ctx):
    m = z.shape[0]
    rows = t_lat // GRID_W
    z3 = z.reshape(m // GRID_W, GRID_W, IN_COLS)
    nblk = GRID_W // CB
    blk = _col_of(d, nblk)
    cblk = t_lat // t_ctx
    rc = min(RCH, rows)

    def body(zc_ref, zx_ref, cw_ref, cb_ref, wr_ref, br_ref, wi_ref, bi_ref, lam_ref,
             hx_ref, hpx_ref, ax_ref, hpc_ref, a_s, b_s, h_s, hcar, a3, b3, cin_s):
        s = pl.program_id(0)
        sp = _softplus_neg(lam_ref[d:d + 1, :])
        br_ = br_ref[d:d + 1, :]
        bi_ = bi_ref[d:d + 1, :]

        @pl.when(s == 0)
        def _():
            xc = _conv_fwd(zc_ref[...], cw_ref, cb_ref, t_ctx)
            _, i, a, om = _gates(xc, wr_ref, br_, wi_ref, bi_, sp)
            a_s[...] = a
            b_s[...] = jnp.sqrt(om) * (i * xc)
            h0 = jnp.zeros((1, D), F32)
            hcar[0:1, :] = _scan_rows(d, t_ctx, a_s, b_s, h_s, h0)
            hs = h_s[...]
            row = lax.broadcasted_iota(jnp.int32, (t_ctx, D), 0)
            if d:
                hpc_ref[...] = jnp.where(row == t_ctx - 1, h0, pltpu.roll(hs, t_ctx - 1, 0))
            else:
                hpc_ref[...] = jnp.where(row == 0, h0, pltpu.roll(hs, 1, 0))

        @pl.when(s > 0)
        def _():
            for r0 in range(0, rows, rc):
                xc = _conv_cols(zx_ref, cw_ref, cb_ref, r0, rc, rows).reshape(rc * CB, D)
                _, i, a, om = _gates(xc, wr_ref, br_, wi_ref, bi_, sp)
                a3[r0:r0 + rc] = a.reshape(rc, CB, D)
                ax_ref[r0:r0 + rc] = a.reshape(rc, CB, D)
                b3[r0:r0 + rc] = (jnp.sqrt(om) * (i * xc)).reshape(rc, CB, D)

            def local(t, carry):
                hl, p = carry
                r = (rows - 1 - t) if d else t
                a = a3[r]
                hl = a * hl + b3[r]
                p = a * p
                b3[r] = hl
                a3[r] = p
                return hl, p

            hl, p = lax.fori_loop(0, rows, local, (jnp.zeros((CB, D), F32), jnp.ones((CB, D), F32)))
            cin = hcar[0:1, :]
            for j in (range(CB - 1, -1, -1) if d else range(CB)):
                cin_s[j:j + 1, :] = cin
                cin = hl[j:j + 1, :] + p[j:j + 1, :] * cin
            hcar[0:1, :] = cin
            c_in = cin_s[...]

            def fix(t, prev):
                r = (rows - 1 - t) if d else t
                h = b3[r] + a3[r] * c_in
                hx_ref[r] = h
                hpx_ref[r] = prev
                return h

            lax.fori_loop(0, rows, fix, c_in)

    full = lambda shp: pl.BlockSpec(shp, lambda s: (0,) * len(shp))
    colspec = pl.BlockSpec((rows, CB, D), lambda s: (0, blk(s), 0))
    outs = pl.pallas_call(
        body, grid=(nblk + 1,),
        in_specs=[pl.BlockSpec((t_ctx, D), lambda s: (cblk, 5)),
                  pl.BlockSpec((rows, CB, D), lambda s: (0, blk(s), 5)),
                  full((4, D)), full((1, D)),
                  pl.BlockSpec((None, H, DH, DH), lambda s: (d, 0, 0, 0)), full((2, D)),
                  pl.BlockSpec((None, H, DH, DH), lambda s: (d, 0, 0, 0)), full((2, D)), full((2, D))],
        out_specs=(colspec, colspec, colspec, full((t_ctx, D))),
        out_shape=(jax.ShapeDtypeStruct((rows, GRID_W, D), F32),) * 3 + (jax.ShapeDtypeStruct((t_ctx, D), F32),),
        scratch_shapes=[pltpu.VMEM((t_ctx, D), F32), pltpu.VMEM((t_ctx, D), F32), pltpu.VMEM((t_ctx, D), F32),
                        pltpu.VMEM((8, D), F32), pltpu.VMEM((rows, CB, D), F32), pltpu.VMEM((rows, CB, D), F32),
                        pltpu.VMEM((CB, D), F32)],
        name=f"rglru_fwd{d}", compiler_params=_cp(("arbitrary",)))(z, z3, cw, cb, wr, br, wi, bi, lam)
    return outs[0].reshape(t_lat, D), outs[1].reshape(t_lat, D), outs[2].reshape(t_lat, D), outs[3]


def _rglru_bwd(z, cw, cb, wr, br, wi, bi, lam, dh_lat, hp_lat, a_lat, hp_ctx, d, t_lat, t_ctx, prev=None):
    m = z.shape[0]
    rows = t_lat // GRID_W
    z3 = z.reshape(m // GRID_W, GRID_W, IN_COLS)
    nblk = GRID_W // CB
    fblk = _col_of(d, nblk)
    blk = lambda s: fblk(nblk - jnp.minimum(s, nblk - 1))
    cblk = t_lat // t_ctx
    rc = min(RCH, rows)
    last = prev is not None
    view3 = lambda v: v.reshape(rows, GRID_W, D)

    def body(*refs):
        (zc_ref, zx_ref, cw_ref, cb_ref, wr_ref, br_ref, wi_ref, bi_ref, lam_ref, dh_ref, hpx_ref, ax_ref,
         hpc_ref) = refs[:13]
        k = 13
        if last:
            pdx_ref, pdc_ref = refs[13:15]
            k = 15
        ox_ref, oc_ref, dwr_ref, dwi_ref, sum_ref, a_s, b_s, gcar, a3, b3, cin_s = refs[k:]
        s = pl.program_id(0)
        lam_d = lam_ref[d:d + 1, :]
        sp = _softplus_neg(lam_d)
        br_ = br_ref[d:d + 1, :]
        bi_ = bi_ref[d:d + 1, :]
        flat = lambda v: v.reshape(-1, D)

        @pl.when(s == 0)
        def _():
            gcar[...] = jnp.zeros_like(gcar)
            dwr_ref[...] = jnp.zeros_like(dwr_ref)
            dwi_ref[...] = jnp.zeros_like(dwi_ref)
            sum_ref[...] = jnp.zeros_like(sum_ref)


        def conv_sums(dxc, dxb, xm1, x0, xp1, xp2):
            sum_ref[3:4, :] += _colsum(flat(dxc))
            sum_ref[4:5, :] += _colsum(flat(dxb))
            sum_ref[8:9, :] += _colsum(flat(dxc * xm1))
            sum_ref[9:10, :] += _colsum(flat(dxc * x0))
            sum_ref[10:11, :] += _colsum(flat(dxc * xp1))
            sum_ref[11:12, :] += _colsum(flat(dxc * xp2))

        def gate_grads(g, hp, xc, rr, i, a, om):
            mult = jnp.sqrt(om)
            da = g * hp
            ixc = i * xc
            dmult = g * ixc
            dixc = g * mult
            di = dixc * xc
            dxc = dixc * i
            dlog_a = da * a - dmult * ((1.0 - om) / mult)
            dr = dlog_a * (-RG_C * sp)
            sum_ref[2:3, :] += _colsum(dlog_a * rr)
            drp = dr * rr * (1.0 - rr)
            dip = di * i * (1.0 - i)
            sum_ref[0:1, :] += _colsum(drp)
            sum_ref[1:2, :] += _colsum(dip)
            xcb = xc.astype(BF)
            drb = drp.astype(BF)
            dib = dip.astype(BF)
            parts = []
            for gi in range(H):
                gs = slice(gi * DH, (gi + 1) * DH)
                parts.append(_dot_nt(drb[:, gs], wr_ref[gi]) + _dot_nt(dib[:, gs], wi_ref[gi]))
                dwr_ref[gi] += _dot_tn(xcb[:, gs], drb[:, gs])
                dwi_ref[gi] += _dot_tn(xcb[:, gs], dib[:, gs])
            return dxc + jnp.concatenate(parts, axis=1)

        @pl.when(s < nblk)
        def _():
            def local(t, carry):
                c, q = carry
                r = t if d else (rows - 1 - t)
                a = ax_ref[r]
                c = a * (c + dh_ref[r])
                q = a * q
                b3[r] = c
                a3[r] = q
                return c, q

            c, q = lax.fori_loop(0, rows, local, (jnp.zeros((CB, D), F32), jnp.ones((CB, D), F32)))
            cin = gcar[0:1, :]
            for j in (range(CB) if d else range(CB - 1, -1, -1)):
                cin_s[j:j + 1, :] = cin
                cin = c[j:j + 1, :] + q[j:j + 1, :] * cin
            gcar[0:1, :] = cin
            c_in = cin_s[...]
            for r0 in (range(rows - rc, -1, -rc) if d else range(0, rows, rc)):
                if d:
                    lo = max(r0 - 1, 0)
                    cn = b3[lo:r0 + rc - 1] + a3[lo:r0 + rc - 1] * c_in
                    if r0 == 0:
                        cn = jnp.concatenate([c_in[None], cn], axis=0)
                else:
                    hi = min(r0 + rc + 1, rows)
                    cn = b3[r0 + 1:hi] + a3[r0 + 1:hi] * c_in
                    if hi == rows:
                        cn = jnp.concatenate([cn, c_in[None]], axis=0)
                g = flat(dh_ref[r0:r0 + rc] + cn)
                xc = flat(_conv_cols(zx_ref, cw_ref, cb_ref, r0, rc, rows))
                rr, i, a, om = _gates(xc, wr_ref, br_, wi_ref, bi_, sp)
                b3[r0:r0 + rc] = gate_grads(g, flat(hpx_ref[r0:r0 + rc]), xc, rr, i, a, om).reshape(rc, CB, D)
            if last:
                for r0 in range(0, rows, rc):
                    b3[r0:r0 + rc] = b3[r0:r0 + rc] + pdx_ref[r0:r0 + rc]
                for r0 in range(0, rows, rc):
                    w = _window(b3, r0 - 2, rc + 3, rows)
                    xw = _window(zx_ref, r0 - 1, rc + 3, rows)
                    dxc = w[2:rc + 2]
                    dxb = (w[3:rc + 3] * cw_ref[0:1, :] + dxc * cw_ref[1:2, :] + w[1:rc + 1] * cw_ref[2:3, :]
                           + w[0:rc] * cw_ref[3:4, :])
                    conv_sums(dxc, dxb, xw[0:rc], xw[1:rc + 1], xw[2:rc + 2], xw[3:rc + 3])
                    ox_ref[r0:r0 + rc] = dxb
            else:
                for r0 in range(0, rows, rc):
                    ox_ref[r0:r0 + rc] = b3[r0:r0 + rc]

        @pl.when(s == nblk)
        def _():
            r = t_ctx
            xb = zc_ref[...]
            xc = _conv_fwd(xb, cw_ref, cb_ref, r)
            rr, i, a, om = _gates(xc, wr_ref, br_, wi_ref, bi_, sp)
            a_s[...] = a
            b_s[...] = jnp.zeros((r, D), F32)
            c0 = gcar[0:1, :]
            _scan_rows(1 - d, r, a_s, b_s, b_s, c0)
            cs = b_s[...]
            row = lax.broadcasted_iota(jnp.int32, (r, D), 0)
            if d:
                g = jnp.where(row == 0, c0, pltpu.roll(cs, 1, 0))
            else:
                g = jnp.where(row == r - 1, c0, pltpu.roll(cs, r - 1, 0))
            dxc = gate_grads(g, hpc_ref[...], xc, rr, i, a, om)
            if last:
                dxc = dxc + pdc_ref[...]
                dxb = (_shift_rows(dxc, -1, r) * cw_ref[0:1, :] + dxc * cw_ref[1:2, :]
                       + _shift_rows(dxc, 1, r) * cw_ref[2:3, :] + _shift_rows(dxc, 2, r) * cw_ref[3:4, :])
                conv_sums(dxc, dxb, _shift_rows(xb, 1, r), xb, _shift_rows(xb, -1, r), _shift_rows(xb, -2, r))
                oc_ref[...] = dxb
            else:
                oc_ref[...] = dxc
            sum_ref[2:3, :] = sum_ref[2:3, :] * (RG_C * _sig(-lam_d))

    full = lambda shp: pl.BlockSpec(shp, lambda s: (0,) * len(shp))
    once = lambda shp: pl.BlockSpec(shp, lambda s: (0,) * len(shp), pipeline_mode=pl.Buffered(1))
    colspec = pl.BlockSpec((rows, CB, D), lambda s: (0, blk(s), 0))
    colonce = pl.BlockSpec((rows, CB, D), lambda s: (0, blk(s), 0), pipeline_mode=pl.Buffered(1))
    in_specs = [pl.BlockSpec((t_ctx, D), lambda s: (cblk, 5), pipeline_mode=pl.Buffered(1)),
                pl.BlockSpec((rows, CB, D), lambda s: (0, blk(s), 5), pipeline_mode=pl.Buffered(1)),
                full((4, D)), full((1, D)),
                pl.BlockSpec((None, H, DH, DH), lambda s: (d, 0, 0, 0)), full((2, D)),
                pl.BlockSpec((None, H, DH, DH), lambda s: (d, 0, 0, 0)), full((2, D)), full((2, D)),
                colspec, colspec, colonce, once((t_ctx, D))]
    args = [z, z3, cw, cb, wr, br, wi, bi, lam, view3(dh_lat), view3(hp_lat), view3(a_lat), hp_ctx]
    if last:
        in_specs += [colonce, once((t_ctx, D))]
        args += [view3(prev[0]), prev[1]]
    outs = pl.pallas_call(
        body, grid=(nblk + 1,), in_specs=in_specs,
        out_specs=(colspec, full((t_ctx, D)), full((H, DH, DH)), full((H, DH, DH)), full((16, D))),
        out_shape=(jax.ShapeDtypeStruct((rows, GRID_W, D), F32), jax.ShapeDtypeStruct((t_ctx, D), F32),
                   jax.ShapeDtypeStruct((H, DH, DH), F32), jax.ShapeDtypeStruct((H, DH, DH), F32),
                   jax.ShapeDtypeStruct((16, D), F32)),
        scratch_shapes=[pltpu.VMEM((t_ctx, D), F32), pltpu.VMEM((t_ctx, D), F32), pltpu.VMEM((8, D), F32),
                        pltpu.VMEM((rows, CB, D), F32), pltpu.VMEM((rows, CB, D), F32), pltpu.VMEM((CB, D), F32)],
        name=f"rglru_bwd{d}", compiler_params=_cp(("arbitrary",), VMEM_LIMIT_RGLRU_BWD))(*args)
    return (outs[0].reshape(t_lat, D), outs[1]) + tuple(outs[2:])


def _merge(o_f, o_b, h_f, h_b, z, x, tgt, mod, norm_g, ln_g, ln_b, p_a, p_b, w_out, t_lat):
    tm = 256
    nt = t_lat // tm

    def body(of_ref, ob_ref, hf_ref, hb_ref, z4_ref, z6_ref, z7_ref, z8_ref, x_ref, t_ref, mod_ref, ng_ref,
             lg_ref, lb_ref, pa_ref, pb_ref, wo_ref,
             do_ref, dh_ref, dz4a_ref, dz4b_ref, dz6a_ref, sh3_ref, gx_ref,
             y_ref, dout_ref, oa_ref, dpa_ref, obv_ref, dpb_ref, acc_ref):
        i = pl.program_id(0)
        lat = i < nt
        latf = lat.astype(F32)

        @pl.when(i == 0)
        def _():
            acc_ref[...] = jnp.zeros_like(acc_ref)

        def per_head(v):
            return jnp.concatenate(
                [jnp.broadcast_to(jnp.mean(v[:, h * DH:(h + 1) * DH], axis=-1, keepdims=True), (tm, DH))
                 for h in range(H)], axis=1)

        gt = mod_ref[0:1, 2 * D:3 * D]
        gfull = jnp.concatenate([ng_ref[...]] * H, axis=1)
        o = of_ref[...] + ob_ref[...]
        rinv = lax.rsqrt(per_head(o * o) + RMS_EPS)
        n = o * rinv
        na = n * gfull
        z4 = z4_ref[...]
        s4 = _sig(z4)
        silu4 = z4 * s4
        oa = na * silu4
        z6 = z6_ref[...]
        s6 = _sig(z6)
        silu6 = z6 * s6
        hsum = hf_ref[...] + hb_ref[...]
        obv = hsum * silu6
        pa = _dot(oa, pa_ref[...])
        pb = _dot(obv, pb_ref[...])
        s7 = _sig(z7_ref[...])
        s8 = _sig(z8_ref[...])
        y = s7 * pa + s8 * pb
        out = _dot(y, wo_ref[...])
        pre = ALPHA * x_ref[...] + gt * out
        mu = jnp.mean(pre, axis=-1, keepdims=True)
        xc = pre - mu
        rstd = lax.rsqrt(jnp.mean(xc * xc, axis=-1, keepdims=True) + LN_EPS)
        xhat = xc * rstd
        lg = lg_ref[...]
        diff = xhat * lg + lb_ref[...] - t_ref[...]
        acc_ref[8:9, :] += _colsum(diff * diff) * (0.5 / D * latf)
        dxn = diff * (1.0 / D)
        acc_ref[1:2, :] += _colsum(dxn * xhat) * latf
        acc_ref[2:3, :] += _colsum(dxn) * latf
        dxhat = dxn * lg
        dpre = rstd * (dxhat - jnp.mean(dxhat, axis=-1, keepdims=True)
                       - xhat * jnp.mean(dxhat * xhat, axis=-1, keepdims=True))
        gx_ref[...] = ALPHA * dpre
        acc_ref[0:1, :] += _colsum(dpre * out) * latf
        dout = dpre * gt
        dy = _dot_nt(dout, wo_ref[...])
        dpa = dy * s7
        dpb = dy * s8
        dz7 = dy * pa * (s7 * (1.0 - s7))
        dz8 = dy * pb * (s8 * (1.0 - s8))
        doa = _dot_nt(dpa, pa_ref[...])
        dob = _dot_nt(dpb, pb_ref[...])
        dh_ref[...] = dob * silu6
        dz6 = dob * hsum * (s6 * (1.0 + z6 * (1.0 - s6)))
        dna = doa * silu4
        dz4 = doa * na * (s4 * (1.0 + z4 * (1.0 - s4)))
        dng = _colsum(dna * n)
        acc_ref[7:8, 0:DH] += sum(dng[:, h * DH:(h + 1) * DH] for h in range(H)) * latf
        dn = dna * gfull
        do_ref[...] = rinv * (dn - n * per_head(dn * n))
        acc_ref[3:4, :] += _colsum(dz4) * latf
        acc_ref[4:5, :] += _colsum(dz6) * latf
        acc_ref[5:6, :] += _colsum(dz7) * latf
        acc_ref[6:7, :] += _colsum(dz8) * latf
        dz4b, dz6b = (dz4 * latf).astype(BF), (dz6 * latf).astype(BF)
        dz4a_ref[...] = dz4b[:, :CUT4]
        dz4b_ref[...] = dz4b[:, CUT4:]
        dz6a_ref[...] = dz6b[:, :CUT6]
        sh3_ref[:, 0:D - CUT6] = dz6b[:, CUT6:]
        sh3_ref[:, D - CUT6:2 * D - CUT6] = (dz7 * latf).astype(BF)
        sh3_ref[:, 2 * D - CUT6:] = (dz8 * latf).astype(BF)
        y_ref[...] = y.astype(BF)
        dout_ref[...] = dout.astype(BF)
        oa_ref[...] = oa.astype(BF)
        dpa_ref[...] = dpa.astype(BF)
        obv_ref[...] = obv.astype(BF)
        dpb_ref[...] = dpb.astype(BF)

        @pl.when(i == nt - 1)
        def _():
            acc_ref[9:10, :] = jnp.broadcast_to(jnp.sum(acc_ref[8:9, :], axis=-1, keepdims=True), (1, D))

    m = z.shape[0]
    lrow = lambda i: jnp.minimum(i, nt - 1)
    row = pl.BlockSpec((tm, D), lambda i: (lrow(i), 0))
    allrow = lambda cols: pl.BlockSpec((tm, cols), lambda i: (i, 0))
    zs = lambda cb: pl.BlockSpec((tm, D), lambda i: (lrow(i), cb))
    full = lambda shp: pl.BlockSpec(shp, lambda i: (0,) * len(shp))
    wfull = pl.BlockSpec((D, D), lambda i: (0, 0), pipeline_mode=pl.Buffered(1))
    f32o = jax.ShapeDtypeStruct((t_lat, D), F32)
    bfo = jax.ShapeDtypeStruct((t_lat, D), BF)
    bfall = lambda cols: jax.ShapeDtypeStruct((m, cols), BF)
    return pl.pallas_call(
        body, grid=(m // tm,),
        in_specs=[row, row, row, row, zs(4), zs(6), zs(7), zs(8), row, row, full((16, 3 * D)), full((1, DH)),
                  full((1, D)), full((1, D)), wfull, wfull, wfull],
        out_specs=(row, row) + tuple(allrow(c) for c in (CUT4, D - CUT4, CUT6, SHC)) + (row,) * 7 + (full((16, D)),),
        out_shape=(f32o, f32o, bfall(CUT4), bfall(D - CUT4), bfall(CUT6), bfall(SHC), f32o, bfo, bfo, bfo, bfo, bfo, bfo,
                   jax.ShapeDtypeStruct((16, D), F32)),
        name="merge", compiler_params=_cp(("arbitrary",), VMEM_LIMIT_MERGE))(
            o_f, o_b, h_f, h_b, z, z, z, z, x, tgt, mod, norm_g, ln_g, ln_b, p_a, p_b, w_out)


def _wgrad(a, b, name):
    tm = 1024

    def body(a_ref, b_ref, o_ref):
        @pl.when(pl.program_id(0) == 0)
        def _():
            o_ref[...] = jnp.zeros_like(o_ref)
        o_ref[...] += _dot_tn(a_ref[...], b_ref[...])

    row = pl.BlockSpec((tm, D), lambda i: (i, 0))
    return pl.pallas_call(body, grid=(a.shape[0] // tm,), in_specs=[row, row],
                          out_specs=pl.BlockSpec((D, D), lambda i: (0, 0)),
                          out_shape=jax.ShapeDtypeStruct((D, D), F32), name=name,
                          compiler_params=_cp(("arbitrary",)))(a, b)


def _pack_shard2(dz4b, dz5_lat, dz5_ctx, dz6a):
    m, t_lat, t_ctx = dz4b.shape[0], dz5_lat.shape[0], dz5_ctx.shape[0]
    tm = t_ctx
    nlt = t_lat // tm
    w4 = D - CUT4

    def body(a_ref, bl_ref, bc_ref, c_ref, o_ref):
        i = pl.program_id(0)
        o_ref[:, 0:w4] = a_ref[...]
        o_ref[:, w4:w4 + D] = jnp.where(i < nlt, bl_ref[...], bc_ref[...]).astype(BF)
        o_ref[:, w4 + D:] = c_ref[...]

    return pl.pallas_call(
        body, grid=(m // tm,),
        in_specs=[pl.BlockSpec((tm, w4), lambda i: (i, 0)),
                  pl.BlockSpec((tm, D), lambda i: (jnp.minimum(i, nlt - 1), 0)),
                  pl.BlockSpec((tm, D), lambda i: (0, 0)),
                  pl.BlockSpec((tm, CUT6), lambda i: (i, 0))],
        out_specs=pl.BlockSpec((tm, SHC), lambda i: (i, 0)),
        out_shape=jax.ShapeDtypeStruct((m, SHC), BF), name="pack_shard2",
        compiler_params=_cp(("arbitrary",)))(dz4b, dz5_lat, dz5_ctx, dz6a)


def _wgrad_in(u_all, dz_shards):
    m = u_all.shape[1]
    assert m % (6 * 128) == 0
    tm = m // 6
    out = None
    for k, dz_k in enumerate(dz_shards):
        def body(u_ref, dz_ref, *rest):
            o_ref = rest[-1]

            @pl.when(pl.program_id(0) == 0)
            def _():
                o_ref[...] = jnp.zeros_like(o_ref)
            o_ref[0] += _dot(u_ref[...], dz_ref[...])

        out = pl.pallas_call(
            body, grid=(m // tm,),
            in_specs=[pl.BlockSpec((D, tm), lambda i: (0, i)), pl.BlockSpec((tm, SHC), lambda i: (i, 0))]
            + ([] if out is None else [_ANY]),
            out_specs=pl.BlockSpec((1, D, SHC), lambda i, k=k: (k, 0, 0)),
            out_shape=jax.ShapeDtypeStruct((NSH, D, SHC), F32),
            input_output_aliases={} if out is None else {2: 0},
            name=f"wgrad_in{k}", compiler_params=_cp(("arbitrary",)))(u_all, dz_k, *(() if out is None else (out,)))
    return out


def _du(dz_shards, w_in_g, x, ctx, mod, gxres, sums=()):
    tm = 256
    n_lat_tiles, nct = x.shape[0] // tm, ctx.shape[0] // tm
    nt = n_lat_tiles + nct
    ns = len(sums)
    rblk = lambda i: jnp.where(i < nct, n_lat_tiles + i, i - nct)
    lblk = lambda i: jnp.maximum(i - nct, 0)

    def body(*refs):
        dz_refs, refs = refs[:NSH], refs[NSH:]
        w_ref, x_ref, c_ref, mod_ref, gr_ref = refs[:5]
        sum_refs = refs[5:5 + ns]
        gx_ref, dm_ref = refs[5 + ns:7 + ns]
        got_refs = refs[7 + ns:7 + 2 * ns]
        sems = refs[7 + 2 * ns:]
        i = pl.program_id(0)
        is_lat = i >= nct

        @pl.when(i == 0)
        def _():
            dm_ref[...] = jnp.zeros_like(dm_ref)
            if ns:
                for cp in _rs_chip_copies(sum_refs, got_refs, *sems):
                    cp.start()

        du = _dot_nt(dz_refs[0][...], w_ref[0])
        for n in range(1, NSH):
            du = du + _dot_nt(dz_refs[n][...], w_ref[n])
        sc = jnp.where(is_lat, mod_ref[0:1, D:2 * D], mod_ref[1:2, D:2 * D])
        dsh = _colsum(du)
        dsc = _colsum(du * jnp.where(is_lat, x_ref[...], c_ref[...]))

        @pl.when(is_lat)
        def _():
            gx_ref[...] = du * (1.0 + sc) + gr_ref[...]
            dm_ref[0:1, 0:D] += dsh
            dm_ref[0:1, D:2 * D] += dsc

        @pl.when(jnp.logical_not(is_lat))
        def _():
            dm_ref[1:2, 0:D] += dsh
            dm_ref[1:2, D:2 * D] += dsc

        if ns:
            @pl.when(i == nt - 1)
            def _():
                for cp in _rs_chip_copies(sum_refs, got_refs, *sems):
                    cp.wait()

    outs = pl.pallas_call(
        body, grid=(nt,),
        in_specs=[pl.BlockSpec((tm, SHC), lambda i: (rblk(i), 0))] * NSH + [
                  pl.BlockSpec((NSH, D, SHC), lambda i: (0, 0, 0), pipeline_mode=pl.Buffered(1)),
                  pl.BlockSpec((tm, D), lambda i: (lblk(i), 0)),
                  pl.BlockSpec((tm, D), lambda i: (jnp.minimum(i, nct - 1), 0)),
                  pl.BlockSpec((16, 3 * D), lambda i: (0, 0)),
                  pl.BlockSpec((tm, D), lambda i: (lblk(i), 0))] + [_ANY] * ns,
        out_specs=[pl.BlockSpec((tm, D), lambda i: (lblk(i), 0)),
                   pl.BlockSpec((8, 2 * D), lambda i: (0, 0))] + [_ANY] * ns,
        out_shape=[jax.ShapeDtypeStruct((n_lat_tiles * tm, D), F32), jax.ShapeDtypeStruct((8, 2 * D), F32)]
        + [jax.ShapeDtypeStruct((3,) + g.shape[1:], g.dtype) for g in sums],
        scratch_shapes=[pltpu.SemaphoreType.DMA((3 * ns,)), pltpu.SemaphoreType.DMA((3 * ns,))] if ns else [],
        name="du", compiler_params=_cp(("arbitrary",)))(*dz_shards, w_in_g, x, ctx, mod, gxres, *sums)
    return outs[0], outs[1], list(outs[2:])


def _row_tile(rows, cols):
    t = 8
    while t * 2 * cols * 4 <= (1 << 20) and rows % (t * 2) == 0:
        t *= 2
    return t


def _adamw_update(w_ref, g_ref, m_ref, v_ref, d_ref, nm_ref, nv_ref):
    gg = g_ref[...]
    m2 = ADAM_B1 * m_ref[...] + (1.0 - ADAM_B1) * gg
    v2 = ADAM_B2 * v_ref[...] + (1.0 - ADAM_B2) * (gg * gg)
    m_hat = m2 / (1.0 - ADAM_B1 ** ADAM_STEP)
    v_hat = v2 / (1.0 - ADAM_B2 ** ADAM_STEP)
    d_ref[...] = -ADAM_LR * (m_hat / (jnp.sqrt(v_hat) + ADAM_EPS) + ADAM_WD * w_ref[...])
    nm_ref[...] = m2
    nv_ref[...] = v2


def _adamw_many(ws, gs, ms, vs):
    n = len(ws)

    def body(*refs):
        for j in range(n):
            _adamw_update(*refs[4 * j:4 * j + 4], *refs[4 * n + 3 * j:4 * n + 3 * j + 3])

    args = [a for quad in zip(ws, gs, ms, vs) for a in quad]
    outs = pl.pallas_call(body, out_shape=[jax.ShapeDtypeStruct(w.shape, F32) for w in ws for _ in range(3)],
                          name="adamw_small", compiler_params=_cp())(*args)
    return outs[0::3], outs[1::3], outs[2::3]


def _adamw(w, g, m, v, name):
    rows, cols = w.shape
    tr = _row_tile(rows, cols)

    def body(*refs):
        _adamw_update(*refs)

    spec = pl.BlockSpec((tr, cols), lambda i: (i, 0))
    o = jax.ShapeDtypeStruct((rows, cols), F32)
    return pl.pallas_call(body, grid=(rows // tr,), in_specs=[spec] * 4, out_specs=(spec,) * 3,
                          out_shape=(o, o, o), name=name, compiler_params=_cp(("arbitrary",)))(w, g, m, v)


_ANY = pl.BlockSpec(memory_space=pl.ANY)


def _place():
    return lax.axis_index("x"), lax.axis_index("y"), lax.axis_index("c")


def _ag_copies(ins, outs, send, recv, fsend, frecv, lsem):
    x, y, c = _place()
    me = 2 * x + y
    chips = ((1 - x, y), (x, 1 - y), (1 - x, 1 - y))
    local, chip, hand = [], [], []
    for j in range(len(ins)):
        hr = ins[j].shape[0] // 2
        half = pl.ds(pl.multiple_of(c * hr, 8), hr)
        local.append(pltpu.make_async_copy(ins[j], outs[j].at[me], lsem.at[j]))
        for k, (px, py) in enumerate(chips):
            chip.append(pltpu.make_async_remote_copy(
                src_ref=ins[j].at[half, :], dst_ref=outs[j].at[me, half, :], send_sem=send.at[3 * j + k],
                recv_sem=recv.at[3 * j + k], device_id=(px, py, c), device_id_type=MESH))
            got = outs[j].at[2 * px + py, half, :]
            hand.append(pltpu.make_async_remote_copy(
                src_ref=got, dst_ref=got, send_sem=fsend.at[3 * j + k], recv_sem=frecv.at[3 * j + k],
                device_id=(x, y, 1 - c), device_id_type=MESH))
    return local, chip, hand


def _ag_sems(n):
    return [pltpu.SemaphoreType.DMA((3 * n,))] * 4 + [pltpu.SemaphoreType.DMA((n,))]


def _ag_finish(local, chip, hand, done=0):
    for k in range(done, len(chip)):
        chip[k].wait_recv()
        hand[k].start()
    for cp in chip:
        cp.wait_send()
    for k in range(done):
        hand[k].wait_send()
    for k in range(done, len(chip)):
        hand[k].wait_send()
        hand[k].wait_recv()
    for cp in local:
        cp.wait()


def _mod_tp(c8, c_ctx, w_mod_sh, b_mod_sh):
    mc = w_mod_sh.shape[1]

    def body(c8_ref, cctx_ref, w_ref, b_ref, mod_ref, cc_ref, cc_s, part_s, send1, recv1, send3, recv3):
        x, y, c = _place()
        me = 4 * x + 2 * y + c
        ms = 2 * x + y
        copies = []
        for k in range(1, 8):
            peer = (x ^ ((k >> 2) & 1), y ^ ((k >> 1) & 1), c ^ (k & 1))
            cp = pltpu.make_async_remote_copy(src_ref=c8_ref, dst_ref=cc_s.at[me], send_sem=send1.at[k],
                                              recv_sem=recv1.at[k], device_id=peer, device_id_type=MESH)
            cp.start()
            copies.append(cp)
        cc_s[me] = c8_ref[...]
        for cp in copies:
            cp.wait()
        cc_ref[...] = jnp.zeros_like(cc_ref)
        for j in range(8):
            cc_ref[j:j + 1, :] = cc_s[j, 0:1, :]
        cc_ref[8:9, :] = cctx_ref[...]
        v = cc_ref[...]
        part_s[ms] = _dot(v * _sig(v), w_ref[...]) + b_ref[...]
        copies = []
        for k in range(1, 4):
            peer = (x ^ ((k >> 1) & 1), y ^ (k & 1), c)
            cp = pltpu.make_async_remote_copy(src_ref=part_s.at[ms], dst_ref=part_s.at[ms], send_sem=send3.at[k],
                                              recv_sem=recv3.at[k], device_id=peer, device_id_type=MESH)
            cp.start()
            copies.append(cp)
        for cp in copies:
            cp.wait()
        for s in range(NSH):
            mod_ref[:, s * mc:(s + 1) * mc] = part_s[s]

    vm = pl.BlockSpec(memory_space=pltpu.VMEM)
    return pl.pallas_call(
        body, in_specs=[vm] * 4, out_specs=(vm, vm),
        out_shape=(jax.ShapeDtypeStruct((16, NSH * mc), F32), jax.ShapeDtypeStruct((16, D), F32)),
        scratch_shapes=[pltpu.VMEM((8, 8, D), F32), pltpu.VMEM((NSH, 16, mc), F32),
                        pltpu.SemaphoreType.DMA((8,)), pltpu.SemaphoreType.DMA((8,)),
                        pltpu.SemaphoreType.DMA((4,)), pltpu.SemaphoreType.DMA((4,))],
        name="mod_tp", compiler_params=_cp())(c8, c_ctx, w_mod_sh, b_mod_sh)


def _inproj_ag(x, ctx, mod, w_in_sh, b_in, narrow_sh):
    n_lat = x.shape[0]
    m = n_lat + ctx.shape[0]
    assert m % (11 * 128) == 0
    tm = m // 11
    nt = m // tm
    nfull = n_lat // tm
    assert nfull == nt - 1
    tail = jnp.concatenate([x[nfull * tm:], ctx], axis=0)
    x_, y_ = lax.axis_index("x"), lax.axis_index("y")
    sids = jnp.stack([2 * x_ + y_, 2 * (1 - x_) + y_, 2 * x_ + 1 - y_, 2 * (1 - x_) + 1 - y_]).astype(jnp.int32)

    def body(sid_ref, x_ref, tail_ref, mod_ref, b_ref, wsh_ref, nsh_ref, z_ref, u_ref, wg_ref, ng_ref, w_s, u_s, *sems):
        n = pl.program_id(0)
        i = pl.program_id(1)
        rows = pl.ds(pl.multiple_of(i * tm, tm), tm)
        ag = ((wsh_ref, nsh_ref), (wg_ref, ng_ref)) + tuple(sems[:5])
        wsem = sems[5]

        def load(src):
            cp = pltpu.make_async_copy(src, w_s, wsem)
            cp.start()
            cp.wait()

        @pl.when((n == 0) & (i == 0))
        def _():
            local, chip, _ = _ag_copies(*ag)
            for cp in chip + local:
                cp.start()
            load(wsh_ref)

        for k in range(NSH - 1):
            @pl.when((n == k + 1) & (i == 0))
            def _():
                _, chip, hand = _ag_copies(*ag)
                chip[k].wait_recv()
                hand[k].start()
                hand[k].wait_recv()
                load(wg_ref.at[sid_ref[k + 1]])

        @pl.when(n == 0)
        def _():
            is_lat = (i * tm + lax.broadcasted_iota(jnp.int32, (tm, 1), 0)) < n_lat
            u = _modulate(jnp.where(i < nfull, x_ref[...], tail_ref[...]), mod_ref, is_lat)
            u_s[rows, :] = u.astype(BF)
            u_ref[...] = u.T.astype(BF)

        z_ref[...] = _dot(u_s[rows, :], w_s[...]) + b_ref[...]

        @pl.when((n == NSH - 1) & (i == nt - 1))
        def _():
            _ag_finish(*_ag_copies(*ag), done=NSH - 1)

    first = lambda n, i: jnp.where(n == 0, i, nt - 1)
    outs = pl.pallas_call(
        body, grid_spec=pltpu.PrefetchScalarGridSpec(
            num_scalar_prefetch=1, grid=(NSH, nt),
            in_specs=[pl.BlockSpec((tm, D), lambda n, i, sid: (jnp.minimum(first(n, i), nfull - 1), 0)),
                      pl.BlockSpec((tm, D), lambda n, i, sid: (0, 0)),
                      pl.BlockSpec((16, 3 * D), lambda n, i, sid: (0, 0)),
                      pl.BlockSpec((1, SHC), lambda n, i, sid: (0, sid[n])), _ANY, _ANY],
            out_specs=[pl.BlockSpec((tm, SHC), lambda n, i, sid: (i, sid[n])),
                       pl.BlockSpec((D, tm), lambda n, i, sid: (0, first(n, i))), _ANY, _ANY],
            scratch_shapes=[pltpu.VMEM((D, SHC), BF), pltpu.VMEM((m, D), BF)] + _ag_sems(2)
            + [pltpu.SemaphoreType.DMA]),
        out_shape=[jax.ShapeDtypeStruct((m, IN_COLS), F32), jax.ShapeDtypeStruct((D, m), BF),
                   jax.ShapeDtypeStruct((NSH,) + w_in_sh.shape, BF),
                   jax.ShapeDtypeStruct((NSH,) + narrow_sh.shape, narrow_sh.dtype)],
        name="inproj_ag", compiler_params=_cp(("arbitrary", "arbitrary")))(sids, x, tail, mod, b_in, w_in_sh, narrow_sh)
    return outs


def _rs_sibling(grads, name):
    n = len(grads)

    def body(*refs):
        ins, got = refs[:n], refs[n:2 * n]
        send, recv = refs[2 * n:]
        x, y, c = _place()
        copies = []
        for j in range(n):
            hr = ins[j].shape[1] // 2
            for s in range(NSH):
                give = ins[j].at[s, pl.ds(pl.multiple_of((1 - c) * hr, 8), hr), :]
                cp = pltpu.make_async_remote_copy(src_ref=give, dst_ref=got[j].at[s], send_sem=send.at[NSH * j + s],
                                                  recv_sem=recv.at[NSH * j + s], device_id=(x, y, 1 - c),
                                                  device_id_type=MESH)
                cp.start()
                copies.append(cp)
        for cp in copies:
            cp.wait()

    half = [jax.ShapeDtypeStruct((NSH, g.shape[1] // 2, g.shape[2]), F32) for g in grads]
    return pl.pallas_call(
        body, in_specs=[_ANY] * n, out_specs=[_ANY] * n, out_shape=half,
        scratch_shapes=[pltpu.SemaphoreType.DMA((NSH * n,)), pltpu.SemaphoreType.DMA((NSH * n,))],
        name=name)(*grads)


def _core_vec():
    return lax.axis_index("c").astype(jnp.int32).reshape(1)


def _rs_add1(g, got, name):
    _, r, cols = g.shape
    hr = r // 2
    tr = _row_tile(hr, cols)
    nb = hr // tr

    def body(c_ref, g_ref, got_ref, o_ref):
        o_ref[...] = (g_ref[...] + got_ref[...]).astype(BF)

    spec = pl.BlockSpec((1, tr, cols), lambda s, i, c_ref: (s, i, 0))
    return pl.pallas_call(
        body, grid_spec=pltpu.PrefetchScalarGridSpec(
            num_scalar_prefetch=1, grid=(NSH, nb),
            in_specs=[pl.BlockSpec((1, tr, cols), lambda s, i, c_ref: (s, c_ref[0] * nb + i, 0)), spec],
            out_specs=spec),
        out_shape=jax.ShapeDtypeStruct((NSH, hr, cols), BF), name=name,
        compiler_params=_cp(("arbitrary", "arbitrary")))(_core_vec(), g, got)


def _rs_add2(sums, got, name):
    _, hr, cols = sums.shape
    tr = _row_tile(hr, cols)
    nb = hr // tr
    place = jnp.stack([2 * lax.axis_index("x") + lax.axis_index("y"), lax.axis_index("c")]).astype(jnp.int32)

    def body(p_ref, s_ref, got_ref, o_ref):
        f = lambda v: v.astype(F32)
        o_ref[...] = f(s_ref[0]) + f(got_ref[0]) + f(got_ref[1]) + f(got_ref[2])

    return pl.pallas_call(
        body, grid_spec=pltpu.PrefetchScalarGridSpec(
            num_scalar_prefetch=1, grid=(nb,),
            in_specs=[pl.BlockSpec((1, tr, cols), lambda i, p_ref: (p_ref[0], i, 0)),
                      pl.BlockSpec((3, tr, cols), lambda i, p_ref: (0, i, 0))],
            out_specs=pl.BlockSpec((tr, cols), lambda i, p_ref: (p_ref[1] * nb + i, 0))),
        out_shape=jax.ShapeDtypeStruct((2 * hr, cols), F32), name=name,
        compiler_params=_cp(("arbitrary",)))(place, sums, got)


def _rs_chip_copies(ins, got, send, recv):
    x, y, c = _place()
    peers = ((1 - x, y), (x, 1 - y), (1 - x, 1 - y))
    return [pltpu.make_async_remote_copy(src_ref=ins[j].at[2 * px + py], dst_ref=got[j].at[k],
                                         send_sem=send.at[3 * j + k], recv_sem=recv.at[3 * j + k],
                                         device_id=(px, py, c), device_id_type=MESH)
            for j in range(len(ins)) for k, (px, py) in enumerate(peers)]


def _ag_sibling(fulls):
    n = len(fulls)
    nck = 4

    def body(*refs):
        outs = refs[n:2 * n]
        send, recv = refs[2 * n:]
        x, y, c = _place()
        copies = []
        for j in range(n):
            qr = outs[j].shape[0] // (2 * nck)
            for k in range(nck):
                rows = outs[j].at[pl.ds(pl.multiple_of((c * nck + k) * qr, 8), qr), :]
                cp = pltpu.make_async_remote_copy(src_ref=rows, dst_ref=rows, send_sem=send.at[nck * j + k],
                                                  recv_sem=recv.at[nck * j + k], device_id=(x, y, 1 - c),
                                                  device_id_type=MESH)
                cp.start()
                copies.append(cp)
        for cp in copies:
            cp.wait()

    return pl.pallas_call(
        body, in_specs=[_ANY] * n, out_specs=[_ANY] * n,
        out_shape=[jax.ShapeDtypeStruct(f.shape, F32) for f in fulls],
        input_output_aliases={j: j for j in range(n)},
        scratch_shapes=[pltpu.SemaphoreType.DMA((nck * n,)), pltpu.SemaphoreType.DMA((nck * n,))],
        name="ag_sibling")(*fulls)


def _allreduce_small(buf):
    rows = buf.shape[0]
    pr = rows // 8

    def body(in_ref, out_ref, part, stage_a, stage_b, *sems):
        (sa, ra, sb, rb, sc, rc, sd, rd) = sems
        x, y, c = _place()
        s = 2 * x + y
        sib = (x, y, 1 - c)
        chips = ((1 - x, y), (x, 1 - y), (1 - x, 1 - y))

        def piece(ref, chip, core):
            return ref.at[pl.ds(pl.multiple_of((2 * chip + core) * pr, 8), pr), :]

        def run(copies):
            for cp in copies:
                cp.start()
            for cp in copies:
                cp.wait()

        run([pltpu.make_async_remote_copy(src_ref=piece(in_ref, j, 1 - c), dst_ref=stage_a.at[j], send_sem=sa.at[j],
                                          recv_sem=ra.at[j], device_id=sib, device_id_type=MESH) for j in range(NSH)])
        for j in range(NSH):
            part[j] = piece(in_ref, j, c)[...] + stage_a[j]
        run([pltpu.make_async_remote_copy(src_ref=part.at[2 * px + py], dst_ref=stage_b.at[k], send_sem=sb.at[k],
                                          recv_sem=rb.at[k], device_id=(px, py, c), device_id_type=MESH)
             for k, (px, py) in enumerate(chips)])
        piece(out_ref, s, c)[...] = part[s] + stage_b[0] + stage_b[1] + stage_b[2]
        run([pltpu.make_async_remote_copy(src_ref=piece(out_ref, s, c), dst_ref=piece(out_ref, s, c),
                                          send_sem=sc.at[k], recv_sem=rc.at[k], device_id=(px, py, c),
                                          device_id_type=MESH) for k, (px, py) in enumerate(chips)])
        run([pltpu.make_async_remote_copy(src_ref=piece(out_ref, j, c), dst_ref=piece(out_ref, j, c), send_sem=sd.at[j],
                                          recv_sem=rd.at[j], device_id=sib, device_id_type=MESH) for j in range(NSH)])

    vm = pl.BlockSpec(memory_space=pltpu.VMEM)
    return pl.pallas_call(
        body, in_specs=[vm], out_specs=vm, out_shape=jax.ShapeDtypeStruct((rows, D), F32),
        scratch_shapes=[pltpu.VMEM((NSH, pr, D), F32), pltpu.VMEM((NSH, pr, D), F32), pltpu.VMEM((3, pr, D), F32)]
        + [pltpu.SemaphoreType.DMA((NSH,))] * 8,
        name="allreduce_small", compiler_params=_cp())(buf)


def _pad_rows(a, mult):
    pad = (-a.shape[0]) % mult
    return jnp.concatenate([a, jnp.zeros((pad, a.shape[1]), a.dtype)]) if pad else a


def _local_step(x, c, ctx, c_ctx, tgt, me, shard, sh, b_mod, b_in, norm_g, cb, wr, wi, ln_g, ln_b):
    t_lat, t_ctx = x.shape[0], ctx.shape[0]
    nlt = t_lat // 256
    nlb, ncb = t_lat // RB, t_ctx // RB
    mc = 3 * D // NSH
    mod_all, cc_all = _mod_tp(jnp.zeros((8, D), F32).at[0].set(c), c_ctx.reshape(1, D), sh["w_mod"],
                              lax.dynamic_slice_in_dim(b_mod, shard * mc, mc, axis=1))
    mod = jnp.zeros((16, 3 * D), F32).at[0].set(mod_all[me]).at[1].set(mod_all[8])
    cc = jnp.zeros((16, D), F32).at[0].set(c).at[1].set(c_ctx)
    z, u_all, w_in_g, nar = _inproj_ag(x, ctx, mod, sh["w_in"], b_in, sh["narrow"])
    nar = jnp.transpose(nar, (1, 0, 2)).reshape(-1, D)
    lbl, cw, br, bi, lam = nar[0:4].reshape(2, 2, D), nar[4:8], nar[8:10], nar[10:12], nar[12:14]
    o0, st0, (w_mod_g, p_a, p_b, w_out) = _gla_fwd(z, lbl, 0, nlb, ncb,
                                                   gather=[sh[k] for k in ("w_mod", "p_a", "p_b", "w_out")])
    p_a, p_b, w_out = p_a.reshape(D, D), p_b.reshape(D, D), w_out.reshape(D, D)
    o1, st1, _ = _gla_fwd(z, lbl, 1, nlb, ncb)
    h0, hp0, a0, hpc0 = _rglru_fwd(z, cw, cb, wr, br, wi, bi, lam, 0, t_lat, t_ctx)
    h1, hp1, a1, hpc1 = _rglru_fwd(z, cw, cb, wr, br, wi, bi, lam, 1, t_lat, t_ctx)
    (do, dh, dz4a, dz4b, dz6a, dz_sh3, gxres, y, dout, oa, dpa, obv, dpb, acc) = _merge(
        o0, o1, h0, h1, z, x, tgt, mod, norm_g, ln_g, ln_b, p_a, p_b, w_out, t_lat)
    gp_a = _wgrad(oa, dpa, "wgrad_pa")
    gp_b = _wgrad(obv, dpb, "wgrad_pb")
    gw_out = _wgrad(y, dout, "wgrad_wout")
    dxc_lat, dxc_ctx, dwr0, dwi0, sb0 = _rglru_bwd(z, cw, cb, wr, br, wi, bi, lam, dh, hp0, a0, hpc0, 0, t_lat, t_ctx)
    dz5_lat, dz5_ctx, dwr1, dwi1, sb1 = _rglru_bwd(z, cw, cb, wr, br, wi, bi, lam, dh, hp1, a1, hpc1, 1, t_lat, t_ctx,
                                                   prev=(dxc_lat, dxc_ctx))
    dq0, dv0, dz1, sa0 = _gla_bwd(z, lbl, do, st0, 0, nlb, ncb)
    dz_sh0, dz_sh1, sa1 = _gla_bwd(z, lbl, do, st1, 1, nlb, ncb, prev=(dq0, dv0, dz1, dz4a))
    dz = (dz_sh0, dz_sh1, _pack_shard2(dz4b, dz5_lat, dz5_ctx, dz6a), dz_sh3)
    grads = [_wgrad_in(u_all, dz)] + [g.reshape(NSH, D // NSH, D) for g in (gp_a, gp_b, gw_out)]
    sums = [_rs_add1(g, b, f"rs_add1_{j}") for j, (g, b) in enumerate(zip(grads, _rs_sibling(grads, "rs_sibling")))]
    gx, dm, got = _du(dz, w_in_g, x, ctx, mod, gxres, sums)
    fulls = [_rs_add2(a, b, f"rs_add2_{j}") for j, (a, b) in enumerate(zip(sums, got))]
    big = dict(zip(_RS, _ag_sibling(fulls)))
    dmod = jnp.zeros((16, 3 * D), F32).at[0:2, 0:2 * D].set(dm[0:2]).at[0, 2 * D:].set(acc[0])
    dcc = _mod_bwd(cc, dmod, w_mod_g)
    small = dict(
        c_ctx=dcc[1:2], b_mod=(dmod[0] + dmod[1]).reshape(3, D),
        b_in=jnp.stack([sa1[2], sa0[0], sa1[0], sa1[3], acc[3], sb1[4], acc[4], acc[5], acc[6]]),
        lb_logits=jnp.stack([sa0[1], sa1[1], -sa0[1], -sa1[1]]),
        norm_a_g=acc[7:8], conv_w=sb1[8:12], conv_b=sb1[3:4],
        w_r=jnp.stack([dwr0, dwr1]).reshape(-1, D), w_i=jnp.stack([dwi0, dwi1]).reshape(-1, D),
        b_r=jnp.stack([sb0[0], sb1[0]]), b_i=jnp.stack([sb0[1], sb1[1]]), lam=jnp.stack([sb0[2], sb1[2]]),
        ln_g=acc[1:2], ln_b=acc[2:3])
    return acc[9, 0], gx, big, small, dmod, cc_all


_RS =("w_in", "p_a", "p_b", "w_out")
_SMALL =("c_ctx", "b_mod", "b_in", "lb_logits", "norm_a_g", "conv_w", "conv_b", "w_r", "w_i", "b_r", "b_i", "lam",
          "ln_g", "ln_b")
_BIG = ("w_mod", "w_in", "p_a", "p_b", "w_out")
_COL_SHARDED = ("lb_logits", "conv_w", "b_r", "b_i", "lam")
_WEIGHTS = ("c_ctx", "w_mod", "b_mod", "w_in", "b_in", "lb_logits", "norm_a_g", "conv_w", "conv_b", "w_r", "b_r", "w_i",
            "b_i", "lam", "p_a", "p_b", "w_out", "ln_g", "ln_b")


def kernel(x, c, ctx, c_ctx, w_mod, b_mod, w_in, b_in, lb_logits, norm_a_g, conv_w, conv_b, w_r, b_r, w_i, b_i, lam, p_a, p_b, w_out, ln_g, ln_b, loss_target, m_c_ctx, m_w_mod, m_b_mod, m_w_in, m_b_in, m_lb_logits, m_norm_a_g, m_conv_w, m_conv_b, m_w_r, m_b_r, m_w_i, m_b_i, m_lam, m_p_a, m_p_b, m_w_out, m_ln_g, m_ln_b, v_c_ctx, v_w_mod, v_b_mod, v_w_in, v_b_in, v_lb_logits, v_norm_a_g, v_conv_w, v_conv_b, v_w_r, v_b_r, v_w_i, v_b_i, v_lam, v_p_a, v_p_b, v_w_out, v_ln_g, v_ln_b):
    w = dict(c_ctx=c_ctx, w_mod=w_mod, b_mod=b_mod, w_in=w_in, b_in=b_in, lb_logits=lb_logits, norm_a_g=norm_a_g,
             conv_w=conv_w, conv_b=conv_b, w_r=w_r, b_r=b_r, w_i=w_i, b_i=b_i, lam=lam, p_a=p_a, p_b=p_b, w_out=w_out,
             ln_g=ln_g, ln_b=ln_b)
    m = dict(c_ctx=m_c_ctx, w_mod=m_w_mod, b_mod=m_b_mod, w_in=m_w_in, b_in=m_b_in, lb_logits=m_lb_logits,
             norm_a_g=m_norm_a_g, conv_w=m_conv_w, conv_b=m_conv_b, w_r=m_w_r, b_r=m_b_r, w_i=m_w_i, b_i=m_b_i,
             lam=m_lam, p_a=m_p_a, p_b=m_p_b, w_out=m_w_out, ln_g=m_ln_g, ln_b=m_ln_b)
    v = dict(c_ctx=v_c_ctx, w_mod=v_w_mod, b_mod=v_b_mod, w_in=v_w_in, b_in=v_b_in, lb_logits=v_lb_logits,
             norm_a_g=v_norm_a_g, conv_w=v_conv_w, conv_b=v_conv_b, w_r=v_w_r, b_r=v_b_r, w_i=v_w_i, b_i=v_b_i,
             lam=v_lam, p_a=v_p_a, p_b=v_p_b, w_out=v_w_out, ln_g=v_ln_g, ln_b=v_ln_b)
    shard = 2 * lax.axis_index("x") + lax.axis_index("y")
    cs = D // NSH

    sh = {k: w[k][0].astype(BF) for k in _BIG}
    sh["narrow"] = _pad_rows(jnp.concatenate([lb_logits.reshape(4, cs), conv_w[0], b_r[0], b_i[0], lam[0]], axis=0), 8)
    me = 2 * shard + lax.axis_index("c")
    loss, gx, big, small, dmod, cc_all = _local_step(
        x[0], c[0], ctx[0], c_ctx, loss_target[0], me, shard, sh, b_mod, b_in, norm_a_g, conv_b, w_r[0], w_i[0],
        ln_g, ln_b)

    dmod_rows = jnp.zeros((16, 3 * D), F32).at[me].set(dmod[0]).at[8].set(dmod[1]).reshape(48, D)
    loss_rows = jnp.zeros((8, D), F32).at[0, 0].set(loss)
    sizes = [small[k].shape[0] for k in _SMALL]
    red = _allreduce_small(_pad_rows(jnp.concatenate([_pad_rows(small[k], 8) for k in _SMALL] + [dmod_rows, loss_rows],
                                                     axis=0), 64))
    grads = {}
    off = 0
    for k, n in zip(_SMALL, sizes):
        g = red[off:off + n]
        off += n + (-n) % 8
        if k == "norm_a_g":
            g = g[:, :DH]
        if k in _COL_SHARDED:
            g = lax.dynamic_slice_in_dim(g, shard * cs, cs, axis=1)
        grads[k] = g.reshape(w[k].shape)
    for k in _RS:
        grads[k] = big[k].reshape(w[k].shape)
    dmod_all = red[off:off + 48].reshape(16, 3 * D)
    loss = red[off + 48, 0]
    mc = 3 * D // NSH
    grads["w_mod"] = _wmod_grad(cc_all, lax.dynamic_slice_in_dim(dmod_all, shard * mc, mc, axis=1)).reshape(
        w["w_mod"].shape)

    delta, new_m, new_v = {}, {}, {}
    for k in _BIG:
        shp = w[k].shape
        two = lambda a: a.reshape(shp[-2], shp[-1])
        d_, m_, v_ = _adamw(two(w[k]), two(grads[k]), two(m[k]), two(v[k]), f"adamw_{k}")
        delta[k], new_m[k], new_v[k] = d_.reshape(shp), m_.reshape(shp), v_.reshape(shp)
    d_, m_, v_ = _adamw_many(*[[t[k] for k in _SMALL] for t in (w, grads, m, v)])
    delta.update(zip(_SMALL, d_))
    new_m.update(zip(_SMALL, m_))
    new_v.update(zip(_SMALL, v_))

    return (loss, gx[None], *[grads[k] for k in _WEIGHTS], *[delta[k] for k in _WEIGHTS],
            *[new_m[k] for k in _WEIGHTS], *[new_v[k] for k in _WEIGHTS])
```

```python
import jax
import jax.numpy as jnp
from jax import lax
from jax.experimental import pallas as pl
from jax.experimental.pallas import tpu as pltpu

F32 = jnp.float32
BF = jnp.bfloat16
MESH = pl.DeviceIdType.MESH

D = 1024
H = 8
DH = 128
CH = 64
RB = 256
NCK = RB // CH
GRID_W = 64
CB = 8
RCH = 16
IN_COLS = 9 * D
NSH = 4
SHC = IN_COLS // NSH
CUT2 = SHC - 2 * D
CUT4 = 2 * SHC - 4 * D
CUT6 = 3 * SHC - 6 * D
RG_C = 8.0
ALPHA = 2.0 ** 0.25
LN_EPS = 1e-5
RMS_EPS = 1e-6
Q_SCALE = DH ** -0.5
ADAM_LR, ADAM_B1, ADAM_B2, ADAM_EPS, ADAM_WD, ADAM_STEP = 0.001, 0.9, 0.999, 1e-08, 0.01, 10
VMEM_LIMIT = 56 * 1024 * 1024
VMEM_LIMIT_MERGE = 60 * 1024 * 1024
VMEM_LIMIT_RGLRU_BWD = 63 * 1024 * 1024


def _cp(sem=None, vmem=VMEM_LIMIT):
    return pltpu.CompilerParams(dimension_semantics=sem, vmem_limit_bytes=vmem)


def _sig(x):
    return 0.5 * jnp.tanh(0.5 * x) + 0.5


def _sig_tail(x):
    return 1.0 / (1.0 + jnp.exp(-x))


def _dot(a, b):
    return jnp.dot(a.astype(BF), b.astype(BF), preferred_element_type=F32)


def _dot_nt(a, b):
    return lax.dot_general(a.astype(BF), b.astype(BF), (((1,), (1,)), ((), ())), preferred_element_type=F32)


def _dot_tn(a, b):
    return lax.dot_general(a.astype(BF), b.astype(BF), (((0,), (0,)), ((), ())), preferred_element_type=F32)


def _colsum(v):
    return jnp.sum(v, axis=0, keepdims=True)


def _mod_bwd(cc, dmod, w_mod_g):
    def body(cc_ref, dm_ref, w_ref, dcc_ref):
        v = cc_ref[...]
        sg = _sig(v)
        ds = jnp.zeros((16, D), F32)
        for k in range(NSH):
            ds = ds + _dot_nt(dm_ref[:, k * 768:(k + 1) * 768], w_ref[k])
        dcc_ref[...] = ds * (sg * (1.0 + v * (1.0 - sg)))
    return pl.pallas_call(body, out_shape=jax.ShapeDtypeStruct((16, D), F32),
                          name="mod_bwd", compiler_params=_cp())(cc, dmod, w_mod_g)


def _wmod_grad(cc, dmod_cols):
    def body(cc_ref, dm_ref, dw_ref):
        v = cc_ref[...]
        dw_ref[...] = _dot_tn(v * _sig(v), dm_ref[...])
    return pl.pallas_call(body, out_shape=jax.ShapeDtypeStruct((D, dmod_cols.shape[1]), F32),
                          name="wmod_grad", compiler_params=_cp())(cc, dmod_cols)


def _modulate(xv, mod_ref, is_lat):
    sh = jnp.where(is_lat, mod_ref[0:1, 0:D], mod_ref[1:2, 0:D])
    sc = jnp.where(is_lat, mod_ref[0:1, D:2 * D], mod_ref[1:2, D:2 * D])
    return xv * (1.0 + sc) + sh


def _gla_mask(d, n):
    row = lax.broadcasted_iota(jnp.int32, (n, n), 0)
    col = lax.broadcasted_iota(jnp.int32, (n, n), 1)
    same = (row // CH) == (col // CH)
    return same & ((row <= col) if d else (row >= col))


def _chunk_cumsum(v, rev):
    n = v.shape[0]
    pos = lax.broadcasted_iota(jnp.int32, v.shape, 0) & (CH - 1)
    s = 1
    while s < CH:
        if rev:
            v = v + jnp.where(pos < CH - s, pltpu.roll(v, n - s, 0), 0.0)
        else:
            v = v + jnp.where(pos >= s, pltpu.roll(v, s, 0), 0.0)
        s *= 2
    return v


def _chunk_rows(c):
    return slice(c * CH, (c + 1) * CH)


def _gla_features(zq, zf, lb, d):
    sq = _sig(zq)
    q = zq * sq * Q_SCALE
    sf = _sig(zf)
    f = lb + (1.0 - lb) * sf
    k = 1.0 - f
    g = _chunk_cumsum(jnp.log(f), d)
    last = 0 if d else CH - 1
    gls = [g[c * CH + last:c * CH + last + 1, :] for c in range(NCK)]
    glb = jnp.concatenate([jnp.broadcast_to(gl, (CH, D)) for gl in gls], axis=0)
    eg, eig, eeg = jnp.exp(g), jnp.exp(-g), jnp.exp(glb - g)
    decs = [jnp.exp(gl) for gl in gls]
    return sq, sf, f, q * eg, k * eig, k * eeg, eg, eig, eeg, decs


def _lower_bound(lbl_ref, d):
    return _sig_tail(lbl_ref[0, d:d + 1, :] - lbl_ref[1, d:d + 1, :])


def _gla_rb(d, nlb, ncb):
    nrb = nlb + ncb
    if d:
        return lambda s: nrb - 1 - s
    return lambda s: jnp.where(s < ncb, nlb + s, s - ncb)


def _gla_fwd(z, lbl, d, nlb, ncb, gather=()):
    m = z.shape[0]
    nrb = nlb + ncb
    rb = _gla_rb(d, nlb, ncb)
    ng = len(gather)

    def body(*refs):
        q_ref, f_ref, v_ref, lbl_ref = refs[:4]
        ag_in = refs[4:4 + ng]
        o_ref, st_ref = refs[4 + ng:6 + ng]
        ag_out = refs[6 + ng:6 + 2 * ng]
        S = refs[6 + 2 * ng]
        ag_sems = refs[7 + 2 * ng:]
        s = pl.program_id(0)

        @pl.when(s == 0)
        def _():
            S[...] = jnp.zeros_like(S)
            if ng:
                local, chip, _ = _ag_copies(ag_in, ag_out, *ag_sems)
                for cp in chip + local:
                    cp.start()

        lb = _lower_bound(lbl_ref, d)
        mb = _gla_mask(d, CH)
        _, _, _, qd, ki, ke, _, _, _, decs = _gla_features(q_ref[...], f_ref[...], lb, d)
        qd, ki, ke, v = qd.astype(BF), ki.astype(BF), ke.astype(BF), v_ref[...].astype(BF)
        order = range(NCK - 1, -1, -1) if d else range(NCK)
        for h in range(H):
            hs = slice(h * DH, (h + 1) * DH)
            intra, upd = {}, {}
            for c in range(NCK):
                rs = _chunk_rows(c)
                a = jnp.where(mb, _dot_nt(qd[rs, hs], ki[rs, hs]), 0.0)
                intra[c] = _dot(a, v[rs, hs])
                upd[c] = _dot_tn(v[rs, hs], ke[rs, hs])
            st = S[h]
            for c in order:
                rs = _chunk_rows(c)
                st_ref[c, h] = st
                o_ref[rs, hs] = intra[c] + _dot_nt(qd[rs, hs], st)
                st = st * decs[c][:, hs] + upd[c]
            S[h] = st

        if ng:
            @pl.when(s == nrb - 1)
            def _():
                _ag_finish(*_ag_copies(ag_in, ag_out, *ag_sems))

    def zspec(cb):
        return pl.BlockSpec((RB, D), lambda s: (rb(s), cb))

    outs = pl.pallas_call(
        body, grid=(nrb,),
        in_specs=[zspec(0), zspec(1 + d), zspec(3), pl.BlockSpec((2, 2, D), lambda s: (0, 0, 0))] + [_ANY] * ng,
        out_specs=[pl.BlockSpec((RB, D), lambda s: (rb(s), 0)),
                   pl.BlockSpec((NCK, H, DH, DH), lambda s: (rb(s), 0, 0, 0))] + [_ANY] * ng,
        out_shape=[jax.ShapeDtypeStruct((m, D), F32), jax.ShapeDtypeStruct((m // CH, H, DH, DH), F32)]
        + [jax.ShapeDtypeStruct((NSH,) + g.shape, g.dtype) for g in gather],
        scratch_shapes=[pltpu.VMEM((H, DH, DH), F32)] + (_ag_sems(ng) if ng else []),
        name=f"gla_fwd{d}", compiler_params=_cp(("arbitrary",)))(z, z, z, lbl, *gather)
    return outs[0], outs[1], list(outs[2:])


def _gla_bwd(z, lbl, do_lat, states, d, nlb, ncb, prev=None):
    m = z.shape[0]
    nrb = nlb + ncb
    fwd_rb = _gla_rb(d, nlb, ncb)
    rb = lambda s: fwd_rb(nrb - 1 - s)
    last = prev is not None

    def body(*refs):
        if last:
            (q_ref, f_ref, v_ref, lbl_ref, do_ref, st_ref, pq_ref, pv_ref, dz1_ref, dz4a_ref, sh0_ref, sh1_ref, sum_ref,
             dS) = refs
        else:
            q_ref, f_ref, v_ref, lbl_ref, do_ref, st_ref, o0_ref, o1_ref, o2_ref, sum_ref, dS = refs
        s = pl.program_id(0)
        is_lat = rb(s) < nlb

        @pl.when(s == 0)
        def _():
            dS[...] = jnp.zeros_like(dS)
            sum_ref[...] = jnp.zeros_like(sum_ref)

        lb = _lower_bound(lbl_ref, d)
        mb = _gla_mask(d, RB)
        zq = q_ref[...]
        sq, sf, f, qd, ki, ke, eg, eig, eeg, decs = _gla_features(zq, f_ref[...], lb, d)
        qdb, kib, keb, vb = qd.astype(BF), ki.astype(BF), ke.astype(BF), v_ref[...].astype(BF)
        dob = jnp.where(is_lat, do_ref[...], 0.0).astype(BF)
        order = range(NCK) if d else range(NCK - 1, -1, -1)
        dqd_h, dki_h, dke_h, dv_h, ddec_h = [], [], [], [], []
        for h in range(H):
            hs = slice(h * DH, (h + 1) * DH)
            a = jnp.where(mb, _dot_nt(qdb[:, hs], kib[:, hs]), 0.0).astype(BF)
            da = jnp.where(mb, _dot_nt(dob[:, hs], vb[:, hs]), 0.0).astype(BF)
            dqd_i = _dot(da, kib[:, hs])
            dki_h.append(_dot_tn(da, qdb[:, hs]))
            dvi = _dot_tn(a, dob[:, hs])
            dqd, inc = {}, {}
            for c in range(NCK):
                rs = _chunk_rows(c)
                dqd[c] = dqd_i[rs, :] + _dot(dob[rs, hs], st_ref[c, h])
                inc[c] = _dot_tn(dob[rs, hs], qdb[rs, hs])
            dst = dS[h]
            dke, dv, ddec = {}, {}, {}
            for c in order:
                rs = _chunk_rows(c)
                dv[c] = dvi[rs, :] + _dot_nt(keb[rs, hs], dst)
                dke[c] = _dot(vb[rs, hs], dst)
                ddec[c] = _colsum(st_ref[c, h] * dst)
                dst = inc[c] + dst * decs[c][:, hs]
            dS[h] = dst
            cat = lambda t: jnp.concatenate([t[c] for c in range(NCK)], axis=0)
            dqd_h.append(cat(dqd))
            dke_h.append(cat(dke))
            dv_h.append(cat(dv))
            ddec_h.append([ddec[c] for c in range(NCK)])
        lanes = lambda parts: jnp.concatenate(parts, axis=1)
        dqd, dki, dke, dv = lanes(dqd_h), lanes(dki_h), lanes(dke_h), lanes(dv_h)
        dq = dqd * eg
        dk = dki * eig + dke * eeg
        dke_ke = dke * ke
        dg = dqd * qd - dki * ki - dke_ke
        dgl = [_colsum(dke_ke[_chunk_rows(c), :]) + lanes([ddec_h[h][c] for h in range(H)]) * decs[c]
               for c in range(NCK)]
        dglb = jnp.concatenate([jnp.broadcast_to(t, (CH, D)) for t in dgl], axis=0)
        df = (_chunk_cumsum(dg, 1 - d) + dglb) / f - dk
        dzf = df * (1.0 - lb) * (sf * (1.0 - sf))
        sum_ref[0:1, :] += _colsum(dzf)
        sum_ref[1:2, :] += _colsum(df * (1.0 - sf))
        if last:
            dz0 = (dq + pq_ref[...]) * (Q_SCALE * (sq * (1.0 + zq * (1.0 - sq))))
            dz3 = dv + pv_ref[...]
            sum_ref[2:3, :] += _colsum(dz0)
            sum_ref[3:4, :] += _colsum(dz3)
            dz2 = dzf.astype(BF)
            sh0_ref[:, 0:D] = dz0.astype(BF)
            sh0_ref[:, D:2 * D] = dz1_ref[...]
            sh0_ref[:, 2 * D:] = dz2[:, :CUT2]
            sh1_ref[:, 0:D - CUT2] = dz2[:, CUT2:]
            sh1_ref[:, D - CUT2:2 * D - CUT2] = dz3.astype(BF)
            sh1_ref[:, 2 * D - CUT2:] = dz4a_ref[...]
        else:
            o0_ref[...] = dq
            o1_ref[...] = dv
            o2_ref[...] = dzf.astype(BF)

        @pl.when(s == nrb - 1)
        def _():
            sum_ref[1:2, :] = sum_ref[1:2, :] * (lb * (1.0 - lb))

    def zspec(cb):
        return pl.BlockSpec((RB, D), lambda s: (rb(s), cb))

    rowspec = pl.BlockSpec((RB, D), lambda s: (rb(s), 0))
    in_specs = [zspec(0), zspec(1 + d), zspec(3), pl.BlockSpec((2, 2, D), lambda s: (0, 0, 0)),
                pl.BlockSpec((RB, D), lambda s: (jnp.minimum(rb(s), nlb - 1), 0)),
                pl.BlockSpec((NCK, H, DH, DH), lambda s: (rb(s), 0, 0, 0))]
    args = [z, z, z, lbl, do_lat, states]
    sumspec = pl.BlockSpec((8, D), lambda s: (0, 0))
    if last:
        in_specs += [rowspec, rowspec, rowspec, pl.BlockSpec((RB, CUT4), lambda s: (rb(s), 0))]
        args += list(prev)
        shspec = pl.BlockSpec((RB, SHC), lambda s: (rb(s), 0))
        out_specs = (shspec, shspec, sumspec)
        out_shape = (jax.ShapeDtypeStruct((m, SHC), BF), jax.ShapeDtypeStruct((m, SHC), BF))
    else:
        out_specs = (rowspec, rowspec, rowspec, sumspec)
        out_shape = (jax.ShapeDtypeStruct((m, D), F32), jax.ShapeDtypeStruct((m, D), F32),
                     jax.ShapeDtypeStruct((m, D), BF))
    return pl.pallas_call(
        body, grid=(nrb,), in_specs=in_specs, out_specs=out_specs,
        out_shape=out_shape + (jax.ShapeDtypeStruct((8, D), F32),),
        scratch_shapes=[pltpu.VMEM((H, DH, DH), F32)],
        name=f"gla_bwd{d}", compiler_params=_cp(("arbitrary",)))(*args)


def _shift_rows(v, k, r):
    row = lax.broadcasted_iota(jnp.int32, v.shape, 0)
    rolled = pltpu.roll(v, k % r, 0)
    return jnp.where((row >= k) & (row < r + k), rolled, 0.0)


def _conv_fwd(xb, cw_ref, cb_ref, r):
    return (cb_ref[...] + _shift_rows(xb, 1, r) * cw_ref[0:1, :] + xb * cw_ref[1:2, :]
            + _shift_rows(xb, -1, r) * cw_ref[2:3, :] + _shift_rows(xb, -2, r) * cw_ref[3:4, :])


def _window(ref, lo, n, rows):
    parts = []
    if lo < 0:
        parts.append(jnp.zeros((-lo,) + tuple(ref.shape[1:]), F32))
    parts.append(ref[max(lo, 0):min(lo + n, rows)])
    if lo + n > rows:
        parts.append(jnp.zeros((lo + n - rows,) + tuple(ref.shape[1:]), F32))
    return parts[0] if len(parts) == 1 else jnp.concatenate(parts, axis=0)


def _conv_cols(x_ref, cw_ref, cb_ref, r0, n, rows):
    w = _window(x_ref, r0 - 1, n + 3, rows)
    return (cb_ref[...] + w[0:n] * cw_ref[0:1, :] + w[1:n + 1] * cw_ref[1:2, :] + w[2:n + 2] * cw_ref[2:3, :]
            + w[3:n + 3] * cw_ref[3:4, :])


def _softplus_neg(lam):
    y = jnp.exp(-jnp.abs(lam))
    u = 1.0 + y
    tiny = u == 1.0
    l1p = jnp.where(tiny, y, jnp.log(u) * (y / jnp.where(tiny, 1.0, u - 1.0)))
    return jnp.maximum(-lam, 0.0) + l1p


def _gates(xc, wr_ref, br, wi_ref, bi, sp):
    xcb = xc.astype(BF)
    rs, is_ = [], []
    for g in range(H):
        gs = slice(g * DH, (g + 1) * DH)
        rs.append(jnp.dot(xcb[:, gs], wr_ref[g].astype(BF), preferred_element_type=F32))
        is_.append(jnp.dot(xcb[:, gs], wi_ref[g].astype(BF), preferred_element_type=F32))
    r = _sig_tail(jnp.concatenate(rs, axis=1) + br)
    i = _sig(jnp.concatenate(is_, axis=1) + bi)
    log_a = (-RG_C * r) * sp
    a = jnp.exp(log_a)
    t = jnp.tanh(log_a)
    om = (-2.0 * t) / (1.0 - t)
    return r, i, a, om


def _scan_rows(d, nrows, a_s, b_s, h_s, h0):
    nsl = nrows // 8

    def slab(j, h):
        jj = (nsl - 1 - j) if d else j
        r0 = pl.multiple_of(jj * 8, 8)
        for t in (range(7, -1, -1) if d else range(8)):
            h = a_s[pl.ds(r0 + t, 1), :] * h + b_s[pl.ds(r0 + t, 1), :]
            h_s[pl.ds(r0 + t, 1), :] = h
        return h

    return lax.fori_loop(0, nsl, slab, h0)


def _col_of(d, ncols):
    if d:
        return lambda s: ncols - jnp.maximum(s, 1)
    return lambda s: jnp.maximum(s, 1) - 1


def _rglru_fwd(z, cw, cb, wr, br, wi, bi, lam, d, t_lat, t_ctx):
    m = z.shape[0]
    rows = t_lat // GRID_W
    z3 = z.reshape(m // GRID_W, GRID_W, IN_COLS)
    nblk = GRID_W // CB
    blk = _col_of(d, nblk)
    cblk = t_lat // t_ctx
    rc = min(RCH, rows)

    def body(zc_ref, zx_ref, cw_ref, cb_ref, wr_ref, br_ref, wi_ref, bi_ref, lam_ref,
             hx_ref, hpx_ref, ax_ref, hpc_ref, a_s, b_s, h_s, hcar, a3, b3, cin_s):
        s = pl.program_id(0)
        sp = _softplus_neg(lam_ref[d:d + 1, :])
        br_ = br_ref[d:d + 1, :]
        bi_ = bi_ref[d:d + 1, :]

        @pl.when(s == 0)
        def _():
            xc = _conv_fwd(zc_ref[...], cw_ref, cb_ref, t_ctx)
            _, i, a, om = _gates(xc, wr_ref, br_, wi_ref, bi_, sp)
            a_s[...] = a
            b_s[...] = jnp.sqrt(om) * (i * xc)
            h0 = jnp.zeros((1, D), F32)
            hcar[0:1, :] = _scan_rows(d, t_ctx, a_s, b_s, h_s, h0)
            hs = h_s[...]
            row = lax.broadcasted_iota(jnp.int32, (t_ctx, D), 0)
            if d:
                hpc_ref[...] = jnp.where(row == t_ctx - 1, h0, pltpu.roll(hs, t_ctx - 1, 0))
            else:
                hpc_ref[...] = jnp.where(row == 0, h0, pltpu.roll(hs, 1, 0))

        @pl.when(s > 0)
        def _():
            for r0 in range(0, rows, rc):
                xc = _conv_cols(zx_ref, cw_ref, cb_ref, r0, rc, rows).reshape(rc * CB, D)
                _, i, a, om = _gates(xc, wr_ref, br_, wi_ref, bi_, sp)
                a3[r0:r0 + rc] = a.reshape(rc, CB, D)
                ax_ref[r0:r0 + rc] = a.reshape(rc, CB, D)
                b3[r0:r0 + rc] = (jnp.sqrt(om) * (i * xc)).reshape(rc, CB, D)

            def local(t, carry):
                hl, p = carry
                r = (rows - 1 - t) if d else t
                a = a3[r]
                hl = a * hl + b3[r]
                p = a * p
                b3[r] = hl
                a3[r] = p
                return hl, p

            hl, p = lax.fori_loop(0, rows, local, (jnp.zeros((CB, D), F32), jnp.ones((CB, D), F32)))
            cin = hcar[0:1, :]
            for j in (range(CB - 1, -1, -1) if d else range(CB)):
                cin_s[j:j + 1, :] = cin
                cin = hl[j:j + 1, :] + p[j:j + 1, :] * cin
            hcar[0:1, :] = cin
            c_in = cin_s[...]

            def fix(t, prev):
                r = (rows - 1 - t) if d else t
                h = b3[r] + a3[r] * c_in
                hx_ref[r] = h
                hpx_ref[r] = prev
                return h

            lax.fori_loop(0, rows, fix, c_in)

    full = lambda shp: pl.BlockSpec(shp, lambda s: (0,) * len(shp))
    colspec = pl.BlockSpec((rows, CB, D), lambda s: (0, blk(s), 0))
    outs = pl.pallas_call(
        body, grid=(nblk + 1,),
        in_specs=[pl.BlockSpec((t_ctx, D), lambda s: (cblk, 5)),
                  pl.BlockSpec((rows, CB, D), lambda s: (0, blk(s), 5)),
                  full((4, D)), full((1, D)),
                  pl.BlockSpec((None, H, DH, DH), lambda s: (d, 0, 0, 0)), full((2, D)),
                  pl.BlockSpec((None, H, DH, DH), lambda s: (d, 0, 0, 0)), full((2, D)), full((2, D))],
        out_specs=(colspec, colspec, colspec, full((t_ctx, D))),
        out_shape=(jax.ShapeDtypeStruct((rows, GRID_W, D), F32),) * 3 + (jax.ShapeDtypeStruct((t_ctx, D), F32),),
        scratch_shapes=[pltpu.VMEM((t_ctx, D), F32), pltpu.VMEM((t_ctx, D), F32), pltpu.VMEM((t_ctx, D), F32),
                        pltpu.VMEM((8, D), F32), pltpu.VMEM((rows, CB, D), F32), pltpu.VMEM((rows, CB, D), F32),
                        pltpu.VMEM((CB, D), F32)],
        name=f"rglru_fwd{d}", compiler_params=_cp(("arbitrary",)))(z, z3, cw, cb, wr, br, wi, bi, lam)
    return outs[0].reshape(t_lat, D), outs[1].reshape(t_lat, D), outs[2].reshape(t_lat, D), outs[3]


def _rglru_bwd(z, cw, cb, wr, br, wi, bi, lam, dh_lat, hp_lat, a_lat, hp_ctx, d, t_lat, t_ctx, prev=None):
    m = z.shape[0]
    rows = t_lat // GRID_W
    z3 = z.reshape(m // GRID_W, GRID_W, IN_COLS)
    nblk = GRID_W // CB
    fblk = _col_of(d, nblk)
    blk = lambda s: fblk(nblk - jnp.minimum(s, nblk - 1))
    cblk = t_lat // t_ctx
    rc = min(RCH, rows)
    last = prev is not None
    view3 = lambda v: v.reshape(rows, GRID_W, D)

    def body(*refs):
        (zc_ref, zx_ref, cw_ref, cb_ref, wr_ref, br_ref, wi_ref, bi_ref, lam_ref, dh_ref, hpx_ref, ax_ref,
         hpc_ref) = refs[:13]
        k = 13
        if last:
            pdx_ref, pdc_ref = refs[13:15]
            k = 15
        ox_ref, oc_ref, dwr_ref, dwi_ref, sum_ref, a_s, b_s, gcar, a3, b3, cin_s = refs[k:]
        s = pl.program_id(0)
        lam_d = lam_ref[d:d + 1, :]
        sp = _softplus_neg(lam_d)
        br_ = br_ref[d:d + 1, :]
        bi_ = bi_ref[d:d + 1, :]
        flat = lambda v: v.reshape(-1, D)

        @pl.when(s == 0)
        def _():
            gcar[...] = jnp.zeros_like(gcar)
            dwr_ref[...] = jnp.zeros_like(dwr_ref)
            dwi_ref[...] = jnp.zeros_like(dwi_ref)
            sum_ref[...] = jnp.zeros_like(sum_ref)


        def conv_sums(dxc, dxb, xm1, x0, xp1, xp2):
            sum_ref[3:4, :] += _colsum(flat(dxc))
            sum_ref[4:5, :] += _colsum(flat(dxb))
            sum_ref[8:9, :] += _colsum(flat(dxc * xm1))
            sum_ref[9:10, :] += _colsum(flat(dxc * x0))
            sum_ref[10:11, :] += _colsum(flat(dxc * xp1))
            sum_ref[11:12, :] += _colsum(flat(dxc * xp2))

        def gate_grads(g, hp, xc, rr, i, a, om):
            mult = jnp.sqrt(om)
            da = g * hp
            ixc = i * xc
            dmult = g * ixc
            dixc = g * mult
            di = dixc * xc
            dxc = dixc * i
            dlog_a = da * a - dmult * ((1.0 - om) / mult)
            dr = dlog_a * (-RG_C * sp)
            sum_ref[2:3, :] += _colsum(dlog_a * rr)
            drp = dr * rr * (1.0 - rr)
            dip = di * i * (1.0 - i)
            sum_ref[0:1, :] += _colsum(drp)
            sum_ref[1:2, :] += _colsum(dip)
            xcb = xc.astype(BF)
            drb = drp.astype(BF)
            dib = dip.astype(BF)
            parts = []
            for gi in range(H):
                gs = slice(gi * DH, (gi + 1) * DH)
                parts.append(_dot_nt(drb[:, gs], wr_ref[gi]) + _dot_nt(dib[:, gs], wi_ref[gi]))
                dwr_ref[gi] += _dot_tn(xcb[:, gs], drb[:, gs])
                dwi_ref[gi] += _dot_tn(xcb[:, gs], dib[:, gs])
            return dxc + jnp.concatenate(parts, axis=1)

        @pl.when(s < nblk)
        def _():
            def local(t, carry):
                c, q = carry
                r = t if d else (rows - 1 - t)
                a = ax_ref[r]
                c = a * (c + dh_ref[r])
                q = a * q
                b3[r] = c
                a3[r] = q
                return c, q

            c, q = lax.fori_loop(0, rows, local, (jnp.zeros((CB, D), F32), jnp.ones((CB, D), F32)))
            cin = gcar[0:1, :]
            for j in (range(CB) if d else range(CB - 1, -1, -1)):
                cin_s[j:j + 1, :] = cin
                cin = c[j:j + 1, :] + q[j:j + 1, :] * cin
            gcar[0:1, :] = cin
            c_in = cin_s[...]
            for r0 in (range(rows - rc, -1, -rc) if d else range(0, rows, rc)):
                if d:
                    lo = max(r0 - 1, 0)
                    cn = b3[lo:r0 + rc - 1] + a3[lo:r0 + rc - 1] * c_in
                    if r0 == 0:
                        cn = jnp.concatenate([c_in[None], cn], axis=0)
                else:
                    hi = min(r0 + rc + 1, rows)
                    cn = b3[r0 + 1:hi] + a3[r0 + 1:hi] * c_in
                    if hi == rows:
                        cn = jnp.concatenate([cn, c_in[None]], axis=0)
                g = flat(dh_ref[r0:r0 + rc] + cn)
                xc = flat(_conv_cols(zx_ref, cw_ref, cb_ref, r0, rc, rows))
                rr, i, a, om = _gates(xc, wr_ref, br_, wi_ref, bi_, sp)
                b3[r0:r0 + rc] = gate_grads(g, flat(hpx_ref[r0:r0 + rc]), xc, rr, i, a, om).reshape(rc, CB, D)
            if last:
                for r0 in range(0, rows, rc):
                    b3[r0:r0 + rc] = b3[r0:r0 + rc] + pdx_ref[r0:r0 + rc]
                for r0 in range(0, rows, rc):
                    w = _window(b3, r0 - 2, rc + 3, rows)
                    xw = _window(zx_ref, r0 - 1, rc + 3, rows)
                    dxc = w[2:rc + 2]
                    dxb = (w[3:rc + 3] * cw_ref[0:1, :] + dxc * cw_ref[1:2, :] + w[1:rc + 1] * cw_ref[2:3, :]
                           + w[0:rc] * cw_ref[3:4, :])
                    conv_sums(dxc, dxb, xw[0:rc], xw[1:rc + 1], xw[2:rc + 2], xw[3:rc + 3])
                    ox_ref[r0:r0 + rc] = dxb
            else:
                for r0 in range(0, rows, rc):
                    ox_ref[r0:r0 + rc] = b3[r0:r0 + rc]

        @pl.when(s == nblk)
        def _():
            r = t_ctx
            xb = zc_ref[...]
            xc = _conv_fwd(xb, cw_ref, cb_ref, r)
            rr, i, a, om = _gates(xc, wr_ref, br_, wi_ref, bi_, sp)
            a_s[...] = a
            b_s[...] = jnp.zeros((r, D), F32)
            c0 = gcar[0:1, :]
            _scan_rows(1 - d, r, a_s, b_s, b_s, c0)
            cs = b_s[...]
            row = lax.broadcasted_iota(jnp.int32, (r, D), 0)
            if d:
                g = jnp.where(row == 0, c0, pltpu.roll(cs, 1, 0))
            else:
                g = jnp.where(row == r - 1, c0, pltpu.roll(cs, r - 1, 0))
            dxc = gate_grads(g, hpc_ref[...], xc, rr, i, a, om)
            if last:
                dxc = dxc + pdc_ref[...]
                dxb = (_shift_rows(dxc, -1, r) * cw_ref[0:1, :] + dxc * cw_ref[1:2, :]
                       + _shift_rows(dxc, 1, r) * cw_ref[2:3, :] + _shift_rows(dxc, 2, r) * cw_ref[3:4, :])
                conv_sums(dxc, dxb, _shift_rows(xb, 1, r), xb, _shift_rows(xb, -1, r), _shift_rows(xb, -2, r))
                oc_ref[...] = dxb
            else:
                oc_ref[...] = dxc
            sum_ref[2:3, :] = sum_ref[2:3, :] * (RG_C * _sig_tail(-lam_d))

    full = lambda shp: pl.BlockSpec(shp, lambda s: (0,) * len(shp))
    once = lambda shp: pl.BlockSpec(shp, lambda s: (0,) * len(shp), pipeline_mode=pl.Buffered(1))
    colspec = pl.BlockSpec((rows, CB, D), lambda s: (0, blk(s), 0))
    colonce = pl.BlockSpec((rows, CB, D), lambda s: (0, blk(s), 0), pipeline_mode=pl.Buffered(1))
    in_specs = [pl.BlockSpec((t_ctx, D), lambda s: (cblk, 5), pipeline_mode=pl.Buffered(1)),
                pl.BlockSpec((rows, CB, D), lambda s: (0, blk(s), 5), pipeline_mode=pl.Buffered(1)),
                full((4, D)), full((1, D)),
                pl.BlockSpec((None, H, DH, DH), lambda s: (d, 0, 0, 0)), full((2, D)),
                pl.BlockSpec((None, H, DH, DH), lambda s: (d, 0, 0, 0)), full((2, D)), full((2, D)),
                colspec, colspec, colonce, once((t_ctx, D))]
    args = [z, z3, cw, cb, wr, br, wi, bi, lam, view3(dh_lat), view3(hp_lat), view3(a_lat), hp_ctx]
    if last:
        in_specs += [colonce, once((t_ctx, D))]
        args += [view3(prev[0]), prev[1]]
    outs = pl.pallas_call(
        body, grid=(nblk + 1,), in_specs=in_specs,
        out_specs=(colspec, full((t_ctx, D)), full((H, DH, DH)), full((H, DH, DH)), full((16, D))),
        out_shape=(jax.ShapeDtypeStruct((rows, GRID_W, D), F32), jax.ShapeDtypeStruct((t_ctx, D), F32),
                   jax.ShapeDtypeStruct((H, DH, DH), F32), jax.ShapeDtypeStruct((H, DH, DH), F32),
                   jax.ShapeDtypeStruct((16, D), F32)),
        scratch_shapes=[pltpu.VMEM((t_ctx, D), F32), pltpu.VMEM((t_ctx, D), F32), pltpu.VMEM((8, D), F32),
                        pltpu.VMEM((rows, CB, D), F32), pltpu.VMEM((rows, CB, D), F32), pltpu.VMEM((CB, D), F32)],
        name=f"rglru_bwd{d}", compiler_params=_cp(("arbitrary",), VMEM_LIMIT_RGLRU_BWD))(*args)
    return (outs[0].reshape(t_lat, D), outs[1]) + tuple(outs[2:])


def _merge(o_f, o_b, h_f, h_b, z, x, tgt, mod, norm_g, ln_g, ln_b, p_a, p_b, w_out, t_lat):
    tm = 256
    nt = t_lat // tm

    def body(of_ref, ob_ref, hf_ref, hb_ref, z4_ref, z6_ref, z7_ref, z8_ref, x_ref, t_ref, mod_ref, ng_ref,
             lg_ref, lb_ref, pa_ref, pb_ref, wo_ref,
             do_ref, dh_ref, dz4a_ref, dz4b_ref, dz6a_ref, sh3_ref, gx_ref,
             y_ref, dout_ref, oa_ref, dpa_ref, obv_ref, dpb_ref, acc_ref):
        i = pl.program_id(0)
        lat = i < nt
        latf = lat.astype(F32)

        @pl.when(i == 0)
        def _():
            acc_ref[...] = jnp.zeros_like(acc_ref)

        def per_head(v):
            return jnp.concatenate(
                [jnp.broadcast_to(jnp.mean(v[:, h * DH:(h + 1) * DH], axis=-1, keepdims=True), (tm, DH))
                 for h in range(H)], axis=1)

        gt = mod_ref[0:1, 2 * D:3 * D]
        gfull = jnp.concatenate([ng_ref[...]] * H, axis=1)
        o = of_ref[...] + ob_ref[...]
        rinv = lax.rsqrt(per_head(o * o) + RMS_EPS)
        n = o * rinv
        na = n * gfull
        z4 = z4_ref[...]
        s4 = _sig(z4)
        silu4 = z4 * s4
        oa = na * silu4
        z6 = z6_ref[...]
        s6 = _sig(z6)
        silu6 = z6 * s6
        hsum = hf_ref[...] + hb_ref[...]
        obv = hsum * silu6
        pa = _dot(oa, pa_ref[...])
        pb = _dot(obv, pb_ref[...])
        s7 = _sig(z7_ref[...])
        s8 = _sig(z8_ref[...])
        y = s7 * pa + s8 * pb
        out = _dot(y, wo_ref[...])
        pre = ALPHA * x_ref[...] + gt * out
        mu = jnp.mean(pre, axis=-1, keepdims=True)
        xc = pre - mu
        rstd = lax.rsqrt(jnp.mean(xc * xc, axis=-1, keepdims=True) + LN_EPS)
        xhat = xc * rstd
        lg = lg_ref[...]
        diff = xhat * lg + lb_ref[...] - t_ref[...]
        acc_ref[8:9, :] += _colsum(diff * diff) * (0.5 / D * latf)
        dxn = diff * (1.0 / D)
        acc_ref[1:2, :] += _colsum(dxn * xhat) * latf
        acc_ref[2:3, :] += _colsum(dxn) * latf
        dxhat = dxn * lg
        dpre = rstd * (dxhat - jnp.mean(dxhat, axis=-1, keepdims=True)
                       - xhat * jnp.mean(dxhat * xhat, axis=-1, keepdims=True))
        gx_ref[...] = ALPHA * dpre
        acc_ref[0:1, :] += _colsum(dpre * out) * latf
        dout = dpre * gt
        dy = _dot_nt(dout, wo_ref[...])
        dpa = dy * s7
        dpb = dy * s8
        dz7 = dy * pa * (s7 * (1.0 - s7))
        dz8 = dy * pb * (s8 * (1.0 - s8))
        doa = _dot_nt(dpa, pa_ref[...])
        dob = _dot_nt(dpb, pb_ref[...])
        dh_ref[...] = dob * silu6
        dz6 = dob * hsum * (s6 * (1.0 + z6 * (1.0 - s6)))
        dna = doa * silu4
        dz4 = doa * na * (s4 * (1.0 + z4 * (1.0 - s4)))
        dng = _colsum(dna * n)
        acc_ref[7:8, 0:DH] += sum(dng[:, h * DH:(h + 1) * DH] for h in range(H)) * latf
        dn = dna * gfull
        do_ref[...] = rinv * (dn - n * per_head(dn * n))
        acc_ref[3:4, :] += _colsum(dz4) * latf
        acc_ref[4:5, :] += _colsum(dz6) * latf
        acc_ref[5:6, :] += _colsum(dz7) * latf
        acc_ref[6:7, :] += _colsum(dz8) * latf
        dz4b, dz6b = (dz4 * latf).astype(BF), (dz6 * latf).astype(BF)
        dz4a_ref[...] = dz4b[:, :CUT4]
        dz4b_ref[...] = dz4b[:, CUT4:]
        dz6a_ref[...] = dz6b[:, :CUT6]
        sh3_ref[:, 0:D - CUT6] = dz6b[:, CUT6:]
        sh3_ref[:, D - CUT6:2 * D - CUT6] = (dz7 * latf).astype(BF)
        sh3_ref[:, 2 * D - CUT6:] = (dz8 * latf).astype(BF)
        y_ref[...] = y.astype(BF)
        dout_ref[...] = dout.astype(BF)
        oa_ref[...] = oa.astype(BF)
        dpa_ref[...] = dpa.astype(BF)
        obv_ref[...] = obv.astype(BF)
        dpb_ref[...] = dpb.astype(BF)

        @pl.when(i == nt - 1)
        def _():
            acc_ref[9:10, :] = jnp.broadcast_to(jnp.sum(acc_ref[8:9, :], axis=-1, keepdims=True), (1, D))

    m = z.shape[0]
    lrow = lambda i: jnp.minimum(i, nt - 1)
    row = pl.BlockSpec((tm, D), lambda i: (lrow(i), 0))
    allrow = lambda cols: pl.BlockSpec((tm, cols), lambda i: (i, 0))
    zs = lambda cb: pl.BlockSpec((tm, D), lambda i: (lrow(i), cb))
    full = lambda shp: pl.BlockSpec(shp, lambda i: (0,) * len(shp))
    wfull = pl.BlockSpec((D, D), lambda i: (0, 0), pipeline_mode=pl.Buffered(1))
    f32o = jax.ShapeDtypeStruct((t_lat, D), F32)
    bfo = jax.ShapeDtypeStruct((t_lat, D), BF)
    bfall = lambda cols: jax.ShapeDtypeStruct((m, cols), BF)
    return pl.pallas_call(
        body, grid=(m // tm,),
        in_specs=[row, row, row, row, zs(4), zs(6), zs(7), zs(8), row, row, full((16, 3 * D)), full((1, DH)),
                  full((1, D)), full((1, D)), wfull, wfull, wfull],
        out_specs=(row, row) + tuple(allrow(c) for c in (CUT4, D - CUT4, CUT6, SHC)) + (row,) * 7 + (full((16, D)),),
        out_shape=(f32o, f32o, bfall(CUT4), bfall(D - CUT4), bfall(CUT6), bfall(SHC), f32o, bfo, bfo, bfo, bfo, bfo, bfo,
                   jax.ShapeDtypeStruct((16, D), F32)),
        name="merge", compiler_params=_cp(("arbitrary",), VMEM_LIMIT_MERGE))(
            o_f, o_b, h_f, h_b, z, z, z, z, x, tgt, mod, norm_g, ln_g, ln_b, p_a, p_b, w_out)


def _wgrad(a, b, name):
    tm = 1024

    def body(a_ref, b_ref, o_ref):
        @pl.when(pl.program_id(0) == 0)
        def _():
            o_ref[...] = jnp.zeros_like(o_ref)
        o_ref[...] += _dot_tn(a_ref[...], b_ref[...])

    row = pl.BlockSpec((tm, D), lambda i: (i, 0))
    return pl.pallas_call(body, grid=(a.shape[0] // tm,), in_specs=[row, row],
                          out_specs=pl.BlockSpec((D, D), lambda i: (0, 0)),
                          out_shape=jax.ShapeDtypeStruct((D, D), F32), name=name,
                          compiler_params=_cp(("arbitrary",)))(a, b)


def _pack_shard2(dz4b, dz5_lat, dz5_ctx, dz6a):
    m, t_lat, t_ctx = dz4b.shape[0], dz5_lat.shape[0], dz5_ctx.shape[0]
    tm = t_ctx
    nlt = t_lat // tm
    w4 = D - CUT4

    def body(a_ref, bl_ref, bc_ref, c_ref, o_ref):
        i = pl.program_id(0)
        o_ref[:, 0:w4] = a_ref[...]
        o_ref[:, w4:w4 + D] = jnp.where(i < nlt, bl_ref[...], bc_ref[...]).astype(BF)
        o_ref[:, w4 + D:] = c_ref[...]

    return pl.pallas_call(
        body, grid=(m // tm,),
        in_specs=[pl.BlockSpec((tm, w4), lambda i: (i, 0)),
                  pl.BlockSpec((tm, D), lambda i: (jnp.minimum(i, nlt - 1), 0)),
                  pl.BlockSpec((tm, D), lambda i: (0, 0)),
                  pl.BlockSpec((tm, CUT6), lambda i: (i, 0))],
        out_specs=pl.BlockSpec((tm, SHC), lambda i: (i, 0)),
        out_shape=jax.ShapeDtypeStruct((m, SHC), BF), name="pack_shard2",
        compiler_params=_cp(("arbitrary",)))(dz4b, dz5_lat, dz5_ctx, dz6a)


def _wgrad_in(u_all, dz_shards):
    m = u_all.shape[1]
    assert m % (6 * 128) == 0
    tm = m // 6
    out = None
    for k, dz_k in enumerate(dz_shards):
        def body(u_ref, dz_ref, *rest):
            o_ref = rest[-1]

            @pl.when(pl.program_id(0) == 0)
            def _():
                o_ref[...] = jnp.zeros_like(o_ref)
            o_ref[0] += _dot(u_ref[...], dz_ref[...])

        out = pl.pallas_call(
            body, grid=(m // tm,),
            in_specs=[pl.BlockSpec((D, tm), lambda i: (0, i)), pl.BlockSpec((tm, SHC), lambda i: (i, 0))]
            + ([] if out is None else [_ANY]),
            out_specs=pl.BlockSpec((1, D, SHC), lambda i, k=k: (k, 0, 0)),
            out_shape=jax.ShapeDtypeStruct((NSH, D, SHC), F32),
            input_output_aliases={} if out is None else {2: 0},
            name=f"wgrad_in{k}", compiler_params=_cp(("arbitrary",)))(u_all, dz_k, *(() if out is None else (out,)))
    return out


def _du(dz_shards, w_in_g, x, ctx, mod, gxres, sums=()):
    tm = 256
    n_lat_tiles, nct = x.shape[0] // tm, ctx.shape[0] // tm
    nt = n_lat_tiles + nct
    ns = len(sums)
    rblk = lambda i: jnp.where(i < nct, n_lat_tiles + i, i - nct)
    lblk = lambda i: jnp.maximum(i - nct, 0)

    def body(*refs):
        dz_refs, refs = refs[:NSH], refs[NSH:]
        w_ref, x_ref, c_ref, mod_ref, gr_ref = refs[:5]
        sum_refs = refs[5:5 + ns]
        gx_ref, dm_ref = refs[5 + ns:7 + ns]
        got_refs = refs[7 + ns:7 + 2 * ns]
        sems = refs[7 + 2 * ns:]
        i = pl.program_id(0)
        is_lat = i >= nct

        @pl.when(i == 0)
        def _():
            dm_ref[...] = jnp.zeros_like(dm_ref)
            if ns:
                for cp in _rs_chip_copies(sum_refs, got_refs, *sems):
                    cp.start()

        du = _dot_nt(dz_refs[0][...], w_ref[0])
        for n in range(1, NSH):
            du = du + _dot_nt(dz_refs[n][...], w_ref[n])
        sc = jnp.where(is_lat, mod_ref[0:1, D:2 * D], mod_ref[1:2, D:2 * D])
        dsh = _colsum(du)
        dsc = _colsum(du * jnp.where(is_lat, x_ref[...], c_ref[...]))

        @pl.when(is_lat)
        def _():
            gx_ref[...] = du * (1.0 + sc) + gr_ref[...]
            dm_ref[0:1, 0:D] += dsh
            dm_ref[0:1, D:2 * D] += dsc

        @pl.when(jnp.logical_not(is_lat))
        def _():
            dm_ref[1:2, 0:D] += dsh
            dm_ref[1:2, D:2 * D] += dsc

        if ns:
            @pl.when(i == nt - 1)
            def _():
                for cp in _rs_chip_copies(sum_refs, got_refs, *sems):
                    cp.wait()

    outs = pl.pallas_call(
        body, grid=(nt,),
        in_specs=[pl.BlockSpec((tm, SHC), lambda i: (rblk(i), 0))] * NSH + [
                  pl.BlockSpec((NSH, D, SHC), lambda i: (0, 0, 0), pipeline_mode=pl.Buffered(1)),
                  pl.BlockSpec((tm, D), lambda i: (lblk(i), 0)),
                  pl.BlockSpec((tm, D), lambda i: (jnp.minimum(i, nct - 1), 0)),
                  pl.BlockSpec((16, 3 * D), lambda i: (0, 0)),
                  pl.BlockSpec((tm, D), lambda i: (lblk(i), 0))] + [_ANY] * ns,
        out_specs=[pl.BlockSpec((tm, D), lambda i: (lblk(i), 0)),
                   pl.BlockSpec((8, 2 * D), lambda i: (0, 0))] + [_ANY] * ns,
        out_shape=[jax.ShapeDtypeStruct((n_lat_tiles * tm, D), F32), jax.ShapeDtypeStruct((8, 2 * D), F32)]
        + [jax.ShapeDtypeStruct((3,) + g.shape[1:], g.dtype) for g in sums],
        scratch_shapes=[pltpu.SemaphoreType.DMA((3 * ns,)), pltpu.SemaphoreType.DMA((3 * ns,))] if ns else [],
        name="du", compiler_params=_cp(("arbitrary",)))(*dz_shards, w_in_g, x, ctx, mod, gxres, *sums)
    return outs[0], outs[1], list(outs[2:])


def _row_tile(rows, cols):
    t = 8
    while t * 2 * cols * 4 <= (1 << 20) and rows % (t * 2) == 0:
        t *= 2
    return t


def _adamw_update(w_ref, g_ref, m_ref, v_ref, d_ref, nm_ref, nv_ref):
    gg = g_ref[...]
    m2 = ADAM_B1 * m_ref[...] + (1.0 - ADAM_B1) * gg
    v2 = ADAM_B2 * v_ref[...] + (1.0 - ADAM_B2) * (gg * gg)
    m_hat = m2 / (1.0 - ADAM_B1 ** ADAM_STEP)
    v_hat = v2 / (1.0 - ADAM_B2 ** ADAM_STEP)
    d_ref[...] = -ADAM_LR * (m_hat / (jnp.sqrt(v_hat) + ADAM_EPS) + ADAM_WD * w_ref[...])
    nm_ref[...] = m2
    nv_ref[...] = v2


def _adamw_many(ws, gs, ms, vs):
    n = len(ws)

    def body(*refs):
        for j in range(n):
            _adamw_update(*refs[4 * j:4 * j + 4], *refs[4 * n + 3 * j:4 * n + 3 * j + 3])

    args = [a for quad in zip(ws, gs, ms, vs) for a in quad]
    outs = pl.pallas_call(body, out_shape=[jax.ShapeDtypeStruct(w.shape, F32) for w in ws for _ in range(3)],
                          name="adamw_small", compiler_params=_cp())(*args)
    return outs[0::3], outs[1::3], outs[2::3]


def _adamw(w, g, m, v, name):
    rows, cols = w.shape
    tr = _row_tile(rows, cols)

    def body(*refs):
        _adamw_update(*refs)

    spec = pl.BlockSpec((tr, cols), lambda i: (i, 0))
    o = jax.ShapeDtypeStruct((rows, cols), F32)
    return pl.pallas_call(body, grid=(rows // tr,), in_specs=[spec] * 4, out_specs=(spec,) * 3,
                          out_shape=(o, o, o), name=name, compiler_params=_cp(("arbitrary",)))(w, g, m, v)


_ANY = pl.BlockSpec(memory_space=pl.ANY)


def _place():
    return lax.axis_index("x"), lax.axis_index("y"), lax.axis_index("c")


def _ag_copies(ins, outs, send, recv, fsend, frecv, lsem):
    x, y, c = _place()
    me = 2 * x + y
    chips = ((1 - x, y), (x, 1 - y), (1 - x, 1 - y))
    local, chip, hand = [], [], []
    for j in range(len(ins)):
        hr = ins[j].shape[0] // 2
        half = pl.ds(pl.multiple_of(c * hr, 8), hr)
        local.append(pltpu.make_async_copy(ins[j], outs[j].at[me], lsem.at[j]))
        for k, (px, py) in enumerate(chips):
            chip.append(pltpu.make_async_remote_copy(
                src_ref=ins[j].at[half, :], dst_ref=outs[j].at[me, half, :], send_sem=send.at[3 * j + k],
                recv_sem=recv.at[3 * j + k], device_id=(px, py, c), device_id_type=MESH))
            got = outs[j].at[2 * px + py, half, :]
            hand.append(pltpu.make_async_remote_copy(
                src_ref=got, dst_ref=got, send_sem=fsend.at[3 * j + k], recv_sem=frecv.at[3 * j + k],
                device_id=(x, y, 1 - c), device_id_type=MESH))
    return local, chip, hand


def _ag_sems(n):
    return [pltpu.SemaphoreType.DMA((3 * n,))] * 4 + [pltpu.SemaphoreType.DMA((n,))]


def _ag_finish(local, chip, hand, done=0):
    for k in range(done, len(chip)):
        chip[k].wait_recv()
        hand[k].start()
    for cp in chip:
        cp.wait_send()
    for k in range(done):
        hand[k].wait_send()
    for k in range(done, len(chip)):
        hand[k].wait_send()
        hand[k].wait_recv()
    for cp in local:
        cp.wait()


def _mod_tp(c8, c_ctx, w_mod_sh, b_mod_sh):
    mc = w_mod_sh.shape[1]

    def body(c8_ref, cctx_ref, w_ref, b_ref, mod_ref, cc_ref, cc_s, part_s, send1, recv1, send3, recv3):
        x, y, c = _place()
        me = 4 * x + 2 * y + c
        ms = 2 * x + y
        copies = []
        for k in range(1, 8):
            peer = (x ^ ((k >> 2) & 1), y ^ ((k >> 1) & 1), c ^ (k & 1))
            cp = pltpu.make_async_remote_copy(src_ref=c8_ref, dst_ref=cc_s.at[me], send_sem=send1.at[k],
                                              recv_sem=recv1.at[k], device_id=peer, device_id_type=MESH)
            cp.start()
            copies.append(cp)
        cc_s[me] = c8_ref[...]
        for cp in copies:
            cp.wait()
        cc_ref[...] = jnp.zeros_like(cc_ref)
        for j in range(8):
            cc_ref[j:j + 1, :] = cc_s[j, 0:1, :]
        cc_ref[8:9, :] = cctx_ref[...]
        v = cc_ref[...]
        part_s[ms] = _dot(v * _sig(v), w_ref[...]) + b_ref[...]
        copies = []
        for k in range(1, 4):
            peer = (x ^ ((k >> 1) & 1), y ^ (k & 1), c)
            cp = pltpu.make_async_remote_copy(src_ref=part_s.at[ms], dst_ref=part_s.at[ms], send_sem=send3.at[k],
                                              recv_sem=recv3.at[k], device_id=peer, device_id_type=MESH)
            cp.start()
            copies.append(cp)
        for cp in copies:
            cp.wait()
        for s in range(NSH):
            mod_ref[:, s * mc:(s + 1) * mc] = part_s[s]

    vm = pl.BlockSpec(memory_space=pltpu.VMEM)
    return pl.pallas_call(
        body, in_specs=[vm] * 4, out_specs=(vm, vm),
        out_shape=(jax.ShapeDtypeStruct((16, NSH * mc), F32), jax.ShapeDtypeStruct((16, D), F32)),
        scratch_shapes=[pltpu.VMEM((8, 8, D), F32), pltpu.VMEM((NSH, 16, mc), F32),
                        pltpu.SemaphoreType.DMA((8,)), pltpu.SemaphoreType.DMA((8,)),
                        pltpu.SemaphoreType.DMA((4,)), pltpu.SemaphoreType.DMA((4,))],
        name="mod_tp", compiler_params=_cp())(c8, c_ctx, w_mod_sh, b_mod_sh)


def _inproj_ag(x, ctx, mod, w_in_sh, b_in, narrow_sh):
    n_lat = x.shape[0]
    m = n_lat + ctx.shape[0]
    assert m % (11 * 128) == 0
    tm = m // 11
    nt = m // tm
    nfull = n_lat // tm
    assert nfull == nt - 1
    tail = jnp.concatenate([x[nfull * tm:], ctx], axis=0)
    x_, y_ = lax.axis_index("x"), lax.axis_index("y")
    sids = jnp.stack([2 * x_ + y_, 2 * (1 - x_) + y_, 2 * x_ + 1 - y_, 2 * (1 - x_) + 1 - y_]).astype(jnp.int32)

    def body(sid_ref, x_ref, tail_ref, mod_ref, b_ref, wsh_ref, nsh_ref, z_ref, u_ref, wg_ref, ng_ref, w_s, u_s, *sems):
        n = pl.program_id(0)
        i = pl.program_id(1)
        rows = pl.ds(pl.multiple_of(i * tm, tm), tm)
        ag = ((wsh_ref, nsh_ref), (wg_ref, ng_ref)) + tuple(sems[:5])
        wsem = sems[5]

        def load(src):
            cp = pltpu.make_async_copy(src, w_s, wsem)
            cp.start()
            cp.wait()

        @pl.when((n == 0) & (i == 0))
        def _():
            local, chip, _ = _ag_copies(*ag)
            for cp in chip + local:
                cp.start()
            load(wsh_ref)

        for k in range(NSH - 1):
            @pl.when((n == k + 1) & (i == 0))
            def _():
                _, chip, hand = _ag_copies(*ag)
                chip[k].wait_recv()
                hand[k].start()
                hand[k].wait_recv()
                load(wg_ref.at[sid_ref[k + 1]])

        @pl.when(n == 0)
        def _():
            is_lat = (i * tm + lax.broadcasted_iota(jnp.int32, (tm, 1), 0)) < n_lat
            u = _modulate(jnp.where(i < nfull, x_ref[...], tail_ref[...]), mod_ref, is_lat)
            u_s[rows, :] = u.astype(BF)
            u_ref[...] = u.T.astype(BF)

        z_ref[...] = _dot(u_s[rows, :], w_s[...]) + b_ref[...]

        @pl.when((n == NSH - 1) & (i == nt - 1))
        def _():
            _ag_finish(*_ag_copies(*ag), done=NSH - 1)

    first = lambda n, i: jnp.where(n == 0, i, nt - 1)
    outs = pl.pallas_call(
        body, grid_spec=pltpu.PrefetchScalarGridSpec(
            num_scalar_prefetch=1, grid=(NSH, nt),
            in_specs=[pl.BlockSpec((tm, D), lambda n, i, sid: (jnp.minimum(first(n, i), nfull - 1), 0)),
                      pl.BlockSpec((tm, D), lambda n, i, sid: (0, 0)),
                      pl.BlockSpec((16, 3 * D), lambda n, i, sid: (0, 0)),
                      pl.BlockSpec((1, SHC), lambda n, i, sid: (0, sid[n])), _ANY, _ANY],
            out_specs=[pl.BlockSpec((tm, SHC), lambda n, i, sid: (i, sid[n])),
                       pl.BlockSpec((D, tm), lambda n, i, sid: (0, first(n, i))), _ANY, _ANY],
            scratch_shapes=[pltpu.VMEM((D, SHC), BF), pltpu.VMEM((m, D), BF)] + _ag_sems(2)
            + [pltpu.SemaphoreType.DMA]),
        out_shape=[jax.ShapeDtypeStruct((m, IN_COLS), F32), jax.ShapeDtypeStruct((D, m), BF),
                   jax.ShapeDtypeStruct((NSH,) + w_in_sh.shape, BF),
                   jax.ShapeDtypeStruct((NSH,) + narrow_sh.shape, narrow_sh.dtype)],
        name="inproj_ag", compiler_params=_cp(("arbitrary", "arbitrary")))(sids, x, tail, mod, b_in, w_in_sh, narrow_sh)
    return outs


def _rs_sibling(grads, name):
    n = len(grads)

    def body(*refs):
        ins, got = refs[:n], refs[n:2 * n]
        send, recv = refs[2 * n:]
        x, y, c = _place()
        copies = []
        for j in range(n):
            hr = ins[j].shape[1] // 2
            for s in range(NSH):
                give = ins[j].at[s, pl.ds(pl.multiple_of((1 - c) * hr, 8), hr), :]
                cp = pltpu.make_async_remote_copy(src_ref=give, dst_ref=got[j].at[s], send_sem=send.at[NSH * j + s],
                                                  recv_sem=recv.at[NSH * j + s], device_id=(x, y, 1 - c),
                                                  device_id_type=MESH)
                cp.start()
                copies.append(cp)
        for cp in copies:
            cp.wait()

    half = [jax.ShapeDtypeStruct((NSH, g.shape[1] // 2, g.shape[2]), F32) for g in grads]
    return pl.pallas_call(
        body, in_specs=[_ANY] * n, out_specs=[_ANY] * n, out_shape=half,
        scratch_shapes=[pltpu.SemaphoreType.DMA((NSH * n,)), pltpu.SemaphoreType.DMA((NSH * n,))],
        name=name)(*grads)


def _core_vec():
    return lax.axis_index("c").astype(jnp.int32).reshape(1)


def _rs_add1(g, got, name):
    _, r, cols = g.shape
    hr = r // 2
    tr = _row_tile(hr, cols)
    nb = hr // tr

    def body(c_ref, g_ref, got_ref, o_ref):
        o_ref[...] = (g_ref[...] + got_ref[...]).astype(BF)

    spec = pl.BlockSpec((1, tr, cols), lambda s, i, c_ref: (s, i, 0))
    return pl.pallas_call(
        body, grid_spec=pltpu.PrefetchScalarGridSpec(
            num_scalar_prefetch=1, grid=(NSH, nb),
            in_specs=[pl.BlockSpec((1, tr, cols), lambda s, i, c_ref: (s, c_ref[0] * nb + i, 0)), spec],
            out_specs=spec),
        out_shape=jax.ShapeDtypeStruct((NSH, hr, cols), BF), name=name,
        compiler_params=_cp(("arbitrary", "arbitrary")))(_core_vec(), g, got)


def _rs_add2(sums, got, name):
    _, hr, cols = sums.shape
    tr = _row_tile(hr, cols)
    nb = hr // tr
    place = jnp.stack([2 * lax.axis_index("x") + lax.axis_index("y"), lax.axis_index("c")]).astype(jnp.int32)

    def body(p_ref, s_ref, got_ref, o_ref):
        f = lambda v: v.astype(F32)
        o_ref[...] = f(s_ref[0]) + f(got_ref[0]) + f(got_ref[1]) + f(got_ref[2])

    return pl.pallas_call(
        body, grid_spec=pltpu.PrefetchScalarGridSpec(
            num_scalar_prefetch=1, grid=(nb,),
            in_specs=[pl.BlockSpec((1, tr, cols), lambda i, p_ref: (p_ref[0], i, 0)),
                      pl.BlockSpec((3, tr, cols), lambda i, p_ref: (0, i, 0))],
            out_specs=pl.BlockSpec((tr, cols), lambda i, p_ref: (p_ref[1] * nb + i, 0))),
        out_shape=jax.ShapeDtypeStruct((2 * hr, cols), F32), name=name,
        compiler_params=_cp(("arbitrary",)))(place, sums, got)


def _rs_chip_copies(ins, got, send, recv):
    x, y, c = _place()
    peers = ((1 - x, y), (x, 1 - y), (1 - x, 1 - y))
    return [pltpu.make_async_remote_copy(src_ref=ins[j].at[2 * px + py], dst_ref=got[j].at[k],
                                         send_sem=send.at[3 * j + k], recv_sem=recv.at[3 * j + k],
                                         device_id=(px, py, c), device_id_type=MESH)
            for j in range(len(ins)) for k, (px, py) in enumerate(peers)]


def _ag_sibling(fulls):
    n = len(fulls)
    nck = 4

    def body(*refs):
        outs = refs[n:2 * n]
        send, recv = refs[2 * n:]
        x, y, c = _place()
        copies = []
        for j in range(n):
            qr = outs[j].shape[0] // (2 * nck)
            for k in range(nck):
                rows = outs[j].at[pl.ds(pl.multiple_of((c * nck + k) * qr, 8), qr), :]
                cp = pltpu.make_async_remote_copy(src_ref=rows, dst_ref=rows, send_sem=send.at[nck * j + k],
                                                  recv_sem=recv.at[nck * j + k], device_id=(x, y, 1 - c),
                                                  device_id_type=MESH)
                cp.start()
                copies.append(cp)
        for cp in copies:
            cp.wait()

    return pl.pallas_call(
        body, in_specs=[_ANY] * n, out_specs=[_ANY] * n,
        out_shape=[jax.ShapeDtypeStruct(f.shape, F32) for f in fulls],
        input_output_aliases={j: j for j in range(n)},
        scratch_shapes=[pltpu.SemaphoreType.DMA((nck * n,)), pltpu.SemaphoreType.DMA((nck * n,))],
        name="ag_sibling")(*fulls)


def _allreduce_small(buf):
    rows = buf.shape[0]
    pr = rows // 8

    def body(in_ref, out_ref, part, stage_a, stage_b, *sems):
        (sa, ra, sb, rb, sc, rc, sd, rd) = sems
        x, y, c = _place()
        s = 2 * x + y
        sib = (x, y, 1 - c)
        chips = ((1 - x, y), (x, 1 - y), (1 - x, 1 - y))

        def piece(ref, chip, core):
            return ref.at[pl.ds(pl.multiple_of((2 * chip + core) * pr, 8), pr), :]

        def run(copies):
            for cp in copies:
                cp.start()
            for cp in copies:
                cp.wait()

        run([pltpu.make_async_remote_copy(src_ref=piece(in_ref, j, 1 - c), dst_ref=stage_a.at[j], send_sem=sa.at[j],
                                          recv_sem=ra.at[j], device_id=sib, device_id_type=MESH) for j in range(NSH)])
        for j in range(NSH):
            part[j] = piece(in_ref, j, c)[...] + stage_a[j]
        run([pltpu.make_async_remote_copy(src_ref=part.at[2 * px + py], dst_ref=stage_b.at[k], send_sem=sb.at[k],
                                          recv_sem=rb.at[k], device_id=(px, py, c), device_id_type=MESH)
             for k, (px, py) in enumerate(chips)])
        piece(out_ref, s, c)[...] = part[s] + stage_b[0] + stage_b[1] + stage_b[2]
        run([pltpu.make_async_remote_copy(src_ref=piece(out_ref, s, c), dst_ref=piece(out_ref, s, c),
                                          send_sem=sc.at[k], recv_sem=rc.at[k], device_id=(px, py, c),
                                          device_id_type=MESH) for k, (px, py) in enumerate(chips)])
        run([pltpu.make_async_remote_copy(src_ref=piece(out_ref, j, c), dst_ref=piece(out_ref, j, c), send_sem=sd.at[j],
                                          recv_sem=rd.at[j], device_id=sib, device_id_type=MESH) for j in range(NSH)])

    vm = pl.BlockSpec(memory_space=pltpu.VMEM)
    return pl.pallas_call(
        body, in_specs=[vm], out_specs=vm, out_shape=jax.ShapeDtypeStruct((rows, D), F32),
        scratch_shapes=[pltpu.VMEM((NSH, pr, D), F32), pltpu.VMEM((NSH, pr, D), F32), pltpu.VMEM((3, pr, D), F32)]
        + [pltpu.SemaphoreType.DMA((NSH,))] * 8,
        name="allreduce_small", compiler_params=_cp())(buf)


def _pad_rows(a, mult):
    pad = (-a.shape[0]) % mult
    return jnp.concatenate([a, jnp.zeros((pad, a.shape[1]), a.dtype)]) if pad else a


def _local_step(x, c, ctx, c_ctx, tgt, me, shard, sh, b_mod, b_in, norm_g, cb, wr, wi, ln_g, ln_b):
    t_lat, t_ctx = x.shape[0], ctx.shape[0]
    nlt = t_lat // 256
    nlb, ncb = t_lat // RB, t_ctx // RB
    mc = 3 * D // NSH
    mod_all, cc_all = _mod_tp(jnp.zeros((8, D), F32).at[0].set(c), c_ctx.reshape(1, D), sh["w_mod"],
                              lax.dynamic_slice_in_dim(b_mod, shard * mc, mc, axis=1))
    mod = jnp.zeros((16, 3 * D), F32).at[0].set(mod_all[me]).at[1].set(mod_all[8])
    cc = jnp.zeros((16, D), F32).at[0].set(c).at[1].set(c_ctx)
    z, u_all, w_in_g, nar = _inproj_ag(x, ctx, mod, sh["w_in"], b_in, sh["narrow"])
    nar = jnp.transpose(nar, (1, 0, 2)).reshape(-1, D)
    lbl, cw, br, bi, lam = nar[0:4].reshape(2, 2, D), nar[4:8], nar[8:10], nar[10:12], nar[12:14]
    o0, st0, (w_mod_g, p_a, p_b, w_out) = _gla_fwd(z, lbl, 0, nlb, ncb,
                                                   gather=[sh[k] for k in ("w_mod", "p_a", "p_b", "w_out")])
    p_a, p_b, w_out = p_a.reshape(D, D), p_b.reshape(D, D), w_out.reshape(D, D)
    o1, st1, _ = _gla_fwd(z, lbl, 1, nlb, ncb)
    h0, hp0, a0, hpc0 = _rglru_fwd(z, cw, cb, wr, br, wi, bi, lam, 0, t_lat, t_ctx)
    h1, hp1, a1, hpc1 = _rglru_fwd(z, cw, cb, wr, br, wi, bi, lam, 1, t_lat, t_ctx)
    (do, dh, dz4a, dz4b, dz6a, dz_sh3, gxres, y, dout, oa, dpa, obv, dpb, acc) = _merge(
        o0, o1, h0, h1, z, x, tgt, mod, norm_g, ln_g, ln_b, p_a, p_b, w_out, t_lat)
    gp_a = _wgrad(oa, dpa, "wgrad_pa")
    gp_b = _wgrad(obv, dpb, "wgrad_pb")
    gw_out = _wgrad(y, dout, "wgrad_wout")
    dxc_lat, dxc_ctx, dwr0, dwi0, sb0 = _rglru_bwd(z, cw, cb, wr, br, wi, bi, lam, dh, hp0, a0, hpc0, 0, t_lat, t_ctx)
    dz5_lat, dz5_ctx, dwr1, dwi1, sb1 = _rglru_bwd(z, cw, cb, wr, br, wi, bi, lam, dh, hp1, a1, hpc1, 1, t_lat, t_ctx,
                                                   prev=(dxc_lat, dxc_ctx))
    dq0, dv0, dz1, sa0 = _gla_bwd(z, lbl, do, st0, 0, nlb, ncb)
    dz_sh0, dz_sh1, sa1 = _gla_bwd(z, lbl, do, st1, 1, nlb, ncb, prev=(dq0, dv0, dz1, dz4a))
    dz = (dz_sh0, dz_sh1, _pack_shard2(dz4b, dz5_lat, dz5_ctx, dz6a), dz_sh3)
    grads = [_wgrad_in(u_all, dz)] + [g.reshape(NSH, D // NSH, D) for g in (gp_a, gp_b, gw_out)]
    sums = [_rs_add1(g, b, f"rs_add1_{j}") for j, (g, b) in enumerate(zip(grads, _rs_sibling(grads, "rs_sibling")))]
    gx, dm, got = _du(dz, w_in_g, x, ctx, mod, gxres, sums)
    fulls = [_rs_add2(a, b, f"rs_add2_{j}") for j, (a, b) in enumerate(zip(sums, got))]
    big = dict(zip(_RS, _ag_sibling(fulls)))
    dmod = jnp.zeros((16, 3 * D), F32).at[0:2, 0:2 * D].set(dm[0:2]).at[0, 2 * D:].set(acc[0])
    dcc = _mod_bwd(cc, dmod, w_mod_g)
    small = dict(
        c_ctx=dcc[1:2], b_mod=(dmod[0] + dmod[1]).reshape(3, D),
        b_in=jnp.stack([sa1[2], sa0[0], sa1[0], sa1[3], acc[3], sb1[4], acc[4], acc[5], acc[6]]),
        lb_logits=jnp.stack([sa0[1], sa1[1], -sa0[1], -sa1[1]]),
        norm_a_g=acc[7:8], conv_w=sb1[8:12], conv_b=sb1[3:4],
        w_r=jnp.stack([dwr0, dwr1]).reshape(-1, D), w_i=jnp.stack([dwi0, dwi1]).reshape(-1, D),
        b_r=jnp.stack([sb0[0], sb1[0]]), b_i=jnp.stack([sb0[1], sb1[1]]), lam=jnp.stack([sb0[2], sb1[2]]),
        ln_g=acc[1:2], ln_b=acc[2:3])
    return acc[9, 0], gx, big, small, dmod, cc_all


_RS =("w_in", "p_a", "p_b", "w_out")
_SMALL =("c_ctx", "b_mod", "b_in", "lb_logits", "norm_a_g", "conv_w", "conv_b", "w_r", "w_i", "b_r", "b_i", "lam",
          "ln_g", "ln_b")
_BIG = ("w_mod", "w_in", "p_a", "p_b", "w_out")
_COL_SHARDED = ("lb_logits", "conv_w", "b_r", "b_i", "lam")
_WEIGHTS = ("c_ctx", "w_mod", "b_mod", "w_in", "b_in", "lb_logits", "norm_a_g", "conv_w", "conv_b", "w_r", "b_r", "w_i",
            "b_i", "lam", "p_a", "p_b", "w_out", "ln_g", "ln_b")


def kernel(x, c, ctx, c_ctx, w_mod, b_mod, w_in, b_in, lb_logits, norm_a_g, conv_w, conv_b, w_r, b_r, w_i, b_i, lam, p_a, p_b, w_out, ln_g, ln_b, loss_target, m_c_ctx, m_w_mod, m_b_mod, m_w_in, m_b_in, m_lb_logits, m_norm_a_g, m_conv_w, m_conv_b, m_w_r, m_b_r, m_w_i, m_b_i, m_lam, m_p_a, m_p_b, m_w_out, m_ln_g, m_ln_b, v_c_ctx, v_w_mod, v_b_mod, v_w_in, v_b_in, v_lb_logits, v_norm_a_g, v_conv_w, v_conv_b, v_w_r, v_b_r, v_w_i, v_b_i, v_lam, v_p_a, v_p_b, v_w_out, v_ln_g, v_ln_b):
    w = dict(c_ctx=c_ctx, w_mod=w_mod, b_mod=b_mod, w_in=w_in, b_in=b_in, lb_logits=lb_logits, norm_a_g=norm_a_g,
             conv_w=conv_w, conv_b=conv_b, w_r=w_r, b_r=b_r, w_i=w_i, b_i=b_i, lam=lam, p_a=p_a, p_b=p_b, w_out=w_out,
             ln_g=ln_g, ln_b=ln_b)
    m = dict(c_ctx=m_c_ctx, w_mod=m_w_mod, b_mod=m_b_mod, w_in=m_w_in, b_in=m_b_in, lb_logits=m_lb_logits,
             norm_a_g=m_norm_a_g, conv_w=m_conv_w, conv_b=m_conv_b, w_r=m_w_r, b_r=m_b_r, w_i=m_w_i, b_i=m_b_i,
             lam=m_lam, p_a=m_p_a, p_b=m_p_b, w_out=m_w_out, ln_g=m_ln_g, ln_b=m_ln_b)
    v = dict(c_ctx=v_c_ctx, w_mod=v_w_mod, b_mod=v_b_mod, w_in=v_w_in, b_in=v_b_in, lb_logits=v_lb_logits,
             norm_a_g=v_norm_a_g, conv_w=v_conv_w, conv_b=v_conv_b, w_r=v_w_r, b_r=v_b_r, w_i=v_w_i, b_i=v_b_i,
             lam=v_lam, p_a=v_p_a, p_b=v_p_b, w_out=v_w_out, ln_g=v_ln_g, ln_b=v_ln_b)
    shard = 2 * lax.axis_index("x") + lax.axis_index("y")
    cs = D // NSH

    sh = {k: w[k][0].astype(BF) for k in _BIG}
    sh["narrow"] = _pad_rows(jnp.concatenate([lb_logits.reshape(4, cs), conv_w[0], b_r[0], b_i[0], lam[0]], axis=0), 8)
    me = 2 * shard + lax.axis_index("c")
    loss, gx, big, small, dmod, cc_all = _local_step(
        x[0], c[0], ctx[0], c_ctx, loss_target[0], me, shard, sh, b_mod, b_in, norm_a_g, conv_b, w_r[0], w_i[0],
        ln_g, ln_b)

    dmod_rows = jnp.zeros((16, 3 * D), F32).at[me].set(dmod[0]).at[8].set(dmod[1]).reshape(48, D)
    loss_rows = jnp.zeros((8, D), F32).at[0, 0].set(loss)
    sizes = [small[k].shape[0] for k in _SMALL]
    red = _allreduce_small(_pad_rows(jnp.concatenate([_pad_rows(small[k], 8) for k in _SMALL] + [dmod_rows, loss_rows],
                                                     axis=0), 64))
    grads = {}
    off = 0
    for k, n in zip(_SMALL, sizes):
        g = red[off:off + n]
        off += n + (-n) % 8
        if k == "norm_a_g":
            g = g[:, :DH]
        if k in _COL_SHARDED:
            g = lax.dynamic_slice_in_dim(g, shard * cs, cs, axis=1)
        grads[k] = g.reshape(w[k].shape)
    for k in _RS:
        grads[k] = big[k].reshape(w[k].shape)
    dmod_all = red[off:off + 48].reshape(16, 3 * D)
    loss = red[off + 48, 0]
    mc = 3 * D // NSH
    grads["w_mod"] = _wmod_grad(cc_all, lax.dynamic_slice_in_dim(dmod_all, shard * mc, mc, axis=1)).reshape(
        w["w_mod"].shape)

    delta, new_m, new_v = {}, {}, {}
    for k in _BIG:
        shp = w[k].shape
        two = lambda a: a.reshape(shp[-2], shp[-1])
        d_, m_, v_ = _adamw(two(w[k]), two(grads[k]), two(m[k]), two(v[k]), f"adamw_{k}")
        delta[k], new_m[k], new_v[k] = d_.reshape(shp), m_.reshape(shp), v_.reshape(shp)
    d_, m_, v_ = _adamw_many(*[[t[k] for k in _SMALL] for t in (w, grads, m, v)])
    delta.update(zip(_SMALL, d_))
    new_m.update(zip(_SMALL, m_))
    new_v.update(zip(_SMALL, v_))

    return (loss, gx[None], *[grads[k] for k in _WEIGHTS], *[delta[k] for k in _WEIGHTS],
            *[new_m[k] for k in _WEIGHTS], *[new_v[k] for k in _WEIGHTS])
```

```python
import jax
import jax.numpy as jnp
from jax import lax
from jax.experimental import pallas as pl
from jax.experimental.pallas import tpu as pltpu

F32 = jnp.float32
BF = jnp.bfloat16
MESH = pl.DeviceIdType.MESH

D = 1024
H = 8
DH = 128
CH = 64
RB = 256
NCK = RB // CH
GRID_W = 64
CB = 8
RCH = 16
IN_COLS = 9 * D
NSH = 4
SHC = IN_COLS // NSH
MC = 3 * D // NSH
CUT2 = SHC - 2 * D
CUT4 = 2 * SHC - 4 * D
CUT6 = 3 * SHC - 6 * D
RG_C = 8.0
ALPHA = 2.0 ** 0.25
LN_EPS = 1e-5
RMS_EPS = 1e-6
Q_SCALE = DH ** -0.5
ADAM_LR, ADAM_B1, ADAM_B2, ADAM_EPS, ADAM_WD, ADAM_STEP = 0.001, 0.9, 0.999, 1e-08, 0.01, 10
VMEM_LIMIT = 56 * 1024 * 1024
VMEM_LIMIT_MERGE = 60 * 1024 * 1024
VMEM_LIMIT_RGLRU_BWD = 63 * 1024 * 1024


def _cp(sem=None, vmem=VMEM_LIMIT):
    return pltpu.CompilerParams(dimension_semantics=sem, vmem_limit_bytes=vmem)


def _sig(x):
    return 0.5 * jnp.tanh(0.5 * x) + 0.5


def _sig_tail(x):
    return 1.0 / (1.0 + jnp.exp(-x))


def _dot(a, b):
    return jnp.dot(a.astype(BF), b.astype(BF), preferred_element_type=F32)


def _dot_nt(a, b):
    return lax.dot_general(a.astype(BF), b.astype(BF), (((1,), (1,)), ((), ())), preferred_element_type=F32)


def _dot_tn(a, b):
    return lax.dot_general(a.astype(BF), b.astype(BF), (((0,), (0,)), ((), ())), preferred_element_type=F32)


def _colsum(v):
    return jnp.sum(v, axis=0, keepdims=True)


def _mod_bwd(cc, dmod, w_mod_g):
    def body(cc_ref, dm_ref, w_ref, dcc_ref):
        v = cc_ref[...]
        sg = _sig(v)
        ds = jnp.zeros((16, D), F32)
        for k in range(NSH):
            ds = ds + _dot_nt(dm_ref[:, k * MC:(k + 1) * MC], w_ref[k])
        dcc_ref[...] = ds * (sg * (1.0 + v * (1.0 - sg)))
    return pl.pallas_call(body, out_shape=jax.ShapeDtypeStruct((16, D), F32),
                          name="mod_bwd", compiler_params=_cp())(cc, dmod, w_mod_g)


def _wmod_grad(cc, dmod_cols):
    def body(cc_ref, dm_ref, dw_ref):
        v = cc_ref[...]
        dw_ref[...] = _dot_tn(v * _sig(v), dm_ref[...])
    return pl.pallas_call(body, out_shape=jax.ShapeDtypeStruct((D, dmod_cols.shape[1]), F32),
                          name="wmod_grad", compiler_params=_cp())(cc, dmod_cols)


def _modulate(xv, mod_ref, is_lat):
    sh = jnp.where(is_lat, mod_ref[0:1, 0:D], mod_ref[1:2, 0:D])
    sc = jnp.where(is_lat, mod_ref[0:1, D:2 * D], mod_ref[1:2, D:2 * D])
    return xv * (1.0 + sc) + sh


def _gla_mask(d, n):
    row = lax.broadcasted_iota(jnp.int32, (n, n), 0)
    col = lax.broadcasted_iota(jnp.int32, (n, n), 1)
    same = (row // CH) == (col // CH)
    return same & ((row <= col) if d else (row >= col))


def _chunk_cumsum(v, rev):
    n = v.shape[0]
    pos = lax.broadcasted_iota(jnp.int32, v.shape, 0) & (CH - 1)
    s = 1
    while s < CH:
        if rev:
            v = v + jnp.where(pos < CH - s, pltpu.roll(v, n - s, 0), 0.0)
        else:
            v = v + jnp.where(pos >= s, pltpu.roll(v, s, 0), 0.0)
        s *= 2
    return v


def _chunk_rows(c):
    return slice(c * CH, (c + 1) * CH)


def _gla_features(zq, zf, lb, d):
    sq = _sig(zq)
    q = zq * sq * Q_SCALE
    sf = _sig(zf)
    f = lb + (1.0 - lb) * sf
    k = 1.0 - f
    g = _chunk_cumsum(jnp.log(f), d)
    last = 0 if d else CH - 1
    gls = [g[c * CH + last:c * CH + last + 1, :] for c in range(NCK)]
    glb = jnp.concatenate([jnp.broadcast_to(gl, (CH, D)) for gl in gls], axis=0)
    eg, eig, eeg = jnp.exp(g), jnp.exp(-g), jnp.exp(glb - g)
    decs = [jnp.exp(gl) for gl in gls]
    return sq, sf, f, q * eg, k * eig, k * eeg, eg, eig, eeg, decs


def _lower_bound(lbl_ref, d):
    return _sig_tail(lbl_ref[0, d:d + 1, :] - lbl_ref[1, d:d + 1, :])


def _gla_rb(d, nlb, ncb):
    nrb = nlb + ncb
    if d:
        return lambda s: nrb - 1 - s
    return lambda s: jnp.where(s < ncb, nlb + s, s - ncb)


def _gla_fwd(z, lbl, d, nlb, ncb, gather=()):
    m = z.shape[0]
    nrb = nlb + ncb
    rb = _gla_rb(d, nlb, ncb)
    ng = len(gather)

    def body(*refs):
        q_ref, f_ref, v_ref, lbl_ref = refs[:4]
        ag_in = refs[4:4 + ng]
        o_ref, st_ref = refs[4 + ng:6 + ng]
        ag_out = refs[6 + ng:6 + 2 * ng]
        S = refs[6 + 2 * ng]
        ag_sems = refs[7 + 2 * ng:]
        s = pl.program_id(0)

        @pl.when(s == 0)
        def _():
            S[...] = jnp.zeros_like(S)
            if ng:
                local, chip, _ = _ag_copies(ag_in, ag_out, *ag_sems)
                for cp in chip + local:
                    cp.start()

        lb = _lower_bound(lbl_ref, d)
        mb = _gla_mask(d, CH)
        _, _, _, qd, ki, ke, _, _, _, decs = _gla_features(q_ref[...], f_ref[...], lb, d)
        qd, ki, ke, v = qd.astype(BF), ki.astype(BF), ke.astype(BF), v_ref[...].astype(BF)
        order = range(NCK - 1, -1, -1) if d else range(NCK)
        for h in range(H):
            hs = slice(h * DH, (h + 1) * DH)
            intra, upd = {}, {}
            for c in range(NCK):
                rs = _chunk_rows(c)
                a = jnp.where(mb, _dot_nt(qd[rs, hs], ki[rs, hs]), 0.0)
                intra[c] = _dot(a, v[rs, hs])
                upd[c] = _dot_tn(v[rs, hs], ke[rs, hs])
            st = S[h]
            for c in order:
                rs = _chunk_rows(c)
                st_ref[c, h] = st
                o_ref[rs, hs] = intra[c] + _dot_nt(qd[rs, hs], st)
                st = st * decs[c][:, hs] + upd[c]
            S[h] = st

        if ng:
            @pl.when(s == nrb - 1)
            def _():
                _ag_finish(*_ag_copies(ag_in, ag_out, *ag_sems))

    def zspec(cb):
        return pl.BlockSpec((RB, D), lambda s: (rb(s), cb))

    outs = pl.pallas_call(
        body, grid=(nrb,),
        in_specs=[zspec(0), zspec(1 + d), zspec(3), pl.BlockSpec((2, 2, D), lambda s: (0, 0, 0))] + [_ANY] * ng,
        out_specs=[pl.BlockSpec((RB, D), lambda s: (rb(s), 0)),
                   pl.BlockSpec((NCK, H, DH, DH), lambda s: (rb(s), 0, 0, 0))] + [_ANY] * ng,
        out_shape=[jax.ShapeDtypeStruct((m, D), F32), jax.ShapeDtypeStruct((m // CH, H, DH, DH), F32)]
        + [jax.ShapeDtypeStruct((NSH,) + g.shape, g.dtype) for g in gather],
        scratch_shapes=[pltpu.VMEM((H, DH, DH), F32)] + (_ag_sems(ng) if ng else []),
        name=f"gla_fwd{d}", compiler_params=_cp(("arbitrary",)))(z, z, z, lbl, *gather)
    return outs[0], outs[1], list(outs[2:])


def _gla_bwd(z, lbl, do_lat, states, d, nlb, ncb, prev=None):
    m = z.shape[0]
    nrb = nlb + ncb
    fwd_rb = _gla_rb(d, nlb, ncb)
    rb = lambda s: fwd_rb(nrb - 1 - s)
    last = prev is not None

    def body(*refs):
        if last:
            (q_ref, f_ref, v_ref, lbl_ref, do_ref, st_ref, pq_ref, pv_ref, dz1_ref, dz4a_ref, sh0_ref, sh1_ref, sum_ref,
             dS) = refs
        else:
            q_ref, f_ref, v_ref, lbl_ref, do_ref, st_ref, o0_ref, o1_ref, o2_ref, sum_ref, dS = refs
        s = pl.program_id(0)
        is_lat = rb(s) < nlb

        @pl.when(s == 0)
        def _():
            dS[...] = jnp.zeros_like(dS)
            sum_ref[...] = jnp.zeros_like(sum_ref)

        lb = _lower_bound(lbl_ref, d)
        mb = _gla_mask(d, RB)
        zq = q_ref[...]
        sq, sf, f, qd, ki, ke, eg, eig, eeg, decs = _gla_features(zq, f_ref[...], lb, d)
        qdb, kib, keb, vb = qd.astype(BF), ki.astype(BF), ke.astype(BF), v_ref[...].astype(BF)
        dob = jnp.where(is_lat, do_ref[...], 0.0).astype(BF)
        order = range(NCK) if d else range(NCK - 1, -1, -1)
        dqd_h, dki_h, dke_h, dv_h, ddec_h = [], [], [], [], []
        for h in range(H):
            hs = slice(h * DH, (h + 1) * DH)
            a = jnp.where(mb, _dot_nt(qdb[:, hs], kib[:, hs]), 0.0).astype(BF)
            da = jnp.where(mb, _dot_nt(dob[:, hs], vb[:, hs]), 0.0).astype(BF)
            dqd_i = _dot(da, kib[:, hs])
            dki_h.append(_dot_tn(da, qdb[:, hs]))
            dvi = _dot_tn(a, dob[:, hs])
            dqd, inc = {}, {}
            for c in range(NCK):
                rs = _chunk_rows(c)
                dqd[c] = dqd_i[rs, :] + _dot(dob[rs, hs], st_ref[c, h])
                inc[c] = _dot_tn(dob[rs, hs], qdb[rs, hs])
            dst = dS[h]
            dke, dv, ddec = {}, {}, {}
            for c in order:
                rs = _chunk_rows(c)
                dv[c] = dvi[rs, :] + _dot_nt(keb[rs, hs], dst)
                dke[c] = _dot(vb[rs, hs], dst)
                ddec[c] = _colsum(st_ref[c, h] * dst)
                dst = inc[c] + dst * decs[c][:, hs]
            dS[h] = dst
            cat = lambda t: jnp.concatenate([t[c] for c in range(NCK)], axis=0)
            dqd_h.append(cat(dqd))
            dke_h.append(cat(dke))
            dv_h.append(cat(dv))
            ddec_h.append([ddec[c] for c in range(NCK)])
        lanes = lambda parts: jnp.concatenate(parts, axis=1)
        dqd, dki, dke, dv = lanes(dqd_h), lanes(dki_h), lanes(dke_h), lanes(dv_h)
        dq = dqd * eg
        dk = dki * eig + dke * eeg
        dke_ke = dke * ke
        dg = dqd * qd - dki * ki - dke_ke
        dgl = [_colsum(dke_ke[_chunk_rows(c), :]) + lanes([ddec_h[h][c] for h in range(H)]) * decs[c]
               for c in range(NCK)]
        dglb = jnp.concatenate([jnp.broadcast_to(t, (CH, D)) for t in dgl], axis=0)
        df = (_chunk_cumsum(dg, 1 - d) + dglb) / f - dk
        dzf = df * (1.0 - lb) * (sf * (1.0 - sf))
        sum_ref[0:1, :] += _colsum(dzf)
        sum_ref[1:2, :] += _colsum(df * (1.0 - sf))
        if last:
            dz0 = (dq + pq_ref[...]) * (Q_SCALE * (sq * (1.0 + zq * (1.0 - sq))))
            dz3 = dv + pv_ref[...]
            sum_ref[2:3, :] += _colsum(dz0)
            sum_ref[3:4, :] += _colsum(dz3)
            dz2 = dzf.astype(BF)
            sh0_ref[:, 0:D] = dz0.astype(BF)
            sh0_ref[:, D:2 * D] = dz1_ref[...]
            sh0_ref[:, 2 * D:] = dz2[:, :CUT2]
            sh1_ref[:, 0:D - CUT2] = dz2[:, CUT2:]
            sh1_ref[:, D - CUT2:2 * D - CUT2] = dz3.astype(BF)
            sh1_ref[:, 2 * D - CUT2:] = dz4a_ref[...]
        else:
            o0_ref[...] = dq
            o1_ref[...] = dv
            o2_ref[...] = dzf.astype(BF)

        @pl.when(s == nrb - 1)
        def _():
            sum_ref[1:2, :] = sum_ref[1:2, :] * (lb * (1.0 - lb))

    def zspec(cb):
        return pl.BlockSpec((RB, D), lambda s: (rb(s), cb))

    rowspec = pl.BlockSpec((RB, D), lambda s: (rb(s), 0))
    in_specs = [zspec(0), zspec(1 + d), zspec(3), pl.BlockSpec((2, 2, D), lambda s: (0, 0, 0)),
                pl.BlockSpec((RB, D), lambda s: (jnp.minimum(rb(s), nlb - 1), 0)),
                pl.BlockSpec((NCK, H, DH, DH), lambda s: (rb(s), 0, 0, 0))]
    args = [z, z, z, lbl, do_lat, states]
    sumspec = pl.BlockSpec((8, D), lambda s: (0, 0))
    if last:
        in_specs += [rowspec, rowspec, rowspec, pl.BlockSpec((RB, CUT4), lambda s: (rb(s), 0))]
        args += list(prev)
        shspec = pl.BlockSpec((RB, SHC), lambda s: (rb(s), 0))
        out_specs = (shspec, shspec, sumspec)
        out_shape = (jax.ShapeDtypeStruct((m, SHC), BF), jax.ShapeDtypeStruct((m, SHC), BF))
    else:
        out_specs = (rowspec, rowspec, rowspec, sumspec)
        out_shape = (jax.ShapeDtypeStruct((m, D), F32), jax.ShapeDtypeStruct((m, D), F32),
                     jax.ShapeDtypeStruct((m, D), BF))
    return pl.pallas_call(
        body, grid=(nrb,), in_specs=in_specs, out_specs=out_specs,
        out_shape=out_shape + (jax.ShapeDtypeStruct((8, D), F32),),
        scratch_shapes=[pltpu.VMEM((H, DH, DH), F32)],
        name=f"gla_bwd{d}", compiler_params=_cp(("arbitrary",)))(*args)


def _shift_rows(v, k, r):
    row = lax.broadcasted_iota(jnp.int32, v.shape, 0)
    rolled = pltpu.roll(v, k % r, 0)
    return jnp.where((row >= k) & (row < r + k), rolled, 0.0)


def _conv_fwd(xb, cw_ref, cb_ref, r):
    return (cb_ref[...] + _shift_rows(xb, 1, r) * cw_ref[0:1, :] + xb * cw_ref[1:2, :]
            + _shift_rows(xb, -1, r) * cw_ref[2:3, :] + _shift_rows(xb, -2, r) * cw_ref[3:4, :])


def _window(ref, lo, n, rows):
    parts = []
    if lo < 0:
        parts.append(jnp.zeros((-lo,) + tuple(ref.shape[1:]), F32))
    parts.append(ref[max(lo, 0):min(lo + n, rows)])
    if lo + n > rows:
        parts.append(jnp.zeros((lo + n - rows,) + tuple(ref.shape[1:]), F32))
    return parts[0] if len(parts) == 1 else jnp.concatenate(parts, axis=0)


def _conv_cols(x_ref, cw_ref, cb_ref, r0, n, rows):
    w = _window(x_ref, r0 - 1, n + 3, rows)
    return (cb_ref[...] + w[0:n] * cw_ref[0:1, :] + w[1:n + 1] * cw_ref[1:2, :] + w[2:n + 2] * cw_ref[2:3, :]
            + w[3:n + 3] * cw_ref[3:4, :])


def _softplus_neg(lam):
    y = jnp.exp(-jnp.abs(lam))
    u = 1.0 + y
    tiny = u == 1.0
    l1p = jnp.where(tiny, y, jnp.log(u) * (y / jnp.where(tiny, 1.0, u - 1.0)))
    return jnp.maximum(-lam, 0.0) + l1p


def _gates(xc, wr_ref, br, wi_ref, bi, sp):
    xcb = xc.astype(BF)
    rs, is_ = [], []
    for g in range(H):
        gs = slice(g * DH, (g + 1) * DH)
        rs.append(jnp.dot(xcb[:, gs], wr_ref[g].astype(BF), preferred_element_type=F32))
        is_.append(jnp.dot(xcb[:, gs], wi_ref[g].astype(BF), preferred_element_type=F32))
    r = _sig_tail(jnp.concatenate(rs, axis=1) + br)
    i = _sig(jnp.concatenate(is_, axis=1) + bi)
    log_a = (-RG_C * r) * sp
    a = jnp.exp(log_a)
    t = jnp.tanh(log_a)
    om = (-2.0 * t) / (1.0 - t)
    return r, i, a, om


def _scan_rows(d, nrows, a_s, b_s, h_s, h0):
    nsl = nrows // 8

    def slab(j, h):
        jj = (nsl - 1 - j) if d else j
        r0 = pl.multiple_of(jj * 8, 8)
        for t in (range(7, -1, -1) if d else range(8)):
            h = a_s[pl.ds(r0 + t, 1), :] * h + b_s[pl.ds(r0 + t, 1), :]
            h_s[pl.ds(r0 + t, 1), :] = h
        return h

    return lax.fori_loop(0, nsl, slab, h0)


def _col_of(d, ncols):
    if d:
        return lambda s: ncols - jnp.maximum(s, 1)
    return lambda s: jnp.maximum(s, 1) - 1


def _rglru_fwd(z, cw, cb, wr, br, wi, bi, lam, d, t_lat, t_ctx):
    m = z.shape[0]
    rows = t_lat // GRID_W
    z3 = z.reshape(m // GRID_W, GRID_W, IN_COLS)
    nblk = GRID_W // CB
    blk = _col_of(d, nblk)
    cblk = t_lat // t_ctx
    rc = min(RCH, rows)

    def body(zc_ref, zx_ref, cw_ref, cb_ref, wr_ref, br_ref, wi_ref, bi_ref, lam_ref,
             hx_ref, hpx_ref, ax_ref, hpc_ref, a_s, b_s, h_s, hcar, a3, b3, cin_s):
        s = pl.program_id(0)
        sp = _softplus_neg(lam_ref[d:d + 1, :])
        br_ = br_ref[d:d + 1, :]
        bi_ = bi_ref[d:d + 1, :]

        @pl.when(s == 0)
        def _():
            xc = _conv_fwd(zc_ref[...], cw_ref, cb_ref, t_ctx)
            _, i, a, om = _gates(xc, wr_ref, br_, wi_ref, bi_, sp)
            a_s[...] = a
            b_s[...] = jnp.sqrt(om) * (i * xc)
            h0 = jnp.zeros((1, D), F32)
            hcar[0:1, :] = _scan_rows(d, t_ctx, a_s, b_s, h_s, h0)
            hs = h_s[...]
            row = lax.broadcasted_iota(jnp.int32, (t_ctx, D), 0)
            if d:
                hpc_ref[...] = jnp.where(row == t_ctx - 1, h0, pltpu.roll(hs, t_ctx - 1, 0))
            else:
                hpc_ref[...] = jnp.where(row == 0, h0, pltpu.roll(hs, 1, 0))

        @pl.when(s > 0)
        def _():
            for r0 in range(0, rows, rc):
                xc = _conv_cols(zx_ref, cw_ref, cb_ref, r0, rc, rows).reshape(rc * CB, D)
                _, i, a, om = _gates(xc, wr_ref, br_, wi_ref, bi_, sp)
                a3[r0:r0 + rc] = a.reshape(rc, CB, D)
                ax_ref[r0:r0 + rc] = a.reshape(rc, CB, D)
                b3[r0:r0 + rc] = (jnp.sqrt(om) * (i * xc)).reshape(rc, CB, D)

            def local(t, carry):
                hl, p = carry
                r = (rows - 1 - t) if d else t
                a = a3[r]
                hl = a * hl + b3[r]
                p = a * p
                b3[r] = hl
                a3[r] = p
                return hl, p

            hl, p = lax.fori_loop(0, rows, local, (jnp.zeros((CB, D), F32), jnp.ones((CB, D), F32)))
            cin = hcar[0:1, :]
            for j in (range(CB - 1, -1, -1) if d else range(CB)):
                cin_s[j:j + 1, :] = cin
                cin = hl[j:j + 1, :] + p[j:j + 1, :] * cin
            hcar[0:1, :] = cin
            c_in = cin_s[...]

            def fix(t, prev):
                r = (rows - 1 - t) if d else t
                h = b3[r] + a3[r] * c_in
                hx_ref[r] = h
                hpx_ref[r] = prev
                return h

            lax.fori_loop(0, rows, fix, c_in)

    full = lambda shp: pl.BlockSpec(shp, lambda s: (0,) * len(shp))
    colspec = pl.BlockSpec((rows, CB, D), lambda s: (0, blk(s), 0))
    outs = pl.pallas_call(
        body, grid=(nblk + 1,),
        in_specs=[pl.BlockSpec((t_ctx, D), lambda s: (cblk, 5)),
                  pl.BlockSpec((rows, CB, D), lambda s: (0, blk(s), 5)),
                  full((4, D)), full((1, D)),
                  pl.BlockSpec((None, H, DH, DH), lambda s: (d, 0, 0, 0)), full((2, D)),
                  pl.BlockSpec((None, H, DH, DH), lambda s: (d, 0, 0, 0)), full((2, D)), full((2, D))],
        out_specs=(colspec, colspec, colspec, full((t_ctx, D))),
        out_shape=(jax.ShapeDtypeStruct((rows, GRID_W, D), F32),) * 3 + (jax.ShapeDtypeStruct((t_ctx, D), F32),),
        scratch_shapes=[pltpu.VMEM((t_ctx, D), F32), pltpu.VMEM((t_ctx, D), F32), pltpu.VMEM((t_ctx, D), F32),
                        pltpu.VMEM((8, D), F32), pltpu.VMEM((rows, CB, D), F32), pltpu.VMEM((rows, CB, D), F32),
                        pltpu.VMEM((CB, D), F32)],
        name=f"rglru_fwd{d}", compiler_params=_cp(("arbitrary",)))(z, z3, cw, cb, wr, br, wi, bi, lam)
    return outs[0].reshape(t_lat, D), outs[1].reshape(t_lat, D), outs[2].reshape(t_lat, D), outs[3]


def _rglru_bwd(z, cw, cb, wr, br, wi, bi, lam, dh_lat, hp_lat, a_lat, hp_ctx, d, t_lat, t_ctx, prev=None):
    m = z.shape[0]
    rows = t_lat // GRID_W
    z3 = z.reshape(m // GRID_W, GRID_W, IN_COLS)
    nblk = GRID_W // CB
    fblk = _col_of(d, nblk)
    blk = lambda s: fblk(nblk - jnp.minimum(s, nblk - 1))
    cblk = t_lat // t_ctx
    rc = min(RCH, rows)
    last = prev is not None
    view3 = lambda v: v.reshape(rows, GRID_W, D)

    def body(*refs):
        (zc_ref, zx_ref, cw_ref, cb_ref, wr_ref, br_ref, wi_ref, bi_ref, lam_ref, dh_ref, hpx_ref, ax_ref,
         hpc_ref) = refs[:13]
        k = 13
        if last:
            pdx_ref, pdc_ref = refs[13:15]
            k = 15
        ox_ref, oc_ref, dwr_ref, dwi_ref, sum_ref, a_s, b_s, gcar, a3, b3, cin_s = refs[k:]
        s = pl.program_id(0)
        lam_d = lam_ref[d:d + 1, :]
        sp = _softplus_neg(lam_d)
        br_ = br_ref[d:d + 1, :]
        bi_ = bi_ref[d:d + 1, :]
        flat = lambda v: v.reshape(-1, D)

        @pl.when(s == 0)
        def _():
            gcar[...] = jnp.zeros_like(gcar)
            dwr_ref[...] = jnp.zeros_like(dwr_ref)
            dwi_ref[...] = jnp.zeros_like(dwi_ref)
            sum_ref[...] = jnp.zeros_like(sum_ref)


        def conv_sums(dxc, dxb, xm1, x0, xp1, xp2):
            sum_ref[3:4, :] += _colsum(flat(dxc))
            sum_ref[4:5, :] += _colsum(flat(dxb))
            sum_ref[8:9, :] += _colsum(flat(dxc * xm1))
            sum_ref[9:10, :] += _colsum(flat(dxc * x0))
            sum_ref[10:11, :] += _colsum(flat(dxc * xp1))
            sum_ref[11:12, :] += _colsum(flat(dxc * xp2))

        def gate_grads(g, hp, xc, rr, i, a, om):
            mult = jnp.sqrt(om)
            da = g * hp
            ixc = i * xc
            dmult = g * ixc
            dixc = g * mult
            di = dixc * xc
            dxc = dixc * i
            dlog_a = da * a - dmult * ((1.0 - om) / mult)
            dr = dlog_a * (-RG_C * sp)
            sum_ref[2:3, :] += _colsum(dlog_a * rr)
            drp = dr * rr * (1.0 - rr)
            dip = di * i * (1.0 - i)
            sum_ref[0:1, :] += _colsum(drp)
            sum_ref[1:2, :] += _colsum(dip)
            xcb = xc.astype(BF)
            drb = drp.astype(BF)
            dib = dip.astype(BF)
            parts = []
            for gi in range(H):
                gs = slice(gi * DH, (gi + 1) * DH)
                parts.append(_dot_nt(drb[:, gs], wr_ref[gi]) + _dot_nt(dib[:, gs], wi_ref[gi]))
                dwr_ref[gi] += _dot_tn(xcb[:, gs], drb[:, gs])
                dwi_ref[gi] += _dot_tn(xcb[:, gs], dib[:, gs])
            return dxc + jnp.concatenate(parts, axis=1)

        @pl.when(s < nblk)
        def _():
            def local(t, carry):
                c, q = carry
                r = t if d else (rows - 1 - t)
                a = ax_ref[r]
                c = a * (c + dh_ref[r])
                q = a * q
                b3[r] = c
                a3[r] = q
                return c, q

            c, q = lax.fori_loop(0, rows, local, (jnp.zeros((CB, D), F32), jnp.ones((CB, D), F32)))
            cin = gcar[0:1, :]
            for j in (range(CB) if d else range(CB - 1, -1, -1)):
                cin_s[j:j + 1, :] = cin
                cin = c[j:j + 1, :] + q[j:j + 1, :] * cin
            gcar[0:1, :] = cin
            c_in = cin_s[...]
            for r0 in (range(rows - rc, -1, -rc) if d else range(0, rows, rc)):
                if d:
                    lo = max(r0 - 1, 0)
                    cn = b3[lo:r0 + rc - 1] + a3[lo:r0 + rc - 1] * c_in
                    if r0 == 0:
                        cn = jnp.concatenate([c_in[None], cn], axis=0)
                else:
                    hi = min(r0 + rc + 1, rows)
                    cn = b3[r0 + 1:hi] + a3[r0 + 1:hi] * c_in
                    if hi == rows:
                        cn = jnp.concatenate([cn, c_in[None]], axis=0)
                g = flat(dh_ref[r0:r0 + rc] + cn)
                xc = flat(_conv_cols(zx_ref, cw_ref, cb_ref, r0, rc, rows))
                rr, i, a, om = _gates(xc, wr_ref, br_, wi_ref, bi_, sp)
                b3[r0:r0 + rc] = gate_grads(g, flat(hpx_ref[r0:r0 + rc]), xc, rr, i, a, om).reshape(rc, CB, D)
            if last:
                for r0 in range(0, rows, rc):
                    b3[r0:r0 + rc] = b3[r0:r0 + rc] + pdx_ref[r0:r0 + rc]
                for r0 in range(0, rows, rc):
                    w = _window(b3, r0 - 2, rc + 3, rows)
                    xw = _window(zx_ref, r0 - 1, rc + 3, rows)
                    dxc = w[2:rc + 2]
                    dxb = (w[3:rc + 3] * cw_ref[0:1, :] + dxc * cw_ref[1:2, :] + w[1:rc + 1] * cw_ref[2:3, :]
                           + w[0:rc] * cw_ref[3:4, :])
                    conv_sums(dxc, dxb, xw[0:rc], xw[1:rc + 1], xw[2:rc + 2], xw[3:rc + 3])
                    ox_ref[r0:r0 + rc] = dxb
            else:
                for r0 in range(0, rows, rc):
                    ox_ref[r0:r0 + rc] = b3[r0:r0 + rc]

        @pl.when(s == nblk)
        def _():
            r = t_ctx
            xb = zc_ref[...]
            xc = _conv_fwd(xb, cw_ref, cb_ref, r)
            rr, i, a, om = _gates(xc, wr_ref, br_, wi_ref, bi_, sp)
            a_s[...] = a
            b_s[...] = jnp.zeros((r, D), F32)
            c0 = gcar[0:1, :]
            _scan_rows(1 - d, r, a_s, b_s, b_s, c0)
            cs = b_s[...]
            row = lax.broadcasted_iota(jnp.int32, (r, D), 0)
            if d:
                g = jnp.where(row == 0, c0, pltpu.roll(cs, 1, 0))
            else:
                g = jnp.where(row == r - 1, c0, pltpu.roll(cs, r - 1, 0))
            dxc = gate_grads(g, hpc_ref[...], xc, rr, i, a, om)
            if last:
                dxc = dxc + pdc_ref[...]
                dxb = (_shift_rows(dxc, -1, r) * cw_ref[0:1, :] + dxc * cw_ref[1:2, :]
                       + _shift_rows(dxc, 1, r) * cw_ref[2:3, :] + _shift_rows(dxc, 2, r) * cw_ref[3:4, :])
                conv_sums(dxc, dxb, _shift_rows(xb, 1, r), xb, _shift_rows(xb, -1, r), _shift_rows(xb, -2, r))
                oc_ref[...] = dxb
            else:
                oc_ref[...] = dxc
            sum_ref[2:3, :] = sum_ref[2:3, :] * (RG_C * _sig_tail(-lam_d))

    full = lambda shp: pl.BlockSpec(shp, lambda s: (0,) * len(shp))
    once = lambda shp: pl.BlockSpec(shp, lambda s: (0,) * len(shp), pipeline_mode=pl.Buffered(1))
    colspec = pl.BlockSpec((rows, CB, D), lambda s: (0, blk(s), 0))
    colonce = pl.BlockSpec((rows, CB, D), lambda s: (0, blk(s), 0), pipeline_mode=pl.Buffered(1))
    in_specs = [pl.BlockSpec((t_ctx, D), lambda s: (cblk, 5), pipeline_mode=pl.Buffered(1)),
                pl.BlockSpec((rows, CB, D), lambda s: (0, blk(s), 5), pipeline_mode=pl.Buffered(1)),
                full((4, D)), full((1, D)),
                pl.BlockSpec((None, H, DH, DH), lambda s: (d, 0, 0, 0)), full((2, D)),
                pl.BlockSpec((None, H, DH, DH), lambda s: (d, 0, 0, 0)), full((2, D)), full((2, D)),
                colspec, colspec, colonce, once((t_ctx, D))]
    args = [z, z3, cw, cb, wr, br, wi, bi, lam, view3(dh_lat), view3(hp_lat), view3(a_lat), hp_ctx]
    if last:
        in_specs += [colonce, once((t_ctx, D))]
        args += [view3(prev[0]), prev[1]]
    outs = pl.pallas_call(
        body, grid=(nblk + 1,), in_specs=in_specs,
        out_specs=(colspec, full((t_ctx, D)), full((H, DH, DH)), full((H, DH, DH)), full((16, D))),
        out_shape=(jax.ShapeDtypeStruct((rows, GRID_W, D), F32), jax.ShapeDtypeStruct((t_ctx, D), F32),
                   jax.ShapeDtypeStruct((H, DH, DH), F32), jax.ShapeDtypeStruct((H, DH, DH), F32),
                   jax.ShapeDtypeStruct((16, D), F32)),
        scratch_shapes=[pltpu.VMEM((t_ctx, D), F32), pltpu.VMEM((t_ctx, D), F32), pltpu.VMEM((8, D), F32),
                        pltpu.VMEM((rows, CB, D), F32), pltpu.VMEM((rows, CB, D), F32), pltpu.VMEM((CB, D), F32)],
        name=f"rglru_bwd{d}", compiler_params=_cp(("arbitrary",), VMEM_LIMIT_RGLRU_BWD))(*args)
    return (outs[0].reshape(t_lat, D), outs[1]) + tuple(outs[2:])


def _merge(o_f, o_b, h_f, h_b, z, x, tgt, mod, norm_g, ln_g, ln_b, p_a, p_b, w_out, t_lat):
    tm = 256
    nt = t_lat // tm

    def body(of_ref, ob_ref, hf_ref, hb_ref, z4_ref, z6_ref, z7_ref, z8_ref, x_ref, t_ref, mod_ref, ng_ref,
             lg_ref, lb_ref, pa_ref, pb_ref, wo_ref,
             do_ref, dh_ref, dz4a_ref, dz4b_ref, dz6a_ref, sh3_ref, gx_ref,
             y_ref, dout_ref, oa_ref, dpa_ref, obv_ref, dpb_ref, acc_ref):
        i = pl.program_id(0)
        lat = i < nt
        latf = lat.astype(F32)

        @pl.when(i == 0)
        def _():
            acc_ref[...] = jnp.zeros_like(acc_ref)

        def per_head(v):
            return jnp.concatenate(
                [jnp.broadcast_to(jnp.mean(v[:, h * DH:(h + 1) * DH], axis=-1, keepdims=True), (tm, DH))
                 for h in range(H)], axis=1)

        gt = mod_ref[0:1, 2 * D:3 * D]
        gfull = jnp.concatenate([ng_ref[...]] * H, axis=1)
        o = of_ref[...] + ob_ref[...]
        rinv = lax.rsqrt(per_head(o * o) + RMS_EPS)
        n = o * rinv
        na = n * gfull
        z4 = z4_ref[...]
        s4 = _sig(z4)
        silu4 = z4 * s4
        oa = na * silu4
        z6 = z6_ref[...]
        s6 = _sig(z6)
        silu6 = z6 * s6
        hsum = hf_ref[...] + hb_ref[...]
        obv = hsum * silu6
        pa = _dot(oa, pa_ref[...])
        pb = _dot(obv, pb_ref[...])
        s7 = _sig(z7_ref[...])
        s8 = _sig(z8_ref[...])
        y = s7 * pa + s8 * pb
        out = _dot(y, wo_ref[...])
        pre = ALPHA * x_ref[...] + gt * out
        mu = jnp.mean(pre, axis=-1, keepdims=True)
        xc = pre - mu
        rstd = lax.rsqrt(jnp.mean(xc * xc, axis=-1, keepdims=True) + LN_EPS)
        xhat = xc * rstd
        lg = lg_ref[...]
        diff = xhat * lg + lb_ref[...] - t_ref[...]
        acc_ref[8:9, :] += _colsum(diff * diff) * (0.5 / D * latf)
        dxn = diff * (1.0 / D)
        acc_ref[1:2, :] += _colsum(dxn * xhat) * latf
        acc_ref[2:3, :] += _colsum(dxn) * latf
        dxhat = dxn * lg
        dpre = rstd * (dxhat - jnp.mean(dxhat, axis=-1, keepdims=True)
                       - xhat * jnp.mean(dxhat * xhat, axis=-1, keepdims=True))
        gx_ref[...] = ALPHA * dpre
        acc_ref[0:1, :] += _colsum(dpre * out) * latf
        dout = dpre * gt
        dy = _dot_nt(dout, wo_ref[...])
        dpa = dy * s7
        dpb = dy * s8
        dz7 = dy * pa * (s7 * (1.0 - s7))
        dz8 = dy * pb * (s8 * (1.0 - s8))
        doa = _dot_nt(dpa, pa_ref[...])
        dob = _dot_nt(dpb, pb_ref[...])
        dh_ref[...] = dob * silu6
        dz6 = dob * hsum * (s6 * (1.0 + z6 * (1.0 - s6)))
        dna = doa * silu4
        dz4 = doa * na * (s4 * (1.0 + z4 * (1.0 - s4)))
        dng = _colsum(dna * n)
        acc_ref[7:8, 0:DH] += sum(dng[:, h * DH:(h + 1) * DH] for h in range(H)) * latf
        dn = dna * gfull
        do_ref[...] = rinv * (dn - n * per_head(dn * n))
        acc_ref[3:4, :] += _colsum(dz4) * latf
        acc_ref[4:5, :] += _colsum(dz6) * latf
        acc_ref[5:6, :] += _colsum(dz7) * latf
        acc_ref[6:7, :] += _colsum(dz8) * latf
        dz4b, dz6b = (dz4 * latf).astype(BF), (dz6 * latf).astype(BF)
        dz4a_ref[...] = dz4b[:, :CUT4]
        dz4b_ref[...] = dz4b[:, CUT4:]
        dz6a_ref[...] = dz6b[:, :CUT6]
        sh3_ref[:, 0:D - CUT6] = dz6b[:, CUT6:]
        sh3_ref[:, D - CUT6:2 * D - CUT6] = (dz7 * latf).astype(BF)
        sh3_ref[:, 2 * D - CUT6:] = (dz8 * latf).astype(BF)
        y_ref[...] = y.astype(BF)
        dout_ref[...] = dout.astype(BF)
        oa_ref[...] = oa.astype(BF)
        dpa_ref[...] = dpa.astype(BF)
        obv_ref[...] = obv.astype(BF)
        dpb_ref[...] = dpb.astype(BF)

        @pl.when(i == nt - 1)
        def _():
            acc_ref[9:10, :] = jnp.broadcast_to(jnp.sum(acc_ref[8:9, :], axis=-1, keepdims=True), (1, D))

    m = z.shape[0]
    lrow = lambda i: jnp.minimum(i, nt - 1)
    row = pl.BlockSpec((tm, D), lambda i: (lrow(i), 0))
    allrow = lambda cols: pl.BlockSpec((tm, cols), lambda i: (i, 0))
    zs = lambda cb: pl.BlockSpec((tm, D), lambda i: (lrow(i), cb))
    full = lambda shp: pl.BlockSpec(shp, lambda i: (0,) * len(shp))
    wfull = pl.BlockSpec((D, D), lambda i: (0, 0), pipeline_mode=pl.Buffered(1))
    f32o = jax.ShapeDtypeStruct((t_lat, D), F32)
    bfo = jax.ShapeDtypeStruct((t_lat, D), BF)
    bfall = lambda cols: jax.ShapeDtypeStruct((m, cols), BF)
    return pl.pallas_call(
        body, grid=(m // tm,),
        in_specs=[row, row, row, row, zs(4), zs(6), zs(7), zs(8), row, row, full((16, 3 * D)), full((1, DH)),
                  full((1, D)), full((1, D)), wfull, wfull, wfull],
        out_specs=(row, row) + tuple(allrow(c) for c in (CUT4, D - CUT4, CUT6, SHC)) + (row,) * 7 + (full((16, D)),),
        out_shape=(f32o, f32o, bfall(CUT4), bfall(D - CUT4), bfall(CUT6), bfall(SHC), f32o, bfo, bfo, bfo, bfo, bfo, bfo,
                   jax.ShapeDtypeStruct((16, D), F32)),
        name="merge", compiler_params=_cp(("arbitrary",), VMEM_LIMIT_MERGE))(
            o_f, o_b, h_f, h_b, z, z, z, z, x, tgt, mod, norm_g, ln_g, ln_b, p_a, p_b, w_out)


def _wgrad(a, b, name):
    tm = 1024

    def body(a_ref, b_ref, o_ref):
        @pl.when(pl.program_id(0) == 0)
        def _():
            o_ref[...] = jnp.zeros_like(o_ref)
        o_ref[...] += _dot_tn(a_ref[...], b_ref[...])

    row = pl.BlockSpec((tm, D), lambda i: (i, 0))
    return pl.pallas_call(body, grid=(a.shape[0] // tm,), in_specs=[row, row],
                          out_specs=pl.BlockSpec((D, D), lambda i: (0, 0)),
                          out_shape=jax.ShapeDtypeStruct((D, D), F32), name=name,
                          compiler_params=_cp(("arbitrary",)))(a, b)


def _pack_shard2(dz4b, dz5_lat, dz5_ctx, dz6a):
    m, t_lat, t_ctx = dz4b.shape[0], dz5_lat.shape[0], dz5_ctx.shape[0]
    tm = t_ctx
    nlt = t_lat // tm
    w4 = D - CUT4

    def body(a_ref, bl_ref, bc_ref, c_ref, o_ref):
        i = pl.program_id(0)
        o_ref[:, 0:w4] = a_ref[...]
        o_ref[:, w4:w4 + D] = jnp.where(i < nlt, bl_ref[...], bc_ref[...]).astype(BF)
        o_ref[:, w4 + D:] = c_ref[...]

    return pl.pallas_call(
        body, grid=(m // tm,),
        in_specs=[pl.BlockSpec((tm, w4), lambda i: (i, 0)),
                  pl.BlockSpec((tm, D), lambda i: (jnp.minimum(i, nlt - 1), 0)),
                  pl.BlockSpec((tm, D), lambda i: (0, 0)),
                  pl.BlockSpec((tm, CUT6), lambda i: (i, 0))],
        out_specs=pl.BlockSpec((tm, SHC), lambda i: (i, 0)),
        out_shape=jax.ShapeDtypeStruct((m, SHC), BF), name="pack_shard2",
        compiler_params=_cp(("arbitrary",)))(dz4b, dz5_lat, dz5_ctx, dz6a)


def _wgrad_in(u_all, dz_shards):
    m = u_all.shape[1]
    assert m % (6 * 128) == 0
    tm = m // 6
    out = None
    for k, dz_k in enumerate(dz_shards):
        def body(u_ref, dz_ref, *rest):
            o_ref = rest[-1]

            @pl.when(pl.program_id(0) == 0)
            def _():
                o_ref[...] = jnp.zeros_like(o_ref)
            o_ref[0] += _dot(u_ref[...], dz_ref[...])

        out = pl.pallas_call(
            body, grid=(m // tm,),
            in_specs=[pl.BlockSpec((D, tm), lambda i: (0, i)), pl.BlockSpec((tm, SHC), lambda i: (i, 0))]
            + ([] if out is None else [_ANY]),
            out_specs=pl.BlockSpec((1, D, SHC), lambda i, k=k: (k, 0, 0)),
            out_shape=jax.ShapeDtypeStruct((NSH, D, SHC), F32),
            input_output_aliases={} if out is None else {2: 0},
            name=f"wgrad_in{k}", compiler_params=_cp(("arbitrary",)))(u_all, dz_k, *(() if out is None else (out,)))
    return out


def _du(dz_shards, w_in_g, x, ctx, mod, gxres, sums=()):
    tm = 256
    n_lat_tiles, nct = x.shape[0] // tm, ctx.shape[0] // tm
    nt = n_lat_tiles + nct
    ns = len(sums)
    rblk = lambda i: jnp.where(i < nct, n_lat_tiles + i, i - nct)
    lblk = lambda i: jnp.maximum(i - nct, 0)

    def body(*refs):
        dz_refs, refs = refs[:NSH], refs[NSH:]
        w_ref, x_ref, c_ref, mod_ref, gr_ref = refs[:5]
        sum_refs = refs[5:5 + ns]
        gx_ref, dm_ref = refs[5 + ns:7 + ns]
        got_refs = refs[7 + ns:7 + 2 * ns]
        sems = refs[7 + 2 * ns:]
        i = pl.program_id(0)
        is_lat = i >= nct

        @pl.when(i == 0)
        def _():
            dm_ref[...] = jnp.zeros_like(dm_ref)
            if ns:
                for cp in _rs_chip_copies(sum_refs, got_refs, *sems):
                    cp.start()

        du = _dot_nt(dz_refs[0][...], w_ref[0])
        for n in range(1, NSH):
            du = du + _dot_nt(dz_refs[n][...], w_ref[n])
        sc = jnp.where(is_lat, mod_ref[0:1, D:2 * D], mod_ref[1:2, D:2 * D])
        dsh = _colsum(du)
        dsc = _colsum(du * jnp.where(is_lat, x_ref[...], c_ref[...]))

        @pl.when(is_lat)
        def _():
            gx_ref[...] = du * (1.0 + sc) + gr_ref[...]
            dm_ref[0:1, 0:D] += dsh
            dm_ref[0:1, D:2 * D] += dsc

        @pl.when(jnp.logical_not(is_lat))
        def _():
            dm_ref[1:2, 0:D] += dsh
            dm_ref[1:2, D:2 * D] += dsc

        if ns:
            @pl.when(i == nt - 1)
            def _():
                for cp in _rs_chip_copies(sum_refs, got_refs, *sems):
                    cp.wait()

    outs = pl.pallas_call(
        body, grid=(nt,),
        in_specs=[pl.BlockSpec((tm, SHC), lambda i: (rblk(i), 0))] * NSH + [
                  pl.BlockSpec((NSH, D, SHC), lambda i: (0, 0, 0), pipeline_mode=pl.Buffered(1)),
                  pl.BlockSpec((tm, D), lambda i: (lblk(i), 0)),
                  pl.BlockSpec((tm, D), lambda i: (jnp.minimum(i, nct - 1), 0)),
                  pl.BlockSpec((16, 3 * D), lambda i: (0, 0)),
                  pl.BlockSpec((tm, D), lambda i: (lblk(i), 0))] + [_ANY] * ns,
        out_specs=[pl.BlockSpec((tm, D), lambda i: (lblk(i), 0)),
                   pl.BlockSpec((8, 2 * D), lambda i: (0, 0))] + [_ANY] * ns,
        out_shape=[jax.ShapeDtypeStruct((n_lat_tiles * tm, D), F32), jax.ShapeDtypeStruct((8, 2 * D), F32)]
        + [jax.ShapeDtypeStruct((3,) + g.shape[1:], g.dtype) for g in sums],
        scratch_shapes=[pltpu.SemaphoreType.DMA((3 * ns,)), pltpu.SemaphoreType.DMA((3 * ns,))] if ns else [],
        name="du", compiler_params=_cp(("arbitrary",)))(*dz_shards, w_in_g, x, ctx, mod, gxres, *sums)
    return outs[0], outs[1], list(outs[2:])


def _row_tile(rows, cols, block_bytes=1 << 20):
    t = 8
    while t * 2 * cols * 4 <= block_bytes and rows % (t * 2) == 0:
        t *= 2
    return t


def _adamw_update(w_ref, g_ref, m_ref, v_ref, d_ref, nm_ref, nv_ref):
    gg = g_ref[...]
    m2 = ADAM_B1 * m_ref[...] + (1.0 - ADAM_B1) * gg
    v2 = ADAM_B2 * v_ref[...] + (1.0 - ADAM_B2) * (gg * gg)
    m_hat = m2 / (1.0 - ADAM_B1 ** ADAM_STEP)
    v_hat = v2 / (1.0 - ADAM_B2 ** ADAM_STEP)
    d_ref[...] = -ADAM_LR * (m_hat / (jnp.sqrt(v_hat) + ADAM_EPS) + ADAM_WD * w_ref[...])
    nm_ref[...] = m2
    nv_ref[...] = v2


def _adamw_many(ws, gs, ms, vs):
    n = len(ws)

    def body(*refs):
        for j in range(n):
            _adamw_update(*refs[4 * j:4 * j + 4], *refs[4 * n + 3 * j:4 * n + 3 * j + 3])

    args = [a for quad in zip(ws, gs, ms, vs) for a in quad]
    outs = pl.pallas_call(body, out_shape=[jax.ShapeDtypeStruct(w.shape, F32) for w in ws for _ in range(3)],
                          name="adamw_small", compiler_params=_cp())(*args)
    return outs[0::3], outs[1::3], outs[2::3]


def _adamw(w, g, m, v, name):
    rows, cols = w.shape
    tr = _row_tile(rows, cols, 2 << 20)

    def body(*refs):
        _adamw_update(*refs)

    spec = pl.BlockSpec((tr, cols), lambda i: (i, 0))
    o = jax.ShapeDtypeStruct((rows, cols), F32)
    return pl.pallas_call(body, grid=(rows // tr,), in_specs=[spec] * 4, out_specs=(spec,) * 3,
                          out_shape=(o, o, o), name=name, compiler_params=_cp(("arbitrary",)))(w, g, m, v)


_ANY = pl.BlockSpec(memory_space=pl.ANY)


def _place():
    return lax.axis_index("x"), lax.axis_index("y"), lax.axis_index("c")


def _ag_copies(ins, outs, send, recv, fsend, frecv, lsem):
    x, y, c = _place()
    me = 2 * x + y
    chips = ((1 - x, y), (x, 1 - y), (1 - x, 1 - y))
    local, chip, hand = [], [], []
    for j in range(len(ins)):
        hr = ins[j].shape[0] // 2
        half = pl.ds(pl.multiple_of(c * hr, 8), hr)
        local.append(pltpu.make_async_copy(ins[j], outs[j].at[me], lsem.at[j]))
        for k, (px, py) in enumerate(chips):
            chip.append(pltpu.make_async_remote_copy(
                src_ref=ins[j].at[half, :], dst_ref=outs[j].at[me, half, :], send_sem=send.at[3 * j + k],
                recv_sem=recv.at[3 * j + k], device_id=(px, py, c), device_id_type=MESH))
            got = outs[j].at[2 * px + py, half, :]
            hand.append(pltpu.make_async_remote_copy(
                src_ref=got, dst_ref=got, send_sem=fsend.at[3 * j + k], recv_sem=frecv.at[3 * j + k],
                device_id=(x, y, 1 - c), device_id_type=MESH))
    return local, chip, hand


def _ag_sems(n):
    return [pltpu.SemaphoreType.DMA((3 * n,))] * 4 + [pltpu.SemaphoreType.DMA((n,))]


def _ag_finish(local, chip, hand, done=0):
    for k in range(done, len(chip)):
        chip[k].wait_recv()
        hand[k].start()
    for cp in chip:
        cp.wait_send()
    for k in range(done):
        hand[k].wait_send()
    for k in range(done, len(chip)):
        hand[k].wait_send()
        hand[k].wait_recv()
    for cp in local:
        cp.wait()


def _mod_tp(c8, c_ctx, w_mod_sh, b_mod_sh):
    mc = w_mod_sh.shape[1]

    def body(c8_ref, cctx_ref, w_ref, b_ref, mod_ref, cc_ref, cc_s, part_s, send1, recv1, send3, recv3):
        x, y, c = _place()
        me = 4 * x + 2 * y + c
        ms = 2 * x + y
        copies = []
        for k in range(1, 8):
            peer = (x ^ ((k >> 2) & 1), y ^ ((k >> 1) & 1), c ^ (k & 1))
            cp = pltpu.make_async_remote_copy(src_ref=c8_ref, dst_ref=cc_s.at[me], send_sem=send1.at[k],
                                              recv_sem=recv1.at[k], device_id=peer, device_id_type=MESH)
            cp.start()
            copies.append(cp)
        cc_s[me] = c8_ref[...]
        for cp in copies:
            cp.wait()
        cc_ref[...] = jnp.zeros_like(cc_ref)
        for j in range(8):
            cc_ref[j:j + 1, :] = cc_s[j, 0:1, :]
        cc_ref[8:9, :] = cctx_ref[...]
        v = cc_ref[...]
        part_s[ms] = _dot(v * _sig(v), w_ref[...]) + b_ref[...]
        copies = []
        for k in range(1, 4):
            peer = (x ^ ((k >> 1) & 1), y ^ (k & 1), c)
            cp = pltpu.make_async_remote_copy(src_ref=part_s.at[ms], dst_ref=part_s.at[ms], send_sem=send3.at[k],
                                              recv_sem=recv3.at[k], device_id=peer, device_id_type=MESH)
            cp.start()
            copies.append(cp)
        for cp in copies:
            cp.wait()
        for s in range(NSH):
            mod_ref[:, s * mc:(s + 1) * mc] = part_s[s]

    vm = pl.BlockSpec(memory_space=pltpu.VMEM)
    return pl.pallas_call(
        body, in_specs=[vm] * 4, out_specs=(vm, vm),
        out_shape=(jax.ShapeDtypeStruct((16, NSH * mc), F32), jax.ShapeDtypeStruct((16, D), F32)),
        scratch_shapes=[pltpu.VMEM((8, 8, D), F32), pltpu.VMEM((NSH, 16, mc), F32),
                        pltpu.SemaphoreType.DMA((8,)), pltpu.SemaphoreType.DMA((8,)),
                        pltpu.SemaphoreType.DMA((4,)), pltpu.SemaphoreType.DMA((4,))],
        name="mod_tp", compiler_params=_cp())(c8, c_ctx, w_mod_sh, b_mod_sh)


def _inproj_ag(x, ctx, mod, w_in_sh, b_in, narrow_sh):
    n_lat = x.shape[0]
    m = n_lat + ctx.shape[0]
    assert m % (11 * 128) == 0
    tm = m // 11
    nt = m // tm
    nfull = n_lat // tm
    assert nfull == nt - 1
    tail = jnp.concatenate([x[nfull * tm:], ctx], axis=0)
    x_, y_ = lax.axis_index("x"), lax.axis_index("y")
    sids = jnp.stack([2 * x_ + y_, 2 * (1 - x_) + y_, 2 * x_ + 1 - y_, 2 * (1 - x_) + 1 - y_]).astype(jnp.int32)

    def body(sid_ref, x_ref, tail_ref, mod_ref, b_ref, wsh_ref, nsh_ref, z_ref, u_ref, wg_ref, ng_ref, w_s, u_s, *sems):
        n = pl.program_id(0)
        i = pl.program_id(1)
        rows = pl.ds(pl.multiple_of(i * tm, tm), tm)
        ag = ((wsh_ref, nsh_ref), (wg_ref, ng_ref)) + tuple(sems[:5])
        wsem = sems[5]

        def load(src):
            cp = pltpu.make_async_copy(src, w_s, wsem)
            cp.start()
            cp.wait()

        @pl.when((n == 0) & (i == 0))
        def _():
            local, chip, _ = _ag_copies(*ag)
            for cp in chip + local:
                cp.start()
            load(wsh_ref)

        for k in range(NSH - 1):
            @pl.when((n == k + 1) & (i == 0))
            def _():
                _, chip, hand = _ag_copies(*ag)
                chip[k].wait_recv()
                hand[k].start()
                hand[k].wait_recv()
                load(wg_ref.at[sid_ref[k + 1]])

        @pl.when(n == 0)
        def _():
            is_lat = (i * tm + lax.broadcasted_iota(jnp.int32, (tm, 1), 0)) < n_lat
            u = _modulate(jnp.where(i < nfull, x_ref[...], tail_ref[...]), mod_ref, is_lat)
            u_s[rows, :] = u.astype(BF)
            u_ref[...] = u.T.astype(BF)

        z_ref[...] = _dot(u_s[rows, :], w_s[...]) + b_ref[...]

        @pl.when((n == NSH - 1) & (i == nt - 1))
        def _():
            _ag_finish(*_ag_copies(*ag), done=NSH - 1)

    first = lambda n, i: jnp.where(n == 0, i, nt - 1)
    outs = pl.pallas_call(
        body, grid_spec=pltpu.PrefetchScalarGridSpec(
            num_scalar_prefetch=1, grid=(NSH, nt),
            in_specs=[pl.BlockSpec((tm, D), lambda n, i, sid: (jnp.minimum(first(n, i), nfull - 1), 0)),
                      pl.BlockSpec((tm, D), lambda n, i, sid: (0, 0)),
                      pl.BlockSpec((16, 3 * D), lambda n, i, sid: (0, 0)),
                      pl.BlockSpec((1, SHC), lambda n, i, sid: (0, sid[n])), _ANY, _ANY],
            out_specs=[pl.BlockSpec((tm, SHC), lambda n, i, sid: (i, sid[n])),
                       pl.BlockSpec((D, tm), lambda n, i, sid: (0, first(n, i))), _ANY, _ANY],
            scratch_shapes=[pltpu.VMEM((D, SHC), BF), pltpu.VMEM((m, D), BF)] + _ag_sems(2)
            + [pltpu.SemaphoreType.DMA]),
        out_shape=[jax.ShapeDtypeStruct((m, IN_COLS), F32), jax.ShapeDtypeStruct((D, m), BF),
                   jax.ShapeDtypeStruct((NSH,) + w_in_sh.shape, BF),
                   jax.ShapeDtypeStruct((NSH,) + narrow_sh.shape, narrow_sh.dtype)],
        name="inproj_ag", compiler_params=_cp(("arbitrary", "arbitrary")))(sids, x, tail, mod, b_in, w_in_sh, narrow_sh)
    return outs


def _rs_sibling(grads, name):
    n = len(grads)

    def body(*refs):
        ins, got = refs[:n], refs[n:2 * n]
        send, recv = refs[2 * n:]
        x, y, c = _place()
        copies = []
        for j in range(n):
            hr = ins[j].shape[1] // 2
            for s in range(NSH):
                give = ins[j].at[s, pl.ds(pl.multiple_of((1 - c) * hr, 8), hr), :]
                cp = pltpu.make_async_remote_copy(src_ref=give, dst_ref=got[j].at[s], send_sem=send.at[NSH * j + s],
                                                  recv_sem=recv.at[NSH * j + s], device_id=(x, y, 1 - c),
                                                  device_id_type=MESH)
                cp.start()
                copies.append(cp)
        for cp in copies:
            cp.wait()

    half = [jax.ShapeDtypeStruct((NSH, g.shape[1] // 2, g.shape[2]), F32) for g in grads]
    return pl.pallas_call(
        body, in_specs=[_ANY] * n, out_specs=[_ANY] * n, out_shape=half,
        scratch_shapes=[pltpu.SemaphoreType.DMA((NSH * n,)), pltpu.SemaphoreType.DMA((NSH * n,))],
        name=name)(*grads)


def _core_vec():
    return lax.axis_index("c").astype(jnp.int32).reshape(1)


def _rs_add1(g, got, name):
    _, r, cols = g.shape
    hr = r // 2
    tr = _row_tile(hr, cols, 4 << 20)
    nb = hr // tr

    def body(c_ref, g_ref, got_ref, o_ref):
        o_ref[...] = (g_ref[...] + got_ref[...]).astype(BF)

    spec = pl.BlockSpec((1, tr, cols), lambda s, i, c_ref: (s, i, 0))
    return pl.pallas_call(
        body, grid_spec=pltpu.PrefetchScalarGridSpec(
            num_scalar_prefetch=1, grid=(NSH, nb),
            in_specs=[pl.BlockSpec((1, tr, cols), lambda s, i, c_ref: (s, c_ref[0] * nb + i, 0)), spec],
            out_specs=spec),
        out_shape=jax.ShapeDtypeStruct((NSH, hr, cols), BF), name=name,
        compiler_params=_cp(("arbitrary", "arbitrary")))(_core_vec(), g, got)


def _rs_add2(sums, got, name):
    _, hr, cols = sums.shape
    tr = _row_tile(hr, cols, 2 << 20)
    nb = hr // tr
    place = jnp.stack([2 * lax.axis_index("x") + lax.axis_index("y"), lax.axis_index("c")]).astype(jnp.int32)

    def body(p_ref, s_ref, got_ref, o_ref):
        f = lambda v: v.astype(F32)
        o_ref[...] = f(s_ref[0]) + f(got_ref[0]) + f(got_ref[1]) + f(got_ref[2])

    return pl.pallas_call(
        body, grid_spec=pltpu.PrefetchScalarGridSpec(
            num_scalar_prefetch=1, grid=(nb,),
            in_specs=[pl.BlockSpec((1, tr, cols), lambda i, p_ref: (p_ref[0], i, 0)),
                      pl.BlockSpec((3, tr, cols), lambda i, p_ref: (0, i, 0))],
            out_specs=pl.BlockSpec((tr, cols), lambda i, p_ref: (p_ref[1] * nb + i, 0))),
        out_shape=jax.ShapeDtypeStruct((2 * hr, cols), F32), name=name,
        compiler_params=_cp(("arbitrary",)))(place, sums, got)


def _rs_chip_copies(ins, got, send, recv):
    x, y, c = _place()
    peers = ((1 - x, y), (x, 1 - y), (1 - x, 1 - y))
    return [pltpu.make_async_remote_copy(src_ref=ins[j].at[2 * px + py], dst_ref=got[j].at[k],
                                         send_sem=send.at[3 * j + k], recv_sem=recv.at[3 * j + k],
                                         device_id=(px, py, c), device_id_type=MESH)
            for j in range(len(ins)) for k, (px, py) in enumerate(peers)]


def _ag_sibling(fulls):
    n = len(fulls)
    nck = 4

    def body(*refs):
        outs = refs[n:2 * n]
        send, recv = refs[2 * n:]
        x, y, c = _place()
        copies = []
        for j in range(n):
            qr = outs[j].shape[0] // (2 * nck)
            for k in range(nck):
                rows = outs[j].at[pl.ds(pl.multiple_of((c * nck + k) * qr, 8), qr), :]
                cp = pltpu.make_async_remote_copy(src_ref=rows, dst_ref=rows, send_sem=send.at[nck * j + k],
                                                  recv_sem=recv.at[nck * j + k], device_id=(x, y, 1 - c),
                                                  device_id_type=MESH)
                cp.start()
                copies.append(cp)
        for cp in copies:
            cp.wait()

    return pl.pallas_call(
        body, in_specs=[_ANY] * n, out_specs=[_ANY] * n,
        out_shape=[jax.ShapeDtypeStruct(f.shape, F32) for f in fulls],
        input_output_aliases={j: j for j in range(n)},
        scratch_shapes=[pltpu.SemaphoreType.DMA((nck * n,)), pltpu.SemaphoreType.DMA((nck * n,))],
        name="ag_sibling")(*fulls)


def _allreduce_small(buf):
    rows = buf.shape[0]
    pr = rows // 8

    def body(in_ref, out_ref, part, stage_a, stage_b, *sems):
        (sa, ra, sb, rb, sc, rc, sd, rd) = sems
        x, y, c = _place()
        s = 2 * x + y
        sib = (x, y, 1 - c)
        chips = ((1 - x, y), (x, 1 - y), (1 - x, 1 - y))

        def piece(ref, chip, core):
            return ref.at[pl.ds(pl.multiple_of((2 * chip + core) * pr, 8), pr), :]

        def run(copies):
            for cp in copies:
                cp.start()
            for cp in copies:
                cp.wait()

        run([pltpu.make_async_remote_copy(src_ref=piece(in_ref, j, 1 - c), dst_ref=stage_a.at[j], send_sem=sa.at[j],
                                          recv_sem=ra.at[j], device_id=sib, device_id_type=MESH) for j in range(NSH)])
        for j in range(NSH):
            part[j] = piece(in_ref, j, c)[...] + stage_a[j]
        run([pltpu.make_async_remote_copy(src_ref=part.at[2 * px + py], dst_ref=stage_b.at[k], send_sem=sb.at[k],
                                          recv_sem=rb.at[k], device_id=(px, py, c), device_id_type=MESH)
             for k, (px, py) in enumerate(chips)])
        piece(out_ref, s, c)[...] = part[s] + stage_b[0] + stage_b[1] + stage_b[2]
        run([pltpu.make_async_remote_copy(src_ref=piece(out_ref, s, c), dst_ref=piece(out_ref, s, c),
                                          send_sem=sc.at[k], recv_sem=rc.at[k], device_id=(px, py, c),
                                          device_id_type=MESH) for k, (px, py) in enumerate(chips)])
        run([pltpu.make_async_remote_copy(src_ref=piece(out_ref, j, c), dst_ref=piece(out_ref, j, c), send_sem=sd.at[j],
                                          recv_sem=rd.at[j], device_id=sib, device_id_type=MESH) for j in range(NSH)])

    vm = pl.BlockSpec(memory_space=pltpu.VMEM)
    return pl.pallas_call(
        body, in_specs=[vm], out_specs=vm, out_shape=jax.ShapeDtypeStruct((rows, D), F32),
        scratch_shapes=[pltpu.VMEM((NSH, pr, D), F32), pltpu.VMEM((NSH, pr, D), F32), pltpu.VMEM((3, pr, D), F32)]
        + [pltpu.SemaphoreType.DMA((NSH,))] * 8,
        name="allreduce_small", compiler_params=_cp())(buf)


def _pad_rows(a, mult):
    pad = (-a.shape[0]) % mult
    return jnp.concatenate([a, jnp.zeros((pad, a.shape[1]), a.dtype)]) if pad else a


def _local_step(x, c, ctx, c_ctx, tgt, me, shard, sh, b_mod, b_in, norm_g, cb, wr, wi, ln_g, ln_b):
    t_lat, t_ctx = x.shape[0], ctx.shape[0]
    nlt = t_lat // 256
    nlb, ncb = t_lat // RB, t_ctx // RB
    mc = 3 * D // NSH
    mod_all, cc_all = _mod_tp(jnp.zeros((8, D), F32).at[0].set(c), c_ctx.reshape(1, D), sh["w_mod"],
                              lax.dynamic_slice_in_dim(b_mod, shard * mc, mc, axis=1))
    mod = jnp.zeros((16, 3 * D), F32).at[0].set(mod_all[me]).at[1].set(mod_all[8])
    cc = jnp.zeros((16, D), F32).at[0].set(c).at[1].set(c_ctx)
    z, u_all, w_in_g, nar = _inproj_ag(x, ctx, mod, sh["w_in"], b_in, sh["narrow"])
    nar = jnp.transpose(nar, (1, 0, 2)).reshape(-1, D)
    lbl, cw, br, bi, lam = nar[0:4].reshape(2, 2, D), nar[4:8], nar[8:10], nar[10:12], nar[12:14]
    o0, st0, (w_mod_g, p_a, p_b, w_out) = _gla_fwd(z, lbl, 0, nlb, ncb,
                                                   gather=[sh[k] for k in ("w_mod", "p_a", "p_b", "w_out")])
    p_a, p_b, w_out = p_a.reshape(D, D), p_b.reshape(D, D), w_out.reshape(D, D)
    o1, st1, _ = _gla_fwd(z, lbl, 1, nlb, ncb)
    h0, hp0, a0, hpc0 = _rglru_fwd(z, cw, cb, wr, br, wi, bi, lam, 0, t_lat, t_ctx)
    h1, hp1, a1, hpc1 = _rglru_fwd(z, cw, cb, wr, br, wi, bi, lam, 1, t_lat, t_ctx)
    (do, dh, dz4a, dz4b, dz6a, dz_sh3, gxres, y, dout, oa, dpa, obv, dpb, acc) = _merge(
        o0, o1, h0, h1, z, x, tgt, mod, norm_g, ln_g, ln_b, p_a, p_b, w_out, t_lat)
    gp_a = _wgrad(oa, dpa, "wgrad_pa")
    gp_b = _wgrad(obv, dpb, "wgrad_pb")
    gw_out = _wgrad(y, dout, "wgrad_wout")
    dxc_lat, dxc_ctx, dwr0, dwi0, sb0 = _rglru_bwd(z, cw, cb, wr, br, wi, bi, lam, dh, hp0, a0, hpc0, 0, t_lat, t_ctx)
    dz5_lat, dz5_ctx, dwr1, dwi1, sb1 = _rglru_bwd(z, cw, cb, wr, br, wi, bi, lam, dh, hp1, a1, hpc1, 1, t_lat, t_ctx,
                                                   prev=(dxc_lat, dxc_ctx))
    dq0, dv0, dz1, sa0 = _gla_bwd(z, lbl, do, st0, 0, nlb, ncb)
    dz_sh0, dz_sh1, sa1 = _gla_bwd(z, lbl, do, st1, 1, nlb, ncb, prev=(dq0, dv0, dz1, dz4a))
    dz = (dz_sh0, dz_sh1, _pack_shard2(dz4b, dz5_lat, dz5_ctx, dz6a), dz_sh3)
    grads = [_wgrad_in(u_all, dz)] + [g.reshape(NSH, D // NSH, D) for g in (gp_a, gp_b, gw_out)]
    sums = [_rs_add1(g, b, f"rs_add1_{j}") for j, (g, b) in enumerate(zip(grads, _rs_sibling(grads, "rs_sibling")))]
    gx, dm, got = _du(dz, w_in_g, x, ctx, mod, gxres, sums)
    fulls = [_rs_add2(a, b, f"rs_add2_{j}") for j, (a, b) in enumerate(zip(sums, got))]
    big = dict(zip(_RS, _ag_sibling(fulls)))
    dmod = jnp.zeros((16, 3 * D), F32).at[0:2, 0:2 * D].set(dm[0:2]).at[0, 2 * D:].set(acc[0])
    dcc = _mod_bwd(cc, dmod, w_mod_g)
    small = dict(
        c_ctx=dcc[1:2], b_mod=(dmod[0] + dmod[1]).reshape(3, D),
        b_in=jnp.stack([sa1[2], sa0[0], sa1[0], sa1[3], acc[3], sb1[4], acc[4], acc[5], acc[6]]),
        lb_logits=jnp.stack([sa0[1], sa1[1], -sa0[1], -sa1[1]]),
        norm_a_g=acc[7:8], conv_w=sb1[8:12], conv_b=sb1[3:4],
        w_r=jnp.stack([dwr0, dwr1]).reshape(-1, D), w_i=jnp.stack([dwi0, dwi1]).reshape(-1, D),
        b_r=jnp.stack([sb0[0], sb1[0]]), b_i=jnp.stack([sb0[1], sb1[1]]), lam=jnp.stack([sb0[2], sb1[2]]),
        ln_g=acc[1:2], ln_b=acc[2:3])
    return acc[9, 0], gx, big, small, dmod, cc_all


_RS =("w_in", "p_a", "p_b", "w_out")
_SMALL =("c_ctx", "b_mod", "b_in", "lb_logits", "norm_a_g", "conv_w", "conv_b", "w_r", "w_i", "b_r", "b_i", "lam",
          "ln_g", "ln_b")
_BIG = ("w_mod", "w_in", "p_a", "p_b", "w_out")
_COL_SHARDED = ("lb_logits", "conv_w", "b_r", "b_i", "lam")
_WEIGHTS = ("c_ctx", "w_mod", "b_mod", "w_in", "b_in", "lb_logits", "norm_a_g", "conv_w", "conv_b", "w_r", "b_r", "w_i",
            "b_i", "lam", "p_a", "p_b", "w_out", "ln_g", "ln_b")


def kernel(x, c, ctx, c_ctx, w_mod, b_mod, w_in, b_in, lb_logits, norm_a_g, conv_w, conv_b, w_r, b_r, w_i, b_i, lam, p_a, p_b, w_out, ln_g, ln_b, loss_target, m_c_ctx, m_w_mod, m_b_mod, m_w_in, m_b_in, m_lb_logits, m_norm_a_g, m_conv_w, m_conv_b, m_w_r, m_b_r, m_w_i, m_b_i, m_lam, m_p_a, m_p_b, m_w_out, m_ln_g, m_ln_b, v_c_ctx, v_w_mod, v_b_mod, v_w_in, v_b_in, v_lb_logits, v_norm_a_g, v_conv_w, v_conv_b, v_w_r, v_b_r, v_w_i, v_b_i, v_lam, v_p_a, v_p_b, v_w_out, v_ln_g, v_ln_b):
    w = dict(c_ctx=c_ctx, w_mod=w_mod, b_mod=b_mod, w_in=w_in, b_in=b_in, lb_logits=lb_logits, norm_a_g=norm_a_g,
             conv_w=conv_w, conv_b=conv_b, w_r=w_r, b_r=b_r, w_i=w_i, b_i=b_i, lam=lam, p_a=p_a, p_b=p_b, w_out=w_out,
             ln_g=ln_g, ln_b=ln_b)
    m = dict(c_ctx=m_c_ctx, w_mod=m_w_mod, b_mod=m_b_mod, w_in=m_w_in, b_in=m_b_in, lb_logits=m_lb_logits,
             norm_a_g=m_norm_a_g, conv_w=m_conv_w, conv_b=m_conv_b, w_r=m_w_r, b_r=m_b_r, w_i=m_w_i, b_i=m_b_i,
             lam=m_lam, p_a=m_p_a, p_b=m_p_b, w_out=m_w_out, ln_g=m_ln_g, ln_b=m_ln_b)
    v = dict(c_ctx=v_c_ctx, w_mod=v_w_mod, b_mod=v_b_mod, w_in=v_w_in, b_in=v_b_in, lb_logits=v_lb_logits,
             norm_a_g=v_norm_a_g, conv_w=v_conv_w, conv_b=v_conv_b, w_r=v_w_r, b_r=v_b_r, w_i=v_w_i, b_i=v_b_i,
             lam=v_lam, p_a=v_p_a, p_b=v_p_b, w_out=v_w_out, ln_g=v_ln_g, ln_b=v_ln_b)
    shard = 2 * lax.axis_index("x") + lax.axis_index("y")
    cs = D // NSH

    sh = {k: w[k][0].astype(BF) for k in _BIG}
    sh["narrow"] = _pad_rows(jnp.concatenate([lb_logits.reshape(4, cs), conv_w[0], b_r[0], b_i[0], lam[0]], axis=0), 8)
    me = 2 * shard + lax.axis_index("c")
    loss, gx, big, small, dmod, cc_all = _local_step(
        x[0], c[0], ctx[0], c_ctx, loss_target[0], me, shard, sh, b_mod, b_in, norm_a_g, conv_b, w_r[0], w_i[0],
        ln_g, ln_b)

    dmod_rows = jnp.zeros((16, 3 * D), F32).at[me].set(dmod[0]).at[8].set(dmod[1]).reshape(48, D)
    loss_rows = jnp.zeros((8, D), F32).at[0, 0].set(loss)
    sizes = [small[k].shape[0] for k in _SMALL]
    red = _allreduce_small(_pad_rows(jnp.concatenate([_pad_rows(small[k], 8) for k in _SMALL] + [dmod_rows, loss_rows],
                                                     axis=0), 64))
    grads = {}
    off = 0
    for k, n in zip(_SMALL, sizes):
        g = red[off:off + n]
        off += n + (-n) % 8
        if k == "norm_a_g":
            g = g[:, :DH]
        if k in _COL_SHARDED:
            g = lax.dynamic_slice_in_dim(g, shard * cs, cs, axis=1)
        grads[k] = g.reshape(w[k].shape)
    for k in _RS:
        grads[k] = big[k].reshape(w[k].shape)
    dmod_all = red[off:off + 48].reshape(16, 3 * D)
    loss = red[off + 48, 0]
    mc = 3 * D // NSH
    grads["w_mod"] = _wmod_grad(cc_all, lax.dynamic_slice_in_dim(dmod_all, shard * mc, mc, axis=1)).reshape(
        w["w_mod"].shape)

    delta, new_m, new_v = {}, {}, {}
    for k in _BIG:
        shp = w[k].shape
        two = lambda a: a.reshape(shp[-2], shp[-1])
        d_, m_, v_ = _adamw(two(w[k]), two(grads[k]), two(m[k]), two(v[k]), f"adamw_{k}")
        delta[k], new_m[k], new_v[k] = d_.reshape(shp), m_.reshape(shp), v_.reshape(shp)
    d_, m_, v_ = _adamw_many(*[[t[k] for k in _SMALL] for t in (w, grads, m, v)])
    delta.update(zip(_SMALL, d_))
    new_m.update(zip(_SMALL, m_))
    new_v.update(zip(_SMALL, v_))

    return (loss, gx[None], *[grads[k] for k in _WEIGHTS], *[delta[k] for k in _WEIGHTS],
            *[new_m[k] for k in _WEIGHTS], *[new_v[k] for k in _WEIGHTS])
```

```python
import jax
import jax.numpy as jnp
from jax import lax
from jax.experimental import pallas as pl
from jax.experimental.pallas import tpu as pltpu

F32 = jnp.float32
BF = jnp.bfloat16
MESH = pl.DeviceIdType.MESH

D = 1024
H = 8
DH = 128
CH = 64
RB = 256
NCK = RB // CH
GRID_W = 64
CB = 8
RCH = 16
IN_COLS = 9 * D
NSH = 4
SHC = IN_COLS // NSH
MC = 3 * D // NSH
CUT2 = SHC - 2 * D
CUT4 = 2 * SHC - 4 * D
CUT6 = 3 * SHC - 6 * D
RG_C = 8.0
ALPHA = 2.0 ** 0.25
LN_EPS = 1e-5
RMS_EPS = 1e-6
Q_SCALE = DH ** -0.5
ADAM_LR, ADAM_B1, ADAM_B2, ADAM_EPS, ADAM_WD, ADAM_STEP = 0.001, 0.9, 0.999, 1e-08, 0.01, 10
VMEM_LIMIT = 56 * 1024 * 1024
VMEM_LIMIT_MERGE = 60 * 1024 * 1024
VMEM_LIMIT_RGLRU_BWD = 63 * 1024 * 1024


def _cp(sem=None, vmem=VMEM_LIMIT):
    return pltpu.CompilerParams(dimension_semantics=sem, vmem_limit_bytes=vmem)


def _sig(x):
    return 0.5 * jnp.tanh(0.5 * x) + 0.5


def _sig_tail(x):
    return 1.0 / (1.0 + jnp.exp(-x))


def _dot(a, b):
    return jnp.dot(a.astype(BF), b.astype(BF), preferred_element_type=F32)


def _dot_nt(a, b):
    return lax.dot_general(a.astype(BF), b.astype(BF), (((1,), (1,)), ((), ())), preferred_element_type=F32)


def _dot_tn(a, b):
    return lax.dot_general(a.astype(BF), b.astype(BF), (((0,), (0,)), ((), ())), preferred_element_type=F32)


def _colsum(v):
    return jnp.sum(v, axis=0, keepdims=True)


def _mod_bwd(cc, dmod, w_mod_g):
    def body(cc_ref, dm_ref, w_ref, dcc_ref):
        v = cc_ref[...]
        sg = _sig(v)
        ds = jnp.zeros((16, D), F32)
        for k in range(NSH):
            ds = ds + _dot_nt(dm_ref[:, k * MC:(k + 1) * MC], w_ref[k])
        dcc_ref[...] = ds * (sg * (1.0 + v * (1.0 - sg)))
    return pl.pallas_call(body, out_shape=jax.ShapeDtypeStruct((16, D), F32),
                          name="mod_bwd", compiler_params=_cp())(cc, dmod, w_mod_g)


def _wmod_grad(cc, dmod_cols):
    def body(cc_ref, dm_ref, dw_ref):
        v = cc_ref[...]
        dw_ref[...] = _dot_tn(v * _sig(v), dm_ref[...])
    return pl.pallas_call(body, out_shape=jax.ShapeDtypeStruct((D, dmod_cols.shape[1]), F32),
                          name="wmod_grad", compiler_params=_cp())(cc, dmod_cols)


def _modulate(xv, mod_ref, is_lat):
    sh = jnp.where(is_lat, mod_ref[0:1, 0:D], mod_ref[1:2, 0:D])
    sc = jnp.where(is_lat, mod_ref[0:1, D:2 * D], mod_ref[1:2, D:2 * D])
    return xv * (1.0 + sc) + sh


def _gla_mask(d, n):
    row = lax.broadcasted_iota(jnp.int32, (n, n), 0)
    col = lax.broadcasted_iota(jnp.int32, (n, n), 1)
    same = (row // CH) == (col // CH)
    return same & ((row <= col) if d else (row >= col))


def _chunk_cumsum(v, rev):
    n = v.shape[0]
    pos = lax.broadcasted_iota(jnp.int32, v.shape, 0) & (CH - 1)
    s = 1
    while s < CH:
        if rev:
            v = v + jnp.where(pos < CH - s, pltpu.roll(v, n - s, 0), 0.0)
        else:
            v = v + jnp.where(pos >= s, pltpu.roll(v, s, 0), 0.0)
        s *= 2
    return v


def _chunk_rows(c):
    return slice(c * CH, (c + 1) * CH)


def _gla_features(zq, zf, lb, d):
    sq = _sig(zq)
    q = zq * sq * Q_SCALE
    sf = _sig(zf)
    f = lb + (1.0 - lb) * sf
    k = 1.0 - f
    g = _chunk_cumsum(jnp.log(f), d)
    last = 0 if d else CH - 1
    gls = [g[c * CH + last:c * CH + last + 1, :] for c in range(NCK)]
    glb = jnp.concatenate([jnp.broadcast_to(gl, (CH, D)) for gl in gls], axis=0)
    eg, eig, eeg = jnp.exp(g), jnp.exp(-g), jnp.exp(glb - g)
    decs = [jnp.exp(gl) for gl in gls]
    return sq, sf, f, q * eg, k * eig, k * eeg, eg, eig, eeg, decs


def _lower_bound(lbl_ref, d):
    return _sig_tail(lbl_ref[0, d:d + 1, :] - lbl_ref[1, d:d + 1, :])


def _gla_rb(d, nlb, ncb):
    nrb = nlb + ncb
    if d:
        return lambda s: nrb - 1 - s
    return lambda s: jnp.where(s < ncb, nlb + s, s - ncb)


def _gla_fwd(z, lbl, d, nlb, ncb, gather=()):
    m = z.shape[0]
    nrb = nlb + ncb
    rb = _gla_rb(d, nlb, ncb)
    ng = len(gather)

    def body(*refs):
        q_ref, f_ref, v_ref, lbl_ref = refs[:4]
        ag_in = refs[4:4 + ng]
        o_ref, st_ref = refs[4 + ng:6 + ng]
        ag_out = refs[6 + ng:6 + 2 * ng]
        S = refs[6 + 2 * ng]
        ag_sems = refs[7 + 2 * ng:]
        s = pl.program_id(0)

        @pl.when(s == 0)
        def _():
            S[...] = jnp.zeros_like(S)
            if ng:
                local, chip, _ = _ag_copies(ag_in, ag_out, *ag_sems)
                for cp in chip + local:
                    cp.start()

        lb = _lower_bound(lbl_ref, d)
        mb = _gla_mask(d, CH)
        _, _, _, qd, ki, ke, _, _, _, decs = _gla_features(q_ref[...], f_ref[...], lb, d)
        qd, ki, ke, v = qd.astype(BF), ki.astype(BF), ke.astype(BF), v_ref[...].astype(BF)
        order = range(NCK - 1, -1, -1) if d else range(NCK)
        for h in range(H):
            hs = slice(h * DH, (h + 1) * DH)
            intra, upd = {}, {}
            for c in range(NCK):
                rs = _chunk_rows(c)
                a = jnp.where(mb, _dot_nt(qd[rs, hs], ki[rs, hs]), 0.0)
                intra[c] = _dot(a, v[rs, hs])
                upd[c] = _dot_tn(v[rs, hs], ke[rs, hs])
            st = S[h]
            for c in order:
                rs = _chunk_rows(c)
                st_ref[c, h] = st
                o_ref[rs, hs] = intra[c] + _dot_nt(qd[rs, hs], st)
                st = st * decs[c][:, hs] + upd[c]
            S[h] = st

        if ng:
            @pl.when(s == nrb - 1)
            def _():
                _ag_finish(*_ag_copies(ag_in, ag_out, *ag_sems))

    def zspec(cb):
        return pl.BlockSpec((RB, D), lambda s: (rb(s), cb))

    outs = pl.pallas_call(
        body, grid=(nrb,),
        in_specs=[zspec(0), zspec(1 + d), zspec(3), pl.BlockSpec((2, 2, D), lambda s: (0, 0, 0))] + [_ANY] * ng,
        out_specs=[pl.BlockSpec((RB, D), lambda s: (rb(s), 0)),
                   pl.BlockSpec((NCK, H, DH, DH), lambda s: (rb(s), 0, 0, 0))] + [_ANY] * ng,
        out_shape=[jax.ShapeDtypeStruct((m, D), F32), jax.ShapeDtypeStruct((m // CH, H, DH, DH), F32)]
        + [jax.ShapeDtypeStruct((NSH,) + g.shape, g.dtype) for g in gather],
        scratch_shapes=[pltpu.VMEM((H, DH, DH), F32)] + (_ag_sems(ng) if ng else []),
        name=f"gla_fwd{d}", compiler_params=_cp(("arbitrary",)))(z, z, z, lbl, *gather)
    return outs[0], outs[1], list(outs[2:])


def _gla_bwd(z, lbl, do_lat, states, d, nlb, ncb, prev=None):
    m = z.shape[0]
    nrb = nlb + ncb
    fwd_rb = _gla_rb(d, nlb, ncb)
    rb = lambda s: fwd_rb(nrb - 1 - s)
    last = prev is not None

    def body(*refs):
        if last:
            (q_ref, f_ref, v_ref, lbl_ref, do_ref, st_ref, pq_ref, pv_ref, dz1_ref, dz4a_ref, sh0_ref, sh1_ref, sum_ref,
             dS) = refs
        else:
            q_ref, f_ref, v_ref, lbl_ref, do_ref, st_ref, o0_ref, o1_ref, o2_ref, sum_ref, dS = refs
        s = pl.program_id(0)
        is_lat = rb(s) < nlb

        @pl.when(s == 0)
        def _():
            dS[...] = jnp.zeros_like(dS)
            sum_ref[...] = jnp.zeros_like(sum_ref)

        lb = _lower_bound(lbl_ref, d)
        mb = _gla_mask(d, RB)
        zq = q_ref[...]
        sq, sf, f, qd, ki, ke, eg, eig, eeg, decs = _gla_features(zq, f_ref[...], lb, d)
        qdb, kib, keb, vb = qd.astype(BF), ki.astype(BF), ke.astype(BF), v_ref[...].astype(BF)
        dob = jnp.where(is_lat, do_ref[...], 0.0).astype(BF)
        order = range(NCK) if d else range(NCK - 1, -1, -1)
        dqd_h, dki_h, dke_h, dv_h, ddec_h = [], [], [], [], []
        for h in range(H):
            hs = slice(h * DH, (h + 1) * DH)
            a = jnp.where(mb, _dot_nt(qdb[:, hs], kib[:, hs]), 0.0).astype(BF)
            da = jnp.where(mb, _dot_nt(dob[:, hs], vb[:, hs]), 0.0).astype(BF)
            dqd_i = _dot(da, kib[:, hs])
            dki_h.append(_dot_tn(da, qdb[:, hs]))
            dvi = _dot_tn(a, dob[:, hs])
            dqd, inc = {}, {}
            for c in range(NCK):
                rs = _chunk_rows(c)
                dqd[c] = dqd_i[rs, :] + _dot(dob[rs, hs], st_ref[c, h])
                inc[c] = _dot_tn(dob[rs, hs], qdb[rs, hs])
            dst = dS[h]
            dke, dv, ddec = {}, {}, {}
            for c in order:
                rs = _chunk_rows(c)
                dv[c] = dvi[rs, :] + _dot_nt(keb[rs, hs], dst)
                dke[c] = _dot(vb[rs, hs], dst)
                ddec[c] = _colsum(st_ref[c, h] * dst)
                dst = inc[c] + dst * decs[c][:, hs]
            dS[h] = dst
            cat = lambda t: jnp.concatenate([t[c] for c in range(NCK)], axis=0)
            dqd_h.append(cat(dqd))
            dke_h.append(cat(dke))
            dv_h.append(cat(dv))
            ddec_h.append([ddec[c] for c in range(NCK)])
        lanes = lambda parts: jnp.concatenate(parts, axis=1)
        dqd, dki, dke, dv = lanes(dqd_h), lanes(dki_h), lanes(dke_h), lanes(dv_h)
        dq = dqd * eg
        dk = dki * eig + dke * eeg
        dke_ke = dke * ke
        dg = dqd * qd - dki * ki - dke_ke
        dgl = [_colsum(dke_ke[_chunk_rows(c), :]) + lanes([ddec_h[h][c] for h in range(H)]) * decs[c]
               for c in range(NCK)]
        dglb = jnp.concatenate([jnp.broadcast_to(t, (CH, D)) for t in dgl], axis=0)
        df = (_chunk_cumsum(dg, 1 - d) + dglb) / f - dk
        dzf = df * (1.0 - lb) * (sf * (1.0 - sf))
        sum_ref[0:1, :] += _colsum(dzf)
        sum_ref[1:2, :] += _colsum(df * (1.0 - sf))
        if last:
            dz0 = (dq + pq_ref[...]) * (Q_SCALE * (sq * (1.0 + zq * (1.0 - sq))))
            dz3 = dv + pv_ref[...]
            sum_ref[2:3, :] += _colsum(dz0)
            sum_ref[3:4, :] += _colsum(dz3)
            dz2 = dzf.astype(BF)
            sh0_ref[:, 0:D] = dz0.astype(BF)
            sh0_ref[:, D:2 * D] = dz1_ref[...]
            sh0_ref[:, 2 * D:] = dz2[:, :CUT2]
            sh1_ref[:, 0:D - CUT2] = dz2[:, CUT2:]
            sh1_ref[:, D - CUT2:2 * D - CUT2] = dz3.astype(BF)
            sh1_ref[:, 2 * D - CUT2:] = dz4a_ref[...]
        else:
            o0_ref[...] = dq
            o1_ref[...] = dv
            o2_ref[...] = dzf.astype(BF)

        @pl.when(s == nrb - 1)
        def _():
            sum_ref[1:2, :] = sum_ref[1:2, :] * (lb * (1.0 - lb))

    def zspec(cb):
        return pl.BlockSpec((RB, D), lambda s: (rb(s), cb))

    rowspec = pl.BlockSpec((RB, D), lambda s: (rb(s), 0))
    in_specs = [zspec(0), zspec(1 + d), zspec(3), pl.BlockSpec((2, 2, D), lambda s: (0, 0, 0)),
                pl.BlockSpec((RB, D), lambda s: (jnp.minimum(rb(s), nlb - 1), 0)),
                pl.BlockSpec((NCK, H, DH, DH), lambda s: (rb(s), 0, 0, 0))]
    args = [z, z, z, lbl, do_lat, states]
    sumspec = pl.BlockSpec((8, D), lambda s: (0, 0))
    if last:
        in_specs += [rowspec, rowspec, rowspec, pl.BlockSpec((RB, CUT4), lambda s: (rb(s), 0))]
        args += list(prev)
        shspec = pl.BlockSpec((RB, SHC), lambda s: (rb(s), 0))
        out_specs = (shspec, shspec, sumspec)
        out_shape = (jax.ShapeDtypeStruct((m, SHC), BF), jax.ShapeDtypeStruct((m, SHC), BF))
    else:
        out_specs = (rowspec, rowspec, rowspec, sumspec)
        out_shape = (jax.ShapeDtypeStruct((m, D), F32), jax.ShapeDtypeStruct((m, D), F32),
                     jax.ShapeDtypeStruct((m, D), BF))
    return pl.pallas_call(
        body, grid=(nrb,), in_specs=in_specs, out_specs=out_specs,
        out_shape=out_shape + (jax.ShapeDtypeStruct((8, D), F32),),
        scratch_shapes=[pltpu.VMEM((H, DH, DH), F32)],
        name=f"gla_bwd{d}", compiler_params=_cp(("arbitrary",)))(*args)


def _shift_rows(v, k, r):
    row = lax.broadcasted_iota(jnp.int32, v.shape, 0)
    rolled = pltpu.roll(v, k % r, 0)
    return jnp.where((row >= k) & (row < r + k), rolled, 0.0)


def _conv_fwd(xb, cw_ref, cb_ref, r):
    return (cb_ref[...] + _shift_rows(xb, 1, r) * cw_ref[0:1, :] + xb * cw_ref[1:2, :]
            + _shift_rows(xb, -1, r) * cw_ref[2:3, :] + _shift_rows(xb, -2, r) * cw_ref[3:4, :])


def _window(ref, lo, n, rows):
    parts = []
    if lo < 0:
        parts.append(jnp.zeros((-lo,) + tuple(ref.shape[1:]), F32))
    parts.append(ref[max(lo, 0):min(lo + n, rows)])
    if lo + n > rows:
        parts.append(jnp.zeros((lo + n - rows,) + tuple(ref.shape[1:]), F32))
    return parts[0] if len(parts) == 1 else jnp.concatenate(parts, axis=0)


def _conv_cols(x_ref, cw_ref, cb_ref, r0, n, rows):
    w = _window(x_ref, r0 - 1, n + 3, rows)
    return (cb_ref[...] + w[0:n] * cw_ref[0:1, :] + w[1:n + 1] * cw_ref[1:2, :] + w[2:n + 2] * cw_ref[2:3, :]
            + w[3:n + 3] * cw_ref[3:4, :])


def _softplus_neg(lam):
    y = jnp.exp(-jnp.abs(lam))
    u = 1.0 + y
    tiny = u == 1.0
    l1p = jnp.where(tiny, y, jnp.log(u) * (y / jnp.where(tiny, 1.0, u - 1.0)))
    return jnp.maximum(-lam, 0.0) + l1p


def _gates(xc, wr_ref, br, wi_ref, bi, sp):
    xcb = xc.astype(BF)
    rs, is_ = [], []
    for g in range(H):
        gs = slice(g * DH, (g + 1) * DH)
        rs.append(jnp.dot(xcb[:, gs], wr_ref[g].astype(BF), preferred_element_type=F32))
        is_.append(jnp.dot(xcb[:, gs], wi_ref[g].astype(BF), preferred_element_type=F32))
    r = _sig_tail(jnp.concatenate(rs, axis=1) + br)
    i = _sig(jnp.concatenate(is_, axis=1) + bi)
    log_a = (-RG_C * r) * sp
    a = jnp.exp(log_a)
    t = jnp.tanh(log_a)
    om = (-2.0 * t) / (1.0 - t)
    return r, i, a, om


def _scan_rows(d, nrows, a_s, b_s, h_s, h0):
    nsl = nrows // 8

    def slab(j, h):
        jj = (nsl - 1 - j) if d else j
        r0 = pl.multiple_of(jj * 8, 8)
        for t in (range(7, -1, -1) if d else range(8)):
            h = a_s[pl.ds(r0 + t, 1), :] * h + b_s[pl.ds(r0 + t, 1), :]
            h_s[pl.ds(r0 + t, 1), :] = h
        return h

    return lax.fori_loop(0, nsl, slab, h0)


def _col_of(d, ncols):
    if d:
        return lambda s: ncols - jnp.maximum(s, 1)
    return lambda s: jnp.maximum(s, 1) - 1


def _rglru_fwd(z, cw, cb, wr, br, wi, bi, lam, d, t_lat, t_ctx):
    m = z.shape[0]
    rows = t_lat // GRID_W
    z3 = z.reshape(m // GRID_W, GRID_W, IN_COLS)
    nblk = GRID_W // CB
    blk = _col_of(d, nblk)
    cblk = t_lat // t_ctx
    rc = min(RCH, rows)

    def body(zc_ref, zx_ref, cw_ref, cb_ref, wr_ref, br_ref, wi_ref, bi_ref, lam_ref,
             hx_ref, hpx_ref, ax_ref, hpc_ref, a_s, b_s, h_s, hcar, a3, b3, cin_s):
        s = pl.program_id(0)
        sp = _softplus_neg(lam_ref[d:d + 1, :])
        br_ = br_ref[d:d + 1, :]
        bi_ = bi_ref[d:d + 1, :]

        @pl.when(s == 0)
        def _():
            xc = _conv_fwd(zc_ref[...], cw_ref, cb_ref, t_ctx)
            _, i, a, om = _gates(xc, wr_ref, br_, wi_ref, bi_, sp)
            a_s[...] = a
            b_s[...] = jnp.sqrt(om) * (i * xc)
            h0 = jnp.zeros((1, D), F32)
            hcar[0:1, :] = _scan_rows(d, t_ctx, a_s, b_s, h_s, h0)
            hs = h_s[...]
            row = lax.broadcasted_iota(jnp.int32, (t_ctx, D), 0)
            if d:
                hpc_ref[...] = jnp.where(row == t_ctx - 1, h0, pltpu.roll(hs, t_ctx - 1, 0))
            else:
                hpc_ref[...] = jnp.where(row == 0, h0, pltpu.roll(hs, 1, 0))

        @pl.when(s > 0)
        def _():
            for r0 in range(0, rows, rc):
                xc = _conv_cols(zx_ref, cw_ref, cb_ref, r0, rc, rows).reshape(rc * CB, D)
                _, i, a, om = _gates(xc, wr_ref, br_, wi_ref, bi_, sp)
                a3[r0:r0 + rc] = a.reshape(rc, CB, D)
                ax_ref[r0:r0 + rc] = a.reshape(rc, CB, D)
                b3[r0:r0 + rc] = (jnp.sqrt(om) * (i * xc)).reshape(rc, CB, D)

            def local(t, carry):
                hl, p = carry
                r = (rows - 1 - t) if d else t
                a = a3[r]
                hl = a * hl + b3[r]
                p = a * p
                b3[r] = hl
                a3[r] = p
                return hl, p

            hl, p = lax.fori_loop(0, rows, local, (jnp.zeros((CB, D), F32), jnp.ones((CB, D), F32)))
            cin = hcar[0:1, :]
            for j in (range(CB - 1, -1, -1) if d else range(CB)):
                cin_s[j:j + 1, :] = cin
                cin = hl[j:j + 1, :] + p[j:j + 1, :] * cin
            hcar[0:1, :] = cin
            c_in = cin_s[...]

            def fix(t, prev):
                r = (rows - 1 - t) if d else t
                h = b3[r] + a3[r] * c_in
                hx_ref[r] = h
                hpx_ref[r] = prev
                return h

            lax.fori_loop(0, rows, fix, c_in)

    full = lambda shp: pl.BlockSpec(shp, lambda s: (0,) * len(shp))
    colspec = pl.BlockSpec((rows, CB, D), lambda s: (0, blk(s), 0))
    outs = pl.pallas_call(
        body, grid=(nblk + 1,),
        in_specs=[pl.BlockSpec((t_ctx, D), lambda s: (cblk, 5)),
                  pl.BlockSpec((rows, CB, D), lambda s: (0, blk(s), 5)),
                  full((4, D)), full((1, D)),
                  pl.BlockSpec((None, H, DH, DH), lambda s: (d, 0, 0, 0)), full((2, D)),
                  pl.BlockSpec((None, H, DH, DH), lambda s: (d, 0, 0, 0)), full((2, D)), full((2, D))],
        out_specs=(colspec, colspec, colspec, full((t_ctx, D))),
        out_shape=(jax.ShapeDtypeStruct((rows, GRID_W, D), F32),) * 3 + (jax.ShapeDtypeStruct((t_ctx, D), F32),),
        scratch_shapes=[pltpu.VMEM((t_ctx, D), F32), pltpu.VMEM((t_ctx, D), F32), pltpu.VMEM((t_ctx, D), F32),
                        pltpu.VMEM((8, D), F32), pltpu.VMEM((rows, CB, D), F32), pltpu.VMEM((rows, CB, D), F32),
                        pltpu.VMEM((CB, D), F32)],
        name=f"rglru_fwd{d}", compiler_params=_cp(("arbitrary",)))(z, z3, cw, cb, wr, br, wi, bi, lam)
    return outs[0].reshape(t_lat, D), outs[1].reshape(t_lat, D), outs[2].reshape(t_lat, D), outs[3]


def _rglru_bwd(z, cw, cb, wr, br, wi, bi, lam, dh_lat, hp_lat, a_lat, hp_ctx, d, t_lat, t_ctx, prev=None):
    m = z.shape[0]
    rows = t_lat // GRID_W
    z3 = z.reshape(m // GRID_W, GRID_W, IN_COLS)
    nblk = GRID_W // CB
    fblk = _col_of(d, nblk)
    blk = lambda s: fblk(nblk - jnp.minimum(s, nblk - 1))
    cblk = t_lat // t_ctx
    rc = min(RCH, rows)
    last = prev is not None
    view3 = lambda v: v.reshape(rows, GRID_W, D)

    def body(*refs):
        (zc_ref, zx_ref, cw_ref, cb_ref, wr_ref, br_ref, wi_ref, bi_ref, lam_ref, dh_ref, hpx_ref, ax_ref,
         hpc_ref) = refs[:13]
        k = 13
        if last:
            pdx_ref, pdc_ref = refs[13:15]
            k = 15
        ox_ref, oc_ref, dwr_ref, dwi_ref, sum_ref, a_s, b_s, gcar, a3, b3, cin_s = refs[k:]
        s = pl.program_id(0)
        lam_d = lam_ref[d:d + 1, :]
        sp = _softplus_neg(lam_d)
        br_ = br_ref[d:d + 1, :]
        bi_ = bi_ref[d:d + 1, :]
        flat = lambda v: v.reshape(-1, D)

        @pl.when(s == 0)
        def _():
            gcar[...] = jnp.zeros_like(gcar)
            dwr_ref[...] = jnp.zeros_like(dwr_ref)
            dwi_ref[...] = jnp.zeros_like(dwi_ref)
            sum_ref[...] = jnp.zeros_like(sum_ref)


        def conv_sums(dxc, dxb, xm1, x0, xp1, xp2):
            sum_ref[3:4, :] += _colsum(flat(dxc))
            sum_ref[4:5, :] += _colsum(flat(dxb))
            sum_ref[8:9, :] += _colsum(flat(dxc * xm1))
            sum_ref[9:10, :] += _colsum(flat(dxc * x0))
            sum_ref[10:11, :] += _colsum(flat(dxc * xp1))
            sum_ref[11:12, :] += _colsum(flat(dxc * xp2))

        def gate_grads(g, hp, xc, rr, i, a, om):
            mult = jnp.sqrt(om)
            da = g * hp
            ixc = i * xc
            dmult = g * ixc
            dixc = g * mult
            di = dixc * xc
            dxc = dixc * i
            dlog_a = da * a - dmult * ((1.0 - om) / mult)
            dr = dlog_a * (-RG_C * sp)
            sum_ref[2:3, :] += _colsum(dlog_a * rr)
            drp = dr * rr * (1.0 - rr)
            dip = di * i * (1.0 - i)
            sum_ref[0:1, :] += _colsum(drp)
            sum_ref[1:2, :] += _colsum(dip)
            xcb = xc.astype(BF)
            drb = drp.astype(BF)
            dib = dip.astype(BF)
            parts = []
            for gi in range(H):
                gs = slice(gi * DH, (gi + 1) * DH)
                parts.append(_dot_nt(drb[:, gs], wr_ref[gi]) + _dot_nt(dib[:, gs], wi_ref[gi]))
                dwr_ref[gi] += _dot_tn(xcb[:, gs], drb[:, gs])
                dwi_ref[gi] += _dot_tn(xcb[:, gs], dib[:, gs])
            return dxc + jnp.concatenate(parts, axis=1)

        @pl.when(s < nblk)
        def _():
            def local(t, carry):
                c, q = carry
                r = t if d else (rows - 1 - t)
                a = ax_ref[r]
                c = a * (c + dh_ref[r])
                q = a * q
                b3[r] = c
                a3[r] = q
                return c, q

            c, q = lax.fori_loop(0, rows, local, (jnp.zeros((CB, D), F32), jnp.ones((CB, D), F32)))
            cin = gcar[0:1, :]
            for j in (range(CB) if d else range(CB - 1, -1, -1)):
                cin_s[j:j + 1, :] = cin
                cin = c[j:j + 1, :] + q[j:j + 1, :] * cin
            gcar[0:1, :] = cin
            c_in = cin_s[...]
            for r0 in (range(rows - rc, -1, -rc) if d else range(0, rows, rc)):
                if d:
                    lo = max(r0 - 1, 0)
                    cn = b3[lo:r0 + rc - 1] + a3[lo:r0 + rc - 1] * c_in
                    if r0 == 0:
                        cn = jnp.concatenate([c_in[None], cn], axis=0)
                else:
                    hi = min(r0 + rc + 1, rows)
                    cn = b3[r0 + 1:hi] + a3[r0 + 1:hi] * c_in
                    if hi == rows:
                        cn = jnp.concatenate([cn, c_in[None]], axis=0)
                g = flat(dh_ref[r0:r0 + rc] + cn)
                xc = flat(_conv_cols(zx_ref, cw_ref, cb_ref, r0, rc, rows))
                rr, i, a, om = _gates(xc, wr_ref, br_, wi_ref, bi_, sp)
                b3[r0:r0 + rc] = gate_grads(g, flat(hpx_ref[r0:r0 + rc]), xc, rr, i, a, om).reshape(rc, CB, D)
            if last:
                for r0 in range(0, rows, rc):
                    b3[r0:r0 + rc] = b3[r0:r0 + rc] + pdx_ref[r0:r0 + rc]
                for r0 in range(0, rows, rc):
                    w = _window(b3, r0 - 2, rc + 3, rows)
                    xw = _window(zx_ref, r0 - 1, rc + 3, rows)
                    dxc = w[2:rc + 2]
                    dxb = (w[3:rc + 3] * cw_ref[0:1, :] + dxc * cw_ref[1:2, :] + w[1:rc + 1] * cw_ref[2:3, :]
                           + w[0:rc] * cw_ref[3:4, :])
                    conv_sums(dxc, dxb, xw[0:rc], xw[1:rc + 1], xw[2:rc + 2], xw[3:rc + 3])
                    ox_ref[r0:r0 + rc] = dxb
            else:
                for r0 in range(0, rows, rc):
                    ox_ref[r0:r0 + rc] = b3[r0:r0 + rc]

        @pl.when(s == nblk)
        def _():
            r = t_ctx
            xb = zc_ref[...]
            xc = _conv_fwd(xb, cw_ref, cb_ref, r)
            rr, i, a, om = _gates(xc, wr_ref, br_, wi_ref, bi_, sp)
            a_s[...] = a
            b_s[...] = jnp.zeros((r, D), F32)
            c0 = gcar[0:1, :]
            _scan_rows(1 - d, r, a_s, b_s, b_s, c0)
            cs = b_s[...]
            row = lax.broadcasted_iota(jnp.int32, (r, D), 0)
            if d:
                g = jnp.where(row == 0, c0, pltpu.roll(cs, 1, 0))
            else:
                g = jnp.where(row == r - 1, c0, pltpu.roll(cs, r - 1, 0))
            dxc = gate_grads(g, hpc_ref[...], xc, rr, i, a, om)
            if last:
                dxc = dxc + pdc_ref[...]
                dxb = (_shift_rows(dxc, -1, r) * cw_ref[0:1, :] + dxc * cw_ref[1:2, :]
                       + _shift_rows(dxc, 1, r) * cw_ref[2:3, :] + _shift_rows(dxc, 2, r) * cw_ref[3:4, :])
                conv_sums(dxc, dxb, _shift_rows(xb, 1, r), xb, _shift_rows(xb, -1, r), _shift_rows(xb, -2, r))
                oc_ref[...] = dxb
            else:
                oc_ref[...] = dxc
            sum_ref[2:3, :] = sum_ref[2:3, :] * (RG_C * _sig_tail(-lam_d))

    full = lambda shp: pl.BlockSpec(shp, lambda s: (0,) * len(shp))
    once = lambda shp: pl.BlockSpec(shp, lambda s: (0,) * len(shp), pipeline_mode=pl.Buffered(1))
    colspec = pl.BlockSpec((rows, CB, D), lambda s: (0, blk(s), 0))
    colonce = pl.BlockSpec((rows, CB, D), lambda s: (0, blk(s), 0), pipeline_mode=pl.Buffered(1))
    in_specs = [pl.BlockSpec((t_ctx, D), lambda s: (cblk, 5), pipeline_mode=pl.Buffered(1)),
                pl.BlockSpec((rows, CB, D), lambda s: (0, blk(s), 5), pipeline_mode=pl.Buffered(1)),
                full((4, D)), full((1, D)),
                pl.BlockSpec((None, H, DH, DH), lambda s: (d, 0, 0, 0)), full((2, D)),
                pl.BlockSpec((None, H, DH, DH), lambda s: (d, 0, 0, 0)), full((2, D)), full((2, D)),
                colspec, colspec, colonce, once((t_ctx, D))]
    args = [z, z3, cw, cb, wr, br, wi, bi, lam, view3(dh_lat), view3(hp_lat), view3(a_lat), hp_ctx]
    if last:
        in_specs += [colonce, once((t_ctx, D))]
        args += [view3(prev[0]), prev[1]]
    outs = pl.pallas_call(
        body, grid=(nblk + 1,), in_specs=in_specs,
        out_specs=(colspec, full((t_ctx, D)), full((H, DH, DH)), full((H, DH, DH)), full((16, D))),
        out_shape=(jax.ShapeDtypeStruct((rows, GRID_W, D), F32), jax.ShapeDtypeStruct((t_ctx, D), F32),
                   jax.ShapeDtypeStruct((H, DH, DH), F32), jax.ShapeDtypeStruct((H, DH, DH), F32),
                   jax.ShapeDtypeStruct((16, D), F32)),
        scratch_shapes=[pltpu.VMEM((t_ctx, D), F32), pltpu.VMEM((t_ctx, D), F32), pltpu.VMEM((8, D), F32),
                        pltpu.VMEM((rows, CB, D), F32), pltpu.VMEM((rows, CB, D), F32), pltpu.VMEM((CB, D), F32)],
        name=f"rglru_bwd{d}", compiler_params=_cp(("arbitrary",), VMEM_LIMIT_RGLRU_BWD))(*args)
    return (outs[0].reshape(t_lat, D), outs[1]) + tuple(outs[2:])


def _merge(o_f, o_b, h_f, h_b, z, x, tgt, mod, norm_g, ln_g, ln_b, p_a, p_b, w_out, t_lat):
    tm = 256
    nt = t_lat // tm

    def body(of_ref, ob_ref, hf_ref, hb_ref, z4_ref, z6_ref, z7_ref, z8_ref, x_ref, t_ref, mod_ref, ng_ref,
             lg_ref, lb_ref, pa_ref, pb_ref, wo_ref,
             do_ref, dh_ref, dz4a_ref, dz4b_ref, dz6a_ref, sh3_ref, gx_ref,
             y_ref, dout_ref, oa_ref, dpa_ref, obv_ref, dpb_ref, acc_ref):
        i = pl.program_id(0)
        lat = i < nt
        latf = lat.astype(F32)

        @pl.when(i == 0)
        def _():
            acc_ref[...] = jnp.zeros_like(acc_ref)

        def per_head(v):
            return jnp.concatenate(
                [jnp.broadcast_to(jnp.mean(v[:, h * DH:(h + 1) * DH], axis=-1, keepdims=True), (tm, DH))
                 for h in range(H)], axis=1)

        gt = mod_ref[0:1, 2 * D:3 * D]
        gfull = jnp.concatenate([ng_ref[...]] * H, axis=1)
        o = of_ref[...] + ob_ref[...]
        rinv = lax.rsqrt(per_head(o * o) + RMS_EPS)
        n = o * rinv
        na = n * gfull
        z4 = z4_ref[...]
        s4 = _sig(z4)
        silu4 = z4 * s4
        oa = na * silu4
        z6 = z6_ref[...]
        s6 = _sig(z6)
        silu6 = z6 * s6
        hsum = hf_ref[...] + hb_ref[...]
        obv = hsum * silu6
        pa = _dot(oa, pa_ref[...])
        pb = _dot(obv, pb_ref[...])
        s7 = _sig(z7_ref[...])
        s8 = _sig(z8_ref[...])
        y = s7 * pa + s8 * pb
        out = _dot(y, wo_ref[...])
        pre = ALPHA * x_ref[...] + gt * out
        mu = jnp.mean(pre, axis=-1, keepdims=True)
        xc = pre - mu
        rstd = lax.rsqrt(jnp.mean(xc * xc, axis=-1, keepdims=True) + LN_EPS)
        xhat = xc * rstd
        lg = lg_ref[...]
        diff = xhat * lg + lb_ref[...] - t_ref[...]
        acc_ref[8:9, :] += _colsum(diff * diff) * (0.5 / D * latf)
        dxn = diff * (1.0 / D)
        acc_ref[1:2, :] += _colsum(dxn * xhat) * latf
        acc_ref[2:3, :] += _colsum(dxn) * latf
        dxhat = dxn * lg
        dpre = rstd * (dxhat - jnp.mean(dxhat, axis=-1, keepdims=True)
                       - xhat * jnp.mean(dxhat * xhat, axis=-1, keepdims=True))
        gx_ref[...] = ALPHA * dpre
        acc_ref[0:1, :] += _colsum(dpre * out) * latf
        dout = dpre * gt
        dy = _dot_nt(dout, wo_ref[...])
        dpa = dy * s7
        dpb = dy * s8
        dz7 = dy * pa * (s7 * (1.0 - s7))
        dz8 = dy * pb * (s8 * (1.0 - s8))
        doa = _dot_nt(dpa, pa_ref[...])
        dob = _dot_nt(dpb, pb_ref[...])
        dh_ref[...] = dob * silu6
        dz6 = dob * hsum * (s6 * (1.0 + z6 * (1.0 - s6)))
        dna = doa * silu4
        dz4 = doa * na * (s4 * (1.0 + z4 * (1.0 - s4)))
        dng = _colsum(dna * n)
        acc_ref[7:8, 0:DH] += sum(dng[:, h * DH:(h + 1) * DH] for h in range(H)) * latf
        dn = dna * gfull
        do_ref[...] = rinv * (dn - n * per_head(dn * n))
        acc_ref[3:4, :] += _colsum(dz4) * latf
        acc_ref[4:5, :] += _colsum(dz6) * latf
        acc_ref[5:6, :] += _colsum(dz7) * latf
        acc_ref[6:7, :] += _colsum(dz8) * latf
        dz4b, dz6b = (dz4 * latf).astype(BF), (dz6 * latf).astype(BF)
        dz4a_ref[...] = dz4b[:, :CUT4]
        dz4b_ref[...] = dz4b[:, CUT4:]
        dz6a_ref[...] = dz6b[:, :CUT6]
        sh3_ref[:, 0:D - CUT6] = dz6b[:, CUT6:]
        sh3_ref[:, D - CUT6:2 * D - CUT6] = (dz7 * latf).astype(BF)
        sh3_ref[:, 2 * D - CUT6:] = (dz8 * latf).astype(BF)
        y_ref[...] = y.astype(BF)
        dout_ref[...] = dout.astype(BF)
        oa_ref[...] = oa.astype(BF)
        dpa_ref[...] = dpa.astype(BF)
        obv_ref[...] = obv.astype(BF)
        dpb_ref[...] = dpb.astype(BF)

        @pl.when(i == nt - 1)
        def _():
            acc_ref[9:10, :] = jnp.broadcast_to(jnp.sum(acc_ref[8:9, :], axis=-1, keepdims=True), (1, D))

    m = z.shape[0]
    lrow = lambda i: jnp.minimum(i, nt - 1)
    row = pl.BlockSpec((tm, D), lambda i: (lrow(i), 0))
    allrow = lambda cols: pl.BlockSpec((tm, cols), lambda i: (i, 0))
    zs = lambda cb: pl.BlockSpec((tm, D), lambda i: (lrow(i), cb))
    full = lambda shp: pl.BlockSpec(shp, lambda i: (0,) * len(shp))
    wfull = pl.BlockSpec((D, D), lambda i: (0, 0), pipeline_mode=pl.Buffered(1))
    f32o = jax.ShapeDtypeStruct((t_lat, D), F32)
    bfo = jax.ShapeDtypeStruct((t_lat, D), BF)
    bfall = lambda cols: jax.ShapeDtypeStruct((m, cols), BF)
    return pl.pallas_call(
        body, grid=(m // tm,),
        in_specs=[row, row, row, row, zs(4), zs(6), zs(7), zs(8), row, row, full((16, 3 * D)), full((1, DH)),
                  full((1, D)), full((1, D)), wfull, wfull, wfull],
        out_specs=(row, row) + tuple(allrow(c) for c in (CUT4, D - CUT4, CUT6, SHC)) + (row,) * 7 + (full((16, D)),),
        out_shape=(f32o, f32o, bfall(CUT4), bfall(D - CUT4), bfall(CUT6), bfall(SHC), f32o, bfo, bfo, bfo, bfo, bfo, bfo,
                   jax.ShapeDtypeStruct((16, D), F32)),
        name="merge", compiler_params=_cp(("arbitrary",), VMEM_LIMIT_MERGE))(
            o_f, o_b, h_f, h_b, z, z, z, z, x, tgt, mod, norm_g, ln_g, ln_b, p_a, p_b, w_out)


def _wgrad(a, b, name):
    tm = 1024

    def body(a_ref, b_ref, o_ref):
        @pl.when(pl.program_id(0) == 0)
        def _():
            o_ref[...] = jnp.zeros_like(o_ref)
        o_ref[...] += _dot_tn(a_ref[...], b_ref[...])

    row = pl.BlockSpec((tm, D), lambda i: (i, 0))
    return pl.pallas_call(body, grid=(a.shape[0] // tm,), in_specs=[row, row],
                          out_specs=pl.BlockSpec((D, D), lambda i: (0, 0)),
                          out_shape=jax.ShapeDtypeStruct((D, D), F32), name=name,
                          compiler_params=_cp(("arbitrary",)))(a, b)


def _pack_shard2(dz4b, dz5_lat, dz5_ctx, dz6a):
    m, t_lat, t_ctx = dz4b.shape[0], dz5_lat.shape[0], dz5_ctx.shape[0]
    tm = t_ctx
    nlt = t_lat // tm
    w4 = D - CUT4

    def body(a_ref, bl_ref, bc_ref, c_ref, o_ref):
        i = pl.program_id(0)
        o_ref[:, 0:w4] = a_ref[...]
        o_ref[:, w4:w4 + D] = jnp.where(i < nlt, bl_ref[...], bc_ref[...]).astype(BF)
        o_ref[:, w4 + D:] = c_ref[...]

    return pl.pallas_call(
        body, grid=(m // tm,),
        in_specs=[pl.BlockSpec((tm, w4), lambda i: (i, 0)),
                  pl.BlockSpec((tm, D), lambda i: (jnp.minimum(i, nlt - 1), 0)),
                  pl.BlockSpec((tm, D), lambda i: (0, 0)),
                  pl.BlockSpec((tm, CUT6), lambda i: (i, 0))],
        out_specs=pl.BlockSpec((tm, SHC), lambda i: (i, 0)),
        out_shape=jax.ShapeDtypeStruct((m, SHC), BF), name="pack_shard2",
        compiler_params=_cp(("arbitrary",)))(dz4b, dz5_lat, dz5_ctx, dz6a)


def _wgrad_in(u_all, dz_shards):
    m = u_all.shape[1]
    assert m % (6 * 128) == 0
    tm = m // 6
    out = None
    for k, dz_k in enumerate(dz_shards):
        def body(u_ref, dz_ref, *rest):
            o_ref = rest[-1]

            @pl.when(pl.program_id(0) == 0)
            def _():
                o_ref[...] = jnp.zeros_like(o_ref)
            o_ref[0] += _dot(u_ref[...], dz_ref[...])

        out = pl.pallas_call(
            body, grid=(m // tm,),
            in_specs=[pl.BlockSpec((D, tm), lambda i: (0, i)), pl.BlockSpec((tm, SHC), lambda i: (i, 0))]
            + ([] if out is None else [_ANY]),
            out_specs=pl.BlockSpec((1, D, SHC), lambda i, k=k: (k, 0, 0)),
            out_shape=jax.ShapeDtypeStruct((NSH, D, SHC), F32),
            input_output_aliases={} if out is None else {2: 0},
            name=f"wgrad_in{k}", compiler_params=_cp(("arbitrary",)))(u_all, dz_k, *(() if out is None else (out,)))
    return out


def _du(dz_shards, w_in_g, x, ctx, mod, gxres, sums=()):
    tm = 256
    n_lat_tiles, nct = x.shape[0] // tm, ctx.shape[0] // tm
    nt = n_lat_tiles + nct
    ns = len(sums)
    rblk = lambda i: jnp.where(i < nct, n_lat_tiles + i, i - nct)
    lblk = lambda i: jnp.maximum(i - nct, 0)

    def body(*refs):
        dz_refs, refs = refs[:NSH], refs[NSH:]
        w_ref, x_ref, c_ref, mod_ref, gr_ref = refs[:5]
        sum_refs = refs[5:5 + ns]
        gx_ref, dm_ref = refs[5 + ns:7 + ns]
        got_refs = refs[7 + ns:7 + 2 * ns]
        sems = refs[7 + 2 * ns:]
        i = pl.program_id(0)
        is_lat = i >= nct

        @pl.when(i == 0)
        def _():
            dm_ref[...] = jnp.zeros_like(dm_ref)
            if ns:
                for cp in _rs_chip_copies(sum_refs, got_refs, *sems):
                    cp.start()

        du = _dot_nt(dz_refs[0][...], w_ref[0])
        for n in range(1, NSH):
            du = du + _dot_nt(dz_refs[n][...], w_ref[n])
        sc = jnp.where(is_lat, mod_ref[0:1, D:2 * D], mod_ref[1:2, D:2 * D])
        dsh = _colsum(du)
        dsc = _colsum(du * jnp.where(is_lat, x_ref[...], c_ref[...]))

        @pl.when(is_lat)
        def _():
            gx_ref[...] = du * (1.0 + sc) + gr_ref[...]
            dm_ref[0:1, 0:D] += dsh
            dm_ref[0:1, D:2 * D] += dsc

        @pl.when(jnp.logical_not(is_lat))
        def _():
            dm_ref[1:2, 0:D] += dsh
            dm_ref[1:2, D:2 * D] += dsc

        if ns:
            @pl.when(i == nt - 1)
            def _():
                for cp in _rs_chip_copies(sum_refs, got_refs, *sems):
                    cp.wait()

    outs = pl.pallas_call(
        body, grid=(nt,),
        in_specs=[pl.BlockSpec((tm, SHC), lambda i: (rblk(i), 0))] * NSH + [
                  pl.BlockSpec((NSH, D, SHC), lambda i: (0, 0, 0), pipeline_mode=pl.Buffered(1)),
                  pl.BlockSpec((tm, D), lambda i: (lblk(i), 0)),
                  pl.BlockSpec((tm, D), lambda i: (jnp.minimum(i, nct - 1), 0)),
                  pl.BlockSpec((16, 3 * D), lambda i: (0, 0)),
                  pl.BlockSpec((tm, D), lambda i: (lblk(i), 0))] + [_ANY] * ns,
        out_specs=[pl.BlockSpec((tm, D), lambda i: (lblk(i), 0)),
                   pl.BlockSpec((8, 2 * D), lambda i: (0, 0))] + [_ANY] * ns,
        out_shape=[jax.ShapeDtypeStruct((n_lat_tiles * tm, D), F32), jax.ShapeDtypeStruct((8, 2 * D), F32)]
        + [jax.ShapeDtypeStruct((3,) + g.shape[1:], g.dtype) for g in sums],
        scratch_shapes=[pltpu.SemaphoreType.DMA((3 * ns,)), pltpu.SemaphoreType.DMA((3 * ns,))] if ns else [],
        name="du", compiler_params=_cp(("arbitrary",)))(*dz_shards, w_in_g, x, ctx, mod, gxres, *sums)
    return outs[0], outs[1], list(outs[2:])


def _row_tile(rows, cols, block_bytes=1 << 20):
    t = 8
    while t * 2 * cols * 4 <= block_bytes and rows % (t * 2) == 0:
        t *= 2
    return t


def _adamw_update(w_ref, g_ref, m_ref, v_ref, d_ref, nm_ref, nv_ref):
    gg = g_ref[...]
    m2 = ADAM_B1 * m_ref[...] + (1.0 - ADAM_B1) * gg
    v2 = ADAM_B2 * v_ref[...] + (1.0 - ADAM_B2) * (gg * gg)
    m_hat = m2 / (1.0 - ADAM_B1 ** ADAM_STEP)
    v_hat = v2 / (1.0 - ADAM_B2 ** ADAM_STEP)
    d_ref[...] = -ADAM_LR * (m_hat / (jnp.sqrt(v_hat) + ADAM_EPS) + ADAM_WD * w_ref[...])
    nm_ref[...] = m2
    nv_ref[...] = v2


def _adamw_many(ws, gs, ms, vs):
    n = len(ws)

    def body(*refs):
        for j in range(n):
            _adamw_update(*refs[4 * j:4 * j + 4], *refs[4 * n + 3 * j:4 * n + 3 * j + 3])

    args = [a for quad in zip(ws, gs, ms, vs) for a in quad]
    outs = pl.pallas_call(body, out_shape=[jax.ShapeDtypeStruct(w.shape, F32) for w in ws for _ in range(3)],
                          name="adamw_small", compiler_params=_cp())(*args)
    return outs[0::3], outs[1::3], outs[2::3]


def _adamw(ws, gs, ms, vs, name):
    n = len(ws)
    rows, cols = ws[0].shape
    tr = _row_tile(rows, cols, (2 << 20) // n)

    def body(*refs):
        for j in range(n):
            _adamw_update(*refs[4 * j:4 * j + 4], *refs[4 * n + 3 * j:4 * n + 3 * j + 3])

    spec = pl.BlockSpec((tr, cols), lambda i: (i, 0))
    o = jax.ShapeDtypeStruct((rows, cols), F32)
    args = [a for quad in zip(ws, gs, ms, vs) for a in quad]
    outs = pl.pallas_call(body, grid=(rows // tr,), in_specs=[spec] * (4 * n), out_specs=[spec] * (3 * n),
                          out_shape=[o] * (3 * n), name=name, compiler_params=_cp(("arbitrary",)))(*args)
    return outs[0::3], outs[1::3], outs[2::3]


_ANY = pl.BlockSpec(memory_space=pl.ANY)


def _place():
    return lax.axis_index("x"), lax.axis_index("y"), lax.axis_index("c")


def _ag_copies(ins, outs, send, recv, fsend, frecv, lsem):
    x, y, c = _place()
    me = 2 * x + y
    chips = ((1 - x, y), (x, 1 - y), (1 - x, 1 - y))
    local, chip, hand = [], [], []
    for j in range(len(ins)):
        hr = ins[j].shape[0] // 2
        half = pl.ds(pl.multiple_of(c * hr, 8), hr)
        local.append(pltpu.make_async_copy(ins[j], outs[j].at[me], lsem.at[j]))
        for k, (px, py) in enumerate(chips):
            chip.append(pltpu.make_async_remote_copy(
                src_ref=ins[j].at[half, :], dst_ref=outs[j].at[me, half, :], send_sem=send.at[3 * j + k],
                recv_sem=recv.at[3 * j + k], device_id=(px, py, c), device_id_type=MESH))
            got = outs[j].at[2 * px + py, half, :]
            hand.append(pltpu.make_async_remote_copy(
                src_ref=got, dst_ref=got, send_sem=fsend.at[3 * j + k], recv_sem=frecv.at[3 * j + k],
                device_id=(x, y, 1 - c), device_id_type=MESH))
    return local, chip, hand


def _ag_sems(n):
    return [pltpu.SemaphoreType.DMA((3 * n,))] * 4 + [pltpu.SemaphoreType.DMA((n,))]


def _ag_finish(local, chip, hand, done=0):
    for k in range(done, len(chip)):
        chip[k].wait_recv()
        hand[k].start()
    for cp in chip:
        cp.wait_send()
    for k in range(done):
        hand[k].wait_send()
    for k in range(done, len(chip)):
        hand[k].wait_send()
        hand[k].wait_recv()
    for cp in local:
        cp.wait()


def _mod_tp(c8, c_ctx, w_mod_sh, b_mod_sh):
    mc = w_mod_sh.shape[1]

    def body(c8_ref, cctx_ref, w_ref, b_ref, mod_ref, cc_ref, cc_s, part_s, send1, recv1, send3, recv3):
        x, y, c = _place()
        me = 4 * x + 2 * y + c
        ms = 2 * x + y
        copies = []
        for k in range(1, 8):
            peer = (x ^ ((k >> 2) & 1), y ^ ((k >> 1) & 1), c ^ (k & 1))
            cp = pltpu.make_async_remote_copy(src_ref=c8_ref, dst_ref=cc_s.at[me], send_sem=send1.at[k],
                                              recv_sem=recv1.at[k], device_id=peer, device_id_type=MESH)
            cp.start()
            copies.append(cp)
        cc_s[me] = c8_ref[...]
        for cp in copies:
            cp.wait()
        cc_ref[...] = jnp.zeros_like(cc_ref)
        for j in range(8):
            cc_ref[j:j + 1, :] = cc_s[j, 0:1, :]
        cc_ref[8:9, :] = cctx_ref[...]
        v = cc_ref[...]
        part_s[ms] = _dot(v * _sig(v), w_ref[...]) + b_ref[...]
        copies = []
        for k in range(1, 4):
            peer = (x ^ ((k >> 1) & 1), y ^ (k & 1), c)
            cp = pltpu.make_async_remote_copy(src_ref=part_s.at[ms], dst_ref=part_s.at[ms], send_sem=send3.at[k],
                                              recv_sem=recv3.at[k], device_id=peer, device_id_type=MESH)
            cp.start()
            copies.append(cp)
        for cp in copies:
            cp.wait()
        for s in range(NSH):
            mod_ref[:, s * mc:(s + 1) * mc] = part_s[s]

    vm = pl.BlockSpec(memory_space=pltpu.VMEM)
    return pl.pallas_call(
        body, in_specs=[vm] * 4, out_specs=(vm, vm),
        out_shape=(jax.ShapeDtypeStruct((16, NSH * mc), F32), jax.ShapeDtypeStruct((16, D), F32)),
        scratch_shapes=[pltpu.VMEM((8, 8, D), F32), pltpu.VMEM((NSH, 16, mc), F32),
                        pltpu.SemaphoreType.DMA((8,)), pltpu.SemaphoreType.DMA((8,)),
                        pltpu.SemaphoreType.DMA((4,)), pltpu.SemaphoreType.DMA((4,))],
        name="mod_tp", compiler_params=_cp())(c8, c_ctx, w_mod_sh, b_mod_sh)


def _inproj_ag(x, ctx, mod, w_in_sh, b_in, narrow_sh):
    n_lat = x.shape[0]
    m = n_lat + ctx.shape[0]
    assert m % (11 * 128) == 0
    tm = m // 11
    nt = m // tm
    nfull = n_lat // tm
    assert nfull == nt - 1
    tail = jnp.concatenate([x[nfull * tm:], ctx], axis=0)
    x_, y_ = lax.axis_index("x"), lax.axis_index("y")
    sids = jnp.stack([2 * x_ + y_, 2 * (1 - x_) + y_, 2 * x_ + 1 - y_, 2 * (1 - x_) + 1 - y_]).astype(jnp.int32)

    def body(sid_ref, x_ref, tail_ref, mod_ref, b_ref, wsh_ref, nsh_ref, z_ref, u_ref, wg_ref, ng_ref, w_s, u_s, *sems):
        n = pl.program_id(0)
        i = pl.program_id(1)
        rows = pl.ds(pl.multiple_of(i * tm, tm), tm)
        ag = ((wsh_ref, nsh_ref), (wg_ref, ng_ref)) + tuple(sems[:5])
        wsem = sems[5]

        def load(src):
            cp = pltpu.make_async_copy(src, w_s, wsem)
            cp.start()
            cp.wait()

        @pl.when((n == 0) & (i == 0))
        def _():
            local, chip, _ = _ag_copies(*ag)
            for cp in chip + local:
                cp.start()
            load(wsh_ref)

        for k in range(NSH - 1):
            @pl.when((n == k + 1) & (i == 0))
            def _():
                _, chip, hand = _ag_copies(*ag)
                chip[k].wait_recv()
                hand[k].start()
                hand[k].wait_recv()
                load(wg_ref.at[sid_ref[k + 1]])

        @pl.when(n == 0)
        def _():
            is_lat = (i * tm + lax.broadcasted_iota(jnp.int32, (tm, 1), 0)) < n_lat
            u = _modulate(jnp.where(i < nfull, x_ref[...], tail_ref[...]), mod_ref, is_lat)
            u_s[rows, :] = u.astype(BF)
            u_ref[...] = u.T.astype(BF)

        z_ref[...] = _dot(u_s[rows, :], w_s[...]) + b_ref[...]

        @pl.when((n == NSH - 1) & (i == nt - 1))
        def _():
            _ag_finish(*_ag_copies(*ag), done=NSH - 1)

    first = lambda n, i: jnp.where(n == 0, i, nt - 1)
    outs = pl.pallas_call(
        body, grid_spec=pltpu.PrefetchScalarGridSpec(
            num_scalar_prefetch=1, grid=(NSH, nt),
            in_specs=[pl.BlockSpec((tm, D), lambda n, i, sid: (jnp.minimum(first(n, i), nfull - 1), 0)),
                      pl.BlockSpec((tm, D), lambda n, i, sid: (0, 0)),
                      pl.BlockSpec((16, 3 * D), lambda n, i, sid: (0, 0)),
                      pl.BlockSpec((1, SHC), lambda n, i, sid: (0, sid[n])), _ANY, _ANY],
            out_specs=[pl.BlockSpec((tm, SHC), lambda n, i, sid: (i, sid[n])),
                       pl.BlockSpec((D, tm), lambda n, i, sid: (0, first(n, i))), _ANY, _ANY],
            scratch_shapes=[pltpu.VMEM((D, SHC), BF), pltpu.VMEM((m, D), BF)] + _ag_sems(2)
            + [pltpu.SemaphoreType.DMA]),
        out_shape=[jax.ShapeDtypeStruct((m, IN_COLS), F32), jax.ShapeDtypeStruct((D, m), BF),
                   jax.ShapeDtypeStruct((NSH,) + w_in_sh.shape, BF),
                   jax.ShapeDtypeStruct((NSH,) + narrow_sh.shape, narrow_sh.dtype)],
        name="inproj_ag", compiler_params=_cp(("arbitrary", "arbitrary")))(sids, x, tail, mod, b_in, w_in_sh, narrow_sh)
    return outs


def _rs_sibling(grads, name):
    n = len(grads)

    def body(*refs):
        ins, got = refs[:n], refs[n:2 * n]
        send, recv = refs[2 * n:]
        x, y, c = _place()
        copies = []
        for j in range(n):
            hr = ins[j].shape[1] // 2
            for s in range(NSH):
                give = ins[j].at[s, pl.ds(pl.multiple_of((1 - c) * hr, 8), hr), :]
                cp = pltpu.make_async_remote_copy(src_ref=give, dst_ref=got[j].at[s], send_sem=send.at[NSH * j + s],
                                                  recv_sem=recv.at[NSH * j + s], device_id=(x, y, 1 - c),
                                                  device_id_type=MESH)
                cp.start()
                copies.append(cp)
        for cp in copies:
            cp.wait()

    half = [jax.ShapeDtypeStruct((NSH, g.shape[1] // 2, g.shape[2]), F32) for g in grads]
    return pl.pallas_call(
        body, in_specs=[_ANY] * n, out_specs=[_ANY] * n, out_shape=half,
        scratch_shapes=[pltpu.SemaphoreType.DMA((NSH * n,)), pltpu.SemaphoreType.DMA((NSH * n,))],
        name=name)(*grads)


def _core_vec():
    return lax.axis_index("c").astype(jnp.int32).reshape(1)


def _rs_add1(gs, gots, name):
    n = len(gs)
    _, r, cols = gs[0].shape
    hr = r // 2
    tr = _row_tile(hr, cols, (4 << 20) // n)
    nb = hr // tr

    def body(c_ref, *refs):
        for j in range(n):
            refs[2 * n + j][...] = (refs[j][...] + refs[n + j][...]).astype(BF)

    spec = pl.BlockSpec((1, tr, cols), lambda s, i, c_ref: (s, i, 0))
    keep = pl.BlockSpec((1, tr, cols), lambda s, i, c_ref: (s, c_ref[0] * nb + i, 0))
    return pl.pallas_call(
        body, grid_spec=pltpu.PrefetchScalarGridSpec(
            num_scalar_prefetch=1, grid=(NSH, nb), in_specs=[keep] * n + [spec] * n, out_specs=[spec] * n),
        out_shape=[jax.ShapeDtypeStruct((NSH, hr, cols), BF)] * n, name=name,
        compiler_params=_cp(("arbitrary", "arbitrary")))(_core_vec(), *gs, *gots)


def _rs_add2(sums, gots, name):
    n = len(sums)
    _, hr, cols = sums[0].shape
    tr = _row_tile(hr, cols, (2 << 20) // n)
    nb = hr // tr
    place = jnp.stack([2 * lax.axis_index("x") + lax.axis_index("y"), lax.axis_index("c")]).astype(jnp.int32)

    def body(p_ref, *refs):
        f = lambda v: v.astype(F32)
        for j in range(n):
            s_ref, got_ref = refs[j], refs[n + j]
            refs[2 * n + j][...] = f(s_ref[0]) + f(got_ref[0]) + f(got_ref[1]) + f(got_ref[2])

    return pl.pallas_call(
        body, grid_spec=pltpu.PrefetchScalarGridSpec(
            num_scalar_prefetch=1, grid=(nb,),
            in_specs=[pl.BlockSpec((1, tr, cols), lambda i, p_ref: (p_ref[0], i, 0))] * n
            + [pl.BlockSpec((3, tr, cols), lambda i, p_ref: (0, i, 0))] * n,
            out_specs=[pl.BlockSpec((tr, cols), lambda i, p_ref: (p_ref[1] * nb + i, 0))] * n),
        out_shape=[jax.ShapeDtypeStruct((2 * hr, cols), F32)] * n, name=name,
        compiler_params=_cp(("arbitrary",)))(place, *sums, *gots)


def _rs_chip_copies(ins, got, send, recv):
    x, y, c = _place()
    peers = ((1 - x, y), (x, 1 - y), (1 - x, 1 - y))
    return [pltpu.make_async_remote_copy(src_ref=ins[j].at[2 * px + py], dst_ref=got[j].at[k],
                                         send_sem=send.at[3 * j + k], recv_sem=recv.at[3 * j + k],
                                         device_id=(px, py, c), device_id_type=MESH)
            for j in range(len(ins)) for k, (px, py) in enumerate(peers)]


def _ag_sibling(fulls):
    n = len(fulls)
    nck = 4

    def body(*refs):
        outs = refs[n:2 * n]
        send, recv = refs[2 * n:]
        x, y, c = _place()
        copies = []
        for j in range(n):
            qr = outs[j].shape[0] // (2 * nck)
            for k in range(nck):
                rows = outs[j].at[pl.ds(pl.multiple_of((c * nck + k) * qr, 8), qr), :]
                cp = pltpu.make_async_remote_copy(src_ref=rows, dst_ref=rows, send_sem=send.at[nck * j + k],
                                                  recv_sem=recv.at[nck * j + k], device_id=(x, y, 1 - c),
                                                  device_id_type=MESH)
                cp.start()
                copies.append(cp)
        for cp in copies:
            cp.wait()

    return pl.pallas_call(
        body, in_specs=[_ANY] * n, out_specs=[_ANY] * n,
        out_shape=[jax.ShapeDtypeStruct(f.shape, F32) for f in fulls],
        input_output_aliases={j: j for j in range(n)},
        scratch_shapes=[pltpu.SemaphoreType.DMA((nck * n,)), pltpu.SemaphoreType.DMA((nck * n,))],
        name="ag_sibling")(*fulls)


def _allreduce_small(buf):
    rows = buf.shape[0]
    pr = rows // 8

    def body(in_ref, out_ref, part, stage_a, stage_b, *sems):
        (sa, ra, sb, rb, sc, rc, sd, rd) = sems
        x, y, c = _place()
        s = 2 * x + y
        sib = (x, y, 1 - c)
        chips = ((1 - x, y), (x, 1 - y), (1 - x, 1 - y))

        def piece(ref, chip, core):
            return ref.at[pl.ds(pl.multiple_of((2 * chip + core) * pr, 8), pr), :]

        def run(copies):
            for cp in copies:
                cp.start()
            for cp in copies:
                cp.wait()

        run([pltpu.make_async_remote_copy(src_ref=piece(in_ref, j, 1 - c), dst_ref=stage_a.at[j], send_sem=sa.at[j],
                                          recv_sem=ra.at[j], device_id=sib, device_id_type=MESH) for j in range(NSH)])
        for j in range(NSH):
            part[j] = piece(in_ref, j, c)[...] + stage_a[j]
        run([pltpu.make_async_remote_copy(src_ref=part.at[2 * px + py], dst_ref=stage_b.at[k], send_sem=sb.at[k],
                                          recv_sem=rb.at[k], device_id=(px, py, c), device_id_type=MESH)
             for k, (px, py) in enumerate(chips)])
        piece(out_ref, s, c)[...] = part[s] + stage_b[0] + stage_b[1] + stage_b[2]
        run([pltpu.make_async_remote_copy(src_ref=piece(out_ref, s, c), dst_ref=piece(out_ref, s, c),
                                          send_sem=sc.at[k], recv_sem=rc.at[k], device_id=(px, py, c),
                                          device_id_type=MESH) for k, (px, py) in enumerate(chips)])
        run([pltpu.make_async_remote_copy(src_ref=piece(out_ref, j, c), dst_ref=piece(out_ref, j, c), send_sem=sd.at[j],
                                          recv_sem=rd.at[j], device_id=sib, device_id_type=MESH) for j in range(NSH)])

    vm = pl.BlockSpec(memory_space=pltpu.VMEM)
    return pl.pallas_call(
        body, in_specs=[vm], out_specs=vm, out_shape=jax.ShapeDtypeStruct((rows, D), F32),
        scratch_shapes=[pltpu.VMEM((NSH, pr, D), F32), pltpu.VMEM((NSH, pr, D), F32), pltpu.VMEM((3, pr, D), F32)]
        + [pltpu.SemaphoreType.DMA((NSH,))] * 8,
        name="allreduce_small", compiler_params=_cp())(buf)


def _pad_rows(a, mult):
    pad = (-a.shape[0]) % mult
    return jnp.concatenate([a, jnp.zeros((pad, a.shape[1]), a.dtype)]) if pad else a


def _local_step(x, c, ctx, c_ctx, tgt, me, shard, sh, b_mod, b_in, norm_g, cb, wr, wi, ln_g, ln_b):
    t_lat, t_ctx = x.shape[0], ctx.shape[0]
    nlt = t_lat // 256
    nlb, ncb = t_lat // RB, t_ctx // RB
    mc = 3 * D // NSH
    mod_all, cc_all = _mod_tp(jnp.zeros((8, D), F32).at[0].set(c), c_ctx.reshape(1, D), sh["w_mod"],
                              lax.dynamic_slice_in_dim(b_mod, shard * mc, mc, axis=1))
    mod = jnp.zeros((16, 3 * D), F32).at[0].set(mod_all[me]).at[1].set(mod_all[8])
    cc = jnp.zeros((16, D), F32).at[0].set(c).at[1].set(c_ctx)
    z, u_all, w_in_g, nar = _inproj_ag(x, ctx, mod, sh["w_in"], b_in, sh["narrow"])
    nar = jnp.transpose(nar, (1, 0, 2)).reshape(-1, D)
    lbl, cw, br, bi, lam = nar[0:4].reshape(2, 2, D), nar[4:8], nar[8:10], nar[10:12], nar[12:14]
    o0, st0, (w_mod_g, p_a, p_b, w_out) = _gla_fwd(z, lbl, 0, nlb, ncb,
                                                   gather=[sh[k] for k in ("w_mod", "p_a", "p_b", "w_out")])
    p_a, p_b, w_out = p_a.reshape(D, D), p_b.reshape(D, D), w_out.reshape(D, D)
    o1, st1, _ = _gla_fwd(z, lbl, 1, nlb, ncb)
    h0, hp0, a0, hpc0 = _rglru_fwd(z, cw, cb, wr, br, wi, bi, lam, 0, t_lat, t_ctx)
    h1, hp1, a1, hpc1 = _rglru_fwd(z, cw, cb, wr, br, wi, bi, lam, 1, t_lat, t_ctx)
    (do, dh, dz4a, dz4b, dz6a, dz_sh3, gxres, y, dout, oa, dpa, obv, dpb, acc) = _merge(
        o0, o1, h0, h1, z, x, tgt, mod, norm_g, ln_g, ln_b, p_a, p_b, w_out, t_lat)
    gp_a = _wgrad(oa, dpa, "wgrad_pa")
    gp_b = _wgrad(obv, dpb, "wgrad_pb")
    gw_out = _wgrad(y, dout, "wgrad_wout")
    dxc_lat, dxc_ctx, dwr0, dwi0, sb0 = _rglru_bwd(z, cw, cb, wr, br, wi, bi, lam, dh, hp0, a0, hpc0, 0, t_lat, t_ctx)
    dz5_lat, dz5_ctx, dwr1, dwi1, sb1 = _rglru_bwd(z, cw, cb, wr, br, wi, bi, lam, dh, hp1, a1, hpc1, 1, t_lat, t_ctx,
                                                   prev=(dxc_lat, dxc_ctx))
    dq0, dv0, dz1, sa0 = _gla_bwd(z, lbl, do, st0, 0, nlb, ncb)
    dz_sh0, dz_sh1, sa1 = _gla_bwd(z, lbl, do, st1, 1, nlb, ncb, prev=(dq0, dv0, dz1, dz4a))
    dz = (dz_sh0, dz_sh1, _pack_shard2(dz4b, dz5_lat, dz5_ctx, dz6a), dz_sh3)
    grads = [_wgrad_in(u_all, dz)] + [g.reshape(NSH, D // NSH, D) for g in (gp_a, gp_b, gw_out)]
    got = _rs_sibling(grads, "rs_sibling")
    sums = list(_rs_add1(grads[:1], got[:1], "rs_add1_w")) + list(_rs_add1(grads[1:], got[1:], "rs_add1_p"))
    gx, dm, got = _du(dz, w_in_g, x, ctx, mod, gxres, sums)
    fulls = list(_rs_add2(sums[:1], got[:1], "rs_add2_w")) + list(_rs_add2(sums[1:], got[1:], "rs_add2_p"))
    big = dict(zip(_RS, _ag_sibling(fulls)))
    dmod = jnp.zeros((16, 3 * D), F32).at[0:2, 0:2 * D].set(dm[0:2]).at[0, 2 * D:].set(acc[0])
    dcc = _mod_bwd(cc, dmod, w_mod_g)
    small = dict(
        c_ctx=dcc[1:2], b_mod=(dmod[0] + dmod[1]).reshape(3, D),
        b_in=jnp.stack([sa1[2], sa0[0], sa1[0], sa1[3], acc[3], sb1[4], acc[4], acc[5], acc[6]]),
        lb_logits=jnp.stack([sa0[1], sa1[1], -sa0[1], -sa1[1]]),
        norm_a_g=acc[7:8], conv_w=sb1[8:12], conv_b=sb1[3:4],
        w_r=jnp.stack([dwr0, dwr1]).reshape(-1, D), w_i=jnp.stack([dwi0, dwi1]).reshape(-1, D),
        b_r=jnp.stack([sb0[0], sb1[0]]), b_i=jnp.stack([sb0[1], sb1[1]]), lam=jnp.stack([sb0[2], sb1[2]]),
        ln_g=acc[1:2], ln_b=acc[2:3])
    return acc[9, 0], gx, big, small, dmod, cc_all


_RS =("w_in", "p_a", "p_b", "w_out")
_SMALL =("c_ctx", "b_mod", "b_in", "lb_logits", "norm_a_g", "conv_w", "conv_b", "w_r", "w_i", "b_r", "b_i", "lam",
          "ln_g", "ln_b")
_BIG = ("w_mod", "w_in", "p_a", "p_b", "w_out")
_COL_SHARDED = ("lb_logits", "conv_w", "b_r", "b_i", "lam")
_WEIGHTS = ("c_ctx", "w_mod", "b_mod", "w_in", "b_in", "lb_logits", "norm_a_g", "conv_w", "conv_b", "w_r", "b_r", "w_i",
            "b_i", "lam", "p_a", "p_b", "w_out", "ln_g", "ln_b")


def kernel(x, c, ctx, c_ctx, w_mod, b_mod, w_in, b_in, lb_logits, norm_a_g, conv_w, conv_b, w_r, b_r, w_i, b_i, lam, p_a, p_b, w_out, ln_g, ln_b, loss_target, m_c_ctx, m_w_mod, m_b_mod, m_w_in, m_b_in, m_lb_logits, m_norm_a_g, m_conv_w, m_conv_b, m_w_r, m_b_r, m_w_i, m_b_i, m_lam, m_p_a, m_p_b, m_w_out, m_ln_g, m_ln_b, v_c_ctx, v_w_mod, v_b_mod, v_w_in, v_b_in, v_lb_logits, v_norm_a_g, v_conv_w, v_conv_b, v_w_r, v_b_r, v_w_i, v_b_i, v_lam, v_p_a, v_p_b, v_w_out, v_ln_g, v_ln_b):
    w = dict(c_ctx=c_ctx, w_mod=w_mod, b_mod=b_mod, w_in=w_in, b_in=b_in, lb_logits=lb_logits, norm_a_g=norm_a_g,
             conv_w=conv_w, conv_b=conv_b, w_r=w_r, b_r=b_r, w_i=w_i, b_i=b_i, lam=lam, p_a=p_a, p_b=p_b, w_out=w_out,
             ln_g=ln_g, ln_b=ln_b)
    m = dict(c_ctx=m_c_ctx, w_mod=m_w_mod, b_mod=m_b_mod, w_in=m_w_in, b_in=m_b_in, lb_logits=m_lb_logits,
             norm_a_g=m_norm_a_g, conv_w=m_conv_w, conv_b=m_conv_b, w_r=m_w_r, b_r=m_b_r, w_i=m_w_i, b_i=m_b_i,
             lam=m_lam, p_a=m_p_a, p_b=m_p_b, w_out=m_w_out, ln_g=m_ln_g, ln_b=m_ln_b)
    v = dict(c_ctx=v_c_ctx, w_mod=v_w_mod, b_mod=v_b_mod, w_in=v_w_in, b_in=v_b_in, lb_logits=v_lb_logits,
             norm_a_g=v_norm_a_g, conv_w=v_conv_w, conv_b=v_conv_b, w_r=v_w_r, b_r=v_b_r, w_i=v_w_i, b_i=v_b_i,
             lam=v_lam, p_a=v_p_a, p_b=v_p_b, w_out=v_w_out, ln_g=v_ln_g, ln_b=v_ln_b)
    shard = 2 * lax.axis_index("x") + lax.axis_index("y")
    cs = D // NSH

    sh = {k: w[k][0].astype(BF) for k in _BIG}
    sh["narrow"] = _pad_rows(jnp.concatenate([lb_logits.reshape(4, cs), conv_w[0], b_r[0], b_i[0], lam[0]], axis=0), 8)
    me = 2 * shard + lax.axis_index("c")
    loss, gx, big, small, dmod, cc_all = _local_step(
        x[0], c[0], ctx[0], c_ctx, loss_target[0], me, shard, sh, b_mod, b_in, norm_a_g, conv_b, w_r[0], w_i[0],
        ln_g, ln_b)

    dmod_rows = jnp.zeros((16, 3 * D), F32).at[me].set(dmod[0]).at[8].set(dmod[1]).reshape(48, D)
    loss_rows = jnp.zeros((8, D), F32).at[0, 0].set(loss)
    sizes = [small[k].shape[0] for k in _SMALL]
    red = _allreduce_small(_pad_rows(jnp.concatenate([_pad_rows(small[k], 8) for k in _SMALL] + [dmod_rows, loss_rows],
                                                     axis=0), 64))
    grads = {}
    off = 0
    for k, n in zip(_SMALL, sizes):
        g = red[off:off + n]
        off += n + (-n) % 8
        if k == "norm_a_g":
            g = g[:, :DH]
        if k in _COL_SHARDED:
            g = lax.dynamic_slice_in_dim(g, shard * cs, cs, axis=1)
        grads[k] = g.reshape(w[k].shape)
    for k in _RS:
        grads[k] = big[k].reshape(w[k].shape)
    dmod_all = red[off:off + 48].reshape(16, 3 * D)
    loss = red[off + 48, 0]
    mc = 3 * D // NSH
    grads["w_mod"] = _wmod_grad(cc_all, lax.dynamic_slice_in_dim(dmod_all, shard * mc, mc, axis=1)).reshape(
        w["w_mod"].shape)

    delta, new_m, new_v = {}, {}, {}
    for group in (("w_mod",), ("w_in",), ("p_a", "p_b", "w_out")):
        two = lambda t: [t[k].reshape(t[k].shape[-2:]) for k in group]
        d_, m_, v_ = _adamw(two(w), two(grads), two(m), two(v), "adamw_" + group[0])
        for j, k in enumerate(group):
            shp = w[k].shape
            delta[k], new_m[k], new_v[k] = d_[j].reshape(shp), m_[j].reshape(shp), v_[j].reshape(shp)
    d_, m_, v_ = _adamw_many(*[[t[k] for k in _SMALL] for t in (w, grads, m, v)])
    delta.update(zip(_SMALL, d_))
    new_m.update(zip(_SMALL, m_))
    new_v.update(zip(_SMALL, v_))

    return (loss, gx[None], *[grads[k] for k in _WEIGHTS], *[delta[k] for k in _WEIGHTS],
            *[new_m[k] for k in _WEIGHTS], *[new_v[k] for k in _WEIGHTS])
```

```python
import jax
import jax.numpy as jnp
from jax import lax
from jax.experimental import pallas as pl
from jax.experimental.pallas import tpu as pltpu

F32 = jnp.float32
BF = jnp.bfloat16
MESH = pl.DeviceIdType.MESH

D = 1024
H = 8
DH = 128
CH = 64
RB = 256
NCK = RB // CH
GRID_W = 64
CB = 8
RCH = 16
IN_COLS = 9 * D
NSH = 4
SHC = IN_COLS // NSH
MC = 3 * D // NSH
CUT2 = SHC - 2 * D
CUT4 = 2 * SHC - 4 * D
CUT6 = 3 * SHC - 6 * D
RG_C = 8.0
ALPHA = 2.0 ** 0.25
LN_EPS = 1e-5
RMS_EPS = 1e-6
Q_SCALE = DH ** -0.5
ADAM_LR, ADAM_B1, ADAM_B2, ADAM_EPS, ADAM_WD, ADAM_STEP = 0.001, 0.9, 0.999, 1e-08, 0.01, 10
VMEM_LIMIT = 56 * 1024 * 1024
VMEM_LIMIT_MERGE = 60 * 1024 * 1024
VMEM_LIMIT_RGLRU_BWD = 63 * 1024 * 1024


def _cp(sem=None, vmem=VMEM_LIMIT):
    return pltpu.CompilerParams(dimension_semantics=sem, vmem_limit_bytes=vmem)


def _sig(x):
    return 0.5 * jnp.tanh(0.5 * x) + 0.5


def _sig_tail(x):
    return 1.0 / (1.0 + jnp.exp(-x))


def _dot(a, b):
    return jnp.dot(a.astype(BF), b.astype(BF), preferred_element_type=F32)


def _dot_nt(a, b):
    return lax.dot_general(a.astype(BF), b.astype(BF), (((1,), (1,)), ((), ())), preferred_element_type=F32)


def _dot_tn(a, b):
    return lax.dot_general(a.astype(BF), b.astype(BF), (((0,), (0,)), ((), ())), preferred_element_type=F32)


def _colsum(v):
    return jnp.sum(v, axis=0, keepdims=True)


def _mod_bwd(cc, dmod, w_mod_g):
    def body(cc_ref, dm_ref, w_ref, dcc_ref):
        v = cc_ref[...]
        sg = _sig(v)
        ds = jnp.zeros((16, D), F32)
        for k in range(NSH):
            ds = ds + _dot_nt(dm_ref[:, k * MC:(k + 1) * MC], w_ref[k])
        dcc_ref[...] = ds * (sg * (1.0 + v * (1.0 - sg)))
    return pl.pallas_call(body, out_shape=jax.ShapeDtypeStruct((16, D), F32),
                          name="mod_bwd", compiler_params=_cp())(cc, dmod, w_mod_g)


def _wmod_grad(cc, dmod_cols):
    def body(cc_ref, dm_ref, dw_ref):
        v = cc_ref[...]
        dw_ref[...] = _dot_tn(v * _sig(v), dm_ref[...])
    return pl.pallas_call(body, out_shape=jax.ShapeDtypeStruct((D, dmod_cols.shape[1]), F32),
                          name="wmod_grad", compiler_params=_cp())(cc, dmod_cols)


def _modulate(xv, mod_ref, is_lat):
    sh = jnp.where(is_lat, mod_ref[0:1, 0:D], mod_ref[1:2, 0:D])
    sc = jnp.where(is_lat, mod_ref[0:1, D:2 * D], mod_ref[1:2, D:2 * D])
    return xv * (1.0 + sc) + sh


def _gla_mask(d, n):
    row = lax.broadcasted_iota(jnp.int32, (n, n), 0)
    col = lax.broadcasted_iota(jnp.int32, (n, n), 1)
    same = (row // CH) == (col // CH)
    return same & ((row <= col) if d else (row >= col))


def _chunk_cumsum(v, rev):
    n = v.shape[0]
    pos = lax.broadcasted_iota(jnp.int32, v.shape, 0) & (CH - 1)
    s = 1
    while s < CH:
        if rev:
            v = v + jnp.where(pos < CH - s, pltpu.roll(v, n - s, 0), 0.0)
        else:
            v = v + jnp.where(pos >= s, pltpu.roll(v, s, 0), 0.0)
        s *= 2
    return v


def _chunk_rows(c):
    return slice(c * CH, (c + 1) * CH)


def _gla_features(zq, zf, lb, d):
    sq = _sig(zq)
    q = zq * sq * Q_SCALE
    sf = _sig(zf)
    f = lb + (1.0 - lb) * sf
    k = 1.0 - f
    g = _chunk_cumsum(jnp.log(f), d)
    last = 0 if d else CH - 1
    gls = [g[c * CH + last:c * CH + last + 1, :] for c in range(NCK)]
    glb = jnp.concatenate([jnp.broadcast_to(gl, (CH, D)) for gl in gls], axis=0)
    eg, eig, eeg = jnp.exp(g), jnp.exp(-g), jnp.exp(glb - g)
    decs = [jnp.exp(gl) for gl in gls]
    return sq, sf, f, q * eg, k * eig, k * eeg, eg, eig, eeg, decs


def _lower_bound(lbl_ref, d):
    return _sig_tail(lbl_ref[0, d:d + 1, :] - lbl_ref[1, d:d + 1, :])


def _gla_rb(d, nlb, ncb):
    nrb = nlb + ncb
    if d:
        return lambda s: nrb - 1 - s
    return lambda s: jnp.where(s < ncb, nlb + s, s - ncb)


def _gla_fwd(z, lbl, d, nlb, ncb, gather=()):
    m = z.shape[0]
    nrb = nlb + ncb
    rb = _gla_rb(d, nlb, ncb)
    ng = len(gather)

    def body(*refs):
        q_ref, f_ref, v_ref, lbl_ref = refs[:4]
        ag_in = refs[4:4 + ng]
        o_ref, st_ref = refs[4 + ng:6 + ng]
        ag_out = refs[6 + ng:6 + 2 * ng]
        S = refs[6 + 2 * ng]
        ag_sems = refs[7 + 2 * ng:]
        s = pl.program_id(0)

        @pl.when(s == 0)
        def _():
            S[...] = jnp.zeros_like(S)
            if ng:
                local, chip, _ = _ag_copies(ag_in, ag_out, *ag_sems)
                for cp in chip + local:
                    cp.start()

        lb = _lower_bound(lbl_ref, d)
        mb = _gla_mask(d, CH)
        _, _, _, qd, ki, ke, _, _, _, decs = _gla_features(q_ref[...], f_ref[...], lb, d)
        qd, ki, ke, v = qd.astype(BF), ki.astype(BF), ke.astype(BF), v_ref[...].astype(BF)
        order = range(NCK - 1, -1, -1) if d else range(NCK)
        for h in range(H):
            hs = slice(h * DH, (h + 1) * DH)
            intra, upd = {}, {}
            for c in range(NCK):
                rs = _chunk_rows(c)
                a = jnp.where(mb, _dot_nt(qd[rs, hs], ki[rs, hs]), 0.0)
                intra[c] = _dot(a, v[rs, hs])
                upd[c] = _dot_tn(v[rs, hs], ke[rs, hs])
            st = S[h]
            for c in order:
                rs = _chunk_rows(c)
                st_ref[c, h] = st
                o_ref[rs, hs] = intra[c] + _dot_nt(qd[rs, hs], st)
                st = st * decs[c][:, hs] + upd[c]
            S[h] = st

        if ng:
            @pl.when(s == nrb - 1)
            def _():
                _ag_finish(*_ag_copies(ag_in, ag_out, *ag_sems))

    def zspec(cb):
        return pl.BlockSpec((RB, D), lambda s: (rb(s), cb))

    outs = pl.pallas_call(
        body, grid=(nrb,),
        in_specs=[zspec(0), zspec(1 + d), zspec(3), pl.BlockSpec((2, 2, D), lambda s: (0, 0, 0))] + [_ANY] * ng,
        out_specs=[pl.BlockSpec((RB, D), lambda s: (rb(s), 0)),
                   pl.BlockSpec((NCK, H, DH, DH), lambda s: (rb(s), 0, 0, 0))] + [_ANY] * ng,
        out_shape=[jax.ShapeDtypeStruct((m, D), F32), jax.ShapeDtypeStruct((m // CH, H, DH, DH), F32)]
        + [jax.ShapeDtypeStruct((NSH,) + g.shape, g.dtype) for g in gather],
        scratch_shapes=[pltpu.VMEM((H, DH, DH), F32)] + (_ag_sems(ng) if ng else []),
        name=f"gla_fwd{d}", compiler_params=_cp(("arbitrary",)))(z, z, z, lbl, *gather)
    return outs[0], outs[1], list(outs[2:])


def _gla_bwd(z, lbl, do_lat, states, d, nlb, ncb, prev=None):
    m = z.shape[0]
    nrb = nlb + ncb
    fwd_rb = _gla_rb(d, nlb, ncb)
    rb = lambda s: fwd_rb(nrb - 1 - s)
    last = prev is not None

    def body(*refs):
        if last:
            (q_ref, f_ref, v_ref, lbl_ref, do_ref, st_ref, pq_ref, pv_ref, dz1_ref, dz4a_ref, sh0_ref, sh1_ref, sum_ref,
             dS) = refs
        else:
            q_ref, f_ref, v_ref, lbl_ref, do_ref, st_ref, o0_ref, o1_ref, o2_ref, sum_ref, dS = refs
        s = pl.program_id(0)
        is_lat = rb(s) < nlb

        @pl.when(s == 0)
        def _():
            dS[...] = jnp.zeros_like(dS)
            sum_ref[...] = jnp.zeros_like(sum_ref)

        lb = _lower_bound(lbl_ref, d)
        mb = _gla_mask(d, RB)
        zq = q_ref[...]
        sq, sf, f, qd, ki, ke, eg, eig, eeg, decs = _gla_features(zq, f_ref[...], lb, d)
        qdb, kib, keb, vb = qd.astype(BF), ki.astype(BF), ke.astype(BF), v_ref[...].astype(BF)
        dob = jnp.where(is_lat, do_ref[...], 0.0).astype(BF)
        order = range(NCK) if d else range(NCK - 1, -1, -1)
        dqd_h, dki_h, dke_h, dv_h, ddec_h = [], [], [], [], []
        for h in range(H):
            hs = slice(h * DH, (h + 1) * DH)
            a = jnp.where(mb, _dot_nt(qdb[:, hs], kib[:, hs]), 0.0).astype(BF)
            da = jnp.where(mb, _dot_nt(dob[:, hs], vb[:, hs]), 0.0).astype(BF)
            dqd_i = _dot(da, kib[:, hs])
            dki_h.append(_dot_tn(da, qdb[:, hs]))
            dvi = _dot_tn(a, dob[:, hs])
            dqd, inc = {}, {}
            for c in range(NCK):
                rs = _chunk_rows(c)
                dqd[c] = dqd_i[rs, :] + _dot(dob[rs, hs], st_ref[c, h])
                inc[c] = _dot_tn(dob[rs, hs], qdb[rs, hs])
            dst = dS[h]
            dke, dv, ddec = {}, {}, {}
            for c in order:
                rs = _chunk_rows(c)
                dv[c] = dvi[rs, :] + _dot_nt(keb[rs, hs], dst)
                dke[c] = _dot(vb[rs, hs], dst)
                ddec[c] = _colsum(st_ref[c, h] * dst)
                dst = inc[c] + dst * decs[c][:, hs]
            dS[h] = dst
            cat = lambda t: jnp.concatenate([t[c] for c in range(NCK)], axis=0)
            dqd_h.append(cat(dqd))
            dke_h.append(cat(dke))
            dv_h.append(cat(dv))
            ddec_h.append([ddec[c] for c in range(NCK)])
        lanes = lambda parts: jnp.concatenate(parts, axis=1)
        dqd, dki, dke, dv = lanes(dqd_h), lanes(dki_h), lanes(dke_h), lanes(dv_h)
        dq = dqd * eg
        dk = dki * eig + dke * eeg
        dke_ke = dke * ke
        dg = dqd * qd - dki * ki - dke_ke
        dgl = [_colsum(dke_ke[_chunk_rows(c), :]) + lanes([ddec_h[h][c] for h in range(H)]) * decs[c]
               for c in range(NCK)]
        dglb = jnp.concatenate([jnp.broadcast_to(t, (CH, D)) for t in dgl], axis=0)
        df = (_chunk_cumsum(dg, 1 - d) + dglb) / f - dk
        dzf = df * (1.0 - lb) * (sf * (1.0 - sf))
        sum_ref[0:1, :] += _colsum(dzf)
        sum_ref[1:2, :] += _colsum(df * (1.0 - sf))
        if last:
            dz0 = (dq + pq_ref[...]) * (Q_SCALE * (sq * (1.0 + zq * (1.0 - sq))))
            dz3 = dv + pv_ref[...]
            sum_ref[2:3, :] += _colsum(dz0)
            sum_ref[3:4, :] += _colsum(dz3)
            dz2 = dzf.astype(BF)
            sh0_ref[:, 0:D] = dz0.astype(BF)
            sh0_ref[:, D:2 * D] = dz1_ref[...]
            sh0_ref[:, 2 * D:] = dz2[:, :CUT2]
            sh1_ref[:, 0:D - CUT2] = dz2[:, CUT2:]
            sh1_ref[:, D - CUT2:2 * D - CUT2] = dz3.astype(BF)
            sh1_ref[:, 2 * D - CUT2:] = dz4a_ref[...]
        else:
            o0_ref[...] = dq
            o1_ref[...] = dv
            o2_ref[...] = dzf.astype(BF)

        @pl.when(s == nrb - 1)
        def _():
            sum_ref[1:2, :] = sum_ref[1:2, :] * (lb * (1.0 - lb))

    def zspec(cb):
        return pl.BlockSpec((RB, D), lambda s: (rb(s), cb))

    rowspec = pl.BlockSpec((RB, D), lambda s: (rb(s), 0))
    in_specs = [zspec(0), zspec(1 + d), zspec(3), pl.BlockSpec((2, 2, D), lambda s: (0, 0, 0)),
                pl.BlockSpec((RB, D), lambda s: (jnp.minimum(rb(s), nlb - 1), 0)),
                pl.BlockSpec((NCK, H, DH, DH), lambda s: (rb(s), 0, 0, 0))]
    args = [z, z, z, lbl, do_lat, states]
    sumspec = pl.BlockSpec((8, D), lambda s: (0, 0))
    if last:
        in_specs += [rowspec, rowspec, rowspec, pl.BlockSpec((RB, CUT4), lambda s: (rb(s), 0))]
        args += list(prev)
        shspec = pl.BlockSpec((RB, SHC), lambda s: (rb(s), 0))
        out_specs = (shspec, shspec, sumspec)
        out_shape = (jax.ShapeDtypeStruct((m, SHC), BF), jax.ShapeDtypeStruct((m, SHC), BF))
    else:
        out_specs = (rowspec, rowspec, rowspec, sumspec)
        out_shape = (jax.ShapeDtypeStruct((m, D), F32), jax.ShapeDtypeStruct((m, D), F32),
                     jax.ShapeDtypeStruct((m, D), BF))
    return pl.pallas_call(
        body, grid=(nrb,), in_specs=in_specs, out_specs=out_specs,
        out_shape=out_shape + (jax.ShapeDtypeStruct((8, D), F32),),
        scratch_shapes=[pltpu.VMEM((H, DH, DH), F32)],
        name=f"gla_bwd{d}", compiler_params=_cp(("arbitrary",)))(*args)


def _shift_rows(v, k, r):
    row = lax.broadcasted_iota(jnp.int32, v.shape, 0)
    rolled = pltpu.roll(v, k % r, 0)
    return jnp.where((row >= k) & (row < r + k), rolled, 0.0)


def _conv_fwd(xb, cw_ref, cb_ref, r):
    return (cb_ref[...] + _shift_rows(xb, 1, r) * cw_ref[0:1, :] + xb * cw_ref[1:2, :]
            + _shift_rows(xb, -1, r) * cw_ref[2:3, :] + _shift_rows(xb, -2, r) * cw_ref[3:4, :])


def _window(ref, lo, n, rows):
    parts = []
    if lo < 0:
        parts.append(jnp.zeros((-lo,) + tuple(ref.shape[1:]), F32))
    parts.append(ref[max(lo, 0):min(lo + n, rows)])
    if lo + n > rows:
        parts.append(jnp.zeros((lo + n - rows,) + tuple(ref.shape[1:]), F32))
    return parts[0] if len(parts) == 1 else jnp.concatenate(parts, axis=0)


def _conv_cols(x_ref, cw_ref, cb_ref, r0, n, rows):
    w = _window(x_ref, r0 - 1, n + 3, rows)
    return (cb_ref[...] + w[0:n] * cw_ref[0:1, :] + w[1:n + 1] * cw_ref[1:2, :] + w[2:n + 2] * cw_ref[2:3, :]
            + w[3:n + 3] * cw_ref[3:4, :])


def _softplus_neg(lam):
    y = jnp.exp(-jnp.abs(lam))
    u = 1.0 + y
    tiny = u == 1.0
    l1p = jnp.where(tiny, y, jnp.log(u) * (y / jnp.where(tiny, 1.0, u - 1.0)))
    return jnp.maximum(-lam, 0.0) + l1p


def _gates(xc, wr_ref, br, wi_ref, bi, sp):
    xcb = xc.astype(BF)
    rs, is_ = [], []
    for g in range(H):
        gs = slice(g * DH, (g + 1) * DH)
        rs.append(jnp.dot(xcb[:, gs], wr_ref[g].astype(BF), preferred_element_type=F32))
        is_.append(jnp.dot(xcb[:, gs], wi_ref[g].astype(BF), preferred_element_type=F32))
    r = _sig_tail(jnp.concatenate(rs, axis=1) + br)
    i = _sig(jnp.concatenate(is_, axis=1) + bi)
    log_a = (-RG_C * r) * sp
    a = jnp.exp(log_a)
    t = jnp.tanh(log_a)
    om = (-2.0 * t) / (1.0 - t)
    return r, i, a, om


def _scan_rows(d, nrows, a_s, b_s, h_s, h0):
    nsl = nrows // 8

    def slab(j, h):
        jj = (nsl - 1 - j) if d else j
        r0 = pl.multiple_of(jj * 8, 8)
        for t in (range(7, -1, -1) if d else range(8)):
            h = a_s[pl.ds(r0 + t, 1), :] * h + b_s[pl.ds(r0 + t, 1), :]
            h_s[pl.ds(r0 + t, 1), :] = h
        return h

    return lax.fori_loop(0, nsl, slab, h0)


def _col_of(d, ncols):
    if d:
        return lambda s: ncols - jnp.maximum(s, 1)
    return lambda s: jnp.maximum(s, 1) - 1


def _rglru_fwd(z, cw, cb, wr, br, wi, bi, lam, d, t_lat, t_ctx):
    m = z.shape[0]
    rows = t_lat // GRID_W
    z3 = z.reshape(m // GRID_W, GRID_W, IN_COLS)
    nblk = GRID_W // CB
    blk = _col_of(d, nblk)
    cblk = t_lat // t_ctx
    rc = min(RCH, rows)

    def body(zc_ref, zx_ref, cw_ref, cb_ref, wr_ref, br_ref, wi_ref, bi_ref, lam_ref,
             hx_ref, hpx_ref, ax_ref, hpc_ref, a_s, b_s, h_s, hcar, a3, b3, cin_s):
        s = pl.program_id(0)
        sp = _softplus_neg(lam_ref[d:d + 1, :])
        br_ = br_ref[d:d + 1, :]
        bi_ = bi_ref[d:d + 1, :]

        @pl.when(s == 0)
        def _():
            xc = _conv_fwd(zc_ref[...], cw_ref, cb_ref, t_ctx)
            _, i, a, om = _gates(xc, wr_ref, br_, wi_ref, bi_, sp)
            a_s[...] = a
            b_s[...] = jnp.sqrt(om) * (i * xc)
            h0 = jnp.zeros((1, D), F32)
            hcar[0:1, :] = _scan_rows(d, t_ctx, a_s, b_s, h_s, h0)
            hs = h_s[...]
            row = lax.broadcasted_iota(jnp.int32, (t_ctx, D), 0)
            if d:
                hpc_ref[...] = jnp.where(row == t_ctx - 1, h0, pltpu.roll(hs, t_ctx - 1, 0))
            else:
                hpc_ref[...] = jnp.where(row == 0, h0, pltpu.roll(hs, 1, 0))

        @pl.when(s > 0)
        def _():
            for r0 in range(0, rows, rc):
                xc = _conv_cols(zx_ref, cw_ref, cb_ref, r0, rc, rows).reshape(rc * CB, D)
                _, i, a, om = _gates(xc, wr_ref, br_, wi_ref, bi_, sp)
                a3[r0:r0 + rc] = a.reshape(rc, CB, D)
                ax_ref[r0:r0 + rc] = a.reshape(rc, CB, D)
                b3[r0:r0 + rc] = (jnp.sqrt(om) * (i * xc)).reshape(rc, CB, D)

            def local(t, carry):
                hl, p = carry
                r = (rows - 1 - t) if d else t
                a = a3[r]
                hl = a * hl + b3[r]
                p = a * p
                b3[r] = hl
                a3[r] = p
                return hl, p

            hl, p = lax.fori_loop(0, rows, local, (jnp.zeros((CB, D), F32), jnp.ones((CB, D), F32)))
            cin = hcar[0:1, :]
            for j in (range(CB - 1, -1, -1) if d else range(CB)):
                cin_s[j:j + 1, :] = cin
                cin = hl[j:j + 1, :] + p[j:j + 1, :] * cin
            hcar[0:1, :] = cin
            c_in = cin_s[...]

            def fix(t, prev):
                r = (rows - 1 - t) if d else t
                h = b3[r] + a3[r] * c_in
                hx_ref[r] = h
                hpx_ref[r] = prev
                return h

            lax.fori_loop(0, rows, fix, c_in)

    full = lambda shp: pl.BlockSpec(shp, lambda s: (0,) * len(shp))
    colspec = pl.BlockSpec((rows, CB, D), lambda s: (0, blk(s), 0))
    outs = pl.pallas_call(
        body, grid=(nblk + 1,),
        in_specs=[pl.BlockSpec((t_ctx, D), lambda s: (cblk, 5)),
                  pl.BlockSpec((rows, CB, D), lambda s: (0, blk(s), 5)),
                  full((4, D)), full((1, D)),
                  pl.BlockSpec((None, H, DH, DH), lambda s: (d, 0, 0, 0)), full((2, D)),
                  pl.BlockSpec((None, H, DH, DH), lambda s: (d, 0, 0, 0)), full((2, D)), full((2, D))],
        out_specs=(colspec, colspec, colspec, full((t_ctx, D))),
        out_shape=(jax.ShapeDtypeStruct((rows, GRID_W, D), F32),) * 3 + (jax.ShapeDtypeStruct((t_ctx, D), F32),),
        scratch_shapes=[pltpu.VMEM((t_ctx, D), F32), pltpu.VMEM((t_ctx, D), F32), pltpu.VMEM((t_ctx, D), F32),
                        pltpu.VMEM((8, D), F32), pltpu.VMEM((rows, CB, D), F32), pltpu.VMEM((rows, CB, D), F32),
                        pltpu.VMEM((CB, D), F32)],
        name=f"rglru_fwd{d}", compiler_params=_cp(("arbitrary",)))(z, z3, cw, cb, wr, br, wi, bi, lam)
    return outs[0].reshape(t_lat, D), outs[1].reshape(t_lat, D), outs[2].reshape(t_lat, D), outs[3]


def _rglru_bwd(z, cw, cb, wr, br, wi, bi, lam, dh_lat, hp_lat, a_lat, hp_ctx, d, t_lat, t_ctx, prev=None):
    m = z.shape[0]
    rows = t_lat // GRID_W
    z3 = z.reshape(m // GRID_W, GRID_W, IN_COLS)
    nblk = GRID_W // CB
    fblk = _col_of(d, nblk)
    blk = lambda s: fblk(nblk - jnp.minimum(s, nblk - 1))
    cblk = t_lat // t_ctx
    rc = min(RCH, rows)
    last = prev is not None
    view3 = lambda v: v.reshape(rows, GRID_W, D)

    def body(*refs):
        (zc_ref, zx_ref, cw_ref, cb_ref, wr_ref, br_ref, wi_ref, bi_ref, lam_ref, dh_ref, hpx_ref, ax_ref,
         hpc_ref) = refs[:13]
        k = 13
        if last:
            pdx_ref, pdc_ref = refs[13:15]
            k = 15
        ox_ref, oc_ref, dwr_ref, dwi_ref, sum_ref, a_s, b_s, gcar, a3, b3, cin_s = refs[k:]
        s = pl.program_id(0)
        lam_d = lam_ref[d:d + 1, :]
        sp = _softplus_neg(lam_d)
        br_ = br_ref[d:d + 1, :]
        bi_ = bi_ref[d:d + 1, :]
        flat = lambda v: v.reshape(-1, D)

        @pl.when(s == 0)
        def _():
            gcar[...] = jnp.zeros_like(gcar)
            dwr_ref[...] = jnp.zeros_like(dwr_ref)
            dwi_ref[...] = jnp.zeros_like(dwi_ref)
            sum_ref[...] = jnp.zeros_like(sum_ref)


        def conv_sums(dxc, dxb, xm1, x0, xp1, xp2):
            sum_ref[3:4, :] += _colsum(flat(dxc))
            sum_ref[4:5, :] += _colsum(flat(dxb))
            sum_ref[8:9, :] += _colsum(flat(dxc * xm1))
            sum_ref[9:10, :] += _colsum(flat(dxc * x0))
            sum_ref[10:11, :] += _colsum(flat(dxc * xp1))
            sum_ref[11:12, :] += _colsum(flat(dxc * xp2))

        def gate_grads(g, hp, xc, rr, i, a, om):
            mult = jnp.sqrt(om)
            da = g * hp
            ixc = i * xc
            dmult = g * ixc
            dixc = g * mult
            di = dixc * xc
            dxc = dixc * i
            dlog_a = da * a - dmult * ((1.0 - om) / mult)
            dr = dlog_a * (-RG_C * sp)
            sum_ref[2:3, :] += _colsum(dlog_a * rr)
            drp = dr * rr * (1.0 - rr)
            dip = di * i * (1.0 - i)
            sum_ref[0:1, :] += _colsum(drp)
            sum_ref[1:2, :] += _colsum(dip)
            xcb = xc.astype(BF)
            drb = drp.astype(BF)
            dib = dip.astype(BF)
            parts = []
            for gi in range(H):
                gs = slice(gi * DH, (gi + 1) * DH)
                parts.append(_dot_nt(drb[:, gs], wr_ref[gi]) + _dot_nt(dib[:, gs], wi_ref[gi]))
                dwr_ref[gi] += _dot_tn(xcb[:, gs], drb[:, gs])
                dwi_ref[gi] += _dot_tn(xcb[:, gs], dib[:, gs])
            return dxc + jnp.concatenate(parts, axis=1)

        @pl.when(s < nblk)
        def _():
            def local(t, carry):
                c, q = carry
                r = t if d else (rows - 1 - t)
                a = ax_ref[r]
                c = a * (c + dh_ref[r])
                q = a * q
                b3[r] = c
                a3[r] = q
                return c, q

            c, q = lax.fori_loop(0, rows, local, (jnp.zeros((CB, D), F32), jnp.ones((CB, D), F32)))
            cin = gcar[0:1, :]
            for j in (range(CB) if d else range(CB - 1, -1, -1)):
                cin_s[j:j + 1, :] = cin
                cin = c[j:j + 1, :] + q[j:j + 1, :] * cin
            gcar[0:1, :] = cin
            c_in = cin_s[...]
            for r0 in (range(rows - rc, -1, -rc) if d else range(0, rows, rc)):
                if d:
                    lo = max(r0 - 1, 0)
                    cn = b3[lo:r0 + rc - 1] + a3[lo:r0 + rc - 1] * c_in
                    if r0 == 0:
                        cn = jnp.concatenate([c_in[None], cn], axis=0)
                else:
                    hi = min(r0 + rc + 1, rows)
                    cn = b3[r0 + 1:hi] + a3[r0 + 1:hi] * c_in
                    if hi == rows:
                        cn = jnp.concatenate([cn, c_in[None]], axis=0)
                g = flat(dh_ref[r0:r0 + rc] + cn)
                xc = flat(_conv_cols(zx_ref, cw_ref, cb_ref, r0, rc, rows))
                rr, i, a, om = _gates(xc, wr_ref, br_, wi_ref, bi_, sp)
                b3[r0:r0 + rc] = gate_grads(g, flat(hpx_ref[r0:r0 + rc]), xc, rr, i, a, om).reshape(rc, CB, D)
            if last:
                for r0 in range(0, rows, rc):
                    b3[r0:r0 + rc] = b3[r0:r0 + rc] + pdx_ref[r0:r0 + rc]
                for r0 in range(0, rows, rc):
                    w = _window(b3, r0 - 2, rc + 3, rows)
                    xw = _window(zx_ref, r0 - 1, rc + 3, rows)
                    dxc = w[2:rc + 2]
                    dxb = (w[3:rc + 3] * cw_ref[0:1, :] + dxc * cw_ref[1:2, :] + w[1:rc + 1] * cw_ref[2:3, :]
                           + w[0:rc] * cw_ref[3:4, :])
                    conv_sums(dxc, dxb, xw[0:rc], xw[1:rc + 1], xw[2:rc + 2], xw[3:rc + 3])
                    ox_ref[r0:r0 + rc] = dxb
            else:
                for r0 in range(0, rows, rc):
                    ox_ref[r0:r0 + rc] = b3[r0:r0 + rc]

        @pl.when(s == nblk)
        def _():
            r = t_ctx
            xb = zc_ref[...]
            xc = _conv_fwd(xb, cw_ref, cb_ref, r)
            rr, i, a, om = _gates(xc, wr_ref, br_, wi_ref, bi_, sp)
            a_s[...] = a
            b_s[...] = jnp.zeros((r, D), F32)
            c0 = gcar[0:1, :]
            _scan_rows(1 - d, r, a_s, b_s, b_s, c0)
            cs = b_s[...]
            row = lax.broadcasted_iota(jnp.int32, (r, D), 0)
            if d:
                g = jnp.where(row == 0, c0, pltpu.roll(cs, 1, 0))
            else:
                g = jnp.where(row == r - 1, c0, pltpu.roll(cs, r - 1, 0))
            dxc = gate_grads(g, hpc_ref[...], xc, rr, i, a, om)
            if last:
                dxc = dxc + pdc_ref[...]
                dxb = (_shift_rows(dxc, -1, r) * cw_ref[0:1, :] + dxc * cw_ref[1:2, :]
                       + _shift_rows(dxc, 1, r) * cw_ref[2:3, :] + _shift_rows(dxc, 2, r) * cw_ref[3:4, :])
                conv_sums(dxc, dxb, _shift_rows(xb, 1, r), xb, _shift_rows(xb, -1, r), _shift_rows(xb, -2, r))
                oc_ref[...] = dxb
            else:
                oc_ref[...] = dxc
            sum_ref[2:3, :] = sum_ref[2:3, :] * (RG_C * _sig_tail(-lam_d))

    full = lambda shp: pl.BlockSpec(shp, lambda s: (0,) * len(shp))
    once = lambda shp: pl.BlockSpec(shp, lambda s: (0,) * len(shp), pipeline_mode=pl.Buffered(1))
    colspec = pl.BlockSpec((rows, CB, D), lambda s: (0, blk(s), 0))
    colonce = pl.BlockSpec((rows, CB, D), lambda s: (0, blk(s), 0), pipeline_mode=pl.Buffered(1))
    in_specs = [pl.BlockSpec((t_ctx, D), lambda s: (cblk, 5), pipeline_mode=pl.Buffered(1)),
                pl.BlockSpec((rows, CB, D), lambda s: (0, blk(s), 5), pipeline_mode=pl.Buffered(1)),
                full((4, D)), full((1, D)),
                pl.BlockSpec((None, H, DH, DH), lambda s: (d, 0, 0, 0)), full((2, D)),
                pl.BlockSpec((None, H, DH, DH), lambda s: (d, 0, 0, 0)), full((2, D)), full((2, D)),
                colspec, colspec, colonce, once((t_ctx, D))]
    args = [z, z3, cw, cb, wr, br, wi, bi, lam, view3(dh_lat), view3(hp_lat), view3(a_lat), hp_ctx]
    if last:
        in_specs += [colonce, once((t_ctx, D))]
        args += [view3(prev[0]), prev[1]]
    outs = pl.pallas_call(
        body, grid=(nblk + 1,), in_specs=in_specs,
        out_specs=(colspec, full((t_ctx, D)), full((H, DH, DH)), full((H, DH, DH)), full((16, D))),
        out_shape=(jax.ShapeDtypeStruct((rows, GRID_W, D), F32), jax.ShapeDtypeStruct((t_ctx, D), F32),
                   jax.ShapeDtypeStruct((H, DH, DH), F32), jax.ShapeDtypeStruct((H, DH, DH), F32),
                   jax.ShapeDtypeStruct((16, D), F32)),
        scratch_shapes=[pltpu.VMEM((t_ctx, D), F32), pltpu.VMEM((t_ctx, D), F32), pltpu.VMEM((8, D), F32),
                        pltpu.VMEM((rows, CB, D), F32), pltpu.VMEM((rows, CB, D), F32), pltpu.VMEM((CB, D), F32)],
        name=f"rglru_bwd{d}", compiler_params=_cp(("arbitrary",), VMEM_LIMIT_RGLRU_BWD))(*args)
    return (outs[0].reshape(t_lat, D), outs[1]) + tuple(outs[2:])


def _merge(o_f, o_b, h_f, h_b, z, x, tgt, mod, norm_g, ln_g, ln_b, p_a, p_b, w_out, t_lat):
    tm = 256
    nt = t_lat // tm

    def body(of_ref, ob_ref, hf_ref, hb_ref, z4_ref, z6_ref, z7_ref, z8_ref, x_ref, t_ref, mod_ref, ng_ref,
             lg_ref, lb_ref, pa_ref, pb_ref, wo_ref,
             do_ref, dh_ref, dz4a_ref, dz4b_ref, dz6a_ref, sh3_ref, gx_ref,
             y_ref, dout_ref, oa_ref, dpa_ref, obv_ref, dpb_ref, acc_ref):
        i = pl.program_id(0)
        lat = i < nt
        latf = lat.astype(F32)

        @pl.when(i == 0)
        def _():
            acc_ref[...] = jnp.zeros_like(acc_ref)

        def per_head(v):
            return jnp.concatenate(
                [jnp.broadcast_to(jnp.mean(v[:, h * DH:(h + 1) * DH], axis=-1, keepdims=True), (tm, DH))
                 for h in range(H)], axis=1)

        gt = mod_ref[0:1, 2 * D:3 * D]
        gfull = jnp.concatenate([ng_ref[...]] * H, axis=1)
        o = of_ref[...] + ob_ref[...]
        rinv = lax.rsqrt(per_head(o * o) + RMS_EPS)
        n = o * rinv
        na = n * gfull
        z4 = z4_ref[...]
        s4 = _sig(z4)
        silu4 = z4 * s4
        oa = na * silu4
        z6 = z6_ref[...]
        s6 = _sig(z6)
        silu6 = z6 * s6
        hsum = hf_ref[...] + hb_ref[...]
        obv = hsum * silu6
        pa = _dot(oa, pa_ref[...])
        pb = _dot(obv, pb_ref[...])
        s7 = _sig(z7_ref[...])
        s8 = _sig(z8_ref[...])
        y = s7 * pa + s8 * pb
        out = _dot(y, wo_ref[...])
        pre = ALPHA * x_ref[...] + gt * out
        mu = jnp.mean(pre, axis=-1, keepdims=True)
        xc = pre - mu
        rstd = lax.rsqrt(jnp.mean(xc * xc, axis=-1, keepdims=True) + LN_EPS)
        xhat = xc * rstd
        lg = lg_ref[...]
        diff = xhat * lg + lb_ref[...] - t_ref[...]
        acc_ref[8:9, :] += _colsum(diff * diff) * (0.5 / D * latf)
        dxn = diff * (1.0 / D)
        acc_ref[1:2, :] += _colsum(dxn * xhat) * latf
        acc_ref[2:3, :] += _colsum(dxn) * latf
        dxhat = dxn * lg
        dpre = rstd * (dxhat - jnp.mean(dxhat, axis=-1, keepdims=True)
                       - xhat * jnp.mean(dxhat * xhat, axis=-1, keepdims=True))
        gx_ref[...] = ALPHA * dpre
        acc_ref[0:1, :] += _colsum(dpre * out) * latf
        dout = dpre * gt
        dy = _dot_nt(dout, wo_ref[...])
        dpa = dy * s7
        dpb = dy * s8
        dz7 = dy * pa * (s7 * (1.0 - s7))
        dz8 = dy * pb * (s8 * (1.0 - s8))
        doa = _dot_nt(dpa, pa_ref[...])
        dob = _dot_nt(dpb, pb_ref[...])
        dh_ref[...] = dob * silu6
        dz6 = dob * hsum * (s6 * (1.0 + z6 * (1.0 - s6)))
        dna = doa * silu4
        dz4 = doa * na * (s4 * (1.0 + z4 * (1.0 - s4)))
        dng = _colsum(dna * n)
        acc_ref[7:8, 0:DH] += sum(dng[:, h * DH:(h + 1) * DH] for h in range(H)) * latf
        dn = dna * gfull
        do_ref[...] = rinv * (dn - n * per_head(dn * n))
        acc_ref[3:4, :] += _colsum(dz4) * latf
        acc_ref[4:5, :] += _colsum(dz6) * latf
        acc_ref[5:6, :] += _colsum(dz7) * latf
        acc_ref[6:7, :] += _colsum(dz8) * latf
        dz4b, dz6b = (dz4 * latf).astype(BF), (dz6 * latf).astype(BF)
        dz4a_ref[...] = dz4b[:, :CUT4]
        dz4b_ref[...] = dz4b[:, CUT4:]
        dz6a_ref[...] = dz6b[:, :CUT6]
        sh3_ref[:, 0:D - CUT6] = dz6b[:, CUT6:]
        sh3_ref[:, D - CUT6:2 * D - CUT6] = (dz7 * latf).astype(BF)
        sh3_ref[:, 2 * D - CUT6:] = (dz8 * latf).astype(BF)
        y_ref[...] = y.astype(BF)
        dout_ref[...] = dout.astype(BF)
        oa_ref[...] = oa.astype(BF)
        dpa_ref[...] = dpa.astype(BF)
        obv_ref[...] = obv.astype(BF)
        dpb_ref[...] = dpb.astype(BF)

        @pl.when(i == nt - 1)
        def _():
            acc_ref[9:10, :] = jnp.broadcast_to(jnp.sum(acc_ref[8:9, :], axis=-1, keepdims=True), (1, D))

    m = z.shape[0]
    lrow = lambda i: jnp.minimum(i, nt - 1)
    row = pl.BlockSpec((tm, D), lambda i: (lrow(i), 0))
    allrow = lambda cols: pl.BlockSpec((tm, cols), lambda i: (i, 0))
    zs = lambda cb: pl.BlockSpec((tm, D), lambda i: (lrow(i), cb))
    full = lambda shp: pl.BlockSpec(shp, lambda i: (0,) * len(shp))
    wfull = pl.BlockSpec((D, D), lambda i: (0, 0), pipeline_mode=pl.Buffered(1))
    f32o = jax.ShapeDtypeStruct((t_lat, D), F32)
    bfo = jax.ShapeDtypeStruct((t_lat, D), BF)
    bfall = lambda cols: jax.ShapeDtypeStruct((m, cols), BF)
    return pl.pallas_call(
        body, grid=(m // tm,),
        in_specs=[row, row, row, row, zs(4), zs(6), zs(7), zs(8), row, row, full((16, 3 * D)), full((1, DH)),
                  full((1, D)), full((1, D)), wfull, wfull, wfull],
        out_specs=(row, row) + tuple(allrow(c) for c in (CUT4, D - CUT4, CUT6, SHC)) + (row,) * 7 + (full((16, D)),),
        out_shape=(f32o, f32o, bfall(CUT4), bfall(D - CUT4), bfall(CUT6), bfall(SHC), f32o, bfo, bfo, bfo, bfo, bfo, bfo,
                   jax.ShapeDtypeStruct((16, D), F32)),
        name="merge", compiler_params=_cp(("arbitrary",), VMEM_LIMIT_MERGE))(
            o_f, o_b, h_f, h_b, z, z, z, z, x, tgt, mod, norm_g, ln_g, ln_b, p_a, p_b, w_out)


def _wgrad(a, b, name):
    tm = 1024

    def body(a_ref, b_ref, o_ref):
        @pl.when(pl.program_id(0) == 0)
        def _():
            o_ref[...] = jnp.zeros_like(o_ref)
        o_ref[...] += _dot_tn(a_ref[...], b_ref[...])

    row = pl.BlockSpec((tm, D), lambda i: (i, 0))
    return pl.pallas_call(body, grid=(a.shape[0] // tm,), in_specs=[row, row],
                          out_specs=pl.BlockSpec((D, D), lambda i: (0, 0)),
                          out_shape=jax.ShapeDtypeStruct((D, D), F32), name=name,
                          compiler_params=_cp(("arbitrary",)))(a, b)


def _pack_shard2(dz4b, dz5_lat, dz5_ctx, dz6a):
    m, t_lat, t_ctx = dz4b.shape[0], dz5_lat.shape[0], dz5_ctx.shape[0]
    tm = t_ctx
    nlt = t_lat // tm
    w4 = D - CUT4

    def body(a_ref, bl_ref, bc_ref, c_ref, o_ref):
        i = pl.program_id(0)
        o_ref[:, 0:w4] = a_ref[...]
        o_ref[:, w4:w4 + D] = jnp.where(i < nlt, bl_ref[...], bc_ref[...]).astype(BF)
        o_ref[:, w4 + D:] = c_ref[...]

    return pl.pallas_call(
        body, grid=(m // tm,),
        in_specs=[pl.BlockSpec((tm, w4), lambda i: (i, 0)),
                  pl.BlockSpec((tm, D), lambda i: (jnp.minimum(i, nlt - 1), 0)),
                  pl.BlockSpec((tm, D), lambda i: (0, 0)),
                  pl.BlockSpec((tm, CUT6), lambda i: (i, 0))],
        out_specs=pl.BlockSpec((tm, SHC), lambda i: (i, 0)),
        out_shape=jax.ShapeDtypeStruct((m, SHC), BF), name="pack_shard2",
        compiler_params=_cp(("arbitrary",)))(dz4b, dz5_lat, dz5_ctx, dz6a)


def _wgrad_in(u_all, dz_shards, others):
    m = u_all.shape[1]
    assert m % (6 * 128) == 0
    tm = m // 6
    nt = m // tm
    no = len(others)
    spec_u, spec_dz = pl.BlockSpec((D, tm), lambda i: (0, i)), pl.BlockSpec((tm, SHC), lambda i: (i, 0))
    oshape = jax.ShapeDtypeStruct((NSH, D, SHC), F32)
    out = None
    for k, dz_k in enumerate(dz_shards[:-1]):
        def body(u_ref, dz_ref, *rest):
            o_ref = rest[-1]

            @pl.when(pl.program_id(0) == 0)
            def _():
                o_ref[...] = jnp.zeros_like(o_ref)
            o_ref[0] += _dot(u_ref[...], dz_ref[...])

        out = pl.pallas_call(
            body, grid=(nt,), in_specs=[spec_u, spec_dz] + ([] if out is None else [_ANY]),
            out_specs=pl.BlockSpec((1, D, SHC), lambda i, k=k: (k, 0, 0)), out_shape=oshape,
            input_output_aliases={} if out is None else {2: 0},
            name=f"wgrad_in{k}", compiler_params=_cp(("arbitrary",)))(u_all, dz_k, *(() if out is None else (out,)))

    def last(u_ref, dz_ref, done_ref, *rest):
        other_refs, o_ref, got_refs = rest[:no], rest[no], rest[no + 1:2 * no + 2]
        send, recv = rest[2 * no + 2:]
        i = pl.program_id(0)
        give = ([done_ref] + list(other_refs), got_refs, send, recv, [tuple(range(NSH - 1))] + [tuple(range(NSH))] * no)

        @pl.when(i == 0)
        def _():
            o_ref[...] = jnp.zeros_like(o_ref)
            for cp in _rs_sibling_copies(*give):
                cp.start()
        o_ref[0] += _dot(u_ref[...], dz_ref[...])

        @pl.when(i == nt - 1)
        def _():
            for cp in _rs_sibling_copies(*give):
                cp.wait()

    half = lambda g: jax.ShapeDtypeStruct((NSH, g.shape[1] // 2, g.shape[2]), F32)
    outs = pl.pallas_call(
        last, grid=(nt,), in_specs=[spec_u, spec_dz, _ANY] + [_ANY] * no,
        out_specs=[pl.BlockSpec((1, D, SHC), lambda i: (NSH - 1, 0, 0))] + [_ANY] * (no + 1),
        out_shape=[oshape, half(oshape)] + [half(g) for g in others], input_output_aliases={2: 0},
        scratch_shapes=[pltpu.SemaphoreType.DMA((NSH * (no + 1),)), pltpu.SemaphoreType.DMA((NSH * (no + 1),))],
        name=f"wgrad_in{NSH - 1}", compiler_params=_cp(("arbitrary",)))(u_all, dz_shards[-1], out, *others)
    return outs[0], list(outs[1:])


def _du(dz_shards, w_in_g, x, ctx, mod, gxres, sums=()):
    tm = 256
    n_lat_tiles, nct = x.shape[0] // tm, ctx.shape[0] // tm
    nt = n_lat_tiles + nct
    ns = len(sums)
    rblk = lambda i: jnp.where(i < nct, n_lat_tiles + i, i - nct)
    lblk = lambda i: jnp.maximum(i - nct, 0)

    def body(*refs):
        dz_refs, refs = refs[:NSH], refs[NSH:]
        w_ref, x_ref, c_ref, mod_ref, gr_ref = refs[:5]
        sum_refs = refs[5:5 + ns]
        gx_ref, dm_ref = refs[5 + ns:7 + ns]
        got_refs = refs[7 + ns:7 + 2 * ns]
        sems = refs[7 + 2 * ns:]
        i = pl.program_id(0)
        is_lat = i >= nct

        @pl.when(i == 0)
        def _():
            dm_ref[...] = jnp.zeros_like(dm_ref)
            if ns:
                for cp in _rs_chip_copies(sum_refs, got_refs, *sems):
                    cp.start()

        du = _dot_nt(dz_refs[0][...], w_ref[0])
        for n in range(1, NSH):
            du = du + _dot_nt(dz_refs[n][...], w_ref[n])
        sc = jnp.where(is_lat, mod_ref[0:1, D:2 * D], mod_ref[1:2, D:2 * D])
        dsh = _colsum(du)
        dsc = _colsum(du * jnp.where(is_lat, x_ref[...], c_ref[...]))

        @pl.when(is_lat)
        def _():
            gx_ref[...] = du * (1.0 + sc) + gr_ref[...]
            dm_ref[0:1, 0:D] += dsh
            dm_ref[0:1, D:2 * D] += dsc

        @pl.when(jnp.logical_not(is_lat))
        def _():
            dm_ref[1:2, 0:D] += dsh
            dm_ref[1:2, D:2 * D] += dsc

        if ns:
            @pl.when(i == nt - 1)
            def _():
                for cp in _rs_chip_copies(sum_refs, got_refs, *sems):
                    cp.wait()

    outs = pl.pallas_call(
        body, grid=(nt,),
        in_specs=[pl.BlockSpec((tm, SHC), lambda i: (rblk(i), 0))] * NSH + [
                  pl.BlockSpec((NSH, D, SHC), lambda i: (0, 0, 0), pipeline_mode=pl.Buffered(1)),
                  pl.BlockSpec((tm, D), lambda i: (lblk(i), 0)),
                  pl.BlockSpec((tm, D), lambda i: (jnp.minimum(i, nct - 1), 0)),
                  pl.BlockSpec((16, 3 * D), lambda i: (0, 0)),
                  pl.BlockSpec((tm, D), lambda i: (lblk(i), 0))] + [_ANY] * ns,
        out_specs=[pl.BlockSpec((tm, D), lambda i: (lblk(i), 0)),
                   pl.BlockSpec((8, 2 * D), lambda i: (0, 0))] + [_ANY] * ns,
        out_shape=[jax.ShapeDtypeStruct((n_lat_tiles * tm, D), F32), jax.ShapeDtypeStruct((8, 2 * D), F32)]
        + [jax.ShapeDtypeStruct((3,) + g.shape[1:], g.dtype) for g in sums],
        scratch_shapes=[pltpu.SemaphoreType.DMA((3 * ns,)), pltpu.SemaphoreType.DMA((3 * ns,))] if ns else [],
        name="du", compiler_params=_cp(("arbitrary",)))(*dz_shards, w_in_g, x, ctx, mod, gxres, *sums)
    return outs[0], outs[1], list(outs[2:])


def _row_tile(rows, cols, block_bytes=1 << 20):
    t = 8
    while t * 2 * cols * 4 <= block_bytes and rows % (t * 2) == 0:
        t *= 2
    return t


def _adamw_update(w_ref, g_ref, m_ref, v_ref, d_ref, nm_ref, nv_ref):
    gg = g_ref[...]
    m2 = ADAM_B1 * m_ref[...] + (1.0 - ADAM_B1) * gg
    v2 = ADAM_B2 * v_ref[...] + (1.0 - ADAM_B2) * (gg * gg)
    m_hat = m2 / (1.0 - ADAM_B1 ** ADAM_STEP)
    v_hat = v2 / (1.0 - ADAM_B2 ** ADAM_STEP)
    d_ref[...] = -ADAM_LR * (m_hat / (jnp.sqrt(v_hat) + ADAM_EPS) + ADAM_WD * w_ref[...])
    nm_ref[...] = m2
    nv_ref[...] = v2


def _adamw_many(ws, gs, ms, vs):
    n = len(ws)

    def body(*refs):
        for j in range(n):
            _adamw_update(*refs[4 * j:4 * j + 4], *refs[4 * n + 3 * j:4 * n + 3 * j + 3])

    args = [a for quad in zip(ws, gs, ms, vs) for a in quad]
    outs = pl.pallas_call(body, out_shape=[jax.ShapeDtypeStruct(w.shape, F32) for w in ws for _ in range(3)],
                          name="adamw_small", compiler_params=_cp())(*args)
    return outs[0::3], outs[1::3], outs[2::3]


def _adamw(ws, gs, ms, vs, name):
    n = len(ws)
    rows, cols = ws[0].shape
    tr = _row_tile(rows, cols, (2 << 20) // n)

    def body(*refs):
        for j in range(n):
            _adamw_update(*refs[4 * j:4 * j + 4], *refs[4 * n + 3 * j:4 * n + 3 * j + 3])

    spec = pl.BlockSpec((tr, cols), lambda i: (i, 0))
    o = jax.ShapeDtypeStruct((rows, cols), F32)
    args = [a for quad in zip(ws, gs, ms, vs) for a in quad]
    outs = pl.pallas_call(body, grid=(rows // tr,), in_specs=[spec] * (4 * n), out_specs=[spec] * (3 * n),
                          out_shape=[o] * (3 * n), name=name, compiler_params=_cp(("arbitrary",)))(*args)
    return outs[0::3], outs[1::3], outs[2::3]


_ANY = pl.BlockSpec(memory_space=pl.ANY)


def _place():
    return lax.axis_index("x"), lax.axis_index("y"), lax.axis_index("c")


def _ag_copies(ins, outs, send, recv, fsend, frecv, lsem):
    x, y, c = _place()
    me = 2 * x + y
    chips = ((1 - x, y), (x, 1 - y), (1 - x, 1 - y))
    local, chip, hand = [], [], []
    for j in range(len(ins)):
        hr = ins[j].shape[0] // 2
        half = pl.ds(pl.multiple_of(c * hr, 8), hr)
        local.append(pltpu.make_async_copy(ins[j], outs[j].at[me], lsem.at[j]))
        for k, (px, py) in enumerate(chips):
            chip.append(pltpu.make_async_remote_copy(
                src_ref=ins[j].at[half, :], dst_ref=outs[j].at[me, half, :], send_sem=send.at[3 * j + k],
                recv_sem=recv.at[3 * j + k], device_id=(px, py, c), device_id_type=MESH))
            got = outs[j].at[2 * px + py, half, :]
            hand.append(pltpu.make_async_remote_copy(
                src_ref=got, dst_ref=got, send_sem=fsend.at[3 * j + k], recv_sem=frecv.at[3 * j + k],
                device_id=(x, y, 1 - c), device_id_type=MESH))
    return local, chip, hand


def _ag_sems(n):
    return [pltpu.SemaphoreType.DMA((3 * n,))] * 4 + [pltpu.SemaphoreType.DMA((n,))]


def _ag_finish(local, chip, hand, done=0):
    for k in range(done, len(chip)):
        chip[k].wait_recv()
        hand[k].start()
    for cp in chip:
        cp.wait_send()
    for k in range(done):
        hand[k].wait_send()
    for k in range(done, len(chip)):
        hand[k].wait_send()
        hand[k].wait_recv()
    for cp in local:
        cp.wait()


def _mod_tp(c8, c_ctx, w_mod_sh, b_mod_sh):
    mc = w_mod_sh.shape[1]

    def body(c8_ref, cctx_ref, w_ref, b_ref, mod_ref, cc_ref, cc_s, part_s, send1, recv1, send3, recv3):
        x, y, c = _place()
        me = 4 * x + 2 * y + c
        ms = 2 * x + y
        copies = []
        for k in range(1, 8):
            peer = (x ^ ((k >> 2) & 1), y ^ ((k >> 1) & 1), c ^ (k & 1))
            cp = pltpu.make_async_remote_copy(src_ref=c8_ref, dst_ref=cc_s.at[me], send_sem=send1.at[k],
                                              recv_sem=recv1.at[k], device_id=peer, device_id_type=MESH)
            cp.start()
            copies.append(cp)
        cc_s[me] = c8_ref[...]
        for cp in copies:
            cp.wait()
        cc_ref[...] = jnp.zeros_like(cc_ref)
        for j in range(8):
            cc_ref[j:j + 1, :] = cc_s[j, 0:1, :]
        cc_ref[8:9, :] = cctx_ref[...]
        v = cc_ref[...]
        part_s[ms] = _dot(v * _sig(v), w_ref[...]) + b_ref[...]
        copies = []
        for k in range(1, 4):
            peer = (x ^ ((k >> 1) & 1), y ^ (k & 1), c)
            cp = pltpu.make_async_remote_copy(src_ref=part_s.at[ms], dst_ref=part_s.at[ms], send_sem=send3.at[k],
                                              recv_sem=recv3.at[k], device_id=peer, device_id_type=MESH)
            cp.start()
            copies.append(cp)
        for cp in copies:
            cp.wait()
        for s in range(NSH):
            mod_ref[:, s * mc:(s + 1) * mc] = part_s[s]

    vm = pl.BlockSpec(memory_space=pltpu.VMEM)
    return pl.pallas_call(
        body, in_specs=[vm] * 4, out_specs=(vm, vm),
        out_shape=(jax.ShapeDtypeStruct((16, NSH * mc), F32), jax.ShapeDtypeStruct((16, D), F32)),
        scratch_shapes=[pltpu.VMEM((8, 8, D), F32), pltpu.VMEM((NSH, 16, mc), F32),
                        pltpu.SemaphoreType.DMA((8,)), pltpu.SemaphoreType.DMA((8,)),
                        pltpu.SemaphoreType.DMA((4,)), pltpu.SemaphoreType.DMA((4,))],
        name="mod_tp", compiler_params=_cp())(c8, c_ctx, w_mod_sh, b_mod_sh)


def _inproj_ag(x, ctx, mod, w_in_sh, b_in, narrow_sh):
    n_lat = x.shape[0]
    m = n_lat + ctx.shape[0]
    assert m % (11 * 128) == 0
    tm = m // 11
    nt = m // tm
    nfull = n_lat // tm
    assert nfull == nt - 1
    tail = jnp.concatenate([x[nfull * tm:], ctx], axis=0)
    x_, y_ = lax.axis_index("x"), lax.axis_index("y")
    sids = jnp.stack([2 * x_ + y_, 2 * (1 - x_) + y_, 2 * x_ + 1 - y_, 2 * (1 - x_) + 1 - y_]).astype(jnp.int32)

    def body(sid_ref, x_ref, tail_ref, mod_ref, b_ref, wsh_ref, nsh_ref, z_ref, u_ref, wg_ref, ng_ref, w_s, u_s, *sems):
        n = pl.program_id(0)
        i = pl.program_id(1)
        rows = pl.ds(pl.multiple_of(i * tm, tm), tm)
        ag = ((wsh_ref, nsh_ref), (wg_ref, ng_ref)) + tuple(sems[:5])
        wsem = sems[5]

        def load(src):
            cp = pltpu.make_async_copy(src, w_s, wsem)
            cp.start()
            cp.wait()

        @pl.when((n == 0) & (i == 0))
        def _():
            local, chip, _ = _ag_copies(*ag)
            for cp in chip + local:
                cp.start()
            load(wsh_ref)

        for k in range(NSH - 1):
            @pl.when((n == k + 1) & (i == 0))
            def _():
                _, chip, hand = _ag_copies(*ag)
                chip[k].wait_recv()
                hand[k].start()
                hand[k].wait_recv()
                load(wg_ref.at[sid_ref[k + 1]])

        @pl.when(n == 0)
        def _():
            is_lat = (i * tm + lax.broadcasted_iota(jnp.int32, (tm, 1), 0)) < n_lat
            u = _modulate(jnp.where(i < nfull, x_ref[...], tail_ref[...]), mod_ref, is_lat)
            u_s[rows, :] = u.astype(BF)
            u_ref[...] = u.T.astype(BF)

        z_ref[...] = _dot(u_s[rows, :], w_s[...]) + b_ref[...]

        @pl.when((n == NSH - 1) & (i == nt - 1))
        def _():
            _ag_finish(*_ag_copies(*ag), done=NSH - 1)

    first = lambda n, i: jnp.where(n == 0, i, nt - 1)
    outs = pl.pallas_call(
        body, grid_spec=pltpu.PrefetchScalarGridSpec(
            num_scalar_prefetch=1, grid=(NSH, nt),
            in_specs=[pl.BlockSpec((tm, D), lambda n, i, sid: (jnp.minimum(first(n, i), nfull - 1), 0)),
                      pl.BlockSpec((tm, D), lambda n, i, sid: (0, 0)),
                      pl.BlockSpec((16, 3 * D), lambda n, i, sid: (0, 0)),
                      pl.BlockSpec((1, SHC), lambda n, i, sid: (0, sid[n])), _ANY, _ANY],
            out_specs=[pl.BlockSpec((tm, SHC), lambda n, i, sid: (i, sid[n])),
                       pl.BlockSpec((D, tm), lambda n, i, sid: (0, first(n, i))), _ANY, _ANY],
            scratch_shapes=[pltpu.VMEM((D, SHC), BF), pltpu.VMEM((m, D), BF)] + _ag_sems(2)
            + [pltpu.SemaphoreType.DMA]),
        out_shape=[jax.ShapeDtypeStruct((m, IN_COLS), F32), jax.ShapeDtypeStruct((D, m), BF),
                   jax.ShapeDtypeStruct((NSH,) + w_in_sh.shape, BF),
                   jax.ShapeDtypeStruct((NSH,) + narrow_sh.shape, narrow_sh.dtype)],
        name="inproj_ag", compiler_params=_cp(("arbitrary", "arbitrary")))(sids, x, tail, mod, b_in, w_in_sh, narrow_sh)
    return outs


def _rs_sibling_copies(ins, got, send, recv, shards):
    x, y, c = _place()
    copies = []
    for j in range(len(ins)):
        hr = ins[j].shape[1] // 2
        for s in shards[j]:
            give = ins[j].at[s, pl.ds(pl.multiple_of((1 - c) * hr, 8), hr), :]
            copies.append(pltpu.make_async_remote_copy(
                src_ref=give, dst_ref=got[j].at[s], send_sem=send.at[NSH * j + s], recv_sem=recv.at[NSH * j + s],
                device_id=(x, y, 1 - c), device_id_type=MESH))
    return copies


def _rs_sibling_last(g, got):
    def body(g_ref, _, got_ref, send, recv):
        copies = _rs_sibling_copies([g_ref], [got_ref], send, recv, [(NSH - 1,)])
        for cp in copies:
            cp.start()
        for cp in copies:
            cp.wait()

    return pl.pallas_call(
        body, in_specs=[_ANY, _ANY], out_specs=_ANY, out_shape=jax.ShapeDtypeStruct(got.shape, got.dtype),
        input_output_aliases={1: 0},
        scratch_shapes=[pltpu.SemaphoreType.DMA((NSH,)), pltpu.SemaphoreType.DMA((NSH,))],
        name="rs_sibling_last")(g, got)


def _core_vec():
    return lax.axis_index("c").astype(jnp.int32).reshape(1)


def _rs_add1(gs, gots, name):
    n = len(gs)
    _, r, cols = gs[0].shape
    hr = r // 2
    tr = _row_tile(hr, cols, (4 << 20) // n)
    nb = hr // tr

    def body(c_ref, *refs):
        for j in range(n):
            refs[2 * n + j][...] = (refs[j][...] + refs[n + j][...]).astype(BF)

    spec = pl.BlockSpec((1, tr, cols), lambda s, i, c_ref: (s, i, 0))
    keep = pl.BlockSpec((1, tr, cols), lambda s, i, c_ref: (s, c_ref[0] * nb + i, 0))
    return pl.pallas_call(
        body, grid_spec=pltpu.PrefetchScalarGridSpec(
            num_scalar_prefetch=1, grid=(NSH, nb), in_specs=[keep] * n + [spec] * n, out_specs=[spec] * n),
        out_shape=[jax.ShapeDtypeStruct((NSH, hr, cols), BF)] * n, name=name,
        compiler_params=_cp(("arbitrary", "arbitrary")))(_core_vec(), *gs, *gots)


def _rs_add2(sums, gots, name):
    n = len(sums)
    _, hr, cols = sums[0].shape
    tr = _row_tile(hr, cols, (2 << 20) // n)
    nb = hr // tr
    place = jnp.stack([2 * lax.axis_index("x") + lax.axis_index("y"), lax.axis_index("c")]).astype(jnp.int32)

    def body(p_ref, *refs):
        f = lambda v: v.astype(F32)
        for j in range(n):
            s_ref, got_ref = refs[j], refs[n + j]
            refs[2 * n + j][...] = f(s_ref[0]) + f(got_ref[0]) + f(got_ref[1]) + f(got_ref[2])

    return pl.pallas_call(
        body, grid_spec=pltpu.PrefetchScalarGridSpec(
            num_scalar_prefetch=1, grid=(nb,),
            in_specs=[pl.BlockSpec((1, tr, cols), lambda i, p_ref: (p_ref[0], i, 0))] * n
            + [pl.BlockSpec((3, tr, cols), lambda i, p_ref: (0, i, 0))] * n,
            out_specs=[pl.BlockSpec((tr, cols), lambda i, p_ref: (p_ref[1] * nb + i, 0))] * n),
        out_shape=[jax.ShapeDtypeStruct((2 * hr, cols), F32)] * n, name=name,
        compiler_params=_cp(("arbitrary",)))(place, *sums, *gots)


def _rs_chip_copies(ins, got, send, recv):
    x, y, c = _place()
    peers = ((1 - x, y), (x, 1 - y), (1 - x, 1 - y))
    return [pltpu.make_async_remote_copy(src_ref=ins[j].at[2 * px + py], dst_ref=got[j].at[k],
                                         send_sem=send.at[3 * j + k], recv_sem=recv.at[3 * j + k],
                                         device_id=(px, py, c), device_id_type=MESH)
            for j in range(len(ins)) for k, (px, py) in enumerate(peers)]


def _ag_sibling(fulls):
    n = len(fulls)
    nck = 4

    def body(*refs):
        outs = refs[n:2 * n]
        send, recv = refs[2 * n:]
        x, y, c = _place()
        copies = []
        for j in range(n):
            qr = outs[j].shape[0] // (2 * nck)
            for k in range(nck):
                rows = outs[j].at[pl.ds(pl.multiple_of((c * nck + k) * qr, 8), qr), :]
                cp = pltpu.make_async_remote_copy(src_ref=rows, dst_ref=rows, send_sem=send.at[nck * j + k],
                                                  recv_sem=recv.at[nck * j + k], device_id=(x, y, 1 - c),
                                                  device_id_type=MESH)
                cp.start()
                copies.append(cp)
        for cp in copies:
            cp.wait()

    return pl.pallas_call(
        body, in_specs=[_ANY] * n, out_specs=[_ANY] * n,
        out_shape=[jax.ShapeDtypeStruct(f.shape, F32) for f in fulls],
        input_output_aliases={j: j for j in range(n)},
        scratch_shapes=[pltpu.SemaphoreType.DMA((nck * n,)), pltpu.SemaphoreType.DMA((nck * n,))],
        name="ag_sibling")(*fulls)


def _allreduce_small(buf):
    rows = buf.shape[0]
    pr = rows // 8

    def body(in_ref, out_ref, part, stage_a, stage_b, *sems):
        (sa, ra, sb, rb, sc, rc, sd, rd) = sems
        x, y, c = _place()
        s = 2 * x + y
        sib = (x, y, 1 - c)
        chips = ((1 - x, y), (x, 1 - y), (1 - x, 1 - y))

        def piece(ref, chip, core):
            return ref.at[pl.ds(pl.multiple_of((2 * chip + core) * pr, 8), pr), :]

        def run(copies):
            for cp in copies:
                cp.start()
            for cp in copies:
                cp.wait()

        run([pltpu.make_async_remote_copy(src_ref=piece(in_ref, j, 1 - c), dst_ref=stage_a.at[j], send_sem=sa.at[j],
                                          recv_sem=ra.at[j], device_id=sib, device_id_type=MESH) for j in range(NSH)])
        for j in range(NSH):
            part[j] = piece(in_ref, j, c)[...] + stage_a[j]
        run([pltpu.make_async_remote_copy(src_ref=part.at[2 * px + py], dst_ref=stage_b.at[k], send_sem=sb.at[k],
                                          recv_sem=rb.at[k], device_id=(px, py, c), device_id_type=MESH)
             for k, (px, py) in enumerate(chips)])
        piece(out_ref, s, c)[...] = part[s] + stage_b[0] + stage_b[1] + stage_b[2]
        run([pltpu.make_async_remote_copy(src_ref=piece(out_ref, s, c), dst_ref=piece(out_ref, s, c),
                                          send_sem=sc.at[k], recv_sem=rc.at[k], device_id=(px, py, c),
                                          device_id_type=MESH) for k, (px, py) in enumerate(chips)])
        run([pltpu.make_async_remote_copy(src_ref=piece(out_ref, j, c), dst_ref=piece(out_ref, j, c), send_sem=sd.at[j],
                                          recv_sem=rd.at[j], device_id=sib, device_id_type=MESH) for j in range(NSH)])

    vm = pl.BlockSpec(memory_space=pltpu.VMEM)
    return pl.pallas_call(
        body, in_specs=[vm], out_specs=vm, out_shape=jax.ShapeDtypeStruct((rows, D), F32),
        scratch_shapes=[pltpu.VMEM((NSH, pr, D), F32), pltpu.VMEM((NSH, pr, D), F32), pltpu.VMEM((3, pr, D), F32)]
        + [pltpu.SemaphoreType.DMA((NSH,))] * 8,
        name="allreduce_small", compiler_params=_cp())(buf)


def _pad_rows(a, mult):
    pad = (-a.shape[0]) % mult
    return jnp.concatenate([a, jnp.zeros((pad, a.shape[1]), a.dtype)]) if pad else a


def _local_step(x, c, ctx, c_ctx, tgt, me, shard, sh, b_mod, b_in, norm_g, cb, wr, wi, ln_g, ln_b):
    t_lat, t_ctx = x.shape[0], ctx.shape[0]
    nlb, ncb = t_lat // RB, t_ctx // RB
    mc = 3 * D // NSH
    mod_all, cc_all = _mod_tp(jnp.zeros((8, D), F32).at[0].set(c), c_ctx.reshape(1, D), sh["w_mod"],
                              lax.dynamic_slice_in_dim(b_mod, shard * mc, mc, axis=1))
    mod = jnp.zeros((16, 3 * D), F32).at[0].set(mod_all[me]).at[1].set(mod_all[8])
    cc = jnp.zeros((16, D), F32).at[0].set(c).at[1].set(c_ctx)
    z, u_all, w_in_g, nar = _inproj_ag(x, ctx, mod, sh["w_in"], b_in, sh["narrow"])
    nar = jnp.transpose(nar, (1, 0, 2)).reshape(-1, D)
    lbl, cw, br, bi, lam = nar[0:4].reshape(2, 2, D), nar[4:8], nar[8:10], nar[10:12], nar[12:14]
    o0, st0, (w_mod_g, p_a, p_b, w_out) = _gla_fwd(z, lbl, 0, nlb, ncb,
                                                   gather=[sh[k] for k in ("w_mod", "p_a", "p_b", "w_out")])
    p_a, p_b, w_out = p_a.reshape(D, D), p_b.reshape(D, D), w_out.reshape(D, D)
    o1, st1, _ = _gla_fwd(z, lbl, 1, nlb, ncb)
    h0, hp0, a0, hpc0 = _rglru_fwd(z, cw, cb, wr, br, wi, bi, lam, 0, t_lat, t_ctx)
    h1, hp1, a1, hpc1 = _rglru_fwd(z, cw, cb, wr, br, wi, bi, lam, 1, t_lat, t_ctx)
    (do, dh, dz4a, dz4b, dz6a, dz_sh3, gxres, y, dout, oa, dpa, obv, dpb, acc) = _merge(
        o0, o1, h0, h1, z, x, tgt, mod, norm_g, ln_g, ln_b, p_a, p_b, w_out, t_lat)
    gp_a = _wgrad(oa, dpa, "wgrad_pa")
    gp_b = _wgrad(obv, dpb, "wgrad_pb")
    gw_out = _wgrad(y, dout, "wgrad_wout")
    dxc_lat, dxc_ctx, dwr0, dwi0, sb0 = _rglru_bwd(z, cw, cb, wr, br, wi, bi, lam, dh, hp0, a0, hpc0, 0, t_lat, t_ctx)
    dz5_lat, dz5_ctx, dwr1, dwi1, sb1 = _rglru_bwd(z, cw, cb, wr, br, wi, bi, lam, dh, hp1, a1, hpc1, 1, t_lat, t_ctx,
                                                   prev=(dxc_lat, dxc_ctx))
    dq0, dv0, dz1, sa0 = _gla_bwd(z, lbl, do, st0, 0, nlb, ncb)
    dz_sh0, dz_sh1, sa1 = _gla_bwd(z, lbl, do, st1, 1, nlb, ncb, prev=(dq0, dv0, dz1, dz4a))
    dz = (dz_sh0, dz_sh1, _pack_shard2(dz4b, dz5_lat, dz5_ctx, dz6a), dz_sh3)
    g_p = [g.reshape(NSH, D // NSH, D) for g in (gp_a, gp_b, gw_out)]
    gw_in, got = _wgrad_in(u_all, dz, g_p)
    got[0] = _rs_sibling_last(gw_in, got[0])
    grads = [gw_in] + g_p
    sums = list(_rs_add1(grads[:1], got[:1], "rs_add1_w")) + list(_rs_add1(grads[1:], got[1:], "rs_add1_p"))
    gx, dm, got = _du(dz, w_in_g, x, ctx, mod, gxres, sums)
    fulls = list(_rs_add2(sums[:1], got[:1], "rs_add2_w")) + list(_rs_add2(sums[1:], got[1:], "rs_add2_p"))
    big = dict(zip(_RS, _ag_sibling(fulls)))
    dmod = jnp.zeros((16, 3 * D), F32).at[0:2, 0:2 * D].set(dm[0:2]).at[0, 2 * D:].set(acc[0])
    dcc = _mod_bwd(cc, dmod, w_mod_g)
    small = dict(
        c_ctx=dcc[1:2], b_mod=(dmod[0] + dmod[1]).reshape(3, D),
        b_in=jnp.stack([sa1[2], sa0[0], sa1[0], sa1[3], acc[3], sb1[4], acc[4], acc[5], acc[6]]),
        lb_logits=jnp.stack([sa0[1], sa1[1], -sa0[1], -sa1[1]]),
        norm_a_g=acc[7:8], conv_w=sb1[8:12], conv_b=sb1[3:4],
        w_r=jnp.stack([dwr0, dwr1]).reshape(-1, D), w_i=jnp.stack([dwi0, dwi1]).reshape(-1, D),
        b_r=jnp.stack([sb0[0], sb1[0]]), b_i=jnp.stack([sb0[1], sb1[1]]), lam=jnp.stack([sb0[2], sb1[2]]),
        ln_g=acc[1:2], ln_b=acc[2:3])
    return acc[9, 0], gx, big, small, dmod, cc_all


_RS =("w_in", "p_a", "p_b", "w_out")
_SMALL =("c_ctx", "b_mod", "b_in", "lb_logits", "norm_a_g", "conv_w", "conv_b", "w_r", "w_i", "b_r", "b_i", "lam",
          "ln_g", "ln_b")
_BIG = ("w_mod", "w_in", "p_a", "p_b", "w_out")
_COL_SHARDED = ("lb_logits", "conv_w", "b_r", "b_i", "lam")
_WEIGHTS = ("c_ctx", "w_mod", "b_mod", "w_in", "b_in", "lb_logits", "norm_a_g", "conv_w", "conv_b", "w_r", "b_r", "w_i",
            "b_i", "lam", "p_a", "p_b", "w_out", "ln_g", "ln_b")


def kernel(x, c, ctx, c_ctx, w_mod, b_mod, w_in, b_in, lb_logits, norm_a_g, conv_w, conv_b, w_r, b_r, w_i, b_i, lam, p_a, p_b, w_out, ln_g, ln_b, loss_target, m_c_ctx, m_w_mod, m_b_mod, m_w_in, m_b_in, m_lb_logits, m_norm_a_g, m_conv_w, m_conv_b, m_w_r, m_b_r, m_w_i, m_b_i, m_lam, m_p_a, m_p_b, m_w_out, m_ln_g, m_ln_b, v_c_ctx, v_w_mod, v_b_mod, v_w_in, v_b_in, v_lb_logits, v_norm_a_g, v_conv_w, v_conv_b, v_w_r, v_b_r, v_w_i, v_b_i, v_lam, v_p_a, v_p_b, v_w_out, v_ln_g, v_ln_b):
    w = dict(c_ctx=c_ctx, w_mod=w_mod, b_mod=b_mod, w_in=w_in, b_in=b_in, lb_logits=lb_logits, norm_a_g=norm_a_g,
             conv_w=conv_w, conv_b=conv_b, w_r=w_r, b_r=b_r, w_i=w_i, b_i=b_i, lam=lam, p_a=p_a, p_b=p_b, w_out=w_out,
             ln_g=ln_g, ln_b=ln_b)
    m = dict(c_ctx=m_c_ctx, w_mod=m_w_mod, b_mod=m_b_mod, w_in=m_w_in, b_in=m_b_in, lb_logits=m_lb_logits,
             norm_a_g=m_norm_a_g, conv_w=m_conv_w, conv_b=m_conv_b, w_r=m_w_r, b_r=m_b_r, w_i=m_w_i, b_i=m_b_i,
             lam=m_lam, p_a=m_p_a, p_b=m_p_b, w_out=m_w_out, ln_g=m_ln_g, ln_b=m_ln_b)
    v = dict(c_ctx=v_c_ctx, w_mod=v_w_mod, b_mod=v_b_mod, w_in=v_w_in, b_in=v_b_in, lb_logits=v_lb_logits,
             norm_a_g=v_norm_a_g, conv_w=v_conv_w, conv_b=v_conv_b, w_r=v_w_r, b_r=v_b_r, w_i=v_w_i, b_i=v_b_i,
             lam=v_lam, p_a=v_p_a, p_b=v_p_b, w_out=v_w_out, ln_g=v_ln_g, ln_b=v_ln_b)
    shard = 2 * lax.axis_index("x") + lax.axis_index("y")
    cs = D // NSH

    sh = {k: w[k][0].astype(BF) for k in _BIG}
    sh["narrow"] = _pad_rows(jnp.concatenate([lb_logits.reshape(4, cs), conv_w[0], b_r[0], b_i[0], lam[0]], axis=0), 8)
    me = 2 * shard + lax.axis_index("c")
    loss, gx, big, small, dmod, cc_all = _local_step(
        x[0], c[0], ctx[0], c_ctx, loss_target[0], me, shard, sh, b_mod, b_in, norm_a_g, conv_b, w_r[0], w_i[0],
        ln_g, ln_b)

    dmod_rows = jnp.zeros((16, 3 * D), F32).at[me].set(dmod[0]).at[8].set(dmod[1]).reshape(48, D)
    loss_rows = jnp.zeros((8, D), F32).at[0, 0].set(loss)
    sizes = [small[k].shape[0] for k in _SMALL]
    red = _allreduce_small(_pad_rows(jnp.concatenate([_pad_rows(small[k], 8) for k in _SMALL] + [dmod_rows, loss_rows],
                                                     axis=0), 64))
    grads = {}
    off = 0
    for k, n in zip(_SMALL, sizes):
        g = red[off:off + n]
        off += n + (-n) % 8
        if k == "norm_a_g":
            g = g[:, :DH]
        if k in _COL_SHARDED:
            g = lax.dynamic_slice_in_dim(g, shard * cs, cs, axis=1)
        grads[k] = g.reshape(w[k].shape)
    for k in _RS:
        grads[k] = big[k].reshape(w[k].shape)
    dmod_all = red[off:off + 48].reshape(16, 3 * D)
    loss = red[off + 48, 0]
    mc = 3 * D // NSH
    grads["w_mod"] = _wmod_grad(cc_all, lax.dynamic_slice_in_dim(dmod_all, shard * mc, mc, axis=1)).reshape(
        w["w_mod"].shape)

    delta, new_m, new_v = {}, {}, {}
    for group in (("w_mod",), ("w_in",), ("p_a", "p_b", "w_out")):
        two = lambda t: [t[k].reshape(t[k].shape[-2:]) for k in group]
        d_, m_, v_ = _adamw(two(w), two(grads), two(m), two(v), "adamw_" + group[0])
        for j, k in enumerate(group):
            shp = w[k].shape
            delta[k], new_m[k], new_v[k] = d_[j].reshape(shp), m_[j].reshape(shp), v_[j].reshape(shp)
    d_, m_, v_ = _adamw_many(*[[t[k] for k in _SMALL] for t in (w, grads, m, v)])
    delta.update(zip(_SMALL, d_))
    new_m.update(zip(_SMALL, m_))
    new_v.update(zip(_SMALL, v_))

    return (loss, gx[None], *[grads[k] for k in _WEIGHTS], *[delta[k] for k in _WEIGHTS],
            *[new_m[k] for k in _WEIGHTS], *[new_v[k] for k in _WEIGHTS])
```

```python
import jax
import jax.numpy as jnp
from jax import lax
from jax.experimental import pallas as pl
from jax.experimental.pallas import tpu as pltpu

F32 = jnp.float32
BF = jnp.bfloat16
MESH = pl.DeviceIdType.MESH

D = 1024
H = 8
DH = 128
CH = 64
RB = 256
NCK = RB // CH
GRID_W = 64
CB = 8
RCH = 16
IN_COLS = 9 * D
NSH = 4
SHC = IN_COLS // NSH
MC = 3 * D // NSH
CUT2 = SHC - 2 * D
CUT4 = 2 * SHC - 4 * D
CUT6 = 3 * SHC - 6 * D
RG_C = 8.0
ALPHA = 2.0 ** 0.25
LN_EPS = 1e-5
RMS_EPS = 1e-6
Q_SCALE = DH ** -0.5
ADAM_LR, ADAM_B1, ADAM_B2, ADAM_EPS, ADAM_WD, ADAM_STEP = 0.001, 0.9, 0.999, 1e-08, 0.01, 10
VMEM_LIMIT = 56 * 1024 * 1024
VMEM_LIMIT_MERGE = 60 * 1024 * 1024
VMEM_LIMIT_RGLRU_BWD = 63 * 1024 * 1024


def _cp(sem=None, vmem=VMEM_LIMIT):
    return pltpu.CompilerParams(dimension_semantics=sem, vmem_limit_bytes=vmem)


def _sig(x):
    return 0.5 * jnp.tanh(0.5 * x) + 0.5


def _sig_tail(x):
    return 1.0 / (1.0 + jnp.exp(-x))


def _dot(a, b):
    return jnp.dot(a.astype(BF), b.astype(BF), preferred_element_type=F32)


def _dot_nt(a, b):
    return lax.dot_general(a.astype(BF), b.astype(BF), (((1,), (1,)), ((), ())), preferred_element_type=F32)


def _dot_tn(a, b):
    return lax.dot_general(a.astype(BF), b.astype(BF), (((0,), (0,)), ((), ())), preferred_element_type=F32)


def _colsum(v):
    return jnp.sum(v, axis=0, keepdims=True)


def _mod_bwd(cc, dmod, w_mod_g):
    def body(cc_ref, dm_ref, w_ref, dcc_ref):
        v = cc_ref[...]
        sg = _sig(v)
        ds = jnp.zeros((16, D), F32)
        for k in range(NSH):
            ds = ds + _dot_nt(dm_ref[:, k * MC:(k + 1) * MC], w_ref[k])
        dcc_ref[...] = ds * (sg * (1.0 + v * (1.0 - sg)))
    return pl.pallas_call(body, out_shape=jax.ShapeDtypeStruct((16, D), F32),
                          name="mod_bwd", compiler_params=_cp())(cc, dmod, w_mod_g)


def _wmod_grad(cc, dmod_cols):
    def body(cc_ref, dm_ref, dw_ref):
        v = cc_ref[...]
        dw_ref[...] = _dot_tn(v * _sig(v), dm_ref[...])
    return pl.pallas_call(body, out_shape=jax.ShapeDtypeStruct((D, dmod_cols.shape[1]), F32),
                          name="wmod_grad", compiler_params=_cp())(cc, dmod_cols)


def _modulate(xv, mod_ref, is_lat):
    sh = jnp.where(is_lat, mod_ref[0:1, 0:D], mod_ref[1:2, 0:D])
    sc = jnp.where(is_lat, mod_ref[0:1, D:2 * D], mod_ref[1:2, D:2 * D])
    return xv * (1.0 + sc) + sh


def _gla_mask(d, n):
    row = lax.broadcasted_iota(jnp.int32, (n, n), 0)
    col = lax.broadcasted_iota(jnp.int32, (n, n), 1)
    same = (row // CH) == (col // CH)
    return same & ((row <= col) if d else (row >= col))


def _chunk_cumsum(v, rev):
    n = v.shape[0]
    pos = lax.broadcasted_iota(jnp.int32, v.shape, 0) & (CH - 1)
    s = 1
    while s < CH:
        if rev:
            v = v + jnp.where(pos < CH - s, pltpu.roll(v, n - s, 0), 0.0)
        else:
            v = v + jnp.where(pos >= s, pltpu.roll(v, s, 0), 0.0)
        s *= 2
    return v


def _chunk_rows(c):
    return slice(c * CH, (c + 1) * CH)


def _gla_features(zq, zf, lb, d):
    sq = _sig(zq)
    q = zq * sq * Q_SCALE
    sf = _sig(zf)
    f = lb + (1.0 - lb) * sf
    k = 1.0 - f
    g = _chunk_cumsum(jnp.log(f), d)
    last = 0 if d else CH - 1
    gls = [g[c * CH + last:c * CH + last + 1, :] for c in range(NCK)]
    glb = jnp.concatenate([jnp.broadcast_to(gl, (CH, D)) for gl in gls], axis=0)
    eg, eig, eeg = jnp.exp(g), jnp.exp(-g), jnp.exp(glb - g)
    decs = [jnp.exp(gl) for gl in gls]
    return sq, sf, f, q * eg, k * eig, k * eeg, eg, eig, eeg, decs


def _lower_bound(lbl_ref, d):
    return _sig_tail(lbl_ref[0, d:d + 1, :] - lbl_ref[1, d:d + 1, :])


def _gla_rb(d, nlb, ncb):
    nrb = nlb + ncb
    if d:
        return lambda s: nrb - 1 - s
    return lambda s: jnp.where(s < ncb, nlb + s, s - ncb)


def _gla_fwd(z, lbl, d, nlb, ncb, gather=()):
    m = z.shape[0]
    nrb = nlb + ncb
    rb = _gla_rb(d, nlb, ncb)
    ng = len(gather)

    def body(*refs):
        q_ref, f_ref, v_ref, lbl_ref = refs[:4]
        ag_in = refs[4:4 + ng]
        o_ref, st_ref = refs[4 + ng:6 + ng]
        ag_out = refs[6 + ng:6 + 2 * ng]
        S = refs[6 + 2 * ng]
        ag_sems = refs[7 + 2 * ng:]
        s = pl.program_id(0)

        @pl.when(s == 0)
        def _():
            S[...] = jnp.zeros_like(S)
            if ng:
                local, chip, _ = _ag_copies(ag_in, ag_out, *ag_sems)
                for cp in chip + local:
                    cp.start()

        lb = _lower_bound(lbl_ref, d)
        mb = _gla_mask(d, CH)
        _, _, _, qd, ki, ke, _, _, _, decs = _gla_features(q_ref[...], f_ref[...], lb, d)
        qd, ki, ke, v = qd.astype(BF), ki.astype(BF), ke.astype(BF), v_ref[...].astype(BF)
        order = range(NCK - 1, -1, -1) if d else range(NCK)
        for h in range(H):
            hs = slice(h * DH, (h + 1) * DH)
            intra, upd = {}, {}
            for c in range(NCK):
                rs = _chunk_rows(c)
                a = jnp.where(mb, _dot_nt(qd[rs, hs], ki[rs, hs]), 0.0)
                intra[c] = _dot(a, v[rs, hs])
                upd[c] = _dot_tn(v[rs, hs], ke[rs, hs])
            st = S[h]
            for c in order:
                rs = _chunk_rows(c)
                st_ref[c, h] = st
                o_ref[rs, hs] = intra[c] + _dot_nt(qd[rs, hs], st)
                st = st * decs[c][:, hs] + upd[c]
            S[h] = st

        if ng:
            @pl.when(s == nrb - 1)
            def _():
                _ag_finish(*_ag_copies(ag_in, ag_out, *ag_sems))

    def zspec(cb):
        return pl.BlockSpec((RB, D), lambda s: (rb(s), cb))

    outs = pl.pallas_call(
        body, grid=(nrb,),
        in_specs=[zspec(0), zspec(1 + d), zspec(3), pl.BlockSpec((2, 2, D), lambda s: (0, 0, 0))] + [_ANY] * ng,
        out_specs=[pl.BlockSpec((RB, D), lambda s: (rb(s), 0)),
                   pl.BlockSpec((NCK, H, DH, DH), lambda s: (rb(s), 0, 0, 0))] + [_ANY] * ng,
        out_shape=[jax.ShapeDtypeStruct((m, D), F32), jax.ShapeDtypeStruct((m // CH, H, DH, DH), F32)]
        + [jax.ShapeDtypeStruct((NSH,) + g.shape, g.dtype) for g in gather],
        scratch_shapes=[pltpu.VMEM((H, DH, DH), F32)] + (_ag_sems(ng) if ng else []),
        name=f"gla_fwd{d}", compiler_params=_cp(("arbitrary",)))(z, z, z, lbl, *gather)
    return outs[0], outs[1], list(outs[2:])


def _gla_bwd(z, lbl, do_lat, states, d, nlb, ncb, prev=None):
    m = z.shape[0]
    nrb = nlb + ncb
    fwd_rb = _gla_rb(d, nlb, ncb)
    rb = lambda s: fwd_rb(nrb - 1 - s)
    last = prev is not None

    def body(*refs):
        if last:
            (q_ref, f_ref, v_ref, lbl_ref, do_ref, st_ref, pq_ref, pv_ref, dz1_ref, dz4a_ref, sh0_ref, sh1_ref, sum_ref,
             dS) = refs
        else:
            q_ref, f_ref, v_ref, lbl_ref, do_ref, st_ref, o0_ref, o1_ref, o2_ref, sum_ref, dS = refs
        s = pl.program_id(0)
        is_lat = rb(s) < nlb

        @pl.when(s == 0)
        def _():
            dS[...] = jnp.zeros_like(dS)
            sum_ref[...] = jnp.zeros_like(sum_ref)

        lb = _lower_bound(lbl_ref, d)
        mb = _gla_mask(d, RB)
        zq = q_ref[...]
        sq, sf, f, qd, ki, ke, eg, eig, eeg, decs = _gla_features(zq, f_ref[...], lb, d)
        qdb, kib, keb, vb = qd.astype(BF), ki.astype(BF), ke.astype(BF), v_ref[...].astype(BF)
        dob = jnp.where(is_lat, do_ref[...], 0.0).astype(BF)
        order = range(NCK) if d else range(NCK - 1, -1, -1)
        dqd_h, dki_h, dke_h, dv_h, ddec_h = [], [], [], [], []
        for h in range(H):
            hs = slice(h * DH, (h + 1) * DH)
            a = jnp.where(mb, _dot_nt(qdb[:, hs], kib[:, hs]), 0.0).astype(BF)
            da = jnp.where(mb, _dot_nt(dob[:, hs], vb[:, hs]), 0.0).astype(BF)
            dqd_i = _dot(da, kib[:, hs])
            dki_h.append(_dot_tn(da, qdb[:, hs]))
            dvi = _dot_tn(a, dob[:, hs])
            dqd, inc = {}, {}
            for c in range(NCK):
                rs = _chunk_rows(c)
                dqd[c] = dqd_i[rs, :] + _dot(dob[rs, hs], st_ref[c, h])
                inc[c] = _dot_tn(dob[rs, hs], qdb[rs, hs])
            dst = dS[h]
            dke, dv, ddec = {}, {}, {}
            for c in order:
                rs = _chunk_rows(c)
                dv[c] = dvi[rs, :] + _dot_nt(keb[rs, hs], dst)
                dke[c] = _dot(vb[rs, hs], dst)
                ddec[c] = _colsum(st_ref[c, h] * dst)
                dst = inc[c] + dst * decs[c][:, hs]
            dS[h] = dst
            cat = lambda t: jnp.concatenate([t[c] for c in range(NCK)], axis=0)
            dqd_h.append(cat(dqd))
            dke_h.append(cat(dke))
            dv_h.append(cat(dv))
            ddec_h.append([ddec[c] for c in range(NCK)])
        lanes = lambda parts: jnp.concatenate(parts, axis=1)
        dqd, dki, dke, dv = lanes(dqd_h), lanes(dki_h), lanes(dke_h), lanes(dv_h)
        dq = dqd * eg
        dk = dki * eig + dke * eeg
        dke_ke = dke * ke
        dg = dqd * qd - dki * ki - dke_ke
        dgl = [_colsum(dke_ke[_chunk_rows(c), :]) + lanes([ddec_h[h][c] for h in range(H)]) * decs[c]
               for c in range(NCK)]
        dglb = jnp.concatenate([jnp.broadcast_to(t, (CH, D)) for t in dgl], axis=0)
        df = (_chunk_cumsum(dg, 1 - d) + dglb) / f - dk
        dzf = df * (1.0 - lb) * (sf * (1.0 - sf))
        sum_ref[0:1, :] += _colsum(dzf)
        sum_ref[1:2, :] += _colsum(df * (1.0 - sf))
        if last:
            dz0 = (dq + pq_ref[...]) * (Q_SCALE * (sq * (1.0 + zq * (1.0 - sq))))
            dz3 = dv + pv_ref[...]
            sum_ref[2:3, :] += _colsum(dz0)
            sum_ref[3:4, :] += _colsum(dz3)
            dz2 = dzf.astype(BF)
            sh0_ref[:, 0:D] = dz0.astype(BF)
            sh0_ref[:, D:2 * D] = dz1_ref[...]
            sh0_ref[:, 2 * D:] = dz2[:, :CUT2]
            sh1_ref[:, 0:D - CUT2] = dz2[:, CUT2:]
            sh1_ref[:, D - CUT2:2 * D - CUT2] = dz3.astype(BF)
            sh1_ref[:, 2 * D - CUT2:] = dz4a_ref[...]
        else:
            o0_ref[...] = dq
            o1_ref[...] = dv
            o2_ref[...] = dzf.astype(BF)

        @pl.when(s == nrb - 1)
        def _():
            sum_ref[1:2, :] = sum_ref[1:2, :] * (lb * (1.0 - lb))

    def zspec(cb):
        return pl.BlockSpec((RB, D), lambda s: (rb(s), cb))

    rowspec = pl.BlockSpec((RB, D), lambda s: (rb(s), 0))
    in_specs = [zspec(0), zspec(1 + d), zspec(3), pl.BlockSpec((2, 2, D), lambda s: (0, 0, 0)),
                pl.BlockSpec((RB, D), lambda s: (jnp.minimum(rb(s), nlb - 1), 0)),
                pl.BlockSpec((NCK, H, DH, DH), lambda s: (rb(s), 0, 0, 0))]
    args = [z, z, z, lbl, do_lat, states]
    sumspec = pl.BlockSpec((8, D), lambda s: (0, 0))
    if last:
        in_specs += [rowspec, rowspec, rowspec, pl.BlockSpec((RB, CUT4), lambda s: (rb(s), 0))]
        args += list(prev)
        shspec = pl.BlockSpec((RB, SHC), lambda s: (rb(s), 0))
        out_specs = (shspec, shspec, sumspec)
        out_shape = (jax.ShapeDtypeStruct((m, SHC), BF), jax.ShapeDtypeStruct((m, SHC), BF))
    else:
        out_specs = (rowspec, rowspec, rowspec, sumspec)
        out_shape = (jax.ShapeDtypeStruct((m, D), F32), jax.ShapeDtypeStruct((m, D), F32),
                     jax.ShapeDtypeStruct((m, D), BF))
    return pl.pallas_call(
        body, grid=(nrb,), in_specs=in_specs, out_specs=out_specs,
        out_shape=out_shape + (jax.ShapeDtypeStruct((8, D), F32),),
        scratch_shapes=[pltpu.VMEM((H, DH, DH), F32)],
        name=f"gla_bwd{d}", compiler_params=_cp(("arbitrary",)))(*args)


def _shift_rows(v, k, r):
    row = lax.broadcasted_iota(jnp.int32, v.shape, 0)
    rolled = pltpu.roll(v, k % r, 0)
    return jnp.where((row >= k) & (row < r + k), rolled, 0.0)


def _conv_fwd(xb, cw_ref, cb_ref, r):
    return (cb_ref[...] + _shift_rows(xb, 1, r) * cw_ref[0:1, :] + xb * cw_ref[1:2, :]
            + _shift_rows(xb, -1, r) * cw_ref[2:3, :] + _shift_rows(xb, -2, r) * cw_ref[3:4, :])


def _window(ref, lo, n, rows):
    parts = []
    if lo < 0:
        parts.append(jnp.zeros((-lo,) + tuple(ref.shape[1:]), F32))
    parts.append(ref[max(lo, 0):min(lo + n, rows)])
    if lo + n > rows:
        parts.append(jnp.zeros((lo + n - rows,) + tuple(ref.shape[1:]), F32))
    return parts[0] if len(parts) == 1 else jnp.concatenate(parts, axis=0)


def _conv_cols(x_ref, cw_ref, cb_ref, r0, n, rows):
    w = _window(x_ref, r0 - 1, n + 3, rows)
    return (cb_ref[...] + w[0:n] * cw_ref[0:1, :] + w[1:n + 1] * cw_ref[1:2, :] + w[2:n + 2] * cw_ref[2:3, :]
            + w[3:n + 3] * cw_ref[3:4, :])


def _softplus_neg(lam):
    y = jnp.exp(-jnp.abs(lam))
    u = 1.0 + y
    tiny = u == 1.0
    l1p = jnp.where(tiny, y, jnp.log(u) * (y / jnp.where(tiny, 1.0, u - 1.0)))
    return jnp.maximum(-lam, 0.0) + l1p


def _gates(xc, wr_ref, br, wi_ref, bi, sp):
    xcb = xc.astype(BF)
    rs, is_ = [], []
    for g in range(H):
        gs = slice(g * DH, (g + 1) * DH)
        rs.append(jnp.dot(xcb[:, gs], wr_ref[g].astype(BF), preferred_element_type=F32))
        is_.append(jnp.dot(xcb[:, gs], wi_ref[g].astype(BF), preferred_element_type=F32))
    r = _sig_tail(jnp.concatenate(rs, axis=1) + br)
    i = _sig(jnp.concatenate(is_, axis=1) + bi)
    log_a = (-RG_C * r) * sp
    a = jnp.exp(log_a)
    t = jnp.tanh(log_a)
    om = (-2.0 * t) / (1.0 - t)
    return r, i, a, om


def _scan_rows(d, nrows, a_s, b_s, h_s, h0):
    nsl = nrows // 8

    def slab(j, h):
        jj = (nsl - 1 - j) if d else j
        r0 = pl.multiple_of(jj * 8, 8)
        for t in (range(7, -1, -1) if d else range(8)):
            h = a_s[pl.ds(r0 + t, 1), :] * h + b_s[pl.ds(r0 + t, 1), :]
            h_s[pl.ds(r0 + t, 1), :] = h
        return h

    return lax.fori_loop(0, nsl, slab, h0)


def _col_of(d, ncols):
    if d:
        return lambda s: ncols - jnp.maximum(s, 1)
    return lambda s: jnp.maximum(s, 1) - 1


def _rglru_fwd(z, cw, cb, wr, br, wi, bi, lam, d, t_lat, t_ctx):
    m = z.shape[0]
    rows = t_lat // GRID_W
    z3 = z.reshape(m // GRID_W, GRID_W, IN_COLS)
    nblk = GRID_W // CB
    blk = _col_of(d, nblk)
    cblk = t_lat // t_ctx
    rc = min(RCH, rows)

    def body(zc_ref, zx_ref, cw_ref, cb_ref, wr_ref, br_ref, wi_ref, bi_ref, lam_ref,
             hx_ref, hpx_ref, ax_ref, hpc_ref, a_s, b_s, h_s, hcar, a3, b3, cin_s):
        s = pl.program_id(0)
        sp = _softplus_neg(lam_ref[d:d + 1, :])
        br_ = br_ref[d:d + 1, :]
        bi_ = bi_ref[d:d + 1, :]

        @pl.when(s == 0)
        def _():
            xc = _conv_fwd(zc_ref[...], cw_ref, cb_ref, t_ctx)
            _, i, a, om = _gates(xc, wr_ref, br_, wi_ref, bi_, sp)
            a_s[...] = a
            b_s[...] = jnp.sqrt(om) * (i * xc)
            h0 = jnp.zeros((1, D), F32)
            hcar[0:1, :] = _scan_rows(d, t_ctx, a_s, b_s, h_s, h0)
            hs = h_s[...]
            row = lax.broadcasted_iota(jnp.int32, (t_ctx, D), 0)
            if d:
                hpc_ref[...] = jnp.where(row == t_ctx - 1, h0, pltpu.roll(hs, t_ctx - 1, 0))
            else:
                hpc_ref[...] = jnp.where(row == 0, h0, pltpu.roll(hs, 1, 0))

        @pl.when(s > 0)
        def _():
            for r0 in range(0, rows, rc):
                xc = _conv_cols(zx_ref, cw_ref, cb_ref, r0, rc, rows).reshape(rc * CB, D)
                _, i, a, om = _gates(xc, wr_ref, br_, wi_ref, bi_, sp)
                a3[r0:r0 + rc] = a.reshape(rc, CB, D)
                ax_ref[r0:r0 + rc] = a.reshape(rc, CB, D)
                b3[r0:r0 + rc] = (jnp.sqrt(om) * (i * xc)).reshape(rc, CB, D)

            def local(t, carry):
                hl, p = carry
                r = (rows - 1 - t) if d else t
                a = a3[r]
                hl = a * hl + b3[r]
                p = a * p
                b3[r] = hl
                a3[r] = p
                return hl, p

            hl, p = lax.fori_loop(0, rows, local, (jnp.zeros((CB, D), F32), jnp.ones((CB, D), F32)))
            cin = hcar[0:1, :]
            for j in (range(CB - 1, -1, -1) if d else range(CB)):
                cin_s[j:j + 1, :] = cin
                cin = hl[j:j + 1, :] + p[j:j + 1, :] * cin
            hcar[0:1, :] = cin
            c_in = cin_s[...]

            def fix(t, prev):
                r = (rows - 1 - t) if d else t
                h = b3[r] + a3[r] * c_in
                hx_ref[r] = h
                hpx_ref[r] = prev
                return h

            lax.fori_loop(0, rows, fix, c_in)

    full = lambda shp: pl.BlockSpec(shp, lambda s: (0,) * len(shp))
    colspec = pl.BlockSpec((rows, CB, D), lambda s: (0, blk(s), 0))
    outs = pl.pallas_call(
        body, grid=(nblk + 1,),
        in_specs=[pl.BlockSpec((t_ctx, D), lambda s: (cblk, 5)),
                  pl.BlockSpec((rows, CB, D), lambda s: (0, blk(s), 5)),
                  full((4, D)), full((1, D)),
                  pl.BlockSpec((None, H, DH, DH), lambda s: (d, 0, 0, 0)), full((2, D)),
                  pl.BlockSpec((None, H, DH, DH), lambda s: (d, 0, 0, 0)), full((2, D)), full((2, D))],
        out_specs=(colspec, colspec, colspec, full((t_ctx, D))),
        out_shape=(jax.ShapeDtypeStruct((rows, GRID_W, D), F32),) * 3 + (jax.ShapeDtypeStruct((t_ctx, D), F32),),
        scratch_shapes=[pltpu.VMEM((t_ctx, D), F32), pltpu.VMEM((t_ctx, D), F32), pltpu.VMEM((t_ctx, D), F32),
                        pltpu.VMEM((8, D), F32), pltpu.VMEM((rows, CB, D), F32), pltpu.VMEM((rows, CB, D), F32),
                        pltpu.VMEM((CB, D), F32)],
        name=f"rglru_fwd{d}", compiler_params=_cp(("arbitrary",)))(z, z3, cw, cb, wr, br, wi, bi, lam)
    return outs[0].reshape(t_lat, D), outs[1].reshape(t_lat, D), outs[2].reshape(t_lat, D), outs[3]


def _rglru_bwd(z, cw, cb, wr, br, wi, bi, lam, dh_lat, hp_lat, a_lat, hp_ctx, d, t_lat, t_ctx, prev=None):
    m = z.shape[0]
    rows = t_lat // GRID_W
    z3 = z.reshape(m // GRID_W, GRID_W, IN_COLS)
    nblk = GRID_W // CB
    fblk = _col_of(d, nblk)
    blk = lambda s: fblk(nblk - jnp.minimum(s, nblk - 1))
    cblk = t_lat // t_ctx
    rc = min(RCH, rows)
    last = prev is not None
    view3 = lambda v: v.reshape(rows, GRID_W, D)

    def body(*refs):
        (zc_ref, zx_ref, cw_ref, cb_ref, wr_ref, br_ref, wi_ref, bi_ref, lam_ref, dh_ref, hpx_ref, ax_ref,
         hpc_ref) = refs[:13]
        k = 13
        if last:
            pdx_ref, pdc_ref = refs[13:15]
            k = 15
        ox_ref, oc_ref, dwr_ref, dwi_ref, sum_ref, a_s, b_s, gcar, a3, b3, cin_s = refs[k:]
        s = pl.program_id(0)
        lam_d = lam_ref[d:d + 1, :]
        sp = _softplus_neg(lam_d)
        br_ = br_ref[d:d + 1, :]
        bi_ = bi_ref[d:d + 1, :]
        flat = lambda v: v.reshape(-1, D)

        @pl.when(s == 0)
        def _():
            gcar[...] = jnp.zeros_like(gcar)
            dwr_ref[...] = jnp.zeros_like(dwr_ref)
            dwi_ref[...] = jnp.zeros_like(dwi_ref)
            sum_ref[...] = jnp.zeros_like(sum_ref)


        def conv_sums(dxc, dxb, xm1, x0, xp1, xp2):
            sum_ref[3:4, :] += _colsum(flat(dxc))
            sum_ref[4:5, :] += _colsum(flat(dxb))
            sum_ref[8:9, :] += _colsum(flat(dxc * xm1))
            sum_ref[9:10, :] += _colsum(flat(dxc * x0))
            sum_ref[10:11, :] += _colsum(flat(dxc * xp1))
            sum_ref[11:12, :] += _colsum(flat(dxc * xp2))

        def gate_grads(g, hp, xc, rr, i, a, om):
            mult = jnp.sqrt(om)
            da = g * hp
            ixc = i * xc
            dmult = g * ixc
            dixc = g * mult
            di = dixc * xc
            dxc = dixc * i
            dlog_a = da * a - dmult * ((1.0 - om) / mult)
            dr = dlog_a * (-RG_C * sp)
            sum_ref[2:3, :] += _colsum(dlog_a * rr)
            drp = dr * rr * (1.0 - rr)
            dip = di * i * (1.0 - i)
            sum_ref[0:1, :] += _colsum(drp)
            sum_ref[1:2, :] += _colsum(dip)
            xcb = xc.astype(BF)
            drb = drp.astype(BF)
            dib = dip.astype(BF)
            parts = []
            for gi in range(H):
                gs = slice(gi * DH, (gi + 1) * DH)
                parts.append(_dot_nt(drb[:, gs], wr_ref[gi]) + _dot_nt(dib[:, gs], wi_ref[gi]))
                dwr_ref[gi] += _dot_tn(xcb[:, gs], drb[:, gs])
                dwi_ref[gi] += _dot_tn(xcb[:, gs], dib[:, gs])
            return dxc + jnp.concatenate(parts, axis=1)

        @pl.when(s < nblk)
        def _():
            def local(t, carry):
                c, q = carry
                r = t if d else (rows - 1 - t)
                a = ax_ref[r]
                c = a * (c + dh_ref[r])
                q = a * q
                b3[r] = c
                a3[r] = q
                return c, q

            c, q = lax.fori_loop(0, rows, local, (jnp.zeros((CB, D), F32), jnp.ones((CB, D), F32)))
            cin = gcar[0:1, :]
            for j in (range(CB) if d else range(CB - 1, -1, -1)):
                cin_s[j:j + 1, :] = cin
                cin = c[j:j + 1, :] + q[j:j + 1, :] * cin
            gcar[0:1, :] = cin
            c_in = cin_s[...]
            for r0 in (range(rows - rc, -1, -rc) if d else range(0, rows, rc)):
                if d:
                    lo = max(r0 - 1, 0)
                    cn = b3[lo:r0 + rc - 1] + a3[lo:r0 + rc - 1] * c_in
                    if r0 == 0:
                        cn = jnp.concatenate([c_in[None], cn], axis=0)
                else:
                    hi = min(r0 + rc + 1, rows)
                    cn = b3[r0 + 1:hi] + a3[r0 + 1:hi] * c_in
                    if hi == rows:
                        cn = jnp.concatenate([cn, c_in[None]], axis=0)
                g = flat(dh_ref[r0:r0 + rc] + cn)
                xc = flat(_conv_cols(zx_ref, cw_ref, cb_ref, r0, rc, rows))
                rr, i, a, om = _gates(xc, wr_ref, br_, wi_ref, bi_, sp)
                b3[r0:r0 + rc] = gate_grads(g, flat(hpx_ref[r0:r0 + rc]), xc, rr, i, a, om).reshape(rc, CB, D)
            if last:
                for r0 in range(0, rows, rc):
                    b3[r0:r0 + rc] = b3[r0:r0 + rc] + pdx_ref[r0:r0 + rc]
                for r0 in range(0, rows, rc):
                    w = _window(b3, r0 - 2, rc + 3, rows)
                    xw = _window(zx_ref, r0 - 1, rc + 3, rows)
                    dxc = w[2:rc + 2]
                    dxb = (w[3:rc + 3] * cw_ref[0:1, :] + dxc * cw_ref[1:2, :] + w[1:rc + 1] * cw_ref[2:3, :]
                           + w[0:rc] * cw_ref[3:4, :])
                    conv_sums(dxc, dxb, xw[0:rc], xw[1:rc + 1], xw[2:rc + 2], xw[3:rc + 3])
                    ox_ref[r0:r0 + rc] = dxb
            else:
                for r0 in range(0, rows, rc):
                    ox_ref[r0:r0 + rc] = b3[r0:r0 + rc]

        @pl.when(s == nblk)
        def _():
            r = t_ctx
            xb = zc_ref[...]
            xc = _conv_fwd(xb, cw_ref, cb_ref, r)
            rr, i, a, om = _gates(xc, wr_ref, br_, wi_ref, bi_, sp)
            a_s[...] = a
            b_s[...] = jnp.zeros((r, D), F32)
            c0 = gcar[0:1, :]
            _scan_rows(1 - d, r, a_s, b_s, b_s, c0)
            cs = b_s[...]
            row = lax.broadcasted_iota(jnp.int32, (r, D), 0)
            if d:
                g = jnp.where(row == 0, c0, pltpu.roll(cs, 1, 0))
            else:
                g = jnp.where(row == r - 1, c0, pltpu.roll(cs, r - 1, 0))
            dxc = gate_grads(g, hpc_ref[...], xc, rr, i, a, om)
            if last:
                dxc = dxc + pdc_ref[...]
                dxb = (_shift_rows(dxc, -1, r) * cw_ref[0:1, :] + dxc * cw_ref[1:2, :]
                       + _shift_rows(dxc, 1, r) * cw_ref[2:3, :] + _shift_rows(dxc, 2, r) * cw_ref[3:4, :])
                conv_sums(dxc, dxb, _shift_rows(xb, 1, r), xb, _shift_rows(xb, -1, r), _shift_rows(xb, -2, r))
                oc_ref[...] = dxb
            else:
                oc_ref[...] = dxc
            sum_ref[2:3, :] = sum_ref[2:3, :] * (RG_C * _sig_tail(-lam_d))

    full = lambda shp: pl.BlockSpec(shp, lambda s: (0,) * len(shp))
    once = lambda shp: pl.BlockSpec(shp, lambda s: (0,) * len(shp), pipeline_mode=pl.Buffered(1))
    colspec = pl.BlockSpec((rows, CB, D), lambda s: (0, blk(s), 0))
    colonce = pl.BlockSpec((rows, CB, D), lambda s: (0, blk(s), 0), pipeline_mode=pl.Buffered(1))
    in_specs = [pl.BlockSpec((t_ctx, D), lambda s: (cblk, 5), pipeline_mode=pl.Buffered(1)),
                pl.BlockSpec((rows, CB, D), lambda s: (0, blk(s), 5), pipeline_mode=pl.Buffered(1)),
                full((4, D)), full((1, D)),
                pl.BlockSpec((None, H, DH, DH), lambda s: (d, 0, 0, 0)), full((2, D)),
                pl.BlockSpec((None, H, DH, DH), lambda s: (d, 0, 0, 0)), full((2, D)), full((2, D)),
                colspec, colspec, colonce, once((t_ctx, D))]
    args = [z, z3, cw, cb, wr, br, wi, bi, lam, view3(dh_lat), view3(hp_lat), view3(a_lat), hp_ctx]
    if last:
        in_specs += [colonce, once((t_ctx, D))]
        args += [view3(prev[0]), prev[1]]
    outs = pl.pallas_call(
        body, grid=(nblk + 1,), in_specs=in_specs,
        out_specs=(colspec, full((t_ctx, D)), full((H, DH, DH)), full((H, DH, DH)), full((16, D))),
        out_shape=(jax.ShapeDtypeStruct((rows, GRID_W, D), F32), jax.ShapeDtypeStruct((t_ctx, D), F32),
                   jax.ShapeDtypeStruct((H, DH, DH), F32), jax.ShapeDtypeStruct((H, DH, DH), F32),
                   jax.ShapeDtypeStruct((16, D), F32)),
        scratch_shapes=[pltpu.VMEM((t_ctx, D), F32), pltpu.VMEM((t_ctx, D), F32), pltpu.VMEM((8, D), F32),
                        pltpu.VMEM((rows, CB, D), F32), pltpu.VMEM((rows, CB, D), F32), pltpu.VMEM((CB, D), F32)],
        name=f"rglru_bwd{d}", compiler_params=_cp(("arbitrary",), VMEM_LIMIT_RGLRU_BWD))(*args)
    return (outs[0].reshape(t_lat, D), outs[1]) + tuple(outs[2:])


def _merge(o_f, o_b, h_f, h_b, z, x, tgt, mod, norm_g, ln_g, ln_b, p_a, p_b, w_out, t_lat):
    tm = 256
    nt = t_lat // tm

    def body(of_ref, ob_ref, hf_ref, hb_ref, z4_ref, z6_ref, z7_ref, z8_ref, x_ref, t_ref, mod_ref, ng_ref,
             lg_ref, lb_ref, pa_ref, pb_ref, wo_ref,
             do_ref, dh_ref, dz4a_ref, dz4b_ref, dz6a_ref, sh3_ref, gx_ref,
             y_ref, dout_ref, oa_ref, dpa_ref, obv_ref, dpb_ref, acc_ref):
        i = pl.program_id(0)
        lat = i < nt
        latf = lat.astype(F32)

        @pl.when(i == 0)
        def _():
            acc_ref[...] = jnp.zeros_like(acc_ref)

        def per_head(v):
            return jnp.concatenate(
                [jnp.broadcast_to(jnp.mean(v[:, h * DH:(h + 1) * DH], axis=-1, keepdims=True), (tm, DH))
                 for h in range(H)], axis=1)

        gt = mod_ref[0:1, 2 * D:3 * D]
        gfull = jnp.concatenate([ng_ref[...]] * H, axis=1)
        o = of_ref[...] + ob_ref[...]
        rinv = lax.rsqrt(per_head(o * o) + RMS_EPS)
        n = o * rinv
        na = n * gfull
        z4 = z4_ref[...]
        s4 = _sig(z4)
        silu4 = z4 * s4
        oa = na * silu4
        z6 = z6_ref[...]
        s6 = _sig(z6)
        silu6 = z6 * s6
        hsum = hf_ref[...] + hb_ref[...]
        obv = hsum * silu6
        pa = _dot(oa, pa_ref[...])
        pb = _dot(obv, pb_ref[...])
        s7 = _sig(z7_ref[...])
        s8 = _sig(z8_ref[...])
        y = s7 * pa + s8 * pb
        out = _dot(y, wo_ref[...])
        pre = ALPHA * x_ref[...] + gt * out
        mu = jnp.mean(pre, axis=-1, keepdims=True)
        xc = pre - mu
        rstd = lax.rsqrt(jnp.mean(xc * xc, axis=-1, keepdims=True) + LN_EPS)
        xhat = xc * rstd
        lg = lg_ref[...]
        diff = xhat * lg + lb_ref[...] - t_ref[...]
        acc_ref[8:9, :] += _colsum(diff * diff) * (0.5 / D * latf)
        dxn = diff * (1.0 / D)
        acc_ref[1:2, :] += _colsum(dxn * xhat) * latf
        acc_ref[2:3, :] += _colsum(dxn) * latf
        dxhat = dxn * lg
        dpre = rstd * (dxhat - jnp.mean(dxhat, axis=-1, keepdims=True)
                       - xhat * jnp.mean(dxhat * xhat, axis=-1, keepdims=True))
        gx_ref[...] = ALPHA * dpre
        acc_ref[0:1, :] += _colsum(dpre * out) * latf
        dout = dpre * gt
        dy = _dot_nt(dout, wo_ref[...])
        dpa = dy * s7
        dpb = dy * s8
        dz7 = dy * pa * (s7 * (1.0 - s7))
        dz8 = dy * pb * (s8 * (1.0 - s8))
        doa = _dot_nt(dpa, pa_ref[...])
        dob = _dot_nt(dpb, pb_ref[...])
        dh_ref[...] = dob * silu6
        dz6 = dob * hsum * (s6 * (1.0 + z6 * (1.0 - s6)))
        dna = doa * silu4
        dz4 = doa * na * (s4 * (1.0 + z4 * (1.0 - s4)))
        dng = _colsum(dna * n)
        acc_ref[7:8, 0:DH] += sum(dng[:, h * DH:(h + 1) * DH] for h in range(H)) * latf
        dn = dna * gfull
        do_ref[...] = rinv * (dn - n * per_head(dn * n))
        acc_ref[3:4, :] += _colsum(dz4) * latf
        acc_ref[4:5, :] += _colsum(dz6) * latf
        acc_ref[5:6, :] += _colsum(dz7) * latf
        acc_ref[6:7, :] += _colsum(dz8) * latf
        dz4b, dz6b = (dz4 * latf).astype(BF), (dz6 * latf).astype(BF)
        dz4a_ref[...] = dz4b[:, :CUT4]
        dz4b_ref[...] = dz4b[:, CUT4:]
        dz6a_ref[...] = dz6b[:, :CUT6]
        sh3_ref[:, 0:D - CUT6] = dz6b[:, CUT6:]
        sh3_ref[:, D - CUT6:2 * D - CUT6] = (dz7 * latf).astype(BF)
        sh3_ref[:, 2 * D - CUT6:] = (dz8 * latf).astype(BF)
        y_ref[...] = y.astype(BF)
        dout_ref[...] = dout.astype(BF)
        oa_ref[...] = oa.astype(BF)
        dpa_ref[...] = dpa.astype(BF)
        obv_ref[...] = obv.astype(BF)
        dpb_ref[...] = dpb.astype(BF)

        @pl.when(i == nt - 1)
        def _():
            acc_ref[9:10, :] = jnp.broadcast_to(jnp.sum(acc_ref[8:9, :], axis=-1, keepdims=True), (1, D))

    m = z.shape[0]
    lrow = lambda i: jnp.minimum(i, nt - 1)
    row = pl.BlockSpec((tm, D), lambda i: (lrow(i), 0))
    allrow = lambda cols: pl.BlockSpec((tm, cols), lambda i: (i, 0))
    zs = lambda cb: pl.BlockSpec((tm, D), lambda i: (lrow(i), cb))
    full = lambda shp: pl.BlockSpec(shp, lambda i: (0,) * len(shp))
    wfull = pl.BlockSpec((D, D), lambda i: (0, 0), pipeline_mode=pl.Buffered(1))
    f32o = jax.ShapeDtypeStruct((t_lat, D), F32)
    bfo = jax.ShapeDtypeStruct((t_lat, D), BF)
    bfall = lambda cols: jax.ShapeDtypeStruct((m, cols), BF)
    return pl.pallas_call(
        body, grid=(m // tm,),
        in_specs=[row, row, row, row, zs(4), zs(6), zs(7), zs(8), row, row, full((16, 3 * D)), full((1, DH)),
                  full((1, D)), full((1, D)), wfull, wfull, wfull],
        out_specs=(row, row) + tuple(allrow(c) for c in (CUT4, D - CUT4, CUT6, SHC)) + (row,) * 7 + (full((16, D)),),
        out_shape=(f32o, f32o, bfall(CUT4), bfall(D - CUT4), bfall(CUT6), bfall(SHC), f32o, bfo, bfo, bfo, bfo, bfo, bfo,
                   jax.ShapeDtypeStruct((16, D), F32)),
        name="merge", compiler_params=_cp(("arbitrary",), VMEM_LIMIT_MERGE))(
            o_f, o_b, h_f, h_b, z, z, z, z, x, tgt, mod, norm_g, ln_g, ln_b, p_a, p_b, w_out)


def _wgrad(a, b, name):
    tm = 1024

    def body(a_ref, b_ref, o_ref):
        @pl.when(pl.program_id(0) == 0)
        def _():
            o_ref[...] = jnp.zeros_like(o_ref)
        o_ref[...] += _dot_tn(a_ref[...], b_ref[...])

    row = pl.BlockSpec((tm, D), lambda i: (i, 0))
    return pl.pallas_call(body, grid=(a.shape[0] // tm,), in_specs=[row, row],
                          out_specs=pl.BlockSpec((D, D), lambda i: (0, 0)),
                          out_shape=jax.ShapeDtypeStruct((D, D), F32), name=name,
                          compiler_params=_cp(("arbitrary",)))(a, b)


def _pack_shard2(dz4b, dz5_lat, dz5_ctx, dz6a):
    m, t_lat, t_ctx = dz4b.shape[0], dz5_lat.shape[0], dz5_ctx.shape[0]
    tm = t_ctx
    nlt = t_lat // tm
    w4 = D - CUT4

    def body(a_ref, bl_ref, bc_ref, c_ref, o_ref):
        i = pl.program_id(0)
        o_ref[:, 0:w4] = a_ref[...]
        o_ref[:, w4:w4 + D] = jnp.where(i < nlt, bl_ref[...], bc_ref[...]).astype(BF)
        o_ref[:, w4 + D:] = c_ref[...]

    return pl.pallas_call(
        body, grid=(m // tm,),
        in_specs=[pl.BlockSpec((tm, w4), lambda i: (i, 0)),
                  pl.BlockSpec((tm, D), lambda i: (jnp.minimum(i, nlt - 1), 0)),
                  pl.BlockSpec((tm, D), lambda i: (0, 0)),
                  pl.BlockSpec((tm, CUT6), lambda i: (i, 0))],
        out_specs=pl.BlockSpec((tm, SHC), lambda i: (i, 0)),
        out_shape=jax.ShapeDtypeStruct((m, SHC), BF), name="pack_shard2",
        compiler_params=_cp(("arbitrary",)))(dz4b, dz5_lat, dz5_ctx, dz6a)


def _wgrad_in(u_all, dz_shards, others):
    m = u_all.shape[1]
    assert m % (6 * 128) == 0
    tm = m // 6
    nt = m // tm
    no = len(others)
    spec_u, spec_dz = pl.BlockSpec((D, tm), lambda i: (0, i)), pl.BlockSpec((tm, SHC), lambda i: (i, 0))
    oshape = jax.ShapeDtypeStruct((NSH, D, SHC), F32)
    out = None
    for k, dz_k in enumerate(dz_shards[:-1]):
        def body(u_ref, dz_ref, *rest):
            o_ref = rest[-1]

            @pl.when(pl.program_id(0) == 0)
            def _():
                o_ref[...] = jnp.zeros_like(o_ref)
            o_ref[0] += _dot(u_ref[...], dz_ref[...])

        out = pl.pallas_call(
            body, grid=(nt,), in_specs=[spec_u, spec_dz] + ([] if out is None else [_ANY]),
            out_specs=pl.BlockSpec((1, D, SHC), lambda i, k=k: (k, 0, 0)), out_shape=oshape,
            input_output_aliases={} if out is None else {2: 0},
            name=f"wgrad_in{k}", compiler_params=_cp(("arbitrary",)))(u_all, dz_k, *(() if out is None else (out,)))

    def last(u_ref, dz_ref, done_ref, *rest):
        other_refs, o_ref, got_refs = rest[:no], rest[no], rest[no + 1:2 * no + 2]
        send, recv = rest[2 * no + 2:]
        i = pl.program_id(0)
        give = ([done_ref] + list(other_refs), got_refs, send, recv, [tuple(range(NSH - 1))] + [tuple(range(NSH))] * no)

        @pl.when(i == 0)
        def _():
            o_ref[...] = jnp.zeros_like(o_ref)
            for cp in _rs_sibling_copies(*give):
                cp.start()
        o_ref[0] += _dot(u_ref[...], dz_ref[...])

        @pl.when(i == nt - 1)
        def _():
            for cp in _rs_sibling_copies(*give):
                cp.wait()

    half = lambda g: jax.ShapeDtypeStruct((NSH, g.shape[1] // 2, g.shape[2]), F32)
    outs = pl.pallas_call(
        last, grid=(nt,), in_specs=[spec_u, spec_dz, _ANY] + [_ANY] * no,
        out_specs=[pl.BlockSpec((1, D, SHC), lambda i: (NSH - 1, 0, 0))] + [_ANY] * (no + 1),
        out_shape=[oshape, half(oshape)] + [half(g) for g in others], input_output_aliases={2: 0},
        scratch_shapes=[pltpu.SemaphoreType.DMA((NSH * (no + 1),)), pltpu.SemaphoreType.DMA((NSH * (no + 1),))],
        name=f"wgrad_in{NSH - 1}", compiler_params=_cp(("arbitrary",)))(u_all, dz_shards[-1], out, *others)
    return outs[0], list(outs[1:])


def _du(dz_shards, w_in_g, x, ctx, mod, gxres, sums=()):
    tm = 256
    n_lat_tiles, nct = x.shape[0] // tm, ctx.shape[0] // tm
    nt = n_lat_tiles + nct
    ns = len(sums)
    rblk = lambda i: jnp.where(i < nct, n_lat_tiles + i, i - nct)
    lblk = lambda i: jnp.maximum(i - nct, 0)

    def body(*refs):
        dz_refs, refs = refs[:NSH], refs[NSH:]
        w_ref, x_ref, c_ref, mod_ref, gr_ref = refs[:5]
        sum_refs = refs[5:5 + ns]
        gx_ref, dm_ref = refs[5 + ns:7 + ns]
        got_refs = refs[7 + ns:7 + 2 * ns]
        sems = refs[7 + 2 * ns:]
        i = pl.program_id(0)
        is_lat = i >= nct

        @pl.when(i == 0)
        def _():
            dm_ref[...] = jnp.zeros_like(dm_ref)
            if ns:
                for cp in _rs_chip_copies(sum_refs, got_refs, *sems):
                    cp.start()

        du = _dot_nt(dz_refs[0][...], w_ref[0])
        for n in range(1, NSH):
            du = du + _dot_nt(dz_refs[n][...], w_ref[n])
        sc = jnp.where(is_lat, mod_ref[0:1, D:2 * D], mod_ref[1:2, D:2 * D])
        dsh = _colsum(du)
        dsc = _colsum(du * jnp.where(is_lat, x_ref[...], c_ref[...]))

        @pl.when(is_lat)
        def _():
            gx_ref[...] = du * (1.0 + sc) + gr_ref[...]
            dm_ref[0:1, 0:D] += dsh
            dm_ref[0:1, D:2 * D] += dsc

        @pl.when(jnp.logical_not(is_lat))
        def _():
            dm_ref[1:2, 0:D] += dsh
            dm_ref[1:2, D:2 * D] += dsc

        if ns:
            @pl.when(i == nt - 1)
            def _():
                for cp in _rs_chip_copies(sum_refs, got_refs, *sems):
                    cp.wait()

    outs = pl.pallas_call(
        body, grid=(nt,),
        in_specs=[pl.BlockSpec((tm, SHC), lambda i: (rblk(i), 0))] * NSH + [
                  pl.BlockSpec((NSH, D, SHC), lambda i: (0, 0, 0), pipeline_mode=pl.Buffered(1)),
                  pl.BlockSpec((tm, D), lambda i: (lblk(i), 0)),
                  pl.BlockSpec((tm, D), lambda i: (jnp.minimum(i, nct - 1), 0)),
                  pl.BlockSpec((16, 3 * D), lambda i: (0, 0)),
                  pl.BlockSpec((tm, D), lambda i: (lblk(i), 0))] + [_ANY] * ns,
        out_specs=[pl.BlockSpec((tm, D), lambda i: (lblk(i), 0)),
                   pl.BlockSpec((8, 2 * D), lambda i: (0, 0))] + [_ANY] * ns,
        out_shape=[jax.ShapeDtypeStruct((n_lat_tiles * tm, D), F32), jax.ShapeDtypeStruct((8, 2 * D), F32)]
        + [jax.ShapeDtypeStruct((3,) + g.shape[1:], g.dtype) for g in sums],
        scratch_shapes=[pltpu.SemaphoreType.DMA((3 * ns,)), pltpu.SemaphoreType.DMA((3 * ns,))] if ns else [],
        name="du", compiler_params=_cp(("arbitrary",)))(*dz_shards, w_in_g, x, ctx, mod, gxres, *sums)
    return outs[0], outs[1], list(outs[2:])


def _row_tile(rows, cols, block_bytes=1 << 20):
    t = 8
    while t * 2 * cols * 4 <= block_bytes and rows % (t * 2) == 0:
        t *= 2
    return t


def _adamw_update(w_ref, g_ref, m_ref, v_ref, d_ref, nm_ref, nv_ref):
    gg = g_ref[...]
    m2 = ADAM_B1 * m_ref[...] + (1.0 - ADAM_B1) * gg
    v2 = ADAM_B2 * v_ref[...] + (1.0 - ADAM_B2) * (gg * gg)
    m_hat = m2 / (1.0 - ADAM_B1 ** ADAM_STEP)
    v_hat = v2 / (1.0 - ADAM_B2 ** ADAM_STEP)
    d_ref[...] = -ADAM_LR * (m_hat / (jnp.sqrt(v_hat) + ADAM_EPS) + ADAM_WD * w_ref[...])
    nm_ref[...] = m2
    nv_ref[...] = v2


def _adamw_many(ws, gs, ms, vs):
    n = len(ws)

    def body(*refs):
        for j in range(n):
            _adamw_update(*refs[4 * j:4 * j + 4], *refs[4 * n + 3 * j:4 * n + 3 * j + 3])

    args = [a for quad in zip(ws, gs, ms, vs) for a in quad]
    outs = pl.pallas_call(body, out_shape=[jax.ShapeDtypeStruct(w.shape, F32) for w in ws for _ in range(3)],
                          name="adamw_small", compiler_params=_cp())(*args)
    return outs[0::3], outs[1::3], outs[2::3]


def _adamw(ws, gs, ms, vs, name):
    n = len(ws)
    rows, cols = ws[0].shape
    tr = _row_tile(rows, cols, (2 << 20) // n)

    def body(*refs):
        for j in range(n):
            _adamw_update(*refs[4 * j:4 * j + 4], *refs[4 * n + 3 * j:4 * n + 3 * j + 3])

    spec = pl.BlockSpec((tr, cols), lambda i: (i, 0))
    o = jax.ShapeDtypeStruct((rows, cols), F32)
    args = [a for quad in zip(ws, gs, ms, vs) for a in quad]
    outs = pl.pallas_call(body, grid=(rows // tr,), in_specs=[spec] * (4 * n), out_specs=[spec] * (3 * n),
                          out_shape=[o] * (3 * n), name=name, compiler_params=_cp(("arbitrary",)))(*args)
    return outs[0::3], outs[1::3], outs[2::3]


_ANY = pl.BlockSpec(memory_space=pl.ANY)


def _place():
    return lax.axis_index("x"), lax.axis_index("y"), lax.axis_index("c")


def _ag_copies(ins, outs, send, recv, fsend, frecv, lsem):
    x, y, c = _place()
    me = 2 * x + y
    chips = ((1 - x, y), (x, 1 - y), (1 - x, 1 - y))
    local, chip, hand = [], [], []
    for j in range(len(ins)):
        hr = ins[j].shape[0] // 2
        half = pl.ds(pl.multiple_of(c * hr, 8), hr)
        local.append(pltpu.make_async_copy(ins[j], outs[j].at[me], lsem.at[j]))
        for k, (px, py) in enumerate(chips):
            chip.append(pltpu.make_async_remote_copy(
                src_ref=ins[j].at[half, :], dst_ref=outs[j].at[me, half, :], send_sem=send.at[3 * j + k],
                recv_sem=recv.at[3 * j + k], device_id=(px, py, c), device_id_type=MESH))
            got = outs[j].at[2 * px + py, half, :]
            hand.append(pltpu.make_async_remote_copy(
                src_ref=got, dst_ref=got, send_sem=fsend.at[3 * j + k], recv_sem=frecv.at[3 * j + k],
                device_id=(x, y, 1 - c), device_id_type=MESH))
    return local, chip, hand


def _ag_sems(n):
    return [pltpu.SemaphoreType.DMA((3 * n,))] * 4 + [pltpu.SemaphoreType.DMA((n,))]


def _ag_finish(local, chip, hand, done=0):
    for k in range(done, len(chip)):
        chip[k].wait_recv()
        hand[k].start()
    for cp in chip:
        cp.wait_send()
    for k in range(done):
        hand[k].wait_send()
    for k in range(done, len(chip)):
        hand[k].wait_send()
        hand[k].wait_recv()
    for cp in local:
        cp.wait()


def _mod_tp(c8, c_ctx, w_mod_sh, b_mod_sh):
    mc = w_mod_sh.shape[1]

    def body(c8_ref, cctx_ref, w_ref, b_ref, mod_ref, cc_ref, cc_s, part_s, send1, recv1, send3, recv3):
        x, y, c = _place()
        me = 4 * x + 2 * y + c
        ms = 2 * x + y
        copies = []
        for k in range(1, 8):
            peer = (x ^ ((k >> 2) & 1), y ^ ((k >> 1) & 1), c ^ (k & 1))
            cp = pltpu.make_async_remote_copy(src_ref=c8_ref, dst_ref=cc_s.at[me], send_sem=send1.at[k],
                                              recv_sem=recv1.at[k], device_id=peer, device_id_type=MESH)
            cp.start()
            copies.append(cp)
        cc_s[me] = c8_ref[...]
        for cp in copies:
            cp.wait()
        cc_ref[...] = jnp.zeros_like(cc_ref)
        for j in range(8):
            cc_ref[j:j + 1, :] = cc_s[j, 0:1, :]
        cc_ref[8:9, :] = cctx_ref[...]
        v = cc_ref[...]
        part_s[ms] = _dot(v * _sig(v), w_ref[...]) + b_ref[...]
        copies = []
        for k in range(1, 4):
            peer = (x ^ ((k >> 1) & 1), y ^ (k & 1), c)
            cp = pltpu.make_async_remote_copy(src_ref=part_s.at[ms], dst_ref=part_s.at[ms], send_sem=send3.at[k],
                                              recv_sem=recv3.at[k], device_id=peer, device_id_type=MESH)
            cp.start()
            copies.append(cp)
        for cp in copies:
            cp.wait()
        for s in range(NSH):
            mod_ref[:, s * mc:(s + 1) * mc] = part_s[s]

    vm = pl.BlockSpec(memory_space=pltpu.VMEM)
    return pl.pallas_call(
        body, in_specs=[vm] * 4, out_specs=(vm, vm),
        out_shape=(jax.ShapeDtypeStruct((16, NSH * mc), F32), jax.ShapeDtypeStruct((16, D), F32)),
        scratch_shapes=[pltpu.VMEM((8, 8, D), F32), pltpu.VMEM((NSH, 16, mc), F32),
                        pltpu.SemaphoreType.DMA((8,)), pltpu.SemaphoreType.DMA((8,)),
                        pltpu.SemaphoreType.DMA((4,)), pltpu.SemaphoreType.DMA((4,))],
        name="mod_tp", compiler_params=_cp())(c8, c_ctx, w_mod_sh, b_mod_sh)


def _inproj_ag(x, ctx, mod, w_in_sh, b_in, narrow_sh):
    n_lat = x.shape[0]
    m = n_lat + ctx.shape[0]
    assert m % (11 * 128) == 0
    tm = m // 11
    nt = m // tm
    nfull = n_lat // tm
    assert nfull == nt - 1
    tail = jnp.concatenate([x[nfull * tm:], ctx], axis=0)
    x_, y_ = lax.axis_index("x"), lax.axis_index("y")
    sids = jnp.stack([2 * x_ + y_, 2 * (1 - x_) + y_, 2 * x_ + 1 - y_, 2 * (1 - x_) + 1 - y_]).astype(jnp.int32)

    def body(sid_ref, x_ref, tail_ref, mod_ref, b_ref, wsh_ref, nsh_ref, z_ref, u_ref, wg_ref, ng_ref, w_s, u_s, *sems):
        n = pl.program_id(0)
        i = pl.program_id(1)
        rows = pl.ds(pl.multiple_of(i * tm, tm), tm)
        ag = ((wsh_ref, nsh_ref), (wg_ref, ng_ref)) + tuple(sems[:5])
        wsem = sems[5]

        def load(src):
            cp = pltpu.make_async_copy(src, w_s, wsem)
            cp.start()
            cp.wait()

        @pl.when((n == 0) & (i == 0))
        def _():
            local, chip, _ = _ag_copies(*ag)
            for cp in chip + local:
                cp.start()
            load(wsh_ref)

        for k in range(NSH - 1):
            @pl.when((n == k + 1) & (i == 0))
            def _():
                _, chip, hand = _ag_copies(*ag)
                chip[k].wait_recv()
                hand[k].start()
                hand[k].wait_recv()
                load(wg_ref.at[sid_ref[k + 1]])

        @pl.when(n == 0)
        def _():
            is_lat = (i * tm + lax.broadcasted_iota(jnp.int32, (tm, 1), 0)) < n_lat
            u = _modulate(jnp.where(i < nfull, x_ref[...], tail_ref[...]), mod_ref, is_lat)
            u_s[rows, :] = u.astype(BF)
            u_ref[...] = u.T.astype(BF)

        z_ref[...] = _dot(u_s[rows, :], w_s[...]) + b_ref[...]

        @pl.when((n == NSH - 1) & (i == nt - 1))
        def _():
            _ag_finish(*_ag_copies(*ag), done=NSH - 1)

    first = lambda n, i: jnp.where(n == 0, i, nt - 1)
    outs = pl.pallas_call(
        body, grid_spec=pltpu.PrefetchScalarGridSpec(
            num_scalar_prefetch=1, grid=(NSH, nt),
            in_specs=[pl.BlockSpec((tm, D), lambda n, i, sid: (jnp.minimum(first(n, i), nfull - 1), 0)),
                      pl.BlockSpec((tm, D), lambda n, i, sid: (0, 0)),
                      pl.BlockSpec((16, 3 * D), lambda n, i, sid: (0, 0)),
                      pl.BlockSpec((1, SHC), lambda n, i, sid: (0, sid[n])), _ANY, _ANY],
            out_specs=[pl.BlockSpec((tm, SHC), lambda n, i, sid: (i, sid[n])),
                       pl.BlockSpec((D, tm), lambda n, i, sid: (0, first(n, i))), _ANY, _ANY],
            scratch_shapes=[pltpu.VMEM((D, SHC), BF), pltpu.VMEM((m, D), BF)] + _ag_sems(2)
            + [pltpu.SemaphoreType.DMA]),
        out_shape=[jax.ShapeDtypeStruct((m, IN_COLS), F32), jax.ShapeDtypeStruct((D, m), BF),
                   jax.ShapeDtypeStruct((NSH,) + w_in_sh.shape, BF),
                   jax.ShapeDtypeStruct((NSH,) + narrow_sh.shape, narrow_sh.dtype)],
        name="inproj_ag", compiler_params=_cp(("arbitrary", "arbitrary")))(sids, x, tail, mod, b_in, w_in_sh, narrow_sh)
    return outs


def _rs_sibling_copies(ins, got, send, recv, shards):
    x, y, c = _place()
    copies = []
    for j in range(len(ins)):
        hr = ins[j].shape[1] // 2
        for s in shards[j]:
            give = ins[j].at[s, pl.ds(pl.multiple_of((1 - c) * hr, 8), hr), :]
            copies.append(pltpu.make_async_remote_copy(
                src_ref=give, dst_ref=got[j].at[s], send_sem=send.at[NSH * j + s], recv_sem=recv.at[NSH * j + s],
                device_id=(x, y, 1 - c), device_id_type=MESH))
    return copies


def _rs_sibling_last(g, got):
    def body(g_ref, _, got_ref, send, recv):
        copies = _rs_sibling_copies([g_ref], [got_ref], send, recv, [(NSH - 1,)])
        for cp in copies:
            cp.start()
        for cp in copies:
            cp.wait()

    return pl.pallas_call(
        body, in_specs=[_ANY, _ANY], out_specs=_ANY, out_shape=jax.ShapeDtypeStruct(got.shape, got.dtype),
        input_output_aliases={1: 0},
        scratch_shapes=[pltpu.SemaphoreType.DMA((NSH,)), pltpu.SemaphoreType.DMA((NSH,))],
        name="rs_sibling_last")(g, got)


def _core_vec():
    return lax.axis_index("c").astype(jnp.int32).reshape(1)


def _rs_add1(gs, gots, name):
    n = len(gs)
    _, r, cols = gs[0].shape
    hr = r // 2
    tr = _row_tile(hr, cols, (4 << 20) // n)
    nb = hr // tr

    def body(c_ref, *refs):
        for j in range(n):
            refs[2 * n + j][...] = (refs[j][...] + refs[n + j][...]).astype(BF)

    spec = pl.BlockSpec((1, tr, cols), lambda s, i, c_ref: (s, i, 0))
    keep = pl.BlockSpec((1, tr, cols), lambda s, i, c_ref: (s, c_ref[0] * nb + i, 0))
    return pl.pallas_call(
        body, grid_spec=pltpu.PrefetchScalarGridSpec(
            num_scalar_prefetch=1, grid=(NSH, nb), in_specs=[keep] * n + [spec] * n, out_specs=[spec] * n),
        out_shape=[jax.ShapeDtypeStruct((NSH, hr, cols), BF)] * n, name=name,
        compiler_params=_cp(("arbitrary", "arbitrary")))(_core_vec(), *gs, *gots)


def _rs_add2(sums, gots, name):
    n = len(sums)
    _, hr, cols = sums[0].shape
    tr = _row_tile(hr, cols, (2 << 20) // n)
    nb = hr // tr
    place = jnp.stack([2 * lax.axis_index("x") + lax.axis_index("y"), lax.axis_index("c")]).astype(jnp.int32)

    def body(p_ref, *refs):
        f = lambda v: v.astype(F32)
        for j in range(n):
            s_ref, got_ref = refs[j], refs[n + j]
            refs[2 * n + j][...] = f(s_ref[0]) + f(got_ref[0]) + f(got_ref[1]) + f(got_ref[2])

    return pl.pallas_call(
        body, grid_spec=pltpu.PrefetchScalarGridSpec(
            num_scalar_prefetch=1, grid=(nb,),
            in_specs=[pl.BlockSpec((1, tr, cols), lambda i, p_ref: (p_ref[0], i, 0))] * n
            + [pl.BlockSpec((3, tr, cols), lambda i, p_ref: (0, i, 0))] * n,
            out_specs=[pl.BlockSpec((tr, cols), lambda i, p_ref: (p_ref[1] * nb + i, 0))] * n),
        out_shape=[jax.ShapeDtypeStruct((2 * hr, cols), F32)] * n, name=name,
        compiler_params=_cp(("arbitrary",)))(place, *sums, *gots)


def _rs_chip_copies(ins, got, send, recv):
    x, y, c = _place()
    peers = ((1 - x, y), (x, 1 - y), (1 - x, 1 - y))
    return [pltpu.make_async_remote_copy(src_ref=ins[j].at[2 * px + py], dst_ref=got[j].at[k],
                                         send_sem=send.at[3 * j + k], recv_sem=recv.at[3 * j + k],
                                         device_id=(px, py, c), device_id_type=MESH)
            for j in range(len(ins)) for k, (px, py) in enumerate(peers)]


def _ag_sibling(fulls):
    n = len(fulls)
    nck = 4

    def body(*refs):
        outs = refs[n:2 * n]
        send, recv = refs[2 * n:]
        x, y, c = _place()
        copies = []
        for j in range(n):
            qr = outs[j].shape[0] // (2 * nck)
            for k in range(nck):
                rows = outs[j].at[pl.ds(pl.multiple_of((c * nck + k) * qr, 8), qr), :]
                cp = pltpu.make_async_remote_copy(src_ref=rows, dst_ref=rows, send_sem=send.at[nck * j + k],
                                                  recv_sem=recv.at[nck * j + k], device_id=(x, y, 1 - c),
                                                  device_id_type=MESH)
                cp.start()
                copies.append(cp)
        for cp in copies:
            cp.wait()

    return pl.pallas_call(
        body, in_specs=[_ANY] * n, out_specs=[_ANY] * n,
        out_shape=[jax.ShapeDtypeStruct(f.shape, F32) for f in fulls],
        input_output_aliases={j: j for j in range(n)},
        scratch_shapes=[pltpu.SemaphoreType.DMA((nck * n,)), pltpu.SemaphoreType.DMA((nck * n,))],
        name="ag_sibling")(*fulls)


def _allreduce_small(buf):
    rows = buf.shape[0]
    pr = rows // 8

    def body(in_ref, out_ref, part, stage_a, stage_b, *sems):
        (sa, ra, sb, rb, sc, rc, sd, rd) = sems
        x, y, c = _place()
        s = 2 * x + y
        sib = (x, y, 1 - c)
        chips = ((1 - x, y), (x, 1 - y), (1 - x, 1 - y))

        def piece(ref, chip, core):
            return ref.at[pl.ds(pl.multiple_of((2 * chip + core) * pr, 8), pr), :]

        def run(copies):
            for cp in copies:
                cp.start()
            for cp in copies:
                cp.wait()

        run([pltpu.make_async_remote_copy(src_ref=piece(in_ref, j, 1 - c), dst_ref=stage_a.at[j], send_sem=sa.at[j],
                                          recv_sem=ra.at[j], device_id=sib, device_id_type=MESH) for j in range(NSH)])
        for j in range(NSH):
            part[j] = piece(in_ref, j, c)[...] + stage_a[j]
        run([pltpu.make_async_remote_copy(src_ref=part.at[2 * px + py], dst_ref=stage_b.at[k], send_sem=sb.at[k],
                                          recv_sem=rb.at[k], device_id=(px, py, c), device_id_type=MESH)
             for k, (px, py) in enumerate(chips)])
        piece(out_ref, s, c)[...] = part[s] + stage_b[0] + stage_b[1] + stage_b[2]
        run([pltpu.make_async_remote_copy(src_ref=piece(out_ref, s, c), dst_ref=piece(out_ref, s, c),
                                          send_sem=sc.at[k], recv_sem=rc.at[k], device_id=(px, py, c),
                                          device_id_type=MESH) for k, (px, py) in enumerate(chips)])
        run([pltpu.make_async_remote_copy(src_ref=piece(out_ref, j, c), dst_ref=piece(out_ref, j, c), send_sem=sd.at[j],
                                          recv_sem=rd.at[j], device_id=sib, device_id_type=MESH) for j in range(NSH)])

    vm = pl.BlockSpec(memory_space=pltpu.VMEM)
    return pl.pallas_call(
        body, in_specs=[vm], out_specs=vm, out_shape=jax.ShapeDtypeStruct((rows, D), F32),
        scratch_shapes=[pltpu.VMEM((NSH, pr, D), F32), pltpu.VMEM((NSH, pr, D), F32), pltpu.VMEM((3, pr, D), F32)]
        + [pltpu.SemaphoreType.DMA((NSH,))] * 8,
        name="allreduce_small", compiler_params=_cp())(buf)


def _pad_rows(a, mult):
    pad = (-a.shape[0]) % mult
    return jnp.concatenate([a, jnp.zeros((pad, a.shape[1]), a.dtype)]) if pad else a


def _local_step(x, c, ctx, c_ctx, tgt, me, shard, sh, b_mod, b_in, norm_g, cb, wr, wi, ln_g, ln_b):
    t_lat, t_ctx = x.shape[0], ctx.shape[0]
    nlb, ncb = t_lat // RB, t_ctx // RB
    mc = 3 * D // NSH
    mod_all, cc_all = _mod_tp(jnp.zeros((8, D), F32).at[0].set(c), c_ctx.reshape(1, D), sh["w_mod"],
                              lax.dynamic_slice_in_dim(b_mod, shard * mc, mc, axis=1))
    mod = jnp.zeros((16, 3 * D), F32).at[0].set(mod_all[me]).at[1].set(mod_all[8])
    cc = jnp.zeros((16, D), F32).at[0].set(c).at[1].set(c_ctx)
    z, u_all, w_in_g, nar = _inproj_ag(x, ctx, mod, sh["w_in"], b_in, sh["narrow"])
    nar = jnp.transpose(nar, (1, 0, 2)).reshape(-1, D)
    lbl, cw, br, bi, lam = nar[0:4].reshape(2, 2, D), nar[4:8], nar[8:10], nar[10:12], nar[12:14]
    o0, st0, (p_a, p_b, w_out) = _gla_fwd(z, lbl, 0, nlb, ncb, gather=[sh[k] for k in ("p_a", "p_b", "w_out")])
    p_a, p_b, w_out = p_a.reshape(D, D), p_b.reshape(D, D), w_out.reshape(D, D)
    o1, st1, (w_mod_g,) = _gla_fwd(z, lbl, 1, nlb, ncb, gather=[sh["w_mod"]])
    h0, hp0, a0, hpc0 = _rglru_fwd(z, cw, cb, wr, br, wi, bi, lam, 0, t_lat, t_ctx)
    h1, hp1, a1, hpc1 = _rglru_fwd(z, cw, cb, wr, br, wi, bi, lam, 1, t_lat, t_ctx)
    (do, dh, dz4a, dz4b, dz6a, dz_sh3, gxres, y, dout, oa, dpa, obv, dpb, acc) = _merge(
        o0, o1, h0, h1, z, x, tgt, mod, norm_g, ln_g, ln_b, p_a, p_b, w_out, t_lat)
    gp_a = _wgrad(oa, dpa, "wgrad_pa")
    gp_b = _wgrad(obv, dpb, "wgrad_pb")
    gw_out = _wgrad(y, dout, "wgrad_wout")
    dxc_lat, dxc_ctx, dwr0, dwi0, sb0 = _rglru_bwd(z, cw, cb, wr, br, wi, bi, lam, dh, hp0, a0, hpc0, 0, t_lat, t_ctx)
    dz5_lat, dz5_ctx, dwr1, dwi1, sb1 = _rglru_bwd(z, cw, cb, wr, br, wi, bi, lam, dh, hp1, a1, hpc1, 1, t_lat, t_ctx,
                                                   prev=(dxc_lat, dxc_ctx))
    dq0, dv0, dz1, sa0 = _gla_bwd(z, lbl, do, st0, 0, nlb, ncb)
    dz_sh0, dz_sh1, sa1 = _gla_bwd(z, lbl, do, st1, 1, nlb, ncb, prev=(dq0, dv0, dz1, dz4a))
    dz = (dz_sh0, dz_sh1, _pack_shard2(dz4b, dz5_lat, dz5_ctx, dz6a), dz_sh3)
    g_p = [g.reshape(NSH, D // NSH, D) for g in (gp_a, gp_b, gw_out)]
    gw_in, got = _wgrad_in(u_all, dz, g_p)
    got[0] = _rs_sibling_last(gw_in, got[0])
    grads = [gw_in] + g_p
    sums = list(_rs_add1(grads[:1], got[:1], "rs_add1_w")) + list(_rs_add1(grads[1:], got[1:], "rs_add1_p"))
    gx, dm, got = _du(dz, w_in_g, x, ctx, mod, gxres, sums)
    fulls = list(_rs_add2(sums[:1], got[:1], "rs_add2_w")) + list(_rs_add2(sums[1:], got[1:], "rs_add2_p"))
    big = dict(zip(_RS, _ag_sibling(fulls)))
    dmod = jnp.zeros((16, 3 * D), F32).at[0:2, 0:2 * D].set(dm[0:2]).at[0, 2 * D:].set(acc[0])
    dcc = _mod_bwd(cc, dmod, w_mod_g)
    small = dict(
        c_ctx=dcc[1:2], b_mod=(dmod[0] + dmod[1]).reshape(3, D),
        b_in=jnp.stack([sa1[2], sa0[0], sa1[0], sa1[3], acc[3], sb1[4], acc[4], acc[5], acc[6]]),
        lb_logits=jnp.stack([sa0[1], sa1[1], -sa0[1], -sa1[1]]),
        norm_a_g=acc[7:8], conv_w=sb1[8:12], conv_b=sb1[3:4],
        w_r=jnp.stack([dwr0, dwr1]).reshape(-1, D), w_i=jnp.stack([dwi0, dwi1]).reshape(-1, D),
        b_r=jnp.stack([sb0[0], sb1[0]]), b_i=jnp.stack([sb0[1], sb1[1]]), lam=jnp.stack([sb0[2], sb1[2]]),
        ln_g=acc[1:2], ln_b=acc[2:3])
    return acc[9, 0], gx, big, small, dmod, cc_all


_RS =("w_in", "p_a", "p_b", "w_out")
_SMALL =("c_ctx", "b_mod", "b_in", "lb_logits", "norm_a_g", "conv_w", "conv_b", "w_r", "w_i", "b_r", "b_i", "lam",
          "ln_g", "ln_b")
_BIG = ("w_mod", "w_in", "p_a", "p_b", "w_out")
_COL_SHARDED = ("lb_logits", "conv_w", "b_r", "b_i", "lam")
_WEIGHTS = ("c_ctx", "w_mod", "b_mod", "w_in", "b_in", "lb_logits", "norm_a_g", "conv_w", "conv_b", "w_r", "b_r", "w_i",
            "b_i", "lam", "p_a", "p_b", "w_out", "ln_g", "ln_b")


def kernel(x, c, ctx, c_ctx, w_mod, b_mod, w_in, b_in, lb_logits, norm_a_g, conv_w, conv_b, w_r, b_r, w_i, b_i, lam, p_a, p_b, w_out, ln_g, ln_b, loss_target, m_c_ctx, m_w_mod, m_b_mod, m_w_in, m_b_in, m_lb_logits, m_norm_a_g, m_conv_w, m_conv_b, m_w_r, m_b_r, m_w_i, m_b_i, m_lam, m_p_a, m_p_b, m_w_out, m_ln_g, m_ln_b, v_c_ctx, v_w_mod, v_b_mod, v_w_in, v_b_in, v_lb_logits, v_norm_a_g, v_conv_w, v_conv_b, v_w_r, v_b_r, v_w_i, v_b_i, v_lam, v_p_a, v_p_b, v_w_out, v_ln_g, v_ln_b):
    w = dict(c_ctx=c_ctx, w_mod=w_mod, b_mod=b_mod, w_in=w_in, b_in=b_in, lb_logits=lb_logits, norm_a_g=norm_a_g,
             conv_w=conv_w, conv_b=conv_b, w_r=w_r, b_r=b_r, w_i=w_i, b_i=b_i, lam=lam, p_a=p_a, p_b=p_b, w_out=w_out,
             ln_g=ln_g, ln_b=ln_b)
    m = dict(c_ctx=m_c_ctx, w_mod=m_w_mod, b_mod=m_b_mod, w_in=m_w_in, b_in=m_b_in, lb_logits=m_lb_logits,
             norm_a_g=m_norm_a_g, conv_w=m_conv_w, conv_b=m_conv_b, w_r=m_w_r, b_r=m_b_r, w_i=m_w_i, b_i=m_b_i,
             lam=m_lam, p_a=m_p_a, p_b=m_p_b, w_out=m_w_out, ln_g=m_ln_g, ln_b=m_ln_b)
    v = dict(c_ctx=v_c_ctx, w_mod=v_w_mod, b_mod=v_b_mod, w_in=v_w_in, b_in=v_b_in, lb_logits=v_lb_logits,
             norm_a_g=v_norm_a_g, conv_w=v_conv_w, conv_b=v_conv_b, w_r=v_w_r, b_r=v_b_r, w_i=v_w_i, b_i=v_b_i,
             lam=v_lam, p_a=v_p_a, p_b=v_p_b, w_out=v_w_out, ln_g=v_ln_g, ln_b=v_ln_b)
    shard = 2 * lax.axis_index("x") + lax.axis_index("y")
    cs = D // NSH

    sh = {k: w[k][0].astype(BF) for k in _BIG}
    sh["narrow"] = _pad_rows(jnp.concatenate([lb_logits.reshape(4, cs), conv_w[0], b_r[0], b_i[0], lam[0]], axis=0), 8)
    me = 2 * shard + lax.axis_index("c")
    loss, gx, big, small, dmod, cc_all = _local_step(
        x[0], c[0], ctx[0], c_ctx, loss_target[0], me, shard, sh, b_mod, b_in, norm_a_g, conv_b, w_r[0], w_i[0],
        ln_g, ln_b)

    dmod_rows = jnp.zeros((16, 3 * D), F32).at[me].set(dmod[0]).at[8].set(dmod[1]).reshape(48, D)
    loss_rows = jnp.zeros((8, D), F32).at[0, 0].set(loss)
    sizes = [small[k].shape[0] for k in _SMALL]
    red = _allreduce_small(_pad_rows(jnp.concatenate([_pad_rows(small[k], 8) for k in _SMALL] + [dmod_rows, loss_rows],
                                                     axis=0), 64))
    grads = {}
    off = 0
    for k, n in zip(_SMALL, sizes):
        g = red[off:off + n]
        off += n + (-n) % 8
        if k == "norm_a_g":
            g = g[:, :DH]
        if k in _COL_SHARDED:
            g = lax.dynamic_slice_in_dim(g, shard * cs, cs, axis=1)
        grads[k] = g.reshape(w[k].shape)
    for k in _RS:
        grads[k] = big[k].reshape(w[k].shape)
    dmod_all = red[off:off + 48].reshape(16, 3 * D)
    loss = red[off + 48, 0]
    mc = 3 * D // NSH
    grads["w_mod"] = _wmod_grad(cc_all, lax.dynamic_slice_in_dim(dmod_all, shard * mc, mc, axis=1)).reshape(
        w["w_mod"].shape)

    delta, new_m, new_v = {}, {}, {}
    for group in (("w_mod",), ("w_in",), ("p_a", "p_b", "w_out")):
        two = lambda t: [t[k].reshape(t[k].shape[-2:]) for k in group]
        d_, m_, v_ = _adamw(two(w), two(grads), two(m), two(v), "adamw_" + group[0])
        for j, k in enumerate(group):
            shp = w[k].shape
            delta[k], new_m[k], new_v[k] = d_[j].reshape(shp), m_[j].reshape(shp), v_[j].reshape(shp)
    d_, m_, v_ = _adamw_many(*[[t[k] for k in _SMALL] for t in (w, grads, m, v)])
    delta.update(zip(_SMALL, d_))
    new_m.update(zip(_SMALL, m_))
    new_v.update(zip(_SMALL, v_))

    return (loss, gx[None], *[grads[k] for k in _WEIGHTS], *[delta[k] for k in _WEIGHTS],
            *[new_m[k] for k in _WEIGHTS], *[new_v[k] for k in _WEIGHTS])
```

```python
import jax
import jax.numpy as jnp
from jax import lax
from jax.experimental import pallas as pl
from jax.experimental.pallas import tpu as pltpu

F32 = jnp.float32
BF = jnp.bfloat16
MESH = pl.DeviceIdType.MESH

D = 1024
H = 8
DH = 128
CH = 64
RB = 256
NCK = RB // CH
GRID_W = 64
CB = 8
RCH = 16
IN_COLS = 9 * D
NSH = 4
SHC = IN_COLS // NSH
MC = 3 * D // NSH
CUT2 = SHC - 2 * D
CUT4 = 2 * SHC - 4 * D
CUT6 = 3 * SHC - 6 * D
RG_C = 8.0
ALPHA = 2.0 ** 0.25
LN_EPS = 1e-5
RMS_EPS = 1e-6
Q_SCALE = DH ** -0.5
ADAM_LR, ADAM_B1, ADAM_B2, ADAM_EPS, ADAM_WD, ADAM_STEP = 0.001, 0.9, 0.999, 1e-08, 0.01, 10
VMEM_LIMIT = 56 * 1024 * 1024
VMEM_LIMIT_MERGE = 60 * 1024 * 1024
VMEM_LIMIT_RGLRU_BWD = 63 * 1024 * 1024


def _cp(sem=None, vmem=VMEM_LIMIT):
    return pltpu.CompilerParams(dimension_semantics=sem, vmem_limit_bytes=vmem)


def _sig(x):
    return 0.5 * jnp.tanh(0.5 * x) + 0.5


def _sig_tail(x):
    return 1.0 / (1.0 + jnp.exp(-x))


def _dot(a, b):
    return jnp.dot(a.astype(BF), b.astype(BF), preferred_element_type=F32)


def _dot_nt(a, b):
    return lax.dot_general(a.astype(BF), b.astype(BF), (((1,), (1,)), ((), ())), preferred_element_type=F32)


def _dot_tn(a, b):
    return lax.dot_general(a.astype(BF), b.astype(BF), (((0,), (0,)), ((), ())), preferred_element_type=F32)


def _colsum(v):
    return jnp.sum(v, axis=0, keepdims=True)


def _mod_bwd(cc, dmod, w_mod_g):
    def body(cc_ref, dm_ref, w_ref, dcc_ref):
        v = cc_ref[...]
        sg = _sig(v)
        ds = jnp.zeros((16, D), F32)
        for k in range(NSH):
            ds = ds + _dot_nt(dm_ref[:, k * MC:(k + 1) * MC], w_ref[k])
        dcc_ref[...] = ds * (sg * (1.0 + v * (1.0 - sg)))
    return pl.pallas_call(body, out_shape=jax.ShapeDtypeStruct((16, D), F32),
                          name="mod_bwd", compiler_params=_cp())(cc, dmod, w_mod_g)


def _wmod_grad(cc, dmod_cols):
    def body(cc_ref, dm_ref, dw_ref):
        v = cc_ref[...]
        dw_ref[...] = _dot_tn(v * _sig(v), dm_ref[...])
    return pl.pallas_call(body, out_shape=jax.ShapeDtypeStruct((D, dmod_cols.shape[1]), F32),
                          name="wmod_grad", compiler_params=_cp())(cc, dmod_cols)


def _modulate(xv, mod_ref, is_lat):
    sh = jnp.where(is_lat, mod_ref[0:1, 0:D], mod_ref[1:2, 0:D])
    sc = jnp.where(is_lat, mod_ref[0:1, D:2 * D], mod_ref[1:2, D:2 * D])
    return xv * (1.0 + sc) + sh


def _gla_mask(d, n):
    row = lax.broadcasted_iota(jnp.int32, (n, n), 0)
    col = lax.broadcasted_iota(jnp.int32, (n, n), 1)
    same = (row // CH) == (col // CH)
    return same & ((row <= col) if d else (row >= col))


def _chunk_cumsum(v, rev):
    n = v.shape[0]
    pos = lax.broadcasted_iota(jnp.int32, v.shape, 0) & (CH - 1)
    s = 1
    while s < CH:
        if rev:
            v = v + jnp.where(pos < CH - s, pltpu.roll(v, n - s, 0), 0.0)
        else:
            v = v + jnp.where(pos >= s, pltpu.roll(v, s, 0), 0.0)
        s *= 2
    return v


def _chunk_rows(c):
    return slice(c * CH, (c + 1) * CH)


def _gla_features(zq, zf, lb, d):
    sq = _sig(zq)
    q = zq * sq * Q_SCALE
    sf = _sig(zf)
    f = lb + (1.0 - lb) * sf
    k = 1.0 - f
    g = _chunk_cumsum(jnp.log(f), d)
    last = 0 if d else CH - 1
    gls = [g[c * CH + last:c * CH + last + 1, :] for c in range(NCK)]
    glb = jnp.concatenate([jnp.broadcast_to(gl, (CH, D)) for gl in gls], axis=0)
    eg, eig, eeg = jnp.exp(g), jnp.exp(-g), jnp.exp(glb - g)
    decs = [jnp.exp(gl) for gl in gls]
    return sq, sf, f, q * eg, k * eig, k * eeg, eg, eig, eeg, decs


def _lower_bound(lbl_ref, d):
    return _sig_tail(lbl_ref[0, d:d + 1, :] - lbl_ref[1, d:d + 1, :])


def _gla_rb(d, nlb, ncb):
    nrb = nlb + ncb
    if d:
        return lambda s: nrb - 1 - s
    return lambda s: jnp.where(s < ncb, nlb + s, s - ncb)


def _gla_fwd(z, lbl, d, nlb, ncb, gather=()):
    m = z.shape[0]
    nrb = nlb + ncb
    rb = _gla_rb(d, nlb, ncb)
    ng = len(gather)

    def body(*refs):
        q_ref, f_ref, v_ref, lbl_ref = refs[:4]
        ag_in = refs[4:4 + ng]
        o_ref, st_ref = refs[4 + ng:6 + ng]
        ag_out = refs[6 + ng:6 + 2 * ng]
        S = refs[6 + 2 * ng]
        ag_sems = refs[7 + 2 * ng:]
        s = pl.program_id(0)

        @pl.when(s == 0)
        def _():
            S[...] = jnp.zeros_like(S)
            if ng:
                local, chip, _ = _ag_copies(ag_in, ag_out, *ag_sems)
                for cp in chip + local:
                    cp.start()

        lb = _lower_bound(lbl_ref, d)
        mb = _gla_mask(d, CH)
        _, _, _, qd, ki, ke, _, _, _, decs = _gla_features(q_ref[...], f_ref[...], lb, d)
        qd, ki, ke, v = qd.astype(BF), ki.astype(BF), ke.astype(BF), v_ref[...].astype(BF)
        order = range(NCK - 1, -1, -1) if d else range(NCK)
        for h in range(H):
            hs = slice(h * DH, (h + 1) * DH)
            intra, upd = {}, {}
            for c in range(NCK):
                rs = _chunk_rows(c)
                a = jnp.where(mb, _dot_nt(qd[rs, hs], ki[rs, hs]), 0.0)
                intra[c] = _dot(a, v[rs, hs])
                upd[c] = _dot_tn(v[rs, hs], ke[rs, hs])
            st = S[h]
            for c in order:
                rs = _chunk_rows(c)
                st_ref[c, h] = st
                o_ref[rs, hs] = intra[c] + _dot_nt(qd[rs, hs], st)
                st = st * decs[c][:, hs] + upd[c]
            S[h] = st

        if ng:
            @pl.when(s == nrb - 1)
            def _():
                _ag_finish(*_ag_copies(ag_in, ag_out, *ag_sems))

    def zspec(cb):
        return pl.BlockSpec((RB, D), lambda s: (rb(s), cb))

    outs = pl.pallas_call(
        body, grid=(nrb,),
        in_specs=[zspec(0), zspec(1 + d), zspec(3), pl.BlockSpec((2, 2, D), lambda s: (0, 0, 0))] + [_ANY] * ng,
        out_specs=[pl.BlockSpec((RB, D), lambda s: (rb(s), 0)),
                   pl.BlockSpec((NCK, H, DH, DH), lambda s: (rb(s), 0, 0, 0))] + [_ANY] * ng,
        out_shape=[jax.ShapeDtypeStruct((m, D), F32), jax.ShapeDtypeStruct((m // CH, H, DH, DH), F32)]
        + [jax.ShapeDtypeStruct((NSH,) + g.shape, g.dtype) for g in gather],
        scratch_shapes=[pltpu.VMEM((H, DH, DH), F32)] + (_ag_sems(ng) if ng else []),
        name=f"gla_fwd{d}", compiler_params=_cp(("arbitrary",)))(z, z, z, lbl, *gather)
    return outs[0], outs[1], list(outs[2:])


def _gla_bwd(z, lbl, do_lat, states, d, nlb, ncb, prev=None):
    m = z.shape[0]
    nrb = nlb + ncb
    fwd_rb = _gla_rb(d, nlb, ncb)
    rb = lambda s: fwd_rb(nrb - 1 - s)
    last = prev is not None

    def body(*refs):
        if last:
            (q_ref, f_ref, v_ref, lbl_ref, do_ref, st_ref, pq_ref, pv_ref, dz1_ref, dz4a_ref, sh0_ref, sh1_ref, sum_ref,
             dS) = refs
        else:
            q_ref, f_ref, v_ref, lbl_ref, do_ref, st_ref, o0_ref, o1_ref, o2_ref, sum_ref, dS = refs
        s = pl.program_id(0)
        is_lat = rb(s) < nlb

        @pl.when(s == 0)
        def _():
            dS[...] = jnp.zeros_like(dS)
            sum_ref[...] = jnp.zeros_like(sum_ref)

        lb = _lower_bound(lbl_ref, d)
        mb = _gla_mask(d, RB)
        zq = q_ref[...]
        sq, sf, f, qd, ki, ke, eg, eig, eeg, decs = _gla_features(zq, f_ref[...], lb, d)
        qdb, kib, keb, vb = qd.astype(BF), ki.astype(BF), ke.astype(BF), v_ref[...].astype(BF)
        dob = jnp.where(is_lat, do_ref[...], 0.0).astype(BF)
        order = range(NCK) if d else range(NCK - 1, -1, -1)
        dqd_h, dki_h, dke_h, dv_h, ddec_h = [], [], [], [], []
        for h in range(H):
            hs = slice(h * DH, (h + 1) * DH)
            a = jnp.where(mb, _dot_nt(qdb[:, hs], kib[:, hs]), 0.0).astype(BF)
            da = jnp.where(mb, _dot_nt(dob[:, hs], vb[:, hs]), 0.0).astype(BF)
            dqd_i = _dot(da, kib[:, hs])
            dki_h.append(_dot_tn(da, qdb[:, hs]))
            dvi = _dot_tn(a, dob[:, hs])
            dqd, inc = {}, {}
            for c in range(NCK):
                rs = _chunk_rows(c)
                dqd[c] = dqd_i[rs, :] + _dot(dob[rs, hs], st_ref[c, h])
                inc[c] = _dot_tn(dob[rs, hs], qdb[rs, hs])
            dst = dS[h]
            dke, dv, ddec = {}, {}, {}
            for c in order:
                rs = _chunk_rows(c)
                dv[c] = dvi[rs, :] + _dot_nt(keb[rs, hs], dst)
                dke[c] = _dot(vb[rs, hs], dst)
                ddec[c] = _colsum(st_ref[c, h] * dst)
                dst = inc[c] + dst * decs[c][:, hs]
            dS[h] = dst
            cat = lambda t: jnp.concatenate([t[c] for c in range(NCK)], axis=0)
            dqd_h.append(cat(dqd))
            dke_h.append(cat(dke))
            dv_h.append(cat(dv))
            ddec_h.append([ddec[c] for c in range(NCK)])
        lanes = lambda parts: jnp.concatenate(parts, axis=1)
        dqd, dki, dke, dv = lanes(dqd_h), lanes(dki_h), lanes(dke_h), lanes(dv_h)
        dq = dqd * eg
        dk = dki * eig + dke * eeg
        dke_ke = dke * ke
        dg = dqd * qd - dki * ki - dke_ke
        dgl = [_colsum(dke_ke[_chunk_rows(c), :]) + lanes([ddec_h[h][c] for h in range(H)]) * decs[c]
               for c in range(NCK)]
        dglb = jnp.concatenate([jnp.broadcast_to(t, (CH, D)) for t in dgl], axis=0)
        df = (_chunk_cumsum(dg, 1 - d) + dglb) / f - dk
        dzf = df * (1.0 - lb) * (sf * (1.0 - sf))
        sum_ref[0:1, :] += _colsum(dzf)
        sum_ref[1:2, :] += _colsum(df * (1.0 - sf))
        if last:
            dz0 = (dq + pq_ref[...]) * (Q_SCALE * (sq * (1.0 + zq * (1.0 - sq))))
            dz3 = dv + pv_ref[...]
            sum_ref[2:3, :] += _colsum(dz0)
            sum_ref[3:4, :] += _colsum(dz3)
            dz2 = dzf.astype(BF)
            sh0_ref[:, 0:D] = dz0.astype(BF)
            sh0_ref[:, D:2 * D] = dz1_ref[...]
            sh0_ref[:, 2 * D:] = dz2[:, :CUT2]
            sh1_ref[:, 0:D - CUT2] = dz2[:, CUT2:]
            sh1_ref[:, D - CUT2:2 * D - CUT2] = dz3.astype(BF)
            sh1_ref[:, 2 * D - CUT2:] = dz4a_ref[...]
        else:
            o0_ref[...] = dq
            o1_ref[...] = dv
            o2_ref[...] = dzf.astype(BF)

        @pl.when(s == nrb - 1)
        def _():
            sum_ref[1:2, :] = sum_ref[1:2, :] * (lb * (1.0 - lb))

    def zspec(cb):
        return pl.BlockSpec((RB, D), lambda s: (rb(s), cb))

    rowspec = pl.BlockSpec((RB, D), lambda s: (rb(s), 0))
    in_specs = [zspec(0), zspec(1 + d), zspec(3), pl.BlockSpec((2, 2, D), lambda s: (0, 0, 0)),
                pl.BlockSpec((RB, D), lambda s: (jnp.minimum(rb(s), nlb - 1), 0)),
                pl.BlockSpec((NCK, H, DH, DH), lambda s: (rb(s), 0, 0, 0))]
    args = [z, z, z, lbl, do_lat, states]
    sumspec = pl.BlockSpec((8, D), lambda s: (0, 0))
    if last:
        in_specs += [rowspec, rowspec, rowspec, pl.BlockSpec((RB, CUT4), lambda s: (rb(s), 0))]
        args += list(prev)
        shspec = pl.BlockSpec((RB, SHC), lambda s: (rb(s), 0))
        out_specs = (shspec, shspec, sumspec)
        out_shape = (jax.ShapeDtypeStruct((m, SHC), BF), jax.ShapeDtypeStruct((m, SHC), BF))
    else:
        out_specs = (rowspec, rowspec, rowspec, sumspec)
        out_shape = (jax.ShapeDtypeStruct((m, D), F32), jax.ShapeDtypeStruct((m, D), F32),
                     jax.ShapeDtypeStruct((m, D), BF))
    return pl.pallas_call(
        body, grid=(nrb,), in_specs=in_specs, out_specs=out_specs,
        out_shape=out_shape + (jax.ShapeDtypeStruct((8, D), F32),),
        scratch_shapes=[pltpu.VMEM((H, DH, DH), F32)],
        name=f"gla_bwd{d}", compiler_params=_cp(("arbitrary",)))(*args)


def _shift_rows(v, k, r):
    row = lax.broadcasted_iota(jnp.int32, v.shape, 0)
    rolled = pltpu.roll(v, k % r, 0)
    return jnp.where((row >= k) & (row < r + k), rolled, 0.0)


def _conv_fwd(xb, cw_ref, cb_ref, r):
    return (cb_ref[...] + _shift_rows(xb, 1, r) * cw_ref[0:1, :] + xb * cw_ref[1:2, :]
            + _shift_rows(xb, -1, r) * cw_ref[2:3, :] + _shift_rows(xb, -2, r) * cw_ref[3:4, :])


def _window(ref, lo, n, rows):
    parts = []
    if lo < 0:
        parts.append(jnp.zeros((-lo,) + tuple(ref.shape[1:]), F32))
    parts.append(ref[max(lo, 0):min(lo + n, rows)])
    if lo + n > rows:
        parts.append(jnp.zeros((lo + n - rows,) + tuple(ref.shape[1:]), F32))
    return parts[0] if len(parts) == 1 else jnp.concatenate(parts, axis=0)


def _conv_cols(x_ref, cw_ref, cb_ref, r0, n, rows):
    w = _window(x_ref, r0 - 1, n + 3, rows)
    return (cb_ref[...] + w[0:n] * cw_ref[0:1, :] + w[1:n + 1] * cw_ref[1:2, :] + w[2:n + 2] * cw_ref[2:3, :]
            + w[3:n + 3] * cw_ref[3:4, :])


def _softplus_neg(lam):
    y = jnp.exp(-jnp.abs(lam))
    u = 1.0 + y
    tiny = u == 1.0
    l1p = jnp.where(tiny, y, jnp.log(u) * (y / jnp.where(tiny, 1.0, u - 1.0)))
    return jnp.maximum(-lam, 0.0) + l1p


def _gates(xc, wr_ref, br, wi_ref, bi, sp):
    xcb = xc.astype(BF)
    rs, is_ = [], []
    for g in range(H):
        gs = slice(g * DH, (g + 1) * DH)
        rs.append(jnp.dot(xcb[:, gs], wr_ref[g].astype(BF), preferred_element_type=F32))
        is_.append(jnp.dot(xcb[:, gs], wi_ref[g].astype(BF), preferred_element_type=F32))
    r = _sig_tail(jnp.concatenate(rs, axis=1) + br)
    i = _sig(jnp.concatenate(is_, axis=1) + bi)
    log_a = (-RG_C * r) * sp
    a = jnp.exp(log_a)
    t = jnp.tanh(log_a)
    om = (-2.0 * t) / (1.0 - t)
    return r, i, a, om


def _scan_rows(d, nrows, a_s, b_s, h_s, h0):
    nsl = nrows // 8

    def slab(j, h):
        jj = (nsl - 1 - j) if d else j
        r0 = pl.multiple_of(jj * 8, 8)
        for t in (range(7, -1, -1) if d else range(8)):
            h = a_s[pl.ds(r0 + t, 1), :] * h + b_s[pl.ds(r0 + t, 1), :]
            h_s[pl.ds(r0 + t, 1), :] = h
        return h

    return lax.fori_loop(0, nsl, slab, h0)


def _col_of(d, ncols):
    if d:
        return lambda s: ncols - jnp.maximum(s, 1)
    return lambda s: jnp.maximum(s, 1) - 1


def _rglru_fwd(z, cw, cb, wr, br, wi, bi, lam, d, t_lat, t_ctx):
    m = z.shape[0]
    rows = t_lat // GRID_W
    z3 = z.reshape(m // GRID_W, GRID_W, IN_COLS)
    nblk = GRID_W // CB
    blk = _col_of(d, nblk)
    cblk = t_lat // t_ctx
    rc = min(RCH, rows)

    def body(zc_ref, zx_ref, cw_ref, cb_ref, wr_ref, br_ref, wi_ref, bi_ref, lam_ref,
             hx_ref, hpx_ref, ax_ref, hpc_ref, a_s, b_s, h_s, hcar, a3, b3, cin_s):
        s = pl.program_id(0)
        sp = _softplus_neg(lam_ref[d:d + 1, :])
        br_ = br_ref[d:d + 1, :]
        bi_ = bi_ref[d:d + 1, :]

        @pl.when(s == 0)
        def _():
            xc = _conv_fwd(zc_ref[...], cw_ref, cb_ref, t_ctx)
            _, i, a, om = _gates(xc, wr_ref, br_, wi_ref, bi_, sp)
            a_s[...] = a
            b_s[...] = jnp.sqrt(om) * (i * xc)
            h0 = jnp.zeros((1, D), F32)
            hcar[0:1, :] = _scan_rows(d, t_ctx, a_s, b_s, h_s, h0)
            hs = h_s[...]
            row = lax.broadcasted_iota(jnp.int32, (t_ctx, D), 0)
            if d:
                hpc_ref[...] = jnp.where(row == t_ctx - 1, h0, pltpu.roll(hs, t_ctx - 1, 0))
            else:
                hpc_ref[...] = jnp.where(row == 0, h0, pltpu.roll(hs, 1, 0))

        @pl.when(s > 0)
        def _():
            for r0 in range(0, rows, rc):
                xc = _conv_cols(zx_ref, cw_ref, cb_ref, r0, rc, rows).reshape(rc * CB, D)
                _, i, a, om = _gates(xc, wr_ref, br_, wi_ref, bi_, sp)
                a3[r0:r0 + rc] = a.reshape(rc, CB, D)
                ax_ref[r0:r0 + rc] = a.reshape(rc, CB, D)
                b3[r0:r0 + rc] = (jnp.sqrt(om) * (i * xc)).reshape(rc, CB, D)

            def local(t, carry):
                hl, p = carry
                r = (rows - 1 - t) if d else t
                a = a3[r]
                hl = a * hl + b3[r]
                p = a * p
                b3[r] = hl
                a3[r] = p
                return hl, p

            hl, p = lax.fori_loop(0, rows, local, (jnp.zeros((CB, D), F32), jnp.ones((CB, D), F32)))
            cin = hcar[0:1, :]
            for j in (range(CB - 1, -1, -1) if d else range(CB)):
                cin_s[j:j + 1, :] = cin
                cin = hl[j:j + 1, :] + p[j:j + 1, :] * cin
            hcar[0:1, :] = cin
            c_in = cin_s[...]

            def fix(t, prev):
                r = (rows - 1 - t) if d else t
                h = b3[r] + a3[r] * c_in
                hx_ref[r] = h
                hpx_ref[r] = prev
                return h

            lax.fori_loop(0, rows, fix, c_in)

    full = lambda shp: pl.BlockSpec(shp, lambda s: (0,) * len(shp))
    colspec = pl.BlockSpec((rows, CB, D), lambda s: (0, blk(s), 0))
    outs = pl.pallas_call(
        body, grid=(nblk + 1,),
        in_specs=[pl.BlockSpec((t_ctx, D), lambda s: (cblk, 5)),
                  pl.BlockSpec((rows, CB, D), lambda s: (0, blk(s), 5)),
                  full((4, D)), full((1, D)),
                  pl.BlockSpec((None, H, DH, DH), lambda s: (d, 0, 0, 0)), full((2, D)),
                  pl.BlockSpec((None, H, DH, DH), lambda s: (d, 0, 0, 0)), full((2, D)), full((2, D))],
        out_specs=(colspec, colspec, colspec, full((t_ctx, D))),
        out_shape=(jax.ShapeDtypeStruct((rows, GRID_W, D), F32),) * 3 + (jax.ShapeDtypeStruct((t_ctx, D), F32),),
        scratch_shapes=[pltpu.VMEM((t_ctx, D), F32), pltpu.VMEM((t_ctx, D), F32), pltpu.VMEM((t_ctx, D), F32),
                        pltpu.VMEM((8, D), F32), pltpu.VMEM((rows, CB, D), F32), pltpu.VMEM((rows, CB, D), F32),
                        pltpu.VMEM((CB, D), F32)],
        name=f"rglru_fwd{d}", compiler_params=_cp(("arbitrary",)))(z, z3, cw, cb, wr, br, wi, bi, lam)
    return outs[0].reshape(t_lat, D), outs[1].reshape(t_lat, D), outs[2].reshape(t_lat, D), outs[3]


def _rglru_bwd(z, cw, cb, wr, br, wi, bi, lam, dh_lat, hp_lat, a_lat, hp_ctx, d, t_lat, t_ctx, prev=None):
    m = z.shape[0]
    rows = t_lat // GRID_W
    z3 = z.reshape(m // GRID_W, GRID_W, IN_COLS)
    nblk = GRID_W // CB
    fblk = _col_of(d, nblk)
    blk = lambda s: fblk(nblk - jnp.minimum(s, nblk - 1))
    cblk = t_lat // t_ctx
    rc = min(RCH, rows)
    last = prev is not None
    view3 = lambda v: v.reshape(rows, GRID_W, D)

    def body(*refs):
        (zc_ref, zx_ref, cw_ref, cb_ref, wr_ref, br_ref, wi_ref, bi_ref, lam_ref, dh_ref, hpx_ref, ax_ref,
         hpc_ref) = refs[:13]
        k = 13
        if last:
            pdx_ref, pdc_ref = refs[13:15]
            k = 15
        ox_ref, oc_ref, dwr_ref, dwi_ref, sum_ref, a_s, b_s, gcar, a3, b3, cin_s = refs[k:]
        s = pl.program_id(0)
        lam_d = lam_ref[d:d + 1, :]
        sp = _softplus_neg(lam_d)
        br_ = br_ref[d:d + 1, :]
        bi_ = bi_ref[d:d + 1, :]
        flat = lambda v: v.reshape(-1, D)

        @pl.when(s == 0)
        def _():
            gcar[...] = jnp.zeros_like(gcar)
            dwr_ref[...] = jnp.zeros_like(dwr_ref)
            dwi_ref[...] = jnp.zeros_like(dwi_ref)
            sum_ref[...] = jnp.zeros_like(sum_ref)


        def conv_sums(dxc, dxb, xm1, x0, xp1, xp2):
            sum_ref[3:4, :] += _colsum(flat(dxc))
            sum_ref[4:5, :] += _colsum(flat(dxb))
            sum_ref[8:9, :] += _colsum(flat(dxc * xm1))
            sum_ref[9:10, :] += _colsum(flat(dxc * x0))
            sum_ref[10:11, :] += _colsum(flat(dxc * xp1))
            sum_ref[11:12, :] += _colsum(flat(dxc * xp2))

        def gate_grads(g, hp, xc, rr, i, a, om):
            mult = jnp.sqrt(om)
            da = g * hp
            ixc = i * xc
            dmult = g * ixc
            dixc = g * mult
            di = dixc * xc
            dxc = dixc * i
            dlog_a = da * a - dmult * ((1.0 - om) / mult)
            dr = dlog_a * (-RG_C * sp)
            sum_ref[2:3, :] += _colsum(dlog_a * rr)
            drp = dr * rr * (1.0 - rr)
            dip = di * i * (1.0 - i)
            sum_ref[0:1, :] += _colsum(drp)
            sum_ref[1:2, :] += _colsum(dip)
            xcb = xc.astype(BF)
            drb = drp.astype(BF)
            dib = dip.astype(BF)
            parts = []
            for gi in range(H):
                gs = slice(gi * DH, (gi + 1) * DH)
                parts.append(_dot_nt(drb[:, gs], wr_ref[gi]) + _dot_nt(dib[:, gs], wi_ref[gi]))
                dwr_ref[gi] += _dot_tn(xcb[:, gs], drb[:, gs])
                dwi_ref[gi] += _dot_tn(xcb[:, gs], dib[:, gs])
            return dxc + jnp.concatenate(parts, axis=1)

        @pl.when(s < nblk)
        def _():
            def local(t, carry):
                c, q = carry
                r = t if d else (rows - 1 - t)
                a = ax_ref[r]
                c = a * (c + dh_ref[r])
                q = a * q
                b3[r] = c
                a3[r] = q
                return c, q

            c, q = lax.fori_loop(0, rows, local, (jnp.zeros((CB, D), F32), jnp.ones((CB, D), F32)))
            cin = gcar[0:1, :]
            for j in (range(CB) if d else range(CB - 1, -1, -1)):
                cin_s[j:j + 1, :] = cin
                cin = c[j:j + 1, :] + q[j:j + 1, :] * cin
            gcar[0:1, :] = cin
            c_in = cin_s[...]
            for r0 in (range(rows - rc, -1, -rc) if d else range(0, rows, rc)):
                if d:
                    lo = max(r0 - 1, 0)
                    cn = b3[lo:r0 + rc - 1] + a3[lo:r0 + rc - 1] * c_in
                    if r0 == 0:
                        cn = jnp.concatenate([c_in[None], cn], axis=0)
                else:
                    hi = min(r0 + rc + 1, rows)
                    cn = b3[r0 + 1:hi] + a3[r0 + 1:hi] * c_in
                    if hi == rows:
                        cn = jnp.concatenate([cn, c_in[None]], axis=0)
                g = flat(dh_ref[r0:r0 + rc] + cn)
                xc = flat(_conv_cols(zx_ref, cw_ref, cb_ref, r0, rc, rows))
                rr, i, a, om = _gates(xc, wr_ref, br_, wi_ref, bi_, sp)
                b3[r0:r0 + rc] = gate_grads(g, flat(hpx_ref[r0:r0 + rc]), xc, rr, i, a, om).reshape(rc, CB, D)
            if last:
                for r0 in range(0, rows, rc):
                    b3[r0:r0 + rc] = b3[r0:r0 + rc] + pdx_ref[r0:r0 + rc]
                for r0 in range(0, rows, rc):
                    w = _window(b3, r0 - 2, rc + 3, rows)
                    xw = _window(zx_ref, r0 - 1, rc + 3, rows)
                    dxc = w[2:rc + 2]
                    dxb = (w[3:rc + 3] * cw_ref[0:1, :] + dxc * cw_ref[1:2, :] + w[1:rc + 1] * cw_ref[2:3, :]
                           + w[0:rc] * cw_ref[3:4, :])
                    conv_sums(dxc, dxb, xw[0:rc], xw[1:rc + 1], xw[2:rc + 2], xw[3:rc + 3])
                    ox_ref[r0:r0 + rc] = dxb
            else:
                for r0 in range(0, rows, rc):
                    ox_ref[r0:r0 + rc] = b3[r0:r0 + rc]

        @pl.when(s == nblk)
        def _():
            r = t_ctx
            xb = zc_ref[...]
            xc = _conv_fwd(xb, cw_ref, cb_ref, r)
            rr, i, a, om = _gates(xc, wr_ref, br_, wi_ref, bi_, sp)
            a_s[...] = a
            b_s[...] = jnp.zeros((r, D), F32)
            c0 = gcar[0:1, :]
            _scan_rows(1 - d, r, a_s, b_s, b_s, c0)
            cs = b_s[...]
            row = lax.broadcasted_iota(jnp.int32, (r, D), 0)
            if d:
                g = jnp.where(row == 0, c0, pltpu.roll(cs, 1, 0))
            else:
                g = jnp.where(row == r - 1, c0, pltpu.roll(cs, r - 1, 0))
            dxc = gate_grads(g, hpc_ref[...], xc, rr, i, a, om)
            if last:
                dxc = dxc + pdc_ref[...]
                dxb = (_shift_rows(dxc, -1, r) * cw_ref[0:1, :] + dxc * cw_ref[1:2, :]
                       + _shift_rows(dxc, 1, r) * cw_ref[2:3, :] + _shift_rows(dxc, 2, r) * cw_ref[3:4, :])
                conv_sums(dxc, dxb, _shift_rows(xb, 1, r), xb, _shift_rows(xb, -1, r), _shift_rows(xb, -2, r))
                oc_ref[...] = dxb
            else:
                oc_ref[...] = dxc
            sum_ref[2:3, :] = sum_ref[2:3, :] * (RG_C * _sig_tail(-lam_d))

    full = lambda shp: pl.BlockSpec(shp, lambda s: (0,) * len(shp))
    once = lambda shp: pl.BlockSpec(shp, lambda s: (0,) * len(shp), pipeline_mode=pl.Buffered(1))
    colspec = pl.BlockSpec((rows, CB, D), lambda s: (0, blk(s), 0))
    colonce = pl.BlockSpec((rows, CB, D), lambda s: (0, blk(s), 0), pipeline_mode=pl.Buffered(1))
    in_specs = [pl.BlockSpec((t_ctx, D), lambda s: (cblk, 5), pipeline_mode=pl.Buffered(1)),
                pl.BlockSpec((rows, CB, D), lambda s: (0, blk(s), 5), pipeline_mode=pl.Buffered(1)),
                full((4, D)), full((1, D)),
                pl.BlockSpec((None, H, DH, DH), lambda s: (d, 0, 0, 0)), full((2, D)),
                pl.BlockSpec((None, H, DH, DH), lambda s: (d, 0, 0, 0)), full((2, D)), full((2, D)),
                colspec, colspec, colonce, once((t_ctx, D))]
    args = [z, z3, cw, cb, wr, br, wi, bi, lam, view3(dh_lat), view3(hp_lat), view3(a_lat), hp_ctx]
    if last:
        in_specs += [colonce, once((t_ctx, D))]
        args += [view3(prev[0]), prev[1]]
    outs = pl.pallas_call(
        body, grid=(nblk + 1,), in_specs=in_specs,
        out_specs=(colspec, full((t_ctx, D)), full((H, DH, DH)), full((H, DH, DH)), full((16, D))),
        out_shape=(jax.ShapeDtypeStruct((rows, GRID_W, D), F32), jax.ShapeDtypeStruct((t_ctx, D), F32),
                   jax.ShapeDtypeStruct((H, DH, DH), F32), jax.ShapeDtypeStruct((H, DH, DH), F32),
                   jax.ShapeDtypeStruct((16, D), F32)),
        scratch_shapes=[pltpu.VMEM((t_ctx, D), F32), pltpu.VMEM((t_ctx, D), F32), pltpu.VMEM((8, D), F32),
                        pltpu.VMEM((rows, CB, D), F32), pltpu.VMEM((rows, CB, D), F32), pltpu.VMEM((CB, D), F32)],
        name=f"rglru_bwd{d}", compiler_params=_cp(("arbitrary",), VMEM_LIMIT_RGLRU_BWD))(*args)
    return (outs[0].reshape(t_lat, D), outs[1]) + tuple(outs[2:])


def _merge(o_f, o_b, h_f, h_b, z, x, tgt, mod, norm_g, ln_g, ln_b, p_a, p_b, w_out, t_lat):
    tm = 256
    nt = t_lat // tm

    def body(of_ref, ob_ref, hf_ref, hb_ref, z4_ref, z6_ref, z7_ref, z8_ref, x_ref, t_ref, mod_ref, ng_ref,
             lg_ref, lb_ref, pa_ref, pb_ref, wo_ref,
             do_ref, dh_ref, dz4a_ref, dz4b_ref, dz6a_ref, sh3_ref, gx_ref,
             y_ref, dout_ref, oa_ref, dpa_ref, obv_ref, dpb_ref, acc_ref):
        i = pl.program_id(0)
        lat = i < nt
        latf = lat.astype(F32)

        @pl.when(i == 0)
        def _():
            acc_ref[...] = jnp.zeros_like(acc_ref)

        def per_head(v):
            return jnp.concatenate(
                [jnp.broadcast_to(jnp.mean(v[:, h * DH:(h + 1) * DH], axis=-1, keepdims=True), (tm, DH))
                 for h in range(H)], axis=1)

        gt = mod_ref[0:1, 2 * D:3 * D]
        gfull = jnp.concatenate([ng_ref[...]] * H, axis=1)
        o = of_ref[...] + ob_ref[...]
        rinv = lax.rsqrt(per_head(o * o) + RMS_EPS)
        n = o * rinv
        na = n * gfull
        z4 = z4_ref[...]
        s4 = _sig(z4)
        silu4 = z4 * s4
        oa = na * silu4
        z6 = z6_ref[...]
        s6 = _sig(z6)
        silu6 = z6 * s6
        hsum = hf_ref[...] + hb_ref[...]
        obv = hsum * silu6
        pa = _dot(oa, pa_ref[...])
        pb = _dot(obv, pb_ref[...])
        s7 = _sig(z7_ref[...])
        s8 = _sig(z8_ref[...])
        y = s7 * pa + s8 * pb
        out = _dot(y, wo_ref[...])
        pre = ALPHA * x_ref[...] + gt * out
        mu = jnp.mean(pre, axis=-1, keepdims=True)
        xc = pre - mu
        rstd = lax.rsqrt(jnp.mean(xc * xc, axis=-1, keepdims=True) + LN_EPS)
        xhat = xc * rstd
        lg = lg_ref[...]
        diff = xhat * lg + lb_ref[...] - t_ref[...]
        acc_ref[8:9, :] += _colsum(diff * diff) * (0.5 / D * latf)
        dxn = diff * (1.0 / D)
        acc_ref[1:2, :] += _colsum(dxn * xhat) * latf
        acc_ref[2:3, :] += _colsum(dxn) * latf
        dxhat = dxn * lg
        dpre = rstd * (dxhat - jnp.mean(dxhat, axis=-1, keepdims=True)
                       - xhat * jnp.mean(dxhat * xhat, axis=-1, keepdims=True))
        gx_ref[...] = ALPHA * dpre
        acc_ref[0:1, :] += _colsum(dpre * out) * latf
        dout = dpre * gt
        dy = _dot_nt(dout, wo_ref[...])
        dpa = dy * s7
        dpb = dy * s8
        dz7 = dy * pa * (s7 * (1.0 - s7))
        dz8 = dy * pb * (s8 * (1.0 - s8))
        doa = _dot_nt(dpa, pa_ref[...])
        dob = _dot_nt(dpb, pb_ref[...])
        dh_ref[...] = dob * silu6
        dz6 = dob * hsum * (s6 * (1.0 + z6 * (1.0 - s6)))
        dna = doa * silu4
        dz4 = doa * na * (s4 * (1.0 + z4 * (1.0 - s4)))
        dng = _colsum(dna * n)
        acc_ref[7:8, 0:DH] += sum(dng[:, h * DH:(h + 1) * DH] for h in range(H)) * latf
        dn = dna * gfull
        do_ref[...] = rinv * (dn - n * per_head(dn * n))
        acc_ref[3:4, :] += _colsum(dz4) * latf
        acc_ref[4:5, :] += _colsum(dz6) * latf
        acc_ref[5:6, :] += _colsum(dz7) * latf
        acc_ref[6:7, :] += _colsum(dz8) * latf
        dz4b, dz6b = (dz4 * latf).astype(BF), (dz6 * latf).astype(BF)
        dz4a_ref[...] = dz4b[:, :CUT4]
        dz4b_ref[...] = dz4b[:, CUT4:]
        dz6a_ref[...] = dz6b[:, :CUT6]
        sh3_ref[:, 0:D - CUT6] = dz6b[:, CUT6:]
        sh3_ref[:, D - CUT6:2 * D - CUT6] = (dz7 * latf).astype(BF)
        sh3_ref[:, 2 * D - CUT6:] = (dz8 * latf).astype(BF)
        y_ref[...] = y.astype(BF)
        dout_ref[...] = dout.astype(BF)
        oa_ref[...] = oa.astype(BF)
        dpa_ref[...] = dpa.astype(BF)
        obv_ref[...] = obv.astype(BF)
        dpb_ref[...] = dpb.astype(BF)

        @pl.when(i == nt - 1)
        def _():
            acc_ref[9:10, :] = jnp.broadcast_to(jnp.sum(acc_ref[8:9, :], axis=-1, keepdims=True), (1, D))

    m = z.shape[0]
    lrow = lambda i: jnp.minimum(i, nt - 1)
    row = pl.BlockSpec((tm, D), lambda i: (lrow(i), 0))
    allrow = lambda cols: pl.BlockSpec((tm, cols), lambda i: (i, 0))
    zs = lambda cb: pl.BlockSpec((tm, D), lambda i: (lrow(i), cb))
    full = lambda shp: pl.BlockSpec(shp, lambda i: (0,) * len(shp))
    wfull = pl.BlockSpec((D, D), lambda i: (0, 0), pipeline_mode=pl.Buffered(1))
    f32o = jax.ShapeDtypeStruct((t_lat, D), F32)
    bfo = jax.ShapeDtypeStruct((t_lat, D), BF)
    bfall = lambda cols: jax.ShapeDtypeStruct((m, cols), BF)
    return pl.pallas_call(
        body, grid=(m // tm,),
        in_specs=[row, row, row, row, zs(4), zs(6), zs(7), zs(8), row, row, full((16, 3 * D)), full((1, DH)),
                  full((1, D)), full((1, D)), wfull, wfull, wfull],
        out_specs=(row, row) + tuple(allrow(c) for c in (CUT4, D - CUT4, CUT6, SHC)) + (row,) * 7 + (full((16, D)),),
        out_shape=(f32o, f32o, bfall(CUT4), bfall(D - CUT4), bfall(CUT6), bfall(SHC), f32o, bfo, bfo, bfo, bfo, bfo, bfo,
                   jax.ShapeDtypeStruct((16, D), F32)),
        name="merge", compiler_params=_cp(("arbitrary",), VMEM_LIMIT_MERGE))(
            o_f, o_b, h_f, h_b, z, z, z, z, x, tgt, mod, norm_g, ln_g, ln_b, p_a, p_b, w_out)


def _wgrad(a, b, name):
    tm = 1024

    def body(a_ref, b_ref, o_ref):
        @pl.when(pl.program_id(0) == 0)
        def _():
            o_ref[...] = jnp.zeros_like(o_ref)
        o_ref[...] += _dot_tn(a_ref[...], b_ref[...])

    row = pl.BlockSpec((tm, D), lambda i: (i, 0))
    return pl.pallas_call(body, grid=(a.shape[0] // tm,), in_specs=[row, row],
                          out_specs=pl.BlockSpec((D, D), lambda i: (0, 0)),
                          out_shape=jax.ShapeDtypeStruct((D, D), F32), name=name,
                          compiler_params=_cp(("arbitrary",)))(a, b)


def _pack_shard2(dz4b, dz5_lat, dz5_ctx, dz6a):
    m, t_lat, t_ctx = dz4b.shape[0], dz5_lat.shape[0], dz5_ctx.shape[0]
    tm = t_ctx
    nlt = t_lat // tm
    w4 = D - CUT4

    def body(a_ref, bl_ref, bc_ref, c_ref, o_ref):
        i = pl.program_id(0)
        o_ref[:, 0:w4] = a_ref[...]
        o_ref[:, w4:w4 + D] = jnp.where(i < nlt, bl_ref[...], bc_ref[...]).astype(BF)
        o_ref[:, w4 + D:] = c_ref[...]

    return pl.pallas_call(
        body, grid=(m // tm,),
        in_specs=[pl.BlockSpec((tm, w4), lambda i: (i, 0)),
                  pl.BlockSpec((tm, D), lambda i: (jnp.minimum(i, nlt - 1), 0)),
                  pl.BlockSpec((tm, D), lambda i: (0, 0)),
                  pl.BlockSpec((tm, CUT6), lambda i: (i, 0))],
        out_specs=pl.BlockSpec((tm, SHC), lambda i: (i, 0)),
        out_shape=jax.ShapeDtypeStruct((m, SHC), BF), name="pack_shard2",
        compiler_params=_cp(("arbitrary",)))(dz4b, dz5_lat, dz5_ctx, dz6a)


def _wgrad_in(u_all, dz_shards, others):
    m = u_all.shape[1]
    assert m % (6 * 128) == 0
    tm = m // 6
    nt = m // tm
    no = len(others)
    spec_u, spec_dz = pl.BlockSpec((D, tm), lambda i: (0, i)), pl.BlockSpec((tm, SHC), lambda i: (i, 0))
    oshape = jax.ShapeDtypeStruct((NSH, D, SHC), F32)
    out = None
    for k, dz_k in enumerate(dz_shards[:-1]):
        def body(u_ref, dz_ref, *rest):
            o_ref = rest[-1]

            @pl.when(pl.program_id(0) == 0)
            def _():
                o_ref[...] = jnp.zeros_like(o_ref)
            o_ref[0] += _dot(u_ref[...], dz_ref[...])

        out = pl.pallas_call(
            body, grid=(nt,), in_specs=[spec_u, spec_dz] + ([] if out is None else [_ANY]),
            out_specs=pl.BlockSpec((1, D, SHC), lambda i, k=k: (k, 0, 0)), out_shape=oshape,
            input_output_aliases={} if out is None else {2: 0},
            name=f"wgrad_in{k}", compiler_params=_cp(("arbitrary",)))(u_all, dz_k, *(() if out is None else (out,)))

    def last(u_ref, dz_ref, done_ref, *rest):
        other_refs, o_ref, got_refs = rest[:no], rest[no], rest[no + 1:2 * no + 2]
        send, recv = rest[2 * no + 2:]
        i = pl.program_id(0)
        give = ([done_ref] + list(other_refs), got_refs, send, recv, [tuple(range(NSH - 1))] + [tuple(range(NSH))] * no)

        @pl.when(i == 0)
        def _():
            o_ref[...] = jnp.zeros_like(o_ref)
            for cp in _rs_sibling_copies(*give):
                cp.start()
        o_ref[0] += _dot(u_ref[...], dz_ref[...])

        @pl.when(i == nt - 1)
        def _():
            for cp in _rs_sibling_copies(*give):
                cp.wait()

    half = lambda g: jax.ShapeDtypeStruct((NSH, g.shape[1] // 2, g.shape[2]), F32)
    outs = pl.pallas_call(
        last, grid=(nt,), in_specs=[spec_u, spec_dz, _ANY] + [_ANY] * no,
        out_specs=[pl.BlockSpec((1, D, SHC), lambda i: (NSH - 1, 0, 0))] + [_ANY] * (no + 1),
        out_shape=[oshape, half(oshape)] + [half(g) for g in others], input_output_aliases={2: 0},
        scratch_shapes=[pltpu.SemaphoreType.DMA((NSH * (no + 1),)), pltpu.SemaphoreType.DMA((NSH * (no + 1),))],
        name=f"wgrad_in{NSH - 1}", compiler_params=_cp(("arbitrary",)))(u_all, dz_shards[-1], out, *others)
    return outs[0], list(outs[1:])


def _du(dz_shards, w_in_g, x, ctx, mod, gxres, sums=()):
    tm = 256
    n_lat_tiles, nct = x.shape[0] // tm, ctx.shape[0] // tm
    nt = n_lat_tiles + nct
    ns = len(sums)
    rblk = lambda i: jnp.where(i < nct, n_lat_tiles + i, i - nct)
    lblk = lambda i: jnp.maximum(i - nct, 0)

    def body(*refs):
        dz_refs, refs = refs[:NSH], refs[NSH:]
        w_ref, x_ref, c_ref, mod_ref, gr_ref = refs[:5]
        sum_refs = refs[5:5 + ns]
        gx_ref, dm_ref = refs[5 + ns:7 + ns]
        got_refs = refs[7 + ns:7 + 2 * ns]
        sems = refs[7 + 2 * ns:]
        i = pl.program_id(0)
        is_lat = i >= nct

        @pl.when(i == 0)
        def _():
            dm_ref[...] = jnp.zeros_like(dm_ref)
            if ns:
                for cp in _rs_chip_copies(sum_refs, got_refs, *sems):
                    cp.start()

        du = _dot_nt(dz_refs[0][...], w_ref[0])
        for n in range(1, NSH):
            du = du + _dot_nt(dz_refs[n][...], w_ref[n])
        sc = jnp.where(is_lat, mod_ref[0:1, D:2 * D], mod_ref[1:2, D:2 * D])
        dsh = _colsum(du)
        dsc = _colsum(du * jnp.where(is_lat, x_ref[...], c_ref[...]))

        @pl.when(is_lat)
        def _():
            gx_ref[...] = du * (1.0 + sc) + gr_ref[...]
            dm_ref[0:1, 0:D] += dsh
            dm_ref[0:1, D:2 * D] += dsc

        @pl.when(jnp.logical_not(is_lat))
        def _():
            dm_ref[1:2, 0:D] += dsh
            dm_ref[1:2, D:2 * D] += dsc

        if ns:
            @pl.when(i == nt - 1)
            def _():
                for cp in _rs_chip_copies(sum_refs, got_refs, *sems):
                    cp.wait()

    outs = pl.pallas_call(
        body, grid=(nt,),
        in_specs=[pl.BlockSpec((tm, SHC), lambda i: (rblk(i), 0))] * NSH + [
                  pl.BlockSpec((NSH, D, SHC), lambda i: (0, 0, 0), pipeline_mode=pl.Buffered(1)),
                  pl.BlockSpec((tm, D), lambda i: (lblk(i), 0)),
                  pl.BlockSpec((tm, D), lambda i: (jnp.minimum(i, nct - 1), 0)),
                  pl.BlockSpec((16, 3 * D), lambda i: (0, 0)),
                  pl.BlockSpec((tm, D), lambda i: (lblk(i), 0))] + [_ANY] * ns,
        out_specs=[pl.BlockSpec((tm, D), lambda i: (lblk(i), 0)),
                   pl.BlockSpec((8, 2 * D), lambda i: (0, 0))] + [_ANY] * ns,
        out_shape=[jax.ShapeDtypeStruct((n_lat_tiles * tm, D), F32), jax.ShapeDtypeStruct((8, 2 * D), F32)]
        + [jax.ShapeDtypeStruct((3,) + g.shape[1:], g.dtype) for g in sums],
        scratch_shapes=[pltpu.SemaphoreType.DMA((3 * ns,)), pltpu.SemaphoreType.DMA((3 * ns,))] if ns else [],
        name="du", compiler_params=_cp(("arbitrary",)))(*dz_shards, w_in_g, x, ctx, mod, gxres, *sums)
    return outs[0], outs[1], list(outs[2:])


def _row_tile(rows, cols, block_bytes=1 << 20):
    t = 8
    while t * 2 * cols * 4 <= block_bytes and rows % (t * 2) == 0:
        t *= 2
    return t


def _adamw_update(w_ref, g_ref, m_ref, v_ref, d_ref, nm_ref, nv_ref):
    gg = g_ref[...]
    m2 = ADAM_B1 * m_ref[...] + (1.0 - ADAM_B1) * gg
    v2 = ADAM_B2 * v_ref[...] + (1.0 - ADAM_B2) * (gg * gg)
    m_hat = m2 / (1.0 - ADAM_B1 ** ADAM_STEP)
    v_hat = v2 / (1.0 - ADAM_B2 ** ADAM_STEP)
    d_ref[...] = -ADAM_LR * (m_hat / (jnp.sqrt(v_hat) + ADAM_EPS) + ADAM_WD * w_ref[...])
    nm_ref[...] = m2
    nv_ref[...] = v2


def _adamw_many(ws, gs, ms, vs):
    n = len(ws)

    def body(*refs):
        for j in range(n):
            _adamw_update(*refs[4 * j:4 * j + 4], *refs[4 * n + 3 * j:4 * n + 3 * j + 3])

    args = [a for quad in zip(ws, gs, ms, vs) for a in quad]
    outs = pl.pallas_call(body, out_shape=[jax.ShapeDtypeStruct(w.shape, F32) for w in ws for _ in range(3)],
                          name="adamw_small", compiler_params=_cp())(*args)
    return outs[0::3], outs[1::3], outs[2::3]


def _adamw(ws, gs, ms, vs, name):
    n = len(ws)
    rows, cols = ws[0].shape
    tr = _row_tile(rows, cols, (2 << 20) // n)

    def body(*refs):
        for j in range(n):
            _adamw_update(*refs[4 * j:4 * j + 4], *refs[4 * n + 3 * j:4 * n + 3 * j + 3])

    spec = pl.BlockSpec((tr, cols), lambda i: (i, 0))
    o = jax.ShapeDtypeStruct((rows, cols), F32)
    args = [a for quad in zip(ws, gs, ms, vs) for a in quad]
    outs = pl.pallas_call(body, grid=(rows // tr,), in_specs=[spec] * (4 * n), out_specs=[spec] * (3 * n),
                          out_shape=[o] * (3 * n), name=name, compiler_params=_cp(("arbitrary",)))(*args)
    return outs[0::3], outs[1::3], outs[2::3]


_ANY = pl.BlockSpec(memory_space=pl.ANY)


def _place():
    return lax.axis_index("x"), lax.axis_index("y"), lax.axis_index("c")


def _ag_copies(ins, outs, send, recv, fsend, frecv, lsem):
    x, y, c = _place()
    me = 2 * x + y
    chips = ((1 - x, y), (x, 1 - y), (1 - x, 1 - y))
    local, chip, hand = [], [], []
    for j in range(len(ins)):
        hr = ins[j].shape[0] // 2
        half = pl.ds(pl.multiple_of(c * hr, 8), hr)
        local.append(pltpu.make_async_copy(ins[j], outs[j].at[me], lsem.at[j]))
        for k, (px, py) in enumerate(chips):
            chip.append(pltpu.make_async_remote_copy(
                src_ref=ins[j].at[half, :], dst_ref=outs[j].at[me, half, :], send_sem=send.at[3 * j + k],
                recv_sem=recv.at[3 * j + k], device_id=(px, py, c), device_id_type=MESH))
            got = outs[j].at[2 * px + py, half, :]
            hand.append(pltpu.make_async_remote_copy(
                src_ref=got, dst_ref=got, send_sem=fsend.at[3 * j + k], recv_sem=frecv.at[3 * j + k],
                device_id=(x, y, 1 - c), device_id_type=MESH))
    return local, chip, hand


def _ag_sems(n):
    return [pltpu.SemaphoreType.DMA((3 * n,))] * 4 + [pltpu.SemaphoreType.DMA((n,))]


def _ag_finish(local, chip, hand, done=0):
    for k in range(done, len(chip)):
        chip[k].wait_recv()
        hand[k].start()
    for cp in chip:
        cp.wait_send()
    for k in range(done):
        hand[k].wait_send()
    for k in range(done, len(chip)):
        hand[k].wait_send()
        hand[k].wait_recv()
    for cp in local:
        cp.wait()


def _mod_tp(c8, c_ctx, w_mod_sh, b_mod_sh):
    mc = w_mod_sh.shape[1]

    def body(c8_ref, cctx_ref, w_ref, b_ref, mod_ref, cc_ref, cc_s, part_s, send1, recv1, send3, recv3):
        x, y, c = _place()
        me = 4 * x + 2 * y + c
        ms = 2 * x + y
        copies = []
        for k in range(1, 8):
            peer = (x ^ ((k >> 2) & 1), y ^ ((k >> 1) & 1), c ^ (k & 1))
            cp = pltpu.make_async_remote_copy(src_ref=c8_ref, dst_ref=cc_s.at[me], send_sem=send1.at[k],
                                              recv_sem=recv1.at[k], device_id=peer, device_id_type=MESH)
            cp.start()
            copies.append(cp)
        cc_s[me] = c8_ref[...]
        for cp in copies:
            cp.wait()
        cc_ref[...] = jnp.zeros_like(cc_ref)
        for j in range(8):
            cc_ref[j:j + 1, :] = cc_s[j, 0:1, :]
        cc_ref[8:9, :] = cctx_ref[...]
        v = cc_ref[...]
        part_s[ms] = _dot(v * _sig(v), w_ref[...]) + b_ref[...]
        copies = []
        for k in range(1, 4):
            peer = (x ^ ((k >> 1) & 1), y ^ (k & 1), c)
            cp = pltpu.make_async_remote_copy(src_ref=part_s.at[ms], dst_ref=part_s.at[ms], send_sem=send3.at[k],
                                              recv_sem=recv3.at[k], device_id=peer, device_id_type=MESH)
            cp.start()
            copies.append(cp)
        for cp in copies:
            cp.wait()
        for s in range(NSH):
            mod_ref[:, s * mc:(s + 1) * mc] = part_s[s]

    vm = pl.BlockSpec(memory_space=pltpu.VMEM)
    return pl.pallas_call(
        body, in_specs=[vm] * 4, out_specs=(vm, vm),
        out_shape=(jax.ShapeDtypeStruct((16, NSH * mc), F32), jax.ShapeDtypeStruct((16, D), F32)),
        scratch_shapes=[pltpu.VMEM((8, 8, D), F32), pltpu.VMEM((NSH, 16, mc), F32),
                        pltpu.SemaphoreType.DMA((8,)), pltpu.SemaphoreType.DMA((8,)),
                        pltpu.SemaphoreType.DMA((4,)), pltpu.SemaphoreType.DMA((4,))],
        name="mod_tp", compiler_params=_cp())(c8, c_ctx, w_mod_sh, b_mod_sh)


def _inproj_ag(x, ctx, mod, w_in_sh, b_in, narrow_sh):
    n_lat = x.shape[0]
    m = n_lat + ctx.shape[0]
    assert m % (11 * 128) == 0
    tm = m // 11
    nt = m // tm
    nfull = n_lat // tm
    assert nfull == nt - 1
    tail = jnp.concatenate([x[nfull * tm:], ctx], axis=0)
    x_, y_ = lax.axis_index("x"), lax.axis_index("y")
    sids = jnp.stack([2 * x_ + y_, 2 * (1 - x_) + y_, 2 * x_ + 1 - y_, 2 * (1 - x_) + 1 - y_]).astype(jnp.int32)

    def body(sid_ref, x_ref, tail_ref, mod_ref, b_ref, wsh_ref, nsh_ref, z_ref, u_ref, wg_ref, ng_ref, w_s, u_s, *sems):
        n = pl.program_id(0)
        i = pl.program_id(1)
        rows = pl.ds(pl.multiple_of(i * tm, tm), tm)
        ag = ((wsh_ref, nsh_ref), (wg_ref, ng_ref)) + tuple(sems[:5])
        wsem = sems[5]

        def load(src):
            cp = pltpu.make_async_copy(src, w_s, wsem)
            cp.start()
            cp.wait()

        @pl.when((n == 0) & (i == 0))
        def _():
            local, chip, _ = _ag_copies(*ag)
            for cp in chip + local:
                cp.start()
            load(wsh_ref)

        for k in range(NSH - 1):
            @pl.when((n == k + 1) & (i == 0))
            def _():
                _, chip, hand = _ag_copies(*ag)
                chip[k].wait_recv()
                hand[k].start()
                hand[k].wait_recv()
                load(wg_ref.at[sid_ref[k + 1]])

        @pl.when(n == 0)
        def _():
            is_lat = (i * tm + lax.broadcasted_iota(jnp.int32, (tm, 1), 0)) < n_lat
            u = _modulate(jnp.where(i < nfull, x_ref[...], tail_ref[...]), mod_ref, is_lat)
            u_s[rows, :] = u.astype(BF)
            u_ref[...] = u.T.astype(BF)

        z_ref[...] = _dot(u_s[rows, :], w_s[...]) + b_ref[...]

        @pl.when((n == NSH - 1) & (i == nt - 1))
        def _():
            _ag_finish(*_ag_copies(*ag), done=NSH - 1)

    first = lambda n, i: jnp.where(n == 0, i, nt - 1)
    outs = pl.pallas_call(
        body, grid_spec=pltpu.PrefetchScalarGridSpec(
            num_scalar_prefetch=1, grid=(NSH, nt),
            in_specs=[pl.BlockSpec((tm, D), lambda n, i, sid: (jnp.minimum(first(n, i), nfull - 1), 0)),
                      pl.BlockSpec((tm, D), lambda n, i, sid: (0, 0)),
                      pl.BlockSpec((16, 3 * D), lambda n, i, sid: (0, 0)),
                      pl.BlockSpec((1, SHC), lambda n, i, sid: (0, sid[n])), _ANY, _ANY],
            out_specs=[pl.BlockSpec((tm, SHC), lambda n, i, sid: (i, sid[n])),
                       pl.BlockSpec((D, tm), lambda n, i, sid: (0, first(n, i))), _ANY, _ANY],
            scratch_shapes=[pltpu.VMEM((D, SHC), BF), pltpu.VMEM((m, D), BF)] + _ag_sems(2)
            + [pltpu.SemaphoreType.DMA]),
        out_shape=[jax.ShapeDtypeStruct((m, IN_COLS), F32), jax.ShapeDtypeStruct((D, m), BF),
                   jax.ShapeDtypeStruct((NSH,) + w_in_sh.shape, BF),
                   jax.ShapeDtypeStruct((NSH,) + narrow_sh.shape, narrow_sh.dtype)],
        name="inproj_ag", compiler_params=_cp(("arbitrary", "arbitrary")))(sids, x, tail, mod, b_in, w_in_sh, narrow_sh)
    return outs


def _rs_sibling_copies(ins, got, send, recv, shards):
    x, y, c = _place()
    copies = []
    for j in range(len(ins)):
        hr = ins[j].shape[1] // 2
        for s in shards[j]:
            give = ins[j].at[s, pl.ds(pl.multiple_of((1 - c) * hr, 8), hr), :]
            copies.append(pltpu.make_async_remote_copy(
                src_ref=give, dst_ref=got[j].at[s], send_sem=send.at[NSH * j + s], recv_sem=recv.at[NSH * j + s],
                device_id=(x, y, 1 - c), device_id_type=MESH))
    return copies


def _rs_sibling_last(g, got):
    def body(g_ref, _, got_ref, send, recv):
        copies = _rs_sibling_copies([g_ref], [got_ref], send, recv, [(NSH - 1,)])
        for cp in copies:
            cp.start()
        for cp in copies:
            cp.wait()

    return pl.pallas_call(
        body, in_specs=[_ANY, _ANY], out_specs=_ANY, out_shape=jax.ShapeDtypeStruct(got.shape, got.dtype),
        input_output_aliases={1: 0},
        scratch_shapes=[pltpu.SemaphoreType.DMA((NSH,)), pltpu.SemaphoreType.DMA((NSH,))],
        name="rs_sibling_last")(g, got)


def _core_vec():
    return lax.axis_index("c").astype(jnp.int32).reshape(1)


def _rs_add1(gs, gots, name):
    n = len(gs)
    _, r, cols = gs[0].shape
    hr = r // 2
    tr = _row_tile(hr, cols, (4 << 20) // n)
    nb = hr // tr

    def body(c_ref, *refs):
        for j in range(n):
            refs[2 * n + j][...] = (refs[j][...] + refs[n + j][...]).astype(BF)

    spec = pl.BlockSpec((1, tr, cols), lambda s, i, c_ref: (s, i, 0))
    keep = pl.BlockSpec((1, tr, cols), lambda s, i, c_ref: (s, c_ref[0] * nb + i, 0))
    return pl.pallas_call(
        body, grid_spec=pltpu.PrefetchScalarGridSpec(
            num_scalar_prefetch=1, grid=(NSH, nb), in_specs=[keep] * n + [spec] * n, out_specs=[spec] * n),
        out_shape=[jax.ShapeDtypeStruct((NSH, hr, cols), BF)] * n, name=name,
        compiler_params=_cp(("arbitrary", "arbitrary")))(_core_vec(), *gs, *gots)


def _rs_add2(sums, gots, name):
    n = len(sums)
    _, hr, cols = sums[0].shape
    tr = _row_tile(hr, cols, (2 << 20) // n)
    nb = hr // tr
    place = jnp.stack([2 * lax.axis_index("x") + lax.axis_index("y"), lax.axis_index("c")]).astype(jnp.int32)

    def body(p_ref, *refs):
        f = lambda v: v.astype(F32)
        for j in range(n):
            s_ref, got_ref = refs[j], refs[n + j]
            refs[2 * n + j][...] = f(s_ref[0]) + f(got_ref[0]) + f(got_ref[1]) + f(got_ref[2])

    return pl.pallas_call(
        body, grid_spec=pltpu.PrefetchScalarGridSpec(
            num_scalar_prefetch=1, grid=(nb,),
            in_specs=[pl.BlockSpec((1, tr, cols), lambda i, p_ref: (p_ref[0], i, 0))] * n
            + [pl.BlockSpec((3, tr, cols), lambda i, p_ref: (0, i, 0))] * n,
            out_specs=[pl.BlockSpec((tr, cols), lambda i, p_ref: (p_ref[1] * nb + i, 0))] * n),
        out_shape=[jax.ShapeDtypeStruct((2 * hr, cols), F32)] * n, name=name,
        compiler_params=_cp(("arbitrary",)))(place, *sums, *gots)


def _rs_chip_copies(ins, got, send, recv):
    x, y, c = _place()
    peers = ((1 - x, y), (x, 1 - y), (1 - x, 1 - y))
    return [pltpu.make_async_remote_copy(src_ref=ins[j].at[2 * px + py], dst_ref=got[j].at[k],
                                         send_sem=send.at[3 * j + k], recv_sem=recv.at[3 * j + k],
                                         device_id=(px, py, c), device_id_type=MESH)
            for j in range(len(ins)) for k, (px, py) in enumerate(peers)]


_AG_CHUNKS = 4


def _ag_sibling_copies(outs, send, recv):
    x, y, c = _place()
    copies = []
    for j, out in enumerate(outs):
        qr = out.shape[0] // (2 * _AG_CHUNKS)
        for k in range(_AG_CHUNKS):
            rows = out.at[pl.ds(pl.multiple_of((c * _AG_CHUNKS + k) * qr, 8), qr), :]
            copies.append(pltpu.make_async_remote_copy(
                src_ref=rows, dst_ref=rows, send_sem=send.at[_AG_CHUNKS * j + k], recv_sem=recv.at[_AG_CHUNKS * j + k],
                device_id=(x, y, 1 - c), device_id_type=MESH))
    return copies


def _allreduce_small(buf, fulls):
    rows = buf.shape[0]
    pr = rows // 8
    n = len(fulls)

    def body(*refs):
        in_ref, out_ref = refs[0], refs[1 + n]
        part, stage_a, stage_b = refs[2 + 2 * n:5 + 2 * n]
        (sa, ra, sb, rb, sc, rc, sd, rd, send, recv) = refs[5 + 2 * n:]
        x, y, c = _place()
        s = 2 * x + y
        sib = (x, y, 1 - c)
        chips = ((1 - x, y), (x, 1 - y), (1 - x, 1 - y))
        swaps = _ag_sibling_copies(refs[2 + n:2 + 2 * n], send, recv)
        for cp in swaps:
            cp.start()

        def piece(ref, chip, core):
            return ref.at[pl.ds(pl.multiple_of((2 * chip + core) * pr, 8), pr), :]

        def run(copies):
            for cp in copies:
                cp.start()
            for cp in copies:
                cp.wait()

        run([pltpu.make_async_remote_copy(src_ref=piece(in_ref, j, 1 - c), dst_ref=stage_a.at[j], send_sem=sa.at[j],
                                          recv_sem=ra.at[j], device_id=sib, device_id_type=MESH) for j in range(NSH)])
        for j in range(NSH):
            part[j] = piece(in_ref, j, c)[...] + stage_a[j]
        run([pltpu.make_async_remote_copy(src_ref=part.at[2 * px + py], dst_ref=stage_b.at[k], send_sem=sb.at[k],
                                          recv_sem=rb.at[k], device_id=(px, py, c), device_id_type=MESH)
             for k, (px, py) in enumerate(chips)])
        piece(out_ref, s, c)[...] = part[s] + stage_b[0] + stage_b[1] + stage_b[2]
        run([pltpu.make_async_remote_copy(src_ref=piece(out_ref, s, c), dst_ref=piece(out_ref, s, c),
                                          send_sem=sc.at[k], recv_sem=rc.at[k], device_id=(px, py, c),
                                          device_id_type=MESH) for k, (px, py) in enumerate(chips)])
        run([pltpu.make_async_remote_copy(src_ref=piece(out_ref, j, c), dst_ref=piece(out_ref, j, c), send_sem=sd.at[j],
                                          recv_sem=rd.at[j], device_id=sib, device_id_type=MESH) for j in range(NSH)])
        for cp in swaps:
            cp.wait()

    vm = pl.BlockSpec(memory_space=pltpu.VMEM)
    res = pl.pallas_call(
        body, in_specs=[vm] + [_ANY] * n, out_specs=[vm] + [_ANY] * n,
        out_shape=[jax.ShapeDtypeStruct((rows, D), F32)] + [jax.ShapeDtypeStruct(f.shape, F32) for f in fulls],
        input_output_aliases={1 + j: 1 + j for j in range(n)},
        scratch_shapes=[pltpu.VMEM((NSH, pr, D), F32), pltpu.VMEM((NSH, pr, D), F32), pltpu.VMEM((3, pr, D), F32)]
        + [pltpu.SemaphoreType.DMA((NSH,))] * 8 + [pltpu.SemaphoreType.DMA((_AG_CHUNKS * n,))] * 2,
        name="allreduce_small", compiler_params=_cp())(buf, *fulls)
    return res[0], res[1:]


def _pad_rows(a, mult):
    pad = (-a.shape[0]) % mult
    return jnp.concatenate([a, jnp.zeros((pad, a.shape[1]), a.dtype)]) if pad else a


def _local_step(x, c, ctx, c_ctx, tgt, me, shard, sh, b_mod, b_in, norm_g, cb, wr, wi, ln_g, ln_b):
    t_lat, t_ctx = x.shape[0], ctx.shape[0]
    nlb, ncb = t_lat // RB, t_ctx // RB
    mc = 3 * D // NSH
    mod_all, cc_all = _mod_tp(jnp.zeros((8, D), F32).at[0].set(c), c_ctx.reshape(1, D), sh["w_mod"],
                              lax.dynamic_slice_in_dim(b_mod, shard * mc, mc, axis=1))
    mod = jnp.zeros((16, 3 * D), F32).at[0].set(mod_all[me]).at[1].set(mod_all[8])
    cc = jnp.zeros((16, D), F32).at[0].set(c).at[1].set(c_ctx)
    z, u_all, w_in_g, nar = _inproj_ag(x, ctx, mod, sh["w_in"], b_in, sh["narrow"])
    nar = jnp.transpose(nar, (1, 0, 2)).reshape(-1, D)
    lbl, cw, br, bi, lam = nar[0:4].reshape(2, 2, D), nar[4:8], nar[8:10], nar[10:12], nar[12:14]
    o0, st0, (w_mod_g, p_a, p_b, w_out) = _gla_fwd(z, lbl, 0, nlb, ncb,
                                                   gather=[sh[k] for k in ("w_mod", "p_a", "p_b", "w_out")])
    p_a, p_b, w_out = p_a.reshape(D, D), p_b.reshape(D, D), w_out.reshape(D, D)
    o1, st1, _ = _gla_fwd(z, lbl, 1, nlb, ncb)
    h0, hp0, a0, hpc0 = _rglru_fwd(z, cw, cb, wr, br, wi, bi, lam, 0, t_lat, t_ctx)
    h1, hp1, a1, hpc1 = _rglru_fwd(z, cw, cb, wr, br, wi, bi, lam, 1, t_lat, t_ctx)
    (do, dh, dz4a, dz4b, dz6a, dz_sh3, gxres, y, dout, oa, dpa, obv, dpb, acc) = _merge(
        o0, o1, h0, h1, z, x, tgt, mod, norm_g, ln_g, ln_b, p_a, p_b, w_out, t_lat)
    gp_a = _wgrad(oa, dpa, "wgrad_pa")
    gp_b = _wgrad(obv, dpb, "wgrad_pb")
    gw_out = _wgrad(y, dout, "wgrad_wout")
    dxc_lat, dxc_ctx, dwr0, dwi0, sb0 = _rglru_bwd(z, cw, cb, wr, br, wi, bi, lam, dh, hp0, a0, hpc0, 0, t_lat, t_ctx)
    dz5_lat, dz5_ctx, dwr1, dwi1, sb1 = _rglru_bwd(z, cw, cb, wr, br, wi, bi, lam, dh, hp1, a1, hpc1, 1, t_lat, t_ctx,
                                                   prev=(dxc_lat, dxc_ctx))
    dq0, dv0, dz1, sa0 = _gla_bwd(z, lbl, do, st0, 0, nlb, ncb)
    dz_sh0, dz_sh1, sa1 = _gla_bwd(z, lbl, do, st1, 1, nlb, ncb, prev=(dq0, dv0, dz1, dz4a))
    dz = (dz_sh0, dz_sh1, _pack_shard2(dz4b, dz5_lat, dz5_ctx, dz6a), dz_sh3)
    g_p = [g.reshape(NSH, D // NSH, D) for g in (gp_a, gp_b, gw_out)]
    gw_in, got = _wgrad_in(u_all, dz, g_p)
    got[0] = _rs_sibling_last(gw_in, got[0])
    grads = [gw_in] + g_p
    sums = list(_rs_add1(grads[:1], got[:1], "rs_add1_w")) + list(_rs_add1(grads[1:], got[1:], "rs_add1_p"))
    gx, dm, got = _du(dz, w_in_g, x, ctx, mod, gxres, sums)
    fulls = list(_rs_add2(sums[:1], got[:1], "rs_add2_w")) + list(_rs_add2(sums[1:], got[1:], "rs_add2_p"))
    dmod =jnp.zeros((16, 3 * D), F32).at[0:2, 0:2 * D].set(dm[0:2]).at[0, 2 * D:].set(acc[0])
    dcc = _mod_bwd(cc, dmod, w_mod_g)
    small = dict(
        c_ctx=dcc[1:2], b_mod=(dmod[0] + dmod[1]).reshape(3, D),
        b_in=jnp.stack([sa1[2], sa0[0], sa1[0], sa1[3], acc[3], sb1[4], acc[4], acc[5], acc[6]]),
        lb_logits=jnp.stack([sa0[1], sa1[1], -sa0[1], -sa1[1]]),
        norm_a_g=acc[7:8], conv_w=sb1[8:12], conv_b=sb1[3:4],
        w_r=jnp.stack([dwr0, dwr1]).reshape(-1, D), w_i=jnp.stack([dwi0, dwi1]).reshape(-1, D),
        b_r=jnp.stack([sb0[0], sb1[0]]), b_i=jnp.stack([sb0[1], sb1[1]]), lam=jnp.stack([sb0[2], sb1[2]]),
        ln_g=acc[1:2], ln_b=acc[2:3])
    return acc[9, 0], gx, fulls, small, dmod, cc_all


_RS =("w_in", "p_a", "p_b", "w_out")
_SMALL =("c_ctx", "b_mod", "b_in", "lb_logits", "norm_a_g", "conv_w", "conv_b", "w_r", "w_i", "b_r", "b_i", "lam",
          "ln_g", "ln_b")
_BIG = ("w_mod", "w_in", "p_a", "p_b", "w_out")
_COL_SHARDED = ("lb_logits", "conv_w", "b_r", "b_i", "lam")
_WEIGHTS = ("c_ctx", "w_mod", "b_mod", "w_in", "b_in", "lb_logits", "norm_a_g", "conv_w", "conv_b", "w_r", "b_r", "w_i",
            "b_i", "lam", "p_a", "p_b", "w_out", "ln_g", "ln_b")


def kernel(x, c, ctx, c_ctx, w_mod, b_mod, w_in, b_in, lb_logits, norm_a_g, conv_w, conv_b, w_r, b_r, w_i, b_i, lam, p_a, p_b, w_out, ln_g, ln_b, loss_target, m_c_ctx, m_w_mod, m_b_mod, m_w_in, m_b_in, m_lb_logits, m_norm_a_g, m_conv_w, m_conv_b, m_w_r, m_b_r, m_w_i, m_b_i, m_lam, m_p_a, m_p_b, m_w_out, m_ln_g, m_ln_b, v_c_ctx, v_w_mod, v_b_mod, v_w_in, v_b_in, v_lb_logits, v_norm_a_g, v_conv_w, v_conv_b, v_w_r, v_b_r, v_w_i, v_b_i, v_lam, v_p_a, v_p_b, v_w_out, v_ln_g, v_ln_b):
    w = dict(c_ctx=c_ctx, w_mod=w_mod, b_mod=b_mod, w_in=w_in, b_in=b_in, lb_logits=lb_logits, norm_a_g=norm_a_g,
             conv_w=conv_w, conv_b=conv_b, w_r=w_r, b_r=b_r, w_i=w_i, b_i=b_i, lam=lam, p_a=p_a, p_b=p_b, w_out=w_out,
             ln_g=ln_g, ln_b=ln_b)
    m = dict(c_ctx=m_c_ctx, w_mod=m_w_mod, b_mod=m_b_mod, w_in=m_w_in, b_in=m_b_in, lb_logits=m_lb_logits,
             norm_a_g=m_norm_a_g, conv_w=m_conv_w, conv_b=m_conv_b, w_r=m_w_r, b_r=m_b_r, w_i=m_w_i, b_i=m_b_i,
             lam=m_lam, p_a=m_p_a, p_b=m_p_b, w_out=m_w_out, ln_g=m_ln_g, ln_b=m_ln_b)
    v = dict(c_ctx=v_c_ctx, w_mod=v_w_mod, b_mod=v_b_mod, w_in=v_w_in, b_in=v_b_in, lb_logits=v_lb_logits,
             norm_a_g=v_norm_a_g, conv_w=v_conv_w, conv_b=v_conv_b, w_r=v_w_r, b_r=v_b_r, w_i=v_w_i, b_i=v_b_i,
             lam=v_lam, p_a=v_p_a, p_b=v_p_b, w_out=v_w_out, ln_g=v_ln_g, ln_b=v_ln_b)
    shard = 2 * lax.axis_index("x") + lax.axis_index("y")
    cs = D // NSH

    sh = {k: w[k][0].astype(BF) for k in _BIG}
    sh["narrow"] = _pad_rows(jnp.concatenate([lb_logits.reshape(4, cs), conv_w[0], b_r[0], b_i[0], lam[0]], axis=0), 8)
    me = 2 * shard + lax.axis_index("c")
    loss, gx, fulls, small, dmod, cc_all = _local_step(
        x[0], c[0], ctx[0], c_ctx, loss_target[0], me, shard, sh, b_mod, b_in, norm_a_g, conv_b, w_r[0], w_i[0],
        ln_g, ln_b)

    dmod_rows = jnp.zeros((16, 3 * D), F32).at[me].set(dmod[0]).at[8].set(dmod[1]).reshape(48, D)
    loss_rows = jnp.zeros((8, D), F32).at[0, 0].set(loss)
    sizes = [small[k].shape[0] for k in _SMALL]
    red, fulls = _allreduce_small(_pad_rows(jnp.concatenate(
        [_pad_rows(small[k], 8) for k in _SMALL] + [dmod_rows, loss_rows], axis=0), 64), fulls)
    big = dict(zip(_RS, fulls))
    grads = {}
    off = 0
    for k, n in zip(_SMALL, sizes):
        g = red[off:off + n]
        off += n + (-n) % 8
        if k == "norm_a_g":
            g = g[:, :DH]
        if k in _COL_SHARDED:
            g = lax.dynamic_slice_in_dim(g, shard * cs, cs, axis=1)
        grads[k] = g.reshape(w[k].shape)
    for k in _RS:
        grads[k] = big[k].reshape(w[k].shape)
    dmod_all = red[off:off + 48].reshape(16, 3 * D)
    loss = red[off + 48, 0]
    mc = 3 * D // NSH
    grads["w_mod"] = _wmod_grad(cc_all, lax.dynamic_slice_in_dim(dmod_all, shard * mc, mc, axis=1)).reshape(
        w["w_mod"].shape)

    delta, new_m, new_v = {}, {}, {}
    for group in (("w_mod",), ("w_in",), ("p_a", "p_b", "w_out")):
        two = lambda t: [t[k].reshape(t[k].shape[-2:]) for k in group]
        d_, m_, v_ = _adamw(two(w), two(grads), two(m), two(v), "adamw_" + group[0])
        for j, k in enumerate(group):
            shp = w[k].shape
            delta[k], new_m[k], new_v[k] = d_[j].reshape(shp), m_[j].reshape(shp), v_[j].reshape(shp)
    d_, m_, v_ = _adamw_many(*[[t[k] for k in _SMALL] for t in (w, grads, m, v)])
    delta.update(zip(_SMALL, d_))
    new_m.update(zip(_SMALL, m_))
    new_v.update(zip(_SMALL, v_))

    return (loss, gx[None], *[grads[k] for k in _WEIGHTS], *[delta[k] for k in _WEIGHTS],
            *[new_m[k] for k in _WEIGHTS], *[new_v[k] for k in _WEIGHTS])
```

```python
import jax
import jax.numpy as jnp
from jax import lax
from jax.experimental import pallas as pl
from jax.experimental.pallas import tpu as pltpu

F32 = jnp.float32
BF = jnp.bfloat16
MESH = pl.DeviceIdType.MESH

D = 1024
H = 8
DH = 128
CH = 64
RB = 256
NCK = RB // CH
GRID_W = 64
CB = 8
RCH = 16
IN_COLS = 9 * D
NSH = 4
SHC = IN_COLS // NSH
MC = 3 * D // NSH
CUT2 = SHC - 2 * D
CUT4 = 2 * SHC - 4 * D
CUT6 = 3 * SHC - 6 * D
RG_C = 8.0
ALPHA = 2.0 ** 0.25
LN_EPS = 1e-5
RMS_EPS = 1e-6
Q_SCALE = DH ** -0.5
ADAM_LR, ADAM_B1, ADAM_B2, ADAM_EPS, ADAM_WD, ADAM_STEP = 0.001, 0.9, 0.999, 1e-08, 0.01, 10
VMEM_LIMIT = 56 * 1024 * 1024
VMEM_LIMIT_MERGE = 60 * 1024 * 1024
VMEM_LIMIT_RGLRU_BWD = 63 * 1024 * 1024


def _cp(sem=None, vmem=VMEM_LIMIT):
    return pltpu.CompilerParams(dimension_semantics=sem, vmem_limit_bytes=vmem)


def _sig(x):
    return 0.5 * jnp.tanh(0.5 * x) + 0.5


def _sig_tail(x):
    return 1.0 / (1.0 + jnp.exp(-x))


def _dot(a, b):
    return jnp.dot(a.astype(BF), b.astype(BF), preferred_element_type=F32)


def _dot_nt(a, b):
    return lax.dot_general(a.astype(BF), b.astype(BF), (((1,), (1,)), ((), ())), preferred_element_type=F32)


def _dot_tn(a, b):
    return lax.dot_general(a.astype(BF), b.astype(BF), (((0,), (0,)), ((), ())), preferred_element_type=F32)


def _colsum(v):
    return jnp.sum(v, axis=0, keepdims=True)


def _mod_bwd(cc, dmod, w_mod_g):
    def body(cc_ref, dm_ref, w_ref, dcc_ref):
        v = cc_ref[...]
        sg = _sig(v)
        ds = jnp.zeros((16, D), F32)
        for k in range(NSH):
            ds = ds + _dot_nt(dm_ref[:, k * MC:(k + 1) * MC], w_ref[k])
        dcc_ref[...] = ds * (sg * (1.0 + v * (1.0 - sg)))
    return pl.pallas_call(body, out_shape=jax.ShapeDtypeStruct((16, D), F32),
                          name="mod_bwd", compiler_params=_cp())(cc, dmod, w_mod_g)


def _wmod_grad(cc, dmod_cols):
    def body(cc_ref, dm_ref, dw_ref):
        v = cc_ref[...]
        dw_ref[...] = _dot_tn(v * _sig(v), dm_ref[...])
    return pl.pallas_call(body, out_shape=jax.ShapeDtypeStruct((D, dmod_cols.shape[1]), F32),
                          name="wmod_grad", compiler_params=_cp())(cc, dmod_cols)


def _modulate(xv, mod_ref, is_lat):
    sh = jnp.where(is_lat, mod_ref[0:1, 0:D], mod_ref[1:2, 0:D])
    sc = jnp.where(is_lat, mod_ref[0:1, D:2 * D], mod_ref[1:2, D:2 * D])
    return xv * (1.0 + sc) + sh


def _gla_mask(d, n):
    row = lax.broadcasted_iota(jnp.int32, (n, n), 0)
    col = lax.broadcasted_iota(jnp.int32, (n, n), 1)
    same = (row // CH) == (col // CH)
    return same & ((row <= col) if d else (row >= col))


def _chunk_cumsum(v, rev):
    n = v.shape[0]
    pos = lax.broadcasted_iota(jnp.int32, v.shape, 0) & (CH - 1)
    s = 1
    while s < CH:
        if rev:
            v = v + jnp.where(pos < CH - s, pltpu.roll(v, n - s, 0), 0.0)
        else:
            v = v + jnp.where(pos >= s, pltpu.roll(v, s, 0), 0.0)
        s *= 2
    return v


def _chunk_rows(c):
    return slice(c * CH, (c + 1) * CH)


def _gla_features(zq, zf, lb, d):
    sq = _sig(zq)
    q = zq * sq * Q_SCALE
    sf = _sig(zf)
    f = lb + (1.0 - lb) * sf
    k = 1.0 - f
    g = _chunk_cumsum(jnp.log(f), d)
    last = 0 if d else CH - 1
    gls = [g[c * CH + last:c * CH + last + 1, :] for c in range(NCK)]
    glb = jnp.concatenate([jnp.broadcast_to(gl, (CH, D)) for gl in gls], axis=0)
    eg, eig, eeg = jnp.exp(g), jnp.exp(-g), jnp.exp(glb - g)
    decs = [jnp.exp(gl) for gl in gls]
    return sq, sf, f, q * eg, k * eig, k * eeg, eg, eig, eeg, decs


def _lower_bound(lbl_ref, d):
    return _sig_tail(lbl_ref[0, d:d + 1, :] - lbl_ref[1, d:d + 1, :])


def _gla_rb(d, nlb, ncb):
    nrb = nlb + ncb
    if d:
        return lambda s: nrb - 1 - s
    return lambda s: jnp.where(s < ncb, nlb + s, s - ncb)


def _gla_fwd(z, lbl, d, nlb, ncb, gather=()):
    m = z.shape[0]
    nrb = nlb + ncb
    rb = _gla_rb(d, nlb, ncb)
    ng = len(gather)

    def body(*refs):
        q_ref, f_ref, v_ref, lbl_ref = refs[:4]
        ag_in = refs[4:4 + ng]
        o_ref, st_ref = refs[4 + ng:6 + ng]
        ag_out = refs[6 + ng:6 + 2 * ng]
        S = refs[6 + 2 * ng]
        ag_sems = refs[7 + 2 * ng:]
        s = pl.program_id(0)

        @pl.when(s == 0)
        def _():
            S[...] = jnp.zeros_like(S)
            if ng:
                local, chip, _ = _ag_copies(ag_in, ag_out, *ag_sems)
                for cp in chip + local:
                    cp.start()

        lb = _lower_bound(lbl_ref, d)
        mb = _gla_mask(d, CH)
        _, _, _, qd, ki, ke, _, _, _, decs = _gla_features(q_ref[...], f_ref[...], lb, d)
        qd, ki, ke, v = qd.astype(BF), ki.astype(BF), ke.astype(BF), v_ref[...].astype(BF)
        order = range(NCK - 1, -1, -1) if d else range(NCK)
        for h in range(H):
            hs = slice(h * DH, (h + 1) * DH)
            intra, upd = {}, {}
            for c in range(NCK):
                rs = _chunk_rows(c)
                a = jnp.where(mb, _dot_nt(qd[rs, hs], ki[rs, hs]), 0.0)
                intra[c] = _dot(a, v[rs, hs])
                upd[c] = _dot_tn(v[rs, hs], ke[rs, hs])
            st = S[h]
            for c in order:
                rs = _chunk_rows(c)
                st_ref[c, h] = st
                o_ref[rs, hs] = intra[c] + _dot_nt(qd[rs, hs], st)
                st = st * decs[c][:, hs] + upd[c]
            S[h] = st

        if ng:
            @pl.when(s == nrb - 1)
            def _():
                _ag_finish(*_ag_copies(ag_in, ag_out, *ag_sems))

    def zspec(cb):
        return pl.BlockSpec((RB, D), lambda s: (rb(s), cb))

    outs = pl.pallas_call(
        body, grid=(nrb,),
        in_specs=[zspec(0), zspec(1 + d), zspec(3), pl.BlockSpec((2, 2, D), lambda s: (0, 0, 0))] + [_ANY] * ng,
        out_specs=[pl.BlockSpec((RB, D), lambda s: (rb(s), 0)),
                   pl.BlockSpec((NCK, H, DH, DH), lambda s: (rb(s), 0, 0, 0))] + [_ANY] * ng,
        out_shape=[jax.ShapeDtypeStruct((m, D), F32), jax.ShapeDtypeStruct((m // CH, H, DH, DH), F32)]
        + [jax.ShapeDtypeStruct((NSH,) + g.shape, g.dtype) for g in gather],
        scratch_shapes=[pltpu.VMEM((H, DH, DH), F32)] + (_ag_sems(ng) if ng else []),
        name=f"gla_fwd{d}", compiler_params=_cp(("arbitrary",)))(z, z, z, lbl, *gather)
    return outs[0], outs[1], list(outs[2:])


def _gla_bwd(z, lbl, do_lat, states, d, nlb, ncb, prev=None):
    m = z.shape[0]
    nrb = nlb + ncb
    fwd_rb = _gla_rb(d, nlb, ncb)
    rb = lambda s: fwd_rb(nrb - 1 - s)
    last = prev is not None

    def body(*refs):
        if last:
            (q_ref, f_ref, v_ref, lbl_ref, do_ref, st_ref, pq_ref, pv_ref, dz1_ref, dz4a_ref, sh0_ref, sh1_ref, sum_ref,
             dS) = refs
        else:
            q_ref, f_ref, v_ref, lbl_ref, do_ref, st_ref, o0_ref, o1_ref, o2_ref, sum_ref, dS = refs
        s = pl.program_id(0)
        is_lat = rb(s) < nlb

        @pl.when(s == 0)
        def _():
            dS[...] = jnp.zeros_like(dS)
            sum_ref[...] = jnp.zeros_like(sum_ref)

        lb = _lower_bound(lbl_ref, d)
        mb = _gla_mask(d, RB)
        zq = q_ref[...]
        sq, sf, f, qd, ki, ke, eg, eig, eeg, decs = _gla_features(zq, f_ref[...], lb, d)
        qdb, kib, keb, vb = qd.astype(BF), ki.astype(BF), ke.astype(BF), v_ref[...].astype(BF)
        dob = jnp.where(is_lat, do_ref[...], 0.0).astype(BF)
        order = range(NCK) if d else range(NCK - 1, -1, -1)
        dqd_h, dki_h, dke_h, dv_h, ddec_h = [], [], [], [], []
        for h in range(H):
            hs = slice(h * DH, (h + 1) * DH)
            a = jnp.where(mb, _dot_nt(qdb[:, hs], kib[:, hs]), 0.0).astype(BF)
            da = jnp.where(mb, _dot_nt(dob[:, hs], vb[:, hs]), 0.0).astype(BF)
            dqd_i = _dot(da, kib[:, hs])
            dki_h.append(_dot_tn(da, qdb[:, hs]))
            dvi = _dot_tn(a, dob[:, hs])
            dqd, inc = {}, {}
            for c in range(NCK):
                rs = _chunk_rows(c)
                dqd[c] = dqd_i[rs, :] + _dot(dob[rs, hs], st_ref[c, h])
                inc[c] = _dot_tn(dob[rs, hs], qdb[rs, hs])
            dst = dS[h]
            dke, dv, ddec = {}, {}, {}
            for c in order:
                rs = _chunk_rows(c)
                dv[c] = dvi[rs, :] + _dot_nt(keb[rs, hs], dst)
                dke[c] = _dot(vb[rs, hs], dst)
                ddec[c] = _colsum(st_ref[c, h] * dst)
                dst = inc[c] + dst * decs[c][:, hs]
            dS[h] = dst
            cat = lambda t: jnp.concatenate([t[c] for c in range(NCK)], axis=0)
            dqd_h.append(cat(dqd))
            dke_h.append(cat(dke))
            dv_h.append(cat(dv))
            ddec_h.append([ddec[c] for c in range(NCK)])
        lanes = lambda parts: jnp.concatenate(parts, axis=1)
        dqd, dki, dke, dv = lanes(dqd_h), lanes(dki_h), lanes(dke_h), lanes(dv_h)
        dq = dqd * eg
        dk = dki * eig + dke * eeg
        dke_ke = dke * ke
        dg = dqd * qd - dki * ki - dke_ke
        dgl = [_colsum(dke_ke[_chunk_rows(c), :]) + lanes([ddec_h[h][c] for h in range(H)]) * decs[c]
               for c in range(NCK)]
        dglb = jnp.concatenate([jnp.broadcast_to(t, (CH, D)) for t in dgl], axis=0)
        df = (_chunk_cumsum(dg, 1 - d) + dglb) / f - dk
        dzf = df * (1.0 - lb) * (sf * (1.0 - sf))
        sum_ref[0:1, :] += _colsum(dzf)
        sum_ref[1:2, :] += _colsum(df * (1.0 - sf))
        if last:
            dz0 = (dq + pq_ref[...]) * (Q_SCALE * (sq * (1.0 + zq * (1.0 - sq))))
            dz3 = dv + pv_ref[...]
            sum_ref[2:3, :] += _colsum(dz0)
            sum_ref[3:4, :] += _colsum(dz3)
            dz2 = dzf.astype(BF)
            sh0_ref[:, 0:D] = dz0.astype(BF)
            sh0_ref[:, D:2 * D] = dz1_ref[...]
            sh0_ref[:, 2 * D:] = dz2[:, :CUT2]
            sh1_ref[:, 0:D - CUT2] = dz2[:, CUT2:]
            sh1_ref[:, D - CUT2:2 * D - CUT2] = dz3.astype(BF)
            sh1_ref[:, 2 * D - CUT2:] = dz4a_ref[...]
        else:
            o0_ref[...] = dq
            o1_ref[...] = dv
            o2_ref[...] = dzf.astype(BF)

        @pl.when(s == nrb - 1)
        def _():
            sum_ref[1:2, :] = sum_ref[1:2, :] * (lb * (1.0 - lb))

    def zspec(cb):
        return pl.BlockSpec((RB, D), lambda s: (rb(s), cb))

    rowspec = pl.BlockSpec((RB, D), lambda s: (rb(s), 0))
    in_specs = [zspec(0), zspec(1 + d), zspec(3), pl.BlockSpec((2, 2, D), lambda s: (0, 0, 0)),
                pl.BlockSpec((RB, D), lambda s: (jnp.minimum(rb(s), nlb - 1), 0)),
                pl.BlockSpec((NCK, H, DH, DH), lambda s: (rb(s), 0, 0, 0))]
    args = [z, z, z, lbl, do_lat, states]
    sumspec = pl.BlockSpec((8, D), lambda s: (0, 0))
    if last:
        in_specs += [rowspec, rowspec, rowspec, pl.BlockSpec((RB, CUT4), lambda s: (rb(s), 0))]
        args += list(prev)
        shspec = pl.BlockSpec((RB, SHC), lambda s: (rb(s), 0))
        out_specs = (shspec, shspec, sumspec)
        out_shape = (jax.ShapeDtypeStruct((m, SHC), BF), jax.ShapeDtypeStruct((m, SHC), BF))
    else:
        out_specs = (rowspec, rowspec, rowspec, sumspec)
        out_shape = (jax.ShapeDtypeStruct((m, D), F32), jax.ShapeDtypeStruct((m, D), F32),
                     jax.ShapeDtypeStruct((m, D), BF))
    return pl.pallas_call(
        body, grid=(nrb,), in_specs=in_specs, out_specs=out_specs,
        out_shape=out_shape + (jax.ShapeDtypeStruct((8, D), F32),),
        scratch_shapes=[pltpu.VMEM((H, DH, DH), F32)],
        name=f"gla_bwd{d}", compiler_params=_cp(("arbitrary",)))(*args)


def _shift_rows(v, k, r):
    row = lax.broadcasted_iota(jnp.int32, v.shape, 0)
    rolled = pltpu.roll(v, k % r, 0)
    return jnp.where((row >= k) & (row < r + k), rolled, 0.0)


def _conv_fwd(xb, cw_ref, cb_ref, r):
    return (cb_ref[...] + _shift_rows(xb, 1, r) * cw_ref[0:1, :] + xb * cw_ref[1:2, :]
            + _shift_rows(xb, -1, r) * cw_ref[2:3, :] + _shift_rows(xb, -2, r) * cw_ref[3:4, :])


def _window(ref, lo, n, rows):
    parts = []
    if lo < 0:
        parts.append(jnp.zeros((-lo,) + tuple(ref.shape[1:]), F32))
    parts.append(ref[max(lo, 0):min(lo + n, rows)])
    if lo + n > rows:
        parts.append(jnp.zeros((lo + n - rows,) + tuple(ref.shape[1:]), F32))
    return parts[0] if len(parts) == 1 else jnp.concatenate(parts, axis=0)


def _conv_cols(x_ref, cw_ref, cb_ref, r0, n, rows):
    w = _window(x_ref, r0 - 1, n + 3, rows)
    return (cb_ref[...] + w[0:n] * cw_ref[0:1, :] + w[1:n + 1] * cw_ref[1:2, :] + w[2:n + 2] * cw_ref[2:3, :]
            + w[3:n + 3] * cw_ref[3:4, :])


def _softplus_neg(lam):
    y = jnp.exp(-jnp.abs(lam))
    u = 1.0 + y
    tiny = u == 1.0
    l1p = jnp.where(tiny, y, jnp.log(u) * (y / jnp.where(tiny, 1.0, u - 1.0)))
    return jnp.maximum(-lam, 0.0) + l1p


def _gates(xc, wr_ref, br, wi_ref, bi, sp):
    xcb = xc.astype(BF)
    rs, is_ = [], []
    for g in range(H):
        gs = slice(g * DH, (g + 1) * DH)
        rs.append(jnp.dot(xcb[:, gs], wr_ref[g].astype(BF), preferred_element_type=F32))
        is_.append(jnp.dot(xcb[:, gs], wi_ref[g].astype(BF), preferred_element_type=F32))
    r = _sig_tail(jnp.concatenate(rs, axis=1) + br)
    i = _sig(jnp.concatenate(is_, axis=1) + bi)
    log_a = (-RG_C * r) * sp
    a = jnp.exp(log_a)
    t = jnp.tanh(log_a)
    om = (-2.0 * t) / (1.0 - t)
    return r, i, a, om


def _scan_rows(d, nrows, a_s, b_s, h_s, h0):
    nsl = nrows // 8

    def slab(j, h):
        jj = (nsl - 1 - j) if d else j
        r0 = pl.multiple_of(jj * 8, 8)
        for t in (range(7, -1, -1) if d else range(8)):
            h = a_s[pl.ds(r0 + t, 1), :] * h + b_s[pl.ds(r0 + t, 1), :]
            h_s[pl.ds(r0 + t, 1), :] = h
        return h

    return lax.fori_loop(0, nsl, slab, h0)


def _col_of(d, ncols):
    if d:
        return lambda s: ncols - jnp.maximum(s, 1)
    return lambda s: jnp.maximum(s, 1) - 1


def _rglru_fwd(z, cw, cb, wr, br, wi, bi, lam, d, t_lat, t_ctx):
    m = z.shape[0]
    rows = t_lat // GRID_W
    z3 = z.reshape(m // GRID_W, GRID_W, IN_COLS)
    nblk = GRID_W // CB
    blk = _col_of(d, nblk)
    cblk = t_lat // t_ctx
    rc = min(RCH, rows)

    def body(zc_ref, zx_ref, cw_ref, cb_ref, wr_ref, br_ref, wi_ref, bi_ref, lam_ref,
             hx_ref, hpx_ref, ax_ref, hpc_ref, a_s, b_s, h_s, hcar, a3, b3, cin_s):
        s = pl.program_id(0)
        sp = _softplus_neg(lam_ref[d:d + 1, :])
        br_ = br_ref[d:d + 1, :]
        bi_ = bi_ref[d:d + 1, :]

        @pl.when(s == 0)
        def _():
            xc = _conv_fwd(zc_ref[...], cw_ref, cb_ref, t_ctx)
            _, i, a, om = _gates(xc, wr_ref, br_, wi_ref, bi_, sp)
            a_s[...] = a
            b_s[...] = jnp.sqrt(om) * (i * xc)
            h0 = jnp.zeros((1, D), F32)
            hcar[0:1, :] = _scan_rows(d, t_ctx, a_s, b_s, h_s, h0)
            hs = h_s[...]
            row = lax.broadcasted_iota(jnp.int32, (t_ctx, D), 0)
            if d:
                hpc_ref[...] = jnp.where(row == t_ctx - 1, h0, pltpu.roll(hs, t_ctx - 1, 0))
            else:
                hpc_ref[...] = jnp.where(row == 0, h0, pltpu.roll(hs, 1, 0))

        @pl.when(s > 0)
        def _():
            for r0 in range(0, rows, rc):
                xc = _conv_cols(zx_ref, cw_ref, cb_ref, r0, rc, rows).reshape(rc * CB, D)
                _, i, a, om = _gates(xc, wr_ref, br_, wi_ref, bi_, sp)
                a3[r0:r0 + rc] = a.reshape(rc, CB, D)
                ax_ref[r0:r0 + rc] = a.reshape(rc, CB, D)
                b3[r0:r0 + rc] = (jnp.sqrt(om) * (i * xc)).reshape(rc, CB, D)

            def local(t, carry):
                hl, p = carry
                r = (rows - 1 - t) if d else t
                a = a3[r]
                hl = a * hl + b3[r]
                p = a * p
                b3[r] = hl
                a3[r] = p
                return hl, p

            hl, p = lax.fori_loop(0, rows, local, (jnp.zeros((CB, D), F32), jnp.ones((CB, D), F32)))
            cin = hcar[0:1, :]
            for j in (range(CB - 1, -1, -1) if d else range(CB)):
                cin_s[j:j + 1, :] = cin
                cin = hl[j:j + 1, :] + p[j:j + 1, :] * cin
            hcar[0:1, :] = cin
            c_in = cin_s[...]

            def fix(t, prev):
                r = (rows - 1 - t) if d else t
                h = b3[r] + a3[r] * c_in
                hx_ref[r] = h
                hpx_ref[r] = prev
                return h

            lax.fori_loop(0, rows, fix, c_in)

    full = lambda shp: pl.BlockSpec(shp, lambda s: (0,) * len(shp))
    colspec = pl.BlockSpec((rows, CB, D), lambda s: (0, blk(s), 0))
    outs = pl.pallas_call(
        body, grid=(nblk + 1,),
        in_specs=[pl.BlockSpec((t_ctx, D), lambda s: (cblk, 5)),
                  pl.BlockSpec((rows, CB, D), lambda s: (0, blk(s), 5)),
                  full((4, D)), full((1, D)),
                  pl.BlockSpec((None, H, DH, DH), lambda s: (d, 0, 0, 0)), full((2, D)),
                  pl.BlockSpec((None, H, DH, DH), lambda s: (d, 0, 0, 0)), full((2, D)), full((2, D))],
        out_specs=(colspec, colspec, colspec, full((t_ctx, D))),
        out_shape=(jax.ShapeDtypeStruct((rows, GRID_W, D), F32),) * 3 + (jax.ShapeDtypeStruct((t_ctx, D), F32),),
        scratch_shapes=[pltpu.VMEM((t_ctx, D), F32), pltpu.VMEM((t_ctx, D), F32), pltpu.VMEM((t_ctx, D), F32),
                        pltpu.VMEM((8, D), F32), pltpu.VMEM((rows, CB, D), F32), pltpu.VMEM((rows, CB, D), F32),
                        pltpu.VMEM((CB, D), F32)],
        name=f"rglru_fwd{d}", compiler_params=_cp(("arbitrary",)))(z, z3, cw, cb, wr, br, wi, bi, lam)
    return outs[0].reshape(t_lat, D), outs[1].reshape(t_lat, D), outs[2].reshape(t_lat, D), outs[3]


def _rglru_bwd(z, cw, cb, wr, br, wi, bi, lam, dh_lat, hp_lat, a_lat, hp_ctx, d, t_lat, t_ctx, prev=None):
    m = z.shape[0]
    rows = t_lat // GRID_W
    z3 = z.reshape(m // GRID_W, GRID_W, IN_COLS)
    nblk = GRID_W // CB
    fblk = _col_of(d, nblk)
    blk = lambda s: fblk(nblk - jnp.minimum(s, nblk - 1))
    cblk = t_lat // t_ctx
    rc = min(RCH, rows)
    last = prev is not None
    view3 = lambda v: v.reshape(rows, GRID_W, D)

    def body(*refs):
        (zc_ref, zx_ref, cw_ref, cb_ref, wr_ref, br_ref, wi_ref, bi_ref, lam_ref, dh_ref, hpx_ref, ax_ref,
         hpc_ref) = refs[:13]
        k = 13
        if last:
            pdx_ref, pdc_ref = refs[13:15]
            k = 15
        ox_ref, oc_ref, dwr_ref, dwi_ref, sum_ref, a_s, b_s, gcar, a3, b3, cin_s = refs[k:]
        s = pl.program_id(0)
        lam_d = lam_ref[d:d + 1, :]
        sp = _softplus_neg(lam_d)
        br_ = br_ref[d:d + 1, :]
        bi_ = bi_ref[d:d + 1, :]
        flat = lambda v: v.reshape(-1, D)

        @pl.when(s == 0)
        def _():
            gcar[...] = jnp.zeros_like(gcar)
            dwr_ref[...] = jnp.zeros_like(dwr_ref)
            dwi_ref[...] = jnp.zeros_like(dwi_ref)
            sum_ref[...] = jnp.zeros_like(sum_ref)


        def conv_sums(dxc, dxb, xm1, x0, xp1, xp2):
            sum_ref[3:4, :] += _colsum(flat(dxc))
            sum_ref[4:5, :] += _colsum(flat(dxb))
            sum_ref[8:9, :] += _colsum(flat(dxc * xm1))
            sum_ref[9:10, :] += _colsum(flat(dxc * x0))
            sum_ref[10:11, :] += _colsum(flat(dxc * xp1))
            sum_ref[11:12, :] += _colsum(flat(dxc * xp2))

        def gate_grads(g, hp, xc, rr, i, a, om):
            mult = jnp.sqrt(om)
            da = g * hp
            ixc = i * xc
            dmult = g * ixc
            dixc = g * mult
            di = dixc * xc
            dxc = dixc * i
            dlog_a = da * a - dmult * ((1.0 - om) / mult)
            dr = dlog_a * (-RG_C * sp)
            sum_ref[2:3, :] += _colsum(dlog_a * rr)
            drp = dr * rr * (1.0 - rr)
            dip = di * i * (1.0 - i)
            sum_ref[0:1, :] += _colsum(drp)
            sum_ref[1:2, :] += _colsum(dip)
            xcb = xc.astype(BF)
            drb = drp.astype(BF)
            dib = dip.astype(BF)
            parts = []
            for gi in range(H):
                gs = slice(gi * DH, (gi + 1) * DH)
                parts.append(_dot_nt(drb[:, gs], wr_ref[gi]) + _dot_nt(dib[:, gs], wi_ref[gi]))
                dwr_ref[gi] += _dot_tn(xcb[:, gs], drb[:, gs])
                dwi_ref[gi] += _dot_tn(xcb[:, gs], dib[:, gs])
            return dxc + jnp.concatenate(parts, axis=1)

        @pl.when(s < nblk)
        def _():
            def local(t, carry):
                c, q = carry
                r = t if d else (rows - 1 - t)
                a = ax_ref[r]
                c = a * (c + dh_ref[r])
                q = a * q
                b3[r] = c
                a3[r] = q
                return c, q

            c, q = lax.fori_loop(0, rows, local, (jnp.zeros((CB, D), F32), jnp.ones((CB, D), F32)))
            cin = gcar[0:1, :]
            for j in (range(CB) if d else range(CB - 1, -1, -1)):
                cin_s[j:j + 1, :] = cin
                cin = c[j:j + 1, :] + q[j:j + 1, :] * cin
            gcar[0:1, :] = cin
            c_in = cin_s[...]
            for r0 in (range(rows - rc, -1, -rc) if d else range(0, rows, rc)):
                if d:
                    lo = max(r0 - 1, 0)
                    cn = b3[lo:r0 + rc - 1] + a3[lo:r0 + rc - 1] * c_in
                    if r0 == 0:
                        cn = jnp.concatenate([c_in[None], cn], axis=0)
                else:
                    hi = min(r0 + rc + 1, rows)
                    cn = b3[r0 + 1:hi] + a3[r0 + 1:hi] * c_in
                    if hi == rows:
                        cn = jnp.concatenate([cn, c_in[None]], axis=0)
                g = flat(dh_ref[r0:r0 + rc] + cn)
                xc = flat(_conv_cols(zx_ref, cw_ref, cb_ref, r0, rc, rows))
                rr, i, a, om = _gates(xc, wr_ref, br_, wi_ref, bi_, sp)
                b3[r0:r0 + rc] = gate_grads(g, flat(hpx_ref[r0:r0 + rc]), xc, rr, i, a, om).reshape(rc, CB, D)
            if last:
                for r0 in range(0, rows, rc):
                    b3[r0:r0 + rc] = b3[r0:r0 + rc] + pdx_ref[r0:r0 + rc]
                for r0 in range(0, rows, rc):
                    w = _window(b3, r0 - 2, rc + 3, rows)
                    xw = _window(zx_ref, r0 - 1, rc + 3, rows)
                    dxc = w[2:rc + 2]
                    dxb = (w[3:rc + 3] * cw_ref[0:1, :] + dxc * cw_ref[1:2, :] + w[1:rc + 1] * cw_ref[2:3, :]
                           + w[0:rc] * cw_ref[3:4, :])
                    conv_sums(dxc, dxb, xw[0:rc], xw[1:rc + 1], xw[2:rc + 2], xw[3:rc + 3])
                    ox_ref[r0:r0 + rc] = dxb
            else:
                for r0 in range(0, rows, rc):
                    ox_ref[r0:r0 + rc] = b3[r0:r0 + rc]

        @pl.when(s == nblk)
        def _():
            r = t_ctx
            xb = zc_ref[...]
            xc = _conv_fwd(xb, cw_ref, cb_ref, r)
            rr, i, a, om = _gates(xc, wr_ref, br_, wi_ref, bi_, sp)
            a_s[...] = a
            b_s[...] = jnp.zeros((r, D), F32)
            c0 = gcar[0:1, :]
            _scan_rows(1 - d, r, a_s, b_s, b_s, c0)
            cs = b_s[...]
            row = lax.broadcasted_iota(jnp.int32, (r, D), 0)
            if d:
                g = jnp.where(row == 0, c0, pltpu.roll(cs, 1, 0))
            else:
                g = jnp.where(row == r - 1, c0, pltpu.roll(cs, r - 1, 0))
            dxc = gate_grads(g, hpc_ref[...], xc, rr, i, a, om)
            if last:
                dxc = dxc + pdc_ref[...]
                dxb = (_shift_rows(dxc, -1, r) * cw_ref[0:1, :] + dxc * cw_ref[1:2, :]
                       + _shift_rows(dxc, 1, r) * cw_ref[2:3, :] + _shift_rows(dxc, 2, r) * cw_ref[3:4, :])
                conv_sums(dxc, dxb, _shift_rows(xb, 1, r), xb, _shift_rows(xb, -1, r), _shift_rows(xb, -2, r))
                oc_ref[...] = dxb
            else:
                oc_ref[...] = dxc
            sum_ref[2:3, :] = sum_ref[2:3, :] * (RG_C * _sig_tail(-lam_d))

    full = lambda shp: pl.BlockSpec(shp, lambda s: (0,) * len(shp))
    once = lambda shp: pl.BlockSpec(shp, lambda s: (0,) * len(shp), pipeline_mode=pl.Buffered(1))
    colspec = pl.BlockSpec((rows, CB, D), lambda s: (0, blk(s), 0))
    colonce = pl.BlockSpec((rows, CB, D), lambda s: (0, blk(s), 0), pipeline_mode=pl.Buffered(1))
    in_specs = [pl.BlockSpec((t_ctx, D), lambda s: (cblk, 5), pipeline_mode=pl.Buffered(1)),
                pl.BlockSpec((rows, CB, D), lambda s: (0, blk(s), 5), pipeline_mode=pl.Buffered(1)),
                full((4, D)), full((1, D)),
                pl.BlockSpec((None, H, DH, DH), lambda s: (d, 0, 0, 0)), full((2, D)),
                pl.BlockSpec((None, H, DH, DH), lambda s: (d, 0, 0, 0)), full((2, D)), full((2, D)),
                colspec, colspec, colonce, once((t_ctx, D))]
    args = [z, z3, cw, cb, wr, br, wi, bi, lam, view3(dh_lat), view3(hp_lat), view3(a_lat), hp_ctx]
    if last:
        in_specs += [colonce, once((t_ctx, D))]
        args += [view3(prev[0]), prev[1]]
    outs = pl.pallas_call(
        body, grid=(nblk + 1,), in_specs=in_specs,
        out_specs=(colspec, full((t_ctx, D)), full((H, DH, DH)), full((H, DH, DH)), full((16, D))),
        out_shape=(jax.ShapeDtypeStruct((rows, GRID_W, D), F32), jax.ShapeDtypeStruct((t_ctx, D), F32),
                   jax.ShapeDtypeStruct((H, DH, DH), F32), jax.ShapeDtypeStruct((H, DH, DH), F32),
                   jax.ShapeDtypeStruct((16, D), F32)),
        scratch_shapes=[pltpu.VMEM((t_ctx, D), F32), pltpu.VMEM((t_ctx, D), F32), pltpu.VMEM((8, D), F32),
                        pltpu.VMEM((rows, CB, D), F32), pltpu.VMEM((rows, CB, D), F32), pltpu.VMEM((CB, D), F32)],
        name=f"rglru_bwd{d}", compiler_params=_cp(("arbitrary",), VMEM_LIMIT_RGLRU_BWD))(*args)
    return (outs[0].reshape(t_lat, D), outs[1]) + tuple(outs[2:])


def _merge(o_f, o_b, h_f, h_b, z, x, tgt, mod, norm_g, ln_g, ln_b, p_a, p_b, w_out, t_lat):
    tm = 256
    nt = t_lat // tm

    def body(of_ref, ob_ref, hf_ref, hb_ref, z4_ref, z6_ref, z7_ref, z8_ref, x_ref, t_ref, mod_ref, ng_ref,
             lg_ref, lb_ref, pa_ref, pb_ref, wo_ref,
             do_ref, dh_ref, dz4a_ref, dz4b_ref, dz6a_ref, sh3_ref, gx_ref,
             y_ref, dout_ref, oa_ref, dpa_ref, obv_ref, dpb_ref, acc_ref):
        i = pl.program_id(0)
        lat = i < nt
        latf = lat.astype(F32)

        @pl.when(i == 0)
        def _():
            acc_ref[...] = jnp.zeros_like(acc_ref)

        def per_head(v):
            return jnp.concatenate(
                [jnp.broadcast_to(jnp.mean(v[:, h * DH:(h + 1) * DH], axis=-1, keepdims=True), (tm, DH))
                 for h in range(H)], axis=1)

        gt = mod_ref[0:1, 2 * D:3 * D]
        gfull = jnp.concatenate([ng_ref[...]] * H, axis=1)
        o = of_ref[...] + ob_ref[...]
        rinv = lax.rsqrt(per_head(o * o) + RMS_EPS)
        n = o * rinv
        na = n * gfull
        z4 = z4_ref[...]
        s4 = _sig(z4)
        silu4 = z4 * s4
        oa = na * silu4
        z6 = z6_ref[...]
        s6 = _sig(z6)
        silu6 = z6 * s6
        hsum = hf_ref[...] + hb_ref[...]
        obv = hsum * silu6
        pa = _dot(oa, pa_ref[...])
        pb = _dot(obv, pb_ref[...])
        s7 = _sig(z7_ref[...])
        s8 = _sig(z8_ref[...])
        y = s7 * pa + s8 * pb
        out = _dot(y, wo_ref[...])
        pre = ALPHA * x_ref[...] + gt * out
        mu = jnp.mean(pre, axis=-1, keepdims=True)
        xc = pre - mu
        rstd = lax.rsqrt(jnp.mean(xc * xc, axis=-1, keepdims=True) + LN_EPS)
        xhat = xc * rstd
        lg = lg_ref[...]
        diff = xhat * lg + lb_ref[...] - t_ref[...]
        acc_ref[8:9, :] += _colsum(diff * diff) * (0.5 / D * latf)
        dxn = diff * (1.0 / D)
        acc_ref[1:2, :] += _colsum(dxn * xhat) * latf
        acc_ref[2:3, :] += _colsum(dxn) * latf
        dxhat = dxn * lg
        dpre = rstd * (dxhat - jnp.mean(dxhat, axis=-1, keepdims=True)
                       - xhat * jnp.mean(dxhat * xhat, axis=-1, keepdims=True))
        gx_ref[...] = ALPHA * dpre
        acc_ref[0:1, :] += _colsum(dpre * out) * latf
        dout = dpre * gt
        dy = _dot_nt(dout, wo_ref[...])
        dpa = dy * s7
        dpb = dy * s8
        dz7 = dy * pa * (s7 * (1.0 - s7))
        dz8 = dy * pb * (s8 * (1.0 - s8))
        doa = _dot_nt(dpa, pa_ref[...])
        dob = _dot_nt(dpb, pb_ref[...])
        dh_ref[...] = dob * silu6
        dz6 = dob * hsum * (s6 * (1.0 + z6 * (1.0 - s6)))
        dna = doa * silu4
        dz4 = doa * na * (s4 * (1.0 + z4 * (1.0 - s4)))
        dng = _colsum(dna * n)
        acc_ref[7:8, 0:DH] += sum(dng[:, h * DH:(h + 1) * DH] for h in range(H)) * latf
        dn = dna * gfull
        do_ref[...] = rinv * (dn - n * per_head(dn * n))
        acc_ref[3:4, :] += _colsum(dz4) * latf
        acc_ref[4:5, :] += _colsum(dz6) * latf
        acc_ref[5:6, :] += _colsum(dz7) * latf
        acc_ref[6:7, :] += _colsum(dz8) * latf
        dz4b, dz6b = (dz4 * latf).astype(BF), (dz6 * latf).astype(BF)
        dz4a_ref[...] = dz4b[:, :CUT4]
        dz4b_ref[...] = dz4b[:, CUT4:]
        dz6a_ref[...] = dz6b[:, :CUT6]
        sh3_ref[:, 0:D - CUT6] = dz6b[:, CUT6:]
        sh3_ref[:, D - CUT6:2 * D - CUT6] = (dz7 * latf).astype(BF)
        sh3_ref[:, 2 * D - CUT6:] = (dz8 * latf).astype(BF)
        y_ref[...] = y.astype(BF)
        dout_ref[...] = dout.astype(BF)
        oa_ref[...] = oa.astype(BF)
        dpa_ref[...] = dpa.astype(BF)
        obv_ref[...] = obv.astype(BF)
        dpb_ref[...] = dpb.astype(BF)

        @pl.when(i == nt - 1)
        def _():
            acc_ref[9:10, :] = jnp.broadcast_to(jnp.sum(acc_ref[8:9, :], axis=-1, keepdims=True), (1, D))

    m = z.shape[0]
    lrow = lambda i: jnp.minimum(i, nt - 1)
    row = pl.BlockSpec((tm, D), lambda i: (lrow(i), 0))
    allrow = lambda cols: pl.BlockSpec((tm, cols), lambda i: (i, 0))
    zs = lambda cb: pl.BlockSpec((tm, D), lambda i: (lrow(i), cb))
    full = lambda shp: pl.BlockSpec(shp, lambda i: (0,) * len(shp))
    wfull = pl.BlockSpec((D, D), lambda i: (0, 0), pipeline_mode=pl.Buffered(1))
    f32o = jax.ShapeDtypeStruct((t_lat, D), F32)
    bfo = jax.ShapeDtypeStruct((t_lat, D), BF)
    bfall = lambda cols: jax.ShapeDtypeStruct((m, cols), BF)
    return pl.pallas_call(
        body, grid=(m // tm,),
        in_specs=[row, row, row, row, zs(4), zs(6), zs(7), zs(8), row, row, full((16, 3 * D)), full((1, DH)),
                  full((1, D)), full((1, D)), wfull, wfull, wfull],
        out_specs=(row, row) + tuple(allrow(c) for c in (CUT4, D - CUT4, CUT6, SHC)) + (row,) * 7 + (full((16, D)),),
        out_shape=(f32o, f32o, bfall(CUT4), bfall(D - CUT4), bfall(CUT6), bfall(SHC), f32o, bfo, bfo, bfo, bfo, bfo, bfo,
                   jax.ShapeDtypeStruct((16, D), F32)),
        name="merge", compiler_params=_cp(("arbitrary",), VMEM_LIMIT_MERGE))(
            o_f, o_b, h_f, h_b, z, z, z, z, x, tgt, mod, norm_g, ln_g, ln_b, p_a, p_b, w_out)


def _wgrad(a, b, name):
    tm = 1024

    def body(a_ref, b_ref, o_ref):
        @pl.when(pl.program_id(0) == 0)
        def _():
            o_ref[...] = jnp.zeros_like(o_ref)
        o_ref[...] += _dot_tn(a_ref[...], b_ref[...])

    row = pl.BlockSpec((tm, D), lambda i: (i, 0))
    return pl.pallas_call(body, grid=(a.shape[0] // tm,), in_specs=[row, row],
                          out_specs=pl.BlockSpec((D, D), lambda i: (0, 0)),
                          out_shape=jax.ShapeDtypeStruct((D, D), F32), name=name,
                          compiler_params=_cp(("arbitrary",)))(a, b)


def _pack_shard2(dz4b, dz5_lat, dz5_ctx, dz6a):
    m, t_lat, t_ctx = dz4b.shape[0], dz5_lat.shape[0], dz5_ctx.shape[0]
    tm = t_ctx
    nlt = t_lat // tm
    w4 = D - CUT4

    def body(a_ref, bl_ref, bc_ref, c_ref, o_ref):
        i = pl.program_id(0)
        o_ref[:, 0:w4] = a_ref[...]
        o_ref[:, w4:w4 + D] = jnp.where(i < nlt, bl_ref[...], bc_ref[...]).astype(BF)
        o_ref[:, w4 + D:] = c_ref[...]

    return pl.pallas_call(
        body, grid=(m // tm,),
        in_specs=[pl.BlockSpec((tm, w4), lambda i: (i, 0)),
                  pl.BlockSpec((tm, D), lambda i: (jnp.minimum(i, nlt - 1), 0)),
                  pl.BlockSpec((tm, D), lambda i: (0, 0)),
                  pl.BlockSpec((tm, CUT6), lambda i: (i, 0))],
        out_specs=pl.BlockSpec((tm, SHC), lambda i: (i, 0)),
        out_shape=jax.ShapeDtypeStruct((m, SHC), BF), name="pack_shard2",
        compiler_params=_cp(("arbitrary",)))(dz4b, dz5_lat, dz5_ctx, dz6a)


def _wgrad_in(u_all, dz_shards, others):
    m = u_all.shape[1]
    assert m % (6 * 128) == 0
    tm = m // 6
    nt = m // tm
    no = len(others)
    spec_u, spec_dz = pl.BlockSpec((D, tm), lambda i: (0, i)), pl.BlockSpec((tm, SHC), lambda i: (i, 0))
    oshape = jax.ShapeDtypeStruct((NSH, D, SHC), F32)
    out = None
    for k, dz_k in enumerate(dz_shards[:-1]):
        def body(u_ref, dz_ref, *rest):
            o_ref = rest[-1]

            @pl.when(pl.program_id(0) == 0)
            def _():
                o_ref[...] = jnp.zeros_like(o_ref)
            o_ref[0] += _dot(u_ref[...], dz_ref[...])

        out = pl.pallas_call(
            body, grid=(nt,), in_specs=[spec_u, spec_dz] + ([] if out is None else [_ANY]),
            out_specs=pl.BlockSpec((1, D, SHC), lambda i, k=k: (k, 0, 0)), out_shape=oshape,
            input_output_aliases={} if out is None else {2: 0},
            name=f"wgrad_in{k}", compiler_params=_cp(("arbitrary",)))(u_all, dz_k, *(() if out is None else (out,)))

    def last(u_ref, dz_ref, done_ref, *rest):
        other_refs, o_ref, got_refs = rest[:no], rest[no], rest[no + 1:2 * no + 2]
        send, recv = rest[2 * no + 2:]
        i = pl.program_id(0)
        give = ([done_ref] + list(other_refs), got_refs, send, recv, [tuple(range(NSH - 1))] + [tuple(range(NSH))] * no)

        @pl.when(i == 0)
        def _():
            o_ref[...] = jnp.zeros_like(o_ref)
            for cp in _rs_sibling_copies(*give):
                cp.start()
        o_ref[0] += _dot(u_ref[...], dz_ref[...])

        @pl.when(i == nt - 1)
        def _():
            for cp in _rs_sibling_copies(*give):
                cp.wait()

    half = lambda g: jax.ShapeDtypeStruct((NSH, g.shape[1] // 2, g.shape[2]), F32)
    outs = pl.pallas_call(
        last, grid=(nt,), in_specs=[spec_u, spec_dz, _ANY] + [_ANY] * no,
        out_specs=[pl.BlockSpec((1, D, SHC), lambda i: (NSH - 1, 0, 0))] + [_ANY] * (no + 1),
        out_shape=[oshape, half(oshape)] + [half(g) for g in others], input_output_aliases={2: 0},
        scratch_shapes=[pltpu.SemaphoreType.DMA((NSH * (no + 1),)), pltpu.SemaphoreType.DMA((NSH * (no + 1),))],
        name=f"wgrad_in{NSH - 1}", compiler_params=_cp(("arbitrary",)))(u_all, dz_shards[-1], out, *others)
    return outs[0], list(outs[1:])


def _du(dz_shards, w_in_g, x, ctx, mod, gxres, sums=()):
    tm = 256
    n_lat_tiles, nct = x.shape[0] // tm, ctx.shape[0] // tm
    nt = n_lat_tiles + nct
    ns = len(sums)
    rblk = lambda i: jnp.where(i < nct, n_lat_tiles + i, i - nct)
    lblk = lambda i: jnp.maximum(i - nct, 0)

    def body(*refs):
        dz_refs, refs = refs[:NSH], refs[NSH:]
        w_ref, x_ref, c_ref, mod_ref, gr_ref = refs[:5]
        sum_refs = refs[5:5 + ns]
        gx_ref, dm_ref = refs[5 + ns:7 + ns]
        got_refs = refs[7 + ns:7 + 2 * ns]
        sems = refs[7 + 2 * ns:]
        i = pl.program_id(0)
        is_lat = i >= nct

        @pl.when(i == 0)
        def _():
            dm_ref[...] = jnp.zeros_like(dm_ref)
            if ns:
                for cp in _rs_chip_copies(sum_refs, got_refs, *sems):
                    cp.start()

        du = _dot_nt(dz_refs[0][...], w_ref[0])
        for n in range(1, NSH):
            du = du + _dot_nt(dz_refs[n][...], w_ref[n])
        sc = jnp.where(is_lat, mod_ref[0:1, D:2 * D], mod_ref[1:2, D:2 * D])
        dsh = _colsum(du)
        dsc = _colsum(du * jnp.where(is_lat, x_ref[...], c_ref[...]))

        @pl.when(is_lat)
        def _():
            gx_ref[...] = du * (1.0 + sc) + gr_ref[...]
            dm_ref[0:1, 0:D] += dsh
            dm_ref[0:1, D:2 * D] += dsc

        @pl.when(jnp.logical_not(is_lat))
        def _():
            dm_ref[1:2, 0:D] += dsh
            dm_ref[1:2, D:2 * D] += dsc

        if ns:
            @pl.when(i == nt - 1)
            def _():
                for cp in _rs_chip_copies(sum_refs, got_refs, *sems):
                    cp.wait()

    outs = pl.pallas_call(
        body, grid=(nt,),
        in_specs=[pl.BlockSpec((tm, SHC), lambda i: (rblk(i), 0))] * NSH + [
                  pl.BlockSpec((NSH, D, SHC), lambda i: (0, 0, 0), pipeline_mode=pl.Buffered(1)),
                  pl.BlockSpec((tm, D), lambda i: (lblk(i), 0)),
                  pl.BlockSpec((tm, D), lambda i: (jnp.minimum(i, nct - 1), 0)),
                  pl.BlockSpec((16, 3 * D), lambda i: (0, 0)),
                  pl.BlockSpec((tm, D), lambda i: (lblk(i), 0))] + [_ANY] * ns,
        out_specs=[pl.BlockSpec((tm, D), lambda i: (lblk(i), 0)),
                   pl.BlockSpec((8, 2 * D), lambda i: (0, 0))] + [_ANY] * ns,
        out_shape=[jax.ShapeDtypeStruct((n_lat_tiles * tm, D), F32), jax.ShapeDtypeStruct((8, 2 * D), F32)]
        + [jax.ShapeDtypeStruct((3,) + g.shape[1:], g.dtype) for g in sums],
        scratch_shapes=[pltpu.SemaphoreType.DMA((3 * ns,)), pltpu.SemaphoreType.DMA((3 * ns,))] if ns else [],
        name="du", compiler_params=_cp(("arbitrary",)))(*dz_shards, w_in_g, x, ctx, mod, gxres, *sums)
    return outs[0], outs[1], list(outs[2:])


def _row_tile(rows, cols, block_bytes=1 << 20):
    t = 8
    while t * 2 * cols * 4 <= block_bytes and rows % (t * 2) == 0:
        t *= 2
    return t


def _adamw_update(w_ref, g_ref, m_ref, v_ref, d_ref, nm_ref, nv_ref):
    gg = g_ref[...]
    m2 = ADAM_B1 * m_ref[...] + (1.0 - ADAM_B1) * gg
    v2 = ADAM_B2 * v_ref[...] + (1.0 - ADAM_B2) * (gg * gg)
    m_hat = m2 / (1.0 - ADAM_B1 ** ADAM_STEP)
    v_hat = v2 / (1.0 - ADAM_B2 ** ADAM_STEP)
    d_ref[...] = -ADAM_LR * (m_hat / (jnp.sqrt(v_hat) + ADAM_EPS) + ADAM_WD * w_ref[...])
    nm_ref[...] = m2
    nv_ref[...] = v2


def _adamw_many(ws, gs, ms, vs):
    n = len(ws)

    def body(*refs):
        for j in range(n):
            _adamw_update(*refs[4 * j:4 * j + 4], *refs[4 * n + 3 * j:4 * n + 3 * j + 3])

    args = [a for quad in zip(ws, gs, ms, vs) for a in quad]
    outs = pl.pallas_call(body, out_shape=[jax.ShapeDtypeStruct(w.shape, F32) for w in ws for _ in range(3)],
                          name="adamw_small", compiler_params=_cp())(*args)
    return outs[0::3], outs[1::3], outs[2::3]


def _adamw(ws, gs, ms, vs, name):
    n = len(ws)
    rows, cols = ws[0].shape
    tr = _row_tile(rows, cols, (2 << 20) // n)

    def body(*refs):
        for j in range(n):
            _adamw_update(*refs[4 * j:4 * j + 4], *refs[4 * n + 3 * j:4 * n + 3 * j + 3])

    spec = pl.BlockSpec((tr, cols), lambda i: (i, 0))
    o = jax.ShapeDtypeStruct((rows, cols), F32)
    args = [a for quad in zip(ws, gs, ms, vs) for a in quad]
    outs = pl.pallas_call(body, grid=(rows // tr,), in_specs=[spec] * (4 * n), out_specs=[spec] * (3 * n),
                          out_shape=[o] * (3 * n), name=name, compiler_params=_cp(("arbitrary",)))(*args)
    return outs[0::3], outs[1::3], outs[2::3]


_ANY = pl.BlockSpec(memory_space=pl.ANY)


def _place():
    return lax.axis_index("x"), lax.axis_index("y"), lax.axis_index("c")


def _ag_copies(ins, outs, send, recv, fsend, frecv, lsem):
    x, y, c = _place()
    me = 2 * x + y
    chips = ((1 - x, y), (x, 1 - y), (1 - x, 1 - y))
    local, chip, hand = [], [], []
    for j in range(len(ins)):
        hr = ins[j].shape[0] // 2
        half = pl.ds(pl.multiple_of(c * hr, 8), hr)
        local.append(pltpu.make_async_copy(ins[j], outs[j].at[me], lsem.at[j]))
        for k, (px, py) in enumerate(chips):
            chip.append(pltpu.make_async_remote_copy(
                src_ref=ins[j].at[half, :], dst_ref=outs[j].at[me, half, :], send_sem=send.at[3 * j + k],
                recv_sem=recv.at[3 * j + k], device_id=(px, py, c), device_id_type=MESH))
            got = outs[j].at[2 * px + py, half, :]
            hand.append(pltpu.make_async_remote_copy(
                src_ref=got, dst_ref=got, send_sem=fsend.at[3 * j + k], recv_sem=frecv.at[3 * j + k],
                device_id=(x, y, 1 - c), device_id_type=MESH))
    return local, chip, hand


def _ag_sems(n):
    return [pltpu.SemaphoreType.DMA((3 * n,))] * 4 + [pltpu.SemaphoreType.DMA((n,))]


def _ag_finish(local, chip, hand, done=0):
    for k in range(done, len(chip)):
        chip[k].wait_recv()
        hand[k].start()
    for cp in chip:
        cp.wait_send()
    for k in range(done):
        hand[k].wait_send()
    for k in range(done, len(chip)):
        hand[k].wait_send()
        hand[k].wait_recv()
    for cp in local:
        cp.wait()


def _mod_tp(c8, c_ctx, w_mod_sh, b_mod_sh):
    mc = w_mod_sh.shape[1]

    def body(c8_ref, cctx_ref, w_ref, b_ref, mod_ref, cc_ref, cc_s, part_s, send1, recv1, send3, recv3):
        x, y, c = _place()
        me = 4 * x + 2 * y + c
        ms = 2 * x + y
        copies = []
        for k in range(1, 8):
            peer = (x ^ ((k >> 2) & 1), y ^ ((k >> 1) & 1), c ^ (k & 1))
            cp = pltpu.make_async_remote_copy(src_ref=c8_ref, dst_ref=cc_s.at[me], send_sem=send1.at[k],
                                              recv_sem=recv1.at[k], device_id=peer, device_id_type=MESH)
            cp.start()
            copies.append(cp)
        cc_s[me] = c8_ref[...]
        for cp in copies:
            cp.wait()
        cc_ref[...] = jnp.zeros_like(cc_ref)
        for j in range(8):
            cc_ref[j:j + 1, :] = cc_s[j, 0:1, :]
        cc_ref[8:9, :] = cctx_ref[...]
        v = cc_ref[...]
        part_s[ms] = _dot(v * _sig(v), w_ref[...]) + b_ref[...]
        copies = []
        for k in range(1, 4):
            peer = (x ^ ((k >> 1) & 1), y ^ (k & 1), c)
            cp = pltpu.make_async_remote_copy(src_ref=part_s.at[ms], dst_ref=part_s.at[ms], send_sem=send3.at[k],
                                              recv_sem=recv3.at[k], device_id=peer, device_id_type=MESH)
            cp.start()
            copies.append(cp)
        for cp in copies:
            cp.wait()
        for s in range(NSH):
            mod_ref[:, s * mc:(s + 1) * mc] = part_s[s]

    vm = pl.BlockSpec(memory_space=pltpu.VMEM)
    return pl.pallas_call(
        body, in_specs=[vm] * 4, out_specs=(vm, vm),
        out_shape=(jax.ShapeDtypeStruct((16, NSH * mc), F32), jax.ShapeDtypeStruct((16, D), F32)),
        scratch_shapes=[pltpu.VMEM((8, 8, D), F32), pltpu.VMEM((NSH, 16, mc), F32),
                        pltpu.SemaphoreType.DMA((8,)), pltpu.SemaphoreType.DMA((8,)),
                        pltpu.SemaphoreType.DMA((4,)), pltpu.SemaphoreType.DMA((4,))],
        name="mod_tp", compiler_params=_cp())(c8, c_ctx, w_mod_sh, b_mod_sh)


def _inproj_ag(x, ctx, mod, w_in_sh, b_in, narrow_sh):
    n_lat = x.shape[0]
    m = n_lat + ctx.shape[0]
    assert m % (11 * 128) == 0
    tm = m // 11
    nt = m // tm
    nfull = n_lat // tm
    assert nfull == nt - 1
    tail = jnp.concatenate([x[nfull * tm:], ctx], axis=0)
    x_, y_ = lax.axis_index("x"), lax.axis_index("y")
    sids = jnp.stack([2 * x_ + y_, 2 * (1 - x_) + y_, 2 * x_ + 1 - y_, 2 * (1 - x_) + 1 - y_]).astype(jnp.int32)

    def body(sid_ref, x_ref, tail_ref, mod_ref, b_ref, wsh_ref, nsh_ref, z_ref, u_ref, wg_ref, ng_ref, w_s, u_s, *sems):
        n = pl.program_id(0)
        i = pl.program_id(1)
        rows = pl.ds(pl.multiple_of(i * tm, tm), tm)
        ag = ((wsh_ref, nsh_ref), (wg_ref, ng_ref)) + tuple(sems[:5])
        wsem = sems[5]

        def load(src):
            cp = pltpu.make_async_copy(src, w_s, wsem)
            cp.start()
            cp.wait()

        @pl.when((n == 0) & (i == 0))
        def _():
            local, chip, _ = _ag_copies(*ag)
            for cp in chip + local:
                cp.start()
            load(wsh_ref)

        for k in range(NSH - 1):
            @pl.when((n == k + 1) & (i == 0))
            def _():
                _, chip, hand = _ag_copies(*ag)
                chip[k].wait_recv()
                hand[k].start()
                hand[k].wait_recv()
                load(wg_ref.at[sid_ref[k + 1]])

        @pl.when(n == 0)
        def _():
            is_lat = (i * tm + lax.broadcasted_iota(jnp.int32, (tm, 1), 0)) < n_lat
            u = _modulate(jnp.where(i < nfull, x_ref[...], tail_ref[...]), mod_ref, is_lat)
            u_s[rows, :] = u.astype(BF)
            u_ref[...] = u.T.astype(BF)

        z_ref[...] = _dot(u_s[rows, :], w_s[...]) + b_ref[...]

        @pl.when((n == NSH - 1) & (i == nt - 1))
        def _():
            _ag_finish(*_ag_copies(*ag), done=NSH - 1)

    first = lambda n, i: jnp.where(n == 0, i, nt - 1)
    outs = pl.pallas_call(
        body, grid_spec=pltpu.PrefetchScalarGridSpec(
            num_scalar_prefetch=1, grid=(NSH, nt),
            in_specs=[pl.BlockSpec((tm, D), lambda n, i, sid: (jnp.minimum(first(n, i), nfull - 1), 0)),
                      pl.BlockSpec((tm, D), lambda n, i, sid: (0, 0)),
                      pl.BlockSpec((16, 3 * D), lambda n, i, sid: (0, 0)),
                      pl.BlockSpec((1, SHC), lambda n, i, sid: (0, sid[n])), _ANY, _ANY],
            out_specs=[pl.BlockSpec((tm, SHC), lambda n, i, sid: (i, sid[n])),
                       pl.BlockSpec((D, tm), lambda n, i, sid: (0, first(n, i))), _ANY, _ANY],
            scratch_shapes=[pltpu.VMEM((D, SHC), BF), pltpu.VMEM((m, D), BF)] + _ag_sems(2)
            + [pltpu.SemaphoreType.DMA]),
        out_shape=[jax.ShapeDtypeStruct((m, IN_COLS), F32), jax.ShapeDtypeStruct((D, m), BF),
                   jax.ShapeDtypeStruct((NSH,) + w_in_sh.shape, BF),
                   jax.ShapeDtypeStruct((NSH,) + narrow_sh.shape, narrow_sh.dtype)],
        name="inproj_ag", compiler_params=_cp(("arbitrary", "arbitrary")))(sids, x, tail, mod, b_in, w_in_sh, narrow_sh)
    return outs


def _rs_sibling_copies(ins, got, send, recv, shards):
    x, y, c = _place()
    copies = []
    for j in range(len(ins)):
        hr = ins[j].shape[1] // 2
        for s in shards[j]:
            give = ins[j].at[s, pl.ds(pl.multiple_of((1 - c) * hr, 8), hr), :]
            copies.append(pltpu.make_async_remote_copy(
                src_ref=give, dst_ref=got[j].at[s], send_sem=send.at[NSH * j + s], recv_sem=recv.at[NSH * j + s],
                device_id=(x, y, 1 - c), device_id_type=MESH))
    return copies


def _core_vec():
    return lax.axis_index("c").astype(jnp.int32).reshape(1)


def _rs_add1(gs, gots, name, last=()):
    n = len(gs)
    _, r, cols = gs[0].shape
    hr = r // 2
    tr = _row_tile(hr, cols, (4 << 20) // n)
    nb = hr // tr

    def body(c_ref, *refs):
        for j in range(n):
            refs[2 * n + len(last) + j][...] = (refs[j][...] + refs[n + j][...]).astype(BF)
        if last:
            send, recv = refs[-2:]
            copies = _rs_sibling_copies([refs[2 * n]], [refs[-3]], send, recv, [(NSH - 1,)])
            first = (pl.program_id(0) == 0) & (pl.program_id(1) == 0)
            final = (pl.program_id(0) == NSH - 1) & (pl.program_id(1) == nb - 1)

            @pl.when(first)
            def _():
                for cp in copies:
                    cp.start()

            @pl.when(final)
            def _():
                for cp in copies:
                    cp.wait()

    spec = pl.BlockSpec((1, tr, cols), lambda s, i, c_ref: (s, i, 0))
    keep = pl.BlockSpec((1, tr, cols), lambda s, i, c_ref: (s, c_ref[0] * nb + i, 0))
    extra = [_ANY] * len(last)
    return pl.pallas_call(
        body, grid_spec=pltpu.PrefetchScalarGridSpec(
            num_scalar_prefetch=1, grid=(NSH, nb), in_specs=[keep] * n + [spec] * n + extra,
            out_specs=[spec] * n + extra[:1],
            scratch_shapes=[pltpu.SemaphoreType.DMA((NSH,))] * len(last)),
        out_shape=[jax.ShapeDtypeStruct((NSH, hr, cols), BF)] * n
        + [jax.ShapeDtypeStruct(a.shape, a.dtype) for a in last[1:]],
        input_output_aliases={2 * n + 2: n} if last else {}, name=name,
        compiler_params=_cp(("arbitrary", "arbitrary")))(_core_vec(), *gs, *gots, *last)


def _rs_add2(sums, gots, name):
    n = len(sums)
    _, hr, cols = sums[0].shape
    tr = _row_tile(hr, cols, (2 << 20) // n)
    nb = hr // tr
    place = jnp.stack([2 * lax.axis_index("x") + lax.axis_index("y"), lax.axis_index("c")]).astype(jnp.int32)

    def body(p_ref, *refs):
        f = lambda v: v.astype(F32)
        for j in range(n):
            s_ref, got_ref = refs[j], refs[n + j]
            refs[2 * n + j][...] = f(s_ref[0]) + f(got_ref[0]) + f(got_ref[1]) + f(got_ref[2])

    return pl.pallas_call(
        body, grid_spec=pltpu.PrefetchScalarGridSpec(
            num_scalar_prefetch=1, grid=(nb,),
            in_specs=[pl.BlockSpec((1, tr, cols), lambda i, p_ref: (p_ref[0], i, 0))] * n
            + [pl.BlockSpec((3, tr, cols), lambda i, p_ref: (0, i, 0))] * n,
            out_specs=[pl.BlockSpec((tr, cols), lambda i, p_ref: (p_ref[1] * nb + i, 0))] * n),
        out_shape=[jax.ShapeDtypeStruct((2 * hr, cols), F32)] * n, name=name,
        compiler_params=_cp(("arbitrary",)))(place, *sums, *gots)


def _rs_chip_copies(ins, got, send, recv):
    x, y, c = _place()
    peers = ((1 - x, y), (x, 1 - y), (1 - x, 1 - y))
    return [pltpu.make_async_remote_copy(src_ref=ins[j].at[2 * px + py], dst_ref=got[j].at[k],
                                         send_sem=send.at[3 * j + k], recv_sem=recv.at[3 * j + k],
                                         device_id=(px, py, c), device_id_type=MESH)
            for j in range(len(ins)) for k, (px, py) in enumerate(peers)]


_AG_CHUNKS = 4


def _ag_sibling_copies(outs, send, recv):
    x, y, c = _place()
    copies = []
    for j, out in enumerate(outs):
        qr = out.shape[0] // (2 * _AG_CHUNKS)
        for k in range(_AG_CHUNKS):
            rows = out.at[pl.ds(pl.multiple_of((c * _AG_CHUNKS + k) * qr, 8), qr), :]
            copies.append(pltpu.make_async_remote_copy(
                src_ref=rows, dst_ref=rows, send_sem=send.at[_AG_CHUNKS * j + k], recv_sem=recv.at[_AG_CHUNKS * j + k],
                device_id=(x, y, 1 - c), device_id_type=MESH))
    return copies


def _allreduce_small(buf, fulls):
    rows = buf.shape[0]
    pr = rows // 8
    n = len(fulls)

    def body(*refs):
        in_ref, out_ref = refs[0], refs[1 + n]
        part, stage_a, stage_b = refs[2 + 2 * n:5 + 2 * n]
        (sa, ra, sb, rb, sc, rc, sd, rd, send, recv) = refs[5 + 2 * n:]
        x, y, c = _place()
        s = 2 * x + y
        sib = (x, y, 1 - c)
        chips = ((1 - x, y), (x, 1 - y), (1 - x, 1 - y))
        swaps = _ag_sibling_copies(refs[2 + n:2 + 2 * n], send, recv)
        for cp in swaps:
            cp.start()

        def piece(ref, chip, core):
            return ref.at[pl.ds(pl.multiple_of((2 * chip + core) * pr, 8), pr), :]

        def run(copies):
            for cp in copies:
                cp.start()
            for cp in copies:
                cp.wait()

        run([pltpu.make_async_remote_copy(src_ref=piece(in_ref, j, 1 - c), dst_ref=stage_a.at[j], send_sem=sa.at[j],
                                          recv_sem=ra.at[j], device_id=sib, device_id_type=MESH) for j in range(NSH)])
        for j in range(NSH):
            part[j] = piece(in_ref, j, c)[...] + stage_a[j]
        run([pltpu.make_async_remote_copy(src_ref=part.at[2 * px + py], dst_ref=stage_b.at[k], send_sem=sb.at[k],
                                          recv_sem=rb.at[k], device_id=(px, py, c), device_id_type=MESH)
             for k, (px, py) in enumerate(chips)])
        piece(out_ref, s, c)[...] = part[s] + stage_b[0] + stage_b[1] + stage_b[2]
        run([pltpu.make_async_remote_copy(src_ref=piece(out_ref, s, c), dst_ref=piece(out_ref, s, c),
                                          send_sem=sc.at[k], recv_sem=rc.at[k], device_id=(px, py, c),
                                          device_id_type=MESH) for k, (px, py) in enumerate(chips)])
        run([pltpu.make_async_remote_copy(src_ref=piece(out_ref, j, c), dst_ref=piece(out_ref, j, c), send_sem=sd.at[j],
                                          recv_sem=rd.at[j], device_id=sib, device_id_type=MESH) for j in range(NSH)])
        for cp in swaps:
            cp.wait()

    vm = pl.BlockSpec(memory_space=pltpu.VMEM)
    res = pl.pallas_call(
        body, in_specs=[vm] + [_ANY] * n, out_specs=[vm] + [_ANY] * n,
        out_shape=[jax.ShapeDtypeStruct((rows, D), F32)] + [jax.ShapeDtypeStruct(f.shape, F32) for f in fulls],
        input_output_aliases={1 + j: 1 + j for j in range(n)},
        scratch_shapes=[pltpu.VMEM((NSH, pr, D), F32), pltpu.VMEM((NSH, pr, D), F32), pltpu.VMEM((3, pr, D), F32)]
        + [pltpu.SemaphoreType.DMA((NSH,))] * 8 + [pltpu.SemaphoreType.DMA((_AG_CHUNKS * n,))] * 2,
        name="allreduce_small", compiler_params=_cp())(buf, *fulls)
    return res[0], res[1:]


def _pad_rows(a, mult):
    pad = (-a.shape[0]) % mult
    return jnp.concatenate([a, jnp.zeros((pad, a.shape[1]), a.dtype)]) if pad else a


def _local_step(x, c, ctx, c_ctx, tgt, me, shard, sh, b_mod, b_in, norm_g, cb, wr, wi, ln_g, ln_b):
    t_lat, t_ctx = x.shape[0], ctx.shape[0]
    nlb, ncb = t_lat // RB, t_ctx // RB
    mc = 3 * D // NSH
    mod_all, cc_all = _mod_tp(jnp.zeros((8, D), F32).at[0].set(c), c_ctx.reshape(1, D), sh["w_mod"],
                              lax.dynamic_slice_in_dim(b_mod, shard * mc, mc, axis=1))
    mod = jnp.zeros((16, 3 * D), F32).at[0].set(mod_all[me]).at[1].set(mod_all[8])
    cc = jnp.zeros((16, D), F32).at[0].set(c).at[1].set(c_ctx)
    z, u_all, w_in_g, nar = _inproj_ag(x, ctx, mod, sh["w_in"], b_in, sh["narrow"])
    nar = jnp.transpose(nar, (1, 0, 2)).reshape(-1, D)
    lbl, cw, br, bi, lam = nar[0:4].reshape(2, 2, D), nar[4:8], nar[8:10], nar[10:12], nar[12:14]
    o0, st0, (w_mod_g, p_a, p_b, w_out) = _gla_fwd(z, lbl, 0, nlb, ncb,
                                                   gather=[sh[k] for k in ("w_mod", "p_a", "p_b", "w_out")])
    p_a, p_b, w_out = p_a.reshape(D, D), p_b.reshape(D, D), w_out.reshape(D, D)
    o1, st1, _ = _gla_fwd(z, lbl, 1, nlb, ncb)
    h0, hp0, a0, hpc0 = _rglru_fwd(z, cw, cb, wr, br, wi, bi, lam, 0, t_lat, t_ctx)
    h1, hp1, a1, hpc1 = _rglru_fwd(z, cw, cb, wr, br, wi, bi, lam, 1, t_lat, t_ctx)
    (do, dh, dz4a, dz4b, dz6a, dz_sh3, gxres, y, dout, oa, dpa, obv, dpb, acc) = _merge(
        o0, o1, h0, h1, z, x, tgt, mod, norm_g, ln_g, ln_b, p_a, p_b, w_out, t_lat)
    gp_a = _wgrad(oa, dpa, "wgrad_pa")
    gp_b = _wgrad(obv, dpb, "wgrad_pb")
    gw_out = _wgrad(y, dout, "wgrad_wout")
    dxc_lat, dxc_ctx, dwr0, dwi0, sb0 = _rglru_bwd(z, cw, cb, wr, br, wi, bi, lam, dh, hp0, a0, hpc0, 0, t_lat, t_ctx)
    dz5_lat, dz5_ctx, dwr1, dwi1, sb1 = _rglru_bwd(z, cw, cb, wr, br, wi, bi, lam, dh, hp1, a1, hpc1, 1, t_lat, t_ctx,
                                                   prev=(dxc_lat, dxc_ctx))
    dq0, dv0, dz1, sa0 = _gla_bwd(z, lbl, do, st0, 0, nlb, ncb)
    dz_sh0, dz_sh1, sa1 = _gla_bwd(z, lbl, do, st1, 1, nlb, ncb, prev=(dq0, dv0, dz1, dz4a))
    dz = (dz_sh0, dz_sh1, _pack_shard2(dz4b, dz5_lat, dz5_ctx, dz6a), dz_sh3)
    g_p = [g.reshape(NSH, D // NSH, D) for g in (gp_a, gp_b, gw_out)]
    gw_in, got = _wgrad_in(u_all, dz, g_p)
    grads = [gw_in] + g_p
    *sums_p, got[0] = _rs_add1(grads[1:], got[1:], "rs_add1_p", last=(gw_in, got[0]))
    sums = list(_rs_add1(grads[:1], got[:1], "rs_add1_w")) + sums_p
    gx, dm, got = _du(dz, w_in_g, x, ctx, mod, gxres, sums)
    fulls = list(_rs_add2(sums[:1], got[:1], "rs_add2_w")) + list(_rs_add2(sums[1:], got[1:], "rs_add2_p"))
    dmod =jnp.zeros((16, 3 * D), F32).at[0:2, 0:2 * D].set(dm[0:2]).at[0, 2 * D:].set(acc[0])
    dcc = _mod_bwd(cc, dmod, w_mod_g)
    small = dict(
        c_ctx=dcc[1:2], b_mod=(dmod[0] + dmod[1]).reshape(3, D),
        b_in=jnp.stack([sa1[2], sa0[0], sa1[0], sa1[3], acc[3], sb1[4], acc[4], acc[5], acc[6]]),
        lb_logits=jnp.stack([sa0[1], sa1[1], -sa0[1], -sa1[1]]),
        norm_a_g=acc[7:8], conv_w=sb1[8:12], conv_b=sb1[3:4],
        w_r=jnp.stack([dwr0, dwr1]).reshape(-1, D), w_i=jnp.stack([dwi0, dwi1]).reshape(-1, D),
        b_r=jnp.stack([sb0[0], sb1[0]]), b_i=jnp.stack([sb0[1], sb1[1]]), lam=jnp.stack([sb0[2], sb1[2]]),
        ln_g=acc[1:2], ln_b=acc[2:3])
    return acc[9, 0], gx, fulls, small, dmod, cc_all


_RS =("w_in", "p_a", "p_b", "w_out")
_SMALL =("c_ctx", "b_mod", "b_in", "lb_logits", "norm_a_g", "conv_w", "conv_b", "w_r", "w_i", "b_r", "b_i", "lam",
          "ln_g", "ln_b")
_BIG = ("w_mod", "w_in", "p_a", "p_b", "w_out")
_COL_SHARDED = ("lb_logits", "conv_w", "b_r", "b_i", "lam")
_WEIGHTS = ("c_ctx", "w_mod", "b_mod", "w_in", "b_in", "lb_logits", "norm_a_g", "conv_w", "conv_b", "w_r", "b_r", "w_i",
            "b_i", "lam", "p_a", "p_b", "w_out", "ln_g", "ln_b")


def kernel(x, c, ctx, c_ctx, w_mod, b_mod, w_in, b_in, lb_logits, norm_a_g, conv_w, conv_b, w_r, b_r, w_i, b_i, lam, p_a, p_b, w_out, ln_g, ln_b, loss_target, m_c_ctx, m_w_mod, m_b_mod, m_w_in, m_b_in, m_lb_logits, m_norm_a_g, m_conv_w, m_conv_b, m_w_r, m_b_r, m_w_i, m_b_i, m_lam, m_p_a, m_p_b, m_w_out, m_ln_g, m_ln_b, v_c_ctx, v_w_mod, v_b_mod, v_w_in, v_b_in, v_lb_logits, v_norm_a_g, v_conv_w, v_conv_b, v_w_r, v_b_r, v_w_i, v_b_i, v_lam, v_p_a, v_p_b, v_w_out, v_ln_g, v_ln_b):
    w = dict(c_ctx=c_ctx, w_mod=w_mod, b_mod=b_mod, w_in=w_in, b_in=b_in, lb_logits=lb_logits, norm_a_g=norm_a_g,
             conv_w=conv_w, conv_b=conv_b, w_r=w_r, b_r=b_r, w_i=w_i, b_i=b_i, lam=lam, p_a=p_a, p_b=p_b, w_out=w_out,
             ln_g=ln_g, ln_b=ln_b)
    m = dict(c_ctx=m_c_ctx, w_mod=m_w_mod, b_mod=m_b_mod, w_in=m_w_in, b_in=m_b_in, lb_logits=m_lb_logits,
             norm_a_g=m_norm_a_g, conv_w=m_conv_w, conv_b=m_conv_b, w_r=m_w_r, b_r=m_b_r, w_i=m_w_i, b_i=m_b_i,
             lam=m_lam, p_a=m_p_a, p_b=m_p_b, w_out=m_w_out, ln_g=m_ln_g, ln_b=m_ln_b)
    v = dict(c_ctx=v_c_ctx, w_mod=v_w_mod, b_mod=v_b_mod, w_in=v_w_in, b_in=v_b_in, lb_logits=v_lb_logits,
             norm_a_g=v_norm_a_g, conv_w=v_conv_w, conv_b=v_conv_b, w_r=v_w_r, b_r=v_b_r, w_i=v_w_i, b_i=v_b_i,
             lam=v_lam, p_a=v_p_a, p_b=v_p_b, w_out=v_w_out, ln_g=v_ln_g, ln_b=v_ln_b)
    shard = 2 * lax.axis_index("x") + lax.axis_index("y")
    cs = D // NSH

    sh = {k: w[k][0].astype(BF) for k in _BIG}
    sh["narrow"] = _pad_rows(jnp.concatenate([lb_logits.reshape(4, cs), conv_w[0], b_r[0], b_i[0], lam[0]], axis=0), 8)
    me = 2 * shard + lax.axis_index("c")
    loss, gx, fulls, small, dmod, cc_all = _local_step(
        x[0], c[0], ctx[0], c_ctx, loss_target[0], me, shard, sh, b_mod, b_in, norm_a_g, conv_b, w_r[0], w_i[0],
        ln_g, ln_b)

    dmod_rows = jnp.zeros((16, 3 * D), F32).at[me].set(dmod[0]).at[8].set(dmod[1]).reshape(48, D)
    loss_rows = jnp.zeros((8, D), F32).at[0, 0].set(loss)
    sizes = [small[k].shape[0] for k in _SMALL]
    red, fulls = _allreduce_small(_pad_rows(jnp.concatenate(
        [_pad_rows(small[k], 8) for k in _SMALL] + [dmod_rows, loss_rows], axis=0), 64), fulls)
    big = dict(zip(_RS, fulls))
    grads = {}
    off = 0
    for k, n in zip(_SMALL, sizes):
        g = red[off:off + n]
        off += n + (-n) % 8
        if k == "norm_a_g":
            g = g[:, :DH]
        if k in _COL_SHARDED:
            g = lax.dynamic_slice_in_dim(g, shard * cs, cs, axis=1)
        grads[k] = g.reshape(w[k].shape)
    for k in _RS:
        grads[k] = big[k].reshape(w[k].shape)
    dmod_all = red[off:off + 48].reshape(16, 3 * D)
    loss = red[off + 48, 0]
    mc = 3 * D // NSH
    grads["w_mod"] = _wmod_grad(cc_all, lax.dynamic_slice_in_dim(dmod_all, shard * mc, mc, axis=1)).reshape(
        w["w_mod"].shape)

    delta, new_m, new_v = {}, {}, {}
    for group in (("w_mod",), ("w_in",), ("p_a", "p_b", "w_out")):
        two = lambda t: [t[k].reshape(t[k].shape[-2:]) for k in group]
        d_, m_, v_ = _adamw(two(w), two(grads), two(m), two(v), "adamw_" + group[0])
        for j, k in enumerate(group):
            shp = w[k].shape
            delta[k], new_m[k], new_v[k] = d_[j].reshape(shp), m_[j].reshape(shp), v_[j].reshape(shp)
    d_, m_, v_ = _adamw_many(*[[t[k] for k in _SMALL] for t in (w, grads, m, v)])
    delta.update(zip(_SMALL, d_))
    new_m.update(zip(_SMALL, m_))
    new_v.update(zip(_SMALL, v_))

    return (loss, gx[None], *[grads[k] for k in _WEIGHTS], *[delta[k] for k in _WEIGHTS],
            *[new_m[k] for k in _WEIGHTS], *[new_v[k] for k in _WEIGHTS])
```

```python
import jax
import jax.numpy as jnp
from jax import lax
from jax.experimental import pallas as pl
from jax.experimental.pallas import tpu as pltpu

F32 = jnp.float32
BF = jnp.bfloat16
MESH = pl.DeviceIdType.MESH

D = 1024
H = 8
DH = 128
CH = 64
RB = 256
NCK = RB // CH
GRID_W = 64
CB = 8
RCH = 16
IN_COLS = 9 * D
NSH = 4
SHC = IN_COLS // NSH
MC = 3 * D // NSH
CUT2 = SHC - 2 * D
CUT4 = 2 * SHC - 4 * D
CUT6 = 3 * SHC - 6 * D
RG_C = 8.0
ALPHA = 2.0 ** 0.25
LN_EPS = 1e-5
RMS_EPS = 1e-6
Q_SCALE = DH ** -0.5
ADAM_LR, ADAM_B1, ADAM_B2, ADAM_EPS, ADAM_WD, ADAM_STEP = 0.001, 0.9, 0.999, 1e-08, 0.01, 10
VMEM_LIMIT = 56 * 1024 * 1024
VMEM_LIMIT_MERGE = 60 * 1024 * 1024
VMEM_LIMIT_RGLRU_BWD = 63 * 1024 * 1024


def _cp(sem=None, vmem=VMEM_LIMIT):
    return pltpu.CompilerParams(dimension_semantics=sem, vmem_limit_bytes=vmem)


def _sig(x):
    return 0.5 * jnp.tanh(0.5 * x) + 0.5


def _sig_tail(x):
    return 1.0 / (1.0 + jnp.exp(-x))


def _dot(a, b):
    return jnp.dot(a.astype(BF), b.astype(BF), preferred_element_type=F32)


def _dot_nt(a, b):
    return lax.dot_general(a.astype(BF), b.astype(BF), (((1,), (1,)), ((), ())), preferred_element_type=F32)


def _dot_tn(a, b):
    return lax.dot_general(a.astype(BF), b.astype(BF), (((0,), (0,)), ((), ())), preferred_element_type=F32)


def _colsum(v):
    return jnp.sum(v, axis=0, keepdims=True)


def _mod_bwd(cc, dmod, w_mod_g):
    def body(cc_ref, dm_ref, w_ref, dcc_ref):
        v = cc_ref[...]
        sg = _sig(v)
        ds = jnp.zeros((16, D), F32)
        for k in range(NSH):
            ds = ds + _dot_nt(dm_ref[:, k * MC:(k + 1) * MC], w_ref[k])
        dcc_ref[...] = ds * (sg * (1.0 + v * (1.0 - sg)))
    return pl.pallas_call(body, out_shape=jax.ShapeDtypeStruct((16, D), F32),
                          name="mod_bwd", compiler_params=_cp())(cc, dmod, w_mod_g)


def _wmod_grad(cc, dmod_cols):
    def body(cc_ref, dm_ref, dw_ref):
        v = cc_ref[...]
        dw_ref[...] = _dot_tn(v * _sig(v), dm_ref[...])
    return pl.pallas_call(body, out_shape=jax.ShapeDtypeStruct((D, dmod_cols.shape[1]), F32),
                          name="wmod_grad", compiler_params=_cp())(cc, dmod_cols)


def _modulate(xv, mod_ref, is_lat):
    sh = jnp.where(is_lat, mod_ref[0:1, 0:D], mod_ref[1:2, 0:D])
    sc = jnp.where(is_lat, mod_ref[0:1, D:2 * D], mod_ref[1:2, D:2 * D])
    return xv * (1.0 + sc) + sh


def _gla_mask(d, n):
    row = lax.broadcasted_iota(jnp.int32, (n, n), 0)
    col = lax.broadcasted_iota(jnp.int32, (n, n), 1)
    same = (row // CH) == (col // CH)
    return same & ((row <= col) if d else (row >= col))


def _chunk_cumsum(v, rev):
    n = v.shape[0]
    pos = lax.broadcasted_iota(jnp.int32, v.shape, 0) & (CH - 1)
    s = 1
    while s < CH:
        if rev:
            v = v + jnp.where(pos < CH - s, pltpu.roll(v, n - s, 0), 0.0)
        else:
            v = v + jnp.where(pos >= s, pltpu.roll(v, s, 0), 0.0)
        s *= 2
    return v


def _chunk_rows(c):
    return slice(c * CH, (c + 1) * CH)


def _gla_features(zq, zf, lb, d):
    sq = _sig(zq)
    q = zq * sq * Q_SCALE
    sf = _sig(zf)
    f = lb + (1.0 - lb) * sf
    k = 1.0 - f
    g = _chunk_cumsum(jnp.log(f), d)
    last = 0 if d else CH - 1
    gls = [g[c * CH + last:c * CH + last + 1, :] for c in range(NCK)]
    glb = jnp.concatenate([jnp.broadcast_to(gl, (CH, D)) for gl in gls], axis=0)
    eg, eig, eeg = jnp.exp(g), jnp.exp(-g), jnp.exp(glb - g)
    decs = [jnp.exp(gl) for gl in gls]
    return sq, sf, f, q * eg, k * eig, k * eeg, eg, eig, eeg, decs


def _lower_bound(lbl_ref, d):
    return _sig_tail(lbl_ref[0, d:d + 1, :] - lbl_ref[1, d:d + 1, :])


def _gla_rb(d, nlb, ncb):
    nrb = nlb + ncb
    if d:
        return lambda s: nrb - 1 - s
    return lambda s: jnp.where(s < ncb, nlb + s, s - ncb)


def _gla_fwd(z, lbl, d, nlb, ncb, gather=()):
    m = z.shape[0]
    nrb = nlb + ncb
    rb = _gla_rb(d, nlb, ncb)
    ng = len(gather)

    def body(*refs):
        q_ref, f_ref, v_ref, lbl_ref = refs[:4]
        ag_in = refs[4:4 + ng]
        o_ref, st_ref = refs[4 + ng:6 + ng]
        ag_out = refs[6 + ng:6 + 2 * ng]
        S = refs[6 + 2 * ng]
        ag_sems = refs[7 + 2 * ng:]
        s = pl.program_id(0)

        @pl.when(s == 0)
        def _():
            S[...] = jnp.zeros_like(S)
            if ng:
                local, chip, _ = _ag_copies(ag_in, ag_out, *ag_sems)
                for cp in chip + local:
                    cp.start()

        lb = _lower_bound(lbl_ref, d)
        mb = _gla_mask(d, CH)
        _, _, _, qd, ki, ke, _, _, _, decs = _gla_features(q_ref[...], f_ref[...], lb, d)
        qd, ki, ke, v = qd.astype(BF), ki.astype(BF), ke.astype(BF), v_ref[...].astype(BF)
        order = range(NCK - 1, -1, -1) if d else range(NCK)
        for h in range(H):
            hs = slice(h * DH, (h + 1) * DH)
            intra, upd = {}, {}
            for c in range(NCK):
                rs = _chunk_rows(c)
                a = jnp.where(mb, _dot_nt(qd[rs, hs], ki[rs, hs]), 0.0)
                intra[c] = _dot(a, v[rs, hs])
                upd[c] = _dot_tn(v[rs, hs], ke[rs, hs])
            st = S[h]
            for c in order:
                rs = _chunk_rows(c)
                st_ref[c, h] = st
                o_ref[rs, hs] = intra[c] + _dot_nt(qd[rs, hs], st)
                st = st * decs[c][:, hs] + upd[c]
            S[h] = st

        if ng:
            @pl.when(s == nrb - 1)
            def _():
                _ag_finish(*_ag_copies(ag_in, ag_out, *ag_sems))

    def zspec(cb):
        return pl.BlockSpec((RB, D), lambda s: (rb(s), cb))

    outs = pl.pallas_call(
        body, grid=(nrb,),
        in_specs=[zspec(0), zspec(1 + d), zspec(3), pl.BlockSpec((2, 2, D), lambda s: (0, 0, 0))] + [_ANY] * ng,
        out_specs=[pl.BlockSpec((RB, D), lambda s: (rb(s), 0)),
                   pl.BlockSpec((NCK, H, DH, DH), lambda s: (rb(s), 0, 0, 0))] + [_ANY] * ng,
        out_shape=[jax.ShapeDtypeStruct((m, D), F32), jax.ShapeDtypeStruct((m // CH, H, DH, DH), F32)]
        + [jax.ShapeDtypeStruct((NSH,) + g.shape, g.dtype) for g in gather],
        scratch_shapes=[pltpu.VMEM((H, DH, DH), F32)] + (_ag_sems(ng) if ng else []),
        name=f"gla_fwd{d}", compiler_params=_cp(("arbitrary",)))(z, z, z, lbl, *gather)
    return outs[0], outs[1], list(outs[2:])


def _gla_bwd(z, lbl, do_lat, states, d, nlb, ncb, prev=None):
    m = z.shape[0]
    nrb = nlb + ncb
    fwd_rb = _gla_rb(d, nlb, ncb)
    rb = lambda s: fwd_rb(nrb - 1 - s)
    last = prev is not None

    def body(*refs):
        if last:
            (q_ref, f_ref, v_ref, lbl_ref, do_ref, st_ref, pq_ref, pv_ref, dz1_ref, dz4a_ref, sh0_ref, sh1_ref, sum_ref,
             dS) = refs
        else:
            q_ref, f_ref, v_ref, lbl_ref, do_ref, st_ref, o0_ref, o1_ref, o2_ref, sum_ref, dS = refs
        s = pl.program_id(0)
        is_lat = rb(s) < nlb

        @pl.when(s == 0)
        def _():
            dS[...] = jnp.zeros_like(dS)
            sum_ref[...] = jnp.zeros_like(sum_ref)

        lb = _lower_bound(lbl_ref, d)
        mb = _gla_mask(d, RB)
        zq = q_ref[...]
        sq, sf, f, qd, ki, ke, eg, eig, eeg, decs = _gla_features(zq, f_ref[...], lb, d)
        qdb, kib, keb, vb = qd.astype(BF), ki.astype(BF), ke.astype(BF), v_ref[...].astype(BF)
        dob = jnp.where(is_lat, do_ref[...], 0.0).astype(BF)
        order = range(NCK) if d else range(NCK - 1, -1, -1)
        dqd_h, dki_h, dke_h, dv_h, ddec_h = [], [], [], [], []
        for h in range(H):
            hs = slice(h * DH, (h + 1) * DH)
            a = jnp.where(mb, _dot_nt(qdb[:, hs], kib[:, hs]), 0.0).astype(BF)
            da = jnp.where(mb, _dot_nt(dob[:, hs], vb[:, hs]), 0.0).astype(BF)
            dqd_i = _dot(da, kib[:, hs])
            dki_h.append(_dot_tn(da, qdb[:, hs]))
            dvi = _dot_tn(a, dob[:, hs])
            dqd, inc = {}, {}
            for c in range(NCK):
                rs = _chunk_rows(c)
                dqd[c] = dqd_i[rs, :] + _dot(dob[rs, hs], st_ref[c, h])
                inc[c] = _dot_tn(dob[rs, hs], qdb[rs, hs])
            dst = dS[h]
            dke, dv, ddec = {}, {}, {}
            for c in order:
                rs = _chunk_rows(c)
                dv[c] = dvi[rs, :] + _dot_nt(keb[rs, hs], dst)
                dke[c] = _dot(vb[rs, hs], dst)
                ddec[c] = _colsum(st_ref[c, h] * dst)
                dst = inc[c] + dst * decs[c][:, hs]
            dS[h] = dst
            cat = lambda t: jnp.concatenate([t[c] for c in range(NCK)], axis=0)
            dqd_h.append(cat(dqd))
            dke_h.append(cat(dke))
            dv_h.append(cat(dv))
            ddec_h.append([ddec[c] for c in range(NCK)])
        lanes = lambda parts: jnp.concatenate(parts, axis=1)
        dqd, dki, dke, dv = lanes(dqd_h), lanes(dki_h), lanes(dke_h), lanes(dv_h)
        dq = dqd * eg
        dk = dki * eig + dke * eeg
        dke_ke = dke * ke
        dg = dqd * qd - dki * ki - dke_ke
        dgl = [_colsum(dke_ke[_chunk_rows(c), :]) + lanes([ddec_h[h][c] for h in range(H)]) * decs[c]
               for c in range(NCK)]
        dglb = jnp.concatenate([jnp.broadcast_to(t, (CH, D)) for t in dgl], axis=0)
        df = (_chunk_cumsum(dg, 1 - d) + dglb) / f - dk
        dzf = df * (1.0 - lb) * (sf * (1.0 - sf))
        sum_ref[0:1, :] += _colsum(dzf)
        sum_ref[1:2, :] += _colsum(df * (1.0 - sf))
        if last:
            dz0 = (dq + pq_ref[...]) * (Q_SCALE * (sq * (1.0 + zq * (1.0 - sq))))
            dz3 = dv + pv_ref[...]
            sum_ref[2:3, :] += _colsum(dz0)
            sum_ref[3:4, :] += _colsum(dz3)
            dz2 = dzf.astype(BF)
            sh0_ref[:, 0:D] = dz0.astype(BF)
            sh0_ref[:, D:2 * D] = dz1_ref[...]
            sh0_ref[:, 2 * D:] = dz2[:, :CUT2]
            sh1_ref[:, 0:D - CUT2] = dz2[:, CUT2:]
            sh1_ref[:, D - CUT2:2 * D - CUT2] = dz3.astype(BF)
            sh1_ref[:, 2 * D - CUT2:] = dz4a_ref[...]
        else:
            o0_ref[...] = dq
            o1_ref[...] = dv
            o2_ref[...] = dzf.astype(BF)

        @pl.when(s == nrb - 1)
        def _():
            sum_ref[1:2, :] = sum_ref[1:2, :] * (lb * (1.0 - lb))

    def zspec(cb):
        return pl.BlockSpec((RB, D), lambda s: (rb(s), cb))

    rowspec = pl.BlockSpec((RB, D), lambda s: (rb(s), 0))
    in_specs = [zspec(0), zspec(1 + d), zspec(3), pl.BlockSpec((2, 2, D), lambda s: (0, 0, 0)),
                pl.BlockSpec((RB, D), lambda s: (jnp.minimum(rb(s), nlb - 1), 0)),
                pl.BlockSpec((NCK, H, DH, DH), lambda s: (rb(s), 0, 0, 0))]
    args = [z, z, z, lbl, do_lat, states]
    sumspec = pl.BlockSpec((8, D), lambda s: (0, 0))
    if last:
        in_specs += [rowspec, rowspec, rowspec, pl.BlockSpec((RB, CUT4), lambda s: (rb(s), 0))]
        args += list(prev)
        shspec = pl.BlockSpec((RB, SHC), lambda s: (rb(s), 0))
        out_specs = (shspec, shspec, sumspec)
        out_shape = (jax.ShapeDtypeStruct((m, SHC), BF), jax.ShapeDtypeStruct((m, SHC), BF))
    else:
        out_specs = (rowspec, rowspec, rowspec, sumspec)
        out_shape = (jax.ShapeDtypeStruct((m, D), F32), jax.ShapeDtypeStruct((m, D), F32),
                     jax.ShapeDtypeStruct((m, D), BF))
    return pl.pallas_call(
        body, grid=(nrb,), in_specs=in_specs, out_specs=out_specs,
        out_shape=out_shape + (jax.ShapeDtypeStruct((8, D), F32),),
        scratch_shapes=[pltpu.VMEM((H, DH, DH), F32)],
        name=f"gla_bwd{d}", compiler_params=_cp(("arbitrary",)))(*args)


def _shift_rows(v, k, r):
    row = lax.broadcasted_iota(jnp.int32, v.shape, 0)
    rolled = pltpu.roll(v, k % r, 0)
    return jnp.where((row >= k) & (row < r + k), rolled, 0.0)


def _conv_fwd(xb, cw_ref, cb_ref, r):
    return (cb_ref[...] + _shift_rows(xb, 1, r) * cw_ref[0:1, :] + xb * cw_ref[1:2, :]
            + _shift_rows(xb, -1, r) * cw_ref[2:3, :] + _shift_rows(xb, -2, r) * cw_ref[3:4, :])


def _window(ref, lo, n, rows):
    parts = []
    if lo < 0:
        parts.append(jnp.zeros((-lo,) + tuple(ref.shape[1:]), F32))
    parts.append(ref[max(lo, 0):min(lo + n, rows)])
    if lo + n > rows:
        parts.append(jnp.zeros((lo + n - rows,) + tuple(ref.shape[1:]), F32))
    return parts[0] if len(parts) == 1 else jnp.concatenate(parts, axis=0)


def _conv_cols(x_ref, cw_ref, cb_ref, r0, n, rows):
    w = _window(x_ref, r0 - 1, n + 3, rows)
    return (cb_ref[...] + w[0:n] * cw_ref[0:1, :] + w[1:n + 1] * cw_ref[1:2, :] + w[2:n + 2] * cw_ref[2:3, :]
            + w[3:n + 3] * cw_ref[3:4, :])


def _softplus_neg(lam):
    y = jnp.exp(-jnp.abs(lam))
    u = 1.0 + y
    tiny = u == 1.0
    l1p = jnp.where(tiny, y, jnp.log(u) * (y / jnp.where(tiny, 1.0, u - 1.0)))
    return jnp.maximum(-lam, 0.0) + l1p


def _gates(xc, wr_ref, br, wi_ref, bi, sp):
    xcb = xc.astype(BF)
    rs, is_ = [], []
    for g in range(H):
        gs = slice(g * DH, (g + 1) * DH)
        rs.append(jnp.dot(xcb[:, gs], wr_ref[g].astype(BF), preferred_element_type=F32))
        is_.append(jnp.dot(xcb[:, gs], wi_ref[g].astype(BF), preferred_element_type=F32))
    r = _sig_tail(jnp.concatenate(rs, axis=1) + br)
    i = _sig(jnp.concatenate(is_, axis=1) + bi)
    log_a = (-RG_C * r) * sp
    a = jnp.exp(log_a)
    t = jnp.tanh(log_a)
    om = (-2.0 * t) / (1.0 - t)
    return r, i, a, om


def _scan_rows(d, nrows, a_s, b_s, h_s, h0):
    nsl = nrows // 8

    def slab(j, h):
        jj = (nsl - 1 - j) if d else j
        r0 = pl.multiple_of(jj * 8, 8)
        for t in (range(7, -1, -1) if d else range(8)):
            h = a_s[pl.ds(r0 + t, 1), :] * h + b_s[pl.ds(r0 + t, 1), :]
            h_s[pl.ds(r0 + t, 1), :] = h
        return h

    return lax.fori_loop(0, nsl, slab, h0)


def _col_of(d, ncols):
    if d:
        return lambda s: ncols - jnp.maximum(s, 1)
    return lambda s: jnp.maximum(s, 1) - 1


def _rglru_fwd(z, cw, cb, wr, br, wi, bi, lam, d, t_lat, t_ctx):
    m = z.shape[0]
    rows = t_lat // GRID_W
    z3 = z.reshape(m // GRID_W, GRID_W, IN_COLS)
    nblk = GRID_W // CB
    blk = _col_of(d, nblk)
    cblk = t_lat // t_ctx
    rc = min(RCH, rows)

    def body(zc_ref, zx_ref, cw_ref, cb_ref, wr_ref, br_ref, wi_ref, bi_ref, lam_ref,
             hx_ref, hpx_ref, ax_ref, hpc_ref, a_s, b_s, h_s, hcar, a3, b3, cin_s):
        s = pl.program_id(0)
        sp = _softplus_neg(lam_ref[d:d + 1, :])
        br_ = br_ref[d:d + 1, :]
        bi_ = bi_ref[d:d + 1, :]

        @pl.when(s == 0)
        def _():
            xc = _conv_fwd(zc_ref[...], cw_ref, cb_ref, t_ctx)
            _, i, a, om = _gates(xc, wr_ref, br_, wi_ref, bi_, sp)
            a_s[...] = a
            b_s[...] = jnp.sqrt(om) * (i * xc)
            h0 = jnp.zeros((1, D), F32)
            hcar[0:1, :] = _scan_rows(d, t_ctx, a_s, b_s, h_s, h0)
            hs = h_s[...]
            row = lax.broadcasted_iota(jnp.int32, (t_ctx, D), 0)
            if d:
                hpc_ref[...] = jnp.where(row == t_ctx - 1, h0, pltpu.roll(hs, t_ctx - 1, 0))
            else:
                hpc_ref[...] = jnp.where(row == 0, h0, pltpu.roll(hs, 1, 0))

        @pl.when(s > 0)
        def _():
            for r0 in range(0, rows, rc):
                xc = _conv_cols(zx_ref, cw_ref, cb_ref, r0, rc, rows).reshape(rc * CB, D)
                _, i, a, om = _gates(xc, wr_ref, br_, wi_ref, bi_, sp)
                a3[r0:r0 + rc] = a.reshape(rc, CB, D)
                ax_ref[r0:r0 + rc] = a.reshape(rc, CB, D)
                b3[r0:r0 + rc] = (jnp.sqrt(om) * (i * xc)).reshape(rc, CB, D)

            def local(t, carry):
                hl, p = carry
                r = (rows - 1 - t) if d else t
                a = a3[r]
                hl = a * hl + b3[r]
                p = a * p
                b3[r] = hl
                a3[r] = p
                return hl, p

            hl, p = lax.fori_loop(0, rows, local, (jnp.zeros((CB, D), F32), jnp.ones((CB, D), F32)))
            cin = hcar[0:1, :]
            for j in (range(CB - 1, -1, -1) if d else range(CB)):
                cin_s[j:j + 1, :] = cin
                cin = hl[j:j + 1, :] + p[j:j + 1, :] * cin
            hcar[0:1, :] = cin
            c_in = cin_s[...]

            def fix(t, prev):
                r = (rows - 1 - t) if d else t
                h = b3[r] + a3[r] * c_in
                hx_ref[r] = h
                hpx_ref[r] = prev
                return h

            lax.fori_loop(0, rows, fix, c_in)

    full = lambda shp: pl.BlockSpec(shp, lambda s: (0,) * len(shp))
    colspec = pl.BlockSpec((rows, CB, D), lambda s: (0, blk(s), 0))
    outs = pl.pallas_call(
        body, grid=(nblk + 1,),
        in_specs=[pl.BlockSpec((t_ctx, D), lambda s: (cblk, 5)),
                  pl.BlockSpec((rows, CB, D), lambda s: (0, blk(s), 5)),
                  full((4, D)), full((1, D)),
                  pl.BlockSpec((None, H, DH, DH), lambda s: (d, 0, 0, 0)), full((2, D)),
                  pl.BlockSpec((None, H, DH, DH), lambda s: (d, 0, 0, 0)), full((2, D)), full((2, D))],
        out_specs=(colspec, colspec, colspec, full((t_ctx, D))),
        out_shape=(jax.ShapeDtypeStruct((rows, GRID_W, D), F32),) * 3 + (jax.ShapeDtypeStruct((t_ctx, D), F32),),
        scratch_shapes=[pltpu.VMEM((t_ctx, D), F32), pltpu.VMEM((t_ctx, D), F32), pltpu.VMEM((t_ctx, D), F32),
                        pltpu.VMEM((8, D), F32), pltpu.VMEM((rows, CB, D), F32), pltpu.VMEM((rows, CB, D), F32),
                        pltpu.VMEM((CB, D), F32)],
        name=f"rglru_fwd{d}", compiler_params=_cp(("arbitrary",)))(z, z3, cw, cb, wr, br, wi, bi, lam)
    return outs[0].reshape(t_lat, D), outs[1].reshape(t_lat, D), outs[2].reshape(t_lat, D), outs[3]


def _rglru_bwd(z, cw, cb, wr, br, wi, bi, lam, dh_lat, hp_lat, a_lat, hp_ctx, d, t_lat, t_ctx, prev=None):
    m = z.shape[0]
    rows = t_lat // GRID_W
    z3 = z.reshape(m // GRID_W, GRID_W, IN_COLS)
    nblk = GRID_W // CB
    fblk = _col_of(d, nblk)
    blk = lambda s: fblk(nblk - jnp.minimum(s, nblk - 1))
    cblk = t_lat // t_ctx
    rc = min(RCH, rows)
    last = prev is not None
    view3 = lambda v: v.reshape(rows, GRID_W, D)

    def body(*refs):
        (zc_ref, zx_ref, cw_ref, cb_ref, wr_ref, br_ref, wi_ref, bi_ref, lam_ref, dh_ref, hpx_ref, ax_ref,
         hpc_ref) = refs[:13]
        k = 13
        if last:
            pdx_ref, pdc_ref = refs[13:15]
            k = 15
        ox_ref, oc_ref, dwr_ref, dwi_ref, sum_ref, a_s, b_s, gcar, a3, b3, cin_s = refs[k:]
        s = pl.program_id(0)
        lam_d = lam_ref[d:d + 1, :]
        sp = _softplus_neg(lam_d)
        br_ = br_ref[d:d + 1, :]
        bi_ = bi_ref[d:d + 1, :]
        flat = lambda v: v.reshape(-1, D)

        @pl.when(s == 0)
        def _():
            gcar[...] = jnp.zeros_like(gcar)
            dwr_ref[...] = jnp.zeros_like(dwr_ref)
            dwi_ref[...] = jnp.zeros_like(dwi_ref)
            sum_ref[...] = jnp.zeros_like(sum_ref)


        def conv_sums(dxc, dxb, xm1, x0, xp1, xp2):
            sum_ref[3:4, :] += _colsum(flat(dxc))
            sum_ref[4:5, :] += _colsum(flat(dxb))
            sum_ref[8:9, :] += _colsum(flat(dxc * xm1))
            sum_ref[9:10, :] += _colsum(flat(dxc * x0))
            sum_ref[10:11, :] += _colsum(flat(dxc * xp1))
            sum_ref[11:12, :] += _colsum(flat(dxc * xp2))

        def gate_grads(g, hp, xc, rr, i, a, om):
            mult = jnp.sqrt(om)
            da = g * hp
            ixc = i * xc
            dmult = g * ixc
            dixc = g * mult
            di = dixc * xc
            dxc = dixc * i
            dlog_a = da * a - dmult * ((1.0 - om) / mult)
            dr = dlog_a * (-RG_C * sp)
            sum_ref[2:3, :] += _colsum(dlog_a * rr)
            drp = dr * rr * (1.0 - rr)
            dip = di * i * (1.0 - i)
            sum_ref[0:1, :] += _colsum(drp)
            sum_ref[1:2, :] += _colsum(dip)
            xcb = xc.astype(BF)
            drb = drp.astype(BF)
            dib = dip.astype(BF)
            parts = []
            for gi in range(H):
                gs = slice(gi * DH, (gi + 1) * DH)
                parts.append(_dot_nt(drb[:, gs], wr_ref[gi]) + _dot_nt(dib[:, gs], wi_ref[gi]))
                dwr_ref[gi] += _dot_tn(xcb[:, gs], drb[:, gs])
                dwi_ref[gi] += _dot_tn(xcb[:, gs], dib[:, gs])
            return dxc + jnp.concatenate(parts, axis=1)

        @pl.when(s < nblk)
        def _():
            def local(t, carry):
                c, q = carry
                r = t if d else (rows - 1 - t)
                a = ax_ref[r]
                c = a * (c + dh_ref[r])
                q = a * q
                b3[r] = c
                a3[r] = q
                return c, q

            c, q = lax.fori_loop(0, rows, local, (jnp.zeros((CB, D), F32), jnp.ones((CB, D), F32)))
            cin = gcar[0:1, :]
            for j in (range(CB) if d else range(CB - 1, -1, -1)):
                cin_s[j:j + 1, :] = cin
                cin = c[j:j + 1, :] + q[j:j + 1, :] * cin
            gcar[0:1, :] = cin
            c_in = cin_s[...]
            for r0 in (range(rows - rc, -1, -rc) if d else range(0, rows, rc)):
                if d:
                    lo = max(r0 - 1, 0)
                    cn = b3[lo:r0 + rc - 1] + a3[lo:r0 + rc - 1] * c_in
                    if r0 == 0:
                        cn = jnp.concatenate([c_in[None], cn], axis=0)
                else:
                    hi = min(r0 + rc + 1, rows)
                    cn = b3[r0 + 1:hi] + a3[r0 + 1:hi] * c_in
                    if hi == rows:
                        cn = jnp.concatenate([cn, c_in[None]], axis=0)
                g = flat(dh_ref[r0:r0 + rc] + cn)
                xc = flat(_conv_cols(zx_ref, cw_ref, cb_ref, r0, rc, rows))
                rr, i, a, om = _gates(xc, wr_ref, br_, wi_ref, bi_, sp)
                b3[r0:r0 + rc] = gate_grads(g, flat(hpx_ref[r0:r0 + rc]), xc, rr, i, a, om).reshape(rc, CB, D)
            if last:
                for r0 in range(0, rows, rc):
                    b3[r0:r0 + rc] = b3[r0:r0 + rc] + pdx_ref[r0:r0 + rc]
                for r0 in range(0, rows, rc):
                    w = _window(b3, r0 - 2, rc + 3, rows)
                    xw = _window(zx_ref, r0 - 1, rc + 3, rows)
                    dxc = w[2:rc + 2]
                    dxb = (w[3:rc + 3] * cw_ref[0:1, :] + dxc * cw_ref[1:2, :] + w[1:rc + 1] * cw_ref[2:3, :]
                           + w[0:rc] * cw_ref[3:4, :])
                    conv_sums(dxc, dxb, xw[0:rc], xw[1:rc + 1], xw[2:rc + 2], xw[3:rc + 3])
                    ox_ref[r0:r0 + rc] = dxb
            else:
                for r0 in range(0, rows, rc):
                    ox_ref[r0:r0 + rc] = b3[r0:r0 + rc]

        @pl.when(s == nblk)
        def _():
            r = t_ctx
            xb = zc_ref[...]
            xc = _conv_fwd(xb, cw_ref, cb_ref, r)
            rr, i, a, om = _gates(xc, wr_ref, br_, wi_ref, bi_, sp)
            a_s[...] = a
            b_s[...] = jnp.zeros((r, D), F32)
            c0 = gcar[0:1, :]
            _scan_rows(1 - d, r, a_s, b_s, b_s, c0)
            cs = b_s[...]
            row = lax.broadcasted_iota(jnp.int32, (r, D), 0)
            if d:
                g = jnp.where(row == 0, c0, pltpu.roll(cs, 1, 0))
            else:
                g = jnp.where(row == r - 1, c0, pltpu.roll(cs, r - 1, 0))
            dxc = gate_grads(g, hpc_ref[...], xc, rr, i, a, om)
            if last:
                dxc = dxc + pdc_ref[...]
                dxb = (_shift_rows(dxc, -1, r) * cw_ref[0:1, :] + dxc * cw_ref[1:2, :]
                       + _shift_rows(dxc, 1, r) * cw_ref[2:3, :] + _shift_rows(dxc, 2, r) * cw_ref[3:4, :])
                conv_sums(dxc, dxb, _shift_rows(xb, 1, r), xb, _shift_rows(xb, -1, r), _shift_rows(xb, -2, r))
                oc_ref[...] = dxb
            else:
                oc_ref[...] = dxc
            sum_ref[2:3, :] = sum_ref[2:3, :] * (RG_C * _sig_tail(-lam_d))

    full = lambda shp: pl.BlockSpec(shp, lambda s: (0,) * len(shp))
    once = lambda shp: pl.BlockSpec(shp, lambda s: (0,) * len(shp), pipeline_mode=pl.Buffered(1))
    colspec = pl.BlockSpec((rows, CB, D), lambda s: (0, blk(s), 0))
    colonce = pl.BlockSpec((rows, CB, D), lambda s: (0, blk(s), 0), pipeline_mode=pl.Buffered(1))
    in_specs = [pl.BlockSpec((t_ctx, D), lambda s: (cblk, 5), pipeline_mode=pl.Buffered(1)),
                pl.BlockSpec((rows, CB, D), lambda s: (0, blk(s), 5), pipeline_mode=pl.Buffered(1)),
                full((4, D)), full((1, D)),
                pl.BlockSpec((None, H, DH, DH), lambda s: (d, 0, 0, 0)), full((2, D)),
                pl.BlockSpec((None, H, DH, DH), lambda s: (d, 0, 0, 0)), full((2, D)), full((2, D)),
                colspec, colspec, colonce, once((t_ctx, D))]
    args = [z, z3, cw, cb, wr, br, wi, bi, lam, view3(dh_lat), view3(hp_lat), view3(a_lat), hp_ctx]
    if last:
        in_specs += [colonce, once((t_ctx, D))]
        args += [view3(prev[0]), prev[1]]
    outs = pl.pallas_call(
        body, grid=(nblk + 1,), in_specs=in_specs,
        out_specs=(colspec, full((t_ctx, D)), full((H, DH, DH)), full((H, DH, DH)), full((16, D))),
        out_shape=(jax.ShapeDtypeStruct((rows, GRID_W, D), F32), jax.ShapeDtypeStruct((t_ctx, D), F32),
                   jax.ShapeDtypeStruct((H, DH, DH), F32), jax.ShapeDtypeStruct((H, DH, DH), F32),
                   jax.ShapeDtypeStruct((16, D), F32)),
        scratch_shapes=[pltpu.VMEM((t_ctx, D), F32), pltpu.VMEM((t_ctx, D), F32), pltpu.VMEM((8, D), F32),
                        pltpu.VMEM((rows, CB, D), F32), pltpu.VMEM((rows, CB, D), F32), pltpu.VMEM((CB, D), F32)],
        name=f"rglru_bwd{d}", compiler_params=_cp(("arbitrary",), VMEM_LIMIT_RGLRU_BWD))(*args)
    return (outs[0].reshape(t_lat, D), outs[1]) + tuple(outs[2:])


def _merge(o_f, o_b, h_f, h_b, z, x, tgt, mod, norm_g, ln_g, ln_b, p_a, p_b, w_out, t_lat):
    tm = 256
    nt = t_lat // tm

    def body(of_ref, ob_ref, hf_ref, hb_ref, z4_ref, z6_ref, z7_ref, z8_ref, x_ref, t_ref, mod_ref, ng_ref,
             lg_ref, lb_ref, pa_ref, pb_ref, wo_ref,
             do_ref, dh_ref, dz4a_ref, dz4b_ref, dz6a_ref, sh3_ref, gx_ref,
             y_ref, dout_ref, oa_ref, dpa_ref, obv_ref, dpb_ref, acc_ref):
        i = pl.program_id(0)
        lat = i < nt
        latf = lat.astype(F32)

        @pl.when(i == 0)
        def _():
            acc_ref[...] = jnp.zeros_like(acc_ref)

        def per_head(v):
            return jnp.concatenate(
                [jnp.broadcast_to(jnp.mean(v[:, h * DH:(h + 1) * DH], axis=-1, keepdims=True), (tm, DH))
                 for h in range(H)], axis=1)

        gt = mod_ref[0:1, 2 * D:3 * D]
        gfull = jnp.concatenate([ng_ref[...]] * H, axis=1)
        o = of_ref[...] + ob_ref[...]
        rinv = lax.rsqrt(per_head(o * o) + RMS_EPS)
        n = o * rinv
        na = n * gfull
        z4 = z4_ref[...]
        s4 = _sig(z4)
        silu4 = z4 * s4
        oa = na * silu4
        z6 = z6_ref[...]
        s6 = _sig(z6)
        silu6 = z6 * s6
        hsum = hf_ref[...] + hb_ref[...]
        obv = hsum * silu6
        pa = _dot(oa, pa_ref[...])
        pb = _dot(obv, pb_ref[...])
        s7 = _sig(z7_ref[...])
        s8 = _sig(z8_ref[...])
        y = s7 * pa + s8 * pb
        out = _dot(y, wo_ref[...])
        pre = ALPHA * x_ref[...] + gt * out
        mu = jnp.mean(pre, axis=-1, keepdims=True)
        xc = pre - mu
        rstd = lax.rsqrt(jnp.mean(xc * xc, axis=-1, keepdims=True) + LN_EPS)
        xhat = xc * rstd
        lg = lg_ref[...]
        diff = xhat * lg + lb_ref[...] - t_ref[...]
        acc_ref[8:9, :] += _colsum(diff * diff) * (0.5 / D * latf)
        dxn = diff * (1.0 / D)
        acc_ref[1:2, :] += _colsum(dxn * xhat) * latf
        acc_ref[2:3, :] += _colsum(dxn) * latf
        dxhat = dxn * lg
        dpre = rstd * (dxhat - jnp.mean(dxhat, axis=-1, keepdims=True)
                       - xhat * jnp.mean(dxhat * xhat, axis=-1, keepdims=True))
        gx_ref[...] = ALPHA * dpre
        acc_ref[0:1, :] += _colsum(dpre * out) * latf
        dout = dpre * gt
        dy = _dot_nt(dout, wo_ref[...])
        dpa = dy * s7
        dpb = dy * s8
        dz7 = dy * pa * (s7 * (1.0 - s7))
        dz8 = dy * pb * (s8 * (1.0 - s8))
        doa = _dot_nt(dpa, pa_ref[...])
        dob = _dot_nt(dpb, pb_ref[...])
        dh_ref[...] = dob * silu6
        dz6 = dob * hsum * (s6 * (1.0 + z6 * (1.0 - s6)))
        dna = doa * silu4
        dz4 = doa * na * (s4 * (1.0 + z4 * (1.0 - s4)))
        dng = _colsum(dna * n)
        acc_ref[7:8, 0:DH] += sum(dng[:, h * DH:(h + 1) * DH] for h in range(H)) * latf
        dn = dna * gfull
        do_ref[...] = rinv * (dn - n * per_head(dn * n))
        acc_ref[3:4, :] += _colsum(dz4) * latf
        acc_ref[4:5, :] += _colsum(dz6) * latf
        acc_ref[5:6, :] += _colsum(dz7) * latf
        acc_ref[6:7, :] += _colsum(dz8) * latf
        dz4b, dz6b = (dz4 * latf).astype(BF), (dz6 * latf).astype(BF)
        dz4a_ref[...] = dz4b[:, :CUT4]
        dz4b_ref[...] = dz4b[:, CUT4:]
        dz6a_ref[...] = dz6b[:, :CUT6]
        sh3_ref[:, 0:D - CUT6] = dz6b[:, CUT6:]
        sh3_ref[:, D - CUT6:2 * D - CUT6] = (dz7 * latf).astype(BF)
        sh3_ref[:, 2 * D - CUT6:] = (dz8 * latf).astype(BF)
        y_ref[...] = y.astype(BF)
        dout_ref[...] = dout.astype(BF)
        oa_ref[...] = oa.astype(BF)
        dpa_ref[...] = dpa.astype(BF)
        obv_ref[...] = obv.astype(BF)
        dpb_ref[...] = dpb.astype(BF)

        @pl.when(i == nt - 1)
        def _():
            acc_ref[9:10, :] = jnp.broadcast_to(jnp.sum(acc_ref[8:9, :], axis=-1, keepdims=True), (1, D))

    m = z.shape[0]
    lrow = lambda i: jnp.minimum(i, nt - 1)
    row = pl.BlockSpec((tm, D), lambda i: (lrow(i), 0))
    allrow = lambda cols: pl.BlockSpec((tm, cols), lambda i: (i, 0))
    zs = lambda cb: pl.BlockSpec((tm, D), lambda i: (lrow(i), cb))
    full = lambda shp: pl.BlockSpec(shp, lambda i: (0,) * len(shp))
    wfull = pl.BlockSpec((D, D), lambda i: (0, 0), pipeline_mode=pl.Buffered(1))
    f32o = jax.ShapeDtypeStruct((t_lat, D), F32)
    bfo = jax.ShapeDtypeStruct((t_lat, D), BF)
    bfall = lambda cols: jax.ShapeDtypeStruct((m, cols), BF)
    return pl.pallas_call(
        body, grid=(m // tm,),
        in_specs=[row, row, row, row, zs(4), zs(6), zs(7), zs(8), row, row, full((16, 3 * D)), full((1, DH)),
                  full((1, D)), full((1, D)), wfull, wfull, wfull],
        out_specs=(row, row) + tuple(allrow(c) for c in (CUT4, D - CUT4, CUT6, SHC)) + (row,) * 7 + (full((16, D)),),
        out_shape=(f32o, f32o, bfall(CUT4), bfall(D - CUT4), bfall(CUT6), bfall(SHC), f32o, bfo, bfo, bfo, bfo, bfo, bfo,
                   jax.ShapeDtypeStruct((16, D), F32)),
        name="merge", compiler_params=_cp(("arbitrary",), VMEM_LIMIT_MERGE))(
            o_f, o_b, h_f, h_b, z, z, z, z, x, tgt, mod, norm_g, ln_g, ln_b, p_a, p_b, w_out)


def _wgrad(a, b, name):
    tm = 1024

    def body(a_ref, b_ref, o_ref):
        @pl.when(pl.program_id(0) == 0)
        def _():
            o_ref[...] = jnp.zeros_like(o_ref)
        o_ref[...] += _dot_tn(a_ref[...], b_ref[...])

    row = pl.BlockSpec((tm, D), lambda i: (i, 0))
    return pl.pallas_call(body, grid=(a.shape[0] // tm,), in_specs=[row, row],
                          out_specs=pl.BlockSpec((D, D), lambda i: (0, 0)),
                          out_shape=jax.ShapeDtypeStruct((D, D), F32), name=name,
                          compiler_params=_cp(("arbitrary",)))(a, b)


def _pack_shard2(dz4b, dz5_lat, dz5_ctx, dz6a):
    m, t_lat, t_ctx = dz4b.shape[0], dz5_lat.shape[0], dz5_ctx.shape[0]
    tm = t_ctx
    nlt = t_lat // tm
    w4 = D - CUT4

    def body(a_ref, bl_ref, bc_ref, c_ref, o_ref):
        i = pl.program_id(0)
        o_ref[:, 0:w4] = a_ref[...]
        o_ref[:, w4:w4 + D] = jnp.where(i < nlt, bl_ref[...], bc_ref[...]).astype(BF)
        o_ref[:, w4 + D:] = c_ref[...]

    return pl.pallas_call(
        body, grid=(m // tm,),
        in_specs=[pl.BlockSpec((tm, w4), lambda i: (i, 0)),
                  pl.BlockSpec((tm, D), lambda i: (jnp.minimum(i, nlt - 1), 0)),
                  pl.BlockSpec((tm, D), lambda i: (0, 0)),
                  pl.BlockSpec((tm, CUT6), lambda i: (i, 0))],
        out_specs=pl.BlockSpec((tm, SHC), lambda i: (i, 0)),
        out_shape=jax.ShapeDtypeStruct((m, SHC), BF), name="pack_shard2",
        compiler_params=_cp(("arbitrary",)))(dz4b, dz5_lat, dz5_ctx, dz6a)


def _wgrad_in(u_all, dz_shards, others):
    m = u_all.shape[1]
    assert m % (6 * 128) == 0
    tm = m // 6
    nt = m // tm
    no = len(others)
    spec_u, spec_dz = pl.BlockSpec((D, tm), lambda i: (0, i)), pl.BlockSpec((tm, SHC), lambda i: (i, 0))
    oshape = jax.ShapeDtypeStruct((NSH, D, SHC), F32)
    out = None
    for k, dz_k in enumerate(dz_shards[:-1]):
        def body(u_ref, dz_ref, *rest):
            o_ref = rest[-1]

            @pl.when(pl.program_id(0) == 0)
            def _():
                o_ref[...] = jnp.zeros_like(o_ref)
            o_ref[0] += _dot(u_ref[...], dz_ref[...])

        out = pl.pallas_call(
            body, grid=(nt,), in_specs=[spec_u, spec_dz] + ([] if out is None else [_ANY]),
            out_specs=pl.BlockSpec((1, D, SHC), lambda i, k=k: (k, 0, 0)), out_shape=oshape,
            input_output_aliases={} if out is None else {2: 0},
            name=f"wgrad_in{k}", compiler_params=_cp(("arbitrary",)))(u_all, dz_k, *(() if out is None else (out,)))

    def last(u_ref, dz_ref, done_ref, *rest):
        other_refs, o_ref, got_refs = rest[:no], rest[no], rest[no + 1:2 * no + 2]
        send, recv = rest[2 * no + 2:]
        i = pl.program_id(0)
        give = ([done_ref] + list(other_refs), got_refs, send, recv, [tuple(range(NSH - 1))] + [tuple(range(NSH))] * no)

        @pl.when(i == 0)
        def _():
            o_ref[...] = jnp.zeros_like(o_ref)
            for cp in _rs_sibling_copies(*give):
                cp.start()
        o_ref[0] += _dot(u_ref[...], dz_ref[...])

        @pl.when(i == nt - 1)
        def _():
            for cp in _rs_sibling_copies(*give):
                cp.wait()

    half = lambda g: jax.ShapeDtypeStruct((NSH, g.shape[1] // 2, g.shape[2]), F32)
    outs = pl.pallas_call(
        last, grid=(nt,), in_specs=[spec_u, spec_dz, _ANY] + [_ANY] * no,
        out_specs=[pl.BlockSpec((1, D, SHC), lambda i: (NSH - 1, 0, 0))] + [_ANY] * (no + 1),
        out_shape=[oshape, half(oshape)] + [half(g) for g in others], input_output_aliases={2: 0},
        scratch_shapes=[pltpu.SemaphoreType.DMA((NSH * (no + 1),)), pltpu.SemaphoreType.DMA((NSH * (no + 1),))],
        name=f"wgrad_in{NSH - 1}", compiler_params=_cp(("arbitrary",)))(u_all, dz_shards[-1], out, *others)
    return outs[0], list(outs[1:])


def _du(dz_shards, w_in_g, x, ctx, mod, gxres, sums=()):
    tm = 256
    n_lat_tiles, nct = x.shape[0] // tm, ctx.shape[0] // tm
    nt = n_lat_tiles + nct
    ns = len(sums)
    rblk = lambda i: jnp.where(i < nct, n_lat_tiles + i, i - nct)
    lblk = lambda i: jnp.maximum(i - nct, 0)

    def body(*refs):
        dz_refs, refs = refs[:NSH], refs[NSH:]
        w_ref, x_ref, c_ref, mod_ref, gr_ref = refs[:5]
        sum_refs = refs[5:5 + ns]
        gx_ref, dm_ref = refs[5 + ns:7 + ns]
        got_refs = refs[7 + ns:7 + 2 * ns]
        sems = refs[7 + 2 * ns:]
        i = pl.program_id(0)
        is_lat = i >= nct

        @pl.when(i == 0)
        def _():
            dm_ref[...] = jnp.zeros_like(dm_ref)
            if ns:
                for cp in _rs_chip_copies(sum_refs, got_refs, *sems):
                    cp.start()

        du = _dot_nt(dz_refs[0][...], w_ref[0])
        for n in range(1, NSH):
            du = du + _dot_nt(dz_refs[n][...], w_ref[n])
        sc = jnp.where(is_lat, mod_ref[0:1, D:2 * D], mod_ref[1:2, D:2 * D])
        dsh = _colsum(du)
        dsc = _colsum(du * jnp.where(is_lat, x_ref[...], c_ref[...]))

        @pl.when(is_lat)
        def _():
            gx_ref[...] = du * (1.0 + sc) + gr_ref[...]
            dm_ref[0:1, 0:D] += dsh
            dm_ref[0:1, D:2 * D] += dsc

        @pl.when(jnp.logical_not(is_lat))
        def _():
            dm_ref[1:2, 0:D] += dsh
            dm_ref[1:2, D:2 * D] += dsc

        if ns:
            @pl.when(i == nt - 1)
            def _():
                for cp in _rs_chip_copies(sum_refs, got_refs, *sems):
                    cp.wait()

    outs = pl.pallas_call(
        body, grid=(nt,),
        in_specs=[pl.BlockSpec((tm, SHC), lambda i: (rblk(i), 0))] * NSH + [
                  pl.BlockSpec((NSH, D, SHC), lambda i: (0, 0, 0), pipeline_mode=pl.Buffered(1)),
                  pl.BlockSpec((tm, D), lambda i: (lblk(i), 0)),
                  pl.BlockSpec((tm, D), lambda i: (jnp.minimum(i, nct - 1), 0)),
                  pl.BlockSpec((16, 3 * D), lambda i: (0, 0)),
                  pl.BlockSpec((tm, D), lambda i: (lblk(i), 0))] + [_ANY] * ns,
        out_specs=[pl.BlockSpec((tm, D), lambda i: (lblk(i), 0)),
                   pl.BlockSpec((8, 2 * D), lambda i: (0, 0))] + [_ANY] * ns,
        out_shape=[jax.ShapeDtypeStruct((n_lat_tiles * tm, D), F32), jax.ShapeDtypeStruct((8, 2 * D), F32)]
        + [jax.ShapeDtypeStruct((3,) + g.shape[1:], g.dtype) for g in sums],
        scratch_shapes=[pltpu.SemaphoreType.DMA((3 * ns,)), pltpu.SemaphoreType.DMA((3 * ns,))] if ns else [],
        name="du", compiler_params=_cp(("arbitrary",)))(*dz_shards, w_in_g, x, ctx, mod, gxres, *sums)
    return outs[0], outs[1], list(outs[2:])


def _row_tile(rows, cols, block_bytes=1 << 20):
    t = 8
    while t * 2 * cols * 4 <= block_bytes and rows % (t * 2) == 0:
        t *= 2
    return t


def _adamw_update(w_ref, g_ref, m_ref, v_ref, d_ref, nm_ref, nv_ref):
    gg = g_ref[...]
    m2 = ADAM_B1 * m_ref[...] + (1.0 - ADAM_B1) * gg
    v2 = ADAM_B2 * v_ref[...] + (1.0 - ADAM_B2) * (gg * gg)
    m_hat = m2 / (1.0 - ADAM_B1 ** ADAM_STEP)
    v_hat = v2 / (1.0 - ADAM_B2 ** ADAM_STEP)
    d_ref[...] = -ADAM_LR * (m_hat / (jnp.sqrt(v_hat) + ADAM_EPS) + ADAM_WD * w_ref[...])
    nm_ref[...] = m2
    nv_ref[...] = v2


def _adamw_many(ws, gs, ms, vs):
    n = len(ws)

    def body(*refs):
        for j in range(n):
            _adamw_update(*refs[4 * j:4 * j + 4], *refs[4 * n + 3 * j:4 * n + 3 * j + 3])

    args = [a for quad in zip(ws, gs, ms, vs) for a in quad]
    outs = pl.pallas_call(body, out_shape=[jax.ShapeDtypeStruct(w.shape, F32) for w in ws for _ in range(3)],
                          name="adamw_small", compiler_params=_cp())(*args)
    return outs[0::3], outs[1::3], outs[2::3]


def _adamw(ws, gs, ms, vs, name):
    n = len(ws)
    rows, cols = ws[0].shape
    tr = _row_tile(rows, cols, (2 << 20) // n)

    def body(*refs):
        for j in range(n):
            _adamw_update(*refs[4 * j:4 * j + 4], *refs[4 * n + 3 * j:4 * n + 3 * j + 3])

    spec = pl.BlockSpec((tr, cols), lambda i: (i, 0))
    o = jax.ShapeDtypeStruct((rows, cols), F32)
    args = [a for quad in zip(ws, gs, ms, vs) for a in quad]
    outs = pl.pallas_call(body, grid=(rows // tr,), in_specs=[spec] * (4 * n), out_specs=[spec] * (3 * n),
                          out_shape=[o] * (3 * n), name=name, compiler_params=_cp(("arbitrary",)))(*args)
    return outs[0::3], outs[1::3], outs[2::3]


_ANY = pl.BlockSpec(memory_space=pl.ANY)


def _place():
    return lax.axis_index("x"), lax.axis_index("y"), lax.axis_index("c")


def _ag_copies(ins, outs, send, recv, fsend, frecv, lsem):
    x, y, c = _place()
    me = 2 * x + y
    chips = ((1 - x, y), (x, 1 - y), (1 - x, 1 - y))
    local, chip, hand = [], [], []
    for j in range(len(ins)):
        hr = ins[j].shape[0] // 2
        half = pl.ds(pl.multiple_of(c * hr, 8), hr)
        local.append(pltpu.make_async_copy(ins[j], outs[j].at[me], lsem.at[j]))
        for k, (px, py) in enumerate(chips):
            chip.append(pltpu.make_async_remote_copy(
                src_ref=ins[j].at[half, :], dst_ref=outs[j].at[me, half, :], send_sem=send.at[3 * j + k],
                recv_sem=recv.at[3 * j + k], device_id=(px, py, c), device_id_type=MESH))
            got = outs[j].at[2 * px + py, half, :]
            hand.append(pltpu.make_async_remote_copy(
                src_ref=got, dst_ref=got, send_sem=fsend.at[3 * j + k], recv_sem=frecv.at[3 * j + k],
                device_id=(x, y, 1 - c), device_id_type=MESH))
    return local, chip, hand


def _ag_sems(n):
    return [pltpu.SemaphoreType.DMA((3 * n,))] * 4 + [pltpu.SemaphoreType.DMA((n,))]


def _ag_finish(local, chip, hand, done=0):
    for k in range(done, len(chip)):
        chip[k].wait_recv()
        hand[k].start()
    for cp in chip:
        cp.wait_send()
    for k in range(done):
        hand[k].wait_send()
    for k in range(done, len(chip)):
        hand[k].wait_send()
        hand[k].wait_recv()
    for cp in local:
        cp.wait()


def _mod_tp(c8, c_ctx, w_mod_sh, b_mod_sh):
    mc = w_mod_sh.shape[1]

    def body(c8_ref, cctx_ref, w_ref, b_ref, mod_ref, cc_ref, cc_s, part_s, send1, recv1, send3, recv3):
        x, y, c = _place()
        me = 4 * x + 2 * y + c
        ms = 2 * x + y
        copies = []
        for k in range(1, 8):
            peer = (x ^ ((k >> 2) & 1), y ^ ((k >> 1) & 1), c ^ (k & 1))
            cp = pltpu.make_async_remote_copy(src_ref=c8_ref, dst_ref=cc_s.at[me], send_sem=send1.at[k],
                                              recv_sem=recv1.at[k], device_id=peer, device_id_type=MESH)
            cp.start()
            copies.append(cp)
        cc_s[me] = c8_ref[...]
        for cp in copies:
            cp.wait()
        cc_ref[...] = jnp.zeros_like(cc_ref)
        for j in range(8):
            cc_ref[j:j + 1, :] = cc_s[j, 0:1, :]
        cc_ref[8:9, :] = cctx_ref[...]
        v = cc_ref[...]
        part_s[ms] = _dot(v * _sig(v), w_ref[...]) + b_ref[...]
        copies = []
        for k in range(1, 4):
            peer = (x ^ ((k >> 1) & 1), y ^ (k & 1), c)
            cp = pltpu.make_async_remote_copy(src_ref=part_s.at[ms], dst_ref=part_s.at[ms], send_sem=send3.at[k],
                                              recv_sem=recv3.at[k], device_id=peer, device_id_type=MESH)
            cp.start()
            copies.append(cp)
        for cp in copies:
            cp.wait()
        for s in range(NSH):
            mod_ref[:, s * mc:(s + 1) * mc] = part_s[s]

    vm = pl.BlockSpec(memory_space=pltpu.VMEM)
    return pl.pallas_call(
        body, in_specs=[vm] * 4, out_specs=(vm, vm),
        out_shape=(jax.ShapeDtypeStruct((16, NSH * mc), F32), jax.ShapeDtypeStruct((16, D), F32)),
        scratch_shapes=[pltpu.VMEM((8, 8, D), F32), pltpu.VMEM((NSH, 16, mc), F32),
                        pltpu.SemaphoreType.DMA((8,)), pltpu.SemaphoreType.DMA((8,)),
                        pltpu.SemaphoreType.DMA((4,)), pltpu.SemaphoreType.DMA((4,))],
        name="mod_tp", compiler_params=_cp())(c8, c_ctx, w_mod_sh, b_mod_sh)


def _inproj_ag(x, ctx, mod, w_in_sh, b_in, narrow_sh):
    n_lat = x.shape[0]
    m = n_lat + ctx.shape[0]
    assert m % (11 * 128) == 0
    tm = m // 11
    nt = m // tm
    nfull = n_lat // tm
    assert nfull == nt - 1
    tail = jnp.concatenate([x[nfull * tm:], ctx], axis=0)
    x_, y_ = lax.axis_index("x"), lax.axis_index("y")
    sids = jnp.stack([2 * x_ + y_, 2 * (1 - x_) + y_, 2 * x_ + 1 - y_, 2 * (1 - x_) + 1 - y_]).astype(jnp.int32)

    def body(sid_ref, x_ref, tail_ref, mod_ref, b_ref, wsh_ref, nsh_ref, z_ref, u_ref, wg_ref, ng_ref, w_s, u_s, *sems):
        n = pl.program_id(0)
        i = pl.program_id(1)
        rows = pl.ds(pl.multiple_of(i * tm, tm), tm)
        ag = ((wsh_ref, nsh_ref), (wg_ref, ng_ref)) + tuple(sems[:5])
        wsem = sems[5]

        def load(src):
            cp = pltpu.make_async_copy(src, w_s, wsem)
            cp.start()
            cp.wait()

        @pl.when((n == 0) & (i == 0))
        def _():
            local, chip, _ = _ag_copies(*ag)
            for cp in chip + local:
                cp.start()
            load(wsh_ref)

        for k in range(NSH - 1):
            @pl.when((n == k + 1) & (i == 0))
            def _():
                _, chip, hand = _ag_copies(*ag)
                chip[k].wait_recv()
                hand[k].start()
                hand[k].wait_recv()
                load(wg_ref.at[sid_ref[k + 1]])

        @pl.when(n == 0)
        def _():
            is_lat = (i * tm + lax.broadcasted_iota(jnp.int32, (tm, 1), 0)) < n_lat
            u = _modulate(jnp.where(i < nfull, x_ref[...], tail_ref[...]), mod_ref, is_lat)
            u_s[rows, :] = u.astype(BF)
            u_ref[...] = u.T.astype(BF)

        z_ref[...] = _dot(u_s[rows, :], w_s[...]) + b_ref[...]

        @pl.when((n == NSH - 1) & (i == nt - 1))
        def _():
            _ag_finish(*_ag_copies(*ag), done=NSH - 1)

    first = lambda n, i: jnp.where(n == 0, i, nt - 1)
    outs = pl.pallas_call(
        body, grid_spec=pltpu.PrefetchScalarGridSpec(
            num_scalar_prefetch=1, grid=(NSH, nt),
            in_specs=[pl.BlockSpec((tm, D), lambda n, i, sid: (jnp.minimum(first(n, i), nfull - 1), 0)),
                      pl.BlockSpec((tm, D), lambda n, i, sid: (0, 0)),
                      pl.BlockSpec((16, 3 * D), lambda n, i, sid: (0, 0)),
                      pl.BlockSpec((1, SHC), lambda n, i, sid: (0, sid[n])), _ANY, _ANY],
            out_specs=[pl.BlockSpec((tm, SHC), lambda n, i, sid: (i, sid[n])),
                       pl.BlockSpec((D, tm), lambda n, i, sid: (0, first(n, i))), _ANY, _ANY],
            scratch_shapes=[pltpu.VMEM((D, SHC), BF), pltpu.VMEM((m, D), BF)] + _ag_sems(2)
            + [pltpu.SemaphoreType.DMA]),
        out_shape=[jax.ShapeDtypeStruct((m, IN_COLS), F32), jax.ShapeDtypeStruct((D, m), BF),
                   jax.ShapeDtypeStruct((NSH,) + w_in_sh.shape, BF),
                   jax.ShapeDtypeStruct((NSH,) + narrow_sh.shape, narrow_sh.dtype)],
        name="inproj_ag", compiler_params=_cp(("arbitrary", "arbitrary")))(sids, x, tail, mod, b_in, w_in_sh, narrow_sh)
    return outs


def _rs_sibling_copies(ins, got, send, recv, shards):
    x, y, c = _place()
    copies = []
    for j in range(len(ins)):
        hr = ins[j].shape[1] // 2
        for s in shards[j]:
            give = ins[j].at[s, pl.ds(pl.multiple_of((1 - c) * hr, 8), hr), :]
            copies.append(pltpu.make_async_remote_copy(
                src_ref=give, dst_ref=got[j].at[s], send_sem=send.at[NSH * j + s], recv_sem=recv.at[NSH * j + s],
                device_id=(x, y, 1 - c), device_id_type=MESH))
    return copies


def _rs_sibling_last(g, got):
    def body(g_ref, _, got_ref, send, recv):
        x, y, c = _place()
        hr = g.shape[1] // 2
        qr = hr // NSH
        copies = [pltpu.make_async_remote_copy(
            src_ref=g_ref.at[NSH - 1, pl.ds(pl.multiple_of((1 - c) * hr + k * qr, 8), qr), :],
            dst_ref=got_ref.at[NSH - 1, pl.ds(k * qr, qr), :], send_sem=send.at[k], recv_sem=recv.at[k],
            device_id=(x, y, 1 - c), device_id_type=MESH) for k in range(NSH)]
        for cp in copies:
            cp.start()
        for cp in copies:
            cp.wait()

    return pl.pallas_call(
        body, in_specs=[_ANY, _ANY], out_specs=_ANY, out_shape=jax.ShapeDtypeStruct(got.shape, got.dtype),
        input_output_aliases={1: 0},
        scratch_shapes=[pltpu.SemaphoreType.DMA((NSH,)), pltpu.SemaphoreType.DMA((NSH,))],
        name="rs_sibling_last")(g, got)


def _core_vec():
    return lax.axis_index("c").astype(jnp.int32).reshape(1)


def _rs_add1(gs, gots, name):
    n = len(gs)
    _, r, cols = gs[0].shape
    hr = r // 2
    tr = _row_tile(hr, cols, (4 << 20) // n)
    nb = hr // tr

    def body(c_ref, *refs):
        for j in range(n):
            refs[2 * n + j][...] = (refs[j][...] + refs[n + j][...]).astype(BF)

    spec = pl.BlockSpec((1, tr, cols), lambda s, i, c_ref: (s, i, 0))
    keep = pl.BlockSpec((1, tr, cols), lambda s, i, c_ref: (s, c_ref[0] * nb + i, 0))
    return pl.pallas_call(
        body, grid_spec=pltpu.PrefetchScalarGridSpec(
            num_scalar_prefetch=1, grid=(NSH, nb), in_specs=[keep] * n + [spec] * n, out_specs=[spec] * n),
        out_shape=[jax.ShapeDtypeStruct((NSH, hr, cols), BF)] * n, name=name,
        compiler_params=_cp(("arbitrary", "arbitrary")))(_core_vec(), *gs, *gots)


def _rs_add2(sums, gots, name):
    n = len(sums)
    _, hr, cols = sums[0].shape
    tr = _row_tile(hr, cols, (2 << 20) // n)
    nb = hr // tr
    place = jnp.stack([2 * lax.axis_index("x") + lax.axis_index("y"), lax.axis_index("c")]).astype(jnp.int32)

    def body(p_ref, *refs):
        f = lambda v: v.astype(F32)
        for j in range(n):
            s_ref, got_ref = refs[j], refs[n + j]
            refs[2 * n + j][...] = f(s_ref[0]) + f(got_ref[0]) + f(got_ref[1]) + f(got_ref[2])

    return pl.pallas_call(
        body, grid_spec=pltpu.PrefetchScalarGridSpec(
            num_scalar_prefetch=1, grid=(nb,),
            in_specs=[pl.BlockSpec((1, tr, cols), lambda i, p_ref: (p_ref[0], i, 0))] * n
            + [pl.BlockSpec((3, tr, cols), lambda i, p_ref: (0, i, 0))] * n,
            out_specs=[pl.BlockSpec((tr, cols), lambda i, p_ref: (p_ref[1] * nb + i, 0))] * n),
        out_shape=[jax.ShapeDtypeStruct((2 * hr, cols), F32)] * n, name=name,
        compiler_params=_cp(("arbitrary",)))(place, *sums, *gots)


def _rs_chip_copies(ins, got, send, recv):
    x, y, c = _place()
    peers = ((1 - x, y), (x, 1 - y), (1 - x, 1 - y))
    return [pltpu.make_async_remote_copy(src_ref=ins[j].at[2 * px + py], dst_ref=got[j].at[k],
                                         send_sem=send.at[3 * j + k], recv_sem=recv.at[3 * j + k],
                                         device_id=(px, py, c), device_id_type=MESH)
            for j in range(len(ins)) for k, (px, py) in enumerate(peers)]


_AG_CHUNKS = 4


def _ag_sibling_copies(outs, send, recv):
    x, y, c = _place()
    copies = []
    for j, out in enumerate(outs):
        qr = out.shape[0] // (2 * _AG_CHUNKS)
        for k in range(_AG_CHUNKS):
            rows = out.at[pl.ds(pl.multiple_of((c * _AG_CHUNKS + k) * qr, 8), qr), :]
            copies.append(pltpu.make_async_remote_copy(
                src_ref=rows, dst_ref=rows, send_sem=send.at[_AG_CHUNKS * j + k], recv_sem=recv.at[_AG_CHUNKS * j + k],
                device_id=(x, y, 1 - c), device_id_type=MESH))
    return copies


def _allreduce_small(buf, fulls):
    rows = buf.shape[0]
    pr = rows // 8
    n = len(fulls)

    def body(*refs):
        in_ref, out_ref = refs[0], refs[1 + n]
        part, stage_a, stage_b = refs[2 + 2 * n:5 + 2 * n]
        (sa, ra, sb, rb, sc, rc, sd, rd, send, recv) = refs[5 + 2 * n:]
        x, y, c = _place()
        s = 2 * x + y
        sib = (x, y, 1 - c)
        chips = ((1 - x, y), (x, 1 - y), (1 - x, 1 - y))
        swaps = _ag_sibling_copies(refs[2 + n:2 + 2 * n], send, recv)
        for cp in swaps:
            cp.start()

        def piece(ref, chip, core):
            return ref.at[pl.ds(pl.multiple_of((2 * chip + core) * pr, 8), pr), :]

        def run(copies):
            for cp in copies:
                cp.start()
            for cp in copies:
                cp.wait()

        run([pltpu.make_async_remote_copy(src_ref=piece(in_ref, j, 1 - c), dst_ref=stage_a.at[j], send_sem=sa.at[j],
                                          recv_sem=ra.at[j], device_id=sib, device_id_type=MESH) for j in range(NSH)])
        for j in range(NSH):
            part[j] = piece(in_ref, j, c)[...] + stage_a[j]
        run([pltpu.make_async_remote_copy(src_ref=part.at[2 * px + py], dst_ref=stage_b.at[k], send_sem=sb.at[k],
                                          recv_sem=rb.at[k], device_id=(px, py, c), device_id_type=MESH)
             for k, (px, py) in enumerate(chips)])
        piece(out_ref, s, c)[...] = part[s] + stage_b[0] + stage_b[1] + stage_b[2]
        run([pltpu.make_async_remote_copy(src_ref=piece(out_ref, s, c), dst_ref=piece(out_ref, s, c),
                                          send_sem=sc.at[k], recv_sem=rc.at[k], device_id=(px, py, c),
                                          device_id_type=MESH) for k, (px, py) in enumerate(chips)])
        run([pltpu.make_async_remote_copy(src_ref=piece(out_ref, j, c), dst_ref=piece(out_ref, j, c), send_sem=sd.at[j],
                                          recv_sem=rd.at[j], device_id=sib, device_id_type=MESH) for j in range(NSH)])
        for cp in swaps:
            cp.wait()

    vm = pl.BlockSpec(memory_space=pltpu.VMEM)
    res = pl.pallas_call(
        body, in_specs=[vm] + [_ANY] * n, out_specs=[vm] + [_ANY] * n,
        out_shape=[jax.ShapeDtypeStruct((rows, D), F32)] + [jax.ShapeDtypeStruct(f.shape, F32) for f in fulls],
        input_output_aliases={1 + j: 1 + j for j in range(n)},
        scratch_shapes=[pltpu.VMEM((NSH, pr, D), F32), pltpu.VMEM((NSH, pr, D), F32), pltpu.VMEM((3, pr, D), F32)]
        + [pltpu.SemaphoreType.DMA((NSH,))] * 8 + [pltpu.SemaphoreType.DMA((_AG_CHUNKS * n,))] * 2,
        name="allreduce_small", compiler_params=_cp())(buf, *fulls)
    return res[0], res[1:]


def _pad_rows(a, mult):
    pad = (-a.shape[0]) % mult
    return jnp.concatenate([a, jnp.zeros((pad, a.shape[1]), a.dtype)]) if pad else a


def _local_step(x, c, ctx, c_ctx, tgt, me, shard, sh, b_mod, b_in, norm_g, cb, wr, wi, ln_g, ln_b):
    t_lat, t_ctx = x.shape[0], ctx.shape[0]
    nlb, ncb = t_lat // RB, t_ctx // RB
    mc = 3 * D // NSH
    mod_all, cc_all = _mod_tp(jnp.zeros((8, D), F32).at[0].set(c), c_ctx.reshape(1, D), sh["w_mod"],
                              lax.dynamic_slice_in_dim(b_mod, shard * mc, mc, axis=1))
    mod = jnp.zeros((16, 3 * D), F32).at[0].set(mod_all[me]).at[1].set(mod_all[8])
    cc = jnp.zeros((16, D), F32).at[0].set(c).at[1].set(c_ctx)
    z, u_all, w_in_g, nar = _inproj_ag(x, ctx, mod, sh["w_in"], b_in, sh["narrow"])
    nar = jnp.transpose(nar, (1, 0, 2)).reshape(-1, D)
    lbl, cw, br, bi, lam = nar[0:4].reshape(2, 2, D), nar[4:8], nar[8:10], nar[10:12], nar[12:14]
    o0, st0, (w_mod_g, p_a, p_b, w_out) = _gla_fwd(z, lbl, 0, nlb, ncb,
                                                   gather=[sh[k] for k in ("w_mod", "p_a", "p_b", "w_out")])
    p_a, p_b, w_out = p_a.reshape(D, D), p_b.reshape(D, D), w_out.reshape(D, D)
    o1, st1, _ = _gla_fwd(z, lbl, 1, nlb, ncb)
    h0, hp0, a0, hpc0 = _rglru_fwd(z, cw, cb, wr, br, wi, bi, lam, 0, t_lat, t_ctx)
    h1, hp1, a1, hpc1 = _rglru_fwd(z, cw, cb, wr, br, wi, bi, lam, 1, t_lat, t_ctx)
    (do, dh, dz4a, dz4b, dz6a, dz_sh3, gxres, y, dout, oa, dpa, obv, dpb, acc) = _merge(
        o0, o1, h0, h1, z, x, tgt, mod, norm_g, ln_g, ln_b, p_a, p_b, w_out, t_lat)
    gp_a = _wgrad(oa, dpa, "wgrad_pa")
    gp_b = _wgrad(obv, dpb, "wgrad_pb")
    gw_out = _wgrad(y, dout, "wgrad_wout")
    dxc_lat, dxc_ctx, dwr0, dwi0, sb0 = _rglru_bwd(z, cw, cb, wr, br, wi, bi, lam, dh, hp0, a0, hpc0, 0, t_lat, t_ctx)
    dz5_lat, dz5_ctx, dwr1, dwi1, sb1 = _rglru_bwd(z, cw, cb, wr, br, wi, bi, lam, dh, hp1, a1, hpc1, 1, t_lat, t_ctx,
                                                   prev=(dxc_lat, dxc_ctx))
    dq0, dv0, dz1, sa0 = _gla_bwd(z, lbl, do, st0, 0, nlb, ncb)
    dz_sh0, dz_sh1, sa1 = _gla_bwd(z, lbl, do, st1, 1, nlb, ncb, prev=(dq0, dv0, dz1, dz4a))
    dz = (dz_sh0, dz_sh1, _pack_shard2(dz4b, dz5_lat, dz5_ctx, dz6a), dz_sh3)
    g_p = [g.reshape(NSH, D // NSH, D) for g in (gp_a, gp_b, gw_out)]
    gw_in, got = _wgrad_in(u_all, dz, g_p)
    got[0] = _rs_sibling_last(gw_in, got[0])
    grads = [gw_in] + g_p
    sums = list(_rs_add1(grads[:1], got[:1], "rs_add1_w")) + list(_rs_add1(grads[1:], got[1:], "rs_add1_p"))
    gx, dm, got = _du(dz, w_in_g, x, ctx, mod, gxres, sums)
    fulls = list(_rs_add2(sums[:1], got[:1], "rs_add2_w")) + list(_rs_add2(sums[1:], got[1:], "rs_add2_p"))
    dmod =jnp.zeros((16, 3 * D), F32).at[0:2, 0:2 * D].set(dm[0:2]).at[0, 2 * D:].set(acc[0])
    dcc = _mod_bwd(cc, dmod, w_mod_g)
    small = dict(
        c_ctx=dcc[1:2], b_mod=(dmod[0] + dmod[1]).reshape(3, D),
        b_in=jnp.stack([sa1[2], sa0[0], sa1[0], sa1[3], acc[3], sb1[4], acc[4], acc[5], acc[6]]),
        lb_logits=jnp.stack([sa0[1], sa1[1], -sa0[1], -sa1[1]]),
        norm_a_g=acc[7:8], conv_w=sb1[8:12], conv_b=sb1[3:4],
        w_r=jnp.stack([dwr0, dwr1]).reshape(-1, D), w_i=jnp.stack([dwi0, dwi1]).reshape(-1, D),
        b_r=jnp.stack([sb0[0], sb1[0]]), b_i=jnp.stack([sb0[1], sb1[1]]), lam=jnp.stack([sb0[2], sb1[2]]),
        ln_g=acc[1:2], ln_b=acc[2:3])
    return acc[9, 0], gx, fulls, small, dmod, cc_all


_RS =("w_in", "p_a", "p_b", "w_out")
_SMALL =("c_ctx", "b_mod", "b_in", "lb_logits", "norm_a_g", "conv_w", "conv_b", "w_r", "w_i", "b_r", "b_i", "lam",
          "ln_g", "ln_b")
_BIG = ("w_mod", "w_in", "p_a", "p_b", "w_out")
_COL_SHARDED = ("lb_logits", "conv_w", "b_r", "b_i", "lam")
_WEIGHTS = ("c_ctx", "w_mod", "b_mod", "w_in", "b_in", "lb_logits", "norm_a_g", "conv_w", "conv_b", "w_r", "b_r", "w_i",
            "b_i", "lam", "p_a", "p_b", "w_out", "ln_g", "ln_b")


def kernel(x, c, ctx, c_ctx, w_mod, b_mod, w_in, b_in, lb_logits, norm_a_g, conv_w, conv_b, w_r, b_r, w_i, b_i, lam, p_a, p_b, w_out, ln_g, ln_b, loss_target, m_c_ctx, m_w_mod, m_b_mod, m_w_in, m_b_in, m_lb_logits, m_norm_a_g, m_conv_w, m_conv_b, m_w_r, m_b_r, m_w_i, m_b_i, m_lam, m_p_a, m_p_b, m_w_out, m_ln_g, m_ln_b, v_c_ctx, v_w_mod, v_b_mod, v_w_in, v_b_in, v_lb_logits, v_norm_a_g, v_conv_w, v_conv_b, v_w_r, v_b_r, v_w_i, v_b_i, v_lam, v_p_a, v_p_b, v_w_out, v_ln_g, v_ln_b):
    w = dict(c_ctx=c_ctx, w_mod=w_mod, b_mod=b_mod, w_in=w_in, b_in=b_in, lb_logits=lb_logits, norm_a_g=norm_a_g,
             conv_w=conv_w, conv_b=conv_b, w_r=w_r, b_r=b_r, w_i=w_i, b_i=b_i, lam=lam, p_a=p_a, p_b=p_b, w_out=w_out,
             ln_g=ln_g, ln_b=ln_b)
    m = dict(c_ctx=m_c_ctx, w_mod=m_w_mod, b_mod=m_b_mod, w_in=m_w_in, b_in=m_b_in, lb_logits=m_lb_logits,
             norm_a_g=m_norm_a_g, conv_w=m_conv_w, conv_b=m_conv_b, w_r=m_w_r, b_r=m_b_r, w_i=m_w_i, b_i=m_b_i,
             lam=m_lam, p_a=m_p_a, p_b=m_p_b, w_out=m_w_out, ln_g=m_ln_g, ln_b=m_ln_b)
    v = dict(c_ctx=v_c_ctx, w_mod=v_w_mod, b_mod=v_b_mod, w_in=v_w_in, b_in=v_b_in, lb_logits=v_lb_logits,
             norm_a_g=v_norm_a_g, conv_w=v_conv_w, conv_b=v_conv_b, w_r=v_w_r, b_r=v_b_r, w_i=v_w_i, b_i=v_b_i,
             lam=v_lam, p_a=v_p_a, p_b=v_p_b, w_out=v_w_out, ln_g=v_ln_g, ln_b=v_ln_b)
    shard = 2 * lax.axis_index("x") + lax.axis_index("y")
    cs = D // NSH

    sh = {k: w[k][0].astype(BF) for k in _BIG}
    sh["narrow"] = _pad_rows(jnp.concatenate([lb_logits.reshape(4, cs), conv_w[0], b_r[0], b_i[0], lam[0]], axis=0), 8)
    me = 2 * shard + lax.axis_index("c")
    loss, gx, fulls, small, dmod, cc_all = _local_step(
        x[0], c[0], ctx[0], c_ctx, loss_target[0], me, shard, sh, b_mod, b_in, norm_a_g, conv_b, w_r[0], w_i[0],
        ln_g, ln_b)

    dmod_rows = jnp.zeros((16, 3 * D), F32).at[me].set(dmod[0]).at[8].set(dmod[1]).reshape(48, D)
    loss_rows = jnp.zeros((8, D), F32).at[0, 0].set(loss)
    sizes = [small[k].shape[0] for k in _SMALL]
    red, fulls = _allreduce_small(_pad_rows(jnp.concatenate(
        [_pad_rows(small[k], 8) for k in _SMALL] + [dmod_rows, loss_rows], axis=0), 64), fulls)
    big = dict(zip(_RS, fulls))
    grads = {}
    off = 0
    for k, n in zip(_SMALL, sizes):
        g = red[off:off + n]
        off += n + (-n) % 8
        if k == "norm_a_g":
            g = g[:, :DH]
        if k in _COL_SHARDED:
            g = lax.dynamic_slice_in_dim(g, shard * cs, cs, axis=1)
        grads[k] = g.reshape(w[k].shape)
    for k in _RS:
        grads[k] = big[k].reshape(w[k].shape)
    dmod_all = red[off:off + 48].reshape(16, 3 * D)
    loss = red[off + 48, 0]
    mc = 3 * D // NSH
    grads["w_mod"] = _wmod_grad(cc_all, lax.dynamic_slice_in_dim(dmod_all, shard * mc, mc, axis=1)).reshape(
        w["w_mod"].shape)

    delta, new_m, new_v = {}, {}, {}
    for group in (("w_mod",), ("w_in",), ("p_a", "p_b", "w_out")):
        two = lambda t: [t[k].reshape(t[k].shape[-2:]) for k in group]
        d_, m_, v_ = _adamw(two(w), two(grads), two(m), two(v), "adamw_" + group[0])
        for j, k in enumerate(group):
            shp = w[k].shape
            delta[k], new_m[k], new_v[k] = d_[j].reshape(shp), m_[j].reshape(shp), v_[j].reshape(shp)
    d_, m_, v_ = _adamw_many(*[[t[k] for k in _SMALL] for t in (w, grads, m, v)])
    delta.update(zip(_SMALL, d_))
    new_m.update(zip(_SMALL, m_))
    new_v.update(zip(_SMALL, v_))

    return (loss, gx[None], *[grads[k] for k in _WEIGHTS], *[delta[k] for k in _WEIGHTS],
            *[new_m[k] for k in _WEIGHTS], *[new_v[k] for k in _WEIGHTS])
```
